```python
import jax, jax.numpy as jnp
from jax import lax
import numpy as np

D_MODEL = 1024
BATCH = 8
SEQ = 4096
DEPTH = 2

HEAD_DIM = 64
BLOCK = 128
A_Q_HEADS = 8
A_KV_HEADS = 2
A_GROUP = A_Q_HEADS // A_KV_HEADS
A_WINDOW = 128
B_HEADS = 8
C_HEADS = 16
C_PATTERNS = ((128, 1), (512, 4), (2048, 16))
MEM_LEN = 256
X_HEADS = 4
X_HEAD_DIM = D_MODEL // X_HEADS
D_FF = 2816
RMS_EPS = 1e-6

A_Q_W = A_Q_HEADS * HEAD_DIM
A_KV_W = A_KV_HEADS * HEAD_DIM
B_W = B_HEADS * HEAD_DIM
EVEN_IN = A_Q_W + 2 * A_KV_W + 3 * B_W
EVEN_MIX = A_Q_W + B_W
ODD_IN = 3 * C_HEADS * HEAD_DIM
ODD_MIX = C_HEADS * HEAD_DIM

kernel_name = 'hybrid_swa_stickbreak_dilated_block'


def rms_norm(x, g):
    xf = x.astype(jnp.float32)
    y = xf * lax.rsqrt(jnp.mean(xf * xf, axis=-1, keepdims=True) + RMS_EPS)
    return (y * g.astype(jnp.float32)).astype(x.dtype)


def alibi_slopes(n_heads):
    return jnp.asarray(2.0 ** (-8.0 * np.arange(1, n_heads + 1) / n_heads), dtype=jnp.float32)


def swiglu_ffn(x, w_gu, w_down):
    gate, up = jnp.split(x @ w_gu, 2, axis=-1)
    return (jax.nn.silu(gate) * up) @ w_down


def banded_attention(q, k, v, slopes, max_dist, step, sinks=None):
    b, l, hkv, g, dh = q.shape
    nb = -(-l // BLOCK)
    lp = nb * BLOCK
    pad = lp - l
    qb = jnp.pad(q, ((0, 0), (0, pad), (0, 0), (0, 0), (0, 0))).reshape(b, nb, BLOCK, hkv, g, dh)
    kv_pad = ((0, 0), (BLOCK, pad), (0, 0), (0, 0))
    k = jnp.pad(k, kv_pad).reshape(b, nb + 1, BLOCK, hkv, dh)
    v = jnp.pad(v, kv_pad).reshape(b, nb + 1, BLOCK, hkv, dh)
    kb = jnp.concatenate([k[:, :-1], k[:, 1:]], axis=2)
    vb = jnp.concatenate([v[:, :-1], v[:, 1:]], axis=2)
    s = jnp.einsum('bnqhgd,bnkhd->bnhgqk', qb, kb).astype(jnp.float32) * (dh ** -0.5)
    dist = jnp.arange(BLOCK)[:, None] + BLOCK - jnp.arange(2 * BLOCK)[None, :]
    kpos = jnp.arange(nb)[:, None] * BLOCK - BLOCK + jnp.arange(2 * BLOCK)[None, :]
    valid = (dist >= 0) & (dist <= max_dist) & (kpos[:, None, :] >= 0)
    bias = -(slopes.astype(jnp.float32) * step)[:, :, None, None] * dist.astype(jnp.float32)
    s = jnp.where(valid[None, :, None, None], s + bias[None, None], -jnp.inf)
    m = jnp.max(s, axis=-1)
    if sinks is not None:
        sk = sinks.astype(jnp.float32)[..., None]
        m = jnp.maximum(m, sk)
    p = jnp.exp(s - m[..., None])
    denom = jnp.sum(p, axis=-1)
    if sinks is not None:
        denom = denom + jnp.exp(sk - m)
    o = jnp.einsum('bnhgqk,bnkhd->bnqhgd', (p / denom[..., None]).astype(v.dtype), vb)
    lse = m + jnp.log(denom)
    o = o.reshape(b, lp, hkv, g, dh)[:, :l]
    lse = jnp.moveaxis(lse, -1, 2).reshape(b, lp, hkv, g)[:, :l]
    return o, lse


def stick_breaking_attention(q, k, v):
    b, s, h, dh = q.shape
    nb = s // BLOCK
    qb = q.reshape(b, nb, BLOCK, h, dh).transpose(1, 0, 2, 3, 4)
    spos = jnp.arange(s)
    scale = dh ** -0.5

    def one_block(args):
        i, qi = args
        z = jnp.einsum('bqhd,bkhd->bhqk', qi, k).astype(jnp.float32) * scale
        tpos = i * BLOCK + jnp.arange(BLOCK)
        strict = spos[None, :] < tpos[:, None]
        log_keep = jnp.where(strict, jax.nn.log_sigmoid(-z), 0.0)
        log_after = lax.cumsum(log_keep, axis=3, reverse=True) - log_keep
        a = jnp.where(strict, jnp.exp(jax.nn.log_sigmoid(z) + log_after), 0.0)
        return jnp.einsum('bhqk,bkhd->bqhd', a.astype(v.dtype), v)

    o = lax.map(one_block, (jnp.arange(nb), qb))
    return o.transpose(1, 0, 2, 3, 4).reshape(b, s, h, dh)


def dilated_attention(q, k, v, slopes):
    b, s, h, dh = q.shape
    outs, lses = [], []
    for window, dil in C_PATTERNS:
        sp = -(-s // dil) * dil
        ls = sp // dil

        def strided(t):
            t = jnp.pad(t, ((0, 0), (0, sp - s), (0, 0), (0, 0)))
            return t.reshape(b, ls, dil, h, dh).transpose(0, 2, 1, 3, 4).reshape(b * dil, ls, h, dh)

        o, lse = banded_attention(strided(q)[:, :, :, None], strided(k), strided(v),
                                  slopes[:, None], window // dil, dil)
        o = o[:, :, :, 0].reshape(b, dil, ls, h, dh).transpose(0, 2, 1, 3, 4).reshape(b, sp, h, dh)[:, :s]
        lse = lse[..., 0].reshape(b, dil, ls, h).transpose(0, 2, 1, 3).reshape(b, sp, h)[:, :s]
        outs.append(o)
        lses.append(lse)
    w = jax.nn.softmax(jnp.stack(lses), axis=0)
    o = jnp.sum(w[..., None] * jnp.stack(outs).astype(jnp.float32), axis=0)
    return o.astype(q.dtype)


def even_mixer(h, w_in, q_gain, k_gain, sinks, w_out):
    b, s, _ = h.shape
    cuts = np.cumsum([A_Q_W, A_KV_W, A_KV_W, B_W, B_W]).tolist()
    qa, ka, va, qb, kb, vb = jnp.split(h @ w_in, cuts, axis=-1)
    qa = rms_norm(qa.reshape(b, s, A_KV_HEADS, A_GROUP, HEAD_DIM), q_gain)
    ka = rms_norm(ka.reshape(b, s, A_KV_HEADS, HEAD_DIM), k_gain)
    va = va.reshape(b, s, A_KV_HEADS, HEAD_DIM)
    slopes_a = alibi_slopes(A_Q_HEADS).reshape(A_KV_HEADS, A_GROUP)
    o_a, _ = banded_attention(qa, ka, va, slopes_a, A_WINDOW - 1, 1,
                              sinks.reshape(A_KV_HEADS, A_GROUP))
    o_b = stick_breaking_attention(qb.reshape(b, s, B_HEADS, HEAD_DIM),
                                   kb.reshape(b, s, B_HEADS, HEAD_DIM),
                                   vb.reshape(b, s, B_HEADS, HEAD_DIM))
    o = jnp.concatenate([o_a.reshape(b, s, A_Q_W), o_b.reshape(b, s, B_W)], axis=-1)
    return o @ w_out


def odd_mixer(h, w_in, q_gain, k_gain, w_out):
    b, s, _ = h.shape
    qkv = (h @ w_in).reshape(b, s, 3, C_HEADS, HEAD_DIM)
    q = rms_norm(qkv[:, :, 0], q_gain)
    k = rms_norm(qkv[:, :, 1], k_gain)
    o = dilated_attention(q, k, qkv[:, :, 2], alibi_slopes(C_HEADS))
    return o.reshape(b, s, ODD_MIX) @ w_out


def memory_cross_attention(h, m, w_q, w_kv, q_gain, k_gain, w_o):
    b, s, _ = h.shape
    q = rms_norm((h @ w_q).reshape(b, s, X_HEADS, X_HEAD_DIM), q_gain)
    kv = (m @ w_kv).reshape(b, m.shape[1], 2, X_HEADS, X_HEAD_DIM)
    k = rms_norm(kv[:, :, 0], k_gain)
    v = kv[:, :, 1]
    sc = jnp.einsum('bqhd,bkhd->bhqk', q, k).astype(jnp.float32) * (X_HEAD_DIM ** -0.5)
    p = jax.nn.softmax(sc, axis=-1).astype(v.dtype)
    o = jnp.einsum('bhqk,bkhd->bqhd', p, v).reshape(b, s, X_HEADS * X_HEAD_DIM)
    return o @ w_o


def _fwd_setup_inputs(seed: int = 0) -> dict:
    key = jax.random.key(seed)
    k = jax.random.split(key, 25)
    n_even = (DEPTH + 1) // 2
    n_odd = DEPTH // 2
    f32 = jnp.float32

    def dense(kk, shape, fan_in):
        return jax.random.normal(kk, shape, f32) * (fan_in ** -0.5)

    def gain(kk, shape):
        return 1.0 + 0.02 * jax.random.normal(kk, shape, f32)

    return {
        'x': jax.random.normal(k[0], (BATCH, SEQ, D_MODEL), f32),
        'mem': jax.random.normal(k[1], (BATCH, MEM_LEN, D_MODEL), f32),
        'ffn1_norm': gain(k[2], (DEPTH, D_MODEL)),
        'ffn1_w_gu': dense(k[3], (DEPTH, D_MODEL, 2 * D_FF), D_MODEL),
        'ffn1_w_down': dense(k[4], (DEPTH, D_FF, D_MODEL), D_FF),
        'mix_norm': gain(k[5], (DEPTH, D_MODEL)),
        'ev_w_in': dense(k[6], (n_even, D_MODEL, EVEN_IN), D_MODEL),
        'ev_q_gain': gain(k[7], (n_even, HEAD_DIM)),
        'ev_k_gain': gain(k[8], (n_even, HEAD_DIM)),
        'ev_sinks': 0.5 * jax.random.normal(k[9], (n_even, A_Q_HEADS), f32),
        'ev_w_out': dense(k[10], (n_even, EVEN_MIX, D_MODEL), EVEN_MIX),
        'od_w_in': dense(k[11], (n_odd, D_MODEL, ODD_IN), D_MODEL),
        'od_q_gain': gain(k[12], (n_odd, HEAD_DIM)),
        'od_k_gain': gain(k[13], (n_odd, HEAD_DIM)),
        'od_w_out': dense(k[14], (n_odd, ODD_MIX, D_MODEL), ODD_MIX),
        'xa_norm': gain(k[15], (DEPTH, D_MODEL)),
        'xa_mem_norm': gain(k[16], (DEPTH, D_MODEL)),
        'xa_w_q': dense(k[17], (DEPTH, D_MODEL, X_HEADS * X_HEAD_DIM), D_MODEL),
        'xa_w_kv': dense(k[18], (DEPTH, D_MODEL, 2 * X_HEADS * X_HEAD_DIM), D_MODEL),
        'xa_q_gain': gain(k[19], (DEPTH, X_HEAD_DIM)),
        'xa_k_gain': gain(k[20], (DEPTH, X_HEAD_DIM)),
        'xa_w_o': dense(k[21], (DEPTH, X_HEADS * X_HEAD_DIM, D_MODEL), X_HEADS * X_HEAD_DIM),
        'ffn2_norm': gain(k[22], (DEPTH, D_MODEL)),
        'ffn2_w_gu': dense(k[23], (DEPTH, D_MODEL, 2 * D_FF), D_MODEL),
        'ffn2_w_down': dense(k[24], (DEPTH, D_FF, D_MODEL), D_FF),
    }


def _fwd_reference(x, mem, ffn1_norm, ffn1_w_gu, ffn1_w_down, mix_norm,
              ev_w_in, ev_q_gain, ev_k_gain, ev_sinks, ev_w_out,
              od_w_in, od_q_gain, od_k_gain, od_w_out,
              xa_norm, xa_mem_norm, xa_w_q, xa_w_kv, xa_q_gain, xa_k_gain, xa_w_o,
              ffn2_norm, ffn2_w_gu, ffn2_w_down):
    for layer in range(DEPTH):
        x = x + 0.5 * swiglu_ffn(rms_norm(x, ffn1_norm[layer]), ffn1_w_gu[layer], ffn1_w_down[layer])
        h = rms_norm(x, mix_norm[layer])
        if layer % 2 == 0:
            j = layer // 2
            x = x + even_mixer(h, ev_w_in[j], ev_q_gain[j], ev_k_gain[j], ev_sinks[j], ev_w_out[j])
        else:
            j = layer // 2
            x = x + odd_mixer(h, od_w_in[j], od_q_gain[j], od_k_gain[j], od_w_out[j])
        x = x + memory_cross_attention(rms_norm(x, xa_norm[layer]), rms_norm(mem, xa_mem_norm[layer]),
                                       xa_w_q[layer], xa_w_kv[layer], xa_q_gain[layer],
                                       xa_k_gain[layer], xa_w_o[layer])
        x = x + 0.5 * swiglu_ffn(rms_norm(x, ffn2_norm[layer]), ffn2_w_gu[layer], ffn2_w_down[layer])
    return x


import jax as _jax
import jax.numpy as _jnp

TWIN_FORMAT = 'train_step'
FWD_PARAMS = ['x', 'mem', 'ffn1_norm', 'ffn1_w_gu', 'ffn1_w_down', 'mix_norm', 'ev_w_in', 'ev_q_gain', 'ev_k_gain', 'ev_sinks', 'ev_w_out', 'od_w_in', 'od_q_gain', 'od_k_gain', 'od_w_out', 'xa_norm', 'xa_mem_norm', 'xa_w_q', 'xa_w_kv', 'xa_q_gain', 'xa_k_gain', 'xa_w_o', 'ffn2_norm', 'ffn2_w_gu', 'ffn2_w_down']
TWIN_WEIGHTS = ['ffn1_norm', 'ffn1_w_gu', 'ffn1_w_down', 'mix_norm', 'ev_w_in', 'ev_q_gain', 'ev_k_gain', 'ev_sinks', 'ev_w_out', 'od_w_in', 'od_q_gain', 'od_k_gain', 'od_w_out', 'xa_norm', 'xa_mem_norm', 'xa_w_q', 'xa_w_kv', 'xa_q_gain', 'xa_k_gain', 'xa_w_o', 'ffn2_norm', 'ffn2_w_gu', 'ffn2_w_down']
TWIN_DIFF_INPUT = 'x'
TWIN_INPUTS = ['x', 'mem', 'ffn1_norm', 'ffn1_w_gu', 'ffn1_w_down', 'mix_norm', 'ev_w_in', 'ev_q_gain', 'ev_k_gain', 'ev_sinks', 'ev_w_out', 'od_w_in', 'od_q_gain', 'od_k_gain', 'od_w_out', 'xa_norm', 'xa_mem_norm', 'xa_w_q', 'xa_w_kv', 'xa_q_gain', 'xa_k_gain', 'xa_w_o', 'ffn2_norm', 'ffn2_w_gu', 'ffn2_w_down', 'loss_target', 'm_ffn1_norm', 'm_ffn1_w_gu', 'm_ffn1_w_down', 'm_mix_norm', 'm_ev_w_in', 'm_ev_q_gain', 'm_ev_k_gain', 'm_ev_sinks', 'm_ev_w_out', 'm_od_w_in', 'm_od_q_gain', 'm_od_k_gain', 'm_od_w_out', 'm_xa_norm', 'm_xa_mem_norm', 'm_xa_w_q', 'm_xa_w_kv', 'm_xa_q_gain', 'm_xa_k_gain', 'm_xa_w_o', 'm_ffn2_norm', 'm_ffn2_w_gu', 'm_ffn2_w_down', 'v_ffn1_norm', 'v_ffn1_w_gu', 'v_ffn1_w_down', 'v_mix_norm', 'v_ev_w_in', 'v_ev_q_gain', 'v_ev_k_gain', 'v_ev_sinks', 'v_ev_w_out', 'v_od_w_in', 'v_od_q_gain', 'v_od_k_gain', 'v_od_w_out', 'v_xa_norm', 'v_xa_mem_norm', 'v_xa_w_q', 'v_xa_w_kv', 'v_xa_q_gain', 'v_xa_k_gain', 'v_xa_w_o', 'v_ffn2_norm', 'v_ffn2_w_gu', 'v_ffn2_w_down']
TWIN_OUTPUTS = ['loss', 'grad_x', 'grad_ffn1_norm', 'grad_ffn1_w_gu', 'grad_ffn1_w_down', 'grad_mix_norm', 'grad_ev_w_in', 'grad_ev_q_gain', 'grad_ev_k_gain', 'grad_ev_sinks', 'grad_ev_w_out', 'grad_od_w_in', 'grad_od_q_gain', 'grad_od_k_gain', 'grad_od_w_out', 'grad_xa_norm', 'grad_xa_mem_norm', 'grad_xa_w_q', 'grad_xa_w_kv', 'grad_xa_q_gain', 'grad_xa_k_gain', 'grad_xa_w_o', 'grad_ffn2_norm', 'grad_ffn2_w_gu', 'grad_ffn2_w_down', 'delta_ffn1_norm', 'delta_ffn1_w_gu', 'delta_ffn1_w_down', 'delta_mix_norm', 'delta_ev_w_in', 'delta_ev_q_gain', 'delta_ev_k_gain', 'delta_ev_sinks', 'delta_ev_w_out', 'delta_od_w_in', 'delta_od_q_gain', 'delta_od_k_gain', 'delta_od_w_out', 'delta_xa_norm', 'delta_xa_mem_norm', 'delta_xa_w_q', 'delta_xa_w_kv', 'delta_xa_q_gain', 'delta_xa_k_gain', 'delta_xa_w_o', 'delta_ffn2_norm', 'delta_ffn2_w_gu', 'delta_ffn2_w_down', 'new_m_ffn1_norm', 'new_m_ffn1_w_gu', 'new_m_ffn1_w_down', 'new_m_mix_norm', 'new_m_ev_w_in', 'new_m_ev_q_gain', 'new_m_ev_k_gain', 'new_m_ev_sinks', 'new_m_ev_w_out', 'new_m_od_w_in', 'new_m_od_q_gain', 'new_m_od_k_gain', 'new_m_od_w_out', 'new_m_xa_norm', 'new_m_xa_mem_norm', 'new_m_xa_w_q', 'new_m_xa_w_kv', 'new_m_xa_q_gain', 'new_m_xa_k_gain', 'new_m_xa_w_o', 'new_m_ffn2_norm', 'new_m_ffn2_w_gu', 'new_m_ffn2_w_down', 'new_v_ffn1_norm', 'new_v_ffn1_w_gu', 'new_v_ffn1_w_down', 'new_v_mix_norm', 'new_v_ev_w_in', 'new_v_ev_q_gain', 'new_v_ev_k_gain', 'new_v_ev_sinks', 'new_v_ev_w_out', 'new_v_od_w_in', 'new_v_od_q_gain', 'new_v_od_k_gain', 'new_v_od_w_out', 'new_v_xa_norm', 'new_v_xa_mem_norm', 'new_v_xa_w_q', 'new_v_xa_w_kv', 'new_v_xa_q_gain', 'new_v_xa_k_gain', 'new_v_xa_w_o', 'new_v_ffn2_norm', 'new_v_ffn2_w_gu', 'new_v_ffn2_w_down']
TWIN_LEAF_KINDS = {'loss': 'loss', 'grad_x': 'grad_x', 'grad_ffn1_norm': 'grad_w', 'grad_ffn1_w_gu': 'grad_w', 'grad_ffn1_w_down': 'grad_w', 'grad_mix_norm': 'grad_w', 'grad_ev_w_in': 'grad_w', 'grad_ev_q_gain': 'grad_w', 'grad_ev_k_gain': 'grad_w', 'grad_ev_sinks': 'grad_w', 'grad_ev_w_out': 'grad_w', 'grad_od_w_in': 'grad_w', 'grad_od_q_gain': 'grad_w', 'grad_od_k_gain': 'grad_w', 'grad_od_w_out': 'grad_w', 'grad_xa_norm': 'grad_w', 'grad_xa_mem_norm': 'grad_w', 'grad_xa_w_q': 'grad_w', 'grad_xa_w_kv': 'grad_w', 'grad_xa_q_gain': 'grad_w', 'grad_xa_k_gain': 'grad_w', 'grad_xa_w_o': 'grad_w', 'grad_ffn2_norm': 'grad_w', 'grad_ffn2_w_gu': 'grad_w', 'grad_ffn2_w_down': 'grad_w', 'delta_ffn1_norm': 'delta_w', 'delta_ffn1_w_gu': 'delta_w', 'delta_ffn1_w_down': 'delta_w', 'delta_mix_norm': 'delta_w', 'delta_ev_w_in': 'delta_w', 'delta_ev_q_gain': 'delta_w', 'delta_ev_k_gain': 'delta_w', 'delta_ev_sinks': 'delta_w', 'delta_ev_w_out': 'delta_w', 'delta_od_w_in': 'delta_w', 'delta_od_q_gain': 'delta_w', 'delta_od_k_gain': 'delta_w', 'delta_od_w_out': 'delta_w', 'delta_xa_norm': 'delta_w', 'delta_xa_mem_norm': 'delta_w', 'delta_xa_w_q': 'delta_w', 'delta_xa_w_kv': 'delta_w', 'delta_xa_q_gain': 'delta_w', 'delta_xa_k_gain': 'delta_w', 'delta_xa_w_o': 'delta_w', 'delta_ffn2_norm': 'delta_w', 'delta_ffn2_w_gu': 'delta_w', 'delta_ffn2_w_down': 'delta_w', 'new_m_ffn1_norm': 'new_m', 'new_m_ffn1_w_gu': 'new_m', 'new_m_ffn1_w_down': 'new_m', 'new_m_mix_norm': 'new_m', 'new_m_ev_w_in': 'new_m', 'new_m_ev_q_gain': 'new_m', 'new_m_ev_k_gain': 'new_m', 'new_m_ev_sinks': 'new_m', 'new_m_ev_w_out': 'new_m', 'new_m_od_w_in': 'new_m', 'new_m_od_q_gain': 'new_m', 'new_m_od_k_gain': 'new_m', 'new_m_od_w_out': 'new_m', 'new_m_xa_norm': 'new_m', 'new_m_xa_mem_norm': 'new_m', 'new_m_xa_w_q': 'new_m', 'new_m_xa_w_kv': 'new_m', 'new_m_xa_q_gain': 'new_m', 'new_m_xa_k_gain': 'new_m', 'new_m_xa_w_o': 'new_m', 'new_m_ffn2_norm': 'new_m', 'new_m_ffn2_w_gu': 'new_m', 'new_m_ffn2_w_down': 'new_m', 'new_v_ffn1_norm': 'new_v', 'new_v_ffn1_w_gu': 'new_v', 'new_v_ffn1_w_down': 'new_v', 'new_v_mix_norm': 'new_v', 'new_v_ev_w_in': 'new_v', 'new_v_ev_q_gain': 'new_v', 'new_v_ev_k_gain': 'new_v', 'new_v_ev_sinks': 'new_v', 'new_v_ev_w_out': 'new_v', 'new_v_od_w_in': 'new_v', 'new_v_od_q_gain': 'new_v', 'new_v_od_k_gain': 'new_v', 'new_v_od_w_out': 'new_v', 'new_v_xa_norm': 'new_v', 'new_v_xa_mem_norm': 'new_v', 'new_v_xa_w_q': 'new_v', 'new_v_xa_w_kv': 'new_v', 'new_v_xa_q_gain': 'new_v', 'new_v_xa_k_gain': 'new_v', 'new_v_xa_w_o': 'new_v', 'new_v_ffn2_norm': 'new_v', 'new_v_ffn2_w_gu': 'new_v', 'new_v_ffn2_w_down': 'new_v'}


def _forward(args):
    return _fwd_reference(*[args[k] for k in FWD_PARAMS])


def _output_shape():
    def fwd():
        inp = _fwd_setup_inputs(0)
        return _fwd_reference(*[inp[k] for k in FWD_PARAMS])
    out = _jax.eval_shape(fwd)
    return out.shape, out.dtype

N_MICROBATCH = 1
ADAM_LR = 0.001
ADAM_B1 = 0.9
ADAM_B2 = 0.999
ADAM_EPS = 1e-08
ADAM_WD = 0.01
ADAM_STEP = 10
PER_EXAMPLE_BATCH_AXIS = {'x': 0, 'mem': 0, 'loss_target': 0}
SHARED_INPUTS = []
_WEIGHT_DTYPES = {'ffn1_norm': _jnp.float32, 'ffn1_w_gu': _jnp.float32, 'ffn1_w_down': _jnp.float32, 'mix_norm': _jnp.float32, 'ev_w_in': _jnp.float32, 'ev_q_gain': _jnp.float32, 'ev_k_gain': _jnp.float32, 'ev_sinks': _jnp.float32, 'ev_w_out': _jnp.float32, 'od_w_in': _jnp.float32, 'od_q_gain': _jnp.float32, 'od_k_gain': _jnp.float32, 'od_w_out': _jnp.float32, 'xa_norm': _jnp.float32, 'xa_mem_norm': _jnp.float32, 'xa_w_q': _jnp.float32, 'xa_w_kv': _jnp.float32, 'xa_q_gain': _jnp.float32, 'xa_k_gain': _jnp.float32, 'xa_w_o': _jnp.float32, 'ffn2_norm': _jnp.float32, 'ffn2_w_gu': _jnp.float32, 'ffn2_w_down': _jnp.float32}
MOMENT_SCALE = {'ffn1_norm': 6.134044e+00, 'ffn1_w_gu': 9.406982e-02, 'ffn1_w_down': 1.659940e-01, 'mix_norm': 7.827635e+00, 'ev_w_in': 2.867918e-01, 'ev_q_gain': 9.802485e+00, 'ev_k_gain': 9.823209e+00, 'ev_sinks': 3.254506e+01, 'ev_w_out': 3.812397e-01, 'od_w_in': 3.090257e-01, 'od_q_gain': 1.283800e+01, 'od_k_gain': 1.278249e+01, 'od_w_out': 3.946017e-01, 'xa_norm': 5.186213e-02, 'xa_mem_norm': 4.055777e-01, 'xa_w_q': 5.379482e-02, 'xa_w_kv': 8.909799e-02, 'xa_q_gain': 1.286905e+00, 'xa_k_gain': 1.284090e+00, 'xa_w_o': 1.040767e-01, 'ffn2_norm': 6.147615e+00, 'ffn2_w_gu': 8.283509e-02, 'ffn2_w_down': 1.511817e-01}


def _to_microbatches(a, axis):
    t = _jnp.moveaxis(a, axis, 0)
    t = t.reshape((N_MICROBATCH, t.shape[0] // N_MICROBATCH) + t.shape[1:])
    return _jnp.moveaxis(t, 1, axis + 1)


def setup_inputs(seed: int = 0) -> dict:
    inp = _fwd_setup_inputs(seed)
    key = _jax.random.fold_in(_jax.random.key(seed), 7919)
    shape, _ = _output_shape()
    out = dict(inp)
    out["loss_target"] = _jax.random.normal(_jax.random.fold_in(key, 0), shape, _jnp.float32)
    for i, name in enumerate(TWIN_WEIGHTS):
        w = inp[name].astype(_jnp.float32)
        if MOMENT_SCALE is None:
            s = _jnp.sqrt(_jnp.mean(_jnp.square(w)) + 1e-30)
        else:
            s = MOMENT_SCALE[name]
        km, kv = _jax.random.split(_jax.random.fold_in(key, i + 1))
        out[name] = w
        out["m_" + name] = s * _jax.random.normal(km, w.shape, _jnp.float32)
        out["v_" + name] = (s * s) * _jax.random.uniform(kv, w.shape, _jnp.float32, 0.5, 1.5)
    if N_MICROBATCH > 1:
        for name, axis in PER_EXAMPLE_BATCH_AXIS.items():
            out[name] = _to_microbatches(out[name], axis)
    return {'x': out['x'], 'mem': out['mem'], 'ffn1_norm': out['ffn1_norm'], 'ffn1_w_gu': out['ffn1_w_gu'], 'ffn1_w_down': out['ffn1_w_down'], 'mix_norm': out['mix_norm'], 'ev_w_in': out['ev_w_in'], 'ev_q_gain': out['ev_q_gain'], 'ev_k_gain': out['ev_k_gain'], 'ev_sinks': out['ev_sinks'], 'ev_w_out': out['ev_w_out'], 'od_w_in': out['od_w_in'], 'od_q_gain': out['od_q_gain'], 'od_k_gain': out['od_k_gain'], 'od_w_out': out['od_w_out'], 'xa_norm': out['xa_norm'], 'xa_mem_norm': out['xa_mem_norm'], 'xa_w_q': out['xa_w_q'], 'xa_w_kv': out['xa_w_kv'], 'xa_q_gain': out['xa_q_gain'], 'xa_k_gain': out['xa_k_gain'], 'xa_w_o': out['xa_w_o'], 'ffn2_norm': out['ffn2_norm'], 'ffn2_w_gu': out['ffn2_w_gu'], 'ffn2_w_down': out['ffn2_w_down'], 'loss_target': out['loss_target'], 'm_ffn1_norm': out['m_ffn1_norm'], 'm_ffn1_w_gu': out['m_ffn1_w_gu'], 'm_ffn1_w_down': out['m_ffn1_w_down'], 'm_mix_norm': out['m_mix_norm'], 'm_ev_w_in': out['m_ev_w_in'], 'm_ev_q_gain': out['m_ev_q_gain'], 'm_ev_k_gain': out['m_ev_k_gain'], 'm_ev_sinks': out['m_ev_sinks'], 'm_ev_w_out': out['m_ev_w_out'], 'm_od_w_in': out['m_od_w_in'], 'm_od_q_gain': out['m_od_q_gain'], 'm_od_k_gain': out['m_od_k_gain'], 'm_od_w_out': out['m_od_w_out'], 'm_xa_norm': out['m_xa_norm'], 'm_xa_mem_norm': out['m_xa_mem_norm'], 'm_xa_w_q': out['m_xa_w_q'], 'm_xa_w_kv': out['m_xa_w_kv'], 'm_xa_q_gain': out['m_xa_q_gain'], 'm_xa_k_gain': out['m_xa_k_gain'], 'm_xa_w_o': out['m_xa_w_o'], 'm_ffn2_norm': out['m_ffn2_norm'], 'm_ffn2_w_gu': out['m_ffn2_w_gu'], 'm_ffn2_w_down': out['m_ffn2_w_down'], 'v_ffn1_norm': out['v_ffn1_norm'], 'v_ffn1_w_gu': out['v_ffn1_w_gu'], 'v_ffn1_w_down': out['v_ffn1_w_down'], 'v_mix_norm': out['v_mix_norm'], 'v_ev_w_in': out['v_ev_w_in'], 'v_ev_q_gain': out['v_ev_q_gain'], 'v_ev_k_gain': out['v_ev_k_gain'], 'v_ev_sinks': out['v_ev_sinks'], 'v_ev_w_out': out['v_ev_w_out'], 'v_od_w_in': out['v_od_w_in'], 'v_od_q_gain': out['v_od_q_gain'], 'v_od_k_gain': out['v_od_k_gain'], 'v_od_w_out': out['v_od_w_out'], 'v_xa_norm': out['v_xa_norm'], 'v_xa_mem_norm': out['v_xa_mem_norm'], 'v_xa_w_q': out['v_xa_w_q'], 'v_xa_w_kv': out['v_xa_w_kv'], 'v_xa_q_gain': out['v_xa_q_gain'], 'v_xa_k_gain': out['v_xa_k_gain'], 'v_xa_w_o': out['v_xa_w_o'], 'v_ffn2_norm': out['v_ffn2_norm'], 'v_ffn2_w_gu': out['v_ffn2_w_gu'], 'v_ffn2_w_down': out['v_ffn2_w_down']}


def _loss(weights, diff, rest, loss_target):
    with _jax.named_scope("forward"):
        args = {**rest, TWIN_DIFF_INPUT: diff, **{k: w.astype(_WEIGHT_DTYPES[k]) for k, w in weights.items()}}
        y = _forward(args)
    with _jax.named_scope("loss_head"):
        err = _jnp.square(y.astype(_jnp.float32) - loss_target)
        return 0.5 * _jnp.sum(_jnp.mean(err, axis=-1)) if err.ndim else 0.5 * err


def _adamw(w, g, m, v):
    m = ADAM_B1 * m + (1.0 - ADAM_B1) * g
    v = ADAM_B2 * v + (1.0 - ADAM_B2) * _jnp.square(g)
    m_hat = m / (1.0 - ADAM_B1 ** ADAM_STEP)
    v_hat = v / (1.0 - ADAM_B2 ** ADAM_STEP)
    delta = -ADAM_LR * (m_hat / (_jnp.sqrt(v_hat) + ADAM_EPS) + ADAM_WD * w)
    return delta, m, v


def reference(x, mem, ffn1_norm, ffn1_w_gu, ffn1_w_down, mix_norm, ev_w_in, ev_q_gain, ev_k_gain, ev_sinks, ev_w_out, od_w_in, od_q_gain, od_k_gain, od_w_out, xa_norm, xa_mem_norm, xa_w_q, xa_w_kv, xa_q_gain, xa_k_gain, xa_w_o, ffn2_norm, ffn2_w_gu, ffn2_w_down, loss_target, m_ffn1_norm, m_ffn1_w_gu, m_ffn1_w_down, m_mix_norm, m_ev_w_in, m_ev_q_gain, m_ev_k_gain, m_ev_sinks, m_ev_w_out, m_od_w_in, m_od_q_gain, m_od_k_gain, m_od_w_out, m_xa_norm, m_xa_mem_norm, m_xa_w_q, m_xa_w_kv, m_xa_q_gain, m_xa_k_gain, m_xa_w_o, m_ffn2_norm, m_ffn2_w_gu, m_ffn2_w_down, v_ffn1_norm, v_ffn1_w_gu, v_ffn1_w_down, v_mix_norm, v_ev_w_in, v_ev_q_gain, v_ev_k_gain, v_ev_sinks, v_ev_w_out, v_od_w_in, v_od_q_gain, v_od_k_gain, v_od_w_out, v_xa_norm, v_xa_mem_norm, v_xa_w_q, v_xa_w_kv, v_xa_q_gain, v_xa_k_gain, v_xa_w_o, v_ffn2_norm, v_ffn2_w_gu, v_ffn2_w_down):
    given = dict(x=x, mem=mem, ffn1_norm=ffn1_norm, ffn1_w_gu=ffn1_w_gu, ffn1_w_down=ffn1_w_down, mix_norm=mix_norm, ev_w_in=ev_w_in, ev_q_gain=ev_q_gain, ev_k_gain=ev_k_gain, ev_sinks=ev_sinks, ev_w_out=ev_w_out, od_w_in=od_w_in, od_q_gain=od_q_gain, od_k_gain=od_k_gain, od_w_out=od_w_out, xa_norm=xa_norm, xa_mem_norm=xa_mem_norm, xa_w_q=xa_w_q, xa_w_kv=xa_w_kv, xa_q_gain=xa_q_gain, xa_k_gain=xa_k_gain, xa_w_o=xa_w_o, ffn2_norm=ffn2_norm, ffn2_w_gu=ffn2_w_gu, ffn2_w_down=ffn2_w_down, loss_target=loss_target, m_ffn1_norm=m_ffn1_norm, m_ffn1_w_gu=m_ffn1_w_gu, m_ffn1_w_down=m_ffn1_w_down, m_mix_norm=m_mix_norm, m_ev_w_in=m_ev_w_in, m_ev_q_gain=m_ev_q_gain, m_ev_k_gain=m_ev_k_gain, m_ev_sinks=m_ev_sinks, m_ev_w_out=m_ev_w_out, m_od_w_in=m_od_w_in, m_od_q_gain=m_od_q_gain, m_od_k_gain=m_od_k_gain, m_od_w_out=m_od_w_out, m_xa_norm=m_xa_norm, m_xa_mem_norm=m_xa_mem_norm, m_xa_w_q=m_xa_w_q, m_xa_w_kv=m_xa_w_kv, m_xa_q_gain=m_xa_q_gain, m_xa_k_gain=m_xa_k_gain, m_xa_w_o=m_xa_w_o, m_ffn2_norm=m_ffn2_norm, m_ffn2_w_gu=m_ffn2_w_gu, m_ffn2_w_down=m_ffn2_w_down, v_ffn1_norm=v_ffn1_norm, v_ffn1_w_gu=v_ffn1_w_gu, v_ffn1_w_down=v_ffn1_w_down, v_mix_norm=v_mix_norm, v_ev_w_in=v_ev_w_in, v_ev_q_gain=v_ev_q_gain, v_ev_k_gain=v_ev_k_gain, v_ev_sinks=v_ev_sinks, v_ev_w_out=v_ev_w_out, v_od_w_in=v_od_w_in, v_od_q_gain=v_od_q_gain, v_od_k_gain=v_od_k_gain, v_od_w_out=v_od_w_out, v_xa_norm=v_xa_norm, v_xa_mem_norm=v_xa_mem_norm, v_xa_w_q=v_xa_w_q, v_xa_w_kv=v_xa_w_kv, v_xa_q_gain=v_xa_q_gain, v_xa_k_gain=v_xa_k_gain, v_xa_w_o=v_xa_w_o, v_ffn2_norm=v_ffn2_norm, v_ffn2_w_gu=v_ffn2_w_gu, v_ffn2_w_down=v_ffn2_w_down)
    weights = {n: given[n] for n in TWIN_WEIGHTS}
    shared = {n: given[n] for n in SHARED_INPUTS}
    per_example = {n: given[n] for n in ['x', 'mem']}
    grad_fn = _jax.value_and_grad(_loss, argnums=(0, 1))

    def one_microbatch(ex, loss_target):
        ex = dict(ex)
        diff = ex.pop(TWIN_DIFF_INPUT)
        return grad_fn(weights, diff, {**shared, **ex}, loss_target)

    if N_MICROBATCH == 1:
        loss, (grad_w, grad_x) = one_microbatch(per_example, given["loss_target"])
    else:
        def body(carry, xs):
            loss_sum, grad_sum = carry
            l_k, (gw_k, gx_k) = one_microbatch(xs[0], xs[1])
            with _jax.named_scope("update"):
                return (loss_sum + l_k, _jax.tree.map(_jnp.add, grad_sum, gw_k)), gx_k

        init = (_jnp.zeros((), _jnp.float32), _jax.tree.map(_jnp.zeros_like, weights))
        (loss, grad_w), grad_x = _jax.lax.scan(body, init, (per_example, given["loss_target"]))
    with _jax.named_scope("update"):
        delta_w, new_m, new_v = {}, {}, {}
        for n in TWIN_WEIGHTS:
            delta_w[n], new_m[n], new_v[n] = _adamw(weights[n], grad_w[n], given["m_" + n], given["v_" + n])
    return (loss, grad_x, *[grad_w[n] for n in TWIN_WEIGHTS], *[delta_w[n] for n in TWIN_WEIGHTS],
            *[new_m[n] for n in TWIN_WEIGHTS], *[new_v[n] for n in TWIN_WEIGHTS])
```

```python
import functools

import jax
import jax.numpy as jnp
import numpy as np
from jax import lax
from jax.experimental import pallas as pl
from jax.experimental.pallas import tpu as pltpu

F32 = jnp.float32
BF16 = jnp.bfloat16

D_MODEL = 1024
HEAD_DIM = 64
LANES = 128
BLK = 128
D_FF = 2816
RMS_EPS = 1e-6
MEM_LEN = 256
X_HEADS = 4
X_HEAD_DIM = 256
A_Q_HEADS = 8
A_GROUP = 4
A_WINDOW = 128
C_HEADS = 16
C_PATTERNS = ((128, 1), (512, 4), (2048, 16))
NEG = -1e30
VMEM_LIMIT = 56 * 2 ** 20

ADAM_LR = 0.001
ADAM_B1 = 0.9
ADAM_B2 = 0.999
ADAM_EPS = 1e-08
ADAM_WD = 0.01
ADAM_STEP = 10

N_DEV = 8
MESH = pl.DeviceIdType.MESH


def _cparams(n):
    return pltpu.CompilerParams(dimension_semantics=("arbitrary",) * n, vmem_limit_bytes=VMEM_LIMIT)


def _dot(a, b):
    return jnp.dot(a, b, preferred_element_type=F32)


def _dot_nt(a, b):
    return lax.dot_general(a, b, (((1,), (1,)), ((), ())), preferred_element_type=F32)


def _dot_tn(a, b):
    return lax.dot_general(a, b, (((0,), (0,)), ((), ())), preferred_element_type=F32)


def _sigmoid(z):
    return 1.0 / (1.0 + jnp.exp(-z))


def norm_matmul(x, g, w, *, tm, tn, split, name):
    T, K = x.shape
    N = w.shape[1]
    nj = N // tn

    def body(x_ref, g_ref, w_ref, o_ref, h_ref):
        @pl.when(pl.program_id(1) == 0)
        def _():
            xv = x_ref[...]
            r = lax.rsqrt(jnp.mean(xv * xv, axis=-1, keepdims=True) + RMS_EPS)
            h_ref[...] = (xv * r * g_ref[...]).astype(BF16)

        o_ref[...] = _dot(h_ref[...], w_ref[...])

    if split:
        njh = nj // 2
        o_shape = jax.ShapeDtypeStruct((2, T, N // 2), F32)
        o_spec = pl.BlockSpec((None, tm, tn), lambda i, j: (j // njh, i, j % njh))
    else:
        o_shape = jax.ShapeDtypeStruct((T, N), F32)
        o_spec = pl.BlockSpec((tm, tn), lambda i, j: (i, j))
    return pl.pallas_call(
        body, grid=(T // tm, nj),
        in_specs=[pl.BlockSpec((tm, K), lambda i, j: (i, 0)),
                  pl.BlockSpec((1, K), lambda i, j: (0, 0)),
                  pl.BlockSpec((K, tn), lambda i, j: (0, j))],
        out_specs=[o_spec, pl.BlockSpec((tm, K), lambda i, j: (i, 0))],
        out_shape=[o_shape, jax.ShapeDtypeStruct((T, K), BF16)],
        compiler_params=_cparams(2), name=name)(x, g, w)


def mm_nn(a, b, *, res, scale, swiglu, tm, tn, tk, name):
    T = a.shape[-2]
    K, N = b.shape
    nk = K // tk

    def body(*refs):
        if swiglu:
            g_ref, u_ref, b_ref, r_ref, o_ref, acc = refs
        else:
            a_ref, b_ref, r_ref, o_ref, acc = refs
        k = pl.program_id(2)

        @pl.when(k == 0)
        def _():
            acc[...] = jnp.zeros_like(acc)

        if swiglu:
            gv = g_ref[...]
            av = (gv * _sigmoid(gv) * u_ref[...]).astype(BF16)
        else:
            av = a_ref[...].astype(BF16)
        acc[...] += _dot(av, b_ref[...])

        @pl.when(k == nk - 1)
        def _():
            o_ref[...] = r_ref[...] + scale * acc[...]

    if swiglu:
        a_specs = [pl.BlockSpec((None, tm, tk), lambda i, j, k: (0, i, k)),
                   pl.BlockSpec((None, tm, tk), lambda i, j, k: (1, i, k))]
        a_args = [a, a]
    else:
        a_specs = [pl.BlockSpec((tm, tk), lambda i, j, k: (i, k))]
        a_args = [a]
    return pl.pallas_call(
        body, grid=(T // tm, N // tn, nk),
        in_specs=a_specs + [pl.BlockSpec((tk, tn), lambda i, j, k: (k, j)),
                            pl.BlockSpec((tm, tn), lambda i, j, k: (i, j))],
        out_specs=pl.BlockSpec((tm, tn), lambda i, j, k: (i, j)),
        out_shape=jax.ShapeDtypeStruct((T, N), F32),
        scratch_shapes=[pltpu.VMEM((tm, tn), F32)],
        compiler_params=_cparams(3), name=name)(*a_args, b, res)


def mm_nt(a, b, *, tm, tn, tk, name):
    T, K = a.shape
    N = b.shape[0]
    nk = K // tk

    def body(a_ref, b_ref, o_ref, acc):
        k = pl.program_id(2)

        @pl.when(k == 0)
        def _():
            acc[...] = jnp.zeros_like(acc)

        acc[...] += _dot_nt(a_ref[...].astype(BF16), b_ref[...])

        @pl.when(k == nk - 1)
        def _():
            o_ref[...] = acc[...]

    return pl.pallas_call(
        body, grid=(T // tm, N // tn, nk),
        in_specs=[pl.BlockSpec((tm, tk), lambda i, j, k: (i, k)),
                  pl.BlockSpec((tn, tk), lambda i, j, k: (j, k))],
        out_specs=pl.BlockSpec((tm, tn), lambda i, j, k: (i, j)),
        out_shape=jax.ShapeDtypeStruct((T, N), F32),
        scratch_shapes=[pltpu.VMEM((tm, tn), F32)],
        compiler_params=_cparams(3), name=name)(a, b)


def ffn_bwd_act(d, wd, gu, *, tm, tn, name):
    T, K = d.shape
    Fd = wd.shape[0]

    def body(d_ref, w_ref, g_ref, u_ref, dgu_ref, act_ref):
        da = 0.5 * _dot_nt(d_ref[...].astype(BF16), w_ref[...])
        gv = g_ref[...]
        uv = u_ref[...]
        s = _sigmoid(gv)
        silu = gv * s
        act_ref[...] = (silu * uv).astype(BF16)
        dgu_ref[0] = (da * uv * (s * (1.0 + gv * (1.0 - s)))).astype(BF16)
        dgu_ref[1] = (da * silu).astype(BF16)

    return pl.pallas_call(
        body, grid=(T // tm, Fd // tn),
        in_specs=[pl.BlockSpec((tm, K), lambda i, j: (i, 0)),
                  pl.BlockSpec((tn, K), lambda i, j: (j, 0)),
                  pl.BlockSpec((None, tm, tn), lambda i, j: (0, i, j)),
                  pl.BlockSpec((None, tm, tn), lambda i, j: (1, i, j))],
        out_specs=[pl.BlockSpec((2, tm, tn), lambda i, j: (0, i, j)),
                   pl.BlockSpec((tm, tn), lambda i, j: (i, j))],
        out_shape=[jax.ShapeDtypeStruct((2, T, Fd), BF16), jax.ShapeDtypeStruct((T, Fd), BF16)],
        compiler_params=_cparams(2), name=name)(d, wd, gu, gu)


def mm_nt_normbwd(a, b, x, g, res, *, a_split, tm, tk, name):
    T, Dm = x.shape
    K = b.shape[1]
    nk = K // tk
    nkh = nk // 2
    has_res = res is not None

    def body(*refs):
        if has_res:
            a_ref, b_ref, x_ref, g_ref, r_ref, dx_ref, dg_ref, acc = refs
        else:
            a_ref, b_ref, x_ref, g_ref, dx_ref, dg_ref, acc = refs
        i = pl.program_id(0)
        k = pl.program_id(1)

        @pl.when(k == 0)
        def _():
            acc[...] = jnp.zeros_like(acc)

        acc[...] += _dot_nt(a_ref[...].astype(BF16), b_ref[...])

        @pl.when(k == nk - 1)
        def _():
            xv = x_ref[...]
            r = lax.rsqrt(jnp.mean(xv * xv, axis=-1, keepdims=True) + RMS_EPS)
            xh = xv * r
            dh = acc[...]
            dxh = dh * g_ref[...]
            dx = r * (dxh - xh * jnp.mean(dxh * xh, axis=-1, keepdims=True))
            if has_res:
                dx = dx + r_ref[...]
            dx_ref[...] = dx
            part = jnp.sum(dh * xh, axis=0, keepdims=True)

            @pl.when(i == 0)
            def _():
                dg_ref[...] = part

            @pl.when(i > 0)
            def _():
                dg_ref[...] += part

    if a_split:
        a_spec = pl.BlockSpec((None, tm, tk), lambda i, k: (k // nkh, i, k % nkh))
    else:
        a_spec = pl.BlockSpec((tm, tk), lambda i, k: (i, k))
    in_specs = [a_spec,
                pl.BlockSpec((Dm, tk), lambda i, k: (0, k)),
                pl.BlockSpec((tm, Dm), lambda i, k: (i, 0)),
                pl.BlockSpec((1, Dm), lambda i, k: (0, 0))]
    args = [a, b, x, g]
    if has_res:
        in_specs.append(pl.BlockSpec((tm, Dm), lambda i, k: (i, 0)))
        args.append(res)
    return pl.pallas_call(
        body, grid=(T // tm, nk), in_specs=in_specs,
        out_specs=[pl.BlockSpec((tm, Dm), lambda i, k: (i, 0)),
                   pl.BlockSpec((1, Dm), lambda i, k: (0, 0))],
        out_shape=[jax.ShapeDtypeStruct((T, Dm), F32), jax.ShapeDtypeStruct((1, Dm), F32)],
        scratch_shapes=[pltpu.VMEM((tm, Dm), F32)],
        compiler_params=_cparams(2), name=name)(*args)


def mm_tn(a, b, *, scale, a_split, b_split, tm, tn, tk, name):
    T = a.shape[-2]
    M = a.shape[-1] * (2 if a_split else 1)
    N = b.shape[-1] * (2 if b_split else 1)
    ni, nj, nk = M // tm, N // tn, T // tk
    nih, njh = ni // 2, nj // 2

    def body(a_ref, b_ref, o_ref, acc):
        k = pl.program_id(2)

        @pl.when(k == 0)
        def _():
            acc[...] = jnp.zeros_like(acc)

        acc[...] += _dot_tn(a_ref[...].astype(BF16), b_ref[...].astype(BF16))

        @pl.when(k == nk - 1)
        def _():
            o_ref[...] = acc[...] * scale

    if a_split:
        a_spec = pl.BlockSpec((None, tk, tm), lambda i, j, k: (i // nih, k, i % nih))
    else:
        a_spec = pl.BlockSpec((tk, tm), lambda i, j, k: (k, i))
    if b_split:
        b_spec = pl.BlockSpec((None, tk, tn), lambda i, j, k: (j // njh, k, j % njh))
    else:
        b_spec = pl.BlockSpec((tk, tn), lambda i, j, k: (k, j))
    return pl.pallas_call(
        body, grid=(ni, nj, nk), in_specs=[a_spec, b_spec],
        out_specs=pl.BlockSpec((tm, tn), lambda i, j, k: (i, j)),
        out_shape=jax.ShapeDtypeStruct((M, N), F32),
        scratch_shapes=[pltpu.VMEM((tm, tn), F32)],
        compiler_params=_cparams(3), name=name)(a, b)


def loss_kernel(y, target, *, tm, name):
    T, Dm = y.shape

    def body(y_ref, t_ref, l_ref, dy_ref):
        e = y_ref[...] - t_ref[...]
        dy_ref[...] = e * (1.0 / Dm)
        part = (0.5 / Dm) * jnp.sum(jnp.sum(e * e, axis=-1, keepdims=True), axis=0, keepdims=True)
        part = jnp.broadcast_to(part, (8, LANES))

        @pl.when(pl.program_id(0) == 0)
        def _():
            l_ref[...] = part

        @pl.when(pl.program_id(0) > 0)
        def _():
            l_ref[...] += part

    return pl.pallas_call(
        body, grid=(T // tm,),
        in_specs=[pl.BlockSpec((tm, Dm), lambda i: (i, 0)), pl.BlockSpec((tm, Dm), lambda i: (i, 0))],
        out_specs=[pl.BlockSpec((8, LANES), lambda i: (0, 0)), pl.BlockSpec((tm, Dm), lambda i: (i, 0))],
        out_shape=[jax.ShapeDtypeStruct((8, LANES), F32), jax.ShapeDtypeStruct((T, Dm), F32)],
        compiler_params=_cparams(1), name=name)(y, target)


def adamw(w, g, m, v, *, br, name):
    R, C = w.shape

    def body(w_ref, g_ref, m_ref, v_ref, d_ref, nm_ref, nv_ref):
        gv = g_ref[...]
        nm = ADAM_B1 * m_ref[...] + (1.0 - ADAM_B1) * gv
        nv = ADAM_B2 * v_ref[...] + (1.0 - ADAM_B2) * (gv * gv)
        m_hat = nm / (1.0 - ADAM_B1 ** ADAM_STEP)
        v_hat = nv / (1.0 - ADAM_B2 ** ADAM_STEP)
        d_ref[...] = -ADAM_LR * (m_hat / (jnp.sqrt(v_hat) + ADAM_EPS) + ADAM_WD * w_ref[...])
        nm_ref[...] = nm
        nv_ref[...] = nv

    spec = pl.BlockSpec((br, C), lambda i: (i, 0))
    shp = jax.ShapeDtypeStruct((R, C), F32)
    return pl.pallas_call(
        body, grid=(R // br,), in_specs=[spec] * 4, out_specs=[spec] * 3, out_shape=[shp] * 3,
        compiler_params=_cparams(1), name=name)(w, g, m, v)


def _lane0():
    return lax.broadcasted_iota(jnp.int32, (1, LANES), 1) < HEAD_DIM


def _half_sum(x, m0):
    s0 = jnp.sum(jnp.where(m0, x, 0.0), axis=-1, keepdims=True)
    s1 = jnp.sum(jnp.where(m0, 0.0, x), axis=-1, keepdims=True)
    return jnp.where(m0, s0, s1)


def _half_pick(x, m0, e):
    sel = m0 if e == 0 else jnp.logical_not(m0)
    return jnp.max(jnp.where(sel, x, NEG), axis=-1, keepdims=True)


def _head_rms(x, m0):
    return lax.rsqrt(_half_sum(x * x, m0) * (1.0 / HEAD_DIM) + RMS_EPS)


def _alibi(n):
    return [float(2.0 ** (-8.0 * (h + 1) / n)) for h in range(n)]


def _mask_half(x, m0, e):
    return jnp.where(m0, x, 0.0) if e == 0 else jnp.where(m0, 0.0, x)


def _band_masks(max_dist, has_prev, live):
    row = lax.broadcasted_iota(jnp.int32, (BLK, BLK), 0)
    col = lax.broadcasted_iota(jnp.int32, (BLK, BLK), 1)
    dist_c = row - col
    dist_p = dist_c + BLK
    lim_c = jnp.where(live, max_dist, -1)
    lim_p = jnp.where(live & has_prev, max_dist, -1)
    valid_c = (dist_c >= 0) & (dist_c <= lim_c)
    valid_p = dist_p <= lim_p
    return dist_c.astype(F32), dist_p.astype(F32), valid_c, valid_p


def _rows(r, dil):
    return pl.ds(r, BLK, stride=dil) if dil > 1 else pl.ds(0, BLK)


def _band_specs(dil, ppk, q_blk, k_blk, v_blk, kv_shared, nb):
    RB = BLK * dil
    qw = LANES * ppk
    kw = LANES if kv_shared else qw

    def cur(i):
        return jnp.minimum(i, nb - 1)

    def kidx(base):
        return (lambda p, i: (cur(i), base)) if kv_shared else (lambda p, i: (cur(i), base + p))

    def pidx(base):
        return ((lambda p, i: (jnp.maximum(i - 1, 0), base)) if kv_shared
                else (lambda p, i: (jnp.maximum(i - 1, 0), base + p)))

    return [pl.BlockSpec((RB, qw), lambda p, i: (cur(i), q_blk + p)),
            pl.BlockSpec((RB, kw), kidx(k_blk)), pl.BlockSpec((RB, kw), pidx(k_blk)),
            pl.BlockSpec((RB, kw), kidx(v_blk)), pl.BlockSpec((RB, kw), pidx(v_blk))]


def banded_fwd(qkv, q_gain2, k_gain2, slopes, sinks, *, dil, ppk, q_blk, k_blk, v_blk, n_heads, group,
               max_dist, name):
    T = qkv.shape[0]
    RB = BLK * dil
    nb = T // RB
    npair = n_heads // 2
    kv_shared = group > 1
    scale = HEAD_DIM ** -0.5
    has_sink = sinks is not None

    def body(*refs):
        slope_ref = refs[0]
        if has_sink:
            sink_ref, refs = refs[1], refs[2:]
        else:
            refs = refs[1:]
        q_ref, kc_ref, kp_ref, vc_ref, vp_ref, qg_ref, kg_ref, o_ref, l_ref = refs
        pb = pl.program_id(0)
        i = pl.program_id(1)
        m0 = _lane0()
        dcf, dpf, valid_c, valid_p = _band_masks(max_dist, i > 0, i >= 0)
        qg = qg_ref[...]
        kg = kg_ref[...]
        for r in range(dil):
            rows = _rows(r, dil)
            kcache = {}
            for jp in range(ppk):
                cs = pl.ds(LANES * jp, LANES)
                jk = 0 if kv_shared else jp
                if jk not in kcache:
                    ks = pl.ds(LANES * jk, LANES)
                    kc = kc_ref[rows, ks]
                    kp = kp_ref[rows, ks]
                    kcache[jk] = ((kc * _head_rms(kc, m0) * kg).astype(BF16),
                                  (kp * _head_rms(kp, m0) * kg).astype(BF16),
                                  vc_ref[rows, ks], vp_ref[rows, ks])
                knc, knp, vc, vp = kcache[jk]
                qv = q_ref[rows, cs]
                qn = qv * _head_rms(qv, m0) * qg
                o_pair = jnp.zeros((BLK, LANES), F32)
                lse_pair = jnp.zeros((BLK, LANES), F32)
                for e in range(2):
                    ke = ((2 * jp + e) // group) % 2 if kv_shared else e
                    hidx = 2 * (pb * ppk + jp) + e
                    slope = slope_ref[hidx]
                    qm = _mask_half(qn, m0, e)
                    if ke != e:
                        qm = pltpu.roll(qm, HEAD_DIM, 1)
                    qm = qm.astype(BF16)
                    s_c = jnp.where(valid_c, _dot_nt(qm, knc) * scale - slope * dcf, NEG)
                    s_p = jnp.where(valid_p, _dot_nt(qm, knp) * scale - slope * dpf, NEG)
                    m = jnp.maximum(jnp.max(s_c, axis=-1, keepdims=True), jnp.max(s_p, axis=-1, keepdims=True))
                    if has_sink:
                        sk = sink_ref[hidx]
                        m = jnp.maximum(m, sk)
                    p_c = jnp.exp(s_c - m)
                    p_p = jnp.exp(s_p - m)
                    den = jnp.sum(p_c, axis=-1, keepdims=True) + jnp.sum(p_p, axis=-1, keepdims=True)
                    if has_sink:
                        den = den + jnp.exp(sk - m)
                    inv = 1.0 / den
                    o_h = (_dot((p_c * inv).astype(BF16), _mask_half(vc, m0, ke).astype(BF16))
                           + _dot((p_p * inv).astype(BF16), _mask_half(vp, m0, ke).astype(BF16)))
                    if ke != e:
                        o_h = pltpu.roll(o_h, HEAD_DIM, 1)
                    o_pair = o_pair + o_h
                    sel = m0 if e == 0 else jnp.logical_not(m0)
                    lse_pair = jnp.where(sel, m + jnp.log(den), lse_pair)
                o_ref[rows, cs] = o_pair
                l_ref[rows, cs] = lse_pair

    smem = pl.BlockSpec(memory_space=pltpu.SMEM)
    qw = LANES * ppk
    gspec = pl.BlockSpec((1, LANES), lambda p, i: (0, 0))
    ospec = pl.BlockSpec((RB, qw), lambda p, i: (i, p))
    oshape = jax.ShapeDtypeStruct((T, n_heads * HEAD_DIM), F32)
    args = [slopes] + ([sinks] if has_sink else []) + [qkv] * 5 + [q_gain2, k_gain2]
    return pl.pallas_call(
        body, grid=(npair // ppk, nb),
        in_specs=[smem] * (2 if has_sink else 1) + _band_specs(dil, ppk, q_blk, k_blk, v_blk, kv_shared, nb)
        + [gspec, gspec],
        out_specs=[ospec, ospec], out_shape=[oshape, oshape],
        compiler_params=_cparams(2), name=name)(*args)


def banded_bwd(qkv, q_gain2, k_gain2, slopes, sinks, do, o, lse, w, omix, *, dil, ppk, q_blk, k_blk, v_blk,
               n_heads, group, max_dist, do_blk, name):
    T = qkv.shape[0]
    RB = BLK * dil
    nb = T // RB
    npair = n_heads // 2
    kv_shared = group > 1
    scale = HEAD_DIM ** -0.5
    has_sink = sinks is not None
    mixed = w is not None
    qw = LANES * ppk

    def body(*refs):
        slope_ref = refs[0]
        if has_sink:
            sink_ref, refs = refs[1], refs[2:]
        else:
            refs = refs[1:]
        q_ref, kc_ref, kp_ref, vc_ref, vp_ref, qg_ref, kg_ref, do_ref, o_ref, l_ref = refs[:10]
        refs = refs[10:]
        if mixed:
            w_ref, om_ref, refs = refs[0], refs[1], refs[2:]
        dq_ref, dk_ref, dv_ref, dqg_ref, dkg_ref, dsk_ref, ck_ref, cv_ref = refs
        pb = pl.program_id(0)
        i = pl.program_id(1)
        live = i < nb
        m0 = _lane0()
        lane = lax.broadcasted_iota(jnp.int32, (1, LANES), 1)
        dcf, dpf, valid_c, valid_p = _band_masks(max_dist, i > 0, live)
        livef = live.astype(F32)
        qg = qg_ref[...]
        kg = kg_ref[...]

        @pl.when((pb == 0) & (i == 0))
        def _():
            dqg_ref[...] = jnp.zeros_like(dqg_ref)
            dkg_ref[...] = jnp.zeros_like(dkg_ref)
            dsk_ref[...] = jnp.zeros_like(dsk_ref)

        @pl.when(i == 0)
        def _():
            ck_ref[...] = jnp.zeros_like(ck_ref)
            cv_ref[...] = jnp.zeros_like(cv_ref)

        dqg_acc = jnp.zeros((1, LANES), F32)
        dkg_acc = jnp.zeros((1, LANES), F32)
        dsk_acc = jnp.zeros((1, LANES), F32)
        for r in range(dil):
            rows = _rows(r, dil)
            for jp in range(ppk):
                cs = pl.ds(LANES * jp, LANES)
                ks = pl.ds(0, LANES) if kv_shared else cs
                kc = kc_ref[rows, ks]
                kp = kp_ref[rows, ks]
                rkc = _head_rms(kc, m0)
                rkp = _head_rms(kp, m0)
                khc = kc * rkc
                khp = kp * rkp
                knc = (khc * kg).astype(BF16)
                knp = (khp * kg).astype(BF16)
                vc = vc_ref[rows, ks].astype(BF16)
                vp = vp_ref[rows, ks].astype(BF16)
                qv = q_ref[rows, cs]
                rq = _head_rms(qv, m0)
                qh = qv * rq
                qn = qh * qg
                dov = do_ref[rows, cs]
                ov = o_ref[rows, cs]
                lv = l_ref[rows, cs]
                if mixed:
                    wv = w_ref[rows, cs]
                    dmix = _half_sum(dov * om_ref[rows, cs], m0)
                    dov = dov * wv
                delta2 = _half_sum(dov * ov, m0)
                corr2 = (delta2 - wv * dmix) if mixed else None
                dqn = jnp.zeros((BLK, LANES), F32)
                dknc = jnp.zeros((BLK, LANES), F32)
                dknp = jnp.zeros((BLK, LANES), F32)
                dvc = jnp.zeros((BLK, LANES), F32)
                dvp = jnp.zeros((BLK, LANES), F32)
                for e in range(2):
                    ke = ((2 * jp + e) // group) % 2 if kv_shared else e
                    hidx = 2 * (pb * ppk + jp) + e
                    slope = slope_ref[hidx]
                    qm = _mask_half(qn, m0, e)
                    dom = _mask_half(dov, m0, e)
                    if ke != e:
                        qm = pltpu.roll(qm, HEAD_DIM, 1)
                        dom = pltpu.roll(dom, HEAD_DIM, 1)
                    qm = qm.astype(BF16)
                    dom = dom.astype(BF16)
                    lse = _half_pick(lv, m0, e)
                    delta = _half_pick(delta2, m0, e)
                    shift = delta - _half_pick(corr2, m0, e) if mixed else delta
                    p_c = jnp.where(valid_c, jnp.exp(_dot_nt(qm, knc) * scale - slope * dcf - lse), 0.0)
                    p_p = jnp.where(valid_p, jnp.exp(_dot_nt(qm, knp) * scale - slope * dpf - lse), 0.0)
                    ds_c = (p_c * (_dot_nt(dom, vc) - shift)).astype(BF16)
                    ds_p = (p_p * (_dot_nt(dom, vp) - shift)).astype(BF16)
                    dq_h = (_dot(ds_c, _mask_half(knc, m0, ke)) + _dot(ds_p, _mask_half(knp, m0, ke))) * scale
                    if ke != e:
                        dq_h = pltpu.roll(dq_h, HEAD_DIM, 1)
                    dqn = dqn + dq_h
                    dknc = dknc + _dot_tn(ds_c, qm) * scale
                    dknp = dknp + _dot_tn(ds_p, qm) * scale
                    dvc = dvc + _dot_tn(p_c.astype(BF16), dom)
                    dvp = dvp + _dot_tn(p_p.astype(BF16), dom)
                    if has_sink:
                        psk = jnp.exp(sink_ref[hidx] - lse) * livef
                        tot = jnp.sum(-psk * delta, axis=0, keepdims=True)
                        dsk_acc = dsk_acc + jnp.where(lane == (2 * jp + e), tot, 0.0)
                dqg_acc = dqg_acc + jnp.sum(dqn * qh, axis=0, keepdims=True)
                dqh = dqn * qg
                dq_raw = rq * (dqh - qh * (_half_sum(dqh * qh, m0) * (1.0 / HEAD_DIM)))
                dkg_acc = dkg_acc + jnp.sum(dknc * khc + dknp * khp, axis=0, keepdims=True)
                dkhc = dknc * kg
                dkhp = dknp * kg
                dkc_raw = rkc * (dkhc - khc * (_half_sum(dkhc * khc, m0) * (1.0 / HEAD_DIM)))
                dkp_raw = rkp * (dkhp - khp * (_half_sum(dkhp * khp, m0) * (1.0 / HEAD_DIM)))

                @pl.when(live)
                def _():
                    dq_ref[rows, cs] = dq_raw

                dk_ref[rows, cs] = ck_ref[rows, cs] + dkp_raw
                dv_ref[rows, cs] = cv_ref[rows, cs] + dvp
                ck_ref[rows, cs] = dkc_raw
                cv_ref[rows, cs] = dvc
        dqg_ref[...] += dqg_acc
        dkg_ref[...] += dkg_acc
        dsk_ref[...] += dsk_acc

    smem = pl.BlockSpec(memory_space=pltpu.SMEM)
    gspec = pl.BlockSpec((1, LANES), lambda p, i: (0, 0))

    def cur(i):
        return jnp.minimum(i, nb - 1)

    qspec = pl.BlockSpec((RB, qw), lambda p, i: (cur(i), p))
    dospec = pl.BlockSpec((RB, qw), lambda p, i: (cur(i), do_blk + p))
    kvout = pl.BlockSpec((RB, qw), lambda p, i: (jnp.maximum(i - 1, 0), p))
    in_specs = ([smem] * (2 if has_sink else 1) + _band_specs(dil, ppk, q_blk, k_blk, v_blk, kv_shared, nb)
                + [gspec, gspec, dospec, qspec, qspec] + ([qspec, qspec] if mixed else []))
    args = ([slopes] + ([sinks] if has_sink else []) + [qkv] * 5 + [q_gain2, k_gain2, do, o, lse]
            + ([w, omix] if mixed else []))
    full = jax.ShapeDtypeStruct((T, n_heads * HEAD_DIM), F32)
    row = jax.ShapeDtypeStruct((1, LANES), F32)
    return pl.pallas_call(
        body, grid=(npair // ppk, nb + 1), in_specs=in_specs,
        out_specs=[qspec, kvout, kvout, gspec, gspec, gspec],
        out_shape=[full, full, full, row, row, row],
        scratch_shapes=[pltpu.VMEM((RB, qw), F32), pltpu.VMEM((RB, qw), F32)],
        compiler_params=_cparams(2), name=name)(*args)


def mix_fwd(o1, o2, o3, l1, l2, l3, *, tm, name):
    T, C = o1.shape

    def body(o1r, o2r, o3r, l1r, l2r, l3r, o_ref, w1r, w2r, w3r):
        a, b, c = l1r[...], l2r[...], l3r[...]
        m = jnp.maximum(jnp.maximum(a, b), c)
        ea, eb, ec = jnp.exp(a - m), jnp.exp(b - m), jnp.exp(c - m)
        inv = 1.0 / (ea + eb + ec)
        wa, wb, wc = ea * inv, eb * inv, ec * inv
        o_ref[...] = wa * o1r[...] + wb * o2r[...] + wc * o3r[...]
        w1r[...] = wa
        w2r[...] = wb
        w3r[...] = wc

    spec = pl.BlockSpec((tm, C), lambda i: (i, 0))
    shp = jax.ShapeDtypeStruct((T, C), F32)
    return pl.pallas_call(body, grid=(T // tm,), in_specs=[spec] * 6, out_specs=[spec] * 4, out_shape=[shp] * 4,
                          compiler_params=_cparams(1), name=name)(o1, o2, o3, l1, l2, l3)


def assemble_odd(parts, *, tm, name):
    T, C = parts[0][0].shape

    def body(*refs):
        o_ref = refs[9]
        for j in range(3):
            o_ref[:, pl.ds(C * j, C)] = refs[j][...] + refs[3 + j][...] + refs[6 + j][...]

    spec = pl.BlockSpec((tm, C), lambda i: (i, 0))
    flat = [parts[p][j] for p in range(3) for j in range(3)]
    return pl.pallas_call(body, grid=(T // tm,), in_specs=[spec] * 9,
                          out_specs=pl.BlockSpec((tm, 3 * C), lambda i: (i, 0)),
                          out_shape=jax.ShapeDtypeStruct((T, 3 * C), F32),
                          compiler_params=_cparams(1), name=name)(*flat)


def assemble_even(dqa, dka4, dva4, dqb, dkb, dvb, *, tm, name):
    T = dqa.shape[0]
    W = 512

    def body(dqa_r, dka_r, dva_r, dqb_r, dkb_r, dvb_r, o_ref):
        o_ref[:, pl.ds(0, W)] = dqa_r[...]
        ka = dka_r[...]
        va = dva_r[...]
        o_ref[:, pl.ds(512, LANES)] = ka[:, 0:128] + ka[:, 128:256] + ka[:, 256:384] + ka[:, 384:512]
        o_ref[:, pl.ds(640, LANES)] = va[:, 0:128] + va[:, 128:256] + va[:, 256:384] + va[:, 384:512]
        o_ref[:, pl.ds(768, W)] = dqb_r[...]
        o_ref[:, pl.ds(1280, W)] = dkb_r[...]
        o_ref[:, pl.ds(1792, W)] = dvb_r[...]

    spec = pl.BlockSpec((tm, W), lambda i: (i, 0))
    return pl.pallas_call(body, grid=(T // tm,), in_specs=[spec] * 6,
                          out_specs=pl.BlockSpec((tm, 2304), lambda i: (i, 0)),
                          out_shape=jax.ShapeDtypeStruct((T, 2304), F32),
                          compiler_params=_cparams(1), name=name)(dqa, dka4, dva4, dqb, dkb, dvb)


STICK_T = 256


def _split_bf16(x):
    hi = x.astype(BF16)
    lo = (x - hi.astype(F32)).astype(BF16)
    return hi, lo


def _stick_logits(qm, kt, scale, diag):
    n = STICK_T
    row = lax.broadcasted_iota(jnp.int32, (n, n), 0)
    col = lax.broadcasted_iota(jnp.int32, (n, n), 1)
    mask = col < row + jnp.where(diag, 0, n)
    z = _dot_nt(qm, kt) * scale
    lneg = -(jnp.maximum(z, 0.0) + jnp.log(1.0 + jnp.exp(-jnp.abs(z))))
    lpos = z + lneg
    lk = jnp.where(mask, lneg, 0.0)
    return mask, lpos, lneg, lk


def _cumsum_mm(x, tri):
    hi, lo = _split_bf16(x)
    return _dot(hi, tri) + _dot(lo, tri)


def stick_fwd(qkv, *, q_blk, k_blk, v_blk, n_pairs, name):
    T = qkv.shape[0]
    n = STICK_T
    nq = T // n
    scale = HEAD_DIM ** -0.5

    def body(q_ref, k_ref, v_ref, o_ref, c_ref):
        i = pl.program_id(1)
        m0 = _lane0()
        r2 = lax.broadcasted_iota(jnp.int32, (n, n), 0)
        c2 = lax.broadcasted_iota(jnp.int32, (n, n), 1)
        tri_after = (r2 > c2).astype(BF16)
        qv = q_ref[...]
        out = jnp.zeros((n, LANES), F32)
        ctot = jnp.zeros((n, LANES), F32)
        for e in range(2):
            qm = _mask_half(qv, m0, e).astype(BF16)

            def step(t, carry_state, e=e, qm=qm):
                acc, carry = carry_state
                start = pl.multiple_of((i - t) * n, n)
                kt = k_ref[pl.ds(start, n), :].astype(BF16)
                vt = _mask_half(v_ref[pl.ds(start, n), :], m0, e).astype(BF16)
                mask, lpos, _, lk = _stick_logits(qm, kt, scale, t == 0)
                after = _cumsum_mm(lk, tri_after) + carry
                a = jnp.where(mask, jnp.exp(lpos + after), 0.0)
                acc = acc + _dot(a.astype(BF16), vt)
                carry = carry + jnp.sum(lk, axis=-1, keepdims=True)
                return acc, carry

            acc, carry = lax.fori_loop(0, i + 1, step, (jnp.zeros((n, LANES), F32), jnp.zeros((n, 1), F32)))
            out = out + acc
            sel = m0 if e == 0 else jnp.logical_not(m0)
            ctot = jnp.where(sel, carry, ctot)
        o_ref[...] = out
        c_ref[...] = ctot

    tile = pl.BlockSpec((n, LANES), lambda p, i: (i, p))
    shp = jax.ShapeDtypeStruct((T, n_pairs * LANES), F32)
    return pl.pallas_call(
        body, grid=(n_pairs, nq),
        in_specs=[pl.BlockSpec((n, LANES), lambda p, i: (i, q_blk + p)),
                  pl.BlockSpec((T, LANES), lambda p, i: (0, k_blk + p)),
                  pl.BlockSpec((T, LANES), lambda p, i: (0, v_blk + p))],
        out_specs=[tile, tile], out_shape=[shp, shp],
        compiler_params=_cparams(2), name=name)(qkv, qkv, qkv)


def stick_bwd(qkv, do, ctot, *, q_blk, k_blk, v_blk, do_blk, n_pairs, name):
    T = qkv.shape[0]
    n = STICK_T
    nq = T // n
    scale = HEAD_DIM ** -0.5

    def body(q_ref, k_ref, v_ref, do_ref, c_ref, dq_ref, dk_ref, dv_ref):
        i = pl.program_id(1)
        m0 = _lane0()
        r2 = lax.broadcasted_iota(jnp.int32, (n, n), 0)
        c2 = lax.broadcasted_iota(jnp.int32, (n, n), 1)
        tri_upto = (r2 <= c2).astype(BF16)
        tri_before = (r2 < c2).astype(BF16)

        @pl.when(i == 0)
        def _():
            dk_ref[...] = jnp.zeros_like(dk_ref)
            dv_ref[...] = jnp.zeros_like(dv_ref)

        qv = q_ref[...]
        dov = do_ref[...]
        cv = c_ref[...]
        dq_out = jnp.zeros((n, LANES), F32)
        for e in range(2):
            qm = _mask_half(qv, m0, e).astype(BF16)
            dom = _mask_half(dov, m0, e).astype(BF16)
            ctot = _half_pick(cv, m0, e)

            def step(t, st, e=e, qm=qm, dom=dom, ctot=ctot):
                dq_acc, pc, hc = st
                start = pl.multiple_of(t * n, n)
                kt_f = k_ref[pl.ds(start, n), :]
                kt = kt_f.astype(BF16)
                vt = v_ref[pl.ds(start, n), :].astype(BF16)
                mask, lpos, lneg, lk = _stick_logits(qm, kt, scale, t == i)
                upto = _cumsum_mm(lk, tri_upto) + pc
                a = jnp.where(mask, jnp.exp(lpos + (ctot - upto)), 0.0)
                g = _dot_nt(dom, vt) * a
                before = _cumsum_mm(g, tri_before) + hc
                dz = jnp.where(mask, g * jnp.exp(lneg) - before * jnp.exp(lpos), 0.0) * scale
                dzb = dz.astype(BF16)
                dq_acc = dq_acc + _dot(dzb, _mask_half(kt_f, m0, e).astype(BF16))
                dk_ref[pl.ds(start, n), :] += _dot_tn(dzb, qm)
                dv_ref[pl.ds(start, n), :] += _dot_tn(a.astype(BF16), dom)
                pc = pc + jnp.sum(lk, axis=-1, keepdims=True)
                hc = hc + jnp.sum(g, axis=-1, keepdims=True)
                return dq_acc, pc, hc

            z1 = jnp.zeros((n, 1), F32)
            dq_acc, _, _ = lax.fori_loop(0, i + 1, step, (jnp.zeros((n, LANES), F32), z1, z1))
            dq_out = dq_out + dq_acc
        dq_ref[...] = dq_out

    tile = pl.BlockSpec((n, LANES), lambda p, i: (i, p))
    whole = pl.BlockSpec((T, LANES), lambda p, i: (0, p))
    shp = jax.ShapeDtypeStruct((T, n_pairs * LANES), F32)
    return pl.pallas_call(
        body, grid=(n_pairs, nq),
        in_specs=[pl.BlockSpec((n, LANES), lambda p, i: (i, q_blk + p)),
                  pl.BlockSpec((T, LANES), lambda p, i: (0, k_blk + p)),
                  pl.BlockSpec((T, LANES), lambda p, i: (0, v_blk + p)),
                  pl.BlockSpec((n, LANES), lambda p, i: (i, do_blk + p)),
                  tile],
        out_specs=[tile, whole, whole], out_shape=[shp, shp, shp],
        compiler_params=_cparams(2), name=name)(qkv, qkv, qkv, do, ctot)


def _xnorm(x):
    r = lax.rsqrt(jnp.mean(x * x, axis=-1, keepdims=True) + RMS_EPS)
    return r, x * r


def xattn_fwd(qraw, kvraw, q_gain, k_gain, *, tm, name):
    T = qraw.shape[0]
    scale = X_HEAD_DIM ** -0.5
    W = X_HEADS * X_HEAD_DIM

    def body(q_ref, kv_ref, qg_ref, kg_ref, o_ref):
        for h in range(X_HEADS):
            cs = pl.ds(X_HEAD_DIM * h, X_HEAD_DIM)
            _, qh = _xnorm(q_ref[:, cs])
            _, kh = _xnorm(kv_ref[:, cs])
            qn = (qh * qg_ref[...]).astype(BF16)
            kn = (kh * kg_ref[...]).astype(BF16)
            v = kv_ref[:, pl.ds(W + X_HEAD_DIM * h, X_HEAD_DIM)].astype(BF16)
            s = _dot_nt(qn, kn) * scale
            m = jnp.max(s, axis=-1, keepdims=True)
            p = jnp.exp(s - m)
            p = p / jnp.sum(p, axis=-1, keepdims=True)
            o_ref[:, cs] = _dot(p.astype(BF16), v)

    gspec = pl.BlockSpec((1, X_HEAD_DIM), lambda i: (0, 0))
    return pl.pallas_call(
        body, grid=(T // tm,),
        in_specs=[pl.BlockSpec((tm, W), lambda i: (i, 0)), pl.BlockSpec((MEM_LEN, 2 * W), lambda i: (0, 0)),
                  gspec, gspec],
        out_specs=pl.BlockSpec((tm, W), lambda i: (i, 0)),
        out_shape=jax.ShapeDtypeStruct((T, W), F32),
        compiler_params=_cparams(1), name=name)(qraw, kvraw, q_gain, k_gain)


def xattn_bwd(qraw, kvraw, q_gain, k_gain, do, o, *, tm, name):
    T = qraw.shape[0]
    nt = T // tm
    scale = X_HEAD_DIM ** -0.5
    W = X_HEADS * X_HEAD_DIM

    def body(q_ref, kv_ref, qg_ref, kg_ref, do_ref, o_ref, dq_ref, dkv_ref, dqg_ref, dkg_ref, dkn_ref):
        i = pl.program_id(0)

        @pl.when(i == 0)
        def _():
            dkv_ref[...] = jnp.zeros_like(dkv_ref)
            dkn_ref[...] = jnp.zeros_like(dkn_ref)
            dqg_ref[...] = jnp.zeros_like(dqg_ref)
            dkg_ref[...] = jnp.zeros_like(dkg_ref)

        qg = qg_ref[...]
        kg = kg_ref[...]
        dqg_acc = jnp.zeros((1, X_HEAD_DIM), F32)
        for h in range(X_HEADS):
            cs = pl.ds(X_HEAD_DIM * h, X_HEAD_DIM)
            vs = pl.ds(W + X_HEAD_DIM * h, X_HEAD_DIM)
            rq, qh = _xnorm(q_ref[:, cs])
            _, kh = _xnorm(kv_ref[:, cs])
            qn = (qh * qg).astype(BF16)
            kn = (kh * kg).astype(BF16)
            v = kv_ref[:, vs].astype(BF16)
            s = _dot_nt(qn, kn) * scale
            m = jnp.max(s, axis=-1, keepdims=True)
            p = jnp.exp(s - m)
            p = p / jnp.sum(p, axis=-1, keepdims=True)
            dov = do_ref[:, cs]
            delta = jnp.sum(dov * o_ref[:, cs], axis=-1, keepdims=True)
            dob = dov.astype(BF16)
            ds = (p * (_dot_nt(dob, v) - delta)).astype(BF16)
            dqn = _dot(ds, kn) * scale
            dkn_ref[:, cs] += _dot_tn(ds, qn) * scale
            dkv_ref[:, vs] += _dot_tn(p.astype(BF16), dob)
            dqg_acc = dqg_acc + jnp.sum(dqn * qh, axis=0, keepdims=True)
            dqh = dqn * qg
            dq_ref[:, cs] = rq * (dqh - qh * jnp.mean(dqh * qh, axis=-1, keepdims=True))
        dqg_ref[...] += dqg_acc

        @pl.when(i == nt - 1)
        def _():
            dkg_acc = jnp.zeros((1, X_HEAD_DIM), F32)
            for h in range(X_HEADS):
                cs = pl.ds(X_HEAD_DIM * h, X_HEAD_DIM)
                rk, kh = _xnorm(kv_ref[:, cs])
                dkn = dkn_ref[:, cs]
                dkg_acc = dkg_acc + jnp.sum(dkn * kh, axis=0, keepdims=True)
                dkh = dkn * kg
                dkv_ref[:, cs] = rk * (dkh - kh * jnp.mean(dkh * kh, axis=-1, keepdims=True))
            dkg_ref[...] = dkg_acc

    gspec = pl.BlockSpec((1, X_HEAD_DIM), lambda i: (0, 0))
    tile = pl.BlockSpec((tm, W), lambda i: (i, 0))
    kvspec = pl.BlockSpec((MEM_LEN, 2 * W), lambda i: (0, 0))
    grow = jax.ShapeDtypeStruct((1, X_HEAD_DIM), F32)
    return pl.pallas_call(
        body, grid=(nt,), in_specs=[tile, kvspec, gspec, gspec, tile, tile],
        out_specs=[tile, kvspec, gspec, gspec],
        out_shape=[jax.ShapeDtypeStruct((T, W), F32), jax.ShapeDtypeStruct((MEM_LEN, 2 * W), F32), grow, grow],
        scratch_shapes=[pltpu.VMEM((MEM_LEN, W), F32)],
        compiler_params=_cparams(1), name=name)(qraw, kvraw, q_gain, k_gain, do, o)


_ANY = pl.BlockSpec(memory_space=pl.ANY)


def _my_pos():
    return lax.axis_index("x"), lax.axis_index("y"), lax.axis_index("c")


def all_gather8(blk, *, name):
    R, C = blk.shape

    def body(x_ref, out_ref, send_sems, recv_sems, local_sem):
        x, y, c = _my_pos()
        me, sibling = (x, y, c), (x, y, 1 - c)
        chips = [(1 - x, y), (x, 1 - y), (1 - x, 1 - y)]

        def slot(px, py, pc):
            return out_ref.at[4 * px + 2 * py + pc]

        def copy(k, block, to, src=None):
            return pltpu.make_async_remote_copy(
                src_ref=slot(*block) if src is None else src, dst_ref=slot(*block),
                send_sem=send_sems.at[k], recv_sem=recv_sems.at[k], device_id=to, device_id_type=MESH)

        mine = pltpu.make_async_copy(x_ref, slot(*me), local_sem)
        mine.start()
        first = [copy(0, me, sibling, src=x_ref)]
        first += [copy(1 + j, me, (*chip, c), src=x_ref) for j, chip in enumerate(chips)]
        for cp in first:
            cp.start()
        passed = [copy(4 + j, (*chip, c), sibling) for j, chip in enumerate(chips)]
        for j, chip in enumerate(chips):
            copy(1 + j, (*chip, c), me).wait_recv()
            passed[j].start()
        copy(0, sibling, me).wait_recv()
        for j, chip in enumerate(chips):
            copy(4 + j, (*chip, 1 - c), me).wait_recv()
        for cp in first + passed:
            cp.wait_send()
        mine.wait()

    return pl.pallas_call(
        body, out_shape=jax.ShapeDtypeStruct((N_DEV, R, C), blk.dtype),
        in_specs=[_ANY], out_specs=_ANY,
        scratch_shapes=[pltpu.SemaphoreType.DMA((7,)), pltpu.SemaphoreType.DMA((7,)), pltpu.SemaphoreType.DMA],
        name=name)(blk)


def all_to_all8(big, small, *, name):
    _, R, C = big.shape
    S, C2 = small.shape

    def body(big_ref, small_ref, land_ref, lands_ref, send_sems, recv_sems, ssend_sems, srecv_sems, local_sems):
        x, y, c = _my_pos()
        my_id = 4 * x + 2 * y + c

        def peer(k):
            return x ^ ((k >> 2) & 1), y ^ ((k >> 1) & 1), c ^ (k & 1)

        def copies(k, to_slot):
            px, py, pc = peer(k)
            peer_id = 4 * px + 2 * py + pc
            dst = my_id if to_slot == "mine" else peer_id
            cp = pltpu.make_async_remote_copy(
                src_ref=big_ref.at[peer_id], dst_ref=land_ref.at[dst],
                send_sem=send_sems.at[k - 1], recv_sem=recv_sems.at[k - 1],
                device_id=(px, py, pc), device_id_type=MESH)
            cps = pltpu.make_async_remote_copy(
                src_ref=small_ref, dst_ref=lands_ref.at[dst],
                send_sem=ssend_sems.at[k - 1], recv_sem=srecv_sems.at[k - 1],
                device_id=(px, py, pc), device_id_type=MESH)
            return cp, cps

        own = pltpu.make_async_copy(big_ref.at[my_id], land_ref.at[my_id], local_sems.at[0])
        own.start()
        owns = pltpu.make_async_copy(small_ref, lands_ref.at[my_id], local_sems.at[1])
        owns.start()
        sends = []
        for k in range(1, N_DEV):
            cp, cps = copies(k, "mine")
            cp.start()
            cps.start()
            sends += [cp, cps]
        for k in range(1, N_DEV):
            cp, cps = copies(k, "theirs")
            cp.wait_recv()
            cps.wait_recv()
        for cp in sends:
            cp.wait_send()
        own.wait()
        owns.wait()

    dma7 = pltpu.SemaphoreType.DMA((7,))
    return pl.pallas_call(
        body,
        out_shape=[jax.ShapeDtypeStruct((N_DEV, R, C), big.dtype), jax.ShapeDtypeStruct((N_DEV, S, C2), small.dtype)],
        in_specs=[_ANY, _ANY], out_specs=[_ANY, _ANY],
        scratch_shapes=[dma7, dma7, dma7, dma7, pltpu.SemaphoreType.DMA((2,))],
        name=name)(big, small)


def sibling_swap(blk, *, name):
    R, C = blk.shape

    def body(x_ref, out_ref, send_sem, recv_sem, local_sem):
        x, y, c = _my_pos()

        def copy(slot):
            return pltpu.make_async_remote_copy(src_ref=x_ref, dst_ref=out_ref.at[slot], send_sem=send_sem,
                                                recv_sem=recv_sem, device_id=(x, y, 1 - c), device_id_type=MESH)

        own = pltpu.make_async_copy(x_ref, out_ref.at[c], local_sem)
        own.start()
        cp = copy(c)
        cp.start()
        copy(1 - c).wait_recv()
        cp.wait_send()
        own.wait()

    return pl.pallas_call(
        body, out_shape=jax.ShapeDtypeStruct((2, R, C), blk.dtype), in_specs=[_ANY], out_specs=_ANY,
        scratch_shapes=[pltpu.SemaphoreType.DMA, pltpu.SemaphoreType.DMA, pltpu.SemaphoreType.DMA],
        name=name)(blk)


def reduce_slots(land, *, tr, name):
    n, R, C = land.shape

    def body(l_ref, o_ref):
        acc = l_ref[0].astype(F32)
        for s in range(1, n):
            acc = acc + l_ref[s].astype(F32)
        o_ref[...] = acc

    return pl.pallas_call(
        body, grid=(R // tr,), in_specs=[pl.BlockSpec((n, tr, C), lambda i: (0, i, 0))],
        out_specs=pl.BlockSpec((tr, C), lambda i: (i, 0)), out_shape=jax.ShapeDtypeStruct((R, C), F32),
        compiler_params=_cparams(1), name=name)(land)


TM = 512


def ffn_fwd(x, g, wgu, wd, tag):
    gu, h = norm_matmul(x, g, wgu, tm=TM, tn=1408, split=True, name=f"{tag}_gu")
    xo = mm_nn(gu, wd, res=x, scale=0.5, swiglu=True, tm=TM, tn=D_MODEL, tk=256, name=f"{tag}_down")
    return xo, (x, gu, h)


def ffn_bwd(d, saved, g, wgu, wd, tag):
    x, gu, h = saved
    dgu, act = ffn_bwd_act(d, wd, gu, tm=256, tn=1408, name=f"{tag}_bact")
    dwd = mm_tn(act, d, scale=0.5, a_split=False, b_split=False, tm=1408, tn=D_MODEL, tk=TM, name=f"{tag}_dwd")
    dwgu = mm_tn(h, dgu, scale=1.0, a_split=False, b_split=True, tm=D_MODEL, tn=1408, tk=TM, name=f"{tag}_dwgu")
    dx, dg = mm_nt_normbwd(dgu, wgu, x, g, d, a_split=True, tm=TM, tk=256, name=f"{tag}_dx")
    return dx, dg, dwgu, dwd


def _tile2(v):
    return jnp.concatenate([v, v], axis=-1).reshape(1, LANES)


def _fold2(v):
    return v[:, :HEAD_DIM] + v[:, HEAD_DIM:]


EVEN = dict(dil=1, ppk=4, q_blk=0, k_blk=4, v_blk=5, n_heads=A_Q_HEADS, group=A_GROUP, max_dist=A_WINDOW - 1)
STICK = dict(q_blk=6, k_blk=10, v_blk=14, n_pairs=4)


def _odd_cfg(dil):
    return dict(dil=dil, ppk=1, q_blk=0, k_blk=8, v_blk=16, n_heads=C_HEADS, group=1, max_dist=BLK)


def even_fwd(x, g, win, qg, kg, sinks, wout, tag):
    qkv, h = norm_matmul(x, g, win, tm=TM, tn=1152, split=False, name=f"{tag}_in")
    qg2, kg2 = _tile2(qg), _tile2(kg)
    slopes = jnp.asarray(_alibi(A_Q_HEADS), F32)
    oa, lse = banded_fwd(qkv, qg2, kg2, slopes, sinks, name=f"{tag}_swa", **EVEN)
    ob, ctot = stick_fwd(qkv, name=f"{tag}_stick", **STICK)
    o = jnp.concatenate([oa, ob], axis=1)
    xo = mm_nn(o, wout, res=x, scale=1.0, swiglu=False, tm=TM, tn=D_MODEL, tk=D_MODEL, name=f"{tag}_out")
    return xo, (x, qkv, h, oa, lse, ctot, o)


def even_bwd(d, saved, g, win, qg, kg, sinks, wout, tag):
    x, qkv, h, oa, lse, ctot, o = saved
    qg2, kg2 = _tile2(qg), _tile2(kg)
    slopes = jnp.asarray(_alibi(A_Q_HEADS), F32)
    dwout = mm_tn(o, d, scale=1.0, a_split=False, b_split=False, tm=D_MODEL, tn=D_MODEL, tk=TM, name=f"{tag}_dwout")
    do = mm_nt(d, wout, tm=TM, tn=D_MODEL, tk=D_MODEL, name=f"{tag}_do")
    dqa, dka4, dva4, dqg, dkg, dsk = banded_bwd(qkv, qg2, kg2, slopes, sinks, do, oa, lse, None, None,
                                                do_blk=0, name=f"{tag}_swa_b", **EVEN)
    dqb, dkb, dvb = stick_bwd(qkv, do, ctot, do_blk=4, name=f"{tag}_stick_b", **STICK)
    dqkv = assemble_even(dqa, dka4, dva4, dqb, dkb, dvb, tm=TM, name=f"{tag}_asm")
    dwin = mm_tn(h, dqkv, scale=1.0, a_split=False, b_split=False, tm=D_MODEL, tn=1152, tk=TM, name=f"{tag}_dwin")
    dx, dg = mm_nt_normbwd(dqkv, win, x, g, d, a_split=False, tm=TM, tk=256, name=f"{tag}_dx")
    return dx, dg, dwin, _fold2(dqg), _fold2(dkg), dsk[:, :A_Q_HEADS], dwout


def odd_fwd(x, g, win, qg, kg, wout, tag):
    qkv, h = norm_matmul(x, g, win, tm=TM, tn=1536, split=False, name=f"{tag}_in")
    qg2, kg2 = _tile2(qg), _tile2(kg)
    outs = []
    for p, (window, dil) in enumerate(C_PATTERNS):
        slopes = jnp.asarray(_alibi(C_HEADS), F32) * float(dil)
        outs.append(banded_fwd(qkv, qg2, kg2, slopes, None, name=f"{tag}_dil{p}", **_odd_cfg(dil)))
    o, w1, w2, w3 = mix_fwd(outs[0][0], outs[1][0], outs[2][0], outs[0][1], outs[1][1], outs[2][1],
                            tm=TM, name=f"{tag}_mix")
    xo = mm_nn(o, wout, res=x, scale=1.0, swiglu=False, tm=TM, tn=D_MODEL, tk=D_MODEL, name=f"{tag}_out")
    return xo, (x, qkv, h, outs, (w1, w2, w3), o)


def odd_bwd(d, saved, g, win, qg, kg, wout, tag):
    x, qkv, h, outs, ws, o = saved
    qg2, kg2 = _tile2(qg), _tile2(kg)
    dwout = mm_tn(o, d, scale=1.0, a_split=False, b_split=False, tm=D_MODEL, tn=D_MODEL, tk=TM, name=f"{tag}_dwout")
    do = mm_nt(d, wout, tm=TM, tn=D_MODEL, tk=D_MODEL, name=f"{tag}_do")
    parts, dqg, dkg = [], 0.0, 0.0
    for p, (window, dil) in enumerate(C_PATTERNS):
        slopes = jnp.asarray(_alibi(C_HEADS), F32) * float(dil)
        dq, dk, dv, dqg_p, dkg_p, _ = banded_bwd(qkv, qg2, kg2, slopes, None, do, outs[p][0], outs[p][1], ws[p], o,
                                                 do_blk=0, name=f"{tag}_dil{p}_b", **_odd_cfg(dil))
        parts.append((dq, dk, dv))
        dqg = dqg + dqg_p
        dkg = dkg + dkg_p
    dqkv = assemble_odd(parts, tm=TM, name=f"{tag}_asm")
    dwin = mm_tn(h, dqkv, scale=1.0, a_split=False, b_split=False, tm=D_MODEL, tn=1536, tk=TM, name=f"{tag}_dwin")
    dx, dg = mm_nt_normbwd(dqkv, win, x, g, d, a_split=False, tm=TM, tk=256, name=f"{tag}_dx")
    return dx, dg, dwin, _fold2(dqg), _fold2(dkg), dwout


def xa_fwd(x, mem, g, gm, wq, wkv, qg, kg, wo, tag):
    qraw, h = norm_matmul(x, g, wq, tm=TM, tn=D_MODEL, split=False, name=f"{tag}_q")
    kvraw, hm = norm_matmul(mem, gm, wkv, tm=MEM_LEN, tn=D_MODEL, split=False, name=f"{tag}_kv")
    o = xattn_fwd(qraw, kvraw, qg, kg, tm=TM, name=f"{tag}_att")
    xo = mm_nn(o, wo, res=x, scale=1.0, swiglu=False, tm=TM, tn=D_MODEL, tk=D_MODEL, name=f"{tag}_o")
    return xo, (x, qraw, h, kvraw, hm, o)


def xa_bwd(d, saved, mem, g, gm, wq, wkv, qg, kg, wo, tag):
    x, qraw, h, kvraw, hm, o = saved
    dwo = mm_tn(o, d, scale=1.0, a_split=False, b_split=False, tm=D_MODEL, tn=D_MODEL, tk=TM, name=f"{tag}_dwo")
    do = mm_nt(d, wo, tm=TM, tn=D_MODEL, tk=D_MODEL, name=f"{tag}_do")
    dq, dkv, dqg, dkg = xattn_bwd(qraw, kvraw, qg, kg, do, o, tm=TM, name=f"{tag}_att_b")
    dwq = mm_tn(h, dq, scale=1.0, a_split=False, b_split=False, tm=D_MODEL, tn=D_MODEL, tk=TM, name=f"{tag}_dwq")
    dx, dg = mm_nt_normbwd(dq, wq, x, g, d, a_split=False, tm=TM, tk=256, name=f"{tag}_dx")
    dwkv = mm_tn(hm, dkv, scale=1.0, a_split=False, b_split=False, tm=D_MODEL, tn=D_MODEL, tk=MEM_LEN,
                 name=f"{tag}_dwkv")
    _, dgm = mm_nt_normbwd(dkv, wkv, mem, gm, None, a_split=False, tm=MEM_LEN, tk=256, name=f"{tag}_dmem")
    return dx, dg, dgm, dwq, dwkv, dqg, dkg, dwo


MATS = (("ffn1_w_gu", 1), ("ffn1_w_down", 0), ("ev_w_in", 1), ("ev_w_out", 0), ("od_w_in", 1), ("od_w_out", 0),
        ("xa_w_q", 0), ("xa_w_kv", 1), ("xa_w_o", 0), ("ffn2_w_gu", 1), ("ffn2_w_down", 0))
SMALLS = ("ffn1_norm", "mix_norm", "ev_q_gain", "ev_k_gain", "ev_sinks", "od_q_gain", "od_k_gain", "xa_norm",
          "xa_mem_norm", "xa_q_gain", "xa_k_gain", "ffn2_norm")
WEIGHTS = ("ffn1_norm", "ffn1_w_gu", "ffn1_w_down", "mix_norm", "ev_w_in", "ev_q_gain", "ev_k_gain", "ev_sinks",
           "ev_w_out", "od_w_in", "od_q_gain", "od_k_gain", "od_w_out", "xa_norm", "xa_mem_norm", "xa_w_q",
           "xa_w_kv", "xa_q_gain", "xa_k_gain", "xa_w_o", "ffn2_norm", "ffn2_w_gu", "ffn2_w_down")
PACK_W = 1024
SMALL_ROWS = 16
SUM_ROWS = 32


def _mat_items(shards):
    items = []
    for name, axis in MATS:
        nl, r, cc = shards[name].shape
        for layer in range(nl):
            items.append((name, layer, axis, (r, cc)))
    return items


def pack_halves(shards, c):
    rows = []
    for name, layer, axis, (r, cc) in _mat_items(shards):
        half = lax.dynamic_index_in_dim(shards[name][layer].reshape(2, r // 2, cc), c, 0, keepdims=False)
        rows.append(half.astype(BF16).reshape(-1, PACK_W))
    return jnp.concatenate(rows, axis=0)


def unpack_full(gathered, shards):
    full, off = {}, 0
    for name, layer, axis, (r, cc) in _mat_items(shards):
        nr = (r // 2) * cc // PACK_W
        piece = gathered[:, off:off + nr].reshape(4, r, cc)
        off += nr
        full[(name, layer)] = piece.reshape(4 * r, cc) if axis == 0 else piece.transpose(1, 0, 2).reshape(r, 4 * cc)
    return full


def pack_grads(grads, shards):
    rows = []
    for name, layer, axis, (r, cc) in _mat_items(shards):
        gfull = grads[(name, layer)]
        if axis == 0:
            piece = gfull.reshape(4, 2, r // 2, cc)
        else:
            piece = gfull.reshape(2, r // 2, 4, cc).transpose(2, 0, 1, 3)
        rows.append(piece.astype(BF16).reshape(N_DEV, -1, PACK_W))
    return jnp.concatenate(rows, axis=1)


def unpack_shard_grads(summed, shards):
    per, off = {}, 0
    for name, layer, axis, (r, cc) in _mat_items(shards):
        nr = (r // 2) * cc // PACK_W
        per.setdefault(name, []).append(summed[:, off:off + nr].reshape(r, cc))
        off += nr
    return {name: jnp.stack(lst, axis=0) for name, lst in per.items()}


def pack_small(vals):
    row10 = jnp.concatenate([vals["xa_q_gain"].reshape(1, 512), vals["xa_k_gain"].reshape(1, 512)], axis=1)
    row11 = jnp.concatenate([vals["ev_q_gain"], vals["ev_k_gain"], vals["od_q_gain"], vals["od_k_gain"],
                             vals["ev_sinks"], jnp.zeros((1, 1024 - 4 * 64 - 8), F32)], axis=1)
    return jnp.concatenate([vals["ffn1_norm"], vals["mix_norm"], vals["xa_norm"], vals["xa_mem_norm"],
                            vals["ffn2_norm"], row10, row11, jnp.zeros((SMALL_ROWS - 12, 1024), F32)], axis=0)


def unpack_small(arr):
    return {"ffn1_norm": arr[0:2], "mix_norm": arr[2:4], "xa_norm": arr[4:6], "xa_mem_norm": arr[6:8],
            "ffn2_norm": arr[8:10],
            "xa_q_gain": arr[10:11, 0:512].reshape(2, 256), "xa_k_gain": arr[10:11, 512:1024].reshape(2, 256),
            "ev_q_gain": arr[11:12, 0:64], "ev_k_gain": arr[11:12, 64:128], "od_q_gain": arr[11:12, 128:192],
            "od_k_gain": arr[11:12, 192:256], "ev_sinks": arr[11:12, 256:264]}


def local_step(x, mem, target, W, small):
    depth = small["ffn1_norm"].shape[0]

    def row(name, l):
        return small[name][l:l + 1]

    saved = []
    for l in range(depth):
        j = l // 2
        x, s1 = ffn_fwd(x, row("ffn1_norm", l), W[("ffn1_w_gu", l)], W[("ffn1_w_down", l)], f"l{l}_f1")
        if l % 2 == 0:
            x, s2 = even_fwd(x, row("mix_norm", l), W[("ev_w_in", j)], row("ev_q_gain", j), row("ev_k_gain", j),
                             small["ev_sinks"][j], W[("ev_w_out", j)], f"l{l}_ev")
        else:
            x, s2 = odd_fwd(x, row("mix_norm", l), W[("od_w_in", j)], row("od_q_gain", j), row("od_k_gain", j),
                            W[("od_w_out", j)], f"l{l}_od")
        x, s3 = xa_fwd(x, mem, row("xa_norm", l), row("xa_mem_norm", l), W[("xa_w_q", l)], W[("xa_w_kv", l)],
                       row("xa_q_gain", l), row("xa_k_gain", l), W[("xa_w_o", l)], f"l{l}_xa")
        x, s4 = ffn_fwd(x, row("ffn2_norm", l), W[("ffn2_w_gu", l)], W[("ffn2_w_down", l)], f"l{l}_f2")
        saved.append((s1, s2, s3, s4))
    loss, d = loss_kernel(x, target, tm=TM, name="loss")

    gw = {}
    gs = {name: [None] * small[name].shape[0] for name in SMALLS}
    for l in reversed(range(depth)):
        j = l // 2
        s1, s2, s3, s4 = saved[l]
        d, dg, dwgu, dwd = ffn_bwd(d, s4, row("ffn2_norm", l), W[("ffn2_w_gu", l)], W[("ffn2_w_down", l)], f"l{l}_f2")
        gs["ffn2_norm"][l] = dg
        gw[("ffn2_w_gu", l)], gw[("ffn2_w_down", l)] = dwgu, dwd
        d, dg, dgm, dwq, dwkv, dqg, dkg, dwo = xa_bwd(
            d, s3, mem, row("xa_norm", l), row("xa_mem_norm", l), W[("xa_w_q", l)], W[("xa_w_kv", l)],
            row("xa_q_gain", l), row("xa_k_gain", l), W[("xa_w_o", l)], f"l{l}_xa")
        gs["xa_norm"][l], gs["xa_mem_norm"][l], gs["xa_q_gain"][l], gs["xa_k_gain"][l] = dg, dgm, dqg, dkg
        gw[("xa_w_q", l)], gw[("xa_w_kv", l)], gw[("xa_w_o", l)] = dwq, dwkv, dwo
        if l % 2 == 0:
            d, dg, dwin, dqg, dkg, dsk, dwout = even_bwd(
                d, s2, row("mix_norm", l), W[("ev_w_in", j)], row("ev_q_gain", j), row("ev_k_gain", j),
                small["ev_sinks"][j], W[("ev_w_out", j)], f"l{l}_ev")
            gs["ev_q_gain"][j], gs["ev_k_gain"][j], gs["ev_sinks"][j] = dqg, dkg, dsk
            gw[("ev_w_in", j)], gw[("ev_w_out", j)] = dwin, dwout
        else:
            d, dg, dwin, dqg, dkg, dwout = odd_bwd(
                d, s2, row("mix_norm", l), W[("od_w_in", j)], row("od_q_gain", j), row("od_k_gain", j),
                W[("od_w_out", j)], f"l{l}_od")
            gs["od_q_gain"][j], gs["od_k_gain"][j] = dqg, dkg
            gw[("od_w_in", j)], gw[("od_w_out", j)] = dwin, dwout
        gs["mix_norm"][l] = dg
        d, dg, dwgu, dwd = ffn_bwd(d, s1, row("ffn1_norm", l), W[("ffn1_w_gu", l)], W[("ffn1_w_down", l)], f"l{l}_f1")
        gs["ffn1_norm"][l] = dg
        gw[("ffn1_w_gu", l)], gw[("ffn1_w_down", l)] = dwgu, dwd
    gsmall = {name: jnp.concatenate(v, axis=0) for name, v in gs.items()}
    return loss, d, gw, gsmall


def kernel(x, mem, ffn1_norm, ffn1_w_gu, ffn1_w_down, mix_norm, ev_w_in, ev_q_gain, ev_k_gain, ev_sinks, ev_w_out, od_w_in, od_q_gain, od_k_gain, od_w_out, xa_norm, xa_mem_norm, xa_w_q, xa_w_kv, xa_q_gain, xa_k_gain, xa_w_o, ffn2_norm, ffn2_w_gu, ffn2_w_down, loss_target, m_ffn1_norm, m_ffn1_w_gu, m_ffn1_w_down, m_mix_norm, m_ev_w_in, m_ev_q_gain, m_ev_k_gain, m_ev_sinks, m_ev_w_out, m_od_w_in, m_od_q_gain, m_od_k_gain, m_od_w_out, m_xa_norm, m_xa_mem_norm, m_xa_w_q, m_xa_w_kv, m_xa_q_gain, m_xa_k_gain, m_xa_w_o, m_ffn2_norm, m_ffn2_w_gu, m_ffn2_w_down, v_ffn1_norm, v_ffn1_w_gu, v_ffn1_w_down, v_mix_norm, v_ev_w_in, v_ev_q_gain, v_ev_k_gain, v_ev_sinks, v_ev_w_out, v_od_w_in, v_od_q_gain, v_od_k_gain, v_od_w_out, v_xa_norm, v_xa_mem_norm, v_xa_w_q, v_xa_w_kv, v_xa_q_gain, v_xa_k_gain, v_xa_w_o, v_ffn2_norm, v_ffn2_w_gu, v_ffn2_w_down):
    given = dict(locals())
    w = {n: given[n] for n in WEIGHTS}
    m = {n: given["m_" + n] for n in WEIGHTS}
    v = {n: given["v_" + n] for n in WEIGHTS}
    c = lax.axis_index("c")
    shards = {name: w[name] for name, _ in MATS}
    small = {n: w[n] for n in SMALLS}

    gathered = all_gather8(pack_halves(shards, c), name="gather_weights")
    full = unpack_full(gathered, shards)
    loss_b, grad_x, gw, gsmall = local_step(x[0], mem[0], loss_target[0], full, small)

    land, land_small = all_to_all8(pack_grads(gw, shards), pack_small(gsmall), name="scatter_grads")
    mine = reduce_slots(land, tr=SUM_ROWS, name="sum_grads")
    both = sibling_swap(mine, name="swap_grads")
    g_small = unpack_small(reduce_slots(land_small, tr=SMALL_ROWS, name="sum_small"))
    g = dict(unpack_shard_grads(both, shards))
    g.update(g_small)

    delta, new_m, new_v = {}, {}, {}
    for name, _ in MATS:
        shp = w[name].shape
        flat = [a.reshape(-1, shp[-1]) for a in (w[name], g[name], m[name], v[name])]
        dl, nm, nv = adamw(*flat, br=BLK, name=f"adamw_{name}")
        delta[name], new_m[name], new_v[name] = dl.reshape(shp), nm.reshape(shp), nv.reshape(shp)
    dl, nm, nv = adamw(pack_small(small), pack_small(g_small), pack_small({n: m[n] for n in SMALLS}),
                       pack_small({n: v[n] for n in SMALLS}), br=SMALL_ROWS, name="adamw_small")
    for dst, arr in ((delta, dl), (new_m, nm), (new_v, nv)):
        dst.update(unpack_small(arr))

    loss = lax.psum(loss_b[0, 0], ("x", "y", "c"))
    return (loss, grad_x[None], *[g[n] for n in WEIGHTS], *[delta[n] for n in WEIGHTS],
            *[new_m[n] for n in WEIGHTS], *[new_v[n] for n in WEIGHTS])
```

```python
import functools

import jax
import jax.numpy as jnp
import numpy as np
from jax import lax
from jax.experimental import pallas as pl
from jax.experimental.pallas import tpu as pltpu

F32 = jnp.float32
BF16 = jnp.bfloat16

D_MODEL = 1024
HEAD_DIM = 64
LANES = 128
BLK = 128
D_FF = 2816
RMS_EPS = 1e-6
MEM_LEN = 256
X_HEADS = 4
X_HEAD_DIM = 256
A_Q_HEADS = 8
A_GROUP = 4
A_WINDOW = 128
C_HEADS = 16
C_PATTERNS = ((128, 1), (512, 4), (2048, 16))
NEG = -1e30
VMEM_LIMIT = 56 * 2 ** 20

ADAM_LR = 0.001
ADAM_B1 = 0.9
ADAM_B2 = 0.999
ADAM_EPS = 1e-08
ADAM_WD = 0.01
ADAM_STEP = 10

N_DEV = 8
MESH = pl.DeviceIdType.MESH


def _cparams(n):
    return pltpu.CompilerParams(dimension_semantics=("arbitrary",) * n, vmem_limit_bytes=VMEM_LIMIT)


def _dot(a, b):
    return jnp.dot(a, b, preferred_element_type=F32)


def _dot_nt(a, b):
    return lax.dot_general(a, b, (((1,), (1,)), ((), ())), preferred_element_type=F32)


def _dot_tn(a, b):
    return lax.dot_general(a, b, (((0,), (0,)), ((), ())), preferred_element_type=F32)


def _sigmoid(z):
    return 1.0 / (1.0 + jnp.exp(-z))


def norm_matmul(x, g, w, *, tm, tn, split, name):
    T, K = x.shape
    N = w.shape[1]
    nj = N // tn

    def body(x_ref, g_ref, w_ref, o_ref, h_ref):
        @pl.when(pl.program_id(1) == 0)
        def _():
            xv = x_ref[...]
            r = lax.rsqrt(jnp.mean(xv * xv, axis=-1, keepdims=True) + RMS_EPS)
            h_ref[...] = (xv * r * g_ref[...]).astype(BF16)

        o_ref[...] = _dot(h_ref[...], w_ref[...])

    if split:
        njh = nj // 2
        o_shape = jax.ShapeDtypeStruct((2, T, N // 2), F32)
        o_spec = pl.BlockSpec((None, tm, tn), lambda i, j: (j // njh, i, j % njh))
    else:
        o_shape = jax.ShapeDtypeStruct((T, N), F32)
        o_spec = pl.BlockSpec((tm, tn), lambda i, j: (i, j))
    return pl.pallas_call(
        body, grid=(T // tm, nj),
        in_specs=[pl.BlockSpec((tm, K), lambda i, j: (i, 0)),
                  pl.BlockSpec((1, K), lambda i, j: (0, 0)),
                  pl.BlockSpec((K, tn), lambda i, j: (0, j))],
        out_specs=[o_spec, pl.BlockSpec((tm, K), lambda i, j: (i, 0))],
        out_shape=[o_shape, jax.ShapeDtypeStruct((T, K), BF16)],
        compiler_params=_cparams(2), name=name)(x, g, w)


def mm_nn(a, b, *, res, scale, swiglu, tm, tn, tk, name):
    T = a.shape[-2]
    K, N = b.shape
    nk = K // tk

    def body(*refs):
        if swiglu:
            g_ref, u_ref, b_ref, r_ref, o_ref, acc = refs
        else:
            a_ref, b_ref, r_ref, o_ref, acc = refs
        k = pl.program_id(2)

        @pl.when(k == 0)
        def _():
            acc[...] = jnp.zeros_like(acc)

        if swiglu:
            gv = g_ref[...]
            av = (gv * _sigmoid(gv) * u_ref[...]).astype(BF16)
        else:
            av = a_ref[...].astype(BF16)
        acc[...] += _dot(av, b_ref[...])

        @pl.when(k == nk - 1)
        def _():
            o_ref[...] = r_ref[...] + scale * acc[...]

    if swiglu:
        a_specs = [pl.BlockSpec((None, tm, tk), lambda i, j, k: (0, i, k)),
                   pl.BlockSpec((None, tm, tk), lambda i, j, k: (1, i, k))]
        a_args = [a, a]
    else:
        a_specs = [pl.BlockSpec((tm, tk), lambda i, j, k: (i, k))]
        a_args = [a]
    return pl.pallas_call(
        body, grid=(T // tm, N // tn, nk),
        in_specs=a_specs + [pl.BlockSpec((tk, tn), lambda i, j, k: (k, j)),
                            pl.BlockSpec((tm, tn), lambda i, j, k: (i, j))],
        out_specs=pl.BlockSpec((tm, tn), lambda i, j, k: (i, j)),
        out_shape=jax.ShapeDtypeStruct((T, N), F32),
        scratch_shapes=[pltpu.VMEM((tm, tn), F32)],
        compiler_params=_cparams(3), name=name)(*a_args, b, res)


def mm_nt(a, b, *, tm, tn, tk, name):
    T, K = a.shape
    N = b.shape[0]
    nk = K // tk

    def body(a_ref, b_ref, o_ref, acc):
        k = pl.program_id(2)

        @pl.when(k == 0)
        def _():
            acc[...] = jnp.zeros_like(acc)

        acc[...] += _dot_nt(a_ref[...].astype(BF16), b_ref[...])

        @pl.when(k == nk - 1)
        def _():
            o_ref[...] = acc[...]

    return pl.pallas_call(
        body, grid=(T // tm, N // tn, nk),
        in_specs=[pl.BlockSpec((tm, tk), lambda i, j, k: (i, k)),
                  pl.BlockSpec((tn, tk), lambda i, j, k: (j, k))],
        out_specs=pl.BlockSpec((tm, tn), lambda i, j, k: (i, j)),
        out_shape=jax.ShapeDtypeStruct((T, N), F32),
        scratch_shapes=[pltpu.VMEM((tm, tn), F32)],
        compiler_params=_cparams(3), name=name)(a, b)


def ffn_bwd_act(d, wd, gu, *, tm, tn, name):
    T, K = d.shape
    Fd = wd.shape[0]

    def body(d_ref, w_ref, g_ref, u_ref, dgu_ref, act_ref):
        da = 0.5 * _dot_nt(d_ref[...].astype(BF16), w_ref[...])
        gv = g_ref[...]
        uv = u_ref[...]
        s = _sigmoid(gv)
        silu = gv * s
        act_ref[...] = (silu * uv).astype(BF16)
        dgu_ref[0] = (da * uv * (s * (1.0 + gv * (1.0 - s)))).astype(BF16)
        dgu_ref[1] = (da * silu).astype(BF16)

    return pl.pallas_call(
        body, grid=(T // tm, Fd // tn),
        in_specs=[pl.BlockSpec((tm, K), lambda i, j: (i, 0)),
                  pl.BlockSpec((tn, K), lambda i, j: (j, 0)),
                  pl.BlockSpec((None, tm, tn), lambda i, j: (0, i, j)),
                  pl.BlockSpec((None, tm, tn), lambda i, j: (1, i, j))],
        out_specs=[pl.BlockSpec((2, tm, tn), lambda i, j: (0, i, j)),
                   pl.BlockSpec((tm, tn), lambda i, j: (i, j))],
        out_shape=[jax.ShapeDtypeStruct((2, T, Fd), BF16), jax.ShapeDtypeStruct((T, Fd), BF16)],
        compiler_params=_cparams(2), name=name)(d, wd, gu, gu)


def mm_nt_normbwd(a, b, x, g, res, *, a_split, tm, tk, name):
    T, Dm = x.shape
    K = b.shape[1]
    nk = K // tk
    nkh = nk // 2
    has_res = res is not None

    def body(*refs):
        if has_res:
            a_ref, b_ref, x_ref, g_ref, r_ref, dx_ref, dg_ref, acc = refs
        else:
            a_ref, b_ref, x_ref, g_ref, dx_ref, dg_ref, acc = refs
        i = pl.program_id(0)
        k = pl.program_id(1)

        @pl.when(k == 0)
        def _():
            acc[...] = jnp.zeros_like(acc)

        acc[...] += _dot_nt(a_ref[...].astype(BF16), b_ref[...])

        @pl.when(k == nk - 1)
        def _():
            xv = x_ref[...]
            r = lax.rsqrt(jnp.mean(xv * xv, axis=-1, keepdims=True) + RMS_EPS)
            xh = xv * r
            dh = acc[...]
            dxh = dh * g_ref[...]
            dx = r * (dxh - xh * jnp.mean(dxh * xh, axis=-1, keepdims=True))
            if has_res:
                dx = dx + r_ref[...]
            dx_ref[...] = dx
            part = jnp.sum(dh * xh, axis=0, keepdims=True)

            @pl.when(i == 0)
            def _():
                dg_ref[...] = part

            @pl.when(i > 0)
            def _():
                dg_ref[...] += part

    if a_split:
        a_spec = pl.BlockSpec((None, tm, tk), lambda i, k: (k // nkh, i, k % nkh))
    else:
        a_spec = pl.BlockSpec((tm, tk), lambda i, k: (i, k))
    in_specs = [a_spec,
                pl.BlockSpec((Dm, tk), lambda i, k: (0, k)),
                pl.BlockSpec((tm, Dm), lambda i, k: (i, 0)),
                pl.BlockSpec((1, Dm), lambda i, k: (0, 0))]
    args = [a, b, x, g]
    if has_res:
        in_specs.append(pl.BlockSpec((tm, Dm), lambda i, k: (i, 0)))
        args.append(res)
    return pl.pallas_call(
        body, grid=(T // tm, nk), in_specs=in_specs,
        out_specs=[pl.BlockSpec((tm, Dm), lambda i, k: (i, 0)),
                   pl.BlockSpec((1, Dm), lambda i, k: (0, 0))],
        out_shape=[jax.ShapeDtypeStruct((T, Dm), F32), jax.ShapeDtypeStruct((1, Dm), F32)],
        scratch_shapes=[pltpu.VMEM((tm, Dm), F32)],
        compiler_params=_cparams(2), name=name)(*args)


def mm_tn(a, b, *, scale, a_split, b_split, tm, tn, tk, name):
    T = a.shape[-2]
    M = a.shape[-1] * (2 if a_split else 1)
    N = b.shape[-1] * (2 if b_split else 1)
    ni, nj, nk = M // tm, N // tn, T // tk
    nih, njh = ni // 2, nj // 2

    def body(a_ref, b_ref, o_ref, acc):
        k = pl.program_id(2)

        @pl.when(k == 0)
        def _():
            acc[...] = jnp.zeros_like(acc)

        acc[...] += _dot_tn(a_ref[...].astype(BF16), b_ref[...].astype(BF16))

        @pl.when(k == nk - 1)
        def _():
            o_ref[...] = acc[...] * scale

    if a_split:
        a_spec = pl.BlockSpec((None, tk, tm), lambda i, j, k: (i // nih, k, i % nih))
    else:
        a_spec = pl.BlockSpec((tk, tm), lambda i, j, k: (k, i))
    if b_split:
        b_spec = pl.BlockSpec((None, tk, tn), lambda i, j, k: (j // njh, k, j % njh))
    else:
        b_spec = pl.BlockSpec((tk, tn), lambda i, j, k: (k, j))
    return pl.pallas_call(
        body, grid=(ni, nj, nk), in_specs=[a_spec, b_spec],
        out_specs=pl.BlockSpec((tm, tn), lambda i, j, k: (i, j)),
        out_shape=jax.ShapeDtypeStruct((M, N), F32),
        scratch_shapes=[pltpu.VMEM((tm, tn), F32)],
        compiler_params=_cparams(3), name=name)(a, b)


def loss_kernel(y, target, *, tm, name):
    T, Dm = y.shape

    def body(y_ref, t_ref, l_ref, dy_ref):
        e = y_ref[...] - t_ref[...]
        dy_ref[...] = e * (1.0 / Dm)
        part = (0.5 / Dm) * jnp.sum(jnp.sum(e * e, axis=-1, keepdims=True), axis=0, keepdims=True)
        part = jnp.broadcast_to(part, (8, LANES))

        @pl.when(pl.program_id(0) == 0)
        def _():
            l_ref[...] = part

        @pl.when(pl.program_id(0) > 0)
        def _():
            l_ref[...] += part

    return pl.pallas_call(
        body, grid=(T // tm,),
        in_specs=[pl.BlockSpec((tm, Dm), lambda i: (i, 0)), pl.BlockSpec((tm, Dm), lambda i: (i, 0))],
        out_specs=[pl.BlockSpec((8, LANES), lambda i: (0, 0)), pl.BlockSpec((tm, Dm), lambda i: (i, 0))],
        out_shape=[jax.ShapeDtypeStruct((8, LANES), F32), jax.ShapeDtypeStruct((T, Dm), F32)],
        compiler_params=_cparams(1), name=name)(y, target)


def adamw(w, g, m, v, *, br, name):
    R, C = w.shape

    def body(w_ref, g_ref, m_ref, v_ref, d_ref, nm_ref, nv_ref):
        gv = g_ref[...]
        nm = ADAM_B1 * m_ref[...] + (1.0 - ADAM_B1) * gv
        nv = ADAM_B2 * v_ref[...] + (1.0 - ADAM_B2) * (gv * gv)
        m_hat = nm / (1.0 - ADAM_B1 ** ADAM_STEP)
        v_hat = nv / (1.0 - ADAM_B2 ** ADAM_STEP)
        d_ref[...] = -ADAM_LR * (m_hat / (jnp.sqrt(v_hat) + ADAM_EPS) + ADAM_WD * w_ref[...])
        nm_ref[...] = nm
        nv_ref[...] = nv

    spec = pl.BlockSpec((br, C), lambda i: (i, 0))
    shp = jax.ShapeDtypeStruct((R, C), F32)
    return pl.pallas_call(
        body, grid=(R // br,), in_specs=[spec] * 4, out_specs=[spec] * 3, out_shape=[shp] * 3,
        compiler_params=_cparams(1), name=name)(w, g, m, v)


def _lane0():
    return lax.broadcasted_iota(jnp.int32, (1, LANES), 1) < HEAD_DIM


def _half_sum(x, m0):
    s0 = jnp.sum(jnp.where(m0, x, 0.0), axis=-1, keepdims=True)
    s1 = jnp.sum(jnp.where(m0, 0.0, x), axis=-1, keepdims=True)
    return jnp.where(m0, s0, s1)


def _half_pick(x, m0, e):
    sel = m0 if e == 0 else jnp.logical_not(m0)
    return jnp.max(jnp.where(sel, x, NEG), axis=-1, keepdims=True)


def _head_rms(x, m0):
    return lax.rsqrt(_half_sum(x * x, m0) * (1.0 / HEAD_DIM) + RMS_EPS)


def _alibi(n):
    return [float(2.0 ** (-8.0 * (h + 1) / n)) for h in range(n)]


def _mask_half(x, m0, e):
    return jnp.where(m0, x, 0.0) if e == 0 else jnp.where(m0, 0.0, x)


def _band_masks(max_dist, has_prev, live):
    row = lax.broadcasted_iota(jnp.int32, (BLK, BLK), 0)
    col = lax.broadcasted_iota(jnp.int32, (BLK, BLK), 1)
    dist_c = row - col
    dist_p = dist_c + BLK
    lim_c = jnp.where(live, max_dist, -1)
    lim_p = jnp.where(live & has_prev, max_dist, -1)
    valid_c = (dist_c >= 0) & (dist_c <= lim_c)
    valid_p = dist_p <= lim_p
    return dist_c.astype(F32), dist_p.astype(F32), valid_c, valid_p


def _rows(r, dil):
    return pl.ds(r, BLK, stride=dil) if dil > 1 else pl.ds(0, BLK)


def _band_specs(dil, ppk, q_blk, k_blk, v_blk, kv_shared, nb):
    RB = BLK * dil
    qw = LANES * ppk
    kw = LANES if kv_shared else qw

    def cur(i):
        return jnp.minimum(i, nb - 1)

    def kidx(base):
        return (lambda p, i: (cur(i), base)) if kv_shared else (lambda p, i: (cur(i), base + p))

    def pidx(base):
        return ((lambda p, i: (jnp.maximum(i - 1, 0), base)) if kv_shared
                else (lambda p, i: (jnp.maximum(i - 1, 0), base + p)))

    return [pl.BlockSpec((RB, qw), lambda p, i: (cur(i), q_blk + p)),
            pl.BlockSpec((RB, kw), kidx(k_blk)), pl.BlockSpec((RB, kw), pidx(k_blk)),
            pl.BlockSpec((RB, kw), kidx(v_blk)), pl.BlockSpec((RB, kw), pidx(v_blk))]


def banded_fwd(qkv, q_gain2, k_gain2, slopes, sinks, *, dil, ppk, q_blk, k_blk, v_blk, n_heads, group,
               max_dist, name):
    T = qkv.shape[0]
    RB = BLK * dil
    nb = T // RB
    npair = n_heads // 2
    kv_shared = group > 1
    scale = HEAD_DIM ** -0.5
    has_sink = sinks is not None

    def body(*refs):
        slope_ref = refs[0]
        if has_sink:
            sink_ref, refs = refs[1], refs[2:]
        else:
            refs = refs[1:]
        q_ref, kc_ref, kp_ref, vc_ref, vp_ref, qg_ref, kg_ref, o_ref, l_ref = refs
        pb = pl.program_id(0)
        i = pl.program_id(1)
        m0 = _lane0()
        dcf, dpf, valid_c, valid_p = _band_masks(max_dist, i > 0, i >= 0)
        qg = qg_ref[...]
        kg = kg_ref[...]
        for r in range(dil):
            rows = _rows(r, dil)
            kcache = {}
            for jp in range(ppk):
                cs = pl.ds(LANES * jp, LANES)
                jk = 0 if kv_shared else jp
                if jk not in kcache:
                    ks = pl.ds(LANES * jk, LANES)
                    kc = kc_ref[rows, ks]
                    kp = kp_ref[rows, ks]
                    kcache[jk] = ((kc * _head_rms(kc, m0) * kg).astype(BF16),
                                  (kp * _head_rms(kp, m0) * kg).astype(BF16),
                                  vc_ref[rows, ks], vp_ref[rows, ks])
                knc, knp, vc, vp = kcache[jk]
                qv = q_ref[rows, cs]
                qn = qv * _head_rms(qv, m0) * qg
                o_pair = jnp.zeros((BLK, LANES), F32)
                lse_pair = jnp.zeros((BLK, LANES), F32)
                for e in range(2):
                    ke = ((2 * jp + e) // group) % 2 if kv_shared else e
                    hidx = 2 * (pb * ppk + jp) + e
                    slope = slope_ref[hidx]
                    qm = _mask_half(qn, m0, e)
                    if ke != e:
                        qm = pltpu.roll(qm, HEAD_DIM, 1)
                    qm = qm.astype(BF16)
                    s_c = jnp.where(valid_c, _dot_nt(qm, knc) * scale - slope * dcf, NEG)
                    s_p = jnp.where(valid_p, _dot_nt(qm, knp) * scale - slope * dpf, NEG)
                    m = jnp.maximum(jnp.max(s_c, axis=-1, keepdims=True), jnp.max(s_p, axis=-1, keepdims=True))
                    if has_sink:
                        sk = sink_ref[hidx]
                        m = jnp.maximum(m, sk)
                    p_c = jnp.exp(s_c - m)
                    p_p = jnp.exp(s_p - m)
                    den = jnp.sum(p_c, axis=-1, keepdims=True) + jnp.sum(p_p, axis=-1, keepdims=True)
                    if has_sink:
                        den = den + jnp.exp(sk - m)
                    inv = 1.0 / den
                    o_h = (_dot((p_c * inv).astype(BF16), _mask_half(vc, m0, ke).astype(BF16))
                           + _dot((p_p * inv).astype(BF16), _mask_half(vp, m0, ke).astype(BF16)))
                    if ke != e:
                        o_h = pltpu.roll(o_h, HEAD_DIM, 1)
                    o_pair = o_pair + o_h
                    sel = m0 if e == 0 else jnp.logical_not(m0)
                    lse_pair = jnp.where(sel, m + jnp.log(den), lse_pair)
                o_ref[rows, cs] = o_pair
                l_ref[rows, cs] = lse_pair

    smem = pl.BlockSpec(memory_space=pltpu.SMEM)
    qw = LANES * ppk
    gspec = pl.BlockSpec((1, LANES), lambda p, i: (0, 0))
    ospec = pl.BlockSpec((RB, qw), lambda p, i: (i, p))
    oshape = jax.ShapeDtypeStruct((T, n_heads * HEAD_DIM), F32)
    args = [slopes] + ([sinks] if has_sink else []) + [qkv] * 5 + [q_gain2, k_gain2]
    return pl.pallas_call(
        body, grid=(npair // ppk, nb),
        in_specs=[smem] * (2 if has_sink else 1) + _band_specs(dil, ppk, q_blk, k_blk, v_blk, kv_shared, nb)
        + [gspec, gspec],
        out_specs=[ospec, ospec], out_shape=[oshape, oshape],
        compiler_params=_cparams(2), name=name)(*args)


def banded_bwd(qkv, q_gain2, k_gain2, slopes, sinks, do, o, lse, w, omix, *, dil, ppk, q_blk, k_blk, v_blk,
               n_heads, group, max_dist, do_blk, name):
    T = qkv.shape[0]
    RB = BLK * dil
    nb = T // RB
    npair = n_heads // 2
    kv_shared = group > 1
    scale = HEAD_DIM ** -0.5
    has_sink = sinks is not None
    mixed = w is not None
    qw = LANES * ppk

    def body(*refs):
        slope_ref = refs[0]
        if has_sink:
            sink_ref, refs = refs[1], refs[2:]
        else:
            refs = refs[1:]
        q_ref, kc_ref, kp_ref, vc_ref, vp_ref, qg_ref, kg_ref, do_ref, o_ref, l_ref = refs[:10]
        refs = refs[10:]
        if mixed:
            w_ref, om_ref, refs = refs[0], refs[1], refs[2:]
        dq_ref, dk_ref, dv_ref, dqg_ref, dkg_ref, dsk_ref, ck_ref, cv_ref = refs
        pb = pl.program_id(0)
        i = pl.program_id(1)
        live = i < nb
        m0 = _lane0()
        lane = lax.broadcasted_iota(jnp.int32, (1, LANES), 1)
        dcf, dpf, valid_c, valid_p = _band_masks(max_dist, i > 0, live)
        livef = live.astype(F32)
        qg = qg_ref[...]
        kg = kg_ref[...]

        @pl.when((pb == 0) & (i == 0))
        def _():
            dqg_ref[...] = jnp.zeros_like(dqg_ref)
            dkg_ref[...] = jnp.zeros_like(dkg_ref)
            dsk_ref[...] = jnp.zeros_like(dsk_ref)

        @pl.when(i == 0)
        def _():
            ck_ref[...] = jnp.zeros_like(ck_ref)
            cv_ref[...] = jnp.zeros_like(cv_ref)

        dqg_acc = jnp.zeros((1, LANES), F32)
        dkg_acc = jnp.zeros((1, LANES), F32)
        dsk_acc = jnp.zeros((1, LANES), F32)
        for r in range(dil):
            rows = _rows(r, dil)
            for jp in range(ppk):
                cs = pl.ds(LANES * jp, LANES)
                ks = pl.ds(0, LANES) if kv_shared else cs
                kc = kc_ref[rows, ks]
                kp = kp_ref[rows, ks]
                rkc = _head_rms(kc, m0)
                rkp = _head_rms(kp, m0)
                khc = kc * rkc
                khp = kp * rkp
                knc = (khc * kg).astype(BF16)
                knp = (khp * kg).astype(BF16)
                vc = vc_ref[rows, ks].astype(BF16)
                vp = vp_ref[rows, ks].astype(BF16)
                qv = q_ref[rows, cs]
                rq = _head_rms(qv, m0)
                qh = qv * rq
                qn = qh * qg
                dov = do_ref[rows, cs]
                ov = o_ref[rows, cs]
                lv = l_ref[rows, cs]
                if mixed:
                    wv = w_ref[rows, cs]
                    dmix = _half_sum(dov * om_ref[rows, cs], m0)
                    dov = dov * wv
                delta2 = _half_sum(dov * ov, m0)
                corr2 = (delta2 - wv * dmix) if mixed else None
                dqn = jnp.zeros((BLK, LANES), F32)
                dknc = jnp.zeros((BLK, LANES), F32)
                dknp = jnp.zeros((BLK, LANES), F32)
                dvc = jnp.zeros((BLK, LANES), F32)
                dvp = jnp.zeros((BLK, LANES), F32)
                for e in range(2):
                    ke = ((2 * jp + e) // group) % 2 if kv_shared else e
                    hidx = 2 * (pb * ppk + jp) + e
                    slope = slope_ref[hidx]
                    qm = _mask_half(qn, m0, e)
                    dom = _mask_half(dov, m0, e)
                    if ke != e:
                        qm = pltpu.roll(qm, HEAD_DIM, 1)
                        dom = pltpu.roll(dom, HEAD_DIM, 1)
                    qm = qm.astype(BF16)
                    dom = dom.astype(BF16)
                    lse = _half_pick(lv, m0, e)
                    delta = _half_pick(delta2, m0, e)
                    shift = delta - _half_pick(corr2, m0, e) if mixed else delta
                    p_c = jnp.where(valid_c, jnp.exp(_dot_nt(qm, knc) * scale - slope * dcf - lse), 0.0)
                    p_p = jnp.where(valid_p, jnp.exp(_dot_nt(qm, knp) * scale - slope * dpf - lse), 0.0)
                    ds_c = (p_c * (_dot_nt(dom, vc) - shift)).astype(BF16)
                    ds_p = (p_p * (_dot_nt(dom, vp) - shift)).astype(BF16)
                    dq_h = (_dot(ds_c, _mask_half(knc, m0, ke)) + _dot(ds_p, _mask_half(knp, m0, ke))) * scale
                    if ke != e:
                        dq_h = pltpu.roll(dq_h, HEAD_DIM, 1)
                    dqn = dqn + dq_h
                    dknc = dknc + _dot_tn(ds_c, qm) * scale
                    dknp = dknp + _dot_tn(ds_p, qm) * scale
                    dvc = dvc + _dot_tn(p_c.astype(BF16), dom)
                    dvp = dvp + _dot_tn(p_p.astype(BF16), dom)
                    if has_sink:
                        psk = jnp.exp(sink_ref[hidx] - lse) * livef
                        tot = jnp.sum(-psk * delta, axis=0, keepdims=True)
                        dsk_acc = dsk_acc + jnp.where(lane == (2 * jp + e), tot, 0.0)
                dqg_acc = dqg_acc + jnp.sum(dqn * qh, axis=0, keepdims=True)
                dqh = dqn * qg
                dq_raw = rq * (dqh - qh * (_half_sum(dqh * qh, m0) * (1.0 / HEAD_DIM)))
                dkg_acc = dkg_acc + jnp.sum(dknc * khc + dknp * khp, axis=0, keepdims=True)
                dkhc = dknc * kg
                dkhp = dknp * kg
                dkc_raw = rkc * (dkhc - khc * (_half_sum(dkhc * khc, m0) * (1.0 / HEAD_DIM)))
                dkp_raw = rkp * (dkhp - khp * (_half_sum(dkhp * khp, m0) * (1.0 / HEAD_DIM)))

                @pl.when(live)
                def _():
                    dq_ref[rows, cs] = dq_raw

                dk_ref[rows, cs] = ck_ref[rows, cs] + dkp_raw
                dv_ref[rows, cs] = cv_ref[rows, cs] + dvp
                ck_ref[rows, cs] = dkc_raw
                cv_ref[rows, cs] = dvc
        dqg_ref[...] += dqg_acc
        dkg_ref[...] += dkg_acc
        dsk_ref[...] += dsk_acc

    smem = pl.BlockSpec(memory_space=pltpu.SMEM)
    gspec = pl.BlockSpec((1, LANES), lambda p, i: (0, 0))

    def cur(i):
        return jnp.minimum(i, nb - 1)

    qspec = pl.BlockSpec((RB, qw), lambda p, i: (cur(i), p))
    dospec = pl.BlockSpec((RB, qw), lambda p, i: (cur(i), do_blk + p))
    kvout = pl.BlockSpec((RB, qw), lambda p, i: (jnp.maximum(i - 1, 0), p))
    in_specs = ([smem] * (2 if has_sink else 1) + _band_specs(dil, ppk, q_blk, k_blk, v_blk, kv_shared, nb)
                + [gspec, gspec, dospec, qspec, qspec] + ([qspec, qspec] if mixed else []))
    args = ([slopes] + ([sinks] if has_sink else []) + [qkv] * 5 + [q_gain2, k_gain2, do, o, lse]
            + ([w, omix] if mixed else []))
    full = jax.ShapeDtypeStruct((T, n_heads * HEAD_DIM), F32)
    row = jax.ShapeDtypeStruct((1, LANES), F32)
    return pl.pallas_call(
        body, grid=(npair // ppk, nb + 1), in_specs=in_specs,
        out_specs=[qspec, kvout, kvout, gspec, gspec, gspec],
        out_shape=[full, full, full, row, row, row],
        scratch_shapes=[pltpu.VMEM((RB, qw), F32), pltpu.VMEM((RB, qw), F32)],
        compiler_params=_cparams(2), name=name)(*args)


def mix_fwd(o1, o2, o3, l1, l2, l3, *, tm, name):
    T, C = o1.shape

    def body(o1r, o2r, o3r, l1r, l2r, l3r, o_ref, w1r, w2r, w3r):
        a, b, c = l1r[...], l2r[...], l3r[...]
        m = jnp.maximum(jnp.maximum(a, b), c)
        ea, eb, ec = jnp.exp(a - m), jnp.exp(b - m), jnp.exp(c - m)
        inv = 1.0 / (ea + eb + ec)
        wa, wb, wc = ea * inv, eb * inv, ec * inv
        o_ref[...] = wa * o1r[...] + wb * o2r[...] + wc * o3r[...]
        w1r[...] = wa
        w2r[...] = wb
        w3r[...] = wc

    spec = pl.BlockSpec((tm, C), lambda i: (i, 0))
    shp = jax.ShapeDtypeStruct((T, C), F32)
    return pl.pallas_call(body, grid=(T // tm,), in_specs=[spec] * 6, out_specs=[spec] * 4, out_shape=[shp] * 4,
                          compiler_params=_cparams(1), name=name)(o1, o2, o3, l1, l2, l3)


def assemble_odd(parts, *, tm, name):
    T, C = parts[0][0].shape

    def body(*refs):
        o_ref = refs[9]
        for j in range(3):
            o_ref[:, pl.ds(C * j, C)] = refs[j][...] + refs[3 + j][...] + refs[6 + j][...]

    spec = pl.BlockSpec((tm, C), lambda i: (i, 0))
    flat = [parts[p][j] for p in range(3) for j in range(3)]
    return pl.pallas_call(body, grid=(T // tm,), in_specs=[spec] * 9,
                          out_specs=pl.BlockSpec((tm, 3 * C), lambda i: (i, 0)),
                          out_shape=jax.ShapeDtypeStruct((T, 3 * C), F32),
                          compiler_params=_cparams(1), name=name)(*flat)


def assemble_even(dqa, dka4, dva4, dqb, dkb, dvb, *, tm, name):
    T = dqa.shape[0]
    W = 512

    def body(dqa_r, dka_r, dva_r, dqb_r, dkb_r, dvb_r, o_ref):
        o_ref[:, pl.ds(0, W)] = dqa_r[...]
        ka = dka_r[...]
        va = dva_r[...]
        o_ref[:, pl.ds(512, LANES)] = ka[:, 0:128] + ka[:, 128:256] + ka[:, 256:384] + ka[:, 384:512]
        o_ref[:, pl.ds(640, LANES)] = va[:, 0:128] + va[:, 128:256] + va[:, 256:384] + va[:, 384:512]
        o_ref[:, pl.ds(768, W)] = dqb_r[...]
        o_ref[:, pl.ds(1280, W)] = dkb_r[...]
        o_ref[:, pl.ds(1792, W)] = dvb_r[...]

    spec = pl.BlockSpec((tm, W), lambda i: (i, 0))
    return pl.pallas_call(body, grid=(T // tm,), in_specs=[spec] * 6,
                          out_specs=pl.BlockSpec((tm, 2304), lambda i: (i, 0)),
                          out_shape=jax.ShapeDtypeStruct((T, 2304), F32),
                          compiler_params=_cparams(1), name=name)(dqa, dka4, dva4, dqb, dkb, dvb)


STICK_T = 256
STICK_DEAD = -110.0


def _split_bf16(x):
    hi = x.astype(BF16)
    lo = (x - hi.astype(F32)).astype(BF16)
    return hi, lo


def _stick_logits(qm, kt, scale, diag):
    n = STICK_T
    row = lax.broadcasted_iota(jnp.int32, (n, n), 0)
    col = lax.broadcasted_iota(jnp.int32, (n, n), 1)
    mask = col < row + jnp.where(diag, 0, n)
    z = _dot_nt(qm, kt) * scale
    lneg = -(jnp.maximum(z, 0.0) + jnp.log(1.0 + jnp.exp(-jnp.abs(z))))
    lpos = z + lneg
    lk = jnp.where(mask, lneg, 0.0)
    return mask, lpos, lneg, lk


def _cumsum_mm(x, tri):
    hi, lo = _split_bf16(x)
    return _dot(hi, tri) + _dot(lo, tri)


def stick_fwd(qkv, *, q_blk, k_blk, v_blk, n_pairs, name):
    T = qkv.shape[0]
    n = STICK_T
    nq = T // n
    scale = HEAD_DIM ** -0.5

    def body(q_ref, k_ref, v_ref, o_ref):
        i = pl.program_id(1)
        m0 = _lane0()
        r2 = lax.broadcasted_iota(jnp.int32, (n, n), 0)
        c2 = lax.broadcasted_iota(jnp.int32, (n, n), 1)
        tri_after = (r2 > c2).astype(BF16)
        qv = q_ref[...]
        out = jnp.zeros((n, LANES), F32)
        for e in range(2):
            qm = _mask_half(qv, m0, e).astype(BF16)

            def alive(st):
                t, _, carry = st
                return (t <= i) & (jnp.max(carry) > STICK_DEAD)

            def step(st, e=e, qm=qm):
                t, acc, carry = st
                start = pl.multiple_of((i - t) * n, n)
                kt = k_ref[pl.ds(start, n), :].astype(BF16)
                vt = _mask_half(v_ref[pl.ds(start, n), :], m0, e).astype(BF16)
                mask, lpos, _, lk = _stick_logits(qm, kt, scale, t == 0)
                after = _cumsum_mm(lk, tri_after) + carry
                a = jnp.where(mask, jnp.exp(lpos + after), 0.0)
                acc = acc + _dot(a.astype(BF16), vt)
                carry = carry + jnp.sum(lk, axis=-1, keepdims=True)
                return t + 1, acc, carry

            _, acc, _ = lax.while_loop(alive, step, (jnp.int32(0), jnp.zeros((n, LANES), F32),
                                                     jnp.zeros((n, 1), F32)))
            out = out + acc
        o_ref[...] = out

    return pl.pallas_call(
        body, grid=(n_pairs, nq),
        in_specs=[pl.BlockSpec((n, LANES), lambda p, i: (i, q_blk + p)),
                  pl.BlockSpec((T, LANES), lambda p, i: (0, k_blk + p)),
                  pl.BlockSpec((T, LANES), lambda p, i: (0, v_blk + p))],
        out_specs=pl.BlockSpec((n, LANES), lambda p, i: (i, p)),
        out_shape=jax.ShapeDtypeStruct((T, n_pairs * LANES), F32),
        compiler_params=_cparams(2), name=name)(qkv, qkv, qkv)


def stick_bwd(qkv, do, *, q_blk, k_blk, v_blk, do_blk, n_pairs, name):
    T = qkv.shape[0]
    n = STICK_T
    nq = T // n
    scale = HEAD_DIM ** -0.5

    def body(q_ref, k_ref, v_ref, do_ref, dq_ref, dk_ref, dv_ref):
        i = pl.program_id(1)
        m0 = _lane0()
        r2 = lax.broadcasted_iota(jnp.int32, (n, n), 0)
        c2 = lax.broadcasted_iota(jnp.int32, (n, n), 1)
        tri_after = (r2 > c2).astype(BF16)
        tri_from = (r2 >= c2).astype(BF16)

        @pl.when(i == 0)
        def _():
            dk_ref[...] = jnp.zeros_like(dk_ref)
            dv_ref[...] = jnp.zeros_like(dv_ref)

        qv = q_ref[...]
        dov = do_ref[...]
        dq_out = jnp.zeros((n, LANES), F32)
        for e in range(2):
            qm = _mask_half(qv, m0, e).astype(BF16)
            dom = _mask_half(dov, m0, e).astype(BF16)

            def tile(t, carry, qm=qm, dom=dom):
                start = pl.multiple_of((i - t) * n, n)
                kt_f = k_ref[pl.ds(start, n), :]
                vt = v_ref[pl.ds(start, n), :].astype(BF16)
                mask, lpos, lneg, lk = _stick_logits(qm, kt_f.astype(BF16), scale, t == 0)
                a = jnp.where(mask, jnp.exp(lpos + _cumsum_mm(lk, tri_after) + carry), 0.0)
                g = _dot_nt(dom, vt) * a
                return start, kt_f, mask, lpos, lneg, lk, a, g

            def alive(st):
                t, carry, _ = st
                return (t <= i) & (jnp.max(carry) > STICK_DEAD)

            def scan(st):
                t, carry, gtot = st
                _, _, _, _, _, lk, _, g = tile(t, carry)
                return (t + 1, carry + jnp.sum(lk, axis=-1, keepdims=True),
                        gtot + jnp.sum(g, axis=-1, keepdims=True))

            z1 = jnp.zeros((n, 1), F32)
            n_live, _, gtot = lax.while_loop(alive, scan, (jnp.int32(0), z1, z1))

            def step(t, st, e=e, qm=qm, dom=dom, gtot=gtot):
                dq_acc, carry, gright = st
                start, kt_f, mask, lpos, lneg, lk, a, g = tile(t, carry)
                before = gtot - (_cumsum_mm(g, tri_from) + gright)
                dz = jnp.where(mask, g * jnp.exp(lneg) - before * jnp.exp(lpos), 0.0) * scale
                dzb = dz.astype(BF16)
                dq_acc = dq_acc + _dot(dzb, _mask_half(kt_f, m0, e).astype(BF16))
                dk_ref[pl.ds(start, n), :] += _dot_tn(dzb, qm)
                dv_ref[pl.ds(start, n), :] += _dot_tn(a.astype(BF16), dom)
                return (dq_acc, carry + jnp.sum(lk, axis=-1, keepdims=True),
                        gright + jnp.sum(g, axis=-1, keepdims=True))

            dq_acc, _, _ = lax.fori_loop(0, n_live, step, (jnp.zeros((n, LANES), F32), z1, z1))
            dq_out = dq_out + dq_acc
        dq_ref[...] = dq_out

    tile = pl.BlockSpec((n, LANES), lambda p, i: (i, p))
    whole = pl.BlockSpec((T, LANES), lambda p, i: (0, p))
    shp = jax.ShapeDtypeStruct((T, n_pairs * LANES), F32)
    return pl.pallas_call(
        body, grid=(n_pairs, nq),
        in_specs=[pl.BlockSpec((n, LANES), lambda p, i: (i, q_blk + p)),
                  pl.BlockSpec((T, LANES), lambda p, i: (0, k_blk + p)),
                  pl.BlockSpec((T, LANES), lambda p, i: (0, v_blk + p)),
                  pl.BlockSpec((n, LANES), lambda p, i: (i, do_blk + p))],
        out_specs=[tile, whole, whole], out_shape=[shp, shp, shp],
        compiler_params=_cparams(2), name=name)(qkv, qkv, qkv, do)


def _xnorm(x):
    r = lax.rsqrt(jnp.mean(x * x, axis=-1, keepdims=True) + RMS_EPS)
    return r, x * r


def xattn_fwd(qraw, kvraw, q_gain, k_gain, *, tm, name):
    T = qraw.shape[0]
    scale = X_HEAD_DIM ** -0.5
    W = X_HEADS * X_HEAD_DIM

    def body(q_ref, kv_ref, qg_ref, kg_ref, o_ref):
        for h in range(X_HEADS):
            cs = pl.ds(X_HEAD_DIM * h, X_HEAD_DIM)
            _, qh = _xnorm(q_ref[:, cs])
            _, kh = _xnorm(kv_ref[:, cs])
            qn = (qh * qg_ref[...]).astype(BF16)
            kn = (kh * kg_ref[...]).astype(BF16)
            v = kv_ref[:, pl.ds(W + X_HEAD_DIM * h, X_HEAD_DIM)].astype(BF16)
            s = _dot_nt(qn, kn) * scale
            m = jnp.max(s, axis=-1, keepdims=True)
            p = jnp.exp(s - m)
            p = p / jnp.sum(p, axis=-1, keepdims=True)
            o_ref[:, cs] = _dot(p.astype(BF16), v)

    gspec = pl.BlockSpec((1, X_HEAD_DIM), lambda i: (0, 0))
    return pl.pallas_call(
        body, grid=(T // tm,),
        in_specs=[pl.BlockSpec((tm, W), lambda i: (i, 0)), pl.BlockSpec((MEM_LEN, 2 * W), lambda i: (0, 0)),
                  gspec, gspec],
        out_specs=pl.BlockSpec((tm, W), lambda i: (i, 0)),
        out_shape=jax.ShapeDtypeStruct((T, W), F32),
        compiler_params=_cparams(1), name=name)(qraw, kvraw, q_gain, k_gain)


def xattn_bwd(qraw, kvraw, q_gain, k_gain, do, o, *, tm, name):
    T = qraw.shape[0]
    nt = T // tm
    scale = X_HEAD_DIM ** -0.5
    W = X_HEADS * X_HEAD_DIM

    def body(q_ref, kv_ref, qg_ref, kg_ref, do_ref, o_ref, dq_ref, dkv_ref, dqg_ref, dkg_ref, dkn_ref):
        i = pl.program_id(0)

        @pl.when(i == 0)
        def _():
            dkv_ref[...] = jnp.zeros_like(dkv_ref)
            dkn_ref[...] = jnp.zeros_like(dkn_ref)
            dqg_ref[...] = jnp.zeros_like(dqg_ref)
            dkg_ref[...] = jnp.zeros_like(dkg_ref)

        qg = qg_ref[...]
        kg = kg_ref[...]
        dqg_acc = jnp.zeros((1, X_HEAD_DIM), F32)
        for h in range(X_HEADS):
            cs = pl.ds(X_HEAD_DIM * h, X_HEAD_DIM)
            vs = pl.ds(W + X_HEAD_DIM * h, X_HEAD_DIM)
            rq, qh = _xnorm(q_ref[:, cs])
            _, kh = _xnorm(kv_ref[:, cs])
            qn = (qh * qg).astype(BF16)
            kn = (kh * kg).astype(BF16)
            v = kv_ref[:, vs].astype(BF16)
            s = _dot_nt(qn, kn) * scale
            m = jnp.max(s, axis=-1, keepdims=True)
            p = jnp.exp(s - m)
            p = p / jnp.sum(p, axis=-1, keepdims=True)
            dov = do_ref[:, cs]
            delta = jnp.sum(dov * o_ref[:, cs], axis=-1, keepdims=True)
            dob = dov.astype(BF16)
            ds = (p * (_dot_nt(dob, v) - delta)).astype(BF16)
            dqn = _dot(ds, kn) * scale
            dkn_ref[:, cs] += _dot_tn(ds, qn) * scale
            dkv_ref[:, vs] += _dot_tn(p.astype(BF16), dob)
            dqg_acc = dqg_acc + jnp.sum(dqn * qh, axis=0, keepdims=True)
            dqh = dqn * qg
            dq_ref[:, cs] = rq * (dqh - qh * jnp.mean(dqh * qh, axis=-1, keepdims=True))
        dqg_ref[...] += dqg_acc

        @pl.when(i == nt - 1)
        def _():
            dkg_acc = jnp.zeros((1, X_HEAD_DIM), F32)
            for h in range(X_HEADS):
                cs = pl.ds(X_HEAD_DIM * h, X_HEAD_DIM)
                rk, kh = _xnorm(kv_ref[:, cs])
                dkn = dkn_ref[:, cs]
                dkg_acc = dkg_acc + jnp.sum(dkn * kh, axis=0, keepdims=True)
                dkh = dkn * kg
                dkv_ref[:, cs] = rk * (dkh - kh * jnp.mean(dkh * kh, axis=-1, keepdims=True))
            dkg_ref[...] = dkg_acc

    gspec = pl.BlockSpec((1, X_HEAD_DIM), lambda i: (0, 0))
    tile = pl.BlockSpec((tm, W), lambda i: (i, 0))
    kvspec = pl.BlockSpec((MEM_LEN, 2 * W), lambda i: (0, 0))
    grow = jax.ShapeDtypeStruct((1, X_HEAD_DIM), F32)
    return pl.pallas_call(
        body, grid=(nt,), in_specs=[tile, kvspec, gspec, gspec, tile, tile],
        out_specs=[tile, kvspec, gspec, gspec],
        out_shape=[jax.ShapeDtypeStruct((T, W), F32), jax.ShapeDtypeStruct((MEM_LEN, 2 * W), F32), grow, grow],
        scratch_shapes=[pltpu.VMEM((MEM_LEN, W), F32)],
        compiler_params=_cparams(1), name=name)(qraw, kvraw, q_gain, k_gain, do, o)


_ANY = pl.BlockSpec(memory_space=pl.ANY)


def _my_pos():
    return lax.axis_index("x"), lax.axis_index("y"), lax.axis_index("c")


def all_gather8(blk, *, name):
    R, C = blk.shape

    def body(x_ref, out_ref, send_sems, recv_sems, local_sem):
        x, y, c = _my_pos()
        me, sibling = (x, y, c), (x, y, 1 - c)
        chips = [(1 - x, y), (x, 1 - y), (1 - x, 1 - y)]

        def slot(px, py, pc):
            return out_ref.at[4 * px + 2 * py + pc]

        def copy(k, block, to, src=None):
            return pltpu.make_async_remote_copy(
                src_ref=slot(*block) if src is None else src, dst_ref=slot(*block),
                send_sem=send_sems.at[k], recv_sem=recv_sems.at[k], device_id=to, device_id_type=MESH)

        mine = pltpu.make_async_copy(x_ref, slot(*me), local_sem)
        mine.start()
        first = [copy(0, me, sibling, src=x_ref)]
        first += [copy(1 + j, me, (*chip, c), src=x_ref) for j, chip in enumerate(chips)]
        for cp in first:
            cp.start()
        passed = [copy(4 + j, (*chip, c), sibling) for j, chip in enumerate(chips)]
        for j, chip in enumerate(chips):
            copy(1 + j, (*chip, c), me).wait_recv()
            passed[j].start()
        copy(0, sibling, me).wait_recv()
        for j, chip in enumerate(chips):
            copy(4 + j, (*chip, 1 - c), me).wait_recv()
        for cp in first + passed:
            cp.wait_send()
        mine.wait()

    return pl.pallas_call(
        body, out_shape=jax.ShapeDtypeStruct((N_DEV, R, C), blk.dtype),
        in_specs=[_ANY], out_specs=_ANY,
        scratch_shapes=[pltpu.SemaphoreType.DMA((7,)), pltpu.SemaphoreType.DMA((7,)), pltpu.SemaphoreType.DMA],
        name=name)(blk)


def all_to_all8(big, small, *, name):
    _, R, C = big.shape
    S, C2 = small.shape

    def body(big_ref, small_ref, land_ref, lands_ref, send_sems, recv_sems, ssend_sems, srecv_sems, local_sems):
        x, y, c = _my_pos()
        my_id = 4 * x + 2 * y + c

        def peer(k):
            return x ^ ((k >> 2) & 1), y ^ ((k >> 1) & 1), c ^ (k & 1)

        def copies(k, to_slot):
            px, py, pc = peer(k)
            peer_id = 4 * px + 2 * py + pc
            dst = my_id if to_slot == "mine" else peer_id
            cp = pltpu.make_async_remote_copy(
                src_ref=big_ref.at[peer_id], dst_ref=land_ref.at[dst],
                send_sem=send_sems.at[k - 1], recv_sem=recv_sems.at[k - 1],
                device_id=(px, py, pc), device_id_type=MESH)
            cps = pltpu.make_async_remote_copy(
                src_ref=small_ref, dst_ref=lands_ref.at[dst],
                send_sem=ssend_sems.at[k - 1], recv_sem=srecv_sems.at[k - 1],
                device_id=(px, py, pc), device_id_type=MESH)
            return cp, cps

        own = pltpu.make_async_copy(big_ref.at[my_id], land_ref.at[my_id], local_sems.at[0])
        own.start()
        owns = pltpu.make_async_copy(small_ref, lands_ref.at[my_id], local_sems.at[1])
        owns.start()
        sends = []
        for k in range(1, N_DEV):
            cp, cps = copies(k, "mine")
            cp.start()
            cps.start()
            sends += [cp, cps]
        for k in range(1, N_DEV):
            cp, cps = copies(k, "theirs")
            cp.wait_recv()
            cps.wait_recv()
        for cp in sends:
            cp.wait_send()
        own.wait()
        owns.wait()

    dma7 = pltpu.SemaphoreType.DMA((7,))
    return pl.pallas_call(
        body,
        out_shape=[jax.ShapeDtypeStruct((N_DEV, R, C), big.dtype), jax.ShapeDtypeStruct((N_DEV, S, C2), small.dtype)],
        in_specs=[_ANY, _ANY], out_specs=[_ANY, _ANY],
        scratch_shapes=[dma7, dma7, dma7, dma7, pltpu.SemaphoreType.DMA((2,))],
        name=name)(big, small)


def sibling_swap(blk, *, name):
    R, C = blk.shape

    def body(x_ref, out_ref, send_sem, recv_sem, local_sem):
        x, y, c = _my_pos()

        def copy(slot):
            return pltpu.make_async_remote_copy(src_ref=x_ref, dst_ref=out_ref.at[slot], send_sem=send_sem,
                                                recv_sem=recv_sem, device_id=(x, y, 1 - c), device_id_type=MESH)

        own = pltpu.make_async_copy(x_ref, out_ref.at[c], local_sem)
        own.start()
        cp = copy(c)
        cp.start()
        copy(1 - c).wait_recv()
        cp.wait_send()
        own.wait()

    return pl.pallas_call(
        body, out_shape=jax.ShapeDtypeStruct((2, R, C), blk.dtype), in_specs=[_ANY], out_specs=_ANY,
        scratch_shapes=[pltpu.SemaphoreType.DMA, pltpu.SemaphoreType.DMA, pltpu.SemaphoreType.DMA],
        name=name)(blk)


N_CHUNK = 8


def _chunk(ref, ch, rc):
    return ref.at[pl.ds(ch * rc, rc)]


def gather_blocks(blk, *, name):
    R, C = blk.shape
    rc = R // N_CHUNK

    def body(x_ref, out_ref, send_sems, recv_sems, local_sems):
        x, y, c = _my_pos()
        me, sibling = (x, y, c), (x, y, 1 - c)
        chips = [(1 - x, y), (x, 1 - y), (1 - x, 1 - y)]

        def slot(block, ch):
            px, py, pc = block
            return out_ref.at[4 * px + 2 * py + pc, pl.ds(ch * rc, rc)]

        def copy(k, ch, block, to, from_input=False):
            return pltpu.make_async_remote_copy(
                src_ref=_chunk(x_ref, ch, rc) if from_input else slot(block, ch), dst_ref=slot(block, ch),
                send_sem=send_sems.at[k * N_CHUNK + ch], recv_sem=recv_sems.at[k * N_CHUNK + ch],
                device_id=to, device_id_type=MESH)

        chs = range(N_CHUNK)
        mine = [pltpu.make_async_copy(_chunk(x_ref, ch, rc), slot(me, ch), local_sems.at[ch]) for ch in chs]
        first = [copy(1 + j, ch, me, (*chip, c), from_input=True) for j, chip in enumerate(chips) for ch in chs]
        first += [copy(0, ch, me, sibling, from_input=True) for ch in chs]
        for cp in first + mine:
            cp.start()
        passed = []
        for j, chip in enumerate(chips):
            for ch in chs:
                copy(1 + j, ch, (*chip, c), me).wait_recv()
                fwd = copy(4 + j, ch, (*chip, c), sibling)
                fwd.start()
                passed.append(fwd)
        for ch in chs:
            copy(0, ch, sibling, me).wait_recv()
        for j, chip in enumerate(chips):
            for ch in chs:
                copy(4 + j, ch, (*chip, 1 - c), me).wait_recv()
        for cp in first + passed:
            cp.wait_send()
        for cp in mine:
            cp.wait()

    n_sem = 7 * N_CHUNK
    return pl.pallas_call(
        body, out_shape=jax.ShapeDtypeStruct((N_DEV, R, C), blk.dtype), in_specs=[_ANY], out_specs=_ANY,
        scratch_shapes=[pltpu.SemaphoreType.DMA((n_sem,)), pltpu.SemaphoreType.DMA((n_sem,)),
                        pltpu.SemaphoreType.DMA((N_CHUNK,))],
        name=name)(blk)


def gather_small(small, *, name):
    S, C = small.shape

    def body(s_ref, out_ref, send_sems, recv_sems, local_sem):
        x, y, c = _my_pos()
        my_id = 4 * x + 2 * y + c

        def copy(k, slot):
            px, py, pc = x ^ ((k >> 2) & 1), y ^ ((k >> 1) & 1), c ^ (k & 1)
            dst = my_id if slot == "mine" else 4 * px + 2 * py + pc
            return pltpu.make_async_remote_copy(
                src_ref=s_ref, dst_ref=out_ref.at[dst], send_sem=send_sems.at[k - 1], recv_sem=recv_sems.at[k - 1],
                device_id=(px, py, pc), device_id_type=MESH)

        own = pltpu.make_async_copy(s_ref, out_ref.at[my_id], local_sem)
        own.start()
        sends = [copy(k, "mine") for k in range(1, N_DEV)]
        for cp in sends:
            cp.start()
        for k in range(1, N_DEV):
            copy(k, "theirs").wait_recv()
        for cp in sends:
            cp.wait_send()
        own.wait()

    dma7 = pltpu.SemaphoreType.DMA((7,))
    return pl.pallas_call(
        body, out_shape=jax.ShapeDtypeStruct((N_DEV, S, C), small.dtype), in_specs=[_ANY], out_specs=_ANY,
        scratch_shapes=[dma7, dma7, pltpu.SemaphoreType.DMA], name=name)(small)


def pair_exchange(big, *, name):
    _, R, C = big.shape
    rc = R // N_CHUNK

    def body(big_ref, out_ref, send_sems, recv_sems):
        x, y, c = _my_pos()

        def copy(b, ch):
            return pltpu.make_async_remote_copy(
                src_ref=big_ref.at[2 * b + (1 - c), pl.ds(ch * rc, rc)], dst_ref=out_ref.at[b, pl.ds(ch * rc, rc)],
                send_sem=send_sems.at[b * N_CHUNK + ch], recv_sem=recv_sems.at[b * N_CHUNK + ch],
                device_id=(x, y, 1 - c), device_id_type=MESH)

        cps = [copy(b, ch) for b in range(4) for ch in range(N_CHUNK)]
        for cp in cps:
            cp.start()
        for cp in cps:
            cp.wait_recv()
        for cp in cps:
            cp.wait_send()

    n_sem = 4 * N_CHUNK
    return pl.pallas_call(
        body, out_shape=jax.ShapeDtypeStruct((4, R, C), big.dtype), in_specs=[_ANY], out_specs=_ANY,
        scratch_shapes=[pltpu.SemaphoreType.DMA((n_sem,)), pltpu.SemaphoreType.DMA((n_sem,))],
        name=name)(big)


def pair_sum(big, sib, c, *, tr, name):
    _, R, C = big.shape

    def body(c_ref, a_ref, s_ref, o_ref):
        o_ref[...] = (a_ref[...].astype(F32) + s_ref[...].astype(F32)).astype(o_ref.dtype)

    grid_spec = pltpu.PrefetchScalarGridSpec(
        num_scalar_prefetch=1, grid=(4, R // tr),
        in_specs=[pl.BlockSpec((None, tr, C), lambda b, i, c_ref: (2 * b + c_ref[0], i, 0)),
                  pl.BlockSpec((None, tr, C), lambda b, i, c_ref: (b, i, 0))],
        out_specs=pl.BlockSpec((None, tr, C), lambda b, i, c_ref: (b, i, 0)))
    return pl.pallas_call(body, grid_spec=grid_spec, out_shape=jax.ShapeDtypeStruct((4, R, C), big.dtype),
                          compiler_params=_cparams(2), name=name)(c.reshape(1).astype(jnp.int32), big, sib)


def chip_scatter(pre, *, name):
    _, R, C = pre.shape
    rc = R // N_CHUNK

    def body(pre_ref, out_ref, send_sems, recv_sems, local_sems):
        x, y, c = _my_pos()
        my_chip = 2 * x + y
        chips = [(1 - x, y), (x, 1 - y), (1 - x, 1 - y)]

        def copy(j, ch, slot):
            px, py = chips[j]
            src_slot, dst_slot = (2 * px + py, my_chip) if slot == "mine" else (my_chip, 2 * px + py)
            return pltpu.make_async_remote_copy(
                src_ref=pre_ref.at[src_slot, pl.ds(ch * rc, rc)], dst_ref=out_ref.at[dst_slot, pl.ds(ch * rc, rc)],
                send_sem=send_sems.at[j * N_CHUNK + ch], recv_sem=recv_sems.at[j * N_CHUNK + ch],
                device_id=(px, py, c), device_id_type=MESH)

        chs = range(N_CHUNK)
        own = [pltpu.make_async_copy(pre_ref.at[my_chip, pl.ds(ch * rc, rc)], out_ref.at[my_chip, pl.ds(ch * rc, rc)],
                                     local_sems.at[ch]) for ch in chs]
        sends = [copy(j, ch, "mine") for j in range(3) for ch in chs]
        for cp in sends + own:
            cp.start()
        for j in range(3):
            for ch in chs:
                copy(j, ch, "theirs").wait_recv()
        for cp in sends:
            cp.wait_send()
        for cp in own:
            cp.wait()

    n_sem = 3 * N_CHUNK
    return pl.pallas_call(
        body, out_shape=jax.ShapeDtypeStruct((4, R, C), pre.dtype), in_specs=[_ANY], out_specs=_ANY,
        scratch_shapes=[pltpu.SemaphoreType.DMA((n_sem,)), pltpu.SemaphoreType.DMA((n_sem,)),
                        pltpu.SemaphoreType.DMA((N_CHUNK,))],
        name=name)(pre)


def sibling_send(blk, *, name):
    R, C = blk.shape
    rc = R // N_CHUNK

    def body(x_ref, out_ref, send_sems, recv_sems):
        x, y, c = _my_pos()
        cps = [pltpu.make_async_remote_copy(
            src_ref=_chunk(x_ref, ch, rc), dst_ref=_chunk(out_ref, ch, rc), send_sem=send_sems.at[ch],
            recv_sem=recv_sems.at[ch], device_id=(x, y, 1 - c), device_id_type=MESH) for ch in range(N_CHUNK)]
        for cp in cps:
            cp.start()
        for cp in cps:
            cp.wait_recv()
        for cp in cps:
            cp.wait_send()

    return pl.pallas_call(
        body, out_shape=jax.ShapeDtypeStruct((R, C), blk.dtype), in_specs=[_ANY], out_specs=_ANY,
        scratch_shapes=[pltpu.SemaphoreType.DMA((N_CHUNK,)), pltpu.SemaphoreType.DMA((N_CHUNK,))],
        name=name)(blk)


def reduce_slots(land, *, tr, name):
    n, R, C = land.shape

    def body(l_ref, o_ref):
        acc = l_ref[0].astype(F32)
        for s in range(1, n):
            acc = acc + l_ref[s].astype(F32)
        o_ref[...] = acc

    return pl.pallas_call(
        body, grid=(R // tr,), in_specs=[pl.BlockSpec((n, tr, C), lambda i: (0, i, 0))],
        out_specs=pl.BlockSpec((tr, C), lambda i: (i, 0)), out_shape=jax.ShapeDtypeStruct((R, C), F32),
        compiler_params=_cparams(1), name=name)(land)


TM = 512


def ffn_fwd(x, g, wgu, wd, tag):
    gu, h = norm_matmul(x, g, wgu, tm=TM, tn=1408, split=True, name=f"{tag}_gu")
    xo = mm_nn(gu, wd, res=x, scale=0.5, swiglu=True, tm=TM, tn=D_MODEL, tk=256, name=f"{tag}_down")
    return xo, (x, gu, h)


def ffn_bwd(d, saved, g, wgu, wd, tag):
    x, gu, h = saved
    dgu, act = ffn_bwd_act(d, wd, gu, tm=256, tn=1408, name=f"{tag}_bact")
    dwd = mm_tn(act, d, scale=0.5, a_split=False, b_split=False, tm=1408, tn=D_MODEL, tk=TM, name=f"{tag}_dwd")
    dwgu = mm_tn(h, dgu, scale=1.0, a_split=False, b_split=True, tm=D_MODEL, tn=1408, tk=TM, name=f"{tag}_dwgu")
    dx, dg = mm_nt_normbwd(dgu, wgu, x, g, d, a_split=True, tm=TM, tk=256, name=f"{tag}_dx")
    return dx, dg, dwgu, dwd


def _tile2(v):
    return jnp.concatenate([v, v], axis=-1).reshape(1, LANES)


def _fold2(v):
    return v[:, :HEAD_DIM] + v[:, HEAD_DIM:]


EVEN = dict(dil=1, ppk=4, q_blk=0, k_blk=4, v_blk=5, n_heads=A_Q_HEADS, group=A_GROUP, max_dist=A_WINDOW - 1)
STICK = dict(q_blk=6, k_blk=10, v_blk=14, n_pairs=4)


def _odd_cfg(dil):
    return dict(dil=dil, ppk=1, q_blk=0, k_blk=8, v_blk=16, n_heads=C_HEADS, group=1, max_dist=BLK)


def even_fwd(x, g, win, qg, kg, sinks, wout, tag):
    qkv, h = norm_matmul(x, g, win, tm=TM, tn=1152, split=False, name=f"{tag}_in")
    qg2, kg2 = _tile2(qg), _tile2(kg)
    slopes = jnp.asarray(_alibi(A_Q_HEADS), F32)
    oa, lse = banded_fwd(qkv, qg2, kg2, slopes, sinks, name=f"{tag}_swa", **EVEN)
    ob = stick_fwd(qkv, name=f"{tag}_stick", **STICK)
    o = jnp.concatenate([oa, ob], axis=1)
    xo = mm_nn(o, wout, res=x, scale=1.0, swiglu=False, tm=TM, tn=D_MODEL, tk=D_MODEL, name=f"{tag}_out")
    return xo, (x, qkv, h, oa, lse, o)


def even_bwd(d, saved, g, win, qg, kg, sinks, wout, tag):
    x, qkv, h, oa, lse, o = saved
    qg2, kg2 = _tile2(qg), _tile2(kg)
    slopes = jnp.asarray(_alibi(A_Q_HEADS), F32)
    dwout = mm_tn(o, d, scale=1.0, a_split=False, b_split=False, tm=D_MODEL, tn=D_MODEL, tk=TM, name=f"{tag}_dwout")
    do = mm_nt(d, wout, tm=TM, tn=D_MODEL, tk=D_MODEL, name=f"{tag}_do")
    dqa, dka4, dva4, dqg, dkg, dsk = banded_bwd(qkv, qg2, kg2, slopes, sinks, do, oa, lse, None, None,
                                                do_blk=0, name=f"{tag}_swa_b", **EVEN)
    dqb, dkb, dvb = stick_bwd(qkv, do, do_blk=4, name=f"{tag}_stick_b", **STICK)
    dqkv = assemble_even(dqa, dka4, dva4, dqb, dkb, dvb, tm=TM, name=f"{tag}_asm")
    dwin = mm_tn(h, dqkv, scale=1.0, a_split=False, b_split=False, tm=D_MODEL, tn=1152, tk=TM, name=f"{tag}_dwin")
    dx, dg = mm_nt_normbwd(dqkv, win, x, g, d, a_split=False, tm=TM, tk=256, name=f"{tag}_dx")
    return dx, dg, dwin, _fold2(dqg), _fold2(dkg), dsk[:, :A_Q_HEADS], dwout


def odd_fwd(x, g, win, qg, kg, wout, tag):
    qkv, h = norm_matmul(x, g, win, tm=TM, tn=1536, split=False, name=f"{tag}_in")
    qg2, kg2 = _tile2(qg), _tile2(kg)
    outs = []
    for p, (window, dil) in enumerate(C_PATTERNS):
        slopes = jnp.asarray(_alibi(C_HEADS), F32) * float(dil)
        outs.append(banded_fwd(qkv, qg2, kg2, slopes, None, name=f"{tag}_dil{p}", **_odd_cfg(dil)))
    o, w1, w2, w3 = mix_fwd(outs[0][0], outs[1][0], outs[2][0], outs[0][1], outs[1][1], outs[2][1],
                            tm=TM, name=f"{tag}_mix")
    xo = mm_nn(o, wout, res=x, scale=1.0, swiglu=False, tm=TM, tn=D_MODEL, tk=D_MODEL, name=f"{tag}_out")
    return xo, (x, qkv, h, outs, (w1, w2, w3), o)


def odd_bwd(d, saved, g, win, qg, kg, wout, tag):
    x, qkv, h, outs, ws, o = saved
    qg2, kg2 = _tile2(qg), _tile2(kg)
    dwout = mm_tn(o, d, scale=1.0, a_split=False, b_split=False, tm=D_MODEL, tn=D_MODEL, tk=TM, name=f"{tag}_dwout")
    do = mm_nt(d, wout, tm=TM, tn=D_MODEL, tk=D_MODEL, name=f"{tag}_do")
    parts, dqg, dkg = [], 0.0, 0.0
    for p, (window, dil) in enumerate(C_PATTERNS):
        slopes = jnp.asarray(_alibi(C_HEADS), F32) * float(dil)
        dq, dk, dv, dqg_p, dkg_p, _ = banded_bwd(qkv, qg2, kg2, slopes, None, do, outs[p][0], outs[p][1], ws[p], o,
                                                 do_blk=0, name=f"{tag}_dil{p}_b", **_odd_cfg(dil))
        parts.append((dq, dk, dv))
        dqg = dqg + dqg_p
        dkg = dkg + dkg_p
    dqkv = assemble_odd(parts, tm=TM, name=f"{tag}_asm")
    dwin = mm_tn(h, dqkv, scale=1.0, a_split=False, b_split=False, tm=D_MODEL, tn=1536, tk=TM, name=f"{tag}_dwin")
    dx, dg = mm_nt_normbwd(dqkv, win, x, g, d, a_split=False, tm=TM, tk=256, name=f"{tag}_dx")
    return dx, dg, dwin, _fold2(dqg), _fold2(dkg), dwout


def xa_fwd(x, mem, g, gm, wq, wkv, qg, kg, wo, tag):
    qraw, h = norm_matmul(x, g, wq, tm=TM, tn=D_MODEL, split=False, name=f"{tag}_q")
    kvraw, hm = norm_matmul(mem, gm, wkv, tm=MEM_LEN, tn=D_MODEL, split=False, name=f"{tag}_kv")
    o = xattn_fwd(qraw, kvraw, qg, kg, tm=TM, name=f"{tag}_att")
    xo = mm_nn(o, wo, res=x, scale=1.0, swiglu=False, tm=TM, tn=D_MODEL, tk=D_MODEL, name=f"{tag}_o")
    return xo, (x, qraw, h, kvraw, hm, o)


def xa_bwd(d, saved, mem, g, gm, wq, wkv, qg, kg, wo, tag):
    x, qraw, h, kvraw, hm, o = saved
    dwo = mm_tn(o, d, scale=1.0, a_split=False, b_split=False, tm=D_MODEL, tn=D_MODEL, tk=TM, name=f"{tag}_dwo")
    do = mm_nt(d, wo, tm=TM, tn=D_MODEL, tk=D_MODEL, name=f"{tag}_do")
    dq, dkv, dqg, dkg = xattn_bwd(qraw, kvraw, qg, kg, do, o, tm=TM, name=f"{tag}_att_b")
    dwq = mm_tn(h, dq, scale=1.0, a_split=False, b_split=False, tm=D_MODEL, tn=D_MODEL, tk=TM, name=f"{tag}_dwq")
    dx, dg = mm_nt_normbwd(dq, wq, x, g, d, a_split=False, tm=TM, tk=256, name=f"{tag}_dx")
    dwkv = mm_tn(hm, dkv, scale=1.0, a_split=False, b_split=False, tm=D_MODEL, tn=D_MODEL, tk=MEM_LEN,
                 name=f"{tag}_dwkv")
    _, dgm = mm_nt_normbwd(dkv, wkv, mem, gm, None, a_split=False, tm=MEM_LEN, tk=256, name=f"{tag}_dmem")
    return dx, dg, dgm, dwq, dwkv, dqg, dkg, dwo


MATS = (("ffn1_w_gu", 1), ("ffn1_w_down", 0), ("ev_w_in", 1), ("ev_w_out", 0), ("od_w_in", 1), ("od_w_out", 0),
        ("xa_w_q", 0), ("xa_w_kv", 1), ("xa_w_o", 0), ("ffn2_w_gu", 1), ("ffn2_w_down", 0))
SMALLS = ("ffn1_norm", "mix_norm", "ev_q_gain", "ev_k_gain", "ev_sinks", "od_q_gain", "od_k_gain", "xa_norm",
          "xa_mem_norm", "xa_q_gain", "xa_k_gain", "ffn2_norm")
WEIGHTS = ("ffn1_norm", "ffn1_w_gu", "ffn1_w_down", "mix_norm", "ev_w_in", "ev_q_gain", "ev_k_gain", "ev_sinks",
           "ev_w_out", "od_w_in", "od_q_gain", "od_k_gain", "od_w_out", "xa_norm", "xa_mem_norm", "xa_w_q",
           "xa_w_kv", "xa_q_gain", "xa_k_gain", "xa_w_o", "ffn2_norm", "ffn2_w_gu", "ffn2_w_down")
PACK_W = 1024
SMALL_ROWS = 16
SUM_ROWS = 400


def _mat_items(shards):
    items = []
    for name, axis in MATS:
        nl, r, cc = shards[name].shape
        for layer in range(nl):
            items.append((name, layer, axis, (r, cc)))
    return items


def _padded_rows(n):
    unit = SUM_ROWS * N_CHUNK
    return -(-n // unit) * unit


def pack_halves(shards, c):
    rows = []
    for name, layer, axis, (r, cc) in _mat_items(shards):
        half = lax.dynamic_index_in_dim(shards[name][layer].reshape(2, r // 2, cc), c, 0, keepdims=False)
        rows.append(half.astype(BF16).reshape(-1, PACK_W))
    used = sum(r.shape[0] for r in rows)
    rows.append(jnp.zeros((_padded_rows(used) - used, PACK_W), BF16))
    return jnp.concatenate(rows, axis=0)


def unpack_full(gathered, shards):
    full, off = {}, 0
    for name, layer, axis, (r, cc) in _mat_items(shards):
        nr = (r // 2) * cc // PACK_W
        piece = gathered[:, off:off + nr].reshape(4, r, cc)
        off += nr
        full[(name, layer)] = piece.reshape(4 * r, cc) if axis == 0 else piece.transpose(1, 0, 2).reshape(r, 4 * cc)
    return full


def pack_grads(grads, shards):
    rows = []
    for name, layer, axis, (r, cc) in _mat_items(shards):
        gfull = grads[(name, layer)]
        if axis == 0:
            piece = gfull.reshape(4, 2, r // 2, cc)
        else:
            piece = gfull.reshape(2, r // 2, 4, cc).transpose(2, 0, 1, 3)
        rows.append(piece.astype(BF16).reshape(N_DEV, -1, PACK_W))
    used = sum(r.shape[1] for r in rows)
    rows.append(jnp.zeros((N_DEV, _padded_rows(used) - used, PACK_W), BF16))
    return jnp.concatenate(rows, axis=1)


def unpack_shard_grads(summed, shards):
    per, off = {}, 0
    for name, layer, axis, (r, cc) in _mat_items(shards):
        nr = (r // 2) * cc // PACK_W
        per.setdefault(name, []).append(summed[:, off:off + nr].reshape(r, cc))
        off += nr
    return {name: jnp.stack(lst, axis=0) for name, lst in per.items()}


def pack_small(vals):
    row10 = jnp.concatenate([vals["xa_q_gain"].reshape(1, 512), vals["xa_k_gain"].reshape(1, 512)], axis=1)
    row11 = jnp.concatenate([vals["ev_q_gain"], vals["ev_k_gain"], vals["od_q_gain"], vals["od_k_gain"],
                             vals["ev_sinks"], jnp.zeros((1, 1024 - 4 * 64 - 8), F32)], axis=1)
    return jnp.concatenate([vals["ffn1_norm"], vals["mix_norm"], vals["xa_norm"], vals["xa_mem_norm"],
                            vals["ffn2_norm"], row10, row11, jnp.zeros((SMALL_ROWS - 12, 1024), F32)], axis=0)


def unpack_small(arr):
    return {"ffn1_norm": arr[0:2], "mix_norm": arr[2:4], "xa_norm": arr[4:6], "xa_mem_norm": arr[6:8],
            "ffn2_norm": arr[8:10],
            "xa_q_gain": arr[10:11, 0:512].reshape(2, 256), "xa_k_gain": arr[10:11, 512:1024].reshape(2, 256),
            "ev_q_gain": arr[11:12, 0:64], "ev_k_gain": arr[11:12, 64:128], "od_q_gain": arr[11:12, 128:192],
            "od_k_gain": arr[11:12, 192:256], "ev_sinks": arr[11:12, 256:264]}


def local_step(x, mem, target, W, small):
    depth = small["ffn1_norm"].shape[0]

    def row(name, l):
        return small[name][l:l + 1]

    saved = []
    for l in range(depth):
        j = l // 2
        x, s1 = ffn_fwd(x, row("ffn1_norm", l), W[("ffn1_w_gu", l)], W[("ffn1_w_down", l)], f"l{l}_f1")
        if l % 2 == 0:
            x, s2 = even_fwd(x, row("mix_norm", l), W[("ev_w_in", j)], row("ev_q_gain", j), row("ev_k_gain", j),
                             small["ev_sinks"][j], W[("ev_w_out", j)], f"l{l}_ev")
        else:
            x, s2 = odd_fwd(x, row("mix_norm", l), W[("od_w_in", j)], row("od_q_gain", j), row("od_k_gain", j),
                            W[("od_w_out", j)], f"l{l}_od")
        x, s3 = xa_fwd(x, mem, row("xa_norm", l), row("xa_mem_norm", l), W[("xa_w_q", l)], W[("xa_w_kv", l)],
                       row("xa_q_gain", l), row("xa_k_gain", l), W[("xa_w_o", l)], f"l{l}_xa")
        x, s4 = ffn_fwd(x, row("ffn2_norm", l), W[("ffn2_w_gu", l)], W[("ffn2_w_down", l)], f"l{l}_f2")
        saved.append((s1, s2, s3, s4))
    loss, d = loss_kernel(x, target, tm=TM, name="loss")

    gw = {}
    gs = {name: [None] * small[name].shape[0] for name in SMALLS}
    for l in reversed(range(depth)):
        j = l // 2
        s1, s2, s3, s4 = saved[l]
        d, dg, dwgu, dwd = ffn_bwd(d, s4, row("ffn2_norm", l), W[("ffn2_w_gu", l)], W[("ffn2_w_down", l)], f"l{l}_f2")
        gs["ffn2_norm"][l] = dg
        gw[("ffn2_w_gu", l)], gw[("ffn2_w_down", l)] = dwgu, dwd
        d, dg, dgm, dwq, dwkv, dqg, dkg, dwo = xa_bwd(
            d, s3, mem, row("xa_norm", l), row("xa_mem_norm", l), W[("xa_w_q", l)], W[("xa_w_kv", l)],
            row("xa_q_gain", l), row("xa_k_gain", l), W[("xa_w_o", l)], f"l{l}_xa")
        gs["xa_norm"][l], gs["xa_mem_norm"][l], gs["xa_q_gain"][l], gs["xa_k_gain"][l] = dg, dgm, dqg, dkg
        gw[("xa_w_q", l)], gw[("xa_w_kv", l)], gw[("xa_w_o", l)] = dwq, dwkv, dwo
        if l % 2 == 0:
            d, dg, dwin, dqg, dkg, dsk, dwout = even_bwd(
                d, s2, row("mix_norm", l), W[("ev_w_in", j)], row("ev_q_gain", j), row("ev_k_gain", j),
                small["ev_sinks"][j], W[("ev_w_out", j)], f"l{l}_ev")
            gs["ev_q_gain"][j], gs["ev_k_gain"][j], gs["ev_sinks"][j] = dqg, dkg, dsk
            gw[("ev_w_in", j)], gw[("ev_w_out", j)] = dwin, dwout
        else:
            d, dg, dwin, dqg, dkg, dwout = odd_bwd(
                d, s2, row("mix_norm", l), W[("od_w_in", j)], row("od_q_gain", j), row("od_k_gain", j),
                W[("od_w_out", j)], f"l{l}_od")
            gs["od_q_gain"][j], gs["od_k_gain"][j] = dqg, dkg
            gw[("od_w_in", j)], gw[("od_w_out", j)] = dwin, dwout
        gs["mix_norm"][l] = dg
        d, dg, dwgu, dwd = ffn_bwd(d, s1, row("ffn1_norm", l), W[("ffn1_w_gu", l)], W[("ffn1_w_down", l)], f"l{l}_f1")
        gs["ffn1_norm"][l] = dg
        gw[("ffn1_w_gu", l)], gw[("ffn1_w_down", l)] = dwgu, dwd
    gsmall = {name: jnp.concatenate(v, axis=0) for name, v in gs.items()}
    return loss, d, gw, gsmall


def kernel(x, mem, ffn1_norm, ffn1_w_gu, ffn1_w_down, mix_norm, ev_w_in, ev_q_gain, ev_k_gain, ev_sinks, ev_w_out, od_w_in, od_q_gain, od_k_gain, od_w_out, xa_norm, xa_mem_norm, xa_w_q, xa_w_kv, xa_q_gain, xa_k_gain, xa_w_o, ffn2_norm, ffn2_w_gu, ffn2_w_down, loss_target, m_ffn1_norm, m_ffn1_w_gu, m_ffn1_w_down, m_mix_norm, m_ev_w_in, m_ev_q_gain, m_ev_k_gain, m_ev_sinks, m_ev_w_out, m_od_w_in, m_od_q_gain, m_od_k_gain, m_od_w_out, m_xa_norm, m_xa_mem_norm, m_xa_w_q, m_xa_w_kv, m_xa_q_gain, m_xa_k_gain, m_xa_w_o, m_ffn2_norm, m_ffn2_w_gu, m_ffn2_w_down, v_ffn1_norm, v_ffn1_w_gu, v_ffn1_w_down, v_mix_norm, v_ev_w_in, v_ev_q_gain, v_ev_k_gain, v_ev_sinks, v_ev_w_out, v_od_w_in, v_od_q_gain, v_od_k_gain, v_od_w_out, v_xa_norm, v_xa_mem_norm, v_xa_w_q, v_xa_w_kv, v_xa_q_gain, v_xa_k_gain, v_xa_w_o, v_ffn2_norm, v_ffn2_w_gu, v_ffn2_w_down):
    given = dict(locals())
    w = {n: given[n] for n in WEIGHTS}
    m = {n: given["m_" + n] for n in WEIGHTS}
    v = {n: given["v_" + n] for n in WEIGHTS}
    c = lax.axis_index("c")
    shards = {name: w[name] for name, _ in MATS}
    small = {n: w[n] for n in SMALLS}

    gathered = gather_blocks(pack_halves(shards, c), name="gather_weights")
    full = unpack_full(gathered, shards)
    loss_b, grad_x, gw, gsmall = local_step(x[0], mem[0], loss_target[0], full, small)

    packed = pack_grads(gw, shards)
    pre = pair_sum(packed, pair_exchange(packed, name="pair_grads"), c, tr=SUM_ROWS, name="pair_sum")
    mine = reduce_slots(chip_scatter(pre, name="scatter_grads"), tr=SUM_ROWS, name="sum_grads")
    theirs = sibling_send(mine, name="swap_grads")
    both = jnp.stack([jnp.where(c == 0, mine, theirs), jnp.where(c == 0, theirs, mine)])
    land_small = gather_small(pack_small(gsmall), name="gather_small")
    g_small = unpack_small(reduce_slots(land_small, tr=SMALL_ROWS, name="sum_small"))
    g = dict(unpack_shard_grads(both, shards))
    g.update(g_small)

    delta, new_m, new_v = {}, {}, {}
    for name, _ in MATS:
        shp = w[name].shape
        flat = [a.reshape(-1, shp[-1]) for a in (w[name], g[name], m[name], v[name])]
        dl, nm, nv = adamw(*flat, br=BLK, name=f"adamw_{name}")
        delta[name], new_m[name], new_v[name] = dl.reshape(shp), nm.reshape(shp), nv.reshape(shp)
    dl, nm, nv = adamw(pack_small(small), pack_small(g_small), pack_small({n: m[n] for n in SMALLS}),
                       pack_small({n: v[n] for n in SMALLS}), br=SMALL_ROWS, name="adamw_small")
    for dst, arr in ((delta, dl), (new_m, nm), (new_v, nv)):
        dst.update(unpack_small(arr))

    loss = lax.psum(loss_b[0, 0], ("x", "y", "c"))
    return (loss, grad_x[None], *[g[n] for n in WEIGHTS], *[delta[n] for n in WEIGHTS],
            *[new_m[n] for n in WEIGHTS], *[new_v[n] for n in WEIGHTS])
```

```python
import jax
import jax.numpy as jnp
from jax import lax
from jax.experimental import pallas as pl
from jax.experimental.pallas import tpu as pltpu

F32 = jnp.float32
BF16 = jnp.bfloat16

D_MODEL = 1024
HEAD_DIM = 64
LANES = 128
BLK = 128
D_FF = 2816
RMS_EPS = 1e-6
MEM_LEN = 256
X_HEADS = 4
X_HEAD_DIM = 256
A_Q_HEADS = 8
A_GROUP = 4
A_WINDOW = 128
C_HEADS = 16
C_PATTERNS = ((128, 1), (512, 4), (2048, 16))
NEG = -1e30
VMEM_LIMIT = 56 * 2 ** 20

ADAM_LR = 0.001
ADAM_B1 = 0.9
ADAM_B2 = 0.999
ADAM_EPS = 1e-08
ADAM_WD = 0.01
ADAM_STEP = 10

N_DEV = 8
MESH = pl.DeviceIdType.MESH


def _cparams(n):
    return pltpu.CompilerParams(dimension_semantics=("arbitrary",) * n, vmem_limit_bytes=VMEM_LIMIT)


def _dot(a, b):
    return jnp.dot(a, b, preferred_element_type=F32)


def _dot_nt(a, b):
    return lax.dot_general(a, b, (((1,), (1,)), ((), ())), preferred_element_type=F32)


def _dot_tn(a, b):
    return lax.dot_general(a, b, (((0,), (0,)), ((), ())), preferred_element_type=F32)


def _sigmoid(z):
    return 1.0 / (1.0 + jnp.exp(-z))


def norm_matmul(x, g, w, *, tm, tn, split, name):
    T, K = x.shape
    N = w.shape[1]
    nj = N // tn

    def body(x_ref, g_ref, w_ref, o_ref, h_ref):
        @pl.when(pl.program_id(1) == 0)
        def _():
            xv = x_ref[...]
            r = lax.rsqrt(jnp.mean(xv * xv, axis=-1, keepdims=True) + RMS_EPS)
            h_ref[...] = (xv * r * g_ref[...]).astype(BF16)

        o_ref[...] = _dot(h_ref[...], w_ref[...])

    if split:
        njh = nj // 2
        o_shape = jax.ShapeDtypeStruct((2, T, N // 2), F32)
        o_spec = pl.BlockSpec((None, tm, tn), lambda i, j: (j // njh, i, j % njh))
    else:
        o_shape = jax.ShapeDtypeStruct((T, N), F32)
        o_spec = pl.BlockSpec((tm, tn), lambda i, j: (i, j))
    return pl.pallas_call(
        body, grid=(T // tm, nj),
        in_specs=[pl.BlockSpec((tm, K), lambda i, j: (i, 0)),
                  pl.BlockSpec((1, K), lambda i, j: (0, 0)),
                  pl.BlockSpec((K, tn), lambda i, j: (0, j))],
        out_specs=[o_spec, pl.BlockSpec((tm, K), lambda i, j: (i, 0))],
        out_shape=[o_shape, jax.ShapeDtypeStruct((T, K), BF16)],
        compiler_params=_cparams(2), name=name)(x, g, w)


def mm_nn(a, b, *, res, scale, swiglu, tm, tn, tk, name):
    T = a.shape[-2]
    K, N = b.shape
    nk = K // tk

    def body(*refs):
        if swiglu:
            g_ref, u_ref, b_ref, r_ref, o_ref, acc = refs
        else:
            a_ref, b_ref, r_ref, o_ref, acc = refs
        k = pl.program_id(2)

        @pl.when(k == 0)
        def _():
            acc[...] = jnp.zeros_like(acc)

        if swiglu:
            gv = g_ref[...]
            av = (gv * _sigmoid(gv) * u_ref[...]).astype(BF16)
        else:
            av = a_ref[...].astype(BF16)
        acc[...] += _dot(av, b_ref[...])

        @pl.when(k == nk - 1)
        def _():
            o_ref[...] = r_ref[...] + scale * acc[...]

    if swiglu:
        a_specs = [pl.BlockSpec((None, tm, tk), lambda i, j, k: (0, i, k)),
                   pl.BlockSpec((None, tm, tk), lambda i, j, k: (1, i, k))]
        a_args = [a, a]
    else:
        a_specs = [pl.BlockSpec((tm, tk), lambda i, j, k: (i, k))]
        a_args = [a]
    return pl.pallas_call(
        body, grid=(T // tm, N // tn, nk),
        in_specs=a_specs + [pl.BlockSpec((tk, tn), lambda i, j, k: (k, j)),
                            pl.BlockSpec((tm, tn), lambda i, j, k: (i, j))],
        out_specs=pl.BlockSpec((tm, tn), lambda i, j, k: (i, j)),
        out_shape=jax.ShapeDtypeStruct((T, N), F32),
        scratch_shapes=[pltpu.VMEM((tm, tn), F32)],
        compiler_params=_cparams(3), name=name)(*a_args, b, res)


def mm_nt(a, b, *, tm, tn, tk, name):
    T, K = a.shape
    N = b.shape[0]
    nk = K // tk

    def body(a_ref, b_ref, o_ref, acc):
        k = pl.program_id(2)

        @pl.when(k == 0)
        def _():
            acc[...] = jnp.zeros_like(acc)

        acc[...] += _dot_nt(a_ref[...].astype(BF16), b_ref[...])

        @pl.when(k == nk - 1)
        def _():
            o_ref[...] = acc[...]

    return pl.pallas_call(
        body, grid=(T // tm, N // tn, nk),
        in_specs=[pl.BlockSpec((tm, tk), lambda i, j, k: (i, k)),
                  pl.BlockSpec((tn, tk), lambda i, j, k: (j, k))],
        out_specs=pl.BlockSpec((tm, tn), lambda i, j, k: (i, j)),
        out_shape=jax.ShapeDtypeStruct((T, N), F32),
        scratch_shapes=[pltpu.VMEM((tm, tn), F32)],
        compiler_params=_cparams(3), name=name)(a, b)


def ffn_bwd_act(d, wd, gu, *, tm, tn, name):
    T, K = d.shape
    Fd = wd.shape[0]

    def body(d_ref, w_ref, g_ref, u_ref, dgu_ref, act_ref):
        da = 0.5 * _dot_nt(d_ref[...].astype(BF16), w_ref[...])
        gv = g_ref[...]
        uv = u_ref[...]
        s = _sigmoid(gv)
        silu = gv * s
        act_ref[...] = (silu * uv).astype(BF16)
        dgu_ref[0] = (da * uv * (s * (1.0 + gv * (1.0 - s)))).astype(BF16)
        dgu_ref[1] = (da * silu).astype(BF16)

    return pl.pallas_call(
        body, grid=(T // tm, Fd // tn),
        in_specs=[pl.BlockSpec((tm, K), lambda i, j: (i, 0)),
                  pl.BlockSpec((tn, K), lambda i, j: (j, 0)),
                  pl.BlockSpec((None, tm, tn), lambda i, j: (0, i, j)),
                  pl.BlockSpec((None, tm, tn), lambda i, j: (1, i, j))],
        out_specs=[pl.BlockSpec((2, tm, tn), lambda i, j: (0, i, j)),
                   pl.BlockSpec((tm, tn), lambda i, j: (i, j))],
        out_shape=[jax.ShapeDtypeStruct((2, T, Fd), BF16), jax.ShapeDtypeStruct((T, Fd), BF16)],
        compiler_params=_cparams(2), name=name)(d, wd, gu, gu)


def mm_nt_normbwd(a, b, x, g, res, *, a_split, tm, tk, name):
    T, Dm = x.shape
    K = b.shape[1]
    nk = K // tk
    nkh = nk // 2
    has_res = res is not None

    def body(*refs):
        if has_res:
            a_ref, b_ref, x_ref, g_ref, r_ref, dx_ref, dg_ref, acc = refs
        else:
            a_ref, b_ref, x_ref, g_ref, dx_ref, dg_ref, acc = refs
        i = pl.program_id(0)
        k = pl.program_id(1)

        @pl.when(k == 0)
        def _():
            acc[...] = jnp.zeros_like(acc)

        acc[...] += _dot_nt(a_ref[...].astype(BF16), b_ref[...])

        @pl.when(k == nk - 1)
        def _():
            xv = x_ref[...]
            r = lax.rsqrt(jnp.mean(xv * xv, axis=-1, keepdims=True) + RMS_EPS)
            xh = xv * r
            dh = acc[...]
            dxh = dh * g_ref[...]
            dx = r * (dxh - xh * jnp.mean(dxh * xh, axis=-1, keepdims=True))
            if has_res:
                dx = dx + r_ref[...]
            dx_ref[...] = dx
            part = jnp.sum(dh * xh, axis=0, keepdims=True)

            @pl.when(i == 0)
            def _():
                dg_ref[...] = part

            @pl.when(i > 0)
            def _():
                dg_ref[...] += part

    if a_split:
        a_spec = pl.BlockSpec((None, tm, tk), lambda i, k: (k // nkh, i, k % nkh))
    else:
        a_spec = pl.BlockSpec((tm, tk), lambda i, k: (i, k))
    in_specs = [a_spec,
                pl.BlockSpec((Dm, tk), lambda i, k: (0, k)),
                pl.BlockSpec((tm, Dm), lambda i, k: (i, 0)),
                pl.BlockSpec((1, Dm), lambda i, k: (0, 0))]
    args = [a, b, x, g]
    if has_res:
        in_specs.append(pl.BlockSpec((tm, Dm), lambda i, k: (i, 0)))
        args.append(res)
    return pl.pallas_call(
        body, grid=(T // tm, nk), in_specs=in_specs,
        out_specs=[pl.BlockSpec((tm, Dm), lambda i, k: (i, 0)),
                   pl.BlockSpec((1, Dm), lambda i, k: (0, 0))],
        out_shape=[jax.ShapeDtypeStruct((T, Dm), F32), jax.ShapeDtypeStruct((1, Dm), F32)],
        scratch_shapes=[pltpu.VMEM((tm, Dm), F32)],
        compiler_params=_cparams(2), name=name)(*args)


def mm_tn(a, b, *, scale, a_split, b_split, tm, tn, tk, name):
    T = a.shape[-2]
    M = a.shape[-1] * (2 if a_split else 1)
    N = b.shape[-1] * (2 if b_split else 1)
    ni, nj, nk = M // tm, N // tn, T // tk
    nih, njh = ni // 2, nj // 2

    def body(a_ref, b_ref, o_ref, acc):
        k = pl.program_id(2)

        @pl.when(k == 0)
        def _():
            acc[...] = jnp.zeros_like(acc)

        acc[...] += _dot_tn(a_ref[...].astype(BF16), b_ref[...].astype(BF16))

        @pl.when(k == nk - 1)
        def _():
            o_ref[...] = (acc[...] * scale).astype(o_ref.dtype)

    if a_split:
        a_spec = pl.BlockSpec((None, tk, tm), lambda i, j, k: (i // nih, k, i % nih))
    else:
        a_spec = pl.BlockSpec((tk, tm), lambda i, j, k: (k, i))
    if b_split:
        b_spec = pl.BlockSpec((None, tk, tn), lambda i, j, k: (j // njh, k, j % njh))
    else:
        b_spec = pl.BlockSpec((tk, tn), lambda i, j, k: (k, j))
    return pl.pallas_call(
        body, grid=(ni, nj, nk), in_specs=[a_spec, b_spec],
        out_specs=pl.BlockSpec((tm, tn), lambda i, j, k: (i, j)),
        out_shape=jax.ShapeDtypeStruct((M, N), BF16),
        scratch_shapes=[pltpu.VMEM((tm, tn), F32)],
        compiler_params=_cparams(3), name=name)(a, b)


def loss_kernel(y, target, *, tm, name):
    T, Dm = y.shape

    def body(y_ref, t_ref, l_ref, dy_ref):
        e = y_ref[...] - t_ref[...]
        dy_ref[...] = e * (1.0 / Dm)
        part = (0.5 / Dm) * jnp.sum(jnp.sum(e * e, axis=-1, keepdims=True), axis=0, keepdims=True)
        part = jnp.broadcast_to(part, (8, LANES))

        @pl.when(pl.program_id(0) == 0)
        def _():
            l_ref[...] = part

        @pl.when(pl.program_id(0) > 0)
        def _():
            l_ref[...] += part

    return pl.pallas_call(
        body, grid=(T // tm,),
        in_specs=[pl.BlockSpec((tm, Dm), lambda i: (i, 0)), pl.BlockSpec((tm, Dm), lambda i: (i, 0))],
        out_specs=[pl.BlockSpec((8, LANES), lambda i: (0, 0)), pl.BlockSpec((tm, Dm), lambda i: (i, 0))],
        out_shape=[jax.ShapeDtypeStruct((8, LANES), F32), jax.ShapeDtypeStruct((T, Dm), F32)],
        compiler_params=_cparams(1), name=name)(y, target)


def adamw(w, g, m, v, *, br, name):
    R, C = w.shape

    def body(w_ref, g_ref, m_ref, v_ref, d_ref, nm_ref, nv_ref):
        gv = g_ref[...]
        nm = ADAM_B1 * m_ref[...] + (1.0 - ADAM_B1) * gv
        nv = ADAM_B2 * v_ref[...] + (1.0 - ADAM_B2) * (gv * gv)
        m_hat = nm / (1.0 - ADAM_B1 ** ADAM_STEP)
        v_hat = nv / (1.0 - ADAM_B2 ** ADAM_STEP)
        d_ref[...] = -ADAM_LR * (m_hat / (jnp.sqrt(v_hat) + ADAM_EPS) + ADAM_WD * w_ref[...])
        nm_ref[...] = nm
        nv_ref[...] = nv

    spec = pl.BlockSpec((br, C), lambda i: (i, 0))
    shp = jax.ShapeDtypeStruct((R, C), F32)
    return pl.pallas_call(
        body, grid=(R // br,), in_specs=[spec] * 4, out_specs=[spec] * 3, out_shape=[shp] * 3,
        compiler_params=_cparams(1), name=name)(w, g, m, v)


def _lane0():
    return lax.broadcasted_iota(jnp.int32, (1, LANES), 1) < HEAD_DIM


def _half_sum(x, m0):
    s0 = jnp.sum(jnp.where(m0, x, 0.0), axis=-1, keepdims=True)
    s1 = jnp.sum(jnp.where(m0, 0.0, x), axis=-1, keepdims=True)
    return jnp.where(m0, s0, s1)


def _half_pick(x, m0, e):
    sel = m0 if e == 0 else jnp.logical_not(m0)
    return jnp.max(jnp.where(sel, x, NEG), axis=-1, keepdims=True)


def _head_rms(x, m0):
    return lax.rsqrt(_half_sum(x * x, m0) * (1.0 / HEAD_DIM) + RMS_EPS)


def _alibi(n):
    return [float(2.0 ** (-8.0 * (h + 1) / n)) for h in range(n)]


def _mask_half(x, m0, e):
    return jnp.where(m0, x, 0.0) if e == 0 else jnp.where(m0, 0.0, x)


def _band_masks2(max_dist, has_prev, live):
    row = lax.broadcasted_iota(jnp.int32, (2 * BLK, 2 * BLK), 0)
    col = lax.broadcasted_iota(jnp.int32, (2 * BLK, 2 * BLK), 1)
    dist = (row & (BLK - 1)) - col + BLK
    lim = jnp.where(live, max_dist, -1)
    first = jnp.where(has_prev, 0, BLK)
    valid = (dist >= 0) & (dist <= lim) & (col >= first)
    top = lax.broadcasted_iota(jnp.int32, (2 * BLK, 1), 0) < BLK
    return dist.astype(F32), valid, top


def _stack_heads(x, m0, kes):
    parts = []
    for e in range(2):
        h = _mask_half(x, m0, e)
        parts.append(pltpu.roll(h, HEAD_DIM, 1) if kes[e] != e else h)
    return jnp.concatenate(parts, axis=0)


def _unstack_heads(y, m0, kes):
    parts = []
    for e in range(2):
        h = y[e * BLK:(e + 1) * BLK]
        parts.append(pltpu.roll(h, HEAD_DIM, 1) if kes[e] != e else h)
    return jnp.where(m0, parts[0], parts[1])


def _rows(r, dil):
    return pl.ds(r, BLK, stride=dil) if dil > 1 else pl.ds(0, BLK)


def _band_specs(dil, ppk, q_blk, k_blk, v_blk, kv_shared, nb):
    RB = BLK * dil
    qw = LANES * ppk
    kw = LANES if kv_shared else qw

    def cur(i):
        return jnp.minimum(i, nb - 1)

    def kidx(base):
        return (lambda p, i: (cur(i), base)) if kv_shared else (lambda p, i: (cur(i), base + p))

    def pidx(base):
        return ((lambda p, i: (jnp.maximum(i - 1, 0), base)) if kv_shared
                else (lambda p, i: (jnp.maximum(i - 1, 0), base + p)))

    return [pl.BlockSpec((RB, qw), lambda p, i: (cur(i), q_blk + p)),
            pl.BlockSpec((RB, kw), kidx(k_blk)), pl.BlockSpec((RB, kw), pidx(k_blk)),
            pl.BlockSpec((RB, kw), kidx(v_blk)), pl.BlockSpec((RB, kw), pidx(v_blk))]


def banded_fwd(qkv, q_gain2, k_gain2, slopes, sinks, *, dil, ppk, q_blk, k_blk, v_blk, n_heads, group,
               max_dist, name):
    T = qkv.shape[0]
    RB = BLK * dil
    nb = T // RB
    npair = n_heads // 2
    kv_shared = group > 1
    scale = HEAD_DIM ** -0.5
    has_sink = sinks is not None

    def body(*refs):
        slope_ref = refs[0]
        if has_sink:
            sink_ref, refs = refs[1], refs[2:]
        else:
            refs = refs[1:]
        q_ref, kc_ref, kp_ref, vc_ref, vp_ref, qg_ref, kg_ref, o_ref, l_ref = refs
        pb = pl.program_id(0)
        i = pl.program_id(1)
        m0 = _lane0()
        distf, valid, top = _band_masks2(max_dist, i > 0, i >= 0)
        qg = qg_ref[...]
        kg = kg_ref[...]
        for r in range(dil):
            rows = _rows(r, dil)
            kcache = {}
            for jp in range(ppk):
                cs = pl.ds(LANES * jp, LANES)
                jk = 0 if kv_shared else jp
                if jk not in kcache:
                    ks = pl.ds(LANES * jk, LANES)
                    kcat = jnp.concatenate([kp_ref[rows, ks], kc_ref[rows, ks]], axis=0)
                    vcat = jnp.concatenate([vp_ref[rows, ks], vc_ref[rows, ks]], axis=0)
                    kcache[jk] = ((kcat * _head_rms(kcat, m0) * kg).astype(BF16), vcat.astype(BF16))
                kn, vcat = kcache[jk]
                qv = q_ref[rows, cs]
                qn = qv * _head_rms(qv, m0) * qg
                kes = [((2 * jp + e) // group) % 2 if kv_shared else e for e in range(2)]
                hidx = 2 * (pb * ppk + jp)
                qs = _stack_heads(qn, m0, kes).astype(BF16)
                slope = jnp.where(top, slope_ref[hidx], slope_ref[hidx + 1])
                s = jnp.where(valid, _dot_nt(qs, kn) * scale - slope * distf, NEG)
                m = jnp.max(s, axis=-1, keepdims=True)
                if has_sink:
                    sk = jnp.where(top, sink_ref[hidx], sink_ref[hidx + 1])
                    m = jnp.maximum(m, sk)
                p = jnp.exp(s - m)
                den = jnp.sum(p, axis=-1, keepdims=True)
                if has_sink:
                    den = den + jnp.exp(sk - m)
                o_full = _dot((p * (1.0 / den)).astype(BF16), vcat)
                o_ref[rows, cs] = _unstack_heads(o_full, m0, kes)
                l_ref[rows, cs] = _unstack_heads(jnp.broadcast_to(m + jnp.log(den), (2 * BLK, LANES)), m0, [0, 1])

    smem = pl.BlockSpec(memory_space=pltpu.SMEM)
    qw = LANES * ppk
    gspec = pl.BlockSpec((1, LANES), lambda p, i: (0, 0))
    ospec = pl.BlockSpec((RB, qw), lambda p, i: (i, p))
    oshape = jax.ShapeDtypeStruct((T, n_heads * HEAD_DIM), F32)
    args = [slopes] + ([sinks] if has_sink else []) + [qkv] * 5 + [q_gain2, k_gain2]
    return pl.pallas_call(
        body, grid=(npair // ppk, nb),
        in_specs=[smem] * (2 if has_sink else 1) + _band_specs(dil, ppk, q_blk, k_blk, v_blk, kv_shared, nb)
        + [gspec, gspec],
        out_specs=[ospec, ospec], out_shape=[oshape, oshape],
        compiler_params=_cparams(2), name=name)(*args)


def banded_bwd(qkv, q_gain2, k_gain2, slopes, sinks, do, o, lse, w, omix, *, dil, ppk, q_blk, k_blk, v_blk,
               n_heads, group, max_dist, do_blk, name):
    T = qkv.shape[0]
    RB = BLK * dil
    nb = T // RB
    npair = n_heads // 2
    kv_shared = group > 1
    scale = HEAD_DIM ** -0.5
    has_sink = sinks is not None
    mixed = w is not None
    qw = LANES * ppk

    def body(*refs):
        slope_ref = refs[0]
        if has_sink:
            sink_ref, refs = refs[1], refs[2:]
        else:
            refs = refs[1:]
        q_ref, kc_ref, kp_ref, vc_ref, vp_ref, qg_ref, kg_ref, do_ref, o_ref, l_ref = refs[:10]
        refs = refs[10:]
        if mixed:
            w_ref, om_ref, refs = refs[0], refs[1], refs[2:]
        dq_ref, dk_ref, dv_ref, dqg_ref, dkg_ref, dsk_ref, ck_ref, cv_ref = refs
        pb = pl.program_id(0)
        i = pl.program_id(1)
        live = i < nb
        m0 = _lane0()
        lane = lax.broadcasted_iota(jnp.int32, (1, LANES), 1)
        distf, valid, top = _band_masks2(max_dist, i > 0, live)
        livef = live.astype(F32)
        qg = qg_ref[...]
        kg = kg_ref[...]

        def stack_rows(x2):
            return jnp.concatenate([_half_pick(x2, m0, 0), _half_pick(x2, m0, 1)], axis=0)

        @pl.when((pb == 0) & (i == 0))
        def _():
            dqg_ref[...] = jnp.zeros_like(dqg_ref)
            dkg_ref[...] = jnp.zeros_like(dkg_ref)
            dsk_ref[...] = jnp.zeros_like(dsk_ref)

        @pl.when(i == 0)
        def _():
            ck_ref[...] = jnp.zeros_like(ck_ref)
            cv_ref[...] = jnp.zeros_like(cv_ref)

        dqg_acc = jnp.zeros((1, LANES), F32)
        dkg_acc = jnp.zeros((1, LANES), F32)
        dsk_acc = jnp.zeros((1, LANES), F32)
        for r in range(dil):
            rows = _rows(r, dil)
            for jp in range(ppk):
                cs = pl.ds(LANES * jp, LANES)
                ks = pl.ds(0, LANES) if kv_shared else cs
                kcat = jnp.concatenate([kp_ref[rows, ks], kc_ref[rows, ks]], axis=0)
                vcat = jnp.concatenate([vp_ref[rows, ks], vc_ref[rows, ks]], axis=0).astype(BF16)
                rk = _head_rms(kcat, m0)
                kh = kcat * rk
                kn = (kh * kg).astype(BF16)
                qv = q_ref[rows, cs]
                rq = _head_rms(qv, m0)
                qh = qv * rq
                dov = do_ref[rows, cs]
                lv = l_ref[rows, cs]
                if mixed:
                    wv = w_ref[rows, cs]
                    dmix = _half_sum(dov * om_ref[rows, cs], m0)
                    dov = dov * wv
                delta2 = _half_sum(dov * o_ref[rows, cs], m0)
                shift = stack_rows(wv * dmix if mixed else delta2)
                kes = [((2 * jp + e) // group) % 2 if kv_shared else e for e in range(2)]
                hidx = 2 * (pb * ppk + jp)
                qs = _stack_heads(qh * qg, m0, kes).astype(BF16)
                dos = _stack_heads(dov, m0, kes).astype(BF16)
                lse = stack_rows(lv)
                slope = jnp.where(top, slope_ref[hidx], slope_ref[hidx + 1])
                p = jnp.where(valid, jnp.exp(_dot_nt(qs, kn) * scale - slope * distf - lse), 0.0)
                ds = (p * (_dot_nt(dos, vcat) - shift)).astype(BF16)
                dqn = _unstack_heads(_dot(ds, kn), m0, kes) * scale
                dkn = _dot_tn(ds, qs) * scale
                dvv = _dot_tn(p.astype(BF16), dos)
                if has_sink:
                    sk = jnp.where(top, sink_ref[hidx], sink_ref[hidx + 1])
                    contrib = -jnp.exp(sk - lse) * stack_rows(delta2) * livef
                    for e in range(2):
                        tot = jnp.sum(contrib[e * BLK:(e + 1) * BLK], axis=0, keepdims=True)
                        dsk_acc = dsk_acc + jnp.where(lane == (2 * jp + e), tot, 0.0)
                dqg_acc = dqg_acc + jnp.sum(dqn * qh, axis=0, keepdims=True)
                dqh = dqn * qg
                dq_raw = rq * (dqh - qh * (_half_sum(dqh * qh, m0) * (1.0 / HEAD_DIM)))
                dkg_acc = dkg_acc + jnp.sum(dkn * kh, axis=0, keepdims=True)
                dkh = dkn * kg
                dk_raw = rk * (dkh - kh * (_half_sum(dkh * kh, m0) * (1.0 / HEAD_DIM)))

                @pl.when(live)
                def _():
                    dq_ref[rows, cs] = dq_raw

                dk_ref[rows, cs] = ck_ref[rows, cs] + dk_raw[:BLK]
                dv_ref[rows, cs] = cv_ref[rows, cs] + dvv[:BLK]
                ck_ref[rows, cs] = dk_raw[BLK:]
                cv_ref[rows, cs] = dvv[BLK:]
        dqg_ref[...] += dqg_acc
        dkg_ref[...] += dkg_acc
        dsk_ref[...] += dsk_acc

    smem = pl.BlockSpec(memory_space=pltpu.SMEM)
    gspec = pl.BlockSpec((1, LANES), lambda p, i: (0, 0))

    def cur(i):
        return jnp.minimum(i, nb - 1)

    qspec = pl.BlockSpec((RB, qw), lambda p, i: (cur(i), p))
    dospec = pl.BlockSpec((RB, qw), lambda p, i: (cur(i), do_blk + p))
    kvout = pl.BlockSpec((RB, qw), lambda p, i: (jnp.maximum(i - 1, 0), p))
    in_specs = ([smem] * (2 if has_sink else 1) + _band_specs(dil, ppk, q_blk, k_blk, v_blk, kv_shared, nb)
                + [gspec, gspec, dospec, qspec, qspec] + ([qspec, qspec] if mixed else []))
    args = ([slopes] + ([sinks] if has_sink else []) + [qkv] * 5 + [q_gain2, k_gain2, do, o, lse]
            + ([w, omix] if mixed else []))
    full = jax.ShapeDtypeStruct((T, n_heads * HEAD_DIM), F32)
    row = jax.ShapeDtypeStruct((1, LANES), F32)
    return pl.pallas_call(
        body, grid=(npair // ppk, nb + 1), in_specs=in_specs,
        out_specs=[qspec, kvout, kvout, gspec, gspec, gspec],
        out_shape=[full, full, full, row, row, row],
        scratch_shapes=[pltpu.VMEM((RB, qw), F32), pltpu.VMEM((RB, qw), F32)],
        compiler_params=_cparams(2), name=name)(*args)


def mix_fwd(o1, o2, o3, l1, l2, l3, *, tm, name):
    T, C = o1.shape

    def body(o1r, o2r, o3r, l1r, l2r, l3r, o_ref, w1r, w2r, w3r):
        a, b, c = l1r[...], l2r[...], l3r[...]
        m = jnp.maximum(jnp.maximum(a, b), c)
        ea, eb, ec = jnp.exp(a - m), jnp.exp(b - m), jnp.exp(c - m)
        inv = 1.0 / (ea + eb + ec)
        wa, wb, wc = ea * inv, eb * inv, ec * inv
        o_ref[...] = wa * o1r[...] + wb * o2r[...] + wc * o3r[...]
        w1r[...] = wa
        w2r[...] = wb
        w3r[...] = wc

    spec = pl.BlockSpec((tm, C), lambda i: (i, 0))
    shp = jax.ShapeDtypeStruct((T, C), F32)
    return pl.pallas_call(body, grid=(T // tm,), in_specs=[spec] * 6, out_specs=[spec] * 4, out_shape=[shp] * 4,
                          compiler_params=_cparams(1), name=name)(o1, o2, o3, l1, l2, l3)


def assemble_odd(parts, *, tm, name):
    T, C = parts[0][0].shape

    def body(*refs):
        o_ref = refs[9]
        for j in range(3):
            o_ref[:, pl.ds(C * j, C)] = refs[j][...] + refs[3 + j][...] + refs[6 + j][...]

    spec = pl.BlockSpec((tm, C), lambda i: (i, 0))
    flat = [parts[p][j] for p in range(3) for j in range(3)]
    return pl.pallas_call(body, grid=(T // tm,), in_specs=[spec] * 9,
                          out_specs=pl.BlockSpec((tm, 3 * C), lambda i: (i, 0)),
                          out_shape=jax.ShapeDtypeStruct((T, 3 * C), F32),
                          compiler_params=_cparams(1), name=name)(*flat)


def assemble_even(dqa, dka4, dva4, dqb, dkb, dvb, *, tm, name):
    T = dqa.shape[0]
    W = 512

    def body(dqa_r, dka_r, dva_r, dqb_r, dkb_r, dvb_r, o_ref):
        o_ref[:, pl.ds(0, W)] = dqa_r[...]
        ka = dka_r[...]
        va = dva_r[...]
        o_ref[:, pl.ds(512, LANES)] = ka[:, 0:128] + ka[:, 128:256] + ka[:, 256:384] + ka[:, 384:512]
        o_ref[:, pl.ds(640, LANES)] = va[:, 0:128] + va[:, 128:256] + va[:, 256:384] + va[:, 384:512]
        o_ref[:, pl.ds(768, W)] = dqb_r[...]
        o_ref[:, pl.ds(1280, W)] = dkb_r[...]
        o_ref[:, pl.ds(1792, W)] = dvb_r[...]

    spec = pl.BlockSpec((tm, W), lambda i: (i, 0))
    return pl.pallas_call(body, grid=(T // tm,), in_specs=[spec] * 6,
                          out_specs=pl.BlockSpec((tm, 2304), lambda i: (i, 0)),
                          out_shape=jax.ShapeDtypeStruct((T, 2304), F32),
                          compiler_params=_cparams(1), name=name)(dqa, dka4, dva4, dqb, dkb, dvb)


STICK_T = 256
STICK_DEAD = -110.0


def _split_bf16(x):
    hi = x.astype(BF16)
    lo = (x - hi.astype(F32)).astype(BF16)
    return hi, lo


def _stick_logits(qm, kt, scale, diag):
    n = STICK_T
    row = lax.broadcasted_iota(jnp.int32, (n, n), 0)
    col = lax.broadcasted_iota(jnp.int32, (n, n), 1)
    mask = col < row + jnp.where(diag, 0, n)
    z = _dot_nt(qm, kt) * scale
    lneg = -(jnp.maximum(z, 0.0) + jnp.log(1.0 + jnp.exp(-jnp.abs(z))))
    lpos = z + lneg
    lk = jnp.where(mask, lneg, 0.0)
    return mask, lpos, lneg, lk


def _cumsum_mm(x, tri):
    hi, lo = _split_bf16(x)
    return _dot(hi, tri) + _dot(lo, tri)


def stick_fwd(qkv, *, q_blk, k_blk, v_blk, n_pairs, name):
    T = qkv.shape[0]
    n = STICK_T
    nq = T // n
    scale = HEAD_DIM ** -0.5

    def body(q_ref, k_ref, v_ref, o_ref):
        i = pl.program_id(1)
        m0 = _lane0()
        r2 = lax.broadcasted_iota(jnp.int32, (n, n), 0)
        c2 = lax.broadcasted_iota(jnp.int32, (n, n), 1)
        tri_after = (r2 > c2).astype(BF16)
        qv = q_ref[...]
        out = jnp.zeros((n, LANES), F32)
        for e in range(2):
            qm = _mask_half(qv, m0, e).astype(BF16)

            def alive(st):
                t, _, carry = st
                return (t <= i) & (jnp.max(carry) > STICK_DEAD)

            def step(st, e=e, qm=qm):
                t, acc, carry = st
                start = pl.multiple_of((i - t) * n, n)
                kt = k_ref[pl.ds(start, n), :].astype(BF16)
                vt = _mask_half(v_ref[pl.ds(start, n), :], m0, e).astype(BF16)
                mask, lpos, _, lk = _stick_logits(qm, kt, scale, t == 0)
                after = _cumsum_mm(lk, tri_after) + carry
                a = jnp.where(mask, jnp.exp(lpos + after), 0.0)
                acc = acc + _dot(a.astype(BF16), vt)
                carry = carry + jnp.sum(lk, axis=-1, keepdims=True)
                return t + 1, acc, carry

            _, acc, _ = lax.while_loop(alive, step, (jnp.int32(0), jnp.zeros((n, LANES), F32),
                                                     jnp.zeros((n, 1), F32)))
            out = out + acc
        o_ref[...] = out

    return pl.pallas_call(
        body, grid=(n_pairs, nq),
        in_specs=[pl.BlockSpec((n, LANES), lambda p, i: (i, q_blk + p)),
                  pl.BlockSpec((T, LANES), lambda p, i: (0, k_blk + p)),
                  pl.BlockSpec((T, LANES), lambda p, i: (0, v_blk + p))],
        out_specs=pl.BlockSpec((n, LANES), lambda p, i: (i, p)),
        out_shape=jax.ShapeDtypeStruct((T, n_pairs * LANES), F32),
        compiler_params=_cparams(2), name=name)(qkv, qkv, qkv)


def stick_bwd(qkv, do, *, q_blk, k_blk, v_blk, do_blk, n_pairs, name):
    T = qkv.shape[0]
    n = STICK_T
    nq = T // n
    scale = HEAD_DIM ** -0.5

    def body(q_ref, k_ref, v_ref, do_ref, dq_ref, dk_ref, dv_ref):
        i = pl.program_id(1)
        m0 = _lane0()
        r2 = lax.broadcasted_iota(jnp.int32, (n, n), 0)
        c2 = lax.broadcasted_iota(jnp.int32, (n, n), 1)
        tri_after = (r2 > c2).astype(BF16)
        tri_from = (r2 >= c2).astype(BF16)

        @pl.when(i == 0)
        def _():
            dk_ref[...] = jnp.zeros_like(dk_ref)
            dv_ref[...] = jnp.zeros_like(dv_ref)

        qv = q_ref[...]
        dov = do_ref[...]
        dq_out = jnp.zeros((n, LANES), F32)
        for e in range(2):
            qm = _mask_half(qv, m0, e).astype(BF16)
            dom = _mask_half(dov, m0, e).astype(BF16)

            def tile(t, carry, qm=qm, dom=dom):
                start = pl.multiple_of((i - t) * n, n)
                kt_f = k_ref[pl.ds(start, n), :]
                vt = v_ref[pl.ds(start, n), :].astype(BF16)
                mask, lpos, lneg, lk = _stick_logits(qm, kt_f.astype(BF16), scale, t == 0)
                a = jnp.where(mask, jnp.exp(lpos + _cumsum_mm(lk, tri_after) + carry), 0.0)
                g = _dot_nt(dom, vt) * a
                return start, kt_f, mask, lpos, lneg, lk, a, g

            def alive(st):
                t, carry, _ = st
                return (t <= i) & (jnp.max(carry) > STICK_DEAD)

            def scan(st):
                t, carry, gtot = st
                _, _, _, _, _, lk, _, g = tile(t, carry)
                return (t + 1, carry + jnp.sum(lk, axis=-1, keepdims=True),
                        gtot + jnp.sum(g, axis=-1, keepdims=True))

            z1 = jnp.zeros((n, 1), F32)
            n_live, _, gtot = lax.while_loop(alive, scan, (jnp.int32(0), z1, z1))

            def step(t, st, e=e, qm=qm, dom=dom, gtot=gtot):
                dq_acc, carry, gright = st
                start, kt_f, mask, lpos, lneg, lk, a, g = tile(t, carry)
                before = gtot - (_cumsum_mm(g, tri_from) + gright)
                dz = jnp.where(mask, g * jnp.exp(lneg) - before * jnp.exp(lpos), 0.0) * scale
                dzb = dz.astype(BF16)
                dq_acc = dq_acc + _dot(dzb, _mask_half(kt_f, m0, e).astype(BF16))
                dk_ref[pl.ds(start, n), :] += _dot_tn(dzb, qm)
                dv_ref[pl.ds(start, n), :] += _dot_tn(a.astype(BF16), dom)
                return (dq_acc, carry + jnp.sum(lk, axis=-1, keepdims=True),
                        gright + jnp.sum(g, axis=-1, keepdims=True))

            dq_acc, _, _ = lax.fori_loop(0, n_live, step, (jnp.zeros((n, LANES), F32), z1, z1))
            dq_out = dq_out + dq_acc
        dq_ref[...] = dq_out

    tile = pl.BlockSpec((n, LANES), lambda p, i: (i, p))
    whole = pl.BlockSpec((T, LANES), lambda p, i: (0, p))
    shp = jax.ShapeDtypeStruct((T, n_pairs * LANES), F32)
    return pl.pallas_call(
        body, grid=(n_pairs, nq),
        in_specs=[pl.BlockSpec((n, LANES), lambda p, i: (i, q_blk + p)),
                  pl.BlockSpec((T, LANES), lambda p, i: (0, k_blk + p)),
                  pl.BlockSpec((T, LANES), lambda p, i: (0, v_blk + p)),
                  pl.BlockSpec((n, LANES), lambda p, i: (i, do_blk + p))],
        out_specs=[tile, whole, whole], out_shape=[shp, shp, shp],
        compiler_params=_cparams(2), name=name)(qkv, qkv, qkv, do)


def _xnorm(x):
    r = lax.rsqrt(jnp.mean(x * x, axis=-1, keepdims=True) + RMS_EPS)
    return r, x * r


def xattn_fwd(qraw, kvraw, q_gain, k_gain, *, tm, name):
    T = qraw.shape[0]
    scale = X_HEAD_DIM ** -0.5
    W = X_HEADS * X_HEAD_DIM

    def body(q_ref, kv_ref, qg_ref, kg_ref, o_ref):
        for h in range(X_HEADS):
            cs = pl.ds(X_HEAD_DIM * h, X_HEAD_DIM)
            _, qh = _xnorm(q_ref[:, cs])
            _, kh = _xnorm(kv_ref[:, cs])
            qn = (qh * qg_ref[...]).astype(BF16)
            kn = (kh * kg_ref[...]).astype(BF16)
            v = kv_ref[:, pl.ds(W + X_HEAD_DIM * h, X_HEAD_DIM)].astype(BF16)
            s = _dot_nt(qn, kn) * scale
            m = jnp.max(s, axis=-1, keepdims=True)
            p = jnp.exp(s - m)
            p = p / jnp.sum(p, axis=-1, keepdims=True)
            o_ref[:, cs] = _dot(p.astype(BF16), v)

    gspec = pl.BlockSpec((1, X_HEAD_DIM), lambda i: (0, 0))
    return pl.pallas_call(
        body, grid=(T // tm,),
        in_specs=[pl.BlockSpec((tm, W), lambda i: (i, 0)), pl.BlockSpec((MEM_LEN, 2 * W), lambda i: (0, 0)),
                  gspec, gspec],
        out_specs=pl.BlockSpec((tm, W), lambda i: (i, 0)),
        out_shape=jax.ShapeDtypeStruct((T, W), F32),
        compiler_params=_cparams(1), name=name)(qraw, kvraw, q_gain, k_gain)


def xattn_bwd(qraw, kvraw, q_gain, k_gain, do, o, *, tm, name):
    T = qraw.shape[0]
    nt = T // tm
    scale = X_HEAD_DIM ** -0.5
    W = X_HEADS * X_HEAD_DIM

    def body(q_ref, kv_ref, qg_ref, kg_ref, do_ref, o_ref, dq_ref, dkv_ref, dqg_ref, dkg_ref, dkn_ref):
        i = pl.program_id(0)

        @pl.when(i == 0)
        def _():
            dkv_ref[...] = jnp.zeros_like(dkv_ref)
            dkn_ref[...] = jnp.zeros_like(dkn_ref)
            dqg_ref[...] = jnp.zeros_like(dqg_ref)
            dkg_ref[...] = jnp.zeros_like(dkg_ref)

        qg = qg_ref[...]
        kg = kg_ref[...]
        dqg_acc = jnp.zeros((1, X_HEAD_DIM), F32)
        for h in range(X_HEADS):
            cs = pl.ds(X_HEAD_DIM * h, X_HEAD_DIM)
            vs = pl.ds(W + X_HEAD_DIM * h, X_HEAD_DIM)
            rq, qh = _xnorm(q_ref[:, cs])
            _, kh = _xnorm(kv_ref[:, cs])
            qn = (qh * qg).astype(BF16)
            kn = (kh * kg).astype(BF16)
            v = kv_ref[:, vs].astype(BF16)
            s = _dot_nt(qn, kn) * scale
            m = jnp.max(s, axis=-1, keepdims=True)
            p = jnp.exp(s - m)
            p = p / jnp.sum(p, axis=-1, keepdims=True)
            dov = do_ref[:, cs]
            delta = jnp.sum(dov * o_ref[:, cs], axis=-1, keepdims=True)
            dob = dov.astype(BF16)
            ds = (p * (_dot_nt(dob, v) - delta)).astype(BF16)
            dqn = _dot(ds, kn) * scale
            dkn_ref[:, cs] += _dot_tn(ds, qn) * scale
            dkv_ref[:, vs] += _dot_tn(p.astype(BF16), dob)
            dqg_acc = dqg_acc + jnp.sum(dqn * qh, axis=0, keepdims=True)
            dqh = dqn * qg
            dq_ref[:, cs] = rq * (dqh - qh * jnp.mean(dqh * qh, axis=-1, keepdims=True))
        dqg_ref[...] += dqg_acc

        @pl.when(i == nt - 1)
        def _():
            dkg_acc = jnp.zeros((1, X_HEAD_DIM), F32)
            for h in range(X_HEADS):
                cs = pl.ds(X_HEAD_DIM * h, X_HEAD_DIM)
                rk, kh = _xnorm(kv_ref[:, cs])
                dkn = dkn_ref[:, cs]
                dkg_acc = dkg_acc + jnp.sum(dkn * kh, axis=0, keepdims=True)
                dkh = dkn * kg
                dkv_ref[:, cs] = rk * (dkh - kh * jnp.mean(dkh * kh, axis=-1, keepdims=True))
            dkg_ref[...] = dkg_acc

    gspec = pl.BlockSpec((1, X_HEAD_DIM), lambda i: (0, 0))
    tile = pl.BlockSpec((tm, W), lambda i: (i, 0))
    kvspec = pl.BlockSpec((MEM_LEN, 2 * W), lambda i: (0, 0))
    grow = jax.ShapeDtypeStruct((1, X_HEAD_DIM), F32)
    return pl.pallas_call(
        body, grid=(nt,), in_specs=[tile, kvspec, gspec, gspec, tile, tile],
        out_specs=[tile, kvspec, gspec, gspec],
        out_shape=[jax.ShapeDtypeStruct((T, W), F32), jax.ShapeDtypeStruct((MEM_LEN, 2 * W), F32), grow, grow],
        scratch_shapes=[pltpu.VMEM((MEM_LEN, W), F32)],
        compiler_params=_cparams(1), name=name)(qraw, kvraw, q_gain, k_gain, do, o)


_ANY = pl.BlockSpec(memory_space=pl.ANY)


def _my_pos():
    return lax.axis_index("x"), lax.axis_index("y"), lax.axis_index("c")


N_CHUNK = 8


def _chunk(ref, ch, rc):
    return ref.at[pl.ds(ch * rc, rc)]


def gather_blocks(blk, *, name):
    R, C = blk.shape
    rc = R // N_CHUNK

    def body(x_ref, out_ref, send_sems, recv_sems, local_sems):
        x, y, c = _my_pos()
        me, sibling = (x, y, c), (x, y, 1 - c)
        chips = [(1 - x, y), (x, 1 - y), (1 - x, 1 - y)]

        def slot(block, ch):
            px, py, pc = block
            return out_ref.at[4 * px + 2 * py + pc, pl.ds(ch * rc, rc)]

        def copy(k, ch, block, to, from_input=False):
            return pltpu.make_async_remote_copy(
                src_ref=_chunk(x_ref, ch, rc) if from_input else slot(block, ch), dst_ref=slot(block, ch),
                send_sem=send_sems.at[k * N_CHUNK + ch], recv_sem=recv_sems.at[k * N_CHUNK + ch],
                device_id=to, device_id_type=MESH)

        chs = range(N_CHUNK)
        mine = [pltpu.make_async_copy(_chunk(x_ref, ch, rc), slot(me, ch), local_sems.at[ch]) for ch in chs]
        first = [copy(1 + j, ch, me, (*chip, c), from_input=True) for j, chip in enumerate(chips) for ch in chs]
        first += [copy(0, ch, me, sibling, from_input=True) for ch in chs]
        for cp in first + mine:
            cp.start()
        passed = []
        for j, chip in enumerate(chips):
            for ch in chs:
                copy(1 + j, ch, (*chip, c), me).wait_recv()
                fwd = copy(4 + j, ch, (*chip, c), sibling)
                fwd.start()
                passed.append(fwd)
        for ch in chs:
            copy(0, ch, sibling, me).wait_recv()
        for j, chip in enumerate(chips):
            for ch in chs:
                copy(4 + j, ch, (*chip, 1 - c), me).wait_recv()
        for cp in first + passed:
            cp.wait_send()
        for cp in mine:
            cp.wait()

    n_sem = 7 * N_CHUNK
    return pl.pallas_call(
        body, out_shape=jax.ShapeDtypeStruct((N_DEV, R, C), blk.dtype), in_specs=[_ANY], out_specs=_ANY,
        scratch_shapes=[pltpu.SemaphoreType.DMA((n_sem,)), pltpu.SemaphoreType.DMA((n_sem,)),
                        pltpu.SemaphoreType.DMA((N_CHUNK,))],
        name=name)(blk)


def gather_small(small, *, name):
    S, C = small.shape

    def body(s_ref, out_ref, send_sems, recv_sems, local_sem):
        x, y, c = _my_pos()
        my_id = 4 * x + 2 * y + c

        def copy(k, slot):
            px, py, pc = x ^ ((k >> 2) & 1), y ^ ((k >> 1) & 1), c ^ (k & 1)
            dst = my_id if slot == "mine" else 4 * px + 2 * py + pc
            return pltpu.make_async_remote_copy(
                src_ref=s_ref, dst_ref=out_ref.at[dst], send_sem=send_sems.at[k - 1], recv_sem=recv_sems.at[k - 1],
                device_id=(px, py, pc), device_id_type=MESH)

        own = pltpu.make_async_copy(s_ref, out_ref.at[my_id], local_sem)
        own.start()
        sends = [copy(k, "mine") for k in range(1, N_DEV)]
        for cp in sends:
            cp.start()
        for k in range(1, N_DEV):
            copy(k, "theirs").wait_recv()
        for cp in sends:
            cp.wait_send()
        own.wait()

    dma7 = pltpu.SemaphoreType.DMA((7,))
    return pl.pallas_call(
        body, out_shape=jax.ShapeDtypeStruct((N_DEV, S, C), small.dtype), in_specs=[_ANY], out_specs=_ANY,
        scratch_shapes=[dma7, dma7, pltpu.SemaphoreType.DMA], name=name)(small)


def pair_exchange(big, *, name):
    _, R, C = big.shape
    rc = R // N_CHUNK

    def body(big_ref, out_ref, send_sems, recv_sems):
        x, y, c = _my_pos()

        def copy(b, ch):
            return pltpu.make_async_remote_copy(
                src_ref=big_ref.at[2 * b + (1 - c), pl.ds(ch * rc, rc)], dst_ref=out_ref.at[b, pl.ds(ch * rc, rc)],
                send_sem=send_sems.at[b * N_CHUNK + ch], recv_sem=recv_sems.at[b * N_CHUNK + ch],
                device_id=(x, y, 1 - c), device_id_type=MESH)

        cps = [copy(b, ch) for b in range(4) for ch in range(N_CHUNK)]
        for cp in cps:
            cp.start()
        for cp in cps:
            cp.wait_recv()
        for cp in cps:
            cp.wait_send()

    n_sem = 4 * N_CHUNK
    return pl.pallas_call(
        body, out_shape=jax.ShapeDtypeStruct((4, R, C), big.dtype), in_specs=[_ANY], out_specs=_ANY,
        scratch_shapes=[pltpu.SemaphoreType.DMA((n_sem,)), pltpu.SemaphoreType.DMA((n_sem,))],
        name=name)(big)


def pair_sum(big, sib, c, *, tr, name):
    _, R, C = big.shape

    def body(c_ref, a_ref, s_ref, o_ref):
        o_ref[...] = (a_ref[...].astype(F32) + s_ref[...].astype(F32)).astype(o_ref.dtype)

    grid_spec = pltpu.PrefetchScalarGridSpec(
        num_scalar_prefetch=1, grid=(4, R // tr),
        in_specs=[pl.BlockSpec((None, tr, C), lambda b, i, c_ref: (2 * b + c_ref[0], i, 0)),
                  pl.BlockSpec((None, tr, C), lambda b, i, c_ref: (b, i, 0))],
        out_specs=pl.BlockSpec((None, tr, C), lambda b, i, c_ref: (b, i, 0)))
    return pl.pallas_call(body, grid_spec=grid_spec, out_shape=jax.ShapeDtypeStruct((4, R, C), big.dtype),
                          compiler_params=_cparams(2), name=name)(c.reshape(1).astype(jnp.int32), big, sib)


def chip_scatter(pre, *, name):
    _, R, C = pre.shape
    rc = R // N_CHUNK

    def body(pre_ref, out_ref, send_sems, recv_sems, local_sems):
        x, y, c = _my_pos()
        my_chip = 2 * x + y
        chips = [(1 - x, y), (x, 1 - y), (1 - x, 1 - y)]

        def copy(j, ch, slot):
            px, py = chips[j]
            src_slot, dst_slot = (2 * px + py, my_chip) if slot == "mine" else (my_chip, 2 * px + py)
            return pltpu.make_async_remote_copy(
                src_ref=pre_ref.at[src_slot, pl.ds(ch * rc, rc)], dst_ref=out_ref.at[dst_slot, pl.ds(ch * rc, rc)],
                send_sem=send_sems.at[j * N_CHUNK + ch], recv_sem=recv_sems.at[j * N_CHUNK + ch],
                device_id=(px, py, c), device_id_type=MESH)

        chs = range(N_CHUNK)
        own = [pltpu.make_async_copy(pre_ref.at[my_chip, pl.ds(ch * rc, rc)], out_ref.at[my_chip, pl.ds(ch * rc, rc)],
                                     local_sems.at[ch]) for ch in chs]
        sends = [copy(j, ch, "mine") for j in range(3) for ch in chs]
        for cp in sends + own:
            cp.start()
        for j in range(3):
            for ch in chs:
                copy(j, ch, "theirs").wait_recv()
        for cp in sends:
            cp.wait_send()
        for cp in own:
            cp.wait()

    n_sem = 3 * N_CHUNK
    return pl.pallas_call(
        body, out_shape=jax.ShapeDtypeStruct((4, R, C), pre.dtype), in_specs=[_ANY], out_specs=_ANY,
        scratch_shapes=[pltpu.SemaphoreType.DMA((n_sem,)), pltpu.SemaphoreType.DMA((n_sem,)),
                        pltpu.SemaphoreType.DMA((N_CHUNK,))],
        name=name)(pre)


def sibling_send(blk, *, name):
    R, C = blk.shape
    rc = R // N_CHUNK

    def body(x_ref, out_ref, send_sems, recv_sems):
        x, y, c = _my_pos()
        cps = [pltpu.make_async_remote_copy(
            src_ref=_chunk(x_ref, ch, rc), dst_ref=_chunk(out_ref, ch, rc), send_sem=send_sems.at[ch],
            recv_sem=recv_sems.at[ch], device_id=(x, y, 1 - c), device_id_type=MESH) for ch in range(N_CHUNK)]
        for cp in cps:
            cp.start()
        for cp in cps:
            cp.wait_recv()
        for cp in cps:
            cp.wait_send()

    return pl.pallas_call(
        body, out_shape=jax.ShapeDtypeStruct((R, C), blk.dtype), in_specs=[_ANY], out_specs=_ANY,
        scratch_shapes=[pltpu.SemaphoreType.DMA((N_CHUNK,)), pltpu.SemaphoreType.DMA((N_CHUNK,))],
        name=name)(blk)


def reduce_slots(land, *, tr, name):
    n, R, C = land.shape

    def body(l_ref, o_ref):
        acc = l_ref[0].astype(F32)
        for s in range(1, n):
            acc = acc + l_ref[s].astype(F32)
        o_ref[...] = acc

    return pl.pallas_call(
        body, grid=(R // tr,), in_specs=[pl.BlockSpec((n, tr, C), lambda i: (0, i, 0))],
        out_specs=pl.BlockSpec((tr, C), lambda i: (i, 0)), out_shape=jax.ShapeDtypeStruct((R, C), F32),
        compiler_params=_cparams(1), name=name)(land)


TM = 512


def _tk(d):
    return min(d.shape[0], 1024)


def ffn_fwd(x, g, wgu, wd, tag):
    gu, h = norm_matmul(x, g, wgu, tm=TM, tn=1408, split=True, name=f"{tag}_gu")
    xo = mm_nn(gu, wd, res=x, scale=0.5, swiglu=True, tm=TM, tn=D_MODEL, tk=1408, name=f"{tag}_down")
    return xo, (x, gu, h)


def ffn_bwd(d, saved, g, wgu, wd, tag):
    x, gu, h = saved
    dgu, act = ffn_bwd_act(d, wd, gu, tm=256, tn=1408, name=f"{tag}_bact")
    dwd = mm_tn(act, d, scale=0.5, a_split=False, b_split=False, tm=1408, tn=D_MODEL, tk=_tk(d), name=f"{tag}_dwd")
    dwgu = mm_tn(h, dgu, scale=1.0, a_split=False, b_split=True, tm=D_MODEL, tn=1408, tk=_tk(d), name=f"{tag}_dwgu")
    dx, dg = mm_nt_normbwd(dgu, wgu, x, g, d, a_split=True, tm=TM, tk=1408, name=f"{tag}_dx")
    return dx, dg, dwgu, dwd


def _tile2(v):
    return jnp.concatenate([v, v], axis=-1).reshape(1, LANES)


def _fold2(v):
    return v[:, :HEAD_DIM] + v[:, HEAD_DIM:]


EVEN = dict(dil=1, ppk=4, q_blk=0, k_blk=4, v_blk=5, n_heads=A_Q_HEADS, group=A_GROUP, max_dist=A_WINDOW - 1)
STICK = dict(q_blk=6, k_blk=10, v_blk=14, n_pairs=4)


def _odd_cfg(dil):
    return dict(dil=dil, ppk=1, q_blk=0, k_blk=8, v_blk=16, n_heads=C_HEADS, group=1, max_dist=BLK)


def even_fwd(x, g, win, qg, kg, sinks, wout, tag):
    qkv, h = norm_matmul(x, g, win, tm=TM, tn=1152, split=False, name=f"{tag}_in")
    qg2, kg2 = _tile2(qg), _tile2(kg)
    slopes = jnp.asarray(_alibi(A_Q_HEADS), F32)
    oa, lse = banded_fwd(qkv, qg2, kg2, slopes, sinks, name=f"{tag}_swa", **EVEN)
    ob = stick_fwd(qkv, name=f"{tag}_stick", **STICK)
    o = jnp.concatenate([oa, ob], axis=1)
    xo = mm_nn(o, wout, res=x, scale=1.0, swiglu=False, tm=TM, tn=D_MODEL, tk=D_MODEL, name=f"{tag}_out")
    return xo, (x, qkv, h, oa, lse, o)


def even_bwd(d, saved, g, win, qg, kg, sinks, wout, tag):
    x, qkv, h, oa, lse, o = saved
    qg2, kg2 = _tile2(qg), _tile2(kg)
    slopes = jnp.asarray(_alibi(A_Q_HEADS), F32)
    dwout = mm_tn(o, d, scale=1.0, a_split=False, b_split=False, tm=D_MODEL, tn=D_MODEL, tk=_tk(d), name=f"{tag}_dwout")
    do = mm_nt(d, wout, tm=TM, tn=D_MODEL, tk=D_MODEL, name=f"{tag}_do")
    dqa, dka4, dva4, dqg, dkg, dsk = banded_bwd(qkv, qg2, kg2, slopes, sinks, do, oa, lse, None, None,
                                                do_blk=0, name=f"{tag}_swa_b", **EVEN)
    dqb, dkb, dvb = stick_bwd(qkv, do, do_blk=4, name=f"{tag}_stick_b", **STICK)
    dqkv = assemble_even(dqa, dka4, dva4, dqb, dkb, dvb, tm=TM, name=f"{tag}_asm")
    dwin = mm_tn(h, dqkv, scale=1.0, a_split=False, b_split=False, tm=D_MODEL, tn=1152, tk=_tk(d), name=f"{tag}_dwin")
    dx, dg = mm_nt_normbwd(dqkv, win, x, g, d, a_split=False, tm=TM, tk=1152, name=f"{tag}_dx")
    return dx, dg, dwin, _fold2(dqg), _fold2(dkg), dsk[:, :A_Q_HEADS], dwout


def odd_fwd(x, g, win, qg, kg, wout, tag):
    qkv, h = norm_matmul(x, g, win, tm=TM, tn=1536, split=False, name=f"{tag}_in")
    qg2, kg2 = _tile2(qg), _tile2(kg)
    outs = []
    for p, (window, dil) in enumerate(C_PATTERNS):
        slopes = jnp.asarray(_alibi(C_HEADS), F32) * float(dil)
        outs.append(banded_fwd(qkv, qg2, kg2, slopes, None, name=f"{tag}_dil{p}", **_odd_cfg(dil)))
    o, w1, w2, w3 = mix_fwd(outs[0][0], outs[1][0], outs[2][0], outs[0][1], outs[1][1], outs[2][1],
                            tm=TM, name=f"{tag}_mix")
    xo = mm_nn(o, wout, res=x, scale=1.0, swiglu=False, tm=TM, tn=D_MODEL, tk=D_MODEL, name=f"{tag}_out")
    return xo, (x, qkv, h, outs, (w1, w2, w3), o)


def odd_bwd(d, saved, g, win, qg, kg, wout, tag):
    x, qkv, h, outs, ws, o = saved
    qg2, kg2 = _tile2(qg), _tile2(kg)
    dwout = mm_tn(o, d, scale=1.0, a_split=False, b_split=False, tm=D_MODEL, tn=D_MODEL, tk=_tk(d), name=f"{tag}_dwout")
    do = mm_nt(d, wout, tm=TM, tn=D_MODEL, tk=D_MODEL, name=f"{tag}_do")
    parts, dqg, dkg = [], 0.0, 0.0
    for p, (window, dil) in enumerate(C_PATTERNS):
        slopes = jnp.asarray(_alibi(C_HEADS), F32) * float(dil)
        dq, dk, dv, dqg_p, dkg_p, _ = banded_bwd(qkv, qg2, kg2, slopes, None, do, outs[p][0], outs[p][1], ws[p], o,
                                                 do_blk=0, name=f"{tag}_dil{p}_b", **_odd_cfg(dil))
        parts.append((dq, dk, dv))
        dqg = dqg + dqg_p
        dkg = dkg + dkg_p
    dqkv = assemble_odd(parts, tm=TM, name=f"{tag}_asm")
    dwin = mm_tn(h, dqkv, scale=1.0, a_split=False, b_split=False, tm=D_MODEL, tn=1536, tk=_tk(d), name=f"{tag}_dwin")
    dx, dg = mm_nt_normbwd(dqkv, win, x, g, d, a_split=False, tm=TM, tk=1536, name=f"{tag}_dx")
    return dx, dg, dwin, _fold2(dqg), _fold2(dkg), dwout


def xa_fwd(x, mem, g, gm, wq, wkv, qg, kg, wo, tag):
    qraw, h = norm_matmul(x, g, wq, tm=TM, tn=D_MODEL, split=False, name=f"{tag}_q")
    kvraw, hm = norm_matmul(mem, gm, wkv, tm=MEM_LEN, tn=D_MODEL, split=False, name=f"{tag}_kv")
    o = xattn_fwd(qraw, kvraw, qg, kg, tm=TM, name=f"{tag}_att")
    xo = mm_nn(o, wo, res=x, scale=1.0, swiglu=False, tm=TM, tn=D_MODEL, tk=D_MODEL, name=f"{tag}_o")
    return xo, (x, qraw, h, kvraw, hm, o)


def xa_bwd(d, saved, mem, g, gm, wq, wkv, qg, kg, wo, tag):
    x, qraw, h, kvraw, hm, o = saved
    dwo = mm_tn(o, d, scale=1.0, a_split=False, b_split=False, tm=D_MODEL, tn=D_MODEL, tk=_tk(d), name=f"{tag}_dwo")
    do = mm_nt(d, wo, tm=TM, tn=D_MODEL, tk=D_MODEL, name=f"{tag}_do")
    dq, dkv, dqg, dkg = xattn_bwd(qraw, kvraw, qg, kg, do, o, tm=TM, name=f"{tag}_att_b")
    dwq = mm_tn(h, dq, scale=1.0, a_split=False, b_split=False, tm=D_MODEL, tn=D_MODEL, tk=_tk(d), name=f"{tag}_dwq")
    dx, dg = mm_nt_normbwd(dq, wq, x, g, d, a_split=False, tm=TM, tk=D_MODEL, name=f"{tag}_dx")
    dwkv = mm_tn(hm, dkv, scale=1.0, a_split=False, b_split=False, tm=D_MODEL, tn=D_MODEL, tk=MEM_LEN,
                 name=f"{tag}_dwkv")
    _, dgm = mm_nt_normbwd(dkv, wkv, mem, gm, None, a_split=False, tm=MEM_LEN, tk=D_MODEL, name=f"{tag}_dmem")
    return dx, dg, dgm, dwq, dwkv, dqg, dkg, dwo


MATS = (("ffn1_w_gu", 1), ("ffn1_w_down", 0), ("ev_w_in", 1), ("ev_w_out", 0), ("od_w_in", 1), ("od_w_out", 0),
        ("xa_w_q", 0), ("xa_w_kv", 1), ("xa_w_o", 0), ("ffn2_w_gu", 1), ("ffn2_w_down", 0))
SMALLS = ("ffn1_norm", "mix_norm", "ev_q_gain", "ev_k_gain", "ev_sinks", "od_q_gain", "od_k_gain", "xa_norm",
          "xa_mem_norm", "xa_q_gain", "xa_k_gain", "ffn2_norm")
WEIGHTS = ("ffn1_norm", "ffn1_w_gu", "ffn1_w_down", "mix_norm", "ev_w_in", "ev_q_gain", "ev_k_gain", "ev_sinks",
           "ev_w_out", "od_w_in", "od_q_gain", "od_k_gain", "od_w_out", "xa_norm", "xa_mem_norm", "xa_w_q",
           "xa_w_kv", "xa_q_gain", "xa_k_gain", "xa_w_o", "ffn2_norm", "ffn2_w_gu", "ffn2_w_down")
PACK_W = 1024
SMALL_ROWS = 16
SUM_ROWS = 400


def _mat_items(shards):
    items = []
    for name, axis in MATS:
        nl, r, cc = shards[name].shape
        for layer in range(nl):
            items.append((name, layer, axis, (r, cc)))
    return items


def _padded_rows(n):
    unit = SUM_ROWS * N_CHUNK
    return -(-n // unit) * unit


def pack_halves(shards, c):
    rows = []
    for name, layer, axis, (r, cc) in _mat_items(shards):
        half = lax.dynamic_index_in_dim(shards[name][layer].reshape(2, r // 2, cc), c, 0, keepdims=False)
        rows.append(half.astype(BF16).reshape(-1, PACK_W))
    used = sum(r.shape[0] for r in rows)
    rows.append(jnp.zeros((_padded_rows(used) - used, PACK_W), BF16))
    return jnp.concatenate(rows, axis=0)


def unpack_full(gathered, shards):
    full, off = {}, 0
    for name, layer, axis, (r, cc) in _mat_items(shards):
        nr = (r // 2) * cc // PACK_W
        piece = gathered[:, off:off + nr].reshape(4, r, cc)
        off += nr
        full[(name, layer)] = piece.reshape(4 * r, cc) if axis == 0 else piece.transpose(1, 0, 2).reshape(r, 4 * cc)
    return full


def pack_grads(grads, shards):
    rows = []
    for name, layer, axis, (r, cc) in _mat_items(shards):
        gfull = grads[(name, layer)]
        if axis == 0:
            piece = gfull.reshape(4, 2, r // 2, cc)
        else:
            piece = gfull.reshape(2, r // 2, 4, cc).transpose(2, 0, 1, 3)
        rows.append(piece.astype(BF16).reshape(N_DEV, -1, PACK_W))
    used = sum(r.shape[1] for r in rows)
    rows.append(jnp.zeros((N_DEV, _padded_rows(used) - used, PACK_W), BF16))
    return jnp.concatenate(rows, axis=1)


def unpack_shard_grads(summed, shards):
    per, off = {}, 0
    for name, layer, axis, (r, cc) in _mat_items(shards):
        nr = (r // 2) * cc // PACK_W
        per.setdefault(name, []).append(summed[:, off:off + nr].reshape(r, cc))
        off += nr
    return {name: jnp.stack(lst, axis=0) for name, lst in per.items()}


def pack_small(vals):
    row10 = jnp.concatenate([vals["xa_q_gain"].reshape(1, 512), vals["xa_k_gain"].reshape(1, 512)], axis=1)
    row11 = jnp.concatenate([vals["ev_q_gain"], vals["ev_k_gain"], vals["od_q_gain"], vals["od_k_gain"],
                             vals["ev_sinks"], jnp.zeros((1, 1024 - 4 * 64 - 8), F32)], axis=1)
    return jnp.concatenate([vals["ffn1_norm"], vals["mix_norm"], vals["xa_norm"], vals["xa_mem_norm"],
                            vals["ffn2_norm"], row10, row11, jnp.zeros((SMALL_ROWS - 12, 1024), F32)], axis=0)


def unpack_small(arr):
    return {"ffn1_norm": arr[0:2], "mix_norm": arr[2:4], "xa_norm": arr[4:6], "xa_mem_norm": arr[6:8],
            "ffn2_norm": arr[8:10],
            "xa_q_gain": arr[10:11, 0:512].reshape(2, 256), "xa_k_gain": arr[10:11, 512:1024].reshape(2, 256),
            "ev_q_gain": arr[11:12, 0:64], "ev_k_gain": arr[11:12, 64:128], "od_q_gain": arr[11:12, 128:192],
            "od_k_gain": arr[11:12, 192:256], "ev_sinks": arr[11:12, 256:264]}


def local_step(x, mem, target, W, small):
    depth = small["ffn1_norm"].shape[0]

    def row(name, l):
        return small[name][l:l + 1]

    saved = []
    for l in range(depth):
        j = l // 2
        x, s1 = ffn_fwd(x, row("ffn1_norm", l), W[("ffn1_w_gu", l)], W[("ffn1_w_down", l)], f"l{l}_f1")
        if l % 2 == 0:
            x, s2 = even_fwd(x, row("mix_norm", l), W[("ev_w_in", j)], row("ev_q_gain", j), row("ev_k_gain", j),
                             small["ev_sinks"][j], W[("ev_w_out", j)], f"l{l}_ev")
        else:
            x, s2 = odd_fwd(x, row("mix_norm", l), W[("od_w_in", j)], row("od_q_gain", j), row("od_k_gain", j),
                            W[("od_w_out", j)], f"l{l}_od")
        x, s3 = xa_fwd(x, mem, row("xa_norm", l), row("xa_mem_norm", l), W[("xa_w_q", l)], W[("xa_w_kv", l)],
                       row("xa_q_gain", l), row("xa_k_gain", l), W[("xa_w_o", l)], f"l{l}_xa")
        x, s4 = ffn_fwd(x, row("ffn2_norm", l), W[("ffn2_w_gu", l)], W[("ffn2_w_down", l)], f"l{l}_f2")
        saved.append((s1, s2, s3, s4))
    loss, d = loss_kernel(x, target, tm=TM, name="loss")

    gw = {}
    gs = {name: [None] * small[name].shape[0] for name in SMALLS}
    for l in reversed(range(depth)):
        j = l // 2
        s1, s2, s3, s4 = saved[l]
        d, dg, dwgu, dwd = ffn_bwd(d, s4, row("ffn2_norm", l), W[("ffn2_w_gu", l)], W[("ffn2_w_down", l)], f"l{l}_f2")
        gs["ffn2_norm"][l] = dg
        gw[("ffn2_w_gu", l)], gw[("ffn2_w_down", l)] = dwgu, dwd
        d, dg, dgm, dwq, dwkv, dqg, dkg, dwo = xa_bwd(
            d, s3, mem, row("xa_norm", l), row("xa_mem_norm", l), W[("xa_w_q", l)], W[("xa_w_kv", l)],
            row("xa_q_gain", l), row("xa_k_gain", l), W[("xa_w_o", l)], f"l{l}_xa")
        gs["xa_norm"][l], gs["xa_mem_norm"][l], gs["xa_q_gain"][l], gs["xa_k_gain"][l] = dg, dgm, dqg, dkg
        gw[("xa_w_q", l)], gw[("xa_w_kv", l)], gw[("xa_w_o", l)] = dwq, dwkv, dwo
        if l % 2 == 0:
            d, dg, dwin, dqg, dkg, dsk, dwout = even_bwd(
                d, s2, row("mix_norm", l), W[("ev_w_in", j)], row("ev_q_gain", j), row("ev_k_gain", j),
                small["ev_sinks"][j], W[("ev_w_out", j)], f"l{l}_ev")
            gs["ev_q_gain"][j], gs["ev_k_gain"][j], gs["ev_sinks"][j] = dqg, dkg, dsk
            gw[("ev_w_in", j)], gw[("ev_w_out", j)] = dwin, dwout
        else:
            d, dg, dwin, dqg, dkg, dwout = odd_bwd(
                d, s2, row("mix_norm", l), W[("od_w_in", j)], row("od_q_gain", j), row("od_k_gain", j),
                W[("od_w_out", j)], f"l{l}_od")
            gs["od_q_gain"][j], gs["od_k_gain"][j] = dqg, dkg
            gw[("od_w_in", j)], gw[("od_w_out", j)] = dwin, dwout
        gs["mix_norm"][l] = dg
        d, dg, dwgu, dwd = ffn_bwd(d, s1, row("ffn1_norm", l), W[("ffn1_w_gu", l)], W[("ffn1_w_down", l)], f"l{l}_f1")
        gs["ffn1_norm"][l] = dg
        gw[("ffn1_w_gu", l)], gw[("ffn1_w_down", l)] = dwgu, dwd
    gsmall = {name: jnp.concatenate(v, axis=0) for name, v in gs.items()}
    return loss, d, gw, gsmall


def kernel(x, mem, ffn1_norm, ffn1_w_gu, ffn1_w_down, mix_norm, ev_w_in, ev_q_gain, ev_k_gain, ev_sinks, ev_w_out, od_w_in, od_q_gain, od_k_gain, od_w_out, xa_norm, xa_mem_norm, xa_w_q, xa_w_kv, xa_q_gain, xa_k_gain, xa_w_o, ffn2_norm, ffn2_w_gu, ffn2_w_down, loss_target, m_ffn1_norm, m_ffn1_w_gu, m_ffn1_w_down, m_mix_norm, m_ev_w_in, m_ev_q_gain, m_ev_k_gain, m_ev_sinks, m_ev_w_out, m_od_w_in, m_od_q_gain, m_od_k_gain, m_od_w_out, m_xa_norm, m_xa_mem_norm, m_xa_w_q, m_xa_w_kv, m_xa_q_gain, m_xa_k_gain, m_xa_w_o, m_ffn2_norm, m_ffn2_w_gu, m_ffn2_w_down, v_ffn1_norm, v_ffn1_w_gu, v_ffn1_w_down, v_mix_norm, v_ev_w_in, v_ev_q_gain, v_ev_k_gain, v_ev_sinks, v_ev_w_out, v_od_w_in, v_od_q_gain, v_od_k_gain, v_od_w_out, v_xa_norm, v_xa_mem_norm, v_xa_w_q, v_xa_w_kv, v_xa_q_gain, v_xa_k_gain, v_xa_w_o, v_ffn2_norm, v_ffn2_w_gu, v_ffn2_w_down):
    given = dict(locals())
    w = {n: given[n] for n in WEIGHTS}
    m = {n: given["m_" + n] for n in WEIGHTS}
    v = {n: given["v_" + n] for n in WEIGHTS}
    c = lax.axis_index("c")
    shards = {name: w[name] for name, _ in MATS}
    small = {n: w[n] for n in SMALLS}

    gathered = gather_blocks(pack_halves(shards, c), name="gather_weights")
    full = unpack_full(gathered, shards)
    loss_b, grad_x, gw, gsmall = local_step(x[0], mem[0], loss_target[0], full, small)

    packed = pack_grads(gw, shards)
    pre = pair_sum(packed, pair_exchange(packed, name="pair_grads"), c, tr=SUM_ROWS, name="pair_sum")
    mine = reduce_slots(chip_scatter(pre, name="scatter_grads"), tr=SUM_ROWS, name="sum_grads")
    theirs = sibling_send(mine, name="swap_grads")
    both = jnp.stack([jnp.where(c == 0, mine, theirs), jnp.where(c == 0, theirs, mine)])
    land_small = gather_small(pack_small(gsmall), name="gather_small")
    g_small = unpack_small(reduce_slots(land_small, tr=SMALL_ROWS, name="sum_small"))
    g = dict(unpack_shard_grads(both, shards))
    g.update(g_small)

    delta, new_m, new_v = {}, {}, {}
    for name, _ in MATS:
        shp = w[name].shape
        flat = [a.reshape(-1, shp[-1]) for a in (w[name], g[name], m[name], v[name])]
        dl, nm, nv = adamw(*flat, br=BLK, name=f"adamw_{name}")
        delta[name], new_m[name], new_v[name] = dl.reshape(shp), nm.reshape(shp), nv.reshape(shp)
    dl, nm, nv = adamw(pack_small(small), pack_small(g_small), pack_small({n: m[n] for n in SMALLS}),
                       pack_small({n: v[n] for n in SMALLS}), br=SMALL_ROWS, name="adamw_small")
    for dst, arr in ((delta, dl), (new_m, nm), (new_v, nv)):
        dst.update(unpack_small(arr))

    loss = lax.psum(loss_b[0, 0], ("x", "y", "c"))
    return (loss, grad_x[None], *[g[n] for n in WEIGHTS], *[delta[n] for n in WEIGHTS],
            *[new_m[n] for n in WEIGHTS], *[new_v[n] for n in WEIGHTS])
```

```python
import jax
import jax.numpy as jnp
from jax import lax
from jax.experimental import pallas as pl
from jax.experimental.pallas import tpu as pltpu

F32 = jnp.float32
BF16 = jnp.bfloat16

D_MODEL = 1024
HEAD_DIM = 64
LANES = 128
BLK = 128
D_FF = 2816
RMS_EPS = 1e-6
MEM_LEN = 256
X_HEADS = 4
X_HEAD_DIM = 256
A_Q_HEADS = 8
A_GROUP = 4
A_WINDOW = 128
C_HEADS = 16
C_PATTERNS = ((128, 1), (512, 4), (2048, 16))
NEG = -1e30
VMEM_LIMIT = 56 * 2 ** 20

ADAM_LR = 0.001
ADAM_B1 = 0.9
ADAM_B2 = 0.999
ADAM_EPS = 1e-08
ADAM_WD = 0.01
ADAM_STEP = 10

N_DEV = 8
MESH = pl.DeviceIdType.MESH


def _cparams(n):
    return pltpu.CompilerParams(dimension_semantics=("arbitrary",) * n, vmem_limit_bytes=VMEM_LIMIT)


def _dot(a, b):
    return jnp.dot(a, b, preferred_element_type=F32)


def _dot_nt(a, b):
    return lax.dot_general(a, b, (((1,), (1,)), ((), ())), preferred_element_type=F32)


def _dot_tn(a, b):
    return lax.dot_general(a, b, (((0,), (0,)), ((), ())), preferred_element_type=F32)


def _sigmoid(z):
    return 1.0 / (1.0 + jnp.exp(-z))


def norm_matmul(x, g, w, *, tm, tn, split, name):
    T, K = x.shape
    blocked = w.ndim == 3
    assert not blocked or w.shape[2] == tn
    N = w.shape[0] * w.shape[2] if blocked else w.shape[1]
    nj = N // tn

    def body(x_ref, g_ref, w_ref, o_ref, h_ref):
        @pl.when(pl.program_id(1) == 0)
        def _():
            xv = x_ref[...]
            r = lax.rsqrt(jnp.mean(xv * xv, axis=-1, keepdims=True) + RMS_EPS)
            h_ref[...] = (xv * r * g_ref[...]).astype(BF16)

        o_ref[...] = _dot(h_ref[...], w_ref[...])

    if split:
        njh = nj // 2
        o_shape = jax.ShapeDtypeStruct((2, T, N // 2), F32)
        o_spec = pl.BlockSpec((None, tm, tn), lambda i, j: (j // njh, i, j % njh))
    else:
        o_shape = jax.ShapeDtypeStruct((T, N), F32)
        o_spec = pl.BlockSpec((tm, tn), lambda i, j: (i, j))
    return pl.pallas_call(
        body, grid=(T // tm, nj),
        in_specs=[pl.BlockSpec((tm, K), lambda i, j: (i, 0)),
                  pl.BlockSpec((1, K), lambda i, j: (0, 0)),
                  (pl.BlockSpec((None, K, tn), lambda i, j: (j, 0, 0)) if blocked
                   else pl.BlockSpec((K, tn), lambda i, j: (0, j)))],
        out_specs=[o_spec, pl.BlockSpec((tm, K), lambda i, j: (i, 0))],
        out_shape=[o_shape, jax.ShapeDtypeStruct((T, K), BF16)],
        compiler_params=_cparams(2), name=name)(x, g, w)


def mm_nn(a, b, *, res, scale, swiglu, tm, tn, tk, name):
    T = a.shape[-2]
    K, N = b.shape
    nk = K // tk

    def body(*refs):
        if swiglu:
            g_ref, u_ref, b_ref, r_ref, o_ref, acc = refs
        else:
            a_ref, b_ref, r_ref, o_ref, acc = refs
        k = pl.program_id(2)

        @pl.when(k == 0)
        def _():
            acc[...] = jnp.zeros_like(acc)

        if swiglu:
            gv = g_ref[...]
            av = (gv * _sigmoid(gv) * u_ref[...]).astype(BF16)
        else:
            av = a_ref[...].astype(BF16)
        acc[...] += _dot(av, b_ref[...])

        @pl.when(k == nk - 1)
        def _():
            o_ref[...] = r_ref[...] + scale * acc[...]

    if swiglu:
        a_specs = [pl.BlockSpec((None, tm, tk), lambda i, j, k: (0, i, k)),
                   pl.BlockSpec((None, tm, tk), lambda i, j, k: (1, i, k))]
        a_args = [a, a]
    else:
        a_specs = [pl.BlockSpec((tm, tk), lambda i, j, k: (i, k))]
        a_args = [a]
    return pl.pallas_call(
        body, grid=(T // tm, N // tn, nk),
        in_specs=a_specs + [pl.BlockSpec((tk, tn), lambda i, j, k: (k, j)),
                            pl.BlockSpec((tm, tn), lambda i, j, k: (i, j))],
        out_specs=pl.BlockSpec((tm, tn), lambda i, j, k: (i, j)),
        out_shape=jax.ShapeDtypeStruct((T, N), F32),
        scratch_shapes=[pltpu.VMEM((tm, tn), F32)],
        compiler_params=_cparams(3), name=name)(*a_args, b, res)


def mm_nt(a, b, *, tm, tn, tk, name):
    T, K = a.shape
    N = b.shape[0]
    nk = K // tk

    def body(a_ref, b_ref, o_ref, acc):
        k = pl.program_id(2)

        @pl.when(k == 0)
        def _():
            acc[...] = jnp.zeros_like(acc)

        acc[...] += _dot_nt(a_ref[...].astype(BF16), b_ref[...])

        @pl.when(k == nk - 1)
        def _():
            o_ref[...] = acc[...]

    return pl.pallas_call(
        body, grid=(T // tm, N // tn, nk),
        in_specs=[pl.BlockSpec((tm, tk), lambda i, j, k: (i, k)),
                  pl.BlockSpec((tn, tk), lambda i, j, k: (j, k))],
        out_specs=pl.BlockSpec((tm, tn), lambda i, j, k: (i, j)),
        out_shape=jax.ShapeDtypeStruct((T, N), F32),
        scratch_shapes=[pltpu.VMEM((tm, tn), F32)],
        compiler_params=_cparams(3), name=name)(a, b)


def ffn_bwd_act(d, wd, gu, *, tm, tn, name):
    T, K = d.shape
    Fd = wd.shape[0]

    def body(d_ref, w_ref, g_ref, u_ref, dgu_ref, act_ref):
        da = 0.5 * _dot_nt(d_ref[...].astype(BF16), w_ref[...])
        gv = g_ref[...]
        uv = u_ref[...]
        s = _sigmoid(gv)
        silu = gv * s
        act_ref[...] = (silu * uv).astype(BF16)
        dgu_ref[0] = (da * uv * (s * (1.0 + gv * (1.0 - s)))).astype(BF16)
        dgu_ref[1] = (da * silu).astype(BF16)

    return pl.pallas_call(
        body, grid=(T // tm, Fd // tn),
        in_specs=[pl.BlockSpec((tm, K), lambda i, j: (i, 0)),
                  pl.BlockSpec((tn, K), lambda i, j: (j, 0)),
                  pl.BlockSpec((None, tm, tn), lambda i, j: (0, i, j)),
                  pl.BlockSpec((None, tm, tn), lambda i, j: (1, i, j))],
        out_specs=[pl.BlockSpec((2, tm, tn), lambda i, j: (0, i, j)),
                   pl.BlockSpec((tm, tn), lambda i, j: (i, j))],
        out_shape=[jax.ShapeDtypeStruct((2, T, Fd), BF16), jax.ShapeDtypeStruct((T, Fd), BF16)],
        compiler_params=_cparams(2), name=name)(d, wd, gu, gu)


def mm_nt_normbwd(a, b, x, g, res, *, a_split, tm, tk, name):
    T, Dm = x.shape
    blocked = b.ndim == 3
    assert not blocked or b.shape[2] == tk
    K = b.shape[0] * b.shape[2] if blocked else b.shape[1]
    nk = K // tk
    nkh = nk // 2
    has_res = res is not None

    def body(*refs):
        if has_res:
            a_ref, b_ref, x_ref, g_ref, r_ref, dx_ref, dg_ref, acc = refs
        else:
            a_ref, b_ref, x_ref, g_ref, dx_ref, dg_ref, acc = refs
        i = pl.program_id(0)
        k = pl.program_id(1)

        @pl.when(k == 0)
        def _():
            acc[...] = jnp.zeros_like(acc)

        acc[...] += _dot_nt(a_ref[...].astype(BF16), b_ref[...])

        @pl.when(k == nk - 1)
        def _():
            xv = x_ref[...]
            r = lax.rsqrt(jnp.mean(xv * xv, axis=-1, keepdims=True) + RMS_EPS)
            xh = xv * r
            dh = acc[...]
            dxh = dh * g_ref[...]
            dx = r * (dxh - xh * jnp.mean(dxh * xh, axis=-1, keepdims=True))
            if has_res:
                dx = dx + r_ref[...]
            dx_ref[...] = dx
            part = jnp.sum(dh * xh, axis=0, keepdims=True)

            @pl.when(i == 0)
            def _():
                dg_ref[...] = part

            @pl.when(i > 0)
            def _():
                dg_ref[...] += part

    if a_split:
        a_spec = pl.BlockSpec((None, tm, tk), lambda i, k: (k // nkh, i, k % nkh))
    else:
        a_spec = pl.BlockSpec((tm, tk), lambda i, k: (i, k))
    in_specs = [a_spec,
                (pl.BlockSpec((None, Dm, tk), lambda i, k: (k, 0, 0)) if blocked
                 else pl.BlockSpec((Dm, tk), lambda i, k: (0, k))),
                pl.BlockSpec((tm, Dm), lambda i, k: (i, 0)),
                pl.BlockSpec((1, Dm), lambda i, k: (0, 0))]
    args = [a, b, x, g]
    if has_res:
        in_specs.append(pl.BlockSpec((tm, Dm), lambda i, k: (i, 0)))
        args.append(res)
    return pl.pallas_call(
        body, grid=(T // tm, nk), in_specs=in_specs,
        out_specs=[pl.BlockSpec((tm, Dm), lambda i, k: (i, 0)),
                   pl.BlockSpec((1, Dm), lambda i, k: (0, 0))],
        out_shape=[jax.ShapeDtypeStruct((T, Dm), F32), jax.ShapeDtypeStruct((1, Dm), F32)],
        scratch_shapes=[pltpu.VMEM((tm, Dm), F32)],
        compiler_params=_cparams(2), name=name)(*args)


def mm_tn(a, b, *, scale, a_split, b_split, tm, tn, tk, name, out_blocked=False):
    T = a.shape[-2]
    M = a.shape[-1] * (2 if a_split else 1)
    N = b.shape[-1] * (2 if b_split else 1)
    ni, nj, nk = M // tm, N // tn, T // tk
    nih, njh = ni // 2, nj // 2

    def body(a_ref, b_ref, o_ref, acc):
        k = pl.program_id(2)

        @pl.when(k == 0)
        def _():
            acc[...] = jnp.zeros_like(acc)

        acc[...] += _dot_tn(a_ref[...].astype(BF16), b_ref[...].astype(BF16))

        @pl.when(k == nk - 1)
        def _():
            o_ref[...] = (acc[...] * scale).astype(o_ref.dtype)

    if a_split:
        a_spec = pl.BlockSpec((None, tk, tm), lambda i, j, k: (i // nih, k, i % nih))
    else:
        a_spec = pl.BlockSpec((tk, tm), lambda i, j, k: (k, i))
    if b_split:
        b_spec = pl.BlockSpec((None, tk, tn), lambda i, j, k: (j // njh, k, j % njh))
    else:
        b_spec = pl.BlockSpec((tk, tn), lambda i, j, k: (k, j))
    if out_blocked:
        o_spec = pl.BlockSpec((None, None, tm, tn), lambda i, j, k: (j, i, 0, 0))
        o_shape = jax.ShapeDtypeStruct((nj, ni, tm, tn), BF16)
    else:
        o_spec = pl.BlockSpec((tm, tn), lambda i, j, k: (i, j))
        o_shape = jax.ShapeDtypeStruct((M, N), BF16)
    return pl.pallas_call(
        body, grid=(ni, nj, nk), in_specs=[a_spec, b_spec], out_specs=o_spec, out_shape=o_shape,
        scratch_shapes=[pltpu.VMEM((tm, tn), F32)],
        compiler_params=_cparams(3), name=name)(a, b)


def loss_kernel(y, target, *, tm, name):
    T, Dm = y.shape

    def body(y_ref, t_ref, l_ref, dy_ref):
        e = y_ref[...] - t_ref[...]
        dy_ref[...] = e * (1.0 / Dm)
        part = (0.5 / Dm) * jnp.sum(jnp.sum(e * e, axis=-1, keepdims=True), axis=0, keepdims=True)
        part = jnp.broadcast_to(part, (8, LANES))

        @pl.when(pl.program_id(0) == 0)
        def _():
            l_ref[...] = part

        @pl.when(pl.program_id(0) > 0)
        def _():
            l_ref[...] += part

    return pl.pallas_call(
        body, grid=(T // tm,),
        in_specs=[pl.BlockSpec((tm, Dm), lambda i: (i, 0)), pl.BlockSpec((tm, Dm), lambda i: (i, 0))],
        out_specs=[pl.BlockSpec((8, LANES), lambda i: (0, 0)), pl.BlockSpec((tm, Dm), lambda i: (i, 0))],
        out_shape=[jax.ShapeDtypeStruct((8, LANES), F32), jax.ShapeDtypeStruct((T, Dm), F32)],
        compiler_params=_cparams(1), name=name)(y, target)


def adamw(w, g, m, v, *, br, name):
    R, C = w.shape

    def body(w_ref, g_ref, m_ref, v_ref, d_ref, nm_ref, nv_ref):
        gv = g_ref[...]
        nm = ADAM_B1 * m_ref[...] + (1.0 - ADAM_B1) * gv
        nv = ADAM_B2 * v_ref[...] + (1.0 - ADAM_B2) * (gv * gv)
        m_hat = nm / (1.0 - ADAM_B1 ** ADAM_STEP)
        v_hat = nv / (1.0 - ADAM_B2 ** ADAM_STEP)
        d_ref[...] = -ADAM_LR * (m_hat / (jnp.sqrt(v_hat) + ADAM_EPS) + ADAM_WD * w_ref[...])
        nm_ref[...] = nm
        nv_ref[...] = nv

    spec = pl.BlockSpec((br, C), lambda i: (i, 0))
    shp = jax.ShapeDtypeStruct((R, C), F32)
    return pl.pallas_call(
        body, grid=(R // br,), in_specs=[spec] * 4, out_specs=[spec] * 3, out_shape=[shp] * 3,
        compiler_params=_cparams(1), name=name)(w, g, m, v)


def _lane0():
    return lax.broadcasted_iota(jnp.int32, (1, LANES), 1) < HEAD_DIM


def _half_sum(x, m0):
    s0 = jnp.sum(jnp.where(m0, x, 0.0), axis=-1, keepdims=True)
    s1 = jnp.sum(jnp.where(m0, 0.0, x), axis=-1, keepdims=True)
    return jnp.where(m0, s0, s1)


def _half_pick(x, m0, e):
    sel = m0 if e == 0 else jnp.logical_not(m0)
    return jnp.max(jnp.where(sel, x, NEG), axis=-1, keepdims=True)


def _head_rms(x, m0):
    return lax.rsqrt(_half_sum(x * x, m0) * (1.0 / HEAD_DIM) + RMS_EPS)


def _alibi(n):
    return [float(2.0 ** (-8.0 * (h + 1) / n)) for h in range(n)]


def _mask_half(x, m0, e):
    return jnp.where(m0, x, 0.0) if e == 0 else jnp.where(m0, 0.0, x)


def _band_masks2(max_dist, has_prev, live):
    row = lax.broadcasted_iota(jnp.int32, (2 * BLK, 2 * BLK), 0)
    col = lax.broadcasted_iota(jnp.int32, (2 * BLK, 2 * BLK), 1)
    dist = (row & (BLK - 1)) - col + BLK
    lim = jnp.where(live, max_dist, -1)
    first = jnp.where(has_prev, 0, BLK)
    valid = (dist >= 0) & (dist <= lim) & (col >= first)
    top = lax.broadcasted_iota(jnp.int32, (2 * BLK, 1), 0) < BLK
    return dist.astype(F32), valid, top


def _stack_heads(x, m0, kes):
    parts = []
    for e in range(2):
        h = _mask_half(x, m0, e)
        parts.append(pltpu.roll(h, HEAD_DIM, 1) if kes[e] != e else h)
    return jnp.concatenate(parts, axis=0)


def _unstack_heads(y, m0, kes):
    parts = []
    for e in range(2):
        h = y[e * BLK:(e + 1) * BLK]
        parts.append(pltpu.roll(h, HEAD_DIM, 1) if kes[e] != e else h)
    return jnp.where(m0, parts[0], parts[1])


def _rows(r, dil):
    return pl.ds(r, BLK, stride=dil) if dil > 1 else pl.ds(0, BLK)


def _band_specs(dil, ppk, q_blk, k_blk, v_blk, kv_shared, nb):
    RB = BLK * dil
    qw = LANES * ppk
    kw = LANES if kv_shared else qw

    def cur(i):
        return jnp.minimum(i, nb - 1)

    def kidx(base):
        return (lambda p, i: (cur(i), base)) if kv_shared else (lambda p, i: (cur(i), base + p))

    def pidx(base):
        return ((lambda p, i: (jnp.maximum(i - 1, 0), base)) if kv_shared
                else (lambda p, i: (jnp.maximum(i - 1, 0), base + p)))

    return [pl.BlockSpec((RB, qw), lambda p, i: (cur(i), q_blk + p)),
            pl.BlockSpec((RB, kw), kidx(k_blk)), pl.BlockSpec((RB, kw), pidx(k_blk)),
            pl.BlockSpec((RB, kw), kidx(v_blk)), pl.BlockSpec((RB, kw), pidx(v_blk))]


def banded_fwd(qkv, q_gain2, k_gain2, slopes, sinks, *, dil, ppk, q_blk, k_blk, v_blk, n_heads, group,
               max_dist, name):
    T = qkv.shape[0]
    RB = BLK * dil
    nb = T // RB
    npair = n_heads // 2
    kv_shared = group > 1
    scale = HEAD_DIM ** -0.5
    has_sink = sinks is not None

    def body(*refs):
        slope_ref = refs[0]
        if has_sink:
            sink_ref, refs = refs[1], refs[2:]
        else:
            refs = refs[1:]
        q_ref, kc_ref, kp_ref, vc_ref, vp_ref, qg_ref, kg_ref, o_ref, l_ref = refs
        pb = pl.program_id(0)
        i = pl.program_id(1)
        m0 = _lane0()
        distf, valid, top = _band_masks2(max_dist, i > 0, i >= 0)
        qg = qg_ref[...]
        kg = kg_ref[...]
        for r in range(dil):
            rows = _rows(r, dil)
            kcache = {}
            for jp in range(ppk):
                cs = pl.ds(LANES * jp, LANES)
                jk = 0 if kv_shared else jp
                if jk not in kcache:
                    ks = pl.ds(LANES * jk, LANES)
                    kcat = jnp.concatenate([kp_ref[rows, ks], kc_ref[rows, ks]], axis=0)
                    vcat = jnp.concatenate([vp_ref[rows, ks], vc_ref[rows, ks]], axis=0)
                    kcache[jk] = ((kcat * _head_rms(kcat, m0) * kg).astype(BF16), vcat.astype(BF16))
                kn, vcat = kcache[jk]
                qv = q_ref[rows, cs]
                qn = qv * _head_rms(qv, m0) * qg
                kes = [((2 * jp + e) // group) % 2 if kv_shared else e for e in range(2)]
                hidx = 2 * (pb * ppk + jp)
                qs = _stack_heads(qn, m0, kes).astype(BF16)
                slope = jnp.where(top, slope_ref[hidx], slope_ref[hidx + 1])
                s = jnp.where(valid, _dot_nt(qs, kn) * scale - slope * distf, NEG)
                m = jnp.max(s, axis=-1, keepdims=True)
                if has_sink:
                    sk = jnp.where(top, sink_ref[hidx], sink_ref[hidx + 1])
                    m = jnp.maximum(m, sk)
                p = jnp.exp(s - m)
                den = jnp.sum(p, axis=-1, keepdims=True)
                if has_sink:
                    den = den + jnp.exp(sk - m)
                o_full = _dot((p * (1.0 / den)).astype(BF16), vcat)
                o_ref[rows, cs] = _unstack_heads(o_full, m0, kes)
                l_ref[rows, cs] = _unstack_heads(jnp.broadcast_to(m + jnp.log(den), (2 * BLK, LANES)), m0, [0, 1])

    smem = pl.BlockSpec(memory_space=pltpu.SMEM)
    qw = LANES * ppk
    gspec = pl.BlockSpec((1, LANES), lambda p, i: (0, 0))
    ospec = pl.BlockSpec((RB, qw), lambda p, i: (i, p))
    oshape = jax.ShapeDtypeStruct((T, n_heads * HEAD_DIM), F32)
    args = [slopes] + ([sinks] if has_sink else []) + [qkv] * 5 + [q_gain2, k_gain2]
    return pl.pallas_call(
        body, grid=(npair // ppk, nb),
        in_specs=[smem] * (2 if has_sink else 1) + _band_specs(dil, ppk, q_blk, k_blk, v_blk, kv_shared, nb)
        + [gspec, gspec],
        out_specs=[ospec, ospec], out_shape=[oshape, oshape],
        compiler_params=_cparams(2), name=name)(*args)


def banded_bwd(qkv, q_gain2, k_gain2, slopes, sinks, do, o, lse, w, omix, *, dil, ppk, q_blk, k_blk, v_blk,
               n_heads, group, max_dist, do_blk, name):
    T = qkv.shape[0]
    RB = BLK * dil
    nb = T // RB
    npair = n_heads // 2
    kv_shared = group > 1
    scale = HEAD_DIM ** -0.5
    has_sink = sinks is not None
    mixed = w is not None
    qw = LANES * ppk

    def body(*refs):
        slope_ref = refs[0]
        if has_sink:
            sink_ref, refs = refs[1], refs[2:]
        else:
            refs = refs[1:]
        q_ref, kc_ref, kp_ref, vc_ref, vp_ref, qg_ref, kg_ref, do_ref, o_ref, l_ref = refs[:10]
        refs = refs[10:]
        if mixed:
            w_ref, om_ref, refs = refs[0], refs[1], refs[2:]
        dq_ref, dk_ref, dv_ref, dqg_ref, dkg_ref, dsk_ref, ck_ref, cv_ref = refs
        pb = pl.program_id(0)
        i = pl.program_id(1)
        live = i < nb
        m0 = _lane0()
        lane = lax.broadcasted_iota(jnp.int32, (1, LANES), 1)
        distf, valid, top = _band_masks2(max_dist, i > 0, live)
        livef = live.astype(F32)
        qg = qg_ref[...]
        kg = kg_ref[...]

        def stack_rows(x2):
            return jnp.concatenate([_half_pick(x2, m0, 0), _half_pick(x2, m0, 1)], axis=0)

        @pl.when((pb == 0) & (i == 0))
        def _():
            dqg_ref[...] = jnp.zeros_like(dqg_ref)
            dkg_ref[...] = jnp.zeros_like(dkg_ref)
            dsk_ref[...] = jnp.zeros_like(dsk_ref)

        @pl.when(i == 0)
        def _():
            ck_ref[...] = jnp.zeros_like(ck_ref)
            cv_ref[...] = jnp.zeros_like(cv_ref)

        dqg_acc = jnp.zeros((1, LANES), F32)
        dkg_acc = jnp.zeros((1, LANES), F32)
        dsk_acc = jnp.zeros((1, LANES), F32)
        for r in range(dil):
            rows = _rows(r, dil)
            for jp in range(ppk):
                cs = pl.ds(LANES * jp, LANES)
                ks = pl.ds(0, LANES) if kv_shared else cs
                kcat = jnp.concatenate([kp_ref[rows, ks], kc_ref[rows, ks]], axis=0)
                vcat = jnp.concatenate([vp_ref[rows, ks], vc_ref[rows, ks]], axis=0).astype(BF16)
                rk = _head_rms(kcat, m0)
                kh = kcat * rk
                kn = (kh * kg).astype(BF16)
                qv = q_ref[rows, cs]
                rq = _head_rms(qv, m0)
                qh = qv * rq
                dov = do_ref[rows, cs]
                lv = l_ref[rows, cs]
                if mixed:
                    wv = w_ref[rows, cs]
                    dmix = _half_sum(dov * om_ref[rows, cs], m0)
                    dov = dov * wv
                delta2 = _half_sum(dov * o_ref[rows, cs], m0)
                shift = stack_rows(wv * dmix if mixed else delta2)
                kes = [((2 * jp + e) // group) % 2 if kv_shared else e for e in range(2)]
                hidx = 2 * (pb * ppk + jp)
                qs = _stack_heads(qh * qg, m0, kes).astype(BF16)
                dos = _stack_heads(dov, m0, kes).astype(BF16)
                lse = stack_rows(lv)
                slope = jnp.where(top, slope_ref[hidx], slope_ref[hidx + 1])
                p = jnp.where(valid, jnp.exp(_dot_nt(qs, kn) * scale - slope * distf - lse), 0.0)
                ds = (p * (_dot_nt(dos, vcat) - shift)).astype(BF16)
                dqn = _unstack_heads(_dot(ds, kn), m0, kes) * scale
                dkn = _dot_tn(ds, qs) * scale
                dvv = _dot_tn(p.astype(BF16), dos)
                if has_sink:
                    sk = jnp.where(top, sink_ref[hidx], sink_ref[hidx + 1])
                    contrib = -jnp.exp(sk - lse) * stack_rows(delta2) * livef
                    for e in range(2):
                        tot = jnp.sum(contrib[e * BLK:(e + 1) * BLK], axis=0, keepdims=True)
                        dsk_acc = dsk_acc + jnp.where(lane == (2 * jp + e), tot, 0.0)
                dqg_acc = dqg_acc + jnp.sum(dqn * qh, axis=0, keepdims=True)
                dqh = dqn * qg
                dq_raw = rq * (dqh - qh * (_half_sum(dqh * qh, m0) * (1.0 / HEAD_DIM)))
                dkg_acc = dkg_acc + jnp.sum(dkn * kh, axis=0, keepdims=True)
                dkh = dkn * kg
                dk_raw = rk * (dkh - kh * (_half_sum(dkh * kh, m0) * (1.0 / HEAD_DIM)))

                @pl.when(live)
                def _():
                    dq_ref[rows, cs] = dq_raw

                dk_ref[rows, cs] = ck_ref[rows, cs] + dk_raw[:BLK]
                dv_ref[rows, cs] = cv_ref[rows, cs] + dvv[:BLK]
                ck_ref[rows, cs] = dk_raw[BLK:]
                cv_ref[rows, cs] = dvv[BLK:]
        dqg_ref[...] += dqg_acc
        dkg_ref[...] += dkg_acc
        dsk_ref[...] += dsk_acc

    smem = pl.BlockSpec(memory_space=pltpu.SMEM)
    gspec = pl.BlockSpec((1, LANES), lambda p, i: (0, 0))

    def cur(i):
        return jnp.minimum(i, nb - 1)

    qspec = pl.BlockSpec((RB, qw), lambda p, i: (cur(i), p))
    dospec = pl.BlockSpec((RB, qw), lambda p, i: (cur(i), do_blk + p))
    kvout = pl.BlockSpec((RB, qw), lambda p, i: (jnp.maximum(i - 1, 0), p))
    in_specs = ([smem] * (2 if has_sink else 1) + _band_specs(dil, ppk, q_blk, k_blk, v_blk, kv_shared, nb)
                + [gspec, gspec, dospec, qspec, qspec] + ([qspec, qspec] if mixed else []))
    args = ([slopes] + ([sinks] if has_sink else []) + [qkv] * 5 + [q_gain2, k_gain2, do, o, lse]
            + ([w, omix] if mixed else []))
    full = jax.ShapeDtypeStruct((T, n_heads * HEAD_DIM), F32)
    row = jax.ShapeDtypeStruct((1, LANES), F32)
    return pl.pallas_call(
        body, grid=(npair // ppk, nb + 1), in_specs=in_specs,
        out_specs=[qspec, kvout, kvout, gspec, gspec, gspec],
        out_shape=[full, full, full, row, row, row],
        scratch_shapes=[pltpu.VMEM((RB, qw), F32), pltpu.VMEM((RB, qw), F32)],
        compiler_params=_cparams(2), name=name)(*args)


def mix_fwd(o1, o2, o3, l1, l2, l3, *, tm, name):
    T, C = o1.shape

    def body(o1r, o2r, o3r, l1r, l2r, l3r, o_ref, w1r, w2r, w3r):
        a, b, c = l1r[...], l2r[...], l3r[...]
        m = jnp.maximum(jnp.maximum(a, b), c)
        ea, eb, ec = jnp.exp(a - m), jnp.exp(b - m), jnp.exp(c - m)
        inv = 1.0 / (ea + eb + ec)
        wa, wb, wc = ea * inv, eb * inv, ec * inv
        o_ref[...] = wa * o1r[...] + wb * o2r[...] + wc * o3r[...]
        w1r[...] = wa
        w2r[...] = wb
        w3r[...] = wc

    spec = pl.BlockSpec((tm, C), lambda i: (i, 0))
    shp = jax.ShapeDtypeStruct((T, C), F32)
    return pl.pallas_call(body, grid=(T // tm,), in_specs=[spec] * 6, out_specs=[spec] * 4, out_shape=[shp] * 4,
                          compiler_params=_cparams(1), name=name)(o1, o2, o3, l1, l2, l3)


def assemble_odd(parts, *, tm, name):
    T, C = parts[0][0].shape

    def body(*refs):
        o_ref = refs[9]
        for j in range(3):
            o_ref[:, pl.ds(C * j, C)] = refs[j][...] + refs[3 + j][...] + refs[6 + j][...]

    spec = pl.BlockSpec((tm, C), lambda i: (i, 0))
    flat = [parts[p][j] for p in range(3) for j in range(3)]
    return pl.pallas_call(body, grid=(T // tm,), in_specs=[spec] * 9,
                          out_specs=pl.BlockSpec((tm, 3 * C), lambda i: (i, 0)),
                          out_shape=jax.ShapeDtypeStruct((T, 3 * C), F32),
                          compiler_params=_cparams(1), name=name)(*flat)


def assemble_even(dqa, dka4, dva4, dqb, dkb, dvb, *, tm, name):
    T = dqa.shape[0]
    W = 512

    def body(dqa_r, dka_r, dva_r, dqb_r, dkb_r, dvb_r, o_ref):
        o_ref[:, pl.ds(0, W)] = dqa_r[...]
        ka = dka_r[...]
        va = dva_r[...]
        o_ref[:, pl.ds(512, LANES)] = ka[:, 0:128] + ka[:, 128:256] + ka[:, 256:384] + ka[:, 384:512]
        o_ref[:, pl.ds(640, LANES)] = va[:, 0:128] + va[:, 128:256] + va[:, 256:384] + va[:, 384:512]
        o_ref[:, pl.ds(768, W)] = dqb_r[...]
        o_ref[:, pl.ds(1280, W)] = dkb_r[...]
        o_ref[:, pl.ds(1792, W)] = dvb_r[...]

    spec = pl.BlockSpec((tm, W), lambda i: (i, 0))
    return pl.pallas_call(body, grid=(T // tm,), in_specs=[spec] * 6,
                          out_specs=pl.BlockSpec((tm, 2304), lambda i: (i, 0)),
                          out_shape=jax.ShapeDtypeStruct((T, 2304), F32),
                          compiler_params=_cparams(1), name=name)(dqa, dka4, dva4, dqb, dkb, dvb)


STICK_T = 256
STICK_DEAD = -110.0


def _split_bf16(x):
    hi = x.astype(BF16)
    lo = (x - hi.astype(F32)).astype(BF16)
    return hi, lo


def _stick_logits(qm, kt, scale, diag):
    n = STICK_T
    row = lax.broadcasted_iota(jnp.int32, (n, n), 0)
    col = lax.broadcasted_iota(jnp.int32, (n, n), 1)
    mask = col < row + jnp.where(diag, 0, n)
    z = _dot_nt(qm, kt) * scale
    lneg = -(jnp.maximum(z, 0.0) + jnp.log(1.0 + jnp.exp(-jnp.abs(z))))
    lpos = z + lneg
    lk = jnp.where(mask, lneg, 0.0)
    return mask, lpos, lneg, lk


def _cumsum_mm(x, tri):
    hi, lo = _split_bf16(x)
    return _dot(hi, tri) + _dot(lo, tri)


def stick_fwd(qkv, *, q_blk, k_blk, v_blk, n_pairs, name):
    T = qkv.shape[0]
    n = STICK_T
    nq = T // n
    scale = HEAD_DIM ** -0.5

    def body(q_ref, k_ref, v_ref, o_ref):
        i = pl.program_id(1)
        m0 = _lane0()
        r2 = lax.broadcasted_iota(jnp.int32, (n, n), 0)
        c2 = lax.broadcasted_iota(jnp.int32, (n, n), 1)
        tri_after = (r2 > c2).astype(BF16)
        qv = q_ref[...]
        out = jnp.zeros((n, LANES), F32)
        for e in range(2):
            qm = _mask_half(qv, m0, e).astype(BF16)

            def alive(st):
                t, _, carry = st
                return (t <= i) & (jnp.max(carry) > STICK_DEAD)

            def step(st, e=e, qm=qm):
                t, acc, carry = st
                start = pl.multiple_of((i - t) * n, n)
                kt = k_ref[pl.ds(start, n), :].astype(BF16)
                vt = _mask_half(v_ref[pl.ds(start, n), :], m0, e).astype(BF16)
                mask, lpos, _, lk = _stick_logits(qm, kt, scale, t == 0)
                after = _cumsum_mm(lk, tri_after) + carry
                a = jnp.where(mask, jnp.exp(lpos + after), 0.0)
                acc = acc + _dot(a.astype(BF16), vt)
                carry = carry + jnp.sum(lk, axis=-1, keepdims=True)
                return t + 1, acc, carry

            _, acc, _ = lax.while_loop(alive, step, (jnp.int32(0), jnp.zeros((n, LANES), F32),
                                                     jnp.zeros((n, 1), F32)))
            out = out + acc
        o_ref[...] = out

    return pl.pallas_call(
        body, grid=(n_pairs, nq),
        in_specs=[pl.BlockSpec((n, LANES), lambda p, i: (i, q_blk + p)),
                  pl.BlockSpec((T, LANES), lambda p, i: (0, k_blk + p)),
                  pl.BlockSpec((T, LANES), lambda p, i: (0, v_blk + p))],
        out_specs=pl.BlockSpec((n, LANES), lambda p, i: (i, p)),
        out_shape=jax.ShapeDtypeStruct((T, n_pairs * LANES), F32),
        compiler_params=_cparams(2), name=name)(qkv, qkv, qkv)


def stick_bwd(qkv, do, *, q_blk, k_blk, v_blk, do_blk, n_pairs, name):
    T = qkv.shape[0]
    n = STICK_T
    nq = T // n
    scale = HEAD_DIM ** -0.5

    def body(q_ref, k_ref, v_ref, do_ref, dq_ref, dk_ref, dv_ref):
        i = pl.program_id(1)
        m0 = _lane0()
        r2 = lax.broadcasted_iota(jnp.int32, (n, n), 0)
        c2 = lax.broadcasted_iota(jnp.int32, (n, n), 1)
        tri_after = (r2 > c2).astype(BF16)
        tri_from = (r2 >= c2).astype(BF16)

        @pl.when(i == 0)
        def _():
            dk_ref[...] = jnp.zeros_like(dk_ref)
            dv_ref[...] = jnp.zeros_like(dv_ref)

        qv = q_ref[...]
        dov = do_ref[...]
        dq_out = jnp.zeros((n, LANES), F32)
        for e in range(2):
            qm = _mask_half(qv, m0, e).astype(BF16)
            dom = _mask_half(dov, m0, e).astype(BF16)

            def tile(t, carry, qm=qm, dom=dom):
                start = pl.multiple_of((i - t) * n, n)
                kt_f = k_ref[pl.ds(start, n), :]
                vt = v_ref[pl.ds(start, n), :].astype(BF16)
                mask, lpos, lneg, lk = _stick_logits(qm, kt_f.astype(BF16), scale, t == 0)
                a = jnp.where(mask, jnp.exp(lpos + _cumsum_mm(lk, tri_after) + carry), 0.0)
                g = _dot_nt(dom, vt) * a
                return start, kt_f, mask, lpos, lneg, lk, a, g

            def alive(st):
                t, carry, _ = st
                return (t <= i) & (jnp.max(carry) > STICK_DEAD)

            def scan(st):
                t, carry, gtot = st
                _, _, _, _, _, lk, _, g = tile(t, carry)
                return (t + 1, carry + jnp.sum(lk, axis=-1, keepdims=True),
                        gtot + jnp.sum(g, axis=-1, keepdims=True))

            z1 = jnp.zeros((n, 1), F32)
            n_live, _, gtot = lax.while_loop(alive, scan, (jnp.int32(0), z1, z1))

            def step(t, st, e=e, qm=qm, dom=dom, gtot=gtot):
                dq_acc, carry, gright = st
                start, kt_f, mask, lpos, lneg, lk, a, g = tile(t, carry)
                before = gtot - (_cumsum_mm(g, tri_from) + gright)
                dz = jnp.where(mask, g * jnp.exp(lneg) - before * jnp.exp(lpos), 0.0) * scale
                dzb = dz.astype(BF16)
                dq_acc = dq_acc + _dot(dzb, _mask_half(kt_f, m0, e).astype(BF16))
                dk_ref[pl.ds(start, n), :] += _dot_tn(dzb, qm)
                dv_ref[pl.ds(start, n), :] += _dot_tn(a.astype(BF16), dom)
                return (dq_acc, carry + jnp.sum(lk, axis=-1, keepdims=True),
                        gright + jnp.sum(g, axis=-1, keepdims=True))

            dq_acc, _, _ = lax.fori_loop(0, n_live, step, (jnp.zeros((n, LANES), F32), z1, z1))
            dq_out = dq_out + dq_acc
        dq_ref[...] = dq_out

    tile = pl.BlockSpec((n, LANES), lambda p, i: (i, p))
    whole = pl.BlockSpec((T, LANES), lambda p, i: (0, p))
    shp = jax.ShapeDtypeStruct((T, n_pairs * LANES), F32)
    return pl.pallas_call(
        body, grid=(n_pairs, nq),
        in_specs=[pl.BlockSpec((n, LANES), lambda p, i: (i, q_blk + p)),
                  pl.BlockSpec((T, LANES), lambda p, i: (0, k_blk + p)),
                  pl.BlockSpec((T, LANES), lambda p, i: (0, v_blk + p)),
                  pl.BlockSpec((n, LANES), lambda p, i: (i, do_blk + p))],
        out_specs=[tile, whole, whole], out_shape=[shp, shp, shp],
        compiler_params=_cparams(2), name=name)(qkv, qkv, qkv, do)


def _xnorm(x):
    r = lax.rsqrt(jnp.mean(x * x, axis=-1, keepdims=True) + RMS_EPS)
    return r, x * r


def xattn_fwd(qraw, kvraw, q_gain, k_gain, *, tm, name):
    T = qraw.shape[0]
    scale = X_HEAD_DIM ** -0.5
    W = X_HEADS * X_HEAD_DIM

    def body(q_ref, kv_ref, qg_ref, kg_ref, o_ref):
        for h in range(X_HEADS):
            cs = pl.ds(X_HEAD_DIM * h, X_HEAD_DIM)
            _, qh = _xnorm(q_ref[:, cs])
            _, kh = _xnorm(kv_ref[:, cs])
            qn = (qh * qg_ref[...]).astype(BF16)
            kn = (kh * kg_ref[...]).astype(BF16)
            v = kv_ref[:, pl.ds(W + X_HEAD_DIM * h, X_HEAD_DIM)].astype(BF16)
            s = _dot_nt(qn, kn) * scale
            m = jnp.max(s, axis=-1, keepdims=True)
            p = jnp.exp(s - m)
            p = p / jnp.sum(p, axis=-1, keepdims=True)
            o_ref[:, cs] = _dot(p.astype(BF16), v)

    gspec = pl.BlockSpec((1, X_HEAD_DIM), lambda i: (0, 0))
    return pl.pallas_call(
        body, grid=(T // tm,),
        in_specs=[pl.BlockSpec((tm, W), lambda i: (i, 0)), pl.BlockSpec((MEM_LEN, 2 * W), lambda i: (0, 0)),
                  gspec, gspec],
        out_specs=pl.BlockSpec((tm, W), lambda i: (i, 0)),
        out_shape=jax.ShapeDtypeStruct((T, W), F32),
        compiler_params=_cparams(1), name=name)(qraw, kvraw, q_gain, k_gain)


def xattn_bwd(qraw, kvraw, q_gain, k_gain, do, o, *, tm, name):
    T = qraw.shape[0]
    nt = T // tm
    scale = X_HEAD_DIM ** -0.5
    W = X_HEADS * X_HEAD_DIM

    def body(q_ref, kv_ref, qg_ref, kg_ref, do_ref, o_ref, dq_ref, dkv_ref, dqg_ref, dkg_ref, dkn_ref):
        i = pl.program_id(0)

        @pl.when(i == 0)
        def _():
            dkv_ref[...] = jnp.zeros_like(dkv_ref)
            dkn_ref[...] = jnp.zeros_like(dkn_ref)
            dqg_ref[...] = jnp.zeros_like(dqg_ref)
            dkg_ref[...] = jnp.zeros_like(dkg_ref)

        qg = qg_ref[...]
        kg = kg_ref[...]
        dqg_acc = jnp.zeros((1, X_HEAD_DIM), F32)
        for h in range(X_HEADS):
            cs = pl.ds(X_HEAD_DIM * h, X_HEAD_DIM)
            vs = pl.ds(W + X_HEAD_DIM * h, X_HEAD_DIM)
            rq, qh = _xnorm(q_ref[:, cs])
            _, kh = _xnorm(kv_ref[:, cs])
            qn = (qh * qg).astype(BF16)
            kn = (kh * kg).astype(BF16)
            v = kv_ref[:, vs].astype(BF16)
            s = _dot_nt(qn, kn) * scale
            m = jnp.max(s, axis=-1, keepdims=True)
            p = jnp.exp(s - m)
            p = p / jnp.sum(p, axis=-1, keepdims=True)
            dov = do_ref[:, cs]
            delta = jnp.sum(dov * o_ref[:, cs], axis=-1, keepdims=True)
            dob = dov.astype(BF16)
            ds = (p * (_dot_nt(dob, v) - delta)).astype(BF16)
            dqn = _dot(ds, kn) * scale
            dkn_ref[:, cs] += _dot_tn(ds, qn) * scale
            dkv_ref[:, vs] += _dot_tn(p.astype(BF16), dob)
            dqg_acc = dqg_acc + jnp.sum(dqn * qh, axis=0, keepdims=True)
            dqh = dqn * qg
            dq_ref[:, cs] = rq * (dqh - qh * jnp.mean(dqh * qh, axis=-1, keepdims=True))
        dqg_ref[...] += dqg_acc

        @pl.when(i == nt - 1)
        def _():
            dkg_acc = jnp.zeros((1, X_HEAD_DIM), F32)
            for h in range(X_HEADS):
                cs = pl.ds(X_HEAD_DIM * h, X_HEAD_DIM)
                rk, kh = _xnorm(kv_ref[:, cs])
                dkn = dkn_ref[:, cs]
                dkg_acc = dkg_acc + jnp.sum(dkn * kh, axis=0, keepdims=True)
                dkh = dkn * kg
                dkv_ref[:, cs] = rk * (dkh - kh * jnp.mean(dkh * kh, axis=-1, keepdims=True))
            dkg_ref[...] = dkg_acc

    gspec = pl.BlockSpec((1, X_HEAD_DIM), lambda i: (0, 0))
    tile = pl.BlockSpec((tm, W), lambda i: (i, 0))
    kvspec = pl.BlockSpec((MEM_LEN, 2 * W), lambda i: (0, 0))
    grow = jax.ShapeDtypeStruct((1, X_HEAD_DIM), F32)
    return pl.pallas_call(
        body, grid=(nt,), in_specs=[tile, kvspec, gspec, gspec, tile, tile],
        out_specs=[tile, kvspec, gspec, gspec],
        out_shape=[jax.ShapeDtypeStruct((T, W), F32), jax.ShapeDtypeStruct((MEM_LEN, 2 * W), F32), grow, grow],
        scratch_shapes=[pltpu.VMEM((MEM_LEN, W), F32)],
        compiler_params=_cparams(1), name=name)(qraw, kvraw, q_gain, k_gain, do, o)


_ANY = pl.BlockSpec(memory_space=pl.ANY)


def _my_pos():
    return lax.axis_index("x"), lax.axis_index("y"), lax.axis_index("c")


def _pieces(arrays, chunks):
    out = []
    for a, (arr, n) in enumerate(zip(arrays, chunks)):
        rc = arr.shape[-2] // n
        out += [(a, pl.ds(ch * rc, rc)) for ch in range(n)]
    return out


def gather_blocks(blks, chunks, *, name):
    n = len(blks)
    pieces = _pieces(blks, chunks)
    n_p = len(pieces)

    def body(*refs):
        x_refs, out_refs = refs[:n], refs[n:2 * n]
        send_sems, recv_sems, local_sems = refs[2 * n:]
        x, y, c = _my_pos()
        me, sibling = (x, y, c), (x, y, 1 - c)
        chips = [(1 - x, y), (x, 1 - y), (1 - x, 1 - y)]

        def slot(block, p):
            px, py, pc = block
            a, rows = pieces[p]
            return out_refs[a].at[4 * px + 2 * py + pc, rows]

        def own(p):
            a, rows = pieces[p]
            return x_refs[a].at[rows]

        def copy(k, p, block, to, from_input=False):
            return pltpu.make_async_remote_copy(
                src_ref=own(p) if from_input else slot(block, p), dst_ref=slot(block, p),
                send_sem=send_sems.at[k * n_p + p], recv_sem=recv_sems.at[k * n_p + p],
                device_id=to, device_id_type=MESH)

        ps = range(n_p)
        mine = [pltpu.make_async_copy(own(p), slot(me, p), local_sems.at[p]) for p in ps]
        first = [copy(1 + j, p, me, (*chip, c), from_input=True) for j, chip in enumerate(chips) for p in ps]
        first += [copy(0, p, me, sibling, from_input=True) for p in ps]
        for cp in first + mine:
            cp.start()
        passed = []
        for j, chip in enumerate(chips):
            for p in ps:
                copy(1 + j, p, (*chip, c), me).wait_recv()
                fwd = copy(4 + j, p, (*chip, c), sibling)
                fwd.start()
                passed.append(fwd)
        for p in ps:
            copy(0, p, sibling, me).wait_recv()
        for j, chip in enumerate(chips):
            for p in ps:
                copy(4 + j, p, (*chip, 1 - c), me).wait_recv()
        for cp in first + passed:
            cp.wait_send()
        for cp in mine:
            cp.wait()

    return pl.pallas_call(
        body, out_shape=[jax.ShapeDtypeStruct((N_DEV,) + b.shape, b.dtype) for b in blks],
        in_specs=[_ANY] * n, out_specs=[_ANY] * n,
        scratch_shapes=[pltpu.SemaphoreType.DMA((7 * n_p,)), pltpu.SemaphoreType.DMA((7 * n_p,)),
                        pltpu.SemaphoreType.DMA((n_p,))],
        name=name)(*blks)


def gather_small(small, *, name):
    S, C = small.shape

    def body(s_ref, out_ref, send_sems, recv_sems, local_sem):
        x, y, c = _my_pos()
        my_id = 4 * x + 2 * y + c

        def copy(k, slot):
            px, py, pc = x ^ ((k >> 2) & 1), y ^ ((k >> 1) & 1), c ^ (k & 1)
            dst = my_id if slot == "mine" else 4 * px + 2 * py + pc
            return pltpu.make_async_remote_copy(
                src_ref=s_ref, dst_ref=out_ref.at[dst], send_sem=send_sems.at[k - 1], recv_sem=recv_sems.at[k - 1],
                device_id=(px, py, pc), device_id_type=MESH)

        own = pltpu.make_async_copy(s_ref, out_ref.at[my_id], local_sem)
        own.start()
        sends = [copy(k, "mine") for k in range(1, N_DEV)]
        for cp in sends:
            cp.start()
        for k in range(1, N_DEV):
            copy(k, "theirs").wait_recv()
        for cp in sends:
            cp.wait_send()
        own.wait()

    dma7 = pltpu.SemaphoreType.DMA((7,))
    return pl.pallas_call(
        body, out_shape=jax.ShapeDtypeStruct((N_DEV, S, C), small.dtype), in_specs=[_ANY], out_specs=_ANY,
        scratch_shapes=[dma7, dma7, pltpu.SemaphoreType.DMA], name=name)(small)


def pair_exchange(bigs, chunks, *, name):
    n = len(bigs)
    pieces = _pieces(bigs, chunks)
    n_p = len(pieces)

    def body(*refs):
        big_refs, out_refs = refs[:n], refs[n:2 * n]
        send_sems, recv_sems = refs[2 * n:]
        x, y, c = _my_pos()

        def copy(b, p):
            a, rows = pieces[p]
            return pltpu.make_async_remote_copy(
                src_ref=big_refs[a].at[2 * b + (1 - c), rows], dst_ref=out_refs[a].at[b, rows],
                send_sem=send_sems.at[b * n_p + p], recv_sem=recv_sems.at[b * n_p + p],
                device_id=(x, y, 1 - c), device_id_type=MESH)

        cps = [copy(b, p) for b in range(4) for p in range(n_p)]
        for cp in cps:
            cp.start()
        for cp in cps:
            cp.wait_recv()
        for cp in cps:
            cp.wait_send()

    return pl.pallas_call(
        body, out_shape=[jax.ShapeDtypeStruct((4,) + b.shape[1:], b.dtype) for b in bigs],
        in_specs=[_ANY] * n, out_specs=[_ANY] * n,
        scratch_shapes=[pltpu.SemaphoreType.DMA((4 * n_p,)), pltpu.SemaphoreType.DMA((4 * n_p,))],
        name=name)(*bigs)


def pair_sum(big, sib, c, *, tr, name):
    _, R, C = big.shape

    def body(c_ref, a_ref, s_ref, o_ref):
        o_ref[...] = (a_ref[...].astype(F32) + s_ref[...].astype(F32)).astype(o_ref.dtype)

    grid_spec = pltpu.PrefetchScalarGridSpec(
        num_scalar_prefetch=1, grid=(4, R // tr),
        in_specs=[pl.BlockSpec((None, tr, C), lambda b, i, c_ref: (2 * b + c_ref[0], i, 0)),
                  pl.BlockSpec((None, tr, C), lambda b, i, c_ref: (b, i, 0))],
        out_specs=pl.BlockSpec((None, tr, C), lambda b, i, c_ref: (b, i, 0)))
    return pl.pallas_call(body, grid_spec=grid_spec, out_shape=jax.ShapeDtypeStruct((4, R, C), big.dtype),
                          compiler_params=_cparams(2), name=name)(c.reshape(1).astype(jnp.int32), big, sib)


def chip_scatter(pres, chunks, *, name):
    n = len(pres)
    pieces = _pieces(pres, chunks)
    n_p = len(pieces)

    def body(*refs):
        pre_refs, out_refs = refs[:n], refs[n:2 * n]
        send_sems, recv_sems, local_sems = refs[2 * n:]
        x, y, c = _my_pos()
        my_chip = 2 * x + y
        chips = [(1 - x, y), (x, 1 - y), (1 - x, 1 - y)]

        def copy(j, p, slot):
            px, py = chips[j]
            a, rows = pieces[p]
            src_slot, dst_slot = (2 * px + py, my_chip) if slot == "mine" else (my_chip, 2 * px + py)
            return pltpu.make_async_remote_copy(
                src_ref=pre_refs[a].at[src_slot, rows], dst_ref=out_refs[a].at[dst_slot, rows],
                send_sem=send_sems.at[j * n_p + p], recv_sem=recv_sems.at[j * n_p + p],
                device_id=(px, py, c), device_id_type=MESH)

        own = [pltpu.make_async_copy(pre_refs[a].at[my_chip, rows], out_refs[a].at[my_chip, rows], local_sems.at[p])
               for p, (a, rows) in enumerate(pieces)]
        sends = [copy(j, p, "mine") for j in range(3) for p in range(n_p)]
        for cp in sends + own:
            cp.start()
        for j in range(3):
            for p in range(n_p):
                copy(j, p, "theirs").wait_recv()
        for cp in sends:
            cp.wait_send()
        for cp in own:
            cp.wait()

    return pl.pallas_call(
        body, out_shape=[jax.ShapeDtypeStruct(p.shape, p.dtype) for p in pres],
        in_specs=[_ANY] * n, out_specs=[_ANY] * n,
        scratch_shapes=[pltpu.SemaphoreType.DMA((3 * n_p,)), pltpu.SemaphoreType.DMA((3 * n_p,)),
                        pltpu.SemaphoreType.DMA((n_p,))],
        name=name)(*pres)


def sibling_send(blks, chunks, *, name):
    n = len(blks)
    pieces = _pieces(blks, chunks)
    n_p = len(pieces)

    def body(*refs):
        x_refs, out_refs = refs[:n], refs[n:2 * n]
        send_sems, recv_sems = refs[2 * n:]
        x, y, c = _my_pos()
        cps = [pltpu.make_async_remote_copy(
            src_ref=x_refs[a].at[rows], dst_ref=out_refs[a].at[rows], send_sem=send_sems.at[p],
            recv_sem=recv_sems.at[p], device_id=(x, y, 1 - c), device_id_type=MESH)
            for p, (a, rows) in enumerate(pieces)]
        for cp in cps:
            cp.start()
        for cp in cps:
            cp.wait_recv()
        for cp in cps:
            cp.wait_send()

    return pl.pallas_call(
        body, out_shape=[jax.ShapeDtypeStruct(b.shape, b.dtype) for b in blks],
        in_specs=[_ANY] * n, out_specs=[_ANY] * n,
        scratch_shapes=[pltpu.SemaphoreType.DMA((n_p,)), pltpu.SemaphoreType.DMA((n_p,))],
        name=name)(*blks)


def reduce_slots(land, *, tr, name):
    n, R, C = land.shape

    def body(l_ref, o_ref):
        acc = l_ref[0].astype(F32)
        for s in range(1, n):
            acc = acc + l_ref[s].astype(F32)
        o_ref[...] = acc

    return pl.pallas_call(
        body, grid=(R // tr,), in_specs=[pl.BlockSpec((n, tr, C), lambda i: (0, i, 0))],
        out_specs=pl.BlockSpec((tr, C), lambda i: (i, 0)), out_shape=jax.ShapeDtypeStruct((R, C), F32),
        compiler_params=_cparams(1), name=name)(land)


TM = 512


def _tk(d):
    return min(d.shape[0], 1024)


def ffn_fwd(x, g, wgu, wd, tag):
    gu, h = norm_matmul(x, g, wgu, tm=TM, tn=1408, split=True, name=f"{tag}_gu")
    xo = mm_nn(gu, wd, res=x, scale=0.5, swiglu=True, tm=TM, tn=D_MODEL, tk=1408, name=f"{tag}_down")
    return xo, (x, gu, h)


def ffn_bwd(d, saved, g, wgu, wd, tag):
    x, gu, h = saved
    dgu, act = ffn_bwd_act(d, wd, gu, tm=256, tn=1408, name=f"{tag}_bact")
    dwd = mm_tn(act, d, scale=0.5, a_split=False, b_split=False, tm=1408, tn=D_MODEL, tk=_tk(d), name=f"{tag}_dwd")
    dwgu = mm_tn(h, dgu, scale=1.0, a_split=False, b_split=True, tm=TM, tn=1408, tk=_tk(d), out_blocked=True,
                 name=f"{tag}_dwgu")
    dx, dg = mm_nt_normbwd(dgu, wgu, x, g, d, a_split=True, tm=TM, tk=1408, name=f"{tag}_dx")
    return dx, dg, dwgu, dwd


def _tile2(v):
    return jnp.concatenate([v, v], axis=-1).reshape(1, LANES)


def _fold2(v):
    return v[:, :HEAD_DIM] + v[:, HEAD_DIM:]


EVEN = dict(dil=1, ppk=4, q_blk=0, k_blk=4, v_blk=5, n_heads=A_Q_HEADS, group=A_GROUP, max_dist=A_WINDOW - 1)
STICK = dict(q_blk=6, k_blk=10, v_blk=14, n_pairs=4)


def _odd_cfg(dil):
    return dict(dil=dil, ppk=1, q_blk=0, k_blk=8, v_blk=16, n_heads=C_HEADS, group=1, max_dist=BLK)


def even_fwd(x, g, win, qg, kg, sinks, wout, tag):
    qkv, h = norm_matmul(x, g, win, tm=TM, tn=1152, split=False, name=f"{tag}_in")
    qg2, kg2 = _tile2(qg), _tile2(kg)
    slopes = jnp.asarray(_alibi(A_Q_HEADS), F32)
    oa, lse = banded_fwd(qkv, qg2, kg2, slopes, sinks, name=f"{tag}_swa", **EVEN)
    ob = stick_fwd(qkv, name=f"{tag}_stick", **STICK)
    o = jnp.concatenate([oa, ob], axis=1)
    xo = mm_nn(o, wout, res=x, scale=1.0, swiglu=False, tm=TM, tn=D_MODEL, tk=D_MODEL, name=f"{tag}_out")
    return xo, (x, qkv, h, oa, lse, o)


def even_bwd(d, saved, g, win, qg, kg, sinks, wout, tag):
    x, qkv, h, oa, lse, o = saved
    qg2, kg2 = _tile2(qg), _tile2(kg)
    slopes = jnp.asarray(_alibi(A_Q_HEADS), F32)
    dwout = mm_tn(o, d, scale=1.0, a_split=False, b_split=False, tm=D_MODEL, tn=D_MODEL, tk=_tk(d), name=f"{tag}_dwout")
    do = mm_nt(d, wout, tm=TM, tn=D_MODEL, tk=D_MODEL, name=f"{tag}_do")
    dqa, dka4, dva4, dqg, dkg, dsk = banded_bwd(qkv, qg2, kg2, slopes, sinks, do, oa, lse, None, None,
                                                do_blk=0, name=f"{tag}_swa_b", **EVEN)
    dqb, dkb, dvb = stick_bwd(qkv, do, do_blk=4, name=f"{tag}_stick_b", **STICK)
    dqkv = assemble_even(dqa, dka4, dva4, dqb, dkb, dvb, tm=TM, name=f"{tag}_asm")
    dwin = mm_tn(h, dqkv, scale=1.0, a_split=False, b_split=False, tm=D_MODEL, tn=1152, tk=_tk(d), name=f"{tag}_dwin")
    dx, dg = mm_nt_normbwd(dqkv, win, x, g, d, a_split=False, tm=TM, tk=1152, name=f"{tag}_dx")
    return dx, dg, dwin, _fold2(dqg), _fold2(dkg), dsk[:, :A_Q_HEADS], dwout


def odd_fwd(x, g, win, qg, kg, wout, tag):
    qkv, h = norm_matmul(x, g, win, tm=TM, tn=768, split=False, name=f"{tag}_in")
    qg2, kg2 = _tile2(qg), _tile2(kg)
    outs = []
    for p, (window, dil) in enumerate(C_PATTERNS):
        slopes = jnp.asarray(_alibi(C_HEADS), F32) * float(dil)
        outs.append(banded_fwd(qkv, qg2, kg2, slopes, None, name=f"{tag}_dil{p}", **_odd_cfg(dil)))
    o, w1, w2, w3 = mix_fwd(outs[0][0], outs[1][0], outs[2][0], outs[0][1], outs[1][1], outs[2][1],
                            tm=TM, name=f"{tag}_mix")
    xo = mm_nn(o, wout, res=x, scale=1.0, swiglu=False, tm=TM, tn=D_MODEL, tk=D_MODEL, name=f"{tag}_out")
    return xo, (x, qkv, h, outs, (w1, w2, w3), o)


def odd_bwd(d, saved, g, win, qg, kg, wout, tag):
    x, qkv, h, outs, ws, o = saved
    qg2, kg2 = _tile2(qg), _tile2(kg)
    dwout = mm_tn(o, d, scale=1.0, a_split=False, b_split=False, tm=D_MODEL, tn=D_MODEL, tk=_tk(d), name=f"{tag}_dwout")
    do = mm_nt(d, wout, tm=TM, tn=D_MODEL, tk=D_MODEL, name=f"{tag}_do")
    parts, dqg, dkg = [], 0.0, 0.0
    for p, (window, dil) in enumerate(C_PATTERNS):
        slopes = jnp.asarray(_alibi(C_HEADS), F32) * float(dil)
        dq, dk, dv, dqg_p, dkg_p, _ = banded_bwd(qkv, qg2, kg2, slopes, None, do, outs[p][0], outs[p][1], ws[p], o,
                                                 do_blk=0, name=f"{tag}_dil{p}_b", **_odd_cfg(dil))
        parts.append((dq, dk, dv))
        dqg = dqg + dqg_p
        dkg = dkg + dkg_p
    dqkv = assemble_odd(parts, tm=TM, name=f"{tag}_asm")
    dwin = mm_tn(h, dqkv, scale=1.0, a_split=False, b_split=False, tm=TM, tn=768, tk=_tk(d), out_blocked=True,
                 name=f"{tag}_dwin")
    dx, dg = mm_nt_normbwd(dqkv, win, x, g, d, a_split=False, tm=TM, tk=768, name=f"{tag}_dx")
    return dx, dg, dwin, _fold2(dqg), _fold2(dkg), dwout


def xa_fwd(x, mem, g, gm, wq, wkv, qg, kg, wo, tag):
    qraw, h = norm_matmul(x, g, wq, tm=TM, tn=D_MODEL, split=False, name=f"{tag}_q")
    kvraw, hm = norm_matmul(mem, gm, wkv, tm=MEM_LEN, tn=512, split=False, name=f"{tag}_kv")
    o = xattn_fwd(qraw, kvraw, qg, kg, tm=TM, name=f"{tag}_att")
    xo = mm_nn(o, wo, res=x, scale=1.0, swiglu=False, tm=TM, tn=D_MODEL, tk=D_MODEL, name=f"{tag}_o")
    return xo, (x, qraw, h, kvraw, hm, o)


def xa_bwd(d, saved, mem, g, gm, wq, wkv, qg, kg, wo, tag):
    x, qraw, h, kvraw, hm, o = saved
    dwo = mm_tn(o, d, scale=1.0, a_split=False, b_split=False, tm=D_MODEL, tn=D_MODEL, tk=_tk(d), name=f"{tag}_dwo")
    do = mm_nt(d, wo, tm=TM, tn=D_MODEL, tk=D_MODEL, name=f"{tag}_do")
    dq, dkv, dqg, dkg = xattn_bwd(qraw, kvraw, qg, kg, do, o, tm=TM, name=f"{tag}_att_b")
    dwq = mm_tn(h, dq, scale=1.0, a_split=False, b_split=False, tm=D_MODEL, tn=D_MODEL, tk=_tk(d), name=f"{tag}_dwq")
    dx, dg = mm_nt_normbwd(dq, wq, x, g, d, a_split=False, tm=TM, tk=D_MODEL, name=f"{tag}_dx")
    dwkv = mm_tn(hm, dkv, scale=1.0, a_split=False, b_split=False, tm=TM, tn=512, tk=MEM_LEN, out_blocked=True,
                 name=f"{tag}_dwkv")
    _, dgm = mm_nt_normbwd(dkv, wkv, mem, gm, None, a_split=False, tm=MEM_LEN, tk=512, name=f"{tag}_dmem")
    return dx, dg, dgm, dwq, dwkv, dqg, dkg, dwo


MATS = (("ffn1_w_gu", 1), ("ffn1_w_down", 0), ("ev_w_in", 1), ("ev_w_out", 0), ("od_w_in", 1), ("od_w_out", 0),
        ("xa_w_q", 0), ("xa_w_kv", 1), ("xa_w_o", 0), ("ffn2_w_gu", 1), ("ffn2_w_down", 0))
SMALLS = ("ffn1_norm", "mix_norm", "ev_q_gain", "ev_k_gain", "ev_sinks", "od_q_gain", "od_k_gain", "xa_norm",
          "xa_mem_norm", "xa_q_gain", "xa_k_gain", "ffn2_norm")
WEIGHTS = ("ffn1_norm", "ffn1_w_gu", "ffn1_w_down", "mix_norm", "ev_w_in", "ev_q_gain", "ev_k_gain", "ev_sinks",
           "ev_w_out", "od_w_in", "od_q_gain", "od_k_gain", "od_w_out", "xa_norm", "xa_mem_norm", "xa_w_q",
           "xa_w_kv", "xa_q_gain", "xa_k_gain", "xa_w_o", "ffn2_norm", "ffn2_w_gu", "ffn2_w_down")
SMALL_ROWS = 16
GROUPS = (
    ((("ffn1_w_gu", 0), ("ffn1_w_gu", 1), ("ffn2_w_gu", 0), ("ffn2_w_gu", 1)), 8, 512),
    ((("ffn1_w_down", 0), ("ffn1_w_down", 1), ("ffn2_w_down", 0), ("ffn2_w_down", 1)), 4, 352),
    ((("ev_w_out", 0), ("od_w_out", 0), ("xa_w_q", 0), ("xa_w_q", 1), ("xa_w_o", 0), ("xa_w_o", 1)), 2, 384),
    ((("xa_w_kv", 0), ("xa_w_kv", 1)), 2, 512),
    ((("ev_w_in", 0),), 1, 512),
    ((("od_w_in", 0),), 1, 512),
)
GROUP_CHUNKS = tuple(g[1] for g in GROUPS)
COL_SHARDED = {name for name, axis in MATS if axis == 1}
BLOCKED = {"ffn1_w_gu", "ffn2_w_gu", "xa_w_kv", "od_w_in"}


def group_halves(shards, c):
    out = []
    for members, _, _ in GROUPS:
        halves = []
        for name, layer in members:
            _, r, cc = shards[name].shape
            half = lax.dynamic_index_in_dim(shards[name][layer].reshape(2, r // 2, cc), c, 0, keepdims=False)
            halves.append(half.astype(BF16))
        out.append(jnp.concatenate(halves, axis=0))
    return out


def full_weights(gathered, shards):
    full = {}
    for (members, _, _), arr in zip(GROUPS, gathered):
        for w, (name, layer) in enumerate(members):
            _, r, cc = shards[name].shape
            piece = arr[:, w * (r // 2):(w + 1) * (r // 2)].reshape(4, r, cc)
            if name not in COL_SHARDED:
                piece = piece.reshape(4 * r, cc)
            elif name not in BLOCKED:
                piece = piece.transpose(1, 0, 2).reshape(r, 4 * cc)
            full[(name, layer)] = piece
    return full


def group_grads(grads, shards):
    out = []
    for members, _, _ in GROUPS:
        parts = []
        for name, layer in members:
            _, r, cc = shards[name].shape
            gfull = grads[(name, layer)]
            if name in COL_SHARDED and name not in BLOCKED:
                gfull = gfull.reshape(2, r // 2, 4, cc).transpose(2, 0, 1, 3)
            parts.append(gfull.reshape(N_DEV, r // 2, cc))
        out.append(jnp.concatenate(parts, axis=1))
    return out


def shard_grads(mine, theirs, c, shards):
    per = {}
    for (members, _, _), a, b in zip(GROUPS, mine, theirs):
        for w, (name, layer) in enumerate(members):
            _, r, cc = shards[name].shape
            rows = slice(w * (r // 2), (w + 1) * (r // 2))
            lo = jnp.where(c == 0, a[rows], b[rows])
            hi = jnp.where(c == 0, b[rows], a[rows])
            per.setdefault(name, []).append(jnp.concatenate([lo, hi], axis=0))
    return {name: jnp.stack(lst, axis=0) for name, lst in per.items()}


def pack_small(vals):
    row10 = jnp.concatenate([vals["xa_q_gain"].reshape(1, 512), vals["xa_k_gain"].reshape(1, 512)], axis=1)
    row11 = jnp.concatenate([vals["ev_q_gain"], vals["ev_k_gain"], vals["od_q_gain"], vals["od_k_gain"],
                             vals["ev_sinks"], jnp.zeros((1, 1024 - 4 * 64 - 8), F32)], axis=1)
    return jnp.concatenate([vals["ffn1_norm"], vals["mix_norm"], vals["xa_norm"], vals["xa_mem_norm"],
                            vals["ffn2_norm"], row10, row11, jnp.zeros((SMALL_ROWS - 12, 1024), F32)], axis=0)


def unpack_small(arr):
    return {"ffn1_norm": arr[0:2], "mix_norm": arr[2:4], "xa_norm": arr[4:6], "xa_mem_norm": arr[6:8],
            "ffn2_norm": arr[8:10],
            "xa_q_gain": arr[10:11, 0:512].reshape(2, 256), "xa_k_gain": arr[10:11, 512:1024].reshape(2, 256),
            "ev_q_gain": arr[11:12, 0:64], "ev_k_gain": arr[11:12, 64:128], "od_q_gain": arr[11:12, 128:192],
            "od_k_gain": arr[11:12, 192:256], "ev_sinks": arr[11:12, 256:264]}


def local_step(x, mem, target, W, small):
    depth = small["ffn1_norm"].shape[0]

    def row(name, l):
        return small[name][l:l + 1]

    saved = []
    for l in range(depth):
        j = l // 2
        x, s1 = ffn_fwd(x, row("ffn1_norm", l), W[("ffn1_w_gu", l)], W[("ffn1_w_down", l)], f"l{l}_f1")
        if l % 2 == 0:
            x, s2 = even_fwd(x, row("mix_norm", l), W[("ev_w_in", j)], row("ev_q_gain", j), row("ev_k_gain", j),
                             small["ev_sinks"][j], W[("ev_w_out", j)], f"l{l}_ev")
        else:
            x, s2 = odd_fwd(x, row("mix_norm", l), W[("od_w_in", j)], row("od_q_gain", j), row("od_k_gain", j),
                            W[("od_w_out", j)], f"l{l}_od")
        x, s3 = xa_fwd(x, mem, row("xa_norm", l), row("xa_mem_norm", l), W[("xa_w_q", l)], W[("xa_w_kv", l)],
                       row("xa_q_gain", l), row("xa_k_gain", l), W[("xa_w_o", l)], f"l{l}_xa")
        x, s4 = ffn_fwd(x, row("ffn2_norm", l), W[("ffn2_w_gu", l)], W[("ffn2_w_down", l)], f"l{l}_f2")
        saved.append((s1, s2, s3, s4))
    loss, d = loss_kernel(x, target, tm=TM, name="loss")

    gw = {}
    gs = {name: [None] * small[name].shape[0] for name in SMALLS}
    for l in reversed(range(depth)):
        j = l // 2
        s1, s2, s3, s4 = saved[l]
        d, dg, dwgu, dwd = ffn_bwd(d, s4, row("ffn2_norm", l), W[("ffn2_w_gu", l)], W[("ffn2_w_down", l)], f"l{l}_f2")
        gs["ffn2_norm"][l] = dg
        gw[("ffn2_w_gu", l)], gw[("ffn2_w_down", l)] = dwgu, dwd
        d, dg, dgm, dwq, dwkv, dqg, dkg, dwo = xa_bwd(
            d, s3, mem, row("xa_norm", l), row("xa_mem_norm", l), W[("xa_w_q", l)], W[("xa_w_kv", l)],
            row("xa_q_gain", l), row("xa_k_gain", l), W[("xa_w_o", l)], f"l{l}_xa")
        gs["xa_norm"][l], gs["xa_mem_norm"][l], gs["xa_q_gain"][l], gs["xa_k_gain"][l] = dg, dgm, dqg, dkg
        gw[("xa_w_q", l)], gw[("xa_w_kv", l)], gw[("xa_w_o", l)] = dwq, dwkv, dwo
        if l % 2 == 0:
            d, dg, dwin, dqg, dkg, dsk, dwout = even_bwd(
                d, s2, row("mix_norm", l), W[("ev_w_in", j)], row("ev_q_gain", j), row("ev_k_gain", j),
                small["ev_sinks"][j], W[("ev_w_out", j)], f"l{l}_ev")
            gs["ev_q_gain"][j], gs["ev_k_gain"][j], gs["ev_sinks"][j] = dqg, dkg, dsk
            gw[("ev_w_in", j)], gw[("ev_w_out", j)] = dwin, dwout
        else:
            d, dg, dwin, dqg, dkg, dwout = odd_bwd(
                d, s2, row("mix_norm", l), W[("od_w_in", j)], row("od_q_gain", j), row("od_k_gain", j),
                W[("od_w_out", j)], f"l{l}_od")
            gs["od_q_gain"][j], gs["od_k_gain"][j] = dqg, dkg
            gw[("od_w_in", j)], gw[("od_w_out", j)] = dwin, dwout
        gs["mix_norm"][l] = dg
        d, dg, dwgu, dwd = ffn_bwd(d, s1, row("ffn1_norm", l), W[("ffn1_w_gu", l)], W[("ffn1_w_down", l)], f"l{l}_f1")
        gs["ffn1_norm"][l] = dg
        gw[("ffn1_w_gu", l)], gw[("ffn1_w_down", l)] = dwgu, dwd
    gsmall = {name: jnp.concatenate(v, axis=0) for name, v in gs.items()}
    return loss, d, gw, gsmall


def kernel(x, mem, ffn1_norm, ffn1_w_gu, ffn1_w_down, mix_norm, ev_w_in, ev_q_gain, ev_k_gain, ev_sinks, ev_w_out, od_w_in, od_q_gain, od_k_gain, od_w_out, xa_norm, xa_mem_norm, xa_w_q, xa_w_kv, xa_q_gain, xa_k_gain, xa_w_o, ffn2_norm, ffn2_w_gu, ffn2_w_down, loss_target, m_ffn1_norm, m_ffn1_w_gu, m_ffn1_w_down, m_mix_norm, m_ev_w_in, m_ev_q_gain, m_ev_k_gain, m_ev_sinks, m_ev_w_out, m_od_w_in, m_od_q_gain, m_od_k_gain, m_od_w_out, m_xa_norm, m_xa_mem_norm, m_xa_w_q, m_xa_w_kv, m_xa_q_gain, m_xa_k_gain, m_xa_w_o, m_ffn2_norm, m_ffn2_w_gu, m_ffn2_w_down, v_ffn1_norm, v_ffn1_w_gu, v_ffn1_w_down, v_mix_norm, v_ev_w_in, v_ev_q_gain, v_ev_k_gain, v_ev_sinks, v_ev_w_out, v_od_w_in, v_od_q_gain, v_od_k_gain, v_od_w_out, v_xa_norm, v_xa_mem_norm, v_xa_w_q, v_xa_w_kv, v_xa_q_gain, v_xa_k_gain, v_xa_w_o, v_ffn2_norm, v_ffn2_w_gu, v_ffn2_w_down):
    given = dict(locals())
    w = {n: given[n] for n in WEIGHTS}
    m = {n: given["m_" + n] for n in WEIGHTS}
    v = {n: given["v_" + n] for n in WEIGHTS}
    c = lax.axis_index("c")
    shards = {name: w[name] for name, _ in MATS}
    small = {n: w[n] for n in SMALLS}

    gathered = gather_blocks(group_halves(shards, c), GROUP_CHUNKS, name="gather_weights")
    full = full_weights(gathered, shards)
    loss_b, grad_x, gw, gsmall = local_step(x[0], mem[0], loss_target[0], full, small)

    packed = group_grads(gw, shards)
    sib = pair_exchange(packed, GROUP_CHUNKS, name="pair_grads")
    pre = [pair_sum(p, s, c, tr=g[2], name=f"pair_sum{i}") for i, (g, p, s) in enumerate(zip(GROUPS, packed, sib))]
    land = chip_scatter(pre, GROUP_CHUNKS, name="scatter_grads")
    mine = [reduce_slots(a, tr=g[2], name=f"sum_grads{i}") for i, (g, a) in enumerate(zip(GROUPS, land))]
    theirs = sibling_send(mine, GROUP_CHUNKS, name="swap_grads")
    land_small = gather_small(pack_small(gsmall), name="gather_small")
    g_small = unpack_small(reduce_slots(land_small, tr=SMALL_ROWS, name="sum_small"))
    g = dict(shard_grads(mine, theirs, c, shards))
    g.update(g_small)

    delta, new_m, new_v = {}, {}, {}
    for name, _ in MATS:
        shp = w[name].shape
        flat = [a.reshape(-1, shp[-1]) for a in (w[name], g[name], m[name], v[name])]
        dl, nm, nv = adamw(*flat, br=BLK, name=f"adamw_{name}")
        delta[name], new_m[name], new_v[name] = dl.reshape(shp), nm.reshape(shp), nv.reshape(shp)
    dl, nm, nv = adamw(pack_small(small), pack_small(g_small), pack_small({n: m[n] for n in SMALLS}),
                       pack_small({n: v[n] for n in SMALLS}), br=SMALL_ROWS, name="adamw_small")
    for dst, arr in ((delta, dl), (new_m, nm), (new_v, nv)):
        dst.update(unpack_small(arr))

    loss = lax.psum(loss_b[0, 0], ("x", "y", "c"))
    return (loss, grad_x[None], *[g[n] for n in WEIGHTS], *[delta[n] for n in WEIGHTS],
            *[new_m[n] for n in WEIGHTS], *[new_v[n] for n in WEIGHTS])
```

```python
import jax
import jax.numpy as jnp
from jax import lax
from jax.experimental import pallas as pl
from jax.experimental.pallas import tpu as pltpu

F32 = jnp.float32
BF16 = jnp.bfloat16

D_MODEL = 1024
HEAD_DIM = 64
LANES = 128
BLK = 128
D_FF = 2816
RMS_EPS = 1e-6
MEM_LEN = 256
X_HEADS = 4
X_HEAD_DIM = 256
A_Q_HEADS = 8
A_GROUP = 4
A_WINDOW = 128
C_HEADS = 16
C_PATTERNS = ((128, 1), (512, 4), (2048, 16))
NEG = -1e30
VMEM_LIMIT = 56 * 2 ** 20

ADAM_LR = 0.001
ADAM_B1 = 0.9
ADAM_B2 = 0.999
ADAM_EPS = 1e-08
ADAM_WD = 0.01
ADAM_STEP = 10

N_DEV = 8
MESH = pl.DeviceIdType.MESH


def _cparams(n):
    return pltpu.CompilerParams(dimension_semantics=("arbitrary",) * n, vmem_limit_bytes=VMEM_LIMIT)


def _dot(a, b):
    return jnp.dot(a, b, preferred_element_type=F32)


def _dot_nt(a, b):
    return lax.dot_general(a, b, (((1,), (1,)), ((), ())), preferred_element_type=F32)


def _dot_tn(a, b):
    return lax.dot_general(a, b, (((0,), (0,)), ((), ())), preferred_element_type=F32)


def _sigmoid(z):
    return 1.0 / (1.0 + jnp.exp(-z))


def norm_matmul(x, g, w, *, tm, tn, split, name):
    T, K = x.shape
    blocked = w.ndim == 3
    assert not blocked or w.shape[2] == tn
    N = w.shape[0] * w.shape[2] if blocked else w.shape[1]
    nj = N // tn

    def body(x_ref, g_ref, w_ref, o_ref, h_ref):
        @pl.when(pl.program_id(1) == 0)
        def _():
            xv = x_ref[...]
            r = lax.rsqrt(jnp.mean(xv * xv, axis=-1, keepdims=True) + RMS_EPS)
            h_ref[...] = (xv * r * g_ref[...]).astype(BF16)

        o_ref[...] = _dot(h_ref[...], w_ref[...])

    if split:
        njh = nj // 2
        o_shape = jax.ShapeDtypeStruct((2, T, N // 2), F32)
        o_spec = pl.BlockSpec((None, tm, tn), lambda i, j: (j // njh, i, j % njh))
    else:
        o_shape = jax.ShapeDtypeStruct((T, N), F32)
        o_spec = pl.BlockSpec((tm, tn), lambda i, j: (i, j))
    return pl.pallas_call(
        body, grid=(T // tm, nj),
        in_specs=[pl.BlockSpec((tm, K), lambda i, j: (i, 0)),
                  pl.BlockSpec((1, K), lambda i, j: (0, 0)),
                  (pl.BlockSpec((None, K, tn), lambda i, j: (j, 0, 0)) if blocked
                   else pl.BlockSpec((K, tn), lambda i, j: (0, j)))],
        out_specs=[o_spec, pl.BlockSpec((tm, K), lambda i, j: (i, 0))],
        out_shape=[o_shape, jax.ShapeDtypeStruct((T, K), BF16)],
        compiler_params=_cparams(2), name=name)(x, g, w)


def mm_nn(a, b, *, res, scale, swiglu, tm, tn, tk, name):
    T = a.shape[-2]
    K, N = b.shape
    nk = K // tk

    def body(*refs):
        if swiglu:
            g_ref, u_ref, b_ref, r_ref, o_ref, acc = refs
        else:
            a_ref, b_ref, r_ref, o_ref, acc = refs
        k = pl.program_id(2)

        @pl.when(k == 0)
        def _():
            acc[...] = jnp.zeros_like(acc)

        if swiglu:
            gv = g_ref[...]
            av = (gv * _sigmoid(gv) * u_ref[...]).astype(BF16)
        else:
            av = a_ref[...].astype(BF16)
        acc[...] += _dot(av, b_ref[...])

        @pl.when(k == nk - 1)
        def _():
            o_ref[...] = r_ref[...] + scale * acc[...]

    if swiglu:
        a_specs = [pl.BlockSpec((None, tm, tk), lambda i, j, k: (0, i, k)),
                   pl.BlockSpec((None, tm, tk), lambda i, j, k: (1, i, k))]
        a_args = [a, a]
    else:
        a_specs = [pl.BlockSpec((tm, tk), lambda i, j, k: (i, k))]
        a_args = [a]
    return pl.pallas_call(
        body, grid=(T // tm, N // tn, nk),
        in_specs=a_specs + [pl.BlockSpec((tk, tn), lambda i, j, k: (k, j)),
                            pl.BlockSpec((tm, tn), lambda i, j, k: (i, j))],
        out_specs=pl.BlockSpec((tm, tn), lambda i, j, k: (i, j)),
        out_shape=jax.ShapeDtypeStruct((T, N), F32),
        scratch_shapes=[pltpu.VMEM((tm, tn), F32)],
        compiler_params=_cparams(3), name=name)(*a_args, b, res)


def mm_nt(a, b, *, tm, tn, tk, name):
    T, K = a.shape
    N = b.shape[0]
    nk = K // tk

    def body(a_ref, b_ref, o_ref, acc):
        k = pl.program_id(2)

        @pl.when(k == 0)
        def _():
            acc[...] = jnp.zeros_like(acc)

        acc[...] += _dot_nt(a_ref[...].astype(BF16), b_ref[...])

        @pl.when(k == nk - 1)
        def _():
            o_ref[...] = acc[...]

    return pl.pallas_call(
        body, grid=(T // tm, N // tn, nk),
        in_specs=[pl.BlockSpec((tm, tk), lambda i, j, k: (i, k)),
                  pl.BlockSpec((tn, tk), lambda i, j, k: (j, k))],
        out_specs=pl.BlockSpec((tm, tn), lambda i, j, k: (i, j)),
        out_shape=jax.ShapeDtypeStruct((T, N), F32),
        scratch_shapes=[pltpu.VMEM((tm, tn), F32)],
        compiler_params=_cparams(3), name=name)(a, b)


def ffn_bwd_act(d, wd, gu, *, tm, tn, name):
    T, K = d.shape
    Fd = wd.shape[0]

    def body(d_ref, w_ref, g_ref, u_ref, dgu_ref, act_ref):
        da = 0.5 * _dot_nt(d_ref[...].astype(BF16), w_ref[...])
        gv = g_ref[...]
        uv = u_ref[...]
        s = _sigmoid(gv)
        silu = gv * s
        act_ref[...] = (silu * uv).astype(BF16)
        dgu_ref[0] = (da * uv * (s * (1.0 + gv * (1.0 - s)))).astype(BF16)
        dgu_ref[1] = (da * silu).astype(BF16)

    return pl.pallas_call(
        body, grid=(Fd // tn, T // tm),
        in_specs=[pl.BlockSpec((tm, K), lambda j, i: (i, 0)),
                  pl.BlockSpec((tn, K), lambda j, i: (j, 0)),
                  pl.BlockSpec((None, tm, tn), lambda j, i: (0, i, j)),
                  pl.BlockSpec((None, tm, tn), lambda j, i: (1, i, j))],
        out_specs=[pl.BlockSpec((2, tm, tn), lambda j, i: (0, i, j)),
                   pl.BlockSpec((tm, tn), lambda j, i: (i, j))],
        out_shape=[jax.ShapeDtypeStruct((2, T, Fd), BF16), jax.ShapeDtypeStruct((T, Fd), BF16)],
        compiler_params=_cparams(2), name=name)(d, wd, gu, gu)


def mm_nt_normbwd(a, b, x, g, res, *, a_split, tm, tk, name):
    T, Dm = x.shape
    blocked = b.ndim == 3
    assert not blocked or b.shape[2] == tk
    K = b.shape[0] * b.shape[2] if blocked else b.shape[1]
    nk = K // tk
    nkh = nk // 2
    has_res = res is not None

    def body(*refs):
        if has_res:
            a_ref, b_ref, x_ref, g_ref, r_ref, dx_ref, dg_ref, acc = refs
        else:
            a_ref, b_ref, x_ref, g_ref, dx_ref, dg_ref, acc = refs
        i = pl.program_id(0)
        k = pl.program_id(1)

        @pl.when(k == 0)
        def _():
            acc[...] = jnp.zeros_like(acc)

        acc[...] += _dot_nt(a_ref[...].astype(BF16), b_ref[...])

        @pl.when(k == nk - 1)
        def _():
            xv = x_ref[...]
            r = lax.rsqrt(jnp.mean(xv * xv, axis=-1, keepdims=True) + RMS_EPS)
            xh = xv * r
            dh = acc[...]
            dxh = dh * g_ref[...]
            dx = r * (dxh - xh * jnp.mean(dxh * xh, axis=-1, keepdims=True))
            if has_res:
                dx = dx + r_ref[...]
            dx_ref[...] = dx
            part = jnp.sum(dh * xh, axis=0, keepdims=True)

            @pl.when(i == 0)
            def _():
                dg_ref[...] = part

            @pl.when(i > 0)
            def _():
                dg_ref[...] += part

    if a_split:
        a_spec = pl.BlockSpec((None, tm, tk), lambda i, k: (k // nkh, i, k % nkh))
    else:
        a_spec = pl.BlockSpec((tm, tk), lambda i, k: (i, k))
    in_specs = [a_spec,
                (pl.BlockSpec((None, Dm, tk), lambda i, k: (k, 0, 0)) if blocked
                 else pl.BlockSpec((Dm, tk), lambda i, k: (0, k))),
                pl.BlockSpec((tm, Dm), lambda i, k: (i, 0)),
                pl.BlockSpec((1, Dm), lambda i, k: (0, 0))]
    args = [a, b, x, g]
    if has_res:
        in_specs.append(pl.BlockSpec((tm, Dm), lambda i, k: (i, 0)))
        args.append(res)
    return pl.pallas_call(
        body, grid=(T // tm, nk), in_specs=in_specs,
        out_specs=[pl.BlockSpec((tm, Dm), lambda i, k: (i, 0)),
                   pl.BlockSpec((1, Dm), lambda i, k: (0, 0))],
        out_shape=[jax.ShapeDtypeStruct((T, Dm), F32), jax.ShapeDtypeStruct((1, Dm), F32)],
        scratch_shapes=[pltpu.VMEM((tm, Dm), F32)],
        compiler_params=_cparams(2), name=name)(*args)


def mm_tn(a, b, *, scale, a_split, b_split, tm, tn, tk, name, out_blocked=False):
    T = a.shape[-2]
    M = a.shape[-1] * (2 if a_split else 1)
    N = b.shape[-1] * (2 if b_split else 1)
    ni, nj, nk = M // tm, N // tn, T // tk
    nih, njh = ni // 2, nj // 2

    def body(a_ref, b_ref, o_ref, acc):
        k = pl.program_id(2)

        @pl.when(k == 0)
        def _():
            acc[...] = jnp.zeros_like(acc)

        acc[...] += _dot_tn(a_ref[...].astype(BF16), b_ref[...].astype(BF16))

        @pl.when(k == nk - 1)
        def _():
            o_ref[...] = (acc[...] * scale).astype(o_ref.dtype)

    if a_split:
        a_spec = pl.BlockSpec((None, tk, tm), lambda i, j, k: (i // nih, k, i % nih))
    else:
        a_spec = pl.BlockSpec((tk, tm), lambda i, j, k: (k, i))
    if b_split:
        b_spec = pl.BlockSpec((None, tk, tn), lambda i, j, k: (j // njh, k, j % njh))
    else:
        b_spec = pl.BlockSpec((tk, tn), lambda i, j, k: (k, j))
    if out_blocked:
        o_spec = pl.BlockSpec((None, None, tm, tn), lambda i, j, k: (j, i, 0, 0))
        o_shape = jax.ShapeDtypeStruct((nj, ni, tm, tn), BF16)
    else:
        o_spec = pl.BlockSpec((tm, tn), lambda i, j, k: (i, j))
        o_shape = jax.ShapeDtypeStruct((M, N), BF16)
    return pl.pallas_call(
        body, grid=(ni, nj, nk), in_specs=[a_spec, b_spec], out_specs=o_spec, out_shape=o_shape,
        scratch_shapes=[pltpu.VMEM((tm, tn), F32)],
        compiler_params=_cparams(3), name=name)(a, b)


def loss_kernel(y, target, *, tm, name):
    T, Dm = y.shape

    def body(y_ref, t_ref, l_ref, dy_ref):
        e = y_ref[...] - t_ref[...]
        dy_ref[...] = e * (1.0 / Dm)
        part = (0.5 / Dm) * jnp.sum(jnp.sum(e * e, axis=-1, keepdims=True), axis=0, keepdims=True)
        part = jnp.broadcast_to(part, (8, LANES))

        @pl.when(pl.program_id(0) == 0)
        def _():
            l_ref[...] = part

        @pl.when(pl.program_id(0) > 0)
        def _():
            l_ref[...] += part

    return pl.pallas_call(
        body, grid=(T // tm,),
        in_specs=[pl.BlockSpec((tm, Dm), lambda i: (i, 0)), pl.BlockSpec((tm, Dm), lambda i: (i, 0))],
        out_specs=[pl.BlockSpec((8, LANES), lambda i: (0, 0)), pl.BlockSpec((tm, Dm), lambda i: (i, 0))],
        out_shape=[jax.ShapeDtypeStruct((8, LANES), F32), jax.ShapeDtypeStruct((T, Dm), F32)],
        compiler_params=_cparams(1), name=name)(y, target)


def adamw(w, g, m, v, *, br, name):
    R, C = w.shape

    def body(w_ref, g_ref, m_ref, v_ref, d_ref, nm_ref, nv_ref):
        gv = g_ref[...]
        nm = ADAM_B1 * m_ref[...] + (1.0 - ADAM_B1) * gv
        nv = ADAM_B2 * v_ref[...] + (1.0 - ADAM_B2) * (gv * gv)
        m_hat = nm / (1.0 - ADAM_B1 ** ADAM_STEP)
        v_hat = nv / (1.0 - ADAM_B2 ** ADAM_STEP)
        d_ref[...] = -ADAM_LR * (m_hat / (jnp.sqrt(v_hat) + ADAM_EPS) + ADAM_WD * w_ref[...])
        nm_ref[...] = nm
        nv_ref[...] = nv

    spec = pl.BlockSpec((br, C), lambda i: (i, 0))
    shp = jax.ShapeDtypeStruct((R, C), F32)
    return pl.pallas_call(
        body, grid=(R // br,), in_specs=[spec] * 4, out_specs=[spec] * 3, out_shape=[shp] * 3,
        compiler_params=_cparams(1), name=name)(w, g, m, v)


def _lane0():
    return lax.broadcasted_iota(jnp.int32, (1, LANES), 1) < HEAD_DIM


def _half_sum(x, m0):
    s0 = jnp.sum(jnp.where(m0, x, 0.0), axis=-1, keepdims=True)
    s1 = jnp.sum(jnp.where(m0, 0.0, x), axis=-1, keepdims=True)
    return jnp.where(m0, s0, s1)


def _half_pick(x, m0, e):
    sel = m0 if e == 0 else jnp.logical_not(m0)
    return jnp.max(jnp.where(sel, x, NEG), axis=-1, keepdims=True)


def _head_rms(x, m0):
    return lax.rsqrt(_half_sum(x * x, m0) * (1.0 / HEAD_DIM) + RMS_EPS)


def _alibi(n):
    return [float(2.0 ** (-8.0 * (h + 1) / n)) for h in range(n)]


def _mask_half(x, m0, e):
    return jnp.where(m0, x, 0.0) if e == 0 else jnp.where(m0, 0.0, x)


def _band_masks2(max_dist, has_prev, live):
    row = lax.broadcasted_iota(jnp.int32, (2 * BLK, 2 * BLK), 0)
    col = lax.broadcasted_iota(jnp.int32, (2 * BLK, 2 * BLK), 1)
    dist = (row & (BLK - 1)) - col + BLK
    lim = jnp.where(live, max_dist, -1)
    first = jnp.where(has_prev, 0, BLK)
    valid = (dist >= 0) & (dist <= lim) & (col >= first)
    top = lax.broadcasted_iota(jnp.int32, (2 * BLK, 1), 0) < BLK
    return dist.astype(F32), valid, top


def _stack_heads(x, m0, kes):
    parts = []
    for e in range(2):
        h = _mask_half(x, m0, e)
        parts.append(pltpu.roll(h, HEAD_DIM, 1) if kes[e] != e else h)
    return jnp.concatenate(parts, axis=0)


def _unstack_heads(y, m0, kes):
    parts = []
    for e in range(2):
        h = y[e * BLK:(e + 1) * BLK]
        parts.append(pltpu.roll(h, HEAD_DIM, 1) if kes[e] != e else h)
    return jnp.where(m0, parts[0], parts[1])


def _rows(r, dil):
    return pl.ds(r, BLK, stride=dil) if dil > 1 else pl.ds(0, BLK)


def _band_units(dil, nsub):
    assert dil == 1 or nsub == 1
    if nsub == 1:
        return [(_rows(r, dil), ("prev", _rows(r, dil)), 0) for r in range(dil)]
    units = [(pl.ds(0, BLK), ("prev", pl.ds(0, BLK)), 0)]
    units += [(pl.ds(BLK * s, BLK), ("cur", pl.ds(BLK * (s - 1), BLK)), s) for s in range(1, nsub)]
    return units


def _band_specs(dil, nsub, ppk, q_blk, k_blk, v_blk, kv_shared, nb):
    RB = BLK * dil * nsub
    PB = BLK if nsub > 1 else RB
    qw = LANES * ppk
    kw = LANES if kv_shared else qw

    def cur(i):
        return jnp.minimum(i, nb - 1)

    def prev(i):
        return jnp.maximum(i * nsub - 1, 0) if nsub > 1 else jnp.maximum(i - 1, 0)

    def kidx(base):
        return (lambda p, i: (cur(i), base)) if kv_shared else (lambda p, i: (cur(i), base + p))

    def pidx(base):
        return (lambda p, i: (prev(i), base)) if kv_shared else (lambda p, i: (prev(i), base + p))

    return [pl.BlockSpec((RB, qw), lambda p, i: (cur(i), q_blk + p)),
            pl.BlockSpec((RB, kw), kidx(k_blk)), pl.BlockSpec((PB, kw), pidx(k_blk)),
            pl.BlockSpec((RB, kw), kidx(v_blk)), pl.BlockSpec((PB, kw), pidx(v_blk))]


def banded_fwd(qkv, q_gain2, k_gain2, slopes, sinks, *, dil, nsub, ppk, q_blk, k_blk, v_blk, n_heads, group,
               max_dist, name):
    T = qkv.shape[0]
    RB = BLK * dil * nsub
    nb = T // RB
    npair = n_heads // 2
    kv_shared = group > 1
    scale = HEAD_DIM ** -0.5
    has_sink = sinks is not None

    def body(*refs):
        slope_ref = refs[0]
        if has_sink:
            sink_ref, refs = refs[1], refs[2:]
        else:
            refs = refs[1:]
        q_ref, kc_ref, kp_ref, vc_ref, vp_ref, qg_ref, kg_ref, o_ref, l_ref = refs
        pb = pl.program_id(0)
        i = pl.program_id(1)
        m0 = _lane0()
        distf, valid_first, top = _band_masks2(max_dist, i > 0, i >= 0)
        valid_inner = _band_masks2(max_dist, i >= 0, i >= 0)[1] if nsub > 1 else None
        qg = qg_ref[...]
        kg = kg_ref[...]
        for rows, (src, prows), sub in _band_units(dil, nsub):
            valid = valid_first if sub == 0 else valid_inner
            kpr, vpr = (kp_ref, vp_ref) if src == "prev" else (kc_ref, vc_ref)
            kcache = {}
            for jp in range(ppk):
                cs = pl.ds(LANES * jp, LANES)
                jk = 0 if kv_shared else jp
                if jk not in kcache:
                    ks = pl.ds(LANES * jk, LANES)
                    kcat = jnp.concatenate([kpr[prows, ks], kc_ref[rows, ks]], axis=0)
                    vcat = jnp.concatenate([vpr[prows, ks], vc_ref[rows, ks]], axis=0)
                    kcache[jk] = ((kcat * _head_rms(kcat, m0) * kg).astype(BF16), vcat.astype(BF16))
                kn, vcat = kcache[jk]
                qv = q_ref[rows, cs]
                qn = qv * _head_rms(qv, m0) * qg
                kes = [((2 * jp + e) // group) % 2 if kv_shared else e for e in range(2)]
                hidx = 2 * (pb * ppk + jp)
                qs = _stack_heads(qn, m0, kes).astype(BF16)
                slope = jnp.where(top, slope_ref[hidx], slope_ref[hidx + 1])
                s = jnp.where(valid, _dot_nt(qs, kn) * scale - slope * distf, NEG)
                m = jnp.max(s, axis=-1, keepdims=True)
                if has_sink:
                    sk = jnp.where(top, sink_ref[hidx], sink_ref[hidx + 1])
                    m = jnp.maximum(m, sk)
                p = jnp.exp(s - m)
                den = jnp.sum(p, axis=-1, keepdims=True)
                if has_sink:
                    den = den + jnp.exp(sk - m)
                o_full = _dot((p * (1.0 / den)).astype(BF16), vcat)
                o_ref[rows, cs] = _unstack_heads(o_full, m0, kes)
                l_ref[rows, cs] = _unstack_heads(jnp.broadcast_to(m + jnp.log(den), (2 * BLK, LANES)), m0, [0, 1])

    smem = pl.BlockSpec(memory_space=pltpu.SMEM)
    qw = LANES * ppk
    gspec = pl.BlockSpec((1, LANES), lambda p, i: (0, 0))
    ospec = pl.BlockSpec((RB, qw), lambda p, i: (i, p))
    oshape = jax.ShapeDtypeStruct((T, n_heads * HEAD_DIM), F32)
    args = [slopes] + ([sinks] if has_sink else []) + [qkv] * 5 + [q_gain2, k_gain2]
    return pl.pallas_call(
        body, grid=(npair // ppk, nb),
        in_specs=[smem] * (2 if has_sink else 1) + _band_specs(dil, nsub, ppk, q_blk, k_blk, v_blk, kv_shared, nb)
        + [gspec, gspec],
        out_specs=[ospec, ospec], out_shape=[oshape, oshape],
        compiler_params=_cparams(2), name=name)(*args)


def banded_bwd(qkv, q_gain2, k_gain2, slopes, sinks, do, o, lse, w, omix, *, dil, nsub, ppk, q_blk, k_blk, v_blk,
               n_heads, group, max_dist, do_blk, name):
    T = qkv.shape[0]
    RB = BLK * dil * nsub
    nb = T // RB
    npair = n_heads // 2
    kv_shared = group > 1
    scale = HEAD_DIM ** -0.5
    has_sink = sinks is not None
    mixed = w is not None
    qw = LANES * ppk

    def body(*refs):
        slope_ref = refs[0]
        if has_sink:
            sink_ref, refs = refs[1], refs[2:]
        else:
            refs = refs[1:]
        q_ref, kc_ref, kp_ref, vc_ref, vp_ref, qg_ref, kg_ref, do_ref, o_ref, l_ref = refs[:10]
        refs = refs[10:]
        if mixed:
            w_ref, om_ref, refs = refs[0], refs[1], refs[2:]
        dq_ref, dk_ref, dv_ref, dqg_ref, dkg_ref, dsk_ref, ck_ref, cv_ref = refs
        pb = pl.program_id(0)
        i = pl.program_id(1)
        live = i < nb
        m0 = _lane0()
        lane = lax.broadcasted_iota(jnp.int32, (1, LANES), 1)
        distf, valid_first, top = _band_masks2(max_dist, i > 0, live)
        valid_inner = _band_masks2(max_dist, i >= 0, live)[1] if nsub > 1 else None
        livef = live.astype(F32)
        qg = qg_ref[...]
        kg = kg_ref[...]

        def stack_rows(x2):
            return jnp.concatenate([_half_pick(x2, m0, 0), _half_pick(x2, m0, 1)], axis=0)

        @pl.when((pb == 0) & (i == 0))
        def _():
            dqg_ref[...] = jnp.zeros_like(dqg_ref)
            dkg_ref[...] = jnp.zeros_like(dkg_ref)
            dsk_ref[...] = jnp.zeros_like(dsk_ref)

        @pl.when(i == 0)
        def _():
            ck_ref[...] = jnp.zeros_like(ck_ref)
            cv_ref[...] = jnp.zeros_like(cv_ref)

        dqg_acc = jnp.zeros((1, LANES), F32)
        dkg_acc = jnp.zeros((1, LANES), F32)
        dsk_acc = jnp.zeros((1, LANES), F32)
        if nsub > 1:
            dk_ref[...] = ck_ref[...]
            dv_ref[...] = cv_ref[...]
        for rows, (src, prows), sub in _band_units(dil, nsub):
            valid = valid_first if sub == 0 else valid_inner
            kpr, vpr = (kp_ref, vp_ref) if src == "prev" else (kc_ref, vc_ref)
            for jp in range(ppk):
                cs = pl.ds(LANES * jp, LANES)
                ks = pl.ds(0, LANES) if kv_shared else cs
                kcat = jnp.concatenate([kpr[prows, ks], kc_ref[rows, ks]], axis=0)
                vcat = jnp.concatenate([vpr[prows, ks], vc_ref[rows, ks]], axis=0).astype(BF16)
                rk = _head_rms(kcat, m0)
                kh = kcat * rk
                kn = (kh * kg).astype(BF16)
                qv = q_ref[rows, cs]
                rq = _head_rms(qv, m0)
                qh = qv * rq
                dov = do_ref[rows, cs]
                lv = l_ref[rows, cs]
                if mixed:
                    wv = w_ref[rows, cs]
                    dmix = _half_sum(dov * om_ref[rows, cs], m0)
                    dov = dov * wv
                delta2 = _half_sum(dov * o_ref[rows, cs], m0)
                shift = stack_rows(wv * dmix if mixed else delta2)
                kes = [((2 * jp + e) // group) % 2 if kv_shared else e for e in range(2)]
                hidx = 2 * (pb * ppk + jp)
                qs = _stack_heads(qh * qg, m0, kes).astype(BF16)
                dos = _stack_heads(dov, m0, kes).astype(BF16)
                lse = stack_rows(lv)
                slope = jnp.where(top, slope_ref[hidx], slope_ref[hidx + 1])
                p = jnp.where(valid, jnp.exp(_dot_nt(qs, kn) * scale - slope * distf - lse), 0.0)
                ds = (p * (_dot_nt(dos, vcat) - shift)).astype(BF16)
                dqn = _unstack_heads(_dot(ds, kn), m0, kes) * scale
                dkn = _dot_tn(ds, qs) * scale
                dvv = _dot_tn(p.astype(BF16), dos)
                if has_sink:
                    sk = jnp.where(top, sink_ref[hidx], sink_ref[hidx + 1])
                    contrib = -jnp.exp(sk - lse) * stack_rows(delta2) * livef
                    for e in range(2):
                        tot = jnp.sum(contrib[e * BLK:(e + 1) * BLK], axis=0, keepdims=True)
                        dsk_acc = dsk_acc + jnp.where(lane == (2 * jp + e), tot, 0.0)
                dqg_acc = dqg_acc + jnp.sum(dqn * qh, axis=0, keepdims=True)
                dqh = dqn * qg
                dq_raw = rq * (dqh - qh * (_half_sum(dqh * qh, m0) * (1.0 / HEAD_DIM)))
                dkg_acc = dkg_acc + jnp.sum(dkn * kh, axis=0, keepdims=True)
                dkh = dkn * kg
                dk_raw = rk * (dkh - kh * (_half_sum(dkh * kh, m0) * (1.0 / HEAD_DIM)))

                @pl.when(live)
                def _():
                    dq_ref[rows, cs] = dq_raw

                if nsub == 1:
                    dk_ref[rows, cs] = ck_ref[rows, cs] + dk_raw[:BLK]
                    dv_ref[rows, cs] = cv_ref[rows, cs] + dvv[:BLK]
                elif sub == 0:
                    last = pl.ds(RB - BLK, BLK)
                    dk_ref[last, cs] += dk_raw[:BLK]
                    dv_ref[last, cs] += dvv[:BLK]
                else:
                    ck_ref[prows, cs] += dk_raw[:BLK]
                    cv_ref[prows, cs] += dvv[:BLK]
                ck_ref[rows, cs] = dk_raw[BLK:]
                cv_ref[rows, cs] = dvv[BLK:]
        dqg_ref[...] += dqg_acc
        dkg_ref[...] += dkg_acc
        dsk_ref[...] += dsk_acc

    smem = pl.BlockSpec(memory_space=pltpu.SMEM)
    gspec = pl.BlockSpec((1, LANES), lambda p, i: (0, 0))

    def cur(i):
        return jnp.minimum(i, nb - 1)

    qspec = pl.BlockSpec((RB, qw), lambda p, i: (cur(i), p))
    dospec = pl.BlockSpec((RB, qw), lambda p, i: (cur(i), do_blk + p))
    kvout = pl.BlockSpec((RB, qw), lambda p, i: (jnp.maximum(i - 1, 0), p))
    in_specs = ([smem] * (2 if has_sink else 1) + _band_specs(dil, nsub, ppk, q_blk, k_blk, v_blk, kv_shared, nb)
                + [gspec, gspec, dospec, qspec, qspec] + ([qspec, qspec] if mixed else []))
    args = ([slopes] + ([sinks] if has_sink else []) + [qkv] * 5 + [q_gain2, k_gain2, do, o, lse]
            + ([w, omix] if mixed else []))
    full = jax.ShapeDtypeStruct((T, n_heads * HEAD_DIM), F32)
    row = jax.ShapeDtypeStruct((1, LANES), F32)
    return pl.pallas_call(
        body, grid=(npair // ppk, nb + 1), in_specs=in_specs,
        out_specs=[qspec, kvout, kvout, gspec, gspec, gspec],
        out_shape=[full, full, full, row, row, row],
        scratch_shapes=[pltpu.VMEM((RB, qw), F32), pltpu.VMEM((RB, qw), F32)],
        compiler_params=_cparams(2), name=name)(*args)


def mix_fwd(o1, o2, o3, l1, l2, l3, *, tm, name):
    T, C = o1.shape

    def body(o1r, o2r, o3r, l1r, l2r, l3r, o_ref, w1r, w2r, w3r):
        a, b, c = l1r[...], l2r[...], l3r[...]
        m = jnp.maximum(jnp.maximum(a, b), c)
        ea, eb, ec = jnp.exp(a - m), jnp.exp(b - m), jnp.exp(c - m)
        inv = 1.0 / (ea + eb + ec)
        wa, wb, wc = ea * inv, eb * inv, ec * inv
        o_ref[...] = wa * o1r[...] + wb * o2r[...] + wc * o3r[...]
        w1r[...] = wa
        w2r[...] = wb
        w3r[...] = wc

    spec = pl.BlockSpec((tm, C), lambda i: (i, 0))
    shp = jax.ShapeDtypeStruct((T, C), F32)
    return pl.pallas_call(body, grid=(T // tm,), in_specs=[spec] * 6, out_specs=[spec] * 4, out_shape=[shp] * 4,
                          compiler_params=_cparams(1), name=name)(o1, o2, o3, l1, l2, l3)


def assemble_odd(parts, *, tm, name):
    T, C = parts[0][0].shape

    def body(*refs):
        o_ref = refs[9]
        for j in range(3):
            o_ref[:, pl.ds(C * j, C)] = refs[j][...] + refs[3 + j][...] + refs[6 + j][...]

    spec = pl.BlockSpec((tm, C), lambda i: (i, 0))
    flat = [parts[p][j] for p in range(3) for j in range(3)]
    return pl.pallas_call(body, grid=(T // tm,), in_specs=[spec] * 9,
                          out_specs=pl.BlockSpec((tm, 3 * C), lambda i: (i, 0)),
                          out_shape=jax.ShapeDtypeStruct((T, 3 * C), F32),
                          compiler_params=_cparams(1), name=name)(*flat)


def assemble_even(dqa, dka4, dva4, dqb, dkb, dvb, *, tm, name):
    T = dqa.shape[0]
    W = 512

    def body(dqa_r, dka_r, dva_r, dqb_r, dkb_r, dvb_r, o_ref):
        o_ref[:, pl.ds(0, W)] = dqa_r[...]
        ka = dka_r[...]
        va = dva_r[...]
        o_ref[:, pl.ds(512, LANES)] = ka[:, 0:128] + ka[:, 128:256] + ka[:, 256:384] + ka[:, 384:512]
        o_ref[:, pl.ds(640, LANES)] = va[:, 0:128] + va[:, 128:256] + va[:, 256:384] + va[:, 384:512]
        o_ref[:, pl.ds(768, W)] = dqb_r[...]
        o_ref[:, pl.ds(1280, W)] = dkb_r[...]
        o_ref[:, pl.ds(1792, W)] = dvb_r[...]

    spec = pl.BlockSpec((tm, W), lambda i: (i, 0))
    return pl.pallas_call(body, grid=(T // tm,), in_specs=[spec] * 6,
                          out_specs=pl.BlockSpec((tm, 2304), lambda i: (i, 0)),
                          out_shape=jax.ShapeDtypeStruct((T, 2304), F32),
                          compiler_params=_cparams(1), name=name)(dqa, dka4, dva4, dqb, dkb, dvb)


STICK_T = 256
STICK_DEAD = -110.0


def _split_bf16(x):
    hi = x.astype(BF16)
    lo = (x - hi.astype(F32)).astype(BF16)
    return hi, lo


def _stick_logits(qm, kt, scale, diag):
    n = STICK_T
    row = lax.broadcasted_iota(jnp.int32, (n, n), 0)
    col = lax.broadcasted_iota(jnp.int32, (n, n), 1)
    mask = col < row + jnp.where(diag, 0, n)
    z = _dot_nt(qm, kt) * scale
    lneg = -(jnp.maximum(z, 0.0) + jnp.log(1.0 + jnp.exp(-jnp.abs(z))))
    lpos = z + lneg
    lk = jnp.where(mask, lneg, 0.0)
    return mask, lpos, lneg, lk


def _cumsum_mm(x, tri):
    hi, lo = _split_bf16(x)
    return _dot(hi, tri) + _dot(lo, tri)


def stick_fwd(qkv, *, q_blk, k_blk, v_blk, n_pairs, name):
    T = qkv.shape[0]
    n = STICK_T
    nq = T // n
    scale = HEAD_DIM ** -0.5

    def body(q_ref, k_ref, v_ref, o_ref):
        i = pl.program_id(1)
        m0 = _lane0()
        r2 = lax.broadcasted_iota(jnp.int32, (n, n), 0)
        c2 = lax.broadcasted_iota(jnp.int32, (n, n), 1)
        tri_after = (r2 > c2).astype(BF16)
        qv = q_ref[...]
        out = jnp.zeros((n, LANES), F32)
        for e in range(2):
            qm = _mask_half(qv, m0, e).astype(BF16)

            def alive(st):
                t, _, carry = st
                return (t <= i) & (jnp.max(carry) > STICK_DEAD)

            def step(st, e=e, qm=qm):
                t, acc, carry = st
                start = pl.multiple_of((i - t) * n, n)
                kt = k_ref[pl.ds(start, n), :].astype(BF16)
                vt = _mask_half(v_ref[pl.ds(start, n), :], m0, e).astype(BF16)
                mask, lpos, _, lk = _stick_logits(qm, kt, scale, t == 0)
                after = _cumsum_mm(lk, tri_after) + carry
                a = jnp.where(mask, jnp.exp(lpos + after), 0.0)
                acc = acc + _dot(a.astype(BF16), vt)
                carry = carry + jnp.sum(lk, axis=-1, keepdims=True)
                return t + 1, acc, carry

            _, acc, _ = lax.while_loop(alive, step, (jnp.int32(0), jnp.zeros((n, LANES), F32),
                                                     jnp.zeros((n, 1), F32)))
            out = out + acc
        o_ref[...] = out

    return pl.pallas_call(
        body, grid=(n_pairs, nq),
        in_specs=[pl.BlockSpec((n, LANES), lambda p, i: (i, q_blk + p)),
                  pl.BlockSpec((T, LANES), lambda p, i: (0, k_blk + p)),
                  pl.BlockSpec((T, LANES), lambda p, i: (0, v_blk + p))],
        out_specs=pl.BlockSpec((n, LANES), lambda p, i: (i, p)),
        out_shape=jax.ShapeDtypeStruct((T, n_pairs * LANES), F32),
        compiler_params=_cparams(2), name=name)(qkv, qkv, qkv)


def stick_bwd(qkv, do, *, q_blk, k_blk, v_blk, do_blk, n_pairs, name):
    T = qkv.shape[0]
    n = STICK_T
    nq = T // n
    scale = HEAD_DIM ** -0.5

    def body(q_ref, k_ref, v_ref, do_ref, dq_ref, dk_ref, dv_ref):
        i = pl.program_id(1)
        m0 = _lane0()
        r2 = lax.broadcasted_iota(jnp.int32, (n, n), 0)
        c2 = lax.broadcasted_iota(jnp.int32, (n, n), 1)
        tri_after = (r2 > c2).astype(BF16)
        tri_from = (r2 >= c2).astype(BF16)

        @pl.when(i == 0)
        def _():
            dk_ref[...] = jnp.zeros_like(dk_ref)
            dv_ref[...] = jnp.zeros_like(dv_ref)

        qv = q_ref[...]
        dov = do_ref[...]
        dq_out = jnp.zeros((n, LANES), F32)
        for e in range(2):
            qm = _mask_half(qv, m0, e).astype(BF16)
            dom = _mask_half(dov, m0, e).astype(BF16)

            def tile(t, carry, qm=qm, dom=dom):
                start = pl.multiple_of((i - t) * n, n)
                kt_f = k_ref[pl.ds(start, n), :]
                vt = v_ref[pl.ds(start, n), :].astype(BF16)
                mask, lpos, lneg, lk = _stick_logits(qm, kt_f.astype(BF16), scale, t == 0)
                a = jnp.where(mask, jnp.exp(lpos + _cumsum_mm(lk, tri_after) + carry), 0.0)
                g = _dot_nt(dom, vt) * a
                return start, kt_f, mask, lpos, lneg, lk, a, g

            def alive(st):
                t, carry, _ = st
                return (t <= i) & (jnp.max(carry) > STICK_DEAD)

            def scan(st):
                t, carry, gtot = st
                _, _, _, _, _, lk, _, g = tile(t, carry)
                return (t + 1, carry + jnp.sum(lk, axis=-1, keepdims=True),
                        gtot + jnp.sum(g, axis=-1, keepdims=True))

            z1 = jnp.zeros((n, 1), F32)
            n_live, _, gtot = lax.while_loop(alive, scan, (jnp.int32(0), z1, z1))

            def step(t, st, e=e, qm=qm, dom=dom, gtot=gtot):
                dq_acc, carry, gright = st
                start, kt_f, mask, lpos, lneg, lk, a, g = tile(t, carry)
                before = gtot - (_cumsum_mm(g, tri_from) + gright)
                dz = jnp.where(mask, g * jnp.exp(lneg) - before * jnp.exp(lpos), 0.0) * scale
                dzb = dz.astype(BF16)
                dq_acc = dq_acc + _dot(dzb, _mask_half(kt_f, m0, e).astype(BF16))
                dk_ref[pl.ds(start, n), :] += _dot_tn(dzb, qm)
                dv_ref[pl.ds(start, n), :] += _dot_tn(a.astype(BF16), dom)
                return (dq_acc, carry + jnp.sum(lk, axis=-1, keepdims=True),
                        gright + jnp.sum(g, axis=-1, keepdims=True))

            dq_acc, _, _ = lax.fori_loop(0, n_live, step, (jnp.zeros((n, LANES), F32), z1, z1))
            dq_out = dq_out + dq_acc
        dq_ref[...] = dq_out

    tile = pl.BlockSpec((n, LANES), lambda p, i: (i, p))
    whole = pl.BlockSpec((T, LANES), lambda p, i: (0, p))
    shp = jax.ShapeDtypeStruct((T, n_pairs * LANES), F32)
    return pl.pallas_call(
        body, grid=(n_pairs, nq),
        in_specs=[pl.BlockSpec((n, LANES), lambda p, i: (i, q_blk + p)),
                  pl.BlockSpec((T, LANES), lambda p, i: (0, k_blk + p)),
                  pl.BlockSpec((T, LANES), lambda p, i: (0, v_blk + p)),
                  pl.BlockSpec((n, LANES), lambda p, i: (i, do_blk + p))],
        out_specs=[tile, whole, whole], out_shape=[shp, shp, shp],
        compiler_params=_cparams(2), name=name)(qkv, qkv, qkv, do)


def _xnorm(x):
    r = lax.rsqrt(jnp.mean(x * x, axis=-1, keepdims=True) + RMS_EPS)
    return r, x * r


def xattn_fwd(qraw, kvraw, q_gain, k_gain, *, tm, name):
    T = qraw.shape[0]
    scale = X_HEAD_DIM ** -0.5
    W = X_HEADS * X_HEAD_DIM

    def body(q_ref, kv_ref, qg_ref, kg_ref, o_ref):
        for h in range(X_HEADS):
            cs = pl.ds(X_HEAD_DIM * h, X_HEAD_DIM)
            _, qh = _xnorm(q_ref[:, cs])
            _, kh = _xnorm(kv_ref[:, cs])
            qn = (qh * qg_ref[...]).astype(BF16)
            kn = (kh * kg_ref[...]).astype(BF16)
            v = kv_ref[:, pl.ds(W + X_HEAD_DIM * h, X_HEAD_DIM)].astype(BF16)
            s = _dot_nt(qn, kn) * scale
            m = jnp.max(s, axis=-1, keepdims=True)
            p = jnp.exp(s - m)
            p = p / jnp.sum(p, axis=-1, keepdims=True)
            o_ref[:, cs] = _dot(p.astype(BF16), v)

    gspec = pl.BlockSpec((1, X_HEAD_DIM), lambda i: (0, 0))
    return pl.pallas_call(
        body, grid=(T // tm,),
        in_specs=[pl.BlockSpec((tm, W), lambda i: (i, 0)), pl.BlockSpec((MEM_LEN, 2 * W), lambda i: (0, 0)),
                  gspec, gspec],
        out_specs=pl.BlockSpec((tm, W), lambda i: (i, 0)),
        out_shape=jax.ShapeDtypeStruct((T, W), F32),
        compiler_params=_cparams(1), name=name)(qraw, kvraw, q_gain, k_gain)


def xattn_bwd(qraw, kvraw, q_gain, k_gain, do, o, *, tm, name):
    T = qraw.shape[0]
    nt = T // tm
    scale = X_HEAD_DIM ** -0.5
    W = X_HEADS * X_HEAD_DIM

    def body(q_ref, kv_ref, qg_ref, kg_ref, do_ref, o_ref, dq_ref, dkv_ref, dqg_ref, dkg_ref, dkn_ref):
        i = pl.program_id(0)

        @pl.when(i == 0)
        def _():
            dkv_ref[...] = jnp.zeros_like(dkv_ref)
            dkn_ref[...] = jnp.zeros_like(dkn_ref)
            dqg_ref[...] = jnp.zeros_like(dqg_ref)
            dkg_ref[...] = jnp.zeros_like(dkg_ref)

        qg = qg_ref[...]
        kg = kg_ref[...]
        dqg_acc = jnp.zeros((1, X_HEAD_DIM), F32)
        for h in range(X_HEADS):
            cs = pl.ds(X_HEAD_DIM * h, X_HEAD_DIM)
            vs = pl.ds(W + X_HEAD_DIM * h, X_HEAD_DIM)
            rq, qh = _xnorm(q_ref[:, cs])
            _, kh = _xnorm(kv_ref[:, cs])
            qn = (qh * qg).astype(BF16)
            kn = (kh * kg).astype(BF16)
            v = kv_ref[:, vs].astype(BF16)
            s = _dot_nt(qn, kn) * scale
            m = jnp.max(s, axis=-1, keepdims=True)
            p = jnp.exp(s - m)
            p = p / jnp.sum(p, axis=-1, keepdims=True)
            dov = do_ref[:, cs]
            delta = jnp.sum(dov * o_ref[:, cs], axis=-1, keepdims=True)
            dob = dov.astype(BF16)
            ds = (p * (_dot_nt(dob, v) - delta)).astype(BF16)
            dqn = _dot(ds, kn) * scale
            dkn_ref[:, cs] += _dot_tn(ds, qn) * scale
            dkv_ref[:, vs] += _dot_tn(p.astype(BF16), dob)
            dqg_acc = dqg_acc + jnp.sum(dqn * qh, axis=0, keepdims=True)
            dqh = dqn * qg
            dq_ref[:, cs] = rq * (dqh - qh * jnp.mean(dqh * qh, axis=-1, keepdims=True))
        dqg_ref[...] += dqg_acc

        @pl.when(i == nt - 1)
        def _():
            dkg_acc = jnp.zeros((1, X_HEAD_DIM), F32)
            for h in range(X_HEADS):
                cs = pl.ds(X_HEAD_DIM * h, X_HEAD_DIM)
                rk, kh = _xnorm(kv_ref[:, cs])
                dkn = dkn_ref[:, cs]
                dkg_acc = dkg_acc + jnp.sum(dkn * kh, axis=0, keepdims=True)
                dkh = dkn * kg
                dkv_ref[:, cs] = rk * (dkh - kh * jnp.mean(dkh * kh, axis=-1, keepdims=True))
            dkg_ref[...] = dkg_acc

    gspec = pl.BlockSpec((1, X_HEAD_DIM), lambda i: (0, 0))
    tile = pl.BlockSpec((tm, W), lambda i: (i, 0))
    kvspec = pl.BlockSpec((MEM_LEN, 2 * W), lambda i: (0, 0))
    grow = jax.ShapeDtypeStruct((1, X_HEAD_DIM), F32)
    return pl.pallas_call(
        body, grid=(nt,), in_specs=[tile, kvspec, gspec, gspec, tile, tile],
        out_specs=[tile, kvspec, gspec, gspec],
        out_shape=[jax.ShapeDtypeStruct((T, W), F32), jax.ShapeDtypeStruct((MEM_LEN, 2 * W), F32), grow, grow],
        scratch_shapes=[pltpu.VMEM((MEM_LEN, W), F32)],
        compiler_params=_cparams(1), name=name)(qraw, kvraw, q_gain, k_gain, do, o)


_ANY = pl.BlockSpec(memory_space=pl.ANY)


def _my_pos():
    return lax.axis_index("x"), lax.axis_index("y"), lax.axis_index("c")


def _pieces(arrays, chunks):
    out = []
    for a, (arr, n) in enumerate(zip(arrays, chunks)):
        rc = arr.shape[-2] // n
        out += [(a, pl.ds(ch * rc, rc)) for ch in range(n)]
    return out


def gather_blocks(blks, chunks, *, name):
    n = len(blks)
    pieces = _pieces(blks, chunks)
    n_p = len(pieces)

    def body(*refs):
        x_refs, out_refs = refs[:n], refs[n:2 * n]
        send_sems, recv_sems, local_sems = refs[2 * n:]
        x, y, c = _my_pos()
        me, sibling = (x, y, c), (x, y, 1 - c)
        chips = [(1 - x, y), (x, 1 - y), (1 - x, 1 - y)]

        def slot(block, p):
            px, py, pc = block
            a, rows = pieces[p]
            return out_refs[a].at[4 * px + 2 * py + pc, rows]

        def own(p):
            a, rows = pieces[p]
            return x_refs[a].at[rows]

        def copy(k, p, block, to, from_input=False):
            return pltpu.make_async_remote_copy(
                src_ref=own(p) if from_input else slot(block, p), dst_ref=slot(block, p),
                send_sem=send_sems.at[k * n_p + p], recv_sem=recv_sems.at[k * n_p + p],
                device_id=to, device_id_type=MESH)

        ps = range(n_p)
        xn, yn, dg = [(*chip, c) for chip in chips]
        mine = [pltpu.make_async_copy(own(p), slot(me, p), local_sems.at[p]) for p in ps]
        first = [copy(k, p, me, to, from_input=True) for p in ps for k, to in ((1, xn), (2, yn), (0, sibling))]
        for cp in first + mine:
            cp.start()
        passed = []

        def pass_on(k, p, block, to):
            cp = copy(k, p, block, to)
            cp.start()
            passed.append(cp)

        for p in ps:
            copy(1, p, xn, me).wait_recv()
            if p % 2 == 0:
                pass_on(3, p, xn, yn)
            pass_on(4, p, xn, sibling)
            copy(2, p, yn, me).wait_recv()
            if p % 2 == 1:
                pass_on(3, p, yn, xn)
            pass_on(5, p, yn, sibling)
        for p in ps:
            copy(3, p, dg, me).wait_recv()
            pass_on(6, p, dg, sibling)
        for p in ps:
            copy(0, p, sibling, me).wait_recv()
        for k, chip in zip((4, 5, 6), chips):
            for p in ps:
                copy(k, p, (*chip, 1 - c), me).wait_recv()
        for cp in first + passed:
            cp.wait_send()
        for cp in mine:
            cp.wait()

    return pl.pallas_call(
        body, out_shape=[jax.ShapeDtypeStruct((N_DEV,) + b.shape, b.dtype) for b in blks],
        in_specs=[_ANY] * n, out_specs=[_ANY] * n,
        scratch_shapes=[pltpu.SemaphoreType.DMA((7 * n_p,)), pltpu.SemaphoreType.DMA((7 * n_p,)),
                        pltpu.SemaphoreType.DMA((n_p,))],
        name=name)(*blks)


def gather_small(small, *, name):
    S, C = small.shape

    def body(s_ref, out_ref, send_sems, recv_sems, local_sem):
        x, y, c = _my_pos()
        my_id = 4 * x + 2 * y + c

        def copy(k, slot):
            px, py, pc = x ^ ((k >> 2) & 1), y ^ ((k >> 1) & 1), c ^ (k & 1)
            dst = my_id if slot == "mine" else 4 * px + 2 * py + pc
            return pltpu.make_async_remote_copy(
                src_ref=s_ref, dst_ref=out_ref.at[dst], send_sem=send_sems.at[k - 1], recv_sem=recv_sems.at[k - 1],
                device_id=(px, py, pc), device_id_type=MESH)

        own = pltpu.make_async_copy(s_ref, out_ref.at[my_id], local_sem)
        own.start()
        sends = [copy(k, "mine") for k in range(1, N_DEV)]
        for cp in sends:
            cp.start()
        for k in range(1, N_DEV):
            copy(k, "theirs").wait_recv()
        for cp in sends:
            cp.wait_send()
        own.wait()

    dma7 = pltpu.SemaphoreType.DMA((7,))
    return pl.pallas_call(
        body, out_shape=jax.ShapeDtypeStruct((N_DEV, S, C), small.dtype), in_specs=[_ANY], out_specs=_ANY,
        scratch_shapes=[dma7, dma7, pltpu.SemaphoreType.DMA], name=name)(small)


def pair_exchange(bigs, chunks, *, name):
    n = len(bigs)
    pieces = _pieces(bigs, chunks)
    n_p = len(pieces)

    def body(*refs):
        big_refs, out_refs = refs[:n], refs[n:2 * n]
        send_sems, recv_sems = refs[2 * n:]
        x, y, c = _my_pos()

        def copy(b, p):
            a, rows = pieces[p]
            return pltpu.make_async_remote_copy(
                src_ref=big_refs[a].at[2 * b + (1 - c), rows], dst_ref=out_refs[a].at[b, rows],
                send_sem=send_sems.at[b * n_p + p], recv_sem=recv_sems.at[b * n_p + p],
                device_id=(x, y, 1 - c), device_id_type=MESH)

        cps = [copy(b, p) for b in range(4) for p in range(n_p)]
        for cp in cps:
            cp.start()
        for cp in cps:
            cp.wait_recv()
        for cp in cps:
            cp.wait_send()

    return pl.pallas_call(
        body, out_shape=[jax.ShapeDtypeStruct((4,) + b.shape[1:], b.dtype) for b in bigs],
        in_specs=[_ANY] * n, out_specs=[_ANY] * n,
        scratch_shapes=[pltpu.SemaphoreType.DMA((4 * n_p,)), pltpu.SemaphoreType.DMA((4 * n_p,))],
        name=name)(*bigs)


def pair_sum(big, sib, c, *, tr, name):
    _, R, C = big.shape

    def body(c_ref, a_ref, s_ref, o_ref):
        o_ref[...] = (a_ref[...].astype(F32) + s_ref[...].astype(F32)).astype(o_ref.dtype)

    grid_spec = pltpu.PrefetchScalarGridSpec(
        num_scalar_prefetch=1, grid=(4, R // tr),
        in_specs=[pl.BlockSpec((None, tr, C), lambda b, i, c_ref: (2 * b + c_ref[0], i, 0)),
                  pl.BlockSpec((None, tr, C), lambda b, i, c_ref: (b, i, 0))],
        out_specs=pl.BlockSpec((None, tr, C), lambda b, i, c_ref: (b, i, 0)))
    return pl.pallas_call(body, grid_spec=grid_spec, out_shape=jax.ShapeDtypeStruct((4, R, C), big.dtype),
                          compiler_params=_cparams(2), name=name)(c.reshape(1).astype(jnp.int32), big, sib)


def chip_scatter(pres, chunks, *, name):
    n = len(pres)
    pieces = _pieces(pres, chunks)
    n_p = len(pieces)

    def body(*refs):
        pre_refs, out_refs = refs[:n], refs[n:2 * n]
        send_sems, recv_sems, local_sems = refs[2 * n:]
        x, y, c = _my_pos()
        my_chip = 2 * x + y
        chips = [(1 - x, y), (x, 1 - y), (1 - x, 1 - y)]

        def copy(j, p, slot):
            px, py = chips[j]
            a, rows = pieces[p]
            src_slot, dst_slot = (2 * px + py, my_chip) if slot == "mine" else (my_chip, 2 * px + py)
            return pltpu.make_async_remote_copy(
                src_ref=pre_refs[a].at[src_slot, rows], dst_ref=out_refs[a].at[dst_slot, rows],
                send_sem=send_sems.at[j * n_p + p], recv_sem=recv_sems.at[j * n_p + p],
                device_id=(px, py, c), device_id_type=MESH)

        own = [pltpu.make_async_copy(pre_refs[a].at[my_chip, rows], out_refs[a].at[my_chip, rows], local_sems.at[p])
               for p, (a, rows) in enumerate(pieces)]
        sends = [copy(j, p, "mine") for j in range(3) for p in range(n_p)]
        for cp in sends + own:
            cp.start()
        for j in range(3):
            for p in range(n_p):
                copy(j, p, "theirs").wait_recv()
        for cp in sends:
            cp.wait_send()
        for cp in own:
            cp.wait()

    return pl.pallas_call(
        body, out_shape=[jax.ShapeDtypeStruct(p.shape, p.dtype) for p in pres],
        in_specs=[_ANY] * n, out_specs=[_ANY] * n,
        scratch_shapes=[pltpu.SemaphoreType.DMA((3 * n_p,)), pltpu.SemaphoreType.DMA((3 * n_p,)),
                        pltpu.SemaphoreType.DMA((n_p,))],
        name=name)(*pres)


def sibling_send(blks, chunks, *, name):
    n = len(blks)
    pieces = _pieces(blks, chunks)
    n_p = len(pieces)

    def body(*refs):
        x_refs, out_refs = refs[:n], refs[n:2 * n]
        send_sems, recv_sems = refs[2 * n:]
        x, y, c = _my_pos()
        cps = [pltpu.make_async_remote_copy(
            src_ref=x_refs[a].at[rows], dst_ref=out_refs[a].at[rows], send_sem=send_sems.at[p],
            recv_sem=recv_sems.at[p], device_id=(x, y, 1 - c), device_id_type=MESH)
            for p, (a, rows) in enumerate(pieces)]
        for cp in cps:
            cp.start()
        for cp in cps:
            cp.wait_recv()
        for cp in cps:
            cp.wait_send()

    return pl.pallas_call(
        body, out_shape=[jax.ShapeDtypeStruct(b.shape, b.dtype) for b in blks],
        in_specs=[_ANY] * n, out_specs=[_ANY] * n,
        scratch_shapes=[pltpu.SemaphoreType.DMA((n_p,)), pltpu.SemaphoreType.DMA((n_p,))],
        name=name)(*blks)


def reduce_slots(land, *, tr, name):
    n, R, C = land.shape

    def body(l_ref, o_ref):
        acc = l_ref[0].astype(F32)
        for s in range(1, n):
            acc = acc + l_ref[s].astype(F32)
        o_ref[...] = acc

    return pl.pallas_call(
        body, grid=(R // tr,), in_specs=[pl.BlockSpec((n, tr, C), lambda i: (0, i, 0))],
        out_specs=pl.BlockSpec((tr, C), lambda i: (i, 0)), out_shape=jax.ShapeDtypeStruct((R, C), F32),
        compiler_params=_cparams(1), name=name)(land)


TM = 512


def _tk(d):
    return min(d.shape[0], 1024)


def ffn_fwd(x, g, wgu, wd, tag):
    gu, h = norm_matmul(x, g, wgu, tm=_tk(x), tn=1408, split=True, name=f"{tag}_gu")
    xo = mm_nn(gu, wd, res=x, scale=0.5, swiglu=True, tm=TM, tn=D_MODEL, tk=1408, name=f"{tag}_down")
    return xo, (x, gu, h)


def ffn_bwd(d, saved, g, wgu, wd, tag):
    x, gu, h = saved
    dgu, act = ffn_bwd_act(d, wd, gu, tm=TM, tn=1408, name=f"{tag}_bact")
    dwd = mm_tn(act, d, scale=0.5, a_split=False, b_split=False, tm=1408, tn=D_MODEL, tk=_tk(d), name=f"{tag}_dwd")
    dwgu = mm_tn(h, dgu, scale=1.0, a_split=False, b_split=True, tm=TM, tn=1408, tk=_tk(d), out_blocked=True,
                 name=f"{tag}_dwgu")
    dx, dg = mm_nt_normbwd(dgu, wgu, x, g, d, a_split=True, tm=_tk(d), tk=1408, name=f"{tag}_dx")
    return dx, dg, dwgu, dwd


def _tile2(v):
    return jnp.concatenate([v, v], axis=-1).reshape(1, LANES)


def _fold2(v):
    return v[:, :HEAD_DIM] + v[:, HEAD_DIM:]


EVEN = dict(dil=1, nsub=4, ppk=4, q_blk=0, k_blk=4, v_blk=5, n_heads=A_Q_HEADS, group=A_GROUP, max_dist=A_WINDOW - 1)
STICK = dict(q_blk=6, k_blk=10, v_blk=14, n_pairs=4)


def _odd_cfg(dil):
    return dict(dil=dil, nsub=4 if dil == 1 else 1, ppk=1, q_blk=0, k_blk=8, v_blk=16, n_heads=C_HEADS, group=1,
                max_dist=BLK)


def even_fwd(x, g, win, qg, kg, sinks, wout, tag):
    qkv, h = norm_matmul(x, g, win, tm=_tk(x), tn=1152, split=False, name=f"{tag}_in")
    qg2, kg2 = _tile2(qg), _tile2(kg)
    slopes = jnp.asarray(_alibi(A_Q_HEADS), F32)
    oa, lse = banded_fwd(qkv, qg2, kg2, slopes, sinks, name=f"{tag}_swa", **EVEN)
    ob = stick_fwd(qkv, name=f"{tag}_stick", **STICK)
    o = jnp.concatenate([oa, ob], axis=1)
    xo = mm_nn(o, wout, res=x, scale=1.0, swiglu=False, tm=TM, tn=D_MODEL, tk=D_MODEL, name=f"{tag}_out")
    return xo, (x, qkv, h, oa, lse, o)


def even_bwd(d, saved, g, win, qg, kg, sinks, wout, tag):
    x, qkv, h, oa, lse, o = saved
    qg2, kg2 = _tile2(qg), _tile2(kg)
    slopes = jnp.asarray(_alibi(A_Q_HEADS), F32)
    dwout = mm_tn(o, d, scale=1.0, a_split=False, b_split=False, tm=D_MODEL, tn=D_MODEL, tk=_tk(d), name=f"{tag}_dwout")
    do = mm_nt(d, wout, tm=TM, tn=D_MODEL, tk=D_MODEL, name=f"{tag}_do")
    dqa, dka4, dva4, dqg, dkg, dsk = banded_bwd(qkv, qg2, kg2, slopes, sinks, do, oa, lse, None, None,
                                                do_blk=0, name=f"{tag}_swa_b", **EVEN)
    dqb, dkb, dvb = stick_bwd(qkv, do, do_blk=4, name=f"{tag}_stick_b", **STICK)
    dqkv = assemble_even(dqa, dka4, dva4, dqb, dkb, dvb, tm=TM, name=f"{tag}_asm")
    dwin = mm_tn(h, dqkv, scale=1.0, a_split=False, b_split=False, tm=D_MODEL, tn=1152, tk=_tk(d), name=f"{tag}_dwin")
    dx, dg = mm_nt_normbwd(dqkv, win, x, g, d, a_split=False, tm=TM, tk=1152, name=f"{tag}_dx")
    return dx, dg, dwin, _fold2(dqg), _fold2(dkg), dsk[:, :A_Q_HEADS], dwout


def odd_fwd(x, g, win, qg, kg, wout, tag):
    qkv, h = norm_matmul(x, g, win, tm=_tk(x), tn=768, split=False, name=f"{tag}_in")
    qg2, kg2 = _tile2(qg), _tile2(kg)
    outs = []
    for p, (window, dil) in enumerate(C_PATTERNS):
        slopes = jnp.asarray(_alibi(C_HEADS), F32) * float(dil)
        outs.append(banded_fwd(qkv, qg2, kg2, slopes, None, name=f"{tag}_dil{p}", **_odd_cfg(dil)))
    o, w1, w2, w3 = mix_fwd(outs[0][0], outs[1][0], outs[2][0], outs[0][1], outs[1][1], outs[2][1],
                            tm=TM, name=f"{tag}_mix")
    xo = mm_nn(o, wout, res=x, scale=1.0, swiglu=False, tm=TM, tn=D_MODEL, tk=D_MODEL, name=f"{tag}_out")
    return xo, (x, qkv, h, outs, (w1, w2, w3), o)


def odd_bwd(d, saved, g, win, qg, kg, wout, tag):
    x, qkv, h, outs, ws, o = saved
    qg2, kg2 = _tile2(qg), _tile2(kg)
    dwout = mm_tn(o, d, scale=1.0, a_split=False, b_split=False, tm=D_MODEL, tn=D_MODEL, tk=_tk(d), name=f"{tag}_dwout")
    do = mm_nt(d, wout, tm=TM, tn=D_MODEL, tk=D_MODEL, name=f"{tag}_do")
    parts, dqg, dkg = [], 0.0, 0.0
    for p, (window, dil) in enumerate(C_PATTERNS):
        slopes = jnp.asarray(_alibi(C_HEADS), F32) * float(dil)
        dq, dk, dv, dqg_p, dkg_p, _ = banded_bwd(qkv, qg2, kg2, slopes, None, do, outs[p][0], outs[p][1], ws[p], o,
                                                 do_blk=0, name=f"{tag}_dil{p}_b", **_odd_cfg(dil))
        parts.append((dq, dk, dv))
        dqg = dqg + dqg_p
        dkg = dkg + dkg_p
    dqkv = assemble_odd(parts, tm=TM, name=f"{tag}_asm")
    dwin = mm_tn(h, dqkv, scale=1.0, a_split=False, b_split=False, tm=TM, tn=768, tk=_tk(d), out_blocked=True,
                 name=f"{tag}_dwin")
    dx, dg = mm_nt_normbwd(dqkv, win, x, g, d, a_split=False, tm=TM, tk=768, name=f"{tag}_dx")
    return dx, dg, dwin, _fold2(dqg), _fold2(dkg), dwout


def xa_fwd(x, mem, g, gm, wq, wkv, qg, kg, wo, tag):
    qraw, h = norm_matmul(x, g, wq, tm=TM, tn=D_MODEL, split=False, name=f"{tag}_q")
    kvraw, hm = norm_matmul(mem, gm, wkv, tm=MEM_LEN, tn=512, split=False, name=f"{tag}_kv")
    o = xattn_fwd(qraw, kvraw, qg, kg, tm=TM, name=f"{tag}_att")
    xo = mm_nn(o, wo, res=x, scale=1.0, swiglu=False, tm=TM, tn=D_MODEL, tk=D_MODEL, name=f"{tag}_o")
    return xo, (x, qraw, h, kvraw, hm, o)


def xa_bwd(d, saved, mem, g, gm, wq, wkv, qg, kg, wo, tag):
    x, qraw, h, kvraw, hm, o = saved
    dwo = mm_tn(o, d, scale=1.0, a_split=False, b_split=False, tm=D_MODEL, tn=D_MODEL, tk=_tk(d), name=f"{tag}_dwo")
    do = mm_nt(d, wo, tm=TM, tn=D_MODEL, tk=D_MODEL, name=f"{tag}_do")
    dq, dkv, dqg, dkg = xattn_bwd(qraw, kvraw, qg, kg, do, o, tm=TM, name=f"{tag}_att_b")
    dwq = mm_tn(h, dq, scale=1.0, a_split=False, b_split=False, tm=D_MODEL, tn=D_MODEL, tk=_tk(d), name=f"{tag}_dwq")
    dx, dg = mm_nt_normbwd(dq, wq, x, g, d, a_split=False, tm=TM, tk=D_MODEL, name=f"{tag}_dx")
    dwkv = mm_tn(hm, dkv, scale=1.0, a_split=False, b_split=False, tm=TM, tn=512, tk=MEM_LEN, out_blocked=True,
                 name=f"{tag}_dwkv")
    _, dgm = mm_nt_normbwd(dkv, wkv, mem, gm, None, a_split=False, tm=MEM_LEN, tk=512, name=f"{tag}_dmem")
    return dx, dg, dgm, dwq, dwkv, dqg, dkg, dwo


MATS = (("ffn1_w_gu", 1), ("ffn1_w_down", 0), ("ev_w_in", 1), ("ev_w_out", 0), ("od_w_in", 1), ("od_w_out", 0),
        ("xa_w_q", 0), ("xa_w_kv", 1), ("xa_w_o", 0), ("ffn2_w_gu", 1), ("ffn2_w_down", 0))
SMALLS = ("ffn1_norm", "mix_norm", "ev_q_gain", "ev_k_gain", "ev_sinks", "od_q_gain", "od_k_gain", "xa_norm",
          "xa_mem_norm", "xa_q_gain", "xa_k_gain", "ffn2_norm")
WEIGHTS = ("ffn1_norm", "ffn1_w_gu", "ffn1_w_down", "mix_norm", "ev_w_in", "ev_q_gain", "ev_k_gain", "ev_sinks",
           "ev_w_out", "od_w_in", "od_q_gain", "od_k_gain", "od_w_out", "xa_norm", "xa_mem_norm", "xa_w_q",
           "xa_w_kv", "xa_q_gain", "xa_k_gain", "xa_w_o", "ffn2_norm", "ffn2_w_gu", "ffn2_w_down")
SMALL_ROWS = 16
GROUPS = (
    ((("ffn1_w_gu", 0), ("ffn1_w_gu", 1), ("ffn2_w_gu", 0), ("ffn2_w_gu", 1)), 8, 512),
    ((("ffn1_w_down", 0), ("ffn1_w_down", 1), ("ffn2_w_down", 0), ("ffn2_w_down", 1)), 4, 352),
    ((("ev_w_out", 0), ("od_w_out", 0), ("xa_w_q", 0), ("xa_w_q", 1), ("xa_w_o", 0), ("xa_w_o", 1)), 2, 384),
    ((("xa_w_kv", 0), ("xa_w_kv", 1)), 2, 512),
    ((("ev_w_in", 0),), 1, 512),
    ((("od_w_in", 0),), 1, 512),
)
GROUP_CHUNKS = tuple(g[1] for g in GROUPS)
COL_SHARDED = {name for name, axis in MATS if axis == 1}
BLOCKED = {"ffn1_w_gu", "ffn2_w_gu", "xa_w_kv", "od_w_in"}


def group_halves(shards, c):
    out = []
    for members, _, _ in GROUPS:
        halves = []
        for name, layer in members:
            _, r, cc = shards[name].shape
            half = lax.dynamic_index_in_dim(shards[name][layer].reshape(2, r // 2, cc), c, 0, keepdims=False)
            halves.append(half.astype(BF16))
        out.append(jnp.concatenate(halves, axis=0))
    return out


def full_weights(gathered, shards):
    full = {}
    for (members, _, _), arr in zip(GROUPS, gathered):
        for w, (name, layer) in enumerate(members):
            _, r, cc = shards[name].shape
            piece = arr[:, w * (r // 2):(w + 1) * (r // 2)].reshape(4, r, cc)
            if name not in COL_SHARDED:
                piece = piece.reshape(4 * r, cc)
            elif name not in BLOCKED:
                piece = piece.transpose(1, 0, 2).reshape(r, 4 * cc)
            full[(name, layer)] = piece
    return full


def group_grads(grads, shards):
    out = []
    for members, _, _ in GROUPS:
        parts = []
        for name, layer in members:
            _, r, cc = shards[name].shape
            gfull = grads[(name, layer)]
            if name in COL_SHARDED and name not in BLOCKED:
                gfull = gfull.reshape(2, r // 2, 4, cc).transpose(2, 0, 1, 3)
            parts.append(gfull.reshape(N_DEV, r // 2, cc))
        out.append(jnp.concatenate(parts, axis=1))
    return out


def shard_grads(mine, theirs, c, shards):
    per = {}
    for (members, _, _), a, b in zip(GROUPS, mine, theirs):
        for w, (name, layer) in enumerate(members):
            _, r, cc = shards[name].shape
            rows = slice(w * (r // 2), (w + 1) * (r // 2))
            lo = jnp.where(c == 0, a[rows], b[rows])
            hi = jnp.where(c == 0, b[rows], a[rows])
            per.setdefault(name, []).append(jnp.concatenate([lo, hi], axis=0))
    return {name: jnp.stack(lst, axis=0) for name, lst in per.items()}


def pack_small(vals):
    row10 = jnp.concatenate([vals["xa_q_gain"].reshape(1, 512), vals["xa_k_gain"].reshape(1, 512)], axis=1)
    row11 = jnp.concatenate([vals["ev_q_gain"], vals["ev_k_gain"], vals["od_q_gain"], vals["od_k_gain"],
                             vals["ev_sinks"], jnp.zeros((1, 1024 - 4 * 64 - 8), F32)], axis=1)
    return jnp.concatenate([vals["ffn1_norm"], vals["mix_norm"], vals["xa_norm"], vals["xa_mem_norm"],
                            vals["ffn2_norm"], row10, row11, jnp.zeros((SMALL_ROWS - 12, 1024), F32)], axis=0)


def unpack_small(arr):
    return {"ffn1_norm": arr[0:2], "mix_norm": arr[2:4], "xa_norm": arr[4:6], "xa_mem_norm": arr[6:8],
            "ffn2_norm": arr[8:10],
            "xa_q_gain": arr[10:11, 0:512].reshape(2, 256), "xa_k_gain": arr[10:11, 512:1024].reshape(2, 256),
            "ev_q_gain": arr[11:12, 0:64], "ev_k_gain": arr[11:12, 64:128], "od_q_gain": arr[11:12, 128:192],
            "od_k_gain": arr[11:12, 192:256], "ev_sinks": arr[11:12, 256:264]}


def local_step(x, mem, target, W, small):
    depth = small["ffn1_norm"].shape[0]

    def row(name, l):
        return small[name][l:l + 1]

    saved = []
    for l in range(depth):
        j = l // 2
        x, s1 = ffn_fwd(x, row("ffn1_norm", l), W[("ffn1_w_gu", l)], W[("ffn1_w_down", l)], f"l{l}_f1")
        if l % 2 == 0:
            x, s2 = even_fwd(x, row("mix_norm", l), W[("ev_w_in", j)], row("ev_q_gain", j), row("ev_k_gain", j),
                             small["ev_sinks"][j], W[("ev_w_out", j)], f"l{l}_ev")
        else:
            x, s2 = odd_fwd(x, row("mix_norm", l), W[("od_w_in", j)], row("od_q_gain", j), row("od_k_gain", j),
                            W[("od_w_out", j)], f"l{l}_od")
        x, s3 = xa_fwd(x, mem, row("xa_norm", l), row("xa_mem_norm", l), W[("xa_w_q", l)], W[("xa_w_kv", l)],
                       row("xa_q_gain", l), row("xa_k_gain", l), W[("xa_w_o", l)], f"l{l}_xa")
        x, s4 = ffn_fwd(x, row("ffn2_norm", l), W[("ffn2_w_gu", l)], W[("ffn2_w_down", l)], f"l{l}_f2")
        saved.append((s1, s2, s3, s4))
    loss, d = loss_kernel(x, target, tm=TM, name="loss")

    gw = {}
    gs = {name: [None] * small[name].shape[0] for name in SMALLS}
    for l in reversed(range(depth)):
        j = l // 2
        s1, s2, s3, s4 = saved[l]
        d, dg, dwgu, dwd = ffn_bwd(d, s4, row("ffn2_norm", l), W[("ffn2_w_gu", l)], W[("ffn2_w_down", l)], f"l{l}_f2")
        gs["ffn2_norm"][l] = dg
        gw[("ffn2_w_gu", l)], gw[("ffn2_w_down", l)] = dwgu, dwd
        d, dg, dgm, dwq, dwkv, dqg, dkg, dwo = xa_bwd(
            d, s3, mem, row("xa_norm", l), row("xa_mem_norm", l), W[("xa_w_q", l)], W[("xa_w_kv", l)],
            row("xa_q_gain", l), row("xa_k_gain", l), W[("xa_w_o", l)], f"l{l}_xa")
        gs["xa_norm"][l], gs["xa_mem_norm"][l], gs["xa_q_gain"][l], gs["xa_k_gain"][l] = dg, dgm, dqg, dkg
        gw[("xa_w_q", l)], gw[("xa_w_kv", l)], gw[("xa_w_o", l)] = dwq, dwkv, dwo
        if l % 2 == 0:
            d, dg, dwin, dqg, dkg, dsk, dwout = even_bwd(
                d, s2, row("mix_norm", l), W[("ev_w_in", j)], row("ev_q_gain", j), row("ev_k_gain", j),
                small["ev_sinks"][j], W[("ev_w_out", j)], f"l{l}_ev")
            gs["ev_q_gain"][j], gs["ev_k_gain"][j], gs["ev_sinks"][j] = dqg, dkg, dsk
            gw[("ev_w_in", j)], gw[("ev_w_out", j)] = dwin, dwout
        else:
            d, dg, dwin, dqg, dkg, dwout = odd_bwd(
                d, s2, row("mix_norm", l), W[("od_w_in", j)], row("od_q_gain", j), row("od_k_gain", j),
                W[("od_w_out", j)], f"l{l}_od")
            gs["od_q_gain"][j], gs["od_k_gain"][j] = dqg, dkg
            gw[("od_w_in", j)], gw[("od_w_out", j)] = dwin, dwout
        gs["mix_norm"][l] = dg
        d, dg, dwgu, dwd = ffn_bwd(d, s1, row("ffn1_norm", l), W[("ffn1_w_gu", l)], W[("ffn1_w_down", l)], f"l{l}_f1")
        gs["ffn1_norm"][l] = dg
        gw[("ffn1_w_gu", l)], gw[("ffn1_w_down", l)] = dwgu, dwd
    gsmall = {name: jnp.concatenate(v, axis=0) for name, v in gs.items()}
    return loss, d, gw, gsmall


def kernel(x, mem, ffn1_norm, ffn1_w_gu, ffn1_w_down, mix_norm, ev_w_in, ev_q_gain, ev_k_gain, ev_sinks, ev_w_out, od_w_in, od_q_gain, od_k_gain, od_w_out, xa_norm, xa_mem_norm, xa_w_q, xa_w_kv, xa_q_gain, xa_k_gain, xa_w_o, ffn2_norm, ffn2_w_gu, ffn2_w_down, loss_target, m_ffn1_norm, m_ffn1_w_gu, m_ffn1_w_down, m_mix_norm, m_ev_w_in, m_ev_q_gain, m_ev_k_gain, m_ev_sinks, m_ev_w_out, m_od_w_in, m_od_q_gain, m_od_k_gain, m_od_w_out, m_xa_norm, m_xa_mem_norm, m_xa_w_q, m_xa_w_kv, m_xa_q_gain, m_xa_k_gain, m_xa_w_o, m_ffn2_norm, m_ffn2_w_gu, m_ffn2_w_down, v_ffn1_norm, v_ffn1_w_gu, v_ffn1_w_down, v_mix_norm, v_ev_w_in, v_ev_q_gain, v_ev_k_gain, v_ev_sinks, v_ev_w_out, v_od_w_in, v_od_q_gain, v_od_k_gain, v_od_w_out, v_xa_norm, v_xa_mem_norm, v_xa_w_q, v_xa_w_kv, v_xa_q_gain, v_xa_k_gain, v_xa_w_o, v_ffn2_norm, v_ffn2_w_gu, v_ffn2_w_down):
    given = dict(locals())
    w = {n: given[n] for n in WEIGHTS}
    m = {n: given["m_" + n] for n in WEIGHTS}
    v = {n: given["v_" + n] for n in WEIGHTS}
    c = lax.axis_index("c")
    shards = {name: w[name] for name, _ in MATS}
    small = {n: w[n] for n in SMALLS}

    gathered = gather_blocks(group_halves(shards, c), GROUP_CHUNKS, name="gather_weights")
    full = full_weights(gathered, shards)
    loss_b, grad_x, gw, gsmall = local_step(x[0], mem[0], loss_target[0], full, small)

    packed = group_grads(gw, shards)
    sib = pair_exchange(packed, GROUP_CHUNKS, name="pair_grads")
    pre = [pair_sum(p, s, c, tr=g[2], name=f"pair_sum{i}") for i, (g, p, s) in enumerate(zip(GROUPS, packed, sib))]
    land = chip_scatter(pre, GROUP_CHUNKS, name="scatter_grads")
    mine = [reduce_slots(a, tr=g[2], name=f"sum_grads{i}") for i, (g, a) in enumerate(zip(GROUPS, land))]
    theirs = sibling_send(mine, GROUP_CHUNKS, name="swap_grads")
    land_small = gather_small(pack_small(gsmall), name="gather_small")
    g_small = unpack_small(reduce_slots(land_small, tr=SMALL_ROWS, name="sum_small"))
    g = dict(shard_grads(mine, theirs, c, shards))
    g.update(g_small)

    delta, new_m, new_v = {}, {}, {}
    for name, _ in MATS:
        shp = w[name].shape
        flat = [a.reshape(-1, shp[-1]) for a in (w[name], g[name], m[name], v[name])]
        dl, nm, nv = adamw(*flat, br=BLK, name=f"adamw_{name}")
        delta[name], new_m[name], new_v[name] = dl.reshape(shp), nm.reshape(shp), nv.reshape(shp)
    dl, nm, nv = adamw(pack_small(small), pack_small(g_small), pack_small({n: m[n] for n in SMALLS}),
                       pack_small({n: v[n] for n in SMALLS}), br=SMALL_ROWS, name="adamw_small")
    for dst, arr in ((delta, dl), (new_m, nm), (new_v, nv)):
        dst.update(unpack_small(arr))

    loss = lax.psum(loss_b[0, 0], ("x", "y", "c"))
    return (loss, grad_x[None], *[g[n] for n in WEIGHTS], *[delta[n] for n in WEIGHTS],
            *[new_m[n] for n in WEIGHTS], *[new_v[n] for n in WEIGHTS])
```

```python
import jax
import jax.numpy as jnp
from jax import lax
from jax.experimental import pallas as pl
from jax.experimental.pallas import tpu as pltpu

F32 = jnp.float32
BF16 = jnp.bfloat16

D_MODEL = 1024
HEAD_DIM = 64
LANES = 128
BLK = 128
D_FF = 2816
RMS_EPS = 1e-6
MEM_LEN = 256
X_HEADS = 4
X_HEAD_DIM = 256
A_Q_HEADS = 8
A_GROUP = 4
A_WINDOW = 128
C_HEADS = 16
C_PATTERNS = ((128, 1), (512, 4), (2048, 16))
NEG = -1e30
VMEM_LIMIT = 56 * 2 ** 20

ADAM_LR = 0.001
ADAM_B1 = 0.9
ADAM_B2 = 0.999
ADAM_EPS = 1e-08
ADAM_WD = 0.01
ADAM_STEP = 10

N_DEV = 8
MESH = pl.DeviceIdType.MESH


def _cparams(n):
    return pltpu.CompilerParams(dimension_semantics=("arbitrary",) * n, vmem_limit_bytes=VMEM_LIMIT)


def _dot(a, b):
    return jnp.dot(a, b, preferred_element_type=F32)


def _dot_nt(a, b):
    return lax.dot_general(a, b, (((1,), (1,)), ((), ())), preferred_element_type=F32)


def _dot_tn(a, b):
    return lax.dot_general(a, b, (((0,), (0,)), ((), ())), preferred_element_type=F32)


def _sigmoid(z):
    return 1.0 / (1.0 + jnp.exp(-z))


def norm_matmul(x, g, w, *, tm, tn, split, name):
    T, K = x.shape
    blocked = w.ndim == 3
    assert not blocked or w.shape[2] == tn
    N = w.shape[0] * w.shape[2] if blocked else w.shape[1]
    nj = N // tn

    def body(x_ref, g_ref, w_ref, o_ref, h_ref):
        @pl.when(pl.program_id(1) == 0)
        def _():
            xv = x_ref[...]
            r = lax.rsqrt(jnp.mean(xv * xv, axis=-1, keepdims=True) + RMS_EPS)
            h_ref[...] = (xv * r * g_ref[...]).astype(BF16)

        o_ref[...] = _dot(h_ref[...], w_ref[...])

    if split:
        njh = nj // 2
        o_shape = jax.ShapeDtypeStruct((2, T, N // 2), F32)
        o_spec = pl.BlockSpec((None, tm, tn), lambda i, j: (j // njh, i, j % njh))
    else:
        o_shape = jax.ShapeDtypeStruct((T, N), F32)
        o_spec = pl.BlockSpec((tm, tn), lambda i, j: (i, j))
    return pl.pallas_call(
        body, grid=(T // tm, nj),
        in_specs=[pl.BlockSpec((tm, K), lambda i, j: (i, 0)),
                  pl.BlockSpec((1, K), lambda i, j: (0, 0)),
                  (pl.BlockSpec((None, K, tn), lambda i, j: (j, 0, 0)) if blocked
                   else pl.BlockSpec((K, tn), lambda i, j: (0, j)))],
        out_specs=[o_spec, pl.BlockSpec((tm, K), lambda i, j: (i, 0))],
        out_shape=[o_shape, jax.ShapeDtypeStruct((T, K), BF16)],
        compiler_params=_cparams(2), name=name)(x, g, w)


def mm_nn(a, b, *, res, scale, swiglu, tm, tn, tk, name):
    T = a.shape[-2]
    K, N = b.shape
    nk = K // tk

    def body(*refs):
        if swiglu:
            g_ref, u_ref, b_ref, r_ref, o_ref, acc = refs
        else:
            a_ref, b_ref, r_ref, o_ref, acc = refs
        k = pl.program_id(2)

        @pl.when(k == 0)
        def _():
            acc[...] = jnp.zeros_like(acc)

        if swiglu:
            gv = g_ref[...]
            av = (gv * _sigmoid(gv) * u_ref[...]).astype(BF16)
        else:
            av = a_ref[...].astype(BF16)
        acc[...] += _dot(av, b_ref[...])

        @pl.when(k == nk - 1)
        def _():
            o_ref[...] = r_ref[...] + scale * acc[...]

    if swiglu:
        a_specs = [pl.BlockSpec((None, tm, tk), lambda i, j, k: (0, i, k)),
                   pl.BlockSpec((None, tm, tk), lambda i, j, k: (1, i, k))]
        a_args = [a, a]
    else:
        a_specs = [pl.BlockSpec((tm, tk), lambda i, j, k: (i, k))]
        a_args = [a]
    return pl.pallas_call(
        body, grid=(T // tm, N // tn, nk),
        in_specs=a_specs + [pl.BlockSpec((tk, tn), lambda i, j, k: (k, j)),
                            pl.BlockSpec((tm, tn), lambda i, j, k: (i, j))],
        out_specs=pl.BlockSpec((tm, tn), lambda i, j, k: (i, j)),
        out_shape=jax.ShapeDtypeStruct((T, N), F32),
        scratch_shapes=[pltpu.VMEM((tm, tn), F32)],
        compiler_params=_cparams(3), name=name)(*a_args, b, res)


def mm_nt(a, b, *, tm, tn, tk, name):
    T, K = a.shape
    N = b.shape[0]
    nk = K // tk

    def body(a_ref, b_ref, o_ref, acc):
        k = pl.program_id(2)

        @pl.when(k == 0)
        def _():
            acc[...] = jnp.zeros_like(acc)

        acc[...] += _dot_nt(a_ref[...].astype(BF16), b_ref[...])

        @pl.when(k == nk - 1)
        def _():
            o_ref[...] = acc[...]

    return pl.pallas_call(
        body, grid=(T // tm, N // tn, nk),
        in_specs=[pl.BlockSpec((tm, tk), lambda i, j, k: (i, k)),
                  pl.BlockSpec((tn, tk), lambda i, j, k: (j, k))],
        out_specs=pl.BlockSpec((tm, tn), lambda i, j, k: (i, j)),
        out_shape=jax.ShapeDtypeStruct((T, N), F32),
        scratch_shapes=[pltpu.VMEM((tm, tn), F32)],
        compiler_params=_cparams(3), name=name)(a, b)


def ffn_bwd_act(d, wd, gu, *, tm, tn, name):
    T, K = d.shape
    Fd = wd.shape[0]

    def body(d_ref, w_ref, g_ref, u_ref, dgu_ref, act_ref):
        da = 0.5 * _dot_nt(d_ref[...].astype(BF16), w_ref[...])
        gv = g_ref[...]
        uv = u_ref[...]
        s = _sigmoid(gv)
        silu = gv * s
        act_ref[...] = (silu * uv).astype(BF16)
        dgu_ref[0] = (da * uv * (s * (1.0 + gv * (1.0 - s)))).astype(BF16)
        dgu_ref[1] = (da * silu).astype(BF16)

    return pl.pallas_call(
        body, grid=(Fd // tn, T // tm),
        in_specs=[pl.BlockSpec((tm, K), lambda j, i: (i, 0)),
                  pl.BlockSpec((tn, K), lambda j, i: (j, 0)),
                  pl.BlockSpec((None, tm, tn), lambda j, i: (0, i, j)),
                  pl.BlockSpec((None, tm, tn), lambda j, i: (1, i, j))],
        out_specs=[pl.BlockSpec((2, tm, tn), lambda j, i: (0, i, j)),
                   pl.BlockSpec((tm, tn), lambda j, i: (i, j))],
        out_shape=[jax.ShapeDtypeStruct((2, T, Fd), BF16), jax.ShapeDtypeStruct((T, Fd), BF16)],
        compiler_params=_cparams(2), name=name)(d, wd, gu, gu)


def mm_nt_normbwd(a, b, x, g, res, *, a_split, tm, tk, name):
    T, Dm = x.shape
    blocked = b.ndim == 3
    assert not blocked or b.shape[2] == tk
    K = b.shape[0] * b.shape[2] if blocked else b.shape[1]
    nk = K // tk
    nkh = nk // 2
    has_res = res is not None

    def body(*refs):
        if has_res:
            a_ref, b_ref, x_ref, g_ref, r_ref, dx_ref, dg_ref, acc = refs
        else:
            a_ref, b_ref, x_ref, g_ref, dx_ref, dg_ref, acc = refs
        i = pl.program_id(0)
        k = pl.program_id(1)

        @pl.when(k == 0)
        def _():
            acc[...] = jnp.zeros_like(acc)

        acc[...] += _dot_nt(a_ref[...].astype(BF16), b_ref[...])

        @pl.when(k == nk - 1)
        def _():
            xv = x_ref[...]
            r = lax.rsqrt(jnp.mean(xv * xv, axis=-1, keepdims=True) + RMS_EPS)
            xh = xv * r
            dh = acc[...]
            dxh = dh * g_ref[...]
            dx = r * (dxh - xh * jnp.mean(dxh * xh, axis=-1, keepdims=True))
            if has_res:
                dx = dx + r_ref[...]
            dx_ref[...] = dx
            part = jnp.sum(dh * xh, axis=0, keepdims=True)

            @pl.when(i == 0)
            def _():
                dg_ref[...] = part

            @pl.when(i > 0)
            def _():
                dg_ref[...] += part

    if a_split:
        a_spec = pl.BlockSpec((None, tm, tk), lambda i, k: (k // nkh, i, k % nkh))
    else:
        a_spec = pl.BlockSpec((tm, tk), lambda i, k: (i, k))
    in_specs = [a_spec,
                (pl.BlockSpec((None, Dm, tk), lambda i, k: (k, 0, 0)) if blocked
                 else pl.BlockSpec((Dm, tk), lambda i, k: (0, k))),
                pl.BlockSpec((tm, Dm), lambda i, k: (i, 0)),
                pl.BlockSpec((1, Dm), lambda i, k: (0, 0))]
    args = [a, b, x, g]
    if has_res:
        in_specs.append(pl.BlockSpec((tm, Dm), lambda i, k: (i, 0)))
        args.append(res)
    return pl.pallas_call(
        body, grid=(T // tm, nk), in_specs=in_specs,
        out_specs=[pl.BlockSpec((tm, Dm), lambda i, k: (i, 0)),
                   pl.BlockSpec((1, Dm), lambda i, k: (0, 0))],
        out_shape=[jax.ShapeDtypeStruct((T, Dm), F32), jax.ShapeDtypeStruct((1, Dm), F32)],
        scratch_shapes=[pltpu.VMEM((tm, Dm), F32)],
        compiler_params=_cparams(2), name=name)(*args)


def mm_tn(a, b, *, scale, a_split, b_split, tm, tn, tk, name, out_blocked=False):
    T = a.shape[-2]
    M = a.shape[-1] * (2 if a_split else 1)
    N = b.shape[-1] * (2 if b_split else 1)
    ni, nj, nk = M // tm, N // tn, T // tk
    nih, njh = ni // 2, nj // 2

    def body(a_ref, b_ref, o_ref, acc):
        k = pl.program_id(2)

        @pl.when(k == 0)
        def _():
            acc[...] = jnp.zeros_like(acc)

        acc[...] += _dot_tn(a_ref[...].astype(BF16), b_ref[...].astype(BF16))

        @pl.when(k == nk - 1)
        def _():
            o_ref[...] = (acc[...] * scale).astype(o_ref.dtype)

    if a_split:
        a_spec = pl.BlockSpec((None, tk, tm), lambda i, j, k: (i // nih, k, i % nih))
    else:
        a_spec = pl.BlockSpec((tk, tm), lambda i, j, k: (k, i))
    if b_split:
        b_spec = pl.BlockSpec((None, tk, tn), lambda i, j, k: (j // njh, k, j % njh))
    else:
        b_spec = pl.BlockSpec((tk, tn), lambda i, j, k: (k, j))
    if out_blocked:
        o_spec = pl.BlockSpec((None, None, tm, tn), lambda i, j, k: (j, i, 0, 0))
        o_shape = jax.ShapeDtypeStruct((nj, ni, tm, tn), BF16)
    else:
        o_spec = pl.BlockSpec((tm, tn), lambda i, j, k: (i, j))
        o_shape = jax.ShapeDtypeStruct((M, N), BF16)
    return pl.pallas_call(
        body, grid=(ni, nj, nk), in_specs=[a_spec, b_spec], out_specs=o_spec, out_shape=o_shape,
        scratch_shapes=[pltpu.VMEM((tm, tn), F32)],
        compiler_params=_cparams(3), name=name)(a, b)


def loss_kernel(y, target, *, tm, name):
    T, Dm = y.shape

    def body(y_ref, t_ref, l_ref, dy_ref):
        e = y_ref[...] - t_ref[...]
        dy_ref[...] = e * (1.0 / Dm)
        part = (0.5 / Dm) * jnp.sum(jnp.sum(e * e, axis=-1, keepdims=True), axis=0, keepdims=True)
        part = jnp.broadcast_to(part, (8, LANES))

        @pl.when(pl.program_id(0) == 0)
        def _():
            l_ref[...] = part

        @pl.when(pl.program_id(0) > 0)
        def _():
            l_ref[...] += part

    return pl.pallas_call(
        body, grid=(T // tm,),
        in_specs=[pl.BlockSpec((tm, Dm), lambda i: (i, 0)), pl.BlockSpec((tm, Dm), lambda i: (i, 0))],
        out_specs=[pl.BlockSpec((8, LANES), lambda i: (0, 0)), pl.BlockSpec((tm, Dm), lambda i: (i, 0))],
        out_shape=[jax.ShapeDtypeStruct((8, LANES), F32), jax.ShapeDtypeStruct((T, Dm), F32)],
        compiler_params=_cparams(1), name=name)(y, target)


def adamw(w, g, m, v, *, br, name):
    R, C = w.shape

    def body(w_ref, g_ref, m_ref, v_ref, d_ref, nm_ref, nv_ref):
        gv = g_ref[...]
        nm = ADAM_B1 * m_ref[...] + (1.0 - ADAM_B1) * gv
        nv = ADAM_B2 * v_ref[...] + (1.0 - ADAM_B2) * (gv * gv)
        m_hat = nm / (1.0 - ADAM_B1 ** ADAM_STEP)
        v_hat = nv / (1.0 - ADAM_B2 ** ADAM_STEP)
        d_ref[...] = -ADAM_LR * (m_hat / (jnp.sqrt(v_hat) + ADAM_EPS) + ADAM_WD * w_ref[...])
        nm_ref[...] = nm
        nv_ref[...] = nv

    spec = pl.BlockSpec((br, C), lambda i: (i, 0))
    shp = jax.ShapeDtypeStruct((R, C), F32)
    return pl.pallas_call(
        body, grid=(R // br,), in_specs=[spec] * 4, out_specs=[spec] * 3, out_shape=[shp] * 3,
        compiler_params=_cparams(1), name=name)(w, g, m, v)


def _lane0():
    return lax.broadcasted_iota(jnp.int32, (1, LANES), 1) < HEAD_DIM


def _half_sum(x, m0):
    s0 = jnp.sum(jnp.where(m0, x, 0.0), axis=-1, keepdims=True)
    s1 = jnp.sum(jnp.where(m0, 0.0, x), axis=-1, keepdims=True)
    return jnp.where(m0, s0, s1)


def _half_pick(x, m0, e):
    sel = m0 if e == 0 else jnp.logical_not(m0)
    return jnp.max(jnp.where(sel, x, NEG), axis=-1, keepdims=True)


def _head_rms(x, m0):
    return lax.rsqrt(_half_sum(x * x, m0) * (1.0 / HEAD_DIM) + RMS_EPS)


def _alibi(n):
    return [float(2.0 ** (-8.0 * (h + 1) / n)) for h in range(n)]


def _mask_half(x, m0, e):
    return jnp.where(m0, x, 0.0) if e == 0 else jnp.where(m0, 0.0, x)


def _band_masks2(max_dist, has_prev, live):
    row = lax.broadcasted_iota(jnp.int32, (2 * BLK, 2 * BLK), 0)
    col = lax.broadcasted_iota(jnp.int32, (2 * BLK, 2 * BLK), 1)
    dist = (row & (BLK - 1)) - col + BLK
    lim = jnp.where(live, max_dist, -1)
    first = jnp.where(has_prev, 0, BLK)
    valid = (dist >= 0) & (dist <= lim) & (col >= first)
    top = lax.broadcasted_iota(jnp.int32, (2 * BLK, 1), 0) < BLK
    return dist.astype(F32), valid, top


def _stack_heads(x, m0, kes):
    parts = []
    for e in range(2):
        h = _mask_half(x, m0, e)
        parts.append(pltpu.roll(h, HEAD_DIM, 1) if kes[e] != e else h)
    return jnp.concatenate(parts, axis=0)


def _unstack_heads(y, m0, kes):
    parts = []
    for e in range(2):
        h = y[e * BLK:(e + 1) * BLK]
        parts.append(pltpu.roll(h, HEAD_DIM, 1) if kes[e] != e else h)
    return jnp.where(m0, parts[0], parts[1])


def _rows(r, dil):
    return pl.ds(r, BLK, stride=dil) if dil > 1 else pl.ds(0, BLK)


def _band_units(dil, nsub):
    assert dil == 1 or nsub == 1
    if nsub == 1:
        return [(_rows(r, dil), ("prev", _rows(r, dil)), 0) for r in range(dil)]
    units = [(pl.ds(0, BLK), ("prev", pl.ds(0, BLK)), 0)]
    units += [(pl.ds(BLK * s, BLK), ("cur", pl.ds(BLK * (s - 1), BLK)), s) for s in range(1, nsub)]
    return units


def _band_specs(dil, nsub, ppk, q_blk, k_blk, v_blk, kv_shared, nb):
    RB = BLK * dil * nsub
    PB = BLK if nsub > 1 else RB
    qw = LANES * ppk
    kw = LANES if kv_shared else qw

    def cur(i):
        return jnp.minimum(i, nb - 1)

    def prev(i):
        return jnp.maximum(i * nsub - 1, 0) if nsub > 1 else jnp.maximum(i - 1, 0)

    def kidx(base):
        return (lambda p, i: (cur(i), base)) if kv_shared else (lambda p, i: (cur(i), base + p))

    def pidx(base):
        return (lambda p, i: (prev(i), base)) if kv_shared else (lambda p, i: (prev(i), base + p))

    return [pl.BlockSpec((RB, qw), lambda p, i: (cur(i), q_blk + p)),
            pl.BlockSpec((RB, kw), kidx(k_blk)), pl.BlockSpec((PB, kw), pidx(k_blk)),
            pl.BlockSpec((RB, kw), kidx(v_blk)), pl.BlockSpec((PB, kw), pidx(v_blk))]


def banded_fwd(qkv, q_gain2, k_gain2, slopes, sinks, *, dil, nsub, ppk, q_blk, k_blk, v_blk, n_heads, group,
               max_dist, name):
    T = qkv.shape[0]
    RB = BLK * dil * nsub
    nb = T // RB
    npair = n_heads // 2
    kv_shared = group > 1
    scale = HEAD_DIM ** -0.5
    has_sink = sinks is not None

    def body(*refs):
        slope_ref = refs[0]
        if has_sink:
            sink_ref, refs = refs[1], refs[2:]
        else:
            refs = refs[1:]
        q_ref, kc_ref, kp_ref, vc_ref, vp_ref, qg_ref, kg_ref, o_ref, l_ref = refs
        pb = pl.program_id(0)
        i = pl.program_id(1)
        m0 = _lane0()
        distf, valid_first, top = _band_masks2(max_dist, i > 0, i >= 0)
        valid_inner = _band_masks2(max_dist, i >= 0, i >= 0)[1] if nsub > 1 else None
        qg = qg_ref[...]
        kg = kg_ref[...]
        for rows, (src, prows), sub in _band_units(dil, nsub):
            valid = valid_first if sub == 0 else valid_inner
            kpr, vpr = (kp_ref, vp_ref) if src == "prev" else (kc_ref, vc_ref)
            kcache = {}
            for jp in range(ppk):
                cs = pl.ds(LANES * jp, LANES)
                jk = 0 if kv_shared else jp
                if jk not in kcache:
                    ks = pl.ds(LANES * jk, LANES)
                    kcat = jnp.concatenate([kpr[prows, ks], kc_ref[rows, ks]], axis=0)
                    vcat = jnp.concatenate([vpr[prows, ks], vc_ref[rows, ks]], axis=0)
                    kcache[jk] = ((kcat * _head_rms(kcat, m0) * kg).astype(BF16), vcat.astype(BF16))
                kn, vcat = kcache[jk]
                qv = q_ref[rows, cs]
                qn = qv * _head_rms(qv, m0) * qg
                kes = [((2 * jp + e) // group) % 2 if kv_shared else e for e in range(2)]
                hidx = 2 * (pb * ppk + jp)
                qs = _stack_heads(qn, m0, kes).astype(BF16)
                slope = jnp.where(top, slope_ref[hidx], slope_ref[hidx + 1])
                s = jnp.where(valid, _dot_nt(qs, kn) * scale - slope * distf, NEG)
                m = jnp.max(s, axis=-1, keepdims=True)
                if has_sink:
                    sk = jnp.where(top, sink_ref[hidx], sink_ref[hidx + 1])
                    m = jnp.maximum(m, sk)
                p = jnp.exp(s - m)
                den = jnp.sum(p, axis=-1, keepdims=True)
                if has_sink:
                    den = den + jnp.exp(sk - m)
                o_full = _dot((p * (1.0 / den)).astype(BF16), vcat)
                o_ref[rows, cs] = _unstack_heads(o_full, m0, kes)
                l_ref[rows, cs] = _unstack_heads(jnp.broadcast_to(m + jnp.log(den), (2 * BLK, LANES)), m0, [0, 1])

    smem = pl.BlockSpec(memory_space=pltpu.SMEM)
    qw = LANES * ppk
    gspec = pl.BlockSpec((1, LANES), lambda p, i: (0, 0))
    ospec = pl.BlockSpec((RB, qw), lambda p, i: (i, p))
    oshape = jax.ShapeDtypeStruct((T, n_heads * HEAD_DIM), F32)
    args = [slopes] + ([sinks] if has_sink else []) + [qkv] * 5 + [q_gain2, k_gain2]
    return pl.pallas_call(
        body, grid=(npair // ppk, nb),
        in_specs=[smem] * (2 if has_sink else 1) + _band_specs(dil, nsub, ppk, q_blk, k_blk, v_blk, kv_shared, nb)
        + [gspec, gspec],
        out_specs=[ospec, ospec], out_shape=[oshape, oshape],
        compiler_params=_cparams(2), name=name)(*args)


def banded_bwd(qkv, q_gain2, k_gain2, slopes, sinks, do, o, lse, w, omix, *, dil, nsub, ppk, q_blk, k_blk, v_blk,
               n_heads, group, max_dist, do_blk, name):
    T = qkv.shape[0]
    RB = BLK * dil * nsub
    nb = T // RB
    npair = n_heads // 2
    kv_shared = group > 1
    scale = HEAD_DIM ** -0.5
    has_sink = sinks is not None
    mixed = w is not None
    qw = LANES * ppk

    def body(*refs):
        slope_ref = refs[0]
        if has_sink:
            sink_ref, refs = refs[1], refs[2:]
        else:
            refs = refs[1:]
        q_ref, kc_ref, kp_ref, vc_ref, vp_ref, qg_ref, kg_ref, do_ref, o_ref, l_ref = refs[:10]
        refs = refs[10:]
        if mixed:
            w_ref, om_ref, refs = refs[0], refs[1], refs[2:]
        dq_ref, dk_ref, dv_ref, dqg_ref, dkg_ref, dsk_ref, ck_ref, cv_ref = refs
        pb = pl.program_id(0)
        i = pl.program_id(1)
        live = i < nb
        m0 = _lane0()
        lane = lax.broadcasted_iota(jnp.int32, (1, LANES), 1)
        distf, valid_first, top = _band_masks2(max_dist, i > 0, live)
        valid_inner = _band_masks2(max_dist, i >= 0, live)[1] if nsub > 1 else None
        livef = live.astype(F32)
        qg = qg_ref[...]
        kg = kg_ref[...]

        def stack_rows(x2):
            return jnp.concatenate([_half_pick(x2, m0, 0), _half_pick(x2, m0, 1)], axis=0)

        @pl.when((pb == 0) & (i == 0))
        def _():
            dqg_ref[...] = jnp.zeros_like(dqg_ref)
            dkg_ref[...] = jnp.zeros_like(dkg_ref)
            dsk_ref[...] = jnp.zeros_like(dsk_ref)

        @pl.when(i == 0)
        def _():
            ck_ref[...] = jnp.zeros_like(ck_ref)
            cv_ref[...] = jnp.zeros_like(cv_ref)

        dqg_acc = jnp.zeros((1, LANES), F32)
        dkg_acc = jnp.zeros((1, LANES), F32)
        dsk_acc = jnp.zeros((1, LANES), F32)
        if nsub > 1:
            dk_ref[...] = ck_ref[...]
            dv_ref[...] = cv_ref[...]
        for rows, (src, prows), sub in _band_units(dil, nsub):
            valid = valid_first if sub == 0 else valid_inner
            kpr, vpr = (kp_ref, vp_ref) if src == "prev" else (kc_ref, vc_ref)
            for jp in range(ppk):
                cs = pl.ds(LANES * jp, LANES)
                ks = pl.ds(0, LANES) if kv_shared else cs
                kcat = jnp.concatenate([kpr[prows, ks], kc_ref[rows, ks]], axis=0)
                vcat = jnp.concatenate([vpr[prows, ks], vc_ref[rows, ks]], axis=0).astype(BF16)
                rk = _head_rms(kcat, m0)
                kh = kcat * rk
                kn = (kh * kg).astype(BF16)
                qv = q_ref[rows, cs]
                rq = _head_rms(qv, m0)
                qh = qv * rq
                dov = do_ref[rows, cs]
                lv = l_ref[rows, cs]
                if mixed:
                    wv = w_ref[rows, cs]
                    dmix = _half_sum(dov * om_ref[rows, cs], m0)
                    dov = dov * wv
                delta2 = _half_sum(dov * o_ref[rows, cs], m0)
                shift = stack_rows(wv * dmix if mixed else delta2)
                kes = [((2 * jp + e) // group) % 2 if kv_shared else e for e in range(2)]
                hidx = 2 * (pb * ppk + jp)
                qs = _stack_heads(qh * qg, m0, kes).astype(BF16)
                dos = _stack_heads(dov, m0, kes).astype(BF16)
                lse = stack_rows(lv)
                slope = jnp.where(top, slope_ref[hidx], slope_ref[hidx + 1])
                p = jnp.where(valid, jnp.exp(_dot_nt(qs, kn) * scale - slope * distf - lse), 0.0)
                ds = (p * (_dot_nt(dos, vcat) - shift)).astype(BF16)
                dqn = _unstack_heads(_dot(ds, kn), m0, kes) * scale
                dkn = _dot_tn(ds, qs) * scale
                dvv = _dot_tn(p.astype(BF16), dos)
                if has_sink:
                    sk = jnp.where(top, sink_ref[hidx], sink_ref[hidx + 1])
                    contrib = -jnp.exp(sk - lse) * stack_rows(delta2) * livef
                    for e in range(2):
                        tot = jnp.sum(contrib[e * BLK:(e + 1) * BLK], axis=0, keepdims=True)
                        dsk_acc = dsk_acc + jnp.where(lane == (2 * jp + e), tot, 0.0)
                dqg_acc = dqg_acc + jnp.sum(dqn * qh, axis=0, keepdims=True)
                dqh = dqn * qg
                dq_raw = rq * (dqh - qh * (_half_sum(dqh * qh, m0) * (1.0 / HEAD_DIM)))
                dkg_acc = dkg_acc + jnp.sum(dkn * kh, axis=0, keepdims=True)
                dkh = dkn * kg
                dk_raw = rk * (dkh - kh * (_half_sum(dkh * kh, m0) * (1.0 / HEAD_DIM)))

                @pl.when(live)
                def _():
                    dq_ref[rows, cs] = dq_raw

                if nsub == 1:
                    dk_ref[rows, cs] = ck_ref[rows, cs] + dk_raw[:BLK]
                    dv_ref[rows, cs] = cv_ref[rows, cs] + dvv[:BLK]
                elif sub == 0:
                    last = pl.ds(RB - BLK, BLK)
                    dk_ref[last, cs] += dk_raw[:BLK]
                    dv_ref[last, cs] += dvv[:BLK]
                else:
                    ck_ref[prows, cs] += dk_raw[:BLK]
                    cv_ref[prows, cs] += dvv[:BLK]
                ck_ref[rows, cs] = dk_raw[BLK:]
                cv_ref[rows, cs] = dvv[BLK:]
        dqg_ref[...] += dqg_acc
        dkg_ref[...] += dkg_acc
        dsk_ref[...] += dsk_acc

    smem = pl.BlockSpec(memory_space=pltpu.SMEM)
    gspec = pl.BlockSpec((1, LANES), lambda p, i: (0, 0))

    def cur(i):
        return jnp.minimum(i, nb - 1)

    qspec = pl.BlockSpec((RB, qw), lambda p, i: (cur(i), p))
    dospec = pl.BlockSpec((RB, qw), lambda p, i: (cur(i), do_blk + p))
    kvout = pl.BlockSpec((RB, qw), lambda p, i: (jnp.maximum(i - 1, 0), p))
    in_specs = ([smem] * (2 if has_sink else 1) + _band_specs(dil, nsub, ppk, q_blk, k_blk, v_blk, kv_shared, nb)
                + [gspec, gspec, dospec, qspec, qspec] + ([qspec, qspec] if mixed else []))
    args = ([slopes] + ([sinks] if has_sink else []) + [qkv] * 5 + [q_gain2, k_gain2, do, o, lse]
            + ([w, omix] if mixed else []))
    full = jax.ShapeDtypeStruct((T, n_heads * HEAD_DIM), F32)
    row = jax.ShapeDtypeStruct((1, LANES), F32)
    return pl.pallas_call(
        body, grid=(npair // ppk, nb + 1), in_specs=in_specs,
        out_specs=[qspec, kvout, kvout, gspec, gspec, gspec],
        out_shape=[full, full, full, row, row, row],
        scratch_shapes=[pltpu.VMEM((RB, qw), F32), pltpu.VMEM((RB, qw), F32)],
        compiler_params=_cparams(2), name=name)(*args)


def mix_fwd(o1, o2, o3, l1, l2, l3, *, tm, name):
    T, C = o1.shape

    def body(o1r, o2r, o3r, l1r, l2r, l3r, o_ref, w1r, w2r, w3r):
        a, b, c = l1r[...], l2r[...], l3r[...]
        m = jnp.maximum(jnp.maximum(a, b), c)
        ea, eb, ec = jnp.exp(a - m), jnp.exp(b - m), jnp.exp(c - m)
        inv = 1.0 / (ea + eb + ec)
        wa, wb, wc = ea * inv, eb * inv, ec * inv
        o_ref[...] = wa * o1r[...] + wb * o2r[...] + wc * o3r[...]
        w1r[...] = wa
        w2r[...] = wb
        w3r[...] = wc

    spec = pl.BlockSpec((tm, C), lambda i: (i, 0))
    shp = jax.ShapeDtypeStruct((T, C), F32)
    return pl.pallas_call(body, grid=(T // tm,), in_specs=[spec] * 6, out_specs=[spec] * 4, out_shape=[shp] * 4,
                          compiler_params=_cparams(1), name=name)(o1, o2, o3, l1, l2, l3)


def assemble_odd(parts, *, tm, name):
    T, C = parts[0][0].shape

    def body(*refs):
        o_ref = refs[9]
        for j in range(3):
            o_ref[:, pl.ds(C * j, C)] = refs[j][...] + refs[3 + j][...] + refs[6 + j][...]

    spec = pl.BlockSpec((tm, C), lambda i: (i, 0))
    flat = [parts[p][j] for p in range(3) for j in range(3)]
    return pl.pallas_call(body, grid=(T // tm,), in_specs=[spec] * 9,
                          out_specs=pl.BlockSpec((tm, 3 * C), lambda i: (i, 0)),
                          out_shape=jax.ShapeDtypeStruct((T, 3 * C), F32),
                          compiler_params=_cparams(1), name=name)(*flat)


def assemble_even(dqa, dka4, dva4, dqb, dkb, dvb, *, tm, name):
    T = dqa.shape[0]
    W = 512

    def body(dqa_r, dka_r, dva_r, dqb_r, dkb_r, dvb_r, o_ref):
        o_ref[:, pl.ds(0, W)] = dqa_r[...]
        ka = dka_r[...]
        va = dva_r[...]
        o_ref[:, pl.ds(512, LANES)] = ka[:, 0:128] + ka[:, 128:256] + ka[:, 256:384] + ka[:, 384:512]
        o_ref[:, pl.ds(640, LANES)] = va[:, 0:128] + va[:, 128:256] + va[:, 256:384] + va[:, 384:512]
        o_ref[:, pl.ds(768, W)] = dqb_r[...]
        o_ref[:, pl.ds(1280, W)] = dkb_r[...]
        o_ref[:, pl.ds(1792, W)] = dvb_r[...]

    spec = pl.BlockSpec((tm, W), lambda i: (i, 0))
    return pl.pallas_call(body, grid=(T // tm,), in_specs=[spec] * 6,
                          out_specs=pl.BlockSpec((tm, 2304), lambda i: (i, 0)),
                          out_shape=jax.ShapeDtypeStruct((T, 2304), F32),
                          compiler_params=_cparams(1), name=name)(dqa, dka4, dva4, dqb, dkb, dvb)


STICK_T = 256
STICK_DEAD = -110.0


def _split_bf16(x):
    hi = x.astype(BF16)
    lo = (x - hi.astype(F32)).astype(BF16)
    return hi, lo


def _stick_logits(qm, kt, scale, diag):
    n = STICK_T
    row = lax.broadcasted_iota(jnp.int32, (n, n), 0)
    col = lax.broadcasted_iota(jnp.int32, (n, n), 1)
    mask = col < row + jnp.where(diag, 0, n)
    z = _dot_nt(qm, kt) * scale
    lneg = -(jnp.maximum(z, 0.0) + jnp.log(1.0 + jnp.exp(-jnp.abs(z))))
    lpos = z + lneg
    lk = jnp.where(mask, lneg, 0.0)
    return mask, lpos, lneg, lk


def _cumsum_mm(x, tri):
    hi, lo = _split_bf16(x)
    return _dot(hi, tri) + _dot(lo, tri)


def stick_fwd(qkv, *, q_blk, k_blk, v_blk, n_pairs, name):
    T = qkv.shape[0]
    n = STICK_T
    nq = T // n
    scale = HEAD_DIM ** -0.5

    def body(q_ref, k_ref, v_ref, o_ref):
        i = pl.program_id(1)
        m0 = _lane0()
        r2 = lax.broadcasted_iota(jnp.int32, (n, n), 0)
        c2 = lax.broadcasted_iota(jnp.int32, (n, n), 1)
        tri_after = (r2 > c2).astype(BF16)
        qv = q_ref[...]
        out = jnp.zeros((n, LANES), F32)
        for e in range(2):
            qm = _mask_half(qv, m0, e).astype(BF16)

            def alive(st):
                t, _, carry = st
                return (t <= i) & (jnp.max(carry) > STICK_DEAD)

            def step(st, e=e, qm=qm):
                t, acc, carry = st
                start = pl.multiple_of((i - t) * n, n)
                kt = k_ref[pl.ds(start, n), :].astype(BF16)
                vt = _mask_half(v_ref[pl.ds(start, n), :], m0, e).astype(BF16)
                mask, lpos, _, lk = _stick_logits(qm, kt, scale, t == 0)
                after = _cumsum_mm(lk, tri_after) + carry
                a = jnp.where(mask, jnp.exp(lpos + after), 0.0)
                acc = acc + _dot(a.astype(BF16), vt)
                carry = carry + jnp.sum(lk, axis=-1, keepdims=True)
                return t + 1, acc, carry

            _, acc, _ = lax.while_loop(alive, step, (jnp.int32(0), jnp.zeros((n, LANES), F32),
                                                     jnp.zeros((n, 1), F32)))
            out = out + acc
        o_ref[...] = out

    return pl.pallas_call(
        body, grid=(n_pairs, nq),
        in_specs=[pl.BlockSpec((n, LANES), lambda p, i: (i, q_blk + p)),
                  pl.BlockSpec((T, LANES), lambda p, i: (0, k_blk + p)),
                  pl.BlockSpec((T, LANES), lambda p, i: (0, v_blk + p))],
        out_specs=pl.BlockSpec((n, LANES), lambda p, i: (i, p)),
        out_shape=jax.ShapeDtypeStruct((T, n_pairs * LANES), F32),
        compiler_params=_cparams(2), name=name)(qkv, qkv, qkv)


def stick_bwd(qkv, do, *, q_blk, k_blk, v_blk, do_blk, n_pairs, name, riders=None, rider_args=()):
    T = qkv.shape[0]
    n = STICK_T
    nq = T // n
    scale = HEAD_DIM ** -0.5
    nc = riders.n if riders is not None else 0

    def body(*refs):
        q_ref, k_ref, v_ref, do_ref = refs[:4]
        pre_refs = refs[4:4 + nc]
        dq_ref, dk_ref, dv_ref = refs[4 + nc:7 + nc]
        land_refs = refs[7 + nc:7 + 2 * nc]
        sems = refs[7 + 2 * nc:]
        i = pl.program_id(1)
        first_step = (pl.program_id(0) == 0) & (i == 0)
        last_step = (pl.program_id(0) == n_pairs - 1) & (i == nq - 1)
        m0 = _lane0()
        r2 = lax.broadcasted_iota(jnp.int32, (n, n), 0)
        c2 = lax.broadcasted_iota(jnp.int32, (n, n), 1)
        tri_after = (r2 > c2).astype(BF16)
        tri_from = (r2 >= c2).astype(BF16)

        if riders is not None:
            @pl.when(first_step)
            def _():
                riders.start(pre_refs, land_refs, sems)

        @pl.when(i == 0)
        def _():
            dk_ref[...] = jnp.zeros_like(dk_ref)
            dv_ref[...] = jnp.zeros_like(dv_ref)

        qv = q_ref[...]
        dov = do_ref[...]
        dq_out = jnp.zeros((n, LANES), F32)
        for e in range(2):
            qm = _mask_half(qv, m0, e).astype(BF16)
            dom = _mask_half(dov, m0, e).astype(BF16)

            def tile(t, carry, qm=qm, dom=dom):
                start = pl.multiple_of((i - t) * n, n)
                kt_f = k_ref[pl.ds(start, n), :]
                vt = v_ref[pl.ds(start, n), :].astype(BF16)
                mask, lpos, lneg, lk = _stick_logits(qm, kt_f.astype(BF16), scale, t == 0)
                a = jnp.where(mask, jnp.exp(lpos + _cumsum_mm(lk, tri_after) + carry), 0.0)
                g = _dot_nt(dom, vt) * a
                return start, kt_f, mask, lpos, lneg, lk, a, g

            def alive(st):
                t, carry, _ = st
                return (t <= i) & (jnp.max(carry) > STICK_DEAD)

            def scan(st):
                t, carry, gtot = st
                _, _, _, _, _, lk, _, g = tile(t, carry)
                return (t + 1, carry + jnp.sum(lk, axis=-1, keepdims=True),
                        gtot + jnp.sum(g, axis=-1, keepdims=True))

            z1 = jnp.zeros((n, 1), F32)
            n_live, _, gtot = lax.while_loop(alive, scan, (jnp.int32(0), z1, z1))

            def step(t, st, e=e, qm=qm, dom=dom, gtot=gtot):
                dq_acc, carry, gright = st
                start, kt_f, mask, lpos, lneg, lk, a, g = tile(t, carry)
                before = gtot - (_cumsum_mm(g, tri_from) + gright)
                dz = jnp.where(mask, g * jnp.exp(lneg) - before * jnp.exp(lpos), 0.0) * scale
                dzb = dz.astype(BF16)
                dq_acc = dq_acc + _dot(dzb, _mask_half(kt_f, m0, e).astype(BF16))
                dk_ref[pl.ds(start, n), :] += _dot_tn(dzb, qm)
                dv_ref[pl.ds(start, n), :] += _dot_tn(a.astype(BF16), dom)
                return (dq_acc, carry + jnp.sum(lk, axis=-1, keepdims=True),
                        gright + jnp.sum(g, axis=-1, keepdims=True))

            dq_acc, _, _ = lax.fori_loop(0, n_live, step, (jnp.zeros((n, LANES), F32), z1, z1))
            dq_out = dq_out + dq_acc
        dq_ref[...] = dq_out

        if riders is not None:
            @pl.when(last_step)
            def _():
                riders.finish(pre_refs, land_refs, sems)

    tile = pl.BlockSpec((n, LANES), lambda p, i: (i, p))
    whole = pl.BlockSpec((T, LANES), lambda p, i: (0, p))
    shp = jax.ShapeDtypeStruct((T, n_pairs * LANES), F32)
    outs = pl.pallas_call(
        body, grid=(n_pairs, nq),
        in_specs=[pl.BlockSpec((n, LANES), lambda p, i: (i, q_blk + p)),
                  pl.BlockSpec((T, LANES), lambda p, i: (0, k_blk + p)),
                  pl.BlockSpec((T, LANES), lambda p, i: (0, v_blk + p)),
                  pl.BlockSpec((n, LANES), lambda p, i: (i, do_blk + p))] + [_ANY] * nc,
        out_specs=[tile, whole, whole] + [_ANY] * nc,
        out_shape=[shp, shp, shp] + (riders.shapes if nc else []),
        scratch_shapes=riders.sems if nc else [],
        compiler_params=_cparams(2), name=name)(qkv, qkv, qkv, do, *rider_args)
    return outs[0], outs[1], outs[2], list(outs[3:])


def _xnorm(x):
    r = lax.rsqrt(jnp.mean(x * x, axis=-1, keepdims=True) + RMS_EPS)
    return r, x * r


def xattn_fwd(qraw, kvraw, q_gain, k_gain, *, tm, name):
    T = qraw.shape[0]
    scale = X_HEAD_DIM ** -0.5
    W = X_HEADS * X_HEAD_DIM

    def body(q_ref, kv_ref, qg_ref, kg_ref, o_ref):
        for h in range(X_HEADS):
            cs = pl.ds(X_HEAD_DIM * h, X_HEAD_DIM)
            _, qh = _xnorm(q_ref[:, cs])
            _, kh = _xnorm(kv_ref[:, cs])
            qn = (qh * qg_ref[...]).astype(BF16)
            kn = (kh * kg_ref[...]).astype(BF16)
            v = kv_ref[:, pl.ds(W + X_HEAD_DIM * h, X_HEAD_DIM)].astype(BF16)
            s = _dot_nt(qn, kn) * scale
            m = jnp.max(s, axis=-1, keepdims=True)
            p = jnp.exp(s - m)
            p = p / jnp.sum(p, axis=-1, keepdims=True)
            o_ref[:, cs] = _dot(p.astype(BF16), v)

    gspec = pl.BlockSpec((1, X_HEAD_DIM), lambda i: (0, 0))
    return pl.pallas_call(
        body, grid=(T // tm,),
        in_specs=[pl.BlockSpec((tm, W), lambda i: (i, 0)), pl.BlockSpec((MEM_LEN, 2 * W), lambda i: (0, 0)),
                  gspec, gspec],
        out_specs=pl.BlockSpec((tm, W), lambda i: (i, 0)),
        out_shape=jax.ShapeDtypeStruct((T, W), F32),
        compiler_params=_cparams(1), name=name)(qraw, kvraw, q_gain, k_gain)


def xattn_bwd(qraw, kvraw, q_gain, k_gain, do, o, *, tm, name):
    T = qraw.shape[0]
    nt = T // tm
    scale = X_HEAD_DIM ** -0.5
    W = X_HEADS * X_HEAD_DIM

    def body(q_ref, kv_ref, qg_ref, kg_ref, do_ref, o_ref, dq_ref, dkv_ref, dqg_ref, dkg_ref, dkn_ref):
        i = pl.program_id(0)

        @pl.when(i == 0)
        def _():
            dkv_ref[...] = jnp.zeros_like(dkv_ref)
            dkn_ref[...] = jnp.zeros_like(dkn_ref)
            dqg_ref[...] = jnp.zeros_like(dqg_ref)
            dkg_ref[...] = jnp.zeros_like(dkg_ref)

        qg = qg_ref[...]
        kg = kg_ref[...]
        dqg_acc = jnp.zeros((1, X_HEAD_DIM), F32)
        for h in range(X_HEADS):
            cs = pl.ds(X_HEAD_DIM * h, X_HEAD_DIM)
            vs = pl.ds(W + X_HEAD_DIM * h, X_HEAD_DIM)
            rq, qh = _xnorm(q_ref[:, cs])
            _, kh = _xnorm(kv_ref[:, cs])
            qn = (qh * qg).astype(BF16)
            kn = (kh * kg).astype(BF16)
            v = kv_ref[:, vs].astype(BF16)
            s = _dot_nt(qn, kn) * scale
            m = jnp.max(s, axis=-1, keepdims=True)
            p = jnp.exp(s - m)
            p = p / jnp.sum(p, axis=-1, keepdims=True)
            dov = do_ref[:, cs]
            delta = jnp.sum(dov * o_ref[:, cs], axis=-1, keepdims=True)
            dob = dov.astype(BF16)
            ds = (p * (_dot_nt(dob, v) - delta)).astype(BF16)
            dqn = _dot(ds, kn) * scale
            dkn_ref[:, cs] += _dot_tn(ds, qn) * scale
            dkv_ref[:, vs] += _dot_tn(p.astype(BF16), dob)
            dqg_acc = dqg_acc + jnp.sum(dqn * qh, axis=0, keepdims=True)
            dqh = dqn * qg
            dq_ref[:, cs] = rq * (dqh - qh * jnp.mean(dqh * qh, axis=-1, keepdims=True))
        dqg_ref[...] += dqg_acc

        @pl.when(i == nt - 1)
        def _():
            dkg_acc = jnp.zeros((1, X_HEAD_DIM), F32)
            for h in range(X_HEADS):
                cs = pl.ds(X_HEAD_DIM * h, X_HEAD_DIM)
                rk, kh = _xnorm(kv_ref[:, cs])
                dkn = dkn_ref[:, cs]
                dkg_acc = dkg_acc + jnp.sum(dkn * kh, axis=0, keepdims=True)
                dkh = dkn * kg
                dkv_ref[:, cs] = rk * (dkh - kh * jnp.mean(dkh * kh, axis=-1, keepdims=True))
            dkg_ref[...] = dkg_acc

    gspec = pl.BlockSpec((1, X_HEAD_DIM), lambda i: (0, 0))
    tile = pl.BlockSpec((tm, W), lambda i: (i, 0))
    kvspec = pl.BlockSpec((MEM_LEN, 2 * W), lambda i: (0, 0))
    grow = jax.ShapeDtypeStruct((1, X_HEAD_DIM), F32)
    return pl.pallas_call(
        body, grid=(nt,), in_specs=[tile, kvspec, gspec, gspec, tile, tile],
        out_specs=[tile, kvspec, gspec, gspec],
        out_shape=[jax.ShapeDtypeStruct((T, W), F32), jax.ShapeDtypeStruct((MEM_LEN, 2 * W), F32), grow, grow],
        scratch_shapes=[pltpu.VMEM((MEM_LEN, W), F32)],
        compiler_params=_cparams(1), name=name)(qraw, kvraw, q_gain, k_gain, do, o)


_ANY = pl.BlockSpec(memory_space=pl.ANY)


def _my_pos():
    return lax.axis_index("x"), lax.axis_index("y"), lax.axis_index("c")


def _pieces(arrays, chunks):
    out = []
    for a, (arr, n) in enumerate(zip(arrays, chunks)):
        rc = arr.shape[-2] // n
        out += [(a, pl.ds(ch * rc, rc)) for ch in range(n)]
    return out


def gather_blocks(blks, chunks, *, name):
    n = len(blks)
    pieces = _pieces(blks, chunks)
    n_p = len(pieces)

    def body(*refs):
        x_refs, out_refs = refs[:n], refs[n:2 * n]
        send_sems, recv_sems, local_sems = refs[2 * n:]
        x, y, c = _my_pos()
        me, sibling = (x, y, c), (x, y, 1 - c)
        chips = [(1 - x, y), (x, 1 - y), (1 - x, 1 - y)]

        def slot(block, p):
            px, py, pc = block
            a, rows = pieces[p]
            return out_refs[a].at[4 * px + 2 * py + pc, rows]

        def own(p):
            a, rows = pieces[p]
            return x_refs[a].at[rows]

        def copy(k, p, block, to, from_input=False):
            return pltpu.make_async_remote_copy(
                src_ref=own(p) if from_input else slot(block, p), dst_ref=slot(block, p),
                send_sem=send_sems.at[k * n_p + p], recv_sem=recv_sems.at[k * n_p + p],
                device_id=to, device_id_type=MESH)

        ps = range(n_p)
        xn, yn, dg = [(*chip, c) for chip in chips]
        mine = [pltpu.make_async_copy(own(p), slot(me, p), local_sems.at[p]) for p in ps]
        first = [copy(k, p, me, to, from_input=True) for p in ps for k, to in ((1, xn), (2, yn), (0, sibling))]
        for cp in first + mine:
            cp.start()
        passed = []

        def pass_on(k, p, block, to):
            cp = copy(k, p, block, to)
            cp.start()
            passed.append(cp)

        for p in ps:
            copy(1, p, xn, me).wait_recv()
            if p % 2 == 0:
                pass_on(3, p, xn, yn)
            pass_on(4, p, xn, sibling)
            copy(2, p, yn, me).wait_recv()
            if p % 2 == 1:
                pass_on(3, p, yn, xn)
            pass_on(5, p, yn, sibling)
        for p in ps:
            copy(3, p, dg, me).wait_recv()
            pass_on(6, p, dg, sibling)
        for p in ps:
            copy(0, p, sibling, me).wait_recv()
        for k, chip in zip((4, 5, 6), chips):
            for p in ps:
                copy(k, p, (*chip, 1 - c), me).wait_recv()
        for cp in first + passed:
            cp.wait_send()
        for cp in mine:
            cp.wait()

    return pl.pallas_call(
        body, out_shape=[jax.ShapeDtypeStruct((N_DEV,) + b.shape, b.dtype) for b in blks],
        in_specs=[_ANY] * n, out_specs=[_ANY] * n,
        scratch_shapes=[pltpu.SemaphoreType.DMA((7 * n_p,)), pltpu.SemaphoreType.DMA((7 * n_p,)),
                        pltpu.SemaphoreType.DMA((n_p,))],
        name=name)(*blks)


def gather_small(small, *, name):
    S, C = small.shape

    def body(s_ref, out_ref, send_sems, recv_sems, local_sem):
        x, y, c = _my_pos()
        my_id = 4 * x + 2 * y + c

        def copy(k, slot):
            px, py, pc = x ^ ((k >> 2) & 1), y ^ ((k >> 1) & 1), c ^ (k & 1)
            dst = my_id if slot == "mine" else 4 * px + 2 * py + pc
            return pltpu.make_async_remote_copy(
                src_ref=s_ref, dst_ref=out_ref.at[dst], send_sem=send_sems.at[k - 1], recv_sem=recv_sems.at[k - 1],
                device_id=(px, py, pc), device_id_type=MESH)

        own = pltpu.make_async_copy(s_ref, out_ref.at[my_id], local_sem)
        own.start()
        sends = [copy(k, "mine") for k in range(1, N_DEV)]
        for cp in sends:
            cp.start()
        for k in range(1, N_DEV):
            copy(k, "theirs").wait_recv()
        for cp in sends:
            cp.wait_send()
        own.wait()

    dma7 = pltpu.SemaphoreType.DMA((7,))
    return pl.pallas_call(
        body, out_shape=jax.ShapeDtypeStruct((N_DEV, S, C), small.dtype), in_specs=[_ANY], out_specs=_ANY,
        scratch_shapes=[dma7, dma7, pltpu.SemaphoreType.DMA], name=name)(small)


def pair_exchange(bigs, chunks, *, name):
    n = len(bigs)
    pieces = _pieces(bigs, chunks)
    n_p = len(pieces)

    def body(*refs):
        big_refs, out_refs = refs[:n], refs[n:2 * n]
        send_sems, recv_sems = refs[2 * n:]
        x, y, c = _my_pos()

        def copy(b, p):
            a, rows = pieces[p]
            return pltpu.make_async_remote_copy(
                src_ref=big_refs[a].at[2 * b + (1 - c), rows], dst_ref=out_refs[a].at[b, rows],
                send_sem=send_sems.at[b * n_p + p], recv_sem=recv_sems.at[b * n_p + p],
                device_id=(x, y, 1 - c), device_id_type=MESH)

        cps = [copy(b, p) for b in range(4) for p in range(n_p)]
        for cp in cps:
            cp.start()
        for cp in cps:
            cp.wait_recv()
        for cp in cps:
            cp.wait_send()

    return pl.pallas_call(
        body, out_shape=[jax.ShapeDtypeStruct((4,) + b.shape[1:], b.dtype) for b in bigs],
        in_specs=[_ANY] * n, out_specs=[_ANY] * n,
        scratch_shapes=[pltpu.SemaphoreType.DMA((4 * n_p,)), pltpu.SemaphoreType.DMA((4 * n_p,))],
        name=name)(*bigs)


def pair_sum(big, sib, c, *, tr, name):
    _, R, C = big.shape

    def body(c_ref, a_ref, s_ref, o_ref):
        o_ref[...] = (a_ref[...].astype(F32) + s_ref[...].astype(F32)).astype(o_ref.dtype)

    grid_spec = pltpu.PrefetchScalarGridSpec(
        num_scalar_prefetch=1, grid=(4, R // tr),
        in_specs=[pl.BlockSpec((None, tr, C), lambda b, i, c_ref: (2 * b + c_ref[0], i, 0)),
                  pl.BlockSpec((None, tr, C), lambda b, i, c_ref: (b, i, 0))],
        out_specs=pl.BlockSpec((None, tr, C), lambda b, i, c_ref: (b, i, 0)))
    return pl.pallas_call(body, grid_spec=grid_spec, out_shape=jax.ShapeDtypeStruct((4, R, C), big.dtype),
                          compiler_params=_cparams(2), name=name)(c.reshape(1).astype(jnp.int32), big, sib)


class ChipScatter:
    def __init__(self, pres, chunks):
        self.shapes = [jax.ShapeDtypeStruct(p.shape, p.dtype) for p in pres]
        self.n = len(pres)
        self.pieces = _pieces(pres, chunks)
        n_p = len(self.pieces)
        self.sems = [pltpu.SemaphoreType.DMA((3 * n_p,)), pltpu.SemaphoreType.DMA((3 * n_p,)),
                     pltpu.SemaphoreType.DMA((n_p,))]

    def _copies(self, pre_refs, out_refs, sems):
        send_sems, recv_sems, local_sems = sems
        n_p = len(self.pieces)
        x, y, c = _my_pos()
        my_chip = 2 * x + y
        chips = [(1 - x, y), (x, 1 - y), (1 - x, 1 - y)]

        def copy(j, p, slot):
            px, py = chips[j]
            a, rows = self.pieces[p]
            src_slot, dst_slot = (2 * px + py, my_chip) if slot == "mine" else (my_chip, 2 * px + py)
            return pltpu.make_async_remote_copy(
                src_ref=pre_refs[a].at[src_slot, rows], dst_ref=out_refs[a].at[dst_slot, rows],
                send_sem=send_sems.at[j * n_p + p], recv_sem=recv_sems.at[j * n_p + p],
                device_id=(px, py, c), device_id_type=MESH)

        own = [pltpu.make_async_copy(pre_refs[a].at[my_chip, rows], out_refs[a].at[my_chip, rows], local_sems.at[p])
               for p, (a, rows) in enumerate(self.pieces)]
        sends = [copy(j, p, "mine") for j in range(3) for p in range(n_p)]
        recvs = [copy(j, p, "theirs") for j in range(3) for p in range(n_p)]
        return own, sends, recvs

    def start(self, pre_refs, out_refs, sems):
        own, sends, _ = self._copies(pre_refs, out_refs, sems)
        for cp in sends + own:
            cp.start()

    def finish(self, pre_refs, out_refs, sems):
        own, sends, recvs = self._copies(pre_refs, out_refs, sems)
        for cp in recvs:
            cp.wait_recv()
        for cp in sends:
            cp.wait_send()
        for cp in own:
            cp.wait()


def chip_scatter(pres, chunks, *, name):
    cs = ChipScatter(pres, chunks)
    n = cs.n

    def body(*refs):
        pre_refs, out_refs, sems = refs[:n], refs[n:2 * n], refs[2 * n:]
        cs.start(pre_refs, out_refs, sems)
        cs.finish(pre_refs, out_refs, sems)

    return pl.pallas_call(body, out_shape=cs.shapes, in_specs=[_ANY] * n, out_specs=[_ANY] * n,
                          scratch_shapes=cs.sems, name=name)(*pres)


def sibling_send(blks, chunks, *, name):
    n = len(blks)
    pieces = _pieces(blks, chunks)
    n_p = len(pieces)

    def body(*refs):
        x_refs, out_refs = refs[:n], refs[n:2 * n]
        send_sems, recv_sems = refs[2 * n:]
        x, y, c = _my_pos()
        cps = [pltpu.make_async_remote_copy(
            src_ref=x_refs[a].at[rows], dst_ref=out_refs[a].at[rows], send_sem=send_sems.at[p],
            recv_sem=recv_sems.at[p], device_id=(x, y, 1 - c), device_id_type=MESH)
            for p, (a, rows) in enumerate(pieces)]
        for cp in cps:
            cp.start()
        for cp in cps:
            cp.wait_recv()
        for cp in cps:
            cp.wait_send()

    return pl.pallas_call(
        body, out_shape=[jax.ShapeDtypeStruct(b.shape, b.dtype) for b in blks],
        in_specs=[_ANY] * n, out_specs=[_ANY] * n,
        scratch_shapes=[pltpu.SemaphoreType.DMA((n_p,)), pltpu.SemaphoreType.DMA((n_p,))],
        name=name)(*blks)


def reduce_slots(land, *, tr, name):
    n, R, C = land.shape

    def body(l_ref, o_ref):
        acc = l_ref[0].astype(F32)
        for s in range(1, n):
            acc = acc + l_ref[s].astype(F32)
        o_ref[...] = acc

    return pl.pallas_call(
        body, grid=(R // tr,), in_specs=[pl.BlockSpec((n, tr, C), lambda i: (0, i, 0))],
        out_specs=pl.BlockSpec((tr, C), lambda i: (i, 0)), out_shape=jax.ShapeDtypeStruct((R, C), F32),
        compiler_params=_cparams(1), name=name)(land)


TM = 512


def _tk(d):
    return min(d.shape[0], 1024)


def ffn_fwd(x, g, wgu, wd, tag):
    gu, h = norm_matmul(x, g, wgu, tm=_tk(x), tn=1408, split=True, name=f"{tag}_gu")
    xo = mm_nn(gu, wd, res=x, scale=0.5, swiglu=True, tm=TM, tn=D_MODEL, tk=1408, name=f"{tag}_down")
    return xo, (x, gu, h)


def ffn_bwd(d, saved, g, wgu, wd, tag):
    x, gu, h = saved
    dgu, act = ffn_bwd_act(d, wd, gu, tm=TM, tn=1408, name=f"{tag}_bact")
    dwd = mm_tn(act, d, scale=0.5, a_split=False, b_split=False, tm=1408, tn=D_MODEL, tk=_tk(d), name=f"{tag}_dwd")
    dwgu = mm_tn(h, dgu, scale=1.0, a_split=False, b_split=True, tm=TM, tn=1408, tk=_tk(d), out_blocked=True,
                 name=f"{tag}_dwgu")
    dx, dg = mm_nt_normbwd(dgu, wgu, x, g, d, a_split=True, tm=_tk(d), tk=1408, name=f"{tag}_dx")
    return dx, dg, dwgu, dwd


def _tile2(v):
    return jnp.concatenate([v, v], axis=-1).reshape(1, LANES)


def _fold2(v):
    return v[:, :HEAD_DIM] + v[:, HEAD_DIM:]


EVEN = dict(dil=1, nsub=4, ppk=4, q_blk=0, k_blk=4, v_blk=5, n_heads=A_Q_HEADS, group=A_GROUP, max_dist=A_WINDOW - 1)
STICK = dict(q_blk=6, k_blk=10, v_blk=14, n_pairs=4)


def _odd_cfg(dil):
    return dict(dil=dil, nsub=4 if dil == 1 else 1, ppk=1, q_blk=0, k_blk=8, v_blk=16, n_heads=C_HEADS, group=1,
                max_dist=BLK)


def even_fwd(x, g, win, qg, kg, sinks, wout, tag):
    qkv, h = norm_matmul(x, g, win, tm=_tk(x), tn=1152, split=False, name=f"{tag}_in")
    qg2, kg2 = _tile2(qg), _tile2(kg)
    slopes = jnp.asarray(_alibi(A_Q_HEADS), F32)
    oa, lse = banded_fwd(qkv, qg2, kg2, slopes, sinks, name=f"{tag}_swa", **EVEN)
    ob = stick_fwd(qkv, name=f"{tag}_stick", **STICK)
    o = jnp.concatenate([oa, ob], axis=1)
    xo = mm_nn(o, wout, res=x, scale=1.0, swiglu=False, tm=TM, tn=D_MODEL, tk=D_MODEL, name=f"{tag}_out")
    return xo, (x, qkv, h, oa, lse, o)


def even_bwd(d, saved, g, win, qg, kg, sinks, wout, tag, riders=None, rider_args=()):
    x, qkv, h, oa, lse, o = saved
    qg2, kg2 = _tile2(qg), _tile2(kg)
    slopes = jnp.asarray(_alibi(A_Q_HEADS), F32)
    dwout = mm_tn(o, d, scale=1.0, a_split=False, b_split=False, tm=D_MODEL, tn=D_MODEL, tk=_tk(d), name=f"{tag}_dwout")
    do = mm_nt(d, wout, tm=TM, tn=D_MODEL, tk=D_MODEL, name=f"{tag}_do")
    dqa, dka4, dva4, dqg, dkg, dsk = banded_bwd(qkv, qg2, kg2, slopes, sinks, do, oa, lse, None, None,
                                                do_blk=0, name=f"{tag}_swa_b", **EVEN)
    dqb, dkb, dvb, rode = stick_bwd(qkv, do, do_blk=4, name=f"{tag}_stick_b", riders=riders, rider_args=rider_args,
                                    **STICK)
    dqkv = assemble_even(dqa, dka4, dva4, dqb, dkb, dvb, tm=TM, name=f"{tag}_asm")
    dwin = mm_tn(h, dqkv, scale=1.0, a_split=False, b_split=False, tm=D_MODEL, tn=1152, tk=_tk(d), name=f"{tag}_dwin")
    dx, dg = mm_nt_normbwd(dqkv, win, x, g, d, a_split=False, tm=TM, tk=1152, name=f"{tag}_dx")
    return dx, dg, dwin, _fold2(dqg), _fold2(dkg), dsk[:, :A_Q_HEADS], dwout, rode


def odd_fwd(x, g, win, qg, kg, wout, tag):
    qkv, h = norm_matmul(x, g, win, tm=_tk(x), tn=768, split=False, name=f"{tag}_in")
    qg2, kg2 = _tile2(qg), _tile2(kg)
    outs = []
    for p, (window, dil) in enumerate(C_PATTERNS):
        slopes = jnp.asarray(_alibi(C_HEADS), F32) * float(dil)
        outs.append(banded_fwd(qkv, qg2, kg2, slopes, None, name=f"{tag}_dil{p}", **_odd_cfg(dil)))
    o, w1, w2, w3 = mix_fwd(outs[0][0], outs[1][0], outs[2][0], outs[0][1], outs[1][1], outs[2][1],
                            tm=TM, name=f"{tag}_mix")
    xo = mm_nn(o, wout, res=x, scale=1.0, swiglu=False, tm=TM, tn=D_MODEL, tk=D_MODEL, name=f"{tag}_out")
    return xo, (x, qkv, h, outs, (w1, w2, w3), o)


def odd_bwd(d, saved, g, win, qg, kg, wout, tag):
    x, qkv, h, outs, ws, o = saved
    qg2, kg2 = _tile2(qg), _tile2(kg)
    dwout = mm_tn(o, d, scale=1.0, a_split=False, b_split=False, tm=D_MODEL, tn=D_MODEL, tk=_tk(d), name=f"{tag}_dwout")
    do = mm_nt(d, wout, tm=TM, tn=D_MODEL, tk=D_MODEL, name=f"{tag}_do")
    parts, dqg, dkg = [], 0.0, 0.0
    for p, (window, dil) in enumerate(C_PATTERNS):
        slopes = jnp.asarray(_alibi(C_HEADS), F32) * float(dil)
        dq, dk, dv, dqg_p, dkg_p, _ = banded_bwd(qkv, qg2, kg2, slopes, None, do, outs[p][0], outs[p][1], ws[p], o,
                                                 do_blk=0, name=f"{tag}_dil{p}_b", **_odd_cfg(dil))
        parts.append((dq, dk, dv))
        dqg = dqg + dqg_p
        dkg = dkg + dkg_p
    dqkv = assemble_odd(parts, tm=TM, name=f"{tag}_asm")
    dwin = mm_tn(h, dqkv, scale=1.0, a_split=False, b_split=False, tm=TM, tn=768, tk=_tk(d), out_blocked=True,
                 name=f"{tag}_dwin")
    dx, dg = mm_nt_normbwd(dqkv, win, x, g, d, a_split=False, tm=TM, tk=768, name=f"{tag}_dx")
    return dx, dg, dwin, _fold2(dqg), _fold2(dkg), dwout


def xa_fwd(x, mem, g, gm, wq, wkv, qg, kg, wo, tag):
    qraw, h = norm_matmul(x, g, wq, tm=TM, tn=D_MODEL, split=False, name=f"{tag}_q")
    kvraw, hm = norm_matmul(mem, gm, wkv, tm=MEM_LEN, tn=512, split=False, name=f"{tag}_kv")
    o = xattn_fwd(qraw, kvraw, qg, kg, tm=TM, name=f"{tag}_att")
    xo = mm_nn(o, wo, res=x, scale=1.0, swiglu=False, tm=TM, tn=D_MODEL, tk=D_MODEL, name=f"{tag}_o")
    return xo, (x, qraw, h, kvraw, hm, o)


def xa_bwd(d, saved, mem, g, gm, wq, wkv, qg, kg, wo, tag):
    x, qraw, h, kvraw, hm, o = saved
    dwo = mm_tn(o, d, scale=1.0, a_split=False, b_split=False, tm=D_MODEL, tn=D_MODEL, tk=_tk(d), name=f"{tag}_dwo")
    do = mm_nt(d, wo, tm=TM, tn=D_MODEL, tk=D_MODEL, name=f"{tag}_do")
    dq, dkv, dqg, dkg = xattn_bwd(qraw, kvraw, qg, kg, do, o, tm=TM, name=f"{tag}_att_b")
    dwq = mm_tn(h, dq, scale=1.0, a_split=False, b_split=False, tm=D_MODEL, tn=D_MODEL, tk=_tk(d), name=f"{tag}_dwq")
    dx, dg = mm_nt_normbwd(dq, wq, x, g, d, a_split=False, tm=TM, tk=D_MODEL, name=f"{tag}_dx")
    dwkv = mm_tn(hm, dkv, scale=1.0, a_split=False, b_split=False, tm=TM, tn=512, tk=MEM_LEN, out_blocked=True,
                 name=f"{tag}_dwkv")
    _, dgm = mm_nt_normbwd(dkv, wkv, mem, gm, None, a_split=False, tm=MEM_LEN, tk=512, name=f"{tag}_dmem")
    return dx, dg, dgm, dwq, dwkv, dqg, dkg, dwo


MATS = (("ffn1_w_gu", 1), ("ffn1_w_down", 0), ("ev_w_in", 1), ("ev_w_out", 0), ("od_w_in", 1), ("od_w_out", 0),
        ("xa_w_q", 0), ("xa_w_kv", 1), ("xa_w_o", 0), ("ffn2_w_gu", 1), ("ffn2_w_down", 0))
SMALLS = ("ffn1_norm", "mix_norm", "ev_q_gain", "ev_k_gain", "ev_sinks", "od_q_gain", "od_k_gain", "xa_norm",
          "xa_mem_norm", "xa_q_gain", "xa_k_gain", "ffn2_norm")
WEIGHTS = ("ffn1_norm", "ffn1_w_gu", "ffn1_w_down", "mix_norm", "ev_w_in", "ev_q_gain", "ev_k_gain", "ev_sinks",
           "ev_w_out", "od_w_in", "od_q_gain", "od_k_gain", "od_w_out", "xa_norm", "xa_mem_norm", "xa_w_q",
           "xa_w_kv", "xa_q_gain", "xa_k_gain", "xa_w_o", "ffn2_norm", "ffn2_w_gu", "ffn2_w_down")
SMALL_ROWS = 16
LAYER_GROUPS = (
    (((("ffn1_w_gu", 0), ("ffn2_w_gu", 0)), 4, 512),
     ((("ffn1_w_down", 0), ("ffn2_w_down", 0)), 2, 352),
     ((("ev_w_out", 0), ("xa_w_q", 0), ("xa_w_o", 0)), 1, 384),
     ((("xa_w_kv", 0),), 1, 512),
     ((("ev_w_in", 0),), 1, 512)),
    (((("ffn1_w_gu", 1), ("ffn2_w_gu", 1)), 4, 512),
     ((("ffn1_w_down", 1), ("ffn2_w_down", 1)), 2, 352),
     ((("od_w_out", 0), ("xa_w_q", 1), ("xa_w_o", 1)), 1, 384),
     ((("xa_w_kv", 1),), 1, 512),
     ((("od_w_in", 0),), 1, 512)),
)
GROUPS = LAYER_GROUPS[0] + LAYER_GROUPS[1]


def _chunks_of(groups):
    return tuple(g[1] for g in groups)
COL_SHARDED = {name for name, axis in MATS if axis == 1}
BLOCKED = {"ffn1_w_gu", "ffn2_w_gu", "xa_w_kv", "od_w_in"}


def group_halves(shards, c):
    out = []
    for members, _, _ in GROUPS:
        halves = []
        for name, layer in members:
            _, r, cc = shards[name].shape
            half = lax.dynamic_index_in_dim(shards[name][layer].reshape(2, r // 2, cc), c, 0, keepdims=False)
            halves.append(half.astype(BF16))
        out.append(jnp.concatenate(halves, axis=0))
    return out


def full_weights(gathered, shards):
    full = {}
    for (members, _, _), arr in zip(GROUPS, gathered):
        for w, (name, layer) in enumerate(members):
            _, r, cc = shards[name].shape
            piece = arr[:, w * (r // 2):(w + 1) * (r // 2)].reshape(4, r, cc)
            if name not in COL_SHARDED:
                piece = piece.reshape(4 * r, cc)
            elif name not in BLOCKED:
                piece = piece.transpose(1, 0, 2).reshape(r, 4 * cc)
            full[(name, layer)] = piece
    return full


def group_grads(grads, shards, groups):
    out = []
    for members, _, _ in groups:
        parts = []
        for name, layer in members:
            _, r, cc = shards[name].shape
            gfull = grads[(name, layer)]
            if name in COL_SHARDED and name not in BLOCKED:
                gfull = gfull.reshape(2, r // 2, 4, cc).transpose(2, 0, 1, 3)
            parts.append(gfull.reshape(N_DEV, r // 2, cc))
        out.append(jnp.concatenate(parts, axis=1))
    return out


def shard_grads(mine, theirs, c, shards, groups):
    per = {}
    for (members, _, _), a, b in zip(groups, mine, theirs):
        for w, (name, layer) in enumerate(members):
            _, r, cc = shards[name].shape
            rows = slice(w * (r // 2), (w + 1) * (r // 2))
            lo = jnp.where(c == 0, a[rows], b[rows])
            hi = jnp.where(c == 0, b[rows], a[rows])
            per[(name, layer)] = jnp.concatenate([lo, hi], axis=0)
    return per


def pack_small(vals):
    row10 = jnp.concatenate([vals["xa_q_gain"].reshape(1, 512), vals["xa_k_gain"].reshape(1, 512)], axis=1)
    row11 = jnp.concatenate([vals["ev_q_gain"], vals["ev_k_gain"], vals["od_q_gain"], vals["od_k_gain"],
                             vals["ev_sinks"], jnp.zeros((1, 1024 - 4 * 64 - 8), F32)], axis=1)
    return jnp.concatenate([vals["ffn1_norm"], vals["mix_norm"], vals["xa_norm"], vals["xa_mem_norm"],
                            vals["ffn2_norm"], row10, row11, jnp.zeros((SMALL_ROWS - 12, 1024), F32)], axis=0)


def unpack_small(arr):
    return {"ffn1_norm": arr[0:2], "mix_norm": arr[2:4], "xa_norm": arr[4:6], "xa_mem_norm": arr[6:8],
            "ffn2_norm": arr[8:10],
            "xa_q_gain": arr[10:11, 0:512].reshape(2, 256), "xa_k_gain": arr[10:11, 512:1024].reshape(2, 256),
            "ev_q_gain": arr[11:12, 0:64], "ev_k_gain": arr[11:12, 64:128], "od_q_gain": arr[11:12, 128:192],
            "od_k_gain": arr[11:12, 192:256], "ev_sinks": arr[11:12, 256:264]}


def local_step(x, mem, target, W, small, prereduce):
    depth = small["ffn1_norm"].shape[0]

    def row(name, l):
        return small[name][l:l + 1]

    saved = []
    for l in range(depth):
        j = l // 2
        x, s1 = ffn_fwd(x, row("ffn1_norm", l), W[("ffn1_w_gu", l)], W[("ffn1_w_down", l)], f"l{l}_f1")
        if l % 2 == 0:
            x, s2 = even_fwd(x, row("mix_norm", l), W[("ev_w_in", j)], row("ev_q_gain", j), row("ev_k_gain", j),
                             small["ev_sinks"][j], W[("ev_w_out", j)], f"l{l}_ev")
        else:
            x, s2 = odd_fwd(x, row("mix_norm", l), W[("od_w_in", j)], row("od_q_gain", j), row("od_k_gain", j),
                            W[("od_w_out", j)], f"l{l}_od")
        x, s3 = xa_fwd(x, mem, row("xa_norm", l), row("xa_mem_norm", l), W[("xa_w_q", l)], W[("xa_w_kv", l)],
                       row("xa_q_gain", l), row("xa_k_gain", l), W[("xa_w_o", l)], f"l{l}_xa")
        x, s4 = ffn_fwd(x, row("ffn2_norm", l), W[("ffn2_w_gu", l)], W[("ffn2_w_down", l)], f"l{l}_f2")
        saved.append((s1, s2, s3, s4))
    loss, d = loss_kernel(x, target, tm=TM, name="loss")

    gw = {}
    gs = {name: [None] * small[name].shape[0] for name in SMALLS}
    pending, landed = None, {}
    for l in reversed(range(depth)):
        j = l // 2
        s1, s2, s3, s4 = saved[l]
        d, dg, dwgu, dwd = ffn_bwd(d, s4, row("ffn2_norm", l), W[("ffn2_w_gu", l)], W[("ffn2_w_down", l)], f"l{l}_f2")
        gs["ffn2_norm"][l] = dg
        gw[("ffn2_w_gu", l)], gw[("ffn2_w_down", l)] = dwgu, dwd
        d, dg, dgm, dwq, dwkv, dqg, dkg, dwo = xa_bwd(
            d, s3, mem, row("xa_norm", l), row("xa_mem_norm", l), W[("xa_w_q", l)], W[("xa_w_kv", l)],
            row("xa_q_gain", l), row("xa_k_gain", l), W[("xa_w_o", l)], f"l{l}_xa")
        gs["xa_norm"][l], gs["xa_mem_norm"][l], gs["xa_q_gain"][l], gs["xa_k_gain"][l] = dg, dgm, dqg, dkg
        gw[("xa_w_q", l)], gw[("xa_w_kv", l)], gw[("xa_w_o", l)] = dwq, dwkv, dwo
        if l % 2 == 0:
            riders, rider_args = None, ()
            if pending is not None:
                riders, rider_args = ChipScatter(pending[1], _chunks_of(LAYER_GROUPS[pending[0]])), pending[1]
            d, dg, dwin, dqg, dkg, dsk, dwout, rode = even_bwd(
                d, s2, row("mix_norm", l), W[("ev_w_in", j)], row("ev_q_gain", j), row("ev_k_gain", j),
                small["ev_sinks"][j], W[("ev_w_out", j)], f"l{l}_ev", riders=riders, rider_args=rider_args)
            if pending is not None:
                landed[pending[0]], pending = rode, None
            gs["ev_q_gain"][j], gs["ev_k_gain"][j], gs["ev_sinks"][j] = dqg, dkg, dsk
            gw[("ev_w_in", j)], gw[("ev_w_out", j)] = dwin, dwout
        else:
            d, dg, dwin, dqg, dkg, dwout = odd_bwd(
                d, s2, row("mix_norm", l), W[("od_w_in", j)], row("od_q_gain", j), row("od_k_gain", j),
                W[("od_w_out", j)], f"l{l}_od")
            gs["od_q_gain"][j], gs["od_k_gain"][j] = dqg, dkg
            gw[("od_w_in", j)], gw[("od_w_out", j)] = dwin, dwout
        gs["mix_norm"][l] = dg
        d, dg, dwgu, dwd = ffn_bwd(d, s1, row("ffn1_norm", l), W[("ffn1_w_gu", l)], W[("ffn1_w_down", l)], f"l{l}_f1")
        gs["ffn1_norm"][l] = dg
        gw[("ffn1_w_gu", l)], gw[("ffn1_w_down", l)] = dwgu, dwd
        if pending is not None:
            landed[pending[0]] = chip_scatter(pending[1], _chunks_of(LAYER_GROUPS[pending[0]]),
                                              name=f"scatter_grads{pending[0]}")
        pending = (l, prereduce(gw, l))
    landed[pending[0]] = chip_scatter(pending[1], _chunks_of(LAYER_GROUPS[pending[0]]), name=f"scatter_grads{pending[0]}")
    gsmall = {name: jnp.concatenate(v, axis=0) for name, v in gs.items()}
    return loss, d, landed, gsmall


def kernel(x, mem, ffn1_norm, ffn1_w_gu, ffn1_w_down, mix_norm, ev_w_in, ev_q_gain, ev_k_gain, ev_sinks, ev_w_out, od_w_in, od_q_gain, od_k_gain, od_w_out, xa_norm, xa_mem_norm, xa_w_q, xa_w_kv, xa_q_gain, xa_k_gain, xa_w_o, ffn2_norm, ffn2_w_gu, ffn2_w_down, loss_target, m_ffn1_norm, m_ffn1_w_gu, m_ffn1_w_down, m_mix_norm, m_ev_w_in, m_ev_q_gain, m_ev_k_gain, m_ev_sinks, m_ev_w_out, m_od_w_in, m_od_q_gain, m_od_k_gain, m_od_w_out, m_xa_norm, m_xa_mem_norm, m_xa_w_q, m_xa_w_kv, m_xa_q_gain, m_xa_k_gain, m_xa_w_o, m_ffn2_norm, m_ffn2_w_gu, m_ffn2_w_down, v_ffn1_norm, v_ffn1_w_gu, v_ffn1_w_down, v_mix_norm, v_ev_w_in, v_ev_q_gain, v_ev_k_gain, v_ev_sinks, v_ev_w_out, v_od_w_in, v_od_q_gain, v_od_k_gain, v_od_w_out, v_xa_norm, v_xa_mem_norm, v_xa_w_q, v_xa_w_kv, v_xa_q_gain, v_xa_k_gain, v_xa_w_o, v_ffn2_norm, v_ffn2_w_gu, v_ffn2_w_down):
    given = dict(locals())
    w = {n: given[n] for n in WEIGHTS}
    m = {n: given["m_" + n] for n in WEIGHTS}
    v = {n: given["v_" + n] for n in WEIGHTS}
    c = lax.axis_index("c")
    shards = {name: w[name] for name, _ in MATS}
    small = {n: w[n] for n in SMALLS}

    gathered = gather_blocks(group_halves(shards, c), _chunks_of(GROUPS), name="gather_weights")
    full = full_weights(gathered, shards)

    def prereduce(gw, layer):
        groups = LAYER_GROUPS[layer]
        packed = group_grads(gw, shards, groups)
        sib = pair_exchange(packed, _chunks_of(groups), name=f"pair_grads{layer}")
        return [pair_sum(p, s, c, tr=g[2], name=f"pair_sum{layer}_{i}")
                for i, (g, p, s) in enumerate(zip(groups, packed, sib))]

    loss_b, grad_x, landed, gsmall = local_step(x[0], mem[0], loss_target[0], full, small, prereduce)

    per = {}
    for layer, land in sorted(landed.items()):
        groups = LAYER_GROUPS[layer]
        mine = [reduce_slots(a, tr=g[2], name=f"sum_grads{layer}_{i}") for i, (g, a) in enumerate(zip(groups, land))]
        theirs = sibling_send(mine, _chunks_of(groups), name=f"swap_grads{layer}")
        per.update(shard_grads(mine, theirs, c, shards, groups))
    g = {name: jnp.stack([per[(name, layer)] for layer in range(w[name].shape[0])], axis=0) for name, _ in MATS}
    land_small = gather_small(pack_small(gsmall), name="gather_small")
    g_small = unpack_small(reduce_slots(land_small, tr=SMALL_ROWS, name="sum_small"))
    g.update(g_small)

    delta, new_m, new_v = {}, {}, {}
    for name, _ in MATS:
        shp = w[name].shape
        flat = [a.reshape(-1, shp[-1]) for a in (w[name], g[name], m[name], v[name])]
        dl, nm, nv = adamw(*flat, br=BLK, name=f"adamw_{name}")
        delta[name], new_m[name], new_v[name] = dl.reshape(shp), nm.reshape(shp), nv.reshape(shp)
    dl, nm, nv = adamw(pack_small(small), pack_small(g_small), pack_small({n: m[n] for n in SMALLS}),
                       pack_small({n: v[n] for n in SMALLS}), br=SMALL_ROWS, name="adamw_small")
    for dst, arr in ((delta, dl), (new_m, nm), (new_v, nv)):
        dst.update(unpack_small(arr))

    loss = lax.psum(loss_b[0, 0], ("x", "y", "c"))
    return (loss, grad_x[None], *[g[n] for n in WEIGHTS], *[delta[n] for n in WEIGHTS],
            *[new_m[n] for n in WEIGHTS], *[new_v[n] for n in WEIGHTS])
```

```python
import jax
import jax.numpy as jnp
from jax import lax
from jax.experimental import pallas as pl
from jax.experimental.pallas import tpu as pltpu

F32 = jnp.float32
BF16 = jnp.bfloat16

D_MODEL = 1024
HEAD_DIM = 64
LANES = 128
BLK = 128
D_FF = 2816
RMS_EPS = 1e-6
MEM_LEN = 256
X_HEADS = 4
X_HEAD_DIM = 256
A_Q_HEADS = 8
A_GROUP = 4
A_WINDOW = 128
C_HEADS = 16
C_PATTERNS = ((128, 1), (512, 4), (2048, 16))
NEG = -1e30
VMEM_LIMIT = 56 * 2 ** 20

ADAM_LR = 0.001
ADAM_B1 = 0.9
ADAM_B2 = 0.999
ADAM_EPS = 1e-08
ADAM_WD = 0.01
ADAM_STEP = 10

N_DEV = 8
MESH = pl.DeviceIdType.MESH


def _cparams(n):
    return pltpu.CompilerParams(dimension_semantics=("arbitrary",) * n, vmem_limit_bytes=VMEM_LIMIT)


def _dot(a, b):
    return jnp.dot(a, b, preferred_element_type=F32)


def _dot_nt(a, b):
    return lax.dot_general(a, b, (((1,), (1,)), ((), ())), preferred_element_type=F32)


def _dot_tn(a, b):
    return lax.dot_general(a, b, (((0,), (0,)), ((), ())), preferred_element_type=F32)


def _sigmoid(z):
    return 1.0 / (1.0 + jnp.exp(-z))


def norm_matmul(x, g, w, *, tm, tn, split, name):
    T, K = x.shape
    blocked = w.ndim == 3
    assert not blocked or w.shape[2] == tn
    N = w.shape[0] * w.shape[2] if blocked else w.shape[1]
    nj = N // tn

    def body(x_ref, g_ref, w_ref, o_ref, h_ref):
        @pl.when(pl.program_id(1) == 0)
        def _():
            xv = x_ref[...]
            r = lax.rsqrt(jnp.mean(xv * xv, axis=-1, keepdims=True) + RMS_EPS)
            h_ref[...] = (xv * r * g_ref[...]).astype(BF16)

        o_ref[...] = _dot(h_ref[...], w_ref[...]).astype(o_ref.dtype)

    if split:
        njh = nj // 2
        o_shape = jax.ShapeDtypeStruct((2, T, N // 2), BF16)
        o_spec = pl.BlockSpec((None, tm, tn), lambda i, j: (j // njh, i, j % njh))
    else:
        o_shape = jax.ShapeDtypeStruct((T, N), F32)
        o_spec = pl.BlockSpec((tm, tn), lambda i, j: (i, j))
    return pl.pallas_call(
        body, grid=(T // tm, nj),
        in_specs=[pl.BlockSpec((tm, K), lambda i, j: (i, 0)),
                  pl.BlockSpec((1, K), lambda i, j: (0, 0)),
                  (pl.BlockSpec((None, K, tn), lambda i, j: (j, 0, 0)) if blocked
                   else pl.BlockSpec((K, tn), lambda i, j: (0, j)))],
        out_specs=[o_spec, pl.BlockSpec((tm, K), lambda i, j: (i, 0))],
        out_shape=[o_shape, jax.ShapeDtypeStruct((T, K), BF16)],
        compiler_params=_cparams(2), name=name)(x, g, w)


def mm_nn(a, b, *, res, scale, swiglu, tm, tn, tk, name):
    T = a.shape[-2]
    K, N = b.shape
    nk = K // tk

    def body(*refs):
        if swiglu:
            g_ref, u_ref, b_ref, r_ref, o_ref, acc = refs
        else:
            a_ref, b_ref, r_ref, o_ref, acc = refs
        k = pl.program_id(2)

        @pl.when(k == 0)
        def _():
            acc[...] = jnp.zeros_like(acc)

        if swiglu:
            gv = g_ref[...].astype(F32)
            av = (gv * _sigmoid(gv) * u_ref[...].astype(F32)).astype(BF16)
        else:
            av = a_ref[...].astype(BF16)
        acc[...] += _dot(av, b_ref[...])

        @pl.when(k == nk - 1)
        def _():
            o_ref[...] = r_ref[...] + scale * acc[...]

    if swiglu:
        a_specs = [pl.BlockSpec((None, tm, tk), lambda i, j, k: (0, i, k)),
                   pl.BlockSpec((None, tm, tk), lambda i, j, k: (1, i, k))]
        a_args = [a, a]
    else:
        a_specs = [pl.BlockSpec((tm, tk), lambda i, j, k: (i, k))]
        a_args = [a]
    return pl.pallas_call(
        body, grid=(T // tm, N // tn, nk),
        in_specs=a_specs + [pl.BlockSpec((tk, tn), lambda i, j, k: (k, j)),
                            pl.BlockSpec((tm, tn), lambda i, j, k: (i, j))],
        out_specs=pl.BlockSpec((tm, tn), lambda i, j, k: (i, j)),
        out_shape=jax.ShapeDtypeStruct((T, N), F32),
        scratch_shapes=[pltpu.VMEM((tm, tn), F32)],
        compiler_params=_cparams(3), name=name)(*a_args, b, res)


def mm_nt(a, b, *, tm, tn, tk, name):
    T, K = a.shape
    N = b.shape[0]
    nk = K // tk

    def body(a_ref, b_ref, o_ref, acc):
        k = pl.program_id(2)

        @pl.when(k == 0)
        def _():
            acc[...] = jnp.zeros_like(acc)

        acc[...] += _dot_nt(a_ref[...].astype(BF16), b_ref[...])

        @pl.when(k == nk - 1)
        def _():
            o_ref[...] = acc[...]

    return pl.pallas_call(
        body, grid=(T // tm, N // tn, nk),
        in_specs=[pl.BlockSpec((tm, tk), lambda i, j, k: (i, k)),
                  pl.BlockSpec((tn, tk), lambda i, j, k: (j, k))],
        out_specs=pl.BlockSpec((tm, tn), lambda i, j, k: (i, j)),
        out_shape=jax.ShapeDtypeStruct((T, N), F32),
        scratch_shapes=[pltpu.VMEM((tm, tn), F32)],
        compiler_params=_cparams(3), name=name)(a, b)


def ffn_bwd_act(d, wd, gu, *, tm, tn, name):
    T, K = d.shape
    Fd = wd.shape[0]

    def body(d_ref, w_ref, g_ref, u_ref, dgu_ref, act_ref):
        da = 0.5 * _dot_nt(d_ref[...].astype(BF16), w_ref[...])
        gv = g_ref[...].astype(F32)
        uv = u_ref[...].astype(F32)
        s = _sigmoid(gv)
        silu = gv * s
        act_ref[...] = (silu * uv).astype(BF16)
        dgu_ref[0] = (da * uv * (s * (1.0 + gv * (1.0 - s)))).astype(BF16)
        dgu_ref[1] = (da * silu).astype(BF16)

    return pl.pallas_call(
        body, grid=(Fd // tn, T // tm),
        in_specs=[pl.BlockSpec((tm, K), lambda j, i: (i, 0)),
                  pl.BlockSpec((tn, K), lambda j, i: (j, 0)),
                  pl.BlockSpec((None, tm, tn), lambda j, i: (0, i, j)),
                  pl.BlockSpec((None, tm, tn), lambda j, i: (1, i, j))],
        out_specs=[pl.BlockSpec((2, tm, tn), lambda j, i: (0, i, j)),
                   pl.BlockSpec((tm, tn), lambda j, i: (i, j))],
        out_shape=[jax.ShapeDtypeStruct((2, T, Fd), BF16), jax.ShapeDtypeStruct((T, Fd), BF16)],
        compiler_params=_cparams(2), name=name)(d, wd, gu, gu)


def mm_nt_normbwd(a, b, x, g, res, *, a_split, tm, tk, name):
    T, Dm = x.shape
    blocked = b.ndim == 3
    assert not blocked or b.shape[2] == tk
    K = b.shape[0] * b.shape[2] if blocked else b.shape[1]
    nk = K // tk
    nkh = nk // 2
    has_res = res is not None

    def body(*refs):
        if has_res:
            a_ref, b_ref, x_ref, g_ref, r_ref, dx_ref, dg_ref, acc = refs
        else:
            a_ref, b_ref, x_ref, g_ref, dx_ref, dg_ref, acc = refs
        i = pl.program_id(0)
        k = pl.program_id(1)

        @pl.when(k == 0)
        def _():
            acc[...] = jnp.zeros_like(acc)

        acc[...] += _dot_nt(a_ref[...].astype(BF16), b_ref[...])

        @pl.when(k == nk - 1)
        def _():
            xv = x_ref[...]
            r = lax.rsqrt(jnp.mean(xv * xv, axis=-1, keepdims=True) + RMS_EPS)
            xh = xv * r
            dh = acc[...]
            dxh = dh * g_ref[...]
            dx = r * (dxh - xh * jnp.mean(dxh * xh, axis=-1, keepdims=True))
            if has_res:
                dx = dx + r_ref[...]
            dx_ref[...] = dx
            part = jnp.sum(dh * xh, axis=0, keepdims=True)

            @pl.when(i == 0)
            def _():
                dg_ref[...] = part

            @pl.when(i > 0)
            def _():
                dg_ref[...] += part

    if a_split:
        a_spec = pl.BlockSpec((None, tm, tk), lambda i, k: (k // nkh, i, k % nkh))
    else:
        a_spec = pl.BlockSpec((tm, tk), lambda i, k: (i, k))
    in_specs = [a_spec,
                (pl.BlockSpec((None, Dm, tk), lambda i, k: (k, 0, 0)) if blocked
                 else pl.BlockSpec((Dm, tk), lambda i, k: (0, k))),
                pl.BlockSpec((tm, Dm), lambda i, k: (i, 0)),
                pl.BlockSpec((1, Dm), lambda i, k: (0, 0))]
    args = [a, b, x, g]
    if has_res:
        in_specs.append(pl.BlockSpec((tm, Dm), lambda i, k: (i, 0)))
        args.append(res)
    return pl.pallas_call(
        body, grid=(T // tm, nk), in_specs=in_specs,
        out_specs=[pl.BlockSpec((tm, Dm), lambda i, k: (i, 0)),
                   pl.BlockSpec((1, Dm), lambda i, k: (0, 0))],
        out_shape=[jax.ShapeDtypeStruct((T, Dm), F32), jax.ShapeDtypeStruct((1, Dm), F32)],
        scratch_shapes=[pltpu.VMEM((tm, Dm), F32)],
        compiler_params=_cparams(2), name=name)(*args)


def mm_tn(a, b, *, scale, a_split, b_split, tm, tn, tk, name, out_blocked=False):
    T = a.shape[-2]
    M = a.shape[-1] * (2 if a_split else 1)
    N = b.shape[-1] * (2 if b_split else 1)
    ni, nj, nk = M // tm, N // tn, T // tk
    nih, njh = ni // 2, nj // 2

    def body(a_ref, b_ref, o_ref, acc):
        k = pl.program_id(2)

        @pl.when(k == 0)
        def _():
            acc[...] = jnp.zeros_like(acc)

        acc[...] += _dot_tn(a_ref[...].astype(BF16), b_ref[...].astype(BF16))

        @pl.when(k == nk - 1)
        def _():
            o_ref[...] = (acc[...] * scale).astype(o_ref.dtype)

    if a_split:
        a_spec = pl.BlockSpec((None, tk, tm), lambda i, j, k: (i // nih, k, i % nih))
    else:
        a_spec = pl.BlockSpec((tk, tm), lambda i, j, k: (k, i))
    if b_split:
        b_spec = pl.BlockSpec((None, tk, tn), lambda i, j, k: (j // njh, k, j % njh))
    else:
        b_spec = pl.BlockSpec((tk, tn), lambda i, j, k: (k, j))
    if out_blocked:
        o_spec = pl.BlockSpec((None, None, tm, tn), lambda i, j, k: (j, i, 0, 0))
        o_shape = jax.ShapeDtypeStruct((nj, ni, tm, tn), BF16)
    else:
        o_spec = pl.BlockSpec((tm, tn), lambda i, j, k: (i, j))
        o_shape = jax.ShapeDtypeStruct((M, N), BF16)
    return pl.pallas_call(
        body, grid=(ni, nj, nk), in_specs=[a_spec, b_spec], out_specs=o_spec, out_shape=o_shape,
        scratch_shapes=[pltpu.VMEM((tm, tn), F32)],
        compiler_params=_cparams(3), name=name)(a, b)


def loss_kernel(y, target, *, tm, name):
    T, Dm = y.shape

    def body(y_ref, t_ref, l_ref, dy_ref):
        e = y_ref[...] - t_ref[...]
        dy_ref[...] = e * (1.0 / Dm)
        part = (0.5 / Dm) * jnp.sum(jnp.sum(e * e, axis=-1, keepdims=True), axis=0, keepdims=True)
        part = jnp.broadcast_to(part, (8, LANES))

        @pl.when(pl.program_id(0) == 0)
        def _():
            l_ref[...] = part

        @pl.when(pl.program_id(0) > 0)
        def _():
            l_ref[...] += part

    return pl.pallas_call(
        body, grid=(T // tm,),
        in_specs=[pl.BlockSpec((tm, Dm), lambda i: (i, 0)), pl.BlockSpec((tm, Dm), lambda i: (i, 0))],
        out_specs=[pl.BlockSpec((8, LANES), lambda i: (0, 0)), pl.BlockSpec((tm, Dm), lambda i: (i, 0))],
        out_shape=[jax.ShapeDtypeStruct((8, LANES), F32), jax.ShapeDtypeStruct((T, Dm), F32)],
        compiler_params=_cparams(1), name=name)(y, target)


def adamw(w, g, m, v, *, br, name):
    R, C = w.shape

    def body(w_ref, g_ref, m_ref, v_ref, d_ref, nm_ref, nv_ref):
        gv = g_ref[...]
        nm = ADAM_B1 * m_ref[...] + (1.0 - ADAM_B1) * gv
        nv = ADAM_B2 * v_ref[...] + (1.0 - ADAM_B2) * (gv * gv)
        m_hat = nm / (1.0 - ADAM_B1 ** ADAM_STEP)
        v_hat = nv / (1.0 - ADAM_B2 ** ADAM_STEP)
        d_ref[...] = -ADAM_LR * (m_hat / (jnp.sqrt(v_hat) + ADAM_EPS) + ADAM_WD * w_ref[...])
        nm_ref[...] = nm
        nv_ref[...] = nv

    spec = pl.BlockSpec((br, C), lambda i: (i, 0))
    shp = jax.ShapeDtypeStruct((R, C), F32)
    return pl.pallas_call(
        body, grid=(R // br,), in_specs=[spec] * 4, out_specs=[spec] * 3, out_shape=[shp] * 3,
        compiler_params=_cparams(1), name=name)(w, g, m, v)


def _lane0():
    return lax.broadcasted_iota(jnp.int32, (1, LANES), 1) < HEAD_DIM


def _half_sum(x, m0):
    s0 = jnp.sum(jnp.where(m0, x, 0.0), axis=-1, keepdims=True)
    s1 = jnp.sum(jnp.where(m0, 0.0, x), axis=-1, keepdims=True)
    return jnp.where(m0, s0, s1)


def _half_pick(x, m0, e):
    sel = m0 if e == 0 else jnp.logical_not(m0)
    return jnp.max(jnp.where(sel, x, NEG), axis=-1, keepdims=True)


def _head_rms(x, m0):
    return lax.rsqrt(_half_sum(x * x, m0) * (1.0 / HEAD_DIM) + RMS_EPS)


def _alibi(n):
    return [float(2.0 ** (-8.0 * (h + 1) / n)) for h in range(n)]


def _mask_half(x, m0, e):
    return jnp.where(m0, x, 0.0) if e == 0 else jnp.where(m0, 0.0, x)


def _band_masks2(max_dist, has_prev, live):
    row = lax.broadcasted_iota(jnp.int32, (2 * BLK, 2 * BLK), 0)
    col = lax.broadcasted_iota(jnp.int32, (2 * BLK, 2 * BLK), 1)
    dist = (row & (BLK - 1)) - col + BLK
    lim = jnp.where(live, max_dist, -1)
    first = jnp.where(has_prev, 0, BLK)
    valid = (dist >= 0) & (dist <= lim) & (col >= first)
    top = lax.broadcasted_iota(jnp.int32, (2 * BLK, 1), 0) < BLK
    return dist.astype(F32), valid, top


def _stack_heads(x, m0, kes):
    parts = []
    for e in range(2):
        h = _mask_half(x, m0, e)
        parts.append(pltpu.roll(h, HEAD_DIM, 1) if kes[e] != e else h)
    return jnp.concatenate(parts, axis=0)


def _unstack_heads(y, m0, kes):
    parts = []
    for e in range(2):
        h = y[e * BLK:(e + 1) * BLK]
        parts.append(pltpu.roll(h, HEAD_DIM, 1) if kes[e] != e else h)
    return jnp.where(m0, parts[0], parts[1])


def _rows(r, dil):
    return pl.ds(r, BLK, stride=dil) if dil > 1 else pl.ds(0, BLK)


def _band_units(dil, nsub):
    assert dil == 1 or nsub == 1
    if nsub == 1:
        return [(_rows(r, dil), ("prev", _rows(r, dil)), 0) for r in range(dil)]
    units = [(pl.ds(0, BLK), ("prev", pl.ds(0, BLK)), 0)]
    units += [(pl.ds(BLK * s, BLK), ("cur", pl.ds(BLK * (s - 1), BLK)), s) for s in range(1, nsub)]
    return units


def _band_specs(dil, nsub, ppk, q_blk, k_blk, v_blk, kv_shared, nb):
    RB = BLK * dil * nsub
    PB = BLK if nsub > 1 else RB
    qw = LANES * ppk
    kw = LANES if kv_shared else qw

    def cur(i):
        return jnp.minimum(i, nb - 1)

    def prev(i):
        return jnp.maximum(i * nsub - 1, 0) if nsub > 1 else jnp.maximum(i - 1, 0)

    def kidx(base):
        return (lambda p, i: (cur(i), base)) if kv_shared else (lambda p, i: (cur(i), base + p))

    def pidx(base):
        return (lambda p, i: (prev(i), base)) if kv_shared else (lambda p, i: (prev(i), base + p))

    return [pl.BlockSpec((RB, qw), lambda p, i: (cur(i), q_blk + p)),
            pl.BlockSpec((RB, kw), kidx(k_blk)), pl.BlockSpec((PB, kw), pidx(k_blk)),
            pl.BlockSpec((RB, kw), kidx(v_blk)), pl.BlockSpec((PB, kw), pidx(v_blk))]


def banded_fwd(qkv, q_gain2, k_gain2, slopes, sinks, *, dil, nsub, ppk, q_blk, k_blk, v_blk, n_heads, group,
               max_dist, name):
    T = qkv.shape[0]
    RB = BLK * dil * nsub
    nb = T // RB
    npair = n_heads // 2
    kv_shared = group > 1
    scale = HEAD_DIM ** -0.5
    has_sink = sinks is not None

    def body(*refs):
        slope_ref = refs[0]
        if has_sink:
            sink_ref, refs = refs[1], refs[2:]
        else:
            refs = refs[1:]
        q_ref, kc_ref, kp_ref, vc_ref, vp_ref, qg_ref, kg_ref, o_ref, l_ref = refs
        pb = pl.program_id(0)
        i = pl.program_id(1)
        m0 = _lane0()
        distf, valid_first, top = _band_masks2(max_dist, i > 0, i >= 0)
        valid_inner = _band_masks2(max_dist, i >= 0, i >= 0)[1] if nsub > 1 else None
        qg = qg_ref[...]
        kg = kg_ref[...]
        for rows, (src, prows), sub in _band_units(dil, nsub):
            valid = valid_first if sub == 0 else valid_inner
            kpr, vpr = (kp_ref, vp_ref) if src == "prev" else (kc_ref, vc_ref)
            kcache = {}
            for jp in range(ppk):
                cs = pl.ds(LANES * jp, LANES)
                jk = 0 if kv_shared else jp
                if jk not in kcache:
                    ks = pl.ds(LANES * jk, LANES)
                    kcat = jnp.concatenate([kpr[prows, ks], kc_ref[rows, ks]], axis=0)
                    vcat = jnp.concatenate([vpr[prows, ks], vc_ref[rows, ks]], axis=0)
                    kcache[jk] = ((kcat * _head_rms(kcat, m0) * kg).astype(BF16), vcat.astype(BF16))
                kn, vcat = kcache[jk]
                qv = q_ref[rows, cs]
                qn = qv * _head_rms(qv, m0) * qg
                kes = [((2 * jp + e) // group) % 2 if kv_shared else e for e in range(2)]
                hidx = 2 * (pb * ppk + jp)
                qs = _stack_heads(qn, m0, kes).astype(BF16)
                slope = jnp.where(top, slope_ref[hidx], slope_ref[hidx + 1])
                s = jnp.where(valid, _dot_nt(qs, kn) * scale - slope * distf, NEG)
                m = jnp.max(s, axis=-1, keepdims=True)
                if has_sink:
                    sk = jnp.where(top, sink_ref[hidx], sink_ref[hidx + 1])
                    m = jnp.maximum(m, sk)
                p = jnp.exp(s - m)
                den = jnp.sum(p, axis=-1, keepdims=True)
                if has_sink:
                    den = den + jnp.exp(sk - m)
                o_full = _dot((p * (1.0 / den)).astype(BF16), vcat)
                o_ref[rows, cs] = _unstack_heads(o_full, m0, kes)
                l_ref[rows, cs] = _unstack_heads(jnp.broadcast_to(m + jnp.log(den), (2 * BLK, LANES)), m0, [0, 1])

    smem = pl.BlockSpec(memory_space=pltpu.SMEM)
    qw = LANES * ppk
    gspec = pl.BlockSpec((1, LANES), lambda p, i: (0, 0))
    ospec = pl.BlockSpec((RB, qw), lambda p, i: (i, p))
    oshape = jax.ShapeDtypeStruct((T, n_heads * HEAD_DIM), F32)
    args = [slopes] + ([sinks] if has_sink else []) + [qkv] * 5 + [q_gain2, k_gain2]
    return pl.pallas_call(
        body, grid=(npair // ppk, nb),
        in_specs=[smem] * (2 if has_sink else 1) + _band_specs(dil, nsub, ppk, q_blk, k_blk, v_blk, kv_shared, nb)
        + [gspec, gspec],
        out_specs=[ospec, ospec], out_shape=[oshape, oshape],
        compiler_params=_cparams(2), name=name)(*args)


def banded_bwd(qkv, q_gain2, k_gain2, slopes, sinks, do, o, lse, w, omix, *, dil, nsub, ppk, q_blk, k_blk, v_blk,
               n_heads, group, max_dist, do_blk, name):
    T = qkv.shape[0]
    RB = BLK * dil * nsub
    nb = T // RB
    npair = n_heads // 2
    kv_shared = group > 1
    scale = HEAD_DIM ** -0.5
    has_sink = sinks is not None
    mixed = w is not None
    qw = LANES * ppk

    def body(*refs):
        slope_ref = refs[0]
        if has_sink:
            sink_ref, refs = refs[1], refs[2:]
        else:
            refs = refs[1:]
        q_ref, kc_ref, kp_ref, vc_ref, vp_ref, qg_ref, kg_ref, do_ref, o_ref, l_ref = refs[:10]
        refs = refs[10:]
        if mixed:
            w_ref, om_ref, refs = refs[0], refs[1], refs[2:]
        dq_ref, dk_ref, dv_ref, dqg_ref, dkg_ref, dsk_ref, ck_ref, cv_ref = refs
        pb = pl.program_id(0)
        i = pl.program_id(1)
        live = i < nb
        m0 = _lane0()
        lane = lax.broadcasted_iota(jnp.int32, (1, LANES), 1)
        distf, valid_first, top = _band_masks2(max_dist, i > 0, live)
        valid_inner = _band_masks2(max_dist, i >= 0, live)[1] if nsub > 1 else None
        livef = live.astype(F32)
        qg = qg_ref[...]
        kg = kg_ref[...]

        def stack_rows(x2):
            return jnp.concatenate([_half_pick(x2, m0, 0), _half_pick(x2, m0, 1)], axis=0)

        @pl.when((pb == 0) & (i == 0))
        def _():
            dqg_ref[...] = jnp.zeros_like(dqg_ref)
            dkg_ref[...] = jnp.zeros_like(dkg_ref)
            dsk_ref[...] = jnp.zeros_like(dsk_ref)

        @pl.when(i == 0)
        def _():
            ck_ref[...] = jnp.zeros_like(ck_ref)
            cv_ref[...] = jnp.zeros_like(cv_ref)

        dqg_acc = jnp.zeros((1, LANES), F32)
        dkg_acc = jnp.zeros((1, LANES), F32)
        dsk_acc = jnp.zeros((1, LANES), F32)
        if nsub > 1:
            dk_ref[...] = ck_ref[...]
            dv_ref[...] = cv_ref[...]
        for rows, (src, prows), sub in _band_units(dil, nsub):
            valid = valid_first if sub == 0 else valid_inner
            kpr, vpr = (kp_ref, vp_ref) if src == "prev" else (kc_ref, vc_ref)
            for jp in range(ppk):
                cs = pl.ds(LANES * jp, LANES)
                ks = pl.ds(0, LANES) if kv_shared else cs
                kcat = jnp.concatenate([kpr[prows, ks], kc_ref[rows, ks]], axis=0)
                vcat = jnp.concatenate([vpr[prows, ks], vc_ref[rows, ks]], axis=0).astype(BF16)
                rk = _head_rms(kcat, m0)
                kh = kcat * rk
                kn = (kh * kg).astype(BF16)
                qv = q_ref[rows, cs]
                rq = _head_rms(qv, m0)
                qh = qv * rq
                dov = do_ref[rows, cs]
                lv = l_ref[rows, cs]
                if mixed:
                    wv = w_ref[rows, cs]
                    dmix = _half_sum(dov * om_ref[rows, cs], m0)
                    dov = dov * wv
                delta2 = _half_sum(dov * o_ref[rows, cs], m0)
                shift = stack_rows(wv * dmix if mixed else delta2)
                kes = [((2 * jp + e) // group) % 2 if kv_shared else e for e in range(2)]
                hidx = 2 * (pb * ppk + jp)
                qs = _stack_heads(qh * qg, m0, kes).astype(BF16)
                dos = _stack_heads(dov, m0, kes).astype(BF16)
                lse = stack_rows(lv)
                slope = jnp.where(top, slope_ref[hidx], slope_ref[hidx + 1])
                p = jnp.where(valid, jnp.exp(_dot_nt(qs, kn) * scale - slope * distf - lse), 0.0)
                ds = (p * (_dot_nt(dos, vcat) - shift)).astype(BF16)
                dqn = _unstack_heads(_dot(ds, kn), m0, kes) * scale
                dkn = _dot_tn(ds, qs) * scale
                dvv = _dot_tn(p.astype(BF16), dos)
                if has_sink:
                    sk = jnp.where(top, sink_ref[hidx], sink_ref[hidx + 1])
                    contrib = -jnp.exp(sk - lse) * stack_rows(delta2) * livef
                    for e in range(2):
                        tot = jnp.sum(contrib[e * BLK:(e + 1) * BLK], axis=0, keepdims=True)
                        dsk_acc = dsk_acc + jnp.where(lane == (2 * jp + e), tot, 0.0)
                dqg_acc = dqg_acc + jnp.sum(dqn * qh, axis=0, keepdims=True)
                dqh = dqn * qg
                dq_raw = rq * (dqh - qh * (_half_sum(dqh * qh, m0) * (1.0 / HEAD_DIM)))
                dkg_acc = dkg_acc + jnp.sum(dkn * kh, axis=0, keepdims=True)
                dkh = dkn * kg
                dk_raw = rk * (dkh - kh * (_half_sum(dkh * kh, m0) * (1.0 / HEAD_DIM)))

                @pl.when(live)
                def _():
                    dq_ref[rows, cs] = dq_raw

                if nsub == 1:
                    dk_ref[rows, cs] = ck_ref[rows, cs] + dk_raw[:BLK]
                    dv_ref[rows, cs] = cv_ref[rows, cs] + dvv[:BLK]
                elif sub == 0:
                    last = pl.ds(RB - BLK, BLK)
                    dk_ref[last, cs] += dk_raw[:BLK]
                    dv_ref[last, cs] += dvv[:BLK]
                else:
                    ck_ref[prows, cs] += dk_raw[:BLK]
                    cv_ref[prows, cs] += dvv[:BLK]
                ck_ref[rows, cs] = dk_raw[BLK:]
                cv_ref[rows, cs] = dvv[BLK:]
        dqg_ref[...] += dqg_acc
        dkg_ref[...] += dkg_acc
        dsk_ref[...] += dsk_acc

    smem = pl.BlockSpec(memory_space=pltpu.SMEM)
    gspec = pl.BlockSpec((1, LANES), lambda p, i: (0, 0))

    def cur(i):
        return jnp.minimum(i, nb - 1)

    qspec = pl.BlockSpec((RB, qw), lambda p, i: (cur(i), p))
    dospec = pl.BlockSpec((RB, qw), lambda p, i: (cur(i), do_blk + p))
    kvout = pl.BlockSpec((RB, qw), lambda p, i: (jnp.maximum(i - 1, 0), p))
    in_specs = ([smem] * (2 if has_sink else 1) + _band_specs(dil, nsub, ppk, q_blk, k_blk, v_blk, kv_shared, nb)
                + [gspec, gspec, dospec, qspec, qspec] + ([qspec, qspec] if mixed else []))
    args = ([slopes] + ([sinks] if has_sink else []) + [qkv] * 5 + [q_gain2, k_gain2, do, o, lse]
            + ([w, omix] if mixed else []))
    full = jax.ShapeDtypeStruct((T, n_heads * HEAD_DIM), F32)
    row = jax.ShapeDtypeStruct((1, LANES), F32)
    return pl.pallas_call(
        body, grid=(npair // ppk, nb + 1), in_specs=in_specs,
        out_specs=[qspec, kvout, kvout, gspec, gspec, gspec],
        out_shape=[full, full, full, row, row, row],
        scratch_shapes=[pltpu.VMEM((RB, qw), F32), pltpu.VMEM((RB, qw), F32)],
        compiler_params=_cparams(2), name=name)(*args)


def mix_fwd(o1, o2, o3, l1, l2, l3, *, tm, name):
    T, C = o1.shape

    def body(o1r, o2r, o3r, l1r, l2r, l3r, o_ref, w1r, w2r, w3r):
        a, b, c = l1r[...], l2r[...], l3r[...]
        m = jnp.maximum(jnp.maximum(a, b), c)
        ea, eb, ec = jnp.exp(a - m), jnp.exp(b - m), jnp.exp(c - m)
        inv = 1.0 / (ea + eb + ec)
        wa, wb, wc = ea * inv, eb * inv, ec * inv
        o_ref[...] = wa * o1r[...] + wb * o2r[...] + wc * o3r[...]
        w1r[...] = wa
        w2r[...] = wb
        w3r[...] = wc

    spec = pl.BlockSpec((tm, C), lambda i: (i, 0))
    shp = jax.ShapeDtypeStruct((T, C), F32)
    return pl.pallas_call(body, grid=(T // tm,), in_specs=[spec] * 6, out_specs=[spec] * 4, out_shape=[shp] * 4,
                          compiler_params=_cparams(1), name=name)(o1, o2, o3, l1, l2, l3)


def assemble_odd(parts, *, tm, name):
    T, C = parts[0][0].shape

    def body(*refs):
        o_ref = refs[9]
        for j in range(3):
            o_ref[:, pl.ds(C * j, C)] = refs[j][...] + refs[3 + j][...] + refs[6 + j][...]

    spec = pl.BlockSpec((tm, C), lambda i: (i, 0))
    flat = [parts[p][j] for p in range(3) for j in range(3)]
    return pl.pallas_call(body, grid=(T // tm,), in_specs=[spec] * 9,
                          out_specs=pl.BlockSpec((tm, 3 * C), lambda i: (i, 0)),
                          out_shape=jax.ShapeDtypeStruct((T, 3 * C), F32),
                          compiler_params=_cparams(1), name=name)(*flat)


def assemble_even(dqa, dka4, dva4, dqb, dkb, dvb, *, tm, name):
    T = dqa.shape[0]
    W = 512

    def body(dqa_r, dka_r, dva_r, dqb_r, dkb_r, dvb_r, o_ref):
        o_ref[:, pl.ds(0, W)] = dqa_r[...]
        ka = dka_r[...]
        va = dva_r[...]
        o_ref[:, pl.ds(512, LANES)] = ka[:, 0:128] + ka[:, 128:256] + ka[:, 256:384] + ka[:, 384:512]
        o_ref[:, pl.ds(640, LANES)] = va[:, 0:128] + va[:, 128:256] + va[:, 256:384] + va[:, 384:512]
        o_ref[:, pl.ds(768, W)] = dqb_r[...]
        o_ref[:, pl.ds(1280, W)] = dkb_r[...]
        o_ref[:, pl.ds(1792, W)] = dvb_r[...]

    spec = pl.BlockSpec((tm, W), lambda i: (i, 0))
    return pl.pallas_call(body, grid=(T // tm,), in_specs=[spec] * 6,
                          out_specs=pl.BlockSpec((tm, 2304), lambda i: (i, 0)),
                          out_shape=jax.ShapeDtypeStruct((T, 2304), F32),
                          compiler_params=_cparams(1), name=name)(dqa, dka4, dva4, dqb, dkb, dvb)


STICK_T = 256
STICK_DEAD = -110.0


def _split_bf16(x):
    hi = x.astype(BF16)
    lo = (x - hi.astype(F32)).astype(BF16)
    return hi, lo


def _stick_logits(qm, kt, scale, diag):
    n = STICK_T
    row = lax.broadcasted_iota(jnp.int32, (n, n), 0)
    col = lax.broadcasted_iota(jnp.int32, (n, n), 1)
    mask = col < row + jnp.where(diag, 0, n)
    z = _dot_nt(qm, kt) * scale
    lneg = -(jnp.maximum(z, 0.0) + jnp.log(1.0 + jnp.exp(-jnp.abs(z))))
    lpos = z + lneg
    lk = jnp.where(mask, lneg, 0.0)
    return mask, lpos, lneg, lk


def _cumsum_mm(x, tri):
    hi, lo = _split_bf16(x)
    return _dot(hi, tri) + _dot(lo, tri)


def stick_fwd(qkv, *, q_blk, k_blk, v_blk, n_pairs, name, riders=None, rider_args=()):
    T = qkv.shape[0]
    n = STICK_T
    nq = T // n
    scale = HEAD_DIM ** -0.5
    nc = riders.n if riders is not None else 0
    n_steps = n_pairs * nq
    stage_at = (0, (3 * n_steps) // 4, n_steps - 1, n_steps - 1)

    def body(*refs):
        q_ref, k_ref, v_ref = refs[:3]
        x_refs, o_ref = refs[3:3 + nc], refs[3 + nc]
        out_refs, sems = refs[4 + nc:4 + 2 * nc], refs[4 + 2 * nc:]
        i = pl.program_id(1)
        step_id = pl.program_id(0) * nq + i

        def ride(which):
            if riders is not None:
                @pl.when(step_id == stage_at[which])
                def _():
                    riders.stage(which, x_refs, out_refs, sems)

        ride(0)
        ride(1)
        m0 = _lane0()
        r2 = lax.broadcasted_iota(jnp.int32, (n, n), 0)
        c2 = lax.broadcasted_iota(jnp.int32, (n, n), 1)
        tri_after = (r2 > c2).astype(BF16)
        qv = q_ref[...]
        out = jnp.zeros((n, LANES), F32)
        for e in range(2):
            qm = _mask_half(qv, m0, e).astype(BF16)

            def alive(st):
                t, _, carry = st
                return (t <= i) & (jnp.max(carry) > STICK_DEAD)

            def step(st, e=e, qm=qm):
                t, acc, carry = st
                start = pl.multiple_of((i - t) * n, n)
                kt = k_ref[pl.ds(start, n), :].astype(BF16)
                vt = _mask_half(v_ref[pl.ds(start, n), :], m0, e).astype(BF16)
                mask, lpos, _, lk = _stick_logits(qm, kt, scale, t == 0)
                after = _cumsum_mm(lk, tri_after) + carry
                a = jnp.where(mask, jnp.exp(lpos + after), 0.0)
                acc = acc + _dot(a.astype(BF16), vt)
                carry = carry + jnp.sum(lk, axis=-1, keepdims=True)
                return t + 1, acc, carry

            _, acc, _ = lax.while_loop(alive, step, (jnp.int32(0), jnp.zeros((n, LANES), F32),
                                                     jnp.zeros((n, 1), F32)))
            out = out + acc
        o_ref[...] = out
        ride(2)
        ride(3)

    outs = pl.pallas_call(
        body, grid=(n_pairs, nq),
        in_specs=[pl.BlockSpec((n, LANES), lambda p, i: (i, q_blk + p)),
                  pl.BlockSpec((T, LANES), lambda p, i: (0, k_blk + p)),
                  pl.BlockSpec((T, LANES), lambda p, i: (0, v_blk + p))] + [_ANY] * nc,
        out_specs=[pl.BlockSpec((n, LANES), lambda p, i: (i, p))] + [_ANY] * nc,
        out_shape=[jax.ShapeDtypeStruct((T, n_pairs * LANES), F32)] + (riders.shapes if nc else []),
        scratch_shapes=riders.sems if nc else [],
        compiler_params=_cparams(2), name=name)(qkv, qkv, qkv, *rider_args)
    return outs[0], list(outs[1:])


def stick_bwd(qkv, do, *, q_blk, k_blk, v_blk, do_blk, n_pairs, name, riders=None, rider_args=()):
    T = qkv.shape[0]
    n = STICK_T
    nq = T // n
    scale = HEAD_DIM ** -0.5
    nc = riders.n if riders is not None else 0

    def body(*refs):
        q_ref, k_ref, v_ref, do_ref = refs[:4]
        pre_refs = refs[4:4 + nc]
        dq_ref, dk_ref, dv_ref = refs[4 + nc:7 + nc]
        land_refs = refs[7 + nc:7 + 2 * nc]
        a_keep, g_keep, s_keep = refs[7 + 2 * nc:10 + 2 * nc]
        sems = refs[10 + 2 * nc:]
        i = pl.program_id(1)
        first_step = (pl.program_id(0) == 0) & (i == 0)
        last_step = (pl.program_id(0) == n_pairs - 1) & (i == nq - 1)
        m0 = _lane0()
        r2 = lax.broadcasted_iota(jnp.int32, (n, n), 0)
        c2 = lax.broadcasted_iota(jnp.int32, (n, n), 1)
        tri_after = (r2 > c2).astype(BF16)
        tri_from = (r2 >= c2).astype(BF16)

        if riders is not None:
            @pl.when(first_step)
            def _():
                riders.start(pre_refs, land_refs, sems)

        @pl.when(i == 0)
        def _():
            dk_ref[...] = jnp.zeros_like(dk_ref)
            dv_ref[...] = jnp.zeros_like(dv_ref)

        qv = q_ref[...]
        dov = do_ref[...]
        dq_out = jnp.zeros((n, LANES), F32)
        for e in range(2):
            qm = _mask_half(qv, m0, e).astype(BF16)
            dom = _mask_half(dov, m0, e).astype(BF16)

            def alive(st):
                t, carry, _ = st
                return (t <= i) & (jnp.max(carry) > STICK_DEAD)

            def scan(st, qm=qm, dom=dom):
                t, carry, gtot = st
                start = pl.multiple_of((i - t) * n, n)
                kt = k_ref[pl.ds(start, n), :].astype(BF16)
                vt = v_ref[pl.ds(start, n), :].astype(BF16)
                mask, lpos, lneg, lk = _stick_logits(qm, kt, scale, t == 0)
                a = jnp.where(mask, jnp.exp(lpos + _cumsum_mm(lk, tri_after) + carry), 0.0)
                g = _dot_nt(dom, vt) * a
                a_keep[t] = a.astype(BF16)
                g_keep[t] = g
                s_keep[t] = jnp.exp(lneg).astype(BF16)
                return (t + 1, carry + jnp.sum(lk, axis=-1, keepdims=True),
                        gtot + jnp.sum(g, axis=-1, keepdims=True))

            z1 = jnp.zeros((n, 1), F32)
            n_live, _, gtot = lax.while_loop(alive, scan, (jnp.int32(0), z1, z1))

            def step(t, st, e=e, qm=qm, dom=dom, gtot=gtot):
                dq_acc, gright = st
                start = pl.multiple_of((i - t) * n, n)
                g = g_keep[t]
                sneg = s_keep[t].astype(F32)
                before = gtot - (_cumsum_mm(g, tri_from) + gright)
                mask = c2 < r2 + jnp.where(t == 0, 0, n)
                dz = jnp.where(mask, g * sneg - before * (1.0 - sneg), 0.0) * scale
                dzb = dz.astype(BF16)
                dq_acc = dq_acc + _dot(dzb, _mask_half(k_ref[pl.ds(start, n), :], m0, e).astype(BF16))
                dk_ref[pl.ds(start, n), :] += _dot_tn(dzb, qm)
                dv_ref[pl.ds(start, n), :] += _dot_tn(a_keep[t], dom)
                return dq_acc, gright + jnp.sum(g, axis=-1, keepdims=True)

            dq_acc, _ = lax.fori_loop(0, n_live, step, (jnp.zeros((n, LANES), F32), z1))
            dq_out = dq_out + dq_acc
        dq_ref[...] = dq_out

        if riders is not None:
            @pl.when(last_step)
            def _():
                riders.finish(pre_refs, land_refs, sems)

    tile = pl.BlockSpec((n, LANES), lambda p, i: (i, p))
    whole = pl.BlockSpec((T, LANES), lambda p, i: (0, p))
    shp = jax.ShapeDtypeStruct((T, n_pairs * LANES), F32)
    outs = pl.pallas_call(
        body, grid=(n_pairs, nq),
        in_specs=[pl.BlockSpec((n, LANES), lambda p, i: (i, q_blk + p)),
                  pl.BlockSpec((T, LANES), lambda p, i: (0, k_blk + p)),
                  pl.BlockSpec((T, LANES), lambda p, i: (0, v_blk + p)),
                  pl.BlockSpec((n, LANES), lambda p, i: (i, do_blk + p))] + [_ANY] * nc,
        out_specs=[tile, whole, whole] + [_ANY] * nc,
        out_shape=[shp, shp, shp] + (riders.shapes if nc else []),
        scratch_shapes=[pltpu.VMEM((nq, n, n), BF16), pltpu.VMEM((nq, n, n), F32), pltpu.VMEM((nq, n, n), BF16)]
        + (riders.sems if nc else []),
        compiler_params=_cparams(2), name=name)(qkv, qkv, qkv, do, *rider_args)
    return outs[0], outs[1], outs[2], list(outs[3:])


def _xnorm(x):
    r = lax.rsqrt(jnp.mean(x * x, axis=-1, keepdims=True) + RMS_EPS)
    return r, x * r


def xattn_fwd(qraw, kvraw, q_gain, k_gain, *, tm, name):
    T = qraw.shape[0]
    scale = X_HEAD_DIM ** -0.5
    W = X_HEADS * X_HEAD_DIM

    def body(q_ref, kv_ref, qg_ref, kg_ref, o_ref):
        for h in range(X_HEADS):
            cs = pl.ds(X_HEAD_DIM * h, X_HEAD_DIM)
            _, qh = _xnorm(q_ref[:, cs])
            _, kh = _xnorm(kv_ref[:, cs])
            qn = (qh * qg_ref[...]).astype(BF16)
            kn = (kh * kg_ref[...]).astype(BF16)
            v = kv_ref[:, pl.ds(W + X_HEAD_DIM * h, X_HEAD_DIM)].astype(BF16)
            s = _dot_nt(qn, kn) * scale
            m = jnp.max(s, axis=-1, keepdims=True)
            p = jnp.exp(s - m)
            p = p / jnp.sum(p, axis=-1, keepdims=True)
            o_ref[:, cs] = _dot(p.astype(BF16), v)

    gspec = pl.BlockSpec((1, X_HEAD_DIM), lambda i: (0, 0))
    return pl.pallas_call(
        body, grid=(T // tm,),
        in_specs=[pl.BlockSpec((tm, W), lambda i: (i, 0)), pl.BlockSpec((MEM_LEN, 2 * W), lambda i: (0, 0)),
                  gspec, gspec],
        out_specs=pl.BlockSpec((tm, W), lambda i: (i, 0)),
        out_shape=jax.ShapeDtypeStruct((T, W), F32),
        compiler_params=_cparams(1), name=name)(qraw, kvraw, q_gain, k_gain)


def xattn_bwd(qraw, kvraw, q_gain, k_gain, do, o, *, tm, name):
    T = qraw.shape[0]
    nt = T // tm
    scale = X_HEAD_DIM ** -0.5
    W = X_HEADS * X_HEAD_DIM

    def body(q_ref, kv_ref, qg_ref, kg_ref, do_ref, o_ref, dq_ref, dkv_ref, dqg_ref, dkg_ref, dkn_ref):
        i = pl.program_id(0)

        @pl.when(i == 0)
        def _():
            dkv_ref[...] = jnp.zeros_like(dkv_ref)
            dkn_ref[...] = jnp.zeros_like(dkn_ref)
            dqg_ref[...] = jnp.zeros_like(dqg_ref)
            dkg_ref[...] = jnp.zeros_like(dkg_ref)

        qg = qg_ref[...]
        kg = kg_ref[...]
        dqg_acc = jnp.zeros((1, X_HEAD_DIM), F32)
        for h in range(X_HEADS):
            cs = pl.ds(X_HEAD_DIM * h, X_HEAD_DIM)
            vs = pl.ds(W + X_HEAD_DIM * h, X_HEAD_DIM)
            rq, qh = _xnorm(q_ref[:, cs])
            _, kh = _xnorm(kv_ref[:, cs])
            qn = (qh * qg).astype(BF16)
            kn = (kh * kg).astype(BF16)
            v = kv_ref[:, vs].astype(BF16)
            s = _dot_nt(qn, kn) * scale
            m = jnp.max(s, axis=-1, keepdims=True)
            p = jnp.exp(s - m)
            p = p / jnp.sum(p, axis=-1, keepdims=True)
            dov = do_ref[:, cs]
            delta = jnp.sum(dov * o_ref[:, cs], axis=-1, keepdims=True)
            dob = dov.astype(BF16)
            ds = (p * (_dot_nt(dob, v) - delta)).astype(BF16)
            dqn = _dot(ds, kn) * scale
            dkn_ref[:, cs] += _dot_tn(ds, qn) * scale
            dkv_ref[:, vs] += _dot_tn(p.astype(BF16), dob)
            dqg_acc = dqg_acc + jnp.sum(dqn * qh, axis=0, keepdims=True)
            dqh = dqn * qg
            dq_ref[:, cs] = rq * (dqh - qh * jnp.mean(dqh * qh, axis=-1, keepdims=True))
        dqg_ref[...] += dqg_acc

        @pl.when(i == nt - 1)
        def _():
            dkg_acc = jnp.zeros((1, X_HEAD_DIM), F32)
            for h in range(X_HEADS):
                cs = pl.ds(X_HEAD_DIM * h, X_HEAD_DIM)
                rk, kh = _xnorm(kv_ref[:, cs])
                dkn = dkn_ref[:, cs]
                dkg_acc = dkg_acc + jnp.sum(dkn * kh, axis=0, keepdims=True)
                dkh = dkn * kg
                dkv_ref[:, cs] = rk * (dkh - kh * jnp.mean(dkh * kh, axis=-1, keepdims=True))
            dkg_ref[...] = dkg_acc

    gspec = pl.BlockSpec((1, X_HEAD_DIM), lambda i: (0, 0))
    tile = pl.BlockSpec((tm, W), lambda i: (i, 0))
    kvspec = pl.BlockSpec((MEM_LEN, 2 * W), lambda i: (0, 0))
    grow = jax.ShapeDtypeStruct((1, X_HEAD_DIM), F32)
    return pl.pallas_call(
        body, grid=(nt,), in_specs=[tile, kvspec, gspec, gspec, tile, tile],
        out_specs=[tile, kvspec, gspec, gspec],
        out_shape=[jax.ShapeDtypeStruct((T, W), F32), jax.ShapeDtypeStruct((MEM_LEN, 2 * W), F32), grow, grow],
        scratch_shapes=[pltpu.VMEM((MEM_LEN, W), F32)],
        compiler_params=_cparams(1), name=name)(qraw, kvraw, q_gain, k_gain, do, o)


_ANY = pl.BlockSpec(memory_space=pl.ANY)


def _my_pos():
    return lax.axis_index("x"), lax.axis_index("y"), lax.axis_index("c")


def _pieces(arrays, chunks):
    out = []
    for a, (arr, n) in enumerate(zip(arrays, chunks)):
        rc = arr.shape[-2] // n
        out += [(a, pl.ds(ch * rc, rc)) for ch in range(n)]
    return out


class GatherBlocks:
    N_STAGES = 4

    def __init__(self, blks, chunks):
        self.shapes = [jax.ShapeDtypeStruct((N_DEV,) + b.shape, b.dtype) for b in blks]
        self.n = len(blks)
        self.pieces = _pieces(blks, chunks)
        n_p = len(self.pieces)
        self.sems = [pltpu.SemaphoreType.DMA((7 * n_p,)), pltpu.SemaphoreType.DMA((7 * n_p,)),
                     pltpu.SemaphoreType.DMA((n_p,))]

    def stage(self, which, x_refs, out_refs, sems):
        send_sems, recv_sems, local_sems = sems
        pieces, n_p = self.pieces, len(self.pieces)
        x, y, c = _my_pos()
        me, sibling = (x, y, c), (x, y, 1 - c)
        chips = [(1 - x, y), (x, 1 - y), (1 - x, 1 - y)]
        xn, yn, dg = [(*chip, c) for chip in chips]
        ps = range(n_p)

        def slot(block, p):
            px, py, pc = block
            a, rows = pieces[p]
            return out_refs[a].at[4 * px + 2 * py + pc, rows]

        def own(p):
            a, rows = pieces[p]
            return x_refs[a].at[rows]

        def copy(k, p, block, to, from_input=False):
            return pltpu.make_async_remote_copy(
                src_ref=own(p) if from_input else slot(block, p), dst_ref=slot(block, p),
                send_sem=send_sems.at[k * n_p + p], recv_sem=recv_sems.at[k * n_p + p],
                device_id=to, device_id_type=MESH)

        mine = [pltpu.make_async_copy(own(p), slot(me, p), local_sems.at[p]) for p in ps]
        first = [copy(k, p, me, to, from_input=True) for p in ps for k, to in ((1, xn), (2, yn), (0, sibling))]
        on_x = [copy(3, p, xn, yn) for p in ps if p % 2 == 0] + [copy(4, p, xn, sibling) for p in ps]
        on_y = [copy(3, p, yn, xn) for p in ps if p % 2 == 1] + [copy(5, p, yn, sibling) for p in ps]
        on_d = [copy(6, p, dg, sibling) for p in ps]
        if which == 0:
            for cp in first + mine:
                cp.start()
        elif which == 1:
            for p in ps:
                copy(1, p, xn, me).wait_recv()
                if p % 2 == 0:
                    copy(3, p, xn, yn).start()
                copy(4, p, xn, sibling).start()
                copy(2, p, yn, me).wait_recv()
                if p % 2 == 1:
                    copy(3, p, yn, xn).start()
                copy(5, p, yn, sibling).start()
        elif which == 2:
            for p in ps:
                copy(3, p, dg, me).wait_recv()
                copy(6, p, dg, sibling).start()
        else:
            for p in ps:
                copy(0, p, sibling, me).wait_recv()
            for k, chip in zip((4, 5, 6), chips):
                for p in ps:
                    copy(k, p, (*chip, 1 - c), me).wait_recv()
            for cp in first + on_x + on_y + on_d:
                cp.wait_send()
            for cp in mine:
                cp.wait()


def gather_blocks(blks, chunks, *, name):
    gb = GatherBlocks(blks, chunks)
    n = gb.n

    def body(*refs):
        x_refs, out_refs, sems = refs[:n], refs[n:2 * n], refs[2 * n:]
        for which in range(gb.N_STAGES):
            gb.stage(which, x_refs, out_refs, sems)

    return pl.pallas_call(body, out_shape=gb.shapes, in_specs=[_ANY] * n, out_specs=[_ANY] * n,
                          scratch_shapes=gb.sems, name=name)(*blks)


def gather_small(small, *, name):
    S, C = small.shape

    def body(s_ref, out_ref, send_sems, recv_sems, local_sem):
        x, y, c = _my_pos()
        my_id = 4 * x + 2 * y + c

        def copy(k, slot):
            px, py, pc = x ^ ((k >> 2) & 1), y ^ ((k >> 1) & 1), c ^ (k & 1)
            dst = my_id if slot == "mine" else 4 * px + 2 * py + pc
            return pltpu.make_async_remote_copy(
                src_ref=s_ref, dst_ref=out_ref.at[dst], send_sem=send_sems.at[k - 1], recv_sem=recv_sems.at[k - 1],
                device_id=(px, py, pc), device_id_type=MESH)

        own = pltpu.make_async_copy(s_ref, out_ref.at[my_id], local_sem)
        own.start()
        sends = [copy(k, "mine") for k in range(1, N_DEV)]
        for cp in sends:
            cp.start()
        for k in range(1, N_DEV):
            copy(k, "theirs").wait_recv()
        for cp in sends:
            cp.wait_send()
        own.wait()

    dma7 = pltpu.SemaphoreType.DMA((7,))
    return pl.pallas_call(
        body, out_shape=jax.ShapeDtypeStruct((N_DEV, S, C), small.dtype), in_specs=[_ANY], out_specs=_ANY,
        scratch_shapes=[dma7, dma7, pltpu.SemaphoreType.DMA], name=name)(small)


def pair_exchange(bigs, chunks, *, name):
    n = len(bigs)
    pieces = _pieces(bigs, chunks)
    n_p = len(pieces)

    def body(*refs):
        big_refs, out_refs = refs[:n], refs[n:2 * n]
        send_sems, recv_sems = refs[2 * n:]
        x, y, c = _my_pos()

        def copy(b, p):
            a, rows = pieces[p]
            return pltpu.make_async_remote_copy(
                src_ref=big_refs[a].at[2 * b + (1 - c), rows], dst_ref=out_refs[a].at[b, rows],
                send_sem=send_sems.at[b * n_p + p], recv_sem=recv_sems.at[b * n_p + p],
                device_id=(x, y, 1 - c), device_id_type=MESH)

        cps = [copy(b, p) for b in range(4) for p in range(n_p)]
        for cp in cps:
            cp.start()
        for cp in cps:
            cp.wait_recv()
        for cp in cps:
            cp.wait_send()

    return pl.pallas_call(
        body, out_shape=[jax.ShapeDtypeStruct((4,) + b.shape[1:], b.dtype) for b in bigs],
        in_specs=[_ANY] * n, out_specs=[_ANY] * n,
        scratch_shapes=[pltpu.SemaphoreType.DMA((4 * n_p,)), pltpu.SemaphoreType.DMA((4 * n_p,))],
        name=name)(*bigs)


def pair_sum(big, sib, c, *, tr, name):
    _, R, C = big.shape

    def body(c_ref, a_ref, s_ref, o_ref):
        o_ref[...] = (a_ref[...].astype(F32) + s_ref[...].astype(F32)).astype(o_ref.dtype)

    grid_spec = pltpu.PrefetchScalarGridSpec(
        num_scalar_prefetch=1, grid=(4, R // tr),
        in_specs=[pl.BlockSpec((None, tr, C), lambda b, i, c_ref: (2 * b + c_ref[0], i, 0)),
                  pl.BlockSpec((None, tr, C), lambda b, i, c_ref: (b, i, 0))],
        out_specs=pl.BlockSpec((None, tr, C), lambda b, i, c_ref: (b, i, 0)))
    return pl.pallas_call(body, grid_spec=grid_spec, out_shape=jax.ShapeDtypeStruct((4, R, C), big.dtype),
                          compiler_params=_cparams(2), name=name)(c.reshape(1).astype(jnp.int32), big, sib)


class ChipScatter:
    def __init__(self, pres, chunks):
        self.shapes = [jax.ShapeDtypeStruct(p.shape, p.dtype) for p in pres]
        self.n = len(pres)
        self.pieces = _pieces(pres, chunks)
        n_p = len(self.pieces)
        self.sems = [pltpu.SemaphoreType.DMA((3 * n_p,)), pltpu.SemaphoreType.DMA((3 * n_p,)),
                     pltpu.SemaphoreType.DMA((n_p,))]

    def _copies(self, pre_refs, out_refs, sems):
        send_sems, recv_sems, local_sems = sems
        n_p = len(self.pieces)
        x, y, c = _my_pos()
        my_chip = 2 * x + y
        chips = [(1 - x, y), (x, 1 - y), (1 - x, 1 - y)]

        def copy(j, p, slot):
            px, py = chips[j]
            a, rows = self.pieces[p]
            src_slot, dst_slot = (2 * px + py, my_chip) if slot == "mine" else (my_chip, 2 * px + py)
            return pltpu.make_async_remote_copy(
                src_ref=pre_refs[a].at[src_slot, rows], dst_ref=out_refs[a].at[dst_slot, rows],
                send_sem=send_sems.at[j * n_p + p], recv_sem=recv_sems.at[j * n_p + p],
                device_id=(px, py, c), device_id_type=MESH)

        own = [pltpu.make_async_copy(pre_refs[a].at[my_chip, rows], out_refs[a].at[my_chip, rows], local_sems.at[p])
               for p, (a, rows) in enumerate(self.pieces)]
        sends = [copy(j, p, "mine") for j in range(3) for p in range(n_p)]
        recvs = [copy(j, p, "theirs") for j in range(3) for p in range(n_p)]
        return own, sends, recvs

    def start(self, pre_refs, out_refs, sems):
        own, sends, _ = self._copies(pre_refs, out_refs, sems)
        for cp in sends + own:
            cp.start()

    def finish(self, pre_refs, out_refs, sems):
        own, sends, recvs = self._copies(pre_refs, out_refs, sems)
        for cp in recvs:
            cp.wait_recv()
        for cp in sends:
            cp.wait_send()
        for cp in own:
            cp.wait()


def chip_scatter(pres, chunks, *, name):
    cs = ChipScatter(pres, chunks)
    n = cs.n

    def body(*refs):
        pre_refs, out_refs, sems = refs[:n], refs[n:2 * n], refs[2 * n:]
        cs.start(pre_refs, out_refs, sems)
        cs.finish(pre_refs, out_refs, sems)

    return pl.pallas_call(body, out_shape=cs.shapes, in_specs=[_ANY] * n, out_specs=[_ANY] * n,
                          scratch_shapes=cs.sems, name=name)(*pres)


def sibling_send(blks, chunks, *, name):
    n = len(blks)
    pieces = _pieces(blks, chunks)
    n_p = len(pieces)

    def body(*refs):
        x_refs, out_refs = refs[:n], refs[n:2 * n]
        send_sems, recv_sems = refs[2 * n:]
        x, y, c = _my_pos()
        cps = [pltpu.make_async_remote_copy(
            src_ref=x_refs[a].at[rows], dst_ref=out_refs[a].at[rows], send_sem=send_sems.at[p],
            recv_sem=recv_sems.at[p], device_id=(x, y, 1 - c), device_id_type=MESH)
            for p, (a, rows) in enumerate(pieces)]
        for cp in cps:
            cp.start()
        for cp in cps:
            cp.wait_recv()
        for cp in cps:
            cp.wait_send()

    return pl.pallas_call(
        body, out_shape=[jax.ShapeDtypeStruct(b.shape, b.dtype) for b in blks],
        in_specs=[_ANY] * n, out_specs=[_ANY] * n,
        scratch_shapes=[pltpu.SemaphoreType.DMA((n_p,)), pltpu.SemaphoreType.DMA((n_p,))],
        name=name)(*blks)


def reduce_slots(land, *, tr, name):
    n, R, C = land.shape

    def body(l_ref, o_ref):
        acc = l_ref[0].astype(F32)
        for s in range(1, n):
            acc = acc + l_ref[s].astype(F32)
        o_ref[...] = acc

    return pl.pallas_call(
        body, grid=(R // tr,), in_specs=[pl.BlockSpec((n, tr, C), lambda i: (0, i, 0))],
        out_specs=pl.BlockSpec((tr, C), lambda i: (i, 0)), out_shape=jax.ShapeDtypeStruct((R, C), F32),
        compiler_params=_cparams(1), name=name)(land)


TM = 512


def _tk(d):
    return min(d.shape[0], 1024)


def ffn_fwd(x, g, wgu, wd, tag):
    gu, h = norm_matmul(x, g, wgu, tm=_tk(x), tn=1408, split=True, name=f"{tag}_gu")
    xo = mm_nn(gu, wd, res=x, scale=0.5, swiglu=True, tm=TM, tn=D_MODEL, tk=1408, name=f"{tag}_down")
    return xo, (x, gu, h)


def ffn_bwd(d, saved, g, wgu, wd, tag):
    x, gu, h = saved
    dgu, act = ffn_bwd_act(d, wd, gu, tm=TM, tn=1408, name=f"{tag}_bact")
    dwd = mm_tn(act, d, scale=0.5, a_split=False, b_split=False, tm=1408, tn=D_MODEL, tk=_tk(d), name=f"{tag}_dwd")
    dwgu = mm_tn(h, dgu, scale=1.0, a_split=False, b_split=True, tm=TM, tn=1408, tk=_tk(d), out_blocked=True,
                 name=f"{tag}_dwgu")
    dx, dg = mm_nt_normbwd(dgu, wgu, x, g, d, a_split=True, tm=_tk(d), tk=1408, name=f"{tag}_dx")
    return dx, dg, dwgu, dwd


def _tile2(v):
    return jnp.concatenate([v, v], axis=-1).reshape(1, LANES)


def _fold2(v):
    return v[:, :HEAD_DIM] + v[:, HEAD_DIM:]


EVEN = dict(dil=1, nsub=2, ppk=4, q_blk=0, k_blk=4, v_blk=5, n_heads=A_Q_HEADS, group=A_GROUP, max_dist=A_WINDOW - 1)
STICK = dict(q_blk=6, k_blk=10, v_blk=14, n_pairs=4)


def _odd_cfg(dil):
    return dict(dil=dil, nsub=4 if dil == 1 else 1, ppk=1, q_blk=0, k_blk=8, v_blk=16, n_heads=C_HEADS, group=1,
                max_dist=BLK)


def even_fwd(x, g, win, qg, kg, sinks, wout, tag, riders=None, rider_args=()):
    qkv, h = norm_matmul(x, g, win, tm=_tk(x), tn=1152, split=False, name=f"{tag}_in")
    qg2, kg2 = _tile2(qg), _tile2(kg)
    slopes = jnp.asarray(_alibi(A_Q_HEADS), F32)
    oa, lse = banded_fwd(qkv, qg2, kg2, slopes, sinks, name=f"{tag}_swa", **EVEN)
    ob, rode = stick_fwd(qkv, name=f"{tag}_stick", riders=riders, rider_args=rider_args, **STICK)
    o = jnp.concatenate([oa, ob], axis=1)
    xo = mm_nn(o, wout, res=x, scale=1.0, swiglu=False, tm=TM, tn=D_MODEL, tk=D_MODEL, name=f"{tag}_out")
    return xo, (x, qkv, h, oa, lse, o), rode


def even_bwd(d, saved, g, win, qg, kg, sinks, wout, tag, riders=None, rider_args=()):
    x, qkv, h, oa, lse, o = saved
    qg2, kg2 = _tile2(qg), _tile2(kg)
    slopes = jnp.asarray(_alibi(A_Q_HEADS), F32)
    dwout = mm_tn(o, d, scale=1.0, a_split=False, b_split=False, tm=D_MODEL, tn=D_MODEL, tk=_tk(d), name=f"{tag}_dwout")
    do = mm_nt(d, wout, tm=TM, tn=D_MODEL, tk=D_MODEL, name=f"{tag}_do")
    dqa, dka4, dva4, dqg, dkg, dsk = banded_bwd(qkv, qg2, kg2, slopes, sinks, do, oa, lse, None, None,
                                                do_blk=0, name=f"{tag}_swa_b", **EVEN)
    dqb, dkb, dvb, rode = stick_bwd(qkv, do, do_blk=4, name=f"{tag}_stick_b", riders=riders, rider_args=rider_args,
                                    **STICK)
    dqkv = assemble_even(dqa, dka4, dva4, dqb, dkb, dvb, tm=TM, name=f"{tag}_asm")
    dwin = mm_tn(h, dqkv, scale=1.0, a_split=False, b_split=False, tm=D_MODEL, tn=1152, tk=_tk(d), name=f"{tag}_dwin")
    dx, dg = mm_nt_normbwd(dqkv, win, x, g, d, a_split=False, tm=TM, tk=1152, name=f"{tag}_dx")
    return dx, dg, dwin, _fold2(dqg), _fold2(dkg), dsk[:, :A_Q_HEADS], dwout, rode


def odd_fwd(x, g, win, qg, kg, wout, tag):
    qkv, h = norm_matmul(x, g, win, tm=_tk(x), tn=768, split=False, name=f"{tag}_in")
    qg2, kg2 = _tile2(qg), _tile2(kg)
    outs = []
    for p, (window, dil) in enumerate(C_PATTERNS):
        slopes = jnp.asarray(_alibi(C_HEADS), F32) * float(dil)
        outs.append(banded_fwd(qkv, qg2, kg2, slopes, None, name=f"{tag}_dil{p}", **_odd_cfg(dil)))
    o, w1, w2, w3 = mix_fwd(outs[0][0], outs[1][0], outs[2][0], outs[0][1], outs[1][1], outs[2][1],
                            tm=TM, name=f"{tag}_mix")
    xo = mm_nn(o, wout, res=x, scale=1.0, swiglu=False, tm=TM, tn=D_MODEL, tk=D_MODEL, name=f"{tag}_out")
    return xo, (x, qkv, h, outs, (w1, w2, w3), o)


def odd_bwd(d, saved, g, win, qg, kg, wout, tag):
    x, qkv, h, outs, ws, o = saved
    qg2, kg2 = _tile2(qg), _tile2(kg)
    dwout = mm_tn(o, d, scale=1.0, a_split=False, b_split=False, tm=D_MODEL, tn=D_MODEL, tk=_tk(d), name=f"{tag}_dwout")
    do = mm_nt(d, wout, tm=TM, tn=D_MODEL, tk=D_MODEL, name=f"{tag}_do")
    parts, dqg, dkg = [], 0.0, 0.0
    for p, (window, dil) in enumerate(C_PATTERNS):
        slopes = jnp.asarray(_alibi(C_HEADS), F32) * float(dil)
        dq, dk, dv, dqg_p, dkg_p, _ = banded_bwd(qkv, qg2, kg2, slopes, None, do, outs[p][0], outs[p][1], ws[p], o,
                                                 do_blk=0, name=f"{tag}_dil{p}_b", **_odd_cfg(dil))
        parts.append((dq, dk, dv))
        dqg = dqg + dqg_p
        dkg = dkg + dkg_p
    dqkv = assemble_odd(parts, tm=TM, name=f"{tag}_asm")
    dwin = mm_tn(h, dqkv, scale=1.0, a_split=False, b_split=False, tm=TM, tn=768, tk=_tk(d), out_blocked=True,
                 name=f"{tag}_dwin")
    dx, dg = mm_nt_normbwd(dqkv, win, x, g, d, a_split=False, tm=TM, tk=768, name=f"{tag}_dx")
    return dx, dg, dwin, _fold2(dqg), _fold2(dkg), dwout


def xa_fwd(x, mem, g, gm, wq, wkv, qg, kg, wo, tag):
    qraw, h = norm_matmul(x, g, wq, tm=TM, tn=D_MODEL, split=False, name=f"{tag}_q")
    kvraw, hm = norm_matmul(mem, gm, wkv, tm=MEM_LEN, tn=512, split=False, name=f"{tag}_kv")
    o = xattn_fwd(qraw, kvraw, qg, kg, tm=TM, name=f"{tag}_att")
    xo = mm_nn(o, wo, res=x, scale=1.0, swiglu=False, tm=TM, tn=D_MODEL, tk=D_MODEL, name=f"{tag}_o")
    return xo, (x, qraw, h, kvraw, hm, o)


def xa_bwd(d, saved, mem, g, gm, wq, wkv, qg, kg, wo, tag):
    x, qraw, h, kvraw, hm, o = saved
    dwo = mm_tn(o, d, scale=1.0, a_split=False, b_split=False, tm=D_MODEL, tn=D_MODEL, tk=_tk(d), name=f"{tag}_dwo")
    do = mm_nt(d, wo, tm=TM, tn=D_MODEL, tk=D_MODEL, name=f"{tag}_do")
    dq, dkv, dqg, dkg = xattn_bwd(qraw, kvraw, qg, kg, do, o, tm=TM, name=f"{tag}_att_b")
    dwq = mm_tn(h, dq, scale=1.0, a_split=False, b_split=False, tm=D_MODEL, tn=D_MODEL, tk=_tk(d), name=f"{tag}_dwq")
    dx, dg = mm_nt_normbwd(dq, wq, x, g, d, a_split=False, tm=TM, tk=D_MODEL, name=f"{tag}_dx")
    dwkv = mm_tn(hm, dkv, scale=1.0, a_split=False, b_split=False, tm=TM, tn=512, tk=MEM_LEN, out_blocked=True,
                 name=f"{tag}_dwkv")
    _, dgm = mm_nt_normbwd(dkv, wkv, mem, gm, None, a_split=False, tm=MEM_LEN, tk=512, name=f"{tag}_dmem")
    return dx, dg, dgm, dwq, dwkv, dqg, dkg, dwo


MATS = (("ffn1_w_gu", 1), ("ffn1_w_down", 0), ("ev_w_in", 1), ("ev_w_out", 0), ("od_w_in", 1), ("od_w_out", 0),
        ("xa_w_q", 0), ("xa_w_kv", 1), ("xa_w_o", 0), ("ffn2_w_gu", 1), ("ffn2_w_down", 0))
SMALLS = ("ffn1_norm", "mix_norm", "ev_q_gain", "ev_k_gain", "ev_sinks", "od_q_gain", "od_k_gain", "xa_norm",
          "xa_mem_norm", "xa_q_gain", "xa_k_gain", "ffn2_norm")
WEIGHTS = ("ffn1_norm", "ffn1_w_gu", "ffn1_w_down", "mix_norm", "ev_w_in", "ev_q_gain", "ev_k_gain", "ev_sinks",
           "ev_w_out", "od_w_in", "od_q_gain", "od_k_gain", "od_w_out", "xa_norm", "xa_mem_norm", "xa_w_q",
           "xa_w_kv", "xa_q_gain", "xa_k_gain", "xa_w_o", "ffn2_norm", "ffn2_w_gu", "ffn2_w_down")
SMALL_ROWS = 16
LAYER_GROUPS = (
    (((("ffn1_w_gu", 0), ("ffn2_w_gu", 0)), 4, 512),
     ((("ffn1_w_down", 0), ("ffn2_w_down", 0)), 2, 352),
     ((("ev_w_out", 0), ("xa_w_q", 0), ("xa_w_o", 0)), 1, 384),
     ((("xa_w_kv", 0),), 1, 512),
     ((("ev_w_in", 0),), 1, 512)),
    (((("ffn1_w_gu", 1), ("ffn2_w_gu", 1)), 4, 512),
     ((("ffn1_w_down", 1), ("ffn2_w_down", 1)), 2, 352),
     ((("od_w_out", 0), ("xa_w_q", 1), ("xa_w_o", 1)), 1, 384),
     ((("xa_w_kv", 1),), 1, 512),
     ((("od_w_in", 0),), 1, 512)),
)
GROUPS = LAYER_GROUPS[0] + LAYER_GROUPS[1]


def _chunks_of(groups):
    return tuple(g[1] for g in groups)
COL_SHARDED = {name for name, axis in MATS if axis == 1}
BLOCKED = {"ffn1_w_gu", "ffn2_w_gu", "xa_w_kv", "od_w_in"}


def group_halves(shards, c, groups):
    out = []
    for members, _, _ in groups:
        halves = []
        for name, layer in members:
            _, r, cc = shards[name].shape
            half = lax.dynamic_index_in_dim(shards[name][layer].reshape(2, r // 2, cc), c, 0, keepdims=False)
            halves.append(half.astype(BF16))
        out.append(jnp.concatenate(halves, axis=0))
    return out


def full_weights(gathered, shards, groups):
    full = {}
    for (members, _, _), arr in zip(groups, gathered):
        for w, (name, layer) in enumerate(members):
            _, r, cc = shards[name].shape
            piece = arr[:, w * (r // 2):(w + 1) * (r // 2)].reshape(4, r, cc)
            if name not in COL_SHARDED:
                piece = piece.reshape(4 * r, cc)
            elif name not in BLOCKED:
                piece = piece.transpose(1, 0, 2).reshape(r, 4 * cc)
            full[(name, layer)] = piece
    return full


def group_grads(grads, shards, groups):
    out = []
    for members, _, _ in groups:
        parts = []
        for name, layer in members:
            _, r, cc = shards[name].shape
            gfull = grads[(name, layer)]
            if name in COL_SHARDED and name not in BLOCKED:
                gfull = gfull.reshape(2, r // 2, 4, cc).transpose(2, 0, 1, 3)
            parts.append(gfull.reshape(N_DEV, r // 2, cc))
        out.append(jnp.concatenate(parts, axis=1))
    return out


def shard_grads(mine, theirs, c, shards, groups):
    per = {}
    for (members, _, _), a, b in zip(groups, mine, theirs):
        for w, (name, layer) in enumerate(members):
            _, r, cc = shards[name].shape
            rows = slice(w * (r // 2), (w + 1) * (r // 2))
            lo = jnp.where(c == 0, a[rows], b[rows])
            hi = jnp.where(c == 0, b[rows], a[rows])
            per[(name, layer)] = jnp.concatenate([lo, hi], axis=0)
    return per


def pack_small(vals):
    row10 = jnp.concatenate([vals["xa_q_gain"].reshape(1, 512), vals["xa_k_gain"].reshape(1, 512)], axis=1)
    row11 = jnp.concatenate([vals["ev_q_gain"], vals["ev_k_gain"], vals["od_q_gain"], vals["od_k_gain"],
                             vals["ev_sinks"], jnp.zeros((1, 1024 - 4 * 64 - 8), F32)], axis=1)
    return jnp.concatenate([vals["ffn1_norm"], vals["mix_norm"], vals["xa_norm"], vals["xa_mem_norm"],
                            vals["ffn2_norm"], row10, row11, jnp.zeros((SMALL_ROWS - 12, 1024), F32)], axis=0)


def unpack_small(arr):
    return {"ffn1_norm": arr[0:2], "mix_norm": arr[2:4], "xa_norm": arr[4:6], "xa_mem_norm": arr[6:8],
            "ffn2_norm": arr[8:10],
            "xa_q_gain": arr[10:11, 0:512].reshape(2, 256), "xa_k_gain": arr[10:11, 512:1024].reshape(2, 256),
            "ev_q_gain": arr[11:12, 0:64], "ev_k_gain": arr[11:12, 64:128], "od_q_gain": arr[11:12, 128:192],
            "od_k_gain": arr[11:12, 192:256], "ev_sinks": arr[11:12, 256:264]}


def local_step(x, mem, target, W, small, prereduce, later):
    depth = small["ffn1_norm"].shape[0]

    def row(name, l):
        return small[name][l:l + 1]

    saved = []
    for l in range(depth):
        j = l // 2
        x, s1 = ffn_fwd(x, row("ffn1_norm", l), W[("ffn1_w_gu", l)], W[("ffn1_w_down", l)], f"l{l}_f1")
        if l % 2 == 0:
            riders, rider_args = None, ()
            if later is not None and l == 0:
                riders, rider_args = GatherBlocks(later[0], later[1]), later[0]
            x, s2, rode = even_fwd(x, row("mix_norm", l), W[("ev_w_in", j)], row("ev_q_gain", j),
                                   row("ev_k_gain", j), small["ev_sinks"][j], W[("ev_w_out", j)], f"l{l}_ev",
                                   riders=riders, rider_args=rider_args)
            if riders is not None:
                W = {**W, **later[2](rode)}
        else:
            x, s2 = odd_fwd(x, row("mix_norm", l), W[("od_w_in", j)], row("od_q_gain", j), row("od_k_gain", j),
                            W[("od_w_out", j)], f"l{l}_od")
        x, s3 = xa_fwd(x, mem, row("xa_norm", l), row("xa_mem_norm", l), W[("xa_w_q", l)], W[("xa_w_kv", l)],
                       row("xa_q_gain", l), row("xa_k_gain", l), W[("xa_w_o", l)], f"l{l}_xa")
        x, s4 = ffn_fwd(x, row("ffn2_norm", l), W[("ffn2_w_gu", l)], W[("ffn2_w_down", l)], f"l{l}_f2")
        saved.append((s1, s2, s3, s4))
    loss, d = loss_kernel(x, target, tm=TM, name="loss")

    gw = {}
    gs = {name: [None] * small[name].shape[0] for name in SMALLS}
    pending, landed = None, {}
    for l in reversed(range(depth)):
        j = l // 2
        s1, s2, s3, s4 = saved[l]
        d, dg, dwgu, dwd = ffn_bwd(d, s4, row("ffn2_norm", l), W[("ffn2_w_gu", l)], W[("ffn2_w_down", l)], f"l{l}_f2")
        gs["ffn2_norm"][l] = dg
        gw[("ffn2_w_gu", l)], gw[("ffn2_w_down", l)] = dwgu, dwd
        d, dg, dgm, dwq, dwkv, dqg, dkg, dwo = xa_bwd(
            d, s3, mem, row("xa_norm", l), row("xa_mem_norm", l), W[("xa_w_q", l)], W[("xa_w_kv", l)],
            row("xa_q_gain", l), row("xa_k_gain", l), W[("xa_w_o", l)], f"l{l}_xa")
        gs["xa_norm"][l], gs["xa_mem_norm"][l], gs["xa_q_gain"][l], gs["xa_k_gain"][l] = dg, dgm, dqg, dkg
        gw[("xa_w_q", l)], gw[("xa_w_kv", l)], gw[("xa_w_o", l)] = dwq, dwkv, dwo
        if l % 2 == 0:
            riders, rider_args = None, ()
            if pending is not None:
                riders, rider_args = ChipScatter(pending[1], _chunks_of(LAYER_GROUPS[pending[0]])), pending[1]
            d, dg, dwin, dqg, dkg, dsk, dwout, rode = even_bwd(
                d, s2, row("mix_norm", l), W[("ev_w_in", j)], row("ev_q_gain", j), row("ev_k_gain", j),
                small["ev_sinks"][j], W[("ev_w_out", j)], f"l{l}_ev", riders=riders, rider_args=rider_args)
            if pending is not None:
                landed[pending[0]], pending = rode, None
            gs["ev_q_gain"][j], gs["ev_k_gain"][j], gs["ev_sinks"][j] = dqg, dkg, dsk
            gw[("ev_w_in", j)], gw[("ev_w_out", j)] = dwin, dwout
        else:
            d, dg, dwin, dqg, dkg, dwout = odd_bwd(
                d, s2, row("mix_norm", l), W[("od_w_in", j)], row("od_q_gain", j), row("od_k_gain", j),
                W[("od_w_out", j)], f"l{l}_od")
            gs["od_q_gain"][j], gs["od_k_gain"][j] = dqg, dkg
            gw[("od_w_in", j)], gw[("od_w_out", j)] = dwin, dwout
        gs["mix_norm"][l] = dg
        d, dg, dwgu, dwd = ffn_bwd(d, s1, row("ffn1_norm", l), W[("ffn1_w_gu", l)], W[("ffn1_w_down", l)], f"l{l}_f1")
        gs["ffn1_norm"][l] = dg
        gw[("ffn1_w_gu", l)], gw[("ffn1_w_down", l)] = dwgu, dwd
        if pending is not None:
            landed[pending[0]] = chip_scatter(pending[1], _chunks_of(LAYER_GROUPS[pending[0]]),
                                              name=f"scatter_grads{pending[0]}")
        pending = (l, prereduce(gw, l))
    landed[pending[0]] = chip_scatter(pending[1], _chunks_of(LAYER_GROUPS[pending[0]]), name=f"scatter_grads{pending[0]}")
    gsmall = {name: jnp.concatenate(v, axis=0) for name, v in gs.items()}
    return loss, d, landed, gsmall


def kernel(x, mem, ffn1_norm, ffn1_w_gu, ffn1_w_down, mix_norm, ev_w_in, ev_q_gain, ev_k_gain, ev_sinks, ev_w_out, od_w_in, od_q_gain, od_k_gain, od_w_out, xa_norm, xa_mem_norm, xa_w_q, xa_w_kv, xa_q_gain, xa_k_gain, xa_w_o, ffn2_norm, ffn2_w_gu, ffn2_w_down, loss_target, m_ffn1_norm, m_ffn1_w_gu, m_ffn1_w_down, m_mix_norm, m_ev_w_in, m_ev_q_gain, m_ev_k_gain, m_ev_sinks, m_ev_w_out, m_od_w_in, m_od_q_gain, m_od_k_gain, m_od_w_out, m_xa_norm, m_xa_mem_norm, m_xa_w_q, m_xa_w_kv, m_xa_q_gain, m_xa_k_gain, m_xa_w_o, m_ffn2_norm, m_ffn2_w_gu, m_ffn2_w_down, v_ffn1_norm, v_ffn1_w_gu, v_ffn1_w_down, v_mix_norm, v_ev_w_in, v_ev_q_gain, v_ev_k_gain, v_ev_sinks, v_ev_w_out, v_od_w_in, v_od_q_gain, v_od_k_gain, v_od_w_out, v_xa_norm, v_xa_mem_norm, v_xa_w_q, v_xa_w_kv, v_xa_q_gain, v_xa_k_gain, v_xa_w_o, v_ffn2_norm, v_ffn2_w_gu, v_ffn2_w_down):
    given = dict(locals())
    w = {n: given[n] for n in WEIGHTS}
    m = {n: given["m_" + n] for n in WEIGHTS}
    v = {n: given["v_" + n] for n in WEIGHTS}
    c = lax.axis_index("c")
    shards = {name: w[name] for name, _ in MATS}
    small = {n: w[n] for n in SMALLS}

    groups0, groups1 = LAYER_GROUPS
    gathered = gather_blocks(group_halves(shards, c, groups0), _chunks_of(groups0), name="gather_weights0")
    full = full_weights(gathered, shards, groups0)
    later = (group_halves(shards, c, groups1), _chunks_of(groups1), lambda got: full_weights(got, shards, groups1))

    def prereduce(gw, layer):
        groups = LAYER_GROUPS[layer]
        packed = group_grads(gw, shards, groups)
        sib = pair_exchange(packed, _chunks_of(groups), name=f"pair_grads{layer}")
        return [pair_sum(p, s, c, tr=g[2], name=f"pair_sum{layer}_{i}")
                for i, (g, p, s) in enumerate(zip(groups, packed, sib))]

    loss_b, grad_x, landed, gsmall = local_step(x[0], mem[0], loss_target[0], full, small, prereduce, later)

    per = {}
    for layer, land in sorted(landed.items()):
        groups = LAYER_GROUPS[layer]
        mine = [reduce_slots(a, tr=g[2], name=f"sum_grads{layer}_{i}") for i, (g, a) in enumerate(zip(groups, land))]
        theirs = sibling_send(mine, _chunks_of(groups), name=f"swap_grads{layer}")
        per.update(shard_grads(mine, theirs, c, shards, groups))
    g = {name: jnp.stack([per[(name, layer)] for layer in range(w[name].shape[0])], axis=0) for name, _ in MATS}
    land_small = gather_small(pack_small(gsmall), name="gather_small")
    g_small = unpack_small(reduce_slots(land_small, tr=SMALL_ROWS, name="sum_small"))
    g.update(g_small)

    delta, new_m, new_v = {}, {}, {}
    for name, _ in MATS:
        shp = w[name].shape
        flat = [a.reshape(-1, shp[-1]) for a in (w[name], g[name], m[name], v[name])]
        dl, nm, nv = adamw(*flat, br=BLK, name=f"adamw_{name}")
        delta[name], new_m[name], new_v[name] = dl.reshape(shp), nm.reshape(shp), nv.reshape(shp)
    dl, nm, nv = adamw(pack_small(small), pack_small(g_small), pack_small({n: m[n] for n in SMALLS}),
                       pack_small({n: v[n] for n in SMALLS}), br=SMALL_ROWS, name="adamw_small")
    for dst, arr in ((delta, dl), (new_m, nm), (new_v, nv)):
        dst.update(unpack_small(arr))

    loss = lax.psum(loss_b[0, 0], ("x", "y", "c"))
    return (loss, grad_x[None], *[g[n] for n in WEIGHTS], *[delta[n] for n in WEIGHTS],
            *[new_m[n] for n in WEIGHTS], *[new_v[n] for n in WEIGHTS])
```

```python
import jax
import jax.numpy as jnp
from jax import lax
from jax.experimental import pallas as pl
from jax.experimental.pallas import tpu as pltpu

F32 = jnp.float32
BF16 = jnp.bfloat16

D_MODEL = 1024
HEAD_DIM = 64
LANES = 128
BLK = 128
D_FF = 2816
RMS_EPS = 1e-6
MEM_LEN = 256
X_HEADS = 4
X_HEAD_DIM = 256
A_Q_HEADS = 8
A_GROUP = 4
A_WINDOW = 128
C_HEADS = 16
C_PATTERNS = ((128, 1), (512, 4), (2048, 16))
NEG = -1e30
VMEM_LIMIT = 56 * 2 ** 20

ADAM_LR = 0.001
ADAM_B1 = 0.9
ADAM_B2 = 0.999
ADAM_EPS = 1e-08
ADAM_WD = 0.01
ADAM_STEP = 10

N_DEV = 8
MESH = pl.DeviceIdType.MESH


def _cparams(n):
    return pltpu.CompilerParams(dimension_semantics=("arbitrary",) * n, vmem_limit_bytes=VMEM_LIMIT)


def _dot(a, b):
    return jnp.dot(a, b, preferred_element_type=F32)


def _dot_nt(a, b):
    return lax.dot_general(a, b, (((1,), (1,)), ((), ())), preferred_element_type=F32)


def _dot_tn(a, b):
    return lax.dot_general(a, b, (((0,), (0,)), ((), ())), preferred_element_type=F32)


def _sigmoid(z):
    return 1.0 / (1.0 + jnp.exp(-z))


def norm_matmul(x, g, w, *, tm, tn, split, name):
    T, K = x.shape
    blocked = w.ndim == 3
    assert not blocked or w.shape[2] == tn
    N = w.shape[0] * w.shape[2] if blocked else w.shape[1]
    nj = N // tn

    def body(x_ref, g_ref, w_ref, o_ref, h_ref):
        @pl.when(pl.program_id(1) == 0)
        def _():
            xv = x_ref[...]
            r = lax.rsqrt(jnp.mean(xv * xv, axis=-1, keepdims=True) + RMS_EPS)
            h_ref[...] = (xv * r * g_ref[...]).astype(BF16)

        o_ref[...] = _dot(h_ref[...], w_ref[...]).astype(o_ref.dtype)

    if split:
        njh = nj // 2
        o_shape = jax.ShapeDtypeStruct((2, T, N // 2), BF16)
        o_spec = pl.BlockSpec((None, tm, tn), lambda i, j: (j // njh, i, j % njh))
    else:
        o_shape = jax.ShapeDtypeStruct((T, N), F32)
        o_spec = pl.BlockSpec((tm, tn), lambda i, j: (i, j))
    return pl.pallas_call(
        body, grid=(T // tm, nj),
        in_specs=[pl.BlockSpec((tm, K), lambda i, j: (i, 0)),
                  pl.BlockSpec((1, K), lambda i, j: (0, 0)),
                  (pl.BlockSpec((None, K, tn), lambda i, j: (j, 0, 0)) if blocked
                   else pl.BlockSpec((K, tn), lambda i, j: (0, j)))],
        out_specs=[o_spec, pl.BlockSpec((tm, K), lambda i, j: (i, 0))],
        out_shape=[o_shape, jax.ShapeDtypeStruct((T, K), BF16)],
        compiler_params=_cparams(2), name=name)(x, g, w)


def mm_nn(a, b, *, res, scale, swiglu, tm, tn, tk, name):
    T = a.shape[-2]
    K, N = b.shape
    nk = K // tk

    def body(*refs):
        if swiglu:
            g_ref, u_ref, b_ref, r_ref, o_ref, acc = refs
        else:
            a_ref, b_ref, r_ref, o_ref, acc = refs
        k = pl.program_id(2)

        @pl.when(k == 0)
        def _():
            acc[...] = jnp.zeros_like(acc)

        if swiglu:
            gv = g_ref[...].astype(F32)
            av = (gv * _sigmoid(gv) * u_ref[...].astype(F32)).astype(BF16)
        else:
            av = a_ref[...].astype(BF16)
        acc[...] += _dot(av, b_ref[...])

        @pl.when(k == nk - 1)
        def _():
            o_ref[...] = r_ref[...] + scale * acc[...]

    if swiglu:
        a_specs = [pl.BlockSpec((None, tm, tk), lambda i, j, k: (0, i, k)),
                   pl.BlockSpec((None, tm, tk), lambda i, j, k: (1, i, k))]
        a_args = [a, a]
    else:
        a_specs = [pl.BlockSpec((tm, tk), lambda i, j, k: (i, k))]
        a_args = [a]
    return pl.pallas_call(
        body, grid=(T // tm, N // tn, nk),
        in_specs=a_specs + [pl.BlockSpec((tk, tn), lambda i, j, k: (k, j)),
                            pl.BlockSpec((tm, tn), lambda i, j, k: (i, j))],
        out_specs=pl.BlockSpec((tm, tn), lambda i, j, k: (i, j)),
        out_shape=jax.ShapeDtypeStruct((T, N), F32),
        scratch_shapes=[pltpu.VMEM((tm, tn), F32)],
        compiler_params=_cparams(3), name=name)(*a_args, b, res)


def mm_nt(a, b, *, tm, tn, tk, name):
    T, K = a.shape
    N = b.shape[0]
    nk = K // tk

    def body(a_ref, b_ref, o_ref, acc):
        k = pl.program_id(2)

        @pl.when(k == 0)
        def _():
            acc[...] = jnp.zeros_like(acc)

        acc[...] += _dot_nt(a_ref[...].astype(BF16), b_ref[...])

        @pl.when(k == nk - 1)
        def _():
            o_ref[...] = acc[...]

    return pl.pallas_call(
        body, grid=(T // tm, N // tn, nk),
        in_specs=[pl.BlockSpec((tm, tk), lambda i, j, k: (i, k)),
                  pl.BlockSpec((tn, tk), lambda i, j, k: (j, k))],
        out_specs=pl.BlockSpec((tm, tn), lambda i, j, k: (i, j)),
        out_shape=jax.ShapeDtypeStruct((T, N), F32),
        scratch_shapes=[pltpu.VMEM((tm, tn), F32)],
        compiler_params=_cparams(3), name=name)(a, b)


def ffn_bwd_act(d, wd, gu, *, tm, tn, name, riding=None):
    T, K = d.shape
    Fd = wd.shape[0]

    def body(d_ref, w_ref, g_ref, u_ref, dgu_ref, act_ref):
        da = 0.5 * _dot_nt(d_ref[...].astype(BF16), w_ref[...])
        gv = g_ref[...].astype(F32)
        uv = u_ref[...].astype(F32)
        s = _sigmoid(gv)
        silu = gv * s
        act_ref[...] = (silu * uv).astype(BF16)
        dgu_ref[0] = (da * uv * (s * (1.0 + gv * (1.0 - s)))).astype(BF16)
        dgu_ref[1] = (da * silu).astype(BF16)

    in_specs = [pl.BlockSpec((tm, K), lambda j, i: (i, 0)),
                pl.BlockSpec((tn, K), lambda j, i: (j, 0)),
                pl.BlockSpec((None, tm, tn), lambda j, i: (0, i, j)),
                pl.BlockSpec((None, tm, tn), lambda j, i: (1, i, j))]
    out_specs = [pl.BlockSpec((2, tm, tn), lambda j, i: (0, i, j)), pl.BlockSpec((tm, tn), lambda j, i: (i, j))]
    out_shape = [jax.ShapeDtypeStruct((2, T, Fd), BF16), jax.ShapeDtypeStruct((T, Fd), BF16)]
    return _call_with_riders(body, riding, (Fd // tn, T // tm), in_specs, out_specs, out_shape, [],
                             [d, wd, gu, gu], name)


def _call_with_riders(body, riding, grid, in_specs, out_specs, out_shape, scratch, args, name):
    n_out = len(out_shape)
    if riding is None:
        return pl.pallas_call(body, grid=grid, in_specs=in_specs, out_specs=out_specs, out_shape=out_shape,
                              scratch_shapes=scratch, compiler_params=_cparams(len(grid)), name=name)(*args)

    def is_first():
        ok = pl.program_id(0) == 0
        for ax in range(1, len(grid)):
            ok = ok & (pl.program_id(ax) == 0)
        return ok

    def is_last():
        ok = pl.program_id(0) == grid[0] - 1
        for ax in range(1, len(grid)):
            ok = ok & (pl.program_id(ax) == grid[ax] - 1)
        return ok

    outs = pl.pallas_call(
        riding.wrap(body, len(in_specs), n_out, len(scratch), is_first, is_last), grid=grid,
        in_specs=list(in_specs) + riding.in_specs, out_specs=list(out_specs) + riding.out_specs,
        out_shape=list(out_shape) + riding.out_shapes, scratch_shapes=list(scratch) + riding.scratch,
        compiler_params=_cparams(len(grid)), name=name)(*args, *riding.args)
    core, per = riding.split(outs, n_out)
    return (*core, *per)


def mm_nt_normbwd(a, b, x, g, res, *, a_split, tm, tk, name, riding=None):
    T, Dm = x.shape
    blocked = b.ndim == 3
    assert not blocked or b.shape[2] == tk
    K = b.shape[0] * b.shape[2] if blocked else b.shape[1]
    nk = K // tk
    nkh = nk // 2
    has_res = res is not None

    def body(*refs):
        if has_res:
            a_ref, b_ref, x_ref, g_ref, r_ref, dx_ref, dg_ref, acc = refs
        else:
            a_ref, b_ref, x_ref, g_ref, dx_ref, dg_ref, acc = refs
        i = pl.program_id(0)
        k = pl.program_id(1)

        @pl.when(k == 0)
        def _():
            acc[...] = jnp.zeros_like(acc)

        acc[...] += _dot_nt(a_ref[...].astype(BF16), b_ref[...])

        @pl.when(k == nk - 1)
        def _():
            xv = x_ref[...]
            r = lax.rsqrt(jnp.mean(xv * xv, axis=-1, keepdims=True) + RMS_EPS)
            xh = xv * r
            dh = acc[...]
            dxh = dh * g_ref[...]
            dx = r * (dxh - xh * jnp.mean(dxh * xh, axis=-1, keepdims=True))
            if has_res:
                dx = dx + r_ref[...]
            dx_ref[...] = dx
            part = jnp.sum(dh * xh, axis=0, keepdims=True)

            @pl.when(i == 0)
            def _():
                dg_ref[...] = part

            @pl.when(i > 0)
            def _():
                dg_ref[...] += part

    if a_split:
        a_spec = pl.BlockSpec((None, tm, tk), lambda i, k: (k // nkh, i, k % nkh))
    else:
        a_spec = pl.BlockSpec((tm, tk), lambda i, k: (i, k))
    in_specs = [a_spec,
                (pl.BlockSpec((None, Dm, tk), lambda i, k: (k, 0, 0)) if blocked
                 else pl.BlockSpec((Dm, tk), lambda i, k: (0, k))),
                pl.BlockSpec((tm, Dm), lambda i, k: (i, 0)),
                pl.BlockSpec((1, Dm), lambda i, k: (0, 0))]
    args = [a, b, x, g]
    if has_res:
        in_specs.append(pl.BlockSpec((tm, Dm), lambda i, k: (i, 0)))
        args.append(res)
    out_specs = [pl.BlockSpec((tm, Dm), lambda i, k: (i, 0)), pl.BlockSpec((1, Dm), lambda i, k: (0, 0))]
    out_shape = [jax.ShapeDtypeStruct((T, Dm), F32), jax.ShapeDtypeStruct((1, Dm), F32)]
    scratch = [pltpu.VMEM((tm, Dm), F32)]
    return _call_with_riders(body, riding, (T // tm, nk), in_specs, out_specs, out_shape, scratch, args, name)


def mm_tn(a, b, *, scale, a_split, b_split, tm, tn, tk, name, out_blocked=False, riding=None):
    T = a.shape[-2]
    M = a.shape[-1] * (2 if a_split else 1)
    N = b.shape[-1] * (2 if b_split else 1)
    ni, nj, nk = M // tm, N // tn, T // tk
    nih, njh = ni // 2, nj // 2

    def body(a_ref, b_ref, o_ref, acc):
        k = pl.program_id(2)

        @pl.when(k == 0)
        def _():
            acc[...] = jnp.zeros_like(acc)

        acc[...] += _dot_tn(a_ref[...].astype(BF16), b_ref[...].astype(BF16))

        @pl.when(k == nk - 1)
        def _():
            o_ref[...] = (acc[...] * scale).astype(o_ref.dtype)

    if a_split:
        a_spec = pl.BlockSpec((None, tk, tm), lambda i, j, k: (i // nih, k, i % nih))
    else:
        a_spec = pl.BlockSpec((tk, tm), lambda i, j, k: (k, i))
    if b_split:
        b_spec = pl.BlockSpec((None, tk, tn), lambda i, j, k: (j // njh, k, j % njh))
    else:
        b_spec = pl.BlockSpec((tk, tn), lambda i, j, k: (k, j))
    if out_blocked:
        o_spec = pl.BlockSpec((None, None, tm, tn), lambda i, j, k: (j, i, 0, 0))
        o_shape = jax.ShapeDtypeStruct((nj, ni, tm, tn), BF16)
    else:
        o_spec = pl.BlockSpec((tm, tn), lambda i, j, k: (i, j))
        o_shape = jax.ShapeDtypeStruct((M, N), BF16)
    outs = _call_with_riders(body, riding, (ni, nj, nk), [a_spec, b_spec], [o_spec], [o_shape],
                             [pltpu.VMEM((tm, tn), F32)], [a, b], name)
    return outs[0] if riding is None else tuple(outs)


def loss_kernel(y, target, *, tm, name):
    T, Dm = y.shape

    def body(y_ref, t_ref, l_ref, dy_ref):
        e = y_ref[...] - t_ref[...]
        dy_ref[...] = e * (1.0 / Dm)
        part = (0.5 / Dm) * jnp.sum(jnp.sum(e * e, axis=-1, keepdims=True), axis=0, keepdims=True)
        part = jnp.broadcast_to(part, (8, LANES))

        @pl.when(pl.program_id(0) == 0)
        def _():
            l_ref[...] = part

        @pl.when(pl.program_id(0) > 0)
        def _():
            l_ref[...] += part

    return pl.pallas_call(
        body, grid=(T // tm,),
        in_specs=[pl.BlockSpec((tm, Dm), lambda i: (i, 0)), pl.BlockSpec((tm, Dm), lambda i: (i, 0))],
        out_specs=[pl.BlockSpec((8, LANES), lambda i: (0, 0)), pl.BlockSpec((tm, Dm), lambda i: (i, 0))],
        out_shape=[jax.ShapeDtypeStruct((8, LANES), F32), jax.ShapeDtypeStruct((T, Dm), F32)],
        compiler_params=_cparams(1), name=name)(y, target)


def adamw(w, g, m, v, *, br, name):
    R, C = w.shape

    def body(w_ref, g_ref, m_ref, v_ref, d_ref, nm_ref, nv_ref):
        gv = g_ref[...]
        nm = ADAM_B1 * m_ref[...] + (1.0 - ADAM_B1) * gv
        nv = ADAM_B2 * v_ref[...] + (1.0 - ADAM_B2) * (gv * gv)
        m_hat = nm / (1.0 - ADAM_B1 ** ADAM_STEP)
        v_hat = nv / (1.0 - ADAM_B2 ** ADAM_STEP)
        d_ref[...] = -ADAM_LR * (m_hat / (jnp.sqrt(v_hat) + ADAM_EPS) + ADAM_WD * w_ref[...])
        nm_ref[...] = nm
        nv_ref[...] = nv

    spec = pl.BlockSpec((br, C), lambda i: (i, 0))
    shp = jax.ShapeDtypeStruct((R, C), F32)
    return pl.pallas_call(
        body, grid=(R // br,), in_specs=[spec] * 4, out_specs=[spec] * 3, out_shape=[shp] * 3,
        compiler_params=_cparams(1), name=name)(w, g, m, v)


def _lane0():
    return lax.broadcasted_iota(jnp.int32, (1, LANES), 1) < HEAD_DIM


def _half_sum(x, m0):
    s0 = jnp.sum(jnp.where(m0, x, 0.0), axis=-1, keepdims=True)
    s1 = jnp.sum(jnp.where(m0, 0.0, x), axis=-1, keepdims=True)
    return jnp.where(m0, s0, s1)


def _half_pick(x, m0, e):
    sel = m0 if e == 0 else jnp.logical_not(m0)
    return jnp.max(jnp.where(sel, x, NEG), axis=-1, keepdims=True)


def _head_rms(x, m0):
    return lax.rsqrt(_half_sum(x * x, m0) * (1.0 / HEAD_DIM) + RMS_EPS)


def _alibi(n):
    return [float(2.0 ** (-8.0 * (h + 1) / n)) for h in range(n)]


def _mask_half(x, m0, e):
    return jnp.where(m0, x, 0.0) if e == 0 else jnp.where(m0, 0.0, x)


def _band_masks2(max_dist, has_prev, live):
    row = lax.broadcasted_iota(jnp.int32, (2 * BLK, 2 * BLK), 0)
    col = lax.broadcasted_iota(jnp.int32, (2 * BLK, 2 * BLK), 1)
    dist = (row & (BLK - 1)) - col + BLK
    lim = jnp.where(live, max_dist, -1)
    first = jnp.where(has_prev, 0, BLK)
    valid = (dist >= 0) & (dist <= lim) & (col >= first)
    top = lax.broadcasted_iota(jnp.int32, (2 * BLK, 1), 0) < BLK
    return dist.astype(F32), valid, top


def _stack_heads(x, m0, kes):
    parts = []
    for e in range(2):
        h = _mask_half(x, m0, e)
        parts.append(pltpu.roll(h, HEAD_DIM, 1) if kes[e] != e else h)
    return jnp.concatenate(parts, axis=0)


def _unstack_heads(y, m0, kes):
    parts = []
    for e in range(2):
        h = y[e * BLK:(e + 1) * BLK]
        parts.append(pltpu.roll(h, HEAD_DIM, 1) if kes[e] != e else h)
    return jnp.where(m0, parts[0], parts[1])


def _rows(r, dil):
    return pl.ds(r, BLK, stride=dil) if dil > 1 else pl.ds(0, BLK)


def _band_units(dil, nsub):
    assert dil == 1 or nsub == 1
    if nsub == 1:
        return [(_rows(r, dil), ("prev", _rows(r, dil)), 0) for r in range(dil)]
    units = [(pl.ds(0, BLK), ("prev", pl.ds(0, BLK)), 0)]
    units += [(pl.ds(BLK * s, BLK), ("cur", pl.ds(BLK * (s - 1), BLK)), s) for s in range(1, nsub)]
    return units


def _band_specs(dil, nsub, ppk, q_blk, k_blk, v_blk, kv_shared, nb):
    RB = BLK * dil * nsub
    PB = BLK if nsub > 1 else RB
    qw = LANES * ppk
    kw = LANES if kv_shared else qw

    def cur(i):
        return jnp.minimum(i, nb - 1)

    def prev(i):
        return jnp.maximum(i * nsub - 1, 0) if nsub > 1 else jnp.maximum(i - 1, 0)

    def kidx(base):
        return (lambda p, i: (cur(i), base)) if kv_shared else (lambda p, i: (cur(i), base + p))

    def pidx(base):
        return (lambda p, i: (prev(i), base)) if kv_shared else (lambda p, i: (prev(i), base + p))

    return [pl.BlockSpec((RB, qw), lambda p, i: (cur(i), q_blk + p)),
            pl.BlockSpec((RB, kw), kidx(k_blk)), pl.BlockSpec((PB, kw), pidx(k_blk)),
            pl.BlockSpec((RB, kw), kidx(v_blk)), pl.BlockSpec((PB, kw), pidx(v_blk))]


def banded_fwd(qkv, q_gain2, k_gain2, slopes, sinks, *, dil, nsub, ppk, q_blk, k_blk, v_blk, n_heads, group,
               max_dist, name):
    T = qkv.shape[0]
    RB = BLK * dil * nsub
    nb = T // RB
    npair = n_heads // 2
    kv_shared = group > 1
    scale = HEAD_DIM ** -0.5
    has_sink = sinks is not None

    def body(*refs):
        slope_ref = refs[0]
        if has_sink:
            sink_ref, refs = refs[1], refs[2:]
        else:
            refs = refs[1:]
        q_ref, kc_ref, kp_ref, vc_ref, vp_ref, qg_ref, kg_ref, o_ref, l_ref = refs
        pb = pl.program_id(0)
        i = pl.program_id(1)
        m0 = _lane0()
        distf, valid_first, top = _band_masks2(max_dist, i > 0, i >= 0)
        valid_inner = _band_masks2(max_dist, i >= 0, i >= 0)[1] if nsub > 1 else None
        qg = qg_ref[...]
        kg = kg_ref[...]
        for rows, (src, prows), sub in _band_units(dil, nsub):
            valid = valid_first if sub == 0 else valid_inner
            kpr, vpr = (kp_ref, vp_ref) if src == "prev" else (kc_ref, vc_ref)
            kcache = {}
            for jp in range(ppk):
                cs = pl.ds(LANES * jp, LANES)
                jk = 0 if kv_shared else jp
                if jk not in kcache:
                    ks = pl.ds(LANES * jk, LANES)
                    kcat = jnp.concatenate([kpr[prows, ks], kc_ref[rows, ks]], axis=0)
                    vcat = jnp.concatenate([vpr[prows, ks], vc_ref[rows, ks]], axis=0)
                    kcache[jk] = ((kcat * _head_rms(kcat, m0) * kg).astype(BF16), vcat.astype(BF16))
                kn, vcat = kcache[jk]
                qv = q_ref[rows, cs]
                qn = qv * _head_rms(qv, m0) * qg
                kes = [((2 * jp + e) // group) % 2 if kv_shared else e for e in range(2)]
                hidx = 2 * (pb * ppk + jp)
                qs = _stack_heads(qn, m0, kes).astype(BF16)
                slope = jnp.where(top, slope_ref[hidx], slope_ref[hidx + 1])
                s = jnp.where(valid, _dot_nt(qs, kn) * scale - slope * distf, NEG)
                m = jnp.max(s, axis=-1, keepdims=True)
                if has_sink:
                    sk = jnp.where(top, sink_ref[hidx], sink_ref[hidx + 1])
                    m = jnp.maximum(m, sk)
                p = jnp.exp(s - m)
                den = jnp.sum(p, axis=-1, keepdims=True)
                if has_sink:
                    den = den + jnp.exp(sk - m)
                o_full = _dot((p * (1.0 / den)).astype(BF16), vcat)
                o_ref[rows, cs] = _unstack_heads(o_full, m0, kes)
                l_ref[rows, cs] = _unstack_heads(jnp.broadcast_to(m + jnp.log(den), (2 * BLK, LANES)), m0, [0, 1])

    smem = pl.BlockSpec(memory_space=pltpu.SMEM)
    qw = LANES * ppk
    gspec = pl.BlockSpec((1, LANES), lambda p, i: (0, 0))
    ospec = pl.BlockSpec((RB, qw), lambda p, i: (i, p))
    oshape = jax.ShapeDtypeStruct((T, n_heads * HEAD_DIM), F32)
    args = [slopes] + ([sinks] if has_sink else []) + [qkv] * 5 + [q_gain2, k_gain2]
    return pl.pallas_call(
        body, grid=(npair // ppk, nb),
        in_specs=[smem] * (2 if has_sink else 1) + _band_specs(dil, nsub, ppk, q_blk, k_blk, v_blk, kv_shared, nb)
        + [gspec, gspec],
        out_specs=[ospec, ospec], out_shape=[oshape, oshape],
        compiler_params=_cparams(2), name=name)(*args)


def banded_bwd(qkv, q_gain2, k_gain2, slopes, sinks, do, o, lse, w, omix, *, dil, nsub, ppk, q_blk, k_blk, v_blk,
               n_heads, group, max_dist, do_blk, name):
    T = qkv.shape[0]
    RB = BLK * dil * nsub
    nb = T // RB
    npair = n_heads // 2
    kv_shared = group > 1
    scale = HEAD_DIM ** -0.5
    has_sink = sinks is not None
    mixed = w is not None
    qw = LANES * ppk

    def body(*refs):
        slope_ref = refs[0]
        if has_sink:
            sink_ref, refs = refs[1], refs[2:]
        else:
            refs = refs[1:]
        q_ref, kc_ref, kp_ref, vc_ref, vp_ref, qg_ref, kg_ref, do_ref, o_ref, l_ref = refs[:10]
        refs = refs[10:]
        if mixed:
            w_ref, om_ref, refs = refs[0], refs[1], refs[2:]
        dq_ref, dk_ref, dv_ref, dqg_ref, dkg_ref, dsk_ref, ck_ref, cv_ref = refs
        pb = pl.program_id(0)
        i = pl.program_id(1)
        live = i < nb
        m0 = _lane0()
        lane = lax.broadcasted_iota(jnp.int32, (1, LANES), 1)
        distf, valid_first, top = _band_masks2(max_dist, i > 0, live)
        valid_inner = _band_masks2(max_dist, i >= 0, live)[1] if nsub > 1 else None
        livef = live.astype(F32)
        qg = qg_ref[...]
        kg = kg_ref[...]

        def stack_rows(x2):
            return jnp.concatenate([_half_pick(x2, m0, 0), _half_pick(x2, m0, 1)], axis=0)

        @pl.when((pb == 0) & (i == 0))
        def _():
            dqg_ref[...] = jnp.zeros_like(dqg_ref)
            dkg_ref[...] = jnp.zeros_like(dkg_ref)
            dsk_ref[...] = jnp.zeros_like(dsk_ref)

        @pl.when(i == 0)
        def _():
            ck_ref[...] = jnp.zeros_like(ck_ref)
            cv_ref[...] = jnp.zeros_like(cv_ref)

        dqg_acc = jnp.zeros((1, LANES), F32)
        dkg_acc = jnp.zeros((1, LANES), F32)
        dsk_acc = jnp.zeros((1, LANES), F32)
        if nsub > 1:
            dk_ref[...] = ck_ref[...]
            dv_ref[...] = cv_ref[...]
        for rows, (src, prows), sub in _band_units(dil, nsub):
            valid = valid_first if sub == 0 else valid_inner
            kpr, vpr = (kp_ref, vp_ref) if src == "prev" else (kc_ref, vc_ref)
            for jp in range(ppk):
                cs = pl.ds(LANES * jp, LANES)
                ks = pl.ds(0, LANES) if kv_shared else cs
                kcat = jnp.concatenate([kpr[prows, ks], kc_ref[rows, ks]], axis=0)
                vcat = jnp.concatenate([vpr[prows, ks], vc_ref[rows, ks]], axis=0).astype(BF16)
                rk = _head_rms(kcat, m0)
                kh = kcat * rk
                kn = (kh * kg).astype(BF16)
                qv = q_ref[rows, cs]
                rq = _head_rms(qv, m0)
                qh = qv * rq
                dov = do_ref[rows, cs]
                lv = l_ref[rows, cs]
                if mixed:
                    wv = w_ref[rows, cs]
                    dmix = _half_sum(dov * om_ref[rows, cs], m0)
                    dov = dov * wv
                delta2 = _half_sum(dov * o_ref[rows, cs], m0)
                shift = stack_rows(wv * dmix if mixed else delta2)
                kes = [((2 * jp + e) // group) % 2 if kv_shared else e for e in range(2)]
                hidx = 2 * (pb * ppk + jp)
                qs = _stack_heads(qh * qg, m0, kes).astype(BF16)
                dos = _stack_heads(dov, m0, kes).astype(BF16)
                lse = stack_rows(lv)
                slope = jnp.where(top, slope_ref[hidx], slope_ref[hidx + 1])
                p = jnp.where(valid, jnp.exp(_dot_nt(qs, kn) * scale - slope * distf - lse), 0.0)
                ds = (p * (_dot_nt(dos, vcat) - shift)).astype(BF16)
                dqn = _unstack_heads(_dot(ds, kn), m0, kes) * scale
                dkn = _dot_tn(ds, qs) * scale
                dvv = _dot_tn(p.astype(BF16), dos)
                if has_sink:
                    sk = jnp.where(top, sink_ref[hidx], sink_ref[hidx + 1])
                    contrib = -jnp.exp(sk - lse) * stack_rows(delta2) * livef
                    for e in range(2):
                        tot = jnp.sum(contrib[e * BLK:(e + 1) * BLK], axis=0, keepdims=True)
                        dsk_acc = dsk_acc + jnp.where(lane == (2 * jp + e), tot, 0.0)
                dqg_acc = dqg_acc + jnp.sum(dqn * qh, axis=0, keepdims=True)
                dqh = dqn * qg
                dq_raw = rq * (dqh - qh * (_half_sum(dqh * qh, m0) * (1.0 / HEAD_DIM)))
                dkg_acc = dkg_acc + jnp.sum(dkn * kh, axis=0, keepdims=True)
                dkh = dkn * kg
                dk_raw = rk * (dkh - kh * (_half_sum(dkh * kh, m0) * (1.0 / HEAD_DIM)))

                @pl.when(live)
                def _():
                    dq_ref[rows, cs] = dq_raw

                if nsub == 1:
                    dk_ref[rows, cs] = ck_ref[rows, cs] + dk_raw[:BLK]
                    dv_ref[rows, cs] = cv_ref[rows, cs] + dvv[:BLK]
                elif sub == 0:
                    last = pl.ds(RB - BLK, BLK)
                    dk_ref[last, cs] += dk_raw[:BLK]
                    dv_ref[last, cs] += dvv[:BLK]
                else:
                    ck_ref[prows, cs] += dk_raw[:BLK]
                    cv_ref[prows, cs] += dvv[:BLK]
                ck_ref[rows, cs] = dk_raw[BLK:]
                cv_ref[rows, cs] = dvv[BLK:]
        dqg_ref[...] += dqg_acc
        dkg_ref[...] += dkg_acc
        dsk_ref[...] += dsk_acc

    smem = pl.BlockSpec(memory_space=pltpu.SMEM)
    gspec = pl.BlockSpec((1, LANES), lambda p, i: (0, 0))

    def cur(i):
        return jnp.minimum(i, nb - 1)

    qspec = pl.BlockSpec((RB, qw), lambda p, i: (cur(i), p))
    dospec = pl.BlockSpec((RB, qw), lambda p, i: (cur(i), do_blk + p))
    kvout = pl.BlockSpec((RB, qw), lambda p, i: (jnp.maximum(i - 1, 0), p))
    in_specs = ([smem] * (2 if has_sink else 1) + _band_specs(dil, nsub, ppk, q_blk, k_blk, v_blk, kv_shared, nb)
                + [gspec, gspec, dospec, qspec, qspec] + ([qspec, qspec] if mixed else []))
    args = ([slopes] + ([sinks] if has_sink else []) + [qkv] * 5 + [q_gain2, k_gain2, do, o, lse]
            + ([w, omix] if mixed else []))
    full = jax.ShapeDtypeStruct((T, n_heads * HEAD_DIM), F32)
    row = jax.ShapeDtypeStruct((1, LANES), F32)
    return pl.pallas_call(
        body, grid=(npair // ppk, nb + 1), in_specs=in_specs,
        out_specs=[qspec, kvout, kvout, gspec, gspec, gspec],
        out_shape=[full, full, full, row, row, row],
        scratch_shapes=[pltpu.VMEM((RB, qw), F32), pltpu.VMEM((RB, qw), F32)],
        compiler_params=_cparams(2), name=name)(*args)


def mix_fwd(o1, o2, o3, l1, l2, l3, *, tm, name):
    T, C = o1.shape

    def body(o1r, o2r, o3r, l1r, l2r, l3r, o_ref, w1r, w2r, w3r):
        a, b, c = l1r[...], l2r[...], l3r[...]
        m = jnp.maximum(jnp.maximum(a, b), c)
        ea, eb, ec = jnp.exp(a - m), jnp.exp(b - m), jnp.exp(c - m)
        inv = 1.0 / (ea + eb + ec)
        wa, wb, wc = ea * inv, eb * inv, ec * inv
        o_ref[...] = wa * o1r[...] + wb * o2r[...] + wc * o3r[...]
        w1r[...] = wa
        w2r[...] = wb
        w3r[...] = wc

    spec = pl.BlockSpec((tm, C), lambda i: (i, 0))
    shp = jax.ShapeDtypeStruct((T, C), F32)
    return pl.pallas_call(body, grid=(T // tm,), in_specs=[spec] * 6, out_specs=[spec] * 4, out_shape=[shp] * 4,
                          compiler_params=_cparams(1), name=name)(o1, o2, o3, l1, l2, l3)


def assemble_odd(parts, *, tm, name):
    T, C = parts[0][0].shape

    def body(*refs):
        o_ref = refs[9]
        for j in range(3):
            o_ref[:, pl.ds(C * j, C)] = refs[j][...] + refs[3 + j][...] + refs[6 + j][...]

    spec = pl.BlockSpec((tm, C), lambda i: (i, 0))
    flat = [parts[p][j] for p in range(3) for j in range(3)]
    return pl.pallas_call(body, grid=(T // tm,), in_specs=[spec] * 9,
                          out_specs=pl.BlockSpec((tm, 3 * C), lambda i: (i, 0)),
                          out_shape=jax.ShapeDtypeStruct((T, 3 * C), F32),
                          compiler_params=_cparams(1), name=name)(*flat)


def assemble_even(dqa, dka4, dva4, dqb, dkb, dvb, *, tm, name):
    T = dqa.shape[0]
    W = 512

    def body(dqa_r, dka_r, dva_r, dqb_r, dkb_r, dvb_r, o_ref):
        o_ref[:, pl.ds(0, W)] = dqa_r[...]
        ka = dka_r[...]
        va = dva_r[...]
        o_ref[:, pl.ds(512, LANES)] = ka[:, 0:128] + ka[:, 128:256] + ka[:, 256:384] + ka[:, 384:512]
        o_ref[:, pl.ds(640, LANES)] = va[:, 0:128] + va[:, 128:256] + va[:, 256:384] + va[:, 384:512]
        o_ref[:, pl.ds(768, W)] = dqb_r[...]
        o_ref[:, pl.ds(1280, W)] = dkb_r[...]
        o_ref[:, pl.ds(1792, W)] = dvb_r[...]

    spec = pl.BlockSpec((tm, W), lambda i: (i, 0))
    return pl.pallas_call(body, grid=(T // tm,), in_specs=[spec] * 6,
                          out_specs=pl.BlockSpec((tm, 2304), lambda i: (i, 0)),
                          out_shape=jax.ShapeDtypeStruct((T, 2304), F32),
                          compiler_params=_cparams(1), name=name)(dqa, dka4, dva4, dqb, dkb, dvb)


STICK_T = 256
STICK_DEAD = -110.0


def _split_bf16(x):
    hi = x.astype(BF16)
    lo = (x - hi.astype(F32)).astype(BF16)
    return hi, lo


def _stick_logits(qm, kt, scale, diag):
    n = STICK_T
    row = lax.broadcasted_iota(jnp.int32, (n, n), 0)
    col = lax.broadcasted_iota(jnp.int32, (n, n), 1)
    mask = col < row + jnp.where(diag, 0, n)
    z = _dot_nt(qm, kt) * scale
    lneg = -(jnp.maximum(z, 0.0) + jnp.log(1.0 + jnp.exp(-jnp.abs(z))))
    lpos = z + lneg
    lk = jnp.where(mask, lneg, 0.0)
    return mask, lpos, lneg, lk


def _cumsum_mm(x, tri):
    hi, lo = _split_bf16(x)
    return _dot(hi, tri) + _dot(lo, tri)


def stick_fwd(qkv, *, q_blk, k_blk, v_blk, n_pairs, name, riders=None, rider_args=()):
    T = qkv.shape[0]
    n = STICK_T
    nq = T // n
    scale = HEAD_DIM ** -0.5
    nc = riders.n if riders is not None else 0
    n_steps = n_pairs * nq
    stage_at = (0, (3 * n_steps) // 4, n_steps - 1, n_steps - 1)

    def body(*refs):
        q_ref, k_ref, v_ref = refs[:3]
        x_refs, o_ref = refs[3:3 + nc], refs[3 + nc]
        out_refs, sems = refs[4 + nc:4 + 2 * nc], refs[4 + 2 * nc:]
        i = pl.program_id(1)
        step_id = pl.program_id(0) * nq + i

        def ride(which):
            if riders is not None:
                @pl.when(step_id == stage_at[which])
                def _():
                    riders.stage(which, x_refs, out_refs, sems)

        ride(0)
        ride(1)
        m0 = _lane0()
        r2 = lax.broadcasted_iota(jnp.int32, (n, n), 0)
        c2 = lax.broadcasted_iota(jnp.int32, (n, n), 1)
        tri_after = (r2 > c2).astype(BF16)
        qv = q_ref[...]
        out = jnp.zeros((n, LANES), F32)
        for e in range(2):
            qm = _mask_half(qv, m0, e).astype(BF16)

            def alive(st):
                t, _, carry = st
                return (t <= i) & (jnp.max(carry) > STICK_DEAD)

            def step(st, e=e, qm=qm):
                t, acc, carry = st
                start = pl.multiple_of((i - t) * n, n)
                kt = k_ref[pl.ds(start, n), :].astype(BF16)
                vt = _mask_half(v_ref[pl.ds(start, n), :], m0, e).astype(BF16)
                mask, lpos, _, lk = _stick_logits(qm, kt, scale, t == 0)
                after = _cumsum_mm(lk, tri_after) + carry
                a = jnp.where(mask, jnp.exp(lpos + after), 0.0)
                acc = acc + _dot(a.astype(BF16), vt)
                carry = carry + jnp.sum(lk, axis=-1, keepdims=True)
                return t + 1, acc, carry

            _, acc, _ = lax.while_loop(alive, step, (jnp.int32(0), jnp.zeros((n, LANES), F32),
                                                     jnp.zeros((n, 1), F32)))
            out = out + acc
        o_ref[...] = out
        ride(2)
        ride(3)

    outs = pl.pallas_call(
        body, grid=(n_pairs, nq),
        in_specs=[pl.BlockSpec((n, LANES), lambda p, i: (i, q_blk + p)),
                  pl.BlockSpec((T, LANES), lambda p, i: (0, k_blk + p)),
                  pl.BlockSpec((T, LANES), lambda p, i: (0, v_blk + p))] + [_ANY] * nc,
        out_specs=[pl.BlockSpec((n, LANES), lambda p, i: (i, p))] + [_ANY] * nc,
        out_shape=[jax.ShapeDtypeStruct((T, n_pairs * LANES), F32)] + (riders.shapes if nc else []),
        scratch_shapes=riders.sems if nc else [],
        compiler_params=_cparams(2), name=name)(qkv, qkv, qkv, *rider_args)
    return outs[0], list(outs[1:])


def stick_bwd(qkv, do, *, q_blk, k_blk, v_blk, do_blk, n_pairs, name, riders=None, rider_args=()):
    T = qkv.shape[0]
    n = STICK_T
    nq = T // n
    scale = HEAD_DIM ** -0.5
    nc = riders.n if riders is not None else 0

    def body(*refs):
        q_ref, k_ref, v_ref, do_ref = refs[:4]
        pre_refs = refs[4:4 + nc]
        dq_ref, dk_ref, dv_ref = refs[4 + nc:7 + nc]
        land_refs = refs[7 + nc:7 + 2 * nc]
        a_keep, g_keep, s_keep = refs[7 + 2 * nc:10 + 2 * nc]
        sems = refs[10 + 2 * nc:]
        i = pl.program_id(1)
        first_step = (pl.program_id(0) == 0) & (i == 0)
        last_step = (pl.program_id(0) == n_pairs - 1) & (i == nq - 1)
        m0 = _lane0()
        r2 = lax.broadcasted_iota(jnp.int32, (n, n), 0)
        c2 = lax.broadcasted_iota(jnp.int32, (n, n), 1)
        tri_after = (r2 > c2).astype(BF16)
        tri_from = (r2 >= c2).astype(BF16)

        if riders is not None:
            @pl.when(first_step)
            def _():
                riders.start(pre_refs, land_refs, sems)

        @pl.when(i == 0)
        def _():
            dk_ref[...] = jnp.zeros_like(dk_ref)
            dv_ref[...] = jnp.zeros_like(dv_ref)

        qv = q_ref[...]
        dov = do_ref[...]
        dq_out = jnp.zeros((n, LANES), F32)
        for e in range(2):
            qm = _mask_half(qv, m0, e).astype(BF16)
            dom = _mask_half(dov, m0, e).astype(BF16)

            def alive(st):
                t, carry, _ = st
                return (t <= i) & (jnp.max(carry) > STICK_DEAD)

            def scan(st, qm=qm, dom=dom):
                t, carry, gtot = st
                start = pl.multiple_of((i - t) * n, n)
                kt = k_ref[pl.ds(start, n), :].astype(BF16)
                vt = v_ref[pl.ds(start, n), :].astype(BF16)
                mask, lpos, lneg, lk = _stick_logits(qm, kt, scale, t == 0)
                a = jnp.where(mask, jnp.exp(lpos + _cumsum_mm(lk, tri_after) + carry), 0.0)
                g = _dot_nt(dom, vt) * a
                a_keep[t] = a.astype(BF16)
                g_keep[t] = g
                s_keep[t] = jnp.exp(lneg).astype(BF16)
                return (t + 1, carry + jnp.sum(lk, axis=-1, keepdims=True),
                        gtot + jnp.sum(g, axis=-1, keepdims=True))

            z1 = jnp.zeros((n, 1), F32)
            n_live, _, gtot = lax.while_loop(alive, scan, (jnp.int32(0), z1, z1))

            def step(t, st, e=e, qm=qm, dom=dom, gtot=gtot):
                dq_acc, gright = st
                start = pl.multiple_of((i - t) * n, n)
                g = g_keep[t]
                sneg = s_keep[t].astype(F32)
                before = gtot - (_cumsum_mm(g, tri_from) + gright)
                mask = c2 < r2 + jnp.where(t == 0, 0, n)
                dz = jnp.where(mask, g * sneg - before * (1.0 - sneg), 0.0) * scale
                dzb = dz.astype(BF16)
                dq_acc = dq_acc + _dot(dzb, _mask_half(k_ref[pl.ds(start, n), :], m0, e).astype(BF16))
                dk_ref[pl.ds(start, n), :] += _dot_tn(dzb, qm)
                dv_ref[pl.ds(start, n), :] += _dot_tn(a_keep[t], dom)
                return dq_acc, gright + jnp.sum(g, axis=-1, keepdims=True)

            dq_acc, _ = lax.fori_loop(0, n_live, step, (jnp.zeros((n, LANES), F32), z1))
            dq_out = dq_out + dq_acc
        dq_ref[...] = dq_out

        if riders is not None:
            @pl.when(last_step)
            def _():
                riders.finish(pre_refs, land_refs, sems)

    tile = pl.BlockSpec((n, LANES), lambda p, i: (i, p))
    whole = pl.BlockSpec((T, LANES), lambda p, i: (0, p))
    shp = jax.ShapeDtypeStruct((T, n_pairs * LANES), F32)
    outs = pl.pallas_call(
        body, grid=(n_pairs, nq),
        in_specs=[pl.BlockSpec((n, LANES), lambda p, i: (i, q_blk + p)),
                  pl.BlockSpec((T, LANES), lambda p, i: (0, k_blk + p)),
                  pl.BlockSpec((T, LANES), lambda p, i: (0, v_blk + p)),
                  pl.BlockSpec((n, LANES), lambda p, i: (i, do_blk + p))] + [_ANY] * nc,
        out_specs=[tile, whole, whole] + [_ANY] * nc,
        out_shape=[shp, shp, shp] + (riders.shapes if nc else []),
        scratch_shapes=[pltpu.VMEM((nq, n, n), BF16), pltpu.VMEM((nq, n, n), F32), pltpu.VMEM((nq, n, n), BF16)]
        + (riders.sems if nc else []),
        compiler_params=_cparams(2), name=name)(qkv, qkv, qkv, do, *rider_args)
    return outs[0], outs[1], outs[2], list(outs[3:])


def _xnorm(x):
    r = lax.rsqrt(jnp.mean(x * x, axis=-1, keepdims=True) + RMS_EPS)
    return r, x * r


def xattn_fwd(qraw, kvraw, q_gain, k_gain, *, tm, name):
    T = qraw.shape[0]
    scale = X_HEAD_DIM ** -0.5
    W = X_HEADS * X_HEAD_DIM

    def body(q_ref, kv_ref, qg_ref, kg_ref, o_ref):
        for h in range(X_HEADS):
            cs = pl.ds(X_HEAD_DIM * h, X_HEAD_DIM)
            _, qh = _xnorm(q_ref[:, cs])
            _, kh = _xnorm(kv_ref[:, cs])
            qn = (qh * qg_ref[...]).astype(BF16)
            kn = (kh * kg_ref[...]).astype(BF16)
            v = kv_ref[:, pl.ds(W + X_HEAD_DIM * h, X_HEAD_DIM)].astype(BF16)
            s = _dot_nt(qn, kn) * scale
            m = jnp.max(s, axis=-1, keepdims=True)
            p = jnp.exp(s - m)
            p = p / jnp.sum(p, axis=-1, keepdims=True)
            o_ref[:, cs] = _dot(p.astype(BF16), v)

    gspec = pl.BlockSpec((1, X_HEAD_DIM), lambda i: (0, 0))
    return pl.pallas_call(
        body, grid=(T // tm,),
        in_specs=[pl.BlockSpec((tm, W), lambda i: (i, 0)), pl.BlockSpec((MEM_LEN, 2 * W), lambda i: (0, 0)),
                  gspec, gspec],
        out_specs=pl.BlockSpec((tm, W), lambda i: (i, 0)),
        out_shape=jax.ShapeDtypeStruct((T, W), F32),
        compiler_params=_cparams(1), name=name)(qraw, kvraw, q_gain, k_gain)


def xattn_bwd(qraw, kvraw, q_gain, k_gain, do, o, *, tm, name):
    T = qraw.shape[0]
    nt = T // tm
    scale = X_HEAD_DIM ** -0.5
    W = X_HEADS * X_HEAD_DIM

    def body(q_ref, kv_ref, qg_ref, kg_ref, do_ref, o_ref, dq_ref, dkv_ref, dqg_ref, dkg_ref, dkn_ref):
        i = pl.program_id(0)

        @pl.when(i == 0)
        def _():
            dkv_ref[...] = jnp.zeros_like(dkv_ref)
            dkn_ref[...] = jnp.zeros_like(dkn_ref)
            dqg_ref[...] = jnp.zeros_like(dqg_ref)
            dkg_ref[...] = jnp.zeros_like(dkg_ref)

        qg = qg_ref[...]
        kg = kg_ref[...]
        dqg_acc = jnp.zeros((1, X_HEAD_DIM), F32)
        for h in range(X_HEADS):
            cs = pl.ds(X_HEAD_DIM * h, X_HEAD_DIM)
            vs = pl.ds(W + X_HEAD_DIM * h, X_HEAD_DIM)
            rq, qh = _xnorm(q_ref[:, cs])
            _, kh = _xnorm(kv_ref[:, cs])
            qn = (qh * qg).astype(BF16)
            kn = (kh * kg).astype(BF16)
            v = kv_ref[:, vs].astype(BF16)
            s = _dot_nt(qn, kn) * scale
            m = jnp.max(s, axis=-1, keepdims=True)
            p = jnp.exp(s - m)
            p = p / jnp.sum(p, axis=-1, keepdims=True)
            dov = do_ref[:, cs]
            delta = jnp.sum(dov * o_ref[:, cs], axis=-1, keepdims=True)
            dob = dov.astype(BF16)
            ds = (p * (_dot_nt(dob, v) - delta)).astype(BF16)
            dqn = _dot(ds, kn) * scale
            dkn_ref[:, cs] += _dot_tn(ds, qn) * scale
            dkv_ref[:, vs] += _dot_tn(p.astype(BF16), dob)
            dqg_acc = dqg_acc + jnp.sum(dqn * qh, axis=0, keepdims=True)
            dqh = dqn * qg
            dq_ref[:, cs] = rq * (dqh - qh * jnp.mean(dqh * qh, axis=-1, keepdims=True))
        dqg_ref[...] += dqg_acc

        @pl.when(i == nt - 1)
        def _():
            dkg_acc = jnp.zeros((1, X_HEAD_DIM), F32)
            for h in range(X_HEADS):
                cs = pl.ds(X_HEAD_DIM * h, X_HEAD_DIM)
                rk, kh = _xnorm(kv_ref[:, cs])
                dkn = dkn_ref[:, cs]
                dkg_acc = dkg_acc + jnp.sum(dkn * kh, axis=0, keepdims=True)
                dkh = dkn * kg
                dkv_ref[:, cs] = rk * (dkh - kh * jnp.mean(dkh * kh, axis=-1, keepdims=True))
            dkg_ref[...] = dkg_acc

    gspec = pl.BlockSpec((1, X_HEAD_DIM), lambda i: (0, 0))
    tile = pl.BlockSpec((tm, W), lambda i: (i, 0))
    kvspec = pl.BlockSpec((MEM_LEN, 2 * W), lambda i: (0, 0))
    grow = jax.ShapeDtypeStruct((1, X_HEAD_DIM), F32)
    return pl.pallas_call(
        body, grid=(nt,), in_specs=[tile, kvspec, gspec, gspec, tile, tile],
        out_specs=[tile, kvspec, gspec, gspec],
        out_shape=[jax.ShapeDtypeStruct((T, W), F32), jax.ShapeDtypeStruct((MEM_LEN, 2 * W), F32), grow, grow],
        scratch_shapes=[pltpu.VMEM((MEM_LEN, W), F32)],
        compiler_params=_cparams(1), name=name)(qraw, kvraw, q_gain, k_gain, do, o)


_ANY = pl.BlockSpec(memory_space=pl.ANY)


def _my_pos():
    return lax.axis_index("x"), lax.axis_index("y"), lax.axis_index("c")


def _pieces(arrays, chunks):
    out = []
    for a, (arr, n) in enumerate(zip(arrays, chunks)):
        rc = arr.shape[-2] // n
        out += [(a, pl.ds(ch * rc, rc)) for ch in range(n)]
    return out


class GatherBlocks:
    N_STAGES = 4

    def __init__(self, blks, chunks):
        self.shapes = [jax.ShapeDtypeStruct((N_DEV,) + b.shape, b.dtype) for b in blks]
        self.n = len(blks)
        self.pieces = _pieces(blks, chunks)
        n_p = len(self.pieces)
        self.sems = [pltpu.SemaphoreType.DMA((7 * n_p,)), pltpu.SemaphoreType.DMA((7 * n_p,)),
                     pltpu.SemaphoreType.DMA((n_p,))]

    def stage(self, which, x_refs, out_refs, sems):
        send_sems, recv_sems, local_sems = sems
        pieces, n_p = self.pieces, len(self.pieces)
        x, y, c = _my_pos()
        me, sibling = (x, y, c), (x, y, 1 - c)
        chips = [(1 - x, y), (x, 1 - y), (1 - x, 1 - y)]
        xn, yn, dg = [(*chip, c) for chip in chips]
        ps = range(n_p)

        def slot(block, p):
            px, py, pc = block
            a, rows = pieces[p]
            return out_refs[a].at[4 * px + 2 * py + pc, rows]

        def own(p):
            a, rows = pieces[p]
            return x_refs[a].at[rows]

        def copy(k, p, block, to, from_input=False):
            return pltpu.make_async_remote_copy(
                src_ref=own(p) if from_input else slot(block, p), dst_ref=slot(block, p),
                send_sem=send_sems.at[k * n_p + p], recv_sem=recv_sems.at[k * n_p + p],
                device_id=to, device_id_type=MESH)

        mine = [pltpu.make_async_copy(own(p), slot(me, p), local_sems.at[p]) for p in ps]
        first = [copy(k, p, me, to, from_input=True) for p in ps for k, to in ((1, xn), (2, yn), (0, sibling))]
        on_x = [copy(3, p, xn, yn) for p in ps if p % 2 == 0] + [copy(4, p, xn, sibling) for p in ps]
        on_y = [copy(3, p, yn, xn) for p in ps if p % 2 == 1] + [copy(5, p, yn, sibling) for p in ps]
        on_d = [copy(6, p, dg, sibling) for p in ps]
        if which == 0:
            for cp in first + mine:
                cp.start()
        elif which == 1:
            for p in ps:
                copy(1, p, xn, me).wait_recv()
                if p % 2 == 0:
                    copy(3, p, xn, yn).start()
                copy(4, p, xn, sibling).start()
                copy(2, p, yn, me).wait_recv()
                if p % 2 == 1:
                    copy(3, p, yn, xn).start()
                copy(5, p, yn, sibling).start()
        elif which == 2:
            for p in ps:
                copy(3, p, dg, me).wait_recv()
                copy(6, p, dg, sibling).start()
        else:
            for p in ps:
                copy(0, p, sibling, me).wait_recv()
            for k, chip in zip((4, 5, 6), chips):
                for p in ps:
                    copy(k, p, (*chip, 1 - c), me).wait_recv()
            for cp in first + on_x + on_y + on_d:
                cp.wait_send()
            for cp in mine:
                cp.wait()


def gather_blocks(blks, chunks, *, name):
    gb = GatherBlocks(blks, chunks)
    n = gb.n

    def body(*refs):
        x_refs, out_refs, sems = refs[:n], refs[n:2 * n], refs[2 * n:]
        for which in range(gb.N_STAGES):
            gb.stage(which, x_refs, out_refs, sems)

    return pl.pallas_call(body, out_shape=gb.shapes, in_specs=[_ANY] * n, out_specs=[_ANY] * n,
                          scratch_shapes=gb.sems, name=name)(*blks)


def gather_small(small, *, name):
    S, C = small.shape

    def body(s_ref, out_ref, send_sems, recv_sems, local_sem):
        x, y, c = _my_pos()
        my_id = 4 * x + 2 * y + c

        def copy(k, slot):
            px, py, pc = x ^ ((k >> 2) & 1), y ^ ((k >> 1) & 1), c ^ (k & 1)
            dst = my_id if slot == "mine" else 4 * px + 2 * py + pc
            return pltpu.make_async_remote_copy(
                src_ref=s_ref, dst_ref=out_ref.at[dst], send_sem=send_sems.at[k - 1], recv_sem=recv_sems.at[k - 1],
                device_id=(px, py, pc), device_id_type=MESH)

        own = pltpu.make_async_copy(s_ref, out_ref.at[my_id], local_sem)
        own.start()
        sends = [copy(k, "mine") for k in range(1, N_DEV)]
        for cp in sends:
            cp.start()
        for k in range(1, N_DEV):
            copy(k, "theirs").wait_recv()
        for cp in sends:
            cp.wait_send()
        own.wait()

    dma7 = pltpu.SemaphoreType.DMA((7,))
    return pl.pallas_call(
        body, out_shape=jax.ShapeDtypeStruct((N_DEV, S, C), small.dtype), in_specs=[_ANY], out_specs=_ANY,
        scratch_shapes=[dma7, dma7, pltpu.SemaphoreType.DMA], name=name)(small)


class PairExchange:
    def __init__(self, bigs, chunks):
        self.shapes = [jax.ShapeDtypeStruct((4,) + b.shape[1:], b.dtype) for b in bigs]
        self.n = len(bigs)
        self.pieces = _pieces(bigs, chunks)
        n_p = len(self.pieces)
        self.sems = [pltpu.SemaphoreType.DMA((4 * n_p,)), pltpu.SemaphoreType.DMA((4 * n_p,))]

    def _copies(self, big_refs, out_refs, sems):
        send_sems, recv_sems = sems
        n_p = len(self.pieces)
        x, y, c = _my_pos()

        def copy(b, p):
            a, rows = self.pieces[p]
            return pltpu.make_async_remote_copy(
                src_ref=big_refs[a].at[2 * b + (1 - c), rows], dst_ref=out_refs[a].at[b, rows],
                send_sem=send_sems.at[b * n_p + p], recv_sem=recv_sems.at[b * n_p + p],
                device_id=(x, y, 1 - c), device_id_type=MESH)

        return [copy(b, p) for b in range(4) for p in range(n_p)]

    def start(self, big_refs, out_refs, sems):
        for cp in self._copies(big_refs, out_refs, sems):
            cp.start()

    def finish(self, big_refs, out_refs, sems):
        cps = self._copies(big_refs, out_refs, sems)
        for cp in cps:
            cp.wait_recv()
        for cp in cps:
            cp.wait_send()


def _standalone(exchange, args, name):
    n = exchange.n

    def body(*refs):
        exchange.start(refs[:n], refs[n:2 * n], refs[2 * n:])
        exchange.finish(refs[:n], refs[n:2 * n], refs[2 * n:])

    return pl.pallas_call(body, out_shape=exchange.shapes, in_specs=[_ANY] * n, out_specs=[_ANY] * n,
                          scratch_shapes=exchange.sems, name=name)(*args)


class Riding:
    def __init__(self, riders):
        self.riders = [(ex, list(args)) for ex, args in riders]
        self.args = [a for _, args in self.riders for a in args]
        self.in_specs = [_ANY] * len(self.args)
        self.out_shapes = [s for ex, _ in self.riders for s in ex.shapes]
        self.out_specs = [_ANY] * len(self.out_shapes)
        self.scratch = [s for ex, _ in self.riders for s in ex.sems]

    def wrap(self, body, n_in, n_out, n_scratch, is_first, is_last):
        def wrapped(*refs):
            k = 0
            core = list(refs[:n_in])
            k = n_in
            r_in = []
            for ex, _ in self.riders:
                r_in.append(refs[k:k + ex.n])
                k += ex.n
            core += refs[k:k + n_out]
            k += n_out
            r_out = []
            for ex, _ in self.riders:
                r_out.append(refs[k:k + ex.n])
                k += ex.n
            core += refs[k:k + n_scratch]
            k += n_scratch
            r_sem = []
            for ex, _ in self.riders:
                r_sem.append(refs[k:k + len(ex.sems)])
                k += len(ex.sems)

            @pl.when(is_first())
            def _():
                for (ex, _), a, b, s in zip(self.riders, r_in, r_out, r_sem):
                    ex.start(a, b, s)

            body(*core)

            @pl.when(is_last())
            def _():
                for (ex, _), a, b, s in zip(self.riders, r_in, r_out, r_sem):
                    ex.finish(a, b, s)

        return wrapped

    def split(self, outs, n_out):
        core, rest, per = list(outs[:n_out]), list(outs[n_out:]), []
        for ex, _ in self.riders:
            per.append(rest[:ex.n])
            rest = rest[ex.n:]
        return core, per


def pair_sum(big, sib, c, *, tr, name):
    _, R, C = big.shape

    def body(c_ref, a_ref, s_ref, o_ref):
        o_ref[...] = (a_ref[...].astype(F32) + s_ref[...].astype(F32)).astype(o_ref.dtype)

    grid_spec = pltpu.PrefetchScalarGridSpec(
        num_scalar_prefetch=1, grid=(4, R // tr),
        in_specs=[pl.BlockSpec((None, tr, C), lambda b, i, c_ref: (2 * b + c_ref[0], i, 0)),
                  pl.BlockSpec((None, tr, C), lambda b, i, c_ref: (b, i, 0))],
        out_specs=pl.BlockSpec((None, tr, C), lambda b, i, c_ref: (b, i, 0)))
    return pl.pallas_call(body, grid_spec=grid_spec, out_shape=jax.ShapeDtypeStruct((4, R, C), big.dtype),
                          compiler_params=_cparams(2), name=name)(c.reshape(1).astype(jnp.int32), big, sib)


class ChipScatter:
    def __init__(self, pres, chunks):
        self.shapes = [jax.ShapeDtypeStruct(p.shape, p.dtype) for p in pres]
        self.n = len(pres)
        self.pieces = _pieces(pres, chunks)
        n_p = len(self.pieces)
        self.sems = [pltpu.SemaphoreType.DMA((3 * n_p,)), pltpu.SemaphoreType.DMA((3 * n_p,)),
                     pltpu.SemaphoreType.DMA((n_p,))]

    def _copies(self, pre_refs, out_refs, sems):
        send_sems, recv_sems, local_sems = sems
        n_p = len(self.pieces)
        x, y, c = _my_pos()
        my_chip = 2 * x + y
        chips = [(1 - x, y), (x, 1 - y), (1 - x, 1 - y)]

        def copy(j, p, slot):
            px, py = chips[j]
            a, rows = self.pieces[p]
            src_slot, dst_slot = (2 * px + py, my_chip) if slot == "mine" else (my_chip, 2 * px + py)
            return pltpu.make_async_remote_copy(
                src_ref=pre_refs[a].at[src_slot, rows], dst_ref=out_refs[a].at[dst_slot, rows],
                send_sem=send_sems.at[j * n_p + p], recv_sem=recv_sems.at[j * n_p + p],
                device_id=(px, py, c), device_id_type=MESH)

        own = [pltpu.make_async_copy(pre_refs[a].at[my_chip, rows], out_refs[a].at[my_chip, rows], local_sems.at[p])
               for p, (a, rows) in enumerate(self.pieces)]
        sends = [copy(j, p, "mine") for j in range(3) for p in range(n_p)]
        recvs = [copy(j, p, "theirs") for j in range(3) for p in range(n_p)]
        return own, sends, recvs

    def start(self, pre_refs, out_refs, sems):
        own, sends, _ = self._copies(pre_refs, out_refs, sems)
        for cp in sends + own:
            cp.start()

    def finish(self, pre_refs, out_refs, sems):
        own, sends, recvs = self._copies(pre_refs, out_refs, sems)
        for cp in recvs:
            cp.wait_recv()
        for cp in sends:
            cp.wait_send()
        for cp in own:
            cp.wait()


def chip_scatter(pres, chunks, *, name):
    cs = ChipScatter(pres, chunks)
    n = cs.n

    def body(*refs):
        pre_refs, out_refs, sems = refs[:n], refs[n:2 * n], refs[2 * n:]
        cs.start(pre_refs, out_refs, sems)
        cs.finish(pre_refs, out_refs, sems)

    return pl.pallas_call(body, out_shape=cs.shapes, in_specs=[_ANY] * n, out_specs=[_ANY] * n,
                          scratch_shapes=cs.sems, name=name)(*pres)


def sibling_send(blks, chunks, *, name):
    n = len(blks)
    pieces = _pieces(blks, chunks)
    n_p = len(pieces)

    def body(*refs):
        x_refs, out_refs = refs[:n], refs[n:2 * n]
        send_sems, recv_sems = refs[2 * n:]
        x, y, c = _my_pos()
        cps = [pltpu.make_async_remote_copy(
            src_ref=x_refs[a].at[rows], dst_ref=out_refs[a].at[rows], send_sem=send_sems.at[p],
            recv_sem=recv_sems.at[p], device_id=(x, y, 1 - c), device_id_type=MESH)
            for p, (a, rows) in enumerate(pieces)]
        for cp in cps:
            cp.start()
        for cp in cps:
            cp.wait_recv()
        for cp in cps:
            cp.wait_send()

    return pl.pallas_call(
        body, out_shape=[jax.ShapeDtypeStruct(b.shape, b.dtype) for b in blks],
        in_specs=[_ANY] * n, out_specs=[_ANY] * n,
        scratch_shapes=[pltpu.SemaphoreType.DMA((n_p,)), pltpu.SemaphoreType.DMA((n_p,))],
        name=name)(*blks)


def reduce_slots(land, *, tr, name):
    n, R, C = land.shape

    def body(l_ref, o_ref):
        acc = l_ref[0].astype(F32)
        for s in range(1, n):
            acc = acc + l_ref[s].astype(F32)
        o_ref[...] = acc

    return pl.pallas_call(
        body, grid=(R // tr,), in_specs=[pl.BlockSpec((n, tr, C), lambda i: (0, i, 0))],
        out_specs=pl.BlockSpec((tr, C), lambda i: (i, 0)), out_shape=jax.ShapeDtypeStruct((R, C), F32),
        compiler_params=_cparams(1), name=name)(land)


TM = 512


def _tk(d):
    return min(d.shape[0], 1024)


def ffn_fwd(x, g, wgu, wd, tag):
    gu, h = norm_matmul(x, g, wgu, tm=_tk(x), tn=1408, split=True, name=f"{tag}_gu")
    xo = mm_nn(gu, wd, res=x, scale=0.5, swiglu=True, tm=TM, tn=D_MODEL, tk=1408, name=f"{tag}_down")
    return xo, (x, gu, h)


def ffn_bwd(d, saved, g, wgu, wd, tag, before_dx=None):
    x, gu, h = saved
    dgu, act = ffn_bwd_act(d, wd, gu, tm=TM, tn=1408, name=f"{tag}_bact")
    dwd = mm_tn(act, d, scale=0.5, a_split=False, b_split=False, tm=1408, tn=D_MODEL, tk=_tk(d), name=f"{tag}_dwd")
    dwgu = mm_tn(h, dgu, scale=1.0, a_split=False, b_split=True, tm=TM, tn=1408, tk=_tk(d), out_blocked=True,
                 name=f"{tag}_dwgu")
    riding = before_dx(dwgu, dwd) if before_dx is not None else None
    dx, dg, *rode = mm_nt_normbwd(dgu, wgu, x, g, d, a_split=True, tm=_tk(d), tk=1408, name=f"{tag}_dx",
                                  riding=riding)
    return dx, dg, dwgu, dwd, rode


def _tile2(v):
    return jnp.concatenate([v, v], axis=-1).reshape(1, LANES)


def _fold2(v):
    return v[:, :HEAD_DIM] + v[:, HEAD_DIM:]


EVEN = dict(dil=1, nsub=2, ppk=4, q_blk=0, k_blk=4, v_blk=5, n_heads=A_Q_HEADS, group=A_GROUP, max_dist=A_WINDOW - 1)
STICK = dict(q_blk=6, k_blk=10, v_blk=14, n_pairs=4)


def _odd_cfg(dil):
    return dict(dil=dil, nsub=4 if dil == 1 else 1, ppk=1, q_blk=0, k_blk=8, v_blk=16, n_heads=C_HEADS, group=1,
                max_dist=BLK)


def even_fwd(x, g, win, qg, kg, sinks, wout, tag, riders=None, rider_args=()):
    qkv, h = norm_matmul(x, g, win, tm=_tk(x), tn=1152, split=False, name=f"{tag}_in")
    qg2, kg2 = _tile2(qg), _tile2(kg)
    slopes = jnp.asarray(_alibi(A_Q_HEADS), F32)
    oa, lse = banded_fwd(qkv, qg2, kg2, slopes, sinks, name=f"{tag}_swa", **EVEN)
    ob, rode = stick_fwd(qkv, name=f"{tag}_stick", riders=riders, rider_args=rider_args, **STICK)
    o = jnp.concatenate([oa, ob], axis=1)
    xo = mm_nn(o, wout, res=x, scale=1.0, swiglu=False, tm=TM, tn=D_MODEL, tk=D_MODEL, name=f"{tag}_out")
    return xo, (x, qkv, h, oa, lse, o), rode


def even_bwd(d, saved, g, win, qg, kg, sinks, wout, tag, riders=None, rider_args=()):
    x, qkv, h, oa, lse, o = saved
    qg2, kg2 = _tile2(qg), _tile2(kg)
    slopes = jnp.asarray(_alibi(A_Q_HEADS), F32)
    dwout = mm_tn(o, d, scale=1.0, a_split=False, b_split=False, tm=D_MODEL, tn=D_MODEL, tk=_tk(d), name=f"{tag}_dwout")
    do = mm_nt(d, wout, tm=TM, tn=D_MODEL, tk=D_MODEL, name=f"{tag}_do")
    dqa, dka4, dva4, dqg, dkg, dsk = banded_bwd(qkv, qg2, kg2, slopes, sinks, do, oa, lse, None, None,
                                                do_blk=0, name=f"{tag}_swa_b", **EVEN)
    dqb, dkb, dvb, rode = stick_bwd(qkv, do, do_blk=4, name=f"{tag}_stick_b", riders=riders, rider_args=rider_args,
                                    **STICK)
    dqkv = assemble_even(dqa, dka4, dva4, dqb, dkb, dvb, tm=TM, name=f"{tag}_asm")
    dwin = mm_tn(h, dqkv, scale=1.0, a_split=False, b_split=False, tm=D_MODEL, tn=1152, tk=_tk(d), name=f"{tag}_dwin")
    dx, dg = mm_nt_normbwd(dqkv, win, x, g, d, a_split=False, tm=TM, tk=1152, name=f"{tag}_dx")
    return dx, dg, dwin, _fold2(dqg), _fold2(dkg), dsk[:, :A_Q_HEADS], dwout, rode


def odd_fwd(x, g, win, qg, kg, wout, tag):
    qkv, h = norm_matmul(x, g, win, tm=_tk(x), tn=768, split=False, name=f"{tag}_in")
    qg2, kg2 = _tile2(qg), _tile2(kg)
    outs = []
    for p, (window, dil) in enumerate(C_PATTERNS):
        slopes = jnp.asarray(_alibi(C_HEADS), F32) * float(dil)
        outs.append(banded_fwd(qkv, qg2, kg2, slopes, None, name=f"{tag}_dil{p}", **_odd_cfg(dil)))
    o, w1, w2, w3 = mix_fwd(outs[0][0], outs[1][0], outs[2][0], outs[0][1], outs[1][1], outs[2][1],
                            tm=TM, name=f"{tag}_mix")
    xo = mm_nn(o, wout, res=x, scale=1.0, swiglu=False, tm=TM, tn=D_MODEL, tk=D_MODEL, name=f"{tag}_out")
    return xo, (x, qkv, h, outs, (w1, w2, w3), o)


def odd_bwd(d, saved, g, win, qg, kg, wout, tag):
    x, qkv, h, outs, ws, o = saved
    qg2, kg2 = _tile2(qg), _tile2(kg)
    dwout = mm_tn(o, d, scale=1.0, a_split=False, b_split=False, tm=D_MODEL, tn=D_MODEL, tk=_tk(d), name=f"{tag}_dwout")
    do = mm_nt(d, wout, tm=TM, tn=D_MODEL, tk=D_MODEL, name=f"{tag}_do")
    parts, dqg, dkg = [], 0.0, 0.0
    for p, (window, dil) in enumerate(C_PATTERNS):
        slopes = jnp.asarray(_alibi(C_HEADS), F32) * float(dil)
        dq, dk, dv, dqg_p, dkg_p, _ = banded_bwd(qkv, qg2, kg2, slopes, None, do, outs[p][0], outs[p][1], ws[p], o,
                                                 do_blk=0, name=f"{tag}_dil{p}_b", **_odd_cfg(dil))
        parts.append((dq, dk, dv))
        dqg = dqg + dqg_p
        dkg = dkg + dkg_p
    dqkv = assemble_odd(parts, tm=TM, name=f"{tag}_asm")
    dwin = mm_tn(h, dqkv, scale=1.0, a_split=False, b_split=False, tm=TM, tn=768, tk=_tk(d), out_blocked=True,
                 name=f"{tag}_dwin")
    dx, dg = mm_nt_normbwd(dqkv, win, x, g, d, a_split=False, tm=TM, tk=768, name=f"{tag}_dx")
    return dx, dg, dwin, _fold2(dqg), _fold2(dkg), dwout


def xa_fwd(x, mem, g, gm, wq, wkv, qg, kg, wo, tag):
    qraw, h = norm_matmul(x, g, wq, tm=TM, tn=D_MODEL, split=False, name=f"{tag}_q")
    kvraw, hm = norm_matmul(mem, gm, wkv, tm=MEM_LEN, tn=512, split=False, name=f"{tag}_kv")
    o = xattn_fwd(qraw, kvraw, qg, kg, tm=TM, name=f"{tag}_att")
    xo = mm_nn(o, wo, res=x, scale=1.0, swiglu=False, tm=TM, tn=D_MODEL, tk=D_MODEL, name=f"{tag}_o")
    return xo, (x, qraw, h, kvraw, hm, o)


def xa_bwd(d, saved, mem, g, gm, wq, wkv, qg, kg, wo, tag):
    x, qraw, h, kvraw, hm, o = saved
    dwo = mm_tn(o, d, scale=1.0, a_split=False, b_split=False, tm=D_MODEL, tn=D_MODEL, tk=_tk(d), name=f"{tag}_dwo")
    do = mm_nt(d, wo, tm=TM, tn=D_MODEL, tk=D_MODEL, name=f"{tag}_do")
    dq, dkv, dqg, dkg = xattn_bwd(qraw, kvraw, qg, kg, do, o, tm=TM, name=f"{tag}_att_b")
    dwq = mm_tn(h, dq, scale=1.0, a_split=False, b_split=False, tm=D_MODEL, tn=D_MODEL, tk=_tk(d), name=f"{tag}_dwq")
    dx, dg = mm_nt_normbwd(dq, wq, x, g, d, a_split=False, tm=TM, tk=D_MODEL, name=f"{tag}_dx")
    dwkv = mm_tn(hm, dkv, scale=1.0, a_split=False, b_split=False, tm=TM, tn=512, tk=MEM_LEN, out_blocked=True,
                 name=f"{tag}_dwkv")
    _, dgm = mm_nt_normbwd(dkv, wkv, mem, gm, None, a_split=False, tm=MEM_LEN, tk=512, name=f"{tag}_dmem")
    return dx, dg, dgm, dwq, dwkv, dqg, dkg, dwo


MATS = (("ffn1_w_gu", 1), ("ffn1_w_down", 0), ("ev_w_in", 1), ("ev_w_out", 0), ("od_w_in", 1), ("od_w_out", 0),
        ("xa_w_q", 0), ("xa_w_kv", 1), ("xa_w_o", 0), ("ffn2_w_gu", 1), ("ffn2_w_down", 0))
SMALLS = ("ffn1_norm", "mix_norm", "ev_q_gain", "ev_k_gain", "ev_sinks", "od_q_gain", "od_k_gain", "xa_norm",
          "xa_mem_norm", "xa_q_gain", "xa_k_gain", "ffn2_norm")
WEIGHTS = ("ffn1_norm", "ffn1_w_gu", "ffn1_w_down", "mix_norm", "ev_w_in", "ev_q_gain", "ev_k_gain", "ev_sinks",
           "ev_w_out", "od_w_in", "od_q_gain", "od_k_gain", "od_w_out", "xa_norm", "xa_mem_norm", "xa_w_q",
           "xa_w_kv", "xa_q_gain", "xa_k_gain", "xa_w_o", "ffn2_norm", "ffn2_w_gu", "ffn2_w_down")
SMALL_ROWS = 16
LAYER_GROUPS = (
    (((("ffn1_w_gu", 0), ("ffn2_w_gu", 0)), 4, 512),
     ((("ffn1_w_down", 0), ("ffn2_w_down", 0)), 2, 352),
     ((("ev_w_out", 0), ("xa_w_q", 0), ("xa_w_o", 0)), 1, 384),
     ((("xa_w_kv", 0),), 1, 512),
     ((("ev_w_in", 0),), 1, 512)),
    (((("ffn1_w_gu", 1), ("ffn2_w_gu", 1)), 4, 512),
     ((("ffn1_w_down", 1), ("ffn2_w_down", 1)), 2, 352),
     ((("od_w_out", 0), ("xa_w_q", 1), ("xa_w_o", 1)), 1, 384),
     ((("xa_w_kv", 1),), 1, 512),
     ((("od_w_in", 0),), 1, 512)),
)
GROUPS = LAYER_GROUPS[0] + LAYER_GROUPS[1]


def _chunks_of(groups):
    return tuple(g[1] for g in groups)
COL_SHARDED = {name for name, axis in MATS if axis == 1}
BLOCKED = {"ffn1_w_gu", "ffn2_w_gu", "xa_w_kv", "od_w_in"}


def group_halves(shards, c, groups):
    out = []
    for members, _, _ in groups:
        halves = []
        for name, layer in members:
            _, r, cc = shards[name].shape
            half = lax.dynamic_index_in_dim(shards[name][layer].reshape(2, r // 2, cc), c, 0, keepdims=False)
            halves.append(half.astype(BF16))
        out.append(jnp.concatenate(halves, axis=0))
    return out


def full_weights(gathered, shards, groups):
    full = {}
    for (members, _, _), arr in zip(groups, gathered):
        for w, (name, layer) in enumerate(members):
            _, r, cc = shards[name].shape
            piece = arr[:, w * (r // 2):(w + 1) * (r // 2)].reshape(4, r, cc)
            if name not in COL_SHARDED:
                piece = piece.reshape(4 * r, cc)
            elif name not in BLOCKED:
                piece = piece.transpose(1, 0, 2).reshape(r, 4 * cc)
            full[(name, layer)] = piece
    return full


def group_grads(grads, shards, groups):
    out = []
    for members, _, _ in groups:
        parts = []
        for name, layer in members:
            _, r, cc = shards[name].shape
            gfull = grads[(name, layer)]
            if name in COL_SHARDED and name not in BLOCKED:
                gfull = gfull.reshape(2, r // 2, 4, cc).transpose(2, 0, 1, 3)
            parts.append(gfull.reshape(N_DEV, r // 2, cc))
        out.append(jnp.concatenate(parts, axis=1))
    return out


def shard_grads(mine, theirs, c, shards, groups):
    per = {}
    for (members, _, _), a, b in zip(groups, mine, theirs):
        for w, (name, layer) in enumerate(members):
            _, r, cc = shards[name].shape
            rows = slice(w * (r // 2), (w + 1) * (r // 2))
            lo = jnp.where(c == 0, a[rows], b[rows])
            hi = jnp.where(c == 0, b[rows], a[rows])
            per[(name, layer)] = jnp.concatenate([lo, hi], axis=0)
    return per


def pack_small(vals):
    row10 = jnp.concatenate([vals["xa_q_gain"].reshape(1, 512), vals["xa_k_gain"].reshape(1, 512)], axis=1)
    row11 = jnp.concatenate([vals["ev_q_gain"], vals["ev_k_gain"], vals["od_q_gain"], vals["od_k_gain"],
                             vals["ev_sinks"], jnp.zeros((1, 1024 - 4 * 64 - 8), F32)], axis=1)
    return jnp.concatenate([vals["ffn1_norm"], vals["mix_norm"], vals["xa_norm"], vals["xa_mem_norm"],
                            vals["ffn2_norm"], row10, row11, jnp.zeros((SMALL_ROWS - 12, 1024), F32)], axis=0)


def unpack_small(arr):
    return {"ffn1_norm": arr[0:2], "mix_norm": arr[2:4], "xa_norm": arr[4:6], "xa_mem_norm": arr[6:8],
            "ffn2_norm": arr[8:10],
            "xa_q_gain": arr[10:11, 0:512].reshape(2, 256), "xa_k_gain": arr[10:11, 512:1024].reshape(2, 256),
            "ev_q_gain": arr[11:12, 0:64], "ev_k_gain": arr[11:12, 64:128], "od_q_gain": arr[11:12, 128:192],
            "od_k_gain": arr[11:12, 192:256], "ev_sinks": arr[11:12, 256:264]}


def local_step(x, mem, target, W, small, prereduce, later):
    depth = small["ffn1_norm"].shape[0]

    def row(name, l):
        return small[name][l:l + 1]

    saved = []
    for l in range(depth):
        j = l // 2
        x, s1 = ffn_fwd(x, row("ffn1_norm", l), W[("ffn1_w_gu", l)], W[("ffn1_w_down", l)], f"l{l}_f1")
        if l % 2 == 0:
            riders, rider_args = None, ()
            if later is not None and l == 0:
                riders, rider_args = GatherBlocks(later[0], later[1]), later[0]
            x, s2, rode = even_fwd(x, row("mix_norm", l), W[("ev_w_in", j)], row("ev_q_gain", j),
                                   row("ev_k_gain", j), small["ev_sinks"][j], W[("ev_w_out", j)], f"l{l}_ev",
                                   riders=riders, rider_args=rider_args)
            if riders is not None:
                W = {**W, **later[2](rode)}
        else:
            x, s2 = odd_fwd(x, row("mix_norm", l), W[("od_w_in", j)], row("od_q_gain", j), row("od_k_gain", j),
                            W[("od_w_out", j)], f"l{l}_od")
        x, s3 = xa_fwd(x, mem, row("xa_norm", l), row("xa_mem_norm", l), W[("xa_w_q", l)], W[("xa_w_kv", l)],
                       row("xa_q_gain", l), row("xa_k_gain", l), W[("xa_w_o", l)], f"l{l}_xa")
        x, s4 = ffn_fwd(x, row("ffn2_norm", l), W[("ffn2_w_gu", l)], W[("ffn2_w_down", l)], f"l{l}_f2")
        saved.append((s1, s2, s3, s4))
    loss, d = loss_kernel(x, target, tm=TM, name="loss")

    gw = {}
    gs = {name: [None] * small[name].shape[0] for name in SMALLS}
    pending, landed = None, {}
    for l in reversed(range(depth)):
        j = l // 2
        s1, s2, s3, s4 = saved[l]
        d, dg, dwgu, dwd, _ = ffn_bwd(d, s4, row("ffn2_norm", l), W[("ffn2_w_gu", l)], W[("ffn2_w_down", l)],
                                      f"l{l}_f2")
        gs["ffn2_norm"][l] = dg
        gw[("ffn2_w_gu", l)], gw[("ffn2_w_down", l)] = dwgu, dwd
        d, dg, dgm, dwq, dwkv, dqg, dkg, dwo = xa_bwd(
            d, s3, mem, row("xa_norm", l), row("xa_mem_norm", l), W[("xa_w_q", l)], W[("xa_w_kv", l)],
            row("xa_q_gain", l), row("xa_k_gain", l), W[("xa_w_o", l)], f"l{l}_xa")
        gs["xa_norm"][l], gs["xa_mem_norm"][l], gs["xa_q_gain"][l], gs["xa_k_gain"][l] = dg, dgm, dqg, dkg
        gw[("xa_w_q", l)], gw[("xa_w_kv", l)], gw[("xa_w_o", l)] = dwq, dwkv, dwo
        if l % 2 == 0:
            riders, rider_args = None, ()
            if pending is not None:
                riders, rider_args = ChipScatter(pending[1], _chunks_of(LAYER_GROUPS[pending[0]])), pending[1]
            d, dg, dwin, dqg, dkg, dsk, dwout, rode = even_bwd(
                d, s2, row("mix_norm", l), W[("ev_w_in", j)], row("ev_q_gain", j), row("ev_k_gain", j),
                small["ev_sinks"][j], W[("ev_w_out", j)], f"l{l}_ev", riders=riders, rider_args=rider_args)
            if pending is not None:
                landed[pending[0]], pending = rode, None
            gs["ev_q_gain"][j], gs["ev_k_gain"][j], gs["ev_sinks"][j] = dqg, dkg, dsk
            gw[("ev_w_in", j)], gw[("ev_w_out", j)] = dwin, dwout
        else:
            d, dg, dwin, dqg, dkg, dwout = odd_bwd(
                d, s2, row("mix_norm", l), W[("od_w_in", j)], row("od_q_gain", j), row("od_k_gain", j),
                W[("od_w_out", j)], f"l{l}_od")
            gs["od_q_gain"][j], gs["od_k_gain"][j] = dqg, dkg
            gw[("od_w_in", j)], gw[("od_w_out", j)] = dwin, dwout
        gs["mix_norm"][l] = dg
        packed = []

        def before_dx(dwgu, dwd, l=l, packed=packed):
            gw[("ffn1_w_gu", l)], gw[("ffn1_w_down", l)] = dwgu, dwd
            packed += prereduce.pack(gw, l)
            return Riding([(PairExchange(packed, _chunks_of(LAYER_GROUPS[l])), packed)])

        d, dg, dwgu, dwd, rode = ffn_bwd(d, s1, row("ffn1_norm", l), W[("ffn1_w_gu", l)], W[("ffn1_w_down", l)],
                                         f"l{l}_f1", before_dx=before_dx)
        gs["ffn1_norm"][l] = dg
        if pending is not None:
            landed[pending[0]] = chip_scatter(pending[1], _chunks_of(LAYER_GROUPS[pending[0]]),
                                              name=f"scatter_grads{pending[0]}")
        pending = (l, prereduce.sums(packed, rode[0], l))
    landed[pending[0]] = chip_scatter(pending[1], _chunks_of(LAYER_GROUPS[pending[0]]), name=f"scatter_grads{pending[0]}")
    gsmall = {name: jnp.concatenate(v, axis=0) for name, v in gs.items()}
    return loss, d, landed, gsmall


def kernel(x, mem, ffn1_norm, ffn1_w_gu, ffn1_w_down, mix_norm, ev_w_in, ev_q_gain, ev_k_gain, ev_sinks, ev_w_out, od_w_in, od_q_gain, od_k_gain, od_w_out, xa_norm, xa_mem_norm, xa_w_q, xa_w_kv, xa_q_gain, xa_k_gain, xa_w_o, ffn2_norm, ffn2_w_gu, ffn2_w_down, loss_target, m_ffn1_norm, m_ffn1_w_gu, m_ffn1_w_down, m_mix_norm, m_ev_w_in, m_ev_q_gain, m_ev_k_gain, m_ev_sinks, m_ev_w_out, m_od_w_in, m_od_q_gain, m_od_k_gain, m_od_w_out, m_xa_norm, m_xa_mem_norm, m_xa_w_q, m_xa_w_kv, m_xa_q_gain, m_xa_k_gain, m_xa_w_o, m_ffn2_norm, m_ffn2_w_gu, m_ffn2_w_down, v_ffn1_norm, v_ffn1_w_gu, v_ffn1_w_down, v_mix_norm, v_ev_w_in, v_ev_q_gain, v_ev_k_gain, v_ev_sinks, v_ev_w_out, v_od_w_in, v_od_q_gain, v_od_k_gain, v_od_w_out, v_xa_norm, v_xa_mem_norm, v_xa_w_q, v_xa_w_kv, v_xa_q_gain, v_xa_k_gain, v_xa_w_o, v_ffn2_norm, v_ffn2_w_gu, v_ffn2_w_down):
    given = dict(locals())
    w = {n: given[n] for n in WEIGHTS}
    m = {n: given["m_" + n] for n in WEIGHTS}
    v = {n: given["v_" + n] for n in WEIGHTS}
    c = lax.axis_index("c")
    shards = {name: w[name] for name, _ in MATS}
    small = {n: w[n] for n in SMALLS}

    groups0, groups1 = LAYER_GROUPS
    gathered = gather_blocks(group_halves(shards, c, groups0), _chunks_of(groups0), name="gather_weights0")
    full = full_weights(gathered, shards, groups0)
    later = (group_halves(shards, c, groups1), _chunks_of(groups1), lambda got: full_weights(got, shards, groups1))

    class prereduce:
        @staticmethod
        def pack(gw, layer):
            return group_grads(gw, shards, LAYER_GROUPS[layer])

        @staticmethod
        def sums(packed, sib, layer):
            return [pair_sum(p, s, c, tr=g[2], name=f"pair_sum{layer}_{i}")
                    for i, (g, p, s) in enumerate(zip(LAYER_GROUPS[layer], packed, sib))]

    loss_b, grad_x, landed, gsmall = local_step(x[0], mem[0], loss_target[0], full, small, prereduce, later)

    per = {}
    for layer, land in sorted(landed.items()):
        groups = LAYER_GROUPS[layer]
        mine = [reduce_slots(a, tr=g[2], name=f"sum_grads{layer}_{i}") for i, (g, a) in enumerate(zip(groups, land))]
        theirs = sibling_send(mine, _chunks_of(groups), name=f"swap_grads{layer}")
        per.update(shard_grads(mine, theirs, c, shards, groups))
    g = {name: jnp.stack([per[(name, layer)] for layer in range(w[name].shape[0])], axis=0) for name, _ in MATS}
    land_small = gather_small(pack_small(gsmall), name="gather_small")
    g_small = unpack_small(reduce_slots(land_small, tr=SMALL_ROWS, name="sum_small"))
    g.update(g_small)

    delta, new_m, new_v = {}, {}, {}
    for name, _ in MATS:
        shp = w[name].shape
        flat = [a.reshape(-1, shp[-1]) for a in (w[name], g[name], m[name], v[name])]
        dl, nm, nv = adamw(*flat, br=BLK, name=f"adamw_{name}")
        delta[name], new_m[name], new_v[name] = dl.reshape(shp), nm.reshape(shp), nv.reshape(shp)
    dl, nm, nv = adamw(pack_small(small), pack_small(g_small), pack_small({n: m[n] for n in SMALLS}),
                       pack_small({n: v[n] for n in SMALLS}), br=SMALL_ROWS, name="adamw_small")
    for dst, arr in ((delta, dl), (new_m, nm), (new_v, nv)):
        dst.update(unpack_small(arr))

    loss = lax.psum(loss_b[0, 0], ("x", "y", "c"))
    return (loss, grad_x[None], *[g[n] for n in WEIGHTS], *[delta[n] for n in WEIGHTS],
            *[new_m[n] for n in WEIGHTS], *[new_v[n] for n in WEIGHTS])
```

```python
import jax
import jax.numpy as jnp
from jax import lax
from jax.experimental import pallas as pl
from jax.experimental.pallas import tpu as pltpu

F32 = jnp.float32
BF16 = jnp.bfloat16

D_MODEL = 1024
HEAD_DIM = 64
LANES = 128
BLK = 128
D_FF = 2816
RMS_EPS = 1e-6
MEM_LEN = 256
X_HEADS = 4
X_HEAD_DIM = 256
A_Q_HEADS = 8
A_GROUP = 4
A_WINDOW = 128
C_HEADS = 16
C_PATTERNS = ((128, 1), (512, 4), (2048, 16))
NEG = -1e30
VMEM_LIMIT = 56 * 2 ** 20

ADAM_LR = 0.001
ADAM_B1 = 0.9
ADAM_B2 = 0.999
ADAM_EPS = 1e-08
ADAM_WD = 0.01
ADAM_STEP = 10

N_DEV = 8
MESH = pl.DeviceIdType.MESH


def _cparams(n):
    return pltpu.CompilerParams(dimension_semantics=("arbitrary",) * n, vmem_limit_bytes=VMEM_LIMIT)


def _dot(a, b):
    return jnp.dot(a, b, preferred_element_type=F32)


def _dot_nt(a, b):
    return lax.dot_general(a, b, (((1,), (1,)), ((), ())), preferred_element_type=F32)


def _dot_tn(a, b):
    return lax.dot_general(a, b, (((0,), (0,)), ((), ())), preferred_element_type=F32)


def _sigmoid(z):
    return 1.0 / (1.0 + jnp.exp(-z))


def norm_matmul(x, g, w, *, tm, tn, split, name):
    T, K = x.shape
    blocked = w.ndim == 3
    assert not blocked or w.shape[2] == tn
    N = w.shape[0] * w.shape[2] if blocked else w.shape[1]
    nj = N // tn

    def body(x_ref, g_ref, w_ref, o_ref, h_ref):
        @pl.when(pl.program_id(1) == 0)
        def _():
            xv = x_ref[...]
            r = lax.rsqrt(jnp.mean(xv * xv, axis=-1, keepdims=True) + RMS_EPS)
            h_ref[...] = (xv * r * g_ref[...]).astype(BF16)

        o_ref[...] = _dot(h_ref[...], w_ref[...]).astype(o_ref.dtype)

    if split:
        njh = nj // 2
        o_shape = jax.ShapeDtypeStruct((2, T, N // 2), BF16)
        o_spec = pl.BlockSpec((None, tm, tn), lambda i, j: (j // njh, i, j % njh))
    else:
        o_shape = jax.ShapeDtypeStruct((T, N), F32)
        o_spec = pl.BlockSpec((tm, tn), lambda i, j: (i, j))
    return pl.pallas_call(
        body, grid=(T // tm, nj),
        in_specs=[pl.BlockSpec((tm, K), lambda i, j: (i, 0)),
                  pl.BlockSpec((1, K), lambda i, j: (0, 0)),
                  (pl.BlockSpec((None, K, tn), lambda i, j: (j, 0, 0)) if blocked
                   else pl.BlockSpec((K, tn), lambda i, j: (0, j)))],
        out_specs=[o_spec, pl.BlockSpec((tm, K), lambda i, j: (i, 0))],
        out_shape=[o_shape, jax.ShapeDtypeStruct((T, K), BF16)],
        compiler_params=_cparams(2), name=name)(x, g, w)


def mm_nn(a, b, *, res, scale, swiglu, tm, tn, tk, name):
    T = a.shape[-2]
    K, N = b.shape
    nk = K // tk

    def body(*refs):
        if swiglu:
            g_ref, u_ref, b_ref, r_ref, o_ref, acc = refs
        else:
            a_ref, b_ref, r_ref, o_ref, acc = refs
        k = pl.program_id(2)

        @pl.when(k == 0)
        def _():
            acc[...] = jnp.zeros_like(acc)

        if swiglu:
            gv = g_ref[...].astype(F32)
            av = (gv * _sigmoid(gv) * u_ref[...].astype(F32)).astype(BF16)
        else:
            av = a_ref[...].astype(BF16)
        acc[...] += _dot(av, b_ref[...])

        @pl.when(k == nk - 1)
        def _():
            o_ref[...] = r_ref[...] + scale * acc[...]

    if swiglu:
        a_specs = [pl.BlockSpec((None, tm, tk), lambda i, j, k: (0, i, k)),
                   pl.BlockSpec((None, tm, tk), lambda i, j, k: (1, i, k))]
        a_args = [a, a]
    else:
        a_specs = [pl.BlockSpec((tm, tk), lambda i, j, k: (i, k))]
        a_args = [a]
    return pl.pallas_call(
        body, grid=(T // tm, N // tn, nk),
        in_specs=a_specs + [pl.BlockSpec((tk, tn), lambda i, j, k: (k, j)),
                            pl.BlockSpec((tm, tn), lambda i, j, k: (i, j))],
        out_specs=pl.BlockSpec((tm, tn), lambda i, j, k: (i, j)),
        out_shape=jax.ShapeDtypeStruct((T, N), F32),
        scratch_shapes=[pltpu.VMEM((tm, tn), F32)],
        compiler_params=_cparams(3), name=name)(*a_args, b, res)


def mm_nt(a, b, *, tm, tn, tk, name):
    T, K = a.shape
    N = b.shape[0]
    nk = K // tk

    def body(a_ref, b_ref, o_ref, acc):
        k = pl.program_id(2)

        @pl.when(k == 0)
        def _():
            acc[...] = jnp.zeros_like(acc)

        acc[...] += _dot_nt(a_ref[...].astype(BF16), b_ref[...])

        @pl.when(k == nk - 1)
        def _():
            o_ref[...] = acc[...]

    return pl.pallas_call(
        body, grid=(T // tm, N // tn, nk),
        in_specs=[pl.BlockSpec((tm, tk), lambda i, j, k: (i, k)),
                  pl.BlockSpec((tn, tk), lambda i, j, k: (j, k))],
        out_specs=pl.BlockSpec((tm, tn), lambda i, j, k: (i, j)),
        out_shape=jax.ShapeDtypeStruct((T, N), F32),
        scratch_shapes=[pltpu.VMEM((tm, tn), F32)],
        compiler_params=_cparams(3), name=name)(a, b)


def ffn_bwd_act(d, wd, gu, *, tm, tn, name, riding=None):
    T, K = d.shape
    Fd = wd.shape[0]

    def body(d_ref, w_ref, g_ref, u_ref, dgu_ref, act_ref):
        da = 0.5 * _dot_nt(d_ref[...].astype(BF16), w_ref[...])
        gv = g_ref[...].astype(F32)
        uv = u_ref[...].astype(F32)
        s = _sigmoid(gv)
        silu = gv * s
        act_ref[...] = (silu * uv).astype(BF16)
        dgu_ref[0] = (da * uv * (s * (1.0 + gv * (1.0 - s)))).astype(BF16)
        dgu_ref[1] = (da * silu).astype(BF16)

    in_specs = [pl.BlockSpec((tm, K), lambda j, i: (i, 0)),
                pl.BlockSpec((tn, K), lambda j, i: (j, 0)),
                pl.BlockSpec((None, tm, tn), lambda j, i: (0, i, j)),
                pl.BlockSpec((None, tm, tn), lambda j, i: (1, i, j))]
    out_specs = [pl.BlockSpec((2, tm, tn), lambda j, i: (0, i, j)), pl.BlockSpec((tm, tn), lambda j, i: (i, j))]
    out_shape = [jax.ShapeDtypeStruct((2, T, Fd), BF16), jax.ShapeDtypeStruct((T, Fd), BF16)]
    return _call_with_riders(body, riding, (Fd // tn, T // tm), in_specs, out_specs, out_shape, [],
                             [d, wd, gu, gu], name)


def _call_with_riders(body, riding, grid, in_specs, out_specs, out_shape, scratch, args, name):
    n_out = len(out_shape)
    if riding is None:
        return pl.pallas_call(body, grid=grid, in_specs=in_specs, out_specs=out_specs, out_shape=out_shape,
                              scratch_shapes=scratch, compiler_params=_cparams(len(grid)), name=name)(*args)

    def is_first():
        ok = pl.program_id(0) == 0
        for ax in range(1, len(grid)):
            ok = ok & (pl.program_id(ax) == 0)
        return ok

    def is_last():
        ok = pl.program_id(0) == grid[0] - 1
        for ax in range(1, len(grid)):
            ok = ok & (pl.program_id(ax) == grid[ax] - 1)
        return ok

    outs = pl.pallas_call(
        riding.wrap(body, len(in_specs), n_out, len(scratch), is_first, is_last), grid=grid,
        in_specs=list(in_specs) + riding.in_specs, out_specs=list(out_specs) + riding.out_specs,
        out_shape=list(out_shape) + riding.out_shapes, scratch_shapes=list(scratch) + riding.scratch,
        compiler_params=_cparams(len(grid)), name=name)(*args, *riding.args)
    core, per = riding.split(outs, n_out)
    return (*core, *per)


def mm_nt_normbwd(a, b, x, g, res, *, a_split, tm, tk, name, riding=None):
    T, Dm = x.shape
    blocked = b.ndim == 3
    assert not blocked or b.shape[2] == tk
    K = b.shape[0] * b.shape[2] if blocked else b.shape[1]
    nk = K // tk
    nkh = nk // 2
    has_res = res is not None

    def body(*refs):
        if has_res:
            a_ref, b_ref, x_ref, g_ref, r_ref, dx_ref, dg_ref, acc = refs
        else:
            a_ref, b_ref, x_ref, g_ref, dx_ref, dg_ref, acc = refs
        i = pl.program_id(0)
        k = pl.program_id(1)

        @pl.when(k == 0)
        def _():
            acc[...] = jnp.zeros_like(acc)

        acc[...] += _dot_nt(a_ref[...].astype(BF16), b_ref[...])

        @pl.when(k == nk - 1)
        def _():
            xv = x_ref[...]
            r = lax.rsqrt(jnp.mean(xv * xv, axis=-1, keepdims=True) + RMS_EPS)
            xh = xv * r
            dh = acc[...]
            dxh = dh * g_ref[...]
            dx = r * (dxh - xh * jnp.mean(dxh * xh, axis=-1, keepdims=True))
            if has_res:
                dx = dx + r_ref[...]
            dx_ref[...] = dx
            part = jnp.sum(dh * xh, axis=0, keepdims=True)

            @pl.when(i == 0)
            def _():
                dg_ref[...] = part

            @pl.when(i > 0)
            def _():
                dg_ref[...] += part

    if a_split:
        a_spec = pl.BlockSpec((None, tm, tk), lambda i, k: (k // nkh, i, k % nkh))
    else:
        a_spec = pl.BlockSpec((tm, tk), lambda i, k: (i, k))
    in_specs = [a_spec,
                (pl.BlockSpec((None, Dm, tk), lambda i, k: (k, 0, 0)) if blocked
                 else pl.BlockSpec((Dm, tk), lambda i, k: (0, k))),
                pl.BlockSpec((tm, Dm), lambda i, k: (i, 0)),
                pl.BlockSpec((1, Dm), lambda i, k: (0, 0))]
    args = [a, b, x, g]
    if has_res:
        in_specs.append(pl.BlockSpec((tm, Dm), lambda i, k: (i, 0)))
        args.append(res)
    out_specs = [pl.BlockSpec((tm, Dm), lambda i, k: (i, 0)), pl.BlockSpec((1, Dm), lambda i, k: (0, 0))]
    out_shape = [jax.ShapeDtypeStruct((T, Dm), F32), jax.ShapeDtypeStruct((1, Dm), F32)]
    scratch = [pltpu.VMEM((tm, Dm), F32)]
    return _call_with_riders(body, riding, (T // tm, nk), in_specs, out_specs, out_shape, scratch, args, name)


def mm_tn(a, b, *, scale, a_split, b_split, tm, tn, tk, name, out_blocked=False, riding=None):
    T = a.shape[-2]
    M = a.shape[-1] * (2 if a_split else 1)
    N = b.shape[-1] * (2 if b_split else 1)
    ni, nj, nk = M // tm, N // tn, T // tk
    nih, njh = ni // 2, nj // 2

    def body(a_ref, b_ref, o_ref, acc):
        k = pl.program_id(2)

        @pl.when(k == 0)
        def _():
            acc[...] = jnp.zeros_like(acc)

        acc[...] += _dot_tn(a_ref[...].astype(BF16), b_ref[...].astype(BF16))

        @pl.when(k == nk - 1)
        def _():
            o_ref[...] = (acc[...] * scale).astype(o_ref.dtype)

    if a_split:
        a_spec = pl.BlockSpec((None, tk, tm), lambda i, j, k: (i // nih, k, i % nih))
    else:
        a_spec = pl.BlockSpec((tk, tm), lambda i, j, k: (k, i))
    if b_split:
        b_spec = pl.BlockSpec((None, tk, tn), lambda i, j, k: (j // njh, k, j % njh))
    else:
        b_spec = pl.BlockSpec((tk, tn), lambda i, j, k: (k, j))
    if out_blocked:
        o_spec = pl.BlockSpec((None, None, tm, tn), lambda i, j, k: (j, i, 0, 0))
        o_shape = jax.ShapeDtypeStruct((nj, ni, tm, tn), BF16)
    else:
        o_spec = pl.BlockSpec((tm, tn), lambda i, j, k: (i, j))
        o_shape = jax.ShapeDtypeStruct((M, N), BF16)
    outs = _call_with_riders(body, riding, (ni, nj, nk), [a_spec, b_spec], [o_spec], [o_shape],
                             [pltpu.VMEM((tm, tn), F32)], [a, b], name)
    return outs[0] if riding is None else tuple(outs)


def loss_kernel(y, target, *, tm, name):
    T, Dm = y.shape

    def body(y_ref, t_ref, l_ref, dy_ref):
        e = y_ref[...] - t_ref[...]
        dy_ref[...] = e * (1.0 / Dm)
        part = (0.5 / Dm) * jnp.sum(jnp.sum(e * e, axis=-1, keepdims=True), axis=0, keepdims=True)
        part = jnp.broadcast_to(part, (8, LANES))

        @pl.when(pl.program_id(0) == 0)
        def _():
            l_ref[...] = part

        @pl.when(pl.program_id(0) > 0)
        def _():
            l_ref[...] += part

    return pl.pallas_call(
        body, grid=(T // tm,),
        in_specs=[pl.BlockSpec((tm, Dm), lambda i: (i, 0)), pl.BlockSpec((tm, Dm), lambda i: (i, 0))],
        out_specs=[pl.BlockSpec((8, LANES), lambda i: (0, 0)), pl.BlockSpec((tm, Dm), lambda i: (i, 0))],
        out_shape=[jax.ShapeDtypeStruct((8, LANES), F32), jax.ShapeDtypeStruct((T, Dm), F32)],
        compiler_params=_cparams(1), name=name)(y, target)


def adamw(w, g, m, v, *, br, name):
    R, C = w.shape

    def body(w_ref, g_ref, m_ref, v_ref, d_ref, nm_ref, nv_ref):
        gv = g_ref[...]
        nm = ADAM_B1 * m_ref[...] + (1.0 - ADAM_B1) * gv
        nv = ADAM_B2 * v_ref[...] + (1.0 - ADAM_B2) * (gv * gv)
        m_hat = nm / (1.0 - ADAM_B1 ** ADAM_STEP)
        v_hat = nv / (1.0 - ADAM_B2 ** ADAM_STEP)
        d_ref[...] = -ADAM_LR * (m_hat / (jnp.sqrt(v_hat) + ADAM_EPS) + ADAM_WD * w_ref[...])
        nm_ref[...] = nm
        nv_ref[...] = nv

    spec = pl.BlockSpec((br, C), lambda i: (i, 0))
    shp = jax.ShapeDtypeStruct((R, C), F32)
    return pl.pallas_call(
        body, grid=(R // br,), in_specs=[spec] * 4, out_specs=[spec] * 3, out_shape=[shp] * 3,
        compiler_params=_cparams(1), name=name)(w, g, m, v)


def _lane0():
    return lax.broadcasted_iota(jnp.int32, (1, LANES), 1) < HEAD_DIM


def _half_sum(x, m0):
    s0 = jnp.sum(jnp.where(m0, x, 0.0), axis=-1, keepdims=True)
    s1 = jnp.sum(jnp.where(m0, 0.0, x), axis=-1, keepdims=True)
    return jnp.where(m0, s0, s1)


def _half_pick(x, m0, e):
    sel = m0 if e == 0 else jnp.logical_not(m0)
    return jnp.max(jnp.where(sel, x, NEG), axis=-1, keepdims=True)


def _head_rms(x, m0):
    return lax.rsqrt(_half_sum(x * x, m0) * (1.0 / HEAD_DIM) + RMS_EPS)


def _alibi(n):
    return [float(2.0 ** (-8.0 * (h + 1) / n)) for h in range(n)]


def _mask_half(x, m0, e):
    return jnp.where(m0, x, 0.0) if e == 0 else jnp.where(m0, 0.0, x)


def _band_masks2(max_dist, has_prev, live):
    row = lax.broadcasted_iota(jnp.int32, (2 * BLK, 2 * BLK), 0)
    col = lax.broadcasted_iota(jnp.int32, (2 * BLK, 2 * BLK), 1)
    dist = (row & (BLK - 1)) - col + BLK
    lim = jnp.where(live, max_dist, -1)
    first = jnp.where(has_prev, 0, BLK)
    valid = (dist >= 0) & (dist <= lim) & (col >= first)
    top = lax.broadcasted_iota(jnp.int32, (2 * BLK, 1), 0) < BLK
    return dist.astype(F32), valid, top


def _stack_heads(x, m0, kes):
    parts = []
    for e in range(2):
        h = _mask_half(x, m0, e)
        parts.append(pltpu.roll(h, HEAD_DIM, 1) if kes[e] != e else h)
    return jnp.concatenate(parts, axis=0)


def _unstack_heads(y, m0, kes):
    parts = []
    for e in range(2):
        h = y[e * BLK:(e + 1) * BLK]
        parts.append(pltpu.roll(h, HEAD_DIM, 1) if kes[e] != e else h)
    return jnp.where(m0, parts[0], parts[1])


def _rows(r, dil):
    return pl.ds(r, BLK, stride=dil) if dil > 1 else pl.ds(0, BLK)


def _band_units(dil, nsub):
    assert dil == 1 or nsub == 1
    if nsub == 1:
        return [(_rows(r, dil), ("prev", _rows(r, dil)), 0) for r in range(dil)]
    units = [(pl.ds(0, BLK), ("prev", pl.ds(0, BLK)), 0)]
    units += [(pl.ds(BLK * s, BLK), ("cur", pl.ds(BLK * (s - 1), BLK)), s) for s in range(1, nsub)]
    return units


def _band_specs(dil, nsub, ppk, q_blk, k_blk, v_blk, kv_shared, nb):
    RB = BLK * dil * nsub
    PB = BLK if nsub > 1 else RB
    qw = LANES * ppk
    kw = LANES if kv_shared else qw

    def cur(i):
        return jnp.minimum(i, nb - 1)

    def prev(i):
        return jnp.maximum(i * nsub - 1, 0) if nsub > 1 else jnp.maximum(i - 1, 0)

    def kidx(base):
        return (lambda p, i: (cur(i), base)) if kv_shared else (lambda p, i: (cur(i), base + p))

    def pidx(base):
        return (lambda p, i: (prev(i), base)) if kv_shared else (lambda p, i: (prev(i), base + p))

    return [pl.BlockSpec((RB, qw), lambda p, i: (cur(i), q_blk + p)),
            pl.BlockSpec((RB, kw), kidx(k_blk)), pl.BlockSpec((PB, kw), pidx(k_blk)),
            pl.BlockSpec((RB, kw), kidx(v_blk)), pl.BlockSpec((PB, kw), pidx(v_blk))]


def qk_norm(qkv, q_gain2, k_gain2, *, blocks, n_q, tm, name):
    T = qkv.shape[0]
    nbk = len(blocks)
    assert list(blocks) == list(range(blocks[0], blocks[0] + nbk))

    def body(x_ref, qg_ref, kg_ref, o_ref):
        m0 = _lane0()
        is_q = (pl.program_id(1) < n_q).astype(F32)
        gain = qg_ref[...] * is_q + kg_ref[...] * (1.0 - is_q)
        xv = x_ref[...]
        o_ref[...] = xv * _head_rms(xv, m0) * gain

    gspec = pl.BlockSpec((1, LANES), lambda i, j: (0, 0))
    return pl.pallas_call(
        body, grid=(T // tm, nbk),
        in_specs=[pl.BlockSpec((tm, LANES), lambda i, j: (i, blocks[0] + j)), gspec, gspec],
        out_specs=pl.BlockSpec((tm, LANES), lambda i, j: (i, j)),
        out_shape=jax.ShapeDtypeStruct((T, LANES * nbk), F32),
        compiler_params=_cparams(2), name=name)(qkv, q_gain2, k_gain2)


def banded_fwd(qkn, qkv, slopes, sinks, *, dil, nsub, ppk, q_blk, k_blk, v_blk, n_heads, group,
               max_dist, name):
    T = qkv.shape[0]
    RB = BLK * dil * nsub
    nb = T // RB
    npair = n_heads // 2
    kv_shared = group > 1
    scale = HEAD_DIM ** -0.5
    has_sink = sinks is not None

    def body(*refs):
        slope_ref = refs[0]
        if has_sink:
            sink_ref, refs = refs[1], refs[2:]
        else:
            refs = refs[1:]
        q_ref, kc_ref, kp_ref, vc_ref, vp_ref, o_ref, l_ref = refs
        pb = pl.program_id(0)
        i = pl.program_id(1)
        m0 = _lane0()
        distf, valid_first, top = _band_masks2(max_dist, i > 0, i >= 0)
        valid_inner = _band_masks2(max_dist, i >= 0, i >= 0)[1] if nsub > 1 else None
        for rows, (src, prows), sub in _band_units(dil, nsub):
            valid = valid_first if sub == 0 else valid_inner
            kpr, vpr = (kp_ref, vp_ref) if src == "prev" else (kc_ref, vc_ref)
            kcache = {}
            for jp in range(ppk):
                cs = pl.ds(LANES * jp, LANES)
                jk = 0 if kv_shared else jp
                if jk not in kcache:
                    ks = pl.ds(LANES * jk, LANES)
                    kcat = jnp.concatenate([kpr[prows, ks], kc_ref[rows, ks]], axis=0)
                    vcat = jnp.concatenate([vpr[prows, ks], vc_ref[rows, ks]], axis=0)
                    kcache[jk] = (kcat.astype(BF16), vcat.astype(BF16))
                kn, vcat = kcache[jk]
                qn = q_ref[rows, cs]
                kes = [((2 * jp + e) // group) % 2 if kv_shared else e for e in range(2)]
                hidx = 2 * (pb * ppk + jp)
                qs = _stack_heads(qn, m0, kes).astype(BF16)
                slope = jnp.where(top, slope_ref[hidx], slope_ref[hidx + 1])
                s = jnp.where(valid, _dot_nt(qs, kn) * scale - slope * distf, NEG)
                m = jnp.max(s, axis=-1, keepdims=True)
                if has_sink:
                    sk = jnp.where(top, sink_ref[hidx], sink_ref[hidx + 1])
                    m = jnp.maximum(m, sk)
                p = jnp.exp(s - m)
                den = jnp.sum(p, axis=-1, keepdims=True)
                if has_sink:
                    den = den + jnp.exp(sk - m)
                o_full = _dot((p * (1.0 / den)).astype(BF16), vcat)
                o_ref[rows, cs] = _unstack_heads(o_full, m0, kes)
                l_ref[rows, cs] = _unstack_heads(jnp.broadcast_to(m + jnp.log(den), (2 * BLK, LANES)), m0, [0, 1])

    smem = pl.BlockSpec(memory_space=pltpu.SMEM)
    qw = LANES * ppk
    ospec = pl.BlockSpec((RB, qw), lambda p, i: (i, p))
    oshape = jax.ShapeDtypeStruct((T, n_heads * HEAD_DIM), F32)
    args = [slopes] + ([sinks] if has_sink else []) + [qkn] * 3 + [qkv] * 2
    return pl.pallas_call(
        body, grid=(npair // ppk, nb),
        in_specs=[smem] * (2 if has_sink else 1) + _band_specs(dil, nsub, ppk, q_blk, k_blk, v_blk, kv_shared, nb),
        out_specs=[ospec, ospec], out_shape=[oshape, oshape],
        compiler_params=_cparams(2), name=name)(*args)


def banded_bwd(qkn, qkv, slopes, sinks, do, o, lse, w, omix, *, dil, nsub, ppk, q_blk, k_blk, v_blk,
               n_heads, group, max_dist, do_blk, name):
    T = qkv.shape[0]
    RB = BLK * dil * nsub
    nb = T // RB
    npair = n_heads // 2
    kv_shared = group > 1
    scale = HEAD_DIM ** -0.5
    has_sink = sinks is not None
    mixed = w is not None
    qw = LANES * ppk

    def body(*refs):
        slope_ref = refs[0]
        if has_sink:
            sink_ref, refs = refs[1], refs[2:]
        else:
            refs = refs[1:]
        q_ref, kc_ref, kp_ref, vc_ref, vp_ref, do_ref, o_ref, l_ref = refs[:8]
        refs = refs[8:]
        if mixed:
            w_ref, om_ref, refs = refs[0], refs[1], refs[2:]
        dq_ref, dk_ref, dv_ref, dsk_ref, ck_ref, cv_ref = refs
        pb = pl.program_id(0)
        i = pl.program_id(1)
        live = i < nb
        m0 = _lane0()
        lane = lax.broadcasted_iota(jnp.int32, (1, LANES), 1)
        distf, valid_first, top = _band_masks2(max_dist, i > 0, live)
        valid_inner = _band_masks2(max_dist, i >= 0, live)[1] if nsub > 1 else None
        livef = live.astype(F32)

        def stack_rows(x2):
            return jnp.concatenate([_half_pick(x2, m0, 0), _half_pick(x2, m0, 1)], axis=0)

        @pl.when((pb == 0) & (i == 0))
        def _():
            dsk_ref[...] = jnp.zeros_like(dsk_ref)

        @pl.when(i == 0)
        def _():
            ck_ref[...] = jnp.zeros_like(ck_ref)
            cv_ref[...] = jnp.zeros_like(cv_ref)

        dsk_acc = jnp.zeros((1, LANES), F32)
        if nsub > 1:
            dk_ref[...] = ck_ref[...]
            dv_ref[...] = cv_ref[...]
        for rows, (src, prows), sub in _band_units(dil, nsub):
            valid = valid_first if sub == 0 else valid_inner
            kpr, vpr = (kp_ref, vp_ref) if src == "prev" else (kc_ref, vc_ref)
            for jp in range(ppk):
                cs = pl.ds(LANES * jp, LANES)
                ks = pl.ds(0, LANES) if kv_shared else cs
                kn = jnp.concatenate([kpr[prows, ks], kc_ref[rows, ks]], axis=0).astype(BF16)
                vcat = jnp.concatenate([vpr[prows, ks], vc_ref[rows, ks]], axis=0).astype(BF16)
                dov = do_ref[rows, cs]
                lv = l_ref[rows, cs]
                if mixed:
                    wv = w_ref[rows, cs]
                    dmix = _half_sum(dov * om_ref[rows, cs], m0)
                    dov = dov * wv
                delta2 = _half_sum(dov * o_ref[rows, cs], m0)
                shift = stack_rows(wv * dmix if mixed else delta2)
                kes = [((2 * jp + e) // group) % 2 if kv_shared else e for e in range(2)]
                hidx = 2 * (pb * ppk + jp)
                qs = _stack_heads(q_ref[rows, cs], m0, kes).astype(BF16)
                dos = _stack_heads(dov, m0, kes).astype(BF16)
                lse = stack_rows(lv)
                slope = jnp.where(top, slope_ref[hidx], slope_ref[hidx + 1])
                p = jnp.where(valid, jnp.exp(_dot_nt(qs, kn) * scale - slope * distf - lse), 0.0)
                ds = (p * (_dot_nt(dos, vcat) - shift)).astype(BF16)
                dqn = _unstack_heads(_dot(ds, kn), m0, kes) * scale
                dkn = _dot_tn(ds, qs) * scale
                dvv = _dot_tn(p.astype(BF16), dos)
                if has_sink:
                    sk = jnp.where(top, sink_ref[hidx], sink_ref[hidx + 1])
                    contrib = -jnp.exp(sk - lse) * stack_rows(delta2) * livef
                    for e in range(2):
                        tot = jnp.sum(contrib[e * BLK:(e + 1) * BLK], axis=0, keepdims=True)
                        dsk_acc = dsk_acc + jnp.where(lane == (2 * jp + e), tot, 0.0)
                dk_raw = dkn

                @pl.when(live)
                def _():
                    dq_ref[rows, cs] = dqn

                if nsub == 1:
                    dk_ref[rows, cs] = ck_ref[rows, cs] + dk_raw[:BLK]
                    dv_ref[rows, cs] = cv_ref[rows, cs] + dvv[:BLK]
                elif sub == 0:
                    last = pl.ds(RB - BLK, BLK)
                    dk_ref[last, cs] += dk_raw[:BLK]
                    dv_ref[last, cs] += dvv[:BLK]
                else:
                    ck_ref[prows, cs] += dk_raw[:BLK]
                    cv_ref[prows, cs] += dvv[:BLK]
                ck_ref[rows, cs] = dk_raw[BLK:]
                cv_ref[rows, cs] = dvv[BLK:]
        dsk_ref[...] += dsk_acc

    smem = pl.BlockSpec(memory_space=pltpu.SMEM)
    gspec = pl.BlockSpec((1, LANES), lambda p, i: (0, 0))

    def cur(i):
        return jnp.minimum(i, nb - 1)

    qspec = pl.BlockSpec((RB, qw), lambda p, i: (cur(i), p))
    dospec = pl.BlockSpec((RB, qw), lambda p, i: (cur(i), do_blk + p))
    kvout = pl.BlockSpec((RB, qw), lambda p, i: (jnp.maximum(i - 1, 0), p))
    in_specs = ([smem] * (2 if has_sink else 1) + _band_specs(dil, nsub, ppk, q_blk, k_blk, v_blk, kv_shared, nb)
                + [dospec, qspec, qspec] + ([qspec, qspec] if mixed else []))
    args = ([slopes] + ([sinks] if has_sink else []) + [qkn] * 3 + [qkv] * 2 + [do, o, lse]
            + ([w, omix] if mixed else []))
    full = jax.ShapeDtypeStruct((T, n_heads * HEAD_DIM), F32)
    row = jax.ShapeDtypeStruct((1, LANES), F32)
    return pl.pallas_call(
        body, grid=(npair // ppk, nb + 1), in_specs=in_specs,
        out_specs=[qspec, kvout, kvout, gspec],
        out_shape=[full, full, full, row],
        scratch_shapes=[pltpu.VMEM((RB, qw), F32), pltpu.VMEM((RB, qw), F32)],
        compiler_params=_cparams(2), name=name)(*args)


def mix_fwd(o1, o2, o3, l1, l2, l3, *, tm, name):
    T, C = o1.shape

    def body(o1r, o2r, o3r, l1r, l2r, l3r, o_ref, w1r, w2r, w3r):
        a, b, c = l1r[...], l2r[...], l3r[...]
        m = jnp.maximum(jnp.maximum(a, b), c)
        ea, eb, ec = jnp.exp(a - m), jnp.exp(b - m), jnp.exp(c - m)
        inv = 1.0 / (ea + eb + ec)
        wa, wb, wc = ea * inv, eb * inv, ec * inv
        o_ref[...] = wa * o1r[...] + wb * o2r[...] + wc * o3r[...]
        w1r[...] = wa
        w2r[...] = wb
        w3r[...] = wc

    spec = pl.BlockSpec((tm, C), lambda i: (i, 0))
    shp = jax.ShapeDtypeStruct((T, C), F32)
    return pl.pallas_call(body, grid=(T // tm,), in_specs=[spec] * 6, out_specs=[spec] * 4, out_shape=[shp] * 4,
                          compiler_params=_cparams(1), name=name)(o1, o2, o3, l1, l2, l3)


def _qk_norm_bwd(raw, dn, gain, m0):
    r = _head_rms(raw, m0)
    h = raw * r
    dh = dn * gain
    d_raw = r * (dh - h * (_half_sum(dh * h, m0) * (1.0 / HEAD_DIM)))
    return d_raw, jnp.sum(dn * h, axis=0, keepdims=True)


def _acc_rows(ref, val):
    @pl.when(pl.program_id(0) == 0)
    def _():
        ref[...] = val

    @pl.when(pl.program_id(0) > 0)
    def _():
        ref[...] += val


def assemble_odd(parts, qkv, q_gain2, k_gain2, *, tm, name):
    T, C = parts[0][0].shape
    nbk = C // LANES

    def body(*refs):
        qkv_ref, qg_ref, kg_ref, o_ref, dqg_ref, dkg_ref = refs[9:]
        m0 = _lane0()
        sums = [refs[j][...] + refs[3 + j][...] + refs[6 + j][...] for j in range(3)]
        o_ref[:, pl.ds(2 * C, C)] = sums[2]
        for j, (g_ref, acc_ref) in enumerate(((qg_ref, dqg_ref), (kg_ref, dkg_ref))):
            dgain = jnp.zeros((1, LANES), F32)
            for b in range(nbk):
                cols = pl.ds(C * j + LANES * b, LANES)
                d_raw, part = _qk_norm_bwd(qkv_ref[:, cols], sums[j][:, LANES * b:LANES * (b + 1)], g_ref[...], m0)
                o_ref[:, cols] = d_raw
                dgain = dgain + part
            _acc_rows(acc_ref, dgain)

    spec = pl.BlockSpec((tm, C), lambda i: (i, 0))
    gspec = pl.BlockSpec((1, LANES), lambda i: (0, 0))
    flat = [parts[p][j] for p in range(3) for j in range(3)]
    row = jax.ShapeDtypeStruct((1, LANES), F32)
    return pl.pallas_call(body, grid=(T // tm,),
                          in_specs=[spec] * 9 + [pl.BlockSpec((tm, 2 * C), lambda i: (i, 0)), gspec, gspec],
                          out_specs=[pl.BlockSpec((tm, 3 * C), lambda i: (i, 0)), gspec, gspec],
                          out_shape=[jax.ShapeDtypeStruct((T, 3 * C), F32), row, row],
                          compiler_params=_cparams(1), name=name)(*flat, qkv, q_gain2, k_gain2)


def assemble_even(dqa, dka4, dva4, dqb, dkb, dvb, qkv, q_gain2, k_gain2, *, tm, name):
    T = dqa.shape[0]
    W = 512
    QK = 768

    def body(dqa_r, dka_r, dva_r, dqb_r, dkb_r, dvb_r, qkv_ref, qg_ref, kg_ref, o_ref, dqg_ref, dkg_ref):
        m0 = _lane0()
        ka = dka_r[...]
        va = dva_r[...]
        dqn = dqa_r[...]
        dgain = jnp.zeros((1, LANES), F32)
        for b in range(W // LANES):
            cols = pl.ds(LANES * b, LANES)
            d_raw, part = _qk_norm_bwd(qkv_ref[:, cols], dqn[:, LANES * b:LANES * (b + 1)], qg_ref[...], m0)
            o_ref[:, cols] = d_raw
            dgain = dgain + part
        _acc_rows(dqg_ref, dgain)
        dkn = ka[:, 0:128] + ka[:, 128:256] + ka[:, 256:384] + ka[:, 384:512]
        d_raw, part = _qk_norm_bwd(qkv_ref[:, pl.ds(W, LANES)], dkn, kg_ref[...], m0)
        o_ref[:, pl.ds(W, LANES)] = d_raw
        _acc_rows(dkg_ref, part)
        o_ref[:, pl.ds(640, LANES)] = va[:, 0:128] + va[:, 128:256] + va[:, 256:384] + va[:, 384:512]
        o_ref[:, pl.ds(768, W)] = dqb_r[...]
        o_ref[:, pl.ds(1280, W)] = dkb_r[...]
        o_ref[:, pl.ds(1792, W)] = dvb_r[...]

    spec = pl.BlockSpec((tm, W), lambda i: (i, 0))
    gspec = pl.BlockSpec((1, LANES), lambda i: (0, 0))
    row = jax.ShapeDtypeStruct((1, LANES), F32)
    return pl.pallas_call(body, grid=(T // tm,),
                          in_specs=[spec] * 6 + [pl.BlockSpec((tm, QK), lambda i: (i, 0)), gspec, gspec],
                          out_specs=[pl.BlockSpec((tm, 2304), lambda i: (i, 0)), gspec, gspec],
                          out_shape=[jax.ShapeDtypeStruct((T, 2304), F32), row, row],
                          compiler_params=_cparams(1), name=name)(dqa, dka4, dva4, dqb, dkb, dvb, qkv, q_gain2, k_gain2)


STICK_T = 256
STICK_DEAD = -110.0


def _split_bf16(x):
    hi = x.astype(BF16)
    lo = (x - hi.astype(F32)).astype(BF16)
    return hi, lo


def _stick_logits(qm, kt, scale, diag):
    n = STICK_T
    row = lax.broadcasted_iota(jnp.int32, (n, n), 0)
    col = lax.broadcasted_iota(jnp.int32, (n, n), 1)
    mask = col < row + jnp.where(diag, 0, n)
    z = _dot_nt(qm, kt) * scale
    lneg = -(jnp.maximum(z, 0.0) + jnp.log(1.0 + jnp.exp(-jnp.abs(z))))
    lpos = z + lneg
    lk = jnp.where(mask, lneg, 0.0)
    return mask, lpos, lneg, lk


def _cumsum_mm(x, tri):
    hi, lo = _split_bf16(x)
    return _dot(hi, tri) + _dot(lo, tri)


def stick_fwd(qkv, *, q_blk, k_blk, v_blk, n_pairs, name, riders=None, rider_args=()):
    T = qkv.shape[0]
    n = STICK_T
    nq = T // n
    scale = HEAD_DIM ** -0.5
    nc = riders.n if riders is not None else 0
    n_steps = n_pairs * nq
    stage_at = (0, (3 * n_steps) // 4, n_steps - 1, n_steps - 1)

    def body(*refs):
        q_ref, k_ref, v_ref = refs[:3]
        x_refs, o_ref = refs[3:3 + nc], refs[3 + nc]
        out_refs, sems = refs[4 + nc:4 + 2 * nc], refs[4 + 2 * nc:]
        i = pl.program_id(1)
        step_id = pl.program_id(0) * nq + i

        def ride(which):
            if riders is not None:
                @pl.when(step_id == stage_at[which])
                def _():
                    riders.stage(which, x_refs, out_refs, sems)

        ride(0)
        ride(1)
        m0 = _lane0()
        r2 = lax.broadcasted_iota(jnp.int32, (n, n), 0)
        c2 = lax.broadcasted_iota(jnp.int32, (n, n), 1)
        tri_after = (r2 > c2).astype(BF16)
        qv = q_ref[...]
        out = jnp.zeros((n, LANES), F32)
        for e in range(2):
            qm = _mask_half(qv, m0, e).astype(BF16)

            def alive(st):
                t, _, carry = st
                return (t <= i) & (jnp.max(carry) > STICK_DEAD)

            def step(st, e=e, qm=qm):
                t, acc, carry = st
                start = pl.multiple_of((i - t) * n, n)
                kt = k_ref[pl.ds(start, n), :].astype(BF16)
                vt = _mask_half(v_ref[pl.ds(start, n), :], m0, e).astype(BF16)
                mask, lpos, _, lk = _stick_logits(qm, kt, scale, t == 0)
                after = _cumsum_mm(lk, tri_after) + carry
                a = jnp.where(mask, jnp.exp(lpos + after), 0.0)
                acc = acc + _dot(a.astype(BF16), vt)
                carry = carry + jnp.sum(lk, axis=-1, keepdims=True)
                return t + 1, acc, carry

            _, acc, _ = lax.while_loop(alive, step, (jnp.int32(0), jnp.zeros((n, LANES), F32),
                                                     jnp.zeros((n, 1), F32)))
            out = out + acc
        o_ref[...] = out
        ride(2)
        ride(3)

    outs = pl.pallas_call(
        body, grid=(n_pairs, nq),
        in_specs=[pl.BlockSpec((n, LANES), lambda p, i: (i, q_blk + p)),
                  pl.BlockSpec((T, LANES), lambda p, i: (0, k_blk + p)),
                  pl.BlockSpec((T, LANES), lambda p, i: (0, v_blk + p))] + [_ANY] * nc,
        out_specs=[pl.BlockSpec((n, LANES), lambda p, i: (i, p))] + [_ANY] * nc,
        out_shape=[jax.ShapeDtypeStruct((T, n_pairs * LANES), F32)] + (riders.shapes if nc else []),
        scratch_shapes=riders.sems if nc else [],
        compiler_params=_cparams(2), name=name)(qkv, qkv, qkv, *rider_args)
    return outs[0], list(outs[1:])


def stick_bwd(qkv, do, *, q_blk, k_blk, v_blk, do_blk, n_pairs, name, riders=None, rider_args=()):
    T = qkv.shape[0]
    n = STICK_T
    nq = T // n
    scale = HEAD_DIM ** -0.5
    nc = riders.n if riders is not None else 0

    def body(*refs):
        q_ref, k_ref, v_ref, do_ref = refs[:4]
        pre_refs = refs[4:4 + nc]
        dq_ref, dk_ref, dv_ref = refs[4 + nc:7 + nc]
        land_refs = refs[7 + nc:7 + 2 * nc]
        a_keep, g_keep, s_keep = refs[7 + 2 * nc:10 + 2 * nc]
        sems = refs[10 + 2 * nc:]
        i = pl.program_id(1)
        first_step = (pl.program_id(0) == 0) & (i == 0)
        last_step = (pl.program_id(0) == n_pairs - 1) & (i == nq - 1)
        m0 = _lane0()
        r2 = lax.broadcasted_iota(jnp.int32, (n, n), 0)
        c2 = lax.broadcasted_iota(jnp.int32, (n, n), 1)
        tri_after = (r2 > c2).astype(BF16)
        tri_from = (r2 >= c2).astype(BF16)

        if riders is not None:
            @pl.when(first_step)
            def _():
                riders.start(pre_refs, land_refs, sems)

        @pl.when(i == 0)
        def _():
            dk_ref[...] = jnp.zeros_like(dk_ref)
            dv_ref[...] = jnp.zeros_like(dv_ref)

        qv = q_ref[...]
        dov = do_ref[...]
        dq_out = jnp.zeros((n, LANES), F32)
        for e in range(2):
            qm = _mask_half(qv, m0, e).astype(BF16)
            dom = _mask_half(dov, m0, e).astype(BF16)

            def alive(st):
                t, carry, _ = st
                return (t <= i) & (jnp.max(carry) > STICK_DEAD)

            def scan(st, qm=qm, dom=dom):
                t, carry, gtot = st
                start = pl.multiple_of((i - t) * n, n)
                kt = k_ref[pl.ds(start, n), :].astype(BF16)
                vt = v_ref[pl.ds(start, n), :].astype(BF16)
                mask, lpos, lneg, lk = _stick_logits(qm, kt, scale, t == 0)
                a = jnp.where(mask, jnp.exp(lpos + _cumsum_mm(lk, tri_after) + carry), 0.0)
                g = _dot_nt(dom, vt) * a
                a_keep[t] = a.astype(BF16)
                g_keep[t] = g
                s_keep[t] = jnp.exp(lneg).astype(BF16)
                return (t + 1, carry + jnp.sum(lk, axis=-1, keepdims=True),
                        gtot + jnp.sum(g, axis=-1, keepdims=True))

            z1 = jnp.zeros((n, 1), F32)
            n_live, _, gtot = lax.while_loop(alive, scan, (jnp.int32(0), z1, z1))

            def step(t, st, e=e, qm=qm, dom=dom, gtot=gtot):
                dq_acc, gright = st
                start = pl.multiple_of((i - t) * n, n)
                g = g_keep[t]
                sneg = s_keep[t].astype(F32)
                before = gtot - (_cumsum_mm(g, tri_from) + gright)
                mask = c2 < r2 + jnp.where(t == 0, 0, n)
                dz = jnp.where(mask, g * sneg - before * (1.0 - sneg), 0.0) * scale
                dzb = dz.astype(BF16)
                dq_acc = dq_acc + _dot(dzb, _mask_half(k_ref[pl.ds(start, n), :], m0, e).astype(BF16))
                dk_ref[pl.ds(start, n), :] += _dot_tn(dzb, qm)
                dv_ref[pl.ds(start, n), :] += _dot_tn(a_keep[t], dom)
                return dq_acc, gright + jnp.sum(g, axis=-1, keepdims=True)

            dq_acc, _ = lax.fori_loop(0, n_live, step, (jnp.zeros((n, LANES), F32), z1))
            dq_out = dq_out + dq_acc
        dq_ref[...] = dq_out

        if riders is not None:
            @pl.when(last_step)
            def _():
                riders.finish(pre_refs, land_refs, sems)

    tile = pl.BlockSpec((n, LANES), lambda p, i: (i, p))
    whole = pl.BlockSpec((T, LANES), lambda p, i: (0, p))
    shp = jax.ShapeDtypeStruct((T, n_pairs * LANES), F32)
    outs = pl.pallas_call(
        body, grid=(n_pairs, nq),
        in_specs=[pl.BlockSpec((n, LANES), lambda p, i: (i, q_blk + p)),
                  pl.BlockSpec((T, LANES), lambda p, i: (0, k_blk + p)),
                  pl.BlockSpec((T, LANES), lambda p, i: (0, v_blk + p)),
                  pl.BlockSpec((n, LANES), lambda p, i: (i, do_blk + p))] + [_ANY] * nc,
        out_specs=[tile, whole, whole] + [_ANY] * nc,
        out_shape=[shp, shp, shp] + (riders.shapes if nc else []),
        scratch_shapes=[pltpu.VMEM((nq, n, n), BF16), pltpu.VMEM((nq, n, n), F32), pltpu.VMEM((nq, n, n), BF16)]
        + (riders.sems if nc else []),
        compiler_params=_cparams(2), name=name)(qkv, qkv, qkv, do, *rider_args)
    return outs[0], outs[1], outs[2], list(outs[3:])


def _xnorm(x):
    r = lax.rsqrt(jnp.mean(x * x, axis=-1, keepdims=True) + RMS_EPS)
    return r, x * r


def xattn_fwd(qraw, kvraw, q_gain, k_gain, *, tm, name):
    T = qraw.shape[0]
    scale = X_HEAD_DIM ** -0.5
    W = X_HEADS * X_HEAD_DIM

    def body(q_ref, kv_ref, qg_ref, kg_ref, o_ref):
        for h in range(X_HEADS):
            cs = pl.ds(X_HEAD_DIM * h, X_HEAD_DIM)
            _, qh = _xnorm(q_ref[:, cs])
            _, kh = _xnorm(kv_ref[:, cs])
            qn = (qh * qg_ref[...]).astype(BF16)
            kn = (kh * kg_ref[...]).astype(BF16)
            v = kv_ref[:, pl.ds(W + X_HEAD_DIM * h, X_HEAD_DIM)].astype(BF16)
            s = _dot_nt(qn, kn) * scale
            m = jnp.max(s, axis=-1, keepdims=True)
            p = jnp.exp(s - m)
            p = p / jnp.sum(p, axis=-1, keepdims=True)
            o_ref[:, cs] = _dot(p.astype(BF16), v)

    gspec = pl.BlockSpec((1, X_HEAD_DIM), lambda i: (0, 0))
    return pl.pallas_call(
        body, grid=(T // tm,),
        in_specs=[pl.BlockSpec((tm, W), lambda i: (i, 0)), pl.BlockSpec((MEM_LEN, 2 * W), lambda i: (0, 0)),
                  gspec, gspec],
        out_specs=pl.BlockSpec((tm, W), lambda i: (i, 0)),
        out_shape=jax.ShapeDtypeStruct((T, W), F32),
        compiler_params=_cparams(1), name=name)(qraw, kvraw, q_gain, k_gain)


def xattn_bwd(qraw, kvraw, q_gain, k_gain, do, o, *, tm, name):
    T = qraw.shape[0]
    nt = T // tm
    scale = X_HEAD_DIM ** -0.5
    W = X_HEADS * X_HEAD_DIM

    def body(q_ref, kv_ref, qg_ref, kg_ref, do_ref, o_ref, dq_ref, dkv_ref, dqg_ref, dkg_ref, dkn_ref):
        i = pl.program_id(0)

        @pl.when(i == 0)
        def _():
            dkv_ref[...] = jnp.zeros_like(dkv_ref)
            dkn_ref[...] = jnp.zeros_like(dkn_ref)
            dqg_ref[...] = jnp.zeros_like(dqg_ref)
            dkg_ref[...] = jnp.zeros_like(dkg_ref)

        qg = qg_ref[...]
        kg = kg_ref[...]
        dqg_acc = jnp.zeros((1, X_HEAD_DIM), F32)
        for h in range(X_HEADS):
            cs = pl.ds(X_HEAD_DIM * h, X_HEAD_DIM)
            vs = pl.ds(W + X_HEAD_DIM * h, X_HEAD_DIM)
            rq, qh = _xnorm(q_ref[:, cs])
            _, kh = _xnorm(kv_ref[:, cs])
            qn = (qh * qg).astype(BF16)
            kn = (kh * kg).astype(BF16)
            v = kv_ref[:, vs].astype(BF16)
            s = _dot_nt(qn, kn) * scale
            m = jnp.max(s, axis=-1, keepdims=True)
            p = jnp.exp(s - m)
            p = p / jnp.sum(p, axis=-1, keepdims=True)
            dov = do_ref[:, cs]
            delta = jnp.sum(dov * o_ref[:, cs], axis=-1, keepdims=True)
            dob = dov.astype(BF16)
            ds = (p * (_dot_nt(dob, v) - delta)).astype(BF16)
            dqn = _dot(ds, kn) * scale
            dkn_ref[:, cs] += _dot_tn(ds, qn) * scale
            dkv_ref[:, vs] += _dot_tn(p.astype(BF16), dob)
            dqg_acc = dqg_acc + jnp.sum(dqn * qh, axis=0, keepdims=True)
            dqh = dqn * qg
            dq_ref[:, cs] = rq * (dqh - qh * jnp.mean(dqh * qh, axis=-1, keepdims=True))
        dqg_ref[...] += dqg_acc

        @pl.when(i == nt - 1)
        def _():
            dkg_acc = jnp.zeros((1, X_HEAD_DIM), F32)
            for h in range(X_HEADS):
                cs = pl.ds(X_HEAD_DIM * h, X_HEAD_DIM)
                rk, kh = _xnorm(kv_ref[:, cs])
                dkn = dkn_ref[:, cs]
                dkg_acc = dkg_acc + jnp.sum(dkn * kh, axis=0, keepdims=True)
                dkh = dkn * kg
                dkv_ref[:, cs] = rk * (dkh - kh * jnp.mean(dkh * kh, axis=-1, keepdims=True))
            dkg_ref[...] = dkg_acc

    gspec = pl.BlockSpec((1, X_HEAD_DIM), lambda i: (0, 0))
    tile = pl.BlockSpec((tm, W), lambda i: (i, 0))
    kvspec = pl.BlockSpec((MEM_LEN, 2 * W), lambda i: (0, 0))
    grow = jax.ShapeDtypeStruct((1, X_HEAD_DIM), F32)
    return pl.pallas_call(
        body, grid=(nt,), in_specs=[tile, kvspec, gspec, gspec, tile, tile],
        out_specs=[tile, kvspec, gspec, gspec],
        out_shape=[jax.ShapeDtypeStruct((T, W), F32), jax.ShapeDtypeStruct((MEM_LEN, 2 * W), F32), grow, grow],
        scratch_shapes=[pltpu.VMEM((MEM_LEN, W), F32)],
        compiler_params=_cparams(1), name=name)(qraw, kvraw, q_gain, k_gain, do, o)


_ANY = pl.BlockSpec(memory_space=pl.ANY)


def _my_pos():
    return lax.axis_index("x"), lax.axis_index("y"), lax.axis_index("c")


def _pieces(arrays, chunks):
    out = []
    for a, (arr, n) in enumerate(zip(arrays, chunks)):
        rc = arr.shape[-2] // n
        out += [(a, pl.ds(ch * rc, rc)) for ch in range(n)]
    return out


class GatherBlocks:
    N_STAGES = 4

    def __init__(self, blks, chunks):
        self.shapes = [jax.ShapeDtypeStruct((N_DEV,) + b.shape, b.dtype) for b in blks]
        self.n = len(blks)
        self.pieces = _pieces(blks, chunks)
        n_p = len(self.pieces)
        self.sems = [pltpu.SemaphoreType.DMA((7 * n_p,)), pltpu.SemaphoreType.DMA((7 * n_p,)),
                     pltpu.SemaphoreType.DMA((n_p,))]

    def stage(self, which, x_refs, out_refs, sems):
        send_sems, recv_sems, local_sems = sems
        pieces, n_p = self.pieces, len(self.pieces)
        x, y, c = _my_pos()
        me, sibling = (x, y, c), (x, y, 1 - c)
        chips = [(1 - x, y), (x, 1 - y), (1 - x, 1 - y)]
        xn, yn, dg = [(*chip, c) for chip in chips]
        ps = range(n_p)

        def slot(block, p):
            px, py, pc = block
            a, rows = pieces[p]
            return out_refs[a].at[4 * px + 2 * py + pc, rows]

        def own(p):
            a, rows = pieces[p]
            return x_refs[a].at[rows]

        def copy(k, p, block, to, from_input=False):
            return pltpu.make_async_remote_copy(
                src_ref=own(p) if from_input else slot(block, p), dst_ref=slot(block, p),
                send_sem=send_sems.at[k * n_p + p], recv_sem=recv_sems.at[k * n_p + p],
                device_id=to, device_id_type=MESH)

        mine = [pltpu.make_async_copy(own(p), slot(me, p), local_sems.at[p]) for p in ps]
        first = [copy(k, p, me, to, from_input=True) for p in ps for k, to in ((1, xn), (2, yn), (0, sibling))]
        on_x = [copy(3, p, xn, yn) for p in ps if p % 2 == 0] + [copy(4, p, xn, sibling) for p in ps]
        on_y = [copy(3, p, yn, xn) for p in ps if p % 2 == 1] + [copy(5, p, yn, sibling) for p in ps]
        on_d = [copy(6, p, dg, sibling) for p in ps]
        if which == 0:
            for cp in first + mine:
                cp.start()
        elif which == 1:
            for p in ps:
                copy(1, p, xn, me).wait_recv()
                if p % 2 == 0:
                    copy(3, p, xn, yn).start()
                copy(4, p, xn, sibling).start()
                copy(2, p, yn, me).wait_recv()
                if p % 2 == 1:
                    copy(3, p, yn, xn).start()
                copy(5, p, yn, sibling).start()
        elif which == 2:
            for p in ps:
                copy(3, p, dg, me).wait_recv()
                copy(6, p, dg, sibling).start()
        else:
            for p in ps:
                copy(0, p, sibling, me).wait_recv()
            for k, chip in zip((4, 5, 6), chips):
                for p in ps:
                    copy(k, p, (*chip, 1 - c), me).wait_recv()
            for cp in first + on_x + on_y + on_d:
                cp.wait_send()
            for cp in mine:
                cp.wait()


def gather_blocks(blks, chunks, *, name):
    gb = GatherBlocks(blks, chunks)
    n = gb.n

    def body(*refs):
        x_refs, out_refs, sems = refs[:n], refs[n:2 * n], refs[2 * n:]
        for which in range(gb.N_STAGES):
            gb.stage(which, x_refs, out_refs, sems)

    return pl.pallas_call(body, out_shape=gb.shapes, in_specs=[_ANY] * n, out_specs=[_ANY] * n,
                          scratch_shapes=gb.sems, name=name)(*blks)


def gather_small(small, *, name):
    S, C = small.shape

    def body(s_ref, out_ref, send_sems, recv_sems, local_sem):
        x, y, c = _my_pos()
        my_id = 4 * x + 2 * y + c

        def copy(k, slot):
            px, py, pc = x ^ ((k >> 2) & 1), y ^ ((k >> 1) & 1), c ^ (k & 1)
            dst = my_id if slot == "mine" else 4 * px + 2 * py + pc
            return pltpu.make_async_remote_copy(
                src_ref=s_ref, dst_ref=out_ref.at[dst], send_sem=send_sems.at[k - 1], recv_sem=recv_sems.at[k - 1],
                device_id=(px, py, pc), device_id_type=MESH)

        own = pltpu.make_async_copy(s_ref, out_ref.at[my_id], local_sem)
        own.start()
        sends = [copy(k, "mine") for k in range(1, N_DEV)]
        for cp in sends:
            cp.start()
        for k in range(1, N_DEV):
            copy(k, "theirs").wait_recv()
        for cp in sends:
            cp.wait_send()
        own.wait()

    dma7 = pltpu.SemaphoreType.DMA((7,))
    return pl.pallas_call(
        body, out_shape=jax.ShapeDtypeStruct((N_DEV, S, C), small.dtype), in_specs=[_ANY], out_specs=_ANY,
        scratch_shapes=[dma7, dma7, pltpu.SemaphoreType.DMA], name=name)(small)


class PairExchange:
    def __init__(self, bigs, chunks):
        self.shapes = [jax.ShapeDtypeStruct((4,) + b.shape[1:], b.dtype) for b in bigs]
        self.n = len(bigs)
        self.pieces = _pieces(bigs, chunks)
        n_p = len(self.pieces)
        self.sems = [pltpu.SemaphoreType.DMA((4 * n_p,)), pltpu.SemaphoreType.DMA((4 * n_p,))]

    def _copies(self, big_refs, out_refs, sems):
        send_sems, recv_sems = sems
        n_p = len(self.pieces)
        x, y, c = _my_pos()

        def copy(b, p):
            a, rows = self.pieces[p]
            return pltpu.make_async_remote_copy(
                src_ref=big_refs[a].at[2 * b + (1 - c), rows], dst_ref=out_refs[a].at[b, rows],
                send_sem=send_sems.at[b * n_p + p], recv_sem=recv_sems.at[b * n_p + p],
                device_id=(x, y, 1 - c), device_id_type=MESH)

        return [copy(b, p) for b in range(4) for p in range(n_p)]

    def start(self, big_refs, out_refs, sems):
        for cp in self._copies(big_refs, out_refs, sems):
            cp.start()

    def finish(self, big_refs, out_refs, sems):
        cps = self._copies(big_refs, out_refs, sems)
        for cp in cps:
            cp.wait_recv()
        for cp in cps:
            cp.wait_send()


def _standalone(exchange, args, name):
    n = exchange.n

    def body(*refs):
        exchange.start(refs[:n], refs[n:2 * n], refs[2 * n:])
        exchange.finish(refs[:n], refs[n:2 * n], refs[2 * n:])

    return pl.pallas_call(body, out_shape=exchange.shapes, in_specs=[_ANY] * n, out_specs=[_ANY] * n,
                          scratch_shapes=exchange.sems, name=name)(*args)


class Riding:
    def __init__(self, riders):
        self.riders = [(ex, list(args)) for ex, args in riders]
        self.args = [a for _, args in self.riders for a in args]
        self.in_specs = [_ANY] * len(self.args)
        self.out_shapes = [s for ex, _ in self.riders for s in ex.shapes]
        self.out_specs = [_ANY] * len(self.out_shapes)
        self.scratch = [s for ex, _ in self.riders for s in ex.sems]

    def wrap(self, body, n_in, n_out, n_scratch, is_first, is_last):
        def wrapped(*refs):
            k = 0
            core = list(refs[:n_in])
            k = n_in
            r_in = []
            for ex, _ in self.riders:
                r_in.append(refs[k:k + ex.n])
                k += ex.n
            core += refs[k:k + n_out]
            k += n_out
            r_out = []
            for ex, _ in self.riders:
                r_out.append(refs[k:k + ex.n])
                k += ex.n
            core += refs[k:k + n_scratch]
            k += n_scratch
            r_sem = []
            for ex, _ in self.riders:
                r_sem.append(refs[k:k + len(ex.sems)])
                k += len(ex.sems)

            @pl.when(is_first())
            def _():
                for (ex, _), a, b, s in zip(self.riders, r_in, r_out, r_sem):
                    ex.start(a, b, s)

            body(*core)

            @pl.when(is_last())
            def _():
                for (ex, _), a, b, s in zip(self.riders, r_in, r_out, r_sem):
                    ex.finish(a, b, s)

        return wrapped

    def split(self, outs, n_out):
        core, rest, per = list(outs[:n_out]), list(outs[n_out:]), []
        for ex, _ in self.riders:
            per.append(rest[:ex.n])
            rest = rest[ex.n:]
        return core, per


def pair_sum(big, sib, c, *, tr, name):
    _, R, C = big.shape

    def body(c_ref, a_ref, s_ref, o_ref):
        o_ref[...] = (a_ref[...].astype(F32) + s_ref[...].astype(F32)).astype(o_ref.dtype)

    grid_spec = pltpu.PrefetchScalarGridSpec(
        num_scalar_prefetch=1, grid=(4, R // tr),
        in_specs=[pl.BlockSpec((None, tr, C), lambda b, i, c_ref: (2 * b + c_ref[0], i, 0)),
                  pl.BlockSpec((None, tr, C), lambda b, i, c_ref: (b, i, 0))],
        out_specs=pl.BlockSpec((None, tr, C), lambda b, i, c_ref: (b, i, 0)))
    return pl.pallas_call(body, grid_spec=grid_spec, out_shape=jax.ShapeDtypeStruct((4, R, C), big.dtype),
                          compiler_params=_cparams(2), name=name)(c.reshape(1).astype(jnp.int32), big, sib)


class ChipScatter:
    def __init__(self, pres, chunks):
        self.shapes = [jax.ShapeDtypeStruct(p.shape, p.dtype) for p in pres]
        self.n = len(pres)
        self.pieces = _pieces(pres, chunks)
        n_p = len(self.pieces)
        self.sems = [pltpu.SemaphoreType.DMA((3 * n_p,)), pltpu.SemaphoreType.DMA((3 * n_p,)),
                     pltpu.SemaphoreType.DMA((n_p,))]

    def _copies(self, pre_refs, out_refs, sems):
        send_sems, recv_sems, local_sems = sems
        n_p = len(self.pieces)
        x, y, c = _my_pos()
        my_chip = 2 * x + y
        chips = [(1 - x, y), (x, 1 - y), (1 - x, 1 - y)]

        def copy(j, p, slot):
            px, py = chips[j]
            a, rows = self.pieces[p]
            src_slot, dst_slot = (2 * px + py, my_chip) if slot == "mine" else (my_chip, 2 * px + py)
            return pltpu.make_async_remote_copy(
                src_ref=pre_refs[a].at[src_slot, rows], dst_ref=out_refs[a].at[dst_slot, rows],
                send_sem=send_sems.at[j * n_p + p], recv_sem=recv_sems.at[j * n_p + p],
                device_id=(px, py, c), device_id_type=MESH)

        own = [pltpu.make_async_copy(pre_refs[a].at[my_chip, rows], out_refs[a].at[my_chip, rows], local_sems.at[p])
               for p, (a, rows) in enumerate(self.pieces)]
        sends = [copy(j, p, "mine") for j in range(3) for p in range(n_p)]
        recvs = [copy(j, p, "theirs") for j in range(3) for p in range(n_p)]
        return own, sends, recvs

    def start(self, pre_refs, out_refs, sems):
        own, sends, _ = self._copies(pre_refs, out_refs, sems)
        for cp in sends + own:
            cp.start()

    def finish(self, pre_refs, out_refs, sems):
        own, sends, recvs = self._copies(pre_refs, out_refs, sems)
        for cp in recvs:
            cp.wait_recv()
        for cp in sends:
            cp.wait_send()
        for cp in own:
            cp.wait()


def chip_scatter(pres, chunks, *, name):
    cs = ChipScatter(pres, chunks)
    n = cs.n

    def body(*refs):
        pre_refs, out_refs, sems = refs[:n], refs[n:2 * n], refs[2 * n:]
        cs.start(pre_refs, out_refs, sems)
        cs.finish(pre_refs, out_refs, sems)

    return pl.pallas_call(body, out_shape=cs.shapes, in_specs=[_ANY] * n, out_specs=[_ANY] * n,
                          scratch_shapes=cs.sems, name=name)(*pres)


def sibling_send(blks, chunks, *, name):
    n = len(blks)
    pieces = _pieces(blks, chunks)
    n_p = len(pieces)

    def body(*refs):
        x_refs, out_refs = refs[:n], refs[n:2 * n]
        send_sems, recv_sems = refs[2 * n:]
        x, y, c = _my_pos()
        cps = [pltpu.make_async_remote_copy(
            src_ref=x_refs[a].at[rows], dst_ref=out_refs[a].at[rows], send_sem=send_sems.at[p],
            recv_sem=recv_sems.at[p], device_id=(x, y, 1 - c), device_id_type=MESH)
            for p, (a, rows) in enumerate(pieces)]
        for cp in cps:
            cp.start()
        for cp in cps:
            cp.wait_recv()
        for cp in cps:
            cp.wait_send()

    return pl.pallas_call(
        body, out_shape=[jax.ShapeDtypeStruct(b.shape, b.dtype) for b in blks],
        in_specs=[_ANY] * n, out_specs=[_ANY] * n,
        scratch_shapes=[pltpu.SemaphoreType.DMA((n_p,)), pltpu.SemaphoreType.DMA((n_p,))],
        name=name)(*blks)


def reduce_slots(land, *, tr, name):
    n, R, C = land.shape

    def body(l_ref, o_ref):
        acc = l_ref[0].astype(F32)
        for s in range(1, n):
            acc = acc + l_ref[s].astype(F32)
        o_ref[...] = acc

    return pl.pallas_call(
        body, grid=(R // tr,), in_specs=[pl.BlockSpec((n, tr, C), lambda i: (0, i, 0))],
        out_specs=pl.BlockSpec((tr, C), lambda i: (i, 0)), out_shape=jax.ShapeDtypeStruct((R, C), F32),
        compiler_params=_cparams(1), name=name)(land)


TM = 512


def _tk(d):
    return min(d.shape[0], 1024)


def ffn_fwd(x, g, wgu, wd, tag):
    gu, h = norm_matmul(x, g, wgu, tm=_tk(x), tn=1408, split=True, name=f"{tag}_gu")
    xo = mm_nn(gu, wd, res=x, scale=0.5, swiglu=True, tm=TM, tn=D_MODEL, tk=1408, name=f"{tag}_down")
    return xo, (x, gu, h)


def ffn_bwd(d, saved, g, wgu, wd, tag, before_dx=None):
    x, gu, h = saved
    dgu, act = ffn_bwd_act(d, wd, gu, tm=TM, tn=1408, name=f"{tag}_bact")
    dwd = mm_tn(act, d, scale=0.5, a_split=False, b_split=False, tm=1408, tn=D_MODEL, tk=_tk(d), name=f"{tag}_dwd")
    dwgu = mm_tn(h, dgu, scale=1.0, a_split=False, b_split=True, tm=TM, tn=1408, tk=_tk(d), out_blocked=True,
                 name=f"{tag}_dwgu")
    riding = before_dx(dwgu, dwd) if before_dx is not None else None
    dx, dg, *rode = mm_nt_normbwd(dgu, wgu, x, g, d, a_split=True, tm=_tk(d), tk=1408, name=f"{tag}_dx",
                                  riding=riding)
    return dx, dg, dwgu, dwd, rode


def _tile2(v):
    return jnp.concatenate([v, v], axis=-1).reshape(1, LANES)


def _fold2(v):
    return v[:, :HEAD_DIM] + v[:, HEAD_DIM:]


EVEN = dict(dil=1, nsub=2, ppk=4, q_blk=0, k_blk=4, v_blk=5, n_heads=A_Q_HEADS, group=A_GROUP, max_dist=A_WINDOW - 1)
STICK = dict(q_blk=6, k_blk=10, v_blk=14, n_pairs=4)


def _odd_cfg(dil):
    return dict(dil=dil, nsub=4 if dil == 1 else 1, ppk=1, q_blk=0, k_blk=8, v_blk=16, n_heads=C_HEADS, group=1,
                max_dist=BLK)


def even_fwd(x, g, win, qg, kg, sinks, wout, tag, riders=None, rider_args=()):
    qkv, h = norm_matmul(x, g, win, tm=_tk(x), tn=1152, split=False, name=f"{tag}_in")
    qg2, kg2 = _tile2(qg), _tile2(kg)
    slopes = jnp.asarray(_alibi(A_Q_HEADS), F32)
    qkn = qk_norm(qkv, qg2, kg2, blocks=range(0, 5), n_q=4, tm=TM, name=f"{tag}_qkn")
    oa, lse = banded_fwd(qkn, qkv, slopes, sinks, name=f"{tag}_swa", **EVEN)
    ob, rode = stick_fwd(qkv, name=f"{tag}_stick", riders=riders, rider_args=rider_args, **STICK)
    o = jnp.concatenate([oa, ob], axis=1)
    xo = mm_nn(o, wout, res=x, scale=1.0, swiglu=False, tm=TM, tn=D_MODEL, tk=D_MODEL, name=f"{tag}_out")
    return xo, (x, qkv, qkn, h, oa, lse, o), rode


def even_bwd(d, saved, g, win, qg, kg, sinks, wout, tag, riders=None, rider_args=()):
    x, qkv, qkn, h, oa, lse, o = saved
    qg2, kg2 = _tile2(qg), _tile2(kg)
    slopes = jnp.asarray(_alibi(A_Q_HEADS), F32)
    dwout = mm_tn(o, d, scale=1.0, a_split=False, b_split=False, tm=D_MODEL, tn=D_MODEL, tk=_tk(d), name=f"{tag}_dwout")
    do = mm_nt(d, wout, tm=TM, tn=D_MODEL, tk=D_MODEL, name=f"{tag}_do")
    dqa, dka4, dva4, dsk = banded_bwd(qkn, qkv, slopes, sinks, do, oa, lse, None, None,
                                      do_blk=0, name=f"{tag}_swa_b", **EVEN)
    dqb, dkb, dvb, rode = stick_bwd(qkv, do, do_blk=4, name=f"{tag}_stick_b", riders=riders, rider_args=rider_args,
                                    **STICK)
    dqkv, dqg, dkg = assemble_even(dqa, dka4, dva4, dqb, dkb, dvb, qkv, qg2, kg2, tm=TM, name=f"{tag}_asm")
    dwin = mm_tn(h, dqkv, scale=1.0, a_split=False, b_split=False, tm=D_MODEL, tn=1152, tk=_tk(d), name=f"{tag}_dwin")
    dx, dg = mm_nt_normbwd(dqkv, win, x, g, d, a_split=False, tm=TM, tk=1152, name=f"{tag}_dx")
    return dx, dg, dwin, _fold2(dqg), _fold2(dkg), dsk[:, :A_Q_HEADS], dwout, rode


def odd_fwd(x, g, win, qg, kg, wout, tag):
    qkv, h = norm_matmul(x, g, win, tm=_tk(x), tn=768, split=False, name=f"{tag}_in")
    qg2, kg2 = _tile2(qg), _tile2(kg)
    qkn = qk_norm(qkv, qg2, kg2, blocks=range(0, 16), n_q=8, tm=TM, name=f"{tag}_qkn")
    outs = []
    for p, (window, dil) in enumerate(C_PATTERNS):
        slopes = jnp.asarray(_alibi(C_HEADS), F32) * float(dil)
        outs.append(banded_fwd(qkn, qkv, slopes, None, name=f"{tag}_dil{p}", **_odd_cfg(dil)))
    o, w1, w2, w3 = mix_fwd(outs[0][0], outs[1][0], outs[2][0], outs[0][1], outs[1][1], outs[2][1],
                            tm=TM, name=f"{tag}_mix")
    xo = mm_nn(o, wout, res=x, scale=1.0, swiglu=False, tm=TM, tn=D_MODEL, tk=D_MODEL, name=f"{tag}_out")
    return xo, (x, qkv, qkn, h, outs, (w1, w2, w3), o)


def odd_bwd(d, saved, g, win, qg, kg, wout, tag):
    x, qkv, qkn, h, outs, ws, o = saved
    qg2, kg2 = _tile2(qg), _tile2(kg)
    dwout = mm_tn(o, d, scale=1.0, a_split=False, b_split=False, tm=D_MODEL, tn=D_MODEL, tk=_tk(d), name=f"{tag}_dwout")
    do = mm_nt(d, wout, tm=TM, tn=D_MODEL, tk=D_MODEL, name=f"{tag}_do")
    parts = []
    for p, (window, dil) in enumerate(C_PATTERNS):
        slopes = jnp.asarray(_alibi(C_HEADS), F32) * float(dil)
        dq, dk, dv, _ = banded_bwd(qkn, qkv, slopes, None, do, outs[p][0], outs[p][1], ws[p], o,
                                   do_blk=0, name=f"{tag}_dil{p}_b", **_odd_cfg(dil))
        parts.append((dq, dk, dv))
    dqkv, dqg, dkg = assemble_odd(parts, qkv, qg2, kg2, tm=256, name=f"{tag}_asm")
    dwin = mm_tn(h, dqkv, scale=1.0, a_split=False, b_split=False, tm=TM, tn=768, tk=_tk(d), out_blocked=True,
                 name=f"{tag}_dwin")
    dx, dg = mm_nt_normbwd(dqkv, win, x, g, d, a_split=False, tm=TM, tk=768, name=f"{tag}_dx")
    return dx, dg, dwin, _fold2(dqg), _fold2(dkg), dwout


def xa_fwd(x, mem, g, gm, wq, wkv, qg, kg, wo, tag):
    qraw, h = norm_matmul(x, g, wq, tm=TM, tn=D_MODEL, split=False, name=f"{tag}_q")
    kvraw, hm = norm_matmul(mem, gm, wkv, tm=MEM_LEN, tn=512, split=False, name=f"{tag}_kv")
    o = xattn_fwd(qraw, kvraw, qg, kg, tm=TM, name=f"{tag}_att")
    xo = mm_nn(o, wo, res=x, scale=1.0, swiglu=False, tm=TM, tn=D_MODEL, tk=D_MODEL, name=f"{tag}_o")
    return xo, (x, qraw, h, kvraw, hm, o)


def xa_bwd(d, saved, mem, g, gm, wq, wkv, qg, kg, wo, tag):
    x, qraw, h, kvraw, hm, o = saved
    dwo = mm_tn(o, d, scale=1.0, a_split=False, b_split=False, tm=D_MODEL, tn=D_MODEL, tk=_tk(d), name=f"{tag}_dwo")
    do = mm_nt(d, wo, tm=TM, tn=D_MODEL, tk=D_MODEL, name=f"{tag}_do")
    dq, dkv, dqg, dkg = xattn_bwd(qraw, kvraw, qg, kg, do, o, tm=TM, name=f"{tag}_att_b")
    dwq = mm_tn(h, dq, scale=1.0, a_split=False, b_split=False, tm=D_MODEL, tn=D_MODEL, tk=_tk(d), name=f"{tag}_dwq")
    dx, dg = mm_nt_normbwd(dq, wq, x, g, d, a_split=False, tm=TM, tk=D_MODEL, name=f"{tag}_dx")
    dwkv = mm_tn(hm, dkv, scale=1.0, a_split=False, b_split=False, tm=TM, tn=512, tk=MEM_LEN, out_blocked=True,
                 name=f"{tag}_dwkv")
    _, dgm = mm_nt_normbwd(dkv, wkv, mem, gm, None, a_split=False, tm=MEM_LEN, tk=512, name=f"{tag}_dmem")
    return dx, dg, dgm, dwq, dwkv, dqg, dkg, dwo


MATS = (("ffn1_w_gu", 1), ("ffn1_w_down", 0), ("ev_w_in", 1), ("ev_w_out", 0), ("od_w_in", 1), ("od_w_out", 0),
        ("xa_w_q", 0), ("xa_w_kv", 1), ("xa_w_o", 0), ("ffn2_w_gu", 1), ("ffn2_w_down", 0))
SMALLS = ("ffn1_norm", "mix_norm", "ev_q_gain", "ev_k_gain", "ev_sinks", "od_q_gain", "od_k_gain", "xa_norm",
          "xa_mem_norm", "xa_q_gain", "xa_k_gain", "ffn2_norm")
WEIGHTS = ("ffn1_norm", "ffn1_w_gu", "ffn1_w_down", "mix_norm", "ev_w_in", "ev_q_gain", "ev_k_gain", "ev_sinks",
           "ev_w_out", "od_w_in", "od_q_gain", "od_k_gain", "od_w_out", "xa_norm", "xa_mem_norm", "xa_w_q",
           "xa_w_kv", "xa_q_gain", "xa_k_gain", "xa_w_o", "ffn2_norm", "ffn2_w_gu", "ffn2_w_down")
SMALL_ROWS = 16
LAYER_GROUPS = (
    (((("ffn1_w_gu", 0), ("ffn2_w_gu", 0)), 4, 512),
     ((("ffn1_w_down", 0), ("ffn2_w_down", 0)), 2, 352),
     ((("ev_w_out", 0), ("xa_w_q", 0), ("xa_w_o", 0)), 1, 384),
     ((("xa_w_kv", 0),), 1, 512),
     ((("ev_w_in", 0),), 1, 512)),
    (((("ffn1_w_gu", 1), ("ffn2_w_gu", 1)), 4, 512),
     ((("ffn1_w_down", 1), ("ffn2_w_down", 1)), 2, 352),
     ((("od_w_out", 0), ("xa_w_q", 1), ("xa_w_o", 1)), 1, 384),
     ((("xa_w_kv", 1),), 1, 512),
     ((("od_w_in", 0),), 1, 512)),
)
GROUPS = LAYER_GROUPS[0] + LAYER_GROUPS[1]


def _chunks_of(groups):
    return tuple(g[1] for g in groups)
COL_SHARDED = {name for name, axis in MATS if axis == 1}
BLOCKED = {"ffn1_w_gu", "ffn2_w_gu", "xa_w_kv", "od_w_in"}


def group_halves(shards, c, groups):
    out = []
    for members, _, _ in groups:
        halves = []
        for name, layer in members:
            _, r, cc = shards[name].shape
            half = lax.dynamic_index_in_dim(shards[name][layer].reshape(2, r // 2, cc), c, 0, keepdims=False)
            halves.append(half.astype(BF16))
        out.append(jnp.concatenate(halves, axis=0))
    return out


def full_weights(gathered, shards, groups):
    full = {}
    for (members, _, _), arr in zip(groups, gathered):
        for w, (name, layer) in enumerate(members):
            _, r, cc = shards[name].shape
            piece = arr[:, w * (r // 2):(w + 1) * (r // 2)].reshape(4, r, cc)
            if name not in COL_SHARDED:
                piece = piece.reshape(4 * r, cc)
            elif name not in BLOCKED:
                piece = piece.transpose(1, 0, 2).reshape(r, 4 * cc)
            full[(name, layer)] = piece
    return full


def group_grads(grads, shards, groups):
    out = []
    for members, _, _ in groups:
        parts = []
        for name, layer in members:
            _, r, cc = shards[name].shape
            gfull = grads[(name, layer)]
            if name in COL_SHARDED and name not in BLOCKED:
                gfull = gfull.reshape(2, r // 2, 4, cc).transpose(2, 0, 1, 3)
            parts.append(gfull.reshape(N_DEV, r // 2, cc))
        out.append(jnp.concatenate(parts, axis=1))
    return out


def shard_grads(mine, theirs, c, shards, groups):
    per = {}
    for (members, _, _), a, b in zip(groups, mine, theirs):
        for w, (name, layer) in enumerate(members):
            _, r, cc = shards[name].shape
            rows = slice(w * (r // 2), (w + 1) * (r // 2))
            lo = jnp.where(c == 0, a[rows], b[rows])
            hi = jnp.where(c == 0, b[rows], a[rows])
            per[(name, layer)] = jnp.concatenate([lo, hi], axis=0)
    return per


def pack_small(vals):
    row10 = jnp.concatenate([vals["xa_q_gain"].reshape(1, 512), vals["xa_k_gain"].reshape(1, 512)], axis=1)
    row11 = jnp.concatenate([vals["ev_q_gain"], vals["ev_k_gain"], vals["od_q_gain"], vals["od_k_gain"],
                             vals["ev_sinks"], jnp.zeros((1, 1024 - 4 * 64 - 8), F32)], axis=1)
    return jnp.concatenate([vals["ffn1_norm"], vals["mix_norm"], vals["xa_norm"], vals["xa_mem_norm"],
                            vals["ffn2_norm"], row10, row11, jnp.zeros((SMALL_ROWS - 12, 1024), F32)], axis=0)


def unpack_small(arr):
    return {"ffn1_norm": arr[0:2], "mix_norm": arr[2:4], "xa_norm": arr[4:6], "xa_mem_norm": arr[6:8],
            "ffn2_norm": arr[8:10],
            "xa_q_gain": arr[10:11, 0:512].reshape(2, 256), "xa_k_gain": arr[10:11, 512:1024].reshape(2, 256),
            "ev_q_gain": arr[11:12, 0:64], "ev_k_gain": arr[11:12, 64:128], "od_q_gain": arr[11:12, 128:192],
            "od_k_gain": arr[11:12, 192:256], "ev_sinks": arr[11:12, 256:264]}


def local_step(x, mem, target, W, small, prereduce, later):
    depth = small["ffn1_norm"].shape[0]

    def row(name, l):
        return small[name][l:l + 1]

    saved = []
    for l in range(depth):
        j = l // 2
        x, s1 = ffn_fwd(x, row("ffn1_norm", l), W[("ffn1_w_gu", l)], W[("ffn1_w_down", l)], f"l{l}_f1")
        if l % 2 == 0:
            riders, rider_args = None, ()
            if later is not None and l == 0:
                riders, rider_args = GatherBlocks(later[0], later[1]), later[0]
            x, s2, rode = even_fwd(x, row("mix_norm", l), W[("ev_w_in", j)], row("ev_q_gain", j),
                                   row("ev_k_gain", j), small["ev_sinks"][j], W[("ev_w_out", j)], f"l{l}_ev",
                                   riders=riders, rider_args=rider_args)
            if riders is not None:
                W = {**W, **later[2](rode)}
        else:
            x, s2 = odd_fwd(x, row("mix_norm", l), W[("od_w_in", j)], row("od_q_gain", j), row("od_k_gain", j),
                            W[("od_w_out", j)], f"l{l}_od")
        x, s3 = xa_fwd(x, mem, row("xa_norm", l), row("xa_mem_norm", l), W[("xa_w_q", l)], W[("xa_w_kv", l)],
                       row("xa_q_gain", l), row("xa_k_gain", l), W[("xa_w_o", l)], f"l{l}_xa")
        x, s4 = ffn_fwd(x, row("ffn2_norm", l), W[("ffn2_w_gu", l)], W[("ffn2_w_down", l)], f"l{l}_f2")
        saved.append((s1, s2, s3, s4))
    loss, d = loss_kernel(x, target, tm=TM, name="loss")

    gw = {}
    gs = {name: [None] * small[name].shape[0] for name in SMALLS}
    pending, landed = None, {}
    for l in reversed(range(depth)):
        j = l // 2
        s1, s2, s3, s4 = saved[l]
        d, dg, dwgu, dwd, _ = ffn_bwd(d, s4, row("ffn2_norm", l), W[("ffn2_w_gu", l)], W[("ffn2_w_down", l)],
                                      f"l{l}_f2")
        gs["ffn2_norm"][l] = dg
        gw[("ffn2_w_gu", l)], gw[("ffn2_w_down", l)] = dwgu, dwd
        d, dg, dgm, dwq, dwkv, dqg, dkg, dwo = xa_bwd(
            d, s3, mem, row("xa_norm", l), row("xa_mem_norm", l), W[("xa_w_q", l)], W[("xa_w_kv", l)],
            row("xa_q_gain", l), row("xa_k_gain", l), W[("xa_w_o", l)], f"l{l}_xa")
        gs["xa_norm"][l], gs["xa_mem_norm"][l], gs["xa_q_gain"][l], gs["xa_k_gain"][l] = dg, dgm, dqg, dkg
        gw[("xa_w_q", l)], gw[("xa_w_kv", l)], gw[("xa_w_o", l)] = dwq, dwkv, dwo
        if l % 2 == 0:
            riders, rider_args = None, ()
            if pending is not None:
                riders, rider_args = ChipScatter(pending[1], _chunks_of(LAYER_GROUPS[pending[0]])), pending[1]
            d, dg, dwin, dqg, dkg, dsk, dwout, rode = even_bwd(
                d, s2, row("mix_norm", l), W[("ev_w_in", j)], row("ev_q_gain", j), row("ev_k_gain", j),
                small["ev_sinks"][j], W[("ev_w_out", j)], f"l{l}_ev", riders=riders, rider_args=rider_args)
            if pending is not None:
                landed[pending[0]], pending = rode, None
            gs["ev_q_gain"][j], gs["ev_k_gain"][j], gs["ev_sinks"][j] = dqg, dkg, dsk
            gw[("ev_w_in", j)], gw[("ev_w_out", j)] = dwin, dwout
        else:
            d, dg, dwin, dqg, dkg, dwout = odd_bwd(
                d, s2, row("mix_norm", l), W[("od_w_in", j)], row("od_q_gain", j), row("od_k_gain", j),
                W[("od_w_out", j)], f"l{l}_od")
            gs["od_q_gain"][j], gs["od_k_gain"][j] = dqg, dkg
            gw[("od_w_in", j)], gw[("od_w_out", j)] = dwin, dwout
        gs["mix_norm"][l] = dg
        packed = []

        def before_dx(dwgu, dwd, l=l, packed=packed):
            gw[("ffn1_w_gu", l)], gw[("ffn1_w_down", l)] = dwgu, dwd
            packed += prereduce.pack(gw, l)
            return Riding([(PairExchange(packed, _chunks_of(LAYER_GROUPS[l])), packed)])

        d, dg, dwgu, dwd, rode = ffn_bwd(d, s1, row("ffn1_norm", l), W[("ffn1_w_gu", l)], W[("ffn1_w_down", l)],
                                         f"l{l}_f1", before_dx=before_dx)
        gs["ffn1_norm"][l] = dg
        if pending is not None:
            landed[pending[0]] = chip_scatter(pending[1], _chunks_of(LAYER_GROUPS[pending[0]]),
                                              name=f"scatter_grads{pending[0]}")
        pending = (l, prereduce.sums(packed, rode[0], l))
    landed[pending[0]] = chip_scatter(pending[1], _chunks_of(LAYER_GROUPS[pending[0]]), name=f"scatter_grads{pending[0]}")
    gsmall = {name: jnp.concatenate(v, axis=0) for name, v in gs.items()}
    return loss, d, landed, gsmall


def kernel(x, mem, ffn1_norm, ffn1_w_gu, ffn1_w_down, mix_norm, ev_w_in, ev_q_gain, ev_k_gain, ev_sinks, ev_w_out, od_w_in, od_q_gain, od_k_gain, od_w_out, xa_norm, xa_mem_norm, xa_w_q, xa_w_kv, xa_q_gain, xa_k_gain, xa_w_o, ffn2_norm, ffn2_w_gu, ffn2_w_down, loss_target, m_ffn1_norm, m_ffn1_w_gu, m_ffn1_w_down, m_mix_norm, m_ev_w_in, m_ev_q_gain, m_ev_k_gain, m_ev_sinks, m_ev_w_out, m_od_w_in, m_od_q_gain, m_od_k_gain, m_od_w_out, m_xa_norm, m_xa_mem_norm, m_xa_w_q, m_xa_w_kv, m_xa_q_gain, m_xa_k_gain, m_xa_w_o, m_ffn2_norm, m_ffn2_w_gu, m_ffn2_w_down, v_ffn1_norm, v_ffn1_w_gu, v_ffn1_w_down, v_mix_norm, v_ev_w_in, v_ev_q_gain, v_ev_k_gain, v_ev_sinks, v_ev_w_out, v_od_w_in, v_od_q_gain, v_od_k_gain, v_od_w_out, v_xa_norm, v_xa_mem_norm, v_xa_w_q, v_xa_w_kv, v_xa_q_gain, v_xa_k_gain, v_xa_w_o, v_ffn2_norm, v_ffn2_w_gu, v_ffn2_w_down):
    given = dict(locals())
    w = {n: given[n] for n in WEIGHTS}
    m = {n: given["m_" + n] for n in WEIGHTS}
    v = {n: given["v_" + n] for n in WEIGHTS}
    c = lax.axis_index("c")
    shards = {name: w[name] for name, _ in MATS}
    small = {n: w[n] for n in SMALLS}

    groups0, groups1 = LAYER_GROUPS
    gathered = gather_blocks(group_halves(shards, c, groups0), _chunks_of(groups0), name="gather_weights0")
    full = full_weights(gathered, shards, groups0)
    later = (group_halves(shards, c, groups1), _chunks_of(groups1), lambda got: full_weights(got, shards, groups1))

    class prereduce:
        @staticmethod
        def pack(gw, layer):
            return group_grads(gw, shards, LAYER_GROUPS[layer])

        @staticmethod
        def sums(packed, sib, layer):
            return [pair_sum(p, s, c, tr=g[2], name=f"pair_sum{layer}_{i}")
                    for i, (g, p, s) in enumerate(zip(LAYER_GROUPS[layer], packed, sib))]

    loss_b, grad_x, landed, gsmall = local_step(x[0], mem[0], loss_target[0], full, small, prereduce, later)

    per = {}
    for layer, land in sorted(landed.items()):
        groups = LAYER_GROUPS[layer]
        mine = [reduce_slots(a, tr=g[2], name=f"sum_grads{layer}_{i}") for i, (g, a) in enumerate(zip(groups, land))]
        theirs = sibling_send(mine, _chunks_of(groups), name=f"swap_grads{layer}")
        per.update(shard_grads(mine, theirs, c, shards, groups))
    g = {name: jnp.stack([per[(name, layer)] for layer in range(w[name].shape[0])], axis=0) for name, _ in MATS}
    land_small = gather_small(pack_small(gsmall), name="gather_small")
    g_small = unpack_small(reduce_slots(land_small, tr=SMALL_ROWS, name="sum_small"))
    g.update(g_small)

    delta, new_m, new_v = {}, {}, {}
    for name, _ in MATS:
        shp = w[name].shape
        flat = [a.reshape(-1, shp[-1]) for a in (w[name], g[name], m[name], v[name])]
        dl, nm, nv = adamw(*flat, br=BLK, name=f"adamw_{name}")
        delta[name], new_m[name], new_v[name] = dl.reshape(shp), nm.reshape(shp), nv.reshape(shp)
    dl, nm, nv = adamw(pack_small(small), pack_small(g_small), pack_small({n: m[n] for n in SMALLS}),
                       pack_small({n: v[n] for n in SMALLS}), br=SMALL_ROWS, name="adamw_small")
    for dst, arr in ((delta, dl), (new_m, nm), (new_v, nv)):
        dst.update(unpack_small(arr))

    loss = lax.psum(loss_b[0, 0], ("x", "y", "c"))
    return (loss, grad_x[None], *[g[n] for n in WEIGHTS], *[delta[n] for n in WEIGHTS],
            *[new_m[n] for n in WEIGHTS], *[new_v[n] for n in WEIGHTS])
```

```python
import jax
import jax.numpy as jnp
from jax import lax
from jax.experimental import pallas as pl
from jax.experimental.pallas import tpu as pltpu

F32 = jnp.float32
BF16 = jnp.bfloat16

D_MODEL = 1024
HEAD_DIM = 64
LANES = 128
BLK = 128
D_FF = 2816
RMS_EPS = 1e-6
MEM_LEN = 256
X_HEADS = 4
X_HEAD_DIM = 256
A_Q_HEADS = 8
A_GROUP = 4
A_WINDOW = 128
C_HEADS = 16
C_PATTERNS = ((128, 1), (512, 4), (2048, 16))
NEG = -1e30
VMEM_LIMIT = 56 * 2 ** 20

ADAM_LR = 0.001
ADAM_B1 = 0.9
ADAM_B2 = 0.999
ADAM_EPS = 1e-08
ADAM_WD = 0.01
ADAM_STEP = 10

N_DEV = 8
MESH = pl.DeviceIdType.MESH


def _cparams(n):
    return pltpu.CompilerParams(dimension_semantics=("arbitrary",) * n, vmem_limit_bytes=VMEM_LIMIT)


def _dot(a, b):
    return jnp.dot(a, b, preferred_element_type=F32)


def _dot_nt(a, b):
    return lax.dot_general(a, b, (((1,), (1,)), ((), ())), preferred_element_type=F32)


def _dot_tn(a, b):
    return lax.dot_general(a, b, (((0,), (0,)), ((), ())), preferred_element_type=F32)


def _sigmoid(z):
    return 1.0 / (1.0 + jnp.exp(-z))


def norm_matmul(x, g, w, *, tm, tn, split, name):
    T, K = x.shape
    blocked = w.ndim == 3
    assert not blocked or w.shape[2] == tn
    N = w.shape[0] * w.shape[2] if blocked else w.shape[1]
    nj = N // tn

    def body(x_ref, g_ref, w_ref, o_ref, h_ref):
        @pl.when(pl.program_id(1) == 0)
        def _():
            xv = x_ref[...]
            r = lax.rsqrt(jnp.mean(xv * xv, axis=-1, keepdims=True) + RMS_EPS)
            h_ref[...] = (xv * r * g_ref[...]).astype(BF16)

        o_ref[...] = _dot(h_ref[...], w_ref[...]).astype(o_ref.dtype)

    if split:
        njh = nj // 2
        o_shape = jax.ShapeDtypeStruct((2, T, N // 2), BF16)
        o_spec = pl.BlockSpec((None, tm, tn), lambda i, j: (j // njh, i, j % njh))
    else:
        o_shape = jax.ShapeDtypeStruct((T, N), F32)
        o_spec = pl.BlockSpec((tm, tn), lambda i, j: (i, j))
    return pl.pallas_call(
        body, grid=(T // tm, nj),
        in_specs=[pl.BlockSpec((tm, K), lambda i, j: (i, 0)),
                  pl.BlockSpec((1, K), lambda i, j: (0, 0)),
                  (pl.BlockSpec((None, K, tn), lambda i, j: (j, 0, 0)) if blocked
                   else pl.BlockSpec((K, tn), lambda i, j: (0, j)))],
        out_specs=[o_spec, pl.BlockSpec((tm, K), lambda i, j: (i, 0))],
        out_shape=[o_shape, jax.ShapeDtypeStruct((T, K), BF16)],
        compiler_params=_cparams(2), name=name)(x, g, w)


def mm_nn(a, b, *, res, scale, swiglu, tm, tn, tk, name):
    T = a.shape[-2]
    K, N = b.shape
    nk = K // tk

    def body(*refs):
        if swiglu:
            g_ref, u_ref, b_ref, r_ref, o_ref, acc = refs
        else:
            a_ref, b_ref, r_ref, o_ref, acc = refs
        k = pl.program_id(2)

        @pl.when(k == 0)
        def _():
            acc[...] = jnp.zeros_like(acc)

        if swiglu:
            gv = g_ref[...].astype(F32)
            av = (gv * _sigmoid(gv) * u_ref[...].astype(F32)).astype(BF16)
        else:
            av = a_ref[...].astype(BF16)
        acc[...] += _dot(av, b_ref[...])

        @pl.when(k == nk - 1)
        def _():
            o_ref[...] = r_ref[...] + scale * acc[...]

    if swiglu:
        a_specs = [pl.BlockSpec((None, tm, tk), lambda i, j, k: (0, i, k)),
                   pl.BlockSpec((None, tm, tk), lambda i, j, k: (1, i, k))]
        a_args = [a, a]
    else:
        a_specs = [pl.BlockSpec((tm, tk), lambda i, j, k: (i, k))]
        a_args = [a]
    return pl.pallas_call(
        body, grid=(T // tm, N // tn, nk),
        in_specs=a_specs + [pl.BlockSpec((tk, tn), lambda i, j, k: (k, j)),
                            pl.BlockSpec((tm, tn), lambda i, j, k: (i, j))],
        out_specs=pl.BlockSpec((tm, tn), lambda i, j, k: (i, j)),
        out_shape=jax.ShapeDtypeStruct((T, N), F32),
        scratch_shapes=[pltpu.VMEM((tm, tn), F32)],
        compiler_params=_cparams(3), name=name)(*a_args, b, res)


def mm_nt(a, b, *, tm, tn, tk, name):
    T, K = a.shape
    N = b.shape[0]
    nk = K // tk

    def body(a_ref, b_ref, o_ref, acc):
        k = pl.program_id(2)

        @pl.when(k == 0)
        def _():
            acc[...] = jnp.zeros_like(acc)

        acc[...] += _dot_nt(a_ref[...].astype(BF16), b_ref[...])

        @pl.when(k == nk - 1)
        def _():
            o_ref[...] = acc[...]

    return pl.pallas_call(
        body, grid=(T // tm, N // tn, nk),
        in_specs=[pl.BlockSpec((tm, tk), lambda i, j, k: (i, k)),
                  pl.BlockSpec((tn, tk), lambda i, j, k: (j, k))],
        out_specs=pl.BlockSpec((tm, tn), lambda i, j, k: (i, j)),
        out_shape=jax.ShapeDtypeStruct((T, N), F32),
        scratch_shapes=[pltpu.VMEM((tm, tn), F32)],
        compiler_params=_cparams(3), name=name)(a, b)


def ffn_bwd_act(d, wd, gu, *, tm, tn, name, riding=None):
    T, K = d.shape
    Fd = wd.shape[0]

    def body(d_ref, w_ref, g_ref, u_ref, dgu_ref, act_ref):
        da = 0.5 * _dot_nt(d_ref[...].astype(BF16), w_ref[...])
        gv = g_ref[...].astype(F32)
        uv = u_ref[...].astype(F32)
        s = _sigmoid(gv)
        silu = gv * s
        act_ref[...] = (silu * uv).astype(BF16)
        dgu_ref[0] = (da * uv * (s * (1.0 + gv * (1.0 - s)))).astype(BF16)
        dgu_ref[1] = (da * silu).astype(BF16)

    in_specs = [pl.BlockSpec((tm, K), lambda j, i: (i, 0)),
                pl.BlockSpec((tn, K), lambda j, i: (j, 0)),
                pl.BlockSpec((None, tm, tn), lambda j, i: (0, i, j)),
                pl.BlockSpec((None, tm, tn), lambda j, i: (1, i, j))]
    out_specs = [pl.BlockSpec((2, tm, tn), lambda j, i: (0, i, j)), pl.BlockSpec((tm, tn), lambda j, i: (i, j))]
    out_shape = [jax.ShapeDtypeStruct((2, T, Fd), BF16), jax.ShapeDtypeStruct((T, Fd), BF16)]
    return _call_with_riders(body, riding, (Fd // tn, T // tm), in_specs, out_specs, out_shape, [],
                             [d, wd, gu, gu], name)


def _call_with_riders(body, riding, grid, in_specs, out_specs, out_shape, scratch, args, name):
    n_out = len(out_shape)
    if riding is None:
        return pl.pallas_call(body, grid=grid, in_specs=in_specs, out_specs=out_specs, out_shape=out_shape,
                              scratch_shapes=scratch, compiler_params=_cparams(len(grid)), name=name)(*args)

    def is_first():
        ok = pl.program_id(0) == 0
        for ax in range(1, len(grid)):
            ok = ok & (pl.program_id(ax) == 0)
        return ok

    def is_last():
        ok = pl.program_id(0) == grid[0] - 1
        for ax in range(1, len(grid)):
            ok = ok & (pl.program_id(ax) == grid[ax] - 1)
        return ok

    outs = pl.pallas_call(
        riding.wrap(body, len(in_specs), n_out, len(scratch), is_first, is_last), grid=grid,
        in_specs=list(in_specs) + riding.in_specs, out_specs=list(out_specs) + riding.out_specs,
        out_shape=list(out_shape) + riding.out_shapes, scratch_shapes=list(scratch) + riding.scratch,
        compiler_params=_cparams(len(grid)), name=name)(*args, *riding.args)
    core, per = riding.split(outs, n_out)
    return (*core, *per)


def mm_nt_normbwd(a, b, x, g, res, *, a_split, tm, tk, name, riding=None):
    T, Dm = x.shape
    blocked = b.ndim == 3
    assert not blocked or b.shape[2] == tk
    K = b.shape[0] * b.shape[2] if blocked else b.shape[1]
    nk = K // tk
    nkh = nk // 2
    has_res = res is not None

    def body(*refs):
        if has_res:
            a_ref, b_ref, x_ref, g_ref, r_ref, dx_ref, dg_ref, acc = refs
        else:
            a_ref, b_ref, x_ref, g_ref, dx_ref, dg_ref, acc = refs
        i = pl.program_id(0)
        k = pl.program_id(1)

        @pl.when(k == 0)
        def _():
            acc[...] = jnp.zeros_like(acc)

        acc[...] += _dot_nt(a_ref[...].astype(BF16), b_ref[...])

        @pl.when(k == nk - 1)
        def _():
            xv = x_ref[...]
            r = lax.rsqrt(jnp.mean(xv * xv, axis=-1, keepdims=True) + RMS_EPS)
            xh = xv * r
            dh = acc[...]
            dxh = dh * g_ref[...]
            dx = r * (dxh - xh * jnp.mean(dxh * xh, axis=-1, keepdims=True))
            if has_res:
                dx = dx + r_ref[...]
            dx_ref[...] = dx
            part = jnp.sum(dh * xh, axis=0, keepdims=True)

            @pl.when(i == 0)
            def _():
                dg_ref[...] = part

            @pl.when(i > 0)
            def _():
                dg_ref[...] += part

    if a_split:
        a_spec = pl.BlockSpec((None, tm, tk), lambda i, k: (k // nkh, i, k % nkh))
    else:
        a_spec = pl.BlockSpec((tm, tk), lambda i, k: (i, k))
    in_specs = [a_spec,
                (pl.BlockSpec((None, Dm, tk), lambda i, k: (k, 0, 0)) if blocked
                 else pl.BlockSpec((Dm, tk), lambda i, k: (0, k))),
                pl.BlockSpec((tm, Dm), lambda i, k: (i, 0)),
                pl.BlockSpec((1, Dm), lambda i, k: (0, 0))]
    args = [a, b, x, g]
    if has_res:
        in_specs.append(pl.BlockSpec((tm, Dm), lambda i, k: (i, 0)))
        args.append(res)
    out_specs = [pl.BlockSpec((tm, Dm), lambda i, k: (i, 0)), pl.BlockSpec((1, Dm), lambda i, k: (0, 0))]
    out_shape = [jax.ShapeDtypeStruct((T, Dm), F32), jax.ShapeDtypeStruct((1, Dm), F32)]
    scratch = [pltpu.VMEM((tm, Dm), F32)]
    return _call_with_riders(body, riding, (T // tm, nk), in_specs, out_specs, out_shape, scratch, args, name)


def mm_tn(a, b, *, scale, a_split, b_split, tm, tn, tk, name, out_blocked=False, riding=None):
    T = a.shape[-2]
    M = a.shape[-1] * (2 if a_split else 1)
    N = b.shape[-1] * (2 if b_split else 1)
    ni, nj, nk = M // tm, N // tn, T // tk
    nih, njh = ni // 2, nj // 2

    def body(a_ref, b_ref, o_ref, acc):
        k = pl.program_id(2)

        @pl.when(k == 0)
        def _():
            acc[...] = jnp.zeros_like(acc)

        acc[...] += _dot_tn(a_ref[...].astype(BF16), b_ref[...].astype(BF16))

        @pl.when(k == nk - 1)
        def _():
            o_ref[...] = (acc[...] * scale).astype(o_ref.dtype)

    if a_split:
        a_spec = pl.BlockSpec((None, tk, tm), lambda i, j, k: (i // nih, k, i % nih))
    else:
        a_spec = pl.BlockSpec((tk, tm), lambda i, j, k: (k, i))
    if b_split:
        b_spec = pl.BlockSpec((None, tk, tn), lambda i, j, k: (j // njh, k, j % njh))
    else:
        b_spec = pl.BlockSpec((tk, tn), lambda i, j, k: (k, j))
    if out_blocked:
        o_spec = pl.BlockSpec((None, None, tm, tn), lambda i, j, k: (j, i, 0, 0))
        o_shape = jax.ShapeDtypeStruct((nj, ni, tm, tn), BF16)
    else:
        o_spec = pl.BlockSpec((tm, tn), lambda i, j, k: (i, j))
        o_shape = jax.ShapeDtypeStruct((M, N), BF16)
    outs = _call_with_riders(body, riding, (ni, nj, nk), [a_spec, b_spec], [o_spec], [o_shape],
                             [pltpu.VMEM((tm, tn), F32)], [a, b], name)
    return outs[0] if riding is None else tuple(outs)


def loss_kernel(y, target, *, tm, name):
    T, Dm = y.shape

    def body(y_ref, t_ref, l_ref, dy_ref):
        e = y_ref[...] - t_ref[...]
        dy_ref[...] = e * (1.0 / Dm)
        part = (0.5 / Dm) * jnp.sum(jnp.sum(e * e, axis=-1, keepdims=True), axis=0, keepdims=True)
        part = jnp.broadcast_to(part, (8, LANES))

        @pl.when(pl.program_id(0) == 0)
        def _():
            l_ref[...] = part

        @pl.when(pl.program_id(0) > 0)
        def _():
            l_ref[...] += part

    return pl.pallas_call(
        body, grid=(T // tm,),
        in_specs=[pl.BlockSpec((tm, Dm), lambda i: (i, 0)), pl.BlockSpec((tm, Dm), lambda i: (i, 0))],
        out_specs=[pl.BlockSpec((8, LANES), lambda i: (0, 0)), pl.BlockSpec((tm, Dm), lambda i: (i, 0))],
        out_shape=[jax.ShapeDtypeStruct((8, LANES), F32), jax.ShapeDtypeStruct((T, Dm), F32)],
        compiler_params=_cparams(1), name=name)(y, target)


def adamw(w, g, m, v, *, br, name):
    R, C = w.shape

    def body(w_ref, g_ref, m_ref, v_ref, d_ref, nm_ref, nv_ref):
        gv = g_ref[...]
        nm = ADAM_B1 * m_ref[...] + (1.0 - ADAM_B1) * gv
        nv = ADAM_B2 * v_ref[...] + (1.0 - ADAM_B2) * (gv * gv)
        m_hat = nm / (1.0 - ADAM_B1 ** ADAM_STEP)
        v_hat = nv / (1.0 - ADAM_B2 ** ADAM_STEP)
        d_ref[...] = -ADAM_LR * (m_hat / (jnp.sqrt(v_hat) + ADAM_EPS) + ADAM_WD * w_ref[...])
        nm_ref[...] = nm
        nv_ref[...] = nv

    spec = pl.BlockSpec((br, C), lambda i: (i, 0))
    shp = jax.ShapeDtypeStruct((R, C), F32)
    return pl.pallas_call(
        body, grid=(R // br,), in_specs=[spec] * 4, out_specs=[spec] * 3, out_shape=[shp] * 3,
        compiler_params=_cparams(1), name=name)(w, g, m, v)


def _lane0():
    return lax.broadcasted_iota(jnp.int32, (1, LANES), 1) < HEAD_DIM


def _half_sum(x, m0):
    s0 = jnp.sum(jnp.where(m0, x, 0.0), axis=-1, keepdims=True)
    s1 = jnp.sum(jnp.where(m0, 0.0, x), axis=-1, keepdims=True)
    return jnp.where(m0, s0, s1)


def _half_pick(x, m0, e):
    sel = m0 if e == 0 else jnp.logical_not(m0)
    return jnp.max(jnp.where(sel, x, NEG), axis=-1, keepdims=True)


def _head_rms(x, m0):
    return lax.rsqrt(_half_sum(x * x, m0) * (1.0 / HEAD_DIM) + RMS_EPS)


def _alibi(n):
    return [float(2.0 ** (-8.0 * (h + 1) / n)) for h in range(n)]


def _mask_half(x, m0, e):
    return jnp.where(m0, x, 0.0) if e == 0 else jnp.where(m0, 0.0, x)


def _band_masks2(max_dist, has_prev, live):
    row = lax.broadcasted_iota(jnp.int32, (2 * BLK, 2 * BLK), 0)
    col = lax.broadcasted_iota(jnp.int32, (2 * BLK, 2 * BLK), 1)
    dist = (row & (BLK - 1)) - col + BLK
    lim = jnp.where(live, max_dist, -1)
    first = jnp.where(has_prev, 0, BLK)
    valid = (dist >= 0) & (dist <= lim) & (col >= first)
    top = lax.broadcasted_iota(jnp.int32, (2 * BLK, 1), 0) < BLK
    return dist.astype(F32), valid, top


def _stack_heads(x, m0, kes):
    parts = []
    for e in range(2):
        h = _mask_half(x, m0, e)
        parts.append(pltpu.roll(h, HEAD_DIM, 1) if kes[e] != e else h)
    return jnp.concatenate(parts, axis=0)


def _unstack_heads(y, m0, kes):
    parts = []
    for e in range(2):
        h = y[e * BLK:(e + 1) * BLK]
        parts.append(pltpu.roll(h, HEAD_DIM, 1) if kes[e] != e else h)
    return jnp.where(m0, parts[0], parts[1])


def _rows(r, dil):
    return pl.ds(r, BLK, stride=dil) if dil > 1 else pl.ds(0, BLK)


def _band_units(dil, nsub):
    assert dil == 1 or nsub == 1
    if nsub == 1:
        return [(_rows(r, dil), ("prev", _rows(r, dil)), 0) for r in range(dil)]
    units = [(pl.ds(0, BLK), ("prev", pl.ds(0, BLK)), 0)]
    units += [(pl.ds(BLK * s, BLK), ("cur", pl.ds(BLK * (s - 1), BLK)), s) for s in range(1, nsub)]
    return units


def _band_specs(dil, nsub, ppk, q_blk, k_blk, v_blk, kv_shared, nb):
    RB = BLK * dil * nsub
    PB = BLK if nsub > 1 else RB
    qw = LANES * ppk
    kw = LANES if kv_shared else qw

    def cur(i):
        return jnp.minimum(i, nb - 1)

    def prev(i):
        return jnp.maximum(i * nsub - 1, 0) if nsub > 1 else jnp.maximum(i - 1, 0)

    def kidx(base):
        return (lambda p, i: (cur(i), base)) if kv_shared else (lambda p, i: (cur(i), base + p))

    def pidx(base):
        return (lambda p, i: (prev(i), base)) if kv_shared else (lambda p, i: (prev(i), base + p))

    return [pl.BlockSpec((RB, qw), lambda p, i: (cur(i), q_blk + p)),
            pl.BlockSpec((RB, kw), kidx(k_blk)), pl.BlockSpec((PB, kw), pidx(k_blk)),
            pl.BlockSpec((RB, kw), kidx(v_blk)), pl.BlockSpec((PB, kw), pidx(v_blk))]


def qk_norm(qkv, q_gain2, k_gain2, *, width, steps, n_q, tm, name):
    T = qkv.shape[0]
    nsb = width // LANES

    def body(x_ref, qg_ref, kg_ref, o_ref):
        m0 = _lane0()
        for b in range(nsb):
            is_q = ((pl.program_id(1) * nsb + b) < n_q).astype(F32)
            gain = qg_ref[...] * is_q + kg_ref[...] * (1.0 - is_q)
            cols = pl.ds(LANES * b, LANES)
            xv = x_ref[:, cols]
            o_ref[:, cols] = xv * _head_rms(xv, m0) * gain

    gspec = pl.BlockSpec((1, LANES), lambda i, j: (0, 0))
    return pl.pallas_call(
        body, grid=(T // tm, steps),
        in_specs=[pl.BlockSpec((tm, width), lambda i, j: (i, j)), gspec, gspec],
        out_specs=pl.BlockSpec((tm, width), lambda i, j: (i, j)),
        out_shape=jax.ShapeDtypeStruct((T, width * steps), F32),
        compiler_params=_cparams(2), name=name)(qkv, q_gain2, k_gain2)


def banded_fwd(qkn, qkv, slopes, sinks, *, dil, nsub, ppk, q_blk, k_blk, v_blk, n_heads, group,
               max_dist, name):
    T = qkv.shape[0]
    RB = BLK * dil * nsub
    nb = T // RB
    npair = n_heads // 2
    kv_shared = group > 1
    scale = HEAD_DIM ** -0.5
    has_sink = sinks is not None

    def body(*refs):
        slope_ref = refs[0]
        if has_sink:
            sink_ref, refs = refs[1], refs[2:]
        else:
            refs = refs[1:]
        q_ref, kc_ref, kp_ref, vc_ref, vp_ref, o_ref, l_ref = refs
        pb = pl.program_id(0)
        i = pl.program_id(1)
        m0 = _lane0()
        distf, valid_first, top = _band_masks2(max_dist, i > 0, i >= 0)
        valid_inner = _band_masks2(max_dist, i >= 0, i >= 0)[1] if nsub > 1 else None
        for rows, (src, prows), sub in _band_units(dil, nsub):
            valid = valid_first if sub == 0 else valid_inner
            kpr, vpr = (kp_ref, vp_ref) if src == "prev" else (kc_ref, vc_ref)
            kcache = {}
            for jp in range(ppk):
                cs = pl.ds(LANES * jp, LANES)
                jk = 0 if kv_shared else jp
                if jk not in kcache:
                    ks = pl.ds(LANES * jk, LANES)
                    kcat = jnp.concatenate([kpr[prows, ks], kc_ref[rows, ks]], axis=0)
                    vcat = jnp.concatenate([vpr[prows, ks], vc_ref[rows, ks]], axis=0)
                    kcache[jk] = (kcat.astype(BF16), vcat.astype(BF16))
                kn, vcat = kcache[jk]
                qn = q_ref[rows, cs]
                kes = [((2 * jp + e) // group) % 2 if kv_shared else e for e in range(2)]
                hidx = 2 * (pb * ppk + jp)
                qs = _stack_heads(qn, m0, kes).astype(BF16)
                slope = jnp.where(top, slope_ref[hidx], slope_ref[hidx + 1])
                s = jnp.where(valid, _dot_nt(qs, kn) * scale - slope * distf, NEG)
                m = jnp.max(s, axis=-1, keepdims=True)
                if has_sink:
                    sk = jnp.where(top, sink_ref[hidx], sink_ref[hidx + 1])
                    m = jnp.maximum(m, sk)
                p = jnp.exp(s - m)
                den = jnp.sum(p, axis=-1, keepdims=True)
                if has_sink:
                    den = den + jnp.exp(sk - m)
                o_full = _dot((p * (1.0 / den)).astype(BF16), vcat)
                o_ref[rows, cs] = _unstack_heads(o_full, m0, kes)
                l_ref[rows, cs] = _unstack_heads(jnp.broadcast_to(m + jnp.log(den), (2 * BLK, LANES)), m0, [0, 1])

    smem = pl.BlockSpec(memory_space=pltpu.SMEM)
    qw = LANES * ppk
    ospec = pl.BlockSpec((RB, qw), lambda p, i: (i, p))
    oshape = jax.ShapeDtypeStruct((T, n_heads * HEAD_DIM), F32)
    args = [slopes] + ([sinks] if has_sink else []) + [qkn] * 3 + [qkv] * 2
    return pl.pallas_call(
        body, grid=(npair // ppk, nb),
        in_specs=[smem] * (2 if has_sink else 1) + _band_specs(dil, nsub, ppk, q_blk, k_blk, v_blk, kv_shared, nb),
        out_specs=[ospec, ospec], out_shape=[oshape, oshape],
        compiler_params=_cparams(2), name=name)(*args)


def banded_bwd(qkn, qkv, slopes, sinks, do, o, lse, w, omix, *, dil, nsub, ppk, q_blk, k_blk, v_blk,
               n_heads, group, max_dist, do_blk, name):
    T = qkv.shape[0]
    RB = BLK * dil * nsub
    nb = T // RB
    npair = n_heads // 2
    kv_shared = group > 1
    scale = HEAD_DIM ** -0.5
    has_sink = sinks is not None
    mixed = w is not None
    qw = LANES * ppk

    def body(*refs):
        slope_ref = refs[0]
        if has_sink:
            sink_ref, refs = refs[1], refs[2:]
        else:
            refs = refs[1:]
        q_ref, kc_ref, kp_ref, vc_ref, vp_ref, do_ref, o_ref, l_ref = refs[:8]
        refs = refs[8:]
        if mixed:
            w_ref, om_ref, refs = refs[0], refs[1], refs[2:]
        dq_ref, dk_ref, dv_ref, dsk_ref, ck_ref, cv_ref = refs
        pb = pl.program_id(0)
        i = pl.program_id(1)
        live = i < nb
        m0 = _lane0()
        lane = lax.broadcasted_iota(jnp.int32, (1, LANES), 1)
        distf, valid_first, top = _band_masks2(max_dist, i > 0, live)
        valid_inner = _band_masks2(max_dist, i >= 0, live)[1] if nsub > 1 else None
        livef = live.astype(F32)

        def stack_rows(x2):
            return jnp.concatenate([_half_pick(x2, m0, 0), _half_pick(x2, m0, 1)], axis=0)

        @pl.when((pb == 0) & (i == 0))
        def _():
            dsk_ref[...] = jnp.zeros_like(dsk_ref)

        @pl.when(i == 0)
        def _():
            ck_ref[...] = jnp.zeros_like(ck_ref)
            cv_ref[...] = jnp.zeros_like(cv_ref)

        dsk_acc = jnp.zeros((1, LANES), F32)
        if nsub > 1:
            dk_ref[...] = ck_ref[...]
            dv_ref[...] = cv_ref[...]
        for rows, (src, prows), sub in _band_units(dil, nsub):
            valid = valid_first if sub == 0 else valid_inner
            kpr, vpr = (kp_ref, vp_ref) if src == "prev" else (kc_ref, vc_ref)
            for jp in range(ppk):
                cs = pl.ds(LANES * jp, LANES)
                ks = pl.ds(0, LANES) if kv_shared else cs
                kn = jnp.concatenate([kpr[prows, ks], kc_ref[rows, ks]], axis=0).astype(BF16)
                vcat = jnp.concatenate([vpr[prows, ks], vc_ref[rows, ks]], axis=0).astype(BF16)
                dov = do_ref[rows, cs]
                lv = l_ref[rows, cs]
                if mixed:
                    wv = w_ref[rows, cs]
                    dmix = _half_sum(dov * om_ref[rows, cs], m0)
                    dov = dov * wv
                delta2 = _half_sum(dov * o_ref[rows, cs], m0)
                shift = stack_rows(wv * dmix if mixed else delta2)
                kes = [((2 * jp + e) // group) % 2 if kv_shared else e for e in range(2)]
                hidx = 2 * (pb * ppk + jp)
                qs = _stack_heads(q_ref[rows, cs], m0, kes).astype(BF16)
                dos = _stack_heads(dov, m0, kes).astype(BF16)
                lse = stack_rows(lv)
                slope = jnp.where(top, slope_ref[hidx], slope_ref[hidx + 1])
                p = jnp.where(valid, jnp.exp(_dot_nt(qs, kn) * scale - slope * distf - lse), 0.0)
                ds = (p * (_dot_nt(dos, vcat) - shift)).astype(BF16)
                dqn = _unstack_heads(_dot(ds, kn), m0, kes) * scale
                dkn = _dot_tn(ds, qs) * scale
                dvv = _dot_tn(p.astype(BF16), dos)
                if has_sink:
                    sk = jnp.where(top, sink_ref[hidx], sink_ref[hidx + 1])
                    contrib = -jnp.exp(sk - lse) * stack_rows(delta2) * livef
                    for e in range(2):
                        tot = jnp.sum(contrib[e * BLK:(e + 1) * BLK], axis=0, keepdims=True)
                        dsk_acc = dsk_acc + jnp.where(lane == (2 * jp + e), tot, 0.0)
                dk_raw = dkn

                @pl.when(live)
                def _():
                    dq_ref[rows, cs] = dqn

                if nsub == 1:
                    dk_ref[rows, cs] = ck_ref[rows, cs] + dk_raw[:BLK]
                    dv_ref[rows, cs] = cv_ref[rows, cs] + dvv[:BLK]
                elif sub == 0:
                    last = pl.ds(RB - BLK, BLK)
                    dk_ref[last, cs] += dk_raw[:BLK]
                    dv_ref[last, cs] += dvv[:BLK]
                else:
                    ck_ref[prows, cs] += dk_raw[:BLK]
                    cv_ref[prows, cs] += dvv[:BLK]
                ck_ref[rows, cs] = dk_raw[BLK:]
                cv_ref[rows, cs] = dvv[BLK:]
        dsk_ref[...] += dsk_acc

    smem = pl.BlockSpec(memory_space=pltpu.SMEM)
    gspec = pl.BlockSpec((1, LANES), lambda p, i: (0, 0))

    def cur(i):
        return jnp.minimum(i, nb - 1)

    qspec = pl.BlockSpec((RB, qw), lambda p, i: (cur(i), p))
    dospec = pl.BlockSpec((RB, qw), lambda p, i: (cur(i), do_blk + p))
    kvout = pl.BlockSpec((RB, qw), lambda p, i: (jnp.maximum(i - 1, 0), p))
    in_specs = ([smem] * (2 if has_sink else 1) + _band_specs(dil, nsub, ppk, q_blk, k_blk, v_blk, kv_shared, nb)
                + [dospec, qspec, qspec] + ([qspec, qspec] if mixed else []))
    args = ([slopes] + ([sinks] if has_sink else []) + [qkn] * 3 + [qkv] * 2 + [do, o, lse]
            + ([w, omix] if mixed else []))
    full = jax.ShapeDtypeStruct((T, n_heads * HEAD_DIM), F32)
    row = jax.ShapeDtypeStruct((1, LANES), F32)
    return pl.pallas_call(
        body, grid=(npair // ppk, nb + 1), in_specs=in_specs,
        out_specs=[qspec, kvout, kvout, gspec],
        out_shape=[full, full, full, row],
        scratch_shapes=[pltpu.VMEM((RB, qw), F32), pltpu.VMEM((RB, qw), F32)],
        compiler_params=_cparams(2), name=name)(*args)


def mix_fwd(o1, o2, o3, l1, l2, l3, *, tm, name):
    T, C = o1.shape

    def body(o1r, o2r, o3r, l1r, l2r, l3r, o_ref, w1r, w2r, w3r):
        a, b, c = l1r[...], l2r[...], l3r[...]
        m = jnp.maximum(jnp.maximum(a, b), c)
        ea, eb, ec = jnp.exp(a - m), jnp.exp(b - m), jnp.exp(c - m)
        inv = 1.0 / (ea + eb + ec)
        wa, wb, wc = ea * inv, eb * inv, ec * inv
        o_ref[...] = wa * o1r[...] + wb * o2r[...] + wc * o3r[...]
        w1r[...] = wa
        w2r[...] = wb
        w3r[...] = wc

    spec = pl.BlockSpec((tm, C), lambda i: (i, 0))
    shp = jax.ShapeDtypeStruct((T, C), F32)
    return pl.pallas_call(body, grid=(T // tm,), in_specs=[spec] * 6, out_specs=[spec] * 4, out_shape=[shp] * 4,
                          compiler_params=_cparams(1), name=name)(o1, o2, o3, l1, l2, l3)


def _qk_norm_bwd(raw, dn, gain, m0):
    r = _head_rms(raw, m0)
    h = raw * r
    dh = dn * gain
    d_raw = r * (dh - h * (_half_sum(dh * h, m0) * (1.0 / HEAD_DIM)))
    return d_raw, jnp.sum(dn * h, axis=0, keepdims=True)


def _acc_rows(ref, val):
    @pl.when(pl.program_id(0) == 0)
    def _():
        ref[...] = val

    @pl.when(pl.program_id(0) > 0)
    def _():
        ref[...] += val


def assemble_odd(parts, qkv, q_gain2, k_gain2, *, tm, name):
    T, C = parts[0][0].shape
    nbk = C // LANES

    def body(*refs):
        qkv_ref, qg_ref, kg_ref, o_ref, dqg_ref, dkg_ref = refs[9:]
        m0 = _lane0()
        sums = [refs[j][...] + refs[3 + j][...] + refs[6 + j][...] for j in range(3)]
        o_ref[:, pl.ds(2 * C, C)] = sums[2]
        for j, (g_ref, acc_ref) in enumerate(((qg_ref, dqg_ref), (kg_ref, dkg_ref))):
            dgain = jnp.zeros((1, LANES), F32)
            for b in range(nbk):
                cols = pl.ds(C * j + LANES * b, LANES)
                d_raw, part = _qk_norm_bwd(qkv_ref[:, cols], sums[j][:, LANES * b:LANES * (b + 1)], g_ref[...], m0)
                o_ref[:, cols] = d_raw
                dgain = dgain + part
            _acc_rows(acc_ref, dgain)

    spec = pl.BlockSpec((tm, C), lambda i: (i, 0))
    gspec = pl.BlockSpec((1, LANES), lambda i: (0, 0))
    flat = [parts[p][j] for p in range(3) for j in range(3)]
    row = jax.ShapeDtypeStruct((1, LANES), F32)
    return pl.pallas_call(body, grid=(T // tm,),
                          in_specs=[spec] * 9 + [pl.BlockSpec((tm, 2 * C), lambda i: (i, 0)), gspec, gspec],
                          out_specs=[pl.BlockSpec((tm, 3 * C), lambda i: (i, 0)), gspec, gspec],
                          out_shape=[jax.ShapeDtypeStruct((T, 3 * C), F32), row, row],
                          compiler_params=_cparams(1), name=name)(*flat, qkv, q_gain2, k_gain2)


def assemble_even(dqa, dka4, dva4, dqb, dkb, dvb, qkv, q_gain2, k_gain2, *, tm, name):
    T = dqa.shape[0]
    W = 512
    QK = 768

    def body(dqa_r, dka_r, dva_r, dqb_r, dkb_r, dvb_r, qkv_ref, qg_ref, kg_ref, o_ref, dqg_ref, dkg_ref):
        m0 = _lane0()
        ka = dka_r[...]
        va = dva_r[...]
        dqn = dqa_r[...]
        dgain = jnp.zeros((1, LANES), F32)
        for b in range(W // LANES):
            cols = pl.ds(LANES * b, LANES)
            d_raw, part = _qk_norm_bwd(qkv_ref[:, cols], dqn[:, LANES * b:LANES * (b + 1)], qg_ref[...], m0)
            o_ref[:, cols] = d_raw
            dgain = dgain + part
        _acc_rows(dqg_ref, dgain)
        dkn = ka[:, 0:128] + ka[:, 128:256] + ka[:, 256:384] + ka[:, 384:512]
        d_raw, part = _qk_norm_bwd(qkv_ref[:, pl.ds(W, LANES)], dkn, kg_ref[...], m0)
        o_ref[:, pl.ds(W, LANES)] = d_raw
        _acc_rows(dkg_ref, part)
        o_ref[:, pl.ds(640, LANES)] = va[:, 0:128] + va[:, 128:256] + va[:, 256:384] + va[:, 384:512]
        o_ref[:, pl.ds(768, W)] = dqb_r[...]
        o_ref[:, pl.ds(1280, W)] = dkb_r[...]
        o_ref[:, pl.ds(1792, W)] = dvb_r[...]

    spec = pl.BlockSpec((tm, W), lambda i: (i, 0))
    gspec = pl.BlockSpec((1, LANES), lambda i: (0, 0))
    row = jax.ShapeDtypeStruct((1, LANES), F32)
    return pl.pallas_call(body, grid=(T // tm,),
                          in_specs=[spec] * 6 + [pl.BlockSpec((tm, QK), lambda i: (i, 0)), gspec, gspec],
                          out_specs=[pl.BlockSpec((tm, 2304), lambda i: (i, 0)), gspec, gspec],
                          out_shape=[jax.ShapeDtypeStruct((T, 2304), F32), row, row],
                          compiler_params=_cparams(1), name=name)(dqa, dka4, dva4, dqb, dkb, dvb, qkv, q_gain2, k_gain2)


STICK_T = 256
STICK_DEAD = -110.0


def _split_bf16(x):
    hi = x.astype(BF16)
    lo = (x - hi.astype(F32)).astype(BF16)
    return hi, lo


def _stick_logits(qm, kt, scale, diag):
    n = STICK_T
    row = lax.broadcasted_iota(jnp.int32, (n, n), 0)
    col = lax.broadcasted_iota(jnp.int32, (n, n), 1)
    mask = col < row + jnp.where(diag, 0, n)
    z = _dot_nt(qm, kt) * scale
    lneg = -(jnp.maximum(z, 0.0) + jnp.log(1.0 + jnp.exp(-jnp.abs(z))))
    lpos = z + lneg
    lk = jnp.where(mask, lneg, 0.0)
    return mask, lpos, lneg, lk


def _cumsum_mm(x, tri):
    hi, lo = _split_bf16(x)
    return _dot(hi, tri) + _dot(lo, tri)


def stick_fwd(qkv, *, q_blk, k_blk, v_blk, n_pairs, name, riders=None, rider_args=()):
    T = qkv.shape[0]
    n = STICK_T
    nq = T // n
    scale = HEAD_DIM ** -0.5
    nc = riders.n if riders is not None else 0
    n_steps = n_pairs * nq
    stage_at = (0, (3 * n_steps) // 4, n_steps - 1, n_steps - 1)

    def body(*refs):
        q_ref, k_ref, v_ref = refs[:3]
        x_refs, o_ref = refs[3:3 + nc], refs[3 + nc]
        out_refs, sems = refs[4 + nc:4 + 2 * nc], refs[4 + 2 * nc:]
        i = pl.program_id(1)
        step_id = pl.program_id(0) * nq + i

        def ride(which):
            if riders is not None:
                @pl.when(step_id == stage_at[which])
                def _():
                    riders.stage(which, x_refs, out_refs, sems)

        ride(0)
        ride(1)
        m0 = _lane0()
        r2 = lax.broadcasted_iota(jnp.int32, (n, n), 0)
        c2 = lax.broadcasted_iota(jnp.int32, (n, n), 1)
        tri_after = (r2 > c2).astype(BF16)
        qv = q_ref[...]
        out = jnp.zeros((n, LANES), F32)
        for e in range(2):
            qm = _mask_half(qv, m0, e).astype(BF16)

            def alive(st):
                t, _, carry = st
                return (t <= i) & (jnp.max(carry) > STICK_DEAD)

            def step(st, e=e, qm=qm):
                t, acc, carry = st
                start = pl.multiple_of((i - t) * n, n)
                kt = k_ref[pl.ds(start, n), :].astype(BF16)
                vt = _mask_half(v_ref[pl.ds(start, n), :], m0, e).astype(BF16)
                mask, lpos, _, lk = _stick_logits(qm, kt, scale, t == 0)
                after = _cumsum_mm(lk, tri_after) + carry
                a = jnp.where(mask, jnp.exp(lpos + after), 0.0)
                acc = acc + _dot(a.astype(BF16), vt)
                carry = carry + jnp.sum(lk, axis=-1, keepdims=True)
                return t + 1, acc, carry

            _, acc, _ = lax.while_loop(alive, step, (jnp.int32(0), jnp.zeros((n, LANES), F32),
                                                     jnp.zeros((n, 1), F32)))
            out = out + acc
        o_ref[...] = out
        ride(2)
        ride(3)

    outs = pl.pallas_call(
        body, grid=(n_pairs, nq),
        in_specs=[pl.BlockSpec((n, LANES), lambda p, i: (i, q_blk + p)),
                  pl.BlockSpec((T, LANES), lambda p, i: (0, k_blk + p)),
                  pl.BlockSpec((T, LANES), lambda p, i: (0, v_blk + p))] + [_ANY] * nc,
        out_specs=[pl.BlockSpec((n, LANES), lambda p, i: (i, p))] + [_ANY] * nc,
        out_shape=[jax.ShapeDtypeStruct((T, n_pairs * LANES), F32)] + (riders.shapes if nc else []),
        scratch_shapes=riders.sems if nc else [],
        compiler_params=_cparams(2), name=name)(qkv, qkv, qkv, *rider_args)
    return outs[0], list(outs[1:])


def stick_bwd(qkv, do, *, q_blk, k_blk, v_blk, do_blk, n_pairs, name, riders=None, rider_args=()):
    T = qkv.shape[0]
    n = STICK_T
    nq = T // n
    scale = HEAD_DIM ** -0.5
    nc = riders.n if riders is not None else 0

    def body(*refs):
        q_ref, k_ref, v_ref, do_ref = refs[:4]
        pre_refs = refs[4:4 + nc]
        dq_ref, dk_ref, dv_ref = refs[4 + nc:7 + nc]
        land_refs = refs[7 + nc:7 + 2 * nc]
        a_keep, g_keep, s_keep = refs[7 + 2 * nc:10 + 2 * nc]
        sems = refs[10 + 2 * nc:]
        i = pl.program_id(1)
        first_step = (pl.program_id(0) == 0) & (i == 0)
        last_step = (pl.program_id(0) == n_pairs - 1) & (i == nq - 1)
        m0 = _lane0()
        r2 = lax.broadcasted_iota(jnp.int32, (n, n), 0)
        c2 = lax.broadcasted_iota(jnp.int32, (n, n), 1)
        tri_after = (r2 > c2).astype(BF16)
        tri_from = (r2 >= c2).astype(BF16)

        if riders is not None:
            @pl.when(first_step)
            def _():
                riders.start(pre_refs, land_refs, sems)

        @pl.when(i == 0)
        def _():
            dk_ref[...] = jnp.zeros_like(dk_ref)
            dv_ref[...] = jnp.zeros_like(dv_ref)

        qv = q_ref[...]
        dov = do_ref[...]
        dq_out = jnp.zeros((n, LANES), F32)
        for e in range(2):
            qm = _mask_half(qv, m0, e).astype(BF16)
            dom = _mask_half(dov, m0, e).astype(BF16)

            def alive(st):
                t, carry, _ = st
                return (t <= i) & (jnp.max(carry) > STICK_DEAD)

            def scan(st, qm=qm, dom=dom):
                t, carry, gtot = st
                start = pl.multiple_of((i - t) * n, n)
                kt = k_ref[pl.ds(start, n), :].astype(BF16)
                vt = v_ref[pl.ds(start, n), :].astype(BF16)
                mask, lpos, lneg, lk = _stick_logits(qm, kt, scale, t == 0)
                a = jnp.where(mask, jnp.exp(lpos + _cumsum_mm(lk, tri_after) + carry), 0.0)
                g = _dot_nt(dom, vt) * a
                a_keep[t] = a.astype(BF16)
                g_keep[t] = g
                s_keep[t] = jnp.exp(lneg).astype(BF16)
                return (t + 1, carry + jnp.sum(lk, axis=-1, keepdims=True),
                        gtot + jnp.sum(g, axis=-1, keepdims=True))

            z1 = jnp.zeros((n, 1), F32)
            n_live, _, gtot = lax.while_loop(alive, scan, (jnp.int32(0), z1, z1))

            def step(t, st, e=e, qm=qm, dom=dom, gtot=gtot):
                dq_acc, gright = st
                start = pl.multiple_of((i - t) * n, n)
                g = g_keep[t]
                sneg = s_keep[t].astype(F32)
                before = gtot - (_cumsum_mm(g, tri_from) + gright)
                mask = c2 < r2 + jnp.where(t == 0, 0, n)
                dz = jnp.where(mask, g * sneg - before * (1.0 - sneg), 0.0) * scale
                dzb = dz.astype(BF16)
                dq_acc = dq_acc + _dot(dzb, _mask_half(k_ref[pl.ds(start, n), :], m0, e).astype(BF16))
                dk_ref[pl.ds(start, n), :] += _dot_tn(dzb, qm)
                dv_ref[pl.ds(start, n), :] += _dot_tn(a_keep[t], dom)
                return dq_acc, gright + jnp.sum(g, axis=-1, keepdims=True)

            dq_acc, _ = lax.fori_loop(0, n_live, step, (jnp.zeros((n, LANES), F32), z1))
            dq_out = dq_out + dq_acc
        dq_ref[...] = dq_out

        if riders is not None:
            @pl.when(last_step)
            def _():
                riders.finish(pre_refs, land_refs, sems)

    tile = pl.BlockSpec((n, LANES), lambda p, i: (i, p))
    whole = pl.BlockSpec((T, LANES), lambda p, i: (0, p))
    shp = jax.ShapeDtypeStruct((T, n_pairs * LANES), F32)
    outs = pl.pallas_call(
        body, grid=(n_pairs, nq),
        in_specs=[pl.BlockSpec((n, LANES), lambda p, i: (i, q_blk + p)),
                  pl.BlockSpec((T, LANES), lambda p, i: (0, k_blk + p)),
                  pl.BlockSpec((T, LANES), lambda p, i: (0, v_blk + p)),
                  pl.BlockSpec((n, LANES), lambda p, i: (i, do_blk + p))] + [_ANY] * nc,
        out_specs=[tile, whole, whole] + [_ANY] * nc,
        out_shape=[shp, shp, shp] + (riders.shapes if nc else []),
        scratch_shapes=[pltpu.VMEM((nq, n, n), BF16), pltpu.VMEM((nq, n, n), F32), pltpu.VMEM((nq, n, n), BF16)]
        + (riders.sems if nc else []),
        compiler_params=_cparams(2), name=name)(qkv, qkv, qkv, do, *rider_args)
    return outs[0], outs[1], outs[2], list(outs[3:])


def _xnorm(x):
    r = lax.rsqrt(jnp.mean(x * x, axis=-1, keepdims=True) + RMS_EPS)
    return r, x * r


def xattn_fwd(qraw, kvraw, q_gain, k_gain, *, tm, name):
    T = qraw.shape[0]
    scale = X_HEAD_DIM ** -0.5
    W = X_HEADS * X_HEAD_DIM

    def body(q_ref, kv_ref, qg_ref, kg_ref, o_ref):
        for h in range(X_HEADS):
            cs = pl.ds(X_HEAD_DIM * h, X_HEAD_DIM)
            _, qh = _xnorm(q_ref[:, cs])
            _, kh = _xnorm(kv_ref[:, cs])
            qn = (qh * qg_ref[...]).astype(BF16)
            kn = (kh * kg_ref[...]).astype(BF16)
            v = kv_ref[:, pl.ds(W + X_HEAD_DIM * h, X_HEAD_DIM)].astype(BF16)
            s = _dot_nt(qn, kn) * scale
            m = jnp.max(s, axis=-1, keepdims=True)
            p = jnp.exp(s - m)
            p = p / jnp.sum(p, axis=-1, keepdims=True)
            o_ref[:, cs] = _dot(p.astype(BF16), v)

    gspec = pl.BlockSpec((1, X_HEAD_DIM), lambda i: (0, 0))
    return pl.pallas_call(
        body, grid=(T // tm,),
        in_specs=[pl.BlockSpec((tm, W), lambda i: (i, 0)), pl.BlockSpec((MEM_LEN, 2 * W), lambda i: (0, 0)),
                  gspec, gspec],
        out_specs=pl.BlockSpec((tm, W), lambda i: (i, 0)),
        out_shape=jax.ShapeDtypeStruct((T, W), F32),
        compiler_params=_cparams(1), name=name)(qraw, kvraw, q_gain, k_gain)


def xattn_bwd(qraw, kvraw, q_gain, k_gain, do, o, *, tm, name):
    T = qraw.shape[0]
    nt = T // tm
    scale = X_HEAD_DIM ** -0.5
    W = X_HEADS * X_HEAD_DIM

    def body(q_ref, kv_ref, qg_ref, kg_ref, do_ref, o_ref, dq_ref, dkv_ref, dqg_ref, dkg_ref, dkn_ref):
        i = pl.program_id(0)

        @pl.when(i == 0)
        def _():
            dkv_ref[...] = jnp.zeros_like(dkv_ref)
            dkn_ref[...] = jnp.zeros_like(dkn_ref)
            dqg_ref[...] = jnp.zeros_like(dqg_ref)
            dkg_ref[...] = jnp.zeros_like(dkg_ref)

        qg = qg_ref[...]
        kg = kg_ref[...]
        dqg_acc = jnp.zeros((1, X_HEAD_DIM), F32)
        for h in range(X_HEADS):
            cs = pl.ds(X_HEAD_DIM * h, X_HEAD_DIM)
            vs = pl.ds(W + X_HEAD_DIM * h, X_HEAD_DIM)
            rq, qh = _xnorm(q_ref[:, cs])
            _, kh = _xnorm(kv_ref[:, cs])
            qn = (qh * qg).astype(BF16)
            kn = (kh * kg).astype(BF16)
            v = kv_ref[:, vs].astype(BF16)
            s = _dot_nt(qn, kn) * scale
            m = jnp.max(s, axis=-1, keepdims=True)
            p = jnp.exp(s - m)
            p = p / jnp.sum(p, axis=-1, keepdims=True)
            dov = do_ref[:, cs]
            delta = jnp.sum(dov * o_ref[:, cs], axis=-1, keepdims=True)
            dob = dov.astype(BF16)
            ds = (p * (_dot_nt(dob, v) - delta)).astype(BF16)
            dqn = _dot(ds, kn) * scale
            dkn_ref[:, cs] += _dot_tn(ds, qn) * scale
            dkv_ref[:, vs] += _dot_tn(p.astype(BF16), dob)
            dqg_acc = dqg_acc + jnp.sum(dqn * qh, axis=0, keepdims=True)
            dqh = dqn * qg
            dq_ref[:, cs] = rq * (dqh - qh * jnp.mean(dqh * qh, axis=-1, keepdims=True))
        dqg_ref[...] += dqg_acc

        @pl.when(i == nt - 1)
        def _():
            dkg_acc = jnp.zeros((1, X_HEAD_DIM), F32)
            for h in range(X_HEADS):
                cs = pl.ds(X_HEAD_DIM * h, X_HEAD_DIM)
                rk, kh = _xnorm(kv_ref[:, cs])
                dkn = dkn_ref[:, cs]
                dkg_acc = dkg_acc + jnp.sum(dkn * kh, axis=0, keepdims=True)
                dkh = dkn * kg
                dkv_ref[:, cs] = rk * (dkh - kh * jnp.mean(dkh * kh, axis=-1, keepdims=True))
            dkg_ref[...] = dkg_acc

    gspec = pl.BlockSpec((1, X_HEAD_DIM), lambda i: (0, 0))
    tile = pl.BlockSpec((tm, W), lambda i: (i, 0))
    kvspec = pl.BlockSpec((MEM_LEN, 2 * W), lambda i: (0, 0))
    grow = jax.ShapeDtypeStruct((1, X_HEAD_DIM), F32)
    return pl.pallas_call(
        body, grid=(nt,), in_specs=[tile, kvspec, gspec, gspec, tile, tile],
        out_specs=[tile, kvspec, gspec, gspec],
        out_shape=[jax.ShapeDtypeStruct((T, W), F32), jax.ShapeDtypeStruct((MEM_LEN, 2 * W), F32), grow, grow],
        scratch_shapes=[pltpu.VMEM((MEM_LEN, W), F32)],
        compiler_params=_cparams(1), name=name)(qraw, kvraw, q_gain, k_gain, do, o)


_ANY = pl.BlockSpec(memory_space=pl.ANY)


def _my_pos():
    return lax.axis_index("x"), lax.axis_index("y"), lax.axis_index("c")


def _pieces(arrays, chunks):
    out = []
    for a, (arr, n) in enumerate(zip(arrays, chunks)):
        rc = arr.shape[-2] // n
        out += [(a, pl.ds(ch * rc, rc)) for ch in range(n)]
    return out


class GatherBlocks:
    N_STAGES = 4

    def __init__(self, blks, chunks):
        self.shapes = [jax.ShapeDtypeStruct((N_DEV,) + b.shape, b.dtype) for b in blks]
        self.n = len(blks)
        self.pieces = _pieces(blks, chunks)
        n_p = len(self.pieces)
        self.sems = [pltpu.SemaphoreType.DMA((7 * n_p,)), pltpu.SemaphoreType.DMA((7 * n_p,)),
                     pltpu.SemaphoreType.DMA((n_p,))]

    def stage(self, which, x_refs, out_refs, sems):
        send_sems, recv_sems, local_sems = sems
        pieces, n_p = self.pieces, len(self.pieces)
        x, y, c = _my_pos()
        me, sibling = (x, y, c), (x, y, 1 - c)
        chips = [(1 - x, y), (x, 1 - y), (1 - x, 1 - y)]
        xn, yn, dg = [(*chip, c) for chip in chips]
        ps = range(n_p)

        def slot(block, p):
            px, py, pc = block
            a, rows = pieces[p]
            return out_refs[a].at[4 * px + 2 * py + pc, rows]

        def own(p):
            a, rows = pieces[p]
            return x_refs[a].at[rows]

        def copy(k, p, block, to, from_input=False):
            return pltpu.make_async_remote_copy(
                src_ref=own(p) if from_input else slot(block, p), dst_ref=slot(block, p),
                send_sem=send_sems.at[k * n_p + p], recv_sem=recv_sems.at[k * n_p + p],
                device_id=to, device_id_type=MESH)

        mine = [pltpu.make_async_copy(own(p), slot(me, p), local_sems.at[p]) for p in ps]
        first = [copy(k, p, me, to, from_input=True) for p in ps for k, to in ((1, xn), (2, yn), (0, sibling))]
        on_x = [copy(3, p, xn, yn) for p in ps if p % 2 == 0] + [copy(4, p, xn, sibling) for p in ps]
        on_y = [copy(3, p, yn, xn) for p in ps if p % 2 == 1] + [copy(5, p, yn, sibling) for p in ps]
        on_d = [copy(6, p, dg, sibling) for p in ps]
        if which == 0:
            for cp in first + mine:
                cp.start()
        elif which == 1:
            for p in ps:
                copy(1, p, xn, me).wait_recv()
                if p % 2 == 0:
                    copy(3, p, xn, yn).start()
                copy(4, p, xn, sibling).start()
                copy(2, p, yn, me).wait_recv()
                if p % 2 == 1:
                    copy(3, p, yn, xn).start()
                copy(5, p, yn, sibling).start()
        elif which == 2:
            for p in ps:
                copy(3, p, dg, me).wait_recv()
                copy(6, p, dg, sibling).start()
        else:
            for p in ps:
                copy(0, p, sibling, me).wait_recv()
            for k, chip in zip((4, 5, 6), chips):
                for p in ps:
                    copy(k, p, (*chip, 1 - c), me).wait_recv()
            for cp in first + on_x + on_y + on_d:
                cp.wait_send()
            for cp in mine:
                cp.wait()


def gather_blocks(blks, chunks, *, name):
    gb = GatherBlocks(blks, chunks)
    n = gb.n

    def body(*refs):
        x_refs, out_refs, sems = refs[:n], refs[n:2 * n], refs[2 * n:]
        for which in range(gb.N_STAGES):
            gb.stage(which, x_refs, out_refs, sems)

    return pl.pallas_call(body, out_shape=gb.shapes, in_specs=[_ANY] * n, out_specs=[_ANY] * n,
                          scratch_shapes=gb.sems, name=name)(*blks)


def gather_small(small, *, name):
    S, C = small.shape

    def body(s_ref, out_ref, send_sems, recv_sems, local_sem):
        x, y, c = _my_pos()
        my_id = 4 * x + 2 * y + c

        def copy(k, slot):
            px, py, pc = x ^ ((k >> 2) & 1), y ^ ((k >> 1) & 1), c ^ (k & 1)
            dst = my_id if slot == "mine" else 4 * px + 2 * py + pc
            return pltpu.make_async_remote_copy(
                src_ref=s_ref, dst_ref=out_ref.at[dst], send_sem=send_sems.at[k - 1], recv_sem=recv_sems.at[k - 1],
                device_id=(px, py, pc), device_id_type=MESH)

        own = pltpu.make_async_copy(s_ref, out_ref.at[my_id], local_sem)
        own.start()
        sends = [copy(k, "mine") for k in range(1, N_DEV)]
        for cp in sends:
            cp.start()
        for k in range(1, N_DEV):
            copy(k, "theirs").wait_recv()
        for cp in sends:
            cp.wait_send()
        own.wait()

    dma7 = pltpu.SemaphoreType.DMA((7,))
    return pl.pallas_call(
        body, out_shape=jax.ShapeDtypeStruct((N_DEV, S, C), small.dtype), in_specs=[_ANY], out_specs=_ANY,
        scratch_shapes=[dma7, dma7, pltpu.SemaphoreType.DMA], name=name)(small)


class PairExchange:
    def __init__(self, bigs, chunks):
        self.shapes = [jax.ShapeDtypeStruct((4,) + b.shape[1:], b.dtype) for b in bigs]
        self.n = len(bigs)
        self.pieces = _pieces(bigs, chunks)
        n_p = len(self.pieces)
        self.sems = [pltpu.SemaphoreType.DMA((4 * n_p,)), pltpu.SemaphoreType.DMA((4 * n_p,))]

    def _copies(self, big_refs, out_refs, sems):
        send_sems, recv_sems = sems
        n_p = len(self.pieces)
        x, y, c = _my_pos()

        def copy(b, p):
            a, rows = self.pieces[p]
            return pltpu.make_async_remote_copy(
                src_ref=big_refs[a].at[2 * b + (1 - c), rows], dst_ref=out_refs[a].at[b, rows],
                send_sem=send_sems.at[b * n_p + p], recv_sem=recv_sems.at[b * n_p + p],
                device_id=(x, y, 1 - c), device_id_type=MESH)

        return [copy(b, p) for b in range(4) for p in range(n_p)]

    def start(self, big_refs, out_refs, sems):
        for cp in self._copies(big_refs, out_refs, sems):
            cp.start()

    def finish(self, big_refs, out_refs, sems):
        cps = self._copies(big_refs, out_refs, sems)
        for cp in cps:
            cp.wait_recv()
        for cp in cps:
            cp.wait_send()


def _standalone(exchange, args, name):
    n = exchange.n

    def body(*refs):
        exchange.start(refs[:n], refs[n:2 * n], refs[2 * n:])
        exchange.finish(refs[:n], refs[n:2 * n], refs[2 * n:])

    return pl.pallas_call(body, out_shape=exchange.shapes, in_specs=[_ANY] * n, out_specs=[_ANY] * n,
                          scratch_shapes=exchange.sems, name=name)(*args)


class Riding:
    def __init__(self, riders):
        self.riders = [(ex, list(args)) for ex, args in riders]
        self.args = [a for _, args in self.riders for a in args]
        self.in_specs = [_ANY] * len(self.args)
        self.out_shapes = [s for ex, _ in self.riders for s in ex.shapes]
        self.out_specs = [_ANY] * len(self.out_shapes)
        self.scratch = [s for ex, _ in self.riders for s in ex.sems]

    def wrap(self, body, n_in, n_out, n_scratch, is_first, is_last):
        def wrapped(*refs):
            k = 0
            core = list(refs[:n_in])
            k = n_in
            r_in = []
            for ex, _ in self.riders:
                r_in.append(refs[k:k + ex.n])
                k += ex.n
            core += refs[k:k + n_out]
            k += n_out
            r_out = []
            for ex, _ in self.riders:
                r_out.append(refs[k:k + ex.n])
                k += ex.n
            core += refs[k:k + n_scratch]
            k += n_scratch
            r_sem = []
            for ex, _ in self.riders:
                r_sem.append(refs[k:k + len(ex.sems)])
                k += len(ex.sems)

            @pl.when(is_first())
            def _():
                for (ex, _), a, b, s in zip(self.riders, r_in, r_out, r_sem):
                    ex.start(a, b, s)

            body(*core)

            @pl.when(is_last())
            def _():
                for (ex, _), a, b, s in zip(self.riders, r_in, r_out, r_sem):
                    ex.finish(a, b, s)

        return wrapped

    def split(self, outs, n_out):
        core, rest, per = list(outs[:n_out]), list(outs[n_out:]), []
        for ex, _ in self.riders:
            per.append(rest[:ex.n])
            rest = rest[ex.n:]
        return core, per


def pair_sum(big, sib, c, *, tr, name):
    _, R, C = big.shape

    def body(c_ref, a_ref, s_ref, o_ref):
        o_ref[...] = (a_ref[...].astype(F32) + s_ref[...].astype(F32)).astype(o_ref.dtype)

    grid_spec = pltpu.PrefetchScalarGridSpec(
        num_scalar_prefetch=1, grid=(4, R // tr),
        in_specs=[pl.BlockSpec((None, tr, C), lambda b, i, c_ref: (2 * b + c_ref[0], i, 0)),
                  pl.BlockSpec((None, tr, C), lambda b, i, c_ref: (b, i, 0))],
        out_specs=pl.BlockSpec((None, tr, C), lambda b, i, c_ref: (b, i, 0)))
    return pl.pallas_call(body, grid_spec=grid_spec, out_shape=jax.ShapeDtypeStruct((4, R, C), big.dtype),
                          compiler_params=_cparams(2), name=name)(c.reshape(1).astype(jnp.int32), big, sib)


class ChipScatter:
    def __init__(self, pres, chunks):
        self.shapes = [jax.ShapeDtypeStruct(p.shape, p.dtype) for p in pres]
        self.n = len(pres)
        self.pieces = _pieces(pres, chunks)
        n_p = len(self.pieces)
        self.sems = [pltpu.SemaphoreType.DMA((3 * n_p,)), pltpu.SemaphoreType.DMA((3 * n_p,)),
                     pltpu.SemaphoreType.DMA((n_p,))]

    def _copies(self, pre_refs, out_refs, sems):
        send_sems, recv_sems, local_sems = sems
        n_p = len(self.pieces)
        x, y, c = _my_pos()
        my_chip = 2 * x + y
        chips = [(1 - x, y), (x, 1 - y), (1 - x, 1 - y)]

        def copy(j, p, slot):
            px, py = chips[j]
            a, rows = self.pieces[p]
            src_slot, dst_slot = (2 * px + py, my_chip) if slot == "mine" else (my_chip, 2 * px + py)
            return pltpu.make_async_remote_copy(
                src_ref=pre_refs[a].at[src_slot, rows], dst_ref=out_refs[a].at[dst_slot, rows],
                send_sem=send_sems.at[j * n_p + p], recv_sem=recv_sems.at[j * n_p + p],
                device_id=(px, py, c), device_id_type=MESH)

        own = [pltpu.make_async_copy(pre_refs[a].at[my_chip, rows], out_refs[a].at[my_chip, rows], local_sems.at[p])
               for p, (a, rows) in enumerate(self.pieces)]
        sends = [copy(j, p, "mine") for j in range(3) for p in range(n_p)]
        recvs = [copy(j, p, "theirs") for j in range(3) for p in range(n_p)]
        return own, sends, recvs

    def start(self, pre_refs, out_refs, sems):
        own, sends, _ = self._copies(pre_refs, out_refs, sems)
        for cp in sends + own:
            cp.start()

    def finish(self, pre_refs, out_refs, sems):
        own, sends, recvs = self._copies(pre_refs, out_refs, sems)
        for cp in recvs:
            cp.wait_recv()
        for cp in sends:
            cp.wait_send()
        for cp in own:
            cp.wait()


def chip_scatter(pres, chunks, *, name):
    cs = ChipScatter(pres, chunks)
    n = cs.n

    def body(*refs):
        pre_refs, out_refs, sems = refs[:n], refs[n:2 * n], refs[2 * n:]
        cs.start(pre_refs, out_refs, sems)
        cs.finish(pre_refs, out_refs, sems)

    return pl.pallas_call(body, out_shape=cs.shapes, in_specs=[_ANY] * n, out_specs=[_ANY] * n,
                          scratch_shapes=cs.sems, name=name)(*pres)


def sibling_send(blks, chunks, *, name):
    n = len(blks)
    pieces = _pieces(blks, chunks)
    n_p = len(pieces)

    def body(*refs):
        x_refs, out_refs = refs[:n], refs[n:2 * n]
        send_sems, recv_sems = refs[2 * n:]
        x, y, c = _my_pos()
        cps = [pltpu.make_async_remote_copy(
            src_ref=x_refs[a].at[rows], dst_ref=out_refs[a].at[rows], send_sem=send_sems.at[p],
            recv_sem=recv_sems.at[p], device_id=(x, y, 1 - c), device_id_type=MESH)
            for p, (a, rows) in enumerate(pieces)]
        for cp in cps:
            cp.start()
        for cp in cps:
            cp.wait_recv()
        for cp in cps:
            cp.wait_send()

    return pl.pallas_call(
        body, out_shape=[jax.ShapeDtypeStruct(b.shape, b.dtype) for b in blks],
        in_specs=[_ANY] * n, out_specs=[_ANY] * n,
        scratch_shapes=[pltpu.SemaphoreType.DMA((n_p,)), pltpu.SemaphoreType.DMA((n_p,))],
        name=name)(*blks)


def reduce_slots(land, *, tr, name):
    n, R, C = land.shape

    def body(l_ref, o_ref):
        acc = l_ref[0].astype(F32)
        for s in range(1, n):
            acc = acc + l_ref[s].astype(F32)
        o_ref[...] = acc

    return pl.pallas_call(
        body, grid=(R // tr,), in_specs=[pl.BlockSpec((n, tr, C), lambda i: (0, i, 0))],
        out_specs=pl.BlockSpec((tr, C), lambda i: (i, 0)), out_shape=jax.ShapeDtypeStruct((R, C), F32),
        compiler_params=_cparams(1), name=name)(land)


TM = 512


def _tk(d):
    return min(d.shape[0], 1024)


def ffn_fwd(x, g, wgu, wd, tag):
    gu, h = norm_matmul(x, g, wgu, tm=_tk(x), tn=1408, split=True, name=f"{tag}_gu")
    xo = mm_nn(gu, wd, res=x, scale=0.5, swiglu=True, tm=TM, tn=D_MODEL, tk=1408, name=f"{tag}_down")
    return xo, (x, gu, h)


def ffn_bwd(d, saved, g, wgu, wd, tag, ride_bact=None, ride_dwgu=None, before_dx=None):
    x, gu, h = saved
    dgu, act, *rode_a = ffn_bwd_act(d, wd, gu, tm=TM, tn=1408, name=f"{tag}_bact", riding=ride_bact)
    dwd = mm_tn(act, d, scale=0.5, a_split=False, b_split=False, tm=1408, tn=D_MODEL, tk=_tk(d), name=f"{tag}_dwd")
    dwgu = mm_tn(h, dgu, scale=1.0, a_split=False, b_split=True, tm=TM, tn=1408, tk=_tk(d), out_blocked=True,
                 name=f"{tag}_dwgu", riding=ride_dwgu)
    rode_g = []
    if ride_dwgu is not None:
        dwgu, *rode_g = dwgu
    riding = before_dx(dwgu, dwd) if before_dx is not None else None
    dx, dg, *rode_x = mm_nt_normbwd(dgu, wgu, x, g, d, a_split=True, tm=_tk(d), tk=1408, name=f"{tag}_dx",
                                    riding=riding)
    return dx, dg, dwgu, dwd, (rode_a, rode_g, rode_x)


def _tile2(v):
    return jnp.concatenate([v, v], axis=-1).reshape(1, LANES)


def _fold2(v):
    return v[:, :HEAD_DIM] + v[:, HEAD_DIM:]


EVEN = dict(dil=1, nsub=2, ppk=4, q_blk=0, k_blk=4, v_blk=5, n_heads=A_Q_HEADS, group=A_GROUP, max_dist=A_WINDOW - 1)
STICK = dict(q_blk=6, k_blk=10, v_blk=14, n_pairs=4)


def _odd_cfg(dil):
    return dict(dil=dil, nsub=4 if dil == 1 else 1, ppk=1, q_blk=0, k_blk=8, v_blk=16, n_heads=C_HEADS, group=1,
                max_dist=BLK)


def even_fwd(x, g, win, qg, kg, sinks, wout, tag, riders=None, rider_args=()):
    qkv, h = norm_matmul(x, g, win, tm=_tk(x), tn=1152, split=False, name=f"{tag}_in")
    qg2, kg2 = _tile2(qg), _tile2(kg)
    slopes = jnp.asarray(_alibi(A_Q_HEADS), F32)
    qkn = qk_norm(qkv, qg2, kg2, width=768, steps=1, n_q=4, tm=TM, name=f"{tag}_qkn")
    oa, lse = banded_fwd(qkn, qkv, slopes, sinks, name=f"{tag}_swa", **EVEN)
    ob, rode = stick_fwd(qkv, name=f"{tag}_stick", riders=riders, rider_args=rider_args, **STICK)
    o = jnp.concatenate([oa, ob], axis=1)
    xo = mm_nn(o, wout, res=x, scale=1.0, swiglu=False, tm=TM, tn=D_MODEL, tk=D_MODEL, name=f"{tag}_out")
    return xo, (x, qkv, qkn, h, oa, lse, o), rode


def even_bwd(d, saved, g, win, qg, kg, sinks, wout, tag, riders=None, rider_args=(), before_dx=None):
    x, qkv, qkn, h, oa, lse, o = saved
    qg2, kg2 = _tile2(qg), _tile2(kg)
    slopes = jnp.asarray(_alibi(A_Q_HEADS), F32)
    dwout = mm_tn(o, d, scale=1.0, a_split=False, b_split=False, tm=D_MODEL, tn=D_MODEL, tk=_tk(d), name=f"{tag}_dwout")
    do = mm_nt(d, wout, tm=TM, tn=D_MODEL, tk=D_MODEL, name=f"{tag}_do")
    dqa, dka4, dva4, dsk = banded_bwd(qkn, qkv, slopes, sinks, do, oa, lse, None, None,
                                      do_blk=0, name=f"{tag}_swa_b", **EVEN)
    dqb, dkb, dvb, rode = stick_bwd(qkv, do, do_blk=4, name=f"{tag}_stick_b", riders=riders, rider_args=rider_args,
                                    **STICK)
    dqkv, dqg, dkg = assemble_even(dqa, dka4, dva4, dqb, dkb, dvb, qkv, qg2, kg2, tm=TM, name=f"{tag}_asm")
    dwin = mm_tn(h, dqkv, scale=1.0, a_split=False, b_split=False, tm=D_MODEL, tn=1152, tk=_tk(d), name=f"{tag}_dwin")
    riding = before_dx(dwin, dwout) if before_dx is not None else None
    dx, dg, *rode_x = mm_nt_normbwd(dqkv, win, x, g, d, a_split=False, tm=TM, tk=1152, name=f"{tag}_dx", riding=riding)
    return dx, dg, dwin, _fold2(dqg), _fold2(dkg), dsk[:, :A_Q_HEADS], dwout, (rode, rode_x)


def odd_fwd(x, g, win, qg, kg, wout, tag):
    qkv, h = norm_matmul(x, g, win, tm=_tk(x), tn=768, split=False, name=f"{tag}_in")
    qg2, kg2 = _tile2(qg), _tile2(kg)
    qkn = qk_norm(qkv, qg2, kg2, width=D_MODEL, steps=2, n_q=8, tm=TM, name=f"{tag}_qkn")
    outs = []
    for p, (window, dil) in enumerate(C_PATTERNS):
        slopes = jnp.asarray(_alibi(C_HEADS), F32) * float(dil)
        outs.append(banded_fwd(qkn, qkv, slopes, None, name=f"{tag}_dil{p}", **_odd_cfg(dil)))
    o, w1, w2, w3 = mix_fwd(outs[0][0], outs[1][0], outs[2][0], outs[0][1], outs[1][1], outs[2][1],
                            tm=TM, name=f"{tag}_mix")
    xo = mm_nn(o, wout, res=x, scale=1.0, swiglu=False, tm=TM, tn=D_MODEL, tk=D_MODEL, name=f"{tag}_out")
    return xo, (x, qkv, qkn, h, outs, (w1, w2, w3), o)


def odd_bwd(d, saved, g, win, qg, kg, wout, tag):
    x, qkv, qkn, h, outs, ws, o = saved
    qg2, kg2 = _tile2(qg), _tile2(kg)
    dwout = mm_tn(o, d, scale=1.0, a_split=False, b_split=False, tm=D_MODEL, tn=D_MODEL, tk=_tk(d), name=f"{tag}_dwout")
    do = mm_nt(d, wout, tm=TM, tn=D_MODEL, tk=D_MODEL, name=f"{tag}_do")
    parts = []
    for p, (window, dil) in enumerate(C_PATTERNS):
        slopes = jnp.asarray(_alibi(C_HEADS), F32) * float(dil)
        dq, dk, dv, _ = banded_bwd(qkn, qkv, slopes, None, do, outs[p][0], outs[p][1], ws[p], o,
                                   do_blk=0, name=f"{tag}_dil{p}_b", **_odd_cfg(dil))
        parts.append((dq, dk, dv))
    dqkv, dqg, dkg = assemble_odd(parts, qkv, qg2, kg2, tm=256, name=f"{tag}_asm")
    dwin = mm_tn(h, dqkv, scale=1.0, a_split=False, b_split=False, tm=TM, tn=768, tk=_tk(d), out_blocked=True,
                 name=f"{tag}_dwin")
    dx, dg = mm_nt_normbwd(dqkv, win, x, g, d, a_split=False, tm=TM, tk=768, name=f"{tag}_dx")
    return dx, dg, dwin, _fold2(dqg), _fold2(dkg), dwout


def xa_fwd(x, mem, g, gm, wq, wkv, qg, kg, wo, tag):
    qraw, h = norm_matmul(x, g, wq, tm=TM, tn=D_MODEL, split=False, name=f"{tag}_q")
    kvraw, hm = norm_matmul(mem, gm, wkv, tm=MEM_LEN, tn=512, split=False, name=f"{tag}_kv")
    o = xattn_fwd(qraw, kvraw, qg, kg, tm=TM, name=f"{tag}_att")
    xo = mm_nn(o, wo, res=x, scale=1.0, swiglu=False, tm=TM, tn=D_MODEL, tk=D_MODEL, name=f"{tag}_o")
    return xo, (x, qraw, h, kvraw, hm, o)


def xa_bwd(d, saved, mem, g, gm, wq, wkv, qg, kg, wo, tag):
    x, qraw, h, kvraw, hm, o = saved
    dwo = mm_tn(o, d, scale=1.0, a_split=False, b_split=False, tm=D_MODEL, tn=D_MODEL, tk=_tk(d), name=f"{tag}_dwo")
    do = mm_nt(d, wo, tm=TM, tn=D_MODEL, tk=D_MODEL, name=f"{tag}_do")
    dq, dkv, dqg, dkg = xattn_bwd(qraw, kvraw, qg, kg, do, o, tm=TM, name=f"{tag}_att_b")
    dwq = mm_tn(h, dq, scale=1.0, a_split=False, b_split=False, tm=D_MODEL, tn=D_MODEL, tk=_tk(d), name=f"{tag}_dwq")
    dx, dg = mm_nt_normbwd(dq, wq, x, g, d, a_split=False, tm=TM, tk=D_MODEL, name=f"{tag}_dx")
    dwkv = mm_tn(hm, dkv, scale=1.0, a_split=False, b_split=False, tm=TM, tn=512, tk=MEM_LEN, out_blocked=True,
                 name=f"{tag}_dwkv")
    _, dgm = mm_nt_normbwd(dkv, wkv, mem, gm, None, a_split=False, tm=MEM_LEN, tk=512, name=f"{tag}_dmem")
    return dx, dg, dgm, dwq, dwkv, dqg, dkg, dwo


MATS = (("ffn1_w_gu", 1), ("ffn1_w_down", 0), ("ev_w_in", 1), ("ev_w_out", 0), ("od_w_in", 1), ("od_w_out", 0),
        ("xa_w_q", 0), ("xa_w_kv", 1), ("xa_w_o", 0), ("ffn2_w_gu", 1), ("ffn2_w_down", 0))
SMALLS = ("ffn1_norm", "mix_norm", "ev_q_gain", "ev_k_gain", "ev_sinks", "od_q_gain", "od_k_gain", "xa_norm",
          "xa_mem_norm", "xa_q_gain", "xa_k_gain", "ffn2_norm")
WEIGHTS = ("ffn1_norm", "ffn1_w_gu", "ffn1_w_down", "mix_norm", "ev_w_in", "ev_q_gain", "ev_k_gain", "ev_sinks",
           "ev_w_out", "od_w_in", "od_q_gain", "od_k_gain", "od_w_out", "xa_norm", "xa_mem_norm", "xa_w_q",
           "xa_w_kv", "xa_q_gain", "xa_k_gain", "xa_w_o", "ffn2_norm", "ffn2_w_gu", "ffn2_w_down")
SMALL_ROWS = 16
LAYER_GROUPS = (
    (((("ffn1_w_gu", 0), ("ffn2_w_gu", 0)), 4, 512),
     ((("ffn1_w_down", 0), ("ffn2_w_down", 0)), 2, 352),
     ((("ev_w_out", 0), ("xa_w_q", 0), ("xa_w_o", 0)), 1, 384),
     ((("xa_w_kv", 0),), 1, 512),
     ((("ev_w_in", 0),), 1, 512)),
    (((("ffn1_w_gu", 1), ("ffn2_w_gu", 1)), 4, 512),
     ((("ffn1_w_down", 1), ("ffn2_w_down", 1)), 2, 352),
     ((("od_w_out", 0), ("xa_w_q", 1), ("xa_w_o", 1)), 1, 384),
     ((("xa_w_kv", 1),), 1, 512),
     ((("od_w_in", 0),), 1, 512)),
)
GROUPS = LAYER_GROUPS[0] + LAYER_GROUPS[1]
ROUNDS = {
    "1": LAYER_GROUPS[1],
    "0a": (((("ffn2_w_gu", 0),), 2, 512), ((("ffn2_w_down", 0),), 1, 352)) + LAYER_GROUPS[0][2:],
    "0b": (((("ffn1_w_gu", 0),), 2, 512), ((("ffn1_w_down", 0),), 1, 352)),
}


def _chunks_of(groups):
    return tuple(g[1] for g in groups)
COL_SHARDED = {name for name, axis in MATS if axis == 1}
BLOCKED = {"ffn1_w_gu", "ffn2_w_gu", "xa_w_kv", "od_w_in"}


def group_halves(shards, c, groups):
    out = []
    for members, _, _ in groups:
        halves = []
        for name, layer in members:
            _, r, cc = shards[name].shape
            half = lax.dynamic_index_in_dim(shards[name][layer].reshape(2, r // 2, cc), c, 0, keepdims=False)
            halves.append(half.astype(BF16))
        out.append(jnp.concatenate(halves, axis=0))
    return out


def full_weights(gathered, shards, groups):
    full = {}
    for (members, _, _), arr in zip(groups, gathered):
        for w, (name, layer) in enumerate(members):
            _, r, cc = shards[name].shape
            piece = arr[:, w * (r // 2):(w + 1) * (r // 2)].reshape(4, r, cc)
            if name not in COL_SHARDED:
                piece = piece.reshape(4 * r, cc)
            elif name not in BLOCKED:
                piece = piece.transpose(1, 0, 2).reshape(r, 4 * cc)
            full[(name, layer)] = piece
    return full


def group_grads(grads, shards, groups):
    out = []
    for members, _, _ in groups:
        parts = []
        for name, layer in members:
            _, r, cc = shards[name].shape
            gfull = grads[(name, layer)]
            if name in COL_SHARDED and name not in BLOCKED:
                gfull = gfull.reshape(2, r // 2, 4, cc).transpose(2, 0, 1, 3)
            parts.append(gfull.reshape(N_DEV, r // 2, cc))
        out.append(jnp.concatenate(parts, axis=1))
    return out


def shard_grads(mine, theirs, c, shards, groups):
    per = {}
    for (members, _, _), a, b in zip(groups, mine, theirs):
        for w, (name, layer) in enumerate(members):
            _, r, cc = shards[name].shape
            rows = slice(w * (r // 2), (w + 1) * (r // 2))
            lo = jnp.where(c == 0, a[rows], b[rows])
            hi = jnp.where(c == 0, b[rows], a[rows])
            per[(name, layer)] = jnp.concatenate([lo, hi], axis=0)
    return per


def pack_small(vals):
    row10 = jnp.concatenate([vals["xa_q_gain"].reshape(1, 512), vals["xa_k_gain"].reshape(1, 512)], axis=1)
    row11 = jnp.concatenate([vals["ev_q_gain"], vals["ev_k_gain"], vals["od_q_gain"], vals["od_k_gain"],
                             vals["ev_sinks"], jnp.zeros((1, 1024 - 4 * 64 - 8), F32)], axis=1)
    return jnp.concatenate([vals["ffn1_norm"], vals["mix_norm"], vals["xa_norm"], vals["xa_mem_norm"],
                            vals["ffn2_norm"], row10, row11, jnp.zeros((SMALL_ROWS - 12, 1024), F32)], axis=0)


def unpack_small(arr):
    return {"ffn1_norm": arr[0:2], "mix_norm": arr[2:4], "xa_norm": arr[4:6], "xa_mem_norm": arr[6:8],
            "ffn2_norm": arr[8:10],
            "xa_q_gain": arr[10:11, 0:512].reshape(2, 256), "xa_k_gain": arr[10:11, 512:1024].reshape(2, 256),
            "ev_q_gain": arr[11:12, 0:64], "ev_k_gain": arr[11:12, 64:128], "od_q_gain": arr[11:12, 128:192],
            "od_k_gain": arr[11:12, 192:256], "ev_sinks": arr[11:12, 256:264]}


def local_step(x, mem, target, W, small, prereduce, later):
    depth = small["ffn1_norm"].shape[0]

    def row(name, l):
        return small[name][l:l + 1]

    saved = []
    for l in range(depth):
        j = l // 2
        x, s1 = ffn_fwd(x, row("ffn1_norm", l), W[("ffn1_w_gu", l)], W[("ffn1_w_down", l)], f"l{l}_f1")
        if l % 2 == 0:
            riders, rider_args = None, ()
            if later is not None and l == 0:
                riders, rider_args = GatherBlocks(later[0], later[1]), later[0]
            x, s2, rode = even_fwd(x, row("mix_norm", l), W[("ev_w_in", j)], row("ev_q_gain", j),
                                   row("ev_k_gain", j), small["ev_sinks"][j], W[("ev_w_out", j)], f"l{l}_ev",
                                   riders=riders, rider_args=rider_args)
            if riders is not None:
                W = {**W, **later[2](rode)}
        else:
            x, s2 = odd_fwd(x, row("mix_norm", l), W[("od_w_in", j)], row("od_q_gain", j), row("od_k_gain", j),
                            W[("od_w_out", j)], f"l{l}_od")
        x, s3 = xa_fwd(x, mem, row("xa_norm", l), row("xa_mem_norm", l), W[("xa_w_q", l)], W[("xa_w_kv", l)],
                       row("xa_q_gain", l), row("xa_k_gain", l), W[("xa_w_o", l)], f"l{l}_xa")
        x, s4 = ffn_fwd(x, row("ffn2_norm", l), W[("ffn2_w_gu", l)], W[("ffn2_w_down", l)], f"l{l}_f2")
        saved.append((s1, s2, s3, s4))
    loss, d = loss_kernel(x, target, tm=TM, name="loss")

    gw = {}
    gs = {name: [None] * small[name].shape[0] for name in SMALLS}
    pending, landed = None, {}
    for l in reversed(range(depth)):
        j = l // 2
        s1, s2, s3, s4 = saved[l]
        d, dg, dwgu, dwd, _ = ffn_bwd(d, s4, row("ffn2_norm", l), W[("ffn2_w_gu", l)], W[("ffn2_w_down", l)],
                                      f"l{l}_f2")
        gs["ffn2_norm"][l] = dg
        gw[("ffn2_w_gu", l)], gw[("ffn2_w_down", l)] = dwgu, dwd
        d, dg, dgm, dwq, dwkv, dqg, dkg, dwo = xa_bwd(
            d, s3, mem, row("xa_norm", l), row("xa_mem_norm", l), W[("xa_w_q", l)], W[("xa_w_kv", l)],
            row("xa_q_gain", l), row("xa_k_gain", l), W[("xa_w_o", l)], f"l{l}_xa")
        gs["xa_norm"][l], gs["xa_mem_norm"][l], gs["xa_q_gain"][l], gs["xa_k_gain"][l] = dg, dgm, dqg, dkg
        gw[("xa_w_q", l)], gw[("xa_w_kv", l)], gw[("xa_w_o", l)] = dwq, dwkv, dwo
        split = l == 0 and l % 2 == 0 and ("0a" in ROUNDS)
        early = []
        pre_early = None
        if l % 2 == 0:
            riders, rider_args = None, ()
            if pending is not None:
                riders, rider_args = ChipScatter(pending[1], _chunks_of(ROUNDS[pending[0]])), pending[1]

            def before_mixer_dx(dwin, dwout, j=j, early=early):
                gw[("ev_w_in", j)], gw[("ev_w_out", j)] = dwin, dwout
                early += prereduce.pack(gw, "0a")
                return Riding([(PairExchange(early, _chunks_of(ROUNDS["0a"])), early)])

            d, dg, dwin, dqg, dkg, dsk, dwout, (rode, rode_x) = even_bwd(
                d, s2, row("mix_norm", l), W[("ev_w_in", j)], row("ev_q_gain", j), row("ev_k_gain", j),
                small["ev_sinks"][j], W[("ev_w_out", j)], f"l{l}_ev", riders=riders, rider_args=rider_args,
                before_dx=before_mixer_dx if split else None)
            if pending is not None:
                landed[pending[0]], pending = rode, None
            gs["ev_q_gain"][j], gs["ev_k_gain"][j], gs["ev_sinks"][j] = dqg, dkg, dsk
            gw[("ev_w_in", j)], gw[("ev_w_out", j)] = dwin, dwout
            if split:
                pre_early = prereduce.sums(early, rode_x[0], "0a")
        else:
            d, dg, dwin, dqg, dkg, dwout = odd_bwd(
                d, s2, row("mix_norm", l), W[("od_w_in", j)], row("od_q_gain", j), row("od_k_gain", j),
                W[("od_w_out", j)], f"l{l}_od")
            gs["od_q_gain"][j], gs["od_k_gain"][j] = dqg, dkg
            gw[("od_w_in", j)], gw[("od_w_out", j)] = dwin, dwout
        gs["mix_norm"][l] = dg
        packed = []

        rnd = "0b" if pre_early is not None else str(l)

        def before_dx(dwgu, dwd, l=l, packed=packed, rnd=rnd):
            gw[("ffn1_w_gu", l)], gw[("ffn1_w_down", l)] = dwgu, dwd
            packed += prereduce.pack(gw, rnd)
            return Riding([(PairExchange(packed, _chunks_of(ROUNDS[rnd])), packed)])

        ride_bact = ride_dwgu = None
        if pre_early is not None:
            chunks = _chunks_of(ROUNDS["0a"])
            ride_bact = Riding([(ChipScatter(pre_early[:2], chunks[:2]), pre_early[:2])])
            ride_dwgu = Riding([(ChipScatter(pre_early[2:], chunks[2:]), pre_early[2:])])
        d, dg, dwgu, dwd, (rode_a, rode_g, rode_x) = ffn_bwd(
            d, s1, row("ffn1_norm", l), W[("ffn1_w_gu", l)], W[("ffn1_w_down", l)], f"l{l}_f1",
            ride_bact=ride_bact, ride_dwgu=ride_dwgu, before_dx=before_dx)
        gs["ffn1_norm"][l] = dg
        if pre_early is not None:
            landed["0a"] = list(rode_a[0]) + list(rode_g[0])
        if pending is not None:
            landed[pending[0]] = chip_scatter(pending[1], _chunks_of(ROUNDS[pending[0]]),
                                              name=f"scatter_grads{pending[0]}")
        pending = (rnd, prereduce.sums(packed, rode_x[0], rnd))
    landed[pending[0]] = chip_scatter(pending[1], _chunks_of(ROUNDS[pending[0]]), name=f"scatter_grads{pending[0]}")
    gsmall = {name: jnp.concatenate(v, axis=0) for name, v in gs.items()}
    return loss, d, landed, gsmall


def kernel(x, mem, ffn1_norm, ffn1_w_gu, ffn1_w_down, mix_norm, ev_w_in, ev_q_gain, ev_k_gain, ev_sinks, ev_w_out, od_w_in, od_q_gain, od_k_gain, od_w_out, xa_norm, xa_mem_norm, xa_w_q, xa_w_kv, xa_q_gain, xa_k_gain, xa_w_o, ffn2_norm, ffn2_w_gu, ffn2_w_down, loss_target, m_ffn1_norm, m_ffn1_w_gu, m_ffn1_w_down, m_mix_norm, m_ev_w_in, m_ev_q_gain, m_ev_k_gain, m_ev_sinks, m_ev_w_out, m_od_w_in, m_od_q_gain, m_od_k_gain, m_od_w_out, m_xa_norm, m_xa_mem_norm, m_xa_w_q, m_xa_w_kv, m_xa_q_gain, m_xa_k_gain, m_xa_w_o, m_ffn2_norm, m_ffn2_w_gu, m_ffn2_w_down, v_ffn1_norm, v_ffn1_w_gu, v_ffn1_w_down, v_mix_norm, v_ev_w_in, v_ev_q_gain, v_ev_k_gain, v_ev_sinks, v_ev_w_out, v_od_w_in, v_od_q_gain, v_od_k_gain, v_od_w_out, v_xa_norm, v_xa_mem_norm, v_xa_w_q, v_xa_w_kv, v_xa_q_gain, v_xa_k_gain, v_xa_w_o, v_ffn2_norm, v_ffn2_w_gu, v_ffn2_w_down):
    given = dict(locals())
    w = {n: given[n] for n in WEIGHTS}
    m = {n: given["m_" + n] for n in WEIGHTS}
    v = {n: given["v_" + n] for n in WEIGHTS}
    c = lax.axis_index("c")
    shards = {name: w[name] for name, _ in MATS}
    small = {n: w[n] for n in SMALLS}

    groups0, groups1 = LAYER_GROUPS
    gathered = gather_blocks(group_halves(shards, c, groups0), _chunks_of(groups0), name="gather_weights0")
    full = full_weights(gathered, shards, groups0)
    later = (group_halves(shards, c, groups1), _chunks_of(groups1), lambda got: full_weights(got, shards, groups1))

    class prereduce:
        @staticmethod
        def pack(gw, rnd):
            return group_grads(gw, shards, ROUNDS[rnd])

        @staticmethod
        def sums(packed, sib, rnd):
            return [pair_sum(p, s, c, tr=g[2], name=f"pair_sum{rnd}_{i}")
                    for i, (g, p, s) in enumerate(zip(ROUNDS[rnd], packed, sib))]

    loss_b, grad_x, landed, gsmall = local_step(x[0], mem[0], loss_target[0], full, small, prereduce, later)

    per = {}
    for rnd, land in sorted(landed.items()):
        groups = ROUNDS[rnd]
        mine = [reduce_slots(a, tr=g[2], name=f"sum_grads{rnd}_{i}") for i, (g, a) in enumerate(zip(groups, land))]
        theirs = sibling_send(mine, _chunks_of(groups), name=f"swap_grads{rnd}")
        per.update(shard_grads(mine, theirs, c, shards, groups))
    g = {name: jnp.stack([per[(name, layer)] for layer in range(w[name].shape[0])], axis=0) for name, _ in MATS}
    land_small = gather_small(pack_small(gsmall), name="gather_small")
    g_small = unpack_small(reduce_slots(land_small, tr=SMALL_ROWS, name="sum_small"))
    g.update(g_small)

    delta, new_m, new_v = {}, {}, {}
    for name, _ in MATS:
        shp = w[name].shape
        flat = [a.reshape(-1, shp[-1]) for a in (w[name], g[name], m[name], v[name])]
        dl, nm, nv = adamw(*flat, br=BLK, name=f"adamw_{name}")
        delta[name], new_m[name], new_v[name] = dl.reshape(shp), nm.reshape(shp), nv.reshape(shp)
    dl, nm, nv = adamw(pack_small(small), pack_small(g_small), pack_small({n: m[n] for n in SMALLS}),
                       pack_small({n: v[n] for n in SMALLS}), br=SMALL_ROWS, name="adamw_small")
    for dst, arr in ((delta, dl), (new_m, nm), (new_v, nv)):
        dst.update(unpack_small(arr))

    loss = lax.psum(loss_b[0, 0], ("x", "y", "c"))
    return (loss, grad_x[None], *[g[n] for n in WEIGHTS], *[delta[n] for n in WEIGHTS],
            *[new_m[n] for n in WEIGHTS], *[new_v[n] for n in WEIGHTS])
```

```python
import jax
import jax.numpy as jnp
from jax import lax
from jax.experimental import pallas as pl
from jax.experimental.pallas import tpu as pltpu

F32 = jnp.float32
BF16 = jnp.bfloat16

D_MODEL = 1024
HEAD_DIM = 64
LANES = 128
BLK = 128
D_FF = 2816
RMS_EPS = 1e-6
MEM_LEN = 256
X_HEADS = 4
X_HEAD_DIM = 256
A_Q_HEADS = 8
A_GROUP = 4
A_WINDOW = 128
C_HEADS = 16
C_PATTERNS = ((128, 1), (512, 4), (2048, 16))
NEG = -1e30
VMEM_LIMIT = 56 * 2 ** 20

ADAM_LR = 0.001
ADAM_B1 = 0.9
ADAM_B2 = 0.999
ADAM_EPS = 1e-08
ADAM_WD = 0.01
ADAM_STEP = 10

N_DEV = 8
MESH = pl.DeviceIdType.MESH


def _cparams(n):
    return pltpu.CompilerParams(dimension_semantics=("arbitrary",) * n, vmem_limit_bytes=VMEM_LIMIT)


def _dot(a, b):
    return jnp.dot(a, b, preferred_element_type=F32)


def _dot_nt(a, b):
    return lax.dot_general(a, b, (((1,), (1,)), ((), ())), preferred_element_type=F32)


def _dot_tn(a, b):
    return lax.dot_general(a, b, (((0,), (0,)), ((), ())), preferred_element_type=F32)


def _sigmoid(z):
    return 1.0 / (1.0 + jnp.exp(-z))


def norm_matmul(x, g, w, *, tm, tn, split, name):
    T, K = x.shape
    blocked = w.ndim == 3
    assert not blocked or w.shape[2] == tn
    N = w.shape[0] * w.shape[2] if blocked else w.shape[1]
    nj = N // tn

    def body(x_ref, g_ref, w_ref, o_ref, h_ref):
        @pl.when(pl.program_id(1) == 0)
        def _():
            xv = x_ref[...]
            r = lax.rsqrt(jnp.mean(xv * xv, axis=-1, keepdims=True) + RMS_EPS)
            h_ref[...] = (xv * r * g_ref[...]).astype(BF16)

        o_ref[...] = _dot(h_ref[...], w_ref[...]).astype(o_ref.dtype)

    if split:
        njh = nj // 2
        o_shape = jax.ShapeDtypeStruct((2, T, N // 2), BF16)
        o_spec = pl.BlockSpec((None, tm, tn), lambda i, j: (j // njh, i, j % njh))
    else:
        o_shape = jax.ShapeDtypeStruct((T, N), F32)
        o_spec = pl.BlockSpec((tm, tn), lambda i, j: (i, j))
    return pl.pallas_call(
        body, grid=(T // tm, nj),
        in_specs=[pl.BlockSpec((tm, K), lambda i, j: (i, 0)),
                  pl.BlockSpec((1, K), lambda i, j: (0, 0)),
                  (pl.BlockSpec((None, K, tn), lambda i, j: (j, 0, 0)) if blocked
                   else pl.BlockSpec((K, tn), lambda i, j: (0, j)))],
        out_specs=[o_spec, pl.BlockSpec((tm, K), lambda i, j: (i, 0))],
        out_shape=[o_shape, jax.ShapeDtypeStruct((T, K), BF16)],
        compiler_params=_cparams(2), name=name)(x, g, w)


def mm_nn(a, b, *, res, scale, swiglu, tm, tn, tk, name):
    T = a.shape[-2]
    K, N = b.shape
    nk = K // tk

    def body(*refs):
        if swiglu:
            g_ref, u_ref, b_ref, r_ref, o_ref, acc = refs
        else:
            a_ref, b_ref, r_ref, o_ref, acc = refs
        k = pl.program_id(2)

        @pl.when(k == 0)
        def _():
            acc[...] = jnp.zeros_like(acc)

        if swiglu:
            gv = g_ref[...].astype(F32)
            av = (gv * _sigmoid(gv) * u_ref[...].astype(F32)).astype(BF16)
        else:
            av = a_ref[...].astype(BF16)
        acc[...] += _dot(av, b_ref[...])

        @pl.when(k == nk - 1)
        def _():
            o_ref[...] = r_ref[...] + scale * acc[...]

    if swiglu:
        a_specs = [pl.BlockSpec((None, tm, tk), lambda i, j, k: (0, i, k)),
                   pl.BlockSpec((None, tm, tk), lambda i, j, k: (1, i, k))]
        a_args = [a, a]
    else:
        a_specs = [pl.BlockSpec((tm, tk), lambda i, j, k: (i, k))]
        a_args = [a]
    return pl.pallas_call(
        body, grid=(T // tm, N // tn, nk),
        in_specs=a_specs + [pl.BlockSpec((tk, tn), lambda i, j, k: (k, j)),
                            pl.BlockSpec((tm, tn), lambda i, j, k: (i, j))],
        out_specs=pl.BlockSpec((tm, tn), lambda i, j, k: (i, j)),
        out_shape=jax.ShapeDtypeStruct((T, N), F32),
        scratch_shapes=[pltpu.VMEM((tm, tn), F32)],
        compiler_params=_cparams(3), name=name)(*a_args, b, res)


def mm_nt(a, b, *, tm, tn, tk, name):
    T, K = a.shape
    N = b.shape[0]
    nk = K // tk

    def body(a_ref, b_ref, o_ref, acc):
        k = pl.program_id(2)

        @pl.when(k == 0)
        def _():
            acc[...] = jnp.zeros_like(acc)

        acc[...] += _dot_nt(a_ref[...].astype(BF16), b_ref[...])

        @pl.when(k == nk - 1)
        def _():
            o_ref[...] = acc[...]

    return pl.pallas_call(
        body, grid=(T // tm, N // tn, nk),
        in_specs=[pl.BlockSpec((tm, tk), lambda i, j, k: (i, k)),
                  pl.BlockSpec((tn, tk), lambda i, j, k: (j, k))],
        out_specs=pl.BlockSpec((tm, tn), lambda i, j, k: (i, j)),
        out_shape=jax.ShapeDtypeStruct((T, N), F32),
        scratch_shapes=[pltpu.VMEM((tm, tn), F32)],
        compiler_params=_cparams(3), name=name)(a, b)


def ffn_bwd_act(d, wd, gu, *, tm, tn, name, riding=None):
    T, K = d.shape
    Fd = wd.shape[0]
    ni = T // tm

    def body(d_ref, w_ref, g_ref, u_ref, dgu_ref, dwd_ref, acc):
        i = pl.program_id(1)

        @pl.when(i == 0)
        def _():
            acc[...] = jnp.zeros_like(acc)

        db = d_ref[...].astype(BF16)
        da = 0.5 * _dot_nt(db, w_ref[...])
        gv = g_ref[...].astype(F32)
        uv = u_ref[...].astype(F32)
        s = _sigmoid(gv)
        silu = gv * s
        acc[...] += _dot_tn((silu * uv).astype(BF16), db)
        dgu_ref[0] = (da * uv * (s * (1.0 + gv * (1.0 - s)))).astype(BF16)
        dgu_ref[1] = (da * silu).astype(BF16)

        @pl.when(i == ni - 1)
        def _():
            dwd_ref[...] = (0.5 * acc[...]).astype(BF16)

    in_specs = [pl.BlockSpec((tm, K), lambda j, i: (i, 0)),
                pl.BlockSpec((tn, K), lambda j, i: (j, 0)),
                pl.BlockSpec((None, tm, tn), lambda j, i: (0, i, j)),
                pl.BlockSpec((None, tm, tn), lambda j, i: (1, i, j))]
    out_specs = [pl.BlockSpec((2, tm, tn), lambda j, i: (0, i, j)), pl.BlockSpec((tn, K), lambda j, i: (j, 0))]
    out_shape = [jax.ShapeDtypeStruct((2, T, Fd), BF16), jax.ShapeDtypeStruct((Fd, K), BF16)]
    return _call_with_riders(body, riding, (Fd // tn, ni), in_specs, out_specs, out_shape,
                             [pltpu.VMEM((tn, K), F32)], [d, wd, gu, gu], name)


def _call_with_riders(body, riding, grid, in_specs, out_specs, out_shape, scratch, args, name):
    n_out = len(out_shape)
    if riding is None:
        return pl.pallas_call(body, grid=grid, in_specs=in_specs, out_specs=out_specs, out_shape=out_shape,
                              scratch_shapes=scratch, compiler_params=_cparams(len(grid)), name=name)(*args)

    def is_first():
        ok = pl.program_id(0) == 0
        for ax in range(1, len(grid)):
            ok = ok & (pl.program_id(ax) == 0)
        return ok

    def is_last():
        ok = pl.program_id(0) == grid[0] - 1
        for ax in range(1, len(grid)):
            ok = ok & (pl.program_id(ax) == grid[ax] - 1)
        return ok

    outs = pl.pallas_call(
        riding.wrap(body, len(in_specs), n_out, len(scratch), is_first, is_last), grid=grid,
        in_specs=list(in_specs) + riding.in_specs, out_specs=list(out_specs) + riding.out_specs,
        out_shape=list(out_shape) + riding.out_shapes, scratch_shapes=list(scratch) + riding.scratch,
        compiler_params=_cparams(len(grid)), name=name)(*args, *riding.args)
    core, per = riding.split(outs, n_out)
    return (*core, *per)


def mm_nt_normbwd(a, b, x, g, res, *, a_split, tm, tk, name, riding=None):
    T, Dm = x.shape
    blocked = b.ndim == 3
    assert not blocked or b.shape[2] == tk
    K = b.shape[0] * b.shape[2] if blocked else b.shape[1]
    nk = K // tk
    nkh = nk // 2
    has_res = res is not None

    def body(*refs):
        if has_res:
            a_ref, b_ref, x_ref, g_ref, r_ref, dx_ref, dg_ref, acc = refs
        else:
            a_ref, b_ref, x_ref, g_ref, dx_ref, dg_ref, acc = refs
        i = pl.program_id(0)
        k = pl.program_id(1)

        @pl.when(k == 0)
        def _():
            acc[...] = jnp.zeros_like(acc)

        acc[...] += _dot_nt(a_ref[...].astype(BF16), b_ref[...])

        @pl.when(k == nk - 1)
        def _():
            xv = x_ref[...]
            r = lax.rsqrt(jnp.mean(xv * xv, axis=-1, keepdims=True) + RMS_EPS)
            xh = xv * r
            dh = acc[...]
            dxh = dh * g_ref[...]
            dx = r * (dxh - xh * jnp.mean(dxh * xh, axis=-1, keepdims=True))
            if has_res:
                dx = dx + r_ref[...]
            dx_ref[...] = dx
            part = jnp.sum(dh * xh, axis=0, keepdims=True)

            @pl.when(i == 0)
            def _():
                dg_ref[...] = part

            @pl.when(i > 0)
            def _():
                dg_ref[...] += part

    if a_split:
        a_spec = pl.BlockSpec((None, tm, tk), lambda i, k: (k // nkh, i, k % nkh))
    else:
        a_spec = pl.BlockSpec((tm, tk), lambda i, k: (i, k))
    in_specs = [a_spec,
                (pl.BlockSpec((None, Dm, tk), lambda i, k: (k, 0, 0)) if blocked
                 else pl.BlockSpec((Dm, tk), lambda i, k: (0, k))),
                pl.BlockSpec((tm, Dm), lambda i, k: (i, 0)),
                pl.BlockSpec((1, Dm), lambda i, k: (0, 0))]
    args = [a, b, x, g]
    if has_res:
        in_specs.append(pl.BlockSpec((tm, Dm), lambda i, k: (i, 0)))
        args.append(res)
    out_specs = [pl.BlockSpec((tm, Dm), lambda i, k: (i, 0)), pl.BlockSpec((1, Dm), lambda i, k: (0, 0))]
    out_shape = [jax.ShapeDtypeStruct((T, Dm), F32), jax.ShapeDtypeStruct((1, Dm), F32)]
    scratch = [pltpu.VMEM((tm, Dm), F32)]
    return _call_with_riders(body, riding, (T // tm, nk), in_specs, out_specs, out_shape, scratch, args, name)


def mm_tn(a, b, *, scale, a_split, b_split, tm, tn, tk, name, out_blocked=False, riding=None):
    T = a.shape[-2]
    M = a.shape[-1] * (2 if a_split else 1)
    N = b.shape[-1] * (2 if b_split else 1)
    ni, nj, nk = M // tm, N // tn, T // tk
    nih, njh = ni // 2, nj // 2

    def body(a_ref, b_ref, o_ref, acc):
        k = pl.program_id(2)

        @pl.when(k == 0)
        def _():
            acc[...] = jnp.zeros_like(acc)

        acc[...] += _dot_tn(a_ref[...].astype(BF16), b_ref[...].astype(BF16))

        @pl.when(k == nk - 1)
        def _():
            o_ref[...] = (acc[...] * scale).astype(o_ref.dtype)

    if a_split:
        a_spec = pl.BlockSpec((None, tk, tm), lambda i, j, k: (i // nih, k, i % nih))
    else:
        a_spec = pl.BlockSpec((tk, tm), lambda i, j, k: (k, i))
    if b_split:
        b_spec = pl.BlockSpec((None, tk, tn), lambda i, j, k: (j // njh, k, j % njh))
    else:
        b_spec = pl.BlockSpec((tk, tn), lambda i, j, k: (k, j))
    if out_blocked:
        o_spec = pl.BlockSpec((None, None, tm, tn), lambda i, j, k: (j, i, 0, 0))
        o_shape = jax.ShapeDtypeStruct((nj, ni, tm, tn), BF16)
    else:
        o_spec = pl.BlockSpec((tm, tn), lambda i, j, k: (i, j))
        o_shape = jax.ShapeDtypeStruct((M, N), BF16)
    outs = _call_with_riders(body, riding, (ni, nj, nk), [a_spec, b_spec], [o_spec], [o_shape],
                             [pltpu.VMEM((tm, tn), F32)], [a, b], name)
    return outs[0] if riding is None else tuple(outs)


def loss_kernel(y, target, *, tm, name):
    T, Dm = y.shape

    def body(y_ref, t_ref, l_ref, dy_ref):
        e = y_ref[...] - t_ref[...]
        dy_ref[...] = e * (1.0 / Dm)
        part = (0.5 / Dm) * jnp.sum(jnp.sum(e * e, axis=-1, keepdims=True), axis=0, keepdims=True)
        part = jnp.broadcast_to(part, (8, LANES))

        @pl.when(pl.program_id(0) == 0)
        def _():
            l_ref[...] = part

        @pl.when(pl.program_id(0) > 0)
        def _():
            l_ref[...] += part

    return pl.pallas_call(
        body, grid=(T // tm,),
        in_specs=[pl.BlockSpec((tm, Dm), lambda i: (i, 0)), pl.BlockSpec((tm, Dm), lambda i: (i, 0))],
        out_specs=[pl.BlockSpec((8, LANES), lambda i: (0, 0)), pl.BlockSpec((tm, Dm), lambda i: (i, 0))],
        out_shape=[jax.ShapeDtypeStruct((8, LANES), F32), jax.ShapeDtypeStruct((T, Dm), F32)],
        compiler_params=_cparams(1), name=name)(y, target)


def adamw(w, g, m, v, *, br, name):
    R, C = w.shape

    def body(w_ref, g_ref, m_ref, v_ref, d_ref, nm_ref, nv_ref):
        gv = g_ref[...]
        nm = ADAM_B1 * m_ref[...] + (1.0 - ADAM_B1) * gv
        nv = ADAM_B2 * v_ref[...] + (1.0 - ADAM_B2) * (gv * gv)
        m_hat = nm / (1.0 - ADAM_B1 ** ADAM_STEP)
        v_hat = nv / (1.0 - ADAM_B2 ** ADAM_STEP)
        d_ref[...] = -ADAM_LR * (m_hat / (jnp.sqrt(v_hat) + ADAM_EPS) + ADAM_WD * w_ref[...])
        nm_ref[...] = nm
        nv_ref[...] = nv

    spec = pl.BlockSpec((br, C), lambda i: (i, 0))
    shp = jax.ShapeDtypeStruct((R, C), F32)
    return pl.pallas_call(
        body, grid=(R // br,), in_specs=[spec] * 4, out_specs=[spec] * 3, out_shape=[shp] * 3,
        compiler_params=_cparams(1), name=name)(w, g, m, v)


def _lane0():
    return lax.broadcasted_iota(jnp.int32, (1, LANES), 1) < HEAD_DIM


def _half_sum(x, m0):
    s0 = jnp.sum(jnp.where(m0, x, 0.0), axis=-1, keepdims=True)
    s1 = jnp.sum(jnp.where(m0, 0.0, x), axis=-1, keepdims=True)
    return jnp.where(m0, s0, s1)


def _half_pick(x, m0, e):
    sel = m0 if e == 0 else jnp.logical_not(m0)
    return jnp.max(jnp.where(sel, x, NEG), axis=-1, keepdims=True)


def _head_rms(x, m0):
    return lax.rsqrt(_half_sum(x * x, m0) * (1.0 / HEAD_DIM) + RMS_EPS)


def _alibi(n):
    return [float(2.0 ** (-8.0 * (h + 1) / n)) for h in range(n)]


def _mask_half(x, m0, e):
    return jnp.where(m0, x, 0.0) if e == 0 else jnp.where(m0, 0.0, x)


def _band_masks2(max_dist, has_prev, live):
    row = lax.broadcasted_iota(jnp.int32, (2 * BLK, 2 * BLK), 0)
    col = lax.broadcasted_iota(jnp.int32, (2 * BLK, 2 * BLK), 1)
    dist = (row & (BLK - 1)) - col + BLK
    lim = jnp.where(live, max_dist, -1)
    first = jnp.where(has_prev, 0, BLK)
    valid = (dist >= 0) & (dist <= lim) & (col >= first)
    top = lax.broadcasted_iota(jnp.int32, (2 * BLK, 1), 0) < BLK
    return dist.astype(F32), valid, top


def _stack_heads(x, m0, kes):
    parts = []
    for e in range(2):
        h = _mask_half(x, m0, e)
        parts.append(pltpu.roll(h, HEAD_DIM, 1) if kes[e] != e else h)
    return jnp.concatenate(parts, axis=0)


def _unstack_heads(y, m0, kes):
    parts = []
    for e in range(2):
        h = y[e * BLK:(e + 1) * BLK]
        parts.append(pltpu.roll(h, HEAD_DIM, 1) if kes[e] != e else h)
    return jnp.where(m0, parts[0], parts[1])


def _rows(r, dil):
    return pl.ds(r, BLK, stride=dil) if dil > 1 else pl.ds(0, BLK)


def _band_units(dil, nsub):
    assert dil == 1 or nsub == 1
    if nsub == 1:
        return [(_rows(r, dil), ("prev", _rows(r, dil)), 0) for r in range(dil)]
    units = [(pl.ds(0, BLK), ("prev", pl.ds(0, BLK)), 0)]
    units += [(pl.ds(BLK * s, BLK), ("cur", pl.ds(BLK * (s - 1), BLK)), s) for s in range(1, nsub)]
    return units


def _band_specs(dil, nsub, ppk, q_blk, k_blk, v_blk, kv_shared, nb):
    RB = BLK * dil * nsub
    PB = BLK if nsub > 1 else RB
    qw = LANES * ppk
    kw = LANES if kv_shared else qw

    def cur(i):
        return jnp.minimum(i, nb - 1)

    def prev(i):
        return jnp.maximum(i * nsub - 1, 0) if nsub > 1 else jnp.maximum(i - 1, 0)

    def kidx(base):
        return (lambda p, i: (cur(i), base)) if kv_shared else (lambda p, i: (cur(i), base + p))

    def pidx(base):
        return (lambda p, i: (prev(i), base)) if kv_shared else (lambda p, i: (prev(i), base + p))

    return [pl.BlockSpec((RB, qw), lambda p, i: (cur(i), q_blk + p)),
            pl.BlockSpec((RB, kw), kidx(k_blk)), pl.BlockSpec((PB, kw), pidx(k_blk)),
            pl.BlockSpec((RB, kw), kidx(v_blk)), pl.BlockSpec((PB, kw), pidx(v_blk))]


def qk_norm(qkv, q_gain2, k_gain2, *, width, steps, n_q, tm, name):
    T = qkv.shape[0]
    nsb = width // LANES

    def body(x_ref, qg_ref, kg_ref, o_ref):
        m0 = _lane0()
        for b in range(nsb):
            is_q = ((pl.program_id(1) * nsb + b) < n_q).astype(F32)
            gain = qg_ref[...] * is_q + kg_ref[...] * (1.0 - is_q)
            cols = pl.ds(LANES * b, LANES)
            xv = x_ref[:, cols]
            o_ref[:, cols] = xv * _head_rms(xv, m0) * gain

    gspec = pl.BlockSpec((1, LANES), lambda i, j: (0, 0))
    return pl.pallas_call(
        body, grid=(T // tm, steps),
        in_specs=[pl.BlockSpec((tm, width), lambda i, j: (i, j)), gspec, gspec],
        out_specs=pl.BlockSpec((tm, width), lambda i, j: (i, j)),
        out_shape=jax.ShapeDtypeStruct((T, width * steps), F32),
        compiler_params=_cparams(2), name=name)(qkv, q_gain2, k_gain2)


def banded_fwd(qkn, qkv, slopes, sinks, *, dil, nsub, ppk, q_blk, k_blk, v_blk, n_heads, group,
               max_dist, name):
    T = qkv.shape[0]
    RB = BLK * dil * nsub
    nb = T // RB
    npair = n_heads // 2
    kv_shared = group > 1
    scale = HEAD_DIM ** -0.5
    has_sink = sinks is not None

    def body(*refs):
        slope_ref = refs[0]
        if has_sink:
            sink_ref, refs = refs[1], refs[2:]
        else:
            refs = refs[1:]
        q_ref, kc_ref, kp_ref, vc_ref, vp_ref, o_ref, l_ref = refs
        pb = pl.program_id(0)
        i = pl.program_id(1)
        m0 = _lane0()
        distf, valid_first, top = _band_masks2(max_dist, i > 0, i >= 0)
        valid_inner = _band_masks2(max_dist, i >= 0, i >= 0)[1] if nsub > 1 else None
        for rows, (src, prows), sub in _band_units(dil, nsub):
            valid = valid_first if sub == 0 else valid_inner
            kpr, vpr = (kp_ref, vp_ref) if src == "prev" else (kc_ref, vc_ref)
            kcache = {}
            for jp in range(ppk):
                cs = pl.ds(LANES * jp, LANES)
                jk = 0 if kv_shared else jp
                if jk not in kcache:
                    ks = pl.ds(LANES * jk, LANES)
                    kcat = jnp.concatenate([kpr[prows, ks], kc_ref[rows, ks]], axis=0)
                    vcat = jnp.concatenate([vpr[prows, ks], vc_ref[rows, ks]], axis=0)
                    kcache[jk] = (kcat.astype(BF16), vcat.astype(BF16))
                kn, vcat = kcache[jk]
                qn = q_ref[rows, cs]
                kes = [((2 * jp + e) // group) % 2 if kv_shared else e for e in range(2)]
                hidx = 2 * (pb * ppk + jp)
                qs = _stack_heads(qn, m0, kes).astype(BF16)
                slope = jnp.where(top, slope_ref[hidx], slope_ref[hidx + 1])
                s = jnp.where(valid, _dot_nt(qs, kn) * scale - slope * distf, NEG)
                m = jnp.max(s, axis=-1, keepdims=True)
                if has_sink:
                    sk = jnp.where(top, sink_ref[hidx], sink_ref[hidx + 1])
                    m = jnp.maximum(m, sk)
                p = jnp.exp(s - m)
                den = jnp.sum(p, axis=-1, keepdims=True)
                if has_sink:
                    den = den + jnp.exp(sk - m)
                o_full = _dot((p * (1.0 / den)).astype(BF16), vcat)
                o_ref[rows, cs] = _unstack_heads(o_full, m0, kes)
                l_ref[rows, cs] = _unstack_heads(jnp.broadcast_to(m + jnp.log(den), (2 * BLK, LANES)), m0, [0, 1])

    smem = pl.BlockSpec(memory_space=pltpu.SMEM)
    qw = LANES * ppk
    ospec = pl.BlockSpec((RB, qw), lambda p, i: (i, p))
    oshape = jax.ShapeDtypeStruct((T, n_heads * HEAD_DIM), F32)
    args = [slopes] + ([sinks] if has_sink else []) + [qkn] * 3 + [qkv] * 2
    return pl.pallas_call(
        body, grid=(npair // ppk, nb),
        in_specs=[smem] * (2 if has_sink else 1) + _band_specs(dil, nsub, ppk, q_blk, k_blk, v_blk, kv_shared, nb),
        out_specs=[ospec, ospec], out_shape=[oshape, oshape],
        compiler_params=_cparams(2), name=name)(*args)


def banded_bwd(qkn, qkv, slopes, sinks, do, o, lse, w, omix, *, dil, nsub, ppk, q_blk, k_blk, v_blk,
               n_heads, group, max_dist, do_blk, name):
    T = qkv.shape[0]
    RB = BLK * dil * nsub
    nb = T // RB
    npair = n_heads // 2
    kv_shared = group > 1
    scale = HEAD_DIM ** -0.5
    has_sink = sinks is not None
    mixed = w is not None
    qw = LANES * ppk

    def body(*refs):
        slope_ref = refs[0]
        if has_sink:
            sink_ref, refs = refs[1], refs[2:]
        else:
            refs = refs[1:]
        q_ref, kc_ref, kp_ref, vc_ref, vp_ref, do_ref, o_ref, l_ref = refs[:8]
        refs = refs[8:]
        if mixed:
            w_ref, om_ref, refs = refs[0], refs[1], refs[2:]
        dq_ref, dk_ref, dv_ref, dsk_ref, ck_ref, cv_ref = refs
        pb = pl.program_id(0)
        i = pl.program_id(1)
        live = i < nb
        m0 = _lane0()
        lane = lax.broadcasted_iota(jnp.int32, (1, LANES), 1)
        distf, valid_first, top = _band_masks2(max_dist, i > 0, live)
        valid_inner = _band_masks2(max_dist, i >= 0, live)[1] if nsub > 1 else None
        livef = live.astype(F32)

        def stack_rows(x2):
            return jnp.concatenate([_half_pick(x2, m0, 0), _half_pick(x2, m0, 1)], axis=0)

        @pl.when((pb == 0) & (i == 0))
        def _():
            dsk_ref[...] = jnp.zeros_like(dsk_ref)

        @pl.when(i == 0)
        def _():
            ck_ref[...] = jnp.zeros_like(ck_ref)
            cv_ref[...] = jnp.zeros_like(cv_ref)

        dsk_acc = jnp.zeros((1, LANES), F32)
        if nsub > 1:
            dk_ref[...] = ck_ref[...]
            dv_ref[...] = cv_ref[...]
        for rows, (src, prows), sub in _band_units(dil, nsub):
            valid = valid_first if sub == 0 else valid_inner
            kpr, vpr = (kp_ref, vp_ref) if src == "prev" else (kc_ref, vc_ref)
            for jp in range(ppk):
                cs = pl.ds(LANES * jp, LANES)
                ks = pl.ds(0, LANES) if kv_shared else cs
                kn = jnp.concatenate([kpr[prows, ks], kc_ref[rows, ks]], axis=0).astype(BF16)
                vcat = jnp.concatenate([vpr[prows, ks], vc_ref[rows, ks]], axis=0).astype(BF16)
                dov = do_ref[rows, cs]
                lv = l_ref[rows, cs]
                if mixed:
                    wv = w_ref[rows, cs]
                    dmix = _half_sum(dov * om_ref[rows, cs], m0)
                    dov = dov * wv
                delta2 = _half_sum(dov * o_ref[rows, cs], m0)
                shift = stack_rows(wv * dmix if mixed else delta2)
                kes = [((2 * jp + e) // group) % 2 if kv_shared else e for e in range(2)]
                hidx = 2 * (pb * ppk + jp)
                qs = _stack_heads(q_ref[rows, cs], m0, kes).astype(BF16)
                dos = _stack_heads(dov, m0, kes).astype(BF16)
                lse = stack_rows(lv)
                slope = jnp.where(top, slope_ref[hidx], slope_ref[hidx + 1])
                p = jnp.where(valid, jnp.exp(_dot_nt(qs, kn) * scale - slope * distf - lse), 0.0)
                ds = (p * (_dot_nt(dos, vcat) - shift)).astype(BF16)
                dqn = _unstack_heads(_dot(ds, kn), m0, kes) * scale
                dkn = _dot_tn(ds, qs) * scale
                dvv = _dot_tn(p.astype(BF16), dos)
                if has_sink:
                    sk = jnp.where(top, sink_ref[hidx], sink_ref[hidx + 1])
                    contrib = -jnp.exp(sk - lse) * stack_rows(delta2) * livef
                    for e in range(2):
                        tot = jnp.sum(contrib[e * BLK:(e + 1) * BLK], axis=0, keepdims=True)
                        dsk_acc = dsk_acc + jnp.where(lane == (2 * jp + e), tot, 0.0)
                dk_raw = dkn

                @pl.when(live)
                def _():
                    dq_ref[rows, cs] = dqn

                if nsub == 1:
                    dk_ref[rows, cs] = ck_ref[rows, cs] + dk_raw[:BLK]
                    dv_ref[rows, cs] = cv_ref[rows, cs] + dvv[:BLK]
                elif sub == 0:
                    last = pl.ds(RB - BLK, BLK)
                    dk_ref[last, cs] += dk_raw[:BLK]
                    dv_ref[last, cs] += dvv[:BLK]
                else:
                    ck_ref[prows, cs] += dk_raw[:BLK]
                    cv_ref[prows, cs] += dvv[:BLK]
                ck_ref[rows, cs] = dk_raw[BLK:]
                cv_ref[rows, cs] = dvv[BLK:]
        dsk_ref[...] += dsk_acc

    smem = pl.BlockSpec(memory_space=pltpu.SMEM)
    gspec = pl.BlockSpec((1, LANES), lambda p, i: (0, 0))

    def cur(i):
        return jnp.minimum(i, nb - 1)

    qspec = pl.BlockSpec((RB, qw), lambda p, i: (cur(i), p))
    dospec = pl.BlockSpec((RB, qw), lambda p, i: (cur(i), do_blk + p))
    kvout = pl.BlockSpec((RB, qw), lambda p, i: (jnp.maximum(i - 1, 0), p))
    in_specs = ([smem] * (2 if has_sink else 1) + _band_specs(dil, nsub, ppk, q_blk, k_blk, v_blk, kv_shared, nb)
                + [dospec, qspec, qspec] + ([qspec, qspec] if mixed else []))
    args = ([slopes] + ([sinks] if has_sink else []) + [qkn] * 3 + [qkv] * 2 + [do, o, lse]
            + ([w, omix] if mixed else []))
    full = jax.ShapeDtypeStruct((T, n_heads * HEAD_DIM), F32)
    row = jax.ShapeDtypeStruct((1, LANES), F32)
    return pl.pallas_call(
        body, grid=(npair // ppk, nb + 1), in_specs=in_specs,
        out_specs=[qspec, kvout, kvout, gspec],
        out_shape=[full, full, full, row],
        scratch_shapes=[pltpu.VMEM((RB, qw), F32), pltpu.VMEM((RB, qw), F32)],
        compiler_params=_cparams(2), name=name)(*args)


def mix_fwd(o1, o2, o3, l1, l2, l3, *, tm, name):
    T, C = o1.shape

    def body(o1r, o2r, o3r, l1r, l2r, l3r, o_ref, w1r, w2r, w3r):
        a, b, c = l1r[...], l2r[...], l3r[...]
        m = jnp.maximum(jnp.maximum(a, b), c)
        ea, eb, ec = jnp.exp(a - m), jnp.exp(b - m), jnp.exp(c - m)
        inv = 1.0 / (ea + eb + ec)
        wa, wb, wc = ea * inv, eb * inv, ec * inv
        o_ref[...] = wa * o1r[...] + wb * o2r[...] + wc * o3r[...]
        w1r[...] = wa
        w2r[...] = wb
        w3r[...] = wc

    spec = pl.BlockSpec((tm, C), lambda i: (i, 0))
    shp = jax.ShapeDtypeStruct((T, C), F32)
    return pl.pallas_call(body, grid=(T // tm,), in_specs=[spec] * 6, out_specs=[spec] * 4, out_shape=[shp] * 4,
                          compiler_params=_cparams(1), name=name)(o1, o2, o3, l1, l2, l3)


def _qk_norm_bwd(raw, dn, gain, m0):
    r = _head_rms(raw, m0)
    h = raw * r
    dh = dn * gain
    d_raw = r * (dh - h * (_half_sum(dh * h, m0) * (1.0 / HEAD_DIM)))
    return d_raw, jnp.sum(dn * h, axis=0, keepdims=True)


def _acc_rows(ref, val):
    @pl.when(pl.program_id(0) == 0)
    def _():
        ref[...] = val

    @pl.when(pl.program_id(0) > 0)
    def _():
        ref[...] += val


def assemble_odd(parts, qkv, q_gain2, k_gain2, *, tm, name):
    T, C = parts[0][0].shape
    nbk = C // LANES

    def body(*refs):
        qkv_ref, qg_ref, kg_ref, o_ref, dqg_ref, dkg_ref = refs[9:]
        m0 = _lane0()
        sums = [refs[j][...] + refs[3 + j][...] + refs[6 + j][...] for j in range(3)]
        o_ref[:, pl.ds(2 * C, C)] = sums[2]
        for j, (g_ref, acc_ref) in enumerate(((qg_ref, dqg_ref), (kg_ref, dkg_ref))):
            dgain = jnp.zeros((1, LANES), F32)
            for b in range(nbk):
                cols = pl.ds(C * j + LANES * b, LANES)
                d_raw, part = _qk_norm_bwd(qkv_ref[:, cols], sums[j][:, LANES * b:LANES * (b + 1)], g_ref[...], m0)
                o_ref[:, cols] = d_raw
                dgain = dgain + part
            _acc_rows(acc_ref, dgain)

    spec = pl.BlockSpec((tm, C), lambda i: (i, 0))
    gspec = pl.BlockSpec((1, LANES), lambda i: (0, 0))
    flat = [parts[p][j] for p in range(3) for j in range(3)]
    row = jax.ShapeDtypeStruct((1, LANES), F32)
    return pl.pallas_call(body, grid=(T // tm,),
                          in_specs=[spec] * 9 + [pl.BlockSpec((tm, 2 * C), lambda i: (i, 0)), gspec, gspec],
                          out_specs=[pl.BlockSpec((tm, 3 * C), lambda i: (i, 0)), gspec, gspec],
                          out_shape=[jax.ShapeDtypeStruct((T, 3 * C), F32), row, row],
                          compiler_params=_cparams(1), name=name)(*flat, qkv, q_gain2, k_gain2)


def assemble_even(dqa, dka4, dva4, dqb, dkb, dvb, qkv, q_gain2, k_gain2, *, tm, name):
    T = dqa.shape[0]
    W = 512
    QK = 768

    def body(dqa_r, dka_r, dva_r, dqb_r, dkb_r, dvb_r, qkv_ref, qg_ref, kg_ref, o_ref, dqg_ref, dkg_ref):
        m0 = _lane0()
        ka = dka_r[...]
        va = dva_r[...]
        dqn = dqa_r[...]
        dgain = jnp.zeros((1, LANES), F32)
        for b in range(W // LANES):
            cols = pl.ds(LANES * b, LANES)
            d_raw, part = _qk_norm_bwd(qkv_ref[:, cols], dqn[:, LANES * b:LANES * (b + 1)], qg_ref[...], m0)
            o_ref[:, cols] = d_raw
            dgain = dgain + part
        _acc_rows(dqg_ref, dgain)
        dkn = ka[:, 0:128] + ka[:, 128:256] + ka[:, 256:384] + ka[:, 384:512]
        d_raw, part = _qk_norm_bwd(qkv_ref[:, pl.ds(W, LANES)], dkn, kg_ref[...], m0)
        o_ref[:, pl.ds(W, LANES)] = d_raw
        _acc_rows(dkg_ref, part)
        o_ref[:, pl.ds(640, LANES)] = va[:, 0:128] + va[:, 128:256] + va[:, 256:384] + va[:, 384:512]
        o_ref[:, pl.ds(768, W)] = dqb_r[...]
        o_ref[:, pl.ds(1280, W)] = dkb_r[...]
        o_ref[:, pl.ds(1792, W)] = dvb_r[...]

    spec = pl.BlockSpec((tm, W), lambda i: (i, 0))
    gspec = pl.BlockSpec((1, LANES), lambda i: (0, 0))
    row = jax.ShapeDtypeStruct((1, LANES), F32)
    return pl.pallas_call(body, grid=(T // tm,),
                          in_specs=[spec] * 6 + [pl.BlockSpec((tm, QK), lambda i: (i, 0)), gspec, gspec],
                          out_specs=[pl.BlockSpec((tm, 2304), lambda i: (i, 0)), gspec, gspec],
                          out_shape=[jax.ShapeDtypeStruct((T, 2304), F32), row, row],
                          compiler_params=_cparams(1), name=name)(dqa, dka4, dva4, dqb, dkb, dvb, qkv, q_gain2, k_gain2)


STICK_T = 256
STICK_DEAD = -110.0


def _split_bf16(x):
    hi = x.astype(BF16)
    lo = (x - hi.astype(F32)).astype(BF16)
    return hi, lo


def _stick_logits(qm, kt, scale, diag):
    n = STICK_T
    row = lax.broadcasted_iota(jnp.int32, (n, n), 0)
    col = lax.broadcasted_iota(jnp.int32, (n, n), 1)
    mask = col < row + jnp.where(diag, 0, n)
    z = _dot_nt(qm, kt) * scale
    lneg = -(jnp.maximum(z, 0.0) + jnp.log(1.0 + jnp.exp(-jnp.abs(z))))
    lpos = z + lneg
    lk = jnp.where(mask, lneg, 0.0)
    return mask, lpos, lneg, lk


def _cumsum_mm(x, tri):
    hi, lo = _split_bf16(x)
    return _dot(hi, tri) + _dot(lo, tri)


def stick_fwd(qkv, *, q_blk, k_blk, v_blk, n_pairs, name, riders=None, rider_args=()):
    T = qkv.shape[0]
    n = STICK_T
    nq = T // n
    scale = HEAD_DIM ** -0.5
    nc = riders.n if riders is not None else 0
    n_steps = n_pairs * nq
    stage_at = (0, (3 * n_steps) // 4, n_steps - 1, n_steps - 1)

    def body(*refs):
        q_ref, k_ref, v_ref = refs[:3]
        x_refs, o_ref = refs[3:3 + nc], refs[3 + nc]
        out_refs, sems = refs[4 + nc:4 + 2 * nc], refs[4 + 2 * nc:]
        i = pl.program_id(1)
        step_id = pl.program_id(0) * nq + i

        def ride(which):
            if riders is not None:
                @pl.when(step_id == stage_at[which])
                def _():
                    riders.stage(which, x_refs, out_refs, sems)

        ride(0)
        ride(1)
        m0 = _lane0()
        r2 = lax.broadcasted_iota(jnp.int32, (n, n), 0)
        c2 = lax.broadcasted_iota(jnp.int32, (n, n), 1)
        tri_after = (r2 > c2).astype(BF16)
        qv = q_ref[...]
        out = jnp.zeros((n, LANES), F32)
        for e in range(2):
            qm = _mask_half(qv, m0, e).astype(BF16)

            def alive(st):
                t, _, carry = st
                return (t <= i) & (jnp.max(carry) > STICK_DEAD)

            def step(st, e=e, qm=qm):
                t, acc, carry = st
                start = pl.multiple_of((i - t) * n, n)
                kt = k_ref[pl.ds(start, n), :].astype(BF16)
                vt = _mask_half(v_ref[pl.ds(start, n), :], m0, e).astype(BF16)
                mask, lpos, _, lk = _stick_logits(qm, kt, scale, t == 0)
                after = _cumsum_mm(lk, tri_after) + carry
                a = jnp.where(mask, jnp.exp(lpos + after), 0.0)
                acc = acc + _dot(a.astype(BF16), vt)
                carry = carry + jnp.sum(lk, axis=-1, keepdims=True)
                return t + 1, acc, carry

            _, acc, _ = lax.while_loop(alive, step, (jnp.int32(0), jnp.zeros((n, LANES), F32),
                                                     jnp.zeros((n, 1), F32)))
            out = out + acc
        o_ref[...] = out
        ride(2)
        ride(3)

    outs = pl.pallas_call(
        body, grid=(n_pairs, nq),
        in_specs=[pl.BlockSpec((n, LANES), lambda p, i: (i, q_blk + p)),
                  pl.BlockSpec((T, LANES), lambda p, i: (0, k_blk + p)),
                  pl.BlockSpec((T, LANES), lambda p, i: (0, v_blk + p))] + [_ANY] * nc,
        out_specs=[pl.BlockSpec((n, LANES), lambda p, i: (i, p))] + [_ANY] * nc,
        out_shape=[jax.ShapeDtypeStruct((T, n_pairs * LANES), F32)] + (riders.shapes if nc else []),
        scratch_shapes=riders.sems if nc else [],
        compiler_params=_cparams(2), name=name)(qkv, qkv, qkv, *rider_args)
    return outs[0], list(outs[1:])


def stick_bwd(qkv, do, *, q_blk, k_blk, v_blk, do_blk, n_pairs, name, riders=None, rider_args=()):
    T = qkv.shape[0]
    n = STICK_T
    nq = T // n
    scale = HEAD_DIM ** -0.5
    nc = riders.n if riders is not None else 0

    def body(*refs):
        q_ref, k_ref, v_ref, do_ref = refs[:4]
        pre_refs = refs[4:4 + nc]
        dq_ref, dk_ref, dv_ref = refs[4 + nc:7 + nc]
        land_refs = refs[7 + nc:7 + 2 * nc]
        a_keep, g_keep, s_keep = refs[7 + 2 * nc:10 + 2 * nc]
        sems = refs[10 + 2 * nc:]
        i = pl.program_id(1)
        first_step = (pl.program_id(0) == 0) & (i == 0)
        last_step = (pl.program_id(0) == n_pairs - 1) & (i == nq - 1)
        m0 = _lane0()
        r2 = lax.broadcasted_iota(jnp.int32, (n, n), 0)
        c2 = lax.broadcasted_iota(jnp.int32, (n, n), 1)
        tri_after = (r2 > c2).astype(BF16)
        tri_from = (r2 >= c2).astype(BF16)

        if riders is not None:
            @pl.when(first_step)
            def _():
                riders.start(pre_refs, land_refs, sems)

        @pl.when(i == 0)
        def _():
            dk_ref[...] = jnp.zeros_like(dk_ref)
            dv_ref[...] = jnp.zeros_like(dv_ref)

        qv = q_ref[...]
        dov = do_ref[...]
        dq_out = jnp.zeros((n, LANES), F32)
        for e in range(2):
            qm = _mask_half(qv, m0, e).astype(BF16)
            dom = _mask_half(dov, m0, e).astype(BF16)

            def alive(st):
                t, carry, _ = st
                return (t <= i) & (jnp.max(carry) > STICK_DEAD)

            def scan(st, qm=qm, dom=dom):
                t, carry, gtot = st
                start = pl.multiple_of((i - t) * n, n)
                kt = k_ref[pl.ds(start, n), :].astype(BF16)
                vt = v_ref[pl.ds(start, n), :].astype(BF16)
                mask, lpos, lneg, lk = _stick_logits(qm, kt, scale, t == 0)
                a = jnp.where(mask, jnp.exp(lpos + _cumsum_mm(lk, tri_after) + carry), 0.0)
                g = _dot_nt(dom, vt) * a
                a_keep[t] = a.astype(BF16)
                g_keep[t] = g
                s_keep[t] = jnp.exp(lneg).astype(BF16)
                return (t + 1, carry + jnp.sum(lk, axis=-1, keepdims=True),
                        gtot + jnp.sum(g, axis=-1, keepdims=True))

            z1 = jnp.zeros((n, 1), F32)
            n_live, _, gtot = lax.while_loop(alive, scan, (jnp.int32(0), z1, z1))

            def step(t, st, e=e, qm=qm, dom=dom, gtot=gtot):
                dq_acc, gright = st
                start = pl.multiple_of((i - t) * n, n)
                g = g_keep[t]
                sneg = s_keep[t].astype(F32)
                before = gtot - (_cumsum_mm(g, tri_from) + gright)
                mask = c2 < r2 + jnp.where(t == 0, 0, n)
                dz = jnp.where(mask, g * sneg - before * (1.0 - sneg), 0.0) * scale
                dzb = dz.astype(BF16)
                dq_acc = dq_acc + _dot(dzb, _mask_half(k_ref[pl.ds(start, n), :], m0, e).astype(BF16))
                dk_ref[pl.ds(start, n), :] += _dot_tn(dzb, qm)
                dv_ref[pl.ds(start, n), :] += _dot_tn(a_keep[t], dom)
                return dq_acc, gright + jnp.sum(g, axis=-1, keepdims=True)

            dq_acc, _ = lax.fori_loop(0, n_live, step, (jnp.zeros((n, LANES), F32), z1))
            dq_out = dq_out + dq_acc
        dq_ref[...] = dq_out

        if riders is not None:
            @pl.when(last_step)
            def _():
                riders.finish(pre_refs, land_refs, sems)

    tile = pl.BlockSpec((n, LANES), lambda p, i: (i, p))
    whole = pl.BlockSpec((T, LANES), lambda p, i: (0, p))
    shp = jax.ShapeDtypeStruct((T, n_pairs * LANES), F32)
    outs = pl.pallas_call(
        body, grid=(n_pairs, nq),
        in_specs=[pl.BlockSpec((n, LANES), lambda p, i: (i, q_blk + p)),
                  pl.BlockSpec((T, LANES), lambda p, i: (0, k_blk + p)),
                  pl.BlockSpec((T, LANES), lambda p, i: (0, v_blk + p)),
                  pl.BlockSpec((n, LANES), lambda p, i: (i, do_blk + p))] + [_ANY] * nc,
        out_specs=[tile, whole, whole] + [_ANY] * nc,
        out_shape=[shp, shp, shp] + (riders.shapes if nc else []),
        scratch_shapes=[pltpu.VMEM((nq, n, n), BF16), pltpu.VMEM((nq, n, n), F32), pltpu.VMEM((nq, n, n), BF16)]
        + (riders.sems if nc else []),
        compiler_params=_cparams(2), name=name)(qkv, qkv, qkv, do, *rider_args)
    return outs[0], outs[1], outs[2], list(outs[3:])


def _xnorm(x):
    r = lax.rsqrt(jnp.mean(x * x, axis=-1, keepdims=True) + RMS_EPS)
    return r, x * r


def xattn_fwd(qraw, kvraw, q_gain, k_gain, *, tm, name):
    T = qraw.shape[0]
    scale = X_HEAD_DIM ** -0.5
    W = X_HEADS * X_HEAD_DIM

    def body(q_ref, kv_ref, qg_ref, kg_ref, o_ref):
        for h in range(X_HEADS):
            cs = pl.ds(X_HEAD_DIM * h, X_HEAD_DIM)
            _, qh = _xnorm(q_ref[:, cs])
            _, kh = _xnorm(kv_ref[:, cs])
            qn = (qh * qg_ref[...]).astype(BF16)
            kn = (kh * kg_ref[...]).astype(BF16)
            v = kv_ref[:, pl.ds(W + X_HEAD_DIM * h, X_HEAD_DIM)].astype(BF16)
            s = _dot_nt(qn, kn) * scale
            m = jnp.max(s, axis=-1, keepdims=True)
            p = jnp.exp(s - m)
            p = p / jnp.sum(p, axis=-1, keepdims=True)
            o_ref[:, cs] = _dot(p.astype(BF16), v)

    gspec = pl.BlockSpec((1, X_HEAD_DIM), lambda i: (0, 0))
    return pl.pallas_call(
        body, grid=(T // tm,),
        in_specs=[pl.BlockSpec((tm, W), lambda i: (i, 0)), pl.BlockSpec((MEM_LEN, 2 * W), lambda i: (0, 0)),
                  gspec, gspec],
        out_specs=pl.BlockSpec((tm, W), lambda i: (i, 0)),
        out_shape=jax.ShapeDtypeStruct((T, W), F32),
        compiler_params=_cparams(1), name=name)(qraw, kvraw, q_gain, k_gain)


def xattn_bwd(qraw, kvraw, q_gain, k_gain, do, o, *, tm, name):
    T = qraw.shape[0]
    nt = T // tm
    scale = X_HEAD_DIM ** -0.5
    W = X_HEADS * X_HEAD_DIM

    def body(q_ref, kv_ref, qg_ref, kg_ref, do_ref, o_ref, dq_ref, dkv_ref, dqg_ref, dkg_ref, dkn_ref):
        i = pl.program_id(0)

        @pl.when(i == 0)
        def _():
            dkv_ref[...] = jnp.zeros_like(dkv_ref)
            dkn_ref[...] = jnp.zeros_like(dkn_ref)
            dqg_ref[...] = jnp.zeros_like(dqg_ref)
            dkg_ref[...] = jnp.zeros_like(dkg_ref)

        qg = qg_ref[...]
        kg = kg_ref[...]
        dqg_acc = jnp.zeros((1, X_HEAD_DIM), F32)
        for h in range(X_HEADS):
            cs = pl.ds(X_HEAD_DIM * h, X_HEAD_DIM)
            vs = pl.ds(W + X_HEAD_DIM * h, X_HEAD_DIM)
            rq, qh = _xnorm(q_ref[:, cs])
            _, kh = _xnorm(kv_ref[:, cs])
            qn = (qh * qg).astype(BF16)
            kn = (kh * kg).astype(BF16)
            v = kv_ref[:, vs].astype(BF16)
            s = _dot_nt(qn, kn) * scale
            m = jnp.max(s, axis=-1, keepdims=True)
            p = jnp.exp(s - m)
            p = p / jnp.sum(p, axis=-1, keepdims=True)
            dov = do_ref[:, cs]
            delta = jnp.sum(dov * o_ref[:, cs], axis=-1, keepdims=True)
            dob = dov.astype(BF16)
            ds = (p * (_dot_nt(dob, v) - delta)).astype(BF16)
            dqn = _dot(ds, kn) * scale
            dkn_ref[:, cs] += _dot_tn(ds, qn) * scale
            dkv_ref[:, vs] += _dot_tn(p.astype(BF16), dob)
            dqg_acc = dqg_acc + jnp.sum(dqn * qh, axis=0, keepdims=True)
            dqh = dqn * qg
            dq_ref[:, cs] = rq * (dqh - qh * jnp.mean(dqh * qh, axis=-1, keepdims=True))
        dqg_ref[...] += dqg_acc

        @pl.when(i == nt - 1)
        def _():
            dkg_acc = jnp.zeros((1, X_HEAD_DIM), F32)
            for h in range(X_HEADS):
                cs = pl.ds(X_HEAD_DIM * h, X_HEAD_DIM)
                rk, kh = _xnorm(kv_ref[:, cs])
                dkn = dkn_ref[:, cs]
                dkg_acc = dkg_acc + jnp.sum(dkn * kh, axis=0, keepdims=True)
                dkh = dkn * kg
                dkv_ref[:, cs] = rk * (dkh - kh * jnp.mean(dkh * kh, axis=-1, keepdims=True))
            dkg_ref[...] = dkg_acc

    gspec = pl.BlockSpec((1, X_HEAD_DIM), lambda i: (0, 0))
    tile = pl.BlockSpec((tm, W), lambda i: (i, 0))
    kvspec = pl.BlockSpec((MEM_LEN, 2 * W), lambda i: (0, 0))
    grow = jax.ShapeDtypeStruct((1, X_HEAD_DIM), F32)
    return pl.pallas_call(
        body, grid=(nt,), in_specs=[tile, kvspec, gspec, gspec, tile, tile],
        out_specs=[tile, kvspec, gspec, gspec],
        out_shape=[jax.ShapeDtypeStruct((T, W), F32), jax.ShapeDtypeStruct((MEM_LEN, 2 * W), F32), grow, grow],
        scratch_shapes=[pltpu.VMEM((MEM_LEN, W), F32)],
        compiler_params=_cparams(1), name=name)(qraw, kvraw, q_gain, k_gain, do, o)


_ANY = pl.BlockSpec(memory_space=pl.ANY)


def _my_pos():
    return lax.axis_index("x"), lax.axis_index("y"), lax.axis_index("c")


def _pieces(arrays, chunks):
    out = []
    for a, (arr, n) in enumerate(zip(arrays, chunks)):
        rc = arr.shape[-2] // n
        out += [(a, pl.ds(ch * rc, rc)) for ch in range(n)]
    return out


class GatherBlocks:
    N_STAGES = 4

    def __init__(self, blks, chunks):
        self.shapes = [jax.ShapeDtypeStruct((N_DEV,) + b.shape, b.dtype) for b in blks]
        self.n = len(blks)
        self.pieces = _pieces(blks, chunks)
        n_p = len(self.pieces)
        self.sems = [pltpu.SemaphoreType.DMA((7 * n_p,)), pltpu.SemaphoreType.DMA((7 * n_p,)),
                     pltpu.SemaphoreType.DMA((n_p,))]

    def stage(self, which, x_refs, out_refs, sems):
        send_sems, recv_sems, local_sems = sems
        pieces, n_p = self.pieces, len(self.pieces)
        x, y, c = _my_pos()
        me, sibling = (x, y, c), (x, y, 1 - c)
        chips = [(1 - x, y), (x, 1 - y), (1 - x, 1 - y)]
        xn, yn, dg = [(*chip, c) for chip in chips]
        ps = range(n_p)

        def slot(block, p):
            px, py, pc = block
            a, rows = pieces[p]
            return out_refs[a].at[4 * px + 2 * py + pc, rows]

        def own(p):
            a, rows = pieces[p]
            return x_refs[a].at[rows]

        def copy(k, p, block, to, from_input=False):
            return pltpu.make_async_remote_copy(
                src_ref=own(p) if from_input else slot(block, p), dst_ref=slot(block, p),
                send_sem=send_sems.at[k * n_p + p], recv_sem=recv_sems.at[k * n_p + p],
                device_id=to, device_id_type=MESH)

        mine = [pltpu.make_async_copy(own(p), slot(me, p), local_sems.at[p]) for p in ps]
        first = [copy(k, p, me, to, from_input=True) for p in ps for k, to in ((1, xn), (2, yn), (0, sibling))]
        on_x = [copy(3, p, xn, yn) for p in ps if p % 2 == 0] + [copy(4, p, xn, sibling) for p in ps]
        on_y = [copy(3, p, yn, xn) for p in ps if p % 2 == 1] + [copy(5, p, yn, sibling) for p in ps]
        on_d = [copy(6, p, dg, sibling) for p in ps]
        if which == 0:
            for cp in first + mine:
                cp.start()
        elif which == 1:
            for p in ps:
                copy(1, p, xn, me).wait_recv()
                if p % 2 == 0:
                    copy(3, p, xn, yn).start()
                copy(4, p, xn, sibling).start()
                copy(2, p, yn, me).wait_recv()
                if p % 2 == 1:
                    copy(3, p, yn, xn).start()
                copy(5, p, yn, sibling).start()
        elif which == 2:
            for p in ps:
                copy(3, p, dg, me).wait_recv()
                copy(6, p, dg, sibling).start()
        else:
            for p in ps:
                copy(0, p, sibling, me).wait_recv()
            for k, chip in zip((4, 5, 6), chips):
                for p in ps:
                    copy(k, p, (*chip, 1 - c), me).wait_recv()
            for cp in first + on_x + on_y + on_d:
                cp.wait_send()
            for cp in mine:
                cp.wait()


def gather_blocks(blks, chunks, *, name):
    gb = GatherBlocks(blks, chunks)
    n = gb.n

    def body(*refs):
        x_refs, out_refs, sems = refs[:n], refs[n:2 * n], refs[2 * n:]
        for which in range(gb.N_STAGES):
            gb.stage(which, x_refs, out_refs, sems)

    return pl.pallas_call(body, out_shape=gb.shapes, in_specs=[_ANY] * n, out_specs=[_ANY] * n,
                          scratch_shapes=gb.sems, name=name)(*blks)


def gather_small(small, *, name):
    S, C = small.shape

    def body(s_ref, out_ref, send_sems, recv_sems, local_sem):
        x, y, c = _my_pos()
        my_id = 4 * x + 2 * y + c

        def copy(k, slot):
            px, py, pc = x ^ ((k >> 2) & 1), y ^ ((k >> 1) & 1), c ^ (k & 1)
            dst = my_id if slot == "mine" else 4 * px + 2 * py + pc
            return pltpu.make_async_remote_copy(
                src_ref=s_ref, dst_ref=out_ref.at[dst], send_sem=send_sems.at[k - 1], recv_sem=recv_sems.at[k - 1],
                device_id=(px, py, pc), device_id_type=MESH)

        own = pltpu.make_async_copy(s_ref, out_ref.at[my_id], local_sem)
        own.start()
        sends = [copy(k, "mine") for k in range(1, N_DEV)]
        for cp in sends:
            cp.start()
        for k in range(1, N_DEV):
            copy(k, "theirs").wait_recv()
        for cp in sends:
            cp.wait_send()
        own.wait()

    dma7 = pltpu.SemaphoreType.DMA((7,))
    return pl.pallas_call(
        body, out_shape=jax.ShapeDtypeStruct((N_DEV, S, C), small.dtype), in_specs=[_ANY], out_specs=_ANY,
        scratch_shapes=[dma7, dma7, pltpu.SemaphoreType.DMA], name=name)(small)


class PairExchange:
    def __init__(self, bigs, chunks):
        self.shapes = [jax.ShapeDtypeStruct((4,) + b.shape[1:], b.dtype) for b in bigs]
        self.n = len(bigs)
        self.pieces = _pieces(bigs, chunks)
        n_p = len(self.pieces)
        self.sems = [pltpu.SemaphoreType.DMA((4 * n_p,)), pltpu.SemaphoreType.DMA((4 * n_p,))]

    def _copies(self, big_refs, out_refs, sems):
        send_sems, recv_sems = sems
        n_p = len(self.pieces)
        x, y, c = _my_pos()

        def copy(b, p):
            a, rows = self.pieces[p]
            return pltpu.make_async_remote_copy(
                src_ref=big_refs[a].at[2 * b + (1 - c), rows], dst_ref=out_refs[a].at[b, rows],
                send_sem=send_sems.at[b * n_p + p], recv_sem=recv_sems.at[b * n_p + p],
                device_id=(x, y, 1 - c), device_id_type=MESH)

        return [copy(b, p) for b in range(4) for p in range(n_p)]

    def start(self, big_refs, out_refs, sems):
        for cp in self._copies(big_refs, out_refs, sems):
            cp.start()

    def finish(self, big_refs, out_refs, sems):
        cps = self._copies(big_refs, out_refs, sems)
        for cp in cps:
            cp.wait_recv()
        for cp in cps:
            cp.wait_send()


def _standalone(exchange, args, name):
    n = exchange.n

    def body(*refs):
        exchange.start(refs[:n], refs[n:2 * n], refs[2 * n:])
        exchange.finish(refs[:n], refs[n:2 * n], refs[2 * n:])

    return pl.pallas_call(body, out_shape=exchange.shapes, in_specs=[_ANY] * n, out_specs=[_ANY] * n,
                          scratch_shapes=exchange.sems, name=name)(*args)


class Riding:
    def __init__(self, riders):
        self.riders = [(ex, list(args)) for ex, args in riders]
        self.args = [a for _, args in self.riders for a in args]
        self.in_specs = [_ANY] * len(self.args)
        self.out_shapes = [s for ex, _ in self.riders for s in ex.shapes]
        self.out_specs = [_ANY] * len(self.out_shapes)
        self.scratch = [s for ex, _ in self.riders for s in ex.sems]

    def wrap(self, body, n_in, n_out, n_scratch, is_first, is_last):
        def wrapped(*refs):
            k = 0
            core = list(refs[:n_in])
            k = n_in
            r_in = []
            for ex, _ in self.riders:
                r_in.append(refs[k:k + ex.n])
                k += ex.n
            core += refs[k:k + n_out]
            k += n_out
            r_out = []
            for ex, _ in self.riders:
                r_out.append(refs[k:k + ex.n])
                k += ex.n
            core += refs[k:k + n_scratch]
            k += n_scratch
            r_sem = []
            for ex, _ in self.riders:
                r_sem.append(refs[k:k + len(ex.sems)])
                k += len(ex.sems)

            @pl.when(is_first())
            def _():
                for (ex, _), a, b, s in zip(self.riders, r_in, r_out, r_sem):
                    ex.start(a, b, s)

            body(*core)

            @pl.when(is_last())
            def _():
                for (ex, _), a, b, s in zip(self.riders, r_in, r_out, r_sem):
                    ex.finish(a, b, s)

        return wrapped

    def split(self, outs, n_out):
        core, rest, per = list(outs[:n_out]), list(outs[n_out:]), []
        for ex, _ in self.riders:
            per.append(rest[:ex.n])
            rest = rest[ex.n:]
        return core, per


def pair_sum(big, sib, c, *, tr, name):
    _, R, C = big.shape

    def body(c_ref, a_ref, s_ref, o_ref):
        o_ref[...] = (a_ref[...].astype(F32) + s_ref[...].astype(F32)).astype(o_ref.dtype)

    grid_spec = pltpu.PrefetchScalarGridSpec(
        num_scalar_prefetch=1, grid=(4, R // tr),
        in_specs=[pl.BlockSpec((None, tr, C), lambda b, i, c_ref: (2 * b + c_ref[0], i, 0)),
                  pl.BlockSpec((None, tr, C), lambda b, i, c_ref: (b, i, 0))],
        out_specs=pl.BlockSpec((None, tr, C), lambda b, i, c_ref: (b, i, 0)))
    return pl.pallas_call(body, grid_spec=grid_spec, out_shape=jax.ShapeDtypeStruct((4, R, C), big.dtype),
                          compiler_params=_cparams(2), name=name)(c.reshape(1).astype(jnp.int32), big, sib)


class ChipScatter:
    def __init__(self, pres, chunks):
        self.shapes = [jax.ShapeDtypeStruct(p.shape, p.dtype) for p in pres]
        self.n = len(pres)
        self.pieces = _pieces(pres, chunks)
        n_p = len(self.pieces)
        self.sems = [pltpu.SemaphoreType.DMA((3 * n_p,)), pltpu.SemaphoreType.DMA((3 * n_p,)),
                     pltpu.SemaphoreType.DMA((n_p,))]

    def _copies(self, pre_refs, out_refs, sems):
        send_sems, recv_sems, local_sems = sems
        n_p = len(self.pieces)
        x, y, c = _my_pos()
        my_chip = 2 * x + y
        chips = [(1 - x, y), (x, 1 - y), (1 - x, 1 - y)]

        def copy(j, p, slot):
            px, py = chips[j]
            a, rows = self.pieces[p]
            src_slot, dst_slot = (2 * px + py, my_chip) if slot == "mine" else (my_chip, 2 * px + py)
            return pltpu.make_async_remote_copy(
                src_ref=pre_refs[a].at[src_slot, rows], dst_ref=out_refs[a].at[dst_slot, rows],
                send_sem=send_sems.at[j * n_p + p], recv_sem=recv_sems.at[j * n_p + p],
                device_id=(px, py, c), device_id_type=MESH)

        own = [pltpu.make_async_copy(pre_refs[a].at[my_chip, rows], out_refs[a].at[my_chip, rows], local_sems.at[p])
               for p, (a, rows) in enumerate(self.pieces)]
        sends = [copy(j, p, "mine") for j in range(3) for p in range(n_p)]
        recvs = [copy(j, p, "theirs") for j in range(3) for p in range(n_p)]
        return own, sends, recvs

    def start(self, pre_refs, out_refs, sems):
        own, sends, _ = self._copies(pre_refs, out_refs, sems)
        for cp in sends + own:
            cp.start()

    def finish(self, pre_refs, out_refs, sems):
        own, sends, recvs = self._copies(pre_refs, out_refs, sems)
        for cp in recvs:
            cp.wait_recv()
        for cp in sends:
            cp.wait_send()
        for cp in own:
            cp.wait()


def chip_scatter(pres, chunks, *, name):
    cs = ChipScatter(pres, chunks)
    n = cs.n

    def body(*refs):
        pre_refs, out_refs, sems = refs[:n], refs[n:2 * n], refs[2 * n:]
        cs.start(pre_refs, out_refs, sems)
        cs.finish(pre_refs, out_refs, sems)

    return pl.pallas_call(body, out_shape=cs.shapes, in_specs=[_ANY] * n, out_specs=[_ANY] * n,
                          scratch_shapes=cs.sems, name=name)(*pres)


def sibling_send(blks, chunks, *, name):
    n = len(blks)
    pieces = _pieces(blks, chunks)
    n_p = len(pieces)

    def body(*refs):
        x_refs, out_refs = refs[:n], refs[n:2 * n]
        send_sems, recv_sems = refs[2 * n:]
        x, y, c = _my_pos()
        cps = [pltpu.make_async_remote_copy(
            src_ref=x_refs[a].at[rows], dst_ref=out_refs[a].at[rows], send_sem=send_sems.at[p],
            recv_sem=recv_sems.at[p], device_id=(x, y, 1 - c), device_id_type=MESH)
            for p, (a, rows) in enumerate(pieces)]
        for cp in cps:
            cp.start()
        for cp in cps:
            cp.wait_recv()
        for cp in cps:
            cp.wait_send()

    return pl.pallas_call(
        body, out_shape=[jax.ShapeDtypeStruct(b.shape, b.dtype) for b in blks],
        in_specs=[_ANY] * n, out_specs=[_ANY] * n,
        scratch_shapes=[pltpu.SemaphoreType.DMA((n_p,)), pltpu.SemaphoreType.DMA((n_p,))],
        name=name)(*blks)


def reduce_slots(land, *, tr, name):
    n, R, C = land.shape

    def body(l_ref, o_ref):
        acc = l_ref[0].astype(F32)
        for s in range(1, n):
            acc = acc + l_ref[s].astype(F32)
        o_ref[...] = acc

    return pl.pallas_call(
        body, grid=(R // tr,), in_specs=[pl.BlockSpec((n, tr, C), lambda i: (0, i, 0))],
        out_specs=pl.BlockSpec((tr, C), lambda i: (i, 0)), out_shape=jax.ShapeDtypeStruct((R, C), F32),
        compiler_params=_cparams(1), name=name)(land)


TM = 512


def _tk(d):
    return min(d.shape[0], 1024)


def ffn_fwd_fused(x, g, wgu, wd, *, tm, name):
    T, Dm = x.shape
    nb, _, cb = wgu.shape
    nh = nb // 2
    Fd = nh * cb

    def body(x_ref, g_ref, wgu_ref, wd_ref, o_ref, gu_ref, h_ref):
        xv = x_ref[...]
        r = lax.rsqrt(jnp.mean(xv * xv, axis=-1, keepdims=True) + RMS_EPS)
        hb = (xv * r * g_ref[...]).astype(BF16)
        h_ref[...] = hb
        acc = jnp.zeros((tm, Dm), F32)
        for jj in range(nh):
            cols = pl.ds(cb * jj, cb)
            gate = _dot(hb, wgu_ref[jj]).astype(BF16)
            up = _dot(hb, wgu_ref[nh + jj]).astype(BF16)
            gu_ref[0, :, cols] = gate
            gu_ref[1, :, cols] = up
            gv = gate.astype(F32)
            act = (gv * _sigmoid(gv) * up.astype(F32)).astype(BF16)
            acc = acc + _dot(act, wd_ref[cols, :])
        o_ref[...] = xv + 0.5 * acc

    return pl.pallas_call(
        body, grid=(T // tm,),
        in_specs=[pl.BlockSpec((tm, Dm), lambda i: (i, 0)), pl.BlockSpec((1, Dm), lambda i: (0, 0)),
                  pl.BlockSpec((nb, Dm, cb), lambda i: (0, 0, 0)), pl.BlockSpec((Fd, Dm), lambda i: (0, 0))],
        out_specs=[pl.BlockSpec((tm, Dm), lambda i: (i, 0)), pl.BlockSpec((2, tm, Fd), lambda i: (0, i, 0)),
                   pl.BlockSpec((tm, Dm), lambda i: (i, 0))],
        out_shape=[jax.ShapeDtypeStruct((T, Dm), F32), jax.ShapeDtypeStruct((2, T, Fd), BF16),
                   jax.ShapeDtypeStruct((T, Dm), BF16)],
        compiler_params=_cparams(1), name=name)(x, g, wgu, wd)


def ffn_fwd(x, g, wgu, wd, tag):
    xo, gu, h = ffn_fwd_fused(x, g, wgu, wd, tm=256, name=f"{tag}_fwd")
    return xo, (x, gu, h)


def ffn_bwd(d, saved, g, wgu, wd, tag, ride_bact=None, ride_dwgu=None, before_dx=None):
    x, gu, h = saved
    dgu, dwd, *rode_a = ffn_bwd_act(d, wd, gu, tm=TM, tn=1408, name=f"{tag}_bact", riding=ride_bact)
    dwgu = mm_tn(h, dgu, scale=1.0, a_split=False, b_split=True, tm=TM, tn=1408, tk=_tk(d), out_blocked=True,
                 name=f"{tag}_dwgu", riding=ride_dwgu)
    rode_g = []
    if ride_dwgu is not None:
        dwgu, *rode_g = dwgu
    riding = before_dx(dwgu, dwd) if before_dx is not None else None
    dx, dg, *rode_x = mm_nt_normbwd(dgu, wgu, x, g, d, a_split=True, tm=_tk(d), tk=1408, name=f"{tag}_dx",
                                    riding=riding)
    return dx, dg, dwgu, dwd, (rode_a, rode_g, rode_x)


def _tile2(v):
    return jnp.concatenate([v, v], axis=-1).reshape(1, LANES)


def _fold2(v):
    return v[:, :HEAD_DIM] + v[:, HEAD_DIM:]


EVEN = dict(dil=1, nsub=2, ppk=4, q_blk=0, k_blk=4, v_blk=5, n_heads=A_Q_HEADS, group=A_GROUP, max_dist=A_WINDOW - 1)
STICK = dict(q_blk=6, k_blk=10, v_blk=14, n_pairs=4)


def _odd_cfg(dil):
    return dict(dil=dil, nsub=4 if dil == 1 else 1, ppk=1, q_blk=0, k_blk=8, v_blk=16, n_heads=C_HEADS, group=1,
                max_dist=BLK)


def even_fwd(x, g, win, qg, kg, sinks, wout, tag, riders=None, rider_args=()):
    qkv, h = norm_matmul(x, g, win, tm=_tk(x), tn=1152, split=False, name=f"{tag}_in")
    qg2, kg2 = _tile2(qg), _tile2(kg)
    slopes = jnp.asarray(_alibi(A_Q_HEADS), F32)
    qkn = qk_norm(qkv, qg2, kg2, width=768, steps=1, n_q=4, tm=TM, name=f"{tag}_qkn")
    oa, lse = banded_fwd(qkn, qkv, slopes, sinks, name=f"{tag}_swa", **EVEN)
    ob, rode = stick_fwd(qkv, name=f"{tag}_stick", riders=riders, rider_args=rider_args, **STICK)
    o = jnp.concatenate([oa, ob], axis=1)
    xo = mm_nn(o, wout, res=x, scale=1.0, swiglu=False, tm=TM, tn=D_MODEL, tk=D_MODEL, name=f"{tag}_out")
    return xo, (x, qkv, qkn, h, oa, lse, o), rode


def even_bwd(d, saved, g, win, qg, kg, sinks, wout, tag, riders=None, rider_args=(), before_dx=None):
    x, qkv, qkn, h, oa, lse, o = saved
    qg2, kg2 = _tile2(qg), _tile2(kg)
    slopes = jnp.asarray(_alibi(A_Q_HEADS), F32)
    dwout = mm_tn(o, d, scale=1.0, a_split=False, b_split=False, tm=D_MODEL, tn=D_MODEL, tk=_tk(d), name=f"{tag}_dwout")
    do = mm_nt(d, wout, tm=TM, tn=D_MODEL, tk=D_MODEL, name=f"{tag}_do")
    dqa, dka4, dva4, dsk = banded_bwd(qkn, qkv, slopes, sinks, do, oa, lse, None, None,
                                      do_blk=0, name=f"{tag}_swa_b", **EVEN)
    dqb, dkb, dvb, rode = stick_bwd(qkv, do, do_blk=4, name=f"{tag}_stick_b", riders=riders, rider_args=rider_args,
                                    **STICK)
    dqkv, dqg, dkg = assemble_even(dqa, dka4, dva4, dqb, dkb, dvb, qkv, qg2, kg2, tm=TM, name=f"{tag}_asm")
    dwin = mm_tn(h, dqkv, scale=1.0, a_split=False, b_split=False, tm=D_MODEL, tn=1152, tk=_tk(d), name=f"{tag}_dwin")
    riding = before_dx(dwin, dwout) if before_dx is not None else None
    dx, dg, *rode_x = mm_nt_normbwd(dqkv, win, x, g, d, a_split=False, tm=TM, tk=1152, name=f"{tag}_dx", riding=riding)
    return dx, dg, dwin, _fold2(dqg), _fold2(dkg), dsk[:, :A_Q_HEADS], dwout, (rode, rode_x)


def odd_fwd(x, g, win, qg, kg, wout, tag):
    qkv, h = norm_matmul(x, g, win, tm=_tk(x), tn=768, split=False, name=f"{tag}_in")
    qg2, kg2 = _tile2(qg), _tile2(kg)
    qkn = qk_norm(qkv, qg2, kg2, width=D_MODEL, steps=2, n_q=8, tm=TM, name=f"{tag}_qkn")
    outs = []
    for p, (window, dil) in enumerate(C_PATTERNS):
        slopes = jnp.asarray(_alibi(C_HEADS), F32) * float(dil)
        outs.append(banded_fwd(qkn, qkv, slopes, None, name=f"{tag}_dil{p}", **_odd_cfg(dil)))
    o, w1, w2, w3 = mix_fwd(outs[0][0], outs[1][0], outs[2][0], outs[0][1], outs[1][1], outs[2][1],
                            tm=TM, name=f"{tag}_mix")
    xo = mm_nn(o, wout, res=x, scale=1.0, swiglu=False, tm=TM, tn=D_MODEL, tk=D_MODEL, name=f"{tag}_out")
    return xo, (x, qkv, qkn, h, outs, (w1, w2, w3), o)


def odd_bwd(d, saved, g, win, qg, kg, wout, tag):
    x, qkv, qkn, h, outs, ws, o = saved
    qg2, kg2 = _tile2(qg), _tile2(kg)
    dwout = mm_tn(o, d, scale=1.0, a_split=False, b_split=False, tm=D_MODEL, tn=D_MODEL, tk=_tk(d), name=f"{tag}_dwout")
    do = mm_nt(d, wout, tm=TM, tn=D_MODEL, tk=D_MODEL, name=f"{tag}_do")
    parts = []
    for p, (window, dil) in enumerate(C_PATTERNS):
        slopes = jnp.asarray(_alibi(C_HEADS), F32) * float(dil)
        dq, dk, dv, _ = banded_bwd(qkn, qkv, slopes, None, do, outs[p][0], outs[p][1], ws[p], o,
                                   do_blk=0, name=f"{tag}_dil{p}_b", **_odd_cfg(dil))
        parts.append((dq, dk, dv))
    dqkv, dqg, dkg = assemble_odd(parts, qkv, qg2, kg2, tm=256, name=f"{tag}_asm")
    dwin = mm_tn(h, dqkv, scale=1.0, a_split=False, b_split=False, tm=TM, tn=768, tk=_tk(d), out_blocked=True,
                 name=f"{tag}_dwin")
    dx, dg = mm_nt_normbwd(dqkv, win, x, g, d, a_split=False, tm=TM, tk=768, name=f"{tag}_dx")
    return dx, dg, dwin, _fold2(dqg), _fold2(dkg), dwout


def xa_fwd(x, mem, g, gm, wq, wkv, qg, kg, wo, tag):
    qraw, h = norm_matmul(x, g, wq, tm=TM, tn=D_MODEL, split=False, name=f"{tag}_q")
    kvraw, hm = norm_matmul(mem, gm, wkv, tm=MEM_LEN, tn=512, split=False, name=f"{tag}_kv")
    o = xattn_fwd(qraw, kvraw, qg, kg, tm=TM, name=f"{tag}_att")
    xo = mm_nn(o, wo, res=x, scale=1.0, swiglu=False, tm=TM, tn=D_MODEL, tk=D_MODEL, name=f"{tag}_o")
    return xo, (x, qraw, h, kvraw, hm, o)


def xa_bwd(d, saved, mem, g, gm, wq, wkv, qg, kg, wo, tag):
    x, qraw, h, kvraw, hm, o = saved
    dwo = mm_tn(o, d, scale=1.0, a_split=False, b_split=False, tm=D_MODEL, tn=D_MODEL, tk=_tk(d), name=f"{tag}_dwo")
    do = mm_nt(d, wo, tm=TM, tn=D_MODEL, tk=D_MODEL, name=f"{tag}_do")
    dq, dkv, dqg, dkg = xattn_bwd(qraw, kvraw, qg, kg, do, o, tm=TM, name=f"{tag}_att_b")
    dwq = mm_tn(h, dq, scale=1.0, a_split=False, b_split=False, tm=D_MODEL, tn=D_MODEL, tk=_tk(d), name=f"{tag}_dwq")
    dx, dg = mm_nt_normbwd(dq, wq, x, g, d, a_split=False, tm=TM, tk=D_MODEL, name=f"{tag}_dx")
    dwkv = mm_tn(hm, dkv, scale=1.0, a_split=False, b_split=False, tm=TM, tn=512, tk=MEM_LEN, out_blocked=True,
                 name=f"{tag}_dwkv")
    _, dgm = mm_nt_normbwd(dkv, wkv, mem, gm, None, a_split=False, tm=MEM_LEN, tk=512, name=f"{tag}_dmem")
    return dx, dg, dgm, dwq, dwkv, dqg, dkg, dwo


MATS = (("ffn1_w_gu", 1), ("ffn1_w_down", 0), ("ev_w_in", 1), ("ev_w_out", 0), ("od_w_in", 1), ("od_w_out", 0),
        ("xa_w_q", 0), ("xa_w_kv", 1), ("xa_w_o", 0), ("ffn2_w_gu", 1), ("ffn2_w_down", 0))
SMALLS = ("ffn1_norm", "mix_norm", "ev_q_gain", "ev_k_gain", "ev_sinks", "od_q_gain", "od_k_gain", "xa_norm",
          "xa_mem_norm", "xa_q_gain", "xa_k_gain", "ffn2_norm")
WEIGHTS = ("ffn1_norm", "ffn1_w_gu", "ffn1_w_down", "mix_norm", "ev_w_in", "ev_q_gain", "ev_k_gain", "ev_sinks",
           "ev_w_out", "od_w_in", "od_q_gain", "od_k_gain", "od_w_out", "xa_norm", "xa_mem_norm", "xa_w_q",
           "xa_w_kv", "xa_q_gain", "xa_k_gain", "xa_w_o", "ffn2_norm", "ffn2_w_gu", "ffn2_w_down")
SMALL_ROWS = 16
LAYER_GROUPS = (
    (((("ffn1_w_gu", 0), ("ffn2_w_gu", 0)), 4, 512),
     ((("ffn1_w_down", 0), ("ffn2_w_down", 0)), 2, 352),
     ((("ev_w_out", 0), ("xa_w_q", 0), ("xa_w_o", 0)), 1, 384),
     ((("xa_w_kv", 0),), 1, 512),
     ((("ev_w_in", 0),), 1, 512)),
    (((("ffn1_w_gu", 1), ("ffn2_w_gu", 1)), 4, 512),
     ((("ffn1_w_down", 1), ("ffn2_w_down", 1)), 2, 352),
     ((("od_w_out", 0), ("xa_w_q", 1), ("xa_w_o", 1)), 1, 384),
     ((("xa_w_kv", 1),), 1, 512),
     ((("od_w_in", 0),), 1, 512)),
)
GROUPS = LAYER_GROUPS[0] + LAYER_GROUPS[1]
ROUNDS = {
    "1": LAYER_GROUPS[1],
    "0a": (((("ffn2_w_gu", 0),), 2, 512), ((("ffn2_w_down", 0),), 1, 352)) + LAYER_GROUPS[0][2:],
    "0b": (((("ffn1_w_gu", 0),), 2, 512), ((("ffn1_w_down", 0),), 1, 352)),
}


def _chunks_of(groups):
    return tuple(g[1] for g in groups)
COL_SHARDED = {name for name, axis in MATS if axis == 1}
BLOCKED = {"ffn1_w_gu", "ffn2_w_gu", "xa_w_kv", "od_w_in"}


def group_halves(shards, c, groups):
    out = []
    for members, _, _ in groups:
        halves = []
        for name, layer in members:
            _, r, cc = shards[name].shape
            half = lax.dynamic_index_in_dim(shards[name][layer].reshape(2, r // 2, cc), c, 0, keepdims=False)
            halves.append(half.astype(BF16))
        out.append(jnp.concatenate(halves, axis=0))
    return out


def full_weights(gathered, shards, groups):
    full = {}
    for (members, _, _), arr in zip(groups, gathered):
        for w, (name, layer) in enumerate(members):
            _, r, cc = shards[name].shape
            piece = arr[:, w * (r // 2):(w + 1) * (r // 2)].reshape(4, r, cc)
            if name not in COL_SHARDED:
                piece = piece.reshape(4 * r, cc)
            elif name not in BLOCKED:
                piece = piece.transpose(1, 0, 2).reshape(r, 4 * cc)
            full[(name, layer)] = piece
    return full


def group_grads(grads, shards, groups):
    out = []
    for members, _, _ in groups:
        parts = []
        for name, layer in members:
            _, r, cc = shards[name].shape
            gfull = grads[(name, layer)]
            if name in COL_SHARDED and name not in BLOCKED:
                gfull = gfull.reshape(2, r // 2, 4, cc).transpose(2, 0, 1, 3)
            parts.append(gfull.reshape(N_DEV, r // 2, cc))
        out.append(jnp.concatenate(parts, axis=1))
    return out


def shard_grads(mine, theirs, c, shards, groups):
    per = {}
    for (members, _, _), a, b in zip(groups, mine, theirs):
        for w, (name, layer) in enumerate(members):
            _, r, cc = shards[name].shape
            rows = slice(w * (r // 2), (w + 1) * (r // 2))
            lo = jnp.where(c == 0, a[rows], b[rows])
            hi = jnp.where(c == 0, b[rows], a[rows])
            per[(name, layer)] = jnp.concatenate([lo, hi], axis=0)
    return per


def pack_small(vals):
    row10 = jnp.concatenate([vals["xa_q_gain"].reshape(1, 512), vals["xa_k_gain"].reshape(1, 512)], axis=1)
    row11 = jnp.concatenate([vals["ev_q_gain"], vals["ev_k_gain"], vals["od_q_gain"], vals["od_k_gain"],
                             vals["ev_sinks"], jnp.zeros((1, 1024 - 4 * 64 - 8), F32)], axis=1)
    return jnp.concatenate([vals["ffn1_norm"], vals["mix_norm"], vals["xa_norm"], vals["xa_mem_norm"],
                            vals["ffn2_norm"], row10, row11, jnp.zeros((SMALL_ROWS - 12, 1024), F32)], axis=0)


def unpack_small(arr):
    return {"ffn1_norm": arr[0:2], "mix_norm": arr[2:4], "xa_norm": arr[4:6], "xa_mem_norm": arr[6:8],
            "ffn2_norm": arr[8:10],
            "xa_q_gain": arr[10:11, 0:512].reshape(2, 256), "xa_k_gain": arr[10:11, 512:1024].reshape(2, 256),
            "ev_q_gain": arr[11:12, 0:64], "ev_k_gain": arr[11:12, 64:128], "od_q_gain": arr[11:12, 128:192],
            "od_k_gain": arr[11:12, 192:256], "ev_sinks": arr[11:12, 256:264]}


def local_step(x, mem, target, W, small, prereduce, later):
    depth = small["ffn1_norm"].shape[0]

    def row(name, l):
        return small[name][l:l + 1]

    saved = []
    for l in range(depth):
        j = l // 2
        x, s1 = ffn_fwd(x, row("ffn1_norm", l), W[("ffn1_w_gu", l)], W[("ffn1_w_down", l)], f"l{l}_f1")
        if l % 2 == 0:
            riders, rider_args = None, ()
            if later is not None and l == 0:
                riders, rider_args = GatherBlocks(later[0], later[1]), later[0]
            x, s2, rode = even_fwd(x, row("mix_norm", l), W[("ev_w_in", j)], row("ev_q_gain", j),
                                   row("ev_k_gain", j), small["ev_sinks"][j], W[("ev_w_out", j)], f"l{l}_ev",
                                   riders=riders, rider_args=rider_args)
            if riders is not None:
                W = {**W, **later[2](rode)}
        else:
            x, s2 = odd_fwd(x, row("mix_norm", l), W[("od_w_in", j)], row("od_q_gain", j), row("od_k_gain", j),
                            W[("od_w_out", j)], f"l{l}_od")
        x, s3 = xa_fwd(x, mem, row("xa_norm", l), row("xa_mem_norm", l), W[("xa_w_q", l)], W[("xa_w_kv", l)],
                       row("xa_q_gain", l), row("xa_k_gain", l), W[("xa_w_o", l)], f"l{l}_xa")
        x, s4 = ffn_fwd(x, row("ffn2_norm", l), W[("ffn2_w_gu", l)], W[("ffn2_w_down", l)], f"l{l}_f2")
        saved.append((s1, s2, s3, s4))
    loss, d = loss_kernel(x, target, tm=TM, name="loss")

    gw = {}
    gs = {name: [None] * small[name].shape[0] for name in SMALLS}
    pending, landed = None, {}
    for l in reversed(range(depth)):
        j = l // 2
        s1, s2, s3, s4 = saved[l]
        d, dg, dwgu, dwd, _ = ffn_bwd(d, s4, row("ffn2_norm", l), W[("ffn2_w_gu", l)], W[("ffn2_w_down", l)],
                                      f"l{l}_f2")
        gs["ffn2_norm"][l] = dg
        gw[("ffn2_w_gu", l)], gw[("ffn2_w_down", l)] = dwgu, dwd
        d, dg, dgm, dwq, dwkv, dqg, dkg, dwo = xa_bwd(
            d, s3, mem, row("xa_norm", l), row("xa_mem_norm", l), W[("xa_w_q", l)], W[("xa_w_kv", l)],
            row("xa_q_gain", l), row("xa_k_gain", l), W[("xa_w_o", l)], f"l{l}_xa")
        gs["xa_norm"][l], gs["xa_mem_norm"][l], gs["xa_q_gain"][l], gs["xa_k_gain"][l] = dg, dgm, dqg, dkg
        gw[("xa_w_q", l)], gw[("xa_w_kv", l)], gw[("xa_w_o", l)] = dwq, dwkv, dwo
        split = l == 0 and l % 2 == 0 and ("0a" in ROUNDS)
        early = []
        pre_early = None
        if l % 2 == 0:
            riders, rider_args = None, ()
            if pending is not None:
                riders, rider_args = ChipScatter(pending[1], _chunks_of(ROUNDS[pending[0]])), pending[1]

            def before_mixer_dx(dwin, dwout, j=j, early=early):
                gw[("ev_w_in", j)], gw[("ev_w_out", j)] = dwin, dwout
                early += prereduce.pack(gw, "0a")
                return Riding([(PairExchange(early, _chunks_of(ROUNDS["0a"])), early)])

            d, dg, dwin, dqg, dkg, dsk, dwout, (rode, rode_x) = even_bwd(
                d, s2, row("mix_norm", l), W[("ev_w_in", j)], row("ev_q_gain", j), row("ev_k_gain", j),
                small["ev_sinks"][j], W[("ev_w_out", j)], f"l{l}_ev", riders=riders, rider_args=rider_args,
                before_dx=before_mixer_dx if split else None)
            if pending is not None:
                landed[pending[0]], pending = rode, None
            gs["ev_q_gain"][j], gs["ev_k_gain"][j], gs["ev_sinks"][j] = dqg, dkg, dsk
            gw[("ev_w_in", j)], gw[("ev_w_out", j)] = dwin, dwout
            if split:
                pre_early = prereduce.sums(early, rode_x[0], "0a")
        else:
            d, dg, dwin, dqg, dkg, dwout = odd_bwd(
                d, s2, row("mix_norm", l), W[("od_w_in", j)], row("od_q_gain", j), row("od_k_gain", j),
                W[("od_w_out", j)], f"l{l}_od")
            gs["od_q_gain"][j], gs["od_k_gain"][j] = dqg, dkg
            gw[("od_w_in", j)], gw[("od_w_out", j)] = dwin, dwout
        gs["mix_norm"][l] = dg
        packed = []

        rnd = "0b" if pre_early is not None else str(l)

        def before_dx(dwgu, dwd, l=l, packed=packed, rnd=rnd):
            gw[("ffn1_w_gu", l)], gw[("ffn1_w_down", l)] = dwgu, dwd
            packed += prereduce.pack(gw, rnd)
            return Riding([(PairExchange(packed, _chunks_of(ROUNDS[rnd])), packed)])

        ride_bact = ride_dwgu = None
        if pre_early is not None:
            chunks = _chunks_of(ROUNDS["0a"])
            ride_bact = Riding([(ChipScatter(pre_early[:2], chunks[:2]), pre_early[:2])])
            ride_dwgu = Riding([(ChipScatter(pre_early[2:], chunks[2:]), pre_early[2:])])
        d, dg, dwgu, dwd, (rode_a, rode_g, rode_x) = ffn_bwd(
            d, s1, row("ffn1_norm", l), W[("ffn1_w_gu", l)], W[("ffn1_w_down", l)], f"l{l}_f1",
            ride_bact=ride_bact, ride_dwgu=ride_dwgu, before_dx=before_dx)
        gs["ffn1_norm"][l] = dg
        if pre_early is not None:
            landed["0a"] = list(rode_a[0]) + list(rode_g[0])
        if pending is not None:
            landed[pending[0]] = chip_scatter(pending[1], _chunks_of(ROUNDS[pending[0]]),
                                              name=f"scatter_grads{pending[0]}")
        pending = (rnd, prereduce.sums(packed, rode_x[0], rnd))
    landed[pending[0]] = chip_scatter(pending[1], _chunks_of(ROUNDS[pending[0]]), name=f"scatter_grads{pending[0]}")
    gsmall = {name: jnp.concatenate(v, axis=0) for name, v in gs.items()}
    return loss, d, landed, gsmall


def kernel(x, mem, ffn1_norm, ffn1_w_gu, ffn1_w_down, mix_norm, ev_w_in, ev_q_gain, ev_k_gain, ev_sinks, ev_w_out, od_w_in, od_q_gain, od_k_gain, od_w_out, xa_norm, xa_mem_norm, xa_w_q, xa_w_kv, xa_q_gain, xa_k_gain, xa_w_o, ffn2_norm, ffn2_w_gu, ffn2_w_down, loss_target, m_ffn1_norm, m_ffn1_w_gu, m_ffn1_w_down, m_mix_norm, m_ev_w_in, m_ev_q_gain, m_ev_k_gain, m_ev_sinks, m_ev_w_out, m_od_w_in, m_od_q_gain, m_od_k_gain, m_od_w_out, m_xa_norm, m_xa_mem_norm, m_xa_w_q, m_xa_w_kv, m_xa_q_gain, m_xa_k_gain, m_xa_w_o, m_ffn2_norm, m_ffn2_w_gu, m_ffn2_w_down, v_ffn1_norm, v_ffn1_w_gu, v_ffn1_w_down, v_mix_norm, v_ev_w_in, v_ev_q_gain, v_ev_k_gain, v_ev_sinks, v_ev_w_out, v_od_w_in, v_od_q_gain, v_od_k_gain, v_od_w_out, v_xa_norm, v_xa_mem_norm, v_xa_w_q, v_xa_w_kv, v_xa_q_gain, v_xa_k_gain, v_xa_w_o, v_ffn2_norm, v_ffn2_w_gu, v_ffn2_w_down):
    given = dict(locals())
    w = {n: given[n] for n in WEIGHTS}
    m = {n: given["m_" + n] for n in WEIGHTS}
    v = {n: given["v_" + n] for n in WEIGHTS}
    c = lax.axis_index("c")
    shards = {name: w[name] for name, _ in MATS}
    small = {n: w[n] for n in SMALLS}

    groups0, groups1 = LAYER_GROUPS
    gathered = gather_blocks(group_halves(shards, c, groups0), _chunks_of(groups0), name="gather_weights0")
    full = full_weights(gathered, shards, groups0)
    later = (group_halves(shards, c, groups1), _chunks_of(groups1), lambda got: full_weights(got, shards, groups1))

    class prereduce:
        @staticmethod
        def pack(gw, rnd):
            return group_grads(gw, shards, ROUNDS[rnd])

        @staticmethod
        def sums(packed, sib, rnd):
            return [pair_sum(p, s, c, tr=g[2], name=f"pair_sum{rnd}_{i}")
                    for i, (g, p, s) in enumerate(zip(ROUNDS[rnd], packed, sib))]

    loss_b, grad_x, landed, gsmall = local_step(x[0], mem[0], loss_target[0], full, small, prereduce, later)

    per = {}
    for rnd, land in sorted(landed.items()):
        groups = ROUNDS[rnd]
        mine = [reduce_slots(a, tr=g[2], name=f"sum_grads{rnd}_{i}") for i, (g, a) in enumerate(zip(groups, land))]
        theirs = sibling_send(mine, _chunks_of(groups), name=f"swap_grads{rnd}")
        per.update(shard_grads(mine, theirs, c, shards, groups))
    g = {name: jnp.stack([per[(name, layer)] for layer in range(w[name].shape[0])], axis=0) for name, _ in MATS}
    land_small = gather_small(pack_small(gsmall), name="gather_small")
    g_small = unpack_small(reduce_slots(land_small, tr=SMALL_ROWS, name="sum_small"))
    g.update(g_small)

    delta, new_m, new_v = {}, {}, {}
    for name, _ in MATS:
        shp = w[name].shape
        flat = [a.reshape(-1, shp[-1]) for a in (w[name], g[name], m[name], v[name])]
        dl, nm, nv = adamw(*flat, br=BLK, name=f"adamw_{name}")
        delta[name], new_m[name], new_v[name] = dl.reshape(shp), nm.reshape(shp), nv.reshape(shp)
    dl, nm, nv = adamw(pack_small(small), pack_small(g_small), pack_small({n: m[n] for n in SMALLS}),
                       pack_small({n: v[n] for n in SMALLS}), br=SMALL_ROWS, name="adamw_small")
    for dst, arr in ((delta, dl), (new_m, nm), (new_v, nv)):
        dst.update(unpack_small(arr))

    loss = lax.psum(loss_b[0, 0], ("x", "y", "c"))
    return (loss, grad_x[None], *[g[n] for n in WEIGHTS], *[delta[n] for n in WEIGHTS],
            *[new_m[n] for n in WEIGHTS], *[new_v[n] for n in WEIGHTS])
```

```python
import jax
import jax.numpy as jnp
from jax import lax
from jax.experimental import pallas as pl
from jax.experimental.pallas import tpu as pltpu

F32 = jnp.float32
BF16 = jnp.bfloat16

D_MODEL = 1024
HEAD_DIM = 64
LANES = 128
BLK = 128
D_FF = 2816
RMS_EPS = 1e-6
MEM_LEN = 256
X_HEADS = 4
X_HEAD_DIM = 256
A_Q_HEADS = 8
A_GROUP = 4
A_WINDOW = 128
C_HEADS = 16
C_PATTERNS = ((128, 1), (512, 4), (2048, 16))
NEG = -1e30
VMEM_LIMIT = 56 * 2 ** 20

ADAM_LR = 0.001
ADAM_B1 = 0.9
ADAM_B2 = 0.999
ADAM_EPS = 1e-08
ADAM_WD = 0.01
ADAM_STEP = 10

N_DEV = 8
MESH = pl.DeviceIdType.MESH


def _cparams(n):
    return pltpu.CompilerParams(dimension_semantics=("arbitrary",) * n, vmem_limit_bytes=VMEM_LIMIT)


def _dot(a, b):
    return jnp.dot(a, b, preferred_element_type=F32)


def _dot_nt(a, b):
    return lax.dot_general(a, b, (((1,), (1,)), ((), ())), preferred_element_type=F32)


def _dot_tn(a, b):
    return lax.dot_general(a, b, (((0,), (0,)), ((), ())), preferred_element_type=F32)


def _sigmoid(z):
    return 1.0 / (1.0 + jnp.exp(-z))


def norm_matmul(x, g, w, *, tm, tn, split, name):
    T, K = x.shape
    blocked = w.ndim == 3
    assert not blocked or w.shape[2] == tn
    N = w.shape[0] * w.shape[2] if blocked else w.shape[1]
    nj = N // tn

    def body(x_ref, g_ref, w_ref, o_ref, h_ref):
        @pl.when(pl.program_id(1) == 0)
        def _():
            xv = x_ref[...]
            r = lax.rsqrt(jnp.mean(xv * xv, axis=-1, keepdims=True) + RMS_EPS)
            h_ref[...] = (xv * r * g_ref[...]).astype(BF16)

        o_ref[...] = _dot(h_ref[...], w_ref[...]).astype(o_ref.dtype)

    if split:
        njh = nj // 2
        o_shape = jax.ShapeDtypeStruct((2, T, N // 2), BF16)
        o_spec = pl.BlockSpec((None, tm, tn), lambda i, j: (j // njh, i, j % njh))
    else:
        o_shape = jax.ShapeDtypeStruct((T, N), F32)
        o_spec = pl.BlockSpec((tm, tn), lambda i, j: (i, j))
    return pl.pallas_call(
        body, grid=(T // tm, nj),
        in_specs=[pl.BlockSpec((tm, K), lambda i, j: (i, 0)),
                  pl.BlockSpec((1, K), lambda i, j: (0, 0)),
                  (pl.BlockSpec((None, K, tn), lambda i, j: (j, 0, 0)) if blocked
                   else pl.BlockSpec((K, tn), lambda i, j: (0, j)))],
        out_specs=[o_spec, pl.BlockSpec((tm, K), lambda i, j: (i, 0))],
        out_shape=[o_shape, jax.ShapeDtypeStruct((T, K), BF16)],
        compiler_params=_cparams(2), name=name)(x, g, w)


def mm_nn(a, b, *, res, tm, tn, tk, name):
    T = a.shape[0]
    K, N = b.shape
    nk = K // tk

    def body(a_ref, b_ref, r_ref, o_ref, acc):
        k = pl.program_id(2)

        @pl.when(k == 0)
        def _():
            acc[...] = jnp.zeros_like(acc)

        acc[...] += _dot(a_ref[...].astype(BF16), b_ref[...])

        @pl.when(k == nk - 1)
        def _():
            o_ref[...] = r_ref[...] + acc[...]

    return pl.pallas_call(
        body, grid=(T // tm, N // tn, nk),
        in_specs=[pl.BlockSpec((tm, tk), lambda i, j, k: (i, k)), pl.BlockSpec((tk, tn), lambda i, j, k: (k, j)),
                  pl.BlockSpec((tm, tn), lambda i, j, k: (i, j))],
        out_specs=pl.BlockSpec((tm, tn), lambda i, j, k: (i, j)),
        out_shape=jax.ShapeDtypeStruct((T, N), F32),
        scratch_shapes=[pltpu.VMEM((tm, tn), F32)],
        compiler_params=_cparams(3), name=name)(a, b, res)


def mm_nt(a, b, *, tm, tn, tk, name):
    T, K = a.shape
    N = b.shape[0]
    nk = K // tk

    def body(a_ref, b_ref, o_ref, acc):
        k = pl.program_id(2)

        @pl.when(k == 0)
        def _():
            acc[...] = jnp.zeros_like(acc)

        acc[...] += _dot_nt(a_ref[...].astype(BF16), b_ref[...])

        @pl.when(k == nk - 1)
        def _():
            o_ref[...] = acc[...]

    return pl.pallas_call(
        body, grid=(T // tm, N // tn, nk),
        in_specs=[pl.BlockSpec((tm, tk), lambda i, j, k: (i, k)),
                  pl.BlockSpec((tn, tk), lambda i, j, k: (j, k))],
        out_specs=pl.BlockSpec((tm, tn), lambda i, j, k: (i, j)),
        out_shape=jax.ShapeDtypeStruct((T, N), F32),
        scratch_shapes=[pltpu.VMEM((tm, tn), F32)],
        compiler_params=_cparams(3), name=name)(a, b)


def ffn_bwd_act(d, wd, gu, *, tm, tn, name, riding=None):
    T, K = d.shape
    Fd = wd.shape[0]
    ni = T // tm

    def body(d_ref, w_ref, g_ref, u_ref, dgu_ref, dwd_ref, acc):
        i = pl.program_id(1)

        @pl.when(i == 0)
        def _():
            acc[...] = jnp.zeros_like(acc)

        db = d_ref[...].astype(BF16)
        da = 0.5 * _dot_nt(db, w_ref[...])
        gv = g_ref[...].astype(F32)
        uv = u_ref[...].astype(F32)
        s = _sigmoid(gv)
        silu = gv * s
        acc[...] += _dot_tn((silu * uv).astype(BF16), db)
        dgu_ref[0] = (da * uv * (s * (1.0 + gv * (1.0 - s)))).astype(BF16)
        dgu_ref[1] = (da * silu).astype(BF16)

        @pl.when(i == ni - 1)
        def _():
            dwd_ref[...] = (0.5 * acc[...]).astype(BF16)

    in_specs = [pl.BlockSpec((tm, K), lambda j, i: (i, 0)),
                pl.BlockSpec((tn, K), lambda j, i: (j, 0)),
                pl.BlockSpec((None, tm, tn), lambda j, i: (0, i, j)),
                pl.BlockSpec((None, tm, tn), lambda j, i: (1, i, j))]
    out_specs = [pl.BlockSpec((2, tm, tn), lambda j, i: (0, i, j)), pl.BlockSpec((tn, K), lambda j, i: (j, 0))]
    out_shape = [jax.ShapeDtypeStruct((2, T, Fd), BF16), jax.ShapeDtypeStruct((Fd, K), BF16)]
    return _call_with_riders(body, riding, (Fd // tn, ni), in_specs, out_specs, out_shape,
                             [pltpu.VMEM((tn, K), F32)], [d, wd, gu, gu], name)


def _call_with_riders(body, riding, grid, in_specs, out_specs, out_shape, scratch, args, name):
    n_out = len(out_shape)
    if riding is None:
        return pl.pallas_call(body, grid=grid, in_specs=in_specs, out_specs=out_specs, out_shape=out_shape,
                              scratch_shapes=scratch, compiler_params=_cparams(len(grid)), name=name)(*args)

    def is_first():
        ok = pl.program_id(0) == 0
        for ax in range(1, len(grid)):
            ok = ok & (pl.program_id(ax) == 0)
        return ok

    def is_last():
        ok = pl.program_id(0) == grid[0] - 1
        for ax in range(1, len(grid)):
            ok = ok & (pl.program_id(ax) == grid[ax] - 1)
        return ok

    outs = pl.pallas_call(
        riding.wrap(body, len(in_specs), n_out, len(scratch), is_first, is_last), grid=grid,
        in_specs=list(in_specs) + riding.in_specs, out_specs=list(out_specs) + riding.out_specs,
        out_shape=list(out_shape) + riding.out_shapes, scratch_shapes=list(scratch) + riding.scratch,
        compiler_params=_cparams(len(grid)), name=name)(*args, *riding.args)
    core, per = riding.split(outs, n_out)
    return (*core, *per)


def mm_nt_normbwd(a, b, x, g, res, *, a_split, tm, tk, name, riding=None):
    T, Dm = x.shape
    blocked = b.ndim == 3
    assert not blocked or b.shape[2] == tk
    K = b.shape[0] * b.shape[2] if blocked else b.shape[1]
    nk = K // tk
    nkh = nk // 2
    has_res = res is not None

    def body(*refs):
        if has_res:
            a_ref, b_ref, x_ref, g_ref, r_ref, dx_ref, dg_ref, acc = refs
        else:
            a_ref, b_ref, x_ref, g_ref, dx_ref, dg_ref, acc = refs
        i = pl.program_id(0)
        k = pl.program_id(1)

        @pl.when(k == 0)
        def _():
            acc[...] = jnp.zeros_like(acc)

        acc[...] += _dot_nt(a_ref[...].astype(BF16), b_ref[...])

        @pl.when(k == nk - 1)
        def _():
            xv = x_ref[...]
            r = lax.rsqrt(jnp.mean(xv * xv, axis=-1, keepdims=True) + RMS_EPS)
            xh = xv * r
            dh = acc[...]
            dxh = dh * g_ref[...]
            dx = r * (dxh - xh * jnp.mean(dxh * xh, axis=-1, keepdims=True))
            if has_res:
                dx = dx + r_ref[...]
            dx_ref[...] = dx
            part = jnp.sum(dh * xh, axis=0, keepdims=True)

            @pl.when(i == 0)
            def _():
                dg_ref[...] = part

            @pl.when(i > 0)
            def _():
                dg_ref[...] += part

    if a_split:
        a_spec = pl.BlockSpec((None, tm, tk), lambda i, k: (k // nkh, i, k % nkh))
    else:
        a_spec = pl.BlockSpec((tm, tk), lambda i, k: (i, k))
    in_specs = [a_spec,
                (pl.BlockSpec((None, Dm, tk), lambda i, k: (k, 0, 0)) if blocked
                 else pl.BlockSpec((Dm, tk), lambda i, k: (0, k))),
                pl.BlockSpec((tm, Dm), lambda i, k: (i, 0)),
                pl.BlockSpec((1, Dm), lambda i, k: (0, 0))]
    args = [a, b, x, g]
    if has_res:
        in_specs.append(pl.BlockSpec((tm, Dm), lambda i, k: (i, 0)))
        args.append(res)
    out_specs = [pl.BlockSpec((tm, Dm), lambda i, k: (i, 0)), pl.BlockSpec((1, Dm), lambda i, k: (0, 0))]
    out_shape = [jax.ShapeDtypeStruct((T, Dm), F32), jax.ShapeDtypeStruct((1, Dm), F32)]
    scratch = [pltpu.VMEM((tm, Dm), F32)]
    return _call_with_riders(body, riding, (T // tm, nk), in_specs, out_specs, out_shape, scratch, args, name)


def mm_tn(a, b, *, scale, a_split, b_split, tm, tn, tk, name, out_blocked=False, riding=None):
    T = a.shape[-2]
    M = a.shape[-1] * (2 if a_split else 1)
    N = b.shape[-1] * (2 if b_split else 1)
    ni, nj, nk = M // tm, N // tn, T // tk
    nih, njh = ni // 2, nj // 2

    def body(a_ref, b_ref, o_ref, acc):
        k = pl.program_id(2)

        @pl.when(k == 0)
        def _():
            acc[...] = jnp.zeros_like(acc)

        acc[...] += _dot_tn(a_ref[...].astype(BF16), b_ref[...].astype(BF16))

        @pl.when(k == nk - 1)
        def _():
            o_ref[...] = (acc[...] * scale).astype(o_ref.dtype)

    if a_split:
        a_spec = pl.BlockSpec((None, tk, tm), lambda i, j, k: (i // nih, k, i % nih))
    else:
        a_spec = pl.BlockSpec((tk, tm), lambda i, j, k: (k, i))
    if b_split:
        b_spec = pl.BlockSpec((None, tk, tn), lambda i, j, k: (j // njh, k, j % njh))
    else:
        b_spec = pl.BlockSpec((tk, tn), lambda i, j, k: (k, j))
    if out_blocked:
        o_spec = pl.BlockSpec((None, None, tm, tn), lambda i, j, k: (j, i, 0, 0))
        o_shape = jax.ShapeDtypeStruct((nj, ni, tm, tn), BF16)
    else:
        o_spec = pl.BlockSpec((tm, tn), lambda i, j, k: (i, j))
        o_shape = jax.ShapeDtypeStruct((M, N), BF16)
    outs = _call_with_riders(body, riding, (ni, nj, nk), [a_spec, b_spec], [o_spec], [o_shape],
                             [pltpu.VMEM((tm, tn), F32)], [a, b], name)
    return outs[0] if riding is None else tuple(outs)


def loss_kernel(y, target, *, tm, name):
    T, Dm = y.shape

    def body(y_ref, t_ref, l_ref, dy_ref):
        e = y_ref[...] - t_ref[...]
        dy_ref[...] = e * (1.0 / Dm)
        part = (0.5 / Dm) * jnp.sum(jnp.sum(e * e, axis=-1, keepdims=True), axis=0, keepdims=True)
        part = jnp.broadcast_to(part, (8, LANES))

        @pl.when(pl.program_id(0) == 0)
        def _():
            l_ref[...] = part

        @pl.when(pl.program_id(0) > 0)
        def _():
            l_ref[...] += part

    return pl.pallas_call(
        body, grid=(T // tm,),
        in_specs=[pl.BlockSpec((tm, Dm), lambda i: (i, 0)), pl.BlockSpec((tm, Dm), lambda i: (i, 0))],
        out_specs=[pl.BlockSpec((8, LANES), lambda i: (0, 0)), pl.BlockSpec((tm, Dm), lambda i: (i, 0))],
        out_shape=[jax.ShapeDtypeStruct((8, LANES), F32), jax.ShapeDtypeStruct((T, Dm), F32)],
        compiler_params=_cparams(1), name=name)(y, target)


def adamw(w, g, m, v, *, br, name):
    R, C = w.shape

    def body(w_ref, g_ref, m_ref, v_ref, d_ref, nm_ref, nv_ref):
        gv = g_ref[...]
        nm = ADAM_B1 * m_ref[...] + (1.0 - ADAM_B1) * gv
        nv = ADAM_B2 * v_ref[...] + (1.0 - ADAM_B2) * (gv * gv)
        m_hat = nm / (1.0 - ADAM_B1 ** ADAM_STEP)
        v_hat = nv / (1.0 - ADAM_B2 ** ADAM_STEP)
        d_ref[...] = -ADAM_LR * (m_hat / (jnp.sqrt(v_hat) + ADAM_EPS) + ADAM_WD * w_ref[...])
        nm_ref[...] = nm
        nv_ref[...] = nv

    spec = pl.BlockSpec((br, C), lambda i: (i, 0))
    shp = jax.ShapeDtypeStruct((R, C), F32)
    return pl.pallas_call(
        body, grid=(R // br,), in_specs=[spec] * 4, out_specs=[spec] * 3, out_shape=[shp] * 3,
        compiler_params=_cparams(1), name=name)(w, g, m, v)


def _lane0():
    return lax.broadcasted_iota(jnp.int32, (1, LANES), 1) < HEAD_DIM


def _half_sum(x, m0):
    s0 = jnp.sum(jnp.where(m0, x, 0.0), axis=-1, keepdims=True)
    s1 = jnp.sum(jnp.where(m0, 0.0, x), axis=-1, keepdims=True)
    return jnp.where(m0, s0, s1)


def _head_rms(x, m0):
    return lax.rsqrt(_half_sum(x * x, m0) * (1.0 / HEAD_DIM) + RMS_EPS)


def _alibi(n):
    return [float(2.0 ** (-8.0 * (h + 1) / n)) for h in range(n)]


def _mask_half(x, m0, e):
    return jnp.where(m0, x, 0.0) if e == 0 else jnp.where(m0, 0.0, x)


def _band_masks2(max_dist, has_prev, live):
    row = lax.broadcasted_iota(jnp.int32, (2 * BLK, 2 * BLK), 0)
    col = lax.broadcasted_iota(jnp.int32, (2 * BLK, 2 * BLK), 1)
    dist = (row & (BLK - 1)) - col + BLK
    lim = jnp.where(live, max_dist, -1)
    first = jnp.where(has_prev, 0, BLK)
    valid = (dist >= 0) & (dist <= lim) & (col >= first)
    top = lax.broadcasted_iota(jnp.int32, (2 * BLK, 1), 0) < BLK
    return dist.astype(F32), valid, top


def _stack_heads(x, m0, kes):
    parts = []
    for e in range(2):
        h = _mask_half(x, m0, e)
        parts.append(pltpu.roll(h, HEAD_DIM, 1) if kes[e] != e else h)
    return jnp.concatenate(parts, axis=0)


def _unstack_heads(y, m0, kes):
    parts = []
    for e in range(2):
        h = y[e * BLK:(e + 1) * BLK]
        parts.append(pltpu.roll(h, HEAD_DIM, 1) if kes[e] != e else h)
    return jnp.where(m0, parts[0], parts[1])


def _rows(r, dil):
    return pl.ds(r, BLK, stride=dil) if dil > 1 else pl.ds(0, BLK)


def _band_units(dil, nsub):
    assert dil == 1 or nsub == 1
    if nsub == 1:
        return [(_rows(r, dil), ("prev", _rows(r, dil)), 0) for r in range(dil)]
    units = [(pl.ds(0, BLK), ("prev", pl.ds(0, BLK)), 0)]
    units += [(pl.ds(BLK * s, BLK), ("cur", pl.ds(BLK * (s - 1), BLK)), s) for s in range(1, nsub)]
    return units


def _head_col_spec(ppk, RB, row_block):
    if ppk == 1:
        return pl.BlockSpec((None, RB, 1), lambda p, i: (p, row_block(i), 0))
    return pl.BlockSpec((ppk, RB, 1), lambda p, i: (p, row_block(i), 0))


def _band_specs(dil, nsub, ppk, q_blk, k_blk, v_blk, kv_shared, nb):
    RB = BLK * dil * nsub
    PB = BLK if nsub > 1 else RB
    qw = LANES * ppk
    kw = LANES if kv_shared else qw

    def cur(i):
        return jnp.minimum(i, nb - 1)

    def prev(i):
        return jnp.maximum(i * nsub - 1, 0) if nsub > 1 else jnp.maximum(i - 1, 0)

    def kidx(base):
        return (lambda p, i: (cur(i), base)) if kv_shared else (lambda p, i: (cur(i), base + p))

    def pidx(base):
        return (lambda p, i: (prev(i), base)) if kv_shared else (lambda p, i: (prev(i), base + p))

    return [pl.BlockSpec((RB, qw), lambda p, i: (cur(i), q_blk + p)),
            pl.BlockSpec((RB, kw), kidx(k_blk)), pl.BlockSpec((PB, kw), pidx(k_blk)),
            pl.BlockSpec((RB, kw), kidx(v_blk)), pl.BlockSpec((PB, kw), pidx(v_blk))]


def qk_norm(qkv, q_gain2, k_gain2, *, width, steps, n_q, tm, name):
    T = qkv.shape[0]
    nsb = width // LANES

    def body(x_ref, qg_ref, kg_ref, o_ref):
        m0 = _lane0()
        for b in range(nsb):
            is_q = ((pl.program_id(1) * nsb + b) < n_q).astype(F32)
            gain = qg_ref[...] * is_q + kg_ref[...] * (1.0 - is_q)
            cols = pl.ds(LANES * b, LANES)
            xv = x_ref[:, cols]
            o_ref[:, cols] = xv * _head_rms(xv, m0) * gain

    gspec = pl.BlockSpec((1, LANES), lambda i, j: (0, 0))
    return pl.pallas_call(
        body, grid=(T // tm, steps),
        in_specs=[pl.BlockSpec((tm, width), lambda i, j: (i, j)), gspec, gspec],
        out_specs=pl.BlockSpec((tm, width), lambda i, j: (i, j)),
        out_shape=jax.ShapeDtypeStruct((T, width * steps), F32),
        compiler_params=_cparams(2), name=name)(qkv, q_gain2, k_gain2)


def banded_fwd(qkn, qkv, slopes, sinks, *, dil, nsub, ppk, q_blk, k_blk, v_blk, n_heads, group,
               max_dist, name):
    T = qkv.shape[0]
    RB = BLK * dil * nsub
    nb = T // RB
    npair = n_heads // 2
    kv_shared = group > 1
    scale = HEAD_DIM ** -0.5
    has_sink = sinks is not None

    def body(*refs):
        slope_ref = refs[0]
        if has_sink:
            sink_ref, refs = refs[1], refs[2:]
        else:
            refs = refs[1:]
        q_ref, kc_ref, kp_ref, vc_ref, vp_ref, o_ref, l_ref, lc0_ref, lc1_ref = refs
        pb = pl.program_id(0)
        i = pl.program_id(1)
        m0 = _lane0()
        distf, valid_first, top = _band_masks2(max_dist, i > 0, i >= 0)
        valid_inner = _band_masks2(max_dist, i >= 0, i >= 0)[1] if nsub > 1 else None
        for rows, (src, prows), sub in _band_units(dil, nsub):
            valid = valid_first if sub == 0 else valid_inner
            kpr, vpr = (kp_ref, vp_ref) if src == "prev" else (kc_ref, vc_ref)
            kcache = {}
            for jp in range(ppk):
                cs = pl.ds(LANES * jp, LANES)
                jk = 0 if kv_shared else jp
                if jk not in kcache:
                    ks = pl.ds(LANES * jk, LANES)
                    kcat = jnp.concatenate([kpr[prows, ks], kc_ref[rows, ks]], axis=0)
                    vcat = jnp.concatenate([vpr[prows, ks], vc_ref[rows, ks]], axis=0)
                    kcache[jk] = (kcat.astype(BF16), vcat.astype(BF16))
                kn, vcat = kcache[jk]
                qn = q_ref[rows, cs]
                kes = [((2 * jp + e) // group) % 2 if kv_shared else e for e in range(2)]
                hidx = 2 * (pb * ppk + jp)
                qs = _stack_heads(qn, m0, kes).astype(BF16)
                slope = jnp.where(top, slope_ref[hidx], slope_ref[hidx + 1])
                s = jnp.where(valid, _dot_nt(qs, kn) * scale - slope * distf, NEG)
                m = jnp.max(s, axis=-1, keepdims=True)
                if has_sink:
                    sk = jnp.where(top, sink_ref[hidx], sink_ref[hidx + 1])
                    m = jnp.maximum(m, sk)
                p = jnp.exp(s - m)
                den = jnp.sum(p, axis=-1, keepdims=True)
                if has_sink:
                    den = den + jnp.exp(sk - m)
                o_full = _dot((p * (1.0 / den)).astype(BF16), vcat)
                o_ref[rows, cs] = _unstack_heads(o_full, m0, kes)
                lse = m + jnp.log(den)
                l_ref[rows, cs] = _unstack_heads(jnp.broadcast_to(lse, (2 * BLK, LANES)), m0, [0, 1])
                for e, lc_ref in enumerate((lc0_ref, lc1_ref)):
                    if ppk == 1:
                        lc_ref[rows, :] = lse[e * BLK:(e + 1) * BLK]
                    else:
                        lc_ref[jp, rows, :] = lse[e * BLK:(e + 1) * BLK]

    smem = pl.BlockSpec(memory_space=pltpu.SMEM)
    qw = LANES * ppk
    ospec = pl.BlockSpec((RB, qw), lambda p, i: (i, p))
    oshape = jax.ShapeDtypeStruct((T, n_heads * HEAD_DIM), F32)
    args = [slopes] + ([sinks] if has_sink else []) + [qkn] * 3 + [qkv] * 2
    return pl.pallas_call(
        body, grid=(npair // ppk, nb),
        in_specs=[smem] * (2 if has_sink else 1) + _band_specs(dil, nsub, ppk, q_blk, k_blk, v_blk, kv_shared, nb),
        out_specs=[ospec, ospec] + [_head_col_spec(ppk, RB, lambda i: i)] * 2,
        out_shape=[oshape, oshape] + [jax.ShapeDtypeStruct((npair, T, 1), F32)] * 2,
        compiler_params=_cparams(2), name=name)(*args)


def banded_bwd(qkn, qkv, slopes, sinks, do, o, lsec, w, omix, *, dil, nsub, ppk, q_blk, k_blk, v_blk,
               n_heads, group, max_dist, do_blk, name):
    T = qkv.shape[0]
    RB = BLK * dil * nsub
    nb = T // RB
    npair = n_heads // 2
    kv_shared = group > 1
    scale = HEAD_DIM ** -0.5
    has_sink = sinks is not None
    mixed = w is not None
    qw = LANES * ppk

    def body(*refs):
        slope_ref = refs[0]
        if has_sink:
            sink_ref, refs = refs[1], refs[2:]
        else:
            refs = refs[1:]
        q_ref, kc_ref, kp_ref, vc_ref, vp_ref, do_ref, lc0_ref, lc1_ref = refs[:8]
        refs = refs[8:]
        if mixed:
            w_ref, om_ref, refs = refs[0], refs[1], refs[2:]
        else:
            o_ref, refs = refs[0], refs[1:]
        dq_ref, dk_ref, dv_ref, dsk_ref, ck_ref, cv_ref = refs
        pb = pl.program_id(0)
        i = pl.program_id(1)
        live = i < nb
        m0 = _lane0()
        lane = lax.broadcasted_iota(jnp.int32, (1, LANES), 1)
        distf, valid_first, top = _band_masks2(max_dist, i > 0, live)
        valid_inner = _band_masks2(max_dist, i >= 0, live)[1] if nsub > 1 else None
        livef = live.astype(F32)

        def half_rows(x):
            s0 = jnp.sum(jnp.where(m0, x, 0.0), axis=-1, keepdims=True)
            s1 = jnp.sum(jnp.where(m0, 0.0, x), axis=-1, keepdims=True)
            return jnp.concatenate([s0, s1], axis=0)

        @pl.when((pb == 0) & (i == 0))
        def _():
            dsk_ref[...] = jnp.zeros_like(dsk_ref)

        @pl.when(i == 0)
        def _():
            ck_ref[...] = jnp.zeros_like(ck_ref)
            cv_ref[...] = jnp.zeros_like(cv_ref)

        dsk_acc = jnp.zeros((1, LANES), F32)
        if nsub > 1:
            dk_ref[...] = ck_ref[...]
            dv_ref[...] = cv_ref[...]
        for rows, (src, prows), sub in _band_units(dil, nsub):
            valid = valid_first if sub == 0 else valid_inner
            kpr, vpr = (kp_ref, vp_ref) if src == "prev" else (kc_ref, vc_ref)
            for jp in range(ppk):
                cs = pl.ds(LANES * jp, LANES)
                ks = pl.ds(0, LANES) if kv_shared else cs
                kn = jnp.concatenate([kpr[prows, ks], kc_ref[rows, ks]], axis=0).astype(BF16)
                vcat = jnp.concatenate([vpr[prows, ks], vc_ref[rows, ks]], axis=0).astype(BF16)
                dov = do_ref[rows, cs]
                if mixed:
                    dov = dov * w_ref[rows, cs]
                    shift = half_rows(dov * om_ref[rows, cs])
                else:
                    shift = half_rows(dov * o_ref[rows, cs])
                kes = [((2 * jp + e) // group) % 2 if kv_shared else e for e in range(2)]
                hidx = 2 * (pb * ppk + jp)
                qs = _stack_heads(q_ref[rows, cs], m0, kes).astype(BF16)
                dos = _stack_heads(dov, m0, kes).astype(BF16)
                lse = jnp.concatenate([ref[rows, :] if ppk == 1 else ref[jp, rows, :]
                                       for ref in (lc0_ref, lc1_ref)], axis=0)
                slope = jnp.where(top, slope_ref[hidx], slope_ref[hidx + 1])
                p = jnp.where(valid, jnp.exp(_dot_nt(qs, kn) * scale - slope * distf - lse), 0.0)
                ds = (p * (_dot_nt(dos, vcat) - shift)).astype(BF16)
                dqn = _unstack_heads(_dot(ds, kn), m0, kes) * scale
                dkn = _dot_tn(ds, qs) * scale
                dvv = _dot_tn(p.astype(BF16), dos)
                if has_sink:
                    sk = jnp.where(top, sink_ref[hidx], sink_ref[hidx + 1])
                    contrib = -jnp.exp(sk - lse) * shift * livef
                    for e in range(2):
                        tot = jnp.sum(contrib[e * BLK:(e + 1) * BLK], axis=0, keepdims=True)
                        dsk_acc = dsk_acc + jnp.where(lane == (2 * jp + e), tot, 0.0)
                dk_raw = dkn

                @pl.when(live)
                def _():
                    dq_ref[rows, cs] = dqn

                if nsub == 1:
                    dk_ref[rows, cs] = ck_ref[rows, cs] + dk_raw[:BLK]
                    dv_ref[rows, cs] = cv_ref[rows, cs] + dvv[:BLK]
                elif sub == 0:
                    last = pl.ds(RB - BLK, BLK)
                    dk_ref[last, cs] += dk_raw[:BLK]
                    dv_ref[last, cs] += dvv[:BLK]
                else:
                    ck_ref[prows, cs] += dk_raw[:BLK]
                    cv_ref[prows, cs] += dvv[:BLK]
                ck_ref[rows, cs] = dk_raw[BLK:]
                cv_ref[rows, cs] = dvv[BLK:]
        dsk_ref[...] += dsk_acc

    smem = pl.BlockSpec(memory_space=pltpu.SMEM)
    gspec = pl.BlockSpec((1, LANES), lambda p, i: (0, 0))

    def cur(i):
        return jnp.minimum(i, nb - 1)

    qspec = pl.BlockSpec((RB, qw), lambda p, i: (cur(i), p))
    dospec = pl.BlockSpec((RB, qw), lambda p, i: (cur(i), do_blk + p))
    kvout = pl.BlockSpec((RB, qw), lambda p, i: (jnp.maximum(i - 1, 0), p))
    in_specs = ([smem] * (2 if has_sink else 1) + _band_specs(dil, nsub, ppk, q_blk, k_blk, v_blk, kv_shared, nb)
                + [dospec] + [_head_col_spec(ppk, RB, cur)] * 2
                + ([qspec, qspec] if mixed else [qspec]))
    args = ([slopes] + ([sinks] if has_sink else []) + [qkn] * 3 + [qkv] * 2 + [do, lsec[0], lsec[1]]
            + ([w, omix] if mixed else [o]))
    full = jax.ShapeDtypeStruct((T, n_heads * HEAD_DIM), F32)
    row = jax.ShapeDtypeStruct((1, LANES), F32)
    return pl.pallas_call(
        body, grid=(npair // ppk, nb + 1), in_specs=in_specs,
        out_specs=[qspec, kvout, kvout, gspec],
        out_shape=[full, full, full, row],
        scratch_shapes=[pltpu.VMEM((RB, qw), F32), pltpu.VMEM((RB, qw), F32)],
        compiler_params=_cparams(2), name=name)(*args)


def mix_fwd(o1, o2, o3, l1, l2, l3, *, tm, name):
    T, C = o1.shape

    def body(o1r, o2r, o3r, l1r, l2r, l3r, o_ref, w1r, w2r, w3r):
        a, b, c = l1r[...], l2r[...], l3r[...]
        m = jnp.maximum(jnp.maximum(a, b), c)
        ea, eb, ec = jnp.exp(a - m), jnp.exp(b - m), jnp.exp(c - m)
        inv = 1.0 / (ea + eb + ec)
        wa, wb, wc = ea * inv, eb * inv, ec * inv
        o_ref[...] = wa * o1r[...] + wb * o2r[...] + wc * o3r[...]
        w1r[...] = wa
        w2r[...] = wb
        w3r[...] = wc

    spec = pl.BlockSpec((tm, C), lambda i: (i, 0))
    shp = jax.ShapeDtypeStruct((T, C), F32)
    return pl.pallas_call(body, grid=(T // tm,), in_specs=[spec] * 6, out_specs=[spec] * 4, out_shape=[shp] * 4,
                          compiler_params=_cparams(1), name=name)(o1, o2, o3, l1, l2, l3)


def _qk_norm_bwd(raw, dn, gain, m0):
    r = _head_rms(raw, m0)
    h = raw * r
    dh = dn * gain
    d_raw = r * (dh - h * (_half_sum(dh * h, m0) * (1.0 / HEAD_DIM)))
    return d_raw, jnp.sum(dn * h, axis=0, keepdims=True)


def _acc_rows(ref, val):
    @pl.when(pl.program_id(0) == 0)
    def _():
        ref[...] = val

    @pl.when(pl.program_id(0) > 0)
    def _():
        ref[...] += val


def assemble_odd(parts, qkv, q_gain2, k_gain2, *, tm, name):
    T, C = parts[0][0].shape
    nbk = C // LANES

    def body(*refs):
        qkv_ref, qg_ref, kg_ref, o_ref, dqg_ref, dkg_ref = refs[9:]
        m0 = _lane0()
        sums = [refs[j][...] + refs[3 + j][...] + refs[6 + j][...] for j in range(3)]
        o_ref[:, pl.ds(2 * C, C)] = sums[2]
        for j, (g_ref, acc_ref) in enumerate(((qg_ref, dqg_ref), (kg_ref, dkg_ref))):
            dgain = jnp.zeros((1, LANES), F32)
            for b in range(nbk):
                cols = pl.ds(C * j + LANES * b, LANES)
                d_raw, part = _qk_norm_bwd(qkv_ref[:, cols], sums[j][:, LANES * b:LANES * (b + 1)], g_ref[...], m0)
                o_ref[:, cols] = d_raw
                dgain = dgain + part
            _acc_rows(acc_ref, dgain)

    spec = pl.BlockSpec((tm, C), lambda i: (i, 0))
    gspec = pl.BlockSpec((1, LANES), lambda i: (0, 0))
    flat = [parts[p][j] for p in range(3) for j in range(3)]
    row = jax.ShapeDtypeStruct((1, LANES), F32)
    return pl.pallas_call(body, grid=(T // tm,),
                          in_specs=[spec] * 9 + [pl.BlockSpec((tm, 2 * C), lambda i: (i, 0)), gspec, gspec],
                          out_specs=[pl.BlockSpec((tm, 3 * C), lambda i: (i, 0)), gspec, gspec],
                          out_shape=[jax.ShapeDtypeStruct((T, 3 * C), F32), row, row],
                          compiler_params=_cparams(1), name=name)(*flat, qkv, q_gain2, k_gain2)


def assemble_even(dqa, dka4, dva4, dqb, dkb, dvb, qkv, q_gain2, k_gain2, *, tm, name):
    T = dqa.shape[0]
    W = 512
    QK = 768

    def body(dqa_r, dka_r, dva_r, dqb_r, dkb_r, dvb_r, qkv_ref, qg_ref, kg_ref, o_ref, dqg_ref, dkg_ref):
        m0 = _lane0()
        ka = dka_r[...]
        va = dva_r[...]
        dqn = dqa_r[...]
        dgain = jnp.zeros((1, LANES), F32)
        for b in range(W // LANES):
            cols = pl.ds(LANES * b, LANES)
            d_raw, part = _qk_norm_bwd(qkv_ref[:, cols], dqn[:, LANES * b:LANES * (b + 1)], qg_ref[...], m0)
            o_ref[:, cols] = d_raw
            dgain = dgain + part
        _acc_rows(dqg_ref, dgain)
        dkn = ka[:, 0:128] + ka[:, 128:256] + ka[:, 256:384] + ka[:, 384:512]
        d_raw, part = _qk_norm_bwd(qkv_ref[:, pl.ds(W, LANES)], dkn, kg_ref[...], m0)
        o_ref[:, pl.ds(W, LANES)] = d_raw
        _acc_rows(dkg_ref, part)
        o_ref[:, pl.ds(640, LANES)] = va[:, 0:128] + va[:, 128:256] + va[:, 256:384] + va[:, 384:512]
        o_ref[:, pl.ds(768, W)] = dqb_r[...]
        o_ref[:, pl.ds(1280, W)] = dkb_r[...]
        o_ref[:, pl.ds(1792, W)] = dvb_r[...]

    spec = pl.BlockSpec((tm, W), lambda i: (i, 0))
    gspec = pl.BlockSpec((1, LANES), lambda i: (0, 0))
    row = jax.ShapeDtypeStruct((1, LANES), F32)
    return pl.pallas_call(body, grid=(T // tm,),
                          in_specs=[spec] * 6 + [pl.BlockSpec((tm, QK), lambda i: (i, 0)), gspec, gspec],
                          out_specs=[pl.BlockSpec((tm, 2304), lambda i: (i, 0)), gspec, gspec],
                          out_shape=[jax.ShapeDtypeStruct((T, 2304), F32), row, row],
                          compiler_params=_cparams(1), name=name)(dqa, dka4, dva4, dqb, dkb, dvb, qkv, q_gain2, k_gain2)


STICK_T = 256
STICK_DEAD = -110.0


def _split_bf16(x):
    hi = x.astype(BF16)
    lo = (x - hi.astype(F32)).astype(BF16)
    return hi, lo


def _stick_logits(qm, kt, scale, diag):
    n = STICK_T
    row = lax.broadcasted_iota(jnp.int32, (n, n), 0)
    col = lax.broadcasted_iota(jnp.int32, (n, n), 1)
    mask = col < row + jnp.where(diag, 0, n)
    z = _dot_nt(qm, kt) * scale
    lneg = -(jnp.maximum(z, 0.0) + jnp.log(1.0 + jnp.exp(-jnp.abs(z))))
    lpos = z + lneg
    lk = jnp.where(mask, lneg, 0.0)
    return mask, lpos, lneg, lk


def _cumsum_mm(x, tri):
    hi, lo = _split_bf16(x)
    return _dot(hi, tri) + _dot(lo, tri)


def stick_fwd(qkv, *, q_blk, k_blk, v_blk, n_pairs, name, riders=None, rider_args=()):
    T = qkv.shape[0]
    n = STICK_T
    nq = T // n
    scale = HEAD_DIM ** -0.5
    nc = riders.n if riders is not None else 0
    n_steps = n_pairs * nq
    stage_at = (0, (3 * n_steps) // 4, n_steps - 1, n_steps - 1)

    def body(*refs):
        q_ref, k_ref, v_ref = refs[:3]
        x_refs, o_ref = refs[3:3 + nc], refs[3 + nc]
        out_refs, sems = refs[4 + nc:4 + 2 * nc], refs[4 + 2 * nc:]
        i = pl.program_id(1)
        step_id = pl.program_id(0) * nq + i

        def ride(which):
            if riders is not None:
                @pl.when(step_id == stage_at[which])
                def _():
                    riders.stage(which, x_refs, out_refs, sems)

        ride(0)
        ride(1)
        m0 = _lane0()
        r2 = lax.broadcasted_iota(jnp.int32, (n, n), 0)
        c2 = lax.broadcasted_iota(jnp.int32, (n, n), 1)
        tri_after = (r2 > c2).astype(BF16)
        qv = q_ref[...]
        out = jnp.zeros((n, LANES), F32)
        for e in range(2):
            qm = _mask_half(qv, m0, e).astype(BF16)

            def alive(st):
                t, _, carry = st
                return (t <= i) & (jnp.max(carry) > STICK_DEAD)

            def step(st, e=e, qm=qm):
                t, acc, carry = st
                start = pl.multiple_of((i - t) * n, n)
                kt = k_ref[pl.ds(start, n), :].astype(BF16)
                vt = _mask_half(v_ref[pl.ds(start, n), :], m0, e).astype(BF16)
                mask, lpos, _, lk = _stick_logits(qm, kt, scale, t == 0)
                after = _cumsum_mm(lk, tri_after) + carry
                a = jnp.where(mask, jnp.exp(lpos + after), 0.0)
                acc = acc + _dot(a.astype(BF16), vt)
                carry = carry + jnp.sum(lk, axis=-1, keepdims=True)
                return t + 1, acc, carry

            _, acc, _ = lax.while_loop(alive, step, (jnp.int32(0), jnp.zeros((n, LANES), F32),
                                                     jnp.zeros((n, 1), F32)))
            out = out + acc
        o_ref[...] = out
        ride(2)
        ride(3)

    outs = pl.pallas_call(
        body, grid=(n_pairs, nq),
        in_specs=[pl.BlockSpec((n, LANES), lambda p, i: (i, q_blk + p)),
                  pl.BlockSpec((T, LANES), lambda p, i: (0, k_blk + p)),
                  pl.BlockSpec((T, LANES), lambda p, i: (0, v_blk + p))] + [_ANY] * nc,
        out_specs=[pl.BlockSpec((n, LANES), lambda p, i: (i, p))] + [_ANY] * nc,
        out_shape=[jax.ShapeDtypeStruct((T, n_pairs * LANES), F32)] + (riders.shapes if nc else []),
        scratch_shapes=riders.sems if nc else [],
        compiler_params=_cparams(2), name=name)(qkv, qkv, qkv, *rider_args)
    return outs[0], list(outs[1:])


def stick_bwd(qkv, do, *, q_blk, k_blk, v_blk, do_blk, n_pairs, name, riders=None, rider_args=()):
    T = qkv.shape[0]
    n = STICK_T
    nq = T // n
    scale = HEAD_DIM ** -0.5
    nc = riders.n if riders is not None else 0

    def body(*refs):
        q_ref, k_ref, v_ref, do_ref = refs[:4]
        pre_refs = refs[4:4 + nc]
        dq_ref, dk_ref, dv_ref = refs[4 + nc:7 + nc]
        land_refs = refs[7 + nc:7 + 2 * nc]
        a_keep, g_keep, s_keep = refs[7 + 2 * nc:10 + 2 * nc]
        sems = refs[10 + 2 * nc:]
        i = pl.program_id(1)
        first_step = (pl.program_id(0) == 0) & (i == 0)
        last_step = (pl.program_id(0) == n_pairs - 1) & (i == nq - 1)
        m0 = _lane0()
        r2 = lax.broadcasted_iota(jnp.int32, (n, n), 0)
        c2 = lax.broadcasted_iota(jnp.int32, (n, n), 1)
        tri_after = (r2 > c2).astype(BF16)
        tri_from = (r2 >= c2).astype(BF16)

        if riders is not None:
            @pl.when(first_step)
            def _():
                riders.start(pre_refs, land_refs, sems)

        @pl.when(i == 0)
        def _():
            dk_ref[...] = jnp.zeros_like(dk_ref)
            dv_ref[...] = jnp.zeros_like(dv_ref)

        qv = q_ref[...]
        dov = do_ref[...]
        dq_out = jnp.zeros((n, LANES), F32)
        for e in range(2):
            qm = _mask_half(qv, m0, e).astype(BF16)
            dom = _mask_half(dov, m0, e).astype(BF16)

            def alive(st):
                t, carry, _ = st
                return (t <= i) & (jnp.max(carry) > STICK_DEAD)

            def scan(st, qm=qm, dom=dom):
                t, carry, gtot = st
                start = pl.multiple_of((i - t) * n, n)
                kt = k_ref[pl.ds(start, n), :].astype(BF16)
                vt = v_ref[pl.ds(start, n), :].astype(BF16)
                mask, lpos, lneg, lk = _stick_logits(qm, kt, scale, t == 0)
                a = jnp.where(mask, jnp.exp(lpos + _cumsum_mm(lk, tri_after) + carry), 0.0)
                g = _dot_nt(dom, vt) * a
                a_keep[t] = a.astype(BF16)
                g_keep[t] = g
                s_keep[t] = jnp.exp(lneg).astype(BF16)
                return (t + 1, carry + jnp.sum(lk, axis=-1, keepdims=True),
                        gtot + jnp.sum(g, axis=-1, keepdims=True))

            z1 = jnp.zeros((n, 1), F32)
            n_live, _, gtot = lax.while_loop(alive, scan, (jnp.int32(0), z1, z1))

            def step(t, st, e=e, qm=qm, dom=dom, gtot=gtot):
                dq_acc, gright = st
                start = pl.multiple_of((i - t) * n, n)
                g = g_keep[t]
                sneg = s_keep[t].astype(F32)
                before = gtot - (_cumsum_mm(g, tri_from) + gright)
                mask = c2 < r2 + jnp.where(t == 0, 0, n)
                dz = jnp.where(mask, g * sneg - before * (1.0 - sneg), 0.0) * scale
                dzb = dz.astype(BF16)
                dq_acc = dq_acc + _dot(dzb, _mask_half(k_ref[pl.ds(start, n), :], m0, e).astype(BF16))
                dk_ref[pl.ds(start, n), :] += _dot_tn(dzb, qm)
                dv_ref[pl.ds(start, n), :] += _dot_tn(a_keep[t], dom)
                return dq_acc, gright + jnp.sum(g, axis=-1, keepdims=True)

            dq_acc, _ = lax.fori_loop(0, n_live, step, (jnp.zeros((n, LANES), F32), z1))
            dq_out = dq_out + dq_acc
        dq_ref[...] = dq_out

        if riders is not None:
            @pl.when(last_step)
            def _():
                riders.finish(pre_refs, land_refs, sems)

    tile = pl.BlockSpec((n, LANES), lambda p, i: (i, p))
    whole = pl.BlockSpec((T, LANES), lambda p, i: (0, p))
    shp = jax.ShapeDtypeStruct((T, n_pairs * LANES), F32)
    outs = pl.pallas_call(
        body, grid=(n_pairs, nq),
        in_specs=[pl.BlockSpec((n, LANES), lambda p, i: (i, q_blk + p)),
                  pl.BlockSpec((T, LANES), lambda p, i: (0, k_blk + p)),
                  pl.BlockSpec((T, LANES), lambda p, i: (0, v_blk + p)),
                  pl.BlockSpec((n, LANES), lambda p, i: (i, do_blk + p))] + [_ANY] * nc,
        out_specs=[tile, whole, whole] + [_ANY] * nc,
        out_shape=[shp, shp, shp] + (riders.shapes if nc else []),
        scratch_shapes=[pltpu.VMEM((nq, n, n), BF16), pltpu.VMEM((nq, n, n), F32), pltpu.VMEM((nq, n, n), BF16)]
        + (riders.sems if nc else []),
        compiler_params=_cparams(2), name=name)(qkv, qkv, qkv, do, *rider_args)
    return outs[0], outs[1], outs[2], list(outs[3:])


def _xnorm(x):
    r = lax.rsqrt(jnp.mean(x * x, axis=-1, keepdims=True) + RMS_EPS)
    return r, x * r


def xattn_fwd(qraw, kvraw, q_gain, k_gain, *, tm, name):
    T = qraw.shape[0]
    scale = X_HEAD_DIM ** -0.5
    W = X_HEADS * X_HEAD_DIM

    def body(q_ref, kv_ref, qg_ref, kg_ref, o_ref):
        for h in range(X_HEADS):
            cs = pl.ds(X_HEAD_DIM * h, X_HEAD_DIM)
            _, qh = _xnorm(q_ref[:, cs])
            _, kh = _xnorm(kv_ref[:, cs])
            qn = (qh * qg_ref[...]).astype(BF16)
            kn = (kh * kg_ref[...]).astype(BF16)
            v = kv_ref[:, pl.ds(W + X_HEAD_DIM * h, X_HEAD_DIM)].astype(BF16)
            s = _dot_nt(qn, kn) * scale
            m = jnp.max(s, axis=-1, keepdims=True)
            p = jnp.exp(s - m)
            p = p / jnp.sum(p, axis=-1, keepdims=True)
            o_ref[:, cs] = _dot(p.astype(BF16), v)

    gspec = pl.BlockSpec((1, X_HEAD_DIM), lambda i: (0, 0))
    return pl.pallas_call(
        body, grid=(T // tm,),
        in_specs=[pl.BlockSpec((tm, W), lambda i: (i, 0)), pl.BlockSpec((MEM_LEN, 2 * W), lambda i: (0, 0)),
                  gspec, gspec],
        out_specs=pl.BlockSpec((tm, W), lambda i: (i, 0)),
        out_shape=jax.ShapeDtypeStruct((T, W), F32),
        compiler_params=_cparams(1), name=name)(qraw, kvraw, q_gain, k_gain)


def xattn_bwd(qraw, kvraw, q_gain, k_gain, do, o, *, tm, name):
    T = qraw.shape[0]
    nt = T // tm
    scale = X_HEAD_DIM ** -0.5
    W = X_HEADS * X_HEAD_DIM

    def body(q_ref, kv_ref, qg_ref, kg_ref, do_ref, o_ref, dq_ref, dkv_ref, dqg_ref, dkg_ref, dkn_ref):
        i = pl.program_id(0)

        @pl.when(i == 0)
        def _():
            dkv_ref[...] = jnp.zeros_like(dkv_ref)
            dkn_ref[...] = jnp.zeros_like(dkn_ref)
            dqg_ref[...] = jnp.zeros_like(dqg_ref)
            dkg_ref[...] = jnp.zeros_like(dkg_ref)

        qg = qg_ref[...]
        kg = kg_ref[...]
        dqg_acc = jnp.zeros((1, X_HEAD_DIM), F32)
        for h in range(X_HEADS):
            cs = pl.ds(X_HEAD_DIM * h, X_HEAD_DIM)
            vs = pl.ds(W + X_HEAD_DIM * h, X_HEAD_DIM)
            rq, qh = _xnorm(q_ref[:, cs])
            _, kh = _xnorm(kv_ref[:, cs])
            qn = (qh * qg).astype(BF16)
            kn = (kh * kg).astype(BF16)
            v = kv_ref[:, vs].astype(BF16)
            s = _dot_nt(qn, kn) * scale
            m = jnp.max(s, axis=-1, keepdims=True)
            p = jnp.exp(s - m)
            p = p / jnp.sum(p, axis=-1, keepdims=True)
            dov = do_ref[:, cs]
            delta = jnp.sum(dov * o_ref[:, cs], axis=-1, keepdims=True)
            dob = dov.astype(BF16)
            ds = (p * (_dot_nt(dob, v) - delta)).astype(BF16)
            dqn = _dot(ds, kn) * scale
            dkn_ref[:, cs] += _dot_tn(ds, qn) * scale
            dkv_ref[:, vs] += _dot_tn(p.astype(BF16), dob)
            dqg_acc = dqg_acc + jnp.sum(dqn * qh, axis=0, keepdims=True)
            dqh = dqn * qg
            dq_ref[:, cs] = rq * (dqh - qh * jnp.mean(dqh * qh, axis=-1, keepdims=True))
        dqg_ref[...] += dqg_acc

        @pl.when(i == nt - 1)
        def _():
            dkg_acc = jnp.zeros((1, X_HEAD_DIM), F32)
            for h in range(X_HEADS):
                cs = pl.ds(X_HEAD_DIM * h, X_HEAD_DIM)
                rk, kh = _xnorm(kv_ref[:, cs])
                dkn = dkn_ref[:, cs]
                dkg_acc = dkg_acc + jnp.sum(dkn * kh, axis=0, keepdims=True)
                dkh = dkn * kg
                dkv_ref[:, cs] = rk * (dkh - kh * jnp.mean(dkh * kh, axis=-1, keepdims=True))
            dkg_ref[...] = dkg_acc

    gspec = pl.BlockSpec((1, X_HEAD_DIM), lambda i: (0, 0))
    tile = pl.BlockSpec((tm, W), lambda i: (i, 0))
    kvspec = pl.BlockSpec((MEM_LEN, 2 * W), lambda i: (0, 0))
    grow = jax.ShapeDtypeStruct((1, X_HEAD_DIM), F32)
    return pl.pallas_call(
        body, grid=(nt,), in_specs=[tile, kvspec, gspec, gspec, tile, tile],
        out_specs=[tile, kvspec, gspec, gspec],
        out_shape=[jax.ShapeDtypeStruct((T, W), F32), jax.ShapeDtypeStruct((MEM_LEN, 2 * W), F32), grow, grow],
        scratch_shapes=[pltpu.VMEM((MEM_LEN, W), F32)],
        compiler_params=_cparams(1), name=name)(qraw, kvraw, q_gain, k_gain, do, o)


_ANY = pl.BlockSpec(memory_space=pl.ANY)


def _my_pos():
    return lax.axis_index("x"), lax.axis_index("y"), lax.axis_index("c")


def _pieces(arrays, chunks):
    out = []
    for a, (arr, n) in enumerate(zip(arrays, chunks)):
        rc = arr.shape[-2] // n
        out += [(a, pl.ds(ch * rc, rc)) for ch in range(n)]
    return out


class GatherBlocks:
    N_STAGES = 4

    def __init__(self, blks, chunks):
        self.shapes = [jax.ShapeDtypeStruct((N_DEV,) + b.shape, b.dtype) for b in blks]
        self.n = len(blks)
        self.pieces = _pieces(blks, chunks)
        n_p = len(self.pieces)
        self.sems = [pltpu.SemaphoreType.DMA((7 * n_p,)), pltpu.SemaphoreType.DMA((7 * n_p,)),
                     pltpu.SemaphoreType.DMA((n_p,))]

    def stage(self, which, x_refs, out_refs, sems):
        send_sems, recv_sems, local_sems = sems
        pieces, n_p = self.pieces, len(self.pieces)
        x, y, c = _my_pos()
        me, sibling = (x, y, c), (x, y, 1 - c)
        chips = [(1 - x, y), (x, 1 - y), (1 - x, 1 - y)]
        xn, yn, dg = [(*chip, c) for chip in chips]
        ps = range(n_p)

        def slot(block, p):
            px, py, pc = block
            a, rows = pieces[p]
            return out_refs[a].at[4 * px + 2 * py + pc, rows]

        def own(p):
            a, rows = pieces[p]
            return x_refs[a].at[rows]

        def copy(k, p, block, to, from_input=False):
            return pltpu.make_async_remote_copy(
                src_ref=own(p) if from_input else slot(block, p), dst_ref=slot(block, p),
                send_sem=send_sems.at[k * n_p + p], recv_sem=recv_sems.at[k * n_p + p],
                device_id=to, device_id_type=MESH)

        mine = [pltpu.make_async_copy(own(p), slot(me, p), local_sems.at[p]) for p in ps]
        first = [copy(k, p, me, to, from_input=True) for p in ps for k, to in ((1, xn), (2, yn), (0, sibling))]
        on_x = [copy(3, p, xn, yn) for p in ps if p % 2 == 0] + [copy(4, p, xn, sibling) for p in ps]
        on_y = [copy(3, p, yn, xn) for p in ps if p % 2 == 1] + [copy(5, p, yn, sibling) for p in ps]
        on_d = [copy(6, p, dg, sibling) for p in ps]
        if which == 0:
            for cp in first + mine:
                cp.start()
        elif which == 1:
            for p in ps:
                copy(1, p, xn, me).wait_recv()
                if p % 2 == 0:
                    copy(3, p, xn, yn).start()
                copy(4, p, xn, sibling).start()
                copy(2, p, yn, me).wait_recv()
                if p % 2 == 1:
                    copy(3, p, yn, xn).start()
                copy(5, p, yn, sibling).start()
        elif which == 2:
            for p in ps:
                copy(3, p, dg, me).wait_recv()
                copy(6, p, dg, sibling).start()
        else:
            for p in ps:
                copy(0, p, sibling, me).wait_recv()
            for k, chip in zip((4, 5, 6), chips):
                for p in ps:
                    copy(k, p, (*chip, 1 - c), me).wait_recv()
            for cp in first + on_x + on_y + on_d:
                cp.wait_send()
            for cp in mine:
                cp.wait()


def gather_blocks(blks, chunks, *, name):
    gb = GatherBlocks(blks, chunks)
    n = gb.n

    def body(*refs):
        x_refs, out_refs, sems = refs[:n], refs[n:2 * n], refs[2 * n:]
        for which in range(gb.N_STAGES):
            gb.stage(which, x_refs, out_refs, sems)

    return pl.pallas_call(body, out_shape=gb.shapes, in_specs=[_ANY] * n, out_specs=[_ANY] * n,
                          scratch_shapes=gb.sems, name=name)(*blks)


def gather_small(small, *, name):
    S, C = small.shape

    def body(s_ref, out_ref, send_sems, recv_sems, local_sem):
        x, y, c = _my_pos()
        my_id = 4 * x + 2 * y + c

        def copy(k, slot):
            px, py, pc = x ^ ((k >> 2) & 1), y ^ ((k >> 1) & 1), c ^ (k & 1)
            dst = my_id if slot == "mine" else 4 * px + 2 * py + pc
            return pltpu.make_async_remote_copy(
                src_ref=s_ref, dst_ref=out_ref.at[dst], send_sem=send_sems.at[k - 1], recv_sem=recv_sems.at[k - 1],
                device_id=(px, py, pc), device_id_type=MESH)

        own = pltpu.make_async_copy(s_ref, out_ref.at[my_id], local_sem)
        own.start()
        sends = [copy(k, "mine") for k in range(1, N_DEV)]
        for cp in sends:
            cp.start()
        for k in range(1, N_DEV):
            copy(k, "theirs").wait_recv()
        for cp in sends:
            cp.wait_send()
        own.wait()

    dma7 = pltpu.SemaphoreType.DMA((7,))
    return pl.pallas_call(
        body, out_shape=jax.ShapeDtypeStruct((N_DEV, S, C), small.dtype), in_specs=[_ANY], out_specs=_ANY,
        scratch_shapes=[dma7, dma7, pltpu.SemaphoreType.DMA], name=name)(small)


class PairExchange:
    def __init__(self, bigs, chunks):
        self.shapes = [jax.ShapeDtypeStruct((4,) + b.shape[1:], b.dtype) for b in bigs]
        self.n = len(bigs)
        self.pieces = _pieces(bigs, chunks)
        n_p = len(self.pieces)
        self.sems = [pltpu.SemaphoreType.DMA((4 * n_p,)), pltpu.SemaphoreType.DMA((4 * n_p,))]

    def _copies(self, big_refs, out_refs, sems):
        send_sems, recv_sems = sems
        n_p = len(self.pieces)
        x, y, c = _my_pos()

        def copy(b, p):
            a, rows = self.pieces[p]
            return pltpu.make_async_remote_copy(
                src_ref=big_refs[a].at[2 * b + (1 - c), rows], dst_ref=out_refs[a].at[b, rows],
                send_sem=send_sems.at[b * n_p + p], recv_sem=recv_sems.at[b * n_p + p],
                device_id=(x, y, 1 - c), device_id_type=MESH)

        return [copy(b, p) for b in range(4) for p in range(n_p)]

    def start(self, big_refs, out_refs, sems):
        for cp in self._copies(big_refs, out_refs, sems):
            cp.start()

    def finish(self, big_refs, out_refs, sems):
        cps = self._copies(big_refs, out_refs, sems)
        for cp in cps:
            cp.wait_recv()
        for cp in cps:
            cp.wait_send()


def _standalone(exchange, args, name):
    n = exchange.n

    def body(*refs):
        exchange.start(refs[:n], refs[n:2 * n], refs[2 * n:])
        exchange.finish(refs[:n], refs[n:2 * n], refs[2 * n:])

    return pl.pallas_call(body, out_shape=exchange.shapes, in_specs=[_ANY] * n, out_specs=[_ANY] * n,
                          scratch_shapes=exchange.sems, name=name)(*args)


class Riding:
    def __init__(self, riders):
        self.riders = [(ex, list(args)) for ex, args in riders]
        self.args = [a for _, args in self.riders for a in args]
        self.in_specs = [_ANY] * len(self.args)
        self.out_shapes = [s for ex, _ in self.riders for s in ex.shapes]
        self.out_specs = [_ANY] * len(self.out_shapes)
        self.scratch = [s for ex, _ in self.riders for s in ex.sems]

    def wrap(self, body, n_in, n_out, n_scratch, is_first, is_last):
        def wrapped(*refs):
            k = 0
            core = list(refs[:n_in])
            k = n_in
            r_in = []
            for ex, _ in self.riders:
                r_in.append(refs[k:k + ex.n])
                k += ex.n
            core += refs[k:k + n_out]
            k += n_out
            r_out = []
            for ex, _ in self.riders:
                r_out.append(refs[k:k + ex.n])
                k += ex.n
            core += refs[k:k + n_scratch]
            k += n_scratch
            r_sem = []
            for ex, _ in self.riders:
                r_sem.append(refs[k:k + len(ex.sems)])
                k += len(ex.sems)

            @pl.when(is_first())
            def _():
                for (ex, _), a, b, s in zip(self.riders, r_in, r_out, r_sem):
                    ex.start(a, b, s)

            body(*core)

            @pl.when(is_last())
            def _():
                for (ex, _), a, b, s in zip(self.riders, r_in, r_out, r_sem):
                    ex.finish(a, b, s)

        return wrapped

    def split(self, outs, n_out):
        core, rest, per = list(outs[:n_out]), list(outs[n_out:]), []
        for ex, _ in self.riders:
            per.append(rest[:ex.n])
            rest = rest[ex.n:]
        return core, per


def pair_sum(big, sib, c, *, tr, name):
    _, R, C = big.shape

    def body(c_ref, a_ref, s_ref, o_ref):
        o_ref[...] = (a_ref[...].astype(F32) + s_ref[...].astype(F32)).astype(o_ref.dtype)

    grid_spec = pltpu.PrefetchScalarGridSpec(
        num_scalar_prefetch=1, grid=(4, R // tr),
        in_specs=[pl.BlockSpec((None, tr, C), lambda b, i, c_ref: (2 * b + c_ref[0], i, 0)),
                  pl.BlockSpec((None, tr, C), lambda b, i, c_ref: (b, i, 0))],
        out_specs=pl.BlockSpec((None, tr, C), lambda b, i, c_ref: (b, i, 0)))
    return pl.pallas_call(body, grid_spec=grid_spec, out_shape=jax.ShapeDtypeStruct((4, R, C), big.dtype),
                          compiler_params=_cparams(2), name=name)(c.reshape(1).astype(jnp.int32), big, sib)


class ChipScatter:
    def __init__(self, pres, chunks):
        self.shapes = [jax.ShapeDtypeStruct(p.shape, p.dtype) for p in pres]
        self.n = len(pres)
        self.pieces = _pieces(pres, chunks)
        n_p = len(self.pieces)
        self.sems = [pltpu.SemaphoreType.DMA((3 * n_p,)), pltpu.SemaphoreType.DMA((3 * n_p,)),
                     pltpu.SemaphoreType.DMA((n_p,))]

    def _copies(self, pre_refs, out_refs, sems):
        send_sems, recv_sems, local_sems = sems
        n_p = len(self.pieces)
        x, y, c = _my_pos()
        my_chip = 2 * x + y
        chips = [(1 - x, y), (x, 1 - y), (1 - x, 1 - y)]

        def copy(j, p, slot):
            px, py = chips[j]
            a, rows = self.pieces[p]
            src_slot, dst_slot = (2 * px + py, my_chip) if slot == "mine" else (my_chip, 2 * px + py)
            return pltpu.make_async_remote_copy(
                src_ref=pre_refs[a].at[src_slot, rows], dst_ref=out_refs[a].at[dst_slot, rows],
                send_sem=send_sems.at[j * n_p + p], recv_sem=recv_sems.at[j * n_p + p],
                device_id=(px, py, c), device_id_type=MESH)

        own = [pltpu.make_async_copy(pre_refs[a].at[my_chip, rows], out_refs[a].at[my_chip, rows], local_sems.at[p])
               for p, (a, rows) in enumerate(self.pieces)]
        sends = [copy(j, p, "mine") for j in range(3) for p in range(n_p)]
        recvs = [copy(j, p, "theirs") for j in range(3) for p in range(n_p)]
        return own, sends, recvs

    def start(self, pre_refs, out_refs, sems):
        own, sends, _ = self._copies(pre_refs, out_refs, sems)
        for cp in sends + own:
            cp.start()

    def finish(self, pre_refs, out_refs, sems):
        own, sends, recvs = self._copies(pre_refs, out_refs, sems)
        for cp in recvs:
            cp.wait_recv()
        for cp in sends:
            cp.wait_send()
        for cp in own:
            cp.wait()


def chip_scatter(pres, chunks, *, name):
    cs = ChipScatter(pres, chunks)
    n = cs.n

    def body(*refs):
        pre_refs, out_refs, sems = refs[:n], refs[n:2 * n], refs[2 * n:]
        cs.start(pre_refs, out_refs, sems)
        cs.finish(pre_refs, out_refs, sems)

    return pl.pallas_call(body, out_shape=cs.shapes, in_specs=[_ANY] * n, out_specs=[_ANY] * n,
                          scratch_shapes=cs.sems, name=name)(*pres)


def sibling_send(blks, chunks, *, name):
    n = len(blks)
    pieces = _pieces(blks, chunks)
    n_p = len(pieces)

    def body(*refs):
        x_refs, out_refs = refs[:n], refs[n:2 * n]
        send_sems, recv_sems = refs[2 * n:]
        x, y, c = _my_pos()
        cps = [pltpu.make_async_remote_copy(
            src_ref=x_refs[a].at[rows], dst_ref=out_refs[a].at[rows], send_sem=send_sems.at[p],
            recv_sem=recv_sems.at[p], device_id=(x, y, 1 - c), device_id_type=MESH)
            for p, (a, rows) in enumerate(pieces)]
        for cp in cps:
            cp.start()
        for cp in cps:
            cp.wait_recv()
        for cp in cps:
            cp.wait_send()

    return pl.pallas_call(
        body, out_shape=[jax.ShapeDtypeStruct(b.shape, b.dtype) for b in blks],
        in_specs=[_ANY] * n, out_specs=[_ANY] * n,
        scratch_shapes=[pltpu.SemaphoreType.DMA((n_p,)), pltpu.SemaphoreType.DMA((n_p,))],
        name=name)(*blks)


def reduce_slots(land, *, tr, name):
    n, R, C = land.shape

    def body(l_ref, o_ref):
        acc = l_ref[0].astype(F32)
        for s in range(1, n):
            acc = acc + l_ref[s].astype(F32)
        o_ref[...] = acc

    return pl.pallas_call(
        body, grid=(R // tr,), in_specs=[pl.BlockSpec((n, tr, C), lambda i: (0, i, 0))],
        out_specs=pl.BlockSpec((tr, C), lambda i: (i, 0)), out_shape=jax.ShapeDtypeStruct((R, C), F32),
        compiler_params=_cparams(1), name=name)(land)


TM = 512


def _tk(d):
    return min(d.shape[0], 1024)


def ffn_fwd_fused(x, g, wgu, wd, *, tm, name):
    T, Dm = x.shape
    nb, _, cb = wgu.shape
    nh = nb // 2
    Fd = nh * cb

    def body(x_ref, g_ref, wgu_ref, wd_ref, o_ref, gu_ref, h_ref):
        xv = x_ref[...]
        r = lax.rsqrt(jnp.mean(xv * xv, axis=-1, keepdims=True) + RMS_EPS)
        hb = (xv * r * g_ref[...]).astype(BF16)
        h_ref[...] = hb
        acc = jnp.zeros((tm, Dm), F32)
        for jj in range(nh):
            cols = pl.ds(cb * jj, cb)
            gate = _dot(hb, wgu_ref[jj]).astype(BF16)
            up = _dot(hb, wgu_ref[nh + jj]).astype(BF16)
            gu_ref[0, :, cols] = gate
            gu_ref[1, :, cols] = up
            gv = gate.astype(F32)
            act = (gv * _sigmoid(gv) * up.astype(F32)).astype(BF16)
            acc = acc + _dot(act, wd_ref[cols, :])
        o_ref[...] = xv + 0.5 * acc

    return pl.pallas_call(
        body, grid=(T // tm,),
        in_specs=[pl.BlockSpec((tm, Dm), lambda i: (i, 0)), pl.BlockSpec((1, Dm), lambda i: (0, 0)),
                  pl.BlockSpec((nb, Dm, cb), lambda i: (0, 0, 0)), pl.BlockSpec((Fd, Dm), lambda i: (0, 0))],
        out_specs=[pl.BlockSpec((tm, Dm), lambda i: (i, 0)), pl.BlockSpec((2, tm, Fd), lambda i: (0, i, 0)),
                   pl.BlockSpec((tm, Dm), lambda i: (i, 0))],
        out_shape=[jax.ShapeDtypeStruct((T, Dm), F32), jax.ShapeDtypeStruct((2, T, Fd), BF16),
                   jax.ShapeDtypeStruct((T, Dm), BF16)],
        compiler_params=_cparams(1), name=name)(x, g, wgu, wd)


def ffn_fwd(x, g, wgu, wd, tag):
    xo, gu, h = ffn_fwd_fused(x, g, wgu, wd, tm=256, name=f"{tag}_fwd")
    return xo, (x, gu, h)


def ffn_bwd(d, saved, g, wgu, wd, tag, ride_bact=None, ride_dwgu=None, before_dx=None):
    x, gu, h = saved
    dgu, dwd, *rode_a = ffn_bwd_act(d, wd, gu, tm=TM, tn=1408, name=f"{tag}_bact", riding=ride_bact)
    dwgu = mm_tn(h, dgu, scale=1.0, a_split=False, b_split=True, tm=TM, tn=1408, tk=_tk(d), out_blocked=True,
                 name=f"{tag}_dwgu", riding=ride_dwgu)
    rode_g = []
    if ride_dwgu is not None:
        dwgu, *rode_g = dwgu
    riding = before_dx(dwgu, dwd) if before_dx is not None else None
    dx, dg, *rode_x = mm_nt_normbwd(dgu, wgu, x, g, d, a_split=True, tm=_tk(d), tk=1408, name=f"{tag}_dx",
                                    riding=riding)
    return dx, dg, dwgu, dwd, (rode_a, rode_g, rode_x)


def _tile2(v):
    return jnp.concatenate([v, v], axis=-1).reshape(1, LANES)


def _fold2(v):
    return v[:, :HEAD_DIM] + v[:, HEAD_DIM:]


EVEN = dict(dil=1, nsub=2, ppk=4, q_blk=0, k_blk=4, v_blk=5, n_heads=A_Q_HEADS, group=A_GROUP, max_dist=A_WINDOW - 1)
STICK = dict(q_blk=6, k_blk=10, v_blk=14, n_pairs=4)


def _odd_cfg(dil):
    return dict(dil=dil, nsub=4 if dil == 1 else 1, ppk=1, q_blk=0, k_blk=8, v_blk=16, n_heads=C_HEADS, group=1,
                max_dist=BLK)


def even_fwd(x, g, win, qg, kg, sinks, wout, tag, riders=None, rider_args=()):
    qkv, h = norm_matmul(x, g, win, tm=_tk(x), tn=1152, split=False, name=f"{tag}_in")
    qg2, kg2 = _tile2(qg), _tile2(kg)
    slopes = jnp.asarray(_alibi(A_Q_HEADS), F32)
    qkn = qk_norm(qkv, qg2, kg2, width=768, steps=1, n_q=4, tm=TM, name=f"{tag}_qkn")
    oa, _, *lse = banded_fwd(qkn, qkv, slopes, sinks, name=f"{tag}_swa", **EVEN)
    ob, rode = stick_fwd(qkv, name=f"{tag}_stick", riders=riders, rider_args=rider_args, **STICK)
    o = jnp.concatenate([oa, ob], axis=1)
    xo = mm_nn(o, wout, res=x, tm=TM, tn=D_MODEL, tk=D_MODEL, name=f"{tag}_out")
    return xo, (x, qkv, qkn, h, oa, lse, o), rode


def even_bwd(d, saved, g, win, qg, kg, sinks, wout, tag, riders=None, rider_args=(), before_dx=None):
    x, qkv, qkn, h, oa, lse, o = saved
    qg2, kg2 = _tile2(qg), _tile2(kg)
    slopes = jnp.asarray(_alibi(A_Q_HEADS), F32)
    dwout = mm_tn(o, d, scale=1.0, a_split=False, b_split=False, tm=D_MODEL, tn=D_MODEL, tk=_tk(d), name=f"{tag}_dwout")
    do = mm_nt(d, wout, tm=TM, tn=D_MODEL, tk=D_MODEL, name=f"{tag}_do")
    dqa, dka4, dva4, dsk = banded_bwd(qkn, qkv, slopes, sinks, do, oa, lse, None, None,
                                      do_blk=0, name=f"{tag}_swa_b", **EVEN)
    dqb, dkb, dvb, rode = stick_bwd(qkv, do, do_blk=4, name=f"{tag}_stick_b", riders=riders, rider_args=rider_args,
                                    **STICK)
    dqkv, dqg, dkg = assemble_even(dqa, dka4, dva4, dqb, dkb, dvb, qkv, qg2, kg2, tm=TM, name=f"{tag}_asm")
    dwin = mm_tn(h, dqkv, scale=1.0, a_split=False, b_split=False, tm=D_MODEL, tn=1152, tk=_tk(d), name=f"{tag}_dwin")
    riding = before_dx(dwin, dwout) if before_dx is not None else None
    dx, dg, *rode_x = mm_nt_normbwd(dqkv, win, x, g, d, a_split=False, tm=TM, tk=1152, name=f"{tag}_dx", riding=riding)
    return dx, dg, dwin, _fold2(dqg), _fold2(dkg), dsk[:, :A_Q_HEADS], dwout, (rode, rode_x)


def odd_fwd(x, g, win, qg, kg, wout, tag):
    qkv, h = norm_matmul(x, g, win, tm=_tk(x), tn=768, split=False, name=f"{tag}_in")
    qg2, kg2 = _tile2(qg), _tile2(kg)
    qkn = qk_norm(qkv, qg2, kg2, width=D_MODEL, steps=2, n_q=8, tm=TM, name=f"{tag}_qkn")
    outs = []
    for p, (window, dil) in enumerate(C_PATTERNS):
        slopes = jnp.asarray(_alibi(C_HEADS), F32) * float(dil)
        outs.append(banded_fwd(qkn, qkv, slopes, None, name=f"{tag}_dil{p}", **_odd_cfg(dil)))
    o, w1, w2, w3 = mix_fwd(outs[0][0], outs[1][0], outs[2][0], outs[0][1], outs[1][1], outs[2][1],
                            tm=TM, name=f"{tag}_mix")
    xo = mm_nn(o, wout, res=x, tm=TM, tn=D_MODEL, tk=D_MODEL, name=f"{tag}_out")
    return xo, (x, qkv, qkn, h, outs, (w1, w2, w3), o)


def odd_bwd(d, saved, g, win, qg, kg, wout, tag):
    x, qkv, qkn, h, outs, ws, o = saved
    qg2, kg2 = _tile2(qg), _tile2(kg)
    dwout = mm_tn(o, d, scale=1.0, a_split=False, b_split=False, tm=D_MODEL, tn=D_MODEL, tk=_tk(d), name=f"{tag}_dwout")
    do = mm_nt(d, wout, tm=TM, tn=D_MODEL, tk=D_MODEL, name=f"{tag}_do")
    parts = []
    for p, (window, dil) in enumerate(C_PATTERNS):
        slopes = jnp.asarray(_alibi(C_HEADS), F32) * float(dil)
        dq, dk, dv, _ = banded_bwd(qkn, qkv, slopes, None, do, None, outs[p][2:], ws[p], o,
                                   do_blk=0, name=f"{tag}_dil{p}_b", **_odd_cfg(dil))
        parts.append((dq, dk, dv))
    dqkv, dqg, dkg = assemble_odd(parts, qkv, qg2, kg2, tm=256, name=f"{tag}_asm")
    dwin = mm_tn(h, dqkv, scale=1.0, a_split=False, b_split=False, tm=TM, tn=768, tk=_tk(d), out_blocked=True,
                 name=f"{tag}_dwin")
    dx, dg = mm_nt_normbwd(dqkv, win, x, g, d, a_split=False, tm=TM, tk=768, name=f"{tag}_dx")
    return dx, dg, dwin, _fold2(dqg), _fold2(dkg), dwout


def xa_fwd(x, mem, g, gm, wq, wkv, qg, kg, wo, tag):
    qraw, h = norm_matmul(x, g, wq, tm=TM, tn=D_MODEL, split=False, name=f"{tag}_q")
    kvraw, hm = norm_matmul(mem, gm, wkv, tm=MEM_LEN, tn=512, split=False, name=f"{tag}_kv")
    o = xattn_fwd(qraw, kvraw, qg, kg, tm=TM, name=f"{tag}_att")
    xo = mm_nn(o, wo, res=x, tm=TM, tn=D_MODEL, tk=D_MODEL, name=f"{tag}_o")
    return xo, (x, qraw, h, kvraw, hm, o)


def xa_bwd(d, saved, mem, g, gm, wq, wkv, qg, kg, wo, tag):
    x, qraw, h, kvraw, hm, o = saved
    dwo = mm_tn(o, d, scale=1.0, a_split=False, b_split=False, tm=D_MODEL, tn=D_MODEL, tk=_tk(d), name=f"{tag}_dwo")
    do = mm_nt(d, wo, tm=TM, tn=D_MODEL, tk=D_MODEL, name=f"{tag}_do")
    dq, dkv, dqg, dkg = xattn_bwd(qraw, kvraw, qg, kg, do, o, tm=TM, name=f"{tag}_att_b")
    dwq = mm_tn(h, dq, scale=1.0, a_split=False, b_split=False, tm=D_MODEL, tn=D_MODEL, tk=_tk(d), name=f"{tag}_dwq")
    dx, dg = mm_nt_normbwd(dq, wq, x, g, d, a_split=False, tm=TM, tk=D_MODEL, name=f"{tag}_dx")
    dwkv = mm_tn(hm, dkv, scale=1.0, a_split=False, b_split=False, tm=TM, tn=512, tk=MEM_LEN, out_blocked=True,
                 name=f"{tag}_dwkv")
    _, dgm = mm_nt_normbwd(dkv, wkv, mem, gm, None, a_split=False, tm=MEM_LEN, tk=512, name=f"{tag}_dmem")
    return dx, dg, dgm, dwq, dwkv, dqg, dkg, dwo


MATS = (("ffn1_w_gu", 1), ("ffn1_w_down", 0), ("ev_w_in", 1), ("ev_w_out", 0), ("od_w_in", 1), ("od_w_out", 0),
        ("xa_w_q", 0), ("xa_w_kv", 1), ("xa_w_o", 0), ("ffn2_w_gu", 1), ("ffn2_w_down", 0))
SMALLS = ("ffn1_norm", "mix_norm", "ev_q_gain", "ev_k_gain", "ev_sinks", "od_q_gain", "od_k_gain", "xa_norm",
          "xa_mem_norm", "xa_q_gain", "xa_k_gain", "ffn2_norm")
WEIGHTS = ("ffn1_norm", "ffn1_w_gu", "ffn1_w_down", "mix_norm", "ev_w_in", "ev_q_gain", "ev_k_gain", "ev_sinks",
           "ev_w_out", "od_w_in", "od_q_gain", "od_k_gain", "od_w_out", "xa_norm", "xa_mem_norm", "xa_w_q",
           "xa_w_kv", "xa_q_gain", "xa_k_gain", "xa_w_o", "ffn2_norm", "ffn2_w_gu", "ffn2_w_down")
SMALL_ROWS = 16
LAYER_GROUPS = (
    (((("ffn1_w_gu", 0), ("ffn2_w_gu", 0)), 4, 512),
     ((("ffn1_w_down", 0), ("ffn2_w_down", 0)), 2, 352),
     ((("ev_w_out", 0), ("xa_w_q", 0), ("xa_w_o", 0)), 1, 384),
     ((("xa_w_kv", 0),), 1, 512),
     ((("ev_w_in", 0),), 1, 512)),
    (((("ffn1_w_gu", 1), ("ffn2_w_gu", 1)), 4, 512),
     ((("ffn1_w_down", 1), ("ffn2_w_down", 1)), 2, 352),
     ((("od_w_out", 0), ("xa_w_q", 1), ("xa_w_o", 1)), 1, 384),
     ((("xa_w_kv", 1),), 1, 512),
     ((("od_w_in", 0),), 1, 512)),
)
GROUPS = LAYER_GROUPS[0] + LAYER_GROUPS[1]
ROUNDS = {
    "1": LAYER_GROUPS[1],
    "0a": (((("ffn2_w_gu", 0),), 2, 512), ((("ffn2_w_down", 0),), 1, 352)) + LAYER_GROUPS[0][2:],
    "0b": (((("ffn1_w_gu", 0),), 2, 512), ((("ffn1_w_down", 0),), 1, 352)),
}


def _chunks_of(groups):
    return tuple(g[1] for g in groups)
COL_SHARDED = {name for name, axis in MATS if axis == 1}
BLOCKED = {"ffn1_w_gu", "ffn2_w_gu", "xa_w_kv", "od_w_in"}


def group_halves(shards, c, groups):
    out = []
    for members, _, _ in groups:
        halves = []
        for name, layer in members:
            _, r, cc = shards[name].shape
            half = lax.dynamic_index_in_dim(shards[name][layer].reshape(2, r // 2, cc), c, 0, keepdims=False)
            halves.append(half.astype(BF16))
        out.append(jnp.concatenate(halves, axis=0))
    return out


def full_weights(gathered, shards, groups):
    full = {}
    for (members, _, _), arr in zip(groups, gathered):
        for w, (name, layer) in enumerate(members):
            _, r, cc = shards[name].shape
            piece = arr[:, w * (r // 2):(w + 1) * (r // 2)].reshape(4, r, cc)
            if name not in COL_SHARDED:
                piece = piece.reshape(4 * r, cc)
            elif name not in BLOCKED:
                piece = piece.transpose(1, 0, 2).reshape(r, 4 * cc)
            full[(name, layer)] = piece
    return full


def group_grads(grads, shards, groups):
    out = []
    for members, _, _ in groups:
        parts = []
        for name, layer in members:
            _, r, cc = shards[name].shape
            gfull = grads[(name, layer)]
            if name in COL_SHARDED and name not in BLOCKED:
                gfull = gfull.reshape(2, r // 2, 4, cc).transpose(2, 0, 1, 3)
            parts.append(gfull.reshape(N_DEV, r // 2, cc))
        out.append(jnp.concatenate(parts, axis=1))
    return out


def shard_grads(mine, theirs, c, shards, groups):
    per = {}
    for (members, _, _), a, b in zip(groups, mine, theirs):
        for w, (name, layer) in enumerate(members):
            _, r, cc = shards[name].shape
            rows = slice(w * (r // 2), (w + 1) * (r // 2))
            lo = jnp.where(c == 0, a[rows], b[rows])
            hi = jnp.where(c == 0, b[rows], a[rows])
            per[(name, layer)] = jnp.concatenate([lo, hi], axis=0)
    return per


def pack_small(vals):
    row10 = jnp.concatenate([vals["xa_q_gain"].reshape(1, 512), vals["xa_k_gain"].reshape(1, 512)], axis=1)
    row11 = jnp.concatenate([vals["ev_q_gain"], vals["ev_k_gain"], vals["od_q_gain"], vals["od_k_gain"],
                             vals["ev_sinks"], jnp.zeros((1, 1024 - 4 * 64 - 8), F32)], axis=1)
    return jnp.concatenate([vals["ffn1_norm"], vals["mix_norm"], vals["xa_norm"], vals["xa_mem_norm"],
                            vals["ffn2_norm"], row10, row11, jnp.zeros((SMALL_ROWS - 12, 1024), F32)], axis=0)


def unpack_small(arr):
    return {"ffn1_norm": arr[0:2], "mix_norm": arr[2:4], "xa_norm": arr[4:6], "xa_mem_norm": arr[6:8],
            "ffn2_norm": arr[8:10],
            "xa_q_gain": arr[10:11, 0:512].reshape(2, 256), "xa_k_gain": arr[10:11, 512:1024].reshape(2, 256),
            "ev_q_gain": arr[11:12, 0:64], "ev_k_gain": arr[11:12, 64:128], "od_q_gain": arr[11:12, 128:192],
            "od_k_gain": arr[11:12, 192:256], "ev_sinks": arr[11:12, 256:264]}


def local_step(x, mem, target, W, small, prereduce, later):
    depth = small["ffn1_norm"].shape[0]

    def row(name, l):
        return small[name][l:l + 1]

    saved = []
    for l in range(depth):
        j = l // 2
        x, s1 = ffn_fwd(x, row("ffn1_norm", l), W[("ffn1_w_gu", l)], W[("ffn1_w_down", l)], f"l{l}_f1")
        if l % 2 == 0:
            riders, rider_args = None, ()
            if later is not None and l == 0:
                riders, rider_args = GatherBlocks(later[0], later[1]), later[0]
            x, s2, rode = even_fwd(x, row("mix_norm", l), W[("ev_w_in", j)], row("ev_q_gain", j),
                                   row("ev_k_gain", j), small["ev_sinks"][j], W[("ev_w_out", j)], f"l{l}_ev",
                                   riders=riders, rider_args=rider_args)
            if riders is not None:
                W = {**W, **later[2](rode)}
        else:
            x, s2 = odd_fwd(x, row("mix_norm", l), W[("od_w_in", j)], row("od_q_gain", j), row("od_k_gain", j),
                            W[("od_w_out", j)], f"l{l}_od")
        x, s3 = xa_fwd(x, mem, row("xa_norm", l), row("xa_mem_norm", l), W[("xa_w_q", l)], W[("xa_w_kv", l)],
                       row("xa_q_gain", l), row("xa_k_gain", l), W[("xa_w_o", l)], f"l{l}_xa")
        x, s4 = ffn_fwd(x, row("ffn2_norm", l), W[("ffn2_w_gu", l)], W[("ffn2_w_down", l)], f"l{l}_f2")
        saved.append((s1, s2, s3, s4))
    loss, d = loss_kernel(x, target, tm=TM, name="loss")

    gw = {}
    gs = {name: [None] * small[name].shape[0] for name in SMALLS}
    pending, landed = None, {}
    for l in reversed(range(depth)):
        j = l // 2
        s1, s2, s3, s4 = saved[l]
        d, dg, dwgu, dwd, _ = ffn_bwd(d, s4, row("ffn2_norm", l), W[("ffn2_w_gu", l)], W[("ffn2_w_down", l)],
                                      f"l{l}_f2")
        gs["ffn2_norm"][l] = dg
        gw[("ffn2_w_gu", l)], gw[("ffn2_w_down", l)] = dwgu, dwd
        d, dg, dgm, dwq, dwkv, dqg, dkg, dwo = xa_bwd(
            d, s3, mem, row("xa_norm", l), row("xa_mem_norm", l), W[("xa_w_q", l)], W[("xa_w_kv", l)],
            row("xa_q_gain", l), row("xa_k_gain", l), W[("xa_w_o", l)], f"l{l}_xa")
        gs["xa_norm"][l], gs["xa_mem_norm"][l], gs["xa_q_gain"][l], gs["xa_k_gain"][l] = dg, dgm, dqg, dkg
        gw[("xa_w_q", l)], gw[("xa_w_kv", l)], gw[("xa_w_o", l)] = dwq, dwkv, dwo
        split = l == 0 and l % 2 == 0 and ("0a" in ROUNDS)
        early = []
        pre_early = None
        if l % 2 == 0:
            riders, rider_args = None, ()
            if pending is not None:
                riders, rider_args = ChipScatter(pending[1], _chunks_of(ROUNDS[pending[0]])), pending[1]

            def before_mixer_dx(dwin, dwout, j=j, early=early):
                gw[("ev_w_in", j)], gw[("ev_w_out", j)] = dwin, dwout
                early += prereduce.pack(gw, "0a")
                return Riding([(PairExchange(early, _chunks_of(ROUNDS["0a"])), early)])

            d, dg, dwin, dqg, dkg, dsk, dwout, (rode, rode_x) = even_bwd(
                d, s2, row("mix_norm", l), W[("ev_w_in", j)], row("ev_q_gain", j), row("ev_k_gain", j),
                small["ev_sinks"][j], W[("ev_w_out", j)], f"l{l}_ev", riders=riders, rider_args=rider_args,
                before_dx=before_mixer_dx if split else None)
            if pending is not None:
                landed[pending[0]], pending = rode, None
            gs["ev_q_gain"][j], gs["ev_k_gain"][j], gs["ev_sinks"][j] = dqg, dkg, dsk
            gw[("ev_w_in", j)], gw[("ev_w_out", j)] = dwin, dwout
            if split:
                pre_early = prereduce.sums(early, rode_x[0], "0a")
        else:
            d, dg, dwin, dqg, dkg, dwout = odd_bwd(
                d, s2, row("mix_norm", l), W[("od_w_in", j)], row("od_q_gain", j), row("od_k_gain", j),
                W[("od_w_out", j)], f"l{l}_od")
            gs["od_q_gain"][j], gs["od_k_gain"][j] = dqg, dkg
            gw[("od_w_in", j)], gw[("od_w_out", j)] = dwin, dwout
        gs["mix_norm"][l] = dg
        packed = []

        rnd = "0b" if pre_early is not None else str(l)

        def before_dx(dwgu, dwd, l=l, packed=packed, rnd=rnd):
            gw[("ffn1_w_gu", l)], gw[("ffn1_w_down", l)] = dwgu, dwd
            packed += prereduce.pack(gw, rnd)
            return Riding([(PairExchange(packed, _chunks_of(ROUNDS[rnd])), packed)])

        ride_bact = ride_dwgu = None
        if pre_early is not None:
            chunks = _chunks_of(ROUNDS["0a"])
            ride_bact = Riding([(ChipScatter(pre_early[:2], chunks[:2]), pre_early[:2])])
            ride_dwgu = Riding([(ChipScatter(pre_early[2:], chunks[2:]), pre_early[2:])])
        d, dg, dwgu, dwd, (rode_a, rode_g, rode_x) = ffn_bwd(
            d, s1, row("ffn1_norm", l), W[("ffn1_w_gu", l)], W[("ffn1_w_down", l)], f"l{l}_f1",
            ride_bact=ride_bact, ride_dwgu=ride_dwgu, before_dx=before_dx)
        gs["ffn1_norm"][l] = dg
        if pre_early is not None:
            landed["0a"] = list(rode_a[0]) + list(rode_g[0])
        if pending is not None:
            landed[pending[0]] = chip_scatter(pending[1], _chunks_of(ROUNDS[pending[0]]),
                                              name=f"scatter_grads{pending[0]}")
        pending = (rnd, prereduce.sums(packed, rode_x[0], rnd))
    landed[pending[0]] = chip_scatter(pending[1], _chunks_of(ROUNDS[pending[0]]), name=f"scatter_grads{pending[0]}")
    gsmall = {name: jnp.concatenate(v, axis=0) for name, v in gs.items()}
    return loss, d, landed, gsmall


def kernel(x, mem, ffn1_norm, ffn1_w_gu, ffn1_w_down, mix_norm, ev_w_in, ev_q_gain, ev_k_gain, ev_sinks, ev_w_out, od_w_in, od_q_gain, od_k_gain, od_w_out, xa_norm, xa_mem_norm, xa_w_q, xa_w_kv, xa_q_gain, xa_k_gain, xa_w_o, ffn2_norm, ffn2_w_gu, ffn2_w_down, loss_target, m_ffn1_norm, m_ffn1_w_gu, m_ffn1_w_down, m_mix_norm, m_ev_w_in, m_ev_q_gain, m_ev_k_gain, m_ev_sinks, m_ev_w_out, m_od_w_in, m_od_q_gain, m_od_k_gain, m_od_w_out, m_xa_norm, m_xa_mem_norm, m_xa_w_q, m_xa_w_kv, m_xa_q_gain, m_xa_k_gain, m_xa_w_o, m_ffn2_norm, m_ffn2_w_gu, m_ffn2_w_down, v_ffn1_norm, v_ffn1_w_gu, v_ffn1_w_down, v_mix_norm, v_ev_w_in, v_ev_q_gain, v_ev_k_gain, v_ev_sinks, v_ev_w_out, v_od_w_in, v_od_q_gain, v_od_k_gain, v_od_w_out, v_xa_norm, v_xa_mem_norm, v_xa_w_q, v_xa_w_kv, v_xa_q_gain, v_xa_k_gain, v_xa_w_o, v_ffn2_norm, v_ffn2_w_gu, v_ffn2_w_down):
    given = dict(locals())
    w = {n: given[n] for n in WEIGHTS}
    m = {n: given["m_" + n] for n in WEIGHTS}
    v = {n: given["v_" + n] for n in WEIGHTS}
    c = lax.axis_index("c")
    shards = {name: w[name] for name, _ in MATS}
    small = {n: w[n] for n in SMALLS}

    groups0, groups1 = LAYER_GROUPS
    gathered = gather_blocks(group_halves(shards, c, groups0), _chunks_of(groups0), name="gather_weights0")
    full = full_weights(gathered, shards, groups0)
    later = (group_halves(shards, c, groups1), _chunks_of(groups1), lambda got: full_weights(got, shards, groups1))

    class prereduce:
        @staticmethod
        def pack(gw, rnd):
            return group_grads(gw, shards, ROUNDS[rnd])

        @staticmethod
        def sums(packed, sib, rnd):
            return [pair_sum(p, s, c, tr=g[2], name=f"pair_sum{rnd}_{i}")
                    for i, (g, p, s) in enumerate(zip(ROUNDS[rnd], packed, sib))]

    loss_b, grad_x, landed, gsmall = local_step(x[0], mem[0], loss_target[0], full, small, prereduce, later)

    per = {}
    for rnd, land in sorted(landed.items()):
        groups = ROUNDS[rnd]
        mine = [reduce_slots(a, tr=g[2], name=f"sum_grads{rnd}_{i}") for i, (g, a) in enumerate(zip(groups, land))]
        theirs = sibling_send(mine, _chunks_of(groups), name=f"swap_grads{rnd}")
        per.update(shard_grads(mine, theirs, c, shards, groups))
    g = {name: jnp.stack([per[(name, layer)] for layer in range(w[name].shape[0])], axis=0) for name, _ in MATS}
    land_small = gather_small(pack_small(gsmall), name="gather_small")
    g_small = unpack_small(reduce_slots(land_small, tr=SMALL_ROWS, name="sum_small"))
    g.update(g_small)

    delta, new_m, new_v = {}, {}, {}
    for name, _ in MATS:
        shp = w[name].shape
        flat = [a.reshape(-1, shp[-1]) for a in (w[name], g[name], m[name], v[name])]
        dl, nm, nv = adamw(*flat, br=BLK, name=f"adamw_{name}")
        delta[name], new_m[name], new_v[name] = dl.reshape(shp), nm.reshape(shp), nv.reshape(shp)
    dl, nm, nv = adamw(pack_small(small), pack_small(g_small), pack_small({n: m[n] for n in SMALLS}),
                       pack_small({n: v[n] for n in SMALLS}), br=SMALL_ROWS, name="adamw_small")
    for dst, arr in ((delta, dl), (new_m, nm), (new_v, nv)):
        dst.update(unpack_small(arr))

    loss = lax.psum(loss_b[0, 0], ("x", "y", "c"))
    return (loss, grad_x[None], *[g[n] for n in WEIGHTS], *[delta[n] for n in WEIGHTS],
            *[new_m[n] for n in WEIGHTS], *[new_v[n] for n in WEIGHTS])
```

```python
import jax
import jax.numpy as jnp
from jax import lax
from jax.experimental import pallas as pl
from jax.experimental.pallas import tpu as pltpu

F32 = jnp.float32
BF16 = jnp.bfloat16

D_MODEL = 1024
HEAD_DIM = 64
LANES = 128
BLK = 128
D_FF = 2816
RMS_EPS = 1e-6
MEM_LEN = 256
X_HEADS = 4
X_HEAD_DIM = 256
A_Q_HEADS = 8
A_GROUP = 4
A_WINDOW = 128
C_HEADS = 16
C_PATTERNS = ((128, 1), (512, 4), (2048, 16))
NEG = -1e30
VMEM_LIMIT = 56 * 2 ** 20

ADAM_LR = 0.001
ADAM_B1 = 0.9
ADAM_B2 = 0.999
ADAM_EPS = 1e-08
ADAM_WD = 0.01
ADAM_STEP = 10

N_DEV = 8
MESH = pl.DeviceIdType.MESH


def _cparams(n):
    return pltpu.CompilerParams(dimension_semantics=("arbitrary",) * n, vmem_limit_bytes=VMEM_LIMIT)


def _dot(a, b):
    return jnp.dot(a, b, preferred_element_type=F32)


def _dot_nt(a, b):
    return lax.dot_general(a, b, (((1,), (1,)), ((), ())), preferred_element_type=F32)


def _dot_tn(a, b):
    return lax.dot_general(a, b, (((0,), (0,)), ((), ())), preferred_element_type=F32)


def _sigmoid(z):
    return 1.0 / (1.0 + jnp.exp(-z))


def norm_matmul(x, g, w, *, tm, tn, split, name):
    T, K = x.shape
    blocked = w.ndim == 3
    assert not blocked or w.shape[2] == tn
    N = w.shape[0] * w.shape[2] if blocked else w.shape[1]
    nj = N // tn

    def body(x_ref, g_ref, w_ref, o_ref, h_ref):
        @pl.when(pl.program_id(1) == 0)
        def _():
            xv = x_ref[...]
            r = lax.rsqrt(jnp.mean(xv * xv, axis=-1, keepdims=True) + RMS_EPS)
            h_ref[...] = (xv * r * g_ref[...]).astype(BF16)

        o_ref[...] = _dot(h_ref[...], w_ref[...]).astype(o_ref.dtype)

    if split:
        njh = nj // 2
        o_shape = jax.ShapeDtypeStruct((2, T, N // 2), BF16)
        o_spec = pl.BlockSpec((None, tm, tn), lambda i, j: (j // njh, i, j % njh))
    else:
        o_shape = jax.ShapeDtypeStruct((T, N), F32)
        o_spec = pl.BlockSpec((tm, tn), lambda i, j: (i, j))
    return pl.pallas_call(
        body, grid=(T // tm, nj),
        in_specs=[pl.BlockSpec((tm, K), lambda i, j: (i, 0)),
                  pl.BlockSpec((1, K), lambda i, j: (0, 0)),
                  (pl.BlockSpec((None, K, tn), lambda i, j: (j, 0, 0)) if blocked
                   else pl.BlockSpec((K, tn), lambda i, j: (0, j)))],
        out_specs=[o_spec, pl.BlockSpec((tm, K), lambda i, j: (i, 0))],
        out_shape=[o_shape, jax.ShapeDtypeStruct((T, K), BF16)],
        compiler_params=_cparams(2), name=name)(x, g, w)


def mm_nn(a, b, *, res, tm, tn, tk, name):
    T = a.shape[0]
    K, N = b.shape
    nk = K // tk

    def body(a_ref, b_ref, r_ref, o_ref, acc):
        k = pl.program_id(2)

        @pl.when(k == 0)
        def _():
            acc[...] = jnp.zeros_like(acc)

        acc[...] += _dot(a_ref[...].astype(BF16), b_ref[...])

        @pl.when(k == nk - 1)
        def _():
            o_ref[...] = r_ref[...] + acc[...]

    return pl.pallas_call(
        body, grid=(T // tm, N // tn, nk),
        in_specs=[pl.BlockSpec((tm, tk), lambda i, j, k: (i, k)), pl.BlockSpec((tk, tn), lambda i, j, k: (k, j)),
                  pl.BlockSpec((tm, tn), lambda i, j, k: (i, j))],
        out_specs=pl.BlockSpec((tm, tn), lambda i, j, k: (i, j)),
        out_shape=jax.ShapeDtypeStruct((T, N), F32),
        scratch_shapes=[pltpu.VMEM((tm, tn), F32)],
        compiler_params=_cparams(3), name=name)(a, b, res)


def mm_nt(a, b, *, tm, tn, tk, name):
    T, K = a.shape
    N = b.shape[0]
    nk = K // tk

    def body(a_ref, b_ref, o_ref, acc):
        k = pl.program_id(2)

        @pl.when(k == 0)
        def _():
            acc[...] = jnp.zeros_like(acc)

        acc[...] += _dot_nt(a_ref[...].astype(BF16), b_ref[...])

        @pl.when(k == nk - 1)
        def _():
            o_ref[...] = acc[...]

    return pl.pallas_call(
        body, grid=(T // tm, N // tn, nk),
        in_specs=[pl.BlockSpec((tm, tk), lambda i, j, k: (i, k)),
                  pl.BlockSpec((tn, tk), lambda i, j, k: (j, k))],
        out_specs=pl.BlockSpec((tm, tn), lambda i, j, k: (i, j)),
        out_shape=jax.ShapeDtypeStruct((T, N), F32),
        scratch_shapes=[pltpu.VMEM((tm, tn), F32)],
        compiler_params=_cparams(3), name=name)(a, b)


def ffn_bwd_act(d, wd, gu, *, tm, tn, name, riding=None):
    T, K = d.shape
    Fd = wd.shape[0]
    ni = T // tm

    def body(d_ref, w_ref, g_ref, u_ref, dgu_ref, dwd_ref, acc):
        i = pl.program_id(1)

        @pl.when(i == 0)
        def _():
            acc[...] = jnp.zeros_like(acc)

        db = d_ref[...].astype(BF16)
        da = 0.5 * _dot_nt(db, w_ref[...])
        gv = g_ref[...].astype(F32)
        uv = u_ref[...].astype(F32)
        s = _sigmoid(gv)
        silu = gv * s
        acc[...] += _dot_tn((silu * uv).astype(BF16), db)
        dgu_ref[0] = (da * uv * (s * (1.0 + gv * (1.0 - s)))).astype(BF16)
        dgu_ref[1] = (da * silu).astype(BF16)

        @pl.when(i == ni - 1)
        def _():
            dwd_ref[...] = (0.5 * acc[...]).astype(BF16)

    in_specs = [pl.BlockSpec((tm, K), lambda j, i: (i, 0)),
                pl.BlockSpec((tn, K), lambda j, i: (j, 0)),
                pl.BlockSpec((None, tm, tn), lambda j, i: (0, i, j)),
                pl.BlockSpec((None, tm, tn), lambda j, i: (1, i, j))]
    out_specs = [pl.BlockSpec((2, tm, tn), lambda j, i: (0, i, j)), pl.BlockSpec((tn, K), lambda j, i: (j, 0))]
    out_shape = [jax.ShapeDtypeStruct((2, T, Fd), BF16), jax.ShapeDtypeStruct((Fd, K), BF16)]
    return _call_with_riders(body, riding, (Fd // tn, ni), in_specs, out_specs, out_shape,
                             [pltpu.VMEM((tn, K), F32)], [d, wd, gu, gu], name)


def _call_with_riders(body, riding, grid, in_specs, out_specs, out_shape, scratch, args, name):
    n_out = len(out_shape)
    if riding is None:
        return pl.pallas_call(body, grid=grid, in_specs=in_specs, out_specs=out_specs, out_shape=out_shape,
                              scratch_shapes=scratch, compiler_params=_cparams(len(grid)), name=name)(*args)

    def is_first():
        ok = pl.program_id(0) == 0
        for ax in range(1, len(grid)):
            ok = ok & (pl.program_id(ax) == 0)
        return ok

    def is_last():
        ok = pl.program_id(0) == grid[0] - 1
        for ax in range(1, len(grid)):
            ok = ok & (pl.program_id(ax) == grid[ax] - 1)
        return ok

    outs = pl.pallas_call(
        riding.wrap(body, len(in_specs), n_out, len(scratch), is_first, is_last), grid=grid,
        in_specs=list(in_specs) + riding.in_specs, out_specs=list(out_specs) + riding.out_specs,
        out_shape=list(out_shape) + riding.out_shapes, scratch_shapes=list(scratch) + riding.scratch,
        compiler_params=_cparams(len(grid)), name=name)(*args, *riding.args)
    core, per = riding.split(outs, n_out)
    return (*core, *per)


def mm_nt_normbwd(a, b, x, g, res, *, a_split, tm, tk, name, riding=None):
    T, Dm = x.shape
    blocked = b.ndim == 3
    assert not blocked or b.shape[2] == tk
    K = b.shape[0] * b.shape[2] if blocked else b.shape[1]
    nk = K // tk
    nkh = nk // 2
    has_res = res is not None

    def body(*refs):
        if has_res:
            a_ref, b_ref, x_ref, g_ref, r_ref, dx_ref, dg_ref, acc = refs
        else:
            a_ref, b_ref, x_ref, g_ref, dx_ref, dg_ref, acc = refs
        i = pl.program_id(0)
        k = pl.program_id(1)

        @pl.when(k == 0)
        def _():
            acc[...] = jnp.zeros_like(acc)

        acc[...] += _dot_nt(a_ref[...].astype(BF16), b_ref[...])

        @pl.when(k == nk - 1)
        def _():
            xv = x_ref[...]
            r = lax.rsqrt(jnp.mean(xv * xv, axis=-1, keepdims=True) + RMS_EPS)
            xh = xv * r
            dh = acc[...]
            dxh = dh * g_ref[...]
            dx = r * (dxh - xh * jnp.mean(dxh * xh, axis=-1, keepdims=True))
            if has_res:
                dx = dx + r_ref[...]
            dx_ref[...] = dx
            part = jnp.sum(dh * xh, axis=0, keepdims=True)

            @pl.when(i == 0)
            def _():
                dg_ref[...] = part

            @pl.when(i > 0)
            def _():
                dg_ref[...] += part

    if a_split:
        a_spec = pl.BlockSpec((None, tm, tk), lambda i, k: (k // nkh, i, k % nkh))
    else:
        a_spec = pl.BlockSpec((tm, tk), lambda i, k: (i, k))
    in_specs = [a_spec,
                (pl.BlockSpec((None, Dm, tk), lambda i, k: (k, 0, 0)) if blocked
                 else pl.BlockSpec((Dm, tk), lambda i, k: (0, k))),
                pl.BlockSpec((tm, Dm), lambda i, k: (i, 0)),
                pl.BlockSpec((1, Dm), lambda i, k: (0, 0))]
    args = [a, b, x, g]
    if has_res:
        in_specs.append(pl.BlockSpec((tm, Dm), lambda i, k: (i, 0)))
        args.append(res)
    out_specs = [pl.BlockSpec((tm, Dm), lambda i, k: (i, 0)), pl.BlockSpec((1, Dm), lambda i, k: (0, 0))]
    out_shape = [jax.ShapeDtypeStruct((T, Dm), F32), jax.ShapeDtypeStruct((1, Dm), F32)]
    scratch = [pltpu.VMEM((tm, Dm), F32)]
    return _call_with_riders(body, riding, (T // tm, nk), in_specs, out_specs, out_shape, scratch, args, name)


def mm_tn(a, b, *, scale, a_split, b_split, tm, tn, tk, name, out_blocked=False, riding=None):
    T = a.shape[-2]
    M = a.shape[-1] * (2 if a_split else 1)
    N = b.shape[-1] * (2 if b_split else 1)
    ni, nj, nk = M // tm, N // tn, T // tk
    nih, njh = ni // 2, nj // 2

    def body(a_ref, b_ref, o_ref, acc):
        k = pl.program_id(2)

        @pl.when(k == 0)
        def _():
            acc[...] = jnp.zeros_like(acc)

        acc[...] += _dot_tn(a_ref[...].astype(BF16), b_ref[...].astype(BF16))

        @pl.when(k == nk - 1)
        def _():
            o_ref[...] = (acc[...] * scale).astype(o_ref.dtype)

    if a_split:
        a_spec = pl.BlockSpec((None, tk, tm), lambda i, j, k: (i // nih, k, i % nih))
    else:
        a_spec = pl.BlockSpec((tk, tm), lambda i, j, k: (k, i))
    if b_split:
        b_spec = pl.BlockSpec((None, tk, tn), lambda i, j, k: (j // njh, k, j % njh))
    else:
        b_spec = pl.BlockSpec((tk, tn), lambda i, j, k: (k, j))
    if out_blocked:
        o_spec = pl.BlockSpec((None, None, tm, tn), lambda i, j, k: (j, i, 0, 0))
        o_shape = jax.ShapeDtypeStruct((nj, ni, tm, tn), BF16)
    else:
        o_spec = pl.BlockSpec((tm, tn), lambda i, j, k: (i, j))
        o_shape = jax.ShapeDtypeStruct((M, N), BF16)
    outs = _call_with_riders(body, riding, (ni, nj, nk), [a_spec, b_spec], [o_spec], [o_shape],
                             [pltpu.VMEM((tm, tn), F32)], [a, b], name)
    return outs[0] if riding is None else tuple(outs)


def loss_kernel(y, target, *, tm, name):
    T, Dm = y.shape

    def body(y_ref, t_ref, l_ref, dy_ref):
        e = y_ref[...] - t_ref[...]
        dy_ref[...] = e * (1.0 / Dm)
        part = (0.5 / Dm) * jnp.sum(jnp.sum(e * e, axis=-1, keepdims=True), axis=0, keepdims=True)
        part = jnp.broadcast_to(part, (8, LANES))

        @pl.when(pl.program_id(0) == 0)
        def _():
            l_ref[...] = part

        @pl.when(pl.program_id(0) > 0)
        def _():
            l_ref[...] += part

    return pl.pallas_call(
        body, grid=(T // tm,),
        in_specs=[pl.BlockSpec((tm, Dm), lambda i: (i, 0)), pl.BlockSpec((tm, Dm), lambda i: (i, 0))],
        out_specs=[pl.BlockSpec((8, LANES), lambda i: (0, 0)), pl.BlockSpec((tm, Dm), lambda i: (i, 0))],
        out_shape=[jax.ShapeDtypeStruct((8, LANES), F32), jax.ShapeDtypeStruct((T, Dm), F32)],
        compiler_params=_cparams(1), name=name)(y, target)


def adamw(w, g, m, v, *, br, name):
    R, C = w.shape

    def body(w_ref, g_ref, m_ref, v_ref, d_ref, nm_ref, nv_ref):
        gv = g_ref[...]
        nm = ADAM_B1 * m_ref[...] + (1.0 - ADAM_B1) * gv
        nv = ADAM_B2 * v_ref[...] + (1.0 - ADAM_B2) * (gv * gv)
        m_hat = nm / (1.0 - ADAM_B1 ** ADAM_STEP)
        v_hat = nv / (1.0 - ADAM_B2 ** ADAM_STEP)
        d_ref[...] = -ADAM_LR * (m_hat / (jnp.sqrt(v_hat) + ADAM_EPS) + ADAM_WD * w_ref[...])
        nm_ref[...] = nm
        nv_ref[...] = nv

    spec = pl.BlockSpec((br, C), lambda i: (i, 0))
    shp = jax.ShapeDtypeStruct((R, C), F32)
    return pl.pallas_call(
        body, grid=(R // br,), in_specs=[spec] * 4, out_specs=[spec] * 3, out_shape=[shp] * 3,
        compiler_params=_cparams(1), name=name)(w, g, m, v)


def _lane0():
    return lax.broadcasted_iota(jnp.int32, (1, LANES), 1) < HEAD_DIM


def _half_sum(x, m0):
    s0 = jnp.sum(jnp.where(m0, x, 0.0), axis=-1, keepdims=True)
    s1 = jnp.sum(jnp.where(m0, 0.0, x), axis=-1, keepdims=True)
    return jnp.where(m0, s0, s1)


def _head_rms(x, m0):
    return lax.rsqrt(_half_sum(x * x, m0) * (1.0 / HEAD_DIM) + RMS_EPS)


def _alibi(n):
    return [float(2.0 ** (-8.0 * (h + 1) / n)) for h in range(n)]


def _mask_half(x, m0, e):
    return jnp.where(m0, x, 0.0) if e == 0 else jnp.where(m0, 0.0, x)


def _band_masks2(max_dist, has_prev, live):
    row = lax.broadcasted_iota(jnp.int32, (2 * BLK, 2 * BLK), 0)
    col = lax.broadcasted_iota(jnp.int32, (2 * BLK, 2 * BLK), 1)
    dist = (row & (BLK - 1)) - col + BLK
    lim = jnp.where(live, max_dist, -1)
    first = jnp.where(has_prev, 0, BLK)
    valid = (dist >= 0) & (dist <= lim) & (col >= first)
    top = lax.broadcasted_iota(jnp.int32, (2 * BLK, 1), 0) < BLK
    return dist.astype(F32), valid, top


def _stack_heads(x, m0, kes):
    parts = []
    for e in range(2):
        h = _mask_half(x, m0, e)
        parts.append(pltpu.roll(h, HEAD_DIM, 1) if kes[e] != e else h)
    return jnp.concatenate(parts, axis=0)


def _unstack_heads(y, m0, kes):
    parts = []
    for e in range(2):
        h = y[e * BLK:(e + 1) * BLK]
        parts.append(pltpu.roll(h, HEAD_DIM, 1) if kes[e] != e else h)
    return jnp.where(m0, parts[0], parts[1])


def _rows(r, dil):
    return pl.ds(r, BLK, stride=dil) if dil > 1 else pl.ds(0, BLK)


def _band_units(dil, nsub):
    assert dil == 1 or nsub == 1
    if nsub == 1:
        return [(_rows(r, dil), ("prev", _rows(r, dil)), 0) for r in range(dil)]
    units = [(pl.ds(0, BLK), ("prev", pl.ds(0, BLK)), 0)]
    units += [(pl.ds(BLK * s, BLK), ("cur", pl.ds(BLK * (s - 1), BLK)), s) for s in range(1, nsub)]
    return units


def _head_col_spec(ppk, RB, row_block):
    if ppk == 1:
        return pl.BlockSpec((None, RB, 1), lambda p, i: (p, row_block(i), 0))
    return pl.BlockSpec((ppk, RB, 1), lambda p, i: (p, row_block(i), 0))


def _band_specs(dil, nsub, ppk, q_blk, k_blk, v_blk, kv_shared, nb):
    RB = BLK * dil * nsub
    PB = BLK if nsub > 1 else RB
    qw = LANES * ppk
    kw = LANES if kv_shared else qw

    def cur(i):
        return jnp.minimum(i, nb - 1)

    def prev(i):
        return jnp.maximum(i * nsub - 1, 0) if nsub > 1 else jnp.maximum(i - 1, 0)

    def kidx(base):
        return (lambda p, i: (cur(i), base)) if kv_shared else (lambda p, i: (cur(i), base + p))

    def pidx(base):
        return (lambda p, i: (prev(i), base)) if kv_shared else (lambda p, i: (prev(i), base + p))

    return [pl.BlockSpec((RB, qw), lambda p, i: (cur(i), q_blk + p)),
            pl.BlockSpec((RB, kw), kidx(k_blk)), pl.BlockSpec((PB, kw), pidx(k_blk)),
            pl.BlockSpec((RB, kw), kidx(v_blk)), pl.BlockSpec((PB, kw), pidx(v_blk))]


def qk_norm(qkv, q_gain2, k_gain2, *, width, steps, n_q, tm, name):
    T = qkv.shape[0]
    nsb = width // LANES

    def body(x_ref, qg_ref, kg_ref, o_ref):
        m0 = _lane0()
        for b in range(nsb):
            is_q = ((pl.program_id(1) * nsb + b) < n_q).astype(F32)
            gain = qg_ref[...] * is_q + kg_ref[...] * (1.0 - is_q)
            cols = pl.ds(LANES * b, LANES)
            xv = x_ref[:, cols]
            o_ref[:, cols] = xv * _head_rms(xv, m0) * gain

    gspec = pl.BlockSpec((1, LANES), lambda i, j: (0, 0))
    return pl.pallas_call(
        body, grid=(T // tm, steps),
        in_specs=[pl.BlockSpec((tm, width), lambda i, j: (i, j)), gspec, gspec],
        out_specs=pl.BlockSpec((tm, width), lambda i, j: (i, j)),
        out_shape=jax.ShapeDtypeStruct((T, width * steps), F32),
        compiler_params=_cparams(2), name=name)(qkv, q_gain2, k_gain2)


def banded_fwd(qkn, qkv, slopes, sinks, *, dil, nsub, ppk, q_blk, k_blk, v_blk, n_heads, group,
               max_dist, name):
    T = qkv.shape[0]
    RB = BLK * dil * nsub
    nb = T // RB
    npair = n_heads // 2
    kv_shared = group > 1
    scale = HEAD_DIM ** -0.5
    has_sink = sinks is not None

    def body(*refs):
        slope_ref = refs[0]
        if has_sink:
            sink_ref, refs = refs[1], refs[2:]
        else:
            refs = refs[1:]
        q_ref, kc_ref, kp_ref, vc_ref, vp_ref, o_ref, l_ref, lc0_ref, lc1_ref = refs
        pb = pl.program_id(0)
        i = pl.program_id(1)
        m0 = _lane0()
        distf, valid_first, top = _band_masks2(max_dist, i > 0, i >= 0)
        valid_inner = _band_masks2(max_dist, i >= 0, i >= 0)[1] if nsub > 1 else None
        for rows, (src, prows), sub in _band_units(dil, nsub):
            valid = valid_first if sub == 0 else valid_inner
            kpr, vpr = (kp_ref, vp_ref) if src == "prev" else (kc_ref, vc_ref)
            kcache = {}
            for jp in range(ppk):
                cs = pl.ds(LANES * jp, LANES)
                jk = 0 if kv_shared else jp
                if jk not in kcache:
                    ks = pl.ds(LANES * jk, LANES)
                    kcat = jnp.concatenate([kpr[prows, ks], kc_ref[rows, ks]], axis=0)
                    vcat = jnp.concatenate([vpr[prows, ks], vc_ref[rows, ks]], axis=0)
                    kcache[jk] = (kcat.astype(BF16), vcat.astype(BF16))
                kn, vcat = kcache[jk]
                qn = q_ref[rows, cs]
                kes = [((2 * jp + e) // group) % 2 if kv_shared else e for e in range(2)]
                hidx = 2 * (pb * ppk + jp)
                qs = _stack_heads(qn, m0, kes).astype(BF16)
                slope = jnp.where(top, slope_ref[hidx], slope_ref[hidx + 1])
                s = jnp.where(valid, _dot_nt(qs, kn) * scale - slope * distf, NEG)
                m = jnp.max(s, axis=-1, keepdims=True)
                if has_sink:
                    sk = jnp.where(top, sink_ref[hidx], sink_ref[hidx + 1])
                    m = jnp.maximum(m, sk)
                p = jnp.exp(s - m)
                den = jnp.sum(p, axis=-1, keepdims=True)
                if has_sink:
                    den = den + jnp.exp(sk - m)
                o_full = _dot((p * (1.0 / den)).astype(BF16), vcat)
                o_ref[rows, cs] = _unstack_heads(o_full, m0, kes)
                lse = m + jnp.log(den)
                l_ref[rows, cs] = _unstack_heads(jnp.broadcast_to(lse, (2 * BLK, LANES)), m0, [0, 1])
                for e, lc_ref in enumerate((lc0_ref, lc1_ref)):
                    if ppk == 1:
                        lc_ref[rows, :] = lse[e * BLK:(e + 1) * BLK]
                    else:
                        lc_ref[jp, rows, :] = lse[e * BLK:(e + 1) * BLK]

    smem = pl.BlockSpec(memory_space=pltpu.SMEM)
    qw = LANES * ppk
    ospec = pl.BlockSpec((RB, qw), lambda p, i: (i, p))
    oshape = jax.ShapeDtypeStruct((T, n_heads * HEAD_DIM), F32)
    args = [slopes] + ([sinks] if has_sink else []) + [qkn] * 3 + [qkv] * 2
    return pl.pallas_call(
        body, grid=(npair // ppk, nb),
        in_specs=[smem] * (2 if has_sink else 1) + _band_specs(dil, nsub, ppk, q_blk, k_blk, v_blk, kv_shared, nb),
        out_specs=[ospec, ospec] + [_head_col_spec(ppk, RB, lambda i: i)] * 2,
        out_shape=[oshape, oshape] + [jax.ShapeDtypeStruct((npair, T, 1), F32)] * 2,
        compiler_params=_cparams(2), name=name)(*args)


def banded_bwd(qkn, qkv, slopes, sinks, do, o, lsec, w, omix, *, dil, nsub, ppk, q_blk, k_blk, v_blk,
               n_heads, group, max_dist, do_blk, name):
    T = qkv.shape[0]
    RB = BLK * dil * nsub
    nb = T // RB
    npair = n_heads // 2
    kv_shared = group > 1
    scale = HEAD_DIM ** -0.5
    has_sink = sinks is not None
    mixed = w is not None
    qw = LANES * ppk

    def body(*refs):
        slope_ref = refs[0]
        if has_sink:
            sink_ref, refs = refs[1], refs[2:]
        else:
            refs = refs[1:]
        q_ref, kc_ref, kp_ref, vc_ref, vp_ref, do_ref, lc0_ref, lc1_ref = refs[:8]
        refs = refs[8:]
        if mixed:
            w_ref, om_ref, refs = refs[0], refs[1], refs[2:]
        else:
            o_ref, refs = refs[0], refs[1:]
        dq_ref, dk_ref, dv_ref, dsk_ref, ck_ref, cv_ref = refs
        pb = pl.program_id(0)
        i = pl.program_id(1)
        live = i < nb
        m0 = _lane0()
        lane = lax.broadcasted_iota(jnp.int32, (1, LANES), 1)
        distf, valid_first, top = _band_masks2(max_dist, i > 0, live)
        valid_inner = _band_masks2(max_dist, i >= 0, live)[1] if nsub > 1 else None
        livef = live.astype(F32)

        def half_rows(x):
            s0 = jnp.sum(jnp.where(m0, x, 0.0), axis=-1, keepdims=True)
            s1 = jnp.sum(jnp.where(m0, 0.0, x), axis=-1, keepdims=True)
            return jnp.concatenate([s0, s1], axis=0)

        @pl.when((pb == 0) & (i == 0))
        def _():
            dsk_ref[...] = jnp.zeros_like(dsk_ref)

        @pl.when(i == 0)
        def _():
            ck_ref[...] = jnp.zeros_like(ck_ref)
            cv_ref[...] = jnp.zeros_like(cv_ref)

        dsk_acc = jnp.zeros((1, LANES), F32)
        if nsub > 1:
            dk_ref[...] = ck_ref[...]
            dv_ref[...] = cv_ref[...]
        for rows, (src, prows), sub in _band_units(dil, nsub):
            valid = valid_first if sub == 0 else valid_inner
            kpr, vpr = (kp_ref, vp_ref) if src == "prev" else (kc_ref, vc_ref)
            for jp in range(ppk):
                cs = pl.ds(LANES * jp, LANES)
                ks = pl.ds(0, LANES) if kv_shared else cs
                kn = jnp.concatenate([kpr[prows, ks], kc_ref[rows, ks]], axis=0).astype(BF16)
                vcat = jnp.concatenate([vpr[prows, ks], vc_ref[rows, ks]], axis=0).astype(BF16)
                dov = do_ref[rows, cs]
                if mixed:
                    dov = dov * w_ref[rows, cs]
                    shift = half_rows(dov * om_ref[rows, cs])
                else:
                    shift = half_rows(dov * o_ref[rows, cs])
                kes = [((2 * jp + e) // group) % 2 if kv_shared else e for e in range(2)]
                hidx = 2 * (pb * ppk + jp)
                qs = _stack_heads(q_ref[rows, cs], m0, kes).astype(BF16)
                dos = _stack_heads(dov, m0, kes).astype(BF16)
                lse = jnp.concatenate([ref[rows, :] if ppk == 1 else ref[jp, rows, :]
                                       for ref in (lc0_ref, lc1_ref)], axis=0)
                slope = jnp.where(top, slope_ref[hidx], slope_ref[hidx + 1])
                p = jnp.where(valid, jnp.exp(_dot_nt(qs, kn) * scale - slope * distf - lse), 0.0)
                ds = (p * (_dot_nt(dos, vcat) - shift)).astype(BF16)
                dqn = _unstack_heads(_dot(ds, kn), m0, kes) * scale
                dkn = _dot_tn(ds, qs) * scale
                dvv = _dot_tn(p.astype(BF16), dos)
                if has_sink:
                    sk = jnp.where(top, sink_ref[hidx], sink_ref[hidx + 1])
                    contrib = -jnp.exp(sk - lse) * shift * livef
                    for e in range(2):
                        tot = jnp.sum(contrib[e * BLK:(e + 1) * BLK], axis=0, keepdims=True)
                        dsk_acc = dsk_acc + jnp.where(lane == (2 * jp + e), tot, 0.0)
                dk_raw = dkn

                @pl.when(live)
                def _():
                    dq_ref[rows, cs] = dqn

                if nsub == 1:
                    dk_ref[rows, cs] = ck_ref[rows, cs] + dk_raw[:BLK]
                    dv_ref[rows, cs] = cv_ref[rows, cs] + dvv[:BLK]
                elif sub == 0:
                    last = pl.ds(RB - BLK, BLK)
                    dk_ref[last, cs] += dk_raw[:BLK]
                    dv_ref[last, cs] += dvv[:BLK]
                else:
                    ck_ref[prows, cs] += dk_raw[:BLK]
                    cv_ref[prows, cs] += dvv[:BLK]
                ck_ref[rows, cs] = dk_raw[BLK:]
                cv_ref[rows, cs] = dvv[BLK:]
        dsk_ref[...] += dsk_acc

    smem = pl.BlockSpec(memory_space=pltpu.SMEM)
    gspec = pl.BlockSpec((1, LANES), lambda p, i: (0, 0))

    def cur(i):
        return jnp.minimum(i, nb - 1)

    qspec = pl.BlockSpec((RB, qw), lambda p, i: (cur(i), p))
    dospec = pl.BlockSpec((RB, qw), lambda p, i: (cur(i), do_blk + p))
    kvout = pl.BlockSpec((RB, qw), lambda p, i: (jnp.maximum(i - 1, 0), p))
    in_specs = ([smem] * (2 if has_sink else 1) + _band_specs(dil, nsub, ppk, q_blk, k_blk, v_blk, kv_shared, nb)
                + [dospec] + [_head_col_spec(ppk, RB, cur)] * 2
                + ([qspec, qspec] if mixed else [qspec]))
    args = ([slopes] + ([sinks] if has_sink else []) + [qkn] * 3 + [qkv] * 2 + [do, lsec[0], lsec[1]]
            + ([w, omix] if mixed else [o]))
    full = jax.ShapeDtypeStruct((T, n_heads * HEAD_DIM), F32)
    row = jax.ShapeDtypeStruct((1, LANES), F32)
    return pl.pallas_call(
        body, grid=(npair // ppk, nb + 1), in_specs=in_specs,
        out_specs=[qspec, kvout, kvout, gspec],
        out_shape=[full, full, full, row],
        scratch_shapes=[pltpu.VMEM((RB, qw), F32), pltpu.VMEM((RB, qw), F32)],
        compiler_params=_cparams(2), name=name)(*args)


def mix_fwd(o1, o2, o3, l1, l2, l3, *, tm, name):
    T, C = o1.shape

    def body(o1r, o2r, o3r, l1r, l2r, l3r, o_ref, w1r, w2r, w3r):
        a, b, c = l1r[...], l2r[...], l3r[...]
        m = jnp.maximum(jnp.maximum(a, b), c)
        ea, eb, ec = jnp.exp(a - m), jnp.exp(b - m), jnp.exp(c - m)
        inv = 1.0 / (ea + eb + ec)
        wa, wb, wc = ea * inv, eb * inv, ec * inv
        o_ref[...] = wa * o1r[...] + wb * o2r[...] + wc * o3r[...]
        w1r[...] = wa
        w2r[...] = wb
        w3r[...] = wc

    spec = pl.BlockSpec((tm, C), lambda i: (i, 0))
    shp = jax.ShapeDtypeStruct((T, C), F32)
    return pl.pallas_call(body, grid=(T // tm,), in_specs=[spec] * 6, out_specs=[spec] * 4, out_shape=[shp] * 4,
                          compiler_params=_cparams(1), name=name)(o1, o2, o3, l1, l2, l3)


def _qk_norm_bwd(raw, dn, gain, m0):
    r = _head_rms(raw, m0)
    h = raw * r
    dh = dn * gain
    d_raw = r * (dh - h * (_half_sum(dh * h, m0) * (1.0 / HEAD_DIM)))
    return d_raw, jnp.sum(dn * h, axis=0, keepdims=True)


def _acc_rows(ref, val):
    @pl.when(pl.program_id(0) == 0)
    def _():
        ref[...] = val

    @pl.when(pl.program_id(0) > 0)
    def _():
        ref[...] += val


def assemble_odd(parts, qkv, q_gain2, k_gain2, *, tm, name):
    T, C = parts[0][0].shape
    nbk = C // LANES

    def body(*refs):
        qkv_ref, qg_ref, kg_ref, o_ref, dqg_ref, dkg_ref = refs[9:]
        m0 = _lane0()
        sums = [refs[j][...] + refs[3 + j][...] + refs[6 + j][...] for j in range(3)]
        o_ref[:, pl.ds(2 * C, C)] = sums[2]
        for j, (g_ref, acc_ref) in enumerate(((qg_ref, dqg_ref), (kg_ref, dkg_ref))):
            dgain = jnp.zeros((1, LANES), F32)
            for b in range(nbk):
                cols = pl.ds(C * j + LANES * b, LANES)
                d_raw, part = _qk_norm_bwd(qkv_ref[:, cols], sums[j][:, LANES * b:LANES * (b + 1)], g_ref[...], m0)
                o_ref[:, cols] = d_raw
                dgain = dgain + part
            _acc_rows(acc_ref, dgain)

    spec = pl.BlockSpec((tm, C), lambda i: (i, 0))
    gspec = pl.BlockSpec((1, LANES), lambda i: (0, 0))
    flat = [parts[p][j] for p in range(3) for j in range(3)]
    row = jax.ShapeDtypeStruct((1, LANES), F32)
    return pl.pallas_call(body, grid=(T // tm,),
                          in_specs=[spec] * 9 + [pl.BlockSpec((tm, 2 * C), lambda i: (i, 0)), gspec, gspec],
                          out_specs=[pl.BlockSpec((tm, 3 * C), lambda i: (i, 0)), gspec, gspec],
                          out_shape=[jax.ShapeDtypeStruct((T, 3 * C), F32), row, row],
                          compiler_params=_cparams(1), name=name)(*flat, qkv, q_gain2, k_gain2)


def assemble_even(dqa, dka4, dva4, dqb, dkb, dvb, qkv, q_gain2, k_gain2, *, tm, name):
    T = dqa.shape[0]
    W = 512
    QK = 768

    def body(dqa_r, dka_r, dva_r, dqb_r, dkb_r, dvb_r, qkv_ref, qg_ref, kg_ref, o_ref, dqg_ref, dkg_ref):
        m0 = _lane0()
        ka = dka_r[...]
        va = dva_r[...]
        dqn = dqa_r[...]
        dgain = jnp.zeros((1, LANES), F32)
        for b in range(W // LANES):
            cols = pl.ds(LANES * b, LANES)
            d_raw, part = _qk_norm_bwd(qkv_ref[:, cols], dqn[:, LANES * b:LANES * (b + 1)], qg_ref[...], m0)
            o_ref[:, cols] = d_raw
            dgain = dgain + part
        _acc_rows(dqg_ref, dgain)
        dkn = ka[:, 0:128] + ka[:, 128:256] + ka[:, 256:384] + ka[:, 384:512]
        d_raw, part = _qk_norm_bwd(qkv_ref[:, pl.ds(W, LANES)], dkn, kg_ref[...], m0)
        o_ref[:, pl.ds(W, LANES)] = d_raw
        _acc_rows(dkg_ref, part)
        o_ref[:, pl.ds(640, LANES)] = va[:, 0:128] + va[:, 128:256] + va[:, 256:384] + va[:, 384:512]
        o_ref[:, pl.ds(768, W)] = dqb_r[...]
        o_ref[:, pl.ds(1280, W)] = dkb_r[...]
        o_ref[:, pl.ds(1792, W)] = dvb_r[...]

    spec = pl.BlockSpec((tm, W), lambda i: (i, 0))
    gspec = pl.BlockSpec((1, LANES), lambda i: (0, 0))
    row = jax.ShapeDtypeStruct((1, LANES), F32)
    return pl.pallas_call(body, grid=(T // tm,),
                          in_specs=[spec] * 6 + [pl.BlockSpec((tm, QK), lambda i: (i, 0)), gspec, gspec],
                          out_specs=[pl.BlockSpec((tm, 2304), lambda i: (i, 0)), gspec, gspec],
                          out_shape=[jax.ShapeDtypeStruct((T, 2304), F32), row, row],
                          compiler_params=_cparams(1), name=name)(dqa, dka4, dva4, dqb, dkb, dvb, qkv, q_gain2, k_gain2)


STICK_T = 256
STICK_DEAD = -110.0


def _split_bf16(x):
    hi = x.astype(BF16)
    lo = (x - hi.astype(F32)).astype(BF16)
    return hi, lo


def _stick_logits(qm, kt, scale, diag):
    n = STICK_T
    row = lax.broadcasted_iota(jnp.int32, (n, n), 0)
    col = lax.broadcasted_iota(jnp.int32, (n, n), 1)
    mask = col < row + jnp.where(diag, 0, n)
    z = _dot_nt(qm, kt) * scale
    lneg = -(jnp.maximum(z, 0.0) + jnp.log(1.0 + jnp.exp(-jnp.abs(z))))
    lpos = z + lneg
    lk = jnp.where(mask, lneg, 0.0)
    return mask, lpos, lneg, lk


def _cumsum_mm(x, tri):
    hi, lo = _split_bf16(x)
    return _dot(hi, tri) + _dot(lo, tri)


def stick_fwd(qkv, *, q_blk, k_blk, v_blk, n_pairs, name, riders=None, rider_args=()):
    T = qkv.shape[0]
    n = STICK_T
    nq = T // n
    scale = HEAD_DIM ** -0.5
    nc = riders.n if riders is not None else 0
    n_steps = n_pairs * nq
    stage_at = (0, (3 * n_steps) // 4, n_steps - 1, n_steps - 1)

    def body(*refs):
        q_ref, k_ref, v_ref = refs[:3]
        x_refs, o_ref = refs[3:3 + nc], refs[3 + nc]
        out_refs, sems = refs[4 + nc:4 + 2 * nc], refs[4 + 2 * nc:]
        i = pl.program_id(1)
        step_id = pl.program_id(0) * nq + i

        def ride(which):
            if riders is not None:
                @pl.when(step_id == stage_at[which])
                def _():
                    riders.stage(which, x_refs, out_refs, sems)

        ride(0)
        ride(1)
        m0 = _lane0()
        r2 = lax.broadcasted_iota(jnp.int32, (n, n), 0)
        c2 = lax.broadcasted_iota(jnp.int32, (n, n), 1)
        tri_after = (r2 > c2).astype(BF16)
        qv = q_ref[...]
        out = jnp.zeros((n, LANES), F32)
        for e in range(2):
            qm = _mask_half(qv, m0, e).astype(BF16)

            def alive(st):
                t, _, carry = st
                return (t <= i) & (jnp.max(carry) > STICK_DEAD)

            def step(st, e=e, qm=qm):
                t, acc, carry = st
                start = pl.multiple_of((i - t) * n, n)
                kt = k_ref[pl.ds(start, n), :].astype(BF16)
                vt = _mask_half(v_ref[pl.ds(start, n), :], m0, e).astype(BF16)
                mask, lpos, _, lk = _stick_logits(qm, kt, scale, t == 0)
                after = _cumsum_mm(lk, tri_after) + carry
                a = jnp.where(mask, jnp.exp(lpos + after), 0.0)
                acc = acc + _dot(a.astype(BF16), vt)
                carry = carry + jnp.sum(lk, axis=-1, keepdims=True)
                return t + 1, acc, carry

            _, acc, _ = lax.while_loop(alive, step, (jnp.int32(0), jnp.zeros((n, LANES), F32),
                                                     jnp.zeros((n, 1), F32)))
            out = out + acc
        o_ref[...] = out
        ride(2)
        ride(3)

    outs = pl.pallas_call(
        body, grid=(n_pairs, nq),
        in_specs=[pl.BlockSpec((n, LANES), lambda p, i: (i, q_blk + p)),
                  pl.BlockSpec((T, LANES), lambda p, i: (0, k_blk + p)),
                  pl.BlockSpec((T, LANES), lambda p, i: (0, v_blk + p))] + [_ANY] * nc,
        out_specs=[pl.BlockSpec((n, LANES), lambda p, i: (i, p))] + [_ANY] * nc,
        out_shape=[jax.ShapeDtypeStruct((T, n_pairs * LANES), F32)] + (riders.shapes if nc else []),
        scratch_shapes=riders.sems if nc else [],
        compiler_params=_cparams(2), name=name)(qkv, qkv, qkv, *rider_args)
    return outs[0], list(outs[1:])


def stick_bwd(qkv, do, *, q_blk, k_blk, v_blk, do_blk, n_pairs, name, riders=None, rider_args=()):
    T = qkv.shape[0]
    n = STICK_T
    nq = T // n
    scale = HEAD_DIM ** -0.5
    nc = riders.n if riders is not None else 0

    def body(*refs):
        q_ref, k_ref, v_ref, do_ref = refs[:4]
        pre_refs = refs[4:4 + nc]
        dq_ref, dk_ref, dv_ref = refs[4 + nc:7 + nc]
        land_refs = refs[7 + nc:7 + 2 * nc]
        a_keep, g_keep, s_keep = refs[7 + 2 * nc:10 + 2 * nc]
        sems = refs[10 + 2 * nc:]
        i = pl.program_id(1)
        first_step = (pl.program_id(0) == 0) & (i == 0)
        last_step = (pl.program_id(0) == n_pairs - 1) & (i == nq - 1)
        m0 = _lane0()
        r2 = lax.broadcasted_iota(jnp.int32, (n, n), 0)
        c2 = lax.broadcasted_iota(jnp.int32, (n, n), 1)
        tri_after = (r2 > c2).astype(BF16)
        tri_from = (r2 >= c2).astype(BF16)

        if riders is not None:
            @pl.when(first_step)
            def _():
                riders.start(pre_refs, land_refs, sems)

        @pl.when(i == 0)
        def _():
            dk_ref[...] = jnp.zeros_like(dk_ref)
            dv_ref[...] = jnp.zeros_like(dv_ref)

        qv = q_ref[...]
        dov = do_ref[...]
        dq_out = jnp.zeros((n, LANES), F32)
        for e in range(2):
            qm = _mask_half(qv, m0, e).astype(BF16)
            dom = _mask_half(dov, m0, e).astype(BF16)

            def alive(st):
                t, carry, _ = st
                return (t <= i) & (jnp.max(carry) > STICK_DEAD)

            def scan(st, qm=qm, dom=dom):
                t, carry, gtot = st
                start = pl.multiple_of((i - t) * n, n)
                kt = k_ref[pl.ds(start, n), :].astype(BF16)
                vt = v_ref[pl.ds(start, n), :].astype(BF16)
                mask, lpos, lneg, lk = _stick_logits(qm, kt, scale, t == 0)
                a = jnp.where(mask, jnp.exp(lpos + _cumsum_mm(lk, tri_after) + carry), 0.0)
                g = _dot_nt(dom, vt) * a
                a_keep[t] = a.astype(BF16)
                g_keep[t] = g
                s_keep[t] = jnp.exp(lneg).astype(BF16)
                return (t + 1, carry + jnp.sum(lk, axis=-1, keepdims=True),
                        gtot + jnp.sum(g, axis=-1, keepdims=True))

            z1 = jnp.zeros((n, 1), F32)
            n_live, _, gtot = lax.while_loop(alive, scan, (jnp.int32(0), z1, z1))

            def step(t, st, e=e, qm=qm, dom=dom, gtot=gtot):
                dq_acc, gright = st
                start = pl.multiple_of((i - t) * n, n)
                g = g_keep[t]
                sneg = s_keep[t].astype(F32)
                before = gtot - (_cumsum_mm(g, tri_from) + gright)
                mask = c2 < r2 + jnp.where(t == 0, 0, n)
                dz = jnp.where(mask, g * sneg - before * (1.0 - sneg), 0.0) * scale
                dzb = dz.astype(BF16)
                dq_acc = dq_acc + _dot(dzb, _mask_half(k_ref[pl.ds(start, n), :], m0, e).astype(BF16))
                dk_ref[pl.ds(start, n), :] += _dot_tn(dzb, qm)
                dv_ref[pl.ds(start, n), :] += _dot_tn(a_keep[t], dom)
                return dq_acc, gright + jnp.sum(g, axis=-1, keepdims=True)

            dq_acc, _ = lax.fori_loop(0, n_live, step, (jnp.zeros((n, LANES), F32), z1))
            dq_out = dq_out + dq_acc
        dq_ref[...] = dq_out

        if riders is not None:
            @pl.when(last_step)
            def _():
                riders.finish(pre_refs, land_refs, sems)

    tile = pl.BlockSpec((n, LANES), lambda p, i: (i, p))
    whole = pl.BlockSpec((T, LANES), lambda p, i: (0, p))
    shp = jax.ShapeDtypeStruct((T, n_pairs * LANES), F32)
    outs = pl.pallas_call(
        body, grid=(n_pairs, nq),
        in_specs=[pl.BlockSpec((n, LANES), lambda p, i: (i, q_blk + p)),
                  pl.BlockSpec((T, LANES), lambda p, i: (0, k_blk + p)),
                  pl.BlockSpec((T, LANES), lambda p, i: (0, v_blk + p)),
                  pl.BlockSpec((n, LANES), lambda p, i: (i, do_blk + p))] + [_ANY] * nc,
        out_specs=[tile, whole, whole] + [_ANY] * nc,
        out_shape=[shp, shp, shp] + (riders.shapes if nc else []),
        scratch_shapes=[pltpu.VMEM((nq, n, n), BF16), pltpu.VMEM((nq, n, n), F32), pltpu.VMEM((nq, n, n), BF16)]
        + (riders.sems if nc else []),
        compiler_params=_cparams(2), name=name)(qkv, qkv, qkv, do, *rider_args)
    return outs[0], outs[1], outs[2], list(outs[3:])


def _xnorm(x):
    r = lax.rsqrt(jnp.mean(x * x, axis=-1, keepdims=True) + RMS_EPS)
    return r, x * r


def xattn_fwd(qraw, kvraw, q_gain, k_gain, *, tm, name):
    T = qraw.shape[0]
    scale = X_HEAD_DIM ** -0.5
    W = X_HEADS * X_HEAD_DIM

    def body(q_ref, kv_ref, qg_ref, kg_ref, o_ref):
        for h in range(X_HEADS):
            cs = pl.ds(X_HEAD_DIM * h, X_HEAD_DIM)
            _, qh = _xnorm(q_ref[:, cs])
            _, kh = _xnorm(kv_ref[:, cs])
            qn = (qh * qg_ref[...]).astype(BF16)
            kn = (kh * kg_ref[...]).astype(BF16)
            v = kv_ref[:, pl.ds(W + X_HEAD_DIM * h, X_HEAD_DIM)].astype(BF16)
            s = _dot_nt(qn, kn) * scale
            m = jnp.max(s, axis=-1, keepdims=True)
            p = jnp.exp(s - m)
            p = p / jnp.sum(p, axis=-1, keepdims=True)
            o_ref[:, cs] = _dot(p.astype(BF16), v)

    gspec = pl.BlockSpec((1, X_HEAD_DIM), lambda i: (0, 0))
    return pl.pallas_call(
        body, grid=(T // tm,),
        in_specs=[pl.BlockSpec((tm, W), lambda i: (i, 0)), pl.BlockSpec((MEM_LEN, 2 * W), lambda i: (0, 0)),
                  gspec, gspec],
        out_specs=pl.BlockSpec((tm, W), lambda i: (i, 0)),
        out_shape=jax.ShapeDtypeStruct((T, W), F32),
        compiler_params=_cparams(1), name=name)(qraw, kvraw, q_gain, k_gain)


def xattn_bwd(qraw, kvraw, q_gain, k_gain, do, o, *, tm, name):
    T = qraw.shape[0]
    nt = T // tm
    scale = X_HEAD_DIM ** -0.5
    W = X_HEADS * X_HEAD_DIM

    def body(q_ref, kv_ref, qg_ref, kg_ref, do_ref, o_ref, dq_ref, dkv_ref, dqg_ref, dkg_ref, dkn_ref):
        i = pl.program_id(0)

        @pl.when(i == 0)
        def _():
            dkv_ref[...] = jnp.zeros_like(dkv_ref)
            dkn_ref[...] = jnp.zeros_like(dkn_ref)
            dqg_ref[...] = jnp.zeros_like(dqg_ref)
            dkg_ref[...] = jnp.zeros_like(dkg_ref)

        qg = qg_ref[...]
        kg = kg_ref[...]
        dqg_acc = jnp.zeros((1, X_HEAD_DIM), F32)
        for h in range(X_HEADS):
            cs = pl.ds(X_HEAD_DIM * h, X_HEAD_DIM)
            vs = pl.ds(W + X_HEAD_DIM * h, X_HEAD_DIM)
            rq, qh = _xnorm(q_ref[:, cs])
            _, kh = _xnorm(kv_ref[:, cs])
            qn = (qh * qg).astype(BF16)
            kn = (kh * kg).astype(BF16)
            v = kv_ref[:, vs].astype(BF16)
            s = _dot_nt(qn, kn) * scale
            m = jnp.max(s, axis=-1, keepdims=True)
            p = jnp.exp(s - m)
            p = p / jnp.sum(p, axis=-1, keepdims=True)
            dov = do_ref[:, cs]
            delta = jnp.sum(dov * o_ref[:, cs], axis=-1, keepdims=True)
            dob = dov.astype(BF16)
            ds = (p * (_dot_nt(dob, v) - delta)).astype(BF16)
            dqn = _dot(ds, kn) * scale
            dkn_ref[:, cs] += _dot_tn(ds, qn) * scale
            dkv_ref[:, vs] += _dot_tn(p.astype(BF16), dob)
            dqg_acc = dqg_acc + jnp.sum(dqn * qh, axis=0, keepdims=True)
            dqh = dqn * qg
            dq_ref[:, cs] = rq * (dqh - qh * jnp.mean(dqh * qh, axis=-1, keepdims=True))
        dqg_ref[...] += dqg_acc

        @pl.when(i == nt - 1)
        def _():
            dkg_acc = jnp.zeros((1, X_HEAD_DIM), F32)
            for h in range(X_HEADS):
                cs = pl.ds(X_HEAD_DIM * h, X_HEAD_DIM)
                rk, kh = _xnorm(kv_ref[:, cs])
                dkn = dkn_ref[:, cs]
                dkg_acc = dkg_acc + jnp.sum(dkn * kh, axis=0, keepdims=True)
                dkh = dkn * kg
                dkv_ref[:, cs] = rk * (dkh - kh * jnp.mean(dkh * kh, axis=-1, keepdims=True))
            dkg_ref[...] = dkg_acc

    gspec = pl.BlockSpec((1, X_HEAD_DIM), lambda i: (0, 0))
    tile = pl.BlockSpec((tm, W), lambda i: (i, 0))
    kvspec = pl.BlockSpec((MEM_LEN, 2 * W), lambda i: (0, 0))
    grow = jax.ShapeDtypeStruct((1, X_HEAD_DIM), F32)
    return pl.pallas_call(
        body, grid=(nt,), in_specs=[tile, kvspec, gspec, gspec, tile, tile],
        out_specs=[tile, kvspec, gspec, gspec],
        out_shape=[jax.ShapeDtypeStruct((T, W), F32), jax.ShapeDtypeStruct((MEM_LEN, 2 * W), F32), grow, grow],
        scratch_shapes=[pltpu.VMEM((MEM_LEN, W), F32)],
        compiler_params=_cparams(1), name=name)(qraw, kvraw, q_gain, k_gain, do, o)


_ANY = pl.BlockSpec(memory_space=pl.ANY)


def _my_pos():
    return lax.axis_index("x"), lax.axis_index("y"), lax.axis_index("c")


def _pieces(arrays, chunks):
    out = []
    for a, (arr, n) in enumerate(zip(arrays, chunks)):
        rc = arr.shape[-2] // n
        out += [(a, pl.ds(ch * rc, rc)) for ch in range(n)]
    return out


class GatherBlocks:
    N_STAGES = 4

    def __init__(self, blks, chunks):
        self.shapes = [jax.ShapeDtypeStruct((N_DEV,) + b.shape, b.dtype) for b in blks]
        self.n = len(blks)
        self.pieces = _pieces(blks, chunks)
        n_p = len(self.pieces)
        self.sems = [pltpu.SemaphoreType.DMA((7 * n_p,)), pltpu.SemaphoreType.DMA((7 * n_p,)),
                     pltpu.SemaphoreType.DMA((n_p,))]

    def stage(self, which, x_refs, out_refs, sems):
        send_sems, recv_sems, local_sems = sems
        pieces, n_p = self.pieces, len(self.pieces)
        x, y, c = _my_pos()
        me, sibling = (x, y, c), (x, y, 1 - c)
        chips = [(1 - x, y), (x, 1 - y), (1 - x, 1 - y)]
        xn, yn, dg = [(*chip, c) for chip in chips]
        ps = range(n_p)

        def slot(block, p):
            px, py, pc = block
            a, rows = pieces[p]
            return out_refs[a].at[4 * px + 2 * py + pc, rows]

        def own(p):
            a, rows = pieces[p]
            return x_refs[a].at[rows]

        def copy(k, p, block, to, from_input=False):
            return pltpu.make_async_remote_copy(
                src_ref=own(p) if from_input else slot(block, p), dst_ref=slot(block, p),
                send_sem=send_sems.at[k * n_p + p], recv_sem=recv_sems.at[k * n_p + p],
                device_id=to, device_id_type=MESH)

        mine = [pltpu.make_async_copy(own(p), slot(me, p), local_sems.at[p]) for p in ps]
        first = [copy(k, p, me, to, from_input=True) for p in ps for k, to in ((1, xn), (2, yn), (0, sibling))]
        on_x = [copy(3, p, xn, yn) for p in ps if p % 2 == 0] + [copy(4, p, xn, sibling) for p in ps]
        on_y = [copy(3, p, yn, xn) for p in ps if p % 2 == 1] + [copy(5, p, yn, sibling) for p in ps]
        on_d = [copy(6, p, dg, sibling) for p in ps]
        if which == 0:
            for cp in first + mine:
                cp.start()
        elif which == 1:
            for p in ps:
                copy(1, p, xn, me).wait_recv()
                if p % 2 == 0:
                    copy(3, p, xn, yn).start()
                copy(4, p, xn, sibling).start()
                copy(2, p, yn, me).wait_recv()
                if p % 2 == 1:
                    copy(3, p, yn, xn).start()
                copy(5, p, yn, sibling).start()
        elif which == 2:
            for p in ps:
                copy(3, p, dg, me).wait_recv()
                copy(6, p, dg, sibling).start()
        else:
            for p in ps:
                copy(0, p, sibling, me).wait_recv()
            for k, chip in zip((4, 5, 6), chips):
                for p in ps:
                    copy(k, p, (*chip, 1 - c), me).wait_recv()
            for cp in first + on_x + on_y + on_d:
                cp.wait_send()
            for cp in mine:
                cp.wait()


def gather_blocks(blks, chunks, *, name):
    gb = GatherBlocks(blks, chunks)
    n = gb.n

    def body(*refs):
        x_refs, out_refs, sems = refs[:n], refs[n:2 * n], refs[2 * n:]
        for which in range(gb.N_STAGES):
            gb.stage(which, x_refs, out_refs, sems)

    return pl.pallas_call(body, out_shape=gb.shapes, in_specs=[_ANY] * n, out_specs=[_ANY] * n,
                          scratch_shapes=gb.sems, name=name)(*blks)


def gather_small(small, *, name):
    S, C = small.shape

    def body(s_ref, out_ref, send_sems, recv_sems, local_sem):
        x, y, c = _my_pos()
        my_id = 4 * x + 2 * y + c

        def copy(k, slot):
            px, py, pc = x ^ ((k >> 2) & 1), y ^ ((k >> 1) & 1), c ^ (k & 1)
            dst = my_id if slot == "mine" else 4 * px + 2 * py + pc
            return pltpu.make_async_remote_copy(
                src_ref=s_ref, dst_ref=out_ref.at[dst], send_sem=send_sems.at[k - 1], recv_sem=recv_sems.at[k - 1],
                device_id=(px, py, pc), device_id_type=MESH)

        own = pltpu.make_async_copy(s_ref, out_ref.at[my_id], local_sem)
        own.start()
        sends = [copy(k, "mine") for k in range(1, N_DEV)]
        for cp in sends:
            cp.start()
        for k in range(1, N_DEV):
            copy(k, "theirs").wait_recv()
        for cp in sends:
            cp.wait_send()
        own.wait()

    dma7 = pltpu.SemaphoreType.DMA((7,))
    return pl.pallas_call(
        body, out_shape=jax.ShapeDtypeStruct((N_DEV, S, C), small.dtype), in_specs=[_ANY], out_specs=_ANY,
        scratch_shapes=[dma7, dma7, pltpu.SemaphoreType.DMA], name=name)(small)


class PairExchange:
    def __init__(self, bigs, chunks):
        self.shapes = [jax.ShapeDtypeStruct((4,) + b.shape[1:], b.dtype) for b in bigs]
        self.n = len(bigs)
        self.pieces = _pieces(bigs, chunks)
        n_p = len(self.pieces)
        self.sems = [pltpu.SemaphoreType.DMA((4 * n_p,)), pltpu.SemaphoreType.DMA((4 * n_p,))]

    def _copies(self, big_refs, out_refs, sems):
        send_sems, recv_sems = sems
        n_p = len(self.pieces)
        x, y, c = _my_pos()

        def copy(b, p):
            a, rows = self.pieces[p]
            return pltpu.make_async_remote_copy(
                src_ref=big_refs[a].at[2 * b + (1 - c), rows], dst_ref=out_refs[a].at[b, rows],
                send_sem=send_sems.at[b * n_p + p], recv_sem=recv_sems.at[b * n_p + p],
                device_id=(x, y, 1 - c), device_id_type=MESH)

        return [copy(b, p) for b in range(4) for p in range(n_p)]

    def start(self, big_refs, out_refs, sems):
        for cp in self._copies(big_refs, out_refs, sems):
            cp.start()

    def finish(self, big_refs, out_refs, sems):
        cps = self._copies(big_refs, out_refs, sems)
        for cp in cps:
            cp.wait_recv()
        for cp in cps:
            cp.wait_send()


def _standalone(exchange, args, name):
    n = exchange.n

    def body(*refs):
        exchange.start(refs[:n], refs[n:2 * n], refs[2 * n:])
        exchange.finish(refs[:n], refs[n:2 * n], refs[2 * n:])

    return pl.pallas_call(body, out_shape=exchange.shapes, in_specs=[_ANY] * n, out_specs=[_ANY] * n,
                          scratch_shapes=exchange.sems, name=name)(*args)


class Riding:
    def __init__(self, riders):
        self.riders = [(ex, list(args)) for ex, args in riders]
        self.args = [a for _, args in self.riders for a in args]
        self.in_specs = [_ANY] * len(self.args)
        self.out_shapes = [s for ex, _ in self.riders for s in ex.shapes]
        self.out_specs = [_ANY] * len(self.out_shapes)
        self.scratch = [s for ex, _ in self.riders for s in ex.sems]

    def wrap(self, body, n_in, n_out, n_scratch, is_first, is_last):
        def wrapped(*refs):
            k = 0
            core = list(refs[:n_in])
            k = n_in
            r_in = []
            for ex, _ in self.riders:
                r_in.append(refs[k:k + ex.n])
                k += ex.n
            core += refs[k:k + n_out]
            k += n_out
            r_out = []
            for ex, _ in self.riders:
                r_out.append(refs[k:k + ex.n])
                k += ex.n
            core += refs[k:k + n_scratch]
            k += n_scratch
            r_sem = []
            for ex, _ in self.riders:
                r_sem.append(refs[k:k + len(ex.sems)])
                k += len(ex.sems)

            @pl.when(is_first())
            def _():
                for (ex, _), a, b, s in zip(self.riders, r_in, r_out, r_sem):
                    ex.start(a, b, s)

            body(*core)

            @pl.when(is_last())
            def _():
                for (ex, _), a, b, s in zip(self.riders, r_in, r_out, r_sem):
                    ex.finish(a, b, s)

        return wrapped

    def split(self, outs, n_out):
        core, rest, per = list(outs[:n_out]), list(outs[n_out:]), []
        for ex, _ in self.riders:
            per.append(rest[:ex.n])
            rest = rest[ex.n:]
        return core, per


def pair_sum(big, sib, c, *, tr, name):
    _, R, C = big.shape

    def body(c_ref, a_ref, s_ref, o_ref):
        o_ref[...] = (a_ref[...].astype(F32) + s_ref[...].astype(F32)).astype(o_ref.dtype)

    grid_spec = pltpu.PrefetchScalarGridSpec(
        num_scalar_prefetch=1, grid=(4, R // tr),
        in_specs=[pl.BlockSpec((None, tr, C), lambda b, i, c_ref: (2 * b + c_ref[0], i, 0)),
                  pl.BlockSpec((None, tr, C), lambda b, i, c_ref: (b, i, 0))],
        out_specs=pl.BlockSpec((None, tr, C), lambda b, i, c_ref: (b, i, 0)))
    return pl.pallas_call(body, grid_spec=grid_spec, out_shape=jax.ShapeDtypeStruct((4, R, C), big.dtype),
                          compiler_params=_cparams(2), name=name)(c.reshape(1).astype(jnp.int32), big, sib)


class ChipScatter:
    def __init__(self, pres, chunks):
        self.shapes = [jax.ShapeDtypeStruct(p.shape, p.dtype) for p in pres]
        self.n = len(pres)
        self.pieces = _pieces(pres, chunks)
        n_p = len(self.pieces)
        self.sems = [pltpu.SemaphoreType.DMA((3 * n_p,)), pltpu.SemaphoreType.DMA((3 * n_p,)),
                     pltpu.SemaphoreType.DMA((n_p,))]

    def _copies(self, pre_refs, out_refs, sems):
        send_sems, recv_sems, local_sems = sems
        n_p = len(self.pieces)
        x, y, c = _my_pos()
        my_chip = 2 * x + y
        chips = [(1 - x, y), (x, 1 - y), (1 - x, 1 - y)]

        def copy(j, p, slot):
            px, py = chips[j]
            a, rows = self.pieces[p]
            src_slot, dst_slot = (2 * px + py, my_chip) if slot == "mine" else (my_chip, 2 * px + py)
            return pltpu.make_async_remote_copy(
                src_ref=pre_refs[a].at[src_slot, rows], dst_ref=out_refs[a].at[dst_slot, rows],
                send_sem=send_sems.at[j * n_p + p], recv_sem=recv_sems.at[j * n_p + p],
                device_id=(px, py, c), device_id_type=MESH)

        own = [pltpu.make_async_copy(pre_refs[a].at[my_chip, rows], out_refs[a].at[my_chip, rows], local_sems.at[p])
               for p, (a, rows) in enumerate(self.pieces)]
        sends = [copy(j, p, "mine") for j in range(3) for p in range(n_p)]
        recvs = [copy(j, p, "theirs") for j in range(3) for p in range(n_p)]
        return own, sends, recvs

    def start(self, pre_refs, out_refs, sems):
        own, sends, _ = self._copies(pre_refs, out_refs, sems)
        for cp in sends + own:
            cp.start()

    def finish(self, pre_refs, out_refs, sems):
        own, sends, recvs = self._copies(pre_refs, out_refs, sems)
        for cp in recvs:
            cp.wait_recv()
        for cp in sends:
            cp.wait_send()
        for cp in own:
            cp.wait()


def chip_scatter(pres, chunks, *, name):
    cs = ChipScatter(pres, chunks)
    n = cs.n

    def body(*refs):
        pre_refs, out_refs, sems = refs[:n], refs[n:2 * n], refs[2 * n:]
        cs.start(pre_refs, out_refs, sems)
        cs.finish(pre_refs, out_refs, sems)

    return pl.pallas_call(body, out_shape=cs.shapes, in_specs=[_ANY] * n, out_specs=[_ANY] * n,
                          scratch_shapes=cs.sems, name=name)(*pres)


def sibling_send(blks, chunks, *, name):
    n = len(blks)
    pieces = _pieces(blks, chunks)
    n_p = len(pieces)

    def body(*refs):
        x_refs, out_refs = refs[:n], refs[n:2 * n]
        send_sems, recv_sems = refs[2 * n:]
        x, y, c = _my_pos()
        cps = [pltpu.make_async_remote_copy(
            src_ref=x_refs[a].at[rows], dst_ref=out_refs[a].at[rows], send_sem=send_sems.at[p],
            recv_sem=recv_sems.at[p], device_id=(x, y, 1 - c), device_id_type=MESH)
            for p, (a, rows) in enumerate(pieces)]
        for cp in cps:
            cp.start()
        for cp in cps:
            cp.wait_recv()
        for cp in cps:
            cp.wait_send()

    return pl.pallas_call(
        body, out_shape=[jax.ShapeDtypeStruct(b.shape, b.dtype) for b in blks],
        in_specs=[_ANY] * n, out_specs=[_ANY] * n,
        scratch_shapes=[pltpu.SemaphoreType.DMA((n_p,)), pltpu.SemaphoreType.DMA((n_p,))],
        name=name)(*blks)


def reduce_slots(land, *, tr, name):
    n, R, C = land.shape

    def body(l_ref, o_ref):
        acc = l_ref[0].astype(F32)
        for s in range(1, n):
            acc = acc + l_ref[s].astype(F32)
        o_ref[...] = acc

    return pl.pallas_call(
        body, grid=(R // tr,), in_specs=[pl.BlockSpec((n, tr, C), lambda i: (0, i, 0))],
        out_specs=pl.BlockSpec((tr, C), lambda i: (i, 0)), out_shape=jax.ShapeDtypeStruct((R, C), F32),
        compiler_params=_cparams(1), name=name)(land)


TM = 512


def _tk(d):
    return min(d.shape[0], 1024)


def ffn_fwd_fused(x, g, wgu, wd, *, tm, name, riders=None, rider_args=()):
    T, Dm = x.shape
    nb, _, cb = wgu.shape
    nh = nb // 2
    Fd = nh * cb
    nc = riders.n if riders is not None else 0
    n_steps = T // tm
    stage_at = (0, (2 * n_steps) // 3, n_steps - 1, n_steps - 1)

    def body(*refs):
        x_ref, g_ref, wgu_ref, wd_ref = refs[:4]
        r_in = refs[4:4 + nc]
        o_ref, gu_ref, h_ref = refs[4 + nc:7 + nc]
        r_out, sems = refs[7 + nc:7 + 2 * nc], refs[7 + 2 * nc:]

        def ride(which):
            if riders is not None:
                @pl.when(pl.program_id(0) == stage_at[which])
                def _():
                    riders.stage(which, r_in, r_out, sems)

        ride(0)
        ride(1)
        xv = x_ref[...]
        r = lax.rsqrt(jnp.mean(xv * xv, axis=-1, keepdims=True) + RMS_EPS)
        hb = (xv * r * g_ref[...]).astype(BF16)
        h_ref[...] = hb
        acc = jnp.zeros((tm, Dm), F32)
        for jj in range(nh):
            cols = pl.ds(cb * jj, cb)
            gate = _dot(hb, wgu_ref[jj]).astype(BF16)
            up = _dot(hb, wgu_ref[nh + jj]).astype(BF16)
            gu_ref[0, :, cols] = gate
            gu_ref[1, :, cols] = up
            gv = gate.astype(F32)
            act = (gv * _sigmoid(gv) * up.astype(F32)).astype(BF16)
            acc = acc + _dot(act, wd_ref[cols, :])
        o_ref[...] = xv + 0.5 * acc
        ride(2)
        ride(3)

    outs = pl.pallas_call(
        body, grid=(n_steps,),
        in_specs=[pl.BlockSpec((tm, Dm), lambda i: (i, 0)), pl.BlockSpec((1, Dm), lambda i: (0, 0)),
                  pl.BlockSpec((nb, Dm, cb), lambda i: (0, 0, 0)), pl.BlockSpec((Fd, Dm), lambda i: (0, 0))]
        + [_ANY] * nc,
        out_specs=[pl.BlockSpec((tm, Dm), lambda i: (i, 0)), pl.BlockSpec((2, tm, Fd), lambda i: (0, i, 0)),
                   pl.BlockSpec((tm, Dm), lambda i: (i, 0))] + [_ANY] * nc,
        out_shape=[jax.ShapeDtypeStruct((T, Dm), F32), jax.ShapeDtypeStruct((2, T, Fd), BF16),
                   jax.ShapeDtypeStruct((T, Dm), BF16)] + (riders.shapes if nc else []),
        scratch_shapes=riders.sems if nc else [],
        compiler_params=_cparams(1), name=name)(x, g, wgu, wd, *rider_args)
    return outs[0], outs[1], outs[2], list(outs[3:])


def ffn_fwd(x, g, wgu, wd, tag, riders=None, rider_args=()):
    xo, gu, h, rode = ffn_fwd_fused(x, g, wgu, wd, tm=256, name=f"{tag}_fwd", riders=riders, rider_args=rider_args)
    return xo, (x, gu, h), rode


def ffn_bwd(d, saved, g, wgu, wd, tag, ride_bact=None, ride_dwgu=None, before_dx=None):
    x, gu, h = saved
    dgu, dwd, *rode_a = ffn_bwd_act(d, wd, gu, tm=TM, tn=1408, name=f"{tag}_bact", riding=ride_bact)
    dwgu = mm_tn(h, dgu, scale=1.0, a_split=False, b_split=True, tm=TM, tn=1408, tk=_tk(d), out_blocked=True,
                 name=f"{tag}_dwgu", riding=ride_dwgu)
    rode_g = []
    if ride_dwgu is not None:
        dwgu, *rode_g = dwgu
    riding = before_dx(dwgu, dwd) if before_dx is not None else None
    dx, dg, *rode_x = mm_nt_normbwd(dgu, wgu, x, g, d, a_split=True, tm=_tk(d), tk=1408, name=f"{tag}_dx",
                                    riding=riding)
    return dx, dg, dwgu, dwd, (rode_a, rode_g, rode_x)


def _tile2(v):
    return jnp.concatenate([v, v], axis=-1).reshape(1, LANES)


def _fold2(v):
    return v[:, :HEAD_DIM] + v[:, HEAD_DIM:]


EVEN = dict(dil=1, nsub=2, ppk=4, q_blk=0, k_blk=4, v_blk=5, n_heads=A_Q_HEADS, group=A_GROUP, max_dist=A_WINDOW - 1)
STICK = dict(q_blk=6, k_blk=10, v_blk=14, n_pairs=4)


def _odd_cfg(dil):
    return dict(dil=dil, nsub=4 if dil == 1 else 1, ppk=1, q_blk=0, k_blk=8, v_blk=16, n_heads=C_HEADS, group=1,
                max_dist=BLK)


def even_fwd(x, g, win, qg, kg, sinks, wout, tag, riders=None, rider_args=()):
    qkv, h = norm_matmul(x, g, win, tm=_tk(x), tn=1152, split=False, name=f"{tag}_in")
    qg2, kg2 = _tile2(qg), _tile2(kg)
    slopes = jnp.asarray(_alibi(A_Q_HEADS), F32)
    qkn = qk_norm(qkv, qg2, kg2, width=768, steps=1, n_q=4, tm=TM, name=f"{tag}_qkn")
    oa, _, *lse = banded_fwd(qkn, qkv, slopes, sinks, name=f"{tag}_swa", **EVEN)
    ob, rode = stick_fwd(qkv, name=f"{tag}_stick", riders=riders, rider_args=rider_args, **STICK)
    o = jnp.concatenate([oa, ob], axis=1)
    xo = mm_nn(o, wout, res=x, tm=TM, tn=D_MODEL, tk=D_MODEL, name=f"{tag}_out")
    return xo, (x, qkv, qkn, h, oa, lse, o), rode


def even_bwd(d, saved, g, win, qg, kg, sinks, wout, tag, riders=None, rider_args=(), before_dx=None):
    x, qkv, qkn, h, oa, lse, o = saved
    qg2, kg2 = _tile2(qg), _tile2(kg)
    slopes = jnp.asarray(_alibi(A_Q_HEADS), F32)
    dwout = mm_tn(o, d, scale=1.0, a_split=False, b_split=False, tm=D_MODEL, tn=D_MODEL, tk=_tk(d), name=f"{tag}_dwout")
    do = mm_nt(d, wout, tm=TM, tn=D_MODEL, tk=D_MODEL, name=f"{tag}_do")
    dqa, dka4, dva4, dsk = banded_bwd(qkn, qkv, slopes, sinks, do, oa, lse, None, None,
                                      do_blk=0, name=f"{tag}_swa_b", **EVEN)
    dqb, dkb, dvb, rode = stick_bwd(qkv, do, do_blk=4, name=f"{tag}_stick_b", riders=riders, rider_args=rider_args,
                                    **STICK)
    dqkv, dqg, dkg = assemble_even(dqa, dka4, dva4, dqb, dkb, dvb, qkv, qg2, kg2, tm=TM, name=f"{tag}_asm")
    dwin = mm_tn(h, dqkv, scale=1.0, a_split=False, b_split=False, tm=D_MODEL, tn=1152, tk=_tk(d), name=f"{tag}_dwin")
    riding = before_dx(dwin, dwout) if before_dx is not None else None
    dx, dg, *rode_x = mm_nt_normbwd(dqkv, win, x, g, d, a_split=False, tm=TM, tk=1152, name=f"{tag}_dx", riding=riding)
    return dx, dg, dwin, _fold2(dqg), _fold2(dkg), dsk[:, :A_Q_HEADS], dwout, (rode, rode_x)


def odd_fwd(x, g, win, qg, kg, wout, tag):
    qkv, h = norm_matmul(x, g, win, tm=_tk(x), tn=768, split=False, name=f"{tag}_in")
    qg2, kg2 = _tile2(qg), _tile2(kg)
    qkn = qk_norm(qkv, qg2, kg2, width=D_MODEL, steps=2, n_q=8, tm=TM, name=f"{tag}_qkn")
    outs = []
    for p, (window, dil) in enumerate(C_PATTERNS):
        slopes = jnp.asarray(_alibi(C_HEADS), F32) * float(dil)
        outs.append(banded_fwd(qkn, qkv, slopes, None, name=f"{tag}_dil{p}", **_odd_cfg(dil)))
    o, w1, w2, w3 = mix_fwd(outs[0][0], outs[1][0], outs[2][0], outs[0][1], outs[1][1], outs[2][1],
                            tm=TM, name=f"{tag}_mix")
    xo = mm_nn(o, wout, res=x, tm=TM, tn=D_MODEL, tk=D_MODEL, name=f"{tag}_out")
    return xo, (x, qkv, qkn, h, outs, (w1, w2, w3), o)


def odd_bwd(d, saved, g, win, qg, kg, wout, tag):
    x, qkv, qkn, h, outs, ws, o = saved
    qg2, kg2 = _tile2(qg), _tile2(kg)
    dwout = mm_tn(o, d, scale=1.0, a_split=False, b_split=False, tm=D_MODEL, tn=D_MODEL, tk=_tk(d), name=f"{tag}_dwout")
    do = mm_nt(d, wout, tm=TM, tn=D_MODEL, tk=D_MODEL, name=f"{tag}_do")
    parts = []
    for p, (window, dil) in enumerate(C_PATTERNS):
        slopes = jnp.asarray(_alibi(C_HEADS), F32) * float(dil)
        dq, dk, dv, _ = banded_bwd(qkn, qkv, slopes, None, do, None, outs[p][2:], ws[p], o,
                                   do_blk=0, name=f"{tag}_dil{p}_b", **_odd_cfg(dil))
        parts.append((dq, dk, dv))
    dqkv, dqg, dkg = assemble_odd(parts, qkv, qg2, kg2, tm=256, name=f"{tag}_asm")
    dwin = mm_tn(h, dqkv, scale=1.0, a_split=False, b_split=False, tm=TM, tn=768, tk=_tk(d), out_blocked=True,
                 name=f"{tag}_dwin")
    dx, dg = mm_nt_normbwd(dqkv, win, x, g, d, a_split=False, tm=TM, tk=768, name=f"{tag}_dx")
    return dx, dg, dwin, _fold2(dqg), _fold2(dkg), dwout


def xa_fwd(x, mem, g, gm, wq, wkv, qg, kg, wo, tag):
    qraw, h = norm_matmul(x, g, wq, tm=TM, tn=D_MODEL, split=False, name=f"{tag}_q")
    kvraw, hm = norm_matmul(mem, gm, wkv, tm=MEM_LEN, tn=512, split=False, name=f"{tag}_kv")
    o = xattn_fwd(qraw, kvraw, qg, kg, tm=TM, name=f"{tag}_att")
    xo = mm_nn(o, wo, res=x, tm=TM, tn=D_MODEL, tk=D_MODEL, name=f"{tag}_o")
    return xo, (x, qraw, h, kvraw, hm, o)


def xa_bwd(d, saved, mem, g, gm, wq, wkv, qg, kg, wo, tag):
    x, qraw, h, kvraw, hm, o = saved
    dwo = mm_tn(o, d, scale=1.0, a_split=False, b_split=False, tm=D_MODEL, tn=D_MODEL, tk=_tk(d), name=f"{tag}_dwo")
    do = mm_nt(d, wo, tm=TM, tn=D_MODEL, tk=D_MODEL, name=f"{tag}_do")
    dq, dkv, dqg, dkg = xattn_bwd(qraw, kvraw, qg, kg, do, o, tm=TM, name=f"{tag}_att_b")
    dwq = mm_tn(h, dq, scale=1.0, a_split=False, b_split=False, tm=D_MODEL, tn=D_MODEL, tk=_tk(d), name=f"{tag}_dwq")
    dx, dg = mm_nt_normbwd(dq, wq, x, g, d, a_split=False, tm=TM, tk=D_MODEL, name=f"{tag}_dx")
    dwkv = mm_tn(hm, dkv, scale=1.0, a_split=False, b_split=False, tm=TM, tn=512, tk=MEM_LEN, out_blocked=True,
                 name=f"{tag}_dwkv")
    _, dgm = mm_nt_normbwd(dkv, wkv, mem, gm, None, a_split=False, tm=MEM_LEN, tk=512, name=f"{tag}_dmem")
    return dx, dg, dgm, dwq, dwkv, dqg, dkg, dwo


MATS = (("ffn1_w_gu", 1), ("ffn1_w_down", 0), ("ev_w_in", 1), ("ev_w_out", 0), ("od_w_in", 1), ("od_w_out", 0),
        ("xa_w_q", 0), ("xa_w_kv", 1), ("xa_w_o", 0), ("ffn2_w_gu", 1), ("ffn2_w_down", 0))
SMALLS = ("ffn1_norm", "mix_norm", "ev_q_gain", "ev_k_gain", "ev_sinks", "od_q_gain", "od_k_gain", "xa_norm",
          "xa_mem_norm", "xa_q_gain", "xa_k_gain", "ffn2_norm")
WEIGHTS = ("ffn1_norm", "ffn1_w_gu", "ffn1_w_down", "mix_norm", "ev_w_in", "ev_q_gain", "ev_k_gain", "ev_sinks",
           "ev_w_out", "od_w_in", "od_q_gain", "od_k_gain", "od_w_out", "xa_norm", "xa_mem_norm", "xa_w_q",
           "xa_w_kv", "xa_q_gain", "xa_k_gain", "xa_w_o", "ffn2_norm", "ffn2_w_gu", "ffn2_w_down")
SMALL_ROWS = 16
LAYER_GROUPS = (
    (((("ffn1_w_gu", 0), ("ffn2_w_gu", 0)), 4, 512),
     ((("ffn1_w_down", 0), ("ffn2_w_down", 0)), 2, 352),
     ((("ev_w_out", 0), ("xa_w_q", 0), ("xa_w_o", 0)), 1, 384),
     ((("xa_w_kv", 0),), 1, 512),
     ((("ev_w_in", 0),), 1, 512)),
    (((("ffn1_w_gu", 1), ("ffn2_w_gu", 1)), 4, 512),
     ((("ffn1_w_down", 1), ("ffn2_w_down", 1)), 2, 352),
     ((("od_w_out", 0), ("xa_w_q", 1), ("xa_w_o", 1)), 1, 384),
     ((("xa_w_kv", 1),), 1, 512),
     ((("od_w_in", 0),), 1, 512)),
)
GATHER_FIRST = (((("ffn1_w_gu", 0),), 2, 512), ((("ffn1_w_down", 0),), 1, 352), ((("ev_w_out", 0),), 1, 128),
                ((("ev_w_in", 0),), 1, 512))
GATHER_IN_FFN1 = (((("ffn2_w_gu", 0),), 2, 512), ((("ffn2_w_down", 0),), 1, 352),
                  ((("xa_w_q", 0), ("xa_w_o", 0)), 1, 256), ((("xa_w_kv", 0),), 1, 512))
ROUNDS = {
    "1": LAYER_GROUPS[1],
    "0a": (((("ffn2_w_gu", 0),), 2, 512), ((("ffn2_w_down", 0),), 1, 352)) + LAYER_GROUPS[0][2:],
    "0b": (((("ffn1_w_gu", 0),), 2, 512), ((("ffn1_w_down", 0),), 1, 352)),
}


def _chunks_of(groups):
    return tuple(g[1] for g in groups)
COL_SHARDED = {name for name, axis in MATS if axis == 1}
BLOCKED = {"ffn1_w_gu", "ffn2_w_gu", "xa_w_kv", "od_w_in"}


def group_halves(shards, c, groups):
    out = []
    for members, _, _ in groups:
        halves = []
        for name, layer in members:
            _, r, cc = shards[name].shape
            half = lax.dynamic_index_in_dim(shards[name][layer].reshape(2, r // 2, cc), c, 0, keepdims=False)
            halves.append(half.astype(BF16))
        out.append(jnp.concatenate(halves, axis=0))
    return out


def full_weights(gathered, shards, groups):
    full = {}
    for (members, _, _), arr in zip(groups, gathered):
        for w, (name, layer) in enumerate(members):
            _, r, cc = shards[name].shape
            piece = arr[:, w * (r // 2):(w + 1) * (r // 2)].reshape(4, r, cc)
            if name not in COL_SHARDED:
                piece = piece.reshape(4 * r, cc)
            elif name not in BLOCKED:
                piece = piece.transpose(1, 0, 2).reshape(r, 4 * cc)
            full[(name, layer)] = piece
    return full


def group_grads(grads, shards, groups):
    out = []
    for members, _, _ in groups:
        parts = []
        for name, layer in members:
            _, r, cc = shards[name].shape
            gfull = grads[(name, layer)]
            if name in COL_SHARDED and name not in BLOCKED:
                gfull = gfull.reshape(2, r // 2, 4, cc).transpose(2, 0, 1, 3)
            parts.append(gfull.reshape(N_DEV, r // 2, cc))
        out.append(jnp.concatenate(parts, axis=1))
    return out


def shard_grads(mine, theirs, c, shards, groups):
    per = {}
    for (members, _, _), a, b in zip(groups, mine, theirs):
        for w, (name, layer) in enumerate(members):
            _, r, cc = shards[name].shape
            rows = slice(w * (r // 2), (w + 1) * (r // 2))
            lo = jnp.where(c == 0, a[rows], b[rows])
            hi = jnp.where(c == 0, b[rows], a[rows])
            per[(name, layer)] = jnp.concatenate([lo, hi], axis=0)
    return per


def pack_small(vals):
    row10 = jnp.concatenate([vals["xa_q_gain"].reshape(1, 512), vals["xa_k_gain"].reshape(1, 512)], axis=1)
    row11 = jnp.concatenate([vals["ev_q_gain"], vals["ev_k_gain"], vals["od_q_gain"], vals["od_k_gain"],
                             vals["ev_sinks"], jnp.zeros((1, 1024 - 4 * 64 - 8), F32)], axis=1)
    return jnp.concatenate([vals["ffn1_norm"], vals["mix_norm"], vals["xa_norm"], vals["xa_mem_norm"],
                            vals["ffn2_norm"], row10, row11, jnp.zeros((SMALL_ROWS - 12, 1024), F32)], axis=0)


def unpack_small(arr):
    return {"ffn1_norm": arr[0:2], "mix_norm": arr[2:4], "xa_norm": arr[4:6], "xa_mem_norm": arr[6:8],
            "ffn2_norm": arr[8:10],
            "xa_q_gain": arr[10:11, 0:512].reshape(2, 256), "xa_k_gain": arr[10:11, 512:1024].reshape(2, 256),
            "ev_q_gain": arr[11:12, 0:64], "ev_k_gain": arr[11:12, 64:128], "od_q_gain": arr[11:12, 128:192],
            "od_k_gain": arr[11:12, 192:256], "ev_sinks": arr[11:12, 256:264]}


def local_step(x, mem, target, W, small, prereduce, later):
    depth = small["ffn1_norm"].shape[0]

    def row(name, l):
        return small[name][l:l + 1]

    saved = []
    for l in range(depth):
        j = l // 2
        def riding_gather(host):
            if l == 0 and host in later:
                return GatherBlocks(later[host][0], later[host][1]), later[host][0]
            return None, ()

        riders, rider_args = riding_gather("ffn1")
        x, s1, rode = ffn_fwd(x, row("ffn1_norm", l), W[("ffn1_w_gu", l)], W[("ffn1_w_down", l)], f"l{l}_f1",
                              riders=riders, rider_args=rider_args)
        if riders is not None:
            W = {**W, **later["ffn1"][2](rode)}
        if l % 2 == 0:
            riders, rider_args = riding_gather("stick")
            x, s2, rode = even_fwd(x, row("mix_norm", l), W[("ev_w_in", j)], row("ev_q_gain", j),
                                   row("ev_k_gain", j), small["ev_sinks"][j], W[("ev_w_out", j)], f"l{l}_ev",
                                   riders=riders, rider_args=rider_args)
            if riders is not None:
                W = {**W, **later["stick"][2](rode)}
        else:
            x, s2 = odd_fwd(x, row("mix_norm", l), W[("od_w_in", j)], row("od_q_gain", j), row("od_k_gain", j),
                            W[("od_w_out", j)], f"l{l}_od")
        x, s3 = xa_fwd(x, mem, row("xa_norm", l), row("xa_mem_norm", l), W[("xa_w_q", l)], W[("xa_w_kv", l)],
                       row("xa_q_gain", l), row("xa_k_gain", l), W[("xa_w_o", l)], f"l{l}_xa")
        x, s4, _ = ffn_fwd(x, row("ffn2_norm", l), W[("ffn2_w_gu", l)], W[("ffn2_w_down", l)], f"l{l}_f2")
        saved.append((s1, s2, s3, s4))
    loss, d = loss_kernel(x, target, tm=TM, name="loss")

    gw = {}
    gs = {name: [None] * small[name].shape[0] for name in SMALLS}
    pending, landed = None, {}
    for l in reversed(range(depth)):
        j = l // 2
        s1, s2, s3, s4 = saved[l]
        d, dg, dwgu, dwd, _ = ffn_bwd(d, s4, row("ffn2_norm", l), W[("ffn2_w_gu", l)], W[("ffn2_w_down", l)],
                                      f"l{l}_f2")
        gs["ffn2_norm"][l] = dg
        gw[("ffn2_w_gu", l)], gw[("ffn2_w_down", l)] = dwgu, dwd
        d, dg, dgm, dwq, dwkv, dqg, dkg, dwo = xa_bwd(
            d, s3, mem, row("xa_norm", l), row("xa_mem_norm", l), W[("xa_w_q", l)], W[("xa_w_kv", l)],
            row("xa_q_gain", l), row("xa_k_gain", l), W[("xa_w_o", l)], f"l{l}_xa")
        gs["xa_norm"][l], gs["xa_mem_norm"][l], gs["xa_q_gain"][l], gs["xa_k_gain"][l] = dg, dgm, dqg, dkg
        gw[("xa_w_q", l)], gw[("xa_w_kv", l)], gw[("xa_w_o", l)] = dwq, dwkv, dwo
        split = l == 0 and l % 2 == 0 and ("0a" in ROUNDS)
        early = []
        pre_early = None
        if l % 2 == 0:
            riders, rider_args = None, ()
            if pending is not None:
                riders, rider_args = ChipScatter(pending[1], _chunks_of(ROUNDS[pending[0]])), pending[1]

            def before_mixer_dx(dwin, dwout, j=j, early=early):
                gw[("ev_w_in", j)], gw[("ev_w_out", j)] = dwin, dwout
                early += prereduce.pack(gw, "0a")
                return Riding([(PairExchange(early, _chunks_of(ROUNDS["0a"])), early)])

            d, dg, dwin, dqg, dkg, dsk, dwout, (rode, rode_x) = even_bwd(
                d, s2, row("mix_norm", l), W[("ev_w_in", j)], row("ev_q_gain", j), row("ev_k_gain", j),
                small["ev_sinks"][j], W[("ev_w_out", j)], f"l{l}_ev", riders=riders, rider_args=rider_args,
                before_dx=before_mixer_dx if split else None)
            if pending is not None:
                landed[pending[0]], pending = rode, None
            gs["ev_q_gain"][j], gs["ev_k_gain"][j], gs["ev_sinks"][j] = dqg, dkg, dsk
            gw[("ev_w_in", j)], gw[("ev_w_out", j)] = dwin, dwout
            if split:
                pre_early = prereduce.sums(early, rode_x[0], "0a")
        else:
            d, dg, dwin, dqg, dkg, dwout = odd_bwd(
                d, s2, row("mix_norm", l), W[("od_w_in", j)], row("od_q_gain", j), row("od_k_gain", j),
                W[("od_w_out", j)], f"l{l}_od")
            gs["od_q_gain"][j], gs["od_k_gain"][j] = dqg, dkg
            gw[("od_w_in", j)], gw[("od_w_out", j)] = dwin, dwout
        gs["mix_norm"][l] = dg
        packed = []

        rnd = "0b" if pre_early is not None else str(l)

        def before_dx(dwgu, dwd, l=l, packed=packed, rnd=rnd):
            gw[("ffn1_w_gu", l)], gw[("ffn1_w_down", l)] = dwgu, dwd
            packed += prereduce.pack(gw, rnd)
            return Riding([(PairExchange(packed, _chunks_of(ROUNDS[rnd])), packed)])

        ride_bact = ride_dwgu = None
        if pre_early is not None:
            chunks = _chunks_of(ROUNDS["0a"])
            ride_bact = Riding([(ChipScatter(pre_early[:2], chunks[:2]), pre_early[:2])])
            ride_dwgu = Riding([(ChipScatter(pre_early[2:], chunks[2:]), pre_early[2:])])
        d, dg, dwgu, dwd, (rode_a, rode_g, rode_x) = ffn_bwd(
            d, s1, row("ffn1_norm", l), W[("ffn1_w_gu", l)], W[("ffn1_w_down", l)], f"l{l}_f1",
            ride_bact=ride_bact, ride_dwgu=ride_dwgu, before_dx=before_dx)
        gs["ffn1_norm"][l] = dg
        if pre_early is not None:
            landed["0a"] = list(rode_a[0]) + list(rode_g[0])
        if pending is not None:
            landed[pending[0]] = chip_scatter(pending[1], _chunks_of(ROUNDS[pending[0]]),
                                              name=f"scatter_grads{pending[0]}")
        pending = (rnd, prereduce.sums(packed, rode_x[0], rnd))
    landed[pending[0]] = chip_scatter(pending[1], _chunks_of(ROUNDS[pending[0]]), name=f"scatter_grads{pending[0]}")
    gsmall = {name: jnp.concatenate(v, axis=0) for name, v in gs.items()}
    return loss, d, landed, gsmall


def kernel(x, mem, ffn1_norm, ffn1_w_gu, ffn1_w_down, mix_norm, ev_w_in, ev_q_gain, ev_k_gain, ev_sinks, ev_w_out, od_w_in, od_q_gain, od_k_gain, od_w_out, xa_norm, xa_mem_norm, xa_w_q, xa_w_kv, xa_q_gain, xa_k_gain, xa_w_o, ffn2_norm, ffn2_w_gu, ffn2_w_down, loss_target, m_ffn1_norm, m_ffn1_w_gu, m_ffn1_w_down, m_mix_norm, m_ev_w_in, m_ev_q_gain, m_ev_k_gain, m_ev_sinks, m_ev_w_out, m_od_w_in, m_od_q_gain, m_od_k_gain, m_od_w_out, m_xa_norm, m_xa_mem_norm, m_xa_w_q, m_xa_w_kv, m_xa_q_gain, m_xa_k_gain, m_xa_w_o, m_ffn2_norm, m_ffn2_w_gu, m_ffn2_w_down, v_ffn1_norm, v_ffn1_w_gu, v_ffn1_w_down, v_mix_norm, v_ev_w_in, v_ev_q_gain, v_ev_k_gain, v_ev_sinks, v_ev_w_out, v_od_w_in, v_od_q_gain, v_od_k_gain, v_od_w_out, v_xa_norm, v_xa_mem_norm, v_xa_w_q, v_xa_w_kv, v_xa_q_gain, v_xa_k_gain, v_xa_w_o, v_ffn2_norm, v_ffn2_w_gu, v_ffn2_w_down):
    given = dict(locals())
    w = {n: given[n] for n in WEIGHTS}
    m = {n: given["m_" + n] for n in WEIGHTS}
    v = {n: given["v_" + n] for n in WEIGHTS}
    c = lax.axis_index("c")
    shards = {name: w[name] for name, _ in MATS}
    small = {n: w[n] for n in SMALLS}

    def gathering(groups):
        return group_halves(shards, c, groups), _chunks_of(groups), lambda got: full_weights(got, shards, groups)

    halves, chunks, unpack = gathering(GATHER_FIRST)
    full = unpack(gather_blocks(halves, chunks, name="gather_weights0"))
    later = {"ffn1": gathering(GATHER_IN_FFN1), "stick": gathering(LAYER_GROUPS[1])}

    class prereduce:
        @staticmethod
        def pack(gw, rnd):
            return group_grads(gw, shards, ROUNDS[rnd])

        @staticmethod
        def sums(packed, sib, rnd):
            return [pair_sum(p, s, c, tr=g[2], name=f"pair_sum{rnd}_{i}")
                    for i, (g, p, s) in enumerate(zip(ROUNDS[rnd], packed, sib))]

    loss_b, grad_x, landed, gsmall = local_step(x[0], mem[0], loss_target[0], full, small, prereduce, later)

    per = {}
    for rnd, land in sorted(landed.items()):
        groups = ROUNDS[rnd]
        mine = [reduce_slots(a, tr=g[2], name=f"sum_grads{rnd}_{i}") for i, (g, a) in enumerate(zip(groups, land))]
        theirs = sibling_send(mine, _chunks_of(groups), name=f"swap_grads{rnd}")
        per.update(shard_grads(mine, theirs, c, shards, groups))
    g = {name: jnp.stack([per[(name, layer)] for layer in range(w[name].shape[0])], axis=0) for name, _ in MATS}
    land_small = gather_small(pack_small(gsmall), name="gather_small")
    g_small = unpack_small(reduce_slots(land_small, tr=SMALL_ROWS, name="sum_small"))
    g.update(g_small)

    delta, new_m, new_v = {}, {}, {}
    for name, _ in MATS:
        shp = w[name].shape
        flat = [a.reshape(-1, shp[-1]) for a in (w[name], g[name], m[name], v[name])]
        dl, nm, nv = adamw(*flat, br=BLK, name=f"adamw_{name}")
        delta[name], new_m[name], new_v[name] = dl.reshape(shp), nm.reshape(shp), nv.reshape(shp)
    dl, nm, nv = adamw(pack_small(small), pack_small(g_small), pack_small({n: m[n] for n in SMALLS}),
                       pack_small({n: v[n] for n in SMALLS}), br=SMALL_ROWS, name="adamw_small")
    for dst, arr in ((delta, dl), (new_m, nm), (new_v, nv)):
        dst.update(unpack_small(arr))

    loss = lax.psum(loss_b[0, 0], ("x", "y", "c"))
    return (loss, grad_x[None], *[g[n] for n in WEIGHTS], *[delta[n] for n in WEIGHTS],
            *[new_m[n] for n in WEIGHTS], *[new_v[n] for n in WEIGHTS])
```

```python
import jax
import jax.numpy as jnp
from jax import lax
from jax.experimental import pallas as pl
from jax.experimental.pallas import tpu as pltpu

F32 = jnp.float32
BF16 = jnp.bfloat16

D_MODEL = 1024
HEAD_DIM = 64
LANES = 128
BLK = 128
D_FF = 2816
RMS_EPS = 1e-6
MEM_LEN = 256
X_HEADS = 4
X_HEAD_DIM = 256
A_Q_HEADS = 8
A_GROUP = 4
A_WINDOW = 128
C_HEADS = 16
C_PATTERNS = ((128, 1), (512, 4), (2048, 16))
NEG = -1e30
VMEM_LIMIT = 56 * 2 ** 20

ADAM_LR = 0.001
ADAM_B1 = 0.9
ADAM_B2 = 0.999
ADAM_EPS = 1e-08
ADAM_WD = 0.01
ADAM_STEP = 10

N_DEV = 8
MESH = pl.DeviceIdType.MESH


def _cparams(n):
    return pltpu.CompilerParams(dimension_semantics=("arbitrary",) * n, vmem_limit_bytes=VMEM_LIMIT)


def _dot(a, b):
    return jnp.dot(a, b, preferred_element_type=F32)


def _dot_nt(a, b):
    return lax.dot_general(a, b, (((1,), (1,)), ((), ())), preferred_element_type=F32)


def _dot_tn(a, b):
    return lax.dot_general(a, b, (((0,), (0,)), ((), ())), preferred_element_type=F32)


def _sigmoid(z):
    return 1.0 / (1.0 + jnp.exp(-z))


def norm_matmul(x, g, w, *, tm, tn, split, name):
    T, K = x.shape
    blocked = w.ndim == 3
    assert not blocked or w.shape[2] == tn
    N = w.shape[0] * w.shape[2] if blocked else w.shape[1]
    nj = N // tn

    def body(x_ref, g_ref, w_ref, o_ref, h_ref):
        @pl.when(pl.program_id(1) == 0)
        def _():
            xv = x_ref[...]
            r = lax.rsqrt(jnp.mean(xv * xv, axis=-1, keepdims=True) + RMS_EPS)
            h_ref[...] = (xv * r * g_ref[...]).astype(BF16)

        o_ref[...] = _dot(h_ref[...], w_ref[...]).astype(o_ref.dtype)

    if split:
        njh = nj // 2
        o_shape = jax.ShapeDtypeStruct((2, T, N // 2), BF16)
        o_spec = pl.BlockSpec((None, tm, tn), lambda i, j: (j // njh, i, j % njh))
    else:
        o_shape = jax.ShapeDtypeStruct((T, N), F32)
        o_spec = pl.BlockSpec((tm, tn), lambda i, j: (i, j))
    return pl.pallas_call(
        body, grid=(T // tm, nj),
        in_specs=[pl.BlockSpec((tm, K), lambda i, j: (i, 0)),
                  pl.BlockSpec((1, K), lambda i, j: (0, 0)),
                  (pl.BlockSpec((None, K, tn), lambda i, j: (j, 0, 0)) if blocked
                   else pl.BlockSpec((K, tn), lambda i, j: (0, j)))],
        out_specs=[o_spec, pl.BlockSpec((tm, K), lambda i, j: (i, 0))],
        out_shape=[o_shape, jax.ShapeDtypeStruct((T, K), BF16)],
        compiler_params=_cparams(2), name=name)(x, g, w)


def mm_nn(a, b, *, res, tm, tn, tk, name):
    T = a.shape[0]
    K, N = b.shape
    nk = K // tk

    def body(a_ref, b_ref, r_ref, o_ref, acc):
        k = pl.program_id(2)

        @pl.when(k == 0)
        def _():
            acc[...] = jnp.zeros_like(acc)

        acc[...] += _dot(a_ref[...].astype(BF16), b_ref[...])

        @pl.when(k == nk - 1)
        def _():
            o_ref[...] = r_ref[...] + acc[...]

    return pl.pallas_call(
        body, grid=(T // tm, N // tn, nk),
        in_specs=[pl.BlockSpec((tm, tk), lambda i, j, k: (i, k)), pl.BlockSpec((tk, tn), lambda i, j, k: (k, j)),
                  pl.BlockSpec((tm, tn), lambda i, j, k: (i, j))],
        out_specs=pl.BlockSpec((tm, tn), lambda i, j, k: (i, j)),
        out_shape=jax.ShapeDtypeStruct((T, N), F32),
        scratch_shapes=[pltpu.VMEM((tm, tn), F32)],
        compiler_params=_cparams(3), name=name)(a, b, res)


def mm_nt(a, b, *, tm, tn, tk, name):
    T, K = a.shape
    N = b.shape[0]
    nk = K // tk

    def body(a_ref, b_ref, o_ref, acc):
        k = pl.program_id(2)

        @pl.when(k == 0)
        def _():
            acc[...] = jnp.zeros_like(acc)

        acc[...] += _dot_nt(a_ref[...].astype(BF16), b_ref[...])

        @pl.when(k == nk - 1)
        def _():
            o_ref[...] = acc[...]

    return pl.pallas_call(
        body, grid=(T // tm, N // tn, nk),
        in_specs=[pl.BlockSpec((tm, tk), lambda i, j, k: (i, k)),
                  pl.BlockSpec((tn, tk), lambda i, j, k: (j, k))],
        out_specs=pl.BlockSpec((tm, tn), lambda i, j, k: (i, j)),
        out_shape=jax.ShapeDtypeStruct((T, N), F32),
        scratch_shapes=[pltpu.VMEM((tm, tn), F32)],
        compiler_params=_cparams(3), name=name)(a, b)


def ffn_bwd_act(d, wd, gu, *, tm, tn, name, riding=None):
    T, K = d.shape
    Fd = wd.shape[0]
    ni = T // tm

    def body(d_ref, w_ref, g_ref, u_ref, dgu_ref, dwd_ref, acc):
        i = pl.program_id(1)

        @pl.when(i == 0)
        def _():
            acc[...] = jnp.zeros_like(acc)

        db = d_ref[...].astype(BF16)
        da = 0.5 * _dot_nt(db, w_ref[...])
        gv = g_ref[...].astype(F32)
        uv = u_ref[...].astype(F32)
        s = _sigmoid(gv)
        silu = gv * s
        acc[...] += _dot_tn((silu * uv).astype(BF16), db)
        dgu_ref[0] = (da * uv * (s * (1.0 + gv * (1.0 - s)))).astype(BF16)
        dgu_ref[1] = (da * silu).astype(BF16)

        @pl.when(i == ni - 1)
        def _():
            dwd_ref[...] = (0.5 * acc[...]).astype(BF16)

    in_specs = [pl.BlockSpec((tm, K), lambda j, i: (i, 0)),
                pl.BlockSpec((tn, K), lambda j, i: (j, 0)),
                pl.BlockSpec((None, tm, tn), lambda j, i: (0, i, j)),
                pl.BlockSpec((None, tm, tn), lambda j, i: (1, i, j))]
    out_specs = [pl.BlockSpec((2, tm, tn), lambda j, i: (0, i, j)), pl.BlockSpec((tn, K), lambda j, i: (j, 0))]
    out_shape = [jax.ShapeDtypeStruct((2, T, Fd), BF16), jax.ShapeDtypeStruct((Fd, K), BF16)]
    return _call_with_riders(body, riding, (Fd // tn, ni), in_specs, out_specs, out_shape,
                             [pltpu.VMEM((tn, K), F32)], [d, wd, gu, gu], name)


def _call_with_riders(body, riding, grid, in_specs, out_specs, out_shape, scratch, args, name):
    n_out = len(out_shape)
    if riding is None:
        return pl.pallas_call(body, grid=grid, in_specs=in_specs, out_specs=out_specs, out_shape=out_shape,
                              scratch_shapes=scratch, compiler_params=_cparams(len(grid)), name=name)(*args)

    def is_first():
        ok = pl.program_id(0) == 0
        for ax in range(1, len(grid)):
            ok = ok & (pl.program_id(ax) == 0)
        return ok

    def is_last():
        ok = pl.program_id(0) == grid[0] - 1
        for ax in range(1, len(grid)):
            ok = ok & (pl.program_id(ax) == grid[ax] - 1)
        return ok

    outs = pl.pallas_call(
        riding.wrap(body, len(in_specs), n_out, len(scratch), is_first, is_last), grid=grid,
        in_specs=list(in_specs) + riding.in_specs, out_specs=list(out_specs) + riding.out_specs,
        out_shape=list(out_shape) + riding.out_shapes, scratch_shapes=list(scratch) + riding.scratch,
        compiler_params=_cparams(len(grid)), name=name)(*args, *riding.args)
    core, per = riding.split(outs, n_out)
    return (*core, *per)


def mm_nt_normbwd(a, b, x, g, res, *, a_split, tm, tk, name, riding=None):
    T, Dm = x.shape
    blocked = b.ndim == 3
    assert not blocked or b.shape[2] == tk
    K = b.shape[0] * b.shape[2] if blocked else b.shape[1]
    nk = K // tk
    nkh = nk // 2
    has_res = res is not None

    def body(*refs):
        if has_res:
            a_ref, b_ref, x_ref, g_ref, r_ref, dx_ref, dg_ref, acc = refs
        else:
            a_ref, b_ref, x_ref, g_ref, dx_ref, dg_ref, acc = refs
        i = pl.program_id(0)
        k = pl.program_id(1)

        @pl.when(k == 0)
        def _():
            acc[...] = jnp.zeros_like(acc)

        acc[...] += _dot_nt(a_ref[...].astype(BF16), b_ref[...])

        @pl.when(k == nk - 1)
        def _():
            xv = x_ref[...]
            r = lax.rsqrt(jnp.mean(xv * xv, axis=-1, keepdims=True) + RMS_EPS)
            xh = xv * r
            dh = acc[...]
            dxh = dh * g_ref[...]
            dx = r * (dxh - xh * jnp.mean(dxh * xh, axis=-1, keepdims=True))
            if has_res:
                dx = dx + r_ref[...]
            dx_ref[...] = dx
            part = jnp.sum(dh * xh, axis=0, keepdims=True)

            @pl.when(i == 0)
            def _():
                dg_ref[...] = part

            @pl.when(i > 0)
            def _():
                dg_ref[...] += part

    if a_split:
        a_spec = pl.BlockSpec((None, tm, tk), lambda i, k: (k // nkh, i, k % nkh))
    else:
        a_spec = pl.BlockSpec((tm, tk), lambda i, k: (i, k))
    in_specs = [a_spec,
                (pl.BlockSpec((None, Dm, tk), lambda i, k: (k, 0, 0)) if blocked
                 else pl.BlockSpec((Dm, tk), lambda i, k: (0, k))),
                pl.BlockSpec((tm, Dm), lambda i, k: (i, 0)),
                pl.BlockSpec((1, Dm), lambda i, k: (0, 0))]
    args = [a, b, x, g]
    if has_res:
        in_specs.append(pl.BlockSpec((tm, Dm), lambda i, k: (i, 0)))
        args.append(res)
    out_specs = [pl.BlockSpec((tm, Dm), lambda i, k: (i, 0)), pl.BlockSpec((1, Dm), lambda i, k: (0, 0))]
    out_shape = [jax.ShapeDtypeStruct((T, Dm), F32), jax.ShapeDtypeStruct((1, Dm), F32)]
    scratch = [pltpu.VMEM((tm, Dm), F32)]
    return _call_with_riders(body, riding, (T // tm, nk), in_specs, out_specs, out_shape, scratch, args, name)


def mm_tn(a, b, *, scale, a_split, b_split, tm, tn, tk, name, out_blocked=False, riding=None):
    T = a.shape[-2]
    M = a.shape[-1] * (2 if a_split else 1)
    N = b.shape[-1] * (2 if b_split else 1)
    ni, nj, nk = M // tm, N // tn, T // tk
    nih, njh = ni // 2, nj // 2

    def body(a_ref, b_ref, o_ref, acc):
        k = pl.program_id(2)

        @pl.when(k == 0)
        def _():
            acc[...] = jnp.zeros_like(acc)

        acc[...] += _dot_tn(a_ref[...].astype(BF16), b_ref[...].astype(BF16))

        @pl.when(k == nk - 1)
        def _():
            o_ref[...] = (acc[...] * scale).astype(o_ref.dtype)

    if a_split:
        a_spec = pl.BlockSpec((None, tk, tm), lambda i, j, k: (i // nih, k, i % nih))
    else:
        a_spec = pl.BlockSpec((tk, tm), lambda i, j, k: (k, i))
    if b_split:
        b_spec = pl.BlockSpec((None, tk, tn), lambda i, j, k: (j // njh, k, j % njh))
    else:
        b_spec = pl.BlockSpec((tk, tn), lambda i, j, k: (k, j))
    if out_blocked:
        o_spec = pl.BlockSpec((None, None, tm, tn), lambda i, j, k: (j, i, 0, 0))
        o_shape = jax.ShapeDtypeStruct((nj, ni, tm, tn), BF16)
    else:
        o_spec = pl.BlockSpec((tm, tn), lambda i, j, k: (i, j))
        o_shape = jax.ShapeDtypeStruct((M, N), BF16)
    outs = _call_with_riders(body, riding, (ni, nj, nk), [a_spec, b_spec], [o_spec], [o_shape],
                             [pltpu.VMEM((tm, tn), F32)], [a, b], name)
    return outs[0] if riding is None else tuple(outs)


def loss_kernel(y, target, *, tm, name):
    T, Dm = y.shape

    def body(y_ref, t_ref, l_ref, dy_ref):
        e = y_ref[...] - t_ref[...]
        dy_ref[...] = e * (1.0 / Dm)
        part = (0.5 / Dm) * jnp.sum(jnp.sum(e * e, axis=-1, keepdims=True), axis=0, keepdims=True)
        part = jnp.broadcast_to(part, (8, LANES))

        @pl.when(pl.program_id(0) == 0)
        def _():
            l_ref[...] = part

        @pl.when(pl.program_id(0) > 0)
        def _():
            l_ref[...] += part

    return pl.pallas_call(
        body, grid=(T // tm,),
        in_specs=[pl.BlockSpec((tm, Dm), lambda i: (i, 0)), pl.BlockSpec((tm, Dm), lambda i: (i, 0))],
        out_specs=[pl.BlockSpec((8, LANES), lambda i: (0, 0)), pl.BlockSpec((tm, Dm), lambda i: (i, 0))],
        out_shape=[jax.ShapeDtypeStruct((8, LANES), F32), jax.ShapeDtypeStruct((T, Dm), F32)],
        compiler_params=_cparams(1), name=name)(y, target)


def adamw(w, g, m, v, *, br, name):
    R, C = w.shape

    def body(w_ref, g_ref, m_ref, v_ref, d_ref, nm_ref, nv_ref):
        gv = g_ref[...]
        nm = ADAM_B1 * m_ref[...] + (1.0 - ADAM_B1) * gv
        nv = ADAM_B2 * v_ref[...] + (1.0 - ADAM_B2) * (gv * gv)
        m_hat = nm / (1.0 - ADAM_B1 ** ADAM_STEP)
        v_hat = nv / (1.0 - ADAM_B2 ** ADAM_STEP)
        d_ref[...] = -ADAM_LR * (m_hat / (jnp.sqrt(v_hat) + ADAM_EPS) + ADAM_WD * w_ref[...])
        nm_ref[...] = nm
        nv_ref[...] = nv

    spec = pl.BlockSpec((br, C), lambda i: (i, 0))
    shp = jax.ShapeDtypeStruct((R, C), F32)
    return pl.pallas_call(
        body, grid=(R // br,), in_specs=[spec] * 4, out_specs=[spec] * 3, out_shape=[shp] * 3,
        compiler_params=_cparams(1), name=name)(w, g, m, v)


def _lane0():
    return lax.broadcasted_iota(jnp.int32, (1, LANES), 1) < HEAD_DIM


def _half_sum(x, m0):
    s0 = jnp.sum(jnp.where(m0, x, 0.0), axis=-1, keepdims=True)
    s1 = jnp.sum(jnp.where(m0, 0.0, x), axis=-1, keepdims=True)
    return jnp.where(m0, s0, s1)


def _head_rms(x, m0):
    return lax.rsqrt(_half_sum(x * x, m0) * (1.0 / HEAD_DIM) + RMS_EPS)


def _alibi(n):
    return [float(2.0 ** (-8.0 * (h + 1) / n)) for h in range(n)]


def _mask_half(x, m0, e):
    return jnp.where(m0, x, 0.0) if e == 0 else jnp.where(m0, 0.0, x)


def _band_masks2(max_dist, has_prev, live):
    row = lax.broadcasted_iota(jnp.int32, (2 * BLK, 2 * BLK), 0)
    col = lax.broadcasted_iota(jnp.int32, (2 * BLK, 2 * BLK), 1)
    dist = (row & (BLK - 1)) - col + BLK
    lim = jnp.where(live, max_dist, -1)
    first = jnp.where(has_prev, 0, BLK)
    valid = (dist >= 0) & (dist <= lim) & (col >= first)
    top = lax.broadcasted_iota(jnp.int32, (2 * BLK, 1), 0) < BLK
    return dist.astype(F32), valid, top


def _stack_heads(x, m0, kes):
    parts = []
    for e in range(2):
        h = _mask_half(x, m0, e)
        parts.append(pltpu.roll(h, HEAD_DIM, 1) if kes[e] != e else h)
    return jnp.concatenate(parts, axis=0)


def _unstack_heads(y, m0, kes):
    parts = []
    for e in range(2):
        h = y[e * BLK:(e + 1) * BLK]
        parts.append(pltpu.roll(h, HEAD_DIM, 1) if kes[e] != e else h)
    return jnp.where(m0, parts[0], parts[1])


def _rows(r, dil):
    return pl.ds(r, BLK, stride=dil) if dil > 1 else pl.ds(0, BLK)


def _band_units(dil, nsub):
    assert dil == 1 or nsub == 1
    if nsub == 1:
        return [(_rows(r, dil), ("prev", _rows(r, dil)), 0) for r in range(dil)]
    units = [(pl.ds(0, BLK), ("prev", pl.ds(0, BLK)), 0)]
    units += [(pl.ds(BLK * s, BLK), ("cur", pl.ds(BLK * (s - 1), BLK)), s) for s in range(1, nsub)]
    return units


def _head_col_spec(ppk, RB, row_block):
    if ppk == 1:
        return pl.BlockSpec((None, RB, 1), lambda p, i: (p, row_block(i), 0))
    return pl.BlockSpec((ppk, RB, 1), lambda p, i: (p, row_block(i), 0))


def _band_specs(dil, nsub, ppk, q_blk, k_blk, v_blk, kv_shared, nb):
    RB = BLK * dil * nsub
    PB = BLK if nsub > 1 else RB
    qw = LANES * ppk
    kw = LANES if kv_shared else qw

    def cur(i):
        return jnp.minimum(i, nb - 1)

    def prev(i):
        return jnp.maximum(i * nsub - 1, 0) if nsub > 1 else jnp.maximum(i - 1, 0)

    def kidx(base):
        return (lambda p, i: (cur(i), base)) if kv_shared else (lambda p, i: (cur(i), base + p))

    def pidx(base):
        return (lambda p, i: (prev(i), base)) if kv_shared else (lambda p, i: (prev(i), base + p))

    return [pl.BlockSpec((RB, qw), lambda p, i: (cur(i), q_blk + p)),
            pl.BlockSpec((RB, kw), kidx(k_blk)), pl.BlockSpec((PB, kw), pidx(k_blk)),
            pl.BlockSpec((RB, kw), kidx(v_blk)), pl.BlockSpec((PB, kw), pidx(v_blk))]


def qk_norm(qkv, q_gain2, k_gain2, *, width, steps, n_q, tm, name):
    T = qkv.shape[0]
    nsb = width // LANES

    def body(x_ref, qg_ref, kg_ref, o_ref):
        m0 = _lane0()
        for b in range(nsb):
            is_q = ((pl.program_id(1) * nsb + b) < n_q).astype(F32)
            gain = qg_ref[...] * is_q + kg_ref[...] * (1.0 - is_q)
            cols = pl.ds(LANES * b, LANES)
            xv = x_ref[:, cols]
            o_ref[:, cols] = xv * _head_rms(xv, m0) * gain

    gspec = pl.BlockSpec((1, LANES), lambda i, j: (0, 0))
    return pl.pallas_call(
        body, grid=(T // tm, steps),
        in_specs=[pl.BlockSpec((tm, width), lambda i, j: (i, j)), gspec, gspec],
        out_specs=pl.BlockSpec((tm, width), lambda i, j: (i, j)),
        out_shape=jax.ShapeDtypeStruct((T, width * steps), F32),
        compiler_params=_cparams(2), name=name)(qkv, q_gain2, k_gain2)


def banded_fwd(qkn, qkv, slopes, sinks, *, dil, nsub, ppk, q_blk, k_blk, v_blk, n_heads, group,
               max_dist, name):
    T = qkv.shape[0]
    RB = BLK * dil * nsub
    nb = T // RB
    npair = n_heads // 2
    kv_shared = group > 1
    scale = HEAD_DIM ** -0.5
    has_sink = sinks is not None

    def body(*refs):
        slope_ref = refs[0]
        if has_sink:
            sink_ref, refs = refs[1], refs[2:]
        else:
            refs = refs[1:]
        q_ref, kc_ref, kp_ref, vc_ref, vp_ref, o_ref, l_ref, lc0_ref, lc1_ref = refs
        pb = pl.program_id(0)
        i = pl.program_id(1)
        m0 = _lane0()
        distf, valid_first, top = _band_masks2(max_dist, i > 0, i >= 0)
        valid_inner = _band_masks2(max_dist, i >= 0, i >= 0)[1] if nsub > 1 else None
        for rows, (src, prows), sub in _band_units(dil, nsub):
            valid = valid_first if sub == 0 else valid_inner
            kpr, vpr = (kp_ref, vp_ref) if src == "prev" else (kc_ref, vc_ref)
            kcache = {}
            for jp in range(ppk):
                cs = pl.ds(LANES * jp, LANES)
                jk = 0 if kv_shared else jp
                if jk not in kcache:
                    ks = pl.ds(LANES * jk, LANES)
                    kcat = jnp.concatenate([kpr[prows, ks], kc_ref[rows, ks]], axis=0)
                    vcat = jnp.concatenate([vpr[prows, ks], vc_ref[rows, ks]], axis=0)
                    kcache[jk] = (kcat.astype(BF16), vcat.astype(BF16))
                kn, vcat = kcache[jk]
                qn = q_ref[rows, cs]
                kes = [((2 * jp + e) // group) % 2 if kv_shared else e for e in range(2)]
                hidx = 2 * (pb * ppk + jp)
                qs = _stack_heads(qn, m0, kes).astype(BF16)
                slope = jnp.where(top, slope_ref[hidx], slope_ref[hidx + 1])
                s = jnp.where(valid, _dot_nt(qs, kn) * scale - slope * distf, NEG)
                m = jnp.max(s, axis=-1, keepdims=True)
                if has_sink:
                    sk = jnp.where(top, sink_ref[hidx], sink_ref[hidx + 1])
                    m = jnp.maximum(m, sk)
                p = jnp.exp(s - m)
                den = jnp.sum(p, axis=-1, keepdims=True)
                if has_sink:
                    den = den + jnp.exp(sk - m)
                o_full = _dot((p * (1.0 / den)).astype(BF16), vcat)
                o_ref[rows, cs] = _unstack_heads(o_full, m0, kes)
                lse = m + jnp.log(den)
                l_ref[rows, cs] = _unstack_heads(jnp.broadcast_to(lse, (2 * BLK, LANES)), m0, [0, 1])
                for e, lc_ref in enumerate((lc0_ref, lc1_ref)):
                    if ppk == 1:
                        lc_ref[rows, :] = lse[e * BLK:(e + 1) * BLK]
                    else:
                        lc_ref[jp, rows, :] = lse[e * BLK:(e + 1) * BLK]

    smem = pl.BlockSpec(memory_space=pltpu.SMEM)
    qw = LANES * ppk
    ospec = pl.BlockSpec((RB, qw), lambda p, i: (i, p))
    oshape = jax.ShapeDtypeStruct((T, n_heads * HEAD_DIM), F32)
    args = [slopes] + ([sinks] if has_sink else []) + [qkn] * 3 + [qkv] * 2
    return pl.pallas_call(
        body, grid=(npair // ppk, nb),
        in_specs=[smem] * (2 if has_sink else 1) + _band_specs(dil, nsub, ppk, q_blk, k_blk, v_blk, kv_shared, nb),
        out_specs=[ospec, ospec] + [_head_col_spec(ppk, RB, lambda i: i)] * 2,
        out_shape=[oshape, oshape] + [jax.ShapeDtypeStruct((npair, T, 1), F32)] * 2,
        compiler_params=_cparams(2), name=name)(*args)


def banded_bwd(qkn, qkv, slopes, sinks, do, o, lsec, w, omix, *, dil, nsub, ppk, q_blk, k_blk, v_blk,
               n_heads, group, max_dist, do_blk, name):
    T = qkv.shape[0]
    RB = BLK * dil * nsub
    nb = T // RB
    npair = n_heads // 2
    kv_shared = group > 1
    scale = HEAD_DIM ** -0.5
    has_sink = sinks is not None
    mixed = w is not None
    qw = LANES * ppk

    def body(*refs):
        slope_ref = refs[0]
        if has_sink:
            sink_ref, refs = refs[1], refs[2:]
        else:
            refs = refs[1:]
        q_ref, kc_ref, kp_ref, vc_ref, vp_ref, do_ref, lc0_ref, lc1_ref = refs[:8]
        refs = refs[8:]
        if mixed:
            w_ref, om_ref, refs = refs[0], refs[1], refs[2:]
        else:
            o_ref, refs = refs[0], refs[1:]
        dq_ref, dk_ref, dv_ref, dsk_ref, ck_ref, cv_ref = refs
        pb = pl.program_id(0)
        i = pl.program_id(1)
        live = i < nb
        m0 = _lane0()
        lane = lax.broadcasted_iota(jnp.int32, (1, LANES), 1)
        distf, valid_first, top = _band_masks2(max_dist, i > 0, live)
        valid_inner = _band_masks2(max_dist, i >= 0, live)[1] if nsub > 1 else None
        livef = live.astype(F32)

        def half_rows(x):
            s0 = jnp.sum(jnp.where(m0, x, 0.0), axis=-1, keepdims=True)
            s1 = jnp.sum(jnp.where(m0, 0.0, x), axis=-1, keepdims=True)
            return jnp.concatenate([s0, s1], axis=0)

        @pl.when((pb == 0) & (i == 0))
        def _():
            dsk_ref[...] = jnp.zeros_like(dsk_ref)

        @pl.when(i == 0)
        def _():
            ck_ref[...] = jnp.zeros_like(ck_ref)
            cv_ref[...] = jnp.zeros_like(cv_ref)

        dsk_acc = jnp.zeros((1, LANES), F32)
        if nsub > 1:
            dk_ref[...] = ck_ref[...]
            dv_ref[...] = cv_ref[...]
        for rows, (src, prows), sub in _band_units(dil, nsub):
            valid = valid_first if sub == 0 else valid_inner
            kpr, vpr = (kp_ref, vp_ref) if src == "prev" else (kc_ref, vc_ref)
            for jp in range(ppk):
                cs = pl.ds(LANES * jp, LANES)
                ks = pl.ds(0, LANES) if kv_shared else cs
                kn = jnp.concatenate([kpr[prows, ks], kc_ref[rows, ks]], axis=0).astype(BF16)
                vcat = jnp.concatenate([vpr[prows, ks], vc_ref[rows, ks]], axis=0).astype(BF16)
                dov = do_ref[rows, cs]
                if mixed:
                    dov = dov * w_ref[rows, cs]
                    shift = half_rows(dov * om_ref[rows, cs])
                else:
                    shift = half_rows(dov * o_ref[rows, cs])
                kes = [((2 * jp + e) // group) % 2 if kv_shared else e for e in range(2)]
                hidx = 2 * (pb * ppk + jp)
                qs = _stack_heads(q_ref[rows, cs], m0, kes).astype(BF16)
                dos = _stack_heads(dov, m0, kes).astype(BF16)
                lse = jnp.concatenate([ref[rows, :] if ppk == 1 else ref[jp, rows, :]
                                       for ref in (lc0_ref, lc1_ref)], axis=0)
                slope = jnp.where(top, slope_ref[hidx], slope_ref[hidx + 1])
                p = jnp.where(valid, jnp.exp(_dot_nt(qs, kn) * scale - slope * distf - lse), 0.0)
                ds = (p * (_dot_nt(dos, vcat) - shift)).astype(BF16)
                dqn = _unstack_heads(_dot(ds, kn), m0, kes) * scale
                dkn = _dot_tn(ds, qs) * scale
                dvv = _dot_tn(p.astype(BF16), dos)
                if has_sink:
                    sk = jnp.where(top, sink_ref[hidx], sink_ref[hidx + 1])
                    contrib = -jnp.exp(sk - lse) * shift * livef
                    for e in range(2):
                        tot = jnp.sum(contrib[e * BLK:(e + 1) * BLK], axis=0, keepdims=True)
                        dsk_acc = dsk_acc + jnp.where(lane == (2 * jp + e), tot, 0.0)
                dk_raw = dkn

                @pl.when(live)
                def _():
                    dq_ref[rows, cs] = dqn

                if nsub == 1:
                    dk_ref[rows, cs] = ck_ref[rows, cs] + dk_raw[:BLK]
                    dv_ref[rows, cs] = cv_ref[rows, cs] + dvv[:BLK]
                elif sub == 0:
                    last = pl.ds(RB - BLK, BLK)
                    dk_ref[last, cs] += dk_raw[:BLK]
                    dv_ref[last, cs] += dvv[:BLK]
                else:
                    ck_ref[prows, cs] += dk_raw[:BLK]
                    cv_ref[prows, cs] += dvv[:BLK]
                ck_ref[rows, cs] = dk_raw[BLK:]
                cv_ref[rows, cs] = dvv[BLK:]
        dsk_ref[...] += dsk_acc

    smem = pl.BlockSpec(memory_space=pltpu.SMEM)
    gspec = pl.BlockSpec((1, LANES), lambda p, i: (0, 0))

    def cur(i):
        return jnp.minimum(i, nb - 1)

    qspec = pl.BlockSpec((RB, qw), lambda p, i: (cur(i), p))
    dospec = pl.BlockSpec((RB, qw), lambda p, i: (cur(i), do_blk + p))
    kvout = pl.BlockSpec((RB, qw), lambda p, i: (jnp.maximum(i - 1, 0), p))
    in_specs = ([smem] * (2 if has_sink else 1) + _band_specs(dil, nsub, ppk, q_blk, k_blk, v_blk, kv_shared, nb)
                + [dospec] + [_head_col_spec(ppk, RB, cur)] * 2
                + ([qspec, qspec] if mixed else [qspec]))
    args = ([slopes] + ([sinks] if has_sink else []) + [qkn] * 3 + [qkv] * 2 + [do, lsec[0], lsec[1]]
            + ([w, omix] if mixed else [o]))
    full = jax.ShapeDtypeStruct((T, n_heads * HEAD_DIM), F32)
    row = jax.ShapeDtypeStruct((1, LANES), F32)
    return pl.pallas_call(
        body, grid=(npair // ppk, nb + 1), in_specs=in_specs,
        out_specs=[qspec, kvout, kvout, gspec],
        out_shape=[full, full, full, row],
        scratch_shapes=[pltpu.VMEM((RB, qw), F32), pltpu.VMEM((RB, qw), F32)],
        compiler_params=_cparams(2), name=name)(*args)


def mix_fwd(o1, o2, o3, l1, l2, l3, *, tm, name):
    T, C = o1.shape

    def body(o1r, o2r, o3r, l1r, l2r, l3r, o_ref, w1r, w2r, w3r):
        a, b, c = l1r[...], l2r[...], l3r[...]
        m = jnp.maximum(jnp.maximum(a, b), c)
        ea, eb, ec = jnp.exp(a - m), jnp.exp(b - m), jnp.exp(c - m)
        inv = 1.0 / (ea + eb + ec)
        wa, wb, wc = ea * inv, eb * inv, ec * inv
        o_ref[...] = wa * o1r[...] + wb * o2r[...] + wc * o3r[...]
        w1r[...] = wa
        w2r[...] = wb
        w3r[...] = wc

    spec = pl.BlockSpec((tm, C), lambda i: (i, 0))
    shp = jax.ShapeDtypeStruct((T, C), F32)
    return pl.pallas_call(body, grid=(T // tm,), in_specs=[spec] * 6, out_specs=[spec] * 4, out_shape=[shp] * 4,
                          compiler_params=_cparams(1), name=name)(o1, o2, o3, l1, l2, l3)


def _qk_norm_bwd(raw, dn, gain, m0):
    r = _head_rms(raw, m0)
    h = raw * r
    dh = dn * gain
    d_raw = r * (dh - h * (_half_sum(dh * h, m0) * (1.0 / HEAD_DIM)))
    return d_raw, jnp.sum(dn * h, axis=0, keepdims=True)


def _acc_rows(ref, val):
    @pl.when(pl.program_id(0) == 0)
    def _():
        ref[...] = val

    @pl.when(pl.program_id(0) > 0)
    def _():
        ref[...] += val


def assemble_odd(parts, qkv, q_gain2, k_gain2, *, tm, name):
    T, C = parts[0][0].shape
    nbk = C // LANES

    def body(*refs):
        qkv_ref, qg_ref, kg_ref, o_ref, dqg_ref, dkg_ref = refs[9:]
        m0 = _lane0()
        sums = [refs[j][...] + refs[3 + j][...] + refs[6 + j][...] for j in range(3)]
        o_ref[:, pl.ds(2 * C, C)] = sums[2]
        for j, (g_ref, acc_ref) in enumerate(((qg_ref, dqg_ref), (kg_ref, dkg_ref))):
            dgain = jnp.zeros((1, LANES), F32)
            for b in range(nbk):
                cols = pl.ds(C * j + LANES * b, LANES)
                d_raw, part = _qk_norm_bwd(qkv_ref[:, cols], sums[j][:, LANES * b:LANES * (b + 1)], g_ref[...], m0)
                o_ref[:, cols] = d_raw
                dgain = dgain + part
            _acc_rows(acc_ref, dgain)

    spec = pl.BlockSpec((tm, C), lambda i: (i, 0))
    gspec = pl.BlockSpec((1, LANES), lambda i: (0, 0))
    flat = [parts[p][j] for p in range(3) for j in range(3)]
    row = jax.ShapeDtypeStruct((1, LANES), F32)
    return pl.pallas_call(body, grid=(T // tm,),
                          in_specs=[spec] * 9 + [pl.BlockSpec((tm, 2 * C), lambda i: (i, 0)), gspec, gspec],
                          out_specs=[pl.BlockSpec((tm, 3 * C), lambda i: (i, 0)), gspec, gspec],
                          out_shape=[jax.ShapeDtypeStruct((T, 3 * C), F32), row, row],
                          compiler_params=_cparams(1), name=name)(*flat, qkv, q_gain2, k_gain2)


def assemble_even(dqa, dka4, dva4, dqb, dkb, dvb, qkv, q_gain2, k_gain2, *, tm, name):
    T = dqa.shape[0]
    W = 512
    QK = 768

    def body(dqa_r, dka_r, dva_r, dqb_r, dkb_r, dvb_r, qkv_ref, qg_ref, kg_ref, o_ref, dqg_ref, dkg_ref):
        m0 = _lane0()
        ka = dka_r[...]
        va = dva_r[...]
        dqn = dqa_r[...]
        dgain = jnp.zeros((1, LANES), F32)
        for b in range(W // LANES):
            cols = pl.ds(LANES * b, LANES)
            d_raw, part = _qk_norm_bwd(qkv_ref[:, cols], dqn[:, LANES * b:LANES * (b + 1)], qg_ref[...], m0)
            o_ref[:, cols] = d_raw
            dgain = dgain + part
        _acc_rows(dqg_ref, dgain)
        dkn = ka[:, 0:128] + ka[:, 128:256] + ka[:, 256:384] + ka[:, 384:512]
        d_raw, part = _qk_norm_bwd(qkv_ref[:, pl.ds(W, LANES)], dkn, kg_ref[...], m0)
        o_ref[:, pl.ds(W, LANES)] = d_raw
        _acc_rows(dkg_ref, part)
        o_ref[:, pl.ds(640, LANES)] = va[:, 0:128] + va[:, 128:256] + va[:, 256:384] + va[:, 384:512]
        o_ref[:, pl.ds(768, W)] = dqb_r[...]
        o_ref[:, pl.ds(1280, W)] = dkb_r[...]
        o_ref[:, pl.ds(1792, W)] = dvb_r[...]

    spec = pl.BlockSpec((tm, W), lambda i: (i, 0))
    gspec = pl.BlockSpec((1, LANES), lambda i: (0, 0))
    row = jax.ShapeDtypeStruct((1, LANES), F32)
    return pl.pallas_call(body, grid=(T // tm,),
                          in_specs=[spec] * 6 + [pl.BlockSpec((tm, QK), lambda i: (i, 0)), gspec, gspec],
                          out_specs=[pl.BlockSpec((tm, 2304), lambda i: (i, 0)), gspec, gspec],
                          out_shape=[jax.ShapeDtypeStruct((T, 2304), F32), row, row],
                          compiler_params=_cparams(1), name=name)(dqa, dka4, dva4, dqb, dkb, dvb, qkv, q_gain2, k_gain2)


STICK_T = 256
STICK_DEAD = -110.0


def _split_bf16(x):
    hi = x.astype(BF16)
    lo = (x - hi.astype(F32)).astype(BF16)
    return hi, lo


def _stick_logits(qm, kt, scale, diag):
    n = STICK_T
    row = lax.broadcasted_iota(jnp.int32, (n, n), 0)
    col = lax.broadcasted_iota(jnp.int32, (n, n), 1)
    mask = col < row + jnp.where(diag, 0, n)
    z = _dot_nt(qm, kt) * scale
    lneg = -(jnp.maximum(z, 0.0) + jnp.log(1.0 + jnp.exp(-jnp.abs(z))))
    lpos = z + lneg
    lk = jnp.where(mask, lneg, 0.0)
    return mask, lpos, lneg, lk


def _cumsum_mm(x, tri):
    hi, lo = _split_bf16(x)
    return _dot(hi, tri) + _dot(lo, tri)


def stick_fwd(qkv, *, q_blk, k_blk, v_blk, n_pairs, name, riders=None, rider_args=()):
    T = qkv.shape[0]
    n = STICK_T
    nq = T // n
    scale = HEAD_DIM ** -0.5
    nc = riders.n if riders is not None else 0
    n_steps = n_pairs * nq
    stage_at = (0, (3 * n_steps) // 4, n_steps - 1, n_steps - 1)

    def body(*refs):
        q_ref, k_ref, v_ref = refs[:3]
        x_refs, o_ref = refs[3:3 + nc], refs[3 + nc]
        out_refs, sems = refs[4 + nc:4 + 2 * nc], refs[4 + 2 * nc:]
        i = pl.program_id(1)
        step_id = pl.program_id(0) * nq + i

        def ride(which):
            if riders is not None:
                @pl.when(step_id == stage_at[which])
                def _():
                    riders.stage(which, x_refs, out_refs, sems)

        ride(0)
        ride(1)
        m0 = _lane0()
        r2 = lax.broadcasted_iota(jnp.int32, (n, n), 0)
        c2 = lax.broadcasted_iota(jnp.int32, (n, n), 1)
        tri_after = (r2 > c2).astype(BF16)
        qv = q_ref[...]
        out = jnp.zeros((n, LANES), F32)
        for e in range(2):
            qm = _mask_half(qv, m0, e).astype(BF16)

            def alive(st):
                t, _, carry = st
                return (t <= i) & (jnp.max(carry) > STICK_DEAD)

            def step(st, e=e, qm=qm):
                t, acc, carry = st
                start = pl.multiple_of((i - t) * n, n)
                kt = k_ref[pl.ds(start, n), :].astype(BF16)
                vt = _mask_half(v_ref[pl.ds(start, n), :], m0, e).astype(BF16)
                mask, lpos, _, lk = _stick_logits(qm, kt, scale, t == 0)
                after = _cumsum_mm(lk, tri_after) + carry
                a = jnp.where(mask, jnp.exp(lpos + after), 0.0)
                acc = acc + _dot(a.astype(BF16), vt)
                carry = carry + jnp.sum(lk, axis=-1, keepdims=True)
                return t + 1, acc, carry

            _, acc, _ = lax.while_loop(alive, step, (jnp.int32(0), jnp.zeros((n, LANES), F32),
                                                     jnp.zeros((n, 1), F32)))
            out = out + acc
        o_ref[...] = out
        ride(2)
        ride(3)

    outs = pl.pallas_call(
        body, grid=(n_pairs, nq),
        in_specs=[pl.BlockSpec((n, LANES), lambda p, i: (i, q_blk + p)),
                  pl.BlockSpec((T, LANES), lambda p, i: (0, k_blk + p)),
                  pl.BlockSpec((T, LANES), lambda p, i: (0, v_blk + p))] + [_ANY] * nc,
        out_specs=[pl.BlockSpec((n, LANES), lambda p, i: (i, p))] + [_ANY] * nc,
        out_shape=[jax.ShapeDtypeStruct((T, n_pairs * LANES), F32)] + (riders.shapes if nc else []),
        scratch_shapes=riders.sems if nc else [],
        compiler_params=_cparams(2), name=name)(qkv, qkv, qkv, *rider_args)
    return outs[0], list(outs[1:])


def stick_bwd(qkv, do, *, q_blk, k_blk, v_blk, do_blk, n_pairs, name, riders=None, rider_args=()):
    T = qkv.shape[0]
    n = STICK_T
    nq = T // n
    scale = HEAD_DIM ** -0.5
    nc = riders.n if riders is not None else 0

    def body(*refs):
        q_ref, k_ref, v_ref, do_ref = refs[:4]
        pre_refs = refs[4:4 + nc]
        dq_ref, dk_ref, dv_ref = refs[4 + nc:7 + nc]
        land_refs = refs[7 + nc:7 + 2 * nc]
        a_keep, g_keep, s_keep = refs[7 + 2 * nc:10 + 2 * nc]
        sems = refs[10 + 2 * nc:]
        i = pl.program_id(1)
        first_step = (pl.program_id(0) == 0) & (i == 0)
        last_step = (pl.program_id(0) == n_pairs - 1) & (i == nq - 1)
        m0 = _lane0()
        r2 = lax.broadcasted_iota(jnp.int32, (n, n), 0)
        c2 = lax.broadcasted_iota(jnp.int32, (n, n), 1)
        tri_after = (r2 > c2).astype(BF16)
        tri_from = (r2 >= c2).astype(BF16)

        if riders is not None:
            @pl.when(first_step)
            def _():
                riders.start(pre_refs, land_refs, sems)

        @pl.when(i == 0)
        def _():
            dk_ref[...] = jnp.zeros_like(dk_ref)
            dv_ref[...] = jnp.zeros_like(dv_ref)

        qv = q_ref[...]
        dov = do_ref[...]
        dq_out = jnp.zeros((n, LANES), F32)
        for e in range(2):
            qm = _mask_half(qv, m0, e).astype(BF16)
            dom = _mask_half(dov, m0, e).astype(BF16)

            def alive(st):
                t, carry, _ = st
                return (t <= i) & (jnp.max(carry) > STICK_DEAD)

            def scan(st, qm=qm, dom=dom):
                t, carry, gtot = st
                start = pl.multiple_of((i - t) * n, n)
                kt = k_ref[pl.ds(start, n), :].astype(BF16)
                vt = v_ref[pl.ds(start, n), :].astype(BF16)
                mask, lpos, lneg, lk = _stick_logits(qm, kt, scale, t == 0)
                a = jnp.where(mask, jnp.exp(lpos + _cumsum_mm(lk, tri_after) + carry), 0.0)
                g = _dot_nt(dom, vt) * a
                a_keep[t] = a.astype(BF16)
                g_keep[t] = g
                s_keep[t] = jnp.exp(lneg).astype(BF16)
                return (t + 1, carry + jnp.sum(lk, axis=-1, keepdims=True),
                        gtot + jnp.sum(g, axis=-1, keepdims=True))

            z1 = jnp.zeros((n, 1), F32)
            n_live, _, gtot = lax.while_loop(alive, scan, (jnp.int32(0), z1, z1))

            def step(t, st, e=e, qm=qm, dom=dom, gtot=gtot):
                dq_acc, gright = st
                start = pl.multiple_of((i - t) * n, n)
                g = g_keep[t]
                sneg = s_keep[t].astype(F32)
                before = gtot - (_cumsum_mm(g, tri_from) + gright)
                mask = c2 < r2 + jnp.where(t == 0, 0, n)
                dz = jnp.where(mask, g * sneg - before * (1.0 - sneg), 0.0) * scale
                dzb = dz.astype(BF16)
                dq_acc = dq_acc + _dot(dzb, _mask_half(k_ref[pl.ds(start, n), :], m0, e).astype(BF16))
                dk_ref[pl.ds(start, n), :] += _dot_tn(dzb, qm)
                dv_ref[pl.ds(start, n), :] += _dot_tn(a_keep[t], dom)
                return dq_acc, gright + jnp.sum(g, axis=-1, keepdims=True)

            dq_acc, _ = lax.fori_loop(0, n_live, step, (jnp.zeros((n, LANES), F32), z1))
            dq_out = dq_out + dq_acc
        dq_ref[...] = dq_out

        if riders is not None:
            @pl.when(last_step)
            def _():
                riders.finish(pre_refs, land_refs, sems)

    tile = pl.BlockSpec((n, LANES), lambda p, i: (i, p))
    whole = pl.BlockSpec((T, LANES), lambda p, i: (0, p))
    shp = jax.ShapeDtypeStruct((T, n_pairs * LANES), F32)
    outs = pl.pallas_call(
        body, grid=(n_pairs, nq),
        in_specs=[pl.BlockSpec((n, LANES), lambda p, i: (i, q_blk + p)),
                  pl.BlockSpec((T, LANES), lambda p, i: (0, k_blk + p)),
                  pl.BlockSpec((T, LANES), lambda p, i: (0, v_blk + p)),
                  pl.BlockSpec((n, LANES), lambda p, i: (i, do_blk + p))] + [_ANY] * nc,
        out_specs=[tile, whole, whole] + [_ANY] * nc,
        out_shape=[shp, shp, shp] + (riders.shapes if nc else []),
        scratch_shapes=[pltpu.VMEM((nq, n, n), BF16), pltpu.VMEM((nq, n, n), F32), pltpu.VMEM((nq, n, n), BF16)]
        + (riders.sems if nc else []),
        compiler_params=_cparams(2), name=name)(qkv, qkv, qkv, do, *rider_args)
    return outs[0], outs[1], outs[2], list(outs[3:])


def _xnorm(x):
    r = lax.rsqrt(jnp.mean(x * x, axis=-1, keepdims=True) + RMS_EPS)
    return r, x * r


def xattn_fwd(qraw, kvraw, q_gain, k_gain, *, tm, name):
    T = qraw.shape[0]
    scale = X_HEAD_DIM ** -0.5
    W = X_HEADS * X_HEAD_DIM

    def body(q_ref, kv_ref, qg_ref, kg_ref, o_ref):
        for h in range(X_HEADS):
            cs = pl.ds(X_HEAD_DIM * h, X_HEAD_DIM)
            _, qh = _xnorm(q_ref[:, cs])
            _, kh = _xnorm(kv_ref[:, cs])
            qn = (qh * qg_ref[...]).astype(BF16)
            kn = (kh * kg_ref[...]).astype(BF16)
            v = kv_ref[:, pl.ds(W + X_HEAD_DIM * h, X_HEAD_DIM)].astype(BF16)
            s = _dot_nt(qn, kn) * scale
            m = jnp.max(s, axis=-1, keepdims=True)
            p = jnp.exp(s - m)
            p = p / jnp.sum(p, axis=-1, keepdims=True)
            o_ref[:, cs] = _dot(p.astype(BF16), v)

    gspec = pl.BlockSpec((1, X_HEAD_DIM), lambda i: (0, 0))
    return pl.pallas_call(
        body, grid=(T // tm,),
        in_specs=[pl.BlockSpec((tm, W), lambda i: (i, 0)), pl.BlockSpec((MEM_LEN, 2 * W), lambda i: (0, 0)),
                  gspec, gspec],
        out_specs=pl.BlockSpec((tm, W), lambda i: (i, 0)),
        out_shape=jax.ShapeDtypeStruct((T, W), F32),
        compiler_params=_cparams(1), name=name)(qraw, kvraw, q_gain, k_gain)


def xattn_bwd(qraw, kvraw, q_gain, k_gain, do, o, *, tm, name):
    T = qraw.shape[0]
    nt = T // tm
    scale = X_HEAD_DIM ** -0.5
    W = X_HEADS * X_HEAD_DIM

    def body(q_ref, kv_ref, qg_ref, kg_ref, do_ref, o_ref, dq_ref, dkv_ref, dqg_ref, dkg_ref, dkn_ref):
        i = pl.program_id(0)

        @pl.when(i == 0)
        def _():
            dkv_ref[...] = jnp.zeros_like(dkv_ref)
            dkn_ref[...] = jnp.zeros_like(dkn_ref)
            dqg_ref[...] = jnp.zeros_like(dqg_ref)
            dkg_ref[...] = jnp.zeros_like(dkg_ref)

        qg = qg_ref[...]
        kg = kg_ref[...]
        dqg_acc = jnp.zeros((1, X_HEAD_DIM), F32)
        for h in range(X_HEADS):
            cs = pl.ds(X_HEAD_DIM * h, X_HEAD_DIM)
            vs = pl.ds(W + X_HEAD_DIM * h, X_HEAD_DIM)
            rq, qh = _xnorm(q_ref[:, cs])
            _, kh = _xnorm(kv_ref[:, cs])
            qn = (qh * qg).astype(BF16)
            kn = (kh * kg).astype(BF16)
            v = kv_ref[:, vs].astype(BF16)
            s = _dot_nt(qn, kn) * scale
            m = jnp.max(s, axis=-1, keepdims=True)
            p = jnp.exp(s - m)
            p = p / jnp.sum(p, axis=-1, keepdims=True)
            dov = do_ref[:, cs]
            delta = jnp.sum(dov * o_ref[:, cs], axis=-1, keepdims=True)
            dob = dov.astype(BF16)
            ds = (p * (_dot_nt(dob, v) - delta)).astype(BF16)
            dqn = _dot(ds, kn) * scale
            dkn_ref[:, cs] += _dot_tn(ds, qn) * scale
            dkv_ref[:, vs] += _dot_tn(p.astype(BF16), dob)
            dqg_acc = dqg_acc + jnp.sum(dqn * qh, axis=0, keepdims=True)
            dqh = dqn * qg
            dq_ref[:, cs] = rq * (dqh - qh * jnp.mean(dqh * qh, axis=-1, keepdims=True))
        dqg_ref[...] += dqg_acc

        @pl.when(i == nt - 1)
        def _():
            dkg_acc = jnp.zeros((1, X_HEAD_DIM), F32)
            for h in range(X_HEADS):
                cs = pl.ds(X_HEAD_DIM * h, X_HEAD_DIM)
                rk, kh = _xnorm(kv_ref[:, cs])
                dkn = dkn_ref[:, cs]
                dkg_acc = dkg_acc + jnp.sum(dkn * kh, axis=0, keepdims=True)
                dkh = dkn * kg
                dkv_ref[:, cs] = rk * (dkh - kh * jnp.mean(dkh * kh, axis=-1, keepdims=True))
            dkg_ref[...] = dkg_acc

    gspec = pl.BlockSpec((1, X_HEAD_DIM), lambda i: (0, 0))
    tile = pl.BlockSpec((tm, W), lambda i: (i, 0))
    kvspec = pl.BlockSpec((MEM_LEN, 2 * W), lambda i: (0, 0))
    grow = jax.ShapeDtypeStruct((1, X_HEAD_DIM), F32)
    return pl.pallas_call(
        body, grid=(nt,), in_specs=[tile, kvspec, gspec, gspec, tile, tile],
        out_specs=[tile, kvspec, gspec, gspec],
        out_shape=[jax.ShapeDtypeStruct((T, W), F32), jax.ShapeDtypeStruct((MEM_LEN, 2 * W), F32), grow, grow],
        scratch_shapes=[pltpu.VMEM((MEM_LEN, W), F32)],
        compiler_params=_cparams(1), name=name)(qraw, kvraw, q_gain, k_gain, do, o)


_ANY = pl.BlockSpec(memory_space=pl.ANY)


def _my_pos():
    return lax.axis_index("x"), lax.axis_index("y"), lax.axis_index("c")


def _pieces(arrays, chunks):
    out = []
    for a, (arr, n) in enumerate(zip(arrays, chunks)):
        rc = arr.shape[-2] // n
        out += [(a, pl.ds(ch * rc, rc)) for ch in range(n)]
    return out


class GatherBlocks:
    N_STAGES = 4

    def __init__(self, blks, chunks):
        self.shapes = [jax.ShapeDtypeStruct((N_DEV,) + b.shape, b.dtype) for b in blks]
        self.n = len(blks)
        self.pieces = _pieces(blks, chunks)
        n_p = len(self.pieces)
        self.sems = [pltpu.SemaphoreType.DMA((7 * n_p,)), pltpu.SemaphoreType.DMA((7 * n_p,)),
                     pltpu.SemaphoreType.DMA((n_p,))]

    def stage(self, which, x_refs, out_refs, sems):
        send_sems, recv_sems, local_sems = sems
        pieces, n_p = self.pieces, len(self.pieces)
        x, y, c = _my_pos()
        me, sibling = (x, y, c), (x, y, 1 - c)
        chips = [(1 - x, y), (x, 1 - y), (1 - x, 1 - y)]
        xn, yn, dg = [(*chip, c) for chip in chips]
        ps = range(n_p)

        def slot(block, p):
            px, py, pc = block
            a, rows = pieces[p]
            return out_refs[a].at[4 * px + 2 * py + pc, rows]

        def own(p):
            a, rows = pieces[p]
            return x_refs[a].at[rows]

        def copy(k, p, block, to, from_input=False):
            return pltpu.make_async_remote_copy(
                src_ref=own(p) if from_input else slot(block, p), dst_ref=slot(block, p),
                send_sem=send_sems.at[k * n_p + p], recv_sem=recv_sems.at[k * n_p + p],
                device_id=to, device_id_type=MESH)

        mine = [pltpu.make_async_copy(own(p), slot(me, p), local_sems.at[p]) for p in ps]
        first = [copy(k, p, me, to, from_input=True) for p in ps for k, to in ((1, xn), (2, yn), (0, sibling))]
        on_x = [copy(3, p, xn, yn) for p in ps if p % 2 == 0] + [copy(4, p, xn, sibling) for p in ps]
        on_y = [copy(3, p, yn, xn) for p in ps if p % 2 == 1] + [copy(5, p, yn, sibling) for p in ps]
        on_d = [copy(6, p, dg, sibling) for p in ps]
        if which == 0:
            for cp in first + mine:
                cp.start()
        elif which == 1:
            for p in ps:
                copy(1, p, xn, me).wait_recv()
                if p % 2 == 0:
                    copy(3, p, xn, yn).start()
                copy(4, p, xn, sibling).start()
                copy(2, p, yn, me).wait_recv()
                if p % 2 == 1:
                    copy(3, p, yn, xn).start()
                copy(5, p, yn, sibling).start()
        elif which == 2:
            for p in ps:
                copy(3, p, dg, me).wait_recv()
                copy(6, p, dg, sibling).start()
        else:
            for p in ps:
                copy(0, p, sibling, me).wait_recv()
            for k, chip in zip((4, 5, 6), chips):
                for p in ps:
                    copy(k, p, (*chip, 1 - c), me).wait_recv()
            for cp in first + on_x + on_y + on_d:
                cp.wait_send()
            for cp in mine:
                cp.wait()


def gather_blocks(blks, chunks, *, name):
    gb = GatherBlocks(blks, chunks)
    n = gb.n

    def body(*refs):
        x_refs, out_refs, sems = refs[:n], refs[n:2 * n], refs[2 * n:]
        for which in range(gb.N_STAGES):
            gb.stage(which, x_refs, out_refs, sems)

    return pl.pallas_call(body, out_shape=gb.shapes, in_specs=[_ANY] * n, out_specs=[_ANY] * n,
                          scratch_shapes=gb.sems, name=name)(*blks)


def gather_small(small, *, name):
    S, C = small.shape

    def body(s_ref, out_ref, send_sems, recv_sems, local_sem):
        x, y, c = _my_pos()
        my_id = 4 * x + 2 * y + c

        def copy(k, slot):
            px, py, pc = x ^ ((k >> 2) & 1), y ^ ((k >> 1) & 1), c ^ (k & 1)
            dst = my_id if slot == "mine" else 4 * px + 2 * py + pc
            return pltpu.make_async_remote_copy(
                src_ref=s_ref, dst_ref=out_ref.at[dst], send_sem=send_sems.at[k - 1], recv_sem=recv_sems.at[k - 1],
                device_id=(px, py, pc), device_id_type=MESH)

        own = pltpu.make_async_copy(s_ref, out_ref.at[my_id], local_sem)
        own.start()
        sends = [copy(k, "mine") for k in range(1, N_DEV)]
        for cp in sends:
            cp.start()
        for k in range(1, N_DEV):
            copy(k, "theirs").wait_recv()
        for cp in sends:
            cp.wait_send()
        own.wait()

    dma7 = pltpu.SemaphoreType.DMA((7,))
    return pl.pallas_call(
        body, out_shape=jax.ShapeDtypeStruct((N_DEV, S, C), small.dtype), in_specs=[_ANY], out_specs=_ANY,
        scratch_shapes=[dma7, dma7, pltpu.SemaphoreType.DMA], name=name)(small)


class PairExchange:
    def __init__(self, bigs, chunks):
        self.shapes = [jax.ShapeDtypeStruct((4,) + b.shape[1:], b.dtype) for b in bigs]
        self.n = len(bigs)
        self.pieces = _pieces(bigs, chunks)
        n_p = len(self.pieces)
        self.sems = [pltpu.SemaphoreType.DMA((4 * n_p,)), pltpu.SemaphoreType.DMA((4 * n_p,))]

    def _copies(self, big_refs, out_refs, sems):
        send_sems, recv_sems = sems
        n_p = len(self.pieces)
        x, y, c = _my_pos()

        def copy(b, p):
            a, rows = self.pieces[p]
            return pltpu.make_async_remote_copy(
                src_ref=big_refs[a].at[2 * b + (1 - c), rows], dst_ref=out_refs[a].at[b, rows],
                send_sem=send_sems.at[b * n_p + p], recv_sem=recv_sems.at[b * n_p + p],
                device_id=(x, y, 1 - c), device_id_type=MESH)

        return [copy(b, p) for b in range(4) for p in range(n_p)]

    def start(self, big_refs, out_refs, sems):
        for cp in self._copies(big_refs, out_refs, sems):
            cp.start()

    def finish(self, big_refs, out_refs, sems):
        cps = self._copies(big_refs, out_refs, sems)
        for cp in cps:
            cp.wait_recv()
        for cp in cps:
            cp.wait_send()


def _standalone(exchange, args, name):
    n = exchange.n

    def body(*refs):
        exchange.start(refs[:n], refs[n:2 * n], refs[2 * n:])
        exchange.finish(refs[:n], refs[n:2 * n], refs[2 * n:])

    return pl.pallas_call(body, out_shape=exchange.shapes, in_specs=[_ANY] * n, out_specs=[_ANY] * n,
                          scratch_shapes=exchange.sems, name=name)(*args)


class Riding:
    def __init__(self, riders):
        self.riders = [(ex, list(args)) for ex, args in riders]
        self.args = [a for _, args in self.riders for a in args]
        self.in_specs = [_ANY] * len(self.args)
        self.out_shapes = [s for ex, _ in self.riders for s in ex.shapes]
        self.out_specs = [_ANY] * len(self.out_shapes)
        self.scratch = [s for ex, _ in self.riders for s in ex.sems]

    def wrap(self, body, n_in, n_out, n_scratch, is_first, is_last):
        def wrapped(*refs):
            k = 0
            core = list(refs[:n_in])
            k = n_in
            r_in = []
            for ex, _ in self.riders:
                r_in.append(refs[k:k + ex.n])
                k += ex.n
            core += refs[k:k + n_out]
            k += n_out
            r_out = []
            for ex, _ in self.riders:
                r_out.append(refs[k:k + ex.n])
                k += ex.n
            core += refs[k:k + n_scratch]
            k += n_scratch
            r_sem = []
            for ex, _ in self.riders:
                r_sem.append(refs[k:k + len(ex.sems)])
                k += len(ex.sems)

            @pl.when(is_first())
            def _():
                for (ex, _), a, b, s in zip(self.riders, r_in, r_out, r_sem):
                    ex.start(a, b, s)

            body(*core)

            @pl.when(is_last())
            def _():
                for (ex, _), a, b, s in zip(self.riders, r_in, r_out, r_sem):
                    ex.finish(a, b, s)

        return wrapped

    def split(self, outs, n_out):
        core, rest, per = list(outs[:n_out]), list(outs[n_out:]), []
        for ex, _ in self.riders:
            per.append(rest[:ex.n])
            rest = rest[ex.n:]
        return core, per


def pair_sum(big, sib, c, *, tr, name):
    _, R, C = big.shape

    def body(c_ref, a_ref, s_ref, o_ref):
        o_ref[...] = (a_ref[...].astype(F32) + s_ref[...].astype(F32)).astype(o_ref.dtype)

    grid_spec = pltpu.PrefetchScalarGridSpec(
        num_scalar_prefetch=1, grid=(4, R // tr),
        in_specs=[pl.BlockSpec((None, tr, C), lambda b, i, c_ref: (2 * b + c_ref[0], i, 0)),
                  pl.BlockSpec((None, tr, C), lambda b, i, c_ref: (b, i, 0))],
        out_specs=pl.BlockSpec((None, tr, C), lambda b, i, c_ref: (b, i, 0)))
    return pl.pallas_call(body, grid_spec=grid_spec, out_shape=jax.ShapeDtypeStruct((4, R, C), big.dtype),
                          compiler_params=_cparams(2), name=name)(c.reshape(1).astype(jnp.int32), big, sib)


class ChipScatter:
    def __init__(self, pres, chunks):
        self.shapes = [jax.ShapeDtypeStruct(p.shape, p.dtype) for p in pres]
        self.n = len(pres)
        self.pieces = _pieces(pres, chunks)
        n_p = len(self.pieces)
        self.sems = [pltpu.SemaphoreType.DMA((3 * n_p,)), pltpu.SemaphoreType.DMA((3 * n_p,)),
                     pltpu.SemaphoreType.DMA((n_p,))]

    def _copies(self, pre_refs, out_refs, sems):
        send_sems, recv_sems, local_sems = sems
        n_p = len(self.pieces)
        x, y, c = _my_pos()
        my_chip = 2 * x + y
        chips = [(1 - x, y), (x, 1 - y), (1 - x, 1 - y)]

        def copy(j, p, slot):
            px, py = chips[j]
            a, rows = self.pieces[p]
            src_slot, dst_slot = (2 * px + py, my_chip) if slot == "mine" else (my_chip, 2 * px + py)
            return pltpu.make_async_remote_copy(
                src_ref=pre_refs[a].at[src_slot, rows], dst_ref=out_refs[a].at[dst_slot, rows],
                send_sem=send_sems.at[j * n_p + p], recv_sem=recv_sems.at[j * n_p + p],
                device_id=(px, py, c), device_id_type=MESH)

        own = [pltpu.make_async_copy(pre_refs[a].at[my_chip, rows], out_refs[a].at[my_chip, rows], local_sems.at[p])
               for p, (a, rows) in enumerate(self.pieces)]
        sends = [copy(j, p, "mine") for j in range(3) for p in range(n_p)]
        recvs = [copy(j, p, "theirs") for j in range(3) for p in range(n_p)]
        return own, sends, recvs

    def start(self, pre_refs, out_refs, sems):
        own, sends, _ = self._copies(pre_refs, out_refs, sems)
        for cp in sends + own:
            cp.start()

    def finish(self, pre_refs, out_refs, sems):
        own, sends, recvs = self._copies(pre_refs, out_refs, sems)
        for cp in recvs:
            cp.wait_recv()
        for cp in sends:
            cp.wait_send()
        for cp in own:
            cp.wait()


def chip_scatter(pres, chunks, *, name):
    cs = ChipScatter(pres, chunks)
    n = cs.n

    def body(*refs):
        pre_refs, out_refs, sems = refs[:n], refs[n:2 * n], refs[2 * n:]
        cs.start(pre_refs, out_refs, sems)
        cs.finish(pre_refs, out_refs, sems)

    return pl.pallas_call(body, out_shape=cs.shapes, in_specs=[_ANY] * n, out_specs=[_ANY] * n,
                          scratch_shapes=cs.sems, name=name)(*pres)


def sibling_send(blks, chunks, *, name):
    n = len(blks)
    pieces = _pieces(blks, chunks)
    n_p = len(pieces)

    def body(*refs):
        x_refs, out_refs = refs[:n], refs[n:2 * n]
        send_sems, recv_sems = refs[2 * n:]
        x, y, c = _my_pos()
        cps = [pltpu.make_async_remote_copy(
            src_ref=x_refs[a].at[rows], dst_ref=out_refs[a].at[rows], send_sem=send_sems.at[p],
            recv_sem=recv_sems.at[p], device_id=(x, y, 1 - c), device_id_type=MESH)
            for p, (a, rows) in enumerate(pieces)]
        for cp in cps:
            cp.start()
        for cp in cps:
            cp.wait_recv()
        for cp in cps:
            cp.wait_send()

    return pl.pallas_call(
        body, out_shape=[jax.ShapeDtypeStruct(b.shape, b.dtype) for b in blks],
        in_specs=[_ANY] * n, out_specs=[_ANY] * n,
        scratch_shapes=[pltpu.SemaphoreType.DMA((n_p,)), pltpu.SemaphoreType.DMA((n_p,))],
        name=name)(*blks)


def reduce_slots(land, *, tr, name):
    n, R, C = land.shape

    def body(l_ref, o_ref):
        acc = l_ref[0].astype(F32)
        for s in range(1, n):
            acc = acc + l_ref[s].astype(F32)
        o_ref[...] = acc

    return pl.pallas_call(
        body, grid=(R // tr,), in_specs=[pl.BlockSpec((n, tr, C), lambda i: (0, i, 0))],
        out_specs=pl.BlockSpec((tr, C), lambda i: (i, 0)), out_shape=jax.ShapeDtypeStruct((R, C), F32),
        compiler_params=_cparams(1), name=name)(land)


TM = 512


def _tk(d):
    return min(d.shape[0], 1024)


def ffn_fwd_fused(x, g, wgu, wd, *, tm, name, riders=None, rider_args=()):
    T, Dm = x.shape
    nb, _, cb = wgu.shape
    nh = nb // 2
    Fd = nh * cb
    nc = riders.n if riders is not None else 0
    n_steps = T // tm
    stage_at = (0, (2 * n_steps) // 3, n_steps - 1, n_steps - 1)

    def body(*refs):
        x_ref, g_ref, wgu_ref, wd_ref = refs[:4]
        r_in = refs[4:4 + nc]
        o_ref, gu_ref, h_ref = refs[4 + nc:7 + nc]
        r_out, sems = refs[7 + nc:7 + 2 * nc], refs[7 + 2 * nc:]

        def ride(which):
            if riders is not None:
                @pl.when(pl.program_id(0) == stage_at[which])
                def _():
                    riders.stage(which, r_in, r_out, sems)

        ride(0)
        ride(1)
        xv = x_ref[...]
        r = lax.rsqrt(jnp.mean(xv * xv, axis=-1, keepdims=True) + RMS_EPS)
        hb = (xv * r * g_ref[...]).astype(BF16)
        h_ref[...] = hb
        acc = jnp.zeros((tm, Dm), F32)
        for jj in range(nh):
            cols = pl.ds(cb * jj, cb)
            gate = _dot(hb, wgu_ref[jj]).astype(BF16)
            up = _dot(hb, wgu_ref[nh + jj]).astype(BF16)
            gu_ref[0, :, cols] = gate
            gu_ref[1, :, cols] = up
            gv = gate.astype(F32)
            act = (gv * _sigmoid(gv) * up.astype(F32)).astype(BF16)
            acc = acc + _dot(act, wd_ref[cols, :])
        o_ref[...] = xv + 0.5 * acc
        ride(2)
        ride(3)

    outs = pl.pallas_call(
        body, grid=(n_steps,),
        in_specs=[pl.BlockSpec((tm, Dm), lambda i: (i, 0)), pl.BlockSpec((1, Dm), lambda i: (0, 0)),
                  pl.BlockSpec((nb, Dm, cb), lambda i: (0, 0, 0)), pl.BlockSpec((Fd, Dm), lambda i: (0, 0))]
        + [_ANY] * nc,
        out_specs=[pl.BlockSpec((tm, Dm), lambda i: (i, 0)), pl.BlockSpec((2, tm, Fd), lambda i: (0, i, 0)),
                   pl.BlockSpec((tm, Dm), lambda i: (i, 0))] + [_ANY] * nc,
        out_shape=[jax.ShapeDtypeStruct((T, Dm), F32), jax.ShapeDtypeStruct((2, T, Fd), BF16),
                   jax.ShapeDtypeStruct((T, Dm), BF16)] + (riders.shapes if nc else []),
        scratch_shapes=riders.sems if nc else [],
        compiler_params=_cparams(1), name=name)(x, g, wgu, wd, *rider_args)
    return outs[0], outs[1], outs[2], list(outs[3:])


def ffn_fwd(x, g, wgu, wd, tag, riders=None, rider_args=()):
    xo, gu, h, rode = ffn_fwd_fused(x, g, wgu, wd, tm=256, name=f"{tag}_fwd", riders=riders, rider_args=rider_args)
    return xo, (x, gu, h), rode


def ffn_bwd(d, saved, g, wgu, wd, tag, ride_bact=None, ride_dwgu=None, before_dx=None):
    x, gu, h = saved
    dgu, dwd, *rode_a = ffn_bwd_act(d, wd, gu, tm=TM, tn=1408, name=f"{tag}_bact", riding=ride_bact)
    dwgu = mm_tn(h, dgu, scale=1.0, a_split=False, b_split=True, tm=TM, tn=1408, tk=_tk(d), out_blocked=True,
                 name=f"{tag}_dwgu", riding=ride_dwgu)
    rode_g = []
    if ride_dwgu is not None:
        dwgu, *rode_g = dwgu
    riding = before_dx(dwgu, dwd) if before_dx is not None else None
    dx, dg, *rode_x = mm_nt_normbwd(dgu, wgu, x, g, d, a_split=True, tm=_tk(d), tk=1408, name=f"{tag}_dx",
                                    riding=riding)
    return dx, dg, dwgu, dwd, (rode_a, rode_g, rode_x)


def _tile2(v):
    return jnp.concatenate([v, v], axis=-1).reshape(1, LANES)


def _fold2(v):
    return v[:, :HEAD_DIM] + v[:, HEAD_DIM:]


EVEN = dict(dil=1, nsub=2, ppk=4, q_blk=0, k_blk=4, v_blk=5, n_heads=A_Q_HEADS, group=A_GROUP, max_dist=A_WINDOW - 1)
STICK = dict(q_blk=6, k_blk=10, v_blk=14, n_pairs=4)


def _odd_cfg(dil):
    return dict(dil=dil, nsub=4 if dil == 1 else 1, ppk=1, q_blk=0, k_blk=8, v_blk=16, n_heads=C_HEADS, group=1,
                max_dist=BLK)


def even_fwd(x, g, win, qg, kg, sinks, wout, tag, riders=None, rider_args=()):
    qkv, h = norm_matmul(x, g, win, tm=_tk(x), tn=1152, split=False, name=f"{tag}_in")
    qg2, kg2 = _tile2(qg), _tile2(kg)
    slopes = jnp.asarray(_alibi(A_Q_HEADS), F32)
    qkn = qk_norm(qkv, qg2, kg2, width=768, steps=1, n_q=4, tm=TM, name=f"{tag}_qkn")
    oa, _, *lse = banded_fwd(qkn, qkv, slopes, sinks, name=f"{tag}_swa", **EVEN)
    ob, rode = stick_fwd(qkv, name=f"{tag}_stick", riders=riders, rider_args=rider_args, **STICK)
    o = jnp.concatenate([oa, ob], axis=1)
    xo = mm_nn(o, wout, res=x, tm=TM, tn=D_MODEL, tk=D_MODEL, name=f"{tag}_out")
    return xo, (x, qkv, qkn, h, oa, lse, o), rode


def even_bwd(d, saved, g, win, qg, kg, sinks, wout, tag, riders=None, rider_args=(), before_dx=None):
    x, qkv, qkn, h, oa, lse, o = saved
    qg2, kg2 = _tile2(qg), _tile2(kg)
    slopes = jnp.asarray(_alibi(A_Q_HEADS), F32)
    dwout = mm_tn(o, d, scale=1.0, a_split=False, b_split=False, tm=D_MODEL, tn=D_MODEL, tk=_tk(d), name=f"{tag}_dwout")
    do = mm_nt(d, wout, tm=TM, tn=D_MODEL, tk=D_MODEL, name=f"{tag}_do")
    dqa, dka4, dva4, dsk = banded_bwd(qkn, qkv, slopes, sinks, do, oa, lse, None, None,
                                      do_blk=0, name=f"{tag}_swa_b", **EVEN)
    dqb, dkb, dvb, rode = stick_bwd(qkv, do, do_blk=4, name=f"{tag}_stick_b", riders=riders, rider_args=rider_args,
                                    **STICK)
    dqkv, dqg, dkg = assemble_even(dqa, dka4, dva4, dqb, dkb, dvb, qkv, qg2, kg2, tm=TM, name=f"{tag}_asm")
    dwin = mm_tn(h, dqkv, scale=1.0, a_split=False, b_split=False, tm=D_MODEL, tn=1152, tk=_tk(d), name=f"{tag}_dwin")
    riding = before_dx(dwin, dwout) if before_dx is not None else None
    dx, dg, *rode_x = mm_nt_normbwd(dqkv, win, x, g, d, a_split=False, tm=TM, tk=1152, name=f"{tag}_dx", riding=riding)
    return dx, dg, dwin, _fold2(dqg), _fold2(dkg), dsk[:, :A_Q_HEADS], dwout, (rode, rode_x)


def odd_fwd(x, g, win, qg, kg, wout, tag):
    qkv, h = norm_matmul(x, g, win, tm=_tk(x), tn=768, split=False, name=f"{tag}_in")
    qg2, kg2 = _tile2(qg), _tile2(kg)
    qkn = qk_norm(qkv, qg2, kg2, width=D_MODEL, steps=2, n_q=8, tm=TM, name=f"{tag}_qkn")
    outs = []
    for p, (window, dil) in enumerate(C_PATTERNS):
        slopes = jnp.asarray(_alibi(C_HEADS), F32) * float(dil)
        outs.append(banded_fwd(qkn, qkv, slopes, None, name=f"{tag}_dil{p}", **_odd_cfg(dil)))
    o, w1, w2, w3 = mix_fwd(outs[0][0], outs[1][0], outs[2][0], outs[0][1], outs[1][1], outs[2][1],
                            tm=TM, name=f"{tag}_mix")
    xo = mm_nn(o, wout, res=x, tm=TM, tn=D_MODEL, tk=D_MODEL, name=f"{tag}_out")
    return xo, (x, qkv, qkn, h, outs, (w1, w2, w3), o)


def odd_bwd(d, saved, g, win, qg, kg, wout, tag):
    x, qkv, qkn, h, outs, ws, o = saved
    qg2, kg2 = _tile2(qg), _tile2(kg)
    dwout = mm_tn(o, d, scale=1.0, a_split=False, b_split=False, tm=D_MODEL, tn=D_MODEL, tk=_tk(d), name=f"{tag}_dwout")
    do = mm_nt(d, wout, tm=TM, tn=D_MODEL, tk=D_MODEL, name=f"{tag}_do")
    parts = []
    for p, (window, dil) in enumerate(C_PATTERNS):
        slopes = jnp.asarray(_alibi(C_HEADS), F32) * float(dil)
        dq, dk, dv, _ = banded_bwd(qkn, qkv, slopes, None, do, None, outs[p][2:], ws[p], o,
                                   do_blk=0, name=f"{tag}_dil{p}_b", **_odd_cfg(dil))
        parts.append((dq, dk, dv))
    dqkv, dqg, dkg = assemble_odd(parts, qkv, qg2, kg2, tm=256, name=f"{tag}_asm")
    dwin = mm_tn(h, dqkv, scale=1.0, a_split=False, b_split=False, tm=TM, tn=768, tk=_tk(d), out_blocked=True,
                 name=f"{tag}_dwin")
    dx, dg = mm_nt_normbwd(dqkv, win, x, g, d, a_split=False, tm=TM, tk=768, name=f"{tag}_dx")
    return dx, dg, dwin, _fold2(dqg), _fold2(dkg), dwout


def xa_fwd(x, mem, g, gm, wq, wkv, qg, kg, wo, tag):
    qraw, h = norm_matmul(x, g, wq, tm=TM, tn=D_MODEL, split=False, name=f"{tag}_q")
    kvraw, hm = norm_matmul(mem, gm, wkv, tm=MEM_LEN, tn=512, split=False, name=f"{tag}_kv")
    o = xattn_fwd(qraw, kvraw, qg, kg, tm=TM, name=f"{tag}_att")
    xo = mm_nn(o, wo, res=x, tm=TM, tn=D_MODEL, tk=D_MODEL, name=f"{tag}_o")
    return xo, (x, qraw, h, kvraw, hm, o)


def xa_bwd(d, saved, mem, g, gm, wq, wkv, qg, kg, wo, tag):
    x, qraw, h, kvraw, hm, o = saved
    dwo = mm_tn(o, d, scale=1.0, a_split=False, b_split=False, tm=D_MODEL, tn=D_MODEL, tk=_tk(d), name=f"{tag}_dwo")
    do = mm_nt(d, wo, tm=TM, tn=D_MODEL, tk=D_MODEL, name=f"{tag}_do")
    dq, dkv, dqg, dkg = xattn_bwd(qraw, kvraw, qg, kg, do, o, tm=TM, name=f"{tag}_att_b")
    dwq = mm_tn(h, dq, scale=1.0, a_split=False, b_split=False, tm=D_MODEL, tn=D_MODEL, tk=_tk(d), name=f"{tag}_dwq")
    dx, dg = mm_nt_normbwd(dq, wq, x, g, d, a_split=False, tm=TM, tk=D_MODEL, name=f"{tag}_dx")
    dwkv = mm_tn(hm, dkv, scale=1.0, a_split=False, b_split=False, tm=TM, tn=512, tk=MEM_LEN, out_blocked=True,
                 name=f"{tag}_dwkv")
    _, dgm = mm_nt_normbwd(dkv, wkv, mem, gm, None, a_split=False, tm=MEM_LEN, tk=512, name=f"{tag}_dmem")
    return dx, dg, dgm, dwq, dwkv, dqg, dkg, dwo


MATS = (("ffn1_w_gu", 1), ("ffn1_w_down", 0), ("ev_w_in", 1), ("ev_w_out", 0), ("od_w_in", 1), ("od_w_out", 0),
        ("xa_w_q", 0), ("xa_w_kv", 1), ("xa_w_o", 0), ("ffn2_w_gu", 1), ("ffn2_w_down", 0))
SMALLS = ("ffn1_norm", "mix_norm", "ev_q_gain", "ev_k_gain", "ev_sinks", "od_q_gain", "od_k_gain", "xa_norm",
          "xa_mem_norm", "xa_q_gain", "xa_k_gain", "ffn2_norm")
WEIGHTS = ("ffn1_norm", "ffn1_w_gu", "ffn1_w_down", "mix_norm", "ev_w_in", "ev_q_gain", "ev_k_gain", "ev_sinks",
           "ev_w_out", "od_w_in", "od_q_gain", "od_k_gain", "od_w_out", "xa_norm", "xa_mem_norm", "xa_w_q",
           "xa_w_kv", "xa_q_gain", "xa_k_gain", "xa_w_o", "ffn2_norm", "ffn2_w_gu", "ffn2_w_down")
SMALL_ROWS = 16
LAYER_GROUPS = (
    (((("ffn1_w_gu", 0), ("ffn2_w_gu", 0)), 4, 512),
     ((("ffn1_w_down", 0), ("ffn2_w_down", 0)), 2, 352),
     ((("ev_w_out", 0), ("xa_w_q", 0), ("xa_w_o", 0)), 1, 384),
     ((("xa_w_kv", 0),), 1, 512),
     ((("ev_w_in", 0),), 1, 512)),
    (((("ffn1_w_gu", 1), ("ffn2_w_gu", 1)), 4, 512),
     ((("ffn1_w_down", 1), ("ffn2_w_down", 1)), 2, 352),
     ((("od_w_out", 0), ("xa_w_q", 1), ("xa_w_o", 1)), 1, 384),
     ((("xa_w_kv", 1),), 1, 512),
     ((("od_w_in", 0),), 1, 512)),
)
GATHER_FIRST = (((("ffn1_w_gu", 0),), 2, 512), ((("ffn1_w_down", 0),), 1, 352), ((("ev_w_out", 0),), 1, 128),
                ((("ev_w_in", 0),), 1, 512))
GATHER_IN_FFN1 = (((("ffn2_w_gu", 0),), 2, 512), ((("ffn2_w_down", 0),), 1, 352),
                  ((("xa_w_q", 0), ("xa_w_o", 0)), 1, 256), ((("xa_w_kv", 0),), 1, 512))
GATHER_IN_STICK = (((("ffn1_w_gu", 1),), 2, 512), ((("ffn1_w_down", 1),), 1, 352), ((("od_w_out", 0),), 1, 128),
                   ((("od_w_in", 0),), 1, 512))
GATHER_IN_FFN2 = (((("ffn2_w_gu", 1),), 2, 512), ((("ffn2_w_down", 1),), 1, 352),
                  ((("xa_w_q", 1), ("xa_w_o", 1)), 1, 256), ((("xa_w_kv", 1),), 1, 512))
ROUNDS = {
    "1": LAYER_GROUPS[1],
    "0a": (((("ffn2_w_gu", 0),), 2, 512), ((("ffn2_w_down", 0),), 1, 352)) + LAYER_GROUPS[0][2:],
    "0b": (((("ffn1_w_gu", 0),), 2, 512), ((("ffn1_w_down", 0),), 1, 352)),
}


def _chunks_of(groups):
    return tuple(g[1] for g in groups)
COL_SHARDED = {name for name, axis in MATS if axis == 1}
BLOCKED = {"ffn1_w_gu", "ffn2_w_gu", "xa_w_kv", "od_w_in"}


def group_halves(shards, c, groups):
    out = []
    for members, _, _ in groups:
        halves = []
        for name, layer in members:
            _, r, cc = shards[name].shape
            half = lax.dynamic_index_in_dim(shards[name][layer].reshape(2, r // 2, cc), c, 0, keepdims=False)
            halves.append(half.astype(BF16))
        out.append(jnp.concatenate(halves, axis=0))
    return out


def full_weights(gathered, shards, groups):
    full = {}
    for (members, _, _), arr in zip(groups, gathered):
        for w, (name, layer) in enumerate(members):
            _, r, cc = shards[name].shape
            piece = arr[:, w * (r // 2):(w + 1) * (r // 2)].reshape(4, r, cc)
            if name not in COL_SHARDED:
                piece = piece.reshape(4 * r, cc)
            elif name not in BLOCKED:
                piece = piece.transpose(1, 0, 2).reshape(r, 4 * cc)
            full[(name, layer)] = piece
    return full


def group_grads(grads, shards, groups):
    out = []
    for members, _, _ in groups:
        parts = []
        for name, layer in members:
            _, r, cc = shards[name].shape
            gfull = grads[(name, layer)]
            if name in COL_SHARDED and name not in BLOCKED:
                gfull = gfull.reshape(2, r // 2, 4, cc).transpose(2, 0, 1, 3)
            parts.append(gfull.reshape(N_DEV, r // 2, cc))
        out.append(jnp.concatenate(parts, axis=1))
    return out


def shard_grads(mine, theirs, c, shards, groups):
    per = {}
    for (members, _, _), a, b in zip(groups, mine, theirs):
        for w, (name, layer) in enumerate(members):
            _, r, cc = shards[name].shape
            rows = slice(w * (r // 2), (w + 1) * (r // 2))
            lo = jnp.where(c == 0, a[rows], b[rows])
            hi = jnp.where(c == 0, b[rows], a[rows])
            per[(name, layer)] = jnp.concatenate([lo, hi], axis=0)
    return per


def pack_small(vals):
    row10 = jnp.concatenate([vals["xa_q_gain"].reshape(1, 512), vals["xa_k_gain"].reshape(1, 512)], axis=1)
    row11 = jnp.concatenate([vals["ev_q_gain"], vals["ev_k_gain"], vals["od_q_gain"], vals["od_k_gain"],
                             vals["ev_sinks"], jnp.zeros((1, 1024 - 4 * 64 - 8), F32)], axis=1)
    return jnp.concatenate([vals["ffn1_norm"], vals["mix_norm"], vals["xa_norm"], vals["xa_mem_norm"],
                            vals["ffn2_norm"], row10, row11, jnp.zeros((SMALL_ROWS - 12, 1024), F32)], axis=0)


def unpack_small(arr):
    return {"ffn1_norm": arr[0:2], "mix_norm": arr[2:4], "xa_norm": arr[4:6], "xa_mem_norm": arr[6:8],
            "ffn2_norm": arr[8:10],
            "xa_q_gain": arr[10:11, 0:512].reshape(2, 256), "xa_k_gain": arr[10:11, 512:1024].reshape(2, 256),
            "ev_q_gain": arr[11:12, 0:64], "ev_k_gain": arr[11:12, 64:128], "od_q_gain": arr[11:12, 128:192],
            "od_k_gain": arr[11:12, 192:256], "ev_sinks": arr[11:12, 256:264]}


def local_step(x, mem, target, W, small, prereduce, later):
    depth = small["ffn1_norm"].shape[0]

    def row(name, l):
        return small[name][l:l + 1]

    saved = []
    for l in range(depth):
        j = l // 2
        def riding_gather(host):
            if l == 0 and host in later:
                return GatherBlocks(later[host][0], later[host][1]), later[host][0]
            return None, ()

        riders, rider_args = riding_gather("ffn1")
        x, s1, rode = ffn_fwd(x, row("ffn1_norm", l), W[("ffn1_w_gu", l)], W[("ffn1_w_down", l)], f"l{l}_f1",
                              riders=riders, rider_args=rider_args)
        if riders is not None:
            W = {**W, **later["ffn1"][2](rode)}
        if l % 2 == 0:
            riders, rider_args = riding_gather("stick")
            x, s2, rode = even_fwd(x, row("mix_norm", l), W[("ev_w_in", j)], row("ev_q_gain", j),
                                   row("ev_k_gain", j), small["ev_sinks"][j], W[("ev_w_out", j)], f"l{l}_ev",
                                   riders=riders, rider_args=rider_args)
            if riders is not None:
                W = {**W, **later["stick"][2](rode)}
        else:
            x, s2 = odd_fwd(x, row("mix_norm", l), W[("od_w_in", j)], row("od_q_gain", j), row("od_k_gain", j),
                            W[("od_w_out", j)], f"l{l}_od")
        x, s3 = xa_fwd(x, mem, row("xa_norm", l), row("xa_mem_norm", l), W[("xa_w_q", l)], W[("xa_w_kv", l)],
                       row("xa_q_gain", l), row("xa_k_gain", l), W[("xa_w_o", l)], f"l{l}_xa")
        riders, rider_args = riding_gather("ffn2")
        x, s4, rode = ffn_fwd(x, row("ffn2_norm", l), W[("ffn2_w_gu", l)], W[("ffn2_w_down", l)], f"l{l}_f2",
                              riders=riders, rider_args=rider_args)
        if riders is not None:
            W = {**W, **later["ffn2"][2](rode)}
        saved.append((s1, s2, s3, s4))
    loss, d = loss_kernel(x, target, tm=TM, name="loss")

    gw = {}
    gs = {name: [None] * small[name].shape[0] for name in SMALLS}
    pending, landed = None, {}
    for l in reversed(range(depth)):
        j = l // 2
        s1, s2, s3, s4 = saved[l]
        d, dg, dwgu, dwd, _ = ffn_bwd(d, s4, row("ffn2_norm", l), W[("ffn2_w_gu", l)], W[("ffn2_w_down", l)],
                                      f"l{l}_f2")
        gs["ffn2_norm"][l] = dg
        gw[("ffn2_w_gu", l)], gw[("ffn2_w_down", l)] = dwgu, dwd
        d, dg, dgm, dwq, dwkv, dqg, dkg, dwo = xa_bwd(
            d, s3, mem, row("xa_norm", l), row("xa_mem_norm", l), W[("xa_w_q", l)], W[("xa_w_kv", l)],
            row("xa_q_gain", l), row("xa_k_gain", l), W[("xa_w_o", l)], f"l{l}_xa")
        gs["xa_norm"][l], gs["xa_mem_norm"][l], gs["xa_q_gain"][l], gs["xa_k_gain"][l] = dg, dgm, dqg, dkg
        gw[("xa_w_q", l)], gw[("xa_w_kv", l)], gw[("xa_w_o", l)] = dwq, dwkv, dwo
        split = l == 0 and l % 2 == 0 and ("0a" in ROUNDS)
        early = []
        pre_early = None
        if l % 2 == 0:
            riders, rider_args = None, ()
            if pending is not None:
                riders, rider_args = ChipScatter(pending[1], _chunks_of(ROUNDS[pending[0]])), pending[1]

            def before_mixer_dx(dwin, dwout, j=j, early=early):
                gw[("ev_w_in", j)], gw[("ev_w_out", j)] = dwin, dwout
                early += prereduce.pack(gw, "0a")
                return Riding([(PairExchange(early, _chunks_of(ROUNDS["0a"])), early)])

            d, dg, dwin, dqg, dkg, dsk, dwout, (rode, rode_x) = even_bwd(
                d, s2, row("mix_norm", l), W[("ev_w_in", j)], row("ev_q_gain", j), row("ev_k_gain", j),
                small["ev_sinks"][j], W[("ev_w_out", j)], f"l{l}_ev", riders=riders, rider_args=rider_args,
                before_dx=before_mixer_dx if split else None)
            if pending is not None:
                landed[pending[0]], pending = rode, None
            gs["ev_q_gain"][j], gs["ev_k_gain"][j], gs["ev_sinks"][j] = dqg, dkg, dsk
            gw[("ev_w_in", j)], gw[("ev_w_out", j)] = dwin, dwout
            if split:
                pre_early = prereduce.sums(early, rode_x[0], "0a")
        else:
            d, dg, dwin, dqg, dkg, dwout = odd_bwd(
                d, s2, row("mix_norm", l), W[("od_w_in", j)], row("od_q_gain", j), row("od_k_gain", j),
                W[("od_w_out", j)], f"l{l}_od")
            gs["od_q_gain"][j], gs["od_k_gain"][j] = dqg, dkg
            gw[("od_w_in", j)], gw[("od_w_out", j)] = dwin, dwout
        gs["mix_norm"][l] = dg
        packed = []

        rnd = "0b" if pre_early is not None else str(l)
        final = l == 0

        def before_dx(dwgu, dwd, l=l, packed=packed, rnd=rnd, final=final):
            gw[("ffn1_w_gu", l)], gw[("ffn1_w_down", l)] = dwgu, dwd
            packed += prereduce.pack(gw, rnd)
            chunks = _chunks_of(ROUNDS[rnd])
            if not final:
                return Riding([(PairExchange(packed, chunks), packed)])
            sib = _standalone(PairExchange(packed, chunks), packed, f"pair_grads{rnd}")
            pre = prereduce.sums(packed, sib, rnd)
            return Riding([(ChipScatter(pre, chunks), pre)])

        ride_bact = ride_dwgu = None
        if pre_early is not None:
            chunks = _chunks_of(ROUNDS["0a"])
            ride_bact = Riding([(ChipScatter(pre_early[:2], chunks[:2]), pre_early[:2])])
            ride_dwgu = Riding([(ChipScatter(pre_early[2:], chunks[2:]), pre_early[2:])])
        d, dg, dwgu, dwd, (rode_a, rode_g, rode_x) = ffn_bwd(
            d, s1, row("ffn1_norm", l), W[("ffn1_w_gu", l)], W[("ffn1_w_down", l)], f"l{l}_f1",
            ride_bact=ride_bact, ride_dwgu=ride_dwgu, before_dx=before_dx)
        gs["ffn1_norm"][l] = dg
        if pre_early is not None:
            landed["0a"] = list(rode_a[0]) + list(rode_g[0])
        if pending is not None:
            landed[pending[0]] = chip_scatter(pending[1], _chunks_of(ROUNDS[pending[0]]),
                                              name=f"scatter_grads{pending[0]}")
        if final:
            landed[rnd], pending = rode_x[0], None
        else:
            pending = (rnd, prereduce.sums(packed, rode_x[0], rnd))
    if pending is not None:
        landed[pending[0]] = chip_scatter(pending[1], _chunks_of(ROUNDS[pending[0]]),
                                          name=f"scatter_grads{pending[0]}")
    gsmall = {name: jnp.concatenate(v, axis=0) for name, v in gs.items()}
    return loss, d, landed, gsmall


def kernel(x, mem, ffn1_norm, ffn1_w_gu, ffn1_w_down, mix_norm, ev_w_in, ev_q_gain, ev_k_gain, ev_sinks, ev_w_out, od_w_in, od_q_gain, od_k_gain, od_w_out, xa_norm, xa_mem_norm, xa_w_q, xa_w_kv, xa_q_gain, xa_k_gain, xa_w_o, ffn2_norm, ffn2_w_gu, ffn2_w_down, loss_target, m_ffn1_norm, m_ffn1_w_gu, m_ffn1_w_down, m_mix_norm, m_ev_w_in, m_ev_q_gain, m_ev_k_gain, m_ev_sinks, m_ev_w_out, m_od_w_in, m_od_q_gain, m_od_k_gain, m_od_w_out, m_xa_norm, m_xa_mem_norm, m_xa_w_q, m_xa_w_kv, m_xa_q_gain, m_xa_k_gain, m_xa_w_o, m_ffn2_norm, m_ffn2_w_gu, m_ffn2_w_down, v_ffn1_norm, v_ffn1_w_gu, v_ffn1_w_down, v_mix_norm, v_ev_w_in, v_ev_q_gain, v_ev_k_gain, v_ev_sinks, v_ev_w_out, v_od_w_in, v_od_q_gain, v_od_k_gain, v_od_w_out, v_xa_norm, v_xa_mem_norm, v_xa_w_q, v_xa_w_kv, v_xa_q_gain, v_xa_k_gain, v_xa_w_o, v_ffn2_norm, v_ffn2_w_gu, v_ffn2_w_down):
    given = dict(locals())
    w = {n: given[n] for n in WEIGHTS}
    m = {n: given["m_" + n] for n in WEIGHTS}
    v = {n: given["v_" + n] for n in WEIGHTS}
    c = lax.axis_index("c")
    shards = {name: w[name] for name, _ in MATS}
    small = {n: w[n] for n in SMALLS}

    def gathering(groups):
        return group_halves(shards, c, groups), _chunks_of(groups), lambda got: full_weights(got, shards, groups)

    halves, chunks, unpack = gathering(GATHER_FIRST)
    full = unpack(gather_blocks(halves, chunks, name="gather_weights0"))
    later = {"ffn1": gathering(GATHER_IN_FFN1), "stick": gathering(GATHER_IN_STICK), "ffn2": gathering(GATHER_IN_FFN2)}

    class prereduce:
        @staticmethod
        def pack(gw, rnd):
            return group_grads(gw, shards, ROUNDS[rnd])

        @staticmethod
        def sums(packed, sib, rnd):
            return [pair_sum(p, s, c, tr=g[2], name=f"pair_sum{rnd}_{i}")
                    for i, (g, p, s) in enumerate(zip(ROUNDS[rnd], packed, sib))]

    loss_b, grad_x, landed, gsmall = local_step(x[0], mem[0], loss_target[0], full, small, prereduce, later)

    per = {}
    for rnd, land in sorted(landed.items()):
        groups = ROUNDS[rnd]
        mine = [reduce_slots(a, tr=g[2], name=f"sum_grads{rnd}_{i}") for i, (g, a) in enumerate(zip(groups, land))]
        theirs = sibling_send(mine, _chunks_of(groups), name=f"swap_grads{rnd}")
        per.update(shard_grads(mine, theirs, c, shards, groups))
    g = {name: jnp.stack([per[(name, layer)] for layer in range(w[name].shape[0])], axis=0) for name, _ in MATS}
    land_small = gather_small(pack_small(gsmall), name="gather_small")
    g_small = unpack_small(reduce_slots(land_small, tr=SMALL_ROWS, name="sum_small"))
    g.update(g_small)

    delta, new_m, new_v = {}, {}, {}
    for name, _ in MATS:
        shp = w[name].shape
        flat = [a.reshape(-1, shp[-1]) for a in (w[name], g[name], m[name], v[name])]
        dl, nm, nv = adamw(*flat, br=BLK, name=f"adamw_{name}")
        delta[name], new_m[name], new_v[name] = dl.reshape(shp), nm.reshape(shp), nv.reshape(shp)
    dl, nm, nv = adamw(pack_small(small), pack_small(g_small), pack_small({n: m[n] for n in SMALLS}),
                       pack_small({n: v[n] for n in SMALLS}), br=SMALL_ROWS, name="adamw_small")
    for dst, arr in ((delta, dl), (new_m, nm), (new_v, nv)):
        dst.update(unpack_small(arr))

    loss = lax.psum(loss_b[0, 0], ("x", "y", "c"))
    return (loss, grad_x[None], *[g[n] for n in WEIGHTS], *[delta[n] for n in WEIGHTS],
            *[new_m[n] for n in WEIGHTS], *[new_v[n] for n in WEIGHTS])
```

```python
import jax
import jax.numpy as jnp
from jax import lax
from jax.experimental import pallas as pl
from jax.experimental.pallas import tpu as pltpu

F32 = jnp.float32
BF16 = jnp.bfloat16

D_MODEL = 1024
HEAD_DIM = 64
LANES = 128
BLK = 128
D_FF = 2816
RMS_EPS = 1e-6
MEM_LEN = 256
X_HEADS = 4
X_HEAD_DIM = 256
A_Q_HEADS = 8
A_GROUP = 4
A_WINDOW = 128
C_HEADS = 16
C_PATTERNS = ((128, 1), (512, 4), (2048, 16))
NEG = -1e30
VMEM_LIMIT = 56 * 2 ** 20

ADAM_LR = 0.001
ADAM_B1 = 0.9
ADAM_B2 = 0.999
ADAM_EPS = 1e-08
ADAM_WD = 0.01
ADAM_STEP = 10

N_DEV = 8
MESH = pl.DeviceIdType.MESH


def _cparams(n):
    return pltpu.CompilerParams(dimension_semantics=("arbitrary",) * n, vmem_limit_bytes=VMEM_LIMIT)


def _dot(a, b):
    return jnp.dot(a, b, preferred_element_type=F32)


def _dot_nt(a, b):
    return lax.dot_general(a, b, (((1,), (1,)), ((), ())), preferred_element_type=F32)


def _dot_tn(a, b):
    return lax.dot_general(a, b, (((0,), (0,)), ((), ())), preferred_element_type=F32)


def _sigmoid(z):
    return 1.0 / (1.0 + jnp.exp(-z))


def norm_matmul(x, g, w, *, tm, tn, split, name):
    T, K = x.shape
    blocked = w.ndim == 3
    assert not blocked or w.shape[2] == tn
    N = w.shape[0] * w.shape[2] if blocked else w.shape[1]
    nj = N // tn

    def body(x_ref, g_ref, w_ref, o_ref, h_ref):
        @pl.when(pl.program_id(1) == 0)
        def _():
            xv = x_ref[...]
            r = lax.rsqrt(jnp.mean(xv * xv, axis=-1, keepdims=True) + RMS_EPS)
            h_ref[...] = (xv * r * g_ref[...]).astype(BF16)

        o_ref[...] = _dot(h_ref[...], w_ref[...]).astype(o_ref.dtype)

    if split:
        njh = nj // 2
        o_shape = jax.ShapeDtypeStruct((2, T, N // 2), BF16)
        o_spec = pl.BlockSpec((None, tm, tn), lambda i, j: (j // njh, i, j % njh))
    else:
        o_shape = jax.ShapeDtypeStruct((T, N), F32)
        o_spec = pl.BlockSpec((tm, tn), lambda i, j: (i, j))
    return pl.pallas_call(
        body, grid=(T // tm, nj),
        in_specs=[pl.BlockSpec((tm, K), lambda i, j: (i, 0)),
                  pl.BlockSpec((1, K), lambda i, j: (0, 0)),
                  (pl.BlockSpec((None, K, tn), lambda i, j: (j, 0, 0)) if blocked
                   else pl.BlockSpec((K, tn), lambda i, j: (0, j)))],
        out_specs=[o_spec, pl.BlockSpec((tm, K), lambda i, j: (i, 0))],
        out_shape=[o_shape, jax.ShapeDtypeStruct((T, K), BF16)],
        compiler_params=_cparams(2), name=name)(x, g, w)


def mm_nn(a, b, *, res, tm, tn, tk, name):
    T = a.shape[0]
    K, N = b.shape
    nk = K // tk

    def body(a_ref, b_ref, r_ref, o_ref, acc):
        k = pl.program_id(2)

        @pl.when(k == 0)
        def _():
            acc[...] = jnp.zeros_like(acc)

        acc[...] += _dot(a_ref[...].astype(BF16), b_ref[...])

        @pl.when(k == nk - 1)
        def _():
            o_ref[...] = r_ref[...] + acc[...]

    return pl.pallas_call(
        body, grid=(T // tm, N // tn, nk),
        in_specs=[pl.BlockSpec((tm, tk), lambda i, j, k: (i, k)), pl.BlockSpec((tk, tn), lambda i, j, k: (k, j)),
                  pl.BlockSpec((tm, tn), lambda i, j, k: (i, j))],
        out_specs=pl.BlockSpec((tm, tn), lambda i, j, k: (i, j)),
        out_shape=jax.ShapeDtypeStruct((T, N), F32),
        scratch_shapes=[pltpu.VMEM((tm, tn), F32)],
        compiler_params=_cparams(3), name=name)(a, b, res)


def mm_nt(a, b, *, tm, tn, tk, name):
    T, K = a.shape
    N = b.shape[0]
    nk = K // tk

    def body(a_ref, b_ref, o_ref, acc):
        k = pl.program_id(2)

        @pl.when(k == 0)
        def _():
            acc[...] = jnp.zeros_like(acc)

        acc[...] += _dot_nt(a_ref[...].astype(BF16), b_ref[...])

        @pl.when(k == nk - 1)
        def _():
            o_ref[...] = acc[...]

    return pl.pallas_call(
        body, grid=(T // tm, N // tn, nk),
        in_specs=[pl.BlockSpec((tm, tk), lambda i, j, k: (i, k)),
                  pl.BlockSpec((tn, tk), lambda i, j, k: (j, k))],
        out_specs=pl.BlockSpec((tm, tn), lambda i, j, k: (i, j)),
        out_shape=jax.ShapeDtypeStruct((T, N), F32),
        scratch_shapes=[pltpu.VMEM((tm, tn), F32)],
        compiler_params=_cparams(3), name=name)(a, b)


def ffn_bwd_act(d, wd, gu, *, tm, tn, name, riding=None):
    T, K = d.shape
    Fd = wd.shape[0]
    ni = T // tm

    def body(d_ref, w_ref, g_ref, u_ref, dgu_ref, dwd_ref, acc):
        i = pl.program_id(1)

        @pl.when(i == 0)
        def _():
            acc[...] = jnp.zeros_like(acc)

        db = d_ref[...].astype(BF16)
        da = 0.5 * _dot_nt(db, w_ref[...])
        gv = g_ref[...].astype(F32)
        uv = u_ref[...].astype(F32)
        s = _sigmoid(gv)
        silu = gv * s
        acc[...] += _dot_tn((silu * uv).astype(BF16), db)
        dgu_ref[0] = (da * uv * (s * (1.0 + gv * (1.0 - s)))).astype(BF16)
        dgu_ref[1] = (da * silu).astype(BF16)

        @pl.when(i == ni - 1)
        def _():
            dwd_ref[...] = (0.5 * acc[...]).astype(BF16)

    in_specs = [pl.BlockSpec((tm, K), lambda j, i: (i, 0)),
                pl.BlockSpec((tn, K), lambda j, i: (j, 0)),
                pl.BlockSpec((None, tm, tn), lambda j, i: (0, i, j)),
                pl.BlockSpec((None, tm, tn), lambda j, i: (1, i, j))]
    out_specs = [pl.BlockSpec((2, tm, tn), lambda j, i: (0, i, j)), pl.BlockSpec((tn, K), lambda j, i: (j, 0))]
    out_shape = [jax.ShapeDtypeStruct((2, T, Fd), BF16), jax.ShapeDtypeStruct((Fd, K), BF16)]
    return _call_with_riders(body, riding, (Fd // tn, ni), in_specs, out_specs, out_shape,
                             [pltpu.VMEM((tn, K), F32)], [d, wd, gu, gu], name)


def _call_with_riders(body, riding, grid, in_specs, out_specs, out_shape, scratch, args, name):
    n_out = len(out_shape)
    if riding is None:
        return pl.pallas_call(body, grid=grid, in_specs=in_specs, out_specs=out_specs, out_shape=out_shape,
                              scratch_shapes=scratch, compiler_params=_cparams(len(grid)), name=name)(*args)

    def is_first():
        ok = pl.program_id(0) == 0
        for ax in range(1, len(grid)):
            ok = ok & (pl.program_id(ax) == 0)
        return ok

    def is_last():
        ok = pl.program_id(0) == grid[0] - 1
        for ax in range(1, len(grid)):
            ok = ok & (pl.program_id(ax) == grid[ax] - 1)
        return ok

    outs = pl.pallas_call(
        riding.wrap(body, len(in_specs), n_out, len(scratch), is_first, is_last), grid=grid,
        in_specs=list(in_specs) + riding.in_specs, out_specs=list(out_specs) + riding.out_specs,
        out_shape=list(out_shape) + riding.out_shapes, scratch_shapes=list(scratch) + riding.scratch,
        compiler_params=_cparams(len(grid)), name=name)(*args, *riding.args)
    core, per = riding.split(outs, n_out)
    return (*core, *per)


def mm_nt_normbwd(a, b, x, g, res, *, a_split, tm, tk, name, riding=None):
    T, Dm = x.shape
    blocked = b.ndim == 3
    assert not blocked or b.shape[2] == tk
    K = b.shape[0] * b.shape[2] if blocked else b.shape[1]
    nk = K // tk
    nkh = nk // 2
    has_res = res is not None

    def body(*refs):
        if has_res:
            a_ref, b_ref, x_ref, g_ref, r_ref, dx_ref, dg_ref, acc = refs
        else:
            a_ref, b_ref, x_ref, g_ref, dx_ref, dg_ref, acc = refs
        i = pl.program_id(0)
        k = pl.program_id(1)

        @pl.when(k == 0)
        def _():
            acc[...] = jnp.zeros_like(acc)

        acc[...] += _dot_nt(a_ref[...].astype(BF16), b_ref[...])

        @pl.when(k == nk - 1)
        def _():
            xv = x_ref[...]
            r = lax.rsqrt(jnp.mean(xv * xv, axis=-1, keepdims=True) + RMS_EPS)
            xh = xv * r
            dh = acc[...]
            dxh = dh * g_ref[...]
            dx = r * (dxh - xh * jnp.mean(dxh * xh, axis=-1, keepdims=True))
            if has_res:
                dx = dx + r_ref[...]
            dx_ref[...] = dx
            part = jnp.sum(dh * xh, axis=0, keepdims=True)

            @pl.when(i == 0)
            def _():
                dg_ref[...] = part

            @pl.when(i > 0)
            def _():
                dg_ref[...] += part

    if a_split:
        a_spec = pl.BlockSpec((None, tm, tk), lambda i, k: (k // nkh, i, k % nkh))
    else:
        a_spec = pl.BlockSpec((tm, tk), lambda i, k: (i, k))
    in_specs = [a_spec,
                (pl.BlockSpec((None, Dm, tk), lambda i, k: (k, 0, 0)) if blocked
                 else pl.BlockSpec((Dm, tk), lambda i, k: (0, k))),
                pl.BlockSpec((tm, Dm), lambda i, k: (i, 0)),
                pl.BlockSpec((1, Dm), lambda i, k: (0, 0))]
    args = [a, b, x, g]
    if has_res:
        in_specs.append(pl.BlockSpec((tm, Dm), lambda i, k: (i, 0)))
        args.append(res)
    out_specs = [pl.BlockSpec((tm, Dm), lambda i, k: (i, 0)), pl.BlockSpec((1, Dm), lambda i, k: (0, 0))]
    out_shape = [jax.ShapeDtypeStruct((T, Dm), F32), jax.ShapeDtypeStruct((1, Dm), F32)]
    scratch = [pltpu.VMEM((tm, Dm), F32)]
    return _call_with_riders(body, riding, (T // tm, nk), in_specs, out_specs, out_shape, scratch, args, name)


def mm_tn(a, b, *, scale, a_split, b_split, tm, tn, tk, name, out_blocked=False, riding=None):
    T = a.shape[-2]
    M = a.shape[-1] * (2 if a_split else 1)
    N = b.shape[-1] * (2 if b_split else 1)
    ni, nj, nk = M // tm, N // tn, T // tk
    nih, njh = ni // 2, nj // 2

    def body(a_ref, b_ref, o_ref, acc):
        k = pl.program_id(2)

        @pl.when(k == 0)
        def _():
            acc[...] = jnp.zeros_like(acc)

        acc[...] += _dot_tn(a_ref[...].astype(BF16), b_ref[...].astype(BF16))

        @pl.when(k == nk - 1)
        def _():
            o_ref[...] = (acc[...] * scale).astype(o_ref.dtype)

    if a_split:
        a_spec = pl.BlockSpec((None, tk, tm), lambda i, j, k: (i // nih, k, i % nih))
    else:
        a_spec = pl.BlockSpec((tk, tm), lambda i, j, k: (k, i))
    if b_split:
        b_spec = pl.BlockSpec((None, tk, tn), lambda i, j, k: (j // njh, k, j % njh))
    else:
        b_spec = pl.BlockSpec((tk, tn), lambda i, j, k: (k, j))
    if out_blocked:
        o_spec = pl.BlockSpec((None, None, tm, tn), lambda i, j, k: (j, i, 0, 0))
        o_shape = jax.ShapeDtypeStruct((nj, ni, tm, tn), BF16)
    else:
        o_spec = pl.BlockSpec((tm, tn), lambda i, j, k: (i, j))
        o_shape = jax.ShapeDtypeStruct((M, N), BF16)
    outs = _call_with_riders(body, riding, (ni, nj, nk), [a_spec, b_spec], [o_spec], [o_shape],
                             [pltpu.VMEM((tm, tn), F32)], [a, b], name)
    return outs[0] if riding is None else tuple(outs)


def loss_kernel(y, target, *, tm, name):
    T, Dm = y.shape

    def body(y_ref, t_ref, l_ref, dy_ref):
        e = y_ref[...] - t_ref[...]
        dy_ref[...] = e * (1.0 / Dm)
        part = (0.5 / Dm) * jnp.sum(jnp.sum(e * e, axis=-1, keepdims=True), axis=0, keepdims=True)
        part = jnp.broadcast_to(part, (8, LANES))

        @pl.when(pl.program_id(0) == 0)
        def _():
            l_ref[...] = part

        @pl.when(pl.program_id(0) > 0)
        def _():
            l_ref[...] += part

    return pl.pallas_call(
        body, grid=(T // tm,),
        in_specs=[pl.BlockSpec((tm, Dm), lambda i: (i, 0)), pl.BlockSpec((tm, Dm), lambda i: (i, 0))],
        out_specs=[pl.BlockSpec((8, LANES), lambda i: (0, 0)), pl.BlockSpec((tm, Dm), lambda i: (i, 0))],
        out_shape=[jax.ShapeDtypeStruct((8, LANES), F32), jax.ShapeDtypeStruct((T, Dm), F32)],
        compiler_params=_cparams(1), name=name)(y, target)


def adamw(w, g, m, v, *, br, name):
    R, C = w.shape

    def body(w_ref, g_ref, m_ref, v_ref, d_ref, nm_ref, nv_ref):
        gv = g_ref[...]
        nm = ADAM_B1 * m_ref[...] + (1.0 - ADAM_B1) * gv
        nv = ADAM_B2 * v_ref[...] + (1.0 - ADAM_B2) * (gv * gv)
        m_hat = nm / (1.0 - ADAM_B1 ** ADAM_STEP)
        v_hat = nv / (1.0 - ADAM_B2 ** ADAM_STEP)
        d_ref[...] = -ADAM_LR * (m_hat / (jnp.sqrt(v_hat) + ADAM_EPS) + ADAM_WD * w_ref[...])
        nm_ref[...] = nm
        nv_ref[...] = nv

    spec = pl.BlockSpec((br, C), lambda i: (i, 0))
    shp = jax.ShapeDtypeStruct((R, C), F32)
    return pl.pallas_call(
        body, grid=(R // br,), in_specs=[spec] * 4, out_specs=[spec] * 3, out_shape=[shp] * 3,
        compiler_params=_cparams(1), name=name)(w, g, m, v)


def _lane0():
    return lax.broadcasted_iota(jnp.int32, (1, LANES), 1) < HEAD_DIM


def _half_sum(x, m0):
    s0 = jnp.sum(jnp.where(m0, x, 0.0), axis=-1, keepdims=True)
    s1 = jnp.sum(jnp.where(m0, 0.0, x), axis=-1, keepdims=True)
    return jnp.where(m0, s0, s1)


def _head_rms(x, m0):
    return lax.rsqrt(_half_sum(x * x, m0) * (1.0 / HEAD_DIM) + RMS_EPS)


def _alibi(n):
    return [float(2.0 ** (-8.0 * (h + 1) / n)) for h in range(n)]


def _mask_half(x, m0, e):
    return jnp.where(m0, x, 0.0) if e == 0 else jnp.where(m0, 0.0, x)


def _band_masks2(max_dist, has_prev, live):
    row = lax.broadcasted_iota(jnp.int32, (2 * BLK, 2 * BLK), 0)
    col = lax.broadcasted_iota(jnp.int32, (2 * BLK, 2 * BLK), 1)
    dist = (row & (BLK - 1)) - col + BLK
    lim = jnp.where(live, max_dist, -1)
    first = jnp.where(has_prev, 0, BLK)
    valid = (dist >= 0) & (dist <= lim) & (col >= first)
    top = lax.broadcasted_iota(jnp.int32, (2 * BLK, 1), 0) < BLK
    return dist.astype(F32), valid, top


def _stack_heads(x, m0, kes):
    parts = []
    for e in range(2):
        h = _mask_half(x, m0, e)
        parts.append(pltpu.roll(h, HEAD_DIM, 1) if kes[e] != e else h)
    return jnp.concatenate(parts, axis=0)


def _unstack_heads(y, m0, kes):
    parts = []
    for e in range(2):
        h = y[e * BLK:(e + 1) * BLK]
        parts.append(pltpu.roll(h, HEAD_DIM, 1) if kes[e] != e else h)
    return jnp.where(m0, parts[0], parts[1])


def _rows(r, dil):
    return pl.ds(r, BLK, stride=dil) if dil > 1 else pl.ds(0, BLK)


def _band_units(dil, nsub):
    assert dil == 1 or nsub == 1
    if nsub == 1:
        return [(_rows(r, dil), ("prev", _rows(r, dil)), 0) for r in range(dil)]
    units = [(pl.ds(0, BLK), ("prev", pl.ds(0, BLK)), 0)]
    units += [(pl.ds(BLK * s, BLK), ("cur", pl.ds(BLK * (s - 1), BLK)), s) for s in range(1, nsub)]
    return units


def _head_col_spec(ppk, RB, row_block):
    if ppk == 1:
        return pl.BlockSpec((None, RB, 1), lambda p, i: (p, row_block(i), 0))
    return pl.BlockSpec((ppk, RB, 1), lambda p, i: (p, row_block(i), 0))


def _band_specs(dil, nsub, ppk, q_blk, k_blk, v_blk, kv_shared, nb):
    RB = BLK * dil * nsub
    PB = BLK if nsub > 1 else RB
    qw = LANES * ppk
    kw = LANES if kv_shared else qw

    def cur(i):
        return jnp.minimum(i, nb - 1)

    def prev(i):
        return jnp.maximum(i * nsub - 1, 0) if nsub > 1 else jnp.maximum(i - 1, 0)

    def kidx(base):
        return (lambda p, i: (cur(i), base)) if kv_shared else (lambda p, i: (cur(i), base + p))

    def pidx(base):
        return (lambda p, i: (prev(i), base)) if kv_shared else (lambda p, i: (prev(i), base + p))

    specs = [pl.BlockSpec((RB, qw), lambda p, i: (cur(i), q_blk + p)),
             pl.BlockSpec((RB, kw), kidx(k_blk)), pl.BlockSpec((PB, kw), pidx(k_blk)),
             pl.BlockSpec((RB, kw), kidx(v_blk)), pl.BlockSpec((PB, kw), pidx(v_blk))]
    return specs if dil == 1 else [specs[0], specs[1], specs[3]]


def qk_norm(qkv, q_gain2, k_gain2, *, width, steps, n_q, tm, name):
    T = qkv.shape[0]
    nsb = width // LANES

    def body(x_ref, qg_ref, kg_ref, o_ref):
        m0 = _lane0()
        for b in range(nsb):
            is_q = ((pl.program_id(1) * nsb + b) < n_q).astype(F32)
            gain = qg_ref[...] * is_q + kg_ref[...] * (1.0 - is_q)
            cols = pl.ds(LANES * b, LANES)
            xv = x_ref[:, cols]
            o_ref[:, cols] = xv * _head_rms(xv, m0) * gain

    gspec = pl.BlockSpec((1, LANES), lambda i, j: (0, 0))
    return pl.pallas_call(
        body, grid=(T // tm, steps),
        in_specs=[pl.BlockSpec((tm, width), lambda i, j: (i, j)), gspec, gspec],
        out_specs=pl.BlockSpec((tm, width), lambda i, j: (i, j)),
        out_shape=jax.ShapeDtypeStruct((T, width * steps), F32),
        compiler_params=_cparams(2), name=name)(qkv, q_gain2, k_gain2)


def banded_fwd(qkn, qkv, slopes, sinks, *, dil, nsub, ppk, q_blk, k_blk, v_blk, n_heads, group,
               max_dist, name):
    T = qkv.shape[0]
    RB = BLK * dil * nsub
    nb = T // RB
    npair = n_heads // 2
    kv_shared = group > 1
    scale = HEAD_DIM ** -0.5
    has_sink = sinks is not None

    def body(*refs):
        slope_ref = refs[0]
        if has_sink:
            sink_ref, refs = refs[1], refs[2:]
        else:
            refs = refs[1:]
        if dil == 1:
            q_ref, kc_ref, kp_ref, vc_ref, vp_ref, o_ref, l_ref, lc0_ref, lc1_ref = refs
        else:
            q_ref, kc_ref, vc_ref, o_ref, l_ref, lc0_ref, lc1_ref, kp_ref, vp_ref = refs
        pb = pl.program_id(0)
        i = pl.program_id(1)
        if dil > 1:
            @pl.when(i == 0)
            def _():
                kp_ref[...] = jnp.zeros_like(kp_ref)
                vp_ref[...] = jnp.zeros_like(vp_ref)
        m0 = _lane0()
        distf, valid_first, top = _band_masks2(max_dist, i > 0, i >= 0)
        valid_inner = _band_masks2(max_dist, i >= 0, i >= 0)[1] if nsub > 1 else None
        for u, (rows, (src, prows), sub) in enumerate(_band_units(dil, nsub)):
            valid = valid_first if sub == 0 else valid_inner
            kpr, vpr = (kp_ref, vp_ref) if src == "prev" else (kc_ref, vc_ref)
            kcache = {}
            for jp in range(ppk):
                cs = pl.ds(LANES * jp, LANES)
                jk = 0 if kv_shared else jp
                if jk not in kcache:
                    ks = pl.ds(LANES * jk, LANES)
                    kcur, vcur = kc_ref[rows, ks], vc_ref[rows, ks]
                    if dil == 1:
                        kprev, vprev = kpr[prows, ks], vpr[prows, ks]
                    else:
                        kprev, vprev = kp_ref[u, :, ks], vp_ref[u, :, ks]
                        kp_ref[u, :, ks] = kcur
                        vp_ref[u, :, ks] = vcur
                    kcat = jnp.concatenate([kprev, kcur], axis=0)
                    vcat = jnp.concatenate([vprev, vcur], axis=0)
                    kcache[jk] = (kcat.astype(BF16), vcat.astype(BF16))
                kn, vcat = kcache[jk]
                qn = q_ref[rows, cs]
                kes = [((2 * jp + e) // group) % 2 if kv_shared else e for e in range(2)]
                hidx = 2 * (pb * ppk + jp)
                qs = _stack_heads(qn, m0, kes).astype(BF16)
                slope = jnp.where(top, slope_ref[hidx], slope_ref[hidx + 1])
                s = jnp.where(valid, _dot_nt(qs, kn) * scale - slope * distf, NEG)
                m = jnp.max(s, axis=-1, keepdims=True)
                if has_sink:
                    sk = jnp.where(top, sink_ref[hidx], sink_ref[hidx + 1])
                    m = jnp.maximum(m, sk)
                p = jnp.exp(s - m)
                den = jnp.sum(p, axis=-1, keepdims=True)
                if has_sink:
                    den = den + jnp.exp(sk - m)
                o_full = _dot((p * (1.0 / den)).astype(BF16), vcat)
                o_ref[rows, cs] = _unstack_heads(o_full, m0, kes)
                lse = m + jnp.log(den)
                l_ref[rows, cs] = _unstack_heads(jnp.broadcast_to(lse, (2 * BLK, LANES)), m0, [0, 1])
                for e, lc_ref in enumerate((lc0_ref, lc1_ref)):
                    if ppk == 1:
                        lc_ref[rows, :] = lse[e * BLK:(e + 1) * BLK]
                    else:
                        lc_ref[jp, rows, :] = lse[e * BLK:(e + 1) * BLK]

    smem = pl.BlockSpec(memory_space=pltpu.SMEM)
    qw = LANES * ppk
    ospec = pl.BlockSpec((RB, qw), lambda p, i: (i, p))
    oshape = jax.ShapeDtypeStruct((T, n_heads * HEAD_DIM), F32)
    args = [slopes] + ([sinks] if has_sink else []) + ([qkn] * 3 + [qkv] * 2 if dil == 1 else [qkn, qkn, qkv])
    kw = LANES if kv_shared else qw
    prev_scratch = [] if dil == 1 else [pltpu.VMEM((dil, BLK, kw), F32)] * 2
    return pl.pallas_call(
        body, grid=(npair // ppk, nb),
        in_specs=[smem] * (2 if has_sink else 1) + _band_specs(dil, nsub, ppk, q_blk, k_blk, v_blk, kv_shared, nb),
        out_specs=[ospec, ospec] + [_head_col_spec(ppk, RB, lambda i: i)] * 2,
        out_shape=[oshape, oshape] + [jax.ShapeDtypeStruct((npair, T, 1), F32)] * 2,
        scratch_shapes=prev_scratch, compiler_params=_cparams(2), name=name)(*args)


def banded_bwd(qkn, qkv, slopes, sinks, do, o, lsec, w, omix, *, dil, nsub, ppk, q_blk, k_blk, v_blk,
               n_heads, group, max_dist, do_blk, name):
    T = qkv.shape[0]
    RB = BLK * dil * nsub
    nb = T // RB
    npair = n_heads // 2
    kv_shared = group > 1
    scale = HEAD_DIM ** -0.5
    has_sink = sinks is not None
    mixed = w is not None
    qw = LANES * ppk

    def body(*refs):
        slope_ref = refs[0]
        if has_sink:
            sink_ref, refs = refs[1], refs[2:]
        else:
            refs = refs[1:]
        if dil == 1:
            q_ref, kc_ref, kp_ref, vc_ref, vp_ref, do_ref, lc0_ref, lc1_ref = refs[:8]
            refs = refs[8:]
        else:
            q_ref, kc_ref, vc_ref, do_ref, lc0_ref, lc1_ref = refs[:6]
            refs, kp_ref, vp_ref = refs[6:-2], refs[-2], refs[-1]
        if mixed:
            w_ref, om_ref, refs = refs[0], refs[1], refs[2:]
        else:
            o_ref, refs = refs[0], refs[1:]
        dq_ref, dk_ref, dv_ref, dsk_ref, ck_ref, cv_ref = refs
        pb = pl.program_id(0)
        i = pl.program_id(1)
        live = i < nb
        m0 = _lane0()
        lane = lax.broadcasted_iota(jnp.int32, (1, LANES), 1)
        distf, valid_first, top = _band_masks2(max_dist, i > 0, live)
        valid_inner = _band_masks2(max_dist, i >= 0, live)[1] if nsub > 1 else None
        livef = live.astype(F32)

        def half_rows(x):
            s0 = jnp.sum(jnp.where(m0, x, 0.0), axis=-1, keepdims=True)
            s1 = jnp.sum(jnp.where(m0, 0.0, x), axis=-1, keepdims=True)
            return jnp.concatenate([s0, s1], axis=0)

        @pl.when((pb == 0) & (i == 0))
        def _():
            dsk_ref[...] = jnp.zeros_like(dsk_ref)

        @pl.when(i == 0)
        def _():
            ck_ref[...] = jnp.zeros_like(ck_ref)
            cv_ref[...] = jnp.zeros_like(cv_ref)

        dsk_acc = jnp.zeros((1, LANES), F32)
        if nsub > 1:
            dk_ref[...] = ck_ref[...]
            dv_ref[...] = cv_ref[...]
        if dil > 1:
            @pl.when(i == 0)
            def _():
                kp_ref[...] = jnp.zeros_like(kp_ref)
                vp_ref[...] = jnp.zeros_like(vp_ref)

        for u, (rows, (src, prows), sub) in enumerate(_band_units(dil, nsub)):
            valid = valid_first if sub == 0 else valid_inner
            kpr, vpr = (kp_ref, vp_ref) if src == "prev" else (kc_ref, vc_ref)
            ck_u, cv_u = (ck_ref.at[u], cv_ref.at[u]) if dil > 1 else (None, None)
            for jp in range(ppk):
                cs = pl.ds(LANES * jp, LANES)
                ks = pl.ds(0, LANES) if kv_shared else cs
                kcur, vcur = kc_ref[rows, ks], vc_ref[rows, ks]
                if dil == 1:
                    kprev, vprev = kpr[prows, ks], vpr[prows, ks]
                else:
                    kprev, vprev = kp_ref[u, :, ks], vp_ref[u, :, ks]
                    kp_ref[u, :, ks] = kcur
                    vp_ref[u, :, ks] = vcur
                kn = jnp.concatenate([kprev, kcur], axis=0).astype(BF16)
                vcat = jnp.concatenate([vprev, vcur], axis=0).astype(BF16)
                dov = do_ref[rows, cs]
                if mixed:
                    dov = dov * w_ref[rows, cs]
                    shift = half_rows(dov * om_ref[rows, cs])
                else:
                    shift = half_rows(dov * o_ref[rows, cs])
                kes = [((2 * jp + e) // group) % 2 if kv_shared else e for e in range(2)]
                hidx = 2 * (pb * ppk + jp)
                qs = _stack_heads(q_ref[rows, cs], m0, kes).astype(BF16)
                dos = _stack_heads(dov, m0, kes).astype(BF16)
                lse = jnp.concatenate([ref[rows, :] if ppk == 1 else ref[jp, rows, :]
                                       for ref in (lc0_ref, lc1_ref)], axis=0)
                slope = jnp.where(top, slope_ref[hidx], slope_ref[hidx + 1])
                p = jnp.where(valid, jnp.exp(_dot_nt(qs, kn) * scale - slope * distf - lse), 0.0)
                ds = (p * (_dot_nt(dos, vcat) - shift)).astype(BF16)
                dqn = _unstack_heads(_dot(ds, kn), m0, kes) * scale
                dkn = _dot_tn(ds, qs) * scale
                dvv = _dot_tn(p.astype(BF16), dos)
                if has_sink:
                    sk = jnp.where(top, sink_ref[hidx], sink_ref[hidx + 1])
                    contrib = -jnp.exp(sk - lse) * shift * livef
                    for e in range(2):
                        tot = jnp.sum(contrib[e * BLK:(e + 1) * BLK], axis=0, keepdims=True)
                        dsk_acc = dsk_acc + jnp.where(lane == (2 * jp + e), tot, 0.0)
                dk_raw = dkn

                @pl.when(live)
                def _():
                    dq_ref[rows, cs] = dqn

                if dil > 1:
                    dk_ref[rows, cs] = ck_u[:, cs] + dk_raw[:BLK]
                    dv_ref[rows, cs] = cv_u[:, cs] + dvv[:BLK]
                    ck_u[:, cs] = dk_raw[BLK:]
                    cv_u[:, cs] = dvv[BLK:]
                    continue
                if nsub == 1:
                    dk_ref[rows, cs] = ck_ref[rows, cs] + dk_raw[:BLK]
                    dv_ref[rows, cs] = cv_ref[rows, cs] + dvv[:BLK]
                elif sub == 0:
                    last = pl.ds(RB - BLK, BLK)
                    dk_ref[last, cs] += dk_raw[:BLK]
                    dv_ref[last, cs] += dvv[:BLK]
                else:
                    ck_ref[prows, cs] += dk_raw[:BLK]
                    cv_ref[prows, cs] += dvv[:BLK]
                ck_ref[rows, cs] = dk_raw[BLK:]
                cv_ref[rows, cs] = dvv[BLK:]
        dsk_ref[...] += dsk_acc

    smem = pl.BlockSpec(memory_space=pltpu.SMEM)
    gspec = pl.BlockSpec((1, LANES), lambda p, i: (0, 0))

    def cur(i):
        return jnp.minimum(i, nb - 1)

    qspec = pl.BlockSpec((RB, qw), lambda p, i: (cur(i), p))
    dospec = pl.BlockSpec((RB, qw), lambda p, i: (cur(i), do_blk + p))
    kvout = pl.BlockSpec((RB, qw), lambda p, i: (jnp.maximum(i - 1, 0), p))
    in_specs = ([smem] * (2 if has_sink else 1) + _band_specs(dil, nsub, ppk, q_blk, k_blk, v_blk, kv_shared, nb)
                + [dospec] + [_head_col_spec(ppk, RB, cur)] * 2
                + ([qspec, qspec] if mixed else [qspec]))
    args = ([slopes] + ([sinks] if has_sink else []) + ([qkn] * 3 + [qkv] * 2 if dil == 1 else [qkn, qkn, qkv])
            + [do, lsec[0], lsec[1]]
            + ([w, omix] if mixed else [o]))
    full = jax.ShapeDtypeStruct((T, n_heads * HEAD_DIM), F32)
    row = jax.ShapeDtypeStruct((1, LANES), F32)
    return pl.pallas_call(
        body, grid=(npair // ppk, nb + 1), in_specs=in_specs,
        out_specs=[qspec, kvout, kvout, gspec],
        out_shape=[full, full, full, row],
        scratch_shapes=([pltpu.VMEM((RB, qw), F32)] * 2 if dil == 1
                        else [pltpu.VMEM((dil, BLK, qw), F32)] * 2 + [pltpu.VMEM((dil, BLK, qw), F32)] * 2),
        compiler_params=_cparams(2), name=name)(*args)


def mix_fwd(o1, o2, o3, l1, l2, l3, *, tm, name):
    T, C = o1.shape

    def body(o1r, o2r, o3r, l1r, l2r, l3r, o_ref, w1r, w2r, w3r):
        a, b, c = l1r[...], l2r[...], l3r[...]
        m = jnp.maximum(jnp.maximum(a, b), c)
        ea, eb, ec = jnp.exp(a - m), jnp.exp(b - m), jnp.exp(c - m)
        inv = 1.0 / (ea + eb + ec)
        wa, wb, wc = ea * inv, eb * inv, ec * inv
        o_ref[...] = wa * o1r[...] + wb * o2r[...] + wc * o3r[...]
        w1r[...] = wa
        w2r[...] = wb
        w3r[...] = wc

    spec = pl.BlockSpec((tm, C), lambda i: (i, 0))
    shp = jax.ShapeDtypeStruct((T, C), F32)
    return pl.pallas_call(body, grid=(T // tm,), in_specs=[spec] * 6, out_specs=[spec] * 4, out_shape=[shp] * 4,
                          compiler_params=_cparams(1), name=name)(o1, o2, o3, l1, l2, l3)


def _qk_norm_bwd(raw, dn, gain, m0):
    r = _head_rms(raw, m0)
    h = raw * r
    dh = dn * gain
    d_raw = r * (dh - h * (_half_sum(dh * h, m0) * (1.0 / HEAD_DIM)))
    return d_raw, jnp.sum(dn * h, axis=0, keepdims=True)


def _acc_rows(ref, val):
    @pl.when(pl.program_id(0) == 0)
    def _():
        ref[...] = val

    @pl.when(pl.program_id(0) > 0)
    def _():
        ref[...] += val


def assemble_odd(parts, qkv, q_gain2, k_gain2, *, tm, name):
    T, C = parts[0][0].shape
    nbk = C // LANES

    def body(*refs):
        qkv_ref, qg_ref, kg_ref, o_ref, dqg_ref, dkg_ref = refs[9:]
        m0 = _lane0()
        sums = [refs[j][...] + refs[3 + j][...] + refs[6 + j][...] for j in range(3)]
        o_ref[:, pl.ds(2 * C, C)] = sums[2]
        for j, (g_ref, acc_ref) in enumerate(((qg_ref, dqg_ref), (kg_ref, dkg_ref))):
            dgain = jnp.zeros((1, LANES), F32)
            for b in range(nbk):
                cols = pl.ds(C * j + LANES * b, LANES)
                d_raw, part = _qk_norm_bwd(qkv_ref[:, cols], sums[j][:, LANES * b:LANES * (b + 1)], g_ref[...], m0)
                o_ref[:, cols] = d_raw
                dgain = dgain + part
            _acc_rows(acc_ref, dgain)

    spec = pl.BlockSpec((tm, C), lambda i: (i, 0))
    gspec = pl.BlockSpec((1, LANES), lambda i: (0, 0))
    flat = [parts[p][j] for p in range(3) for j in range(3)]
    row = jax.ShapeDtypeStruct((1, LANES), F32)
    return pl.pallas_call(body, grid=(T // tm,),
                          in_specs=[spec] * 9 + [pl.BlockSpec((tm, 2 * C), lambda i: (i, 0)), gspec, gspec],
                          out_specs=[pl.BlockSpec((tm, 3 * C), lambda i: (i, 0)), gspec, gspec],
                          out_shape=[jax.ShapeDtypeStruct((T, 3 * C), F32), row, row],
                          compiler_params=_cparams(1), name=name)(*flat, qkv, q_gain2, k_gain2)


def assemble_even(dqa, dka4, dva4, dqb, dkb, dvb, qkv, q_gain2, k_gain2, *, tm, name):
    T = dqa.shape[0]
    W = 512
    QK = 768

    def body(dqa_r, dka_r, dva_r, dqb_r, dkb_r, dvb_r, qkv_ref, qg_ref, kg_ref, o_ref, dqg_ref, dkg_ref):
        m0 = _lane0()
        ka = dka_r[...]
        va = dva_r[...]
        dqn = dqa_r[...]
        dgain = jnp.zeros((1, LANES), F32)
        for b in range(W // LANES):
            cols = pl.ds(LANES * b, LANES)
            d_raw, part = _qk_norm_bwd(qkv_ref[:, cols], dqn[:, LANES * b:LANES * (b + 1)], qg_ref[...], m0)
            o_ref[:, cols] = d_raw
            dgain = dgain + part
        _acc_rows(dqg_ref, dgain)
        dkn = ka[:, 0:128] + ka[:, 128:256] + ka[:, 256:384] + ka[:, 384:512]
        d_raw, part = _qk_norm_bwd(qkv_ref[:, pl.ds(W, LANES)], dkn, kg_ref[...], m0)
        o_ref[:, pl.ds(W, LANES)] = d_raw
        _acc_rows(dkg_ref, part)
        o_ref[:, pl.ds(640, LANES)] = va[:, 0:128] + va[:, 128:256] + va[:, 256:384] + va[:, 384:512]
        o_ref[:, pl.ds(768, W)] = dqb_r[...]
        o_ref[:, pl.ds(1280, W)] = dkb_r[...]
        o_ref[:, pl.ds(1792, W)] = dvb_r[...]

    spec = pl.BlockSpec((tm, W), lambda i: (i, 0))
    gspec = pl.BlockSpec((1, LANES), lambda i: (0, 0))
    row = jax.ShapeDtypeStruct((1, LANES), F32)
    return pl.pallas_call(body, grid=(T // tm,),
                          in_specs=[spec] * 6 + [pl.BlockSpec((tm, QK), lambda i: (i, 0)), gspec, gspec],
                          out_specs=[pl.BlockSpec((tm, 2304), lambda i: (i, 0)), gspec, gspec],
                          out_shape=[jax.ShapeDtypeStruct((T, 2304), F32), row, row],
                          compiler_params=_cparams(1), name=name)(dqa, dka4, dva4, dqb, dkb, dvb, qkv, q_gain2, k_gain2)


STICK_T = 256
STICK_DEAD = -110.0


def _split_bf16(x):
    hi = x.astype(BF16)
    lo = (x - hi.astype(F32)).astype(BF16)
    return hi, lo


def _stick_logits(qm, kt, scale, diag):
    n = STICK_T
    row = lax.broadcasted_iota(jnp.int32, (n, n), 0)
    col = lax.broadcasted_iota(jnp.int32, (n, n), 1)
    mask = col < row + jnp.where(diag, 0, n)
    z = _dot_nt(qm, kt) * scale
    lneg = -(jnp.maximum(z, 0.0) + jnp.log(1.0 + jnp.exp(-jnp.abs(z))))
    lpos = z + lneg
    lk = jnp.where(mask, lneg, 0.0)
    return mask, lpos, lneg, lk


def _cumsum_mm(x, tri):
    hi, lo = _split_bf16(x)
    return _dot(hi, tri) + _dot(lo, tri)


def stick_fwd(qkv, *, q_blk, k_blk, v_blk, n_pairs, name, riders=None, rider_args=()):
    T = qkv.shape[0]
    n = STICK_T
    nq = T // n
    scale = HEAD_DIM ** -0.5
    nc = riders.n if riders is not None else 0
    n_steps = n_pairs * nq
    stage_at = (0, (3 * n_steps) // 4, n_steps - 1, n_steps - 1)

    def body(*refs):
        q_ref, k_ref, v_ref = refs[:3]
        x_refs, o_ref = refs[3:3 + nc], refs[3 + nc]
        out_refs, sems = refs[4 + nc:4 + 2 * nc], refs[4 + 2 * nc:]
        i = pl.program_id(1)
        step_id = pl.program_id(0) * nq + i

        def ride(which):
            if riders is not None:
                @pl.when(step_id == stage_at[which])
                def _():
                    riders.stage(which, x_refs, out_refs, sems)

        ride(0)
        ride(1)
        m0 = _lane0()
        r2 = lax.broadcasted_iota(jnp.int32, (n, n), 0)
        c2 = lax.broadcasted_iota(jnp.int32, (n, n), 1)
        tri_after = (r2 > c2).astype(BF16)
        qv = q_ref[...]
        out = jnp.zeros((n, LANES), F32)
        for e in range(2):
            qm = _mask_half(qv, m0, e).astype(BF16)

            def alive(st):
                t, _, carry = st
                return (t <= i) & (jnp.max(carry) > STICK_DEAD)

            def step(st, e=e, qm=qm):
                t, acc, carry = st
                start = pl.multiple_of((i - t) * n, n)
                kt = k_ref[pl.ds(start, n), :].astype(BF16)
                vt = _mask_half(v_ref[pl.ds(start, n), :], m0, e).astype(BF16)
                mask, lpos, _, lk = _stick_logits(qm, kt, scale, t == 0)
                after = _cumsum_mm(lk, tri_after) + carry
                a = jnp.where(mask, jnp.exp(lpos + after), 0.0)
                acc = acc + _dot(a.astype(BF16), vt)
                carry = carry + jnp.sum(lk, axis=-1, keepdims=True)
                return t + 1, acc, carry

            _, acc, _ = lax.while_loop(alive, step, (jnp.int32(0), jnp.zeros((n, LANES), F32),
                                                     jnp.zeros((n, 1), F32)))
            out = out + acc
        o_ref[...] = out
        ride(2)
        ride(3)

    outs = pl.pallas_call(
        body, grid=(n_pairs, nq),
        in_specs=[pl.BlockSpec((n, LANES), lambda p, i: (i, q_blk + p)),
                  pl.BlockSpec((T, LANES), lambda p, i: (0, k_blk + p)),
                  pl.BlockSpec((T, LANES), lambda p, i: (0, v_blk + p))] + [_ANY] * nc,
        out_specs=[pl.BlockSpec((n, LANES), lambda p, i: (i, p))] + [_ANY] * nc,
        out_shape=[jax.ShapeDtypeStruct((T, n_pairs * LANES), F32)] + (riders.shapes if nc else []),
        scratch_shapes=riders.sems if nc else [],
        compiler_params=_cparams(2), name=name)(qkv, qkv, qkv, *rider_args)
    return outs[0], list(outs[1:])


def stick_bwd(qkv, do, *, q_blk, k_blk, v_blk, do_blk, n_pairs, name, riders=None, rider_args=()):
    T = qkv.shape[0]
    n = STICK_T
    nq = T // n
    scale = HEAD_DIM ** -0.5
    nc = riders.n if riders is not None else 0

    def body(*refs):
        q_ref, k_ref, v_ref, do_ref = refs[:4]
        pre_refs = refs[4:4 + nc]
        dq_ref, dk_ref, dv_ref = refs[4 + nc:7 + nc]
        land_refs = refs[7 + nc:7 + 2 * nc]
        a_keep, g_keep, s_keep = refs[7 + 2 * nc:10 + 2 * nc]
        sems = refs[10 + 2 * nc:]
        i = pl.program_id(1)
        first_step = (pl.program_id(0) == 0) & (i == 0)
        last_step = (pl.program_id(0) == n_pairs - 1) & (i == nq - 1)
        m0 = _lane0()
        r2 = lax.broadcasted_iota(jnp.int32, (n, n), 0)
        c2 = lax.broadcasted_iota(jnp.int32, (n, n), 1)
        tri_after = (r2 > c2).astype(BF16)
        tri_from = (r2 >= c2).astype(BF16)

        if riders is not None:
            @pl.when(first_step)
            def _():
                riders.start(pre_refs, land_refs, sems)

        @pl.when(i == 0)
        def _():
            dk_ref[...] = jnp.zeros_like(dk_ref)
            dv_ref[...] = jnp.zeros_like(dv_ref)

        qv = q_ref[...]
        dov = do_ref[...]
        dq_out = jnp.zeros((n, LANES), F32)
        for e in range(2):
            qm = _mask_half(qv, m0, e).astype(BF16)
            dom = _mask_half(dov, m0, e).astype(BF16)

            def alive(st):
                t, carry, _ = st
                return (t <= i) & (jnp.max(carry) > STICK_DEAD)

            def scan(st, qm=qm, dom=dom):
                t, carry, gtot = st
                start = pl.multiple_of((i - t) * n, n)
                kt = k_ref[pl.ds(start, n), :].astype(BF16)
                vt = v_ref[pl.ds(start, n), :].astype(BF16)
                mask, lpos, lneg, lk = _stick_logits(qm, kt, scale, t == 0)
                a = jnp.where(mask, jnp.exp(lpos + _cumsum_mm(lk, tri_after) + carry), 0.0)
                g = _dot_nt(dom, vt) * a
                a_keep[t] = a.astype(BF16)
                g_keep[t] = g
                s_keep[t] = jnp.exp(lneg).astype(BF16)
                return (t + 1, carry + jnp.sum(lk, axis=-1, keepdims=True),
                        gtot + jnp.sum(g, axis=-1, keepdims=True))

            z1 = jnp.zeros((n, 1), F32)
            n_live, _, gtot = lax.while_loop(alive, scan, (jnp.int32(0), z1, z1))

            def step(t, st, e=e, qm=qm, dom=dom, gtot=gtot):
                dq_acc, gright = st
                start = pl.multiple_of((i - t) * n, n)
                g = g_keep[t]
                sneg = s_keep[t].astype(F32)
                before = gtot - (_cumsum_mm(g, tri_from) + gright)
                mask = c2 < r2 + jnp.where(t == 0, 0, n)
                dz = jnp.where(mask, g * sneg - before * (1.0 - sneg), 0.0) * scale
                dzb = dz.astype(BF16)
                dq_acc = dq_acc + _dot(dzb, _mask_half(k_ref[pl.ds(start, n), :], m0, e).astype(BF16))
                dk_ref[pl.ds(start, n), :] += _dot_tn(dzb, qm)
                dv_ref[pl.ds(start, n), :] += _dot_tn(a_keep[t], dom)
                return dq_acc, gright + jnp.sum(g, axis=-1, keepdims=True)

            dq_acc, _ = lax.fori_loop(0, n_live, step, (jnp.zeros((n, LANES), F32), z1))
            dq_out = dq_out + dq_acc
        dq_ref[...] = dq_out

        if riders is not None:
            @pl.when(last_step)
            def _():
                riders.finish(pre_refs, land_refs, sems)

    tile = pl.BlockSpec((n, LANES), lambda p, i: (i, p))
    whole = pl.BlockSpec((T, LANES), lambda p, i: (0, p))
    shp = jax.ShapeDtypeStruct((T, n_pairs * LANES), F32)
    outs = pl.pallas_call(
        body, grid=(n_pairs, nq),
        in_specs=[pl.BlockSpec((n, LANES), lambda p, i: (i, q_blk + p)),
                  pl.BlockSpec((T, LANES), lambda p, i: (0, k_blk + p)),
                  pl.BlockSpec((T, LANES), lambda p, i: (0, v_blk + p)),
                  pl.BlockSpec((n, LANES), lambda p, i: (i, do_blk + p))] + [_ANY] * nc,
        out_specs=[tile, whole, whole] + [_ANY] * nc,
        out_shape=[shp, shp, shp] + (riders.shapes if nc else []),
        scratch_shapes=[pltpu.VMEM((nq, n, n), BF16), pltpu.VMEM((nq, n, n), F32), pltpu.VMEM((nq, n, n), BF16)]
        + (riders.sems if nc else []),
        compiler_params=_cparams(2), name=name)(qkv, qkv, qkv, do, *rider_args)
    return outs[0], outs[1], outs[2], list(outs[3:])


def _xnorm(x):
    r = lax.rsqrt(jnp.mean(x * x, axis=-1, keepdims=True) + RMS_EPS)
    return r, x * r


def xattn_fwd(qraw, kvraw, q_gain, k_gain, *, tm, name):
    T = qraw.shape[0]
    scale = X_HEAD_DIM ** -0.5
    W = X_HEADS * X_HEAD_DIM

    def body(q_ref, kv_ref, qg_ref, kg_ref, o_ref):
        for h in range(X_HEADS):
            cs = pl.ds(X_HEAD_DIM * h, X_HEAD_DIM)
            _, qh = _xnorm(q_ref[:, cs])
            _, kh = _xnorm(kv_ref[:, cs])
            qn = (qh * qg_ref[...]).astype(BF16)
            kn = (kh * kg_ref[...]).astype(BF16)
            v = kv_ref[:, pl.ds(W + X_HEAD_DIM * h, X_HEAD_DIM)].astype(BF16)
            s = _dot_nt(qn, kn) * scale
            m = jnp.max(s, axis=-1, keepdims=True)
            p = jnp.exp(s - m)
            p = p / jnp.sum(p, axis=-1, keepdims=True)
            o_ref[:, cs] = _dot(p.astype(BF16), v)

    gspec = pl.BlockSpec((1, X_HEAD_DIM), lambda i: (0, 0))
    return pl.pallas_call(
        body, grid=(T // tm,),
        in_specs=[pl.BlockSpec((tm, W), lambda i: (i, 0)), pl.BlockSpec((MEM_LEN, 2 * W), lambda i: (0, 0)),
                  gspec, gspec],
        out_specs=pl.BlockSpec((tm, W), lambda i: (i, 0)),
        out_shape=jax.ShapeDtypeStruct((T, W), F32),
        compiler_params=_cparams(1), name=name)(qraw, kvraw, q_gain, k_gain)


def xattn_bwd(qraw, kvraw, q_gain, k_gain, do, o, *, tm, name):
    T = qraw.shape[0]
    nt = T // tm
    scale = X_HEAD_DIM ** -0.5
    W = X_HEADS * X_HEAD_DIM

    def body(q_ref, kv_ref, qg_ref, kg_ref, do_ref, o_ref, dq_ref, dkv_ref, dqg_ref, dkg_ref, dkn_ref):
        i = pl.program_id(0)

        @pl.when(i == 0)
        def _():
            dkv_ref[...] = jnp.zeros_like(dkv_ref)
            dkn_ref[...] = jnp.zeros_like(dkn_ref)
            dqg_ref[...] = jnp.zeros_like(dqg_ref)
            dkg_ref[...] = jnp.zeros_like(dkg_ref)

        qg = qg_ref[...]
        kg = kg_ref[...]
        dqg_acc = jnp.zeros((1, X_HEAD_DIM), F32)
        for h in range(X_HEADS):
            cs = pl.ds(X_HEAD_DIM * h, X_HEAD_DIM)
            vs = pl.ds(W + X_HEAD_DIM * h, X_HEAD_DIM)
            rq, qh = _xnorm(q_ref[:, cs])
            _, kh = _xnorm(kv_ref[:, cs])
            qn = (qh * qg).astype(BF16)
            kn = (kh * kg).astype(BF16)
            v = kv_ref[:, vs].astype(BF16)
            s = _dot_nt(qn, kn) * scale
            m = jnp.max(s, axis=-1, keepdims=True)
            p = jnp.exp(s - m)
            p = p / jnp.sum(p, axis=-1, keepdims=True)
            dov = do_ref[:, cs]
            delta = jnp.sum(dov * o_ref[:, cs], axis=-1, keepdims=True)
            dob = dov.astype(BF16)
            ds = (p * (_dot_nt(dob, v) - delta)).astype(BF16)
            dqn = _dot(ds, kn) * scale
            dkn_ref[:, cs] += _dot_tn(ds, qn) * scale
            dkv_ref[:, vs] += _dot_tn(p.astype(BF16), dob)
            dqg_acc = dqg_acc + jnp.sum(dqn * qh, axis=0, keepdims=True)
            dqh = dqn * qg
            dq_ref[:, cs] = rq * (dqh - qh * jnp.mean(dqh * qh, axis=-1, keepdims=True))
        dqg_ref[...] += dqg_acc

        @pl.when(i == nt - 1)
        def _():
            dkg_acc = jnp.zeros((1, X_HEAD_DIM), F32)
            for h in range(X_HEADS):
                cs = pl.ds(X_HEAD_DIM * h, X_HEAD_DIM)
                rk, kh = _xnorm(kv_ref[:, cs])
                dkn = dkn_ref[:, cs]
                dkg_acc = dkg_acc + jnp.sum(dkn * kh, axis=0, keepdims=True)
                dkh = dkn * kg
                dkv_ref[:, cs] = rk * (dkh - kh * jnp.mean(dkh * kh, axis=-1, keepdims=True))
            dkg_ref[...] = dkg_acc

    gspec = pl.BlockSpec((1, X_HEAD_DIM), lambda i: (0, 0))
    tile = pl.BlockSpec((tm, W), lambda i: (i, 0))
    kvspec = pl.BlockSpec((MEM_LEN, 2 * W), lambda i: (0, 0))
    grow = jax.ShapeDtypeStruct((1, X_HEAD_DIM), F32)
    return pl.pallas_call(
        body, grid=(nt,), in_specs=[tile, kvspec, gspec, gspec, tile, tile],
        out_specs=[tile, kvspec, gspec, gspec],
        out_shape=[jax.ShapeDtypeStruct((T, W), F32), jax.ShapeDtypeStruct((MEM_LEN, 2 * W), F32), grow, grow],
        scratch_shapes=[pltpu.VMEM((MEM_LEN, W), F32)],
        compiler_params=_cparams(1), name=name)(qraw, kvraw, q_gain, k_gain, do, o)


_ANY = pl.BlockSpec(memory_space=pl.ANY)


def _my_pos():
    return lax.axis_index("x"), lax.axis_index("y"), lax.axis_index("c")


def _pieces(arrays, chunks):
    out = []
    for a, (arr, n) in enumerate(zip(arrays, chunks)):
        rc = arr.shape[-2] // n
        out += [(a, pl.ds(ch * rc, rc)) for ch in range(n)]
    return out


class GatherBlocks:
    N_STAGES = 4

    def __init__(self, blks, chunks):
        self.shapes = [jax.ShapeDtypeStruct((N_DEV,) + b.shape, b.dtype) for b in blks]
        self.n = len(blks)
        self.pieces = _pieces(blks, chunks)
        n_p = len(self.pieces)
        self.sems = [pltpu.SemaphoreType.DMA((7 * n_p,)), pltpu.SemaphoreType.DMA((7 * n_p,)),
                     pltpu.SemaphoreType.DMA((n_p,))]

    def stage(self, which, x_refs, out_refs, sems):
        send_sems, recv_sems, local_sems = sems
        pieces, n_p = self.pieces, len(self.pieces)
        x, y, c = _my_pos()
        me, sibling = (x, y, c), (x, y, 1 - c)
        chips = [(1 - x, y), (x, 1 - y), (1 - x, 1 - y)]
        xn, yn, dg = [(*chip, c) for chip in chips]
        ps = range(n_p)

        def slot(block, p):
            px, py, pc = block
            a, rows = pieces[p]
            return out_refs[a].at[4 * px + 2 * py + pc, rows]

        def own(p):
            a, rows = pieces[p]
            return x_refs[a].at[rows]

        def copy(k, p, block, to, from_input=False):
            return pltpu.make_async_remote_copy(
                src_ref=own(p) if from_input else slot(block, p), dst_ref=slot(block, p),
                send_sem=send_sems.at[k * n_p + p], recv_sem=recv_sems.at[k * n_p + p],
                device_id=to, device_id_type=MESH)

        mine = [pltpu.make_async_copy(own(p), slot(me, p), local_sems.at[p]) for p in ps]
        first = [copy(k, p, me, to, from_input=True) for p in ps for k, to in ((1, xn), (2, yn), (0, sibling))]
        on_x = [copy(3, p, xn, yn) for p in ps if p % 2 == 0] + [copy(4, p, xn, sibling) for p in ps]
        on_y = [copy(3, p, yn, xn) for p in ps if p % 2 == 1] + [copy(5, p, yn, sibling) for p in ps]
        on_d = [copy(6, p, dg, sibling) for p in ps]
        if which == 0:
            for cp in first + mine:
                cp.start()
        elif which == 1:
            for p in ps:
                copy(1, p, xn, me).wait_recv()
                if p % 2 == 0:
                    copy(3, p, xn, yn).start()
                copy(4, p, xn, sibling).start()
                copy(2, p, yn, me).wait_recv()
                if p % 2 == 1:
                    copy(3, p, yn, xn).start()
                copy(5, p, yn, sibling).start()
        elif which == 2:
            for p in ps:
                copy(3, p, dg, me).wait_recv()
                copy(6, p, dg, sibling).start()
        else:
            for p in ps:
                copy(0, p, sibling, me).wait_recv()
            for k, chip in zip((4, 5, 6), chips):
                for p in ps:
                    copy(k, p, (*chip, 1 - c), me).wait_recv()
            for cp in first + on_x + on_y + on_d:
                cp.wait_send()
            for cp in mine:
                cp.wait()


def gather_blocks(blks, chunks, *, name):
    gb = GatherBlocks(blks, chunks)
    n = gb.n

    def body(*refs):
        x_refs, out_refs, sems = refs[:n], refs[n:2 * n], refs[2 * n:]
        for which in range(gb.N_STAGES):
            gb.stage(which, x_refs, out_refs, sems)

    return pl.pallas_call(body, out_shape=gb.shapes, in_specs=[_ANY] * n, out_specs=[_ANY] * n,
                          scratch_shapes=gb.sems, name=name)(*blks)


def gather_small(small, *, name):
    S, C = small.shape

    def body(s_ref, out_ref, send_sems, recv_sems, local_sem):
        x, y, c = _my_pos()
        my_id = 4 * x + 2 * y + c

        def copy(k, slot):
            px, py, pc = x ^ ((k >> 2) & 1), y ^ ((k >> 1) & 1), c ^ (k & 1)
            dst = my_id if slot == "mine" else 4 * px + 2 * py + pc
            return pltpu.make_async_remote_copy(
                src_ref=s_ref, dst_ref=out_ref.at[dst], send_sem=send_sems.at[k - 1], recv_sem=recv_sems.at[k - 1],
                device_id=(px, py, pc), device_id_type=MESH)

        own = pltpu.make_async_copy(s_ref, out_ref.at[my_id], local_sem)
        own.start()
        sends = [copy(k, "mine") for k in range(1, N_DEV)]
        for cp in sends:
            cp.start()
        for k in range(1, N_DEV):
            copy(k, "theirs").wait_recv()
        for cp in sends:
            cp.wait_send()
        own.wait()

    dma7 = pltpu.SemaphoreType.DMA((7,))
    return pl.pallas_call(
        body, out_shape=jax.ShapeDtypeStruct((N_DEV, S, C), small.dtype), in_specs=[_ANY], out_specs=_ANY,
        scratch_shapes=[dma7, dma7, pltpu.SemaphoreType.DMA], name=name)(small)


class PairExchange:
    def __init__(self, bigs, chunks):
        self.shapes = [jax.ShapeDtypeStruct((4,) + b.shape[1:], b.dtype) for b in bigs]
        self.n = len(bigs)
        self.pieces = _pieces(bigs, chunks)
        n_p = len(self.pieces)
        self.sems = [pltpu.SemaphoreType.DMA((4 * n_p,)), pltpu.SemaphoreType.DMA((4 * n_p,))]

    def _copies(self, big_refs, out_refs, sems):
        send_sems, recv_sems = sems
        n_p = len(self.pieces)
        x, y, c = _my_pos()

        def copy(b, p):
            a, rows = self.pieces[p]
            return pltpu.make_async_remote_copy(
                src_ref=big_refs[a].at[2 * b + (1 - c), rows], dst_ref=out_refs[a].at[b, rows],
                send_sem=send_sems.at[b * n_p + p], recv_sem=recv_sems.at[b * n_p + p],
                device_id=(x, y, 1 - c), device_id_type=MESH)

        return [copy(b, p) for b in range(4) for p in range(n_p)]

    def start(self, big_refs, out_refs, sems):
        for cp in self._copies(big_refs, out_refs, sems):
            cp.start()

    def finish(self, big_refs, out_refs, sems):
        cps = self._copies(big_refs, out_refs, sems)
        for cp in cps:
            cp.wait_recv()
        for cp in cps:
            cp.wait_send()


def _standalone(exchange, args, name):
    n = exchange.n

    def body(*refs):
        exchange.start(refs[:n], refs[n:2 * n], refs[2 * n:])
        exchange.finish(refs[:n], refs[n:2 * n], refs[2 * n:])

    return pl.pallas_call(body, out_shape=exchange.shapes, in_specs=[_ANY] * n, out_specs=[_ANY] * n,
                          scratch_shapes=exchange.sems, name=name)(*args)


class Riding:
    def __init__(self, riders):
        self.riders = [(ex, list(args)) for ex, args in riders]
        self.args = [a for _, args in self.riders for a in args]
        self.in_specs = [_ANY] * len(self.args)
        self.out_shapes = [s for ex, _ in self.riders for s in ex.shapes]
        self.out_specs = [_ANY] * len(self.out_shapes)
        self.scratch = [s for ex, _ in self.riders for s in ex.sems]

    def wrap(self, body, n_in, n_out, n_scratch, is_first, is_last):
        def wrapped(*refs):
            k = 0
            core = list(refs[:n_in])
            k = n_in
            r_in = []
            for ex, _ in self.riders:
                r_in.append(refs[k:k + ex.n])
                k += ex.n
            core += refs[k:k + n_out]
            k += n_out
            r_out = []
            for ex, _ in self.riders:
                r_out.append(refs[k:k + ex.n])
                k += ex.n
            core += refs[k:k + n_scratch]
            k += n_scratch
            r_sem = []
            for ex, _ in self.riders:
                r_sem.append(refs[k:k + len(ex.sems)])
                k += len(ex.sems)

            @pl.when(is_first())
            def _():
                for (ex, _), a, b, s in zip(self.riders, r_in, r_out, r_sem):
                    ex.start(a, b, s)

            body(*core)

            @pl.when(is_last())
            def _():
                for (ex, _), a, b, s in zip(self.riders, r_in, r_out, r_sem):
                    ex.finish(a, b, s)

        return wrapped

    def split(self, outs, n_out):
        core, rest, per = list(outs[:n_out]), list(outs[n_out:]), []
        for ex, _ in self.riders:
            per.append(rest[:ex.n])
            rest = rest[ex.n:]
        return core, per


def pair_sum(big, sib, c, *, tr, name):
    _, R, C = big.shape

    def body(c_ref, a_ref, s_ref, o_ref):
        o_ref[...] = (a_ref[...].astype(F32) + s_ref[...].astype(F32)).astype(o_ref.dtype)

    grid_spec = pltpu.PrefetchScalarGridSpec(
        num_scalar_prefetch=1, grid=(4, R // tr),
        in_specs=[pl.BlockSpec((None, tr, C), lambda b, i, c_ref: (2 * b + c_ref[0], i, 0)),
                  pl.BlockSpec((None, tr, C), lambda b, i, c_ref: (b, i, 0))],
        out_specs=pl.BlockSpec((None, tr, C), lambda b, i, c_ref: (b, i, 0)))
    return pl.pallas_call(body, grid_spec=grid_spec, out_shape=jax.ShapeDtypeStruct((4, R, C), big.dtype),
                          compiler_params=_cparams(2), name=name)(c.reshape(1).astype(jnp.int32), big, sib)


class ChipScatter:
    def __init__(self, pres, chunks):
        self.shapes = [jax.ShapeDtypeStruct(p.shape, p.dtype) for p in pres]
        self.n = len(pres)
        self.pieces = _pieces(pres, chunks)
        n_p = len(self.pieces)
        self.sems = [pltpu.SemaphoreType.DMA((3 * n_p,)), pltpu.SemaphoreType.DMA((3 * n_p,)),
                     pltpu.SemaphoreType.DMA((n_p,))]

    def _copies(self, pre_refs, out_refs, sems):
        send_sems, recv_sems, local_sems = sems
        n_p = len(self.pieces)
        x, y, c = _my_pos()
        my_chip = 2 * x + y
        chips = [(1 - x, y), (x, 1 - y), (1 - x, 1 - y)]

        def copy(j, p, slot):
            px, py = chips[j]
            a, rows = self.pieces[p]
            src_slot, dst_slot = (2 * px + py, my_chip) if slot == "mine" else (my_chip, 2 * px + py)
            return pltpu.make_async_remote_copy(
                src_ref=pre_refs[a].at[src_slot, rows], dst_ref=out_refs[a].at[dst_slot, rows],
                send_sem=send_sems.at[j * n_p + p], recv_sem=recv_sems.at[j * n_p + p],
                device_id=(px, py, c), device_id_type=MESH)

        own = [pltpu.make_async_copy(pre_refs[a].at[my_chip, rows], out_refs[a].at[my_chip, rows], local_sems.at[p])
               for p, (a, rows) in enumerate(self.pieces)]
        sends = [copy(j, p, "mine") for j in range(3) for p in range(n_p)]
        recvs = [copy(j, p, "theirs") for j in range(3) for p in range(n_p)]
        return own, sends, recvs

    def start(self, pre_refs, out_refs, sems):
        own, sends, _ = self._copies(pre_refs, out_refs, sems)
        for cp in sends + own:
            cp.start()

    def finish(self, pre_refs, out_refs, sems):
        own, sends, recvs = self._copies(pre_refs, out_refs, sems)
        for cp in recvs:
            cp.wait_recv()
        for cp in sends:
            cp.wait_send()
        for cp in own:
            cp.wait()


def chip_scatter(pres, chunks, *, name):
    cs = ChipScatter(pres, chunks)
    n = cs.n

    def body(*refs):
        pre_refs, out_refs, sems = refs[:n], refs[n:2 * n], refs[2 * n:]
        cs.start(pre_refs, out_refs, sems)
        cs.finish(pre_refs, out_refs, sems)

    return pl.pallas_call(body, out_shape=cs.shapes, in_specs=[_ANY] * n, out_specs=[_ANY] * n,
                          scratch_shapes=cs.sems, name=name)(*pres)


def sibling_send(blks, chunks, *, name):
    n = len(blks)
    pieces = _pieces(blks, chunks)
    n_p = len(pieces)

    def body(*refs):
        x_refs, out_refs = refs[:n], refs[n:2 * n]
        send_sems, recv_sems = refs[2 * n:]
        x, y, c = _my_pos()
        cps = [pltpu.make_async_remote_copy(
            src_ref=x_refs[a].at[rows], dst_ref=out_refs[a].at[rows], send_sem=send_sems.at[p],
            recv_sem=recv_sems.at[p], device_id=(x, y, 1 - c), device_id_type=MESH)
            for p, (a, rows) in enumerate(pieces)]
        for cp in cps:
            cp.start()
        for cp in cps:
            cp.wait_recv()
        for cp in cps:
            cp.wait_send()

    return pl.pallas_call(
        body, out_shape=[jax.ShapeDtypeStruct(b.shape, b.dtype) for b in blks],
        in_specs=[_ANY] * n, out_specs=[_ANY] * n,
        scratch_shapes=[pltpu.SemaphoreType.DMA((n_p,)), pltpu.SemaphoreType.DMA((n_p,))],
        name=name)(*blks)


def reduce_slots(land, *, tr, name):
    n, R, C = land.shape

    def body(l_ref, o_ref):
        acc = l_ref[0].astype(F32)
        for s in range(1, n):
            acc = acc + l_ref[s].astype(F32)
        o_ref[...] = acc

    return pl.pallas_call(
        body, grid=(R // tr,), in_specs=[pl.BlockSpec((n, tr, C), lambda i: (0, i, 0))],
        out_specs=pl.BlockSpec((tr, C), lambda i: (i, 0)), out_shape=jax.ShapeDtypeStruct((R, C), F32),
        compiler_params=_cparams(1), name=name)(land)


TM = 512


def _tk(d):
    return min(d.shape[0], 1024)


def ffn_fwd_fused(x, g, wgu, wd, *, tm, name, riders=None, rider_args=()):
    T, Dm = x.shape
    nb, _, cb = wgu.shape
    nh = nb // 2
    Fd = nh * cb
    nc = riders.n if riders is not None else 0
    n_steps = T // tm
    stage_at = (0, (2 * n_steps) // 3, n_steps - 1, n_steps - 1)

    def body(*refs):
        x_ref, g_ref, wgu_ref, wd_ref = refs[:4]
        r_in = refs[4:4 + nc]
        o_ref, gu_ref, h_ref = refs[4 + nc:7 + nc]
        r_out, sems = refs[7 + nc:7 + 2 * nc], refs[7 + 2 * nc:]

        def ride(which):
            if riders is not None:
                @pl.when(pl.program_id(0) == stage_at[which])
                def _():
                    riders.stage(which, r_in, r_out, sems)

        ride(0)
        ride(1)
        xv = x_ref[...]
        r = lax.rsqrt(jnp.mean(xv * xv, axis=-1, keepdims=True) + RMS_EPS)
        hb = (xv * r * g_ref[...]).astype(BF16)
        h_ref[...] = hb
        acc = jnp.zeros((tm, Dm), F32)
        for jj in range(nh):
            cols = pl.ds(cb * jj, cb)
            gate = _dot(hb, wgu_ref[jj]).astype(BF16)
            up = _dot(hb, wgu_ref[nh + jj]).astype(BF16)
            gu_ref[0, :, cols] = gate
            gu_ref[1, :, cols] = up
            gv = gate.astype(F32)
            act = (gv * _sigmoid(gv) * up.astype(F32)).astype(BF16)
            acc = acc + _dot(act, wd_ref[cols, :])
        o_ref[...] = xv + 0.5 * acc
        ride(2)
        ride(3)

    outs = pl.pallas_call(
        body, grid=(n_steps,),
        in_specs=[pl.BlockSpec((tm, Dm), lambda i: (i, 0)), pl.BlockSpec((1, Dm), lambda i: (0, 0)),
                  pl.BlockSpec((nb, Dm, cb), lambda i: (0, 0, 0)), pl.BlockSpec((Fd, Dm), lambda i: (0, 0))]
        + [_ANY] * nc,
        out_specs=[pl.BlockSpec((tm, Dm), lambda i: (i, 0)), pl.BlockSpec((2, tm, Fd), lambda i: (0, i, 0)),
                   pl.BlockSpec((tm, Dm), lambda i: (i, 0))] + [_ANY] * nc,
        out_shape=[jax.ShapeDtypeStruct((T, Dm), F32), jax.ShapeDtypeStruct((2, T, Fd), BF16),
                   jax.ShapeDtypeStruct((T, Dm), BF16)] + (riders.shapes if nc else []),
        scratch_shapes=riders.sems if nc else [],
        compiler_params=_cparams(1), name=name)(x, g, wgu, wd, *rider_args)
    return outs[0], outs[1], outs[2], list(outs[3:])


def ffn_fwd(x, g, wgu, wd, tag, riders=None, rider_args=()):
    xo, gu, h, rode = ffn_fwd_fused(x, g, wgu, wd, tm=256, name=f"{tag}_fwd", riders=riders, rider_args=rider_args)
    return xo, (x, gu, h), rode


def ffn_bwd(d, saved, g, wgu, wd, tag, ride_bact=None, ride_dwgu=None, before_dx=None):
    x, gu, h = saved
    dgu, dwd, *rode_a = ffn_bwd_act(d, wd, gu, tm=TM, tn=1408, name=f"{tag}_bact", riding=ride_bact)
    dwgu = mm_tn(h, dgu, scale=1.0, a_split=False, b_split=True, tm=TM, tn=1408, tk=_tk(d), out_blocked=True,
                 name=f"{tag}_dwgu", riding=ride_dwgu)
    rode_g = []
    if ride_dwgu is not None:
        dwgu, *rode_g = dwgu
    riding = before_dx(dwgu, dwd) if before_dx is not None else None
    dx, dg, *rode_x = mm_nt_normbwd(dgu, wgu, x, g, d, a_split=True, tm=_tk(d), tk=1408, name=f"{tag}_dx",
                                    riding=riding)
    return dx, dg, dwgu, dwd, (rode_a, rode_g, rode_x)


def _tile2(v):
    return jnp.concatenate([v, v], axis=-1).reshape(1, LANES)


def _fold2(v):
    return v[:, :HEAD_DIM] + v[:, HEAD_DIM:]


EVEN = dict(dil=1, nsub=2, ppk=4, q_blk=0, k_blk=4, v_blk=5, n_heads=A_Q_HEADS, group=A_GROUP, max_dist=A_WINDOW - 1)
STICK = dict(q_blk=6, k_blk=10, v_blk=14, n_pairs=4)


def _odd_cfg(dil):
    return dict(dil=dil, nsub=4 if dil == 1 else 1, ppk=1, q_blk=0, k_blk=8, v_blk=16, n_heads=C_HEADS, group=1,
                max_dist=BLK)


def even_fwd(x, g, win, qg, kg, sinks, wout, tag, riders=None, rider_args=()):
    qkv, h = norm_matmul(x, g, win, tm=_tk(x), tn=1152, split=False, name=f"{tag}_in")
    qg2, kg2 = _tile2(qg), _tile2(kg)
    slopes = jnp.asarray(_alibi(A_Q_HEADS), F32)
    qkn = qk_norm(qkv, qg2, kg2, width=768, steps=1, n_q=4, tm=TM, name=f"{tag}_qkn")
    oa, _, *lse = banded_fwd(qkn, qkv, slopes, sinks, name=f"{tag}_swa", **EVEN)
    ob, rode = stick_fwd(qkv, name=f"{tag}_stick", riders=riders, rider_args=rider_args, **STICK)
    o = jnp.concatenate([oa, ob], axis=1)
    xo = mm_nn(o, wout, res=x, tm=TM, tn=D_MODEL, tk=D_MODEL, name=f"{tag}_out")
    return xo, (x, qkv, qkn, h, oa, lse, o), rode


def even_bwd(d, saved, g, win, qg, kg, sinks, wout, tag, riders=None, rider_args=(), before_dx=None):
    x, qkv, qkn, h, oa, lse, o = saved
    qg2, kg2 = _tile2(qg), _tile2(kg)
    slopes = jnp.asarray(_alibi(A_Q_HEADS), F32)
    dwout = mm_tn(o, d, scale=1.0, a_split=False, b_split=False, tm=D_MODEL, tn=D_MODEL, tk=_tk(d), name=f"{tag}_dwout")
    do = mm_nt(d, wout, tm=TM, tn=D_MODEL, tk=D_MODEL, name=f"{tag}_do")
    dqa, dka4, dva4, dsk = banded_bwd(qkn, qkv, slopes, sinks, do, oa, lse, None, None,
                                      do_blk=0, name=f"{tag}_swa_b", **EVEN)
    dqb, dkb, dvb, rode = stick_bwd(qkv, do, do_blk=4, name=f"{tag}_stick_b", riders=riders, rider_args=rider_args,
                                    **STICK)
    dqkv, dqg, dkg = assemble_even(dqa, dka4, dva4, dqb, dkb, dvb, qkv, qg2, kg2, tm=TM, name=f"{tag}_asm")
    dwin = mm_tn(h, dqkv, scale=1.0, a_split=False, b_split=False, tm=D_MODEL, tn=1152, tk=_tk(d), name=f"{tag}_dwin")
    riding = before_dx(dwin, dwout) if before_dx is not None else None
    dx, dg, *rode_x = mm_nt_normbwd(dqkv, win, x, g, d, a_split=False, tm=TM, tk=1152, name=f"{tag}_dx", riding=riding)
    return dx, dg, dwin, _fold2(dqg), _fold2(dkg), dsk[:, :A_Q_HEADS], dwout, (rode, rode_x)


def odd_fwd(x, g, win, qg, kg, wout, tag):
    qkv, h = norm_matmul(x, g, win, tm=_tk(x), tn=768, split=False, name=f"{tag}_in")
    qg2, kg2 = _tile2(qg), _tile2(kg)
    qkn = qk_norm(qkv, qg2, kg2, width=D_MODEL, steps=2, n_q=8, tm=TM, name=f"{tag}_qkn")
    outs = []
    for p, (window, dil) in enumerate(C_PATTERNS):
        slopes = jnp.asarray(_alibi(C_HEADS), F32) * float(dil)
        outs.append(banded_fwd(qkn, qkv, slopes, None, name=f"{tag}_dil{p}", **_odd_cfg(dil)))
    o, w1, w2, w3 = mix_fwd(outs[0][0], outs[1][0], outs[2][0], outs[0][1], outs[1][1], outs[2][1],
                            tm=TM, name=f"{tag}_mix")
    xo = mm_nn(o, wout, res=x, tm=TM, tn=D_MODEL, tk=D_MODEL, name=f"{tag}_out")
    return xo, (x, qkv, qkn, h, outs, (w1, w2, w3), o)


def odd_bwd(d, saved, g, win, qg, kg, wout, tag):
    x, qkv, qkn, h, outs, ws, o = saved
    qg2, kg2 = _tile2(qg), _tile2(kg)
    dwout = mm_tn(o, d, scale=1.0, a_split=False, b_split=False, tm=D_MODEL, tn=D_MODEL, tk=_tk(d), name=f"{tag}_dwout")
    do = mm_nt(d, wout, tm=TM, tn=D_MODEL, tk=D_MODEL, name=f"{tag}_do")
    parts = []
    for p, (window, dil) in enumerate(C_PATTERNS):
        slopes = jnp.asarray(_alibi(C_HEADS), F32) * float(dil)
        dq, dk, dv, _ = banded_bwd(qkn, qkv, slopes, None, do, None, outs[p][2:], ws[p], o,
                                   do_blk=0, name=f"{tag}_dil{p}_b", **_odd_cfg(dil))
        parts.append((dq, dk, dv))
    dqkv, dqg, dkg = assemble_odd(parts, qkv, qg2, kg2, tm=256, name=f"{tag}_asm")
    dwin = mm_tn(h, dqkv, scale=1.0, a_split=False, b_split=False, tm=TM, tn=768, tk=_tk(d), out_blocked=True,
                 name=f"{tag}_dwin")
    dx, dg = mm_nt_normbwd(dqkv, win, x, g, d, a_split=False, tm=TM, tk=768, name=f"{tag}_dx")
    return dx, dg, dwin, _fold2(dqg), _fold2(dkg), dwout


def xa_fwd(x, mem, g, gm, wq, wkv, qg, kg, wo, tag):
    qraw, h = norm_matmul(x, g, wq, tm=TM, tn=D_MODEL, split=False, name=f"{tag}_q")
    kvraw, hm = norm_matmul(mem, gm, wkv, tm=MEM_LEN, tn=512, split=False, name=f"{tag}_kv")
    o = xattn_fwd(qraw, kvraw, qg, kg, tm=TM, name=f"{tag}_att")
    xo = mm_nn(o, wo, res=x, tm=TM, tn=D_MODEL, tk=D_MODEL, name=f"{tag}_o")
    return xo, (x, qraw, h, kvraw, hm, o)


def xa_bwd(d, saved, mem, g, gm, wq, wkv, qg, kg, wo, tag):
    x, qraw, h, kvraw, hm, o = saved
    dwo = mm_tn(o, d, scale=1.0, a_split=False, b_split=False, tm=D_MODEL, tn=D_MODEL, tk=_tk(d), name=f"{tag}_dwo")
    do = mm_nt(d, wo, tm=TM, tn=D_MODEL, tk=D_MODEL, name=f"{tag}_do")
    dq, dkv, dqg, dkg = xattn_bwd(qraw, kvraw, qg, kg, do, o, tm=TM, name=f"{tag}_att_b")
    dwq = mm_tn(h, dq, scale=1.0, a_split=False, b_split=False, tm=D_MODEL, tn=D_MODEL, tk=_tk(d), name=f"{tag}_dwq")
    dx, dg = mm_nt_normbwd(dq, wq, x, g, d, a_split=False, tm=TM, tk=D_MODEL, name=f"{tag}_dx")
    dwkv = mm_tn(hm, dkv, scale=1.0, a_split=False, b_split=False, tm=TM, tn=512, tk=MEM_LEN, out_blocked=True,
                 name=f"{tag}_dwkv")
    _, dgm = mm_nt_normbwd(dkv, wkv, mem, gm, None, a_split=False, tm=MEM_LEN, tk=512, name=f"{tag}_dmem")
    return dx, dg, dgm, dwq, dwkv, dqg, dkg, dwo


MATS = (("ffn1_w_gu", 1), ("ffn1_w_down", 0), ("ev_w_in", 1), ("ev_w_out", 0), ("od_w_in", 1), ("od_w_out", 0),
        ("xa_w_q", 0), ("xa_w_kv", 1), ("xa_w_o", 0), ("ffn2_w_gu", 1), ("ffn2_w_down", 0))
SMALLS = ("ffn1_norm", "mix_norm", "ev_q_gain", "ev_k_gain", "ev_sinks", "od_q_gain", "od_k_gain", "xa_norm",
          "xa_mem_norm", "xa_q_gain", "xa_k_gain", "ffn2_norm")
WEIGHTS = ("ffn1_norm", "ffn1_w_gu", "ffn1_w_down", "mix_norm", "ev_w_in", "ev_q_gain", "ev_k_gain", "ev_sinks",
           "ev_w_out", "od_w_in", "od_q_gain", "od_k_gain", "od_w_out", "xa_norm", "xa_mem_norm", "xa_w_q",
           "xa_w_kv", "xa_q_gain", "xa_k_gain", "xa_w_o", "ffn2_norm", "ffn2_w_gu", "ffn2_w_down")
SMALL_ROWS = 16
LAYER_GROUPS = (
    (((("ffn1_w_gu", 0), ("ffn2_w_gu", 0)), 4, 512),
     ((("ffn1_w_down", 0), ("ffn2_w_down", 0)), 2, 352),
     ((("ev_w_out", 0), ("xa_w_q", 0), ("xa_w_o", 0)), 1, 384),
     ((("xa_w_kv", 0),), 1, 512),
     ((("ev_w_in", 0),), 1, 512)),
    (((("ffn1_w_gu", 1), ("ffn2_w_gu", 1)), 4, 512),
     ((("ffn1_w_down", 1), ("ffn2_w_down", 1)), 2, 352),
     ((("od_w_out", 0), ("xa_w_q", 1), ("xa_w_o", 1)), 1, 384),
     ((("xa_w_kv", 1),), 1, 512),
     ((("od_w_in", 0),), 1, 512)),
)
GATHER_FIRST = (((("ffn1_w_gu", 0),), 2, 512), ((("ffn1_w_down", 0),), 1, 352), ((("ev_w_out", 0),), 1, 128),
                ((("ev_w_in", 0),), 1, 512))
GATHER_IN_FFN1 = (((("ffn2_w_gu", 0),), 2, 512), ((("ffn2_w_down", 0),), 1, 352),
                  ((("xa_w_q", 0), ("xa_w_o", 0)), 1, 256), ((("xa_w_kv", 0),), 1, 512))
GATHER_IN_STICK = (((("ffn1_w_gu", 1),), 2, 512), ((("ffn1_w_down", 1),), 1, 352), ((("od_w_out", 0),), 1, 128),
                   ((("od_w_in", 0),), 1, 512))
GATHER_IN_FFN2 = (((("ffn2_w_gu", 1),), 2, 512), ((("ffn2_w_down", 1),), 1, 352),
                  ((("xa_w_q", 1), ("xa_w_o", 1)), 1, 256), ((("xa_w_kv", 1),), 1, 512))
ROUNDS = {
    "1": LAYER_GROUPS[1],
    "0a": (((("ffn2_w_gu", 0),), 2, 512), ((("ffn2_w_down", 0),), 1, 352)) + LAYER_GROUPS[0][2:],
    "0b": (((("ffn1_w_gu", 0),), 2, 512), ((("ffn1_w_down", 0),), 1, 352)),
}


def _chunks_of(groups):
    return tuple(g[1] for g in groups)
COL_SHARDED = {name for name, axis in MATS if axis == 1}
BLOCKED = {"ffn1_w_gu", "ffn2_w_gu", "xa_w_kv", "od_w_in"}


def group_halves(shards, c, groups):
    out = []
    for members, _, _ in groups:
        halves = []
        for name, layer in members:
            _, r, cc = shards[name].shape
            half = lax.dynamic_index_in_dim(shards[name][layer].reshape(2, r // 2, cc), c, 0, keepdims=False)
            halves.append(half.astype(BF16))
        out.append(jnp.concatenate(halves, axis=0))
    return out


def full_weights(gathered, shards, groups):
    full = {}
    for (members, _, _), arr in zip(groups, gathered):
        for w, (name, layer) in enumerate(members):
            _, r, cc = shards[name].shape
            piece = arr[:, w * (r // 2):(w + 1) * (r // 2)].reshape(4, r, cc)
            if name not in COL_SHARDED:
                piece = piece.reshape(4 * r, cc)
            elif name not in BLOCKED:
                piece = piece.transpose(1, 0, 2).reshape(r, 4 * cc)
            full[(name, layer)] = piece
    return full


def group_grads(grads, shards, groups):
    out = []
    for members, _, _ in groups:
        parts = []
        for name, layer in members:
            _, r, cc = shards[name].shape
            gfull = grads[(name, layer)]
            if name in COL_SHARDED and name not in BLOCKED:
                gfull = gfull.reshape(2, r // 2, 4, cc).transpose(2, 0, 1, 3)
            parts.append(gfull.reshape(N_DEV, r // 2, cc))
        out.append(jnp.concatenate(parts, axis=1))
    return out


def shard_grads(mine, theirs, c, shards, groups):
    per = {}
    for (members, _, _), a, b in zip(groups, mine, theirs):
        for w, (name, layer) in enumerate(members):
            _, r, cc = shards[name].shape
            rows = slice(w * (r // 2), (w + 1) * (r // 2))
            lo = jnp.where(c == 0, a[rows], b[rows])
            hi = jnp.where(c == 0, b[rows], a[rows])
            per[(name, layer)] = jnp.concatenate([lo, hi], axis=0)
    return per


def pack_small(vals):
    row10 = jnp.concatenate([vals["xa_q_gain"].reshape(1, 512), vals["xa_k_gain"].reshape(1, 512)], axis=1)
    row11 = jnp.concatenate([vals["ev_q_gain"], vals["ev_k_gain"], vals["od_q_gain"], vals["od_k_gain"],
                             vals["ev_sinks"], jnp.zeros((1, 1024 - 4 * 64 - 8), F32)], axis=1)
    return jnp.concatenate([vals["ffn1_norm"], vals["mix_norm"], vals["xa_norm"], vals["xa_mem_norm"],
                            vals["ffn2_norm"], row10, row11, jnp.zeros((SMALL_ROWS - 12, 1024), F32)], axis=0)


def unpack_small(arr):
    return {"ffn1_norm": arr[0:2], "mix_norm": arr[2:4], "xa_norm": arr[4:6], "xa_mem_norm": arr[6:8],
            "ffn2_norm": arr[8:10],
            "xa_q_gain": arr[10:11, 0:512].reshape(2, 256), "xa_k_gain": arr[10:11, 512:1024].reshape(2, 256),
            "ev_q_gain": arr[11:12, 0:64], "ev_k_gain": arr[11:12, 64:128], "od_q_gain": arr[11:12, 128:192],
            "od_k_gain": arr[11:12, 192:256], "ev_sinks": arr[11:12, 256:264]}


def local_step(x, mem, target, W, small, prereduce, later):
    depth = small["ffn1_norm"].shape[0]

    def row(name, l):
        return small[name][l:l + 1]

    saved = []
    for l in range(depth):
        j = l // 2
        def riding_gather(host):
            if l == 0 and host in later:
                return GatherBlocks(later[host][0], later[host][1]), later[host][0]
            return None, ()

        riders, rider_args = riding_gather("ffn1")
        x, s1, rode = ffn_fwd(x, row("ffn1_norm", l), W[("ffn1_w_gu", l)], W[("ffn1_w_down", l)], f"l{l}_f1",
                              riders=riders, rider_args=rider_args)
        if riders is not None:
            W = {**W, **later["ffn1"][2](rode)}
        if l % 2 == 0:
            riders, rider_args = riding_gather("stick")
            x, s2, rode = even_fwd(x, row("mix_norm", l), W[("ev_w_in", j)], row("ev_q_gain", j),
                                   row("ev_k_gain", j), small["ev_sinks"][j], W[("ev_w_out", j)], f"l{l}_ev",
                                   riders=riders, rider_args=rider_args)
            if riders is not None:
                W = {**W, **later["stick"][2](rode)}
        else:
            x, s2 = odd_fwd(x, row("mix_norm", l), W[("od_w_in", j)], row("od_q_gain", j), row("od_k_gain", j),
                            W[("od_w_out", j)], f"l{l}_od")
        x, s3 = xa_fwd(x, mem, row("xa_norm", l), row("xa_mem_norm", l), W[("xa_w_q", l)], W[("xa_w_kv", l)],
                       row("xa_q_gain", l), row("xa_k_gain", l), W[("xa_w_o", l)], f"l{l}_xa")
        riders, rider_args = riding_gather("ffn2")
        x, s4, rode = ffn_fwd(x, row("ffn2_norm", l), W[("ffn2_w_gu", l)], W[("ffn2_w_down", l)], f"l{l}_f2",
                              riders=riders, rider_args=rider_args)
        if riders is not None:
            W = {**W, **later["ffn2"][2](rode)}
        saved.append((s1, s2, s3, s4))
    loss, d = loss_kernel(x, target, tm=TM, name="loss")

    gw = {}
    gs = {name: [None] * small[name].shape[0] for name in SMALLS}
    pending, landed = None, {}
    for l in reversed(range(depth)):
        j = l // 2
        s1, s2, s3, s4 = saved[l]
        d, dg, dwgu, dwd, _ = ffn_bwd(d, s4, row("ffn2_norm", l), W[("ffn2_w_gu", l)], W[("ffn2_w_down", l)],
                                      f"l{l}_f2")
        gs["ffn2_norm"][l] = dg
        gw[("ffn2_w_gu", l)], gw[("ffn2_w_down", l)] = dwgu, dwd
        d, dg, dgm, dwq, dwkv, dqg, dkg, dwo = xa_bwd(
            d, s3, mem, row("xa_norm", l), row("xa_mem_norm", l), W[("xa_w_q", l)], W[("xa_w_kv", l)],
            row("xa_q_gain", l), row("xa_k_gain", l), W[("xa_w_o", l)], f"l{l}_xa")
        gs["xa_norm"][l], gs["xa_mem_norm"][l], gs["xa_q_gain"][l], gs["xa_k_gain"][l] = dg, dgm, dqg, dkg
        gw[("xa_w_q", l)], gw[("xa_w_kv", l)], gw[("xa_w_o", l)] = dwq, dwkv, dwo
        split = l == 0 and l % 2 == 0 and ("0a" in ROUNDS)
        early = []
        pre_early = None
        if l % 2 == 0:
            riders, rider_args = None, ()
            if pending is not None:
                riders, rider_args = ChipScatter(pending[1], _chunks_of(ROUNDS[pending[0]])), pending[1]

            def before_mixer_dx(dwin, dwout, j=j, early=early):
                gw[("ev_w_in", j)], gw[("ev_w_out", j)] = dwin, dwout
                early += prereduce.pack(gw, "0a")
                return Riding([(PairExchange(early, _chunks_of(ROUNDS["0a"])), early)])

            d, dg, dwin, dqg, dkg, dsk, dwout, (rode, rode_x) = even_bwd(
                d, s2, row("mix_norm", l), W[("ev_w_in", j)], row("ev_q_gain", j), row("ev_k_gain", j),
                small["ev_sinks"][j], W[("ev_w_out", j)], f"l{l}_ev", riders=riders, rider_args=rider_args,
                before_dx=before_mixer_dx if split else None)
            if pending is not None:
                landed[pending[0]], pending = rode, None
            gs["ev_q_gain"][j], gs["ev_k_gain"][j], gs["ev_sinks"][j] = dqg, dkg, dsk
            gw[("ev_w_in", j)], gw[("ev_w_out", j)] = dwin, dwout
            if split:
                pre_early = prereduce.sums(early, rode_x[0], "0a")
        else:
            d, dg, dwin, dqg, dkg, dwout = odd_bwd(
                d, s2, row("mix_norm", l), W[("od_w_in", j)], row("od_q_gain", j), row("od_k_gain", j),
                W[("od_w_out", j)], f"l{l}_od")
            gs["od_q_gain"][j], gs["od_k_gain"][j] = dqg, dkg
            gw[("od_w_in", j)], gw[("od_w_out", j)] = dwin, dwout
        gs["mix_norm"][l] = dg
        packed = []

        rnd = "0b" if pre_early is not None else str(l)
        final = l == 0

        def before_dx(dwgu, dwd, l=l, packed=packed, rnd=rnd, final=final):
            gw[("ffn1_w_gu", l)], gw[("ffn1_w_down", l)] = dwgu, dwd
            packed += prereduce.pack(gw, rnd)
            chunks = _chunks_of(ROUNDS[rnd])
            if not final:
                return Riding([(PairExchange(packed, chunks), packed)])
            sib = _standalone(PairExchange(packed, chunks), packed, f"pair_grads{rnd}")
            pre = prereduce.sums(packed, sib, rnd)
            return Riding([(ChipScatter(pre, chunks), pre)])

        ride_bact = ride_dwgu = None
        if pre_early is not None:
            chunks = _chunks_of(ROUNDS["0a"])
            ride_bact = Riding([(ChipScatter(pre_early[:2], chunks[:2]), pre_early[:2])])
            ride_dwgu = Riding([(ChipScatter(pre_early[2:], chunks[2:]), pre_early[2:])])
        d, dg, dwgu, dwd, (rode_a, rode_g, rode_x) = ffn_bwd(
            d, s1, row("ffn1_norm", l), W[("ffn1_w_gu", l)], W[("ffn1_w_down", l)], f"l{l}_f1",
            ride_bact=ride_bact, ride_dwgu=ride_dwgu, before_dx=before_dx)
        gs["ffn1_norm"][l] = dg
        if pre_early is not None:
            landed["0a"] = list(rode_a[0]) + list(rode_g[0])
        if pending is not None:
            landed[pending[0]] = chip_scatter(pending[1], _chunks_of(ROUNDS[pending[0]]),
                                              name=f"scatter_grads{pending[0]}")
        if final:
            landed[rnd], pending = rode_x[0], None
        else:
            pending = (rnd, prereduce.sums(packed, rode_x[0], rnd))
    if pending is not None:
        landed[pending[0]] = chip_scatter(pending[1], _chunks_of(ROUNDS[pending[0]]),
                                          name=f"scatter_grads{pending[0]}")
    gsmall = {name: jnp.concatenate(v, axis=0) for name, v in gs.items()}
    return loss, d, landed, gsmall


def kernel(x, mem, ffn1_norm, ffn1_w_gu, ffn1_w_down, mix_norm, ev_w_in, ev_q_gain, ev_k_gain, ev_sinks, ev_w_out, od_w_in, od_q_gain, od_k_gain, od_w_out, xa_norm, xa_mem_norm, xa_w_q, xa_w_kv, xa_q_gain, xa_k_gain, xa_w_o, ffn2_norm, ffn2_w_gu, ffn2_w_down, loss_target, m_ffn1_norm, m_ffn1_w_gu, m_ffn1_w_down, m_mix_norm, m_ev_w_in, m_ev_q_gain, m_ev_k_gain, m_ev_sinks, m_ev_w_out, m_od_w_in, m_od_q_gain, m_od_k_gain, m_od_w_out, m_xa_norm, m_xa_mem_norm, m_xa_w_q, m_xa_w_kv, m_xa_q_gain, m_xa_k_gain, m_xa_w_o, m_ffn2_norm, m_ffn2_w_gu, m_ffn2_w_down, v_ffn1_norm, v_ffn1_w_gu, v_ffn1_w_down, v_mix_norm, v_ev_w_in, v_ev_q_gain, v_ev_k_gain, v_ev_sinks, v_ev_w_out, v_od_w_in, v_od_q_gain, v_od_k_gain, v_od_w_out, v_xa_norm, v_xa_mem_norm, v_xa_w_q, v_xa_w_kv, v_xa_q_gain, v_xa_k_gain, v_xa_w_o, v_ffn2_norm, v_ffn2_w_gu, v_ffn2_w_down):
    given = dict(locals())
    w = {n: given[n] for n in WEIGHTS}
    m = {n: given["m_" + n] for n in WEIGHTS}
    v = {n: given["v_" + n] for n in WEIGHTS}
    c = lax.axis_index("c")
    shards = {name: w[name] for name, _ in MATS}
    small = {n: w[n] for n in SMALLS}

    def gathering(groups):
        return group_halves(shards, c, groups), _chunks_of(groups), lambda got: full_weights(got, shards, groups)

    halves, chunks, unpack = gathering(GATHER_FIRST)
    full = unpack(gather_blocks(halves, chunks, name="gather_weights0"))
    later = {"ffn1": gathering(GATHER_IN_FFN1), "stick": gathering(GATHER_IN_STICK), "ffn2": gathering(GATHER_IN_FFN2)}

    class prereduce:
        @staticmethod
        def pack(gw, rnd):
            return group_grads(gw, shards, ROUNDS[rnd])

        @staticmethod
        def sums(packed, sib, rnd):
            return [pair_sum(p, s, c, tr=g[2], name=f"pair_sum{rnd}_{i}")
                    for i, (g, p, s) in enumerate(zip(ROUNDS[rnd], packed, sib))]

    loss_b, grad_x, landed, gsmall = local_step(x[0], mem[0], loss_target[0], full, small, prereduce, later)

    per = {}
    for rnd, land in sorted(landed.items()):
        groups = ROUNDS[rnd]
        mine = [reduce_slots(a, tr=g[2], name=f"sum_grads{rnd}_{i}") for i, (g, a) in enumerate(zip(groups, land))]
        theirs = sibling_send(mine, _chunks_of(groups), name=f"swap_grads{rnd}")
        per.update(shard_grads(mine, theirs, c, shards, groups))
    g = {name: jnp.stack([per[(name, layer)] for layer in range(w[name].shape[0])], axis=0) for name, _ in MATS}
    land_small = gather_small(pack_small(gsmall), name="gather_small")
    g_small = unpack_small(reduce_slots(land_small, tr=SMALL_ROWS, name="sum_small"))
    g.update(g_small)

    delta, new_m, new_v = {}, {}, {}
    for name, _ in MATS:
        shp = w[name].shape
        flat = [a.reshape(-1, shp[-1]) for a in (w[name], g[name], m[name], v[name])]
        dl, nm, nv = adamw(*flat, br=BLK, name=f"adamw_{name}")
        delta[name], new_m[name], new_v[name] = dl.reshape(shp), nm.reshape(shp), nv.reshape(shp)
    dl, nm, nv = adamw(pack_small(small), pack_small(g_small), pack_small({n: m[n] for n in SMALLS}),
                       pack_small({n: v[n] for n in SMALLS}), br=SMALL_ROWS, name="adamw_small")
    for dst, arr in ((delta, dl), (new_m, nm), (new_v, nv)):
        dst.update(unpack_small(arr))

    loss = lax.psum(loss_b[0, 0], ("x", "y", "c"))
    return (loss, grad_x[None], *[g[n] for n in WEIGHTS], *[delta[n] for n in WEIGHTS],
            *[new_m[n] for n in WEIGHTS], *[new_v[n] for n in WEIGHTS])
```

```python
import jax
import jax.numpy as jnp
from jax import lax
from jax.experimental import pallas as pl
from jax.experimental.pallas import tpu as pltpu

F32 = jnp.float32
BF16 = jnp.bfloat16

D_MODEL = 1024
HEAD_DIM = 64
LANES = 128
BLK = 128
D_FF = 2816
RMS_EPS = 1e-6
MEM_LEN = 256
X_HEADS = 4
X_HEAD_DIM = 256
A_Q_HEADS = 8
A_GROUP = 4
A_WINDOW = 128
C_HEADS = 16
C_PATTERNS = ((128, 1), (512, 4), (2048, 16))
NEG = -1e30
VMEM_LIMIT = 56 * 2 ** 20

ADAM_LR = 0.001
ADAM_B1 = 0.9
ADAM_B2 = 0.999
ADAM_EPS = 1e-08
ADAM_WD = 0.01
ADAM_STEP = 10

N_DEV = 8
MESH = pl.DeviceIdType.MESH


def _cparams(n):
    return pltpu.CompilerParams(dimension_semantics=("arbitrary",) * n, vmem_limit_bytes=VMEM_LIMIT)


def _dot(a, b):
    return jnp.dot(a, b, preferred_element_type=F32)


def _dot_nt(a, b):
    return lax.dot_general(a, b, (((1,), (1,)), ((), ())), preferred_element_type=F32)


def _dot_tn(a, b):
    return lax.dot_general(a, b, (((0,), (0,)), ((), ())), preferred_element_type=F32)


def _sigmoid(z):
    return 1.0 / (1.0 + jnp.exp(-z))


def norm_matmul(x, g, w, *, tm, tn, split, name):
    T, K = x.shape
    blocked = w.ndim == 3
    assert not blocked or w.shape[2] == tn
    N = w.shape[0] * w.shape[2] if blocked else w.shape[1]
    nj = N // tn

    def body(x_ref, g_ref, w_ref, o_ref, h_ref):
        @pl.when(pl.program_id(1) == 0)
        def _():
            xv = x_ref[...]
            r = lax.rsqrt(jnp.mean(xv * xv, axis=-1, keepdims=True) + RMS_EPS)
            h_ref[...] = (xv * r * g_ref[...]).astype(BF16)

        o_ref[...] = _dot(h_ref[...], w_ref[...]).astype(o_ref.dtype)

    if split:
        njh = nj // 2
        o_shape = jax.ShapeDtypeStruct((2, T, N // 2), BF16)
        o_spec = pl.BlockSpec((None, tm, tn), lambda i, j: (j // njh, i, j % njh))
    else:
        o_shape = jax.ShapeDtypeStruct((T, N), F32)
        o_spec = pl.BlockSpec((tm, tn), lambda i, j: (i, j))
    return pl.pallas_call(
        body, grid=(T // tm, nj),
        in_specs=[pl.BlockSpec((tm, K), lambda i, j: (i, 0)),
                  pl.BlockSpec((1, K), lambda i, j: (0, 0)),
                  (pl.BlockSpec((None, K, tn), lambda i, j: (j, 0, 0)) if blocked
                   else pl.BlockSpec((K, tn), lambda i, j: (0, j)))],
        out_specs=[o_spec, pl.BlockSpec((tm, K), lambda i, j: (i, 0))],
        out_shape=[o_shape, jax.ShapeDtypeStruct((T, K), BF16)],
        compiler_params=_cparams(2), name=name)(x, g, w)


def mm_nn(a, b, *, res, tm, tn, tk, name):
    T = a.shape[0]
    K, N = b.shape
    nk = K // tk

    def body(a_ref, b_ref, r_ref, o_ref, acc):
        k = pl.program_id(2)

        @pl.when(k == 0)
        def _():
            acc[...] = jnp.zeros_like(acc)

        acc[...] += _dot(a_ref[...].astype(BF16), b_ref[...])

        @pl.when(k == nk - 1)
        def _():
            o_ref[...] = r_ref[...] + acc[...]

    return pl.pallas_call(
        body, grid=(T // tm, N // tn, nk),
        in_specs=[pl.BlockSpec((tm, tk), lambda i, j, k: (i, k)), pl.BlockSpec((tk, tn), lambda i, j, k: (k, j)),
                  pl.BlockSpec((tm, tn), lambda i, j, k: (i, j))],
        out_specs=pl.BlockSpec((tm, tn), lambda i, j, k: (i, j)),
        out_shape=jax.ShapeDtypeStruct((T, N), F32),
        scratch_shapes=[pltpu.VMEM((tm, tn), F32)],
        compiler_params=_cparams(3), name=name)(a, b, res)


def mm_nt(a, b, *, tm, tn, tk, name):
    T, K = a.shape
    N = b.shape[0]
    nk = K // tk

    def body(a_ref, b_ref, o_ref, acc):
        k = pl.program_id(2)

        @pl.when(k == 0)
        def _():
            acc[...] = jnp.zeros_like(acc)

        acc[...] += _dot_nt(a_ref[...].astype(BF16), b_ref[...])

        @pl.when(k == nk - 1)
        def _():
            o_ref[...] = acc[...]

    return pl.pallas_call(
        body, grid=(T // tm, N // tn, nk),
        in_specs=[pl.BlockSpec((tm, tk), lambda i, j, k: (i, k)),
                  pl.BlockSpec((tn, tk), lambda i, j, k: (j, k))],
        out_specs=pl.BlockSpec((tm, tn), lambda i, j, k: (i, j)),
        out_shape=jax.ShapeDtypeStruct((T, N), F32),
        scratch_shapes=[pltpu.VMEM((tm, tn), F32)],
        compiler_params=_cparams(3), name=name)(a, b)


def ffn_bwd_act(d, wd, gu, *, tm, tn, name, riding=None):
    T, K = d.shape
    Fd = wd.shape[0]
    ni = T // tm

    def body(d_ref, w_ref, g_ref, u_ref, dgu_ref, dwd_ref, acc):
        i = pl.program_id(1)

        @pl.when(i == 0)
        def _():
            acc[...] = jnp.zeros_like(acc)

        db = d_ref[...].astype(BF16)
        da = 0.5 * _dot_nt(db, w_ref[...])
        gv = g_ref[...].astype(F32)
        uv = u_ref[...].astype(F32)
        s = _sigmoid(gv)
        silu = gv * s
        acc[...] += _dot_tn((silu * uv).astype(BF16), db)
        dgu_ref[0] = (da * uv * (s * (1.0 + gv * (1.0 - s)))).astype(BF16)
        dgu_ref[1] = (da * silu).astype(BF16)

        @pl.when(i == ni - 1)
        def _():
            dwd_ref[...] = (0.5 * acc[...]).astype(BF16)

    in_specs = [pl.BlockSpec((tm, K), lambda j, i: (i, 0)),
                pl.BlockSpec((tn, K), lambda j, i: (j, 0)),
                pl.BlockSpec((None, tm, tn), lambda j, i: (0, i, j)),
                pl.BlockSpec((None, tm, tn), lambda j, i: (1, i, j))]
    out_specs = [pl.BlockSpec((2, tm, tn), lambda j, i: (0, i, j)), pl.BlockSpec((tn, K), lambda j, i: (j, 0))]
    out_shape = [jax.ShapeDtypeStruct((2, T, Fd), BF16), jax.ShapeDtypeStruct((Fd, K), BF16)]
    return _call_with_riders(body, riding, (Fd // tn, ni), in_specs, out_specs, out_shape,
                             [pltpu.VMEM((tn, K), F32)], [d, wd, gu, gu], name)


def _call_with_riders(body, riding, grid, in_specs, out_specs, out_shape, scratch, args, name):
    n_out = len(out_shape)
    if riding is None:
        return pl.pallas_call(body, grid=grid, in_specs=in_specs, out_specs=out_specs, out_shape=out_shape,
                              scratch_shapes=scratch, compiler_params=_cparams(len(grid)), name=name)(*args)

    def is_first():
        ok = pl.program_id(0) == 0
        for ax in range(1, len(grid)):
            ok = ok & (pl.program_id(ax) == 0)
        return ok

    def is_last():
        ok = pl.program_id(0) == grid[0] - 1
        for ax in range(1, len(grid)):
            ok = ok & (pl.program_id(ax) == grid[ax] - 1)
        return ok

    outs = pl.pallas_call(
        riding.wrap(body, len(in_specs), n_out, len(scratch), is_first, is_last), grid=grid,
        in_specs=list(in_specs) + riding.in_specs, out_specs=list(out_specs) + riding.out_specs,
        out_shape=list(out_shape) + riding.out_shapes, scratch_shapes=list(scratch) + riding.scratch,
        compiler_params=_cparams(len(grid)), name=name)(*args, *riding.args)
    core, per = riding.split(outs, n_out)
    return (*core, *per)


def mm_nt_normbwd(a, b, x, g, res, *, a_split, tm, tk, name, riding=None):
    T, Dm = x.shape
    blocked = b.ndim == 3
    assert not blocked or b.shape[2] == tk
    K = b.shape[0] * b.shape[2] if blocked else b.shape[1]
    nk = K // tk
    nkh = nk // 2
    has_res = res is not None

    def body(*refs):
        if has_res:
            a_ref, b_ref, x_ref, g_ref, r_ref, dx_ref, dg_ref, acc = refs
        else:
            a_ref, b_ref, x_ref, g_ref, dx_ref, dg_ref, acc = refs
        i = pl.program_id(0)
        k = pl.program_id(1)

        @pl.when(k == 0)
        def _():
            acc[...] = jnp.zeros_like(acc)

        acc[...] += _dot_nt(a_ref[...].astype(BF16), b_ref[...])

        @pl.when(k == nk - 1)
        def _():
            xv = x_ref[...]
            r = lax.rsqrt(jnp.mean(xv * xv, axis=-1, keepdims=True) + RMS_EPS)
            xh = xv * r
            dh = acc[...]
            dxh = dh * g_ref[...]
            dx = r * (dxh - xh * jnp.mean(dxh * xh, axis=-1, keepdims=True))
            if has_res:
                dx = dx + r_ref[...]
            dx_ref[...] = dx
            part = jnp.sum(dh * xh, axis=0, keepdims=True)

            @pl.when(i == 0)
            def _():
                dg_ref[...] = part

            @pl.when(i > 0)
            def _():
                dg_ref[...] += part

    if a_split:
        a_spec = pl.BlockSpec((None, tm, tk), lambda i, k: (k // nkh, i, k % nkh))
    else:
        a_spec = pl.BlockSpec((tm, tk), lambda i, k: (i, k))
    in_specs = [a_spec,
                (pl.BlockSpec((None, Dm, tk), lambda i, k: (k, 0, 0)) if blocked
                 else pl.BlockSpec((Dm, tk), lambda i, k: (0, k))),
                pl.BlockSpec((tm, Dm), lambda i, k: (i, 0)),
                pl.BlockSpec((1, Dm), lambda i, k: (0, 0))]
    args = [a, b, x, g]
    if has_res:
        in_specs.append(pl.BlockSpec((tm, Dm), lambda i, k: (i, 0)))
        args.append(res)
    out_specs = [pl.BlockSpec((tm, Dm), lambda i, k: (i, 0)), pl.BlockSpec((1, Dm), lambda i, k: (0, 0))]
    out_shape = [jax.ShapeDtypeStruct((T, Dm), F32), jax.ShapeDtypeStruct((1, Dm), F32)]
    scratch = [pltpu.VMEM((tm, Dm), F32)]
    return _call_with_riders(body, riding, (T // tm, nk), in_specs, out_specs, out_shape, scratch, args, name)


def mm_tn(a, b, *, scale, a_split, b_split, tm, tn, tk, name, out_blocked=False, riding=None):
    T = a.shape[-2]
    M = a.shape[-1] * (2 if a_split else 1)
    N = b.shape[-1] * (2 if b_split else 1)
    ni, nj, nk = M // tm, N // tn, T // tk
    nih, njh = ni // 2, nj // 2

    def body(a_ref, b_ref, o_ref, acc):
        k = pl.program_id(2)

        @pl.when(k == 0)
        def _():
            acc[...] = jnp.zeros_like(acc)

        acc[...] += _dot_tn(a_ref[...].astype(BF16), b_ref[...].astype(BF16))

        @pl.when(k == nk - 1)
        def _():
            o_ref[...] = (acc[...] * scale).astype(o_ref.dtype)

    if a_split:
        a_spec = pl.BlockSpec((None, tk, tm), lambda i, j, k: (i // nih, k, i % nih))
    else:
        a_spec = pl.BlockSpec((tk, tm), lambda i, j, k: (k, i))
    if b_split:
        b_spec = pl.BlockSpec((None, tk, tn), lambda i, j, k: (j // njh, k, j % njh))
    else:
        b_spec = pl.BlockSpec((tk, tn), lambda i, j, k: (k, j))
    if out_blocked:
        o_spec = pl.BlockSpec((None, None, tm, tn), lambda i, j, k: (j, i, 0, 0))
        o_shape = jax.ShapeDtypeStruct((nj, ni, tm, tn), BF16)
    else:
        o_spec = pl.BlockSpec((tm, tn), lambda i, j, k: (i, j))
        o_shape = jax.ShapeDtypeStruct((M, N), BF16)
    outs = _call_with_riders(body, riding, (ni, nj, nk), [a_spec, b_spec], [o_spec], [o_shape],
                             [pltpu.VMEM((tm, tn), F32)], [a, b], name)
    return outs[0] if riding is None else tuple(outs)


def loss_kernel(y, target, *, tm, name):
    T, Dm = y.shape

    def body(y_ref, t_ref, l_ref, dy_ref):
        e = y_ref[...] - t_ref[...]
        dy_ref[...] = e * (1.0 / Dm)
        part = (0.5 / Dm) * jnp.sum(jnp.sum(e * e, axis=-1, keepdims=True), axis=0, keepdims=True)
        part = jnp.broadcast_to(part, (8, LANES))

        @pl.when(pl.program_id(0) == 0)
        def _():
            l_ref[...] = part

        @pl.when(pl.program_id(0) > 0)
        def _():
            l_ref[...] += part

    return pl.pallas_call(
        body, grid=(T // tm,),
        in_specs=[pl.BlockSpec((tm, Dm), lambda i: (i, 0)), pl.BlockSpec((tm, Dm), lambda i: (i, 0))],
        out_specs=[pl.BlockSpec((8, LANES), lambda i: (0, 0)), pl.BlockSpec((tm, Dm), lambda i: (i, 0))],
        out_shape=[jax.ShapeDtypeStruct((8, LANES), F32), jax.ShapeDtypeStruct((T, Dm), F32)],
        compiler_params=_cparams(1), name=name)(y, target)


def adamw(w, g, m, v, *, br, name):
    R, C = w.shape

    def body(w_ref, g_ref, m_ref, v_ref, d_ref, nm_ref, nv_ref):
        gv = g_ref[...]
        nm = ADAM_B1 * m_ref[...] + (1.0 - ADAM_B1) * gv
        nv = ADAM_B2 * v_ref[...] + (1.0 - ADAM_B2) * (gv * gv)
        m_hat = nm / (1.0 - ADAM_B1 ** ADAM_STEP)
        v_hat = nv / (1.0 - ADAM_B2 ** ADAM_STEP)
        d_ref[...] = -ADAM_LR * (m_hat / (jnp.sqrt(v_hat) + ADAM_EPS) + ADAM_WD * w_ref[...])
        nm_ref[...] = nm
        nv_ref[...] = nv

    spec = pl.BlockSpec((br, C), lambda i: (i, 0))
    shp = jax.ShapeDtypeStruct((R, C), F32)
    return pl.pallas_call(
        body, grid=(R // br,), in_specs=[spec] * 4, out_specs=[spec] * 3, out_shape=[shp] * 3,
        compiler_params=_cparams(1), name=name)(w, g, m, v)


def _lane0():
    return lax.broadcasted_iota(jnp.int32, (1, LANES), 1) < HEAD_DIM


def _half_sum(x, m0):
    s0 = jnp.sum(jnp.where(m0, x, 0.0), axis=-1, keepdims=True)
    s1 = jnp.sum(jnp.where(m0, 0.0, x), axis=-1, keepdims=True)
    return jnp.where(m0, s0, s1)


def _head_rms(x, m0):
    return lax.rsqrt(_half_sum(x * x, m0) * (1.0 / HEAD_DIM) + RMS_EPS)


def _alibi(n):
    return [float(2.0 ** (-8.0 * (h + 1) / n)) for h in range(n)]


def _mask_half(x, m0, e):
    return jnp.where(m0, x, 0.0) if e == 0 else jnp.where(m0, 0.0, x)


def _band_masks2(max_dist, has_prev, live):
    row = lax.broadcasted_iota(jnp.int32, (2 * BLK, 2 * BLK), 0)
    col = lax.broadcasted_iota(jnp.int32, (2 * BLK, 2 * BLK), 1)
    dist = (row & (BLK - 1)) - col + BLK
    lim = jnp.where(live, max_dist, -1)
    first = jnp.where(has_prev, 0, BLK)
    valid = (dist >= 0) & (dist <= lim) & (col >= first)
    top = lax.broadcasted_iota(jnp.int32, (2 * BLK, 1), 0) < BLK
    return dist.astype(F32), valid, top


def _stack_heads(x, m0, kes):
    parts = []
    for e in range(2):
        h = _mask_half(x, m0, e)
        parts.append(pltpu.roll(h, HEAD_DIM, 1) if kes[e] != e else h)
    return jnp.concatenate(parts, axis=0)


def _unstack_heads(y, m0, kes):
    parts = []
    for e in range(2):
        h = y[e * BLK:(e + 1) * BLK]
        parts.append(pltpu.roll(h, HEAD_DIM, 1) if kes[e] != e else h)
    return jnp.where(m0, parts[0], parts[1])


def _rows(r, dil):
    return pl.ds(r, BLK, stride=dil) if dil > 1 else pl.ds(0, BLK)


def _band_units(dil, nsub):
    assert dil == 1 or nsub == 1
    if nsub == 1:
        return [(_rows(r, dil), ("prev", _rows(r, dil)), 0) for r in range(dil)]
    units = [(pl.ds(0, BLK), ("prev", pl.ds(0, BLK)), 0)]
    units += [(pl.ds(BLK * s, BLK), ("cur", pl.ds(BLK * (s - 1), BLK)), s) for s in range(1, nsub)]
    return units


def _head_col_spec(ppk, RB, row_block):
    if ppk == 1:
        return pl.BlockSpec((None, RB, 1), lambda p, i: (p, row_block(i), 0))
    return pl.BlockSpec((ppk, RB, 1), lambda p, i: (p, row_block(i), 0))


def _band_specs(dil, nsub, ppk, q_blk, k_blk, v_blk, kv_shared, nb):
    RB = BLK * dil * nsub
    PB = BLK if nsub > 1 else RB
    qw = LANES * ppk
    kw = LANES if kv_shared else qw

    def cur(i):
        return jnp.minimum(i, nb - 1)

    def prev(i):
        return jnp.maximum(i * nsub - 1, 0) if nsub > 1 else jnp.maximum(i - 1, 0)

    def kidx(base):
        return (lambda p, i: (cur(i), base)) if kv_shared else (lambda p, i: (cur(i), base + p))

    def pidx(base):
        return (lambda p, i: (prev(i), base)) if kv_shared else (lambda p, i: (prev(i), base + p))

    specs = [pl.BlockSpec((RB, qw), lambda p, i: (cur(i), q_blk + p)),
             pl.BlockSpec((RB, kw), kidx(k_blk)), pl.BlockSpec((PB, kw), pidx(k_blk)),
             pl.BlockSpec((RB, kw), kidx(v_blk)), pl.BlockSpec((PB, kw), pidx(v_blk))]
    return specs if dil == 1 else [specs[0], specs[1], specs[3]]


def qk_norm(qkv, q_gain2, k_gain2, *, width, steps, n_q, tm, name):
    T = qkv.shape[0]
    nsb = width // LANES

    def body(x_ref, qg_ref, kg_ref, o_ref):
        m0 = _lane0()
        for b in range(nsb):
            is_q = ((pl.program_id(1) * nsb + b) < n_q).astype(F32)
            gain = qg_ref[...] * is_q + kg_ref[...] * (1.0 - is_q)
            cols = pl.ds(LANES * b, LANES)
            xv = x_ref[:, cols]
            o_ref[:, cols] = xv * _head_rms(xv, m0) * gain

    gspec = pl.BlockSpec((1, LANES), lambda i, j: (0, 0))
    return pl.pallas_call(
        body, grid=(T // tm, steps),
        in_specs=[pl.BlockSpec((tm, width), lambda i, j: (i, j)), gspec, gspec],
        out_specs=pl.BlockSpec((tm, width), lambda i, j: (i, j)),
        out_shape=jax.ShapeDtypeStruct((T, width * steps), F32),
        compiler_params=_cparams(2), name=name)(qkv, q_gain2, k_gain2)


def banded_fwd(qkn, qkv, slopes, sinks, *, dil, nsub, ppk, q_blk, k_blk, v_blk, n_heads, group,
               max_dist, name):
    T = qkv.shape[0]
    RB = BLK * dil * nsub
    nb = T // RB
    npair = n_heads // 2
    kv_shared = group > 1
    scale = HEAD_DIM ** -0.5
    has_sink = sinks is not None

    def body(*refs):
        slope_ref = refs[0]
        if has_sink:
            sink_ref, refs = refs[1], refs[2:]
        else:
            refs = refs[1:]
        if dil == 1:
            q_ref, kc_ref, kp_ref, vc_ref, vp_ref, o_ref, l_ref, lc0_ref, lc1_ref = refs
        else:
            q_ref, kc_ref, vc_ref, o_ref, l_ref, lc0_ref, lc1_ref, kp_ref, vp_ref = refs
        pb = pl.program_id(0)
        i = pl.program_id(1)
        if dil > 1:
            @pl.when(i == 0)
            def _():
                kp_ref[...] = jnp.zeros_like(kp_ref)
                vp_ref[...] = jnp.zeros_like(vp_ref)
        m0 = _lane0()
        distf, valid_first, top = _band_masks2(max_dist, i > 0, i >= 0)
        valid_inner = _band_masks2(max_dist, i >= 0, i >= 0)[1] if nsub > 1 else None
        for u, (rows, (src, prows), sub) in enumerate(_band_units(dil, nsub)):
            valid = valid_first if sub == 0 else valid_inner
            kpr, vpr = (kp_ref, vp_ref) if src == "prev" else (kc_ref, vc_ref)
            kcache = {}
            for jp in range(ppk):
                cs = pl.ds(LANES * jp, LANES)
                jk = 0 if kv_shared else jp
                if jk not in kcache:
                    ks = pl.ds(LANES * jk, LANES)
                    kcur, vcur = kc_ref[rows, ks], vc_ref[rows, ks]
                    if dil == 1:
                        kprev, vprev = kpr[prows, ks], vpr[prows, ks]
                    else:
                        kprev, vprev = kp_ref[u, :, ks], vp_ref[u, :, ks]
                        kp_ref[u, :, ks] = kcur
                        vp_ref[u, :, ks] = vcur
                    kcat = jnp.concatenate([kprev, kcur], axis=0)
                    vcat = jnp.concatenate([vprev, vcur], axis=0)
                    kcache[jk] = (kcat.astype(BF16), vcat.astype(BF16))
                kn, vcat = kcache[jk]
                qn = q_ref[rows, cs]
                kes = [((2 * jp + e) // group) % 2 if kv_shared else e for e in range(2)]
                hidx = 2 * (pb * ppk + jp)
                qs = _stack_heads(qn, m0, kes).astype(BF16)
                slope = jnp.where(top, slope_ref[hidx], slope_ref[hidx + 1])
                s = jnp.where(valid, _dot_nt(qs, kn) * scale - slope * distf, NEG)
                m = jnp.max(s, axis=-1, keepdims=True)
                if has_sink:
                    sk = jnp.where(top, sink_ref[hidx], sink_ref[hidx + 1])
                    m = jnp.maximum(m, sk)
                p = jnp.exp(s - m)
                den = jnp.sum(p, axis=-1, keepdims=True)
                if has_sink:
                    den = den + jnp.exp(sk - m)
                o_full = _dot((p * (1.0 / den)).astype(BF16), vcat)
                o_ref[rows, cs] = _unstack_heads(o_full, m0, kes)
                lse = m + jnp.log(den)
                l_ref[rows, cs] = _unstack_heads(jnp.broadcast_to(lse, (2 * BLK, LANES)), m0, [0, 1])
                for e, lc_ref in enumerate((lc0_ref, lc1_ref)):
                    if ppk == 1:
                        lc_ref[rows, :] = lse[e * BLK:(e + 1) * BLK]
                    else:
                        lc_ref[jp, rows, :] = lse[e * BLK:(e + 1) * BLK]

    smem = pl.BlockSpec(memory_space=pltpu.SMEM)
    qw = LANES * ppk
    ospec = pl.BlockSpec((RB, qw), lambda p, i: (i, p))
    oshape = jax.ShapeDtypeStruct((T, n_heads * HEAD_DIM), F32)
    args = [slopes] + ([sinks] if has_sink else []) + ([qkn] * 3 + [qkv] * 2 if dil == 1 else [qkn, qkn, qkv])
    kw = LANES if kv_shared else qw
    prev_scratch = [] if dil == 1 else [pltpu.VMEM((dil, BLK, kw), F32)] * 2
    return pl.pallas_call(
        body, grid=(npair // ppk, nb),
        in_specs=[smem] * (2 if has_sink else 1) + _band_specs(dil, nsub, ppk, q_blk, k_blk, v_blk, kv_shared, nb),
        out_specs=[ospec, ospec] + [_head_col_spec(ppk, RB, lambda i: i)] * 2,
        out_shape=[oshape, oshape] + [jax.ShapeDtypeStruct((npair, T, 1), F32)] * 2,
        scratch_shapes=prev_scratch, compiler_params=_cparams(2), name=name)(*args)


def banded_bwd(qkn, qkv, slopes, sinks, do, o, lsec, w, omix, *, dil, nsub, ppk, q_blk, k_blk, v_blk,
               n_heads, group, max_dist, do_blk, name):
    T = qkv.shape[0]
    RB = BLK * dil * nsub
    nb = T // RB
    npair = n_heads // 2
    kv_shared = group > 1
    scale = HEAD_DIM ** -0.5
    has_sink = sinks is not None
    mixed = w is not None
    qw = LANES * ppk

    def body(*refs):
        slope_ref = refs[0]
        if has_sink:
            sink_ref, refs = refs[1], refs[2:]
        else:
            refs = refs[1:]
        if dil == 1:
            q_ref, kc_ref, kp_ref, vc_ref, vp_ref, do_ref, lc0_ref, lc1_ref = refs[:8]
            refs = refs[8:]
        else:
            q_ref, kc_ref, vc_ref, do_ref, lc0_ref, lc1_ref = refs[:6]
            refs, kp_ref, vp_ref = refs[6:-2], refs[-2], refs[-1]
        if mixed:
            w_ref, om_ref, refs = refs[0], refs[1], refs[2:]
        else:
            o_ref, refs = refs[0], refs[1:]
        dq_ref, dk_ref, dv_ref, dsk_ref, ck_ref, cv_ref = refs
        pb = pl.program_id(0)
        i = pl.program_id(1)
        live = i < nb
        m0 = _lane0()
        lane = lax.broadcasted_iota(jnp.int32, (1, LANES), 1)
        distf, valid_first, top = _band_masks2(max_dist, i > 0, live)
        valid_inner = _band_masks2(max_dist, i >= 0, live)[1] if nsub > 1 else None
        livef = live.astype(F32)

        def half_rows(x):
            s0 = jnp.sum(jnp.where(m0, x, 0.0), axis=-1, keepdims=True)
            s1 = jnp.sum(jnp.where(m0, 0.0, x), axis=-1, keepdims=True)
            return jnp.concatenate([s0, s1], axis=0)

        @pl.when((pb == 0) & (i == 0))
        def _():
            dsk_ref[...] = jnp.zeros_like(dsk_ref)

        @pl.when(i == 0)
        def _():
            ck_ref[...] = jnp.zeros_like(ck_ref)
            cv_ref[...] = jnp.zeros_like(cv_ref)

        dsk_acc = jnp.zeros((1, LANES), F32)
        if nsub > 1:
            dk_ref[...] = ck_ref[...]
            dv_ref[...] = cv_ref[...]
        if dil > 1:
            @pl.when(i == 0)
            def _():
                kp_ref[...] = jnp.zeros_like(kp_ref)
                vp_ref[...] = jnp.zeros_like(vp_ref)

        for u, (rows, (src, prows), sub) in enumerate(_band_units(dil, nsub)):
            valid = valid_first if sub == 0 else valid_inner
            kpr, vpr = (kp_ref, vp_ref) if src == "prev" else (kc_ref, vc_ref)
            ck_u, cv_u = (ck_ref.at[u], cv_ref.at[u]) if dil > 1 else (None, None)
            for jp in range(ppk):
                cs = pl.ds(LANES * jp, LANES)
                ks = pl.ds(0, LANES) if kv_shared else cs
                kcur, vcur = kc_ref[rows, ks], vc_ref[rows, ks]
                if dil == 1:
                    kprev, vprev = kpr[prows, ks], vpr[prows, ks]
                else:
                    kprev, vprev = kp_ref[u, :, ks], vp_ref[u, :, ks]
                    kp_ref[u, :, ks] = kcur
                    vp_ref[u, :, ks] = vcur
                kn = jnp.concatenate([kprev, kcur], axis=0).astype(BF16)
                vcat = jnp.concatenate([vprev, vcur], axis=0).astype(BF16)
                dov = do_ref[rows, cs]
                if mixed:
                    dov = dov * w_ref[rows, cs]
                    shift = half_rows(dov * om_ref[rows, cs])
                else:
                    shift = half_rows(dov * o_ref[rows, cs])
                kes = [((2 * jp + e) // group) % 2 if kv_shared else e for e in range(2)]
                hidx = 2 * (pb * ppk + jp)
                qs = _stack_heads(q_ref[rows, cs], m0, kes).astype(BF16)
                dos = _stack_heads(dov, m0, kes).astype(BF16)
                lse = jnp.concatenate([ref[rows, :] if ppk == 1 else ref[jp, rows, :]
                                       for ref in (lc0_ref, lc1_ref)], axis=0)
                slope = jnp.where(top, slope_ref[hidx], slope_ref[hidx + 1])
                p = jnp.where(valid, jnp.exp(_dot_nt(qs, kn) * scale - slope * distf - lse), 0.0)
                ds = (p * (_dot_nt(dos, vcat) - shift)).astype(BF16)
                dqn = _unstack_heads(_dot(ds, kn), m0, kes) * scale
                dkn = _dot_tn(ds, qs) * scale
                dvv = _dot_tn(p.astype(BF16), dos)
                if has_sink:
                    sk = jnp.where(top, sink_ref[hidx], sink_ref[hidx + 1])
                    contrib = -jnp.exp(sk - lse) * shift * livef
                    for e in range(2):
                        tot = jnp.sum(contrib[e * BLK:(e + 1) * BLK], axis=0, keepdims=True)
                        dsk_acc = dsk_acc + jnp.where(lane == (2 * jp + e), tot, 0.0)
                dk_raw = dkn

                @pl.when(live)
                def _():
                    dq_ref[rows, cs] = dqn

                if dil > 1:
                    dk_ref[rows, cs] = ck_u[:, cs] + dk_raw[:BLK]
                    dv_ref[rows, cs] = cv_u[:, cs] + dvv[:BLK]
                    ck_u[:, cs] = dk_raw[BLK:]
                    cv_u[:, cs] = dvv[BLK:]
                    continue
                if nsub == 1:
                    dk_ref[rows, cs] = ck_ref[rows, cs] + dk_raw[:BLK]
                    dv_ref[rows, cs] = cv_ref[rows, cs] + dvv[:BLK]
                elif sub == 0:
                    last = pl.ds(RB - BLK, BLK)
                    dk_ref[last, cs] += dk_raw[:BLK]
                    dv_ref[last, cs] += dvv[:BLK]
                else:
                    ck_ref[prows, cs] += dk_raw[:BLK]
                    cv_ref[prows, cs] += dvv[:BLK]
                ck_ref[rows, cs] = dk_raw[BLK:]
                cv_ref[rows, cs] = dvv[BLK:]
        dsk_ref[...] += dsk_acc

    smem = pl.BlockSpec(memory_space=pltpu.SMEM)
    gspec = pl.BlockSpec((1, LANES), lambda p, i: (0, 0))

    def cur(i):
        return jnp.minimum(i, nb - 1)

    qspec = pl.BlockSpec((RB, qw), lambda p, i: (cur(i), p))
    dospec = pl.BlockSpec((RB, qw), lambda p, i: (cur(i), do_blk + p))
    kvout = pl.BlockSpec((RB, qw), lambda p, i: (jnp.maximum(i - 1, 0), p))
    in_specs = ([smem] * (2 if has_sink else 1) + _band_specs(dil, nsub, ppk, q_blk, k_blk, v_blk, kv_shared, nb)
                + [dospec] + [_head_col_spec(ppk, RB, cur)] * 2
                + ([qspec, qspec] if mixed else [qspec]))
    args = ([slopes] + ([sinks] if has_sink else []) + ([qkn] * 3 + [qkv] * 2 if dil == 1 else [qkn, qkn, qkv])
            + [do, lsec[0], lsec[1]]
            + ([w, omix] if mixed else [o]))
    full = jax.ShapeDtypeStruct((T, n_heads * HEAD_DIM), F32)
    row = jax.ShapeDtypeStruct((1, LANES), F32)
    return pl.pallas_call(
        body, grid=(npair // ppk, nb + 1), in_specs=in_specs,
        out_specs=[qspec, kvout, kvout, gspec],
        out_shape=[full, full, full, row],
        scratch_shapes=([pltpu.VMEM((RB, qw), F32)] * 2 if dil == 1
                        else [pltpu.VMEM((dil, BLK, qw), F32)] * 2 + [pltpu.VMEM((dil, BLK, qw), F32)] * 2),
        compiler_params=_cparams(2), name=name)(*args)


def mix_fwd(o1, o2, o3, l1, l2, l3, *, tm, name):
    T, C = o1.shape

    def body(o1r, o2r, o3r, l1r, l2r, l3r, o_ref, w1r, w2r, w3r):
        a, b, c = l1r[...], l2r[...], l3r[...]
        m = jnp.maximum(jnp.maximum(a, b), c)
        ea, eb, ec = jnp.exp(a - m), jnp.exp(b - m), jnp.exp(c - m)
        inv = 1.0 / (ea + eb + ec)
        wa, wb, wc = ea * inv, eb * inv, ec * inv
        o_ref[...] = wa * o1r[...] + wb * o2r[...] + wc * o3r[...]
        w1r[...] = wa
        w2r[...] = wb
        w3r[...] = wc

    spec = pl.BlockSpec((tm, C), lambda i: (i, 0))
    shp = jax.ShapeDtypeStruct((T, C), F32)
    return pl.pallas_call(body, grid=(T // tm,), in_specs=[spec] * 6, out_specs=[spec] * 4, out_shape=[shp] * 4,
                          compiler_params=_cparams(1), name=name)(o1, o2, o3, l1, l2, l3)


def _qk_norm_bwd(raw, dn, gain, m0):
    r = _head_rms(raw, m0)
    h = raw * r
    dh = dn * gain
    d_raw = r * (dh - h * (_half_sum(dh * h, m0) * (1.0 / HEAD_DIM)))
    return d_raw, jnp.sum(dn * h, axis=0, keepdims=True)


def _acc_rows(ref, val):
    @pl.when(pl.program_id(0) == 0)
    def _():
        ref[...] = val

    @pl.when(pl.program_id(0) > 0)
    def _():
        ref[...] += val


def assemble_odd(parts, qkv, q_gain2, k_gain2, *, tm, name):
    T, C = parts[0][0].shape
    nbk = C // LANES

    def body(*refs):
        qkv_ref, qg_ref, kg_ref, o_ref, dqg_ref, dkg_ref = refs[9:]
        m0 = _lane0()
        sums = [refs[j][...] + refs[3 + j][...] + refs[6 + j][...] for j in range(3)]
        o_ref[:, pl.ds(2 * C, C)] = sums[2]
        for j, (g_ref, acc_ref) in enumerate(((qg_ref, dqg_ref), (kg_ref, dkg_ref))):
            dgain = jnp.zeros((1, LANES), F32)
            for b in range(nbk):
                cols = pl.ds(C * j + LANES * b, LANES)
                d_raw, part = _qk_norm_bwd(qkv_ref[:, cols], sums[j][:, LANES * b:LANES * (b + 1)], g_ref[...], m0)
                o_ref[:, cols] = d_raw
                dgain = dgain + part
            _acc_rows(acc_ref, dgain)

    spec = pl.BlockSpec((tm, C), lambda i: (i, 0))
    gspec = pl.BlockSpec((1, LANES), lambda i: (0, 0))
    flat = [parts[p][j] for p in range(3) for j in range(3)]
    row = jax.ShapeDtypeStruct((1, LANES), F32)
    return pl.pallas_call(body, grid=(T // tm,),
                          in_specs=[spec] * 9 + [pl.BlockSpec((tm, 2 * C), lambda i: (i, 0)), gspec, gspec],
                          out_specs=[pl.BlockSpec((tm, 3 * C), lambda i: (i, 0)), gspec, gspec],
                          out_shape=[jax.ShapeDtypeStruct((T, 3 * C), F32), row, row],
                          compiler_params=_cparams(1), name=name)(*flat, qkv, q_gain2, k_gain2)


def assemble_even(dqa, dka4, dva4, dqb, dkb, dvb, qkv, q_gain2, k_gain2, *, tm, name):
    T = dqa.shape[0]
    W = 512
    QK = 768

    def body(dqa_r, dka_r, dva_r, dqb_r, dkb_r, dvb_r, qkv_ref, qg_ref, kg_ref, o_ref, dqg_ref, dkg_ref):
        m0 = _lane0()
        ka = dka_r[...]
        va = dva_r[...]
        dqn = dqa_r[...]
        dgain = jnp.zeros((1, LANES), F32)
        for b in range(W // LANES):
            cols = pl.ds(LANES * b, LANES)
            d_raw, part = _qk_norm_bwd(qkv_ref[:, cols], dqn[:, LANES * b:LANES * (b + 1)], qg_ref[...], m0)
            o_ref[:, cols] = d_raw
            dgain = dgain + part
        _acc_rows(dqg_ref, dgain)
        dkn = ka[:, 0:128] + ka[:, 128:256] + ka[:, 256:384] + ka[:, 384:512]
        d_raw, part = _qk_norm_bwd(qkv_ref[:, pl.ds(W, LANES)], dkn, kg_ref[...], m0)
        o_ref[:, pl.ds(W, LANES)] = d_raw
        _acc_rows(dkg_ref, part)
        o_ref[:, pl.ds(640, LANES)] = va[:, 0:128] + va[:, 128:256] + va[:, 256:384] + va[:, 384:512]
        o_ref[:, pl.ds(768, W)] = dqb_r[...]
        o_ref[:, pl.ds(1280, W)] = dkb_r[...]
        o_ref[:, pl.ds(1792, W)] = dvb_r[...]

    spec = pl.BlockSpec((tm, W), lambda i: (i, 0))
    gspec = pl.BlockSpec((1, LANES), lambda i: (0, 0))
    row = jax.ShapeDtypeStruct((1, LANES), F32)
    return pl.pallas_call(body, grid=(T // tm,),
                          in_specs=[spec] * 6 + [pl.BlockSpec((tm, QK), lambda i: (i, 0)), gspec, gspec],
                          out_specs=[pl.BlockSpec((tm, 2304), lambda i: (i, 0)), gspec, gspec],
                          out_shape=[jax.ShapeDtypeStruct((T, 2304), F32), row, row],
                          compiler_params=_cparams(1), name=name)(dqa, dka4, dva4, dqb, dkb, dvb, qkv, q_gain2, k_gain2)


STICK_T = 256
STICK_DEAD = -110.0


def _split_bf16(x):
    hi = x.astype(BF16)
    lo = (x - hi.astype(F32)).astype(BF16)
    return hi, lo


def _stick_logits(qm, kt, scale, diag):
    n = STICK_T
    row = lax.broadcasted_iota(jnp.int32, (n, n), 0)
    col = lax.broadcasted_iota(jnp.int32, (n, n), 1)
    mask = col < row + jnp.where(diag, 0, n)
    z = _dot_nt(qm, kt) * scale
    lneg = -(jnp.maximum(z, 0.0) + jnp.log(1.0 + jnp.exp(-jnp.abs(z))))
    lpos = z + lneg
    lk = jnp.where(mask, lneg, 0.0)
    return mask, lpos, lneg, lk


def _cumsum_mm(x, tri):
    hi, lo = _split_bf16(x)
    return _dot(hi, tri) + _dot(lo, tri)


def stick_fwd(qkv, *, q_blk, k_blk, v_blk, n_pairs, name, riders=None, rider_args=()):
    T = qkv.shape[0]
    n = STICK_T
    nq = T // n
    scale = HEAD_DIM ** -0.5
    nc = riders.n if riders is not None else 0
    n_steps = n_pairs * nq
    stage_at = (0, (3 * n_steps) // 4, n_steps - 1, n_steps - 1)

    def body(*refs):
        q_ref, k_ref, v_ref = refs[:3]
        x_refs, o_ref = refs[3:3 + nc], refs[3 + nc]
        out_refs, sems = refs[4 + nc:4 + 2 * nc], refs[4 + 2 * nc:]
        i = pl.program_id(1)
        step_id = pl.program_id(0) * nq + i

        def ride(which):
            if riders is not None:
                @pl.when(step_id == stage_at[which])
                def _():
                    riders.stage(which, x_refs, out_refs, sems)

        ride(0)
        ride(1)
        m0 = _lane0()
        r2 = lax.broadcasted_iota(jnp.int32, (n, n), 0)
        c2 = lax.broadcasted_iota(jnp.int32, (n, n), 1)
        tri_after = (r2 > c2).astype(BF16)
        qv = q_ref[...]
        out = jnp.zeros((n, LANES), F32)
        for e in range(2):
            qm = _mask_half(qv, m0, e).astype(BF16)

            def alive(st):
                t, _, carry = st
                return (t <= i) & (jnp.max(carry) > STICK_DEAD)

            def step(st, e=e, qm=qm):
                t, acc, carry = st
                start = pl.multiple_of((i - t) * n, n)
                kt = k_ref[pl.ds(start, n), :].astype(BF16)
                vt = _mask_half(v_ref[pl.ds(start, n), :], m0, e).astype(BF16)
                mask, lpos, _, lk = _stick_logits(qm, kt, scale, t == 0)
                after = _cumsum_mm(lk, tri_after) + carry
                a = jnp.where(mask, jnp.exp(lpos + after), 0.0)
                acc = acc + _dot(a.astype(BF16), vt)
                carry = carry + jnp.sum(lk, axis=-1, keepdims=True)
                return t + 1, acc, carry

            _, acc, _ = lax.while_loop(alive, step, (jnp.int32(0), jnp.zeros((n, LANES), F32),
                                                     jnp.zeros((n, 1), F32)))
            out = out + acc
        o_ref[...] = out
        ride(2)
        ride(3)

    outs = pl.pallas_call(
        body, grid=(n_pairs, nq),
        in_specs=[pl.BlockSpec((n, LANES), lambda p, i: (i, q_blk + p)),
                  pl.BlockSpec((T, LANES), lambda p, i: (0, k_blk + p)),
                  pl.BlockSpec((T, LANES), lambda p, i: (0, v_blk + p))] + [_ANY] * nc,
        out_specs=[pl.BlockSpec((n, LANES), lambda p, i: (i, p))] + [_ANY] * nc,
        out_shape=[jax.ShapeDtypeStruct((T, n_pairs * LANES), F32)] + (riders.shapes if nc else []),
        scratch_shapes=riders.sems if nc else [],
        compiler_params=_cparams(2), name=name)(qkv, qkv, qkv, *rider_args)
    return outs[0], list(outs[1:])


def stick_bwd(qkv, do, *, q_blk, k_blk, v_blk, do_blk, n_pairs, name, riders=None, rider_args=()):
    T = qkv.shape[0]
    n = STICK_T
    nq = T // n
    scale = HEAD_DIM ** -0.5
    nc = riders.n if riders is not None else 0

    def body(*refs):
        q_ref, k_ref, v_ref, do_ref = refs[:4]
        pre_refs = refs[4:4 + nc]
        dq_ref, dk_ref, dv_ref = refs[4 + nc:7 + nc]
        land_refs = refs[7 + nc:7 + 2 * nc]
        a_keep, g_keep, s_keep = refs[7 + 2 * nc:10 + 2 * nc]
        sems = refs[10 + 2 * nc:]
        i = pl.program_id(1)
        first_step = (pl.program_id(0) == 0) & (i == 0)
        last_step = (pl.program_id(0) == n_pairs - 1) & (i == nq - 1)
        m0 = _lane0()
        r2 = lax.broadcasted_iota(jnp.int32, (n, n), 0)
        c2 = lax.broadcasted_iota(jnp.int32, (n, n), 1)
        tri_after = (r2 > c2).astype(BF16)
        tri_from = (r2 >= c2).astype(BF16)

        if riders is not None:
            @pl.when(first_step)
            def _():
                riders.start(pre_refs, land_refs, sems)

        @pl.when(i == 0)
        def _():
            dk_ref[...] = jnp.zeros_like(dk_ref)
            dv_ref[...] = jnp.zeros_like(dv_ref)

        qv = q_ref[...]
        dov = do_ref[...]
        dq_out = jnp.zeros((n, LANES), F32)
        for e in range(2):
            qm = _mask_half(qv, m0, e).astype(BF16)
            dom = _mask_half(dov, m0, e).astype(BF16)

            def alive(st):
                t, carry, _ = st
                return (t <= i) & (jnp.max(carry) > STICK_DEAD)

            def scan(st, qm=qm, dom=dom):
                t, carry, gtot = st
                start = pl.multiple_of((i - t) * n, n)
                kt = k_ref[pl.ds(start, n), :].astype(BF16)
                vt = v_ref[pl.ds(start, n), :].astype(BF16)
                mask, lpos, lneg, lk = _stick_logits(qm, kt, scale, t == 0)
                a = jnp.where(mask, jnp.exp(lpos + _cumsum_mm(lk, tri_after) + carry), 0.0)
                g = _dot_nt(dom, vt) * a
                a_keep[t] = a.astype(BF16)
                g_keep[t] = g
                s_keep[t] = jnp.exp(lneg).astype(BF16)
                return (t + 1, carry + jnp.sum(lk, axis=-1, keepdims=True),
                        gtot + jnp.sum(g, axis=-1, keepdims=True))

            z1 = jnp.zeros((n, 1), F32)
            n_live, _, gtot = lax.while_loop(alive, scan, (jnp.int32(0), z1, z1))

            def step(t, st, e=e, qm=qm, dom=dom, gtot=gtot):
                dq_acc, gright = st
                start = pl.multiple_of((i - t) * n, n)
                g = g_keep[t]
                sneg = s_keep[t].astype(F32)
                before = gtot - (_cumsum_mm(g, tri_from) + gright)
                mask = c2 < r2 + jnp.where(t == 0, 0, n)
                dz = jnp.where(mask, g * sneg - before * (1.0 - sneg), 0.0) * scale
                dzb = dz.astype(BF16)
                dq_acc = dq_acc + _dot(dzb, _mask_half(k_ref[pl.ds(start, n), :], m0, e).astype(BF16))
                dk_ref[pl.ds(start, n), :] += _dot_tn(dzb, qm)
                dv_ref[pl.ds(start, n), :] += _dot_tn(a_keep[t], dom)
                return dq_acc, gright + jnp.sum(g, axis=-1, keepdims=True)

            dq_acc, _ = lax.fori_loop(0, n_live, step, (jnp.zeros((n, LANES), F32), z1))
            dq_out = dq_out + dq_acc
        dq_ref[...] = dq_out

        if riders is not None:
            @pl.when(last_step)
            def _():
                riders.finish(pre_refs, land_refs, sems)

    tile = pl.BlockSpec((n, LANES), lambda p, i: (i, p))
    whole = pl.BlockSpec((T, LANES), lambda p, i: (0, p))
    shp = jax.ShapeDtypeStruct((T, n_pairs * LANES), F32)
    outs = pl.pallas_call(
        body, grid=(n_pairs, nq),
        in_specs=[pl.BlockSpec((n, LANES), lambda p, i: (i, q_blk + p)),
                  pl.BlockSpec((T, LANES), lambda p, i: (0, k_blk + p)),
                  pl.BlockSpec((T, LANES), lambda p, i: (0, v_blk + p)),
                  pl.BlockSpec((n, LANES), lambda p, i: (i, do_blk + p))] + [_ANY] * nc,
        out_specs=[tile, whole, whole] + [_ANY] * nc,
        out_shape=[shp, shp, shp] + (riders.shapes if nc else []),
        scratch_shapes=[pltpu.VMEM((nq, n, n), BF16), pltpu.VMEM((nq, n, n), F32), pltpu.VMEM((nq, n, n), BF16)]
        + (riders.sems if nc else []),
        compiler_params=_cparams(2), name=name)(qkv, qkv, qkv, do, *rider_args)
    return outs[0], outs[1], outs[2], list(outs[3:])


def _xnorm(x):
    r = lax.rsqrt(jnp.mean(x * x, axis=-1, keepdims=True) + RMS_EPS)
    return r, x * r


def xattn_fwd(qraw, kvraw, q_gain, k_gain, *, tm, name):
    T = qraw.shape[0]
    scale = X_HEAD_DIM ** -0.5
    W = X_HEADS * X_HEAD_DIM

    def body(q_ref, kv_ref, qg_ref, kg_ref, o_ref):
        for h in range(X_HEADS):
            cs = pl.ds(X_HEAD_DIM * h, X_HEAD_DIM)
            _, qh = _xnorm(q_ref[:, cs])
            _, kh = _xnorm(kv_ref[:, cs])
            qn = (qh * qg_ref[...]).astype(BF16)
            kn = (kh * kg_ref[...]).astype(BF16)
            v = kv_ref[:, pl.ds(W + X_HEAD_DIM * h, X_HEAD_DIM)].astype(BF16)
            s = _dot_nt(qn, kn) * scale
            m = jnp.max(s, axis=-1, keepdims=True)
            p = jnp.exp(s - m)
            p = p / jnp.sum(p, axis=-1, keepdims=True)
            o_ref[:, cs] = _dot(p.astype(BF16), v)

    gspec = pl.BlockSpec((1, X_HEAD_DIM), lambda i: (0, 0))
    return pl.pallas_call(
        body, grid=(T // tm,),
        in_specs=[pl.BlockSpec((tm, W), lambda i: (i, 0)), pl.BlockSpec((MEM_LEN, 2 * W), lambda i: (0, 0)),
                  gspec, gspec],
        out_specs=pl.BlockSpec((tm, W), lambda i: (i, 0)),
        out_shape=jax.ShapeDtypeStruct((T, W), F32),
        compiler_params=_cparams(1), name=name)(qraw, kvraw, q_gain, k_gain)


def xattn_bwd(qraw, kvraw, q_gain, k_gain, do, o, *, tm, name):
    T = qraw.shape[0]
    nt = T // tm
    scale = X_HEAD_DIM ** -0.5
    W = X_HEADS * X_HEAD_DIM

    def body(q_ref, kv_ref, qg_ref, kg_ref, do_ref, o_ref, dq_ref, dkv_ref, dqg_ref, dkg_ref, dkn_ref):
        i = pl.program_id(0)

        @pl.when(i == 0)
        def _():
            dkv_ref[...] = jnp.zeros_like(dkv_ref)
            dkn_ref[...] = jnp.zeros_like(dkn_ref)
            dqg_ref[...] = jnp.zeros_like(dqg_ref)
            dkg_ref[...] = jnp.zeros_like(dkg_ref)

        qg = qg_ref[...]
        kg = kg_ref[...]
        dqg_acc = jnp.zeros((1, X_HEAD_DIM), F32)
        for h in range(X_HEADS):
            cs = pl.ds(X_HEAD_DIM * h, X_HEAD_DIM)
            vs = pl.ds(W + X_HEAD_DIM * h, X_HEAD_DIM)
            rq, qh = _xnorm(q_ref[:, cs])
            _, kh = _xnorm(kv_ref[:, cs])
            qn = (qh * qg).astype(BF16)
            kn = (kh * kg).astype(BF16)
            v = kv_ref[:, vs].astype(BF16)
            s = _dot_nt(qn, kn) * scale
            m = jnp.max(s, axis=-1, keepdims=True)
            p = jnp.exp(s - m)
            p = p / jnp.sum(p, axis=-1, keepdims=True)
            dov = do_ref[:, cs]
            delta = jnp.sum(dov * o_ref[:, cs], axis=-1, keepdims=True)
            dob = dov.astype(BF16)
            ds = (p * (_dot_nt(dob, v) - delta)).astype(BF16)
            dqn = _dot(ds, kn) * scale
            dkn_ref[:, cs] += _dot_tn(ds, qn) * scale
            dkv_ref[:, vs] += _dot_tn(p.astype(BF16), dob)
            dqg_acc = dqg_acc + jnp.sum(dqn * qh, axis=0, keepdims=True)
            dqh = dqn * qg
            dq_ref[:, cs] = rq * (dqh - qh * jnp.mean(dqh * qh, axis=-1, keepdims=True))
        dqg_ref[...] += dqg_acc

        @pl.when(i == nt - 1)
        def _():
            dkg_acc = jnp.zeros((1, X_HEAD_DIM), F32)
            for h in range(X_HEADS):
                cs = pl.ds(X_HEAD_DIM * h, X_HEAD_DIM)
                rk, kh = _xnorm(kv_ref[:, cs])
                dkn = dkn_ref[:, cs]
                dkg_acc = dkg_acc + jnp.sum(dkn * kh, axis=0, keepdims=True)
                dkh = dkn * kg
                dkv_ref[:, cs] = rk * (dkh - kh * jnp.mean(dkh * kh, axis=-1, keepdims=True))
            dkg_ref[...] = dkg_acc

    gspec = pl.BlockSpec((1, X_HEAD_DIM), lambda i: (0, 0))
    tile = pl.BlockSpec((tm, W), lambda i: (i, 0))
    kvspec = pl.BlockSpec((MEM_LEN, 2 * W), lambda i: (0, 0))
    grow = jax.ShapeDtypeStruct((1, X_HEAD_DIM), F32)
    return pl.pallas_call(
        body, grid=(nt,), in_specs=[tile, kvspec, gspec, gspec, tile, tile],
        out_specs=[tile, kvspec, gspec, gspec],
        out_shape=[jax.ShapeDtypeStruct((T, W), F32), jax.ShapeDtypeStruct((MEM_LEN, 2 * W), F32), grow, grow],
        scratch_shapes=[pltpu.VMEM((MEM_LEN, W), F32)],
        compiler_params=_cparams(1), name=name)(qraw, kvraw, q_gain, k_gain, do, o)


_ANY = pl.BlockSpec(memory_space=pl.ANY)


def _my_pos():
    return lax.axis_index("x"), lax.axis_index("y"), lax.axis_index("c")


def _pieces(arrays, chunks):
    out = []
    for a, (arr, n) in enumerate(zip(arrays, chunks)):
        rc = arr.shape[-2] // n
        out += [(a, pl.ds(ch * rc, rc)) for ch in range(n)]
    return out


class GatherBlocks:
    N_STAGES = 4

    def __init__(self, blks, chunks):
        self.shapes = [jax.ShapeDtypeStruct((N_DEV,) + b.shape, b.dtype) for b in blks]
        self.n = len(blks)
        self.pieces = _pieces(blks, chunks)
        n_p = len(self.pieces)
        self.sems = [pltpu.SemaphoreType.DMA((7 * n_p,)), pltpu.SemaphoreType.DMA((7 * n_p,)),
                     pltpu.SemaphoreType.DMA((n_p,))]

    def stage(self, which, x_refs, out_refs, sems):
        send_sems, recv_sems, local_sems = sems
        pieces, n_p = self.pieces, len(self.pieces)
        x, y, c = _my_pos()
        me, sibling = (x, y, c), (x, y, 1 - c)
        chips = [(1 - x, y), (x, 1 - y), (1 - x, 1 - y)]
        xn, yn, dg = [(*chip, c) for chip in chips]
        ps = range(n_p)

        def slot(block, p):
            px, py, pc = block
            a, rows = pieces[p]
            return out_refs[a].at[4 * px + 2 * py + pc, rows]

        def own(p):
            a, rows = pieces[p]
            return x_refs[a].at[rows]

        def copy(k, p, block, to, from_input=False):
            return pltpu.make_async_remote_copy(
                src_ref=own(p) if from_input else slot(block, p), dst_ref=slot(block, p),
                send_sem=send_sems.at[k * n_p + p], recv_sem=recv_sems.at[k * n_p + p],
                device_id=to, device_id_type=MESH)

        mine = [pltpu.make_async_copy(own(p), slot(me, p), local_sems.at[p]) for p in ps]
        first = [copy(k, p, me, to, from_input=True) for p in ps for k, to in ((1, xn), (2, yn), (0, sibling))]
        on_x = [copy(3, p, xn, yn) for p in ps if p % 2 == 0] + [copy(4, p, xn, sibling) for p in ps]
        on_y = [copy(3, p, yn, xn) for p in ps if p % 2 == 1] + [copy(5, p, yn, sibling) for p in ps]
        on_d = [copy(6, p, dg, sibling) for p in ps]
        if which == 0:
            for cp in first + mine:
                cp.start()
        elif which == 1:
            for p in ps:
                copy(1, p, xn, me).wait_recv()
                if p % 2 == 0:
                    copy(3, p, xn, yn).start()
                copy(4, p, xn, sibling).start()
                copy(2, p, yn, me).wait_recv()
                if p % 2 == 1:
                    copy(3, p, yn, xn).start()
                copy(5, p, yn, sibling).start()
        elif which == 2:
            for p in ps:
                copy(3, p, dg, me).wait_recv()
                copy(6, p, dg, sibling).start()
        else:
            for p in ps:
                copy(0, p, sibling, me).wait_recv()
            for k, chip in zip((4, 5, 6), chips):
                for p in ps:
                    copy(k, p, (*chip, 1 - c), me).wait_recv()
            for cp in first + on_x + on_y + on_d:
                cp.wait_send()
            for cp in mine:
                cp.wait()


def gather_blocks(blks, chunks, *, name):
    gb = GatherBlocks(blks, chunks)
    n = gb.n

    def body(*refs):
        x_refs, out_refs, sems = refs[:n], refs[n:2 * n], refs[2 * n:]
        for which in range(gb.N_STAGES):
            gb.stage(which, x_refs, out_refs, sems)

    return pl.pallas_call(body, out_shape=gb.shapes, in_specs=[_ANY] * n, out_specs=[_ANY] * n,
                          scratch_shapes=gb.sems, name=name)(*blks)


def gather_small(small, *, name):
    S, C = small.shape

    def body(s_ref, out_ref, send_sems, recv_sems, local_sem):
        x, y, c = _my_pos()
        my_id = 4 * x + 2 * y + c

        def copy(k, slot):
            px, py, pc = x ^ ((k >> 2) & 1), y ^ ((k >> 1) & 1), c ^ (k & 1)
            dst = my_id if slot == "mine" else 4 * px + 2 * py + pc
            return pltpu.make_async_remote_copy(
                src_ref=s_ref, dst_ref=out_ref.at[dst], send_sem=send_sems.at[k - 1], recv_sem=recv_sems.at[k - 1],
                device_id=(px, py, pc), device_id_type=MESH)

        own = pltpu.make_async_copy(s_ref, out_ref.at[my_id], local_sem)
        own.start()
        sends = [copy(k, "mine") for k in range(1, N_DEV)]
        for cp in sends:
            cp.start()
        for k in range(1, N_DEV):
            copy(k, "theirs").wait_recv()
        for cp in sends:
            cp.wait_send()
        own.wait()

    dma7 = pltpu.SemaphoreType.DMA((7,))
    return pl.pallas_call(
        body, out_shape=jax.ShapeDtypeStruct((N_DEV, S, C), small.dtype), in_specs=[_ANY], out_specs=_ANY,
        scratch_shapes=[dma7, dma7, pltpu.SemaphoreType.DMA], name=name)(small)


class PairExchange:
    def __init__(self, bigs, chunks):
        self.shapes = [jax.ShapeDtypeStruct((4,) + b.shape[1:], b.dtype) for b in bigs]
        self.n = len(bigs)
        self.pieces = _pieces(bigs, chunks)
        n_p = len(self.pieces)
        self.sems = [pltpu.SemaphoreType.DMA((4 * n_p,)), pltpu.SemaphoreType.DMA((4 * n_p,))]

    def _copies(self, big_refs, out_refs, sems):
        send_sems, recv_sems = sems
        n_p = len(self.pieces)
        x, y, c = _my_pos()

        def copy(b, p):
            a, rows = self.pieces[p]
            return pltpu.make_async_remote_copy(
                src_ref=big_refs[a].at[2 * b + (1 - c), rows], dst_ref=out_refs[a].at[b, rows],
                send_sem=send_sems.at[b * n_p + p], recv_sem=recv_sems.at[b * n_p + p],
                device_id=(x, y, 1 - c), device_id_type=MESH)

        return [copy(b, p) for b in range(4) for p in range(n_p)]

    def start(self, big_refs, out_refs, sems):
        for cp in self._copies(big_refs, out_refs, sems):
            cp.start()

    def finish(self, big_refs, out_refs, sems):
        cps = self._copies(big_refs, out_refs, sems)
        for cp in cps:
            cp.wait_recv()
        for cp in cps:
            cp.wait_send()


def _standalone(exchange, args, name):
    n = exchange.n

    def body(*refs):
        exchange.start(refs[:n], refs[n:2 * n], refs[2 * n:])
        exchange.finish(refs[:n], refs[n:2 * n], refs[2 * n:])

    return pl.pallas_call(body, out_shape=exchange.shapes, in_specs=[_ANY] * n, out_specs=[_ANY] * n,
                          scratch_shapes=exchange.sems, name=name)(*args)


class Riding:
    def __init__(self, riders):
        self.riders = [(ex, list(args)) for ex, args in riders]
        self.args = [a for _, args in self.riders for a in args]
        self.in_specs = [_ANY] * len(self.args)
        self.out_shapes = [s for ex, _ in self.riders for s in ex.shapes]
        self.out_specs = [_ANY] * len(self.out_shapes)
        self.scratch = [s for ex, _ in self.riders for s in ex.sems]

    def wrap(self, body, n_in, n_out, n_scratch, is_first, is_last):
        def wrapped(*refs):
            k = 0
            core = list(refs[:n_in])
            k = n_in
            r_in = []
            for ex, _ in self.riders:
                r_in.append(refs[k:k + ex.n])
                k += ex.n
            core += refs[k:k + n_out]
            k += n_out
            r_out = []
            for ex, _ in self.riders:
                r_out.append(refs[k:k + ex.n])
                k += ex.n
            core += refs[k:k + n_scratch]
            k += n_scratch
            r_sem = []
            for ex, _ in self.riders:
                r_sem.append(refs[k:k + len(ex.sems)])
                k += len(ex.sems)

            @pl.when(is_first())
            def _():
                for (ex, _), a, b, s in zip(self.riders, r_in, r_out, r_sem):
                    ex.start(a, b, s)

            body(*core)

            @pl.when(is_last())
            def _():
                for (ex, _), a, b, s in zip(self.riders, r_in, r_out, r_sem):
                    ex.finish(a, b, s)

        return wrapped

    def split(self, outs, n_out):
        core, rest, per = list(outs[:n_out]), list(outs[n_out:]), []
        for ex, _ in self.riders:
            per.append(rest[:ex.n])
            rest = rest[ex.n:]
        return core, per


def pair_sum(big, sib, c, *, tr, name):
    _, R, C = big.shape

    def body(c_ref, a_ref, s_ref, o_ref):
        o_ref[...] = (a_ref[...].astype(F32) + s_ref[...].astype(F32)).astype(o_ref.dtype)

    grid_spec = pltpu.PrefetchScalarGridSpec(
        num_scalar_prefetch=1, grid=(4, R // tr),
        in_specs=[pl.BlockSpec((None, tr, C), lambda b, i, c_ref: (2 * b + c_ref[0], i, 0)),
                  pl.BlockSpec((None, tr, C), lambda b, i, c_ref: (b, i, 0))],
        out_specs=pl.BlockSpec((None, tr, C), lambda b, i, c_ref: (b, i, 0)))
    return pl.pallas_call(body, grid_spec=grid_spec, out_shape=jax.ShapeDtypeStruct((4, R, C), big.dtype),
                          compiler_params=_cparams(2), name=name)(c.reshape(1).astype(jnp.int32), big, sib)


class ChipScatter:
    def __init__(self, pres, chunks):
        self.shapes = [jax.ShapeDtypeStruct(p.shape, p.dtype) for p in pres]
        self.n = len(pres)
        self.pieces = _pieces(pres, chunks)
        n_p = len(self.pieces)
        self.sems = [pltpu.SemaphoreType.DMA((3 * n_p,)), pltpu.SemaphoreType.DMA((3 * n_p,)),
                     pltpu.SemaphoreType.DMA((n_p,))]

    def _copies(self, pre_refs, out_refs, sems):
        send_sems, recv_sems, local_sems = sems
        n_p = len(self.pieces)
        x, y, c = _my_pos()
        my_chip = 2 * x + y
        chips = [(1 - x, y), (x, 1 - y), (1 - x, 1 - y)]

        def copy(j, p, slot):
            px, py = chips[j]
            a, rows = self.pieces[p]
            src_slot, dst_slot = (2 * px + py, my_chip) if slot == "mine" else (my_chip, 2 * px + py)
            return pltpu.make_async_remote_copy(
                src_ref=pre_refs[a].at[src_slot, rows], dst_ref=out_refs[a].at[dst_slot, rows],
                send_sem=send_sems.at[j * n_p + p], recv_sem=recv_sems.at[j * n_p + p],
                device_id=(px, py, c), device_id_type=MESH)

        own = [pltpu.make_async_copy(pre_refs[a].at[my_chip, rows], out_refs[a].at[my_chip, rows], local_sems.at[p])
               for p, (a, rows) in enumerate(self.pieces)]
        sends = [copy(j, p, "mine") for j in range(3) for p in range(n_p)]
        recvs = [copy(j, p, "theirs") for j in range(3) for p in range(n_p)]
        return own, sends, recvs

    def start(self, pre_refs, out_refs, sems):
        own, sends, _ = self._copies(pre_refs, out_refs, sems)
        for cp in sends + own:
            cp.start()

    def finish(self, pre_refs, out_refs, sems):
        own, sends, recvs = self._copies(pre_refs, out_refs, sems)
        for cp in recvs:
            cp.wait_recv()
        for cp in sends:
            cp.wait_send()
        for cp in own:
            cp.wait()


def chip_scatter(pres, chunks, *, name):
    cs = ChipScatter(pres, chunks)
    n = cs.n

    def body(*refs):
        pre_refs, out_refs, sems = refs[:n], refs[n:2 * n], refs[2 * n:]
        cs.start(pre_refs, out_refs, sems)
        cs.finish(pre_refs, out_refs, sems)

    return pl.pallas_call(body, out_shape=cs.shapes, in_specs=[_ANY] * n, out_specs=[_ANY] * n,
                          scratch_shapes=cs.sems, name=name)(*pres)


def sibling_send(blks, chunks, *, name):
    n = len(blks)
    pieces = _pieces(blks, chunks)
    n_p = len(pieces)

    def body(*refs):
        x_refs, out_refs = refs[:n], refs[n:2 * n]
        send_sems, recv_sems = refs[2 * n:]
        x, y, c = _my_pos()
        cps = [pltpu.make_async_remote_copy(
            src_ref=x_refs[a].at[rows], dst_ref=out_refs[a].at[rows], send_sem=send_sems.at[p],
            recv_sem=recv_sems.at[p], device_id=(x, y, 1 - c), device_id_type=MESH)
            for p, (a, rows) in enumerate(pieces)]
        for cp in cps:
            cp.start()
        for cp in cps:
            cp.wait_recv()
        for cp in cps:
            cp.wait_send()

    return pl.pallas_call(
        body, out_shape=[jax.ShapeDtypeStruct(b.shape, b.dtype) for b in blks],
        in_specs=[_ANY] * n, out_specs=[_ANY] * n,
        scratch_shapes=[pltpu.SemaphoreType.DMA((n_p,)), pltpu.SemaphoreType.DMA((n_p,))],
        name=name)(*blks)


def reduce_slots(land, *, tr, name):
    n, R, C = land.shape

    def body(l_ref, o_ref):
        acc = l_ref[0].astype(F32)
        for s in range(1, n):
            acc = acc + l_ref[s].astype(F32)
        o_ref[...] = acc

    return pl.pallas_call(
        body, grid=(R // tr,), in_specs=[pl.BlockSpec((n, tr, C), lambda i: (0, i, 0))],
        out_specs=pl.BlockSpec((tr, C), lambda i: (i, 0)), out_shape=jax.ShapeDtypeStruct((R, C), F32),
        compiler_params=_cparams(1), name=name)(land)


TM = 512


def _tk(d):
    return min(d.shape[0], 1024)


def ffn_fwd_fused(x, g, wgu, wd, *, tm, name, riders=None, rider_args=()):
    T, Dm = x.shape
    nb, _, cb = wgu.shape
    nh = nb // 2
    Fd = nh * cb
    nc = riders.n if riders is not None else 0
    n_steps = T // tm
    stage_at = (0, (2 * n_steps) // 3, n_steps - 1, n_steps - 1)

    def body(*refs):
        x_ref, g_ref, wgu_ref, wd_ref = refs[:4]
        r_in = refs[4:4 + nc]
        o_ref, gu_ref, h_ref = refs[4 + nc:7 + nc]
        r_out, sems = refs[7 + nc:7 + 2 * nc], refs[7 + 2 * nc:]

        def ride(which):
            if riders is not None:
                @pl.when(pl.program_id(0) == stage_at[which])
                def _():
                    riders.stage(which, r_in, r_out, sems)

        ride(0)
        ride(1)
        xv = x_ref[...]
        r = lax.rsqrt(jnp.mean(xv * xv, axis=-1, keepdims=True) + RMS_EPS)
        hb = (xv * r * g_ref[...]).astype(BF16)
        h_ref[...] = hb
        acc = jnp.zeros((tm, Dm), F32)
        for jj in range(nh):
            cols = pl.ds(cb * jj, cb)
            gate = _dot(hb, wgu_ref[jj]).astype(BF16)
            up = _dot(hb, wgu_ref[nh + jj]).astype(BF16)
            gu_ref[0, :, cols] = gate
            gu_ref[1, :, cols] = up
            gv = gate.astype(F32)
            act = (gv * _sigmoid(gv) * up.astype(F32)).astype(BF16)
            acc = acc + _dot(act, wd_ref[cols, :])
        o_ref[...] = xv + 0.5 * acc
        ride(2)
        ride(3)

    outs = pl.pallas_call(
        body, grid=(n_steps,),
        in_specs=[pl.BlockSpec((tm, Dm), lambda i: (i, 0)), pl.BlockSpec((1, Dm), lambda i: (0, 0)),
                  pl.BlockSpec((nb, Dm, cb), lambda i: (0, 0, 0)), pl.BlockSpec((Fd, Dm), lambda i: (0, 0))]
        + [_ANY] * nc,
        out_specs=[pl.BlockSpec((tm, Dm), lambda i: (i, 0)), pl.BlockSpec((2, tm, Fd), lambda i: (0, i, 0)),
                   pl.BlockSpec((tm, Dm), lambda i: (i, 0))] + [_ANY] * nc,
        out_shape=[jax.ShapeDtypeStruct((T, Dm), F32), jax.ShapeDtypeStruct((2, T, Fd), BF16),
                   jax.ShapeDtypeStruct((T, Dm), BF16)] + (riders.shapes if nc else []),
        scratch_shapes=riders.sems if nc else [],
        compiler_params=_cparams(1), name=name)(x, g, wgu, wd, *rider_args)
    return outs[0], outs[1], outs[2], list(outs[3:])


def ffn_fwd(x, g, wgu, wd, tag, riders=None, rider_args=()):
    xo, gu, h, rode = ffn_fwd_fused(x, g, wgu, wd, tm=256, name=f"{tag}_fwd", riders=riders, rider_args=rider_args)
    return xo, (x, gu, h), rode


def ffn_bwd(d, saved, g, wgu, wd, tag, ride_bact=None, ride_dwgu=None, before_dx=None):
    x, gu, h = saved
    dgu, dwd, *rode_a = ffn_bwd_act(d, wd, gu, tm=TM, tn=1408, name=f"{tag}_bact", riding=ride_bact)
    dwgu = mm_tn(h, dgu, scale=1.0, a_split=False, b_split=True, tm=TM, tn=1408, tk=_tk(d), out_blocked=True,
                 name=f"{tag}_dwgu", riding=ride_dwgu)
    rode_g = []
    if ride_dwgu is not None:
        dwgu, *rode_g = dwgu
    riding = before_dx(dwgu, dwd) if before_dx is not None else None
    dx, dg, *rode_x = mm_nt_normbwd(dgu, wgu, x, g, d, a_split=True, tm=_tk(d), tk=1408, name=f"{tag}_dx",
                                    riding=riding)
    return dx, dg, dwgu, dwd, (rode_a, rode_g, rode_x)


def _tile2(v):
    return jnp.concatenate([v, v], axis=-1).reshape(1, LANES)


def _fold2(v):
    return v[:, :HEAD_DIM] + v[:, HEAD_DIM:]


EVEN = dict(dil=1, nsub=2, ppk=4, q_blk=0, k_blk=4, v_blk=5, n_heads=A_Q_HEADS, group=A_GROUP, max_dist=A_WINDOW - 1)
STICK = dict(q_blk=6, k_blk=10, v_blk=14, n_pairs=4)


def _odd_cfg(dil):
    return dict(dil=dil, nsub=4 if dil == 1 else 1, ppk=1, q_blk=0, k_blk=8, v_blk=16, n_heads=C_HEADS, group=1,
                max_dist=BLK)


def even_fwd(x, g, win, qg, kg, sinks, wout, tag, riders=None, rider_args=()):
    qkv, h = norm_matmul(x, g, win, tm=_tk(x), tn=1152, split=False, name=f"{tag}_in")
    qg2, kg2 = _tile2(qg), _tile2(kg)
    slopes = jnp.asarray(_alibi(A_Q_HEADS), F32)
    qkn = qk_norm(qkv, qg2, kg2, width=768, steps=1, n_q=4, tm=TM, name=f"{tag}_qkn")
    oa, _, *lse = banded_fwd(qkn, qkv, slopes, sinks, name=f"{tag}_swa", **EVEN)
    ob, rode = stick_fwd(qkv, name=f"{tag}_stick", riders=riders, rider_args=rider_args, **STICK)
    o = jnp.concatenate([oa, ob], axis=1)
    xo = mm_nn(o, wout, res=x, tm=TM, tn=D_MODEL, tk=D_MODEL, name=f"{tag}_out")
    return xo, (x, qkv, qkn, h, oa, lse, o), rode


def even_bwd(d, saved, g, win, qg, kg, sinks, wout, tag, riders=None, rider_args=(), before_dx=None):
    x, qkv, qkn, h, oa, lse, o = saved
    qg2, kg2 = _tile2(qg), _tile2(kg)
    slopes = jnp.asarray(_alibi(A_Q_HEADS), F32)
    dwout = mm_tn(o, d, scale=1.0, a_split=False, b_split=False, tm=D_MODEL, tn=D_MODEL, tk=_tk(d), name=f"{tag}_dwout")
    do = mm_nt(d, wout, tm=TM, tn=D_MODEL, tk=D_MODEL, name=f"{tag}_do")
    dqa, dka4, dva4, dsk = banded_bwd(qkn, qkv, slopes, sinks, do, oa, lse, None, None,
                                      do_blk=0, name=f"{tag}_swa_b", **EVEN)
    dqb, dkb, dvb, rode = stick_bwd(qkv, do, do_blk=4, name=f"{tag}_stick_b", riders=riders, rider_args=rider_args,
                                    **STICK)
    dqkv, dqg, dkg = assemble_even(dqa, dka4, dva4, dqb, dkb, dvb, qkv, qg2, kg2, tm=TM, name=f"{tag}_asm")
    dwin = mm_tn(h, dqkv, scale=1.0, a_split=False, b_split=False, tm=D_MODEL, tn=1152, tk=_tk(d), name=f"{tag}_dwin")
    riding = before_dx(dwin, dwout) if before_dx is not None else None
    dx, dg, *rode_x = mm_nt_normbwd(dqkv, win, x, g, d, a_split=False, tm=TM, tk=1152, name=f"{tag}_dx", riding=riding)
    return dx, dg, dwin, _fold2(dqg), _fold2(dkg), dsk[:, :A_Q_HEADS], dwout, (rode, rode_x)


def odd_fwd(x, g, win, qg, kg, wout, tag):
    qkv, h = norm_matmul(x, g, win, tm=_tk(x), tn=768, split=False, name=f"{tag}_in")
    qg2, kg2 = _tile2(qg), _tile2(kg)
    qkn = qk_norm(qkv, qg2, kg2, width=D_MODEL, steps=2, n_q=8, tm=TM, name=f"{tag}_qkn")
    outs = []
    for p, (window, dil) in enumerate(C_PATTERNS):
        slopes = jnp.asarray(_alibi(C_HEADS), F32) * float(dil)
        outs.append(banded_fwd(qkn, qkv, slopes, None, name=f"{tag}_dil{p}", **_odd_cfg(dil)))
    o, w1, w2, w3 = mix_fwd(outs[0][0], outs[1][0], outs[2][0], outs[0][1], outs[1][1], outs[2][1],
                            tm=TM, name=f"{tag}_mix")
    xo = mm_nn(o, wout, res=x, tm=TM, tn=D_MODEL, tk=D_MODEL, name=f"{tag}_out")
    return xo, (x, qkv, qkn, h, outs, (w1, w2, w3), o)


def odd_bwd(d, saved, g, win, qg, kg, wout, tag):
    x, qkv, qkn, h, outs, ws, o = saved
    qg2, kg2 = _tile2(qg), _tile2(kg)
    dwout = mm_tn(o, d, scale=1.0, a_split=False, b_split=False, tm=D_MODEL, tn=D_MODEL, tk=_tk(d), name=f"{tag}_dwout")
    do = mm_nt(d, wout, tm=TM, tn=D_MODEL, tk=D_MODEL, name=f"{tag}_do")
    parts = []
    for p, (window, dil) in enumerate(C_PATTERNS):
        slopes = jnp.asarray(_alibi(C_HEADS), F32) * float(dil)
        dq, dk, dv, _ = banded_bwd(qkn, qkv, slopes, None, do, None, outs[p][2:], ws[p], o,
                                   do_blk=0, name=f"{tag}_dil{p}_b", **_odd_cfg(dil))
        parts.append((dq, dk, dv))
    dqkv, dqg, dkg = assemble_odd(parts, qkv, qg2, kg2, tm=256, name=f"{tag}_asm")
    dwin = mm_tn(h, dqkv, scale=1.0, a_split=False, b_split=False, tm=TM, tn=768, tk=_tk(d), out_blocked=True,
                 name=f"{tag}_dwin")
    dx, dg = mm_nt_normbwd(dqkv, win, x, g, d, a_split=False, tm=TM, tk=768, name=f"{tag}_dx")
    return dx, dg, dwin, _fold2(dqg), _fold2(dkg), dwout


def xa_fwd(x, mem, g, gm, wq, wkv, qg, kg, wo, tag):
    qraw, h = norm_matmul(x, g, wq, tm=TM, tn=D_MODEL, split=False, name=f"{tag}_q")
    kvraw, hm = norm_matmul(mem, gm, wkv, tm=MEM_LEN, tn=512, split=False, name=f"{tag}_kv")
    o = xattn_fwd(qraw, kvraw, qg, kg, tm=TM, name=f"{tag}_att")
    xo = mm_nn(o, wo, res=x, tm=TM, tn=D_MODEL, tk=D_MODEL, name=f"{tag}_o")
    return xo, (x, qraw, h, kvraw, hm, o)


def xa_bwd(d, saved, mem, g, gm, wq, wkv, qg, kg, wo, tag):
    x, qraw, h, kvraw, hm, o = saved
    dwo = mm_tn(o, d, scale=1.0, a_split=False, b_split=False, tm=D_MODEL, tn=D_MODEL, tk=_tk(d), name=f"{tag}_dwo")
    do = mm_nt(d, wo, tm=TM, tn=D_MODEL, tk=D_MODEL, name=f"{tag}_do")
    dq, dkv, dqg, dkg = xattn_bwd(qraw, kvraw, qg, kg, do, o, tm=TM, name=f"{tag}_att_b")
    dwq = mm_tn(h, dq, scale=1.0, a_split=False, b_split=False, tm=D_MODEL, tn=D_MODEL, tk=_tk(d), name=f"{tag}_dwq")
    dx, dg = mm_nt_normbwd(dq, wq, x, g, d, a_split=False, tm=TM, tk=D_MODEL, name=f"{tag}_dx")
    dwkv = mm_tn(hm, dkv, scale=1.0, a_split=False, b_split=False, tm=TM, tn=512, tk=MEM_LEN, out_blocked=True,
                 name=f"{tag}_dwkv")
    _, dgm = mm_nt_normbwd(dkv, wkv, mem, gm, None, a_split=False, tm=MEM_LEN, tk=512, name=f"{tag}_dmem")
    return dx, dg, dgm, dwq, dwkv, dqg, dkg, dwo


MATS = (("ffn1_w_gu", 1), ("ffn1_w_down", 0), ("ev_w_in", 1), ("ev_w_out", 0), ("od_w_in", 1), ("od_w_out", 0),
        ("xa_w_q", 0), ("xa_w_kv", 1), ("xa_w_o", 0), ("ffn2_w_gu", 1), ("ffn2_w_down", 0))
SMALLS = ("ffn1_norm", "mix_norm", "ev_q_gain", "ev_k_gain", "ev_sinks", "od_q_gain", "od_k_gain", "xa_norm",
          "xa_mem_norm", "xa_q_gain", "xa_k_gain", "ffn2_norm")
WEIGHTS = ("ffn1_norm", "ffn1_w_gu", "ffn1_w_down", "mix_norm", "ev_w_in", "ev_q_gain", "ev_k_gain", "ev_sinks",
           "ev_w_out", "od_w_in", "od_q_gain", "od_k_gain", "od_w_out", "xa_norm", "xa_mem_norm", "xa_w_q",
           "xa_w_kv", "xa_q_gain", "xa_k_gain", "xa_w_o", "ffn2_norm", "ffn2_w_gu", "ffn2_w_down")
SMALL_ROWS = 16
LAYER_GROUPS = (
    (((("ffn1_w_gu", 0), ("ffn2_w_gu", 0)), 4, 512),
     ((("ffn1_w_down", 0), ("ffn2_w_down", 0)), 2, 352),
     ((("ev_w_out", 0), ("xa_w_q", 0), ("xa_w_o", 0)), 1, 384),
     ((("xa_w_kv", 0),), 1, 512),
     ((("ev_w_in", 0),), 1, 512)),
    (((("ffn1_w_gu", 1), ("ffn2_w_gu", 1)), 4, 512),
     ((("ffn1_w_down", 1), ("ffn2_w_down", 1)), 2, 352),
     ((("od_w_out", 0), ("xa_w_q", 1), ("xa_w_o", 1)), 1, 384),
     ((("xa_w_kv", 1),), 1, 512),
     ((("od_w_in", 0),), 1, 512)),
)
GATHER_FIRST = (((("ffn1_w_gu", 0),), 2, 512), ((("ffn1_w_down", 0),), 1, 352), ((("ev_w_out", 0),), 1, 128),
                ((("ev_w_in", 0),), 1, 512))
GATHER_IN_FFN1 = (((("ffn2_w_gu", 0),), 2, 512), ((("ffn2_w_down", 0),), 1, 352))
GATHER_IN_STICK = (((("ffn1_w_gu", 1),), 2, 512), ((("ffn1_w_down", 1),), 1, 352), ((("od_w_out", 0),), 1, 128),
                   ((("od_w_in", 0),), 1, 512),
                   ((("xa_w_q", 0), ("xa_w_o", 0), ("xa_w_q", 1), ("xa_w_o", 1)), 1, 512),
                   ((("xa_w_kv", 0), ("xa_w_kv", 1)), 1, 512))
GATHER_IN_FFN2 = (((("ffn2_w_gu", 1),), 2, 512), ((("ffn2_w_down", 1),), 1, 352))
ROUNDS = {
    "1": LAYER_GROUPS[1],
    "0a": (((("ffn2_w_gu", 0),), 2, 512), ((("ffn2_w_down", 0),), 1, 352)) + LAYER_GROUPS[0][2:],
    "0b": (((("ffn1_w_gu", 0),), 2, 512), ((("ffn1_w_down", 0),), 1, 352)),
}


def _chunks_of(groups):
    return tuple(g[1] for g in groups)
COL_SHARDED = {name for name, axis in MATS if axis == 1}
BLOCKED = {"ffn1_w_gu", "ffn2_w_gu", "xa_w_kv", "od_w_in"}


def group_halves(shards, c, groups):
    out = []
    for members, _, _ in groups:
        halves = []
        for name, layer in members:
            _, r, cc = shards[name].shape
            half = lax.dynamic_index_in_dim(shards[name][layer].reshape(2, r // 2, cc), c, 0, keepdims=False)
            halves.append(half.astype(BF16))
        out.append(jnp.concatenate(halves, axis=0))
    return out


def full_weights(gathered, shards, groups):
    full = {}
    for (members, _, _), arr in zip(groups, gathered):
        for w, (name, layer) in enumerate(members):
            _, r, cc = shards[name].shape
            piece = arr[:, w * (r // 2):(w + 1) * (r // 2)].reshape(4, r, cc)
            if name not in COL_SHARDED:
                piece = piece.reshape(4 * r, cc)
            elif name not in BLOCKED:
                piece = piece.transpose(1, 0, 2).reshape(r, 4 * cc)
            full[(name, layer)] = piece
    return full


def group_grads(grads, shards, groups):
    out = []
    for members, _, _ in groups:
        parts = []
        for name, layer in members:
            _, r, cc = shards[name].shape
            gfull = grads[(name, layer)]
            if name in COL_SHARDED and name not in BLOCKED:
                gfull = gfull.reshape(2, r // 2, 4, cc).transpose(2, 0, 1, 3)
            parts.append(gfull.reshape(N_DEV, r // 2, cc))
        out.append(jnp.concatenate(parts, axis=1))
    return out


def shard_grads(mine, theirs, c, shards, groups):
    per = {}
    for (members, _, _), a, b in zip(groups, mine, theirs):
        for w, (name, layer) in enumerate(members):
            _, r, cc = shards[name].shape
            rows = slice(w * (r // 2), (w + 1) * (r // 2))
            lo = jnp.where(c == 0, a[rows], b[rows])
            hi = jnp.where(c == 0, b[rows], a[rows])
            per[(name, layer)] = jnp.concatenate([lo, hi], axis=0)
    return per


def pack_small(vals):
    row10 = jnp.concatenate([vals["xa_q_gain"].reshape(1, 512), vals["xa_k_gain"].reshape(1, 512)], axis=1)
    row11 = jnp.concatenate([vals["ev_q_gain"], vals["ev_k_gain"], vals["od_q_gain"], vals["od_k_gain"],
                             vals["ev_sinks"], jnp.zeros((1, 1024 - 4 * 64 - 8), F32)], axis=1)
    return jnp.concatenate([vals["ffn1_norm"], vals["mix_norm"], vals["xa_norm"], vals["xa_mem_norm"],
                            vals["ffn2_norm"], row10, row11, jnp.zeros((SMALL_ROWS - 12, 1024), F32)], axis=0)


def unpack_small(arr):
    return {"ffn1_norm": arr[0:2], "mix_norm": arr[2:4], "xa_norm": arr[4:6], "xa_mem_norm": arr[6:8],
            "ffn2_norm": arr[8:10],
            "xa_q_gain": arr[10:11, 0:512].reshape(2, 256), "xa_k_gain": arr[10:11, 512:1024].reshape(2, 256),
            "ev_q_gain": arr[11:12, 0:64], "ev_k_gain": arr[11:12, 64:128], "od_q_gain": arr[11:12, 128:192],
            "od_k_gain": arr[11:12, 192:256], "ev_sinks": arr[11:12, 256:264]}


def local_step(x, mem, target, W, small, prereduce, later):
    depth = small["ffn1_norm"].shape[0]

    def row(name, l):
        return small[name][l:l + 1]

    saved = []
    for l in range(depth):
        j = l // 2
        def riding_gather(host):
            if l == 0 and host in later:
                return GatherBlocks(later[host][0], later[host][1]), later[host][0]
            return None, ()

        riders, rider_args = riding_gather("ffn1")
        x, s1, rode = ffn_fwd(x, row("ffn1_norm", l), W[("ffn1_w_gu", l)], W[("ffn1_w_down", l)], f"l{l}_f1",
                              riders=riders, rider_args=rider_args)
        if riders is not None:
            W = {**W, **later["ffn1"][2](rode)}
        if l % 2 == 0:
            riders, rider_args = riding_gather("stick")
            x, s2, rode = even_fwd(x, row("mix_norm", l), W[("ev_w_in", j)], row("ev_q_gain", j),
                                   row("ev_k_gain", j), small["ev_sinks"][j], W[("ev_w_out", j)], f"l{l}_ev",
                                   riders=riders, rider_args=rider_args)
            if riders is not None:
                W = {**W, **later["stick"][2](rode)}
        else:
            x, s2 = odd_fwd(x, row("mix_norm", l), W[("od_w_in", j)], row("od_q_gain", j), row("od_k_gain", j),
                            W[("od_w_out", j)], f"l{l}_od")
        x, s3 = xa_fwd(x, mem, row("xa_norm", l), row("xa_mem_norm", l), W[("xa_w_q", l)], W[("xa_w_kv", l)],
                       row("xa_q_gain", l), row("xa_k_gain", l), W[("xa_w_o", l)], f"l{l}_xa")
        riders, rider_args = riding_gather("ffn2")
        x, s4, rode = ffn_fwd(x, row("ffn2_norm", l), W[("ffn2_w_gu", l)], W[("ffn2_w_down", l)], f"l{l}_f2",
                              riders=riders, rider_args=rider_args)
        if riders is not None:
            W = {**W, **later["ffn2"][2](rode)}
        saved.append((s1, s2, s3, s4))
    loss, d = loss_kernel(x, target, tm=TM, name="loss")

    gw = {}
    gs = {name: [None] * small[name].shape[0] for name in SMALLS}
    pending, landed = None, {}
    for l in reversed(range(depth)):
        j = l // 2
        s1, s2, s3, s4 = saved[l]
        d, dg, dwgu, dwd, _ = ffn_bwd(d, s4, row("ffn2_norm", l), W[("ffn2_w_gu", l)], W[("ffn2_w_down", l)],
                                      f"l{l}_f2")
        gs["ffn2_norm"][l] = dg
        gw[("ffn2_w_gu", l)], gw[("ffn2_w_down", l)] = dwgu, dwd
        d, dg, dgm, dwq, dwkv, dqg, dkg, dwo = xa_bwd(
            d, s3, mem, row("xa_norm", l), row("xa_mem_norm", l), W[("xa_w_q", l)], W[("xa_w_kv", l)],
            row("xa_q_gain", l), row("xa_k_gain", l), W[("xa_w_o", l)], f"l{l}_xa")
        gs["xa_norm"][l], gs["xa_mem_norm"][l], gs["xa_q_gain"][l], gs["xa_k_gain"][l] = dg, dgm, dqg, dkg
        gw[("xa_w_q", l)], gw[("xa_w_kv", l)], gw[("xa_w_o", l)] = dwq, dwkv, dwo
        split = l == 0 and l % 2 == 0 and ("0a" in ROUNDS)
        early = []
        pre_early = None
        if l % 2 == 0:
            riders, rider_args = None, ()
            if pending is not None:
                riders, rider_args = ChipScatter(pending[1], _chunks_of(ROUNDS[pending[0]])), pending[1]

            def before_mixer_dx(dwin, dwout, j=j, early=early):
                gw[("ev_w_in", j)], gw[("ev_w_out", j)] = dwin, dwout
                early += prereduce.pack(gw, "0a")
                return Riding([(PairExchange(early, _chunks_of(ROUNDS["0a"])), early)])

            d, dg, dwin, dqg, dkg, dsk, dwout, (rode, rode_x) = even_bwd(
                d, s2, row("mix_norm", l), W[("ev_w_in", j)], row("ev_q_gain", j), row("ev_k_gain", j),
                small["ev_sinks"][j], W[("ev_w_out", j)], f"l{l}_ev", riders=riders, rider_args=rider_args,
                before_dx=before_mixer_dx if split else None)
            if pending is not None:
                landed[pending[0]], pending = rode, None
            gs["ev_q_gain"][j], gs["ev_k_gain"][j], gs["ev_sinks"][j] = dqg, dkg, dsk
            gw[("ev_w_in", j)], gw[("ev_w_out", j)] = dwin, dwout
            if split:
                pre_early = prereduce.sums(early, rode_x[0], "0a")
        else:
            d, dg, dwin, dqg, dkg, dwout = odd_bwd(
                d, s2, row("mix_norm", l), W[("od_w_in", j)], row("od_q_gain", j), row("od_k_gain", j),
                W[("od_w_out", j)], f"l{l}_od")
            gs["od_q_gain"][j], gs["od_k_gain"][j] = dqg, dkg
            gw[("od_w_in", j)], gw[("od_w_out", j)] = dwin, dwout
        gs["mix_norm"][l] = dg
        packed = []

        rnd = "0b" if pre_early is not None else str(l)
        final = l == 0

        def before_dx(dwgu, dwd, l=l, packed=packed, rnd=rnd, final=final):
            gw[("ffn1_w_gu", l)], gw[("ffn1_w_down", l)] = dwgu, dwd
            packed += prereduce.pack(gw, rnd)
            chunks = _chunks_of(ROUNDS[rnd])
            if not final:
                return Riding([(PairExchange(packed, chunks), packed)])
            sib = _standalone(PairExchange(packed, chunks), packed, f"pair_grads{rnd}")
            pre = prereduce.sums(packed, sib, rnd)
            return Riding([(ChipScatter(pre, chunks), pre)])

        ride_bact = ride_dwgu = None
        if pre_early is not None:
            chunks = _chunks_of(ROUNDS["0a"])
            ride_bact = Riding([(ChipScatter(pre_early[:2], chunks[:2]), pre_early[:2])])
            ride_dwgu = Riding([(ChipScatter(pre_early[2:], chunks[2:]), pre_early[2:])])
        d, dg, dwgu, dwd, (rode_a, rode_g, rode_x) = ffn_bwd(
            d, s1, row("ffn1_norm", l), W[("ffn1_w_gu", l)], W[("ffn1_w_down", l)], f"l{l}_f1",
            ride_bact=ride_bact, ride_dwgu=ride_dwgu, before_dx=before_dx)
        gs["ffn1_norm"][l] = dg
        if pre_early is not None:
            landed["0a"] = list(rode_a[0]) + list(rode_g[0])
        if pending is not None:
            landed[pending[0]] = chip_scatter(pending[1], _chunks_of(ROUNDS[pending[0]]),
                                              name=f"scatter_grads{pending[0]}")
        if final:
            landed[rnd], pending = rode_x[0], None
        else:
            pending = (rnd, prereduce.sums(packed, rode_x[0], rnd))
    if pending is not None:
        landed[pending[0]] = chip_scatter(pending[1], _chunks_of(ROUNDS[pending[0]]),
                                          name=f"scatter_grads{pending[0]}")
    gsmall = {name: jnp.concatenate(v, axis=0) for name, v in gs.items()}
    return loss, d, landed, gsmall


def kernel(x, mem, ffn1_norm, ffn1_w_gu, ffn1_w_down, mix_norm, ev_w_in, ev_q_gain, ev_k_gain, ev_sinks, ev_w_out, od_w_in, od_q_gain, od_k_gain, od_w_out, xa_norm, xa_mem_norm, xa_w_q, xa_w_kv, xa_q_gain, xa_k_gain, xa_w_o, ffn2_norm, ffn2_w_gu, ffn2_w_down, loss_target, m_ffn1_norm, m_ffn1_w_gu, m_ffn1_w_down, m_mix_norm, m_ev_w_in, m_ev_q_gain, m_ev_k_gain, m_ev_sinks, m_ev_w_out, m_od_w_in, m_od_q_gain, m_od_k_gain, m_od_w_out, m_xa_norm, m_xa_mem_norm, m_xa_w_q, m_xa_w_kv, m_xa_q_gain, m_xa_k_gain, m_xa_w_o, m_ffn2_norm, m_ffn2_w_gu, m_ffn2_w_down, v_ffn1_norm, v_ffn1_w_gu, v_ffn1_w_down, v_mix_norm, v_ev_w_in, v_ev_q_gain, v_ev_k_gain, v_ev_sinks, v_ev_w_out, v_od_w_in, v_od_q_gain, v_od_k_gain, v_od_w_out, v_xa_norm, v_xa_mem_norm, v_xa_w_q, v_xa_w_kv, v_xa_q_gain, v_xa_k_gain, v_xa_w_o, v_ffn2_norm, v_ffn2_w_gu, v_ffn2_w_down):
    given = dict(locals())
    w = {n: given[n] for n in WEIGHTS}
    m = {n: given["m_" + n] for n in WEIGHTS}
    v = {n: given["v_" + n] for n in WEIGHTS}
    c = lax.axis_index("c")
    shards = {name: w[name] for name, _ in MATS}
    small = {n: w[n] for n in SMALLS}

    def gathering(groups):
        return group_halves(shards, c, groups), _chunks_of(groups), lambda got: full_weights(got, shards, groups)

    halves, chunks, unpack = gathering(GATHER_FIRST)
    full = unpack(gather_blocks(halves, chunks, name="gather_weights0"))
    later = {"ffn1": gathering(GATHER_IN_FFN1), "stick": gathering(GATHER_IN_STICK), "ffn2": gathering(GATHER_IN_FFN2)}

    class prereduce:
        @staticmethod
        def pack(gw, rnd):
            return group_grads(gw, shards, ROUNDS[rnd])

        @staticmethod
        def sums(packed, sib, rnd):
            return [pair_sum(p, s, c, tr=g[2], name=f"pair_sum{rnd}_{i}")
                    for i, (g, p, s) in enumerate(zip(ROUNDS[rnd], packed, sib))]

    loss_b, grad_x, landed, gsmall = local_step(x[0], mem[0], loss_target[0], full, small, prereduce, later)

    per = {}
    for rnd, land in sorted(landed.items()):
        groups = ROUNDS[rnd]
        mine = [reduce_slots(a, tr=g[2], name=f"sum_grads{rnd}_{i}") for i, (g, a) in enumerate(zip(groups, land))]
        theirs = sibling_send(mine, _chunks_of(groups), name=f"swap_grads{rnd}")
        per.update(shard_grads(mine, theirs, c, shards, groups))
    g = {name: jnp.stack([per[(name, layer)] for layer in range(w[name].shape[0])], axis=0) for name, _ in MATS}
    land_small = gather_small(pack_small(gsmall), name="gather_small")
    g_small = unpack_small(reduce_slots(land_small, tr=SMALL_ROWS, name="sum_small"))
    g.update(g_small)

    delta, new_m, new_v = {}, {}, {}
    for name, _ in MATS:
        shp = w[name].shape
        flat = [a.reshape(-1, shp[-1]) for a in (w[name], g[name], m[name], v[name])]
        dl, nm, nv = adamw(*flat, br=BLK, name=f"adamw_{name}")
        delta[name], new_m[name], new_v[name] = dl.reshape(shp), nm.reshape(shp), nv.reshape(shp)
    dl, nm, nv = adamw(pack_small(small), pack_small(g_small), pack_small({n: m[n] for n in SMALLS}),
                       pack_small({n: v[n] for n in SMALLS}), br=SMALL_ROWS, name="adamw_small")
    for dst, arr in ((delta, dl), (new_m, nm), (new_v, nv)):
        dst.update(unpack_small(arr))

    loss = lax.psum(loss_b[0, 0], ("x", "y", "c"))
    return (loss, grad_x[None], *[g[n] for n in WEIGHTS], *[delta[n] for n in WEIGHTS],
            *[new_m[n] for n in WEIGHTS], *[new_v[n] for n in WEIGHTS])
```

```python
import jax
import jax.numpy as jnp
from jax import lax
from jax.experimental import pallas as pl
from jax.experimental.pallas import tpu as pltpu

F32 = jnp.float32
BF16 = jnp.bfloat16

D_MODEL = 1024
HEAD_DIM = 64
LANES = 128
BLK = 128
RMS_EPS = 1e-6
MEM_LEN = 256
X_HEADS = 4
X_HEAD_DIM = 256
A_Q_HEADS = 8
A_GROUP = 4
A_WINDOW = 128
C_HEADS = 16
C_PATTERNS = ((128, 1), (512, 4), (2048, 16))
NEG = -1e30
VMEM_LIMIT = 56 * 2 ** 20

ADAM_LR = 0.001
ADAM_B1 = 0.9
ADAM_B2 = 0.999
ADAM_EPS = 1e-08
ADAM_WD = 0.01
ADAM_STEP = 10

N_DEV = 8
MESH = pl.DeviceIdType.MESH


def _cparams(n):
    return pltpu.CompilerParams(dimension_semantics=("arbitrary",) * n, vmem_limit_bytes=VMEM_LIMIT)


def _dot(a, b):
    return jnp.dot(a, b, preferred_element_type=F32)


def _dot_nt(a, b):
    return lax.dot_general(a, b, (((1,), (1,)), ((), ())), preferred_element_type=F32)


def _dot_tn(a, b):
    return lax.dot_general(a, b, (((0,), (0,)), ((), ())), preferred_element_type=F32)


def _sigmoid(z):
    return 1.0 / (1.0 + jnp.exp(-z))


def norm_matmul(x, g, w, *, tm, tn, split, name):
    T, K = x.shape
    blocked = w.ndim == 3
    assert not blocked or w.shape[2] == tn
    N = w.shape[0] * w.shape[2] if blocked else w.shape[1]
    nj = N // tn

    def body(x_ref, g_ref, w_ref, o_ref, h_ref):
        @pl.when(pl.program_id(1) == 0)
        def _():
            xv = x_ref[...]
            r = lax.rsqrt(jnp.mean(xv * xv, axis=-1, keepdims=True) + RMS_EPS)
            h_ref[...] = (xv * r * g_ref[...]).astype(BF16)

        o_ref[...] = _dot(h_ref[...], w_ref[...]).astype(o_ref.dtype)

    if split:
        njh = nj // 2
        o_shape = jax.ShapeDtypeStruct((2, T, N // 2), BF16)
        o_spec = pl.BlockSpec((None, tm, tn), lambda i, j: (j // njh, i, j % njh))
    else:
        o_shape = jax.ShapeDtypeStruct((T, N), F32)
        o_spec = pl.BlockSpec((tm, tn), lambda i, j: (i, j))
    return pl.pallas_call(
        body, grid=(T // tm, nj),
        in_specs=[pl.BlockSpec((tm, K), lambda i, j: (i, 0)),
                  pl.BlockSpec((1, K), lambda i, j: (0, 0)),
                  (pl.BlockSpec((None, K, tn), lambda i, j: (j, 0, 0)) if blocked
                   else pl.BlockSpec((K, tn), lambda i, j: (0, j)))],
        out_specs=[o_spec, pl.BlockSpec((tm, K), lambda i, j: (i, 0))],
        out_shape=[o_shape, jax.ShapeDtypeStruct((T, K), BF16)],
        compiler_params=_cparams(2), name=name)(x, g, w)


def mm_nn(a, b, *, res, tm, tn, tk, name):
    T = a.shape[0]
    K, N = b.shape
    nk = K // tk

    def body(a_ref, b_ref, r_ref, o_ref, acc):
        k = pl.program_id(2)

        @pl.when(k == 0)
        def _():
            acc[...] = jnp.zeros_like(acc)

        acc[...] += _dot(a_ref[...].astype(BF16), b_ref[...])

        @pl.when(k == nk - 1)
        def _():
            o_ref[...] = r_ref[...] + acc[...]

    return pl.pallas_call(
        body, grid=(T // tm, N // tn, nk),
        in_specs=[pl.BlockSpec((tm, tk), lambda i, j, k: (i, k)), pl.BlockSpec((tk, tn), lambda i, j, k: (k, j)),
                  pl.BlockSpec((tm, tn), lambda i, j, k: (i, j))],
        out_specs=pl.BlockSpec((tm, tn), lambda i, j, k: (i, j)),
        out_shape=jax.ShapeDtypeStruct((T, N), F32),
        scratch_shapes=[pltpu.VMEM((tm, tn), F32)],
        compiler_params=_cparams(3), name=name)(a, b, res)


def mm_nt(a, b, *, tm, tn, tk, name):
    T, K = a.shape
    N = b.shape[0]
    nk = K // tk

    def body(a_ref, b_ref, o_ref, acc):
        k = pl.program_id(2)

        @pl.when(k == 0)
        def _():
            acc[...] = jnp.zeros_like(acc)

        acc[...] += _dot_nt(a_ref[...].astype(BF16), b_ref[...])

        @pl.when(k == nk - 1)
        def _():
            o_ref[...] = acc[...]

    return pl.pallas_call(
        body, grid=(T // tm, N // tn, nk),
        in_specs=[pl.BlockSpec((tm, tk), lambda i, j, k: (i, k)),
                  pl.BlockSpec((tn, tk), lambda i, j, k: (j, k))],
        out_specs=pl.BlockSpec((tm, tn), lambda i, j, k: (i, j)),
        out_shape=jax.ShapeDtypeStruct((T, N), F32),
        scratch_shapes=[pltpu.VMEM((tm, tn), F32)],
        compiler_params=_cparams(3), name=name)(a, b)


def ffn_bwd_act(d, wd, gu, *, tm, tn, name, riding=None):
    T, K = d.shape
    Fd = wd.shape[0]
    ni = T // tm

    def body(d_ref, w_ref, g_ref, u_ref, dgu_ref, dwd_ref, acc):
        i = pl.program_id(1)

        @pl.when(i == 0)
        def _():
            acc[...] = jnp.zeros_like(acc)

        db = d_ref[...].astype(BF16)
        da = 0.5 * _dot_nt(db, w_ref[...])
        gv = g_ref[...].astype(F32)
        uv = u_ref[...].astype(F32)
        s = _sigmoid(gv)
        silu = gv * s
        acc[...] += _dot_tn((silu * uv).astype(BF16), db)
        dgu_ref[0] = (da * uv * (s * (1.0 + gv * (1.0 - s)))).astype(BF16)
        dgu_ref[1] = (da * silu).astype(BF16)

        @pl.when(i == ni - 1)
        def _():
            dwd_ref[...] = (0.5 * acc[...]).astype(BF16)

    in_specs = [pl.BlockSpec((tm, K), lambda j, i: (i, 0)),
                pl.BlockSpec((tn, K), lambda j, i: (j, 0)),
                pl.BlockSpec((None, tm, tn), lambda j, i: (0, i, j)),
                pl.BlockSpec((None, tm, tn), lambda j, i: (1, i, j))]
    out_specs = [pl.BlockSpec((2, tm, tn), lambda j, i: (0, i, j)), pl.BlockSpec((tn, K), lambda j, i: (j, 0))]
    out_shape = [jax.ShapeDtypeStruct((2, T, Fd), BF16), jax.ShapeDtypeStruct((Fd, K), BF16)]
    return _call_with_riders(body, riding, (Fd // tn, ni), in_specs, out_specs, out_shape,
                             [pltpu.VMEM((tn, K), F32)], [d, wd, gu, gu], name)


def _call_with_riders(body, riding, grid, in_specs, out_specs, out_shape, scratch, args, name):
    n_out = len(out_shape)
    if riding is None:
        return pl.pallas_call(body, grid=grid, in_specs=in_specs, out_specs=out_specs, out_shape=out_shape,
                              scratch_shapes=scratch, compiler_params=_cparams(len(grid)), name=name)(*args)

    def is_first():
        ok = pl.program_id(0) == 0
        for ax in range(1, len(grid)):
            ok = ok & (pl.program_id(ax) == 0)
        return ok

    def is_last():
        ok = pl.program_id(0) == grid[0] - 1
        for ax in range(1, len(grid)):
            ok = ok & (pl.program_id(ax) == grid[ax] - 1)
        return ok

    outs = pl.pallas_call(
        riding.wrap(body, len(in_specs), n_out, len(scratch), is_first, is_last), grid=grid,
        in_specs=list(in_specs) + riding.in_specs, out_specs=list(out_specs) + riding.out_specs,
        out_shape=list(out_shape) + riding.out_shapes, scratch_shapes=list(scratch) + riding.scratch,
        compiler_params=_cparams(len(grid)), name=name)(*args, *riding.args)
    core, per = riding.split(outs, n_out)
    return (*core, *per)


def mm_nt_normbwd(a, b, x, g, res, *, a_split, tm, tk, name, riding=None):
    T, Dm = x.shape
    blocked = b.ndim == 3
    assert not blocked or b.shape[2] == tk
    K = b.shape[0] * b.shape[2] if blocked else b.shape[1]
    nk = K // tk
    nkh = nk // 2
    has_res = res is not None

    def body(*refs):
        if has_res:
            a_ref, b_ref, x_ref, g_ref, r_ref, dx_ref, dg_ref, acc = refs
        else:
            a_ref, b_ref, x_ref, g_ref, dx_ref, dg_ref, acc = refs
        i = pl.program_id(0)
        k = pl.program_id(1)

        @pl.when(k == 0)
        def _():
            acc[...] = jnp.zeros_like(acc)

        acc[...] += _dot_nt(a_ref[...].astype(BF16), b_ref[...])

        @pl.when(k == nk - 1)
        def _():
            xv = x_ref[...]
            r = lax.rsqrt(jnp.mean(xv * xv, axis=-1, keepdims=True) + RMS_EPS)
            xh = xv * r
            dh = acc[...]
            dxh = dh * g_ref[...]
            dx = r * (dxh - xh * jnp.mean(dxh * xh, axis=-1, keepdims=True))
            if has_res:
                dx = dx + r_ref[...]
            dx_ref[...] = dx
            part = jnp.sum(dh * xh, axis=0, keepdims=True)

            @pl.when(i == 0)
            def _():
                dg_ref[...] = part

            @pl.when(i > 0)
            def _():
                dg_ref[...] += part

    if a_split:
        a_spec = pl.BlockSpec((None, tm, tk), lambda i, k: (k // nkh, i, k % nkh))
    else:
        a_spec = pl.BlockSpec((tm, tk), lambda i, k: (i, k))
    in_specs = [a_spec,
                (pl.BlockSpec((None, Dm, tk), lambda i, k: (k, 0, 0)) if blocked
                 else pl.BlockSpec((Dm, tk), lambda i, k: (0, k))),
                pl.BlockSpec((tm, Dm), lambda i, k: (i, 0)),
                pl.BlockSpec((1, Dm), lambda i, k: (0, 0))]
    args = [a, b, x, g]
    if has_res:
        in_specs.append(pl.BlockSpec((tm, Dm), lambda i, k: (i, 0)))
        args.append(res)
    out_specs = [pl.BlockSpec((tm, Dm), lambda i, k: (i, 0)), pl.BlockSpec((1, Dm), lambda i, k: (0, 0))]
    out_shape = [jax.ShapeDtypeStruct((T, Dm), F32), jax.ShapeDtypeStruct((1, Dm), F32)]
    scratch = [pltpu.VMEM((tm, Dm), F32)]
    return _call_with_riders(body, riding, (T // tm, nk), in_specs, out_specs, out_shape, scratch, args, name)


def mm_tn(a, b, *, scale, a_split, b_split, tm, tn, tk, name, out_blocked=False, riding=None):
    T = a.shape[-2]
    M = a.shape[-1] * (2 if a_split else 1)
    N = b.shape[-1] * (2 if b_split else 1)
    ni, nj, nk = M // tm, N // tn, T // tk
    nih, njh = ni // 2, nj // 2

    def body(a_ref, b_ref, o_ref, acc):
        k = pl.program_id(2)

        @pl.when(k == 0)
        def _():
            acc[...] = jnp.zeros_like(acc)

        acc[...] += _dot_tn(a_ref[...].astype(BF16), b_ref[...].astype(BF16))

        @pl.when(k == nk - 1)
        def _():
            o_ref[...] = (acc[...] * scale).astype(o_ref.dtype)

    if a_split:
        a_spec = pl.BlockSpec((None, tk, tm), lambda i, j, k: (i // nih, k, i % nih))
    else:
        a_spec = pl.BlockSpec((tk, tm), lambda i, j, k: (k, i))
    if b_split:
        b_spec = pl.BlockSpec((None, tk, tn), lambda i, j, k: (j // njh, k, j % njh))
    else:
        b_spec = pl.BlockSpec((tk, tn), lambda i, j, k: (k, j))
    if out_blocked:
        o_spec = pl.BlockSpec((None, None, tm, tn), lambda i, j, k: (j, i, 0, 0))
        o_shape = jax.ShapeDtypeStruct((nj, ni, tm, tn), BF16)
    else:
        o_spec = pl.BlockSpec((tm, tn), lambda i, j, k: (i, j))
        o_shape = jax.ShapeDtypeStruct((M, N), BF16)
    outs = _call_with_riders(body, riding, (ni, nj, nk), [a_spec, b_spec], [o_spec], [o_shape],
                             [pltpu.VMEM((tm, tn), F32)], [a, b], name)
    return outs[0] if riding is None else tuple(outs)


def loss_kernel(y, target, *, tm, name):
    T, Dm = y.shape

    def body(y_ref, t_ref, l_ref, dy_ref):
        e = y_ref[...] - t_ref[...]
        dy_ref[...] = e * (1.0 / Dm)
        part = (0.5 / Dm) * jnp.sum(jnp.sum(e * e, axis=-1, keepdims=True), axis=0, keepdims=True)
        part = jnp.broadcast_to(part, (8, LANES))

        @pl.when(pl.program_id(0) == 0)
        def _():
            l_ref[...] = part

        @pl.when(pl.program_id(0) > 0)
        def _():
            l_ref[...] += part

    return pl.pallas_call(
        body, grid=(T // tm,),
        in_specs=[pl.BlockSpec((tm, Dm), lambda i: (i, 0)), pl.BlockSpec((tm, Dm), lambda i: (i, 0))],
        out_specs=[pl.BlockSpec((8, LANES), lambda i: (0, 0)), pl.BlockSpec((tm, Dm), lambda i: (i, 0))],
        out_shape=[jax.ShapeDtypeStruct((8, LANES), F32), jax.ShapeDtypeStruct((T, Dm), F32)],
        compiler_params=_cparams(1), name=name)(y, target)


def adamw(w, g, m, v, *, br, name):
    R, C = w.shape

    def body(w_ref, g_ref, m_ref, v_ref, d_ref, nm_ref, nv_ref):
        gv = g_ref[...]
        nm = ADAM_B1 * m_ref[...] + (1.0 - ADAM_B1) * gv
        nv = ADAM_B2 * v_ref[...] + (1.0 - ADAM_B2) * (gv * gv)
        m_hat = nm / (1.0 - ADAM_B1 ** ADAM_STEP)
        v_hat = nv / (1.0 - ADAM_B2 ** ADAM_STEP)
        d_ref[...] = -ADAM_LR * (m_hat / (jnp.sqrt(v_hat) + ADAM_EPS) + ADAM_WD * w_ref[...])
        nm_ref[...] = nm
        nv_ref[...] = nv

    spec = pl.BlockSpec((br, C), lambda i: (i, 0))
    shp = jax.ShapeDtypeStruct((R, C), F32)
    return pl.pallas_call(
        body, grid=(R // br,), in_specs=[spec] * 4, out_specs=[spec] * 3, out_shape=[shp] * 3,
        compiler_params=_cparams(1), name=name)(w, g, m, v)


def _lane0():
    return lax.broadcasted_iota(jnp.int32, (1, LANES), 1) < HEAD_DIM


def _half_sum(x, m0):
    s0 = jnp.sum(jnp.where(m0, x, 0.0), axis=-1, keepdims=True)
    s1 = jnp.sum(jnp.where(m0, 0.0, x), axis=-1, keepdims=True)
    return jnp.where(m0, s0, s1)


def _head_rms(x, m0):
    return lax.rsqrt(_half_sum(x * x, m0) * (1.0 / HEAD_DIM) + RMS_EPS)


def _alibi(n):
    return [float(2.0 ** (-8.0 * (h + 1) / n)) for h in range(n)]


def _mask_half(x, m0, e):
    return jnp.where(m0, x, 0.0) if e == 0 else jnp.where(m0, 0.0, x)


def _band_masks2(max_dist, has_prev, live):
    row = lax.broadcasted_iota(jnp.int32, (2 * BLK, 2 * BLK), 0)
    col = lax.broadcasted_iota(jnp.int32, (2 * BLK, 2 * BLK), 1)
    dist = (row & (BLK - 1)) - col + BLK
    lim = jnp.where(live, max_dist, -1)
    first = jnp.where(has_prev, 0, BLK)
    valid = (dist >= 0) & (dist <= lim) & (col >= first)
    top = lax.broadcasted_iota(jnp.int32, (2 * BLK, 1), 0) < BLK
    return dist.astype(F32), valid, top


def _stack_heads(x, m0, kes):
    parts = []
    for e in range(2):
        h = _mask_half(x, m0, e)
        parts.append(pltpu.roll(h, HEAD_DIM, 1) if kes[e] != e else h)
    return jnp.concatenate(parts, axis=0)


def _unstack_heads(y, m0, kes):
    parts = []
    for e in range(2):
        h = y[e * BLK:(e + 1) * BLK]
        parts.append(pltpu.roll(h, HEAD_DIM, 1) if kes[e] != e else h)
    return jnp.where(m0, parts[0], parts[1])


def _rows(r, dil):
    return pl.ds(r, BLK, stride=dil) if dil > 1 else pl.ds(0, BLK)


def _band_units(dil, nsub):
    assert dil == 1 or nsub == 1
    if nsub == 1:
        return [(_rows(r, dil), ("prev", _rows(r, dil)), 0) for r in range(dil)]
    units = [(pl.ds(0, BLK), ("prev", pl.ds(0, BLK)), 0)]
    units += [(pl.ds(BLK * s, BLK), ("cur", pl.ds(BLK * (s - 1), BLK)), s) for s in range(1, nsub)]
    return units


def _head_col_spec(ppk, RB, row_block):
    if ppk == 1:
        return pl.BlockSpec((None, RB, 1), lambda p, i: (p, row_block(i), 0))
    return pl.BlockSpec((ppk, RB, 1), lambda p, i: (p, row_block(i), 0))


def _band_specs(dil, nsub, ppk, q_blk, k_blk, v_blk, kv_shared, nb):
    RB = BLK * dil * nsub
    PB = BLK if nsub > 1 else RB
    qw = LANES * ppk
    kw = LANES if kv_shared else qw

    def cur(i):
        return jnp.minimum(i, nb - 1)

    def prev(i):
        return jnp.maximum(i * nsub - 1, 0) if nsub > 1 else jnp.maximum(i - 1, 0)

    def kidx(base):
        return (lambda p, i: (cur(i), base)) if kv_shared else (lambda p, i: (cur(i), base + p))

    def pidx(base):
        return (lambda p, i: (prev(i), base)) if kv_shared else (lambda p, i: (prev(i), base + p))

    specs = [pl.BlockSpec((RB, qw), lambda p, i: (cur(i), q_blk + p)),
             pl.BlockSpec((RB, kw), kidx(k_blk)), pl.BlockSpec((PB, kw), pidx(k_blk)),
             pl.BlockSpec((RB, kw), kidx(v_blk)), pl.BlockSpec((PB, kw), pidx(v_blk))]
    return specs if dil == 1 else [specs[0], specs[1], specs[3]]


def qk_norm(qkv, q_gain2, k_gain2, *, width, steps, n_q, tm, name):
    T = qkv.shape[0]
    nsb = width // LANES

    def body(x_ref, qg_ref, kg_ref, o_ref):
        m0 = _lane0()
        for b in range(nsb):
            is_q = ((pl.program_id(1) * nsb + b) < n_q).astype(F32)
            gain = qg_ref[...] * is_q + kg_ref[...] * (1.0 - is_q)
            cols = pl.ds(LANES * b, LANES)
            xv = x_ref[:, cols]
            o_ref[:, cols] = xv * _head_rms(xv, m0) * gain

    gspec = pl.BlockSpec((1, LANES), lambda i, j: (0, 0))
    return pl.pallas_call(
        body, grid=(T // tm, steps),
        in_specs=[pl.BlockSpec((tm, width), lambda i, j: (i, j)), gspec, gspec],
        out_specs=pl.BlockSpec((tm, width), lambda i, j: (i, j)),
        out_shape=jax.ShapeDtypeStruct((T, width * steps), F32),
        compiler_params=_cparams(2), name=name)(qkv, q_gain2, k_gain2)


def banded_fwd(qkn, qkv, slopes, sinks, *, dil, nsub, ppk, q_blk, k_blk, v_blk, n_heads, group,
               max_dist, name):
    T = qkv.shape[0]
    RB = BLK * dil * nsub
    nb = T // RB
    npair = n_heads // 2
    kv_shared = group > 1
    scale = HEAD_DIM ** -0.5
    has_sink = sinks is not None

    def body(*refs):
        slope_ref = refs[0]
        if has_sink:
            sink_ref, refs = refs[1], refs[2:]
        else:
            refs = refs[1:]
        if dil == 1:
            q_ref, kc_ref, kp_ref, vc_ref, vp_ref, o_ref, l_ref, lc0_ref, lc1_ref = refs
        else:
            q_ref, kc_ref, vc_ref, o_ref, l_ref, lc0_ref, lc1_ref, kp_ref, vp_ref = refs
        pb = pl.program_id(0)
        i = pl.program_id(1)
        if dil > 1:
            @pl.when(i == 0)
            def _():
                kp_ref[...] = jnp.zeros_like(kp_ref)
                vp_ref[...] = jnp.zeros_like(vp_ref)
        m0 = _lane0()
        distf, valid_first, top = _band_masks2(max_dist, i > 0, i >= 0)
        valid_inner = _band_masks2(max_dist, i >= 0, i >= 0)[1] if nsub > 1 else None
        for u, (rows, (src, prows), sub) in enumerate(_band_units(dil, nsub)):
            valid = valid_first if sub == 0 else valid_inner
            kpr, vpr = (kp_ref, vp_ref) if src == "prev" else (kc_ref, vc_ref)
            kcache = {}
            for jp in range(ppk):
                cs = pl.ds(LANES * jp, LANES)
                jk = 0 if kv_shared else jp
                if jk not in kcache:
                    ks = pl.ds(LANES * jk, LANES)
                    kcur, vcur = kc_ref[rows, ks], vc_ref[rows, ks]
                    if dil == 1:
                        kprev, vprev = kpr[prows, ks], vpr[prows, ks]
                    else:
                        kprev, vprev = kp_ref[u, :, ks], vp_ref[u, :, ks]
                        kp_ref[u, :, ks] = kcur
                        vp_ref[u, :, ks] = vcur
                    kcat = jnp.concatenate([kprev, kcur], axis=0)
                    vcat = jnp.concatenate([vprev, vcur], axis=0)
                    kcache[jk] = (kcat.astype(BF16), vcat.astype(BF16))
                kn, vcat = kcache[jk]
                qn = q_ref[rows, cs]
                kes = [((2 * jp + e) // group) % 2 if kv_shared else e for e in range(2)]
                hidx = 2 * (pb * ppk + jp)
                qs = _stack_heads(qn, m0, kes).astype(BF16)
                slope = jnp.where(top, slope_ref[hidx], slope_ref[hidx + 1])
                s = jnp.where(valid, _dot_nt(qs, kn) * scale - slope * distf, NEG)
                m = jnp.max(s, axis=-1, keepdims=True)
                if has_sink:
                    sk = jnp.where(top, sink_ref[hidx], sink_ref[hidx + 1])
                    m = jnp.maximum(m, sk)
                p = jnp.exp(s - m)
                den = jnp.sum(p, axis=-1, keepdims=True)
                if has_sink:
                    den = den + jnp.exp(sk - m)
                o_full = _dot((p * (1.0 / den)).astype(BF16), vcat)
                o_ref[rows, cs] = _unstack_heads(o_full, m0, kes)
                lse = m + jnp.log(den)
                l_ref[rows, cs] = _unstack_heads(jnp.broadcast_to(lse, (2 * BLK, LANES)), m0, [0, 1])
                for e, lc_ref in enumerate((lc0_ref, lc1_ref)):
                    if ppk == 1:
                        lc_ref[rows, :] = lse[e * BLK:(e + 1) * BLK]
                    else:
                        lc_ref[jp, rows, :] = lse[e * BLK:(e + 1) * BLK]

    smem = pl.BlockSpec(memory_space=pltpu.SMEM)
    qw = LANES * ppk
    ospec = pl.BlockSpec((RB, qw), lambda p, i: (i, p))
    oshape = jax.ShapeDtypeStruct((T, n_heads * HEAD_DIM), F32)
    args = [slopes] + ([sinks] if has_sink else []) + ([qkn] * 3 + [qkv] * 2 if dil == 1 else [qkn, qkn, qkv])
    kw = LANES if kv_shared else qw
    prev_scratch = [] if dil == 1 else [pltpu.VMEM((dil, BLK, kw), F32)] * 2
    return pl.pallas_call(
        body, grid=(npair // ppk, nb),
        in_specs=[smem] * (2 if has_sink else 1) + _band_specs(dil, nsub, ppk, q_blk, k_blk, v_blk, kv_shared, nb),
        out_specs=[ospec, ospec] + [_head_col_spec(ppk, RB, lambda i: i)] * 2,
        out_shape=[oshape, oshape] + [jax.ShapeDtypeStruct((npair, T, 1), F32)] * 2,
        scratch_shapes=prev_scratch, compiler_params=_cparams(2), name=name)(*args)


def banded_bwd(qkn, qkv, slopes, sinks, do, o, lsec, w, omix, *, dil, nsub, ppk, q_blk, k_blk, v_blk,
               n_heads, group, max_dist, do_blk, name):
    T = qkv.shape[0]
    RB = BLK * dil * nsub
    nb = T // RB
    npair = n_heads // 2
    kv_shared = group > 1
    scale = HEAD_DIM ** -0.5
    has_sink = sinks is not None
    mixed = w is not None
    qw = LANES * ppk

    def body(*refs):
        slope_ref = refs[0]
        if has_sink:
            sink_ref, refs = refs[1], refs[2:]
        else:
            refs = refs[1:]
        if dil == 1:
            q_ref, kc_ref, kp_ref, vc_ref, vp_ref, do_ref, lc0_ref, lc1_ref = refs[:8]
            refs = refs[8:]
        else:
            q_ref, kc_ref, vc_ref, do_ref, lc0_ref, lc1_ref = refs[:6]
            refs, kp_ref, vp_ref = refs[6:-2], refs[-2], refs[-1]
        if mixed:
            w_ref, om_ref, refs = refs[0], refs[1], refs[2:]
        else:
            o_ref, refs = refs[0], refs[1:]
        dq_ref, dk_ref, dv_ref, dsk_ref, ck_ref, cv_ref = refs
        pb = pl.program_id(0)
        i = pl.program_id(1)
        live = i < nb
        m0 = _lane0()
        lane = lax.broadcasted_iota(jnp.int32, (1, LANES), 1)
        distf, valid_first, top = _band_masks2(max_dist, i > 0, live)
        valid_inner = _band_masks2(max_dist, i >= 0, live)[1] if nsub > 1 else None
        livef = live.astype(F32)

        def half_rows(x):
            s0 = jnp.sum(jnp.where(m0, x, 0.0), axis=-1, keepdims=True)
            s1 = jnp.sum(jnp.where(m0, 0.0, x), axis=-1, keepdims=True)
            return jnp.concatenate([s0, s1], axis=0)

        @pl.when((pb == 0) & (i == 0))
        def _():
            dsk_ref[...] = jnp.zeros_like(dsk_ref)

        @pl.when(i == 0)
        def _():
            ck_ref[...] = jnp.zeros_like(ck_ref)
            cv_ref[...] = jnp.zeros_like(cv_ref)

        dsk_acc = jnp.zeros((1, LANES), F32)
        if nsub > 1:
            dk_ref[...] = ck_ref[...]
            dv_ref[...] = cv_ref[...]
        if dil > 1:
            @pl.when(i == 0)
            def _():
                kp_ref[...] = jnp.zeros_like(kp_ref)
                vp_ref[...] = jnp.zeros_like(vp_ref)

        for u, (rows, (src, prows), sub) in enumerate(_band_units(dil, nsub)):
            valid = valid_first if sub == 0 else valid_inner
            kpr, vpr = (kp_ref, vp_ref) if src == "prev" else (kc_ref, vc_ref)
            ck_u, cv_u = (ck_ref.at[u], cv_ref.at[u]) if dil > 1 else (None, None)
            for jp in range(ppk):
                cs = pl.ds(LANES * jp, LANES)
                ks = pl.ds(0, LANES) if kv_shared else cs
                kcur, vcur = kc_ref[rows, ks], vc_ref[rows, ks]
                if dil == 1:
                    kprev, vprev = kpr[prows, ks], vpr[prows, ks]
                else:
                    kprev, vprev = kp_ref[u, :, ks], vp_ref[u, :, ks]
                    kp_ref[u, :, ks] = kcur
                    vp_ref[u, :, ks] = vcur
                kn = jnp.concatenate([kprev, kcur], axis=0).astype(BF16)
                vcat = jnp.concatenate([vprev, vcur], axis=0).astype(BF16)
                dov = do_ref[rows, cs]
                if mixed:
                    dov = dov * w_ref[rows, cs]
                    shift = half_rows(dov * om_ref[rows, cs])
                else:
                    shift = half_rows(dov * o_ref[rows, cs])
                kes = [((2 * jp + e) // group) % 2 if kv_shared else e for e in range(2)]
                hidx = 2 * (pb * ppk + jp)
                qs = _stack_heads(q_ref[rows, cs], m0, kes).astype(BF16)
                dos = _stack_heads(dov, m0, kes).astype(BF16)
                lse = jnp.concatenate([ref[rows, :] if ppk == 1 else ref[jp, rows, :]
                                       for ref in (lc0_ref, lc1_ref)], axis=0)
                slope = jnp.where(top, slope_ref[hidx], slope_ref[hidx + 1])
                p = jnp.where(valid, jnp.exp(_dot_nt(qs, kn) * scale - slope * distf - lse), 0.0)
                ds = (p * (_dot_nt(dos, vcat) - shift)).astype(BF16)
                dqn = _unstack_heads(_dot(ds, kn), m0, kes) * scale
                dkn = _dot_tn(ds, qs) * scale
                dvv = _dot_tn(p.astype(BF16), dos)
                if has_sink:
                    sk = jnp.where(top, sink_ref[hidx], sink_ref[hidx + 1])
                    contrib = -jnp.exp(sk - lse) * shift * livef
                    for e in range(2):
                        tot = jnp.sum(contrib[e * BLK:(e + 1) * BLK], axis=0, keepdims=True)
                        dsk_acc = dsk_acc + jnp.where(lane == (2 * jp + e), tot, 0.0)
                dk_raw = dkn

                @pl.when(live)
                def _():
                    dq_ref[rows, cs] = dqn

                if dil > 1:
                    dk_ref[rows, cs] = ck_u[:, cs] + dk_raw[:BLK]
                    dv_ref[rows, cs] = cv_u[:, cs] + dvv[:BLK]
                    ck_u[:, cs] = dk_raw[BLK:]
                    cv_u[:, cs] = dvv[BLK:]
                    continue
                if nsub == 1:
                    dk_ref[rows, cs] = ck_ref[rows, cs] + dk_raw[:BLK]
                    dv_ref[rows, cs] = cv_ref[rows, cs] + dvv[:BLK]
                elif sub == 0:
                    last = pl.ds(RB - BLK, BLK)
                    dk_ref[last, cs] += dk_raw[:BLK]
                    dv_ref[last, cs] += dvv[:BLK]
                else:
                    ck_ref[prows, cs] += dk_raw[:BLK]
                    cv_ref[prows, cs] += dvv[:BLK]
                ck_ref[rows, cs] = dk_raw[BLK:]
                cv_ref[rows, cs] = dvv[BLK:]
        dsk_ref[...] += dsk_acc

    smem = pl.BlockSpec(memory_space=pltpu.SMEM)
    gspec = pl.BlockSpec((1, LANES), lambda p, i: (0, 0))

    def cur(i):
        return jnp.minimum(i, nb - 1)

    qspec = pl.BlockSpec((RB, qw), lambda p, i: (cur(i), p))
    dospec = pl.BlockSpec((RB, qw), lambda p, i: (cur(i), do_blk + p))
    kvout = pl.BlockSpec((RB, qw), lambda p, i: (jnp.maximum(i - 1, 0), p))
    in_specs = ([smem] * (2 if has_sink else 1) + _band_specs(dil, nsub, ppk, q_blk, k_blk, v_blk, kv_shared, nb)
                + [dospec] + [_head_col_spec(ppk, RB, cur)] * 2
                + ([qspec, qspec] if mixed else [qspec]))
    args = ([slopes] + ([sinks] if has_sink else []) + ([qkn] * 3 + [qkv] * 2 if dil == 1 else [qkn, qkn, qkv])
            + [do, lsec[0], lsec[1]]
            + ([w, omix] if mixed else [o]))
    full = jax.ShapeDtypeStruct((T, n_heads * HEAD_DIM), F32)
    row = jax.ShapeDtypeStruct((1, LANES), F32)
    return pl.pallas_call(
        body, grid=(npair // ppk, nb + 1), in_specs=in_specs,
        out_specs=[qspec, kvout, kvout, gspec],
        out_shape=[full, full, full, row],
        scratch_shapes=([pltpu.VMEM((RB, qw), F32)] * 2 if dil == 1
                        else [pltpu.VMEM((dil, BLK, qw), F32)] * 2 + [pltpu.VMEM((dil, BLK, qw), F32)] * 2),
        compiler_params=_cparams(2), name=name)(*args)


def mix_fwd(o1, o2, o3, l1, l2, l3, *, tm, name):
    T, C = o1.shape

    def body(o1r, o2r, o3r, l1r, l2r, l3r, o_ref, w1r, w2r, w3r):
        a, b, c = l1r[...], l2r[...], l3r[...]
        m = jnp.maximum(jnp.maximum(a, b), c)
        ea, eb, ec = jnp.exp(a - m), jnp.exp(b - m), jnp.exp(c - m)
        inv = 1.0 / (ea + eb + ec)
        wa, wb, wc = ea * inv, eb * inv, ec * inv
        o_ref[...] = wa * o1r[...] + wb * o2r[...] + wc * o3r[...]
        w1r[...] = wa
        w2r[...] = wb
        w3r[...] = wc

    spec = pl.BlockSpec((tm, C), lambda i: (i, 0))
    shp = jax.ShapeDtypeStruct((T, C), F32)
    return pl.pallas_call(body, grid=(T // tm,), in_specs=[spec] * 6, out_specs=[spec] * 4, out_shape=[shp] * 4,
                          compiler_params=_cparams(1), name=name)(o1, o2, o3, l1, l2, l3)


def _qk_norm_bwd(raw, dn, gain, m0):
    r = _head_rms(raw, m0)
    h = raw * r
    dh = dn * gain
    d_raw = r * (dh - h * (_half_sum(dh * h, m0) * (1.0 / HEAD_DIM)))
    return d_raw, jnp.sum(dn * h, axis=0, keepdims=True)


def _acc_rows(ref, val):
    @pl.when(pl.program_id(0) == 0)
    def _():
        ref[...] = val

    @pl.when(pl.program_id(0) > 0)
    def _():
        ref[...] += val


def assemble_odd(parts, qkv, q_gain2, k_gain2, *, tm, name):
    T, C = parts[0][0].shape
    nbk = C // LANES

    def body(*refs):
        qkv_ref, qg_ref, kg_ref, o_ref, dqg_ref, dkg_ref = refs[9:]
        m0 = _lane0()
        sums = [refs[j][...] + refs[3 + j][...] + refs[6 + j][...] for j in range(3)]
        o_ref[:, pl.ds(2 * C, C)] = sums[2]
        for j, (g_ref, acc_ref) in enumerate(((qg_ref, dqg_ref), (kg_ref, dkg_ref))):
            dgain = jnp.zeros((1, LANES), F32)
            for b in range(nbk):
                cols = pl.ds(C * j + LANES * b, LANES)
                d_raw, part = _qk_norm_bwd(qkv_ref[:, cols], sums[j][:, LANES * b:LANES * (b + 1)], g_ref[...], m0)
                o_ref[:, cols] = d_raw
                dgain = dgain + part
            _acc_rows(acc_ref, dgain)

    spec = pl.BlockSpec((tm, C), lambda i: (i, 0))
    gspec = pl.BlockSpec((1, LANES), lambda i: (0, 0))
    flat = [parts[p][j] for p in range(3) for j in range(3)]
    row = jax.ShapeDtypeStruct((1, LANES), F32)
    return pl.pallas_call(body, grid=(T // tm,),
                          in_specs=[spec] * 9 + [pl.BlockSpec((tm, 2 * C), lambda i: (i, 0)), gspec, gspec],
                          out_specs=[pl.BlockSpec((tm, 3 * C), lambda i: (i, 0)), gspec, gspec],
                          out_shape=[jax.ShapeDtypeStruct((T, 3 * C), F32), row, row],
                          compiler_params=_cparams(1), name=name)(*flat, qkv, q_gain2, k_gain2)


def assemble_even(dqa, dka4, dva4, dqb, dkb, dvb, qkv, q_gain2, k_gain2, *, tm, name):
    T = dqa.shape[0]
    W = 512
    QK = 768

    def body(dqa_r, dka_r, dva_r, dqb_r, dkb_r, dvb_r, qkv_ref, qg_ref, kg_ref, o_ref, dqg_ref, dkg_ref):
        m0 = _lane0()
        ka = dka_r[...]
        va = dva_r[...]
        dqn = dqa_r[...]
        dgain = jnp.zeros((1, LANES), F32)
        for b in range(W // LANES):
            cols = pl.ds(LANES * b, LANES)
            d_raw, part = _qk_norm_bwd(qkv_ref[:, cols], dqn[:, LANES * b:LANES * (b + 1)], qg_ref[...], m0)
            o_ref[:, cols] = d_raw
            dgain = dgain + part
        _acc_rows(dqg_ref, dgain)
        dkn = ka[:, 0:128] + ka[:, 128:256] + ka[:, 256:384] + ka[:, 384:512]
        d_raw, part = _qk_norm_bwd(qkv_ref[:, pl.ds(W, LANES)], dkn, kg_ref[...], m0)
        o_ref[:, pl.ds(W, LANES)] = d_raw
        _acc_rows(dkg_ref, part)
        o_ref[:, pl.ds(640, LANES)] = va[:, 0:128] + va[:, 128:256] + va[:, 256:384] + va[:, 384:512]
        o_ref[:, pl.ds(768, W)] = dqb_r[...]
        o_ref[:, pl.ds(1280, W)] = dkb_r[...]
        o_ref[:, pl.ds(1792, W)] = dvb_r[...]

    spec = pl.BlockSpec((tm, W), lambda i: (i, 0))
    gspec = pl.BlockSpec((1, LANES), lambda i: (0, 0))
    row = jax.ShapeDtypeStruct((1, LANES), F32)
    return pl.pallas_call(body, grid=(T // tm,),
                          in_specs=[spec] * 6 + [pl.BlockSpec((tm, QK), lambda i: (i, 0)), gspec, gspec],
                          out_specs=[pl.BlockSpec((tm, 2304), lambda i: (i, 0)), gspec, gspec],
                          out_shape=[jax.ShapeDtypeStruct((T, 2304), F32), row, row],
                          compiler_params=_cparams(1), name=name)(dqa, dka4, dva4, dqb, dkb, dvb, qkv, q_gain2, k_gain2)


STICK_T = 256
STICK_DEAD = -110.0


def _split_bf16(x):
    hi = x.astype(BF16)
    lo = (x - hi.astype(F32)).astype(BF16)
    return hi, lo


def _stick_logits(qm, kt, scale, diag):
    n = STICK_T
    row = lax.broadcasted_iota(jnp.int32, (n, n), 0)
    col = lax.broadcasted_iota(jnp.int32, (n, n), 1)
    mask = col < row + jnp.where(diag, 0, n)
    z = _dot_nt(qm, kt) * scale
    lneg = -(jnp.maximum(z, 0.0) + jnp.log(1.0 + jnp.exp(-jnp.abs(z))))
    lpos = z + lneg
    lk = jnp.where(mask, lneg, 0.0)
    return mask, lpos, lneg, lk


def _cumsum_mm(x, tri):
    hi, lo = _split_bf16(x)
    return _dot(hi, tri) + _dot(lo, tri)


def stick_fwd(qkv, *, q_blk, k_blk, v_blk, n_pairs, name, riders=None, rider_args=()):
    T = qkv.shape[0]
    n = STICK_T
    nq = T // n
    scale = HEAD_DIM ** -0.5
    nc = riders.n if riders is not None else 0
    n_steps = n_pairs * nq
    stage_at = (0, (5 * n_steps) // 8, (15 * n_steps) // 16, n_steps - 1)

    def body(*refs):
        q_ref, k_ref, v_ref = refs[:3]
        x_refs, o_ref = refs[3:3 + nc], refs[3 + nc]
        out_refs, sems = refs[4 + nc:4 + 2 * nc], refs[4 + 2 * nc:]
        i = pl.program_id(1)
        step_id = pl.program_id(0) * nq + i

        def ride(which):
            if riders is not None:
                @pl.when(step_id == stage_at[which])
                def _():
                    riders.stage(which, x_refs, out_refs, sems)

        ride(0)
        ride(1)
        m0 = _lane0()
        r2 = lax.broadcasted_iota(jnp.int32, (n, n), 0)
        c2 = lax.broadcasted_iota(jnp.int32, (n, n), 1)
        tri_after = (r2 > c2).astype(BF16)
        qv = q_ref[...]
        out = jnp.zeros((n, LANES), F32)
        for e in range(2):
            qm = _mask_half(qv, m0, e).astype(BF16)

            def alive(st):
                t, _, carry = st
                return (t <= i) & (jnp.max(carry) > STICK_DEAD)

            def step(st, e=e, qm=qm):
                t, acc, carry = st
                start = pl.multiple_of((i - t) * n, n)
                kt = k_ref[pl.ds(start, n), :].astype(BF16)
                vt = _mask_half(v_ref[pl.ds(start, n), :], m0, e).astype(BF16)
                mask, lpos, _, lk = _stick_logits(qm, kt, scale, t == 0)
                after = _cumsum_mm(lk, tri_after) + carry
                a = jnp.where(mask, jnp.exp(lpos + after), 0.0)
                acc = acc + _dot(a.astype(BF16), vt)
                carry = carry + jnp.sum(lk, axis=-1, keepdims=True)
                return t + 1, acc, carry

            _, acc, _ = lax.while_loop(alive, step, (jnp.int32(0), jnp.zeros((n, LANES), F32),
                                                     jnp.zeros((n, 1), F32)))
            out = out + acc
        o_ref[...] = out
        ride(2)
        ride(3)

    outs = pl.pallas_call(
        body, grid=(n_pairs, nq),
        in_specs=[pl.BlockSpec((n, LANES), lambda p, i: (i, q_blk + p)),
                  pl.BlockSpec((T, LANES), lambda p, i: (0, k_blk + p)),
                  pl.BlockSpec((T, LANES), lambda p, i: (0, v_blk + p))] + [_ANY] * nc,
        out_specs=[pl.BlockSpec((n, LANES), lambda p, i: (i, p))] + [_ANY] * nc,
        out_shape=[jax.ShapeDtypeStruct((T, n_pairs * LANES), F32)] + (riders.shapes if nc else []),
        scratch_shapes=riders.sems if nc else [],
        compiler_params=_cparams(2), name=name)(qkv, qkv, qkv, *rider_args)
    return outs[0], list(outs[1:])


def stick_bwd(qkv, do, *, q_blk, k_blk, v_blk, do_blk, n_pairs, name, riders=None, rider_args=()):
    T = qkv.shape[0]
    n = STICK_T
    nq = T // n
    scale = HEAD_DIM ** -0.5
    nc = riders.n if riders is not None else 0

    def body(*refs):
        q_ref, k_ref, v_ref, do_ref = refs[:4]
        pre_refs = refs[4:4 + nc]
        dq_ref, dk_ref, dv_ref = refs[4 + nc:7 + nc]
        land_refs = refs[7 + nc:7 + 2 * nc]
        a_keep, g_keep, s_keep = refs[7 + 2 * nc:10 + 2 * nc]
        sems = refs[10 + 2 * nc:]
        i = pl.program_id(1)
        first_step = (pl.program_id(0) == 0) & (i == 0)
        last_step = (pl.program_id(0) == n_pairs - 1) & (i == nq - 1)
        m0 = _lane0()
        r2 = lax.broadcasted_iota(jnp.int32, (n, n), 0)
        c2 = lax.broadcasted_iota(jnp.int32, (n, n), 1)
        tri_after = (r2 > c2).astype(BF16)
        tri_from = (r2 >= c2).astype(BF16)

        if riders is not None:
            @pl.when(first_step)
            def _():
                riders.start(pre_refs, land_refs, sems)

        @pl.when(i == 0)
        def _():
            dk_ref[...] = jnp.zeros_like(dk_ref)
            dv_ref[...] = jnp.zeros_like(dv_ref)

        qv = q_ref[...]
        dov = do_ref[...]
        dq_out = jnp.zeros((n, LANES), F32)
        for e in range(2):
            qm = _mask_half(qv, m0, e).astype(BF16)
            dom = _mask_half(dov, m0, e).astype(BF16)

            def alive(st):
                t, carry, _ = st
                return (t <= i) & (jnp.max(carry) > STICK_DEAD)

            def scan(st, qm=qm, dom=dom):
                t, carry, gtot = st
                start = pl.multiple_of((i - t) * n, n)
                kt = k_ref[pl.ds(start, n), :].astype(BF16)
                vt = v_ref[pl.ds(start, n), :].astype(BF16)
                mask, lpos, lneg, lk = _stick_logits(qm, kt, scale, t == 0)
                a = jnp.where(mask, jnp.exp(lpos + _cumsum_mm(lk, tri_after) + carry), 0.0)
                g = _dot_nt(dom, vt) * a
                a_keep[t] = a.astype(BF16)
                g_keep[t] = g
                s_keep[t] = jnp.exp(lneg).astype(BF16)
                return (t + 1, carry + jnp.sum(lk, axis=-1, keepdims=True),
                        gtot + jnp.sum(g, axis=-1, keepdims=True))

            z1 = jnp.zeros((n, 1), F32)
            n_live, _, gtot = lax.while_loop(alive, scan, (jnp.int32(0), z1, z1))

            def step(t, st, e=e, qm=qm, dom=dom, gtot=gtot):
                dq_acc, gright = st
                start = pl.multiple_of((i - t) * n, n)
                g = g_keep[t]
                sneg = s_keep[t].astype(F32)
                before = gtot - (_cumsum_mm(g, tri_from) + gright)
                mask = c2 < r2 + jnp.where(t == 0, 0, n)
                dz = jnp.where(mask, g * sneg - before * (1.0 - sneg), 0.0) * scale
                dzb = dz.astype(BF16)
                dq_acc = dq_acc + _dot(dzb, _mask_half(k_ref[pl.ds(start, n), :], m0, e).astype(BF16))
                dk_ref[pl.ds(start, n), :] += _dot_tn(dzb, qm)
                dv_ref[pl.ds(start, n), :] += _dot_tn(a_keep[t], dom)
                return dq_acc, gright + jnp.sum(g, axis=-1, keepdims=True)

            dq_acc, _ = lax.fori_loop(0, n_live, step, (jnp.zeros((n, LANES), F32), z1))
            dq_out = dq_out + dq_acc
        dq_ref[...] = dq_out

        if riders is not None:
            @pl.when(last_step)
            def _():
                riders.finish(pre_refs, land_refs, sems)

    tile = pl.BlockSpec((n, LANES), lambda p, i: (i, p))
    whole = pl.BlockSpec((T, LANES), lambda p, i: (0, p))
    shp = jax.ShapeDtypeStruct((T, n_pairs * LANES), F32)
    outs = pl.pallas_call(
        body, grid=(n_pairs, nq),
        in_specs=[pl.BlockSpec((n, LANES), lambda p, i: (i, q_blk + p)),
                  pl.BlockSpec((T, LANES), lambda p, i: (0, k_blk + p)),
                  pl.BlockSpec((T, LANES), lambda p, i: (0, v_blk + p)),
                  pl.BlockSpec((n, LANES), lambda p, i: (i, do_blk + p))] + [_ANY] * nc,
        out_specs=[tile, whole, whole] + [_ANY] * nc,
        out_shape=[shp, shp, shp] + (riders.shapes if nc else []),
        scratch_shapes=[pltpu.VMEM((nq, n, n), BF16), pltpu.VMEM((nq, n, n), F32), pltpu.VMEM((nq, n, n), BF16)]
        + (riders.sems if nc else []),
        compiler_params=_cparams(2), name=name)(qkv, qkv, qkv, do, *rider_args)
    return outs[0], outs[1], outs[2], list(outs[3:])


def _xnorm(x):
    r = lax.rsqrt(jnp.mean(x * x, axis=-1, keepdims=True) + RMS_EPS)
    return r, x * r


def xattn_fwd(qraw, kvraw, q_gain, k_gain, *, tm, name):
    T = qraw.shape[0]
    scale = X_HEAD_DIM ** -0.5
    W = X_HEADS * X_HEAD_DIM

    def body(q_ref, kv_ref, qg_ref, kg_ref, o_ref):
        for h in range(X_HEADS):
            cs = pl.ds(X_HEAD_DIM * h, X_HEAD_DIM)
            _, qh = _xnorm(q_ref[:, cs])
            _, kh = _xnorm(kv_ref[:, cs])
            qn = (qh * qg_ref[...]).astype(BF16)
            kn = (kh * kg_ref[...]).astype(BF16)
            v = kv_ref[:, pl.ds(W + X_HEAD_DIM * h, X_HEAD_DIM)].astype(BF16)
            s = _dot_nt(qn, kn) * scale
            m = jnp.max(s, axis=-1, keepdims=True)
            p = jnp.exp(s - m)
            p = p / jnp.sum(p, axis=-1, keepdims=True)
            o_ref[:, cs] = _dot(p.astype(BF16), v)

    gspec = pl.BlockSpec((1, X_HEAD_DIM), lambda i: (0, 0))
    return pl.pallas_call(
        body, grid=(T // tm,),
        in_specs=[pl.BlockSpec((tm, W), lambda i: (i, 0)), pl.BlockSpec((MEM_LEN, 2 * W), lambda i: (0, 0)),
                  gspec, gspec],
        out_specs=pl.BlockSpec((tm, W), lambda i: (i, 0)),
        out_shape=jax.ShapeDtypeStruct((T, W), F32),
        compiler_params=_cparams(1), name=name)(qraw, kvraw, q_gain, k_gain)


def xattn_bwd(qraw, kvraw, q_gain, k_gain, do, o, *, tm, name):
    T = qraw.shape[0]
    nt = T // tm
    scale = X_HEAD_DIM ** -0.5
    W = X_HEADS * X_HEAD_DIM

    def body(q_ref, kv_ref, qg_ref, kg_ref, do_ref, o_ref, dq_ref, dkv_ref, dqg_ref, dkg_ref, dkn_ref):
        i = pl.program_id(0)

        @pl.when(i == 0)
        def _():
            dkv_ref[...] = jnp.zeros_like(dkv_ref)
            dkn_ref[...] = jnp.zeros_like(dkn_ref)
            dqg_ref[...] = jnp.zeros_like(dqg_ref)
            dkg_ref[...] = jnp.zeros_like(dkg_ref)

        qg = qg_ref[...]
        kg = kg_ref[...]
        dqg_acc = jnp.zeros((1, X_HEAD_DIM), F32)
        for h in range(X_HEADS):
            cs = pl.ds(X_HEAD_DIM * h, X_HEAD_DIM)
            vs = pl.ds(W + X_HEAD_DIM * h, X_HEAD_DIM)
            rq, qh = _xnorm(q_ref[:, cs])
            _, kh = _xnorm(kv_ref[:, cs])
            qn = (qh * qg).astype(BF16)
            kn = (kh * kg).astype(BF16)
            v = kv_ref[:, vs].astype(BF16)
            s = _dot_nt(qn, kn) * scale
            m = jnp.max(s, axis=-1, keepdims=True)
            p = jnp.exp(s - m)
            p = p / jnp.sum(p, axis=-1, keepdims=True)
            dov = do_ref[:, cs]
            delta = jnp.sum(dov * o_ref[:, cs], axis=-1, keepdims=True)
            dob = dov.astype(BF16)
            ds = (p * (_dot_nt(dob, v) - delta)).astype(BF16)
            dqn = _dot(ds, kn) * scale
            dkn_ref[:, cs] += _dot_tn(ds, qn) * scale
            dkv_ref[:, vs] += _dot_tn(p.astype(BF16), dob)
            dqg_acc = dqg_acc + jnp.sum(dqn * qh, axis=0, keepdims=True)
            dqh = dqn * qg
            dq_ref[:, cs] = rq * (dqh - qh * jnp.mean(dqh * qh, axis=-1, keepdims=True))
        dqg_ref[...] += dqg_acc

        @pl.when(i == nt - 1)
        def _():
            dkg_acc = jnp.zeros((1, X_HEAD_DIM), F32)
            for h in range(X_HEADS):
                cs = pl.ds(X_HEAD_DIM * h, X_HEAD_DIM)
                rk, kh = _xnorm(kv_ref[:, cs])
                dkn = dkn_ref[:, cs]
                dkg_acc = dkg_acc + jnp.sum(dkn * kh, axis=0, keepdims=True)
                dkh = dkn * kg
                dkv_ref[:, cs] = rk * (dkh - kh * jnp.mean(dkh * kh, axis=-1, keepdims=True))
            dkg_ref[...] = dkg_acc

    gspec = pl.BlockSpec((1, X_HEAD_DIM), lambda i: (0, 0))
    tile = pl.BlockSpec((tm, W), lambda i: (i, 0))
    kvspec = pl.BlockSpec((MEM_LEN, 2 * W), lambda i: (0, 0))
    grow = jax.ShapeDtypeStruct((1, X_HEAD_DIM), F32)
    return pl.pallas_call(
        body, grid=(nt,), in_specs=[tile, kvspec, gspec, gspec, tile, tile],
        out_specs=[tile, kvspec, gspec, gspec],
        out_shape=[jax.ShapeDtypeStruct((T, W), F32), jax.ShapeDtypeStruct((MEM_LEN, 2 * W), F32), grow, grow],
        scratch_shapes=[pltpu.VMEM((MEM_LEN, W), F32)],
        compiler_params=_cparams(1), name=name)(qraw, kvraw, q_gain, k_gain, do, o)


_ANY = pl.BlockSpec(memory_space=pl.ANY)


def _my_pos():
    return lax.axis_index("x"), lax.axis_index("y"), lax.axis_index("c")


def _pieces(arrays, chunks):
    out = []
    for a, (arr, n) in enumerate(zip(arrays, chunks)):
        rc = arr.shape[-2] // n
        out += [(a, pl.ds(ch * rc, rc)) for ch in range(n)]
    return out


class GatherBlocks:
    N_STAGES = 4

    def __init__(self, blks, chunks):
        self.shapes = [jax.ShapeDtypeStruct((N_DEV,) + b.shape, b.dtype) for b in blks]
        self.n = len(blks)
        self.pieces = _pieces(blks, chunks)
        n_p = len(self.pieces)
        self.sems = [pltpu.SemaphoreType.DMA((7 * n_p,)), pltpu.SemaphoreType.DMA((7 * n_p,)),
                     pltpu.SemaphoreType.DMA((n_p,))]

    def stage(self, which, x_refs, out_refs, sems):
        send_sems, recv_sems, local_sems = sems
        pieces, n_p = self.pieces, len(self.pieces)
        x, y, c = _my_pos()
        me, sibling = (x, y, c), (x, y, 1 - c)
        chips = [(1 - x, y), (x, 1 - y), (1 - x, 1 - y)]
        xn, yn, dg = [(*chip, c) for chip in chips]
        ps = range(n_p)

        def slot(block, p):
            px, py, pc = block
            a, rows = pieces[p]
            return out_refs[a].at[4 * px + 2 * py + pc, rows]

        def own(p):
            a, rows = pieces[p]
            return x_refs[a].at[rows]

        def copy(k, p, block, to, from_input=False):
            return pltpu.make_async_remote_copy(
                src_ref=own(p) if from_input else slot(block, p), dst_ref=slot(block, p),
                send_sem=send_sems.at[k * n_p + p], recv_sem=recv_sems.at[k * n_p + p],
                device_id=to, device_id_type=MESH)

        mine = [pltpu.make_async_copy(own(p), slot(me, p), local_sems.at[p]) for p in ps]
        first = [copy(k, p, me, to, from_input=True) for p in ps for k, to in ((1, xn), (2, yn), (0, sibling))]
        on_x = [copy(3, p, xn, yn) for p in ps if p % 2 == 0] + [copy(4, p, xn, sibling) for p in ps]
        on_y = [copy(3, p, yn, xn) for p in ps if p % 2 == 1] + [copy(5, p, yn, sibling) for p in ps]
        on_d = [copy(6, p, dg, sibling) for p in ps]
        if which == 0:
            for cp in first + mine:
                cp.start()
        elif which == 1:
            for p in ps:
                copy(1, p, xn, me).wait_recv()
                if p % 2 == 0:
                    copy(3, p, xn, yn).start()
                copy(4, p, xn, sibling).start()
                copy(2, p, yn, me).wait_recv()
                if p % 2 == 1:
                    copy(3, p, yn, xn).start()
                copy(5, p, yn, sibling).start()
        elif which == 2:
            for p in ps:
                copy(3, p, dg, me).wait_recv()
                copy(6, p, dg, sibling).start()
        else:
            for p in ps:
                copy(0, p, sibling, me).wait_recv()
            for k, chip in zip((4, 5, 6), chips):
                for p in ps:
                    copy(k, p, (*chip, 1 - c), me).wait_recv()
            for cp in first + on_x + on_y + on_d:
                cp.wait_send()
            for cp in mine:
                cp.wait()


def gather_blocks(blks, chunks, *, name):
    gb = GatherBlocks(blks, chunks)
    n = gb.n

    def body(*refs):
        x_refs, out_refs, sems = refs[:n], refs[n:2 * n], refs[2 * n:]
        for which in range(gb.N_STAGES):
            gb.stage(which, x_refs, out_refs, sems)

    return pl.pallas_call(body, out_shape=gb.shapes, in_specs=[_ANY] * n, out_specs=[_ANY] * n,
                          scratch_shapes=gb.sems, name=name)(*blks)


def gather_small(small, *, name):
    S, C = small.shape

    def body(s_ref, out_ref, send_sems, recv_sems, local_sem):
        x, y, c = _my_pos()
        my_id = 4 * x + 2 * y + c

        def copy(k, slot):
            px, py, pc = x ^ ((k >> 2) & 1), y ^ ((k >> 1) & 1), c ^ (k & 1)
            dst = my_id if slot == "mine" else 4 * px + 2 * py + pc
            return pltpu.make_async_remote_copy(
                src_ref=s_ref, dst_ref=out_ref.at[dst], send_sem=send_sems.at[k - 1], recv_sem=recv_sems.at[k - 1],
                device_id=(px, py, pc), device_id_type=MESH)

        own = pltpu.make_async_copy(s_ref, out_ref.at[my_id], local_sem)
        own.start()
        sends = [copy(k, "mine") for k in range(1, N_DEV)]
        for cp in sends:
            cp.start()
        for k in range(1, N_DEV):
            copy(k, "theirs").wait_recv()
        for cp in sends:
            cp.wait_send()
        own.wait()

    dma7 = pltpu.SemaphoreType.DMA((7,))
    return pl.pallas_call(
        body, out_shape=jax.ShapeDtypeStruct((N_DEV, S, C), small.dtype), in_specs=[_ANY], out_specs=_ANY,
        scratch_shapes=[dma7, dma7, pltpu.SemaphoreType.DMA], name=name)(small)


class PairExchange:
    def __init__(self, bigs, chunks):
        self.shapes = [jax.ShapeDtypeStruct((4,) + b.shape[1:], b.dtype) for b in bigs]
        self.n = len(bigs)
        self.pieces = _pieces(bigs, chunks)
        n_p = len(self.pieces)
        self.sems = [pltpu.SemaphoreType.DMA((4 * n_p,)), pltpu.SemaphoreType.DMA((4 * n_p,))]

    def _copies(self, big_refs, out_refs, sems):
        send_sems, recv_sems = sems
        n_p = len(self.pieces)
        x, y, c = _my_pos()

        def copy(b, p):
            a, rows = self.pieces[p]
            return pltpu.make_async_remote_copy(
                src_ref=big_refs[a].at[2 * b + (1 - c), rows], dst_ref=out_refs[a].at[b, rows],
                send_sem=send_sems.at[b * n_p + p], recv_sem=recv_sems.at[b * n_p + p],
                device_id=(x, y, 1 - c), device_id_type=MESH)

        return [copy(b, p) for b in range(4) for p in range(n_p)]

    def start(self, big_refs, out_refs, sems):
        for cp in self._copies(big_refs, out_refs, sems):
            cp.start()

    def finish(self, big_refs, out_refs, sems):
        cps = self._copies(big_refs, out_refs, sems)
        for cp in cps:
            cp.wait_recv()
        for cp in cps:
            cp.wait_send()


def _standalone(exchange, args, name):
    n = exchange.n

    def body(*refs):
        exchange.start(refs[:n], refs[n:2 * n], refs[2 * n:])
        exchange.finish(refs[:n], refs[n:2 * n], refs[2 * n:])

    return pl.pallas_call(body, out_shape=exchange.shapes, in_specs=[_ANY] * n, out_specs=[_ANY] * n,
                          scratch_shapes=exchange.sems, name=name)(*args)


class Riding:
    def __init__(self, riders):
        self.riders = [(ex, list(args)) for ex, args in riders]
        self.args = [a for _, args in self.riders for a in args]
        self.in_specs = [_ANY] * len(self.args)
        self.out_shapes = [s for ex, _ in self.riders for s in ex.shapes]
        self.out_specs = [_ANY] * len(self.out_shapes)
        self.scratch = [s for ex, _ in self.riders for s in ex.sems]

    def wrap(self, body, n_in, n_out, n_scratch, is_first, is_last):
        def wrapped(*refs):
            k = 0
            core = list(refs[:n_in])
            k = n_in
            r_in = []
            for ex, _ in self.riders:
                r_in.append(refs[k:k + ex.n])
                k += ex.n
            core += refs[k:k + n_out]
            k += n_out
            r_out = []
            for ex, _ in self.riders:
                r_out.append(refs[k:k + ex.n])
                k += ex.n
            core += refs[k:k + n_scratch]
            k += n_scratch
            r_sem = []
            for ex, _ in self.riders:
                r_sem.append(refs[k:k + len(ex.sems)])
                k += len(ex.sems)

            @pl.when(is_first())
            def _():
                for (ex, _), a, b, s in zip(self.riders, r_in, r_out, r_sem):
                    ex.start(a, b, s)

            body(*core)

            @pl.when(is_last())
            def _():
                for (ex, _), a, b, s in zip(self.riders, r_in, r_out, r_sem):
                    ex.finish(a, b, s)

        return wrapped

    def split(self, outs, n_out):
        core, rest, per = list(outs[:n_out]), list(outs[n_out:]), []
        for ex, _ in self.riders:
            per.append(rest[:ex.n])
            rest = rest[ex.n:]
        return core, per


def pair_sum(big, sib, c, *, tr, name):
    _, R, C = big.shape

    def body(c_ref, a_ref, s_ref, o_ref):
        o_ref[...] = (a_ref[...].astype(F32) + s_ref[...].astype(F32)).astype(o_ref.dtype)

    grid_spec = pltpu.PrefetchScalarGridSpec(
        num_scalar_prefetch=1, grid=(4, R // tr),
        in_specs=[pl.BlockSpec((None, tr, C), lambda b, i, c_ref: (2 * b + c_ref[0], i, 0)),
                  pl.BlockSpec((None, tr, C), lambda b, i, c_ref: (b, i, 0))],
        out_specs=pl.BlockSpec((None, tr, C), lambda b, i, c_ref: (b, i, 0)))
    return pl.pallas_call(body, grid_spec=grid_spec, out_shape=jax.ShapeDtypeStruct((4, R, C), big.dtype),
                          compiler_params=_cparams(2), name=name)(c.reshape(1).astype(jnp.int32), big, sib)


class ChipScatter:
    def __init__(self, pres, chunks):
        self.shapes = [jax.ShapeDtypeStruct(p.shape, p.dtype) for p in pres]
        self.n = len(pres)
        self.pieces = _pieces(pres, chunks)
        n_p = len(self.pieces)
        self.sems = [pltpu.SemaphoreType.DMA((3 * n_p,)), pltpu.SemaphoreType.DMA((3 * n_p,)),
                     pltpu.SemaphoreType.DMA((n_p,))]

    def _copies(self, pre_refs, out_refs, sems):
        send_sems, recv_sems, local_sems = sems
        n_p = len(self.pieces)
        x, y, c = _my_pos()
        my_chip = 2 * x + y
        chips = [(1 - x, y), (x, 1 - y), (1 - x, 1 - y)]

        def copy(j, p, slot):
            px, py = chips[j]
            a, rows = self.pieces[p]
            src_slot, dst_slot = (2 * px + py, my_chip) if slot == "mine" else (my_chip, 2 * px + py)
            return pltpu.make_async_remote_copy(
                src_ref=pre_refs[a].at[src_slot, rows], dst_ref=out_refs[a].at[dst_slot, rows],
                send_sem=send_sems.at[j * n_p + p], recv_sem=recv_sems.at[j * n_p + p],
                device_id=(px, py, c), device_id_type=MESH)

        own = [pltpu.make_async_copy(pre_refs[a].at[my_chip, rows], out_refs[a].at[my_chip, rows], local_sems.at[p])
               for p, (a, rows) in enumerate(self.pieces)]
        sends = [copy(j, p, "mine") for j in range(3) for p in range(n_p)]
        recvs = [copy(j, p, "theirs") for j in range(3) for p in range(n_p)]
        return own, sends, recvs

    def start(self, pre_refs, out_refs, sems):
        own, sends, _ = self._copies(pre_refs, out_refs, sems)
        for cp in sends + own:
            cp.start()

    def finish(self, pre_refs, out_refs, sems):
        own, sends, recvs = self._copies(pre_refs, out_refs, sems)
        for cp in recvs:
            cp.wait_recv()
        for cp in sends:
            cp.wait_send()
        for cp in own:
            cp.wait()


def chip_scatter(pres, chunks, *, name):
    cs = ChipScatter(pres, chunks)
    n = cs.n

    def body(*refs):
        pre_refs, out_refs, sems = refs[:n], refs[n:2 * n], refs[2 * n:]
        cs.start(pre_refs, out_refs, sems)
        cs.finish(pre_refs, out_refs, sems)

    return pl.pallas_call(body, out_shape=cs.shapes, in_specs=[_ANY] * n, out_specs=[_ANY] * n,
                          scratch_shapes=cs.sems, name=name)(*pres)


def sibling_send(blks, chunks, *, name):
    n = len(blks)
    pieces = _pieces(blks, chunks)
    n_p = len(pieces)

    def body(*refs):
        x_refs, out_refs = refs[:n], refs[n:2 * n]
        send_sems, recv_sems = refs[2 * n:]
        x, y, c = _my_pos()
        cps = [pltpu.make_async_remote_copy(
            src_ref=x_refs[a].at[rows], dst_ref=out_refs[a].at[rows], send_sem=send_sems.at[p],
            recv_sem=recv_sems.at[p], device_id=(x, y, 1 - c), device_id_type=MESH)
            for p, (a, rows) in enumerate(pieces)]
        for cp in cps:
            cp.start()
        for cp in cps:
            cp.wait_recv()
        for cp in cps:
            cp.wait_send()

    return pl.pallas_call(
        body, out_shape=[jax.ShapeDtypeStruct(b.shape, b.dtype) for b in blks],
        in_specs=[_ANY] * n, out_specs=[_ANY] * n,
        scratch_shapes=[pltpu.SemaphoreType.DMA((n_p,)), pltpu.SemaphoreType.DMA((n_p,))],
        name=name)(*blks)


def reduce_slots(land, *, tr, name):
    n, R, C = land.shape

    def body(l_ref, o_ref):
        acc = l_ref[0].astype(F32)
        for s in range(1, n):
            acc = acc + l_ref[s].astype(F32)
        o_ref[...] = acc

    return pl.pallas_call(
        body, grid=(R // tr,), in_specs=[pl.BlockSpec((n, tr, C), lambda i: (0, i, 0))],
        out_specs=pl.BlockSpec((tr, C), lambda i: (i, 0)), out_shape=jax.ShapeDtypeStruct((R, C), F32),
        compiler_params=_cparams(1), name=name)(land)


TM = 512


def _tk(d):
    return min(d.shape[0], 1024)


def ffn_fwd_fused(x, g, wgu, wd, *, tm, name, riders=None, rider_args=()):
    T, Dm = x.shape
    nb, _, cb = wgu.shape
    nh = nb // 2
    Fd = nh * cb
    nc = riders.n if riders is not None else 0
    n_steps = T // tm
    stage_at = (0, n_steps // 2, (13 * n_steps) // 16, n_steps - 1)

    def body(*refs):
        x_ref, g_ref, wgu_ref, wd_ref = refs[:4]
        r_in = refs[4:4 + nc]
        o_ref, gu_ref, h_ref = refs[4 + nc:7 + nc]
        r_out, sems = refs[7 + nc:7 + 2 * nc], refs[7 + 2 * nc:]

        def ride(which):
            if riders is not None:
                @pl.when(pl.program_id(0) == stage_at[which])
                def _():
                    riders.stage(which, r_in, r_out, sems)

        ride(0)
        ride(1)
        xv = x_ref[...]
        r = lax.rsqrt(jnp.mean(xv * xv, axis=-1, keepdims=True) + RMS_EPS)
        hb = (xv * r * g_ref[...]).astype(BF16)
        h_ref[...] = hb
        acc = jnp.zeros((tm, Dm), F32)
        for jj in range(nh):
            cols = pl.ds(cb * jj, cb)
            gate = _dot(hb, wgu_ref[jj]).astype(BF16)
            up = _dot(hb, wgu_ref[nh + jj]).astype(BF16)
            gu_ref[0, :, cols] = gate
            gu_ref[1, :, cols] = up
            gv = gate.astype(F32)
            act = (gv * _sigmoid(gv) * up.astype(F32)).astype(BF16)
            acc = acc + _dot(act, wd_ref[cols, :])
        o_ref[...] = xv + 0.5 * acc
        ride(2)
        ride(3)

    outs = pl.pallas_call(
        body, grid=(n_steps,),
        in_specs=[pl.BlockSpec((tm, Dm), lambda i: (i, 0)), pl.BlockSpec((1, Dm), lambda i: (0, 0)),
                  pl.BlockSpec((nb, Dm, cb), lambda i: (0, 0, 0)), pl.BlockSpec((Fd, Dm), lambda i: (0, 0))]
        + [_ANY] * nc,
        out_specs=[pl.BlockSpec((tm, Dm), lambda i: (i, 0)), pl.BlockSpec((2, tm, Fd), lambda i: (0, i, 0)),
                   pl.BlockSpec((tm, Dm), lambda i: (i, 0))] + [_ANY] * nc,
        out_shape=[jax.ShapeDtypeStruct((T, Dm), F32), jax.ShapeDtypeStruct((2, T, Fd), BF16),
                   jax.ShapeDtypeStruct((T, Dm), BF16)] + (riders.shapes if nc else []),
        scratch_shapes=riders.sems if nc else [],
        compiler_params=_cparams(1), name=name)(x, g, wgu, wd, *rider_args)
    return outs[0], outs[1], outs[2], list(outs[3:])


def ffn_fwd(x, g, wgu, wd, tag, riders=None, rider_args=()):
    xo, gu, h, rode = ffn_fwd_fused(x, g, wgu, wd, tm=256, name=f"{tag}_fwd", riders=riders, rider_args=rider_args)
    return xo, (x, gu, h), rode


def ffn_bwd(d, saved, g, wgu, wd, tag, ride_bact=None, ride_dwgu=None, before_dx=None):
    x, gu, h = saved
    dgu, dwd, *rode_a = ffn_bwd_act(d, wd, gu, tm=TM, tn=1408, name=f"{tag}_bact", riding=ride_bact)
    dwgu = mm_tn(h, dgu, scale=1.0, a_split=False, b_split=True, tm=TM, tn=1408, tk=_tk(d), out_blocked=True,
                 name=f"{tag}_dwgu", riding=ride_dwgu)
    rode_g = []
    if ride_dwgu is not None:
        dwgu, *rode_g = dwgu
    riding = before_dx(dwgu, dwd) if before_dx is not None else None
    dx, dg, *rode_x = mm_nt_normbwd(dgu, wgu, x, g, d, a_split=True, tm=_tk(d), tk=1408, name=f"{tag}_dx",
                                    riding=riding)
    return dx, dg, dwgu, dwd, (rode_a, rode_g, rode_x)


def _tile2(v):
    return jnp.concatenate([v, v], axis=-1).reshape(1, LANES)


def _fold2(v):
    return v[:, :HEAD_DIM] + v[:, HEAD_DIM:]


EVEN = dict(dil=1, nsub=2, ppk=4, q_blk=0, k_blk=4, v_blk=5, n_heads=A_Q_HEADS, group=A_GROUP, max_dist=A_WINDOW - 1)
STICK = dict(q_blk=6, k_blk=10, v_blk=14, n_pairs=4)


def _odd_cfg(dil):
    return dict(dil=dil, nsub=4 if dil == 1 else 1, ppk=1, q_blk=0, k_blk=8, v_blk=16, n_heads=C_HEADS, group=1,
                max_dist=BLK)


def even_fwd(x, g, win, qg, kg, sinks, wout, tag, riders=None, rider_args=()):
    qkv, h = norm_matmul(x, g, win, tm=_tk(x), tn=1152, split=False, name=f"{tag}_in")
    qg2, kg2 = _tile2(qg), _tile2(kg)
    slopes = jnp.asarray(_alibi(A_Q_HEADS), F32)
    qkn = qk_norm(qkv, qg2, kg2, width=768, steps=1, n_q=4, tm=TM, name=f"{tag}_qkn")
    oa, _, *lse = banded_fwd(qkn, qkv, slopes, sinks, name=f"{tag}_swa", **EVEN)
    ob, rode = stick_fwd(qkv, name=f"{tag}_stick", riders=riders, rider_args=rider_args, **STICK)
    o = jnp.concatenate([oa, ob], axis=1)
    xo = mm_nn(o, wout, res=x, tm=TM, tn=D_MODEL, tk=D_MODEL, name=f"{tag}_out")
    return xo, (x, qkv, qkn, h, oa, lse, o), rode


def even_bwd(d, saved, g, win, qg, kg, sinks, wout, tag, riders=None, rider_args=(), before_dx=None):
    x, qkv, qkn, h, oa, lse, o = saved
    qg2, kg2 = _tile2(qg), _tile2(kg)
    slopes = jnp.asarray(_alibi(A_Q_HEADS), F32)
    dwout = mm_tn(o, d, scale=1.0, a_split=False, b_split=False, tm=D_MODEL, tn=D_MODEL, tk=_tk(d), name=f"{tag}_dwout")
    do = mm_nt(d, wout, tm=TM, tn=D_MODEL, tk=D_MODEL, name=f"{tag}_do")
    dqa, dka4, dva4, dsk = banded_bwd(qkn, qkv, slopes, sinks, do, oa, lse, None, None,
                                      do_blk=0, name=f"{tag}_swa_b", **EVEN)
    dqb, dkb, dvb, rode = stick_bwd(qkv, do, do_blk=4, name=f"{tag}_stick_b", riders=riders, rider_args=rider_args,
                                    **STICK)
    dqkv, dqg, dkg = assemble_even(dqa, dka4, dva4, dqb, dkb, dvb, qkv, qg2, kg2, tm=TM, name=f"{tag}_asm")
    dwin = mm_tn(h, dqkv, scale=1.0, a_split=False, b_split=False, tm=D_MODEL, tn=1152, tk=_tk(d), name=f"{tag}_dwin")
    riding = before_dx(dwin, dwout) if before_dx is not None else None
    dx, dg, *rode_x = mm_nt_normbwd(dqkv, win, x, g, d, a_split=False, tm=TM, tk=1152, name=f"{tag}_dx", riding=riding)
    return dx, dg, dwin, _fold2(dqg), _fold2(dkg), dsk[:, :A_Q_HEADS], dwout, (rode, rode_x)


def odd_fwd(x, g, win, qg, kg, wout, tag):
    qkv, h = norm_matmul(x, g, win, tm=_tk(x), tn=768, split=False, name=f"{tag}_in")
    qg2, kg2 = _tile2(qg), _tile2(kg)
    qkn = qk_norm(qkv, qg2, kg2, width=D_MODEL, steps=2, n_q=8, tm=TM, name=f"{tag}_qkn")
    outs = []
    for p, (window, dil) in enumerate(C_PATTERNS):
        slopes = jnp.asarray(_alibi(C_HEADS), F32) * float(dil)
        outs.append(banded_fwd(qkn, qkv, slopes, None, name=f"{tag}_dil{p}", **_odd_cfg(dil)))
    o, w1, w2, w3 = mix_fwd(outs[0][0], outs[1][0], outs[2][0], outs[0][1], outs[1][1], outs[2][1],
                            tm=TM, name=f"{tag}_mix")
    xo = mm_nn(o, wout, res=x, tm=TM, tn=D_MODEL, tk=D_MODEL, name=f"{tag}_out")
    return xo, (x, qkv, qkn, h, outs, (w1, w2, w3), o)


def odd_bwd(d, saved, g, win, qg, kg, wout, tag):
    x, qkv, qkn, h, outs, ws, o = saved
    qg2, kg2 = _tile2(qg), _tile2(kg)
    dwout = mm_tn(o, d, scale=1.0, a_split=False, b_split=False, tm=D_MODEL, tn=D_MODEL, tk=_tk(d), name=f"{tag}_dwout")
    do = mm_nt(d, wout, tm=TM, tn=D_MODEL, tk=D_MODEL, name=f"{tag}_do")
    parts = []
    for p, (window, dil) in enumerate(C_PATTERNS):
        slopes = jnp.asarray(_alibi(C_HEADS), F32) * float(dil)
        dq, dk, dv, _ = banded_bwd(qkn, qkv, slopes, None, do, None, outs[p][2:], ws[p], o,
                                   do_blk=0, name=f"{tag}_dil{p}_b", **_odd_cfg(dil))
        parts.append((dq, dk, dv))
    dqkv, dqg, dkg = assemble_odd(parts, qkv, qg2, kg2, tm=256, name=f"{tag}_asm")
    dwin = mm_tn(h, dqkv, scale=1.0, a_split=False, b_split=False, tm=TM, tn=768, tk=_tk(d), out_blocked=True,
                 name=f"{tag}_dwin")
    dx, dg = mm_nt_normbwd(dqkv, win, x, g, d, a_split=False, tm=TM, tk=768, name=f"{tag}_dx")
    return dx, dg, dwin, _fold2(dqg), _fold2(dkg), dwout


def xa_fwd(x, mem, g, gm, wq, wkv, qg, kg, wo, tag):
    qraw, h = norm_matmul(x, g, wq, tm=TM, tn=D_MODEL, split=False, name=f"{tag}_q")
    kvraw, hm = norm_matmul(mem, gm, wkv, tm=MEM_LEN, tn=512, split=False, name=f"{tag}_kv")
    o = xattn_fwd(qraw, kvraw, qg, kg, tm=TM, name=f"{tag}_att")
    xo = mm_nn(o, wo, res=x, tm=TM, tn=D_MODEL, tk=D_MODEL, name=f"{tag}_o")
    return xo, (x, qraw, h, kvraw, hm, o)


def xa_bwd(d, saved, mem, g, gm, wq, wkv, qg, kg, wo, tag):
    x, qraw, h, kvraw, hm, o = saved
    dwo = mm_tn(o, d, scale=1.0, a_split=False, b_split=False, tm=D_MODEL, tn=D_MODEL, tk=_tk(d), name=f"{tag}_dwo")
    do = mm_nt(d, wo, tm=TM, tn=D_MODEL, tk=D_MODEL, name=f"{tag}_do")
    dq, dkv, dqg, dkg = xattn_bwd(qraw, kvraw, qg, kg, do, o, tm=TM, name=f"{tag}_att_b")
    dwq = mm_tn(h, dq, scale=1.0, a_split=False, b_split=False, tm=D_MODEL, tn=D_MODEL, tk=_tk(d), name=f"{tag}_dwq")
    dx, dg = mm_nt_normbwd(dq, wq, x, g, d, a_split=False, tm=TM, tk=D_MODEL, name=f"{tag}_dx")
    dwkv = mm_tn(hm, dkv, scale=1.0, a_split=False, b_split=False, tm=TM, tn=512, tk=MEM_LEN, out_blocked=True,
                 name=f"{tag}_dwkv")
    _, dgm = mm_nt_normbwd(dkv, wkv, mem, gm, None, a_split=False, tm=MEM_LEN, tk=512, name=f"{tag}_dmem")
    return dx, dg, dgm, dwq, dwkv, dqg, dkg, dwo


MATS = (("ffn1_w_gu", 1), ("ffn1_w_down", 0), ("ev_w_in", 1), ("ev_w_out", 0), ("od_w_in", 1), ("od_w_out", 0),
        ("xa_w_q", 0), ("xa_w_kv", 1), ("xa_w_o", 0), ("ffn2_w_gu", 1), ("ffn2_w_down", 0))
SMALLS = ("ffn1_norm", "mix_norm", "ev_q_gain", "ev_k_gain", "ev_sinks", "od_q_gain", "od_k_gain", "xa_norm",
          "xa_mem_norm", "xa_q_gain", "xa_k_gain", "ffn2_norm")
WEIGHTS = ("ffn1_norm", "ffn1_w_gu", "ffn1_w_down", "mix_norm", "ev_w_in", "ev_q_gain", "ev_k_gain", "ev_sinks",
           "ev_w_out", "od_w_in", "od_q_gain", "od_k_gain", "od_w_out", "xa_norm", "xa_mem_norm", "xa_w_q",
           "xa_w_kv", "xa_q_gain", "xa_k_gain", "xa_w_o", "ffn2_norm", "ffn2_w_gu", "ffn2_w_down")
SMALL_ROWS = 16
LAYER_GROUPS = (
    (((("ffn1_w_gu", 0), ("ffn2_w_gu", 0)), 4, 512),
     ((("ffn1_w_down", 0), ("ffn2_w_down", 0)), 2, 352),
     ((("ev_w_out", 0), ("xa_w_q", 0), ("xa_w_o", 0)), 1, 384),
     ((("xa_w_kv", 0),), 1, 512),
     ((("ev_w_in", 0),), 1, 512)),
    (((("ffn1_w_gu", 1), ("ffn2_w_gu", 1)), 4, 512),
     ((("ffn1_w_down", 1), ("ffn2_w_down", 1)), 2, 352),
     ((("od_w_out", 0), ("xa_w_q", 1), ("xa_w_o", 1)), 1, 384),
     ((("xa_w_kv", 1),), 1, 512),
     ((("od_w_in", 0),), 1, 512)),
)
GATHER_FIRST = (((("ffn1_w_gu", 0),), 2, 512), ((("ffn1_w_down", 0),), 1, 352), ((("ev_w_out", 0),), 1, 128),
                ((("ev_w_in", 0),), 1, 512))
GATHER_IN_FFN1 = (((("ffn2_w_gu", 0),), 2, 512), ((("ffn2_w_down", 0),), 1, 352))
GATHER_IN_STICK = (((("ffn1_w_gu", 1),), 2, 512), ((("ffn1_w_down", 1),), 1, 352), ((("od_w_out", 0),), 1, 128),
                   ((("od_w_in", 0),), 1, 512),
                   ((("xa_w_q", 0), ("xa_w_o", 0), ("xa_w_q", 1), ("xa_w_o", 1)), 1, 512),
                   ((("xa_w_kv", 0), ("xa_w_kv", 1)), 1, 512))
GATHER_IN_FFN2 = (((("ffn2_w_gu", 1),), 2, 512), ((("ffn2_w_down", 1),), 1, 352))
ROUNDS = {
    "1": LAYER_GROUPS[1],
    "0a": (((("ffn2_w_gu", 0),), 2, 512), ((("ffn2_w_down", 0),), 1, 352)) + LAYER_GROUPS[0][2:],
    "0b": (((("ffn1_w_gu", 0),), 2, 512), ((("ffn1_w_down", 0),), 1, 352)),
}


def _chunks_of(groups):
    return tuple(g[1] for g in groups)
COL_SHARDED = {name for name, axis in MATS if axis == 1}
BLOCKED = {"ffn1_w_gu", "ffn2_w_gu", "xa_w_kv", "od_w_in"}


def group_halves(shards, c, groups):
    out = []
    for members, _, _ in groups:
        halves = []
        for name, layer in members:
            _, r, cc = shards[name].shape
            half = lax.dynamic_index_in_dim(shards[name][layer].reshape(2, r // 2, cc), c, 0, keepdims=False)
            halves.append(half.astype(BF16))
        out.append(jnp.concatenate(halves, axis=0))
    return out


def full_weights(gathered, shards, groups):
    full = {}
    for (members, _, _), arr in zip(groups, gathered):
        for w, (name, layer) in enumerate(members):
            _, r, cc = shards[name].shape
            piece = arr[:, w * (r // 2):(w + 1) * (r // 2)].reshape(4, r, cc)
            if name not in COL_SHARDED:
                piece = piece.reshape(4 * r, cc)
            elif name not in BLOCKED:
                piece = piece.transpose(1, 0, 2).reshape(r, 4 * cc)
            full[(name, layer)] = piece
    return full


def group_grads(grads, shards, groups):
    out = []
    for members, _, _ in groups:
        parts = []
        for name, layer in members:
            _, r, cc = shards[name].shape
            gfull = grads[(name, layer)]
            if name in COL_SHARDED and name not in BLOCKED:
                gfull = gfull.reshape(2, r // 2, 4, cc).transpose(2, 0, 1, 3)
            parts.append(gfull.reshape(N_DEV, r // 2, cc))
        out.append(jnp.concatenate(parts, axis=1))
    return out


def shard_grads(mine, theirs, c, shards, groups):
    per = {}
    for (members, _, _), a, b in zip(groups, mine, theirs):
        for w, (name, layer) in enumerate(members):
            _, r, cc = shards[name].shape
            rows = slice(w * (r // 2), (w + 1) * (r // 2))
            lo = jnp.where(c == 0, a[rows], b[rows])
            hi = jnp.where(c == 0, b[rows], a[rows])
            per[(name, layer)] = jnp.concatenate([lo, hi], axis=0)
    return per


def pack_small(vals):
    row10 = jnp.concatenate([vals["xa_q_gain"].reshape(1, 512), vals["xa_k_gain"].reshape(1, 512)], axis=1)
    row11 = jnp.concatenate([vals["ev_q_gain"], vals["ev_k_gain"], vals["od_q_gain"], vals["od_k_gain"],
                             vals["ev_sinks"], jnp.zeros((1, 1024 - 4 * 64 - 8), F32)], axis=1)
    return jnp.concatenate([vals["ffn1_norm"], vals["mix_norm"], vals["xa_norm"], vals["xa_mem_norm"],
                            vals["ffn2_norm"], row10, row11, jnp.zeros((SMALL_ROWS - 12, 1024), F32)], axis=0)


def unpack_small(arr):
    return {"ffn1_norm": arr[0:2], "mix_norm": arr[2:4], "xa_norm": arr[4:6], "xa_mem_norm": arr[6:8],
            "ffn2_norm": arr[8:10],
            "xa_q_gain": arr[10:11, 0:512].reshape(2, 256), "xa_k_gain": arr[10:11, 512:1024].reshape(2, 256),
            "ev_q_gain": arr[11:12, 0:64], "ev_k_gain": arr[11:12, 64:128], "od_q_gain": arr[11:12, 128:192],
            "od_k_gain": arr[11:12, 192:256], "ev_sinks": arr[11:12, 256:264]}


def local_step(x, mem, target, W, small, prereduce, later):
    depth = small["ffn1_norm"].shape[0]

    def row(name, l):
        return small[name][l:l + 1]

    saved = []
    for l in range(depth):
        j = l // 2
        def riding_gather(host):
            if l == 0 and host in later:
                return GatherBlocks(later[host][0], later[host][1]), later[host][0]
            return None, ()

        riders, rider_args = riding_gather("ffn1")
        x, s1, rode = ffn_fwd(x, row("ffn1_norm", l), W[("ffn1_w_gu", l)], W[("ffn1_w_down", l)], f"l{l}_f1",
                              riders=riders, rider_args=rider_args)
        if riders is not None:
            W = {**W, **later["ffn1"][2](rode)}
        if l % 2 == 0:
            riders, rider_args = riding_gather("stick")
            x, s2, rode = even_fwd(x, row("mix_norm", l), W[("ev_w_in", j)], row("ev_q_gain", j),
                                   row("ev_k_gain", j), small["ev_sinks"][j], W[("ev_w_out", j)], f"l{l}_ev",
                                   riders=riders, rider_args=rider_args)
            if riders is not None:
                W = {**W, **later["stick"][2](rode)}
        else:
            x, s2 = odd_fwd(x, row("mix_norm", l), W[("od_w_in", j)], row("od_q_gain", j), row("od_k_gain", j),
                            W[("od_w_out", j)], f"l{l}_od")
        x, s3 = xa_fwd(x, mem, row("xa_norm", l), row("xa_mem_norm", l), W[("xa_w_q", l)], W[("xa_w_kv", l)],
                       row("xa_q_gain", l), row("xa_k_gain", l), W[("xa_w_o", l)], f"l{l}_xa")
        riders, rider_args = riding_gather("ffn2")
        x, s4, rode = ffn_fwd(x, row("ffn2_norm", l), W[("ffn2_w_gu", l)], W[("ffn2_w_down", l)], f"l{l}_f2",
                              riders=riders, rider_args=rider_args)
        if riders is not None:
            W = {**W, **later["ffn2"][2](rode)}
        saved.append((s1, s2, s3, s4))
    loss, d = loss_kernel(x, target, tm=TM, name="loss")

    gw = {}
    gs = {name: [None] * small[name].shape[0] for name in SMALLS}
    pending, landed = None, {}
    for l in reversed(range(depth)):
        j = l // 2
        s1, s2, s3, s4 = saved[l]
        d, dg, dwgu, dwd, _ = ffn_bwd(d, s4, row("ffn2_norm", l), W[("ffn2_w_gu", l)], W[("ffn2_w_down", l)],
                                      f"l{l}_f2")
        gs["ffn2_norm"][l] = dg
        gw[("ffn2_w_gu", l)], gw[("ffn2_w_down", l)] = dwgu, dwd
        d, dg, dgm, dwq, dwkv, dqg, dkg, dwo = xa_bwd(
            d, s3, mem, row("xa_norm", l), row("xa_mem_norm", l), W[("xa_w_q", l)], W[("xa_w_kv", l)],
            row("xa_q_gain", l), row("xa_k_gain", l), W[("xa_w_o", l)], f"l{l}_xa")
        gs["xa_norm"][l], gs["xa_mem_norm"][l], gs["xa_q_gain"][l], gs["xa_k_gain"][l] = dg, dgm, dqg, dkg
        gw[("xa_w_q", l)], gw[("xa_w_kv", l)], gw[("xa_w_o", l)] = dwq, dwkv, dwo
        split = l == 0 and l % 2 == 0 and ("0a" in ROUNDS)
        early = []
        pre_early = None
        if l % 2 == 0:
            riders, rider_args = None, ()
            if pending is not None:
                riders, rider_args = ChipScatter(pending[1], _chunks_of(ROUNDS[pending[0]])), pending[1]

            def before_mixer_dx(dwin, dwout, j=j, early=early):
                gw[("ev_w_in", j)], gw[("ev_w_out", j)] = dwin, dwout
                early += prereduce.pack(gw, "0a")
                return Riding([(PairExchange(early, _chunks_of(ROUNDS["0a"])), early)])

            d, dg, dwin, dqg, dkg, dsk, dwout, (rode, rode_x) = even_bwd(
                d, s2, row("mix_norm", l), W[("ev_w_in", j)], row("ev_q_gain", j), row("ev_k_gain", j),
                small["ev_sinks"][j], W[("ev_w_out", j)], f"l{l}_ev", riders=riders, rider_args=rider_args,
                before_dx=before_mixer_dx if split else None)
            if pending is not None:
                landed[pending[0]], pending = rode, None
            gs["ev_q_gain"][j], gs["ev_k_gain"][j], gs["ev_sinks"][j] = dqg, dkg, dsk
            gw[("ev_w_in", j)], gw[("ev_w_out", j)] = dwin, dwout
            if split:
                pre_early = prereduce.sums(early, rode_x[0], "0a")
        else:
            d, dg, dwin, dqg, dkg, dwout = odd_bwd(
                d, s2, row("mix_norm", l), W[("od_w_in", j)], row("od_q_gain", j), row("od_k_gain", j),
                W[("od_w_out", j)], f"l{l}_od")
            gs["od_q_gain"][j], gs["od_k_gain"][j] = dqg, dkg
            gw[("od_w_in", j)], gw[("od_w_out", j)] = dwin, dwout
        gs["mix_norm"][l] = dg
        packed = []

        rnd = "0b" if pre_early is not None else str(l)
        final = l == 0

        def before_dx(dwgu, dwd, l=l, packed=packed, rnd=rnd, final=final):
            gw[("ffn1_w_gu", l)], gw[("ffn1_w_down", l)] = dwgu, dwd
            packed += prereduce.pack(gw, rnd)
            chunks = _chunks_of(ROUNDS[rnd])
            if not final:
                return Riding([(PairExchange(packed, chunks), packed)])
            sib = _standalone(PairExchange(packed, chunks), packed, f"pair_grads{rnd}")
            pre = prereduce.sums(packed, sib, rnd)
            return Riding([(ChipScatter(pre, chunks), pre)])

        ride_bact = ride_dwgu = None
        if pre_early is not None:
            chunks = _chunks_of(ROUNDS["0a"])
            ride_bact = Riding([(ChipScatter(pre_early[:2], chunks[:2]), pre_early[:2])])
            ride_dwgu = Riding([(ChipScatter(pre_early[2:], chunks[2:]), pre_early[2:])])
        d, dg, dwgu, dwd, (rode_a, rode_g, rode_x) = ffn_bwd(
            d, s1, row("ffn1_norm", l), W[("ffn1_w_gu", l)], W[("ffn1_w_down", l)], f"l{l}_f1",
            ride_bact=ride_bact, ride_dwgu=ride_dwgu, before_dx=before_dx)
        gs["ffn1_norm"][l] = dg
        if pre_early is not None:
            landed["0a"] = list(rode_a[0]) + list(rode_g[0])
        if pending is not None:
            landed[pending[0]] = chip_scatter(pending[1], _chunks_of(ROUNDS[pending[0]]),
                                              name=f"scatter_grads{pending[0]}")
        if final:
            landed[rnd], pending = rode_x[0], None
        else:
            pending = (rnd, prereduce.sums(packed, rode_x[0], rnd))
    if pending is not None:
        landed[pending[0]] = chip_scatter(pending[1], _chunks_of(ROUNDS[pending[0]]),
                                          name=f"scatter_grads{pending[0]}")
    gsmall = {name: jnp.concatenate(v, axis=0) for name, v in gs.items()}
    return loss, d, landed, gsmall


def kernel(x, mem, ffn1_norm, ffn1_w_gu, ffn1_w_down, mix_norm, ev_w_in, ev_q_gain, ev_k_gain, ev_sinks, ev_w_out, od_w_in, od_q_gain, od_k_gain, od_w_out, xa_norm, xa_mem_norm, xa_w_q, xa_w_kv, xa_q_gain, xa_k_gain, xa_w_o, ffn2_norm, ffn2_w_gu, ffn2_w_down, loss_target, m_ffn1_norm, m_ffn1_w_gu, m_ffn1_w_down, m_mix_norm, m_ev_w_in, m_ev_q_gain, m_ev_k_gain, m_ev_sinks, m_ev_w_out, m_od_w_in, m_od_q_gain, m_od_k_gain, m_od_w_out, m_xa_norm, m_xa_mem_norm, m_xa_w_q, m_xa_w_kv, m_xa_q_gain, m_xa_k_gain, m_xa_w_o, m_ffn2_norm, m_ffn2_w_gu, m_ffn2_w_down, v_ffn1_norm, v_ffn1_w_gu, v_ffn1_w_down, v_mix_norm, v_ev_w_in, v_ev_q_gain, v_ev_k_gain, v_ev_sinks, v_ev_w_out, v_od_w_in, v_od_q_gain, v_od_k_gain, v_od_w_out, v_xa_norm, v_xa_mem_norm, v_xa_w_q, v_xa_w_kv, v_xa_q_gain, v_xa_k_gain, v_xa_w_o, v_ffn2_norm, v_ffn2_w_gu, v_ffn2_w_down):
    given = dict(locals())
    w = {n: given[n] for n in WEIGHTS}
    m = {n: given["m_" + n] for n in WEIGHTS}
    v = {n: given["v_" + n] for n in WEIGHTS}
    c = lax.axis_index("c")
    shards = {name: w[name] for name, _ in MATS}
    small = {n: w[n] for n in SMALLS}

    def gathering(groups):
        return group_halves(shards, c, groups), _chunks_of(groups), lambda got: full_weights(got, shards, groups)

    halves, chunks, unpack = gathering(GATHER_FIRST)
    full = unpack(gather_blocks(halves, chunks, name="gather_weights0"))
    later = {"ffn1": gathering(GATHER_IN_FFN1), "stick": gathering(GATHER_IN_STICK), "ffn2": gathering(GATHER_IN_FFN2)}

    class prereduce:
        @staticmethod
        def pack(gw, rnd):
            return group_grads(gw, shards, ROUNDS[rnd])

        @staticmethod
        def sums(packed, sib, rnd):
            return [pair_sum(p, s, c, tr=g[2], name=f"pair_sum{rnd}_{i}")
                    for i, (g, p, s) in enumerate(zip(ROUNDS[rnd], packed, sib))]

    loss_b, grad_x, landed, gsmall = local_step(x[0], mem[0], loss_target[0], full, small, prereduce, later)

    per = {}
    for rnd, land in sorted(landed.items()):
        groups = ROUNDS[rnd]
        mine = [reduce_slots(a, tr=g[2], name=f"sum_grads{rnd}_{i}") for i, (g, a) in enumerate(zip(groups, land))]
        theirs = sibling_send(mine, _chunks_of(groups), name=f"swap_grads{rnd}")
        per.update(shard_grads(mine, theirs, c, shards, groups))
    g = {name: jnp.stack([per[(name, layer)] for layer in range(w[name].shape[0])], axis=0) for name, _ in MATS}
    land_small = gather_small(pack_small(gsmall), name="gather_small")
    g_small = unpack_small(reduce_slots(land_small, tr=SMALL_ROWS, name="sum_small"))
    g.update(g_small)

    delta, new_m, new_v = {}, {}, {}
    for name, _ in MATS:
        shp = w[name].shape
        flat = [a.reshape(-1, shp[-1]) for a in (w[name], g[name], m[name], v[name])]
        dl, nm, nv = adamw(*flat, br=BLK, name=f"adamw_{name}")
        delta[name], new_m[name], new_v[name] = dl.reshape(shp), nm.reshape(shp), nv.reshape(shp)
    dl, nm, nv = adamw(pack_small(small), pack_small(g_small), pack_small({n: m[n] for n in SMALLS}),
                       pack_small({n: v[n] for n in SMALLS}), br=SMALL_ROWS, name="adamw_small")
    for dst, arr in ((delta, dl), (new_m, nm), (new_v, nv)):
        dst.update(unpack_small(arr))

    loss = lax.psum(loss_b[0, 0], ("x", "y", "c"))
    return (loss, grad_x[None], *[g[n] for n in WEIGHTS], *[delta[n] for n in WEIGHTS],
            *[new_m[n] for n in WEIGHTS], *[new_v[n] for n in WEIGHTS])
```

```python
import jax
import jax.numpy as jnp
from jax import lax
from jax.experimental import pallas as pl
from jax.experimental.pallas import tpu as pltpu

F32 = jnp.float32
BF16 = jnp.bfloat16

D_MODEL = 1024
HEAD_DIM = 64
LANES = 128
BLK = 128
RMS_EPS = 1e-6
MEM_LEN = 256
X_HEADS = 4
X_HEAD_DIM = 256
A_Q_HEADS = 8
A_GROUP = 4
A_WINDOW = 128
C_HEADS = 16
C_PATTERNS = ((128, 1), (512, 4), (2048, 16))
NEG = -1e30
VMEM_LIMIT = 56 * 2 ** 20

ADAM_LR = 0.001
ADAM_B1 = 0.9
ADAM_B2 = 0.999
ADAM_EPS = 1e-08
ADAM_WD = 0.01
ADAM_STEP = 10

N_DEV = 8
MESH = pl.DeviceIdType.MESH


def _cparams(n):
    return pltpu.CompilerParams(dimension_semantics=("arbitrary",) * n, vmem_limit_bytes=VMEM_LIMIT)


def _dot(a, b):
    return jnp.dot(a, b, preferred_element_type=F32)


def _dot_nt(a, b):
    return lax.dot_general(a, b, (((1,), (1,)), ((), ())), preferred_element_type=F32)


def _dot_tn(a, b):
    return lax.dot_general(a, b, (((0,), (0,)), ((), ())), preferred_element_type=F32)


def _sigmoid(z):
    return 1.0 / (1.0 + jnp.exp(-z))


def norm_matmul(x, g, w, *, tm, tn, split, name):
    T, K = x.shape
    blocked = w.ndim == 3
    assert not blocked or w.shape[2] == tn
    N = w.shape[0] * w.shape[2] if blocked else w.shape[1]
    nj = N // tn

    def body(x_ref, g_ref, w_ref, o_ref, h_ref):
        @pl.when(pl.program_id(1) == 0)
        def _():
            xv = x_ref[...]
            r = lax.rsqrt(jnp.mean(xv * xv, axis=-1, keepdims=True) + RMS_EPS)
            h_ref[...] = (xv * r * g_ref[...]).astype(BF16)

        o_ref[...] = _dot(h_ref[...], w_ref[...]).astype(o_ref.dtype)

    if split:
        njh = nj // 2
        o_shape = jax.ShapeDtypeStruct((2, T, N // 2), BF16)
        o_spec = pl.BlockSpec((None, tm, tn), lambda i, j: (j // njh, i, j % njh))
    else:
        o_shape = jax.ShapeDtypeStruct((T, N), F32)
        o_spec = pl.BlockSpec((tm, tn), lambda i, j: (i, j))
    return pl.pallas_call(
        body, grid=(T // tm, nj),
        in_specs=[pl.BlockSpec((tm, K), lambda i, j: (i, 0)),
                  pl.BlockSpec((1, K), lambda i, j: (0, 0)),
                  (pl.BlockSpec((None, K, tn), lambda i, j: (j, 0, 0)) if blocked
                   else pl.BlockSpec((K, tn), lambda i, j: (0, j)))],
        out_specs=[o_spec, pl.BlockSpec((tm, K), lambda i, j: (i, 0))],
        out_shape=[o_shape, jax.ShapeDtypeStruct((T, K), BF16)],
        compiler_params=_cparams(2), name=name)(x, g, w)


def mm_nn(a, b, *, res, tm, tn, tk, name):
    T = a.shape[0]
    K, N = b.shape
    nk = K // tk

    def body(a_ref, b_ref, r_ref, o_ref, acc):
        k = pl.program_id(2)

        @pl.when(k == 0)
        def _():
            acc[...] = jnp.zeros_like(acc)

        acc[...] += _dot(a_ref[...].astype(BF16), b_ref[...])

        @pl.when(k == nk - 1)
        def _():
            o_ref[...] = r_ref[...] + acc[...]

    return pl.pallas_call(
        body, grid=(T // tm, N // tn, nk),
        in_specs=[pl.BlockSpec((tm, tk), lambda i, j, k: (i, k)), pl.BlockSpec((tk, tn), lambda i, j, k: (k, j)),
                  pl.BlockSpec((tm, tn), lambda i, j, k: (i, j))],
        out_specs=pl.BlockSpec((tm, tn), lambda i, j, k: (i, j)),
        out_shape=jax.ShapeDtypeStruct((T, N), F32),
        scratch_shapes=[pltpu.VMEM((tm, tn), F32)],
        compiler_params=_cparams(3), name=name)(a, b, res)


def mm_nt(a, b, *, tm, tn, tk, name):
    T, K = a.shape
    N = b.shape[0]
    nk = K // tk

    def body(a_ref, b_ref, o_ref, acc):
        k = pl.program_id(2)

        @pl.when(k == 0)
        def _():
            acc[...] = jnp.zeros_like(acc)

        acc[...] += _dot_nt(a_ref[...].astype(BF16), b_ref[...])

        @pl.when(k == nk - 1)
        def _():
            o_ref[...] = acc[...]

    return pl.pallas_call(
        body, grid=(T // tm, N // tn, nk),
        in_specs=[pl.BlockSpec((tm, tk), lambda i, j, k: (i, k)),
                  pl.BlockSpec((tn, tk), lambda i, j, k: (j, k))],
        out_specs=pl.BlockSpec((tm, tn), lambda i, j, k: (i, j)),
        out_shape=jax.ShapeDtypeStruct((T, N), F32),
        scratch_shapes=[pltpu.VMEM((tm, tn), F32)],
        compiler_params=_cparams(3), name=name)(a, b)


def ffn_bwd_act(d, wd, gu, *, tm, tn, name, riding=None):
    T, K = d.shape
    Fd = wd.shape[0]
    ni = T // tm

    def body(d_ref, w_ref, g_ref, u_ref, dgu_ref, dwd_ref, acc):
        i = pl.program_id(1)

        @pl.when(i == 0)
        def _():
            acc[...] = jnp.zeros_like(acc)

        db = d_ref[...].astype(BF16)
        da = 0.5 * _dot_nt(db, w_ref[...])
        gv = g_ref[...].astype(F32)
        uv = u_ref[...].astype(F32)
        s = _sigmoid(gv)
        silu = gv * s
        acc[...] += _dot_tn((silu * uv).astype(BF16), db)
        dgu_ref[0] = (da * uv * (s * (1.0 + gv * (1.0 - s)))).astype(BF16)
        dgu_ref[1] = (da * silu).astype(BF16)

        @pl.when(i == ni - 1)
        def _():
            dwd_ref[...] = (0.5 * acc[...]).astype(BF16)

    in_specs = [pl.BlockSpec((tm, K), lambda j, i: (i, 0)),
                pl.BlockSpec((tn, K), lambda j, i: (j, 0)),
                pl.BlockSpec((None, tm, tn), lambda j, i: (0, i, j)),
                pl.BlockSpec((None, tm, tn), lambda j, i: (1, i, j))]
    out_specs = [pl.BlockSpec((2, tm, tn), lambda j, i: (0, i, j)), pl.BlockSpec((tn, K), lambda j, i: (j, 0))]
    out_shape = [jax.ShapeDtypeStruct((2, T, Fd), BF16), jax.ShapeDtypeStruct((Fd, K), BF16)]
    return _call_with_riders(body, riding, (Fd // tn, ni), in_specs, out_specs, out_shape,
                             [pltpu.VMEM((tn, K), F32)], [d, wd, gu, gu], name)


def _call_with_riders(body, riding, grid, in_specs, out_specs, out_shape, scratch, args, name):
    n_out = len(out_shape)
    if riding is None:
        return pl.pallas_call(body, grid=grid, in_specs=in_specs, out_specs=out_specs, out_shape=out_shape,
                              scratch_shapes=scratch, compiler_params=_cparams(len(grid)), name=name)(*args)

    def is_first():
        ok = pl.program_id(0) == 0
        for ax in range(1, len(grid)):
            ok = ok & (pl.program_id(ax) == 0)
        return ok

    def is_last():
        ok = pl.program_id(0) == grid[0] - 1
        for ax in range(1, len(grid)):
            ok = ok & (pl.program_id(ax) == grid[ax] - 1)
        return ok

    outs = pl.pallas_call(
        riding.wrap(body, len(in_specs), n_out, len(scratch), is_first, is_last), grid=grid,
        in_specs=list(in_specs) + riding.in_specs, out_specs=list(out_specs) + riding.out_specs,
        out_shape=list(out_shape) + riding.out_shapes, scratch_shapes=list(scratch) + riding.scratch,
        compiler_params=_cparams(len(grid)), name=name)(*args, *riding.args)
    core, per = riding.split(outs, n_out)
    return (*core, *per)


def mm_nt_normbwd(a, b, x, g, res, *, a_split, tm, tk, name, riding=None):
    T, Dm = x.shape
    blocked = b.ndim == 3
    assert not blocked or b.shape[2] == tk
    K = b.shape[0] * b.shape[2] if blocked else b.shape[1]
    nk = K // tk
    nkh = nk // 2
    has_res = res is not None

    def body(*refs):
        if has_res:
            a_ref, b_ref, x_ref, g_ref, r_ref, dx_ref, dg_ref, acc = refs
        else:
            a_ref, b_ref, x_ref, g_ref, dx_ref, dg_ref, acc = refs
        i = pl.program_id(0)
        k = pl.program_id(1)

        @pl.when(k == 0)
        def _():
            acc[...] = jnp.zeros_like(acc)

        acc[...] += _dot_nt(a_ref[...].astype(BF16), b_ref[...])

        @pl.when(k == nk - 1)
        def _():
            xv = x_ref[...]
            r = lax.rsqrt(jnp.mean(xv * xv, axis=-1, keepdims=True) + RMS_EPS)
            xh = xv * r
            dh = acc[...]
            dxh = dh * g_ref[...]
            dx = r * (dxh - xh * jnp.mean(dxh * xh, axis=-1, keepdims=True))
            if has_res:
                dx = dx + r_ref[...]
            dx_ref[...] = dx
            part = jnp.sum(dh * xh, axis=0, keepdims=True)

            @pl.when(i == 0)
            def _():
                dg_ref[...] = part

            @pl.when(i > 0)
            def _():
                dg_ref[...] += part

    if a_split:
        a_spec = pl.BlockSpec((None, tm, tk), lambda i, k: (k // nkh, i, k % nkh))
    else:
        a_spec = pl.BlockSpec((tm, tk), lambda i, k: (i, k))
    in_specs = [a_spec,
                (pl.BlockSpec((None, Dm, tk), lambda i, k: (k, 0, 0)) if blocked
                 else pl.BlockSpec((Dm, tk), lambda i, k: (0, k))),
                pl.BlockSpec((tm, Dm), lambda i, k: (i, 0)),
                pl.BlockSpec((1, Dm), lambda i, k: (0, 0))]
    args = [a, b, x, g]
    if has_res:
        in_specs.append(pl.BlockSpec((tm, Dm), lambda i, k: (i, 0)))
        args.append(res)
    out_specs = [pl.BlockSpec((tm, Dm), lambda i, k: (i, 0)), pl.BlockSpec((1, Dm), lambda i, k: (0, 0))]
    out_shape = [jax.ShapeDtypeStruct((T, Dm), F32), jax.ShapeDtypeStruct((1, Dm), F32)]
    scratch = [pltpu.VMEM((tm, Dm), F32)]
    return _call_with_riders(body, riding, (T // tm, nk), in_specs, out_specs, out_shape, scratch, args, name)


def mm_tn(a, b, *, scale, a_split, b_split, tm, tn, tk, name, out_blocked=False, riding=None):
    T = a.shape[-2]
    M = a.shape[-1] * (2 if a_split else 1)
    N = b.shape[-1] * (2 if b_split else 1)
    ni, nj, nk = M // tm, N // tn, T // tk
    nih, njh = ni // 2, nj // 2

    def body(a_ref, b_ref, o_ref, acc):
        k = pl.program_id(2)

        @pl.when(k == 0)
        def _():
            acc[...] = jnp.zeros_like(acc)

        acc[...] += _dot_tn(a_ref[...].astype(BF16), b_ref[...].astype(BF16))

        @pl.when(k == nk - 1)
        def _():
            o_ref[...] = (acc[...] * scale).astype(o_ref.dtype)

    if a_split:
        a_spec = pl.BlockSpec((None, tk, tm), lambda i, j, k: (i // nih, k, i % nih))
    else:
        a_spec = pl.BlockSpec((tk, tm), lambda i, j, k: (k, i))
    if b_split:
        b_spec = pl.BlockSpec((None, tk, tn), lambda i, j, k: (j // njh, k, j % njh))
    else:
        b_spec = pl.BlockSpec((tk, tn), lambda i, j, k: (k, j))
    if out_blocked:
        o_spec = pl.BlockSpec((None, None, tm, tn), lambda i, j, k: (j, i, 0, 0))
        o_shape = jax.ShapeDtypeStruct((nj, ni, tm, tn), BF16)
    else:
        o_spec = pl.BlockSpec((tm, tn), lambda i, j, k: (i, j))
        o_shape = jax.ShapeDtypeStruct((M, N), BF16)
    outs = _call_with_riders(body, riding, (ni, nj, nk), [a_spec, b_spec], [o_spec], [o_shape],
                             [pltpu.VMEM((tm, tn), F32)], [a, b], name)
    return outs[0] if riding is None else tuple(outs)


def loss_kernel(y, target, *, tm, name):
    T, Dm = y.shape

    def body(y_ref, t_ref, l_ref, dy_ref):
        e = y_ref[...] - t_ref[...]
        dy_ref[...] = e * (1.0 / Dm)
        part = (0.5 / Dm) * jnp.sum(jnp.sum(e * e, axis=-1, keepdims=True), axis=0, keepdims=True)
        part = jnp.broadcast_to(part, (8, LANES))

        @pl.when(pl.program_id(0) == 0)
        def _():
            l_ref[...] = part

        @pl.when(pl.program_id(0) > 0)
        def _():
            l_ref[...] += part

    return pl.pallas_call(
        body, grid=(T // tm,),
        in_specs=[pl.BlockSpec((tm, Dm), lambda i: (i, 0)), pl.BlockSpec((tm, Dm), lambda i: (i, 0))],
        out_specs=[pl.BlockSpec((8, LANES), lambda i: (0, 0)), pl.BlockSpec((tm, Dm), lambda i: (i, 0))],
        out_shape=[jax.ShapeDtypeStruct((8, LANES), F32), jax.ShapeDtypeStruct((T, Dm), F32)],
        compiler_params=_cparams(1), name=name)(y, target)


def adamw(w, g, m, v, *, br, name):
    R, C = w.shape

    def body(w_ref, g_ref, m_ref, v_ref, d_ref, nm_ref, nv_ref):
        gv = g_ref[...]
        nm = ADAM_B1 * m_ref[...] + (1.0 - ADAM_B1) * gv
        nv = ADAM_B2 * v_ref[...] + (1.0 - ADAM_B2) * (gv * gv)
        m_hat = nm / (1.0 - ADAM_B1 ** ADAM_STEP)
        v_hat = nv / (1.0 - ADAM_B2 ** ADAM_STEP)
        d_ref[...] = -ADAM_LR * (m_hat / (jnp.sqrt(v_hat) + ADAM_EPS) + ADAM_WD * w_ref[...])
        nm_ref[...] = nm
        nv_ref[...] = nv

    spec = pl.BlockSpec((br, C), lambda i: (i, 0))
    shp = jax.ShapeDtypeStruct((R, C), F32)
    return pl.pallas_call(
        body, grid=(R // br,), in_specs=[spec] * 4, out_specs=[spec] * 3, out_shape=[shp] * 3,
        compiler_params=_cparams(1), name=name)(w, g, m, v)


def _lane0():
    return lax.broadcasted_iota(jnp.int32, (1, LANES), 1) < HEAD_DIM


def _half_sum(x, m0):
    s0 = jnp.sum(jnp.where(m0, x, 0.0), axis=-1, keepdims=True)
    s1 = jnp.sum(jnp.where(m0, 0.0, x), axis=-1, keepdims=True)
    return jnp.where(m0, s0, s1)


def _head_rms(x, m0):
    return lax.rsqrt(_half_sum(x * x, m0) * (1.0 / HEAD_DIM) + RMS_EPS)


def _alibi(n):
    return [float(2.0 ** (-8.0 * (h + 1) / n)) for h in range(n)]


def _mask_half(x, m0, e):
    return jnp.where(m0, x, 0.0) if e == 0 else jnp.where(m0, 0.0, x)


def _band_masks2(max_dist, has_prev, live):
    row = lax.broadcasted_iota(jnp.int32, (2 * BLK, 2 * BLK), 0)
    col = lax.broadcasted_iota(jnp.int32, (2 * BLK, 2 * BLK), 1)
    dist = (row & (BLK - 1)) - col + BLK
    lim = jnp.where(live, max_dist, -1)
    first = jnp.where(has_prev, 0, BLK)
    valid = (dist >= 0) & (dist <= lim) & (col >= first)
    top = lax.broadcasted_iota(jnp.int32, (2 * BLK, 1), 0) < BLK
    return dist.astype(F32), valid, top


def _stack_heads(x, m0, kes):
    parts = []
    for e in range(2):
        h = _mask_half(x, m0, e)
        parts.append(pltpu.roll(h, HEAD_DIM, 1) if kes[e] != e else h)
    return jnp.concatenate(parts, axis=0)


def _unstack_heads(y, m0, kes):
    parts = []
    for e in range(2):
        h = y[e * BLK:(e + 1) * BLK]
        parts.append(pltpu.roll(h, HEAD_DIM, 1) if kes[e] != e else h)
    return jnp.where(m0, parts[0], parts[1])


def _rows(r, dil):
    return pl.ds(r, BLK, stride=dil) if dil > 1 else pl.ds(0, BLK)


def _band_units(dil, nsub):
    assert dil == 1 or nsub == 1
    if nsub == 1:
        return [(_rows(r, dil), ("prev", _rows(r, dil)), 0) for r in range(dil)]
    units = [(pl.ds(0, BLK), ("prev", pl.ds(0, BLK)), 0)]
    units += [(pl.ds(BLK * s, BLK), ("cur", pl.ds(BLK * (s - 1), BLK)), s) for s in range(1, nsub)]
    return units


def _head_col_spec(ppk, RB, row_block):
    if ppk == 1:
        return pl.BlockSpec((None, RB, 1), lambda p, i: (p, row_block(i), 0))
    return pl.BlockSpec((ppk, RB, 1), lambda p, i: (p, row_block(i), 0))


def _band_specs(dil, nsub, ppk, q_blk, k_blk, v_blk, kv_shared, nb):
    RB = BLK * dil * nsub
    PB = BLK if nsub > 1 else RB
    qw = LANES * ppk
    kw = LANES if kv_shared else qw

    def cur(i):
        return jnp.minimum(i, nb - 1)

    def prev(i):
        return jnp.maximum(i * nsub - 1, 0) if nsub > 1 else jnp.maximum(i - 1, 0)

    def kidx(base):
        return (lambda p, i: (cur(i), base)) if kv_shared else (lambda p, i: (cur(i), base + p))

    def pidx(base):
        return (lambda p, i: (prev(i), base)) if kv_shared else (lambda p, i: (prev(i), base + p))

    specs = [pl.BlockSpec((RB, qw), lambda p, i: (cur(i), q_blk + p)),
             pl.BlockSpec((RB, kw), kidx(k_blk)), pl.BlockSpec((PB, kw), pidx(k_blk)),
             pl.BlockSpec((RB, kw), kidx(v_blk)), pl.BlockSpec((PB, kw), pidx(v_blk))]
    return specs if dil == 1 else [specs[0], specs[1], specs[3]]


def qk_norm(qkv, q_gain2, k_gain2, *, width, steps, n_q, tm, name):
    T = qkv.shape[0]
    nsb = width // LANES

    def body(x_ref, qg_ref, kg_ref, o_ref):
        m0 = _lane0()
        for b in range(nsb):
            is_q = ((pl.program_id(1) * nsb + b) < n_q).astype(F32)
            gain = qg_ref[...] * is_q + kg_ref[...] * (1.0 - is_q)
            cols = pl.ds(LANES * b, LANES)
            xv = x_ref[:, cols]
            o_ref[:, cols] = xv * _head_rms(xv, m0) * gain

    gspec = pl.BlockSpec((1, LANES), lambda i, j: (0, 0))
    return pl.pallas_call(
        body, grid=(T // tm, steps),
        in_specs=[pl.BlockSpec((tm, width), lambda i, j: (i, j)), gspec, gspec],
        out_specs=pl.BlockSpec((tm, width), lambda i, j: (i, j)),
        out_shape=jax.ShapeDtypeStruct((T, width * steps), F32),
        compiler_params=_cparams(2), name=name)(qkv, q_gain2, k_gain2)


def banded_fwd(qkn, qkv, slopes, sinks, *, dil, nsub, ppk, q_blk, k_blk, v_blk, n_heads, group,
               max_dist, name):
    T = qkv.shape[0]
    RB = BLK * dil * nsub
    nb = T // RB
    npair = n_heads // 2
    kv_shared = group > 1
    scale = HEAD_DIM ** -0.5
    has_sink = sinks is not None

    def body(*refs):
        slope_ref = refs[0]
        if has_sink:
            sink_ref, refs = refs[1], refs[2:]
        else:
            refs = refs[1:]
        if dil == 1:
            q_ref, kc_ref, kp_ref, vc_ref, vp_ref, o_ref, l_ref, lc0_ref, lc1_ref = refs
        else:
            q_ref, kc_ref, vc_ref, o_ref, l_ref, lc0_ref, lc1_ref, kp_ref, vp_ref = refs
        pb = pl.program_id(0)
        i = pl.program_id(1)
        if dil > 1:
            @pl.when(i == 0)
            def _():
                kp_ref[...] = jnp.zeros_like(kp_ref)
                vp_ref[...] = jnp.zeros_like(vp_ref)
        m0 = _lane0()
        distf, valid_first, top = _band_masks2(max_dist, i > 0, i >= 0)
        valid_inner = _band_masks2(max_dist, i >= 0, i >= 0)[1] if nsub > 1 else None
        for u, (rows, (src, prows), sub) in enumerate(_band_units(dil, nsub)):
            valid = valid_first if sub == 0 else valid_inner
            kpr, vpr = (kp_ref, vp_ref) if src == "prev" else (kc_ref, vc_ref)
            kcache = {}
            for jp in range(ppk):
                cs = pl.ds(LANES * jp, LANES)
                jk = 0 if kv_shared else jp
                if jk not in kcache:
                    ks = pl.ds(LANES * jk, LANES)
                    kcur, vcur = kc_ref[rows, ks], vc_ref[rows, ks]
                    if dil == 1:
                        kprev, vprev = kpr[prows, ks], vpr[prows, ks]
                    else:
                        kprev, vprev = kp_ref[u, :, ks], vp_ref[u, :, ks]
                        kp_ref[u, :, ks] = kcur
                        vp_ref[u, :, ks] = vcur
                    kcat = jnp.concatenate([kprev, kcur], axis=0)
                    vcat = jnp.concatenate([vprev, vcur], axis=0)
                    kcache[jk] = (kcat.astype(BF16), vcat.astype(BF16))
                kn, vcat = kcache[jk]
                qn = q_ref[rows, cs]
                kes = [((2 * jp + e) // group) % 2 if kv_shared else e for e in range(2)]
                hidx = 2 * (pb * ppk + jp)
                qs = _stack_heads(qn, m0, kes).astype(BF16)
                slope = jnp.where(top, slope_ref[hidx], slope_ref[hidx + 1])
                s = jnp.where(valid, _dot_nt(qs, kn) * scale - slope * distf, NEG)
                m = jnp.max(s, axis=-1, keepdims=True)
                if has_sink:
                    sk = jnp.where(top, sink_ref[hidx], sink_ref[hidx + 1])
                    m = jnp.maximum(m, sk)
                p = jnp.exp(s - m)
                den = jnp.sum(p, axis=-1, keepdims=True)
                if has_sink:
                    den = den + jnp.exp(sk - m)
                o_full = _dot((p * (1.0 / den)).astype(BF16), vcat)
                o_ref[rows, cs] = _unstack_heads(o_full, m0, kes)
                lse = m + jnp.log(den)
                l_ref[rows, cs] = _unstack_heads(jnp.broadcast_to(lse, (2 * BLK, LANES)), m0, [0, 1])
                for e, lc_ref in enumerate((lc0_ref, lc1_ref)):
                    if ppk == 1:
                        lc_ref[rows, :] = lse[e * BLK:(e + 1) * BLK]
                    else:
                        lc_ref[jp, rows, :] = lse[e * BLK:(e + 1) * BLK]

    smem = pl.BlockSpec(memory_space=pltpu.SMEM)
    qw = LANES * ppk
    ospec = pl.BlockSpec((RB, qw), lambda p, i: (i, p))
    oshape = jax.ShapeDtypeStruct((T, n_heads * HEAD_DIM), F32)
    args = [slopes] + ([sinks] if has_sink else []) + ([qkn] * 3 + [qkv] * 2 if dil == 1 else [qkn, qkn, qkv])
    kw = LANES if kv_shared else qw
    prev_scratch = [] if dil == 1 else [pltpu.VMEM((dil, BLK, kw), F32)] * 2
    return pl.pallas_call(
        body, grid=(npair // ppk, nb),
        in_specs=[smem] * (2 if has_sink else 1) + _band_specs(dil, nsub, ppk, q_blk, k_blk, v_blk, kv_shared, nb),
        out_specs=[ospec, ospec] + [_head_col_spec(ppk, RB, lambda i: i)] * 2,
        out_shape=[oshape, oshape] + [jax.ShapeDtypeStruct((npair, T, 1), F32)] * 2,
        scratch_shapes=prev_scratch, compiler_params=_cparams(2), name=name)(*args)


def banded_bwd(qkn, qkv, slopes, sinks, do, o, lsec, w, omix, *, dil, nsub, ppk, q_blk, k_blk, v_blk,
               n_heads, group, max_dist, do_blk, name):
    T = qkv.shape[0]
    RB = BLK * dil * nsub
    nb = T // RB
    npair = n_heads // 2
    kv_shared = group > 1
    scale = HEAD_DIM ** -0.5
    has_sink = sinks is not None
    mixed = w is not None
    qw = LANES * ppk

    def body(*refs):
        slope_ref = refs[0]
        if has_sink:
            sink_ref, refs = refs[1], refs[2:]
        else:
            refs = refs[1:]
        if dil == 1:
            q_ref, kc_ref, kp_ref, vc_ref, vp_ref, do_ref, lc0_ref, lc1_ref = refs[:8]
            refs = refs[8:]
        else:
            q_ref, kc_ref, vc_ref, do_ref, lc0_ref, lc1_ref = refs[:6]
            refs, kp_ref, vp_ref = refs[6:-2], refs[-2], refs[-1]
        if mixed:
            w_ref, om_ref, refs = refs[0], refs[1], refs[2:]
        else:
            o_ref, refs = refs[0], refs[1:]
        dq_ref, dk_ref, dv_ref, dsk_ref, ck_ref, cv_ref = refs
        pb = pl.program_id(0)
        i = pl.program_id(1)
        live = i < nb
        m0 = _lane0()
        lane = lax.broadcasted_iota(jnp.int32, (1, LANES), 1)
        distf, valid_first, top = _band_masks2(max_dist, i > 0, live)
        valid_inner = _band_masks2(max_dist, i >= 0, live)[1] if nsub > 1 else None
        livef = live.astype(F32)

        def half_rows(x):
            s0 = jnp.sum(jnp.where(m0, x, 0.0), axis=-1, keepdims=True)
            s1 = jnp.sum(jnp.where(m0, 0.0, x), axis=-1, keepdims=True)
            return jnp.concatenate([s0, s1], axis=0)

        @pl.when((pb == 0) & (i == 0))
        def _():
            dsk_ref[...] = jnp.zeros_like(dsk_ref)

        @pl.when(i == 0)
        def _():
            ck_ref[...] = jnp.zeros_like(ck_ref)
            cv_ref[...] = jnp.zeros_like(cv_ref)

        dsk_acc = jnp.zeros((1, LANES), F32)
        if nsub > 1:
            dk_ref[...] = ck_ref[...]
            dv_ref[...] = cv_ref[...]
        if dil > 1:
            @pl.when(i == 0)
            def _():
                kp_ref[...] = jnp.zeros_like(kp_ref)
                vp_ref[...] = jnp.zeros_like(vp_ref)

        for u, (rows, (src, prows), sub) in enumerate(_band_units(dil, nsub)):
            valid = valid_first if sub == 0 else valid_inner
            kpr, vpr = (kp_ref, vp_ref) if src == "prev" else (kc_ref, vc_ref)
            ck_u, cv_u = (ck_ref.at[u], cv_ref.at[u]) if dil > 1 else (None, None)
            for jp in range(ppk):
                cs = pl.ds(LANES * jp, LANES)
                ks = pl.ds(0, LANES) if kv_shared else cs
                kcur, vcur = kc_ref[rows, ks], vc_ref[rows, ks]
                if dil == 1:
                    kprev, vprev = kpr[prows, ks], vpr[prows, ks]
                else:
                    kprev, vprev = kp_ref[u, :, ks], vp_ref[u, :, ks]
                    kp_ref[u, :, ks] = kcur
                    vp_ref[u, :, ks] = vcur
                kn = jnp.concatenate([kprev, kcur], axis=0).astype(BF16)
                vcat = jnp.concatenate([vprev, vcur], axis=0).astype(BF16)
                dov = do_ref[rows, cs]
                if mixed:
                    dov = dov * w_ref[rows, cs]
                    shift = half_rows(dov * om_ref[rows, cs])
                else:
                    shift = half_rows(dov * o_ref[rows, cs])
                kes = [((2 * jp + e) // group) % 2 if kv_shared else e for e in range(2)]
                hidx = 2 * (pb * ppk + jp)
                qs = _stack_heads(q_ref[rows, cs], m0, kes).astype(BF16)
                dos = _stack_heads(dov, m0, kes).astype(BF16)
                lse = jnp.concatenate([ref[rows, :] if ppk == 1 else ref[jp, rows, :]
                                       for ref in (lc0_ref, lc1_ref)], axis=0)
                slope = jnp.where(top, slope_ref[hidx], slope_ref[hidx + 1])
                p = jnp.where(valid, jnp.exp(_dot_nt(qs, kn) * scale - slope * distf - lse), 0.0)
                ds = (p * (_dot_nt(dos, vcat) - shift)).astype(BF16)
                dqn = _unstack_heads(_dot(ds, kn), m0, kes) * scale
                dkn = _dot_tn(ds, qs) * scale
                dvv = _dot_tn(p.astype(BF16), dos)
                if has_sink:
                    sk = jnp.where(top, sink_ref[hidx], sink_ref[hidx + 1])
                    contrib = -jnp.exp(sk - lse) * shift * livef
                    for e in range(2):
                        tot = jnp.sum(contrib[e * BLK:(e + 1) * BLK], axis=0, keepdims=True)
                        dsk_acc = dsk_acc + jnp.where(lane == (2 * jp + e), tot, 0.0)
                dk_raw = dkn

                @pl.when(live)
                def _():
                    dq_ref[rows, cs] = dqn

                if dil > 1:
                    dk_ref[rows, cs] = ck_u[:, cs] + dk_raw[:BLK]
                    dv_ref[rows, cs] = cv_u[:, cs] + dvv[:BLK]
                    ck_u[:, cs] = dk_raw[BLK:]
                    cv_u[:, cs] = dvv[BLK:]
                    continue
                if nsub == 1:
                    dk_ref[rows, cs] = ck_ref[rows, cs] + dk_raw[:BLK]
                    dv_ref[rows, cs] = cv_ref[rows, cs] + dvv[:BLK]
                elif sub == 0:
                    last = pl.ds(RB - BLK, BLK)
                    dk_ref[last, cs] += dk_raw[:BLK]
                    dv_ref[last, cs] += dvv[:BLK]
                else:
                    ck_ref[prows, cs] += dk_raw[:BLK]
                    cv_ref[prows, cs] += dvv[:BLK]
                ck_ref[rows, cs] = dk_raw[BLK:]
                cv_ref[rows, cs] = dvv[BLK:]
        dsk_ref[...] += dsk_acc

    smem = pl.BlockSpec(memory_space=pltpu.SMEM)
    gspec = pl.BlockSpec((1, LANES), lambda p, i: (0, 0))

    def cur(i):
        return jnp.minimum(i, nb - 1)

    qspec = pl.BlockSpec((RB, qw), lambda p, i: (cur(i), p))
    dospec = pl.BlockSpec((RB, qw), lambda p, i: (cur(i), do_blk + p))
    kvout = pl.BlockSpec((RB, qw), lambda p, i: (jnp.maximum(i - 1, 0), p))
    in_specs = ([smem] * (2 if has_sink else 1) + _band_specs(dil, nsub, ppk, q_blk, k_blk, v_blk, kv_shared, nb)
                + [dospec] + [_head_col_spec(ppk, RB, cur)] * 2
                + ([qspec, qspec] if mixed else [qspec]))
    args = ([slopes] + ([sinks] if has_sink else []) + ([qkn] * 3 + [qkv] * 2 if dil == 1 else [qkn, qkn, qkv])
            + [do, lsec[0], lsec[1]]
            + ([w, omix] if mixed else [o]))
    full = jax.ShapeDtypeStruct((T, n_heads * HEAD_DIM), F32)
    row = jax.ShapeDtypeStruct((1, LANES), F32)
    return pl.pallas_call(
        body, grid=(npair // ppk, nb + 1), in_specs=in_specs,
        out_specs=[qspec, kvout, kvout, gspec],
        out_shape=[full, full, full, row],
        scratch_shapes=([pltpu.VMEM((RB, qw), F32)] * 2 if dil == 1
                        else [pltpu.VMEM((dil, BLK, qw), F32)] * 2 + [pltpu.VMEM((dil, BLK, qw), F32)] * 2),
        compiler_params=_cparams(2), name=name)(*args)


def mix_fwd(o1, o2, o3, l1, l2, l3, *, tm, name):
    T, C = o1.shape

    def body(o1r, o2r, o3r, l1r, l2r, l3r, o_ref, w1r, w2r, w3r):
        a, b, c = l1r[...], l2r[...], l3r[...]
        m = jnp.maximum(jnp.maximum(a, b), c)
        ea, eb, ec = jnp.exp(a - m), jnp.exp(b - m), jnp.exp(c - m)
        inv = 1.0 / (ea + eb + ec)
        wa, wb, wc = ea * inv, eb * inv, ec * inv
        o_ref[...] = wa * o1r[...] + wb * o2r[...] + wc * o3r[...]
        w1r[...] = wa
        w2r[...] = wb
        w3r[...] = wc

    spec = pl.BlockSpec((tm, C), lambda i: (i, 0))
    shp = jax.ShapeDtypeStruct((T, C), F32)
    return pl.pallas_call(body, grid=(T // tm,), in_specs=[spec] * 6, out_specs=[spec] * 4, out_shape=[shp] * 4,
                          compiler_params=_cparams(1), name=name)(o1, o2, o3, l1, l2, l3)


def _qk_norm_bwd(raw, dn, gain, m0):
    r = _head_rms(raw, m0)
    h = raw * r
    dh = dn * gain
    d_raw = r * (dh - h * (_half_sum(dh * h, m0) * (1.0 / HEAD_DIM)))
    return d_raw, jnp.sum(dn * h, axis=0, keepdims=True)


def _acc_rows(ref, val):
    @pl.when(pl.program_id(0) == 0)
    def _():
        ref[...] = val

    @pl.when(pl.program_id(0) > 0)
    def _():
        ref[...] += val


def assemble_odd(parts, qkv, q_gain2, k_gain2, *, tm, name):
    T, C = parts[0][0].shape
    nbk = C // LANES

    def body(*refs):
        qkv_ref, qg_ref, kg_ref, o_ref, dqg_ref, dkg_ref = refs[9:]
        m0 = _lane0()
        sums = [refs[j][...] + refs[3 + j][...] + refs[6 + j][...] for j in range(3)]
        o_ref[:, pl.ds(2 * C, C)] = sums[2].astype(o_ref.dtype)
        for j, (g_ref, acc_ref) in enumerate(((qg_ref, dqg_ref), (kg_ref, dkg_ref))):
            dgain = jnp.zeros((1, LANES), F32)
            for b in range(nbk):
                cols = pl.ds(C * j + LANES * b, LANES)
                d_raw, part = _qk_norm_bwd(qkv_ref[:, cols], sums[j][:, LANES * b:LANES * (b + 1)], g_ref[...], m0)
                o_ref[:, cols] = d_raw.astype(o_ref.dtype)
                dgain = dgain + part
            _acc_rows(acc_ref, dgain)

    spec = pl.BlockSpec((tm, C), lambda i: (i, 0))
    gspec = pl.BlockSpec((1, LANES), lambda i: (0, 0))
    flat = [parts[p][j] for p in range(3) for j in range(3)]
    row = jax.ShapeDtypeStruct((1, LANES), F32)
    return pl.pallas_call(body, grid=(T // tm,),
                          in_specs=[spec] * 9 + [pl.BlockSpec((tm, 2 * C), lambda i: (i, 0)), gspec, gspec],
                          out_specs=[pl.BlockSpec((tm, 3 * C), lambda i: (i, 0)), gspec, gspec],
                          out_shape=[jax.ShapeDtypeStruct((T, 3 * C), BF16), row, row],
                          compiler_params=_cparams(1), name=name)(*flat, qkv, q_gain2, k_gain2)


def assemble_even(dqa, dka4, dva4, dqb, dkb, dvb, qkv, q_gain2, k_gain2, *, tm, name):
    T = dqa.shape[0]
    W = 512
    QK = 768

    def body(dqa_r, dka_r, dva_r, dqb_r, dkb_r, dvb_r, qkv_ref, qg_ref, kg_ref, o_ref, dqg_ref, dkg_ref):
        m0 = _lane0()
        ka = dka_r[...]
        va = dva_r[...]
        dqn = dqa_r[...]
        dgain = jnp.zeros((1, LANES), F32)
        for b in range(W // LANES):
            cols = pl.ds(LANES * b, LANES)
            d_raw, part = _qk_norm_bwd(qkv_ref[:, cols], dqn[:, LANES * b:LANES * (b + 1)], qg_ref[...], m0)
            o_ref[:, cols] = d_raw.astype(o_ref.dtype)
            dgain = dgain + part
        _acc_rows(dqg_ref, dgain)
        dkn = ka[:, 0:128] + ka[:, 128:256] + ka[:, 256:384] + ka[:, 384:512]
        d_raw, part = _qk_norm_bwd(qkv_ref[:, pl.ds(W, LANES)], dkn, kg_ref[...], m0)
        dt = o_ref.dtype
        o_ref[:, pl.ds(W, LANES)] = d_raw.astype(dt)
        _acc_rows(dkg_ref, part)
        o_ref[:, pl.ds(640, LANES)] = (va[:, 0:128] + va[:, 128:256] + va[:, 256:384] + va[:, 384:512]).astype(dt)
        o_ref[:, pl.ds(768, W)] = dqb_r[...].astype(dt)
        o_ref[:, pl.ds(1280, W)] = dkb_r[...].astype(dt)
        o_ref[:, pl.ds(1792, W)] = dvb_r[...].astype(dt)

    spec = pl.BlockSpec((tm, W), lambda i: (i, 0))
    gspec = pl.BlockSpec((1, LANES), lambda i: (0, 0))
    row = jax.ShapeDtypeStruct((1, LANES), F32)
    return pl.pallas_call(body, grid=(T // tm,),
                          in_specs=[spec] * 6 + [pl.BlockSpec((tm, QK), lambda i: (i, 0)), gspec, gspec],
                          out_specs=[pl.BlockSpec((tm, 2304), lambda i: (i, 0)), gspec, gspec],
                          out_shape=[jax.ShapeDtypeStruct((T, 2304), BF16), row, row],
                          compiler_params=_cparams(1), name=name)(dqa, dka4, dva4, dqb, dkb, dvb, qkv, q_gain2, k_gain2)


STICK_T = 256
STICK_DEAD = -110.0


def _split_bf16(x):
    hi = x.astype(BF16)
    lo = (x - hi.astype(F32)).astype(BF16)
    return hi, lo


def _stick_logits(qm, kt, scale, diag):
    n = STICK_T
    row = lax.broadcasted_iota(jnp.int32, (n, n), 0)
    col = lax.broadcasted_iota(jnp.int32, (n, n), 1)
    mask = col < row + jnp.where(diag, 0, n)
    z = _dot_nt(qm, kt) * scale
    lneg = -(jnp.maximum(z, 0.0) + jnp.log(1.0 + jnp.exp(-jnp.abs(z))))
    lpos = z + lneg
    lk = jnp.where(mask, lneg, 0.0)
    return mask, lpos, lneg, lk


def _cumsum_mm(x, tri):
    hi, lo = _split_bf16(x)
    return _dot(hi, tri) + _dot(lo, tri)


def stick_fwd(qkv, *, q_blk, k_blk, v_blk, n_pairs, name, riders=None, rider_args=()):
    T = qkv.shape[0]
    n = STICK_T
    nq = T // n
    scale = HEAD_DIM ** -0.5
    nc = riders.n if riders is not None else 0
    n_steps = n_pairs * nq
    stage_at = (0, (5 * n_steps) // 8, (15 * n_steps) // 16, n_steps - 1)

    def body(*refs):
        q_ref, k_ref, v_ref = refs[:3]
        x_refs, o_ref = refs[3:3 + nc], refs[3 + nc]
        out_refs, sems = refs[4 + nc:4 + 2 * nc], refs[4 + 2 * nc:]
        i = pl.program_id(1)
        step_id = pl.program_id(0) * nq + i

        def ride(which):
            if riders is not None:
                @pl.when(step_id == stage_at[which])
                def _():
                    riders.stage(which, x_refs, out_refs, sems)

        ride(0)
        ride(1)
        m0 = _lane0()
        r2 = lax.broadcasted_iota(jnp.int32, (n, n), 0)
        c2 = lax.broadcasted_iota(jnp.int32, (n, n), 1)
        tri_after = (r2 > c2).astype(BF16)
        qv = q_ref[...]
        out = jnp.zeros((n, LANES), F32)
        for e in range(2):
            qm = _mask_half(qv, m0, e).astype(BF16)

            def alive(st):
                t, _, carry = st
                return (t <= i) & (jnp.max(carry) > STICK_DEAD)

            def step(st, e=e, qm=qm):
                t, acc, carry = st
                start = pl.multiple_of((i - t) * n, n)
                kt = k_ref[pl.ds(start, n), :].astype(BF16)
                vt = _mask_half(v_ref[pl.ds(start, n), :], m0, e).astype(BF16)
                mask, lpos, _, lk = _stick_logits(qm, kt, scale, t == 0)
                after = _cumsum_mm(lk, tri_after) + carry
                a = jnp.where(mask, jnp.exp(lpos + after), 0.0)
                acc = acc + _dot(a.astype(BF16), vt)
                carry = carry + jnp.sum(lk, axis=-1, keepdims=True)
                return t + 1, acc, carry

            _, acc, _ = lax.while_loop(alive, step, (jnp.int32(0), jnp.zeros((n, LANES), F32),
                                                     jnp.zeros((n, 1), F32)))
            out = out + acc
        o_ref[...] = out
        ride(2)
        ride(3)

    outs = pl.pallas_call(
        body, grid=(n_pairs, nq),
        in_specs=[pl.BlockSpec((n, LANES), lambda p, i: (i, q_blk + p)),
                  pl.BlockSpec((T, LANES), lambda p, i: (0, k_blk + p)),
                  pl.BlockSpec((T, LANES), lambda p, i: (0, v_blk + p))] + [_ANY] * nc,
        out_specs=[pl.BlockSpec((n, LANES), lambda p, i: (i, p))] + [_ANY] * nc,
        out_shape=[jax.ShapeDtypeStruct((T, n_pairs * LANES), F32)] + (riders.shapes if nc else []),
        scratch_shapes=riders.sems if nc else [],
        compiler_params=_cparams(2), name=name)(qkv, qkv, qkv, *rider_args)
    return outs[0], list(outs[1:])


def stick_bwd(qkv, do, *, q_blk, k_blk, v_blk, do_blk, n_pairs, name, riders=None, rider_args=()):
    T = qkv.shape[0]
    n = STICK_T
    nq = T // n
    scale = HEAD_DIM ** -0.5
    nc = riders.n if riders is not None else 0

    def body(*refs):
        q_ref, k_ref, v_ref, do_ref = refs[:4]
        pre_refs = refs[4:4 + nc]
        dq_ref, dk_ref, dv_ref = refs[4 + nc:7 + nc]
        land_refs = refs[7 + nc:7 + 2 * nc]
        a_keep, g_keep, s_keep = refs[7 + 2 * nc:10 + 2 * nc]
        sems = refs[10 + 2 * nc:]
        i = pl.program_id(1)
        first_step = (pl.program_id(0) == 0) & (i == 0)
        last_step = (pl.program_id(0) == n_pairs - 1) & (i == nq - 1)
        m0 = _lane0()
        r2 = lax.broadcasted_iota(jnp.int32, (n, n), 0)
        c2 = lax.broadcasted_iota(jnp.int32, (n, n), 1)
        tri_after = (r2 > c2).astype(BF16)
        tri_from = (r2 >= c2).astype(BF16)

        if riders is not None:
            @pl.when(first_step)
            def _():
                riders.start(pre_refs, land_refs, sems)

        @pl.when(i == 0)
        def _():
            dk_ref[...] = jnp.zeros_like(dk_ref)
            dv_ref[...] = jnp.zeros_like(dv_ref)

        qv = q_ref[...]
        dov = do_ref[...]
        dq_out = jnp.zeros((n, LANES), F32)
        for e in range(2):
            qm = _mask_half(qv, m0, e).astype(BF16)
            dom = _mask_half(dov, m0, e).astype(BF16)

            def alive(st):
                t, carry, _ = st
                return (t <= i) & (jnp.max(carry) > STICK_DEAD)

            def scan(st, qm=qm, dom=dom):
                t, carry, gtot = st
                start = pl.multiple_of((i - t) * n, n)
                kt = k_ref[pl.ds(start, n), :].astype(BF16)
                vt = v_ref[pl.ds(start, n), :].astype(BF16)
                mask, lpos, lneg, lk = _stick_logits(qm, kt, scale, t == 0)
                a = jnp.where(mask, jnp.exp(lpos + _cumsum_mm(lk, tri_after) + carry), 0.0)
                g = _dot_nt(dom, vt) * a
                a_keep[t] = a.astype(BF16)
                g_keep[t] = g
                s_keep[t] = jnp.exp(lneg).astype(BF16)
                return (t + 1, carry + jnp.sum(lk, axis=-1, keepdims=True),
                        gtot + jnp.sum(g, axis=-1, keepdims=True))

            z1 = jnp.zeros((n, 1), F32)
            n_live, _, gtot = lax.while_loop(alive, scan, (jnp.int32(0), z1, z1))

            def step(t, st, e=e, qm=qm, dom=dom, gtot=gtot):
                dq_acc, gright = st
                start = pl.multiple_of((i - t) * n, n)
                g = g_keep[t]
                sneg = s_keep[t].astype(F32)
                before = gtot - (_cumsum_mm(g, tri_from) + gright)
                mask = c2 < r2 + jnp.where(t == 0, 0, n)
                dz = jnp.where(mask, g * sneg - before * (1.0 - sneg), 0.0) * scale
                dzb = dz.astype(BF16)
                dq_acc = dq_acc + _dot(dzb, _mask_half(k_ref[pl.ds(start, n), :], m0, e).astype(BF16))
                dk_ref[pl.ds(start, n), :] += _dot_tn(dzb, qm)
                dv_ref[pl.ds(start, n), :] += _dot_tn(a_keep[t], dom)
                return dq_acc, gright + jnp.sum(g, axis=-1, keepdims=True)

            dq_acc, _ = lax.fori_loop(0, n_live, step, (jnp.zeros((n, LANES), F32), z1))
            dq_out = dq_out + dq_acc
        dq_ref[...] = dq_out

        if riders is not None:
            @pl.when(last_step)
            def _():
                riders.finish(pre_refs, land_refs, sems)

    tile = pl.BlockSpec((n, LANES), lambda p, i: (i, p))
    whole = pl.BlockSpec((T, LANES), lambda p, i: (0, p))
    shp = jax.ShapeDtypeStruct((T, n_pairs * LANES), F32)
    outs = pl.pallas_call(
        body, grid=(n_pairs, nq),
        in_specs=[pl.BlockSpec((n, LANES), lambda p, i: (i, q_blk + p)),
                  pl.BlockSpec((T, LANES), lambda p, i: (0, k_blk + p)),
                  pl.BlockSpec((T, LANES), lambda p, i: (0, v_blk + p)),
                  pl.BlockSpec((n, LANES), lambda p, i: (i, do_blk + p))] + [_ANY] * nc,
        out_specs=[tile, whole, whole] + [_ANY] * nc,
        out_shape=[shp, shp, shp] + (riders.shapes if nc else []),
        scratch_shapes=[pltpu.VMEM((nq, n, n), BF16), pltpu.VMEM((nq, n, n), F32), pltpu.VMEM((nq, n, n), BF16)]
        + (riders.sems if nc else []),
        compiler_params=_cparams(2), name=name)(qkv, qkv, qkv, do, *rider_args)
    return outs[0], outs[1], outs[2], list(outs[3:])


def _xnorm(x):
    r = lax.rsqrt(jnp.mean(x * x, axis=-1, keepdims=True) + RMS_EPS)
    return r, x * r


def xattn_fwd(qraw, kvraw, q_gain, k_gain, *, tm, name):
    T = qraw.shape[0]
    scale = X_HEAD_DIM ** -0.5
    W = X_HEADS * X_HEAD_DIM

    def body(q_ref, kv_ref, qg_ref, kg_ref, o_ref):
        for h in range(X_HEADS):
            cs = pl.ds(X_HEAD_DIM * h, X_HEAD_DIM)
            _, qh = _xnorm(q_ref[:, cs])
            _, kh = _xnorm(kv_ref[:, cs])
            qn = (qh * qg_ref[...]).astype(BF16)
            kn = (kh * kg_ref[...]).astype(BF16)
            v = kv_ref[:, pl.ds(W + X_HEAD_DIM * h, X_HEAD_DIM)].astype(BF16)
            s = _dot_nt(qn, kn) * scale
            m = jnp.max(s, axis=-1, keepdims=True)
            p = jnp.exp(s - m)
            p = p / jnp.sum(p, axis=-1, keepdims=True)
            o_ref[:, cs] = _dot(p.astype(BF16), v)

    gspec = pl.BlockSpec((1, X_HEAD_DIM), lambda i: (0, 0))
    return pl.pallas_call(
        body, grid=(T // tm,),
        in_specs=[pl.BlockSpec((tm, W), lambda i: (i, 0)), pl.BlockSpec((MEM_LEN, 2 * W), lambda i: (0, 0)),
                  gspec, gspec],
        out_specs=pl.BlockSpec((tm, W), lambda i: (i, 0)),
        out_shape=jax.ShapeDtypeStruct((T, W), F32),
        compiler_params=_cparams(1), name=name)(qraw, kvraw, q_gain, k_gain)


def xattn_bwd(qraw, kvraw, q_gain, k_gain, do, o, *, tm, name):
    T = qraw.shape[0]
    nt = T // tm
    scale = X_HEAD_DIM ** -0.5
    W = X_HEADS * X_HEAD_DIM

    def body(q_ref, kv_ref, qg_ref, kg_ref, do_ref, o_ref, dq_ref, dkv_ref, dqg_ref, dkg_ref, dkn_ref):
        i = pl.program_id(0)

        @pl.when(i == 0)
        def _():
            dkv_ref[...] = jnp.zeros_like(dkv_ref)
            dkn_ref[...] = jnp.zeros_like(dkn_ref)
            dqg_ref[...] = jnp.zeros_like(dqg_ref)
            dkg_ref[...] = jnp.zeros_like(dkg_ref)

        qg = qg_ref[...]
        kg = kg_ref[...]
        dqg_acc = jnp.zeros((1, X_HEAD_DIM), F32)
        for h in range(X_HEADS):
            cs = pl.ds(X_HEAD_DIM * h, X_HEAD_DIM)
            vs = pl.ds(W + X_HEAD_DIM * h, X_HEAD_DIM)
            rq, qh = _xnorm(q_ref[:, cs])
            _, kh = _xnorm(kv_ref[:, cs])
            qn = (qh * qg).astype(BF16)
            kn = (kh * kg).astype(BF16)
            v = kv_ref[:, vs].astype(BF16)
            s = _dot_nt(qn, kn) * scale
            m = jnp.max(s, axis=-1, keepdims=True)
            p = jnp.exp(s - m)
            p = p / jnp.sum(p, axis=-1, keepdims=True)
            dov = do_ref[:, cs]
            delta = jnp.sum(dov * o_ref[:, cs], axis=-1, keepdims=True)
            dob = dov.astype(BF16)
            ds = (p * (_dot_nt(dob, v) - delta)).astype(BF16)
            dqn = _dot(ds, kn) * scale
            dkn_ref[:, cs] += _dot_tn(ds, qn) * scale
            dkv_ref[:, vs] += _dot_tn(p.astype(BF16), dob)
            dqg_acc = dqg_acc + jnp.sum(dqn * qh, axis=0, keepdims=True)
            dqh = dqn * qg
            dq_ref[:, cs] = rq * (dqh - qh * jnp.mean(dqh * qh, axis=-1, keepdims=True))
        dqg_ref[...] += dqg_acc

        @pl.when(i == nt - 1)
        def _():
            dkg_acc = jnp.zeros((1, X_HEAD_DIM), F32)
            for h in range(X_HEADS):
                cs = pl.ds(X_HEAD_DIM * h, X_HEAD_DIM)
                rk, kh = _xnorm(kv_ref[:, cs])
                dkn = dkn_ref[:, cs]
                dkg_acc = dkg_acc + jnp.sum(dkn * kh, axis=0, keepdims=True)
                dkh = dkn * kg
                dkv_ref[:, cs] = rk * (dkh - kh * jnp.mean(dkh * kh, axis=-1, keepdims=True))
            dkg_ref[...] = dkg_acc

    gspec = pl.BlockSpec((1, X_HEAD_DIM), lambda i: (0, 0))
    tile = pl.BlockSpec((tm, W), lambda i: (i, 0))
    kvspec = pl.BlockSpec((MEM_LEN, 2 * W), lambda i: (0, 0))
    grow = jax.ShapeDtypeStruct((1, X_HEAD_DIM), F32)
    return pl.pallas_call(
        body, grid=(nt,), in_specs=[tile, kvspec, gspec, gspec, tile, tile],
        out_specs=[tile, kvspec, gspec, gspec],
        out_shape=[jax.ShapeDtypeStruct((T, W), F32), jax.ShapeDtypeStruct((MEM_LEN, 2 * W), F32), grow, grow],
        scratch_shapes=[pltpu.VMEM((MEM_LEN, W), F32)],
        compiler_params=_cparams(1), name=name)(qraw, kvraw, q_gain, k_gain, do, o)


_ANY = pl.BlockSpec(memory_space=pl.ANY)


def _my_pos():
    return lax.axis_index("x"), lax.axis_index("y"), lax.axis_index("c")


def _pieces(arrays, chunks):
    out = []
    for a, (arr, n) in enumerate(zip(arrays, chunks)):
        rc = arr.shape[-2] // n
        out += [(a, pl.ds(ch * rc, rc)) for ch in range(n)]
    return out


class GatherBlocks:
    N_STAGES = 4

    def __init__(self, blks, chunks):
        self.shapes = [jax.ShapeDtypeStruct((N_DEV,) + b.shape, b.dtype) for b in blks]
        self.n = len(blks)
        self.pieces = _pieces(blks, chunks)
        n_p = len(self.pieces)
        self.sems = [pltpu.SemaphoreType.DMA((7 * n_p,)), pltpu.SemaphoreType.DMA((7 * n_p,)),
                     pltpu.SemaphoreType.DMA((n_p,))]

    def stage(self, which, x_refs, out_refs, sems):
        send_sems, recv_sems, local_sems = sems
        pieces, n_p = self.pieces, len(self.pieces)
        x, y, c = _my_pos()
        me, sibling = (x, y, c), (x, y, 1 - c)
        chips = [(1 - x, y), (x, 1 - y), (1 - x, 1 - y)]
        xn, yn, dg = [(*chip, c) for chip in chips]
        ps = range(n_p)

        def slot(block, p):
            px, py, pc = block
            a, rows = pieces[p]
            return out_refs[a].at[4 * px + 2 * py + pc, rows]

        def own(p):
            a, rows = pieces[p]
            return x_refs[a].at[rows]

        def copy(k, p, block, to, from_input=False):
            return pltpu.make_async_remote_copy(
                src_ref=own(p) if from_input else slot(block, p), dst_ref=slot(block, p),
                send_sem=send_sems.at[k * n_p + p], recv_sem=recv_sems.at[k * n_p + p],
                device_id=to, device_id_type=MESH)

        mine = [pltpu.make_async_copy(own(p), slot(me, p), local_sems.at[p]) for p in ps]
        first = [copy(k, p, me, to, from_input=True) for p in ps for k, to in ((1, xn), (2, yn), (0, sibling))]
        on_x = [copy(3, p, xn, yn) for p in ps if p % 2 == 0] + [copy(4, p, xn, sibling) for p in ps]
        on_y = [copy(3, p, yn, xn) for p in ps if p % 2 == 1] + [copy(5, p, yn, sibling) for p in ps]
        on_d = [copy(6, p, dg, sibling) for p in ps]
        if which == 0:
            for cp in first + mine:
                cp.start()
        elif which == 1:
            for p in ps:
                copy(1, p, xn, me).wait_recv()
                if p % 2 == 0:
                    copy(3, p, xn, yn).start()
                copy(4, p, xn, sibling).start()
                copy(2, p, yn, me).wait_recv()
                if p % 2 == 1:
                    copy(3, p, yn, xn).start()
                copy(5, p, yn, sibling).start()
        elif which == 2:
            for p in ps:
                copy(3, p, dg, me).wait_recv()
                copy(6, p, dg, sibling).start()
        else:
            for p in ps:
                copy(0, p, sibling, me).wait_recv()
            for k, chip in zip((4, 5, 6), chips):
                for p in ps:
                    copy(k, p, (*chip, 1 - c), me).wait_recv()
            for cp in first + on_x + on_y + on_d:
                cp.wait_send()
            for cp in mine:
                cp.wait()


def gather_blocks(blks, chunks, *, name):
    gb = GatherBlocks(blks, chunks)
    n = gb.n

    def body(*refs):
        x_refs, out_refs, sems = refs[:n], refs[n:2 * n], refs[2 * n:]
        for which in range(gb.N_STAGES):
            gb.stage(which, x_refs, out_refs, sems)

    return pl.pallas_call(body, out_shape=gb.shapes, in_specs=[_ANY] * n, out_specs=[_ANY] * n,
                          scratch_shapes=gb.sems, name=name)(*blks)


def gather_small(small, *, name):
    S, C = small.shape

    def body(s_ref, out_ref, send_sems, recv_sems, local_sem):
        x, y, c = _my_pos()
        my_id = 4 * x + 2 * y + c

        def copy(k, slot):
            px, py, pc = x ^ ((k >> 2) & 1), y ^ ((k >> 1) & 1), c ^ (k & 1)
            dst = my_id if slot == "mine" else 4 * px + 2 * py + pc
            return pltpu.make_async_remote_copy(
                src_ref=s_ref, dst_ref=out_ref.at[dst], send_sem=send_sems.at[k - 1], recv_sem=recv_sems.at[k - 1],
                device_id=(px, py, pc), device_id_type=MESH)

        own = pltpu.make_async_copy(s_ref, out_ref.at[my_id], local_sem)
        own.start()
        sends = [copy(k, "mine") for k in range(1, N_DEV)]
        for cp in sends:
            cp.start()
        for k in range(1, N_DEV):
            copy(k, "theirs").wait_recv()
        for cp in sends:
            cp.wait_send()
        own.wait()

    dma7 = pltpu.SemaphoreType.DMA((7,))
    return pl.pallas_call(
        body, out_shape=jax.ShapeDtypeStruct((N_DEV, S, C), small.dtype), in_specs=[_ANY], out_specs=_ANY,
        scratch_shapes=[dma7, dma7, pltpu.SemaphoreType.DMA], name=name)(small)


class PairExchange:
    def __init__(self, bigs, chunks):
        self.shapes = [jax.ShapeDtypeStruct((4,) + b.shape[1:], b.dtype) for b in bigs]
        self.n = len(bigs)
        self.pieces = _pieces(bigs, chunks)
        n_p = len(self.pieces)
        self.sems = [pltpu.SemaphoreType.DMA((4 * n_p,)), pltpu.SemaphoreType.DMA((4 * n_p,))]

    def _copies(self, big_refs, out_refs, sems):
        send_sems, recv_sems = sems
        n_p = len(self.pieces)
        x, y, c = _my_pos()

        def copy(b, p):
            a, rows = self.pieces[p]
            return pltpu.make_async_remote_copy(
                src_ref=big_refs[a].at[2 * b + (1 - c), rows], dst_ref=out_refs[a].at[b, rows],
                send_sem=send_sems.at[b * n_p + p], recv_sem=recv_sems.at[b * n_p + p],
                device_id=(x, y, 1 - c), device_id_type=MESH)

        return [copy(b, p) for b in range(4) for p in range(n_p)]

    def start(self, big_refs, out_refs, sems):
        for cp in self._copies(big_refs, out_refs, sems):
            cp.start()

    def finish(self, big_refs, out_refs, sems):
        cps = self._copies(big_refs, out_refs, sems)
        for cp in cps:
            cp.wait_recv()
        for cp in cps:
            cp.wait_send()


def _standalone(exchange, args, name):
    n = exchange.n

    def body(*refs):
        exchange.start(refs[:n], refs[n:2 * n], refs[2 * n:])
        exchange.finish(refs[:n], refs[n:2 * n], refs[2 * n:])

    return pl.pallas_call(body, out_shape=exchange.shapes, in_specs=[_ANY] * n, out_specs=[_ANY] * n,
                          scratch_shapes=exchange.sems, name=name)(*args)


class Riding:
    def __init__(self, riders):
        self.riders = [(ex, list(args)) for ex, args in riders]
        self.args = [a for _, args in self.riders for a in args]
        self.in_specs = [_ANY] * len(self.args)
        self.out_shapes = [s for ex, _ in self.riders for s in ex.shapes]
        self.out_specs = [_ANY] * len(self.out_shapes)
        self.scratch = [s for ex, _ in self.riders for s in ex.sems]

    def wrap(self, body, n_in, n_out, n_scratch, is_first, is_last):
        def wrapped(*refs):
            k = 0
            core = list(refs[:n_in])
            k = n_in
            r_in = []
            for ex, _ in self.riders:
                r_in.append(refs[k:k + ex.n])
                k += ex.n
            core += refs[k:k + n_out]
            k += n_out
            r_out = []
            for ex, _ in self.riders:
                r_out.append(refs[k:k + ex.n])
                k += ex.n
            core += refs[k:k + n_scratch]
            k += n_scratch
            r_sem = []
            for ex, _ in self.riders:
                r_sem.append(refs[k:k + len(ex.sems)])
                k += len(ex.sems)

            @pl.when(is_first())
            def _():
                for (ex, _), a, b, s in zip(self.riders, r_in, r_out, r_sem):
                    ex.start(a, b, s)

            body(*core)

            @pl.when(is_last())
            def _():
                for (ex, _), a, b, s in zip(self.riders, r_in, r_out, r_sem):
                    ex.finish(a, b, s)

        return wrapped

    def split(self, outs, n_out):
        core, rest, per = list(outs[:n_out]), list(outs[n_out:]), []
        for ex, _ in self.riders:
            per.append(rest[:ex.n])
            rest = rest[ex.n:]
        return core, per


def pair_sum(big, sib, c, *, tr, name):
    _, R, C = big.shape

    def body(c_ref, a_ref, s_ref, o_ref):
        o_ref[...] = (a_ref[...].astype(F32) + s_ref[...].astype(F32)).astype(o_ref.dtype)

    grid_spec = pltpu.PrefetchScalarGridSpec(
        num_scalar_prefetch=1, grid=(4, R // tr),
        in_specs=[pl.BlockSpec((None, tr, C), lambda b, i, c_ref: (2 * b + c_ref[0], i, 0)),
                  pl.BlockSpec((None, tr, C), lambda b, i, c_ref: (b, i, 0))],
        out_specs=pl.BlockSpec((None, tr, C), lambda b, i, c_ref: (b, i, 0)))
    return pl.pallas_call(body, grid_spec=grid_spec, out_shape=jax.ShapeDtypeStruct((4, R, C), big.dtype),
                          compiler_params=_cparams(2), name=name)(c.reshape(1).astype(jnp.int32), big, sib)


class ChipScatter:
    def __init__(self, pres, chunks):
        self.shapes = [jax.ShapeDtypeStruct(p.shape, p.dtype) for p in pres]
        self.n = len(pres)
        self.pieces = _pieces(pres, chunks)
        n_p = len(self.pieces)
        self.sems = [pltpu.SemaphoreType.DMA((3 * n_p,)), pltpu.SemaphoreType.DMA((3 * n_p,)),
                     pltpu.SemaphoreType.DMA((n_p,))]

    def _copies(self, pre_refs, out_refs, sems):
        send_sems, recv_sems, local_sems = sems
        n_p = len(self.pieces)
        x, y, c = _my_pos()
        my_chip = 2 * x + y
        chips = [(1 - x, y), (x, 1 - y), (1 - x, 1 - y)]

        def copy(j, p, slot):
            px, py = chips[j]
            a, rows = self.pieces[p]
            src_slot, dst_slot = (2 * px + py, my_chip) if slot == "mine" else (my_chip, 2 * px + py)
            return pltpu.make_async_remote_copy(
                src_ref=pre_refs[a].at[src_slot, rows], dst_ref=out_refs[a].at[dst_slot, rows],
                send_sem=send_sems.at[j * n_p + p], recv_sem=recv_sems.at[j * n_p + p],
                device_id=(px, py, c), device_id_type=MESH)

        own = [pltpu.make_async_copy(pre_refs[a].at[my_chip, rows], out_refs[a].at[my_chip, rows], local_sems.at[p])
               for p, (a, rows) in enumerate(self.pieces)]
        sends = [copy(j, p, "mine") for j in range(3) for p in range(n_p)]
        recvs = [copy(j, p, "theirs") for j in range(3) for p in range(n_p)]
        return own, sends, recvs

    def start(self, pre_refs, out_refs, sems):
        own, sends, _ = self._copies(pre_refs, out_refs, sems)
        for cp in sends + own:
            cp.start()

    def finish(self, pre_refs, out_refs, sems):
        own, sends, recvs = self._copies(pre_refs, out_refs, sems)
        for cp in recvs:
            cp.wait_recv()
        for cp in sends:
            cp.wait_send()
        for cp in own:
            cp.wait()


def chip_scatter(pres, chunks, *, name):
    cs = ChipScatter(pres, chunks)
    n = cs.n

    def body(*refs):
        pre_refs, out_refs, sems = refs[:n], refs[n:2 * n], refs[2 * n:]
        cs.start(pre_refs, out_refs, sems)
        cs.finish(pre_refs, out_refs, sems)

    return pl.pallas_call(body, out_shape=cs.shapes, in_specs=[_ANY] * n, out_specs=[_ANY] * n,
                          scratch_shapes=cs.sems, name=name)(*pres)


def sibling_send(blks, chunks, *, name):
    n = len(blks)
    pieces = _pieces(blks, chunks)
    n_p = len(pieces)

    def body(*refs):
        x_refs, out_refs = refs[:n], refs[n:2 * n]
        send_sems, recv_sems = refs[2 * n:]
        x, y, c = _my_pos()
        cps = [pltpu.make_async_remote_copy(
            src_ref=x_refs[a].at[rows], dst_ref=out_refs[a].at[rows], send_sem=send_sems.at[p],
            recv_sem=recv_sems.at[p], device_id=(x, y, 1 - c), device_id_type=MESH)
            for p, (a, rows) in enumerate(pieces)]
        for cp in cps:
            cp.start()
        for cp in cps:
            cp.wait_recv()
        for cp in cps:
            cp.wait_send()

    return pl.pallas_call(
        body, out_shape=[jax.ShapeDtypeStruct(b.shape, b.dtype) for b in blks],
        in_specs=[_ANY] * n, out_specs=[_ANY] * n,
        scratch_shapes=[pltpu.SemaphoreType.DMA((n_p,)), pltpu.SemaphoreType.DMA((n_p,))],
        name=name)(*blks)


def reduce_slots(land, *, tr, name):
    n, R, C = land.shape

    def body(l_ref, o_ref):
        acc = l_ref[0].astype(F32)
        for s in range(1, n):
            acc = acc + l_ref[s].astype(F32)
        o_ref[...] = acc

    return pl.pallas_call(
        body, grid=(R // tr,), in_specs=[pl.BlockSpec((n, tr, C), lambda i: (0, i, 0))],
        out_specs=pl.BlockSpec((tr, C), lambda i: (i, 0)), out_shape=jax.ShapeDtypeStruct((R, C), F32),
        compiler_params=_cparams(1), name=name)(land)


TM = 512


def _tk(d):
    return min(d.shape[0], 1024)


def ffn_fwd_fused(x, g, wgu, wd, *, tm, name, riders=None, rider_args=()):
    T, Dm = x.shape
    nb, _, cb = wgu.shape
    nh = nb // 2
    Fd = nh * cb
    nc = riders.n if riders is not None else 0
    n_steps = T // tm
    stage_at = (0, n_steps // 2, (13 * n_steps) // 16, n_steps - 1)

    def body(*refs):
        x_ref, g_ref, wgu_ref, wd_ref = refs[:4]
        r_in = refs[4:4 + nc]
        o_ref, gu_ref, h_ref = refs[4 + nc:7 + nc]
        r_out, sems = refs[7 + nc:7 + 2 * nc], refs[7 + 2 * nc:]

        def ride(which):
            if riders is not None:
                @pl.when(pl.program_id(0) == stage_at[which])
                def _():
                    riders.stage(which, r_in, r_out, sems)

        ride(0)
        ride(1)
        xv = x_ref[...]
        r = lax.rsqrt(jnp.mean(xv * xv, axis=-1, keepdims=True) + RMS_EPS)
        hb = (xv * r * g_ref[...]).astype(BF16)
        h_ref[...] = hb
        acc = jnp.zeros((tm, Dm), F32)
        for jj in range(nh):
            cols = pl.ds(cb * jj, cb)
            gate = _dot(hb, wgu_ref[jj]).astype(BF16)
            up = _dot(hb, wgu_ref[nh + jj]).astype(BF16)
            gu_ref[0, :, cols] = gate
            gu_ref[1, :, cols] = up
            gv = gate.astype(F32)
            act = (gv * _sigmoid(gv) * up.astype(F32)).astype(BF16)
            acc = acc + _dot(act, wd_ref[cols, :])
        o_ref[...] = xv + 0.5 * acc
        ride(2)
        ride(3)

    outs = pl.pallas_call(
        body, grid=(n_steps,),
        in_specs=[pl.BlockSpec((tm, Dm), lambda i: (i, 0)), pl.BlockSpec((1, Dm), lambda i: (0, 0)),
                  pl.BlockSpec((nb, Dm, cb), lambda i: (0, 0, 0)), pl.BlockSpec((Fd, Dm), lambda i: (0, 0))]
        + [_ANY] * nc,
        out_specs=[pl.BlockSpec((tm, Dm), lambda i: (i, 0)), pl.BlockSpec((2, tm, Fd), lambda i: (0, i, 0)),
                   pl.BlockSpec((tm, Dm), lambda i: (i, 0))] + [_ANY] * nc,
        out_shape=[jax.ShapeDtypeStruct((T, Dm), F32), jax.ShapeDtypeStruct((2, T, Fd), BF16),
                   jax.ShapeDtypeStruct((T, Dm), BF16)] + (riders.shapes if nc else []),
        scratch_shapes=riders.sems if nc else [],
        compiler_params=_cparams(1), name=name)(x, g, wgu, wd, *rider_args)
    return outs[0], outs[1], outs[2], list(outs[3:])


def ffn_fwd(x, g, wgu, wd, tag, riders=None, rider_args=()):
    xo, gu, h, rode = ffn_fwd_fused(x, g, wgu, wd, tm=256, name=f"{tag}_fwd", riders=riders, rider_args=rider_args)
    return xo, (x, gu, h), rode


def ffn_bwd(d, saved, g, wgu, wd, tag, ride_bact=None, ride_dwgu=None, before_dx=None):
    x, gu, h = saved
    dgu, dwd, *rode_a = ffn_bwd_act(d, wd, gu, tm=TM, tn=1408, name=f"{tag}_bact", riding=ride_bact)
    dwgu = mm_tn(h, dgu, scale=1.0, a_split=False, b_split=True, tm=TM, tn=1408, tk=_tk(d), out_blocked=True,
                 name=f"{tag}_dwgu", riding=ride_dwgu)
    rode_g = []
    if ride_dwgu is not None:
        dwgu, *rode_g = dwgu
    riding = before_dx(dwgu, dwd) if before_dx is not None else None
    dx, dg, *rode_x = mm_nt_normbwd(dgu, wgu, x, g, d, a_split=True, tm=_tk(d), tk=1408, name=f"{tag}_dx",
                                    riding=riding)
    return dx, dg, dwgu, dwd, (rode_a, rode_g, rode_x)


def _tile2(v):
    return jnp.concatenate([v, v], axis=-1).reshape(1, LANES)


def _fold2(v):
    return v[:, :HEAD_DIM] + v[:, HEAD_DIM:]


EVEN = dict(dil=1, nsub=2, ppk=4, q_blk=0, k_blk=4, v_blk=5, n_heads=A_Q_HEADS, group=A_GROUP, max_dist=A_WINDOW - 1)
STICK = dict(q_blk=6, k_blk=10, v_blk=14, n_pairs=4)


def _odd_cfg(dil):
    return dict(dil=dil, nsub=4 if dil == 1 else 1, ppk=1, q_blk=0, k_blk=8, v_blk=16, n_heads=C_HEADS, group=1,
                max_dist=BLK)


def even_fwd(x, g, win, qg, kg, sinks, wout, tag, riders=None, rider_args=()):
    qkv, h = norm_matmul(x, g, win, tm=_tk(x), tn=1152, split=False, name=f"{tag}_in")
    qg2, kg2 = _tile2(qg), _tile2(kg)
    slopes = jnp.asarray(_alibi(A_Q_HEADS), F32)
    qkn = qk_norm(qkv, qg2, kg2, width=768, steps=1, n_q=4, tm=TM, name=f"{tag}_qkn")
    oa, _, *lse = banded_fwd(qkn, qkv, slopes, sinks, name=f"{tag}_swa", **EVEN)
    ob, rode = stick_fwd(qkv, name=f"{tag}_stick", riders=riders, rider_args=rider_args, **STICK)
    o = jnp.concatenate([oa, ob], axis=1)
    xo = mm_nn(o, wout, res=x, tm=TM, tn=D_MODEL, tk=D_MODEL, name=f"{tag}_out")
    return xo, (x, qkv, qkn, h, oa, lse, o), rode


def even_bwd(d, saved, g, win, qg, kg, sinks, wout, tag, riders=None, rider_args=(), before_dx=None):
    x, qkv, qkn, h, oa, lse, o = saved
    qg2, kg2 = _tile2(qg), _tile2(kg)
    slopes = jnp.asarray(_alibi(A_Q_HEADS), F32)
    dwout = mm_tn(o, d, scale=1.0, a_split=False, b_split=False, tm=D_MODEL, tn=D_MODEL, tk=_tk(d), name=f"{tag}_dwout")
    do = mm_nt(d, wout, tm=TM, tn=D_MODEL, tk=D_MODEL, name=f"{tag}_do")
    dqa, dka4, dva4, dsk = banded_bwd(qkn, qkv, slopes, sinks, do, oa, lse, None, None,
                                      do_blk=0, name=f"{tag}_swa_b", **EVEN)
    dqb, dkb, dvb, rode = stick_bwd(qkv, do, do_blk=4, name=f"{tag}_stick_b", riders=riders, rider_args=rider_args,
                                    **STICK)
    dqkv, dqg, dkg = assemble_even(dqa, dka4, dva4, dqb, dkb, dvb, qkv, qg2, kg2, tm=TM, name=f"{tag}_asm")
    dwin = mm_tn(h, dqkv, scale=1.0, a_split=False, b_split=False, tm=D_MODEL, tn=1152, tk=_tk(d), name=f"{tag}_dwin")
    riding = before_dx(dwin, dwout) if before_dx is not None else None
    dx, dg, *rode_x = mm_nt_normbwd(dqkv, win, x, g, d, a_split=False, tm=TM, tk=1152, name=f"{tag}_dx", riding=riding)
    return dx, dg, dwin, _fold2(dqg), _fold2(dkg), dsk[:, :A_Q_HEADS], dwout, (rode, rode_x)


def odd_fwd(x, g, win, qg, kg, wout, tag):
    qkv, h = norm_matmul(x, g, win, tm=_tk(x), tn=768, split=False, name=f"{tag}_in")
    qg2, kg2 = _tile2(qg), _tile2(kg)
    qkn = qk_norm(qkv, qg2, kg2, width=D_MODEL, steps=2, n_q=8, tm=TM, name=f"{tag}_qkn")
    outs = []
    for p, (window, dil) in enumerate(C_PATTERNS):
        slopes = jnp.asarray(_alibi(C_HEADS), F32) * float(dil)
        outs.append(banded_fwd(qkn, qkv, slopes, None, name=f"{tag}_dil{p}", **_odd_cfg(dil)))
    o, w1, w2, w3 = mix_fwd(outs[0][0], outs[1][0], outs[2][0], outs[0][1], outs[1][1], outs[2][1],
                            tm=TM, name=f"{tag}_mix")
    xo = mm_nn(o, wout, res=x, tm=TM, tn=D_MODEL, tk=D_MODEL, name=f"{tag}_out")
    return xo, (x, qkv, qkn, h, outs, (w1, w2, w3), o)


def odd_bwd(d, saved, g, win, qg, kg, wout, tag):
    x, qkv, qkn, h, outs, ws, o = saved
    qg2, kg2 = _tile2(qg), _tile2(kg)
    dwout = mm_tn(o, d, scale=1.0, a_split=False, b_split=False, tm=D_MODEL, tn=D_MODEL, tk=_tk(d), name=f"{tag}_dwout")
    do = mm_nt(d, wout, tm=TM, tn=D_MODEL, tk=D_MODEL, name=f"{tag}_do")
    parts = []
    for p, (window, dil) in enumerate(C_PATTERNS):
        slopes = jnp.asarray(_alibi(C_HEADS), F32) * float(dil)
        dq, dk, dv, _ = banded_bwd(qkn, qkv, slopes, None, do, None, outs[p][2:], ws[p], o,
                                   do_blk=0, name=f"{tag}_dil{p}_b", **_odd_cfg(dil))
        parts.append((dq, dk, dv))
    dqkv, dqg, dkg = assemble_odd(parts, qkv, qg2, kg2, tm=256, name=f"{tag}_asm")
    dwin = mm_tn(h, dqkv, scale=1.0, a_split=False, b_split=False, tm=TM, tn=768, tk=_tk(d), out_blocked=True,
                 name=f"{tag}_dwin")
    dx, dg = mm_nt_normbwd(dqkv, win, x, g, d, a_split=False, tm=TM, tk=768, name=f"{tag}_dx")
    return dx, dg, dwin, _fold2(dqg), _fold2(dkg), dwout


def xa_fwd(x, mem, g, gm, wq, wkv, qg, kg, wo, tag):
    qraw, h = norm_matmul(x, g, wq, tm=TM, tn=D_MODEL, split=False, name=f"{tag}_q")
    kvraw, hm = norm_matmul(mem, gm, wkv, tm=MEM_LEN, tn=512, split=False, name=f"{tag}_kv")
    o = xattn_fwd(qraw, kvraw, qg, kg, tm=TM, name=f"{tag}_att")
    xo = mm_nn(o, wo, res=x, tm=TM, tn=D_MODEL, tk=D_MODEL, name=f"{tag}_o")
    return xo, (x, qraw, h, kvraw, hm, o)


def xa_bwd(d, saved, mem, g, gm, wq, wkv, qg, kg, wo, tag):
    x, qraw, h, kvraw, hm, o = saved
    dwo = mm_tn(o, d, scale=1.0, a_split=False, b_split=False, tm=D_MODEL, tn=D_MODEL, tk=_tk(d), name=f"{tag}_dwo")
    do = mm_nt(d, wo, tm=TM, tn=D_MODEL, tk=D_MODEL, name=f"{tag}_do")
    dq, dkv, dqg, dkg = xattn_bwd(qraw, kvraw, qg, kg, do, o, tm=TM, name=f"{tag}_att_b")
    dwq = mm_tn(h, dq, scale=1.0, a_split=False, b_split=False, tm=D_MODEL, tn=D_MODEL, tk=_tk(d), name=f"{tag}_dwq")
    dx, dg = mm_nt_normbwd(dq, wq, x, g, d, a_split=False, tm=TM, tk=D_MODEL, name=f"{tag}_dx")
    dwkv = mm_tn(hm, dkv, scale=1.0, a_split=False, b_split=False, tm=TM, tn=512, tk=MEM_LEN, out_blocked=True,
                 name=f"{tag}_dwkv")
    _, dgm = mm_nt_normbwd(dkv, wkv, mem, gm, None, a_split=False, tm=MEM_LEN, tk=512, name=f"{tag}_dmem")
    return dx, dg, dgm, dwq, dwkv, dqg, dkg, dwo


MATS = (("ffn1_w_gu", 1), ("ffn1_w_down", 0), ("ev_w_in", 1), ("ev_w_out", 0), ("od_w_in", 1), ("od_w_out", 0),
        ("xa_w_q", 0), ("xa_w_kv", 1), ("xa_w_o", 0), ("ffn2_w_gu", 1), ("ffn2_w_down", 0))
SMALLS = ("ffn1_norm", "mix_norm", "ev_q_gain", "ev_k_gain", "ev_sinks", "od_q_gain", "od_k_gain", "xa_norm",
          "xa_mem_norm", "xa_q_gain", "xa_k_gain", "ffn2_norm")
WEIGHTS = ("ffn1_norm", "ffn1_w_gu", "ffn1_w_down", "mix_norm", "ev_w_in", "ev_q_gain", "ev_k_gain", "ev_sinks",
           "ev_w_out", "od_w_in", "od_q_gain", "od_k_gain", "od_w_out", "xa_norm", "xa_mem_norm", "xa_w_q",
           "xa_w_kv", "xa_q_gain", "xa_k_gain", "xa_w_o", "ffn2_norm", "ffn2_w_gu", "ffn2_w_down")
SMALL_ROWS = 16
LAYER_GROUPS = (
    (((("ffn1_w_gu", 0), ("ffn2_w_gu", 0)), 4, 512),
     ((("ffn1_w_down", 0), ("ffn2_w_down", 0)), 2, 352),
     ((("ev_w_out", 0), ("xa_w_q", 0), ("xa_w_o", 0)), 1, 384),
     ((("xa_w_kv", 0),), 1, 512),
     ((("ev_w_in", 0),), 1, 512)),
    (((("ffn1_w_gu", 1), ("ffn2_w_gu", 1)), 4, 512),
     ((("ffn1_w_down", 1), ("ffn2_w_down", 1)), 2, 352),
     ((("od_w_out", 0), ("xa_w_q", 1), ("xa_w_o", 1)), 1, 384),
     ((("xa_w_kv", 1),), 1, 512),
     ((("od_w_in", 0),), 1, 512)),
)
GATHER_FIRST = (((("ffn1_w_gu", 0),), 2, 512), ((("ffn1_w_down", 0),), 1, 352), ((("ev_w_out", 0),), 1, 128),
                ((("ev_w_in", 0),), 1, 512))
GATHER_IN_FFN1 = (((("ffn2_w_gu", 0),), 2, 512), ((("ffn2_w_down", 0),), 1, 352))
GATHER_IN_STICK = (((("ffn1_w_gu", 1),), 2, 512), ((("ffn1_w_down", 1),), 1, 352), ((("od_w_out", 0),), 1, 128),
                   ((("od_w_in", 0),), 1, 512),
                   ((("xa_w_q", 0), ("xa_w_o", 0), ("xa_w_q", 1), ("xa_w_o", 1)), 1, 512),
                   ((("xa_w_kv", 0), ("xa_w_kv", 1)), 1, 512))
GATHER_IN_FFN2 = (((("ffn2_w_gu", 1),), 2, 512), ((("ffn2_w_down", 1),), 1, 352))
ROUNDS = {
    "1": LAYER_GROUPS[1],
    "0a": (((("ffn2_w_gu", 0),), 2, 512), ((("ffn2_w_down", 0),), 1, 352)) + LAYER_GROUPS[0][2:],
    "0b": (((("ffn1_w_gu", 0),), 2, 512), ((("ffn1_w_down", 0),), 1, 352)),
}


def _chunks_of(groups):
    return tuple(g[1] for g in groups)
COL_SHARDED = {name for name, axis in MATS if axis == 1}
BLOCKED = {"ffn1_w_gu", "ffn2_w_gu", "xa_w_kv", "od_w_in"}


def group_halves(shards, c, groups):
    out = []
    for members, _, _ in groups:
        halves = []
        for name, layer in members:
            _, r, cc = shards[name].shape
            half = lax.dynamic_index_in_dim(shards[name][layer].reshape(2, r // 2, cc), c, 0, keepdims=False)
            halves.append(half.astype(BF16))
        out.append(jnp.concatenate(halves, axis=0))
    return out


def full_weights(gathered, shards, groups):
    full = {}
    for (members, _, _), arr in zip(groups, gathered):
        for w, (name, layer) in enumerate(members):
            _, r, cc = shards[name].shape
            piece = arr[:, w * (r // 2):(w + 1) * (r // 2)].reshape(4, r, cc)
            if name not in COL_SHARDED:
                piece = piece.reshape(4 * r, cc)
            elif name not in BLOCKED:
                piece = piece.transpose(1, 0, 2).reshape(r, 4 * cc)
            full[(name, layer)] = piece
    return full


def group_grads(grads, shards, groups):
    out = []
    for members, _, _ in groups:
        parts = []
        for name, layer in members:
            _, r, cc = shards[name].shape
            gfull = grads[(name, layer)]
            if name in COL_SHARDED and name not in BLOCKED:
                gfull = gfull.reshape(2, r // 2, 4, cc).transpose(2, 0, 1, 3)
            parts.append(gfull.reshape(N_DEV, r // 2, cc))
        out.append(jnp.concatenate(parts, axis=1))
    return out


def shard_grads(mine, theirs, c, shards, groups):
    per = {}
    for (members, _, _), a, b in zip(groups, mine, theirs):
        for w, (name, layer) in enumerate(members):
            _, r, cc = shards[name].shape
            rows = slice(w * (r // 2), (w + 1) * (r // 2))
            lo = jnp.where(c == 0, a[rows], b[rows])
            hi = jnp.where(c == 0, b[rows], a[rows])
            per[(name, layer)] = jnp.concatenate([lo, hi], axis=0)
    return per


def pack_small(vals):
    row10 = jnp.concatenate([vals["xa_q_gain"].reshape(1, 512), vals["xa_k_gain"].reshape(1, 512)], axis=1)
    row11 = jnp.concatenate([vals["ev_q_gain"], vals["ev_k_gain"], vals["od_q_gain"], vals["od_k_gain"],
                             vals["ev_sinks"], jnp.zeros((1, 1024 - 4 * 64 - 8), F32)], axis=1)
    return jnp.concatenate([vals["ffn1_norm"], vals["mix_norm"], vals["xa_norm"], vals["xa_mem_norm"],
                            vals["ffn2_norm"], row10, row11, jnp.zeros((SMALL_ROWS - 12, 1024), F32)], axis=0)


def unpack_small(arr):
    return {"ffn1_norm": arr[0:2], "mix_norm": arr[2:4], "xa_norm": arr[4:6], "xa_mem_norm": arr[6:8],
            "ffn2_norm": arr[8:10],
            "xa_q_gain": arr[10:11, 0:512].reshape(2, 256), "xa_k_gain": arr[10:11, 512:1024].reshape(2, 256),
            "ev_q_gain": arr[11:12, 0:64], "ev_k_gain": arr[11:12, 64:128], "od_q_gain": arr[11:12, 128:192],
            "od_k_gain": arr[11:12, 192:256], "ev_sinks": arr[11:12, 256:264]}


def local_step(x, mem, target, W, small, prereduce, later):
    depth = small["ffn1_norm"].shape[0]

    def row(name, l):
        return small[name][l:l + 1]

    saved = []
    for l in range(depth):
        j = l // 2
        def riding_gather(host):
            if l == 0 and host in later:
                return GatherBlocks(later[host][0], later[host][1]), later[host][0]
            return None, ()

        riders, rider_args = riding_gather("ffn1")
        x, s1, rode = ffn_fwd(x, row("ffn1_norm", l), W[("ffn1_w_gu", l)], W[("ffn1_w_down", l)], f"l{l}_f1",
                              riders=riders, rider_args=rider_args)
        if riders is not None:
            W = {**W, **later["ffn1"][2](rode)}
        if l % 2 == 0:
            riders, rider_args = riding_gather("stick")
            x, s2, rode = even_fwd(x, row("mix_norm", l), W[("ev_w_in", j)], row("ev_q_gain", j),
                                   row("ev_k_gain", j), small["ev_sinks"][j], W[("ev_w_out", j)], f"l{l}_ev",
                                   riders=riders, rider_args=rider_args)
            if riders is not None:
                W = {**W, **later["stick"][2](rode)}
        else:
            x, s2 = odd_fwd(x, row("mix_norm", l), W[("od_w_in", j)], row("od_q_gain", j), row("od_k_gain", j),
                            W[("od_w_out", j)], f"l{l}_od")
        x, s3 = xa_fwd(x, mem, row("xa_norm", l), row("xa_mem_norm", l), W[("xa_w_q", l)], W[("xa_w_kv", l)],
                       row("xa_q_gain", l), row("xa_k_gain", l), W[("xa_w_o", l)], f"l{l}_xa")
        riders, rider_args = riding_gather("ffn2")
        x, s4, rode = ffn_fwd(x, row("ffn2_norm", l), W[("ffn2_w_gu", l)], W[("ffn2_w_down", l)], f"l{l}_f2",
                              riders=riders, rider_args=rider_args)
        if riders is not None:
            W = {**W, **later["ffn2"][2](rode)}
        saved.append((s1, s2, s3, s4))
    loss, d = loss_kernel(x, target, tm=TM, name="loss")

    gw = {}
    gs = {name: [None] * small[name].shape[0] for name in SMALLS}
    pending, landed = None, {}
    for l in reversed(range(depth)):
        j = l // 2
        s1, s2, s3, s4 = saved[l]
        d, dg, dwgu, dwd, _ = ffn_bwd(d, s4, row("ffn2_norm", l), W[("ffn2_w_gu", l)], W[("ffn2_w_down", l)],
                                      f"l{l}_f2")
        gs["ffn2_norm"][l] = dg
        gw[("ffn2_w_gu", l)], gw[("ffn2_w_down", l)] = dwgu, dwd
        d, dg, dgm, dwq, dwkv, dqg, dkg, dwo = xa_bwd(
            d, s3, mem, row("xa_norm", l), row("xa_mem_norm", l), W[("xa_w_q", l)], W[("xa_w_kv", l)],
            row("xa_q_gain", l), row("xa_k_gain", l), W[("xa_w_o", l)], f"l{l}_xa")
        gs["xa_norm"][l], gs["xa_mem_norm"][l], gs["xa_q_gain"][l], gs["xa_k_gain"][l] = dg, dgm, dqg, dkg
        gw[("xa_w_q", l)], gw[("xa_w_kv", l)], gw[("xa_w_o", l)] = dwq, dwkv, dwo
        split = l == 0 and l % 2 == 0 and ("0a" in ROUNDS)
        early = []
        pre_early = None
        if l % 2 == 0:
            riders, rider_args = None, ()
            if pending is not None:
                riders, rider_args = ChipScatter(pending[1], _chunks_of(ROUNDS[pending[0]])), pending[1]

            def before_mixer_dx(dwin, dwout, j=j, early=early):
                gw[("ev_w_in", j)], gw[("ev_w_out", j)] = dwin, dwout
                early += prereduce.pack(gw, "0a")
                return Riding([(PairExchange(early, _chunks_of(ROUNDS["0a"])), early)])

            d, dg, dwin, dqg, dkg, dsk, dwout, (rode, rode_x) = even_bwd(
                d, s2, row("mix_norm", l), W[("ev_w_in", j)], row("ev_q_gain", j), row("ev_k_gain", j),
                small["ev_sinks"][j], W[("ev_w_out", j)], f"l{l}_ev", riders=riders, rider_args=rider_args,
                before_dx=before_mixer_dx if split else None)
            if pending is not None:
                landed[pending[0]], pending = rode, None
            gs["ev_q_gain"][j], gs["ev_k_gain"][j], gs["ev_sinks"][j] = dqg, dkg, dsk
            gw[("ev_w_in", j)], gw[("ev_w_out", j)] = dwin, dwout
            if split:
                pre_early = prereduce.sums(early, rode_x[0], "0a")
        else:
            d, dg, dwin, dqg, dkg, dwout = odd_bwd(
                d, s2, row("mix_norm", l), W[("od_w_in", j)], row("od_q_gain", j), row("od_k_gain", j),
                W[("od_w_out", j)], f"l{l}_od")
            gs["od_q_gain"][j], gs["od_k_gain"][j] = dqg, dkg
            gw[("od_w_in", j)], gw[("od_w_out", j)] = dwin, dwout
        gs["mix_norm"][l] = dg
        packed = []

        rnd = "0b" if pre_early is not None else str(l)
        final = l == 0

        def before_dx(dwgu, dwd, l=l, packed=packed, rnd=rnd, final=final):
            gw[("ffn1_w_gu", l)], gw[("ffn1_w_down", l)] = dwgu, dwd
            packed += prereduce.pack(gw, rnd)
            chunks = _chunks_of(ROUNDS[rnd])
            if not final:
                return Riding([(PairExchange(packed, chunks), packed)])
            sib = _standalone(PairExchange(packed, chunks), packed, f"pair_grads{rnd}")
            pre = prereduce.sums(packed, sib, rnd)
            return Riding([(ChipScatter(pre, chunks), pre)])

        ride_bact = ride_dwgu = None
        if pre_early is not None:
            chunks = _chunks_of(ROUNDS["0a"])
            ride_bact = Riding([(ChipScatter(pre_early[:2], chunks[:2]), pre_early[:2])])
            ride_dwgu = Riding([(ChipScatter(pre_early[2:], chunks[2:]), pre_early[2:])])
        d, dg, dwgu, dwd, (rode_a, rode_g, rode_x) = ffn_bwd(
            d, s1, row("ffn1_norm", l), W[("ffn1_w_gu", l)], W[("ffn1_w_down", l)], f"l{l}_f1",
            ride_bact=ride_bact, ride_dwgu=ride_dwgu, before_dx=before_dx)
        gs["ffn1_norm"][l] = dg
        if pre_early is not None:
            landed["0a"] = list(rode_a[0]) + list(rode_g[0])
        if pending is not None:
            landed[pending[0]] = chip_scatter(pending[1], _chunks_of(ROUNDS[pending[0]]),
                                              name=f"scatter_grads{pending[0]}")
        if final:
            landed[rnd], pending = rode_x[0], None
        else:
            pending = (rnd, prereduce.sums(packed, rode_x[0], rnd))
    if pending is not None:
        landed[pending[0]] = chip_scatter(pending[1], _chunks_of(ROUNDS[pending[0]]),
                                          name=f"scatter_grads{pending[0]}")
    gsmall = {name: jnp.concatenate(v, axis=0) for name, v in gs.items()}
    return loss, d, landed, gsmall


def kernel(x, mem, ffn1_norm, ffn1_w_gu, ffn1_w_down, mix_norm, ev_w_in, ev_q_gain, ev_k_gain, ev_sinks, ev_w_out, od_w_in, od_q_gain, od_k_gain, od_w_out, xa_norm, xa_mem_norm, xa_w_q, xa_w_kv, xa_q_gain, xa_k_gain, xa_w_o, ffn2_norm, ffn2_w_gu, ffn2_w_down, loss_target, m_ffn1_norm, m_ffn1_w_gu, m_ffn1_w_down, m_mix_norm, m_ev_w_in, m_ev_q_gain, m_ev_k_gain, m_ev_sinks, m_ev_w_out, m_od_w_in, m_od_q_gain, m_od_k_gain, m_od_w_out, m_xa_norm, m_xa_mem_norm, m_xa_w_q, m_xa_w_kv, m_xa_q_gain, m_xa_k_gain, m_xa_w_o, m_ffn2_norm, m_ffn2_w_gu, m_ffn2_w_down, v_ffn1_norm, v_ffn1_w_gu, v_ffn1_w_down, v_mix_norm, v_ev_w_in, v_ev_q_gain, v_ev_k_gain, v_ev_sinks, v_ev_w_out, v_od_w_in, v_od_q_gain, v_od_k_gain, v_od_w_out, v_xa_norm, v_xa_mem_norm, v_xa_w_q, v_xa_w_kv, v_xa_q_gain, v_xa_k_gain, v_xa_w_o, v_ffn2_norm, v_ffn2_w_gu, v_ffn2_w_down):
    given = dict(locals())
    w = {n: given[n] for n in WEIGHTS}
    m = {n: given["m_" + n] for n in WEIGHTS}
    v = {n: given["v_" + n] for n in WEIGHTS}
    c = lax.axis_index("c")
    shards = {name: w[name] for name, _ in MATS}
    small = {n: w[n] for n in SMALLS}

    def gathering(groups):
        return group_halves(shards, c, groups), _chunks_of(groups), lambda got: full_weights(got, shards, groups)

    halves, chunks, unpack = gathering(GATHER_FIRST)
    full = unpack(gather_blocks(halves, chunks, name="gather_weights0"))
    later = {"ffn1": gathering(GATHER_IN_FFN1), "stick": gathering(GATHER_IN_STICK), "ffn2": gathering(GATHER_IN_FFN2)}

    class prereduce:
        @staticmethod
        def pack(gw, rnd):
            return group_grads(gw, shards, ROUNDS[rnd])

        @staticmethod
        def sums(packed, sib, rnd):
            return [pair_sum(p, s, c, tr=g[2], name=f"pair_sum{rnd}_{i}")
                    for i, (g, p, s) in enumerate(zip(ROUNDS[rnd], packed, sib))]

    loss_b, grad_x, landed, gsmall = local_step(x[0], mem[0], loss_target[0], full, small, prereduce, later)

    per = {}
    for rnd, land in sorted(landed.items()):
        groups = ROUNDS[rnd]
        mine = [reduce_slots(a, tr=g[2], name=f"sum_grads{rnd}_{i}") for i, (g, a) in enumerate(zip(groups, land))]
        theirs = sibling_send(mine, _chunks_of(groups), name=f"swap_grads{rnd}")
        per.update(shard_grads(mine, theirs, c, shards, groups))
    g = {name: jnp.stack([per[(name, layer)] for layer in range(w[name].shape[0])], axis=0) for name, _ in MATS}
    land_small = gather_small(pack_small(gsmall), name="gather_small")
    g_small = unpack_small(reduce_slots(land_small, tr=SMALL_ROWS, name="sum_small"))
    g.update(g_small)

    delta, new_m, new_v = {}, {}, {}
    for name, _ in MATS:
        shp = w[name].shape
        flat = [a.reshape(-1, shp[-1]) for a in (w[name], g[name], m[name], v[name])]
        dl, nm, nv = adamw(*flat, br=BLK, name=f"adamw_{name}")
        delta[name], new_m[name], new_v[name] = dl.reshape(shp), nm.reshape(shp), nv.reshape(shp)
    dl, nm, nv = adamw(pack_small(small), pack_small(g_small), pack_small({n: m[n] for n in SMALLS}),
                       pack_small({n: v[n] for n in SMALLS}), br=SMALL_ROWS, name="adamw_small")
    for dst, arr in ((delta, dl), (new_m, nm), (new_v, nv)):
        dst.update(unpack_small(arr))

    loss = lax.psum(loss_b[0, 0], ("x", "y", "c"))
    return (loss, grad_x[None], *[g[n] for n in WEIGHTS], *[delta[n] for n in WEIGHTS],
            *[new_m[n] for n in WEIGHTS], *[new_v[n] for n in WEIGHTS])
```

```python
import jax
import jax.numpy as jnp
from jax import lax
from jax.experimental import pallas as pl
from jax.experimental.pallas import tpu as pltpu

F32 = jnp.float32
BF16 = jnp.bfloat16

D_MODEL = 1024
HEAD_DIM = 64
LANES = 128
BLK = 128
RMS_EPS = 1e-6
MEM_LEN = 256
X_HEADS = 4
X_HEAD_DIM = 256
A_Q_HEADS = 8
A_GROUP = 4
A_WINDOW = 128
C_HEADS = 16
C_PATTERNS = ((128, 1), (512, 4), (2048, 16))
NEG = -1e30
VMEM_LIMIT = 56 * 2 ** 20

ADAM_LR = 0.001
ADAM_B1 = 0.9
ADAM_B2 = 0.999
ADAM_EPS = 1e-08
ADAM_WD = 0.01
ADAM_STEP = 10

N_DEV = 8
MESH = pl.DeviceIdType.MESH


def _cparams(n):
    return pltpu.CompilerParams(dimension_semantics=("arbitrary",) * n, vmem_limit_bytes=VMEM_LIMIT)


def _dot(a, b):
    return jnp.dot(a, b, preferred_element_type=F32)


def _dot_nt(a, b):
    return lax.dot_general(a, b, (((1,), (1,)), ((), ())), preferred_element_type=F32)


def _dot_tn(a, b):
    return lax.dot_general(a, b, (((0,), (0,)), ((), ())), preferred_element_type=F32)


def _sigmoid(z):
    return 1.0 / (1.0 + jnp.exp(-z))


def norm_matmul(x, g, w, *, tm, tn, split, name):
    T, K = x.shape
    blocked = w.ndim == 3
    assert not blocked or w.shape[2] == tn
    N = w.shape[0] * w.shape[2] if blocked else w.shape[1]
    nj = N // tn

    def body(x_ref, g_ref, w_ref, o_ref, h_ref):
        @pl.when(pl.program_id(1) == 0)
        def _():
            xv = x_ref[...]
            r = lax.rsqrt(jnp.mean(xv * xv, axis=-1, keepdims=True) + RMS_EPS)
            h_ref[...] = (xv * r * g_ref[...]).astype(BF16)

        o_ref[...] = _dot(h_ref[...], w_ref[...]).astype(o_ref.dtype)

    if split:
        njh = nj // 2
        o_shape = jax.ShapeDtypeStruct((2, T, N // 2), BF16)
        o_spec = pl.BlockSpec((None, tm, tn), lambda i, j: (j // njh, i, j % njh))
    else:
        o_shape = jax.ShapeDtypeStruct((T, N), F32)
        o_spec = pl.BlockSpec((tm, tn), lambda i, j: (i, j))
    return pl.pallas_call(
        body, grid=(T // tm, nj),
        in_specs=[pl.BlockSpec((tm, K), lambda i, j: (i, 0)),
                  pl.BlockSpec((1, K), lambda i, j: (0, 0)),
                  (pl.BlockSpec((None, K, tn), lambda i, j: (j, 0, 0)) if blocked
                   else pl.BlockSpec((K, tn), lambda i, j: (0, j)))],
        out_specs=[o_spec, pl.BlockSpec((tm, K), lambda i, j: (i, 0))],
        out_shape=[o_shape, jax.ShapeDtypeStruct((T, K), BF16)],
        compiler_params=_cparams(2), name=name)(x, g, w)


def mm_nn(a, b, *, res, tm, tn, tk, name):
    T = a.shape[0]
    K, N = b.shape
    nk = K // tk

    def body(a_ref, b_ref, r_ref, o_ref, acc):
        k = pl.program_id(2)

        @pl.when(k == 0)
        def _():
            acc[...] = jnp.zeros_like(acc)

        acc[...] += _dot(a_ref[...].astype(BF16), b_ref[...])

        @pl.when(k == nk - 1)
        def _():
            o_ref[...] = r_ref[...] + acc[...]

    return pl.pallas_call(
        body, grid=(T // tm, N // tn, nk),
        in_specs=[pl.BlockSpec((tm, tk), lambda i, j, k: (i, k)), pl.BlockSpec((tk, tn), lambda i, j, k: (k, j)),
                  pl.BlockSpec((tm, tn), lambda i, j, k: (i, j))],
        out_specs=pl.BlockSpec((tm, tn), lambda i, j, k: (i, j)),
        out_shape=jax.ShapeDtypeStruct((T, N), F32),
        scratch_shapes=[pltpu.VMEM((tm, tn), F32)],
        compiler_params=_cparams(3), name=name)(a, b, res)


def mm_nt(a, b, *, tm, tn, tk, name):
    T, K = a.shape
    N = b.shape[0]
    nk = K // tk

    def body(a_ref, b_ref, o_ref, acc):
        k = pl.program_id(2)

        @pl.when(k == 0)
        def _():
            acc[...] = jnp.zeros_like(acc)

        acc[...] += _dot_nt(a_ref[...].astype(BF16), b_ref[...])

        @pl.when(k == nk - 1)
        def _():
            o_ref[...] = acc[...]

    return pl.pallas_call(
        body, grid=(T // tm, N // tn, nk),
        in_specs=[pl.BlockSpec((tm, tk), lambda i, j, k: (i, k)),
                  pl.BlockSpec((tn, tk), lambda i, j, k: (j, k))],
        out_specs=pl.BlockSpec((tm, tn), lambda i, j, k: (i, j)),
        out_shape=jax.ShapeDtypeStruct((T, N), F32),
        scratch_shapes=[pltpu.VMEM((tm, tn), F32)],
        compiler_params=_cparams(3), name=name)(a, b)


def ffn_bwd_act(d, wd, gu, *, tm, tn, name, riding=None):
    T, K = d.shape
    Fd = wd.shape[0]
    ni = T // tm

    def body(d_ref, w_ref, g_ref, u_ref, dgu_ref, dwd_ref, acc):
        i = pl.program_id(1)

        @pl.when(i == 0)
        def _():
            acc[...] = jnp.zeros_like(acc)

        db = d_ref[...].astype(BF16)
        da = 0.5 * _dot_nt(db, w_ref[...])
        gv = g_ref[...].astype(F32)
        uv = u_ref[...].astype(F32)
        s = _sigmoid(gv)
        silu = gv * s
        acc[...] += _dot_tn((silu * uv).astype(BF16), db)
        dgu_ref[0] = (da * uv * (s * (1.0 + gv * (1.0 - s)))).astype(BF16)
        dgu_ref[1] = (da * silu).astype(BF16)

        @pl.when(i == ni - 1)
        def _():
            dwd_ref[...] = (0.5 * acc[...]).astype(BF16)

    in_specs = [pl.BlockSpec((tm, K), lambda j, i: (i, 0)),
                pl.BlockSpec((tn, K), lambda j, i: (j, 0)),
                pl.BlockSpec((None, tm, tn), lambda j, i: (0, i, j)),
                pl.BlockSpec((None, tm, tn), lambda j, i: (1, i, j))]
    out_specs = [pl.BlockSpec((2, tm, tn), lambda j, i: (0, i, j)), pl.BlockSpec((tn, K), lambda j, i: (j, 0))]
    out_shape = [jax.ShapeDtypeStruct((2, T, Fd), BF16), jax.ShapeDtypeStruct((Fd, K), BF16)]
    return _call_with_riders(body, riding, (Fd // tn, ni), in_specs, out_specs, out_shape,
                             [pltpu.VMEM((tn, K), F32)], [d, wd, gu, gu], name)


def _call_with_riders(body, riding, grid, in_specs, out_specs, out_shape, scratch, args, name):
    n_out = len(out_shape)
    if riding is None:
        return pl.pallas_call(body, grid=grid, in_specs=in_specs, out_specs=out_specs, out_shape=out_shape,
                              scratch_shapes=scratch, compiler_params=_cparams(len(grid)), name=name)(*args)

    def is_first():
        ok = pl.program_id(0) == 0
        for ax in range(1, len(grid)):
            ok = ok & (pl.program_id(ax) == 0)
        return ok

    def is_last():
        ok = pl.program_id(0) == grid[0] - 1
        for ax in range(1, len(grid)):
            ok = ok & (pl.program_id(ax) == grid[ax] - 1)
        return ok

    outs = pl.pallas_call(
        riding.wrap(body, len(in_specs), n_out, len(scratch), is_first, is_last), grid=grid,
        in_specs=list(in_specs) + riding.in_specs, out_specs=list(out_specs) + riding.out_specs,
        out_shape=list(out_shape) + riding.out_shapes, scratch_shapes=list(scratch) + riding.scratch,
        compiler_params=_cparams(len(grid)), name=name)(*args, *riding.args)
    core, per = riding.split(outs, n_out)
    return (*core, *per)


def mm_nt_normbwd(a, b, x, g, res, *, a_split, tm, tk, name, riding=None):
    T, Dm = x.shape
    blocked = b.ndim == 3
    assert not blocked or b.shape[2] == tk
    K = b.shape[0] * b.shape[2] if blocked else b.shape[1]
    nk = K // tk
    nkh = nk // 2
    has_res = res is not None

    def body(*refs):
        if has_res:
            a_ref, b_ref, x_ref, g_ref, r_ref, dx_ref, dg_ref, acc = refs
        else:
            a_ref, b_ref, x_ref, g_ref, dx_ref, dg_ref, acc = refs
        i = pl.program_id(0)
        k = pl.program_id(1)

        @pl.when(k == 0)
        def _():
            acc[...] = jnp.zeros_like(acc)

        acc[...] += _dot_nt(a_ref[...].astype(BF16), b_ref[...])

        @pl.when(k == nk - 1)
        def _():
            xv = x_ref[...]
            r = lax.rsqrt(jnp.mean(xv * xv, axis=-1, keepdims=True) + RMS_EPS)
            xh = xv * r
            dh = acc[...]
            dxh = dh * g_ref[...]
            dx = r * (dxh - xh * jnp.mean(dxh * xh, axis=-1, keepdims=True))
            if has_res:
                dx = dx + r_ref[...]
            dx_ref[...] = dx
            part = jnp.sum(dh * xh, axis=0, keepdims=True)

            @pl.when(i == 0)
            def _():
                dg_ref[...] = part

            @pl.when(i > 0)
            def _():
                dg_ref[...] += part

    if a_split:
        a_spec = pl.BlockSpec((None, tm, tk), lambda i, k: (k // nkh, i, k % nkh))
    else:
        a_spec = pl.BlockSpec((tm, tk), lambda i, k: (i, k))
    in_specs = [a_spec,
                (pl.BlockSpec((None, Dm, tk), lambda i, k: (k, 0, 0)) if blocked
                 else pl.BlockSpec((Dm, tk), lambda i, k: (0, k))),
                pl.BlockSpec((tm, Dm), lambda i, k: (i, 0)),
                pl.BlockSpec((1, Dm), lambda i, k: (0, 0))]
    args = [a, b, x, g]
    if has_res:
        in_specs.append(pl.BlockSpec((tm, Dm), lambda i, k: (i, 0)))
        args.append(res)
    out_specs = [pl.BlockSpec((tm, Dm), lambda i, k: (i, 0)), pl.BlockSpec((1, Dm), lambda i, k: (0, 0))]
    out_shape = [jax.ShapeDtypeStruct((T, Dm), F32), jax.ShapeDtypeStruct((1, Dm), F32)]
    scratch = [pltpu.VMEM((tm, Dm), F32)]
    return _call_with_riders(body, riding, (T // tm, nk), in_specs, out_specs, out_shape, scratch, args, name)


def mm_tn(a, b, *, scale, a_split, b_split, tm, tn, tk, name, out_blocked=False, riding=None):
    T = a.shape[-2]
    M = a.shape[-1] * (2 if a_split else 1)
    N = b.shape[-1] * (2 if b_split else 1)
    ni, nj, nk = M // tm, N // tn, T // tk
    nih, njh = ni // 2, nj // 2

    def body(a_ref, b_ref, o_ref, acc):
        k = pl.program_id(2)

        @pl.when(k == 0)
        def _():
            acc[...] = jnp.zeros_like(acc)

        acc[...] += _dot_tn(a_ref[...].astype(BF16), b_ref[...].astype(BF16))

        @pl.when(k == nk - 1)
        def _():
            o_ref[...] = (acc[...] * scale).astype(o_ref.dtype)

    if a_split:
        a_spec = pl.BlockSpec((None, tk, tm), lambda i, j, k: (i // nih, k, i % nih))
    else:
        a_spec = pl.BlockSpec((tk, tm), lambda i, j, k: (k, i))
    if b_split:
        b_spec = pl.BlockSpec((None, tk, tn), lambda i, j, k: (j // njh, k, j % njh))
    else:
        b_spec = pl.BlockSpec((tk, tn), lambda i, j, k: (k, j))
    if out_blocked:
        o_spec = pl.BlockSpec((None, None, tm, tn), lambda i, j, k: (j, i, 0, 0))
        o_shape = jax.ShapeDtypeStruct((nj, ni, tm, tn), BF16)
    else:
        o_spec = pl.BlockSpec((tm, tn), lambda i, j, k: (i, j))
        o_shape = jax.ShapeDtypeStruct((M, N), BF16)
    outs = _call_with_riders(body, riding, (ni, nj, nk), [a_spec, b_spec], [o_spec], [o_shape],
                             [pltpu.VMEM((tm, tn), F32)], [a, b], name)
    return outs[0] if riding is None else tuple(outs)


def loss_kernel(y, target, *, tm, name):
    T, Dm = y.shape

    def body(y_ref, t_ref, l_ref, dy_ref):
        e = y_ref[...] - t_ref[...]
        dy_ref[...] = e * (1.0 / Dm)
        part = (0.5 / Dm) * jnp.sum(jnp.sum(e * e, axis=-1, keepdims=True), axis=0, keepdims=True)
        part = jnp.broadcast_to(part, (8, LANES))

        @pl.when(pl.program_id(0) == 0)
        def _():
            l_ref[...] = part

        @pl.when(pl.program_id(0) > 0)
        def _():
            l_ref[...] += part

    return pl.pallas_call(
        body, grid=(T // tm,),
        in_specs=[pl.BlockSpec((tm, Dm), lambda i: (i, 0)), pl.BlockSpec((tm, Dm), lambda i: (i, 0))],
        out_specs=[pl.BlockSpec((8, LANES), lambda i: (0, 0)), pl.BlockSpec((tm, Dm), lambda i: (i, 0))],
        out_shape=[jax.ShapeDtypeStruct((8, LANES), F32), jax.ShapeDtypeStruct((T, Dm), F32)],
        compiler_params=_cparams(1), name=name)(y, target)


def adamw(w, g, m, v, *, br, name):
    R, C = w.shape

    def body(w_ref, g_ref, m_ref, v_ref, d_ref, nm_ref, nv_ref):
        gv = g_ref[...]
        nm = ADAM_B1 * m_ref[...] + (1.0 - ADAM_B1) * gv
        nv = ADAM_B2 * v_ref[...] + (1.0 - ADAM_B2) * (gv * gv)
        m_hat = nm / (1.0 - ADAM_B1 ** ADAM_STEP)
        v_hat = nv / (1.0 - ADAM_B2 ** ADAM_STEP)
        d_ref[...] = -ADAM_LR * (m_hat / (jnp.sqrt(v_hat) + ADAM_EPS) + ADAM_WD * w_ref[...])
        nm_ref[...] = nm
        nv_ref[...] = nv

    spec = pl.BlockSpec((br, C), lambda i: (i, 0))
    shp = jax.ShapeDtypeStruct((R, C), F32)
    return pl.pallas_call(
        body, grid=(R // br,), in_specs=[spec] * 4, out_specs=[spec] * 3, out_shape=[shp] * 3,
        compiler_params=_cparams(1), name=name)(w, g, m, v)


def _lane0():
    return lax.broadcasted_iota(jnp.int32, (1, LANES), 1) < HEAD_DIM


def _half_sum(x, m0):
    s0 = jnp.sum(jnp.where(m0, x, 0.0), axis=-1, keepdims=True)
    s1 = jnp.sum(jnp.where(m0, 0.0, x), axis=-1, keepdims=True)
    return jnp.where(m0, s0, s1)


def _head_rms(x, m0):
    return lax.rsqrt(_half_sum(x * x, m0) * (1.0 / HEAD_DIM) + RMS_EPS)


def _alibi(n):
    return [float(2.0 ** (-8.0 * (h + 1) / n)) for h in range(n)]


def _mask_half(x, m0, e):
    return jnp.where(m0, x, 0.0) if e == 0 else jnp.where(m0, 0.0, x)


def _band_masks2(max_dist, has_prev, live):
    row = lax.broadcasted_iota(jnp.int32, (2 * BLK, 2 * BLK), 0)
    col = lax.broadcasted_iota(jnp.int32, (2 * BLK, 2 * BLK), 1)
    dist = (row & (BLK - 1)) - col + BLK
    lim = jnp.where(live, max_dist, -1)
    first = jnp.where(has_prev, 0, BLK)
    valid = (dist >= 0) & (dist <= lim) & (col >= first)
    top = lax.broadcasted_iota(jnp.int32, (2 * BLK, 1), 0) < BLK
    return dist.astype(F32), valid, top


def _stack_heads(x, m0, kes):
    parts = []
    for e in range(2):
        h = _mask_half(x, m0, e)
        parts.append(pltpu.roll(h, HEAD_DIM, 1) if kes[e] != e else h)
    return jnp.concatenate(parts, axis=0)


def _unstack_heads(y, m0, kes):
    parts = []
    for e in range(2):
        h = y[e * BLK:(e + 1) * BLK]
        parts.append(pltpu.roll(h, HEAD_DIM, 1) if kes[e] != e else h)
    return jnp.where(m0, parts[0], parts[1])


def _rows(r, dil):
    return pl.ds(r, BLK, stride=dil) if dil > 1 else pl.ds(0, BLK)


def _band_units(dil, nsub):
    assert dil == 1 or nsub == 1
    if nsub == 1:
        return [(_rows(r, dil), ("prev", _rows(r, dil)), 0) for r in range(dil)]
    units = [(pl.ds(0, BLK), ("prev", pl.ds(0, BLK)), 0)]
    units += [(pl.ds(BLK * s, BLK), ("cur", pl.ds(BLK * (s - 1), BLK)), s) for s in range(1, nsub)]
    return units


def _head_col_spec(ppk, RB, row_block):
    if ppk == 1:
        return pl.BlockSpec((None, RB, 1), lambda p, i: (p, row_block(i), 0))
    return pl.BlockSpec((ppk, RB, 1), lambda p, i: (p, row_block(i), 0))


def _band_specs(dil, nsub, ppk, q_blk, k_blk, v_blk, kv_shared, nb):
    RB = BLK * dil * nsub
    PB = BLK if nsub > 1 else RB
    qw = LANES * ppk
    kw = LANES if kv_shared else qw

    def cur(i):
        return jnp.minimum(i, nb - 1)

    def prev(i):
        return jnp.maximum(i * nsub - 1, 0) if nsub > 1 else jnp.maximum(i - 1, 0)

    def kidx(base):
        return (lambda p, i: (cur(i), base)) if kv_shared else (lambda p, i: (cur(i), base + p))

    def pidx(base):
        return (lambda p, i: (prev(i), base)) if kv_shared else (lambda p, i: (prev(i), base + p))

    specs = [pl.BlockSpec((RB, qw), lambda p, i: (cur(i), q_blk + p)),
             pl.BlockSpec((RB, kw), kidx(k_blk)), pl.BlockSpec((PB, kw), pidx(k_blk)),
             pl.BlockSpec((RB, kw), kidx(v_blk)), pl.BlockSpec((PB, kw), pidx(v_blk))]
    return specs if dil == 1 else [specs[0], specs[1], specs[3]]


def qk_norm(qkv, q_gain2, k_gain2, *, width, steps, n_q, tm, name):
    T = qkv.shape[0]
    nsb = width // LANES

    def body(x_ref, qg_ref, kg_ref, o_ref):
        m0 = _lane0()
        for b in range(nsb):
            is_q = ((pl.program_id(1) * nsb + b) < n_q).astype(F32)
            gain = qg_ref[...] * is_q + kg_ref[...] * (1.0 - is_q)
            cols = pl.ds(LANES * b, LANES)
            xv = x_ref[:, cols]
            o_ref[:, cols] = xv * _head_rms(xv, m0) * gain

    gspec = pl.BlockSpec((1, LANES), lambda i, j: (0, 0))
    return pl.pallas_call(
        body, grid=(T // tm, steps),
        in_specs=[pl.BlockSpec((tm, width), lambda i, j: (i, j)), gspec, gspec],
        out_specs=pl.BlockSpec((tm, width), lambda i, j: (i, j)),
        out_shape=jax.ShapeDtypeStruct((T, width * steps), F32),
        compiler_params=_cparams(2), name=name)(qkv, q_gain2, k_gain2)


def banded_fwd(qkn, qkv, slopes, sinks, *, dil, nsub, ppk, q_blk, k_blk, v_blk, n_heads, group,
               max_dist, name):
    T = qkv.shape[0]
    RB = BLK * dil * nsub
    nb = T // RB
    npair = n_heads // 2
    kv_shared = group > 1
    scale = HEAD_DIM ** -0.5
    has_sink = sinks is not None

    def body(*refs):
        slope_ref = refs[0]
        if has_sink:
            sink_ref, refs = refs[1], refs[2:]
        else:
            refs = refs[1:]
        if dil == 1:
            q_ref, kc_ref, kp_ref, vc_ref, vp_ref, o_ref, l_ref, lc0_ref, lc1_ref = refs
        else:
            q_ref, kc_ref, vc_ref, o_ref, l_ref, lc0_ref, lc1_ref, kp_ref, vp_ref = refs
        pb = pl.program_id(0)
        i = pl.program_id(1)
        if dil > 1:
            @pl.when(i == 0)
            def _():
                kp_ref[...] = jnp.zeros_like(kp_ref)
                vp_ref[...] = jnp.zeros_like(vp_ref)
        m0 = _lane0()
        distf, valid_first, top = _band_masks2(max_dist, i > 0, i >= 0)
        valid_inner = _band_masks2(max_dist, i >= 0, i >= 0)[1] if nsub > 1 else None
        for u, (rows, (src, prows), sub) in enumerate(_band_units(dil, nsub)):
            valid = valid_first if sub == 0 else valid_inner
            kpr, vpr = (kp_ref, vp_ref) if src == "prev" else (kc_ref, vc_ref)
            kcache = {}
            for jp in range(ppk):
                cs = pl.ds(LANES * jp, LANES)
                jk = 0 if kv_shared else jp
                if jk not in kcache:
                    ks = pl.ds(LANES * jk, LANES)
                    kcur, vcur = kc_ref[rows, ks], vc_ref[rows, ks]
                    if dil == 1:
                        kprev, vprev = kpr[prows, ks], vpr[prows, ks]
                    else:
                        kprev, vprev = kp_ref[u, :, ks], vp_ref[u, :, ks]
                        kp_ref[u, :, ks] = kcur
                        vp_ref[u, :, ks] = vcur
                    kcat = jnp.concatenate([kprev, kcur], axis=0)
                    vcat = jnp.concatenate([vprev, vcur], axis=0)
                    kcache[jk] = (kcat.astype(BF16), vcat.astype(BF16))
                kn, vcat = kcache[jk]
                qn = q_ref[rows, cs]
                kes = [((2 * jp + e) // group) % 2 if kv_shared else e for e in range(2)]
                hidx = 2 * (pb * ppk + jp)
                qs = _stack_heads(qn, m0, kes).astype(BF16)
                slope = jnp.where(top, slope_ref[hidx], slope_ref[hidx + 1])
                s = jnp.where(valid, _dot_nt(qs, kn) * scale - slope * distf, NEG)
                m = jnp.max(s, axis=-1, keepdims=True)
                if has_sink:
                    sk = jnp.where(top, sink_ref[hidx], sink_ref[hidx + 1])
                    m = jnp.maximum(m, sk)
                p = jnp.exp(s - m)
                den = jnp.sum(p, axis=-1, keepdims=True)
                if has_sink:
                    den = den + jnp.exp(sk - m)
                o_full = _dot((p * (1.0 / den)).astype(BF16), vcat)
                o_ref[rows, cs] = _unstack_heads(o_full, m0, kes)
                lse = m + jnp.log(den)
                l_ref[rows, cs] = _unstack_heads(jnp.broadcast_to(lse, (2 * BLK, LANES)), m0, [0, 1])
                for e, lc_ref in enumerate((lc0_ref, lc1_ref)):
                    if ppk == 1:
                        lc_ref[rows, :] = lse[e * BLK:(e + 1) * BLK]
                    else:
                        lc_ref[jp, rows, :] = lse[e * BLK:(e + 1) * BLK]

    smem = pl.BlockSpec(memory_space=pltpu.SMEM)
    qw = LANES * ppk
    ospec = pl.BlockSpec((RB, qw), lambda p, i: (i, p))
    oshape = jax.ShapeDtypeStruct((T, n_heads * HEAD_DIM), F32)
    args = [slopes] + ([sinks] if has_sink else []) + ([qkn] * 3 + [qkv] * 2 if dil == 1 else [qkn, qkn, qkv])
    kw = LANES if kv_shared else qw
    prev_scratch = [] if dil == 1 else [pltpu.VMEM((dil, BLK, kw), F32)] * 2
    return pl.pallas_call(
        body, grid=(npair // ppk, nb),
        in_specs=[smem] * (2 if has_sink else 1) + _band_specs(dil, nsub, ppk, q_blk, k_blk, v_blk, kv_shared, nb),
        out_specs=[ospec, ospec] + [_head_col_spec(ppk, RB, lambda i: i)] * 2,
        out_shape=[oshape, oshape] + [jax.ShapeDtypeStruct((npair, T, 1), F32)] * 2,
        scratch_shapes=prev_scratch, compiler_params=_cparams(2), name=name)(*args)


def banded_bwd(qkn, qkv, slopes, sinks, do, o, lsec, w, omix, *, dil, nsub, ppk, q_blk, k_blk, v_blk,
               n_heads, group, max_dist, do_blk, name):
    T = qkv.shape[0]
    RB = BLK * dil * nsub
    nb = T // RB
    npair = n_heads // 2
    kv_shared = group > 1
    scale = HEAD_DIM ** -0.5
    has_sink = sinks is not None
    mixed = w is not None
    qw = LANES * ppk

    def body(*refs):
        slope_ref = refs[0]
        if has_sink:
            sink_ref, refs = refs[1], refs[2:]
        else:
            refs = refs[1:]
        if dil == 1:
            q_ref, kc_ref, kp_ref, vc_ref, vp_ref, do_ref, lc0_ref, lc1_ref = refs[:8]
            refs = refs[8:]
        else:
            q_ref, kc_ref, vc_ref, do_ref, lc0_ref, lc1_ref = refs[:6]
            refs, kp_ref, vp_ref = refs[6:-2], refs[-2], refs[-1]
        if mixed:
            w_ref, om_ref, refs = refs[0], refs[1], refs[2:]
        else:
            o_ref, refs = refs[0], refs[1:]
        dq_ref, dk_ref, dv_ref, dsk_ref, ck_ref, cv_ref = refs
        pb = pl.program_id(0)
        i = pl.program_id(1)
        live = i < nb
        m0 = _lane0()
        lane = lax.broadcasted_iota(jnp.int32, (1, LANES), 1)
        distf, valid_first, top = _band_masks2(max_dist, i > 0, live)
        valid_inner = _band_masks2(max_dist, i >= 0, live)[1] if nsub > 1 else None
        livef = live.astype(F32)

        def half_rows(x):
            s0 = jnp.sum(jnp.where(m0, x, 0.0), axis=-1, keepdims=True)
            s1 = jnp.sum(jnp.where(m0, 0.0, x), axis=-1, keepdims=True)
            return jnp.concatenate([s0, s1], axis=0)

        @pl.when((pb == 0) & (i == 0))
        def _():
            dsk_ref[...] = jnp.zeros_like(dsk_ref)

        @pl.when(i == 0)
        def _():
            ck_ref[...] = jnp.zeros_like(ck_ref)
            cv_ref[...] = jnp.zeros_like(cv_ref)

        dsk_acc = jnp.zeros((1, LANES), F32)
        if nsub > 1:
            dk_ref[...] = ck_ref[...]
            dv_ref[...] = cv_ref[...]
        if dil > 1:
            @pl.when(i == 0)
            def _():
                kp_ref[...] = jnp.zeros_like(kp_ref)
                vp_ref[...] = jnp.zeros_like(vp_ref)

        for u, (rows, (src, prows), sub) in enumerate(_band_units(dil, nsub)):
            valid = valid_first if sub == 0 else valid_inner
            kpr, vpr = (kp_ref, vp_ref) if src == "prev" else (kc_ref, vc_ref)
            ck_u, cv_u = (ck_ref.at[u], cv_ref.at[u]) if dil > 1 else (None, None)
            for jp in range(ppk):
                cs = pl.ds(LANES * jp, LANES)
                ks = pl.ds(0, LANES) if kv_shared else cs
                kcur, vcur = kc_ref[rows, ks], vc_ref[rows, ks]
                if dil == 1:
                    kprev, vprev = kpr[prows, ks], vpr[prows, ks]
                else:
                    kprev, vprev = kp_ref[u, :, ks], vp_ref[u, :, ks]
                    kp_ref[u, :, ks] = kcur
                    vp_ref[u, :, ks] = vcur
                kn = jnp.concatenate([kprev, kcur], axis=0).astype(BF16)
                vcat = jnp.concatenate([vprev, vcur], axis=0).astype(BF16)
                dov = do_ref[rows, cs]
                if mixed:
                    dov = dov * w_ref[rows, cs]
                    shift = half_rows(dov * om_ref[rows, cs])
                else:
                    shift = half_rows(dov * o_ref[rows, cs])
                kes = [((2 * jp + e) // group) % 2 if kv_shared else e for e in range(2)]
                hidx = 2 * (pb * ppk + jp)
                qs = _stack_heads(q_ref[rows, cs], m0, kes).astype(BF16)
                dos = _stack_heads(dov, m0, kes).astype(BF16)
                lse = jnp.concatenate([ref[rows, :] if ppk == 1 else ref[jp, rows, :]
                                       for ref in (lc0_ref, lc1_ref)], axis=0)
                slope = jnp.where(top, slope_ref[hidx], slope_ref[hidx + 1])
                p = jnp.where(valid, jnp.exp(_dot_nt(qs, kn) * scale - slope * distf - lse), 0.0)
                ds = (p * (_dot_nt(dos, vcat) - shift)).astype(BF16)
                dqn = _unstack_heads(_dot(ds, kn), m0, kes) * scale
                dkn = _dot_tn(ds, qs) * scale
                dvv = _dot_tn(p.astype(BF16), dos)
                if has_sink:
                    sk = jnp.where(top, sink_ref[hidx], sink_ref[hidx + 1])
                    contrib = -jnp.exp(sk - lse) * shift * livef
                    for e in range(2):
                        tot = jnp.sum(contrib[e * BLK:(e + 1) * BLK], axis=0, keepdims=True)
                        dsk_acc = dsk_acc + jnp.where(lane == (2 * jp + e), tot, 0.0)
                dk_raw = dkn

                @pl.when(live)
                def _():
                    dq_ref[rows, cs] = dqn

                if dil > 1:
                    dk_ref[rows, cs] = ck_u[:, cs] + dk_raw[:BLK]
                    dv_ref[rows, cs] = cv_u[:, cs] + dvv[:BLK]
                    ck_u[:, cs] = dk_raw[BLK:]
                    cv_u[:, cs] = dvv[BLK:]
                    continue
                if nsub == 1:
                    dk_ref[rows, cs] = ck_ref[rows, cs] + dk_raw[:BLK]
                    dv_ref[rows, cs] = cv_ref[rows, cs] + dvv[:BLK]
                elif sub == 0:
                    last = pl.ds(RB - BLK, BLK)
                    dk_ref[last, cs] += dk_raw[:BLK]
                    dv_ref[last, cs] += dvv[:BLK]
                else:
                    ck_ref[prows, cs] += dk_raw[:BLK]
                    cv_ref[prows, cs] += dvv[:BLK]
                ck_ref[rows, cs] = dk_raw[BLK:]
                cv_ref[rows, cs] = dvv[BLK:]
        dsk_ref[...] += dsk_acc

    smem = pl.BlockSpec(memory_space=pltpu.SMEM)
    gspec = pl.BlockSpec((1, LANES), lambda p, i: (0, 0))

    def cur(i):
        return jnp.minimum(i, nb - 1)

    qspec = pl.BlockSpec((RB, qw), lambda p, i: (cur(i), p))
    dospec = pl.BlockSpec((RB, qw), lambda p, i: (cur(i), do_blk + p))
    kvout = pl.BlockSpec((RB, qw), lambda p, i: (jnp.maximum(i - 1, 0), p))
    in_specs = ([smem] * (2 if has_sink else 1) + _band_specs(dil, nsub, ppk, q_blk, k_blk, v_blk, kv_shared, nb)
                + [dospec] + [_head_col_spec(ppk, RB, cur)] * 2
                + ([qspec, qspec] if mixed else [qspec]))
    args = ([slopes] + ([sinks] if has_sink else []) + ([qkn] * 3 + [qkv] * 2 if dil == 1 else [qkn, qkn, qkv])
            + [do, lsec[0], lsec[1]]
            + ([w, omix] if mixed else [o]))
    full = jax.ShapeDtypeStruct((T, n_heads * HEAD_DIM), F32)
    row = jax.ShapeDtypeStruct((1, LANES), F32)
    return pl.pallas_call(
        body, grid=(npair // ppk, nb + 1), in_specs=in_specs,
        out_specs=[qspec, kvout, kvout, gspec],
        out_shape=[full, full, full, row],
        scratch_shapes=([pltpu.VMEM((RB, qw), F32)] * 2 if dil == 1
                        else [pltpu.VMEM((dil, BLK, qw), F32)] * 2 + [pltpu.VMEM((dil, BLK, qw), F32)] * 2),
        compiler_params=_cparams(2), name=name)(*args)


def mix_fwd(o1, o2, o3, l1, l2, l3, *, tm, name):
    T, C = o1.shape

    def body(o1r, o2r, o3r, l1r, l2r, l3r, o_ref, w1r, w2r, w3r):
        a, b, c = l1r[...], l2r[...], l3r[...]
        m = jnp.maximum(jnp.maximum(a, b), c)
        ea, eb, ec = jnp.exp(a - m), jnp.exp(b - m), jnp.exp(c - m)
        inv = 1.0 / (ea + eb + ec)
        wa, wb, wc = ea * inv, eb * inv, ec * inv
        o_ref[...] = wa * o1r[...] + wb * o2r[...] + wc * o3r[...]
        w1r[...] = wa
        w2r[...] = wb
        w3r[...] = wc

    spec = pl.BlockSpec((tm, C), lambda i: (i, 0))
    shp = jax.ShapeDtypeStruct((T, C), F32)
    return pl.pallas_call(body, grid=(T // tm,), in_specs=[spec] * 6, out_specs=[spec] * 4, out_shape=[shp] * 4,
                          compiler_params=_cparams(1), name=name)(o1, o2, o3, l1, l2, l3)


def _qk_norm_bwd(raw, dn, gain, m0):
    r = _head_rms(raw, m0)
    h = raw * r
    dh = dn * gain
    d_raw = r * (dh - h * (_half_sum(dh * h, m0) * (1.0 / HEAD_DIM)))
    return d_raw, jnp.sum(dn * h, axis=0, keepdims=True)


def _acc_rows(ref, val):
    @pl.when(pl.program_id(0) == 0)
    def _():
        ref[...] = val

    @pl.when(pl.program_id(0) > 0)
    def _():
        ref[...] += val


def assemble_odd(parts, qkv, q_gain2, k_gain2, *, tm, name):
    T, C = parts[0][0].shape
    nbk = C // LANES

    def body(*refs):
        qkv_ref, qg_ref, kg_ref, o_ref, dqg_ref, dkg_ref = refs[9:]
        m0 = _lane0()
        sums = [refs[j][...] + refs[3 + j][...] + refs[6 + j][...] for j in range(3)]
        o_ref[:, pl.ds(2 * C, C)] = sums[2].astype(o_ref.dtype)
        for j, (g_ref, acc_ref) in enumerate(((qg_ref, dqg_ref), (kg_ref, dkg_ref))):
            dgain = jnp.zeros((1, LANES), F32)
            for b in range(nbk):
                cols = pl.ds(C * j + LANES * b, LANES)
                d_raw, part = _qk_norm_bwd(qkv_ref[:, cols], sums[j][:, LANES * b:LANES * (b + 1)], g_ref[...], m0)
                o_ref[:, cols] = d_raw.astype(o_ref.dtype)
                dgain = dgain + part
            _acc_rows(acc_ref, dgain)

    spec = pl.BlockSpec((tm, C), lambda i: (i, 0))
    gspec = pl.BlockSpec((1, LANES), lambda i: (0, 0))
    flat = [parts[p][j] for p in range(3) for j in range(3)]
    row = jax.ShapeDtypeStruct((1, LANES), F32)
    return pl.pallas_call(body, grid=(T // tm,),
                          in_specs=[spec] * 9 + [pl.BlockSpec((tm, 2 * C), lambda i: (i, 0)), gspec, gspec],
                          out_specs=[pl.BlockSpec((tm, 3 * C), lambda i: (i, 0)), gspec, gspec],
                          out_shape=[jax.ShapeDtypeStruct((T, 3 * C), BF16), row, row],
                          compiler_params=_cparams(1), name=name)(*flat, qkv, q_gain2, k_gain2)


def assemble_even(dqa, dka4, dva4, dqb, dkb, dvb, qkv, q_gain2, k_gain2, *, tm, name):
    T = dqa.shape[0]
    W = 512
    QK = 768

    def body(dqa_r, dka_r, dva_r, dqb_r, dkb_r, dvb_r, qkv_ref, qg_ref, kg_ref, o_ref, dqg_ref, dkg_ref):
        m0 = _lane0()
        ka = dka_r[...]
        va = dva_r[...]
        dqn = dqa_r[...]
        dgain = jnp.zeros((1, LANES), F32)
        for b in range(W // LANES):
            cols = pl.ds(LANES * b, LANES)
            d_raw, part = _qk_norm_bwd(qkv_ref[:, cols], dqn[:, LANES * b:LANES * (b + 1)], qg_ref[...], m0)
            o_ref[:, cols] = d_raw.astype(o_ref.dtype)
            dgain = dgain + part
        _acc_rows(dqg_ref, dgain)
        dkn = ka[:, 0:128] + ka[:, 128:256] + ka[:, 256:384] + ka[:, 384:512]
        d_raw, part = _qk_norm_bwd(qkv_ref[:, pl.ds(W, LANES)], dkn, kg_ref[...], m0)
        dt = o_ref.dtype
        o_ref[:, pl.ds(W, LANES)] = d_raw.astype(dt)
        _acc_rows(dkg_ref, part)
        o_ref[:, pl.ds(640, LANES)] = (va[:, 0:128] + va[:, 128:256] + va[:, 256:384] + va[:, 384:512]).astype(dt)
        o_ref[:, pl.ds(768, W)] = dqb_r[...].astype(dt)
        o_ref[:, pl.ds(1280, W)] = dkb_r[...].astype(dt)
        o_ref[:, pl.ds(1792, W)] = dvb_r[...].astype(dt)

    spec = pl.BlockSpec((tm, W), lambda i: (i, 0))
    gspec = pl.BlockSpec((1, LANES), lambda i: (0, 0))
    row = jax.ShapeDtypeStruct((1, LANES), F32)
    return pl.pallas_call(body, grid=(T // tm,),
                          in_specs=[spec] * 6 + [pl.BlockSpec((tm, QK), lambda i: (i, 0)), gspec, gspec],
                          out_specs=[pl.BlockSpec((tm, 2304), lambda i: (i, 0)), gspec, gspec],
                          out_shape=[jax.ShapeDtypeStruct((T, 2304), BF16), row, row],
                          compiler_params=_cparams(1), name=name)(dqa, dka4, dva4, dqb, dkb, dvb, qkv, q_gain2, k_gain2)


STICK_T = 256
STICK_DEAD = -110.0


def _split_bf16(x):
    hi = x.astype(BF16)
    lo = (x - hi.astype(F32)).astype(BF16)
    return hi, lo


def _stick_logits(qm, kt, scale, diag):
    n = STICK_T
    row = lax.broadcasted_iota(jnp.int32, (n, n), 0)
    col = lax.broadcasted_iota(jnp.int32, (n, n), 1)
    mask = col < row + jnp.where(diag, 0, n)
    z = _dot_nt(qm, kt) * scale
    lneg = -(jnp.maximum(z, 0.0) + jnp.log(1.0 + jnp.exp(-jnp.abs(z))))
    lpos = z + lneg
    lk = jnp.where(mask, lneg, 0.0)
    return mask, lpos, lneg, lk


def _cumsum_mm(x, tri):
    hi, lo = _split_bf16(x)
    return _dot(hi, tri) + _dot(lo, tri)


def stick_fwd(qkv, *, q_blk, k_blk, v_blk, n_pairs, name, riders=None, rider_args=()):
    T = qkv.shape[0]
    n = STICK_T
    nq = T // n
    scale = HEAD_DIM ** -0.5
    nc = riders.n if riders is not None else 0
    n_steps = n_pairs * nq
    stage_at = (0, (5 * n_steps) // 8, (15 * n_steps) // 16, n_steps - 1)

    def body(*refs):
        q_ref, k_ref, v_ref = refs[:3]
        x_refs, o_ref = refs[3:3 + nc], refs[3 + nc]
        out_refs, sems = refs[4 + nc:4 + 2 * nc], refs[4 + 2 * nc:]
        i = pl.program_id(1)
        step_id = pl.program_id(0) * nq + i

        def ride(which):
            if riders is not None:
                @pl.when(step_id == stage_at[which])
                def _():
                    riders.stage(which, x_refs, out_refs, sems)

        ride(0)
        ride(1)
        m0 = _lane0()
        r2 = lax.broadcasted_iota(jnp.int32, (n, n), 0)
        c2 = lax.broadcasted_iota(jnp.int32, (n, n), 1)
        tri_after = (r2 > c2).astype(BF16)
        qv = q_ref[...]
        out = jnp.zeros((n, LANES), F32)
        for e in range(2):
            qm = _mask_half(qv, m0, e).astype(BF16)

            def alive(st):
                t, _, carry = st
                return (t <= i) & (jnp.max(carry) > STICK_DEAD)

            def step(st, e=e, qm=qm):
                t, acc, carry = st
                start = pl.multiple_of((i - t) * n, n)
                kt = k_ref[pl.ds(start, n), :].astype(BF16)
                vt = _mask_half(v_ref[pl.ds(start, n), :], m0, e).astype(BF16)
                mask, lpos, _, lk = _stick_logits(qm, kt, scale, t == 0)
                after = _cumsum_mm(lk, tri_after) + carry
                a = jnp.where(mask, jnp.exp(lpos + after), 0.0)
                acc = acc + _dot(a.astype(BF16), vt)
                carry = carry + jnp.sum(lk, axis=-1, keepdims=True)
                return t + 1, acc, carry

            _, acc, _ = lax.while_loop(alive, step, (jnp.int32(0), jnp.zeros((n, LANES), F32),
                                                     jnp.zeros((n, 1), F32)))
            out = out + acc
        o_ref[...] = out
        ride(2)
        ride(3)

    outs = pl.pallas_call(
        body, grid=(n_pairs, nq),
        in_specs=[pl.BlockSpec((n, LANES), lambda p, i: (i, q_blk + p)),
                  pl.BlockSpec((T, LANES), lambda p, i: (0, k_blk + p)),
                  pl.BlockSpec((T, LANES), lambda p, i: (0, v_blk + p))] + [_ANY] * nc,
        out_specs=[pl.BlockSpec((n, LANES), lambda p, i: (i, p))] + [_ANY] * nc,
        out_shape=[jax.ShapeDtypeStruct((T, n_pairs * LANES), F32)] + (riders.shapes if nc else []),
        scratch_shapes=riders.sems if nc else [],
        compiler_params=_cparams(2), name=name)(qkv, qkv, qkv, *rider_args)
    return outs[0], list(outs[1:])


def stick_bwd(qkv, do, *, q_blk, k_blk, v_blk, do_blk, n_pairs, name, riders=None, rider_args=()):
    T = qkv.shape[0]
    n = STICK_T
    nq = T // n
    scale = HEAD_DIM ** -0.5
    nc = riders.n if riders is not None else 0

    def body(*refs):
        q_ref, k_ref, v_ref, do_ref = refs[:4]
        pre_refs = refs[4:4 + nc]
        dq_ref, dk_ref, dv_ref = refs[4 + nc:7 + nc]
        land_refs = refs[7 + nc:7 + 2 * nc]
        a_keep, g_keep, s_keep = refs[7 + 2 * nc:10 + 2 * nc]
        sems = refs[10 + 2 * nc:]
        i = pl.program_id(1)
        first_step = (pl.program_id(0) == 0) & (i == 0)
        last_step = (pl.program_id(0) == n_pairs - 1) & (i == nq - 1)
        m0 = _lane0()
        r2 = lax.broadcasted_iota(jnp.int32, (n, n), 0)
        c2 = lax.broadcasted_iota(jnp.int32, (n, n), 1)
        tri_after = (r2 > c2).astype(BF16)
        tri_from = (r2 >= c2).astype(BF16)

        if riders is not None:
            @pl.when(first_step)
            def _():
                riders.start(pre_refs, land_refs, sems)

        @pl.when(i == 0)
        def _():
            dk_ref[...] = jnp.zeros_like(dk_ref)
            dv_ref[...] = jnp.zeros_like(dv_ref)

        qv = q_ref[...]
        dov = do_ref[...]
        dq_out = jnp.zeros((n, LANES), F32)
        for e in range(2):
            qm = _mask_half(qv, m0, e).astype(BF16)
            dom = _mask_half(dov, m0, e).astype(BF16)

            def alive(st):
                t, carry, _ = st
                return (t <= i) & (jnp.max(carry) > STICK_DEAD)

            def scan(st, qm=qm, dom=dom):
                t, carry, gtot = st
                start = pl.multiple_of((i - t) * n, n)
                kt = k_ref[pl.ds(start, n), :].astype(BF16)
                vt = v_ref[pl.ds(start, n), :].astype(BF16)
                mask, lpos, lneg, lk = _stick_logits(qm, kt, scale, t == 0)
                a = jnp.where(mask, jnp.exp(lpos + _cumsum_mm(lk, tri_after) + carry), 0.0)
                g = _dot_nt(dom, vt) * a
                a_keep[t] = a.astype(BF16)
                g_keep[t] = g
                s_keep[t] = jnp.exp(lneg).astype(BF16)
                return (t + 1, carry + jnp.sum(lk, axis=-1, keepdims=True),
                        gtot + jnp.sum(g, axis=-1, keepdims=True))

            z1 = jnp.zeros((n, 1), F32)
            n_live, _, gtot = lax.while_loop(alive, scan, (jnp.int32(0), z1, z1))

            def step(t, st, e=e, qm=qm, dom=dom, gtot=gtot):
                dq_acc, gright = st
                start = pl.multiple_of((i - t) * n, n)
                g = g_keep[t]
                sneg = s_keep[t].astype(F32)
                before = gtot - (_cumsum_mm(g, tri_from) + gright)
                mask = c2 < r2 + jnp.where(t == 0, 0, n)
                dz = jnp.where(mask, g * sneg - before * (1.0 - sneg), 0.0) * scale
                dzb = dz.astype(BF16)
                dq_acc = dq_acc + _dot(dzb, _mask_half(k_ref[pl.ds(start, n), :], m0, e).astype(BF16))
                dk_ref[pl.ds(start, n), :] += _dot_tn(dzb, qm)
                dv_ref[pl.ds(start, n), :] += _dot_tn(a_keep[t], dom)
                return dq_acc, gright + jnp.sum(g, axis=-1, keepdims=True)

            dq_acc, _ = lax.fori_loop(0, n_live, step, (jnp.zeros((n, LANES), F32), z1))
            dq_out = dq_out + dq_acc
        dq_ref[...] = dq_out

        if riders is not None:
            @pl.when(last_step)
            def _():
                riders.finish(pre_refs, land_refs, sems)

    tile = pl.BlockSpec((n, LANES), lambda p, i: (i, p))
    whole = pl.BlockSpec((T, LANES), lambda p, i: (0, p))
    shp = jax.ShapeDtypeStruct((T, n_pairs * LANES), F32)
    outs = pl.pallas_call(
        body, grid=(n_pairs, nq),
        in_specs=[pl.BlockSpec((n, LANES), lambda p, i: (i, q_blk + p)),
                  pl.BlockSpec((T, LANES), lambda p, i: (0, k_blk + p)),
                  pl.BlockSpec((T, LANES), lambda p, i: (0, v_blk + p)),
                  pl.BlockSpec((n, LANES), lambda p, i: (i, do_blk + p))] + [_ANY] * nc,
        out_specs=[tile, whole, whole] + [_ANY] * nc,
        out_shape=[shp, shp, shp] + (riders.shapes if nc else []),
        scratch_shapes=[pltpu.VMEM((nq, n, n), BF16), pltpu.VMEM((nq, n, n), F32), pltpu.VMEM((nq, n, n), BF16)]
        + (riders.sems if nc else []),
        compiler_params=_cparams(2), name=name)(qkv, qkv, qkv, do, *rider_args)
    return outs[0], outs[1], outs[2], list(outs[3:])


def _xnorm(x):
    r = lax.rsqrt(jnp.mean(x * x, axis=-1, keepdims=True) + RMS_EPS)
    return r, x * r


def xattn_fwd(qraw, kvraw, q_gain, k_gain, *, tm, name):
    T = qraw.shape[0]
    scale = X_HEAD_DIM ** -0.5
    W = X_HEADS * X_HEAD_DIM

    def body(q_ref, kv_ref, qg_ref, kg_ref, o_ref):
        for h in range(X_HEADS):
            cs = pl.ds(X_HEAD_DIM * h, X_HEAD_DIM)
            _, qh = _xnorm(q_ref[:, cs])
            _, kh = _xnorm(kv_ref[:, cs])
            qn = (qh * qg_ref[...]).astype(BF16)
            kn = (kh * kg_ref[...]).astype(BF16)
            v = kv_ref[:, pl.ds(W + X_HEAD_DIM * h, X_HEAD_DIM)].astype(BF16)
            s = _dot_nt(qn, kn) * scale
            m = jnp.max(s, axis=-1, keepdims=True)
            p = jnp.exp(s - m)
            p = p / jnp.sum(p, axis=-1, keepdims=True)
            o_ref[:, cs] = _dot(p.astype(BF16), v)

    gspec = pl.BlockSpec((1, X_HEAD_DIM), lambda i: (0, 0))
    return pl.pallas_call(
        body, grid=(T // tm,),
        in_specs=[pl.BlockSpec((tm, W), lambda i: (i, 0)), pl.BlockSpec((MEM_LEN, 2 * W), lambda i: (0, 0)),
                  gspec, gspec],
        out_specs=pl.BlockSpec((tm, W), lambda i: (i, 0)),
        out_shape=jax.ShapeDtypeStruct((T, W), F32),
        compiler_params=_cparams(1), name=name)(qraw, kvraw, q_gain, k_gain)


def xattn_bwd(qraw, kvraw, q_gain, k_gain, do, o, *, tm, name):
    T = qraw.shape[0]
    nt = T // tm
    scale = X_HEAD_DIM ** -0.5
    W = X_HEADS * X_HEAD_DIM

    def body(q_ref, kv_ref, qg_ref, kg_ref, do_ref, o_ref, dq_ref, dkv_ref, dqg_ref, dkg_ref, dkn_ref):
        i = pl.program_id(0)

        @pl.when(i == 0)
        def _():
            dkv_ref[...] = jnp.zeros_like(dkv_ref)
            dkn_ref[...] = jnp.zeros_like(dkn_ref)
            dqg_ref[...] = jnp.zeros_like(dqg_ref)
            dkg_ref[...] = jnp.zeros_like(dkg_ref)

        qg = qg_ref[...]
        kg = kg_ref[...]
        dqg_acc = jnp.zeros((1, X_HEAD_DIM), F32)
        for h in range(X_HEADS):
            cs = pl.ds(X_HEAD_DIM * h, X_HEAD_DIM)
            vs = pl.ds(W + X_HEAD_DIM * h, X_HEAD_DIM)
            rq, qh = _xnorm(q_ref[:, cs])
            _, kh = _xnorm(kv_ref[:, cs])
            qn = (qh * qg).astype(BF16)
            kn = (kh * kg).astype(BF16)
            v = kv_ref[:, vs].astype(BF16)
            s = _dot_nt(qn, kn) * scale
            m = jnp.max(s, axis=-1, keepdims=True)
            p = jnp.exp(s - m)
            p = p / jnp.sum(p, axis=-1, keepdims=True)
            dov = do_ref[:, cs]
            delta = jnp.sum(dov * o_ref[:, cs], axis=-1, keepdims=True)
            dob = dov.astype(BF16)
            ds = (p * (_dot_nt(dob, v) - delta)).astype(BF16)
            dqn = _dot(ds, kn) * scale
            dkn_ref[:, cs] += _dot_tn(ds, qn) * scale
            dkv_ref[:, vs] += _dot_tn(p.astype(BF16), dob)
            dqg_acc = dqg_acc + jnp.sum(dqn * qh, axis=0, keepdims=True)
            dqh = dqn * qg
            dq_ref[:, cs] = (rq * (dqh - qh * jnp.mean(dqh * qh, axis=-1, keepdims=True))).astype(dq_ref.dtype)
        dqg_ref[...] += dqg_acc

        @pl.when(i == nt - 1)
        def _():
            dkg_acc = jnp.zeros((1, X_HEAD_DIM), F32)
            for h in range(X_HEADS):
                cs = pl.ds(X_HEAD_DIM * h, X_HEAD_DIM)
                rk, kh = _xnorm(kv_ref[:, cs])
                dkn = dkn_ref[:, cs]
                dkg_acc = dkg_acc + jnp.sum(dkn * kh, axis=0, keepdims=True)
                dkh = dkn * kg
                dkv_ref[:, cs] = rk * (dkh - kh * jnp.mean(dkh * kh, axis=-1, keepdims=True))
            dkg_ref[...] = dkg_acc

    gspec = pl.BlockSpec((1, X_HEAD_DIM), lambda i: (0, 0))
    tile = pl.BlockSpec((tm, W), lambda i: (i, 0))
    kvspec = pl.BlockSpec((MEM_LEN, 2 * W), lambda i: (0, 0))
    grow = jax.ShapeDtypeStruct((1, X_HEAD_DIM), F32)
    return pl.pallas_call(
        body, grid=(nt,), in_specs=[tile, kvspec, gspec, gspec, tile, tile],
        out_specs=[tile, kvspec, gspec, gspec],
        out_shape=[jax.ShapeDtypeStruct((T, W), BF16), jax.ShapeDtypeStruct((MEM_LEN, 2 * W), F32), grow, grow],
        scratch_shapes=[pltpu.VMEM((MEM_LEN, W), F32)],
        compiler_params=_cparams(1), name=name)(qraw, kvraw, q_gain, k_gain, do, o)


_ANY = pl.BlockSpec(memory_space=pl.ANY)


def _my_pos():
    return lax.axis_index("x"), lax.axis_index("y"), lax.axis_index("c")


def _pieces(arrays, chunks):
    out = []
    for a, (arr, n) in enumerate(zip(arrays, chunks)):
        rc = arr.shape[-2] // n
        out += [(a, pl.ds(ch * rc, rc)) for ch in range(n)]
    return out


class GatherBlocks:
    N_STAGES = 4

    def __init__(self, blks, chunks):
        self.shapes = [jax.ShapeDtypeStruct((N_DEV,) + b.shape, b.dtype) for b in blks]
        self.n = len(blks)
        self.pieces = _pieces(blks, chunks)
        n_p = len(self.pieces)
        self.sems = [pltpu.SemaphoreType.DMA((7 * n_p,)), pltpu.SemaphoreType.DMA((7 * n_p,)),
                     pltpu.SemaphoreType.DMA((n_p,))]

    def stage(self, which, x_refs, out_refs, sems):
        send_sems, recv_sems, local_sems = sems
        pieces, n_p = self.pieces, len(self.pieces)
        x, y, c = _my_pos()
        me, sibling = (x, y, c), (x, y, 1 - c)
        chips = [(1 - x, y), (x, 1 - y), (1 - x, 1 - y)]
        xn, yn, dg = [(*chip, c) for chip in chips]
        ps = range(n_p)

        def slot(block, p):
            px, py, pc = block
            a, rows = pieces[p]
            return out_refs[a].at[4 * px + 2 * py + pc, rows]

        def own(p):
            a, rows = pieces[p]
            return x_refs[a].at[rows]

        def copy(k, p, block, to, from_input=False):
            return pltpu.make_async_remote_copy(
                src_ref=own(p) if from_input else slot(block, p), dst_ref=slot(block, p),
                send_sem=send_sems.at[k * n_p + p], recv_sem=recv_sems.at[k * n_p + p],
                device_id=to, device_id_type=MESH)

        mine = [pltpu.make_async_copy(own(p), slot(me, p), local_sems.at[p]) for p in ps]
        first = [copy(k, p, me, to, from_input=True) for p in ps for k, to in ((1, xn), (2, yn), (0, sibling))]
        on_x = [copy(3, p, xn, yn) for p in ps if p % 2 == 0] + [copy(4, p, xn, sibling) for p in ps]
        on_y = [copy(3, p, yn, xn) for p in ps if p % 2 == 1] + [copy(5, p, yn, sibling) for p in ps]
        on_d = [copy(6, p, dg, sibling) for p in ps]
        if which == 0:
            for cp in first + mine:
                cp.start()
        elif which == 1:
            for p in ps:
                copy(1, p, xn, me).wait_recv()
                if p % 2 == 0:
                    copy(3, p, xn, yn).start()
                copy(4, p, xn, sibling).start()
                copy(2, p, yn, me).wait_recv()
                if p % 2 == 1:
                    copy(3, p, yn, xn).start()
                copy(5, p, yn, sibling).start()
        elif which == 2:
            for p in ps:
                copy(3, p, dg, me).wait_recv()
                copy(6, p, dg, sibling).start()
        else:
            for p in ps:
                copy(0, p, sibling, me).wait_recv()
            for k, chip in zip((4, 5, 6), chips):
                for p in ps:
                    copy(k, p, (*chip, 1 - c), me).wait_recv()
            for cp in first + on_x + on_y + on_d:
                cp.wait_send()
            for cp in mine:
                cp.wait()


def gather_blocks(blks, chunks, *, name):
    gb = GatherBlocks(blks, chunks)
    n = gb.n

    def body(*refs):
        x_refs, out_refs, sems = refs[:n], refs[n:2 * n], refs[2 * n:]
        for which in range(gb.N_STAGES):
            gb.stage(which, x_refs, out_refs, sems)

    return pl.pallas_call(body, out_shape=gb.shapes, in_specs=[_ANY] * n, out_specs=[_ANY] * n,
                          scratch_shapes=gb.sems, name=name)(*blks)


def gather_small(small, *, name):
    S, C = small.shape

    def body(s_ref, out_ref, send_sems, recv_sems, local_sem):
        x, y, c = _my_pos()
        my_id = 4 * x + 2 * y + c

        def copy(k, slot):
            px, py, pc = x ^ ((k >> 2) & 1), y ^ ((k >> 1) & 1), c ^ (k & 1)
            dst = my_id if slot == "mine" else 4 * px + 2 * py + pc
            return pltpu.make_async_remote_copy(
                src_ref=s_ref, dst_ref=out_ref.at[dst], send_sem=send_sems.at[k - 1], recv_sem=recv_sems.at[k - 1],
                device_id=(px, py, pc), device_id_type=MESH)

        own = pltpu.make_async_copy(s_ref, out_ref.at[my_id], local_sem)
        own.start()
        sends = [copy(k, "mine") for k in range(1, N_DEV)]
        for cp in sends:
            cp.start()
        for k in range(1, N_DEV):
            copy(k, "theirs").wait_recv()
        for cp in sends:
            cp.wait_send()
        own.wait()

    dma7 = pltpu.SemaphoreType.DMA((7,))
    return pl.pallas_call(
        body, out_shape=jax.ShapeDtypeStruct((N_DEV, S, C), small.dtype), in_specs=[_ANY], out_specs=_ANY,
        scratch_shapes=[dma7, dma7, pltpu.SemaphoreType.DMA], name=name)(small)


class PairExchange:
    def __init__(self, bigs, chunks):
        self.shapes = [jax.ShapeDtypeStruct((4,) + b.shape[1:], b.dtype) for b in bigs]
        self.n = len(bigs)
        self.pieces = _pieces(bigs, chunks)
        n_p = len(self.pieces)
        self.sems = [pltpu.SemaphoreType.DMA((4 * n_p,)), pltpu.SemaphoreType.DMA((4 * n_p,))]

    def _copies(self, big_refs, out_refs, sems):
        send_sems, recv_sems = sems
        n_p = len(self.pieces)
        x, y, c = _my_pos()

        def copy(b, p):
            a, rows = self.pieces[p]
            return pltpu.make_async_remote_copy(
                src_ref=big_refs[a].at[2 * b + (1 - c), rows], dst_ref=out_refs[a].at[b, rows],
                send_sem=send_sems.at[b * n_p + p], recv_sem=recv_sems.at[b * n_p + p],
                device_id=(x, y, 1 - c), device_id_type=MESH)

        return [copy(b, p) for b in range(4) for p in range(n_p)]

    def start(self, big_refs, out_refs, sems):
        for cp in self._copies(big_refs, out_refs, sems):
            cp.start()

    def finish(self, big_refs, out_refs, sems):
        cps = self._copies(big_refs, out_refs, sems)
        for cp in cps:
            cp.wait_recv()
        for cp in cps:
            cp.wait_send()


def _standalone(exchange, args, name):
    n = exchange.n

    def body(*refs):
        exchange.start(refs[:n], refs[n:2 * n], refs[2 * n:])
        exchange.finish(refs[:n], refs[n:2 * n], refs[2 * n:])

    return pl.pallas_call(body, out_shape=exchange.shapes, in_specs=[_ANY] * n, out_specs=[_ANY] * n,
                          scratch_shapes=exchange.sems, name=name)(*args)


class Riding:
    def __init__(self, riders):
        self.riders = [(ex, list(args)) for ex, args in riders]
        self.args = [a for _, args in self.riders for a in args]
        self.in_specs = [_ANY] * len(self.args)
        self.out_shapes = [s for ex, _ in self.riders for s in ex.shapes]
        self.out_specs = [_ANY] * len(self.out_shapes)
        self.scratch = [s for ex, _ in self.riders for s in ex.sems]

    def wrap(self, body, n_in, n_out, n_scratch, is_first, is_last):
        def wrapped(*refs):
            k = 0
            core = list(refs[:n_in])
            k = n_in
            r_in = []
            for ex, _ in self.riders:
                r_in.append(refs[k:k + ex.n])
                k += ex.n
            core += refs[k:k + n_out]
            k += n_out
            r_out = []
            for ex, _ in self.riders:
                r_out.append(refs[k:k + ex.n])
                k += ex.n
            core += refs[k:k + n_scratch]
            k += n_scratch
            r_sem = []
            for ex, _ in self.riders:
                r_sem.append(refs[k:k + len(ex.sems)])
                k += len(ex.sems)

            @pl.when(is_first())
            def _():
                for (ex, _), a, b, s in zip(self.riders, r_in, r_out, r_sem):
                    ex.start(a, b, s)

            body(*core)

            @pl.when(is_last())
            def _():
                for (ex, _), a, b, s in zip(self.riders, r_in, r_out, r_sem):
                    ex.finish(a, b, s)

        return wrapped

    def split(self, outs, n_out):
        core, rest, per = list(outs[:n_out]), list(outs[n_out:]), []
        for ex, _ in self.riders:
            per.append(rest[:ex.n])
            rest = rest[ex.n:]
        return core, per


def pair_sum(big, sib, c, *, tr, name):
    _, R, C = big.shape

    def body(c_ref, a_ref, s_ref, o_ref):
        o_ref[...] = (a_ref[...].astype(F32) + s_ref[...].astype(F32)).astype(o_ref.dtype)

    grid_spec = pltpu.PrefetchScalarGridSpec(
        num_scalar_prefetch=1, grid=(4, R // tr),
        in_specs=[pl.BlockSpec((None, tr, C), lambda b, i, c_ref: (2 * b + c_ref[0], i, 0)),
                  pl.BlockSpec((None, tr, C), lambda b, i, c_ref: (b, i, 0))],
        out_specs=pl.BlockSpec((None, tr, C), lambda b, i, c_ref: (b, i, 0)))
    return pl.pallas_call(body, grid_spec=grid_spec, out_shape=jax.ShapeDtypeStruct((4, R, C), big.dtype),
                          compiler_params=_cparams(2), name=name)(c.reshape(1).astype(jnp.int32), big, sib)


class ChipScatter:
    def __init__(self, pres, chunks):
        self.shapes = [jax.ShapeDtypeStruct(p.shape, p.dtype) for p in pres]
        self.n = len(pres)
        self.pieces = _pieces(pres, chunks)
        n_p = len(self.pieces)
        self.sems = [pltpu.SemaphoreType.DMA((3 * n_p,)), pltpu.SemaphoreType.DMA((3 * n_p,)),
                     pltpu.SemaphoreType.DMA((n_p,))]

    def _copies(self, pre_refs, out_refs, sems):
        send_sems, recv_sems, local_sems = sems
        n_p = len(self.pieces)
        x, y, c = _my_pos()
        my_chip = 2 * x + y
        chips = [(1 - x, y), (x, 1 - y), (1 - x, 1 - y)]

        def copy(j, p, slot):
            px, py = chips[j]
            a, rows = self.pieces[p]
            src_slot, dst_slot = (2 * px + py, my_chip) if slot == "mine" else (my_chip, 2 * px + py)
            return pltpu.make_async_remote_copy(
                src_ref=pre_refs[a].at[src_slot, rows], dst_ref=out_refs[a].at[dst_slot, rows],
                send_sem=send_sems.at[j * n_p + p], recv_sem=recv_sems.at[j * n_p + p],
                device_id=(px, py, c), device_id_type=MESH)

        own = [pltpu.make_async_copy(pre_refs[a].at[my_chip, rows], out_refs[a].at[my_chip, rows], local_sems.at[p])
               for p, (a, rows) in enumerate(self.pieces)]
        sends = [copy(j, p, "mine") for j in range(3) for p in range(n_p)]
        recvs = [copy(j, p, "theirs") for j in range(3) for p in range(n_p)]
        return own, sends, recvs

    def start(self, pre_refs, out_refs, sems):
        own, sends, _ = self._copies(pre_refs, out_refs, sems)
        for cp in sends + own:
            cp.start()

    def finish(self, pre_refs, out_refs, sems):
        own, sends, recvs = self._copies(pre_refs, out_refs, sems)
        for cp in recvs:
            cp.wait_recv()
        for cp in sends:
            cp.wait_send()
        for cp in own:
            cp.wait()


def chip_scatter(pres, chunks, *, name):
    cs = ChipScatter(pres, chunks)
    n = cs.n

    def body(*refs):
        pre_refs, out_refs, sems = refs[:n], refs[n:2 * n], refs[2 * n:]
        cs.start(pre_refs, out_refs, sems)
        cs.finish(pre_refs, out_refs, sems)

    return pl.pallas_call(body, out_shape=cs.shapes, in_specs=[_ANY] * n, out_specs=[_ANY] * n,
                          scratch_shapes=cs.sems, name=name)(*pres)


def sibling_send(blks, chunks, *, name):
    n = len(blks)
    pieces = _pieces(blks, chunks)
    n_p = len(pieces)

    def body(*refs):
        x_refs, out_refs = refs[:n], refs[n:2 * n]
        send_sems, recv_sems = refs[2 * n:]
        x, y, c = _my_pos()
        cps = [pltpu.make_async_remote_copy(
            src_ref=x_refs[a].at[rows], dst_ref=out_refs[a].at[rows], send_sem=send_sems.at[p],
            recv_sem=recv_sems.at[p], device_id=(x, y, 1 - c), device_id_type=MESH)
            for p, (a, rows) in enumerate(pieces)]
        for cp in cps:
            cp.start()
        for cp in cps:
            cp.wait_recv()
        for cp in cps:
            cp.wait_send()

    return pl.pallas_call(
        body, out_shape=[jax.ShapeDtypeStruct(b.shape, b.dtype) for b in blks],
        in_specs=[_ANY] * n, out_specs=[_ANY] * n,
        scratch_shapes=[pltpu.SemaphoreType.DMA((n_p,)), pltpu.SemaphoreType.DMA((n_p,))],
        name=name)(*blks)


def reduce_slots(land, *, tr, name):
    n, R, C = land.shape

    def body(l_ref, o_ref):
        acc = l_ref[0].astype(F32)
        for s in range(1, n):
            acc = acc + l_ref[s].astype(F32)
        o_ref[...] = acc

    return pl.pallas_call(
        body, grid=(R // tr,), in_specs=[pl.BlockSpec((n, tr, C), lambda i: (0, i, 0))],
        out_specs=pl.BlockSpec((tr, C), lambda i: (i, 0)), out_shape=jax.ShapeDtypeStruct((R, C), F32),
        compiler_params=_cparams(1), name=name)(land)


TM = 512


def _tk(d):
    return min(d.shape[0], 1024)


def ffn_fwd_fused(x, g, wgu, wd, *, tm, name, riders=None, rider_args=()):
    T, Dm = x.shape
    nb, _, cb = wgu.shape
    nh = nb // 2
    Fd = nh * cb
    nc = riders.n if riders is not None else 0
    n_steps = T // tm
    stage_at = (0, n_steps // 2, (13 * n_steps) // 16, n_steps - 1)

    def body(*refs):
        x_ref, g_ref, wgu_ref, wd_ref = refs[:4]
        r_in = refs[4:4 + nc]
        o_ref, gu_ref, h_ref = refs[4 + nc:7 + nc]
        r_out, sems = refs[7 + nc:7 + 2 * nc], refs[7 + 2 * nc:]

        def ride(which):
            if riders is not None:
                @pl.when(pl.program_id(0) == stage_at[which])
                def _():
                    riders.stage(which, r_in, r_out, sems)

        ride(0)
        ride(1)
        xv = x_ref[...]
        r = lax.rsqrt(jnp.mean(xv * xv, axis=-1, keepdims=True) + RMS_EPS)
        hb = (xv * r * g_ref[...]).astype(BF16)
        h_ref[...] = hb
        acc = jnp.zeros((tm, Dm), F32)
        for jj in range(nh):
            cols = pl.ds(cb * jj, cb)
            gate = _dot(hb, wgu_ref[jj]).astype(BF16)
            up = _dot(hb, wgu_ref[nh + jj]).astype(BF16)
            gu_ref[0, :, cols] = gate
            gu_ref[1, :, cols] = up
            gv = gate.astype(F32)
            act = (gv * _sigmoid(gv) * up.astype(F32)).astype(BF16)
            acc = acc + _dot(act, wd_ref[cols, :])
        o_ref[...] = xv + 0.5 * acc
        ride(2)
        ride(3)

    outs = pl.pallas_call(
        body, grid=(n_steps,),
        in_specs=[pl.BlockSpec((tm, Dm), lambda i: (i, 0)), pl.BlockSpec((1, Dm), lambda i: (0, 0)),
                  pl.BlockSpec((nb, Dm, cb), lambda i: (0, 0, 0)), pl.BlockSpec((Fd, Dm), lambda i: (0, 0))]
        + [_ANY] * nc,
        out_specs=[pl.BlockSpec((tm, Dm), lambda i: (i, 0)), pl.BlockSpec((2, tm, Fd), lambda i: (0, i, 0)),
                   pl.BlockSpec((tm, Dm), lambda i: (i, 0))] + [_ANY] * nc,
        out_shape=[jax.ShapeDtypeStruct((T, Dm), F32), jax.ShapeDtypeStruct((2, T, Fd), BF16),
                   jax.ShapeDtypeStruct((T, Dm), BF16)] + (riders.shapes if nc else []),
        scratch_shapes=riders.sems if nc else [],
        compiler_params=_cparams(1), name=name)(x, g, wgu, wd, *rider_args)
    return outs[0], outs[1], outs[2], list(outs[3:])


def ffn_fwd(x, g, wgu, wd, tag, riders=None, rider_args=()):
    xo, gu, h, rode = ffn_fwd_fused(x, g, wgu, wd, tm=256, name=f"{tag}_fwd", riders=riders, rider_args=rider_args)
    return xo, (x, gu, h), rode


def ffn_bwd(d, saved, g, wgu, wd, tag, ride_bact=None, ride_dwgu=None, before_dx=None):
    x, gu, h = saved
    dgu, dwd, *rode_a = ffn_bwd_act(d, wd, gu, tm=TM, tn=1408, name=f"{tag}_bact", riding=ride_bact)
    dwgu = mm_tn(h, dgu, scale=1.0, a_split=False, b_split=True, tm=TM, tn=1408, tk=_tk(d), out_blocked=True,
                 name=f"{tag}_dwgu", riding=ride_dwgu)
    rode_g = []
    if ride_dwgu is not None:
        dwgu, *rode_g = dwgu
    riding = before_dx(dwgu, dwd) if before_dx is not None else None
    dx, dg, *rode_x = mm_nt_normbwd(dgu, wgu, x, g, d, a_split=True, tm=_tk(d), tk=1408, name=f"{tag}_dx",
                                    riding=riding)
    return dx, dg, dwgu, dwd, (rode_a, rode_g, rode_x)


def _tile2(v):
    return jnp.concatenate([v, v], axis=-1).reshape(1, LANES)


def _fold2(v):
    return v[:, :HEAD_DIM] + v[:, HEAD_DIM:]


EVEN = dict(dil=1, nsub=2, ppk=4, q_blk=0, k_blk=4, v_blk=5, n_heads=A_Q_HEADS, group=A_GROUP, max_dist=A_WINDOW - 1)
STICK = dict(q_blk=6, k_blk=10, v_blk=14, n_pairs=4)


def _odd_cfg(dil):
    return dict(dil=dil, nsub=8 if dil == 1 else 1, ppk=1, q_blk=0, k_blk=8, v_blk=16, n_heads=C_HEADS, group=1,
                max_dist=BLK)


def even_fwd(x, g, win, qg, kg, sinks, wout, tag, riders=None, rider_args=()):
    qkv, h = norm_matmul(x, g, win, tm=_tk(x), tn=1152, split=False, name=f"{tag}_in")
    qg2, kg2 = _tile2(qg), _tile2(kg)
    slopes = jnp.asarray(_alibi(A_Q_HEADS), F32)
    qkn = qk_norm(qkv, qg2, kg2, width=768, steps=1, n_q=4, tm=TM, name=f"{tag}_qkn")
    oa, _, *lse = banded_fwd(qkn, qkv, slopes, sinks, name=f"{tag}_swa", **EVEN)
    ob, rode = stick_fwd(qkv, name=f"{tag}_stick", riders=riders, rider_args=rider_args, **STICK)
    o = jnp.concatenate([oa, ob], axis=1)
    xo = mm_nn(o, wout, res=x, tm=TM, tn=D_MODEL, tk=D_MODEL, name=f"{tag}_out")
    return xo, (x, qkv, qkn, h, oa, lse, o), rode


def even_bwd(d, saved, g, win, qg, kg, sinks, wout, tag, riders=None, rider_args=(), before_dx=None):
    x, qkv, qkn, h, oa, lse, o = saved
    qg2, kg2 = _tile2(qg), _tile2(kg)
    slopes = jnp.asarray(_alibi(A_Q_HEADS), F32)
    dwout = mm_tn(o, d, scale=1.0, a_split=False, b_split=False, tm=D_MODEL, tn=D_MODEL, tk=_tk(d), name=f"{tag}_dwout")
    do = mm_nt(d, wout, tm=TM, tn=D_MODEL, tk=D_MODEL, name=f"{tag}_do")
    dqa, dka4, dva4, dsk = banded_bwd(qkn, qkv, slopes, sinks, do, oa, lse, None, None,
                                      do_blk=0, name=f"{tag}_swa_b", **EVEN)
    dqb, dkb, dvb, rode = stick_bwd(qkv, do, do_blk=4, name=f"{tag}_stick_b", riders=riders, rider_args=rider_args,
                                    **STICK)
    dqkv, dqg, dkg = assemble_even(dqa, dka4, dva4, dqb, dkb, dvb, qkv, qg2, kg2, tm=TM, name=f"{tag}_asm")
    dwin = mm_tn(h, dqkv, scale=1.0, a_split=False, b_split=False, tm=D_MODEL, tn=1152, tk=_tk(d), name=f"{tag}_dwin")
    riding = before_dx(dwin, dwout) if before_dx is not None else None
    dx, dg, *rode_x = mm_nt_normbwd(dqkv, win, x, g, d, a_split=False, tm=TM, tk=1152, name=f"{tag}_dx", riding=riding)
    return dx, dg, dwin, _fold2(dqg), _fold2(dkg), dsk[:, :A_Q_HEADS], dwout, (rode, rode_x)


def odd_fwd(x, g, win, qg, kg, wout, tag):
    qkv, h = norm_matmul(x, g, win, tm=_tk(x), tn=768, split=False, name=f"{tag}_in")
    qg2, kg2 = _tile2(qg), _tile2(kg)
    qkn = qk_norm(qkv, qg2, kg2, width=D_MODEL, steps=2, n_q=8, tm=TM, name=f"{tag}_qkn")
    outs = []
    for p, (window, dil) in enumerate(C_PATTERNS):
        slopes = jnp.asarray(_alibi(C_HEADS), F32) * float(dil)
        outs.append(banded_fwd(qkn, qkv, slopes, None, name=f"{tag}_dil{p}", **_odd_cfg(dil)))
    o, w1, w2, w3 = mix_fwd(outs[0][0], outs[1][0], outs[2][0], outs[0][1], outs[1][1], outs[2][1],
                            tm=TM, name=f"{tag}_mix")
    xo = mm_nn(o, wout, res=x, tm=TM, tn=D_MODEL, tk=D_MODEL, name=f"{tag}_out")
    return xo, (x, qkv, qkn, h, outs, (w1, w2, w3), o)


def odd_bwd(d, saved, g, win, qg, kg, wout, tag):
    x, qkv, qkn, h, outs, ws, o = saved
    qg2, kg2 = _tile2(qg), _tile2(kg)
    dwout = mm_tn(o, d, scale=1.0, a_split=False, b_split=False, tm=D_MODEL, tn=D_MODEL, tk=_tk(d), name=f"{tag}_dwout")
    do = mm_nt(d, wout, tm=TM, tn=D_MODEL, tk=D_MODEL, name=f"{tag}_do")
    parts = []
    for p, (window, dil) in enumerate(C_PATTERNS):
        slopes = jnp.asarray(_alibi(C_HEADS), F32) * float(dil)
        dq, dk, dv, _ = banded_bwd(qkn, qkv, slopes, None, do, None, outs[p][2:], ws[p], o,
                                   do_blk=0, name=f"{tag}_dil{p}_b", **_odd_cfg(dil))
        parts.append((dq, dk, dv))
    dqkv, dqg, dkg = assemble_odd(parts, qkv, qg2, kg2, tm=256, name=f"{tag}_asm")
    dwin = mm_tn(h, dqkv, scale=1.0, a_split=False, b_split=False, tm=TM, tn=768, tk=_tk(d), out_blocked=True,
                 name=f"{tag}_dwin")
    dx, dg = mm_nt_normbwd(dqkv, win, x, g, d, a_split=False, tm=TM, tk=768, name=f"{tag}_dx")
    return dx, dg, dwin, _fold2(dqg), _fold2(dkg), dwout


def xa_fwd(x, mem, g, gm, wq, wkv, qg, kg, wo, tag):
    qraw, h = norm_matmul(x, g, wq, tm=TM, tn=D_MODEL, split=False, name=f"{tag}_q")
    kvraw, hm = norm_matmul(mem, gm, wkv, tm=MEM_LEN, tn=512, split=False, name=f"{tag}_kv")
    o = xattn_fwd(qraw, kvraw, qg, kg, tm=TM, name=f"{tag}_att")
    xo = mm_nn(o, wo, res=x, tm=TM, tn=D_MODEL, tk=D_MODEL, name=f"{tag}_o")
    return xo, (x, qraw, h, kvraw, hm, o)


def xa_bwd(d, saved, mem, g, gm, wq, wkv, qg, kg, wo, tag):
    x, qraw, h, kvraw, hm, o = saved
    dwo = mm_tn(o, d, scale=1.0, a_split=False, b_split=False, tm=D_MODEL, tn=D_MODEL, tk=_tk(d), name=f"{tag}_dwo")
    do = mm_nt(d, wo, tm=TM, tn=D_MODEL, tk=D_MODEL, name=f"{tag}_do")
    dq, dkv, dqg, dkg = xattn_bwd(qraw, kvraw, qg, kg, do, o, tm=TM, name=f"{tag}_att_b")
    dwq = mm_tn(h, dq, scale=1.0, a_split=False, b_split=False, tm=D_MODEL, tn=D_MODEL, tk=_tk(d), name=f"{tag}_dwq")
    dx, dg = mm_nt_normbwd(dq, wq, x, g, d, a_split=False, tm=TM, tk=D_MODEL, name=f"{tag}_dx")
    dwkv = mm_tn(hm, dkv, scale=1.0, a_split=False, b_split=False, tm=TM, tn=512, tk=MEM_LEN, out_blocked=True,
                 name=f"{tag}_dwkv")
    _, dgm = mm_nt_normbwd(dkv, wkv, mem, gm, None, a_split=False, tm=MEM_LEN, tk=512, name=f"{tag}_dmem")
    return dx, dg, dgm, dwq, dwkv, dqg, dkg, dwo


MATS = (("ffn1_w_gu", 1), ("ffn1_w_down", 0), ("ev_w_in", 1), ("ev_w_out", 0), ("od_w_in", 1), ("od_w_out", 0),
        ("xa_w_q", 0), ("xa_w_kv", 1), ("xa_w_o", 0), ("ffn2_w_gu", 1), ("ffn2_w_down", 0))
SMALLS = ("ffn1_norm", "mix_norm", "ev_q_gain", "ev_k_gain", "ev_sinks", "od_q_gain", "od_k_gain", "xa_norm",
          "xa_mem_norm", "xa_q_gain", "xa_k_gain", "ffn2_norm")
WEIGHTS = ("ffn1_norm", "ffn1_w_gu", "ffn1_w_down", "mix_norm", "ev_w_in", "ev_q_gain", "ev_k_gain", "ev_sinks",
           "ev_w_out", "od_w_in", "od_q_gain", "od_k_gain", "od_w_out", "xa_norm", "xa_mem_norm", "xa_w_q",
           "xa_w_kv", "xa_q_gain", "xa_k_gain", "xa_w_o", "ffn2_norm", "ffn2_w_gu", "ffn2_w_down")
SMALL_ROWS = 16
LAYER_GROUPS = (
    (((("ffn1_w_gu", 0), ("ffn2_w_gu", 0)), 4, 512),
     ((("ffn1_w_down", 0), ("ffn2_w_down", 0)), 2, 352),
     ((("ev_w_out", 0), ("xa_w_q", 0), ("xa_w_o", 0)), 1, 384),
     ((("xa_w_kv", 0),), 1, 512),
     ((("ev_w_in", 0),), 1, 512)),
    (((("ffn1_w_gu", 1), ("ffn2_w_gu", 1)), 4, 512),
     ((("ffn1_w_down", 1), ("ffn2_w_down", 1)), 2, 352),
     ((("od_w_out", 0), ("xa_w_q", 1), ("xa_w_o", 1)), 1, 384),
     ((("xa_w_kv", 1),), 1, 512),
     ((("od_w_in", 0),), 1, 512)),
)
GATHER_FIRST = (((("ffn1_w_gu", 0),), 2, 512), ((("ffn1_w_down", 0),), 1, 352), ((("ev_w_out", 0),), 1, 128),
                ((("ev_w_in", 0),), 1, 512))
GATHER_IN_FFN1 = (((("ffn2_w_gu", 0),), 2, 512), ((("ffn2_w_down", 0),), 1, 352))
GATHER_IN_STICK = (((("ffn1_w_gu", 1),), 2, 512), ((("ffn1_w_down", 1),), 1, 352), ((("od_w_out", 0),), 1, 128),
                   ((("od_w_in", 0),), 1, 512),
                   ((("xa_w_q", 0), ("xa_w_o", 0), ("xa_w_q", 1), ("xa_w_o", 1)), 1, 512),
                   ((("xa_w_kv", 0), ("xa_w_kv", 1)), 1, 512))
GATHER_IN_FFN2 = (((("ffn2_w_gu", 1),), 2, 512), ((("ffn2_w_down", 1),), 1, 352))
ROUNDS = {
    "1": LAYER_GROUPS[1],
    "0a": (((("ffn2_w_gu", 0),), 2, 512), ((("ffn2_w_down", 0),), 1, 352)) + LAYER_GROUPS[0][2:],
    "0b": (((("ffn1_w_gu", 0),), 2, 512), ((("ffn1_w_down", 0),), 1, 352)),
}


def _chunks_of(groups):
    return tuple(g[1] for g in groups)
COL_SHARDED = {name for name, axis in MATS if axis == 1}
BLOCKED = {"ffn1_w_gu", "ffn2_w_gu", "xa_w_kv", "od_w_in"}


def group_halves(shards, c, groups):
    out = []
    for members, _, _ in groups:
        halves = []
        for name, layer in members:
            _, r, cc = shards[name].shape
            half = lax.dynamic_index_in_dim(shards[name][layer].reshape(2, r // 2, cc), c, 0, keepdims=False)
            halves.append(half.astype(BF16))
        out.append(jnp.concatenate(halves, axis=0))
    return out


def full_weights(gathered, shards, groups):
    full = {}
    for (members, _, _), arr in zip(groups, gathered):
        for w, (name, layer) in enumerate(members):
            _, r, cc = shards[name].shape
            piece = arr[:, w * (r // 2):(w + 1) * (r // 2)].reshape(4, r, cc)
            if name not in COL_SHARDED:
                piece = piece.reshape(4 * r, cc)
            elif name not in BLOCKED:
                piece = piece.transpose(1, 0, 2).reshape(r, 4 * cc)
            full[(name, layer)] = piece
    return full


def group_grads(grads, shards, groups):
    out = []
    for members, _, _ in groups:
        parts = []
        for name, layer in members:
            _, r, cc = shards[name].shape
            gfull = grads[(name, layer)]
            if name in COL_SHARDED and name not in BLOCKED:
                gfull = gfull.reshape(2, r // 2, 4, cc).transpose(2, 0, 1, 3)
            parts.append(gfull.reshape(N_DEV, r // 2, cc))
        out.append(jnp.concatenate(parts, axis=1))
    return out


def shard_grads(mine, theirs, c, shards, groups):
    per = {}
    for (members, _, _), a, b in zip(groups, mine, theirs):
        for w, (name, layer) in enumerate(members):
            _, r, cc = shards[name].shape
            rows = slice(w * (r // 2), (w + 1) * (r // 2))
            lo = jnp.where(c == 0, a[rows], b[rows])
            hi = jnp.where(c == 0, b[rows], a[rows])
            per[(name, layer)] = jnp.concatenate([lo, hi], axis=0)
    return per


def pack_small(vals):
    row10 = jnp.concatenate([vals["xa_q_gain"].reshape(1, 512), vals["xa_k_gain"].reshape(1, 512)], axis=1)
    row11 = jnp.concatenate([vals["ev_q_gain"], vals["ev_k_gain"], vals["od_q_gain"], vals["od_k_gain"],
                             vals["ev_sinks"], jnp.zeros((1, 1024 - 4 * 64 - 8), F32)], axis=1)
    return jnp.concatenate([vals["ffn1_norm"], vals["mix_norm"], vals["xa_norm"], vals["xa_mem_norm"],
                            vals["ffn2_norm"], row10, row11, jnp.zeros((SMALL_ROWS - 12, 1024), F32)], axis=0)


def unpack_small(arr):
    return {"ffn1_norm": arr[0:2], "mix_norm": arr[2:4], "xa_norm": arr[4:6], "xa_mem_norm": arr[6:8],
            "ffn2_norm": arr[8:10],
            "xa_q_gain": arr[10:11, 0:512].reshape(2, 256), "xa_k_gain": arr[10:11, 512:1024].reshape(2, 256),
            "ev_q_gain": arr[11:12, 0:64], "ev_k_gain": arr[11:12, 64:128], "od_q_gain": arr[11:12, 128:192],
            "od_k_gain": arr[11:12, 192:256], "ev_sinks": arr[11:12, 256:264]}


def local_step(x, mem, target, W, small, prereduce, later):
    depth = small["ffn1_norm"].shape[0]

    def row(name, l):
        return small[name][l:l + 1]

    saved = []
    for l in range(depth):
        j = l // 2
        def riding_gather(host):
            if l == 0 and host in later:
                return GatherBlocks(later[host][0], later[host][1]), later[host][0]
            return None, ()

        riders, rider_args = riding_gather("ffn1")
        x, s1, rode = ffn_fwd(x, row("ffn1_norm", l), W[("ffn1_w_gu", l)], W[("ffn1_w_down", l)], f"l{l}_f1",
                              riders=riders, rider_args=rider_args)
        if riders is not None:
            W = {**W, **later["ffn1"][2](rode)}
        if l % 2 == 0:
            riders, rider_args = riding_gather("stick")
            x, s2, rode = even_fwd(x, row("mix_norm", l), W[("ev_w_in", j)], row("ev_q_gain", j),
                                   row("ev_k_gain", j), small["ev_sinks"][j], W[("ev_w_out", j)], f"l{l}_ev",
                                   riders=riders, rider_args=rider_args)
            if riders is not None:
                W = {**W, **later["stick"][2](rode)}
        else:
            x, s2 = odd_fwd(x, row("mix_norm", l), W[("od_w_in", j)], row("od_q_gain", j), row("od_k_gain", j),
                            W[("od_w_out", j)], f"l{l}_od")
        x, s3 = xa_fwd(x, mem, row("xa_norm", l), row("xa_mem_norm", l), W[("xa_w_q", l)], W[("xa_w_kv", l)],
                       row("xa_q_gain", l), row("xa_k_gain", l), W[("xa_w_o", l)], f"l{l}_xa")
        riders, rider_args = riding_gather("ffn2")
        x, s4, rode = ffn_fwd(x, row("ffn2_norm", l), W[("ffn2_w_gu", l)], W[("ffn2_w_down", l)], f"l{l}_f2",
                              riders=riders, rider_args=rider_args)
        if riders is not None:
            W = {**W, **later["ffn2"][2](rode)}
        saved.append((s1, s2, s3, s4))
    loss, d = loss_kernel(x, target, tm=TM, name="loss")

    gw = {}
    gs = {name: [None] * small[name].shape[0] for name in SMALLS}
    pending, landed = None, {}
    for l in reversed(range(depth)):
        j = l // 2
        s1, s2, s3, s4 = saved[l]
        d, dg, dwgu, dwd, _ = ffn_bwd(d, s4, row("ffn2_norm", l), W[("ffn2_w_gu", l)], W[("ffn2_w_down", l)],
                                      f"l{l}_f2")
        gs["ffn2_norm"][l] = dg
        gw[("ffn2_w_gu", l)], gw[("ffn2_w_down", l)] = dwgu, dwd
        d, dg, dgm, dwq, dwkv, dqg, dkg, dwo = xa_bwd(
            d, s3, mem, row("xa_norm", l), row("xa_mem_norm", l), W[("xa_w_q", l)], W[("xa_w_kv", l)],
            row("xa_q_gain", l), row("xa_k_gain", l), W[("xa_w_o", l)], f"l{l}_xa")
        gs["xa_norm"][l], gs["xa_mem_norm"][l], gs["xa_q_gain"][l], gs["xa_k_gain"][l] = dg, dgm, dqg, dkg
        gw[("xa_w_q", l)], gw[("xa_w_kv", l)], gw[("xa_w_o", l)] = dwq, dwkv, dwo
        split = l == 0 and l % 2 == 0 and ("0a" in ROUNDS)
        early = []
        pre_early = None
        if l % 2 == 0:
            riders, rider_args = None, ()
            if pending is not None:
                riders, rider_args = ChipScatter(pending[1], _chunks_of(ROUNDS[pending[0]])), pending[1]

            def before_mixer_dx(dwin, dwout, j=j, early=early):
                gw[("ev_w_in", j)], gw[("ev_w_out", j)] = dwin, dwout
                early += prereduce.pack(gw, "0a")
                return Riding([(PairExchange(early, _chunks_of(ROUNDS["0a"])), early)])

            d, dg, dwin, dqg, dkg, dsk, dwout, (rode, rode_x) = even_bwd(
                d, s2, row("mix_norm", l), W[("ev_w_in", j)], row("ev_q_gain", j), row("ev_k_gain", j),
                small["ev_sinks"][j], W[("ev_w_out", j)], f"l{l}_ev", riders=riders, rider_args=rider_args,
                before_dx=before_mixer_dx if split else None)
            if pending is not None:
                landed[pending[0]], pending = rode, None
            gs["ev_q_gain"][j], gs["ev_k_gain"][j], gs["ev_sinks"][j] = dqg, dkg, dsk
            gw[("ev_w_in", j)], gw[("ev_w_out", j)] = dwin, dwout
            if split:
                pre_early = prereduce.sums(early, rode_x[0], "0a")
        else:
            d, dg, dwin, dqg, dkg, dwout = odd_bwd(
                d, s2, row("mix_norm", l), W[("od_w_in", j)], row("od_q_gain", j), row("od_k_gain", j),
                W[("od_w_out", j)], f"l{l}_od")
            gs["od_q_gain"][j], gs["od_k_gain"][j] = dqg, dkg
            gw[("od_w_in", j)], gw[("od_w_out", j)] = dwin, dwout
        gs["mix_norm"][l] = dg
        packed = []

        rnd = "0b" if pre_early is not None else str(l)
        final = l == 0

        def before_dx(dwgu, dwd, l=l, packed=packed, rnd=rnd, final=final):
            gw[("ffn1_w_gu", l)], gw[("ffn1_w_down", l)] = dwgu, dwd
            packed += prereduce.pack(gw, rnd)
            chunks = _chunks_of(ROUNDS[rnd])
            if not final:
                return Riding([(PairExchange(packed, chunks), packed)])
            sib = _standalone(PairExchange(packed, chunks), packed, f"pair_grads{rnd}")
            pre = prereduce.sums(packed, sib, rnd)
            return Riding([(ChipScatter(pre, chunks), pre)])

        ride_bact = ride_dwgu = None
        if pre_early is not None:
            chunks = _chunks_of(ROUNDS["0a"])
            ride_bact = Riding([(ChipScatter(pre_early[:2], chunks[:2]), pre_early[:2])])
            ride_dwgu = Riding([(ChipScatter(pre_early[2:], chunks[2:]), pre_early[2:])])
        d, dg, dwgu, dwd, (rode_a, rode_g, rode_x) = ffn_bwd(
            d, s1, row("ffn1_norm", l), W[("ffn1_w_gu", l)], W[("ffn1_w_down", l)], f"l{l}_f1",
            ride_bact=ride_bact, ride_dwgu=ride_dwgu, before_dx=before_dx)
        gs["ffn1_norm"][l] = dg
        if pre_early is not None:
            landed["0a"] = list(rode_a[0]) + list(rode_g[0])
        if pending is not None:
            landed[pending[0]] = chip_scatter(pending[1], _chunks_of(ROUNDS[pending[0]]),
                                              name=f"scatter_grads{pending[0]}")
        if final:
            landed[rnd], pending = rode_x[0], None
        else:
            pending = (rnd, prereduce.sums(packed, rode_x[0], rnd))
    if pending is not None:
        landed[pending[0]] = chip_scatter(pending[1], _chunks_of(ROUNDS[pending[0]]),
                                          name=f"scatter_grads{pending[0]}")
    gsmall = {name: jnp.concatenate(v, axis=0) for name, v in gs.items()}
    return loss, d, landed, gsmall


def kernel(x, mem, ffn1_norm, ffn1_w_gu, ffn1_w_down, mix_norm, ev_w_in, ev_q_gain, ev_k_gain, ev_sinks, ev_w_out, od_w_in, od_q_gain, od_k_gain, od_w_out, xa_norm, xa_mem_norm, xa_w_q, xa_w_kv, xa_q_gain, xa_k_gain, xa_w_o, ffn2_norm, ffn2_w_gu, ffn2_w_down, loss_target, m_ffn1_norm, m_ffn1_w_gu, m_ffn1_w_down, m_mix_norm, m_ev_w_in, m_ev_q_gain, m_ev_k_gain, m_ev_sinks, m_ev_w_out, m_od_w_in, m_od_q_gain, m_od_k_gain, m_od_w_out, m_xa_norm, m_xa_mem_norm, m_xa_w_q, m_xa_w_kv, m_xa_q_gain, m_xa_k_gain, m_xa_w_o, m_ffn2_norm, m_ffn2_w_gu, m_ffn2_w_down, v_ffn1_norm, v_ffn1_w_gu, v_ffn1_w_down, v_mix_norm, v_ev_w_in, v_ev_q_gain, v_ev_k_gain, v_ev_sinks, v_ev_w_out, v_od_w_in, v_od_q_gain, v_od_k_gain, v_od_w_out, v_xa_norm, v_xa_mem_norm, v_xa_w_q, v_xa_w_kv, v_xa_q_gain, v_xa_k_gain, v_xa_w_o, v_ffn2_norm, v_ffn2_w_gu, v_ffn2_w_down):
    given = dict(locals())
    w = {n: given[n] for n in WEIGHTS}
    m = {n: given["m_" + n] for n in WEIGHTS}
    v = {n: given["v_" + n] for n in WEIGHTS}
    c = lax.axis_index("c")
    shards = {name: w[name] for name, _ in MATS}
    small = {n: w[n] for n in SMALLS}

    def gathering(groups):
        return group_halves(shards, c, groups), _chunks_of(groups), lambda got: full_weights(got, shards, groups)

    halves, chunks, unpack = gathering(GATHER_FIRST)
    full = unpack(gather_blocks(halves, chunks, name="gather_weights0"))
    later = {"ffn1": gathering(GATHER_IN_FFN1), "stick": gathering(GATHER_IN_STICK), "ffn2": gathering(GATHER_IN_FFN2)}

    class prereduce:
        @staticmethod
        def pack(gw, rnd):
            return group_grads(gw, shards, ROUNDS[rnd])

        @staticmethod
        def sums(packed, sib, rnd):
            return [pair_sum(p, s, c, tr=g[2], name=f"pair_sum{rnd}_{i}")
                    for i, (g, p, s) in enumerate(zip(ROUNDS[rnd], packed, sib))]

    loss_b, grad_x, landed, gsmall = local_step(x[0], mem[0], loss_target[0], full, small, prereduce, later)

    per = {}
    for rnd, land in sorted(landed.items()):
        groups = ROUNDS[rnd]
        mine = [reduce_slots(a, tr=g[2], name=f"sum_grads{rnd}_{i}") for i, (g, a) in enumerate(zip(groups, land))]
        theirs = sibling_send(mine, _chunks_of(groups), name=f"swap_grads{rnd}")
        per.update(shard_grads(mine, theirs, c, shards, groups))
    g = {name: jnp.stack([per[(name, layer)] for layer in range(w[name].shape[0])], axis=0) for name, _ in MATS}
    land_small = gather_small(pack_small(gsmall), name="gather_small")
    g_small = unpack_small(reduce_slots(land_small, tr=SMALL_ROWS, name="sum_small"))
    g.update(g_small)

    delta, new_m, new_v = {}, {}, {}
    for name, _ in MATS:
        shp = w[name].shape
        flat = [a.reshape(-1, shp[-1]) for a in (w[name], g[name], m[name], v[name])]
        dl, nm, nv = adamw(*flat, br=BLK, name=f"adamw_{name}")
        delta[name], new_m[name], new_v[name] = dl.reshape(shp), nm.reshape(shp), nv.reshape(shp)
    dl, nm, nv = adamw(pack_small(small), pack_small(g_small), pack_small({n: m[n] for n in SMALLS}),
                       pack_small({n: v[n] for n in SMALLS}), br=SMALL_ROWS, name="adamw_small")
    for dst, arr in ((delta, dl), (new_m, nm), (new_v, nv)):
        dst.update(unpack_small(arr))

    loss = lax.psum(loss_b[0, 0], ("x", "y", "c"))
    return (loss, grad_x[None], *[g[n] for n in WEIGHTS], *[delta[n] for n in WEIGHTS],
            *[new_m[n] for n in WEIGHTS], *[new_v[n] for n in WEIGHTS])
```

```python
import jax
import jax.numpy as jnp
from jax import lax
from jax.experimental import pallas as pl
from jax.experimental.pallas import tpu as pltpu

F32 = jnp.float32
BF16 = jnp.bfloat16

D_MODEL = 1024
HEAD_DIM = 64
LANES = 128
BLK = 128
RMS_EPS = 1e-6
MEM_LEN = 256
X_HEADS = 4
X_HEAD_DIM = 256
A_Q_HEADS = 8
A_GROUP = 4
A_WINDOW = 128
C_HEADS = 16
C_PATTERNS = ((128, 1), (512, 4), (2048, 16))
NEG = -1e30
VMEM_LIMIT = 56 * 2 ** 20

ADAM_LR = 0.001
ADAM_B1 = 0.9
ADAM_B2 = 0.999
ADAM_EPS = 1e-08
ADAM_WD = 0.01
ADAM_STEP = 10

N_DEV = 8
MESH = pl.DeviceIdType.MESH


def _cparams(n):
    return pltpu.CompilerParams(dimension_semantics=("arbitrary",) * n, vmem_limit_bytes=VMEM_LIMIT)


def _dot(a, b):
    return jnp.dot(a, b, preferred_element_type=F32)


def _dot_nt(a, b):
    return lax.dot_general(a, b, (((1,), (1,)), ((), ())), preferred_element_type=F32)


def _dot_tn(a, b):
    return lax.dot_general(a, b, (((0,), (0,)), ((), ())), preferred_element_type=F32)


def _sigmoid(z):
    return 1.0 / (1.0 + jnp.exp(-z))


def norm_matmul(x, g, w, *, tm, tn, split, name):
    T, K = x.shape
    blocked = w.ndim == 3
    assert not blocked or w.shape[2] == tn
    N = w.shape[0] * w.shape[2] if blocked else w.shape[1]
    nj = N // tn

    def body(x_ref, g_ref, w_ref, o_ref, h_ref):
        @pl.when(pl.program_id(1) == 0)
        def _():
            xv = x_ref[...]
            r = lax.rsqrt(jnp.mean(xv * xv, axis=-1, keepdims=True) + RMS_EPS)
            h_ref[...] = (xv * r * g_ref[...]).astype(BF16)

        o_ref[...] = _dot(h_ref[...], w_ref[...]).astype(o_ref.dtype)

    if split:
        njh = nj // 2
        o_shape = jax.ShapeDtypeStruct((2, T, N // 2), BF16)
        o_spec = pl.BlockSpec((None, tm, tn), lambda i, j: (j // njh, i, j % njh))
    else:
        o_shape = jax.ShapeDtypeStruct((T, N), F32)
        o_spec = pl.BlockSpec((tm, tn), lambda i, j: (i, j))
    return pl.pallas_call(
        body, grid=(T // tm, nj),
        in_specs=[pl.BlockSpec((tm, K), lambda i, j: (i, 0)),
                  pl.BlockSpec((1, K), lambda i, j: (0, 0)),
                  (pl.BlockSpec((None, K, tn), lambda i, j: (j, 0, 0)) if blocked
                   else pl.BlockSpec((K, tn), lambda i, j: (0, j)))],
        out_specs=[o_spec, pl.BlockSpec((tm, K), lambda i, j: (i, 0))],
        out_shape=[o_shape, jax.ShapeDtypeStruct((T, K), BF16)],
        compiler_params=_cparams(2), name=name)(x, g, w)


def mm_nn(a, b, *, res, tm, tn, tk, name):
    T = a.shape[0]
    K, N = b.shape
    nk = K // tk

    def body(a_ref, b_ref, r_ref, o_ref, acc):
        k = pl.program_id(2)

        @pl.when(k == 0)
        def _():
            acc[...] = jnp.zeros_like(acc)

        acc[...] += _dot(a_ref[...].astype(BF16), b_ref[...])

        @pl.when(k == nk - 1)
        def _():
            o_ref[...] = r_ref[...] + acc[...]

    return pl.pallas_call(
        body, grid=(T // tm, N // tn, nk),
        in_specs=[pl.BlockSpec((tm, tk), lambda i, j, k: (i, k)), pl.BlockSpec((tk, tn), lambda i, j, k: (k, j)),
                  pl.BlockSpec((tm, tn), lambda i, j, k: (i, j))],
        out_specs=pl.BlockSpec((tm, tn), lambda i, j, k: (i, j)),
        out_shape=jax.ShapeDtypeStruct((T, N), F32),
        scratch_shapes=[pltpu.VMEM((tm, tn), F32)],
        compiler_params=_cparams(3), name=name)(a, b, res)


def mm_nt(a, b, *, tm, tn, tk, name):
    T, K = a.shape
    N = b.shape[0]
    nk = K // tk

    def body(a_ref, b_ref, o_ref, acc):
        k = pl.program_id(2)

        @pl.when(k == 0)
        def _():
            acc[...] = jnp.zeros_like(acc)

        acc[...] += _dot_nt(a_ref[...].astype(BF16), b_ref[...])

        @pl.when(k == nk - 1)
        def _():
            o_ref[...] = acc[...]

    return pl.pallas_call(
        body, grid=(T // tm, N // tn, nk),
        in_specs=[pl.BlockSpec((tm, tk), lambda i, j, k: (i, k)),
                  pl.BlockSpec((tn, tk), lambda i, j, k: (j, k))],
        out_specs=pl.BlockSpec((tm, tn), lambda i, j, k: (i, j)),
        out_shape=jax.ShapeDtypeStruct((T, N), F32),
        scratch_shapes=[pltpu.VMEM((tm, tn), F32)],
        compiler_params=_cparams(3), name=name)(a, b)


def ffn_bwd_act(d, wd, gu, *, tm, tn, name, riding=None):
    T, K = d.shape
    Fd = wd.shape[0]
    ni = T // tm

    def body(d_ref, w_ref, g_ref, u_ref, dgu_ref, dwd_ref, acc):
        i = pl.program_id(1)

        @pl.when(i == 0)
        def _():
            acc[...] = jnp.zeros_like(acc)

        db = d_ref[...].astype(BF16)
        da = 0.5 * _dot_nt(db, w_ref[...])
        gv = g_ref[...].astype(F32)
        uv = u_ref[...].astype(F32)
        s = _sigmoid(gv)
        silu = gv * s
        acc[...] += _dot_tn((silu * uv).astype(BF16), db)
        dgu_ref[0] = (da * uv * (s * (1.0 + gv * (1.0 - s)))).astype(BF16)
        dgu_ref[1] = (da * silu).astype(BF16)

        @pl.when(i == ni - 1)
        def _():
            dwd_ref[...] = (0.5 * acc[...]).astype(BF16)

    in_specs = [pl.BlockSpec((tm, K), lambda j, i: (i, 0)),
                pl.BlockSpec((tn, K), lambda j, i: (j, 0)),
                pl.BlockSpec((None, tm, tn), lambda j, i: (0, i, j)),
                pl.BlockSpec((None, tm, tn), lambda j, i: (1, i, j))]
    out_specs = [pl.BlockSpec((2, tm, tn), lambda j, i: (0, i, j)), pl.BlockSpec((tn, K), lambda j, i: (j, 0))]
    out_shape = [jax.ShapeDtypeStruct((2, T, Fd), BF16), jax.ShapeDtypeStruct((Fd, K), BF16)]
    return _call_with_riders(body, riding, (Fd // tn, ni), in_specs, out_specs, out_shape,
                             [pltpu.VMEM((tn, K), F32)], [d, wd, gu, gu], name)


def _call_with_riders(body, riding, grid, in_specs, out_specs, out_shape, scratch, args, name):
    n_out = len(out_shape)
    if riding is None:
        return pl.pallas_call(body, grid=grid, in_specs=in_specs, out_specs=out_specs, out_shape=out_shape,
                              scratch_shapes=scratch, compiler_params=_cparams(len(grid)), name=name)(*args)

    def is_first():
        ok = pl.program_id(0) == 0
        for ax in range(1, len(grid)):
            ok = ok & (pl.program_id(ax) == 0)
        return ok

    def is_last():
        ok = pl.program_id(0) == grid[0] - 1
        for ax in range(1, len(grid)):
            ok = ok & (pl.program_id(ax) == grid[ax] - 1)
        return ok

    outs = pl.pallas_call(
        riding.wrap(body, len(in_specs), n_out, len(scratch), is_first, is_last), grid=grid,
        in_specs=list(in_specs) + riding.in_specs, out_specs=list(out_specs) + riding.out_specs,
        out_shape=list(out_shape) + riding.out_shapes, scratch_shapes=list(scratch) + riding.scratch,
        compiler_params=_cparams(len(grid)), name=name)(*args, *riding.args)
    core, per = riding.split(outs, n_out)
    return (*core, *per)


def mm_nt_normbwd(a, b, x, g, res, *, a_split, tm, tk, name, riding=None):
    T, Dm = x.shape
    blocked = b.ndim == 3
    assert not blocked or b.shape[2] == tk
    K = b.shape[0] * b.shape[2] if blocked else b.shape[1]
    nk = K // tk
    nkh = nk // 2
    has_res = res is not None

    def body(*refs):
        if has_res:
            a_ref, b_ref, x_ref, g_ref, r_ref, dx_ref, dg_ref, acc = refs
        else:
            a_ref, b_ref, x_ref, g_ref, dx_ref, dg_ref, acc = refs
        i = pl.program_id(0)
        k = pl.program_id(1)

        @pl.when(k == 0)
        def _():
            acc[...] = jnp.zeros_like(acc)

        acc[...] += _dot_nt(a_ref[...].astype(BF16), b_ref[...])

        @pl.when(k == nk - 1)
        def _():
            xv = x_ref[...]
            r = lax.rsqrt(jnp.mean(xv * xv, axis=-1, keepdims=True) + RMS_EPS)
            xh = xv * r
            dh = acc[...]
            dxh = dh * g_ref[...]
            dx = r * (dxh - xh * jnp.mean(dxh * xh, axis=-1, keepdims=True))
            if has_res:
                dx = dx + r_ref[...]
            dx_ref[...] = dx
            part = jnp.sum(dh * xh, axis=0, keepdims=True)

            @pl.when(i == 0)
            def _():
                dg_ref[...] = part

            @pl.when(i > 0)
            def _():
                dg_ref[...] += part

    if a_split:
        a_spec = pl.BlockSpec((None, tm, tk), lambda i, k: (k // nkh, i, k % nkh))
    else:
        a_spec = pl.BlockSpec((tm, tk), lambda i, k: (i, k))
    in_specs = [a_spec,
                (pl.BlockSpec((None, Dm, tk), lambda i, k: (k, 0, 0)) if blocked
                 else pl.BlockSpec((Dm, tk), lambda i, k: (0, k))),
                pl.BlockSpec((tm, Dm), lambda i, k: (i, 0)),
                pl.BlockSpec((1, Dm), lambda i, k: (0, 0))]
    args = [a, b, x, g]
    if has_res:
        in_specs.append(pl.BlockSpec((tm, Dm), lambda i, k: (i, 0)))
        args.append(res)
    out_specs = [pl.BlockSpec((tm, Dm), lambda i, k: (i, 0)), pl.BlockSpec((1, Dm), lambda i, k: (0, 0))]
    out_shape = [jax.ShapeDtypeStruct((T, Dm), F32), jax.ShapeDtypeStruct((1, Dm), F32)]
    scratch = [pltpu.VMEM((tm, Dm), F32)]
    return _call_with_riders(body, riding, (T // tm, nk), in_specs, out_specs, out_shape, scratch, args, name)


def mm_tn(a, b, *, scale, a_split, b_split, tm, tn, tk, name, out_blocked=False, riding=None):
    T = a.shape[-2]
    M = a.shape[-1] * (2 if a_split else 1)
    N = b.shape[-1] * (2 if b_split else 1)
    ni, nj, nk = M // tm, N // tn, T // tk
    nih, njh = ni // 2, nj // 2

    def body(a_ref, b_ref, o_ref, acc):
        k = pl.program_id(2)

        @pl.when(k == 0)
        def _():
            acc[...] = jnp.zeros_like(acc)

        acc[...] += _dot_tn(a_ref[...].astype(BF16), b_ref[...].astype(BF16))

        @pl.when(k == nk - 1)
        def _():
            o_ref[...] = (acc[...] * scale).astype(o_ref.dtype)

    if a_split:
        a_spec = pl.BlockSpec((None, tk, tm), lambda i, j, k: (i // nih, k, i % nih))
    else:
        a_spec = pl.BlockSpec((tk, tm), lambda i, j, k: (k, i))
    if b_split:
        b_spec = pl.BlockSpec((None, tk, tn), lambda i, j, k: (j // njh, k, j % njh))
    else:
        b_spec = pl.BlockSpec((tk, tn), lambda i, j, k: (k, j))
    if out_blocked:
        o_spec = pl.BlockSpec((None, None, tm, tn), lambda i, j, k: (j, i, 0, 0))
        o_shape = jax.ShapeDtypeStruct((nj, ni, tm, tn), BF16)
    else:
        o_spec = pl.BlockSpec((tm, tn), lambda i, j, k: (i, j))
        o_shape = jax.ShapeDtypeStruct((M, N), BF16)
    outs = _call_with_riders(body, riding, (ni, nj, nk), [a_spec, b_spec], [o_spec], [o_shape],
                             [pltpu.VMEM((tm, tn), F32)], [a, b], name)
    return outs[0] if riding is None else tuple(outs)


def loss_kernel(y, target, *, tm, name):
    T, Dm = y.shape

    def body(y_ref, t_ref, l_ref, dy_ref):
        e = y_ref[...] - t_ref[...]
        dy_ref[...] = e * (1.0 / Dm)
        part = (0.5 / Dm) * jnp.sum(jnp.sum(e * e, axis=-1, keepdims=True), axis=0, keepdims=True)
        part = jnp.broadcast_to(part, (8, LANES))

        @pl.when(pl.program_id(0) == 0)
        def _():
            l_ref[...] = part

        @pl.when(pl.program_id(0) > 0)
        def _():
            l_ref[...] += part

    return pl.pallas_call(
        body, grid=(T // tm,),
        in_specs=[pl.BlockSpec((tm, Dm), lambda i: (i, 0)), pl.BlockSpec((tm, Dm), lambda i: (i, 0))],
        out_specs=[pl.BlockSpec((8, LANES), lambda i: (0, 0)), pl.BlockSpec((tm, Dm), lambda i: (i, 0))],
        out_shape=[jax.ShapeDtypeStruct((8, LANES), F32), jax.ShapeDtypeStruct((T, Dm), F32)],
        compiler_params=_cparams(1), name=name)(y, target)


def adamw(w, g, m, v, *, br, name):
    R, C = w.shape

    def body(w_ref, g_ref, m_ref, v_ref, d_ref, nm_ref, nv_ref):
        gv = g_ref[...]
        nm = ADAM_B1 * m_ref[...] + (1.0 - ADAM_B1) * gv
        nv = ADAM_B2 * v_ref[...] + (1.0 - ADAM_B2) * (gv * gv)
        m_hat = nm / (1.0 - ADAM_B1 ** ADAM_STEP)
        v_hat = nv / (1.0 - ADAM_B2 ** ADAM_STEP)
        d_ref[...] = -ADAM_LR * (m_hat / (jnp.sqrt(v_hat) + ADAM_EPS) + ADAM_WD * w_ref[...])
        nm_ref[...] = nm
        nv_ref[...] = nv

    spec = pl.BlockSpec((br, C), lambda i: (i, 0))
    shp = jax.ShapeDtypeStruct((R, C), F32)
    return pl.pallas_call(
        body, grid=(R // br,), in_specs=[spec] * 4, out_specs=[spec] * 3, out_shape=[shp] * 3,
        compiler_params=_cparams(1), name=name)(w, g, m, v)


def _lane0():
    return lax.broadcasted_iota(jnp.int32, (1, LANES), 1) < HEAD_DIM


def _half_sum(x, m0):
    s0 = jnp.sum(jnp.where(m0, x, 0.0), axis=-1, keepdims=True)
    s1 = jnp.sum(jnp.where(m0, 0.0, x), axis=-1, keepdims=True)
    return jnp.where(m0, s0, s1)


def _head_rms(x, m0):
    return lax.rsqrt(_half_sum(x * x, m0) * (1.0 / HEAD_DIM) + RMS_EPS)


def _alibi(n):
    return [float(2.0 ** (-8.0 * (h + 1) / n)) for h in range(n)]


def _mask_half(x, m0, e):
    return jnp.where(m0, x, 0.0) if e == 0 else jnp.where(m0, 0.0, x)


def _band_masks2(max_dist, has_prev, live):
    row = lax.broadcasted_iota(jnp.int32, (2 * BLK, 2 * BLK), 0)
    col = lax.broadcasted_iota(jnp.int32, (2 * BLK, 2 * BLK), 1)
    dist = (row & (BLK - 1)) - col + BLK
    lim = jnp.where(live, max_dist, -1)
    first = jnp.where(has_prev, 0, BLK)
    valid = (dist >= 0) & (dist <= lim) & (col >= first)
    top = lax.broadcasted_iota(jnp.int32, (2 * BLK, 1), 0) < BLK
    return dist.astype(F32), valid, top


def _stack_heads(x, m0, kes):
    parts = []
    for e in range(2):
        h = _mask_half(x, m0, e)
        parts.append(pltpu.roll(h, HEAD_DIM, 1) if kes[e] != e else h)
    return jnp.concatenate(parts, axis=0)


def _unstack_heads(y, m0, kes):
    parts = []
    for e in range(2):
        h = y[e * BLK:(e + 1) * BLK]
        parts.append(pltpu.roll(h, HEAD_DIM, 1) if kes[e] != e else h)
    return jnp.where(m0, parts[0], parts[1])


def _rows(r, dil):
    return pl.ds(r, BLK, stride=dil) if dil > 1 else pl.ds(0, BLK)


def _band_units(dil, nsub):
    assert dil == 1 or nsub == 1
    if nsub == 1:
        return [(_rows(r, dil), ("prev", _rows(r, dil)), 0) for r in range(dil)]
    units = [(pl.ds(0, BLK), ("prev", pl.ds(0, BLK)), 0)]
    units += [(pl.ds(BLK * s, BLK), ("cur", pl.ds(BLK * (s - 1), BLK)), s) for s in range(1, nsub)]
    return units


def _head_col_spec(ppk, RB, row_block):
    if ppk == 1:
        return pl.BlockSpec((None, RB, 1), lambda p, i: (p, row_block(i), 0))
    return pl.BlockSpec((ppk, RB, 1), lambda p, i: (p, row_block(i), 0))


def _band_specs(dil, nsub, ppk, q_blk, k_blk, v_blk, kv_shared, nb):
    RB = BLK * dil * nsub
    PB = BLK if nsub > 1 else RB
    qw = LANES * ppk
    kw = LANES if kv_shared else qw

    def cur(i):
        return jnp.minimum(i, nb - 1)

    def prev(i):
        return jnp.maximum(i * nsub - 1, 0) if nsub > 1 else jnp.maximum(i - 1, 0)

    def kidx(base):
        return (lambda p, i: (cur(i), base)) if kv_shared else (lambda p, i: (cur(i), base + p))

    def pidx(base):
        return (lambda p, i: (prev(i), base)) if kv_shared else (lambda p, i: (prev(i), base + p))

    specs = [pl.BlockSpec((RB, qw), lambda p, i: (cur(i), q_blk + p)),
             pl.BlockSpec((RB, kw), kidx(k_blk)), pl.BlockSpec((PB, kw), pidx(k_blk)),
             pl.BlockSpec((RB, kw), kidx(v_blk)), pl.BlockSpec((PB, kw), pidx(v_blk))]
    return specs if dil == 1 else [specs[0], specs[1], specs[3]]


def qk_norm(qkv, q_gain2, k_gain2, *, width, steps, n_q, tm, name):
    T = qkv.shape[0]
    nsb = width // LANES

    def body(x_ref, qg_ref, kg_ref, o_ref):
        m0 = _lane0()
        for b in range(nsb):
            is_q = ((pl.program_id(1) * nsb + b) < n_q).astype(F32)
            gain = qg_ref[...] * is_q + kg_ref[...] * (1.0 - is_q)
            cols = pl.ds(LANES * b, LANES)
            xv = x_ref[:, cols]
            o_ref[:, cols] = xv * _head_rms(xv, m0) * gain

    gspec = pl.BlockSpec((1, LANES), lambda i, j: (0, 0))
    return pl.pallas_call(
        body, grid=(T // tm, steps),
        in_specs=[pl.BlockSpec((tm, width), lambda i, j: (i, j)), gspec, gspec],
        out_specs=pl.BlockSpec((tm, width), lambda i, j: (i, j)),
        out_shape=jax.ShapeDtypeStruct((T, width * steps), F32),
        compiler_params=_cparams(2), name=name)(qkv, q_gain2, k_gain2)


def banded_fwd(qkn, qkv, slopes, sinks, *, dil, nsub, ppk, q_blk, k_blk, v_blk, n_heads, group,
               max_dist, name):
    T = qkv.shape[0]
    RB = BLK * dil * nsub
    nb = T // RB
    npair = n_heads // 2
    kv_shared = group > 1
    scale = HEAD_DIM ** -0.5
    has_sink = sinks is not None

    def body(*refs):
        slope_ref = refs[0]
        if has_sink:
            sink_ref, refs = refs[1], refs[2:]
        else:
            refs = refs[1:]
        if dil == 1:
            q_ref, kc_ref, kp_ref, vc_ref, vp_ref, o_ref, l_ref, lc0_ref, lc1_ref = refs
        else:
            q_ref, kc_ref, vc_ref, o_ref, l_ref, lc0_ref, lc1_ref, kp_ref, vp_ref = refs
        pb = pl.program_id(0)
        i = pl.program_id(1)
        if dil > 1:
            @pl.when(i == 0)
            def _():
                kp_ref[...] = jnp.zeros_like(kp_ref)
                vp_ref[...] = jnp.zeros_like(vp_ref)
        m0 = _lane0()
        distf, valid_first, top = _band_masks2(max_dist, i > 0, i >= 0)
        valid_inner = _band_masks2(max_dist, i >= 0, i >= 0)[1] if nsub > 1 else None
        for u, (rows, (src, prows), sub) in enumerate(_band_units(dil, nsub)):
            valid = valid_first if sub == 0 else valid_inner
            kpr, vpr = (kp_ref, vp_ref) if src == "prev" else (kc_ref, vc_ref)
            kcache = {}
            for jp in range(ppk):
                cs = pl.ds(LANES * jp, LANES)
                jk = 0 if kv_shared else jp
                if jk not in kcache:
                    ks = pl.ds(LANES * jk, LANES)
                    kcur, vcur = kc_ref[rows, ks], vc_ref[rows, ks]
                    if dil == 1:
                        kprev, vprev = kpr[prows, ks], vpr[prows, ks]
                    else:
                        kprev, vprev = kp_ref[u, :, ks], vp_ref[u, :, ks]
                        kp_ref[u, :, ks] = kcur
                        vp_ref[u, :, ks] = vcur
                    kcat = jnp.concatenate([kprev, kcur], axis=0)
                    vcat = jnp.concatenate([vprev, vcur], axis=0)
                    kcache[jk] = (kcat.astype(BF16), vcat.astype(BF16))
                kn, vcat = kcache[jk]
                qn = q_ref[rows, cs]
                kes = [((2 * jp + e) // group) % 2 if kv_shared else e for e in range(2)]
                hidx = 2 * (pb * ppk + jp)
                qs = _stack_heads(qn, m0, kes).astype(BF16)
                slope = jnp.where(top, slope_ref[hidx], slope_ref[hidx + 1])
                s = jnp.where(valid, _dot_nt(qs, kn) * scale - slope * distf, NEG)
                m = jnp.max(s, axis=-1, keepdims=True)
                if has_sink:
                    sk = jnp.where(top, sink_ref[hidx], sink_ref[hidx + 1])
                    m = jnp.maximum(m, sk)
                p = jnp.exp(s - m)
                den = jnp.sum(p, axis=-1, keepdims=True)
                if has_sink:
                    den = den + jnp.exp(sk - m)
                o_full = _dot((p * (1.0 / den)).astype(BF16), vcat)
                o_ref[rows, cs] = _unstack_heads(o_full, m0, kes)
                lse = m + jnp.log(den)
                l_ref[rows, cs] = _unstack_heads(jnp.broadcast_to(lse, (2 * BLK, LANES)), m0, [0, 1])
                for e, lc_ref in enumerate((lc0_ref, lc1_ref)):
                    if ppk == 1:
                        lc_ref[rows, :] = lse[e * BLK:(e + 1) * BLK]
                    else:
                        lc_ref[jp, rows, :] = lse[e * BLK:(e + 1) * BLK]

    smem = pl.BlockSpec(memory_space=pltpu.SMEM)
    qw = LANES * ppk
    ospec = pl.BlockSpec((RB, qw), lambda p, i: (i, p))
    oshape = jax.ShapeDtypeStruct((T, n_heads * HEAD_DIM), F32)
    args = [slopes] + ([sinks] if has_sink else []) + ([qkn] * 3 + [qkv] * 2 if dil == 1 else [qkn, qkn, qkv])
    kw = LANES if kv_shared else qw
    prev_scratch = [] if dil == 1 else [pltpu.VMEM((dil, BLK, kw), F32)] * 2
    return pl.pallas_call(
        body, grid=(npair // ppk, nb),
        in_specs=[smem] * (2 if has_sink else 1) + _band_specs(dil, nsub, ppk, q_blk, k_blk, v_blk, kv_shared, nb),
        out_specs=[ospec, ospec] + [_head_col_spec(ppk, RB, lambda i: i)] * 2,
        out_shape=[oshape, oshape] + [jax.ShapeDtypeStruct((npair, T, 1), F32)] * 2,
        scratch_shapes=prev_scratch, compiler_params=_cparams(2), name=name)(*args)


def banded_bwd(qkn, qkv, slopes, sinks, do, o, lsec, w, omix, *, dil, nsub, ppk, q_blk, k_blk, v_blk,
               n_heads, group, max_dist, do_blk, name):
    T = qkv.shape[0]
    RB = BLK * dil * nsub
    nb = T // RB
    npair = n_heads // 2
    kv_shared = group > 1
    scale = HEAD_DIM ** -0.5
    has_sink = sinks is not None
    mixed = w is not None
    qw = LANES * ppk

    def body(*refs):
        slope_ref = refs[0]
        if has_sink:
            sink_ref, refs = refs[1], refs[2:]
        else:
            refs = refs[1:]
        if dil == 1:
            q_ref, kc_ref, kp_ref, vc_ref, vp_ref, do_ref, lc0_ref, lc1_ref = refs[:8]
            refs = refs[8:]
        else:
            q_ref, kc_ref, vc_ref, do_ref, lc0_ref, lc1_ref = refs[:6]
            refs, kp_ref, vp_ref = refs[6:-2], refs[-2], refs[-1]
        if mixed:
            w_ref, om_ref, refs = refs[0], refs[1], refs[2:]
        else:
            o_ref, refs = refs[0], refs[1:]
        dq_ref, dk_ref, dv_ref, dsk_ref, ck_ref, cv_ref = refs
        pb = pl.program_id(0)
        i = pl.program_id(1)
        live = i < nb
        m0 = _lane0()
        lane = lax.broadcasted_iota(jnp.int32, (1, LANES), 1)
        distf, valid_first, top = _band_masks2(max_dist, i > 0, live)
        valid_inner = _band_masks2(max_dist, i >= 0, live)[1] if nsub > 1 else None
        livef = live.astype(F32)

        def half_rows(x):
            s0 = jnp.sum(jnp.where(m0, x, 0.0), axis=-1, keepdims=True)
            s1 = jnp.sum(jnp.where(m0, 0.0, x), axis=-1, keepdims=True)
            return jnp.concatenate([s0, s1], axis=0)

        @pl.when((pb == 0) & (i == 0))
        def _():
            dsk_ref[...] = jnp.zeros_like(dsk_ref)

        @pl.when(i == 0)
        def _():
            ck_ref[...] = jnp.zeros_like(ck_ref)
            cv_ref[...] = jnp.zeros_like(cv_ref)

        dsk_acc = jnp.zeros((1, LANES), F32)
        if nsub > 1:
            dk_ref[...] = ck_ref[...]
            dv_ref[...] = cv_ref[...]
        if dil > 1:
            @pl.when(i == 0)
            def _():
                kp_ref[...] = jnp.zeros_like(kp_ref)
                vp_ref[...] = jnp.zeros_like(vp_ref)

        for u, (rows, (src, prows), sub) in enumerate(_band_units(dil, nsub)):
            valid = valid_first if sub == 0 else valid_inner
            kpr, vpr = (kp_ref, vp_ref) if src == "prev" else (kc_ref, vc_ref)
            ck_u, cv_u = (ck_ref.at[u], cv_ref.at[u]) if dil > 1 else (None, None)
            for jp in range(ppk):
                cs = pl.ds(LANES * jp, LANES)
                ks = pl.ds(0, LANES) if kv_shared else cs
                kcur, vcur = kc_ref[rows, ks], vc_ref[rows, ks]
                if dil == 1:
                    kprev, vprev = kpr[prows, ks], vpr[prows, ks]
                else:
                    kprev, vprev = kp_ref[u, :, ks], vp_ref[u, :, ks]
                    kp_ref[u, :, ks] = kcur
                    vp_ref[u, :, ks] = vcur
                kn = jnp.concatenate([kprev, kcur], axis=0).astype(BF16)
                vcat = jnp.concatenate([vprev, vcur], axis=0).astype(BF16)
                dov = do_ref[rows, cs]
                if mixed:
                    dov = dov * w_ref[rows, cs]
                    shift = half_rows(dov * om_ref[rows, cs])
                else:
                    shift = half_rows(dov * o_ref[rows, cs])
                kes = [((2 * jp + e) // group) % 2 if kv_shared else e for e in range(2)]
                hidx = 2 * (pb * ppk + jp)
                qs = _stack_heads(q_ref[rows, cs], m0, kes).astype(BF16)
                dos = _stack_heads(dov, m0, kes).astype(BF16)
                lse = jnp.concatenate([ref[rows, :] if ppk == 1 else ref[jp, rows, :]
                                       for ref in (lc0_ref, lc1_ref)], axis=0)
                slope = jnp.where(top, slope_ref[hidx], slope_ref[hidx + 1])
                p = jnp.where(valid, jnp.exp(_dot_nt(qs, kn) * scale - slope * distf - lse), 0.0)
                ds = (p * (_dot_nt(dos, vcat) - shift)).astype(BF16)
                dqn = _unstack_heads(_dot(ds, kn), m0, kes) * scale
                dkn = _dot_tn(ds, qs) * scale
                dvv = _dot_tn(p.astype(BF16), dos)
                if has_sink:
                    sk = jnp.where(top, sink_ref[hidx], sink_ref[hidx + 1])
                    contrib = -jnp.exp(sk - lse) * shift * livef
                    for e in range(2):
                        tot = jnp.sum(contrib[e * BLK:(e + 1) * BLK], axis=0, keepdims=True)
                        dsk_acc = dsk_acc + jnp.where(lane == (2 * jp + e), tot, 0.0)
                dk_raw = dkn

                @pl.when(live)
                def _():
                    dq_ref[rows, cs] = dqn

                if dil > 1:
                    dk_ref[rows, cs] = ck_u[:, cs] + dk_raw[:BLK]
                    dv_ref[rows, cs] = cv_u[:, cs] + dvv[:BLK]
                    ck_u[:, cs] = dk_raw[BLK:]
                    cv_u[:, cs] = dvv[BLK:]
                    continue
                if nsub == 1:
                    dk_ref[rows, cs] = ck_ref[rows, cs] + dk_raw[:BLK]
                    dv_ref[rows, cs] = cv_ref[rows, cs] + dvv[:BLK]
                elif sub == 0:
                    last = pl.ds(RB - BLK, BLK)
                    dk_ref[last, cs] += dk_raw[:BLK]
                    dv_ref[last, cs] += dvv[:BLK]
                else:
                    ck_ref[prows, cs] += dk_raw[:BLK]
                    cv_ref[prows, cs] += dvv[:BLK]
                ck_ref[rows, cs] = dk_raw[BLK:]
                cv_ref[rows, cs] = dvv[BLK:]
        dsk_ref[...] += dsk_acc

    smem = pl.BlockSpec(memory_space=pltpu.SMEM)
    gspec = pl.BlockSpec((1, LANES), lambda p, i: (0, 0))

    def cur(i):
        return jnp.minimum(i, nb - 1)

    qspec = pl.BlockSpec((RB, qw), lambda p, i: (cur(i), p))
    dospec = pl.BlockSpec((RB, qw), lambda p, i: (cur(i), do_blk + p))
    kvout = pl.BlockSpec((RB, qw), lambda p, i: (jnp.maximum(i - 1, 0), p))
    in_specs = ([smem] * (2 if has_sink else 1) + _band_specs(dil, nsub, ppk, q_blk, k_blk, v_blk, kv_shared, nb)
                + [dospec] + [_head_col_spec(ppk, RB, cur)] * 2
                + ([qspec, qspec] if mixed else [qspec]))
    args = ([slopes] + ([sinks] if has_sink else []) + ([qkn] * 3 + [qkv] * 2 if dil == 1 else [qkn, qkn, qkv])
            + [do, lsec[0], lsec[1]]
            + ([w, omix] if mixed else [o]))
    full = jax.ShapeDtypeStruct((T, n_heads * HEAD_DIM), F32)
    row = jax.ShapeDtypeStruct((1, LANES), F32)
    return pl.pallas_call(
        body, grid=(npair // ppk, nb + 1), in_specs=in_specs,
        out_specs=[qspec, kvout, kvout, gspec],
        out_shape=[full, full, full, row],
        scratch_shapes=([pltpu.VMEM((RB, qw), F32)] * 2 if dil == 1
                        else [pltpu.VMEM((dil, BLK, qw), F32)] * 2 + [pltpu.VMEM((dil, BLK, qw), F32)] * 2),
        compiler_params=_cparams(2), name=name)(*args)


def mix_fwd(o1, o2, o3, l1, l2, l3, *, tm, name):
    T, C = o1.shape

    def body(o1r, o2r, o3r, l1r, l2r, l3r, o_ref, w1r, w2r, w3r):
        a, b, c = l1r[...], l2r[...], l3r[...]
        m = jnp.maximum(jnp.maximum(a, b), c)
        ea, eb, ec = jnp.exp(a - m), jnp.exp(b - m), jnp.exp(c - m)
        inv = 1.0 / (ea + eb + ec)
        wa, wb, wc = ea * inv, eb * inv, ec * inv
        o_ref[...] = wa * o1r[...] + wb * o2r[...] + wc * o3r[...]
        w1r[...] = wa
        w2r[...] = wb
        w3r[...] = wc

    spec = pl.BlockSpec((tm, C), lambda i: (i, 0))
    shp = jax.ShapeDtypeStruct((T, C), F32)
    return pl.pallas_call(body, grid=(T // tm,), in_specs=[spec] * 6, out_specs=[spec] * 4, out_shape=[shp] * 4,
                          compiler_params=_cparams(1), name=name)(o1, o2, o3, l1, l2, l3)


def _qk_norm_bwd(raw, dn, gain, m0):
    r = _head_rms(raw, m0)
    h = raw * r
    dh = dn * gain
    d_raw = r * (dh - h * (_half_sum(dh * h, m0) * (1.0 / HEAD_DIM)))
    return d_raw, jnp.sum(dn * h, axis=0, keepdims=True)


def _acc_rows(ref, val):
    @pl.when(pl.program_id(0) == 0)
    def _():
        ref[...] = val

    @pl.when(pl.program_id(0) > 0)
    def _():
        ref[...] += val


def assemble_odd(parts, qkv, q_gain2, k_gain2, *, tm, name):
    T, C = parts[0][0].shape
    nbk = C // LANES

    def body(*refs):
        qkv_ref, qg_ref, kg_ref, o_ref, dqg_ref, dkg_ref = refs[9:]
        m0 = _lane0()
        sums = [refs[j][...] + refs[3 + j][...] + refs[6 + j][...] for j in range(3)]
        o_ref[:, pl.ds(2 * C, C)] = sums[2].astype(o_ref.dtype)
        for j, (g_ref, acc_ref) in enumerate(((qg_ref, dqg_ref), (kg_ref, dkg_ref))):
            dgain = jnp.zeros((1, LANES), F32)
            for b in range(nbk):
                cols = pl.ds(C * j + LANES * b, LANES)
                d_raw, part = _qk_norm_bwd(qkv_ref[:, cols], sums[j][:, LANES * b:LANES * (b + 1)], g_ref[...], m0)
                o_ref[:, cols] = d_raw.astype(o_ref.dtype)
                dgain = dgain + part
            _acc_rows(acc_ref, dgain)

    spec = pl.BlockSpec((tm, C), lambda i: (i, 0))
    gspec = pl.BlockSpec((1, LANES), lambda i: (0, 0))
    flat = [parts[p][j] for p in range(3) for j in range(3)]
    row = jax.ShapeDtypeStruct((1, LANES), F32)
    return pl.pallas_call(body, grid=(T // tm,),
                          in_specs=[spec] * 9 + [pl.BlockSpec((tm, 2 * C), lambda i: (i, 0)), gspec, gspec],
                          out_specs=[pl.BlockSpec((tm, 3 * C), lambda i: (i, 0)), gspec, gspec],
                          out_shape=[jax.ShapeDtypeStruct((T, 3 * C), BF16), row, row],
                          compiler_params=_cparams(1), name=name)(*flat, qkv, q_gain2, k_gain2)


def assemble_even(dqa, dka4, dva4, dqb, dkb, dvb, qkv, q_gain2, k_gain2, *, tm, name):
    T = dqa.shape[0]
    W = 512
    QK = 768

    def body(dqa_r, dka_r, dva_r, dqb_r, dkb_r, dvb_r, qkv_ref, qg_ref, kg_ref, o_ref, dqg_ref, dkg_ref):
        m0 = _lane0()
        ka = dka_r[...]
        va = dva_r[...]
        dqn = dqa_r[...]
        dgain = jnp.zeros((1, LANES), F32)
        for b in range(W // LANES):
            cols = pl.ds(LANES * b, LANES)
            d_raw, part = _qk_norm_bwd(qkv_ref[:, cols], dqn[:, LANES * b:LANES * (b + 1)], qg_ref[...], m0)
            o_ref[:, cols] = d_raw.astype(o_ref.dtype)
            dgain = dgain + part
        _acc_rows(dqg_ref, dgain)
        dkn = ka[:, 0:128] + ka[:, 128:256] + ka[:, 256:384] + ka[:, 384:512]
        d_raw, part = _qk_norm_bwd(qkv_ref[:, pl.ds(W, LANES)], dkn, kg_ref[...], m0)
        dt = o_ref.dtype
        o_ref[:, pl.ds(W, LANES)] = d_raw.astype(dt)
        _acc_rows(dkg_ref, part)
        o_ref[:, pl.ds(640, LANES)] = (va[:, 0:128] + va[:, 128:256] + va[:, 256:384] + va[:, 384:512]).astype(dt)
        o_ref[:, pl.ds(768, W)] = dqb_r[...].astype(dt)
        o_ref[:, pl.ds(1280, W)] = dkb_r[...].astype(dt)
        o_ref[:, pl.ds(1792, W)] = dvb_r[...].astype(dt)

    spec = pl.BlockSpec((tm, W), lambda i: (i, 0))
    gspec = pl.BlockSpec((1, LANES), lambda i: (0, 0))
    row = jax.ShapeDtypeStruct((1, LANES), F32)
    return pl.pallas_call(body, grid=(T // tm,),
                          in_specs=[spec] * 6 + [pl.BlockSpec((tm, QK), lambda i: (i, 0)), gspec, gspec],
                          out_specs=[pl.BlockSpec((tm, 2304), lambda i: (i, 0)), gspec, gspec],
                          out_shape=[jax.ShapeDtypeStruct((T, 2304), BF16), row, row],
                          compiler_params=_cparams(1), name=name)(dqa, dka4, dva4, dqb, dkb, dvb, qkv, q_gain2, k_gain2)


STICK_T = 256
STICK_DEAD = -110.0


def _split_bf16(x):
    hi = x.astype(BF16)
    lo = (x - hi.astype(F32)).astype(BF16)
    return hi, lo


def _stick_logits(qm, kt, scale, diag):
    n = STICK_T
    row = lax.broadcasted_iota(jnp.int32, (n, n), 0)
    col = lax.broadcasted_iota(jnp.int32, (n, n), 1)
    mask = col < row + jnp.where(diag, 0, n)
    z = _dot_nt(qm, kt) * scale
    lneg = -(jnp.maximum(z, 0.0) + jnp.log(1.0 + jnp.exp(-jnp.abs(z))))
    lpos = z + lneg
    lk = jnp.where(mask, lneg, 0.0)
    return mask, lpos, lneg, lk


def _cumsum_mm(x, tri):
    hi, lo = _split_bf16(x)
    return _dot(hi, tri) + _dot(lo, tri)


def stick_fwd(qkv, *, q_blk, k_blk, v_blk, n_pairs, name, riders=None, rider_args=()):
    T = qkv.shape[0]
    n = STICK_T
    nq = T // n
    scale = HEAD_DIM ** -0.5
    nc = riders.n if riders is not None else 0
    n_steps = n_pairs * nq
    stage_at = (0, (5 * n_steps) // 8, (15 * n_steps) // 16, n_steps - 1)

    def body(*refs):
        q_ref, k_ref, v_ref = refs[:3]
        x_refs, o_ref = refs[3:3 + nc], refs[3 + nc]
        out_refs, sems = refs[4 + nc:4 + 2 * nc], refs[4 + 2 * nc:]
        i = pl.program_id(1)
        step_id = pl.program_id(0) * nq + i

        def ride(which):
            if riders is not None:
                @pl.when(step_id == stage_at[which])
                def _():
                    riders.stage(which, x_refs, out_refs, sems)

        ride(0)
        ride(1)
        m0 = _lane0()
        r2 = lax.broadcasted_iota(jnp.int32, (n, n), 0)
        c2 = lax.broadcasted_iota(jnp.int32, (n, n), 1)
        tri_after = (r2 > c2).astype(BF16)
        qv = q_ref[...]
        out = jnp.zeros((n, LANES), F32)
        for e in range(2):
            qm = _mask_half(qv, m0, e).astype(BF16)

            def alive(st):
                t, _, carry = st
                return (t <= i) & (jnp.max(carry) > STICK_DEAD)

            def step(st, e=e, qm=qm):
                t, acc, carry = st
                start = pl.multiple_of((i - t) * n, n)
                kt = k_ref[pl.ds(start, n), :].astype(BF16)
                vt = _mask_half(v_ref[pl.ds(start, n), :], m0, e).astype(BF16)
                mask, lpos, _, lk = _stick_logits(qm, kt, scale, t == 0)
                after = _cumsum_mm(lk, tri_after) + carry
                a = jnp.where(mask, jnp.exp(lpos + after), 0.0)
                acc = acc + _dot(a.astype(BF16), vt)
                carry = carry + jnp.sum(lk, axis=-1, keepdims=True)
                return t + 1, acc, carry

            _, acc, _ = lax.while_loop(alive, step, (jnp.int32(0), jnp.zeros((n, LANES), F32),
                                                     jnp.zeros((n, 1), F32)))
            out = out + acc
        o_ref[...] = out
        ride(2)
        ride(3)

    outs = pl.pallas_call(
        body, grid=(n_pairs, nq),
        in_specs=[pl.BlockSpec((n, LANES), lambda p, i: (i, q_blk + p)),
                  pl.BlockSpec((T, LANES), lambda p, i: (0, k_blk + p)),
                  pl.BlockSpec((T, LANES), lambda p, i: (0, v_blk + p))] + [_ANY] * nc,
        out_specs=[pl.BlockSpec((n, LANES), lambda p, i: (i, p))] + [_ANY] * nc,
        out_shape=[jax.ShapeDtypeStruct((T, n_pairs * LANES), F32)] + (riders.shapes if nc else []),
        scratch_shapes=riders.sems if nc else [],
        compiler_params=_cparams(2), name=name)(qkv, qkv, qkv, *rider_args)
    return outs[0], list(outs[1:])


def stick_bwd(qkv, do, *, q_blk, k_blk, v_blk, do_blk, n_pairs, name, riders=None, rider_args=()):
    T = qkv.shape[0]
    n = STICK_T
    nq = T // n
    scale = HEAD_DIM ** -0.5
    nc = riders.n if riders is not None else 0

    def body(*refs):
        q_ref, k_ref, v_ref, do_ref = refs[:4]
        pre_refs = refs[4:4 + nc]
        dq_ref, dk_ref, dv_ref = refs[4 + nc:7 + nc]
        land_refs = refs[7 + nc:7 + 2 * nc]
        a_keep, g_keep, s_keep = refs[7 + 2 * nc:10 + 2 * nc]
        sems = refs[10 + 2 * nc:]
        i = pl.program_id(1)
        first_step = (pl.program_id(0) == 0) & (i == 0)
        last_step = (pl.program_id(0) == n_pairs - 1) & (i == nq - 1)
        m0 = _lane0()
        r2 = lax.broadcasted_iota(jnp.int32, (n, n), 0)
        c2 = lax.broadcasted_iota(jnp.int32, (n, n), 1)
        tri_after = (r2 > c2).astype(BF16)
        tri_from = (r2 >= c2).astype(BF16)

        if riders is not None:
            @pl.when(first_step)
            def _():
                riders.start(pre_refs, land_refs, sems)

        @pl.when(i == 0)
        def _():
            dk_ref[...] = jnp.zeros_like(dk_ref)
            dv_ref[...] = jnp.zeros_like(dv_ref)

        qv = q_ref[...]
        dov = do_ref[...]
        dq_out = jnp.zeros((n, LANES), F32)
        for e in range(2):
            qm = _mask_half(qv, m0, e).astype(BF16)
            dom = _mask_half(dov, m0, e).astype(BF16)

            def alive(st):
                t, carry, _ = st
                return (t <= i) & (jnp.max(carry) > STICK_DEAD)

            def scan(st, qm=qm, dom=dom):
                t, carry, gtot = st
                start = pl.multiple_of((i - t) * n, n)
                kt = k_ref[pl.ds(start, n), :].astype(BF16)
                vt = v_ref[pl.ds(start, n), :].astype(BF16)
                mask, lpos, lneg, lk = _stick_logits(qm, kt, scale, t == 0)
                a = jnp.where(mask, jnp.exp(lpos + _cumsum_mm(lk, tri_after) + carry), 0.0)
                g = _dot_nt(dom, vt) * a
                a_keep[t] = a.astype(BF16)
                g_keep[t] = g
                s_keep[t] = jnp.exp(lneg).astype(BF16)
                return (t + 1, carry + jnp.sum(lk, axis=-1, keepdims=True),
                        gtot + jnp.sum(g, axis=-1, keepdims=True))

            z1 = jnp.zeros((n, 1), F32)
            n_live, _, gtot = lax.while_loop(alive, scan, (jnp.int32(0), z1, z1))

            def step(t, st, e=e, qm=qm, dom=dom, gtot=gtot):
                dq_acc, gright = st
                start = pl.multiple_of((i - t) * n, n)
                g = g_keep[t]
                sneg = s_keep[t].astype(F32)
                before = gtot - (_cumsum_mm(g, tri_from) + gright)
                mask = c2 < r2 + jnp.where(t == 0, 0, n)
                dz = jnp.where(mask, g * sneg - before * (1.0 - sneg), 0.0) * scale
                dzb = dz.astype(BF16)
                dq_acc = dq_acc + _dot(dzb, _mask_half(k_ref[pl.ds(start, n), :], m0, e).astype(BF16))
                dk_ref[pl.ds(start, n), :] += _dot_tn(dzb, qm)
                dv_ref[pl.ds(start, n), :] += _dot_tn(a_keep[t], dom)
                return dq_acc, gright + jnp.sum(g, axis=-1, keepdims=True)

            dq_acc, _ = lax.fori_loop(0, n_live, step, (jnp.zeros((n, LANES), F32), z1))
            dq_out = dq_out + dq_acc
        dq_ref[...] = dq_out

        if riders is not None:
            @pl.when(last_step)
            def _():
                riders.finish(pre_refs, land_refs, sems)

    tile = pl.BlockSpec((n, LANES), lambda p, i: (i, p))
    whole = pl.BlockSpec((T, LANES), lambda p, i: (0, p))
    shp = jax.ShapeDtypeStruct((T, n_pairs * LANES), F32)
    outs = pl.pallas_call(
        body, grid=(n_pairs, nq),
        in_specs=[pl.BlockSpec((n, LANES), lambda p, i: (i, q_blk + p)),
                  pl.BlockSpec((T, LANES), lambda p, i: (0, k_blk + p)),
                  pl.BlockSpec((T, LANES), lambda p, i: (0, v_blk + p)),
                  pl.BlockSpec((n, LANES), lambda p, i: (i, do_blk + p))] + [_ANY] * nc,
        out_specs=[tile, whole, whole] + [_ANY] * nc,
        out_shape=[shp, shp, shp] + (riders.shapes if nc else []),
        scratch_shapes=[pltpu.VMEM((nq, n, n), BF16), pltpu.VMEM((nq, n, n), F32), pltpu.VMEM((nq, n, n), BF16)]
        + (riders.sems if nc else []),
        compiler_params=_cparams(2), name=name)(qkv, qkv, qkv, do, *rider_args)
    return outs[0], outs[1], outs[2], list(outs[3:])


def _xnorm(x):
    r = lax.rsqrt(jnp.mean(x * x, axis=-1, keepdims=True) + RMS_EPS)
    return r, x * r


def xattn_fwd(qraw, kvraw, q_gain, k_gain, *, tm, name):
    T = qraw.shape[0]
    scale = X_HEAD_DIM ** -0.5
    W = X_HEADS * X_HEAD_DIM

    def body(q_ref, kv_ref, qg_ref, kg_ref, o_ref):
        for h in range(X_HEADS):
            cs = pl.ds(X_HEAD_DIM * h, X_HEAD_DIM)
            _, qh = _xnorm(q_ref[:, cs])
            _, kh = _xnorm(kv_ref[:, cs])
            qn = (qh * qg_ref[...]).astype(BF16)
            kn = (kh * kg_ref[...]).astype(BF16)
            v = kv_ref[:, pl.ds(W + X_HEAD_DIM * h, X_HEAD_DIM)].astype(BF16)
            s = _dot_nt(qn, kn) * scale
            m = jnp.max(s, axis=-1, keepdims=True)
            p = jnp.exp(s - m)
            p = p / jnp.sum(p, axis=-1, keepdims=True)
            o_ref[:, cs] = _dot(p.astype(BF16), v)

    gspec = pl.BlockSpec((1, X_HEAD_DIM), lambda i: (0, 0))
    return pl.pallas_call(
        body, grid=(T // tm,),
        in_specs=[pl.BlockSpec((tm, W), lambda i: (i, 0)), pl.BlockSpec((MEM_LEN, 2 * W), lambda i: (0, 0)),
                  gspec, gspec],
        out_specs=pl.BlockSpec((tm, W), lambda i: (i, 0)),
        out_shape=jax.ShapeDtypeStruct((T, W), F32),
        compiler_params=_cparams(1), name=name)(qraw, kvraw, q_gain, k_gain)


def xattn_bwd(qraw, kvraw, q_gain, k_gain, do, o, *, tm, name):
    T = qraw.shape[0]
    nt = T // tm
    scale = X_HEAD_DIM ** -0.5
    W = X_HEADS * X_HEAD_DIM

    def body(q_ref, kv_ref, qg_ref, kg_ref, do_ref, o_ref, dq_ref, dkv_ref, dqg_ref, dkg_ref, dkn_ref):
        i = pl.program_id(0)

        @pl.when(i == 0)
        def _():
            dkv_ref[...] = jnp.zeros_like(dkv_ref)
            dkn_ref[...] = jnp.zeros_like(dkn_ref)
            dqg_ref[...] = jnp.zeros_like(dqg_ref)
            dkg_ref[...] = jnp.zeros_like(dkg_ref)

        qg = qg_ref[...]
        kg = kg_ref[...]
        dqg_acc = jnp.zeros((1, X_HEAD_DIM), F32)
        for h in range(X_HEADS):
            cs = pl.ds(X_HEAD_DIM * h, X_HEAD_DIM)
            vs = pl.ds(W + X_HEAD_DIM * h, X_HEAD_DIM)
            rq, qh = _xnorm(q_ref[:, cs])
            _, kh = _xnorm(kv_ref[:, cs])
            qn = (qh * qg).astype(BF16)
            kn = (kh * kg).astype(BF16)
            v = kv_ref[:, vs].astype(BF16)
            s = _dot_nt(qn, kn) * scale
            m = jnp.max(s, axis=-1, keepdims=True)
            p = jnp.exp(s - m)
            p = p / jnp.sum(p, axis=-1, keepdims=True)
            dov = do_ref[:, cs]
            delta = jnp.sum(dov * o_ref[:, cs], axis=-1, keepdims=True)
            dob = dov.astype(BF16)
            ds = (p * (_dot_nt(dob, v) - delta)).astype(BF16)
            dqn = _dot(ds, kn) * scale
            dkn_ref[:, cs] += _dot_tn(ds, qn) * scale
            dkv_ref[:, vs] += _dot_tn(p.astype(BF16), dob)
            dqg_acc = dqg_acc + jnp.sum(dqn * qh, axis=0, keepdims=True)
            dqh = dqn * qg
            dq_ref[:, cs] = (rq * (dqh - qh * jnp.mean(dqh * qh, axis=-1, keepdims=True))).astype(dq_ref.dtype)
        dqg_ref[...] += dqg_acc

        @pl.when(i == nt - 1)
        def _():
            dkg_acc = jnp.zeros((1, X_HEAD_DIM), F32)
            for h in range(X_HEADS):
                cs = pl.ds(X_HEAD_DIM * h, X_HEAD_DIM)
                rk, kh = _xnorm(kv_ref[:, cs])
                dkn = dkn_ref[:, cs]
                dkg_acc = dkg_acc + jnp.sum(dkn * kh, axis=0, keepdims=True)
                dkh = dkn * kg
                dkv_ref[:, cs] = rk * (dkh - kh * jnp.mean(dkh * kh, axis=-1, keepdims=True))
            dkg_ref[...] = dkg_acc

    gspec = pl.BlockSpec((1, X_HEAD_DIM), lambda i: (0, 0))
    tile = pl.BlockSpec((tm, W), lambda i: (i, 0))
    kvspec = pl.BlockSpec((MEM_LEN, 2 * W), lambda i: (0, 0))
    grow = jax.ShapeDtypeStruct((1, X_HEAD_DIM), F32)
    return pl.pallas_call(
        body, grid=(nt,), in_specs=[tile, kvspec, gspec, gspec, tile, tile],
        out_specs=[tile, kvspec, gspec, gspec],
        out_shape=[jax.ShapeDtypeStruct((T, W), BF16), jax.ShapeDtypeStruct((MEM_LEN, 2 * W), F32), grow, grow],
        scratch_shapes=[pltpu.VMEM((MEM_LEN, W), F32)],
        compiler_params=_cparams(1), name=name)(qraw, kvraw, q_gain, k_gain, do, o)


_ANY = pl.BlockSpec(memory_space=pl.ANY)


def _my_pos():
    return lax.axis_index("x"), lax.axis_index("y"), lax.axis_index("c")


def _pieces(arrays, chunks):
    out = []
    for a, (arr, n) in enumerate(zip(arrays, chunks)):
        rc = arr.shape[-2] // n
        out += [(a, pl.ds(ch * rc, rc)) for ch in range(n)]
    return out


class GatherBlocks:
    N_STAGES = 4

    def __init__(self, blks, chunks):
        self.shapes = [jax.ShapeDtypeStruct((N_DEV,) + b.shape, b.dtype) for b in blks]
        self.n = len(blks)
        self.pieces = _pieces(blks, chunks)
        n_p = len(self.pieces)
        self.sems = [pltpu.SemaphoreType.DMA((7 * n_p,)), pltpu.SemaphoreType.DMA((7 * n_p,)),
                     pltpu.SemaphoreType.DMA((n_p,))]

    def stage(self, which, x_refs, out_refs, sems):
        send_sems, recv_sems, local_sems = sems
        pieces, n_p = self.pieces, len(self.pieces)
        x, y, c = _my_pos()
        me, sibling = (x, y, c), (x, y, 1 - c)
        chips = [(1 - x, y), (x, 1 - y), (1 - x, 1 - y)]
        xn, yn, dg = [(*chip, c) for chip in chips]
        ps = range(n_p)

        def slot(block, p):
            px, py, pc = block
            a, rows = pieces[p]
            return out_refs[a].at[4 * px + 2 * py + pc, rows]

        def own(p):
            a, rows = pieces[p]
            return x_refs[a].at[rows]

        def copy(k, p, block, to, from_input=False):
            return pltpu.make_async_remote_copy(
                src_ref=own(p) if from_input else slot(block, p), dst_ref=slot(block, p),
                send_sem=send_sems.at[k * n_p + p], recv_sem=recv_sems.at[k * n_p + p],
                device_id=to, device_id_type=MESH)

        mine = [pltpu.make_async_copy(own(p), slot(me, p), local_sems.at[p]) for p in ps]
        first = [copy(k, p, me, to, from_input=True) for p in ps for k, to in ((1, xn), (2, yn), (0, sibling))]
        on_x = [copy(3, p, xn, yn) for p in ps if p % 2 == 0] + [copy(4, p, xn, sibling) for p in ps]
        on_y = [copy(3, p, yn, xn) for p in ps if p % 2 == 1] + [copy(5, p, yn, sibling) for p in ps]
        on_d = [copy(6, p, dg, sibling) for p in ps]
        if which == 0:
            for cp in first + mine:
                cp.start()
        elif which == 1:
            for p in ps:
                copy(1, p, xn, me).wait_recv()
                if p % 2 == 0:
                    copy(3, p, xn, yn).start()
                copy(4, p, xn, sibling).start()
                copy(2, p, yn, me).wait_recv()
                if p % 2 == 1:
                    copy(3, p, yn, xn).start()
                copy(5, p, yn, sibling).start()
        elif which == 2:
            for p in ps:
                copy(3, p, dg, me).wait_recv()
                copy(6, p, dg, sibling).start()
        else:
            for p in ps:
                copy(0, p, sibling, me).wait_recv()
            for k, chip in zip((4, 5, 6), chips):
                for p in ps:
                    copy(k, p, (*chip, 1 - c), me).wait_recv()
            for cp in first + on_x + on_y + on_d:
                cp.wait_send()
            for cp in mine:
                cp.wait()


def gather_blocks(blks, chunks, *, name):
    gb = GatherBlocks(blks, chunks)
    n = gb.n

    def body(*refs):
        x_refs, out_refs, sems = refs[:n], refs[n:2 * n], refs[2 * n:]
        for which in range(gb.N_STAGES):
            gb.stage(which, x_refs, out_refs, sems)

    return pl.pallas_call(body, out_shape=gb.shapes, in_specs=[_ANY] * n, out_specs=[_ANY] * n,
                          scratch_shapes=gb.sems, name=name)(*blks)


def gather_small(small, *, name):
    S, C = small.shape

    def body(s_ref, out_ref, send_sems, recv_sems, local_sem):
        x, y, c = _my_pos()
        my_id = 4 * x + 2 * y + c

        def copy(k, slot):
            px, py, pc = x ^ ((k >> 2) & 1), y ^ ((k >> 1) & 1), c ^ (k & 1)
            dst = my_id if slot == "mine" else 4 * px + 2 * py + pc
            return pltpu.make_async_remote_copy(
                src_ref=s_ref, dst_ref=out_ref.at[dst], send_sem=send_sems.at[k - 1], recv_sem=recv_sems.at[k - 1],
                device_id=(px, py, pc), device_id_type=MESH)

        own = pltpu.make_async_copy(s_ref, out_ref.at[my_id], local_sem)
        own.start()
        sends = [copy(k, "mine") for k in range(1, N_DEV)]
        for cp in sends:
            cp.start()
        for k in range(1, N_DEV):
            copy(k, "theirs").wait_recv()
        for cp in sends:
            cp.wait_send()
        own.wait()

    dma7 = pltpu.SemaphoreType.DMA((7,))
    return pl.pallas_call(
        body, out_shape=jax.ShapeDtypeStruct((N_DEV, S, C), small.dtype), in_specs=[_ANY], out_specs=_ANY,
        scratch_shapes=[dma7, dma7, pltpu.SemaphoreType.DMA], name=name)(small)


class PairExchange:
    def __init__(self, bigs, chunks):
        self.shapes = [jax.ShapeDtypeStruct((4,) + b.shape[1:], b.dtype) for b in bigs]
        self.n = len(bigs)
        self.pieces = _pieces(bigs, chunks)
        n_p = len(self.pieces)
        self.sems = [pltpu.SemaphoreType.DMA((4 * n_p,)), pltpu.SemaphoreType.DMA((4 * n_p,))]

    def _copies(self, big_refs, out_refs, sems):
        send_sems, recv_sems = sems
        n_p = len(self.pieces)
        x, y, c = _my_pos()

        def copy(b, p):
            a, rows = self.pieces[p]
            return pltpu.make_async_remote_copy(
                src_ref=big_refs[a].at[2 * b + (1 - c), rows], dst_ref=out_refs[a].at[b, rows],
                send_sem=send_sems.at[b * n_p + p], recv_sem=recv_sems.at[b * n_p + p],
                device_id=(x, y, 1 - c), device_id_type=MESH)

        return [copy(b, p) for b in range(4) for p in range(n_p)]

    def start(self, big_refs, out_refs, sems):
        for cp in self._copies(big_refs, out_refs, sems):
            cp.start()

    def finish(self, big_refs, out_refs, sems):
        cps = self._copies(big_refs, out_refs, sems)
        for cp in cps:
            cp.wait_recv()
        for cp in cps:
            cp.wait_send()


def _standalone(exchange, args, name):
    n = exchange.n

    def body(*refs):
        exchange.start(refs[:n], refs[n:2 * n], refs[2 * n:])
        exchange.finish(refs[:n], refs[n:2 * n], refs[2 * n:])

    return pl.pallas_call(body, out_shape=exchange.shapes, in_specs=[_ANY] * n, out_specs=[_ANY] * n,
                          scratch_shapes=exchange.sems, name=name)(*args)


class Riding:
    def __init__(self, riders):
        self.riders = [(ex, list(args)) for ex, args in riders]
        self.args = [a for _, args in self.riders for a in args]
        self.in_specs = [_ANY] * len(self.args)
        self.out_shapes = [s for ex, _ in self.riders for s in ex.shapes]
        self.out_specs = [_ANY] * len(self.out_shapes)
        self.scratch = [s for ex, _ in self.riders for s in ex.sems]

    def wrap(self, body, n_in, n_out, n_scratch, is_first, is_last):
        def wrapped(*refs):
            k = 0
            core = list(refs[:n_in])
            k = n_in
            r_in = []
            for ex, _ in self.riders:
                r_in.append(refs[k:k + ex.n])
                k += ex.n
            core += refs[k:k + n_out]
            k += n_out
            r_out = []
            for ex, _ in self.riders:
                r_out.append(refs[k:k + ex.n])
                k += ex.n
            core += refs[k:k + n_scratch]
            k += n_scratch
            r_sem = []
            for ex, _ in self.riders:
                r_sem.append(refs[k:k + len(ex.sems)])
                k += len(ex.sems)

            @pl.when(is_first())
            def _():
                for (ex, _), a, b, s in zip(self.riders, r_in, r_out, r_sem):
                    ex.start(a, b, s)

            body(*core)

            @pl.when(is_last())
            def _():
                for (ex, _), a, b, s in zip(self.riders, r_in, r_out, r_sem):
                    ex.finish(a, b, s)

        return wrapped

    def split(self, outs, n_out):
        core, rest, per = list(outs[:n_out]), list(outs[n_out:]), []
        for ex, _ in self.riders:
            per.append(rest[:ex.n])
            rest = rest[ex.n:]
        return core, per


def pair_sum(big, sib, c, *, tr, name):
    _, R, C = big.shape

    def body(c_ref, a_ref, s_ref, o_ref):
        o_ref[...] = (a_ref[...].astype(F32) + s_ref[...].astype(F32)).astype(o_ref.dtype)

    grid_spec = pltpu.PrefetchScalarGridSpec(
        num_scalar_prefetch=1, grid=(4, R // tr),
        in_specs=[pl.BlockSpec((None, tr, C), lambda b, i, c_ref: (2 * b + c_ref[0], i, 0)),
                  pl.BlockSpec((None, tr, C), lambda b, i, c_ref: (b, i, 0))],
        out_specs=pl.BlockSpec((None, tr, C), lambda b, i, c_ref: (b, i, 0)))
    return pl.pallas_call(body, grid_spec=grid_spec, out_shape=jax.ShapeDtypeStruct((4, R, C), big.dtype),
                          compiler_params=_cparams(2), name=name)(c.reshape(1).astype(jnp.int32), big, sib)


class ChipScatter:
    def __init__(self, pres, chunks):
        self.shapes = [jax.ShapeDtypeStruct(p.shape, p.dtype) for p in pres]
        self.n = len(pres)
        self.pieces = _pieces(pres, chunks)
        n_p = len(self.pieces)
        self.sems = [pltpu.SemaphoreType.DMA((3 * n_p,)), pltpu.SemaphoreType.DMA((3 * n_p,)),
                     pltpu.SemaphoreType.DMA((n_p,))]

    def _copies(self, pre_refs, out_refs, sems):
        send_sems, recv_sems, local_sems = sems
        n_p = len(self.pieces)
        x, y, c = _my_pos()
        my_chip = 2 * x + y
        chips = [(1 - x, y), (x, 1 - y), (1 - x, 1 - y)]

        def copy(j, p, slot):
            px, py = chips[j]
            a, rows = self.pieces[p]
            src_slot, dst_slot = (2 * px + py, my_chip) if slot == "mine" else (my_chip, 2 * px + py)
            return pltpu.make_async_remote_copy(
                src_ref=pre_refs[a].at[src_slot, rows], dst_ref=out_refs[a].at[dst_slot, rows],
                send_sem=send_sems.at[j * n_p + p], recv_sem=recv_sems.at[j * n_p + p],
                device_id=(px, py, c), device_id_type=MESH)

        own = [pltpu.make_async_copy(pre_refs[a].at[my_chip, rows], out_refs[a].at[my_chip, rows], local_sems.at[p])
               for p, (a, rows) in enumerate(self.pieces)]
        sends = [copy(j, p, "mine") for j in range(3) for p in range(n_p)]
        recvs = [copy(j, p, "theirs") for j in range(3) for p in range(n_p)]
        return own, sends, recvs

    def start(self, pre_refs, out_refs, sems):
        own, sends, _ = self._copies(pre_refs, out_refs, sems)
        for cp in sends + own:
            cp.start()

    def finish(self, pre_refs, out_refs, sems):
        own, sends, recvs = self._copies(pre_refs, out_refs, sems)
        for cp in recvs:
            cp.wait_recv()
        for cp in sends:
            cp.wait_send()
        for cp in own:
            cp.wait()


def chip_scatter(pres, chunks, *, name):
    cs = ChipScatter(pres, chunks)
    n = cs.n

    def body(*refs):
        pre_refs, out_refs, sems = refs[:n], refs[n:2 * n], refs[2 * n:]
        cs.start(pre_refs, out_refs, sems)
        cs.finish(pre_refs, out_refs, sems)

    return pl.pallas_call(body, out_shape=cs.shapes, in_specs=[_ANY] * n, out_specs=[_ANY] * n,
                          scratch_shapes=cs.sems, name=name)(*pres)


def sibling_send(blks, chunks, *, name):
    n = len(blks)
    pieces = _pieces(blks, chunks)
    n_p = len(pieces)

    def body(*refs):
        x_refs, out_refs = refs[:n], refs[n:2 * n]
        send_sems, recv_sems = refs[2 * n:]
        x, y, c = _my_pos()
        cps = [pltpu.make_async_remote_copy(
            src_ref=x_refs[a].at[rows], dst_ref=out_refs[a].at[rows], send_sem=send_sems.at[p],
            recv_sem=recv_sems.at[p], device_id=(x, y, 1 - c), device_id_type=MESH)
            for p, (a, rows) in enumerate(pieces)]
        for cp in cps:
            cp.start()
        for cp in cps:
            cp.wait_recv()
        for cp in cps:
            cp.wait_send()

    return pl.pallas_call(
        body, out_shape=[jax.ShapeDtypeStruct(b.shape, b.dtype) for b in blks],
        in_specs=[_ANY] * n, out_specs=[_ANY] * n,
        scratch_shapes=[pltpu.SemaphoreType.DMA((n_p,)), pltpu.SemaphoreType.DMA((n_p,))],
        name=name)(*blks)


def reduce_slots(land, *, tr, name):
    n, R, C = land.shape

    def body(l_ref, o_ref):
        acc = l_ref[0].astype(F32)
        for s in range(1, n):
            acc = acc + l_ref[s].astype(F32)
        o_ref[...] = acc

    return pl.pallas_call(
        body, grid=(R // tr,), in_specs=[pl.BlockSpec((n, tr, C), lambda i: (0, i, 0))],
        out_specs=pl.BlockSpec((tr, C), lambda i: (i, 0)), out_shape=jax.ShapeDtypeStruct((R, C), F32),
        compiler_params=_cparams(1), name=name)(land)


TM = 512


def _tk(d):
    return min(d.shape[0], 1024)


def ffn_fwd_fused(x, g, wgu, wd, *, tm, name, riders=None, rider_args=()):
    T, Dm = x.shape
    nb, _, cb = wgu.shape
    nh = nb // 2
    Fd = nh * cb
    nc = riders.n if riders is not None else 0
    n_steps = T // tm
    stage_at = (0, n_steps // 2, (13 * n_steps) // 16, n_steps - 1)

    def body(*refs):
        x_ref, g_ref, wgu_ref, wd_ref = refs[:4]
        r_in = refs[4:4 + nc]
        o_ref, gu_ref, h_ref = refs[4 + nc:7 + nc]
        r_out, sems = refs[7 + nc:7 + 2 * nc], refs[7 + 2 * nc:]

        def ride(which):
            if riders is not None:
                @pl.when(pl.program_id(0) == stage_at[which])
                def _():
                    riders.stage(which, r_in, r_out, sems)

        ride(0)
        ride(1)
        xv = x_ref[...]
        r = lax.rsqrt(jnp.mean(xv * xv, axis=-1, keepdims=True) + RMS_EPS)
        hb = (xv * r * g_ref[...]).astype(BF16)
        h_ref[...] = hb
        acc = jnp.zeros((tm, Dm), F32)
        for jj in range(nh):
            cols = pl.ds(cb * jj, cb)
            gate = _dot(hb, wgu_ref[jj]).astype(BF16)
            up = _dot(hb, wgu_ref[nh + jj]).astype(BF16)
            gu_ref[0, :, cols] = gate
            gu_ref[1, :, cols] = up
            gv = gate.astype(F32)
            act = (gv * _sigmoid(gv) * up.astype(F32)).astype(BF16)
            acc = acc + _dot(act, wd_ref[cols, :])
        o_ref[...] = xv + 0.5 * acc
        ride(2)
        ride(3)

    outs = pl.pallas_call(
        body, grid=(n_steps,),
        in_specs=[pl.BlockSpec((tm, Dm), lambda i: (i, 0)), pl.BlockSpec((1, Dm), lambda i: (0, 0)),
                  pl.BlockSpec((nb, Dm, cb), lambda i: (0, 0, 0)), pl.BlockSpec((Fd, Dm), lambda i: (0, 0))]
        + [_ANY] * nc,
        out_specs=[pl.BlockSpec((tm, Dm), lambda i: (i, 0)), pl.BlockSpec((2, tm, Fd), lambda i: (0, i, 0)),
                   pl.BlockSpec((tm, Dm), lambda i: (i, 0))] + [_ANY] * nc,
        out_shape=[jax.ShapeDtypeStruct((T, Dm), F32), jax.ShapeDtypeStruct((2, T, Fd), BF16),
                   jax.ShapeDtypeStruct((T, Dm), BF16)] + (riders.shapes if nc else []),
        scratch_shapes=riders.sems if nc else [],
        compiler_params=_cparams(1), name=name)(x, g, wgu, wd, *rider_args)
    return outs[0], outs[1], outs[2], list(outs[3:])


def ffn_fwd(x, g, wgu, wd, tag, riders=None, rider_args=()):
    xo, gu, h, rode = ffn_fwd_fused(x, g, wgu, wd, tm=256, name=f"{tag}_fwd", riders=riders, rider_args=rider_args)
    return xo, (x, gu, h), rode


def ffn_bwd(d, saved, g, wgu, wd, tag, ride_bact=None, ride_dwgu=None, before_dx=None):
    x, gu, h = saved
    dgu, dwd, *rode_a = ffn_bwd_act(d, wd, gu, tm=TM, tn=1408, name=f"{tag}_bact", riding=ride_bact)
    dwgu = mm_tn(h, dgu, scale=1.0, a_split=False, b_split=True, tm=TM, tn=1408, tk=_tk(d), out_blocked=True,
                 name=f"{tag}_dwgu", riding=ride_dwgu)
    rode_g = []
    if ride_dwgu is not None:
        dwgu, *rode_g = dwgu
    riding = before_dx(dwgu, dwd) if before_dx is not None else None
    dx, dg, *rode_x = mm_nt_normbwd(dgu, wgu, x, g, d, a_split=True, tm=_tk(d), tk=1408, name=f"{tag}_dx",
                                    riding=riding)
    return dx, dg, dwgu, dwd, (rode_a, rode_g, rode_x)


def _tile2(v):
    return jnp.concatenate([v, v], axis=-1).reshape(1, LANES)


def _fold2(v):
    return v[:, :HEAD_DIM] + v[:, HEAD_DIM:]


EVEN = dict(dil=1, nsub=2, ppk=4, q_blk=0, k_blk=4, v_blk=5, n_heads=A_Q_HEADS, group=A_GROUP, max_dist=A_WINDOW - 1)
STICK = dict(q_blk=6, k_blk=10, v_blk=14, n_pairs=4)


def _odd_cfg(dil, backward=False):
    nsub = (4 if backward else 16) if dil == 1 else 1
    return dict(dil=dil, nsub=nsub, ppk=1, q_blk=0, k_blk=8, v_blk=16, n_heads=C_HEADS, group=1, max_dist=BLK)


def even_fwd(x, g, win, qg, kg, sinks, wout, tag, riders=None, rider_args=()):
    qkv, h = norm_matmul(x, g, win, tm=_tk(x), tn=1152, split=False, name=f"{tag}_in")
    qg2, kg2 = _tile2(qg), _tile2(kg)
    slopes = jnp.asarray(_alibi(A_Q_HEADS), F32)
    qkn = qk_norm(qkv, qg2, kg2, width=768, steps=1, n_q=4, tm=TM, name=f"{tag}_qkn")
    oa, _, *lse = banded_fwd(qkn, qkv, slopes, sinks, name=f"{tag}_swa", **EVEN)
    ob, rode = stick_fwd(qkv, name=f"{tag}_stick", riders=riders, rider_args=rider_args, **STICK)
    o = jnp.concatenate([oa, ob], axis=1)
    xo = mm_nn(o, wout, res=x, tm=TM, tn=D_MODEL, tk=D_MODEL, name=f"{tag}_out")
    return xo, (x, qkv, qkn, h, oa, lse, o), rode


def even_bwd(d, saved, g, win, qg, kg, sinks, wout, tag, riders=None, rider_args=(), before_dx=None):
    x, qkv, qkn, h, oa, lse, o = saved
    qg2, kg2 = _tile2(qg), _tile2(kg)
    slopes = jnp.asarray(_alibi(A_Q_HEADS), F32)
    dwout = mm_tn(o, d, scale=1.0, a_split=False, b_split=False, tm=D_MODEL, tn=D_MODEL, tk=_tk(d), name=f"{tag}_dwout")
    do = mm_nt(d, wout, tm=TM, tn=D_MODEL, tk=D_MODEL, name=f"{tag}_do")
    dqa, dka4, dva4, dsk = banded_bwd(qkn, qkv, slopes, sinks, do, oa, lse, None, None,
                                      do_blk=0, name=f"{tag}_swa_b", **EVEN)
    dqb, dkb, dvb, rode = stick_bwd(qkv, do, do_blk=4, name=f"{tag}_stick_b", riders=riders, rider_args=rider_args,
                                    **STICK)
    dqkv, dqg, dkg = assemble_even(dqa, dka4, dva4, dqb, dkb, dvb, qkv, qg2, kg2, tm=TM, name=f"{tag}_asm")
    dwin = mm_tn(h, dqkv, scale=1.0, a_split=False, b_split=False, tm=D_MODEL, tn=1152, tk=_tk(d), name=f"{tag}_dwin")
    riding = before_dx(dwin, dwout) if before_dx is not None else None
    dx, dg, *rode_x = mm_nt_normbwd(dqkv, win, x, g, d, a_split=False, tm=TM, tk=1152, name=f"{tag}_dx", riding=riding)
    return dx, dg, dwin, _fold2(dqg), _fold2(dkg), dsk[:, :A_Q_HEADS], dwout, (rode, rode_x)


def odd_fwd(x, g, win, qg, kg, wout, tag):
    qkv, h = norm_matmul(x, g, win, tm=_tk(x), tn=768, split=False, name=f"{tag}_in")
    qg2, kg2 = _tile2(qg), _tile2(kg)
    qkn = qk_norm(qkv, qg2, kg2, width=D_MODEL, steps=2, n_q=8, tm=TM, name=f"{tag}_qkn")
    outs = []
    for p, (window, dil) in enumerate(C_PATTERNS):
        slopes = jnp.asarray(_alibi(C_HEADS), F32) * float(dil)
        outs.append(banded_fwd(qkn, qkv, slopes, None, name=f"{tag}_dil{p}", **_odd_cfg(dil)))
    o, w1, w2, w3 = mix_fwd(outs[0][0], outs[1][0], outs[2][0], outs[0][1], outs[1][1], outs[2][1],
                            tm=TM, name=f"{tag}_mix")
    xo = mm_nn(o, wout, res=x, tm=TM, tn=D_MODEL, tk=D_MODEL, name=f"{tag}_out")
    return xo, (x, qkv, qkn, h, outs, (w1, w2, w3), o)


def odd_bwd(d, saved, g, win, qg, kg, wout, tag):
    x, qkv, qkn, h, outs, ws, o = saved
    qg2, kg2 = _tile2(qg), _tile2(kg)
    dwout = mm_tn(o, d, scale=1.0, a_split=False, b_split=False, tm=D_MODEL, tn=D_MODEL, tk=_tk(d), name=f"{tag}_dwout")
    do = mm_nt(d, wout, tm=TM, tn=D_MODEL, tk=D_MODEL, name=f"{tag}_do")
    parts = []
    for p, (window, dil) in enumerate(C_PATTERNS):
        slopes = jnp.asarray(_alibi(C_HEADS), F32) * float(dil)
        dq, dk, dv, _ = banded_bwd(qkn, qkv, slopes, None, do, None, outs[p][2:], ws[p], o,
                                   do_blk=0, name=f"{tag}_dil{p}_b", **_odd_cfg(dil, backward=True))
        parts.append((dq, dk, dv))
    dqkv, dqg, dkg = assemble_odd(parts, qkv, qg2, kg2, tm=256, name=f"{tag}_asm")
    dwin = mm_tn(h, dqkv, scale=1.0, a_split=False, b_split=False, tm=TM, tn=768, tk=_tk(d), out_blocked=True,
                 name=f"{tag}_dwin")
    dx, dg = mm_nt_normbwd(dqkv, win, x, g, d, a_split=False, tm=TM, tk=768, name=f"{tag}_dx")
    return dx, dg, dwin, _fold2(dqg), _fold2(dkg), dwout


def xa_fwd(x, mem, g, gm, wq, wkv, qg, kg, wo, tag):
    qraw, h = norm_matmul(x, g, wq, tm=TM, tn=D_MODEL, split=False, name=f"{tag}_q")
    kvraw, hm = norm_matmul(mem, gm, wkv, tm=MEM_LEN, tn=512, split=False, name=f"{tag}_kv")
    o = xattn_fwd(qraw, kvraw, qg, kg, tm=TM, name=f"{tag}_att")
    xo = mm_nn(o, wo, res=x, tm=TM, tn=D_MODEL, tk=D_MODEL, name=f"{tag}_o")
    return xo, (x, qraw, h, kvraw, hm, o)


def xa_bwd(d, saved, mem, g, gm, wq, wkv, qg, kg, wo, tag):
    x, qraw, h, kvraw, hm, o = saved
    dwo = mm_tn(o, d, scale=1.0, a_split=False, b_split=False, tm=D_MODEL, tn=D_MODEL, tk=_tk(d), name=f"{tag}_dwo")
    do = mm_nt(d, wo, tm=TM, tn=D_MODEL, tk=D_MODEL, name=f"{tag}_do")
    dq, dkv, dqg, dkg = xattn_bwd(qraw, kvraw, qg, kg, do, o, tm=TM, name=f"{tag}_att_b")
    dwq = mm_tn(h, dq, scale=1.0, a_split=False, b_split=False, tm=D_MODEL, tn=D_MODEL, tk=_tk(d), name=f"{tag}_dwq")
    dx, dg = mm_nt_normbwd(dq, wq, x, g, d, a_split=False, tm=TM, tk=D_MODEL, name=f"{tag}_dx")
    dwkv = mm_tn(hm, dkv, scale=1.0, a_split=False, b_split=False, tm=TM, tn=512, tk=MEM_LEN, out_blocked=True,
                 name=f"{tag}_dwkv")
    _, dgm = mm_nt_normbwd(dkv, wkv, mem, gm, None, a_split=False, tm=MEM_LEN, tk=512, name=f"{tag}_dmem")
    return dx, dg, dgm, dwq, dwkv, dqg, dkg, dwo


MATS = (("ffn1_w_gu", 1), ("ffn1_w_down", 0), ("ev_w_in", 1), ("ev_w_out", 0), ("od_w_in", 1), ("od_w_out", 0),
        ("xa_w_q", 0), ("xa_w_kv", 1), ("xa_w_o", 0), ("ffn2_w_gu", 1), ("ffn2_w_down", 0))
SMALLS = ("ffn1_norm", "mix_norm", "ev_q_gain", "ev_k_gain", "ev_sinks", "od_q_gain", "od_k_gain", "xa_norm",
          "xa_mem_norm", "xa_q_gain", "xa_k_gain", "ffn2_norm")
WEIGHTS = ("ffn1_norm", "ffn1_w_gu", "ffn1_w_down", "mix_norm", "ev_w_in", "ev_q_gain", "ev_k_gain", "ev_sinks",
           "ev_w_out", "od_w_in", "od_q_gain", "od_k_gain", "od_w_out", "xa_norm", "xa_mem_norm", "xa_w_q",
           "xa_w_kv", "xa_q_gain", "xa_k_gain", "xa_w_o", "ffn2_norm", "ffn2_w_gu", "ffn2_w_down")
SMALL_ROWS = 16
LAYER_GROUPS = (
    (((("ffn1_w_gu", 0), ("ffn2_w_gu", 0)), 4, 512),
     ((("ffn1_w_down", 0), ("ffn2_w_down", 0)), 2, 352),
     ((("ev_w_out", 0), ("xa_w_q", 0), ("xa_w_o", 0)), 1, 384),
     ((("xa_w_kv", 0),), 1, 512),
     ((("ev_w_in", 0),), 1, 512)),
    (((("ffn1_w_gu", 1), ("ffn2_w_gu", 1)), 4, 512),
     ((("ffn1_w_down", 1), ("ffn2_w_down", 1)), 2, 352),
     ((("od_w_out", 0), ("xa_w_q", 1), ("xa_w_o", 1)), 1, 384),
     ((("xa_w_kv", 1),), 1, 512),
     ((("od_w_in", 0),), 1, 512)),
)
GATHER_FIRST = (((("ffn1_w_gu", 0),), 2, 512), ((("ffn1_w_down", 0),), 1, 352), ((("ev_w_out", 0),), 1, 128),
                ((("ev_w_in", 0),), 1, 512))
GATHER_IN_FFN1 = (((("ffn2_w_gu", 0),), 2, 512), ((("ffn2_w_down", 0),), 1, 352))
GATHER_IN_STICK = (((("ffn1_w_gu", 1),), 2, 512), ((("ffn1_w_down", 1),), 1, 352), ((("od_w_out", 0),), 1, 128),
                   ((("od_w_in", 0),), 1, 512),
                   ((("xa_w_q", 0), ("xa_w_o", 0), ("xa_w_q", 1), ("xa_w_o", 1)), 1, 512),
                   ((("xa_w_kv", 0), ("xa_w_kv", 1)), 1, 512))
GATHER_IN_FFN2 = (((("ffn2_w_gu", 1),), 2, 512), ((("ffn2_w_down", 1),), 1, 352))
ROUNDS = {
    "1": LAYER_GROUPS[1],
    "0a": (((("ffn2_w_gu", 0),), 2, 512), ((("ffn2_w_down", 0),), 1, 352)) + LAYER_GROUPS[0][2:],
    "0b": (((("ffn1_w_gu", 0),), 2, 512), ((("ffn1_w_down", 0),), 1, 352)),
}


def _chunks_of(groups):
    return tuple(g[1] for g in groups)
COL_SHARDED = {name for name, axis in MATS if axis == 1}
BLOCKED = {"ffn1_w_gu", "ffn2_w_gu", "xa_w_kv", "od_w_in"}


def group_halves(shards, c, groups):
    out = []
    for members, _, _ in groups:
        halves = []
        for name, layer in members:
            _, r, cc = shards[name].shape
            half = lax.dynamic_index_in_dim(shards[name][layer].reshape(2, r // 2, cc), c, 0, keepdims=False)
            halves.append(half.astype(BF16))
        out.append(jnp.concatenate(halves, axis=0))
    return out


def full_weights(gathered, shards, groups):
    full = {}
    for (members, _, _), arr in zip(groups, gathered):
        for w, (name, layer) in enumerate(members):
            _, r, cc = shards[name].shape
            piece = arr[:, w * (r // 2):(w + 1) * (r // 2)].reshape(4, r, cc)
            if name not in COL_SHARDED:
                piece = piece.reshape(4 * r, cc)
            elif name not in BLOCKED:
                piece = piece.transpose(1, 0, 2).reshape(r, 4 * cc)
            full[(name, layer)] = piece
    return full


def group_grads(grads, shards, groups):
    out = []
    for members, _, _ in groups:
        parts = []
        for name, layer in members:
            _, r, cc = shards[name].shape
            gfull = grads[(name, layer)]
            if name in COL_SHARDED and name not in BLOCKED:
                gfull = gfull.reshape(2, r // 2, 4, cc).transpose(2, 0, 1, 3)
            parts.append(gfull.reshape(N_DEV, r // 2, cc))
        out.append(jnp.concatenate(parts, axis=1))
    return out


def shard_grads(mine, theirs, c, shards, groups):
    per = {}
    for (members, _, _), a, b in zip(groups, mine, theirs):
        for w, (name, layer) in enumerate(members):
            _, r, cc = shards[name].shape
            rows = slice(w * (r // 2), (w + 1) * (r // 2))
            lo = jnp.where(c == 0, a[rows], b[rows])
            hi = jnp.where(c == 0, b[rows], a[rows])
            per[(name, layer)] = jnp.concatenate([lo, hi], axis=0)
    return per


def pack_small(vals):
    row10 = jnp.concatenate([vals["xa_q_gain"].reshape(1, 512), vals["xa_k_gain"].reshape(1, 512)], axis=1)
    row11 = jnp.concatenate([vals["ev_q_gain"], vals["ev_k_gain"], vals["od_q_gain"], vals["od_k_gain"],
                             vals["ev_sinks"], jnp.zeros((1, 1024 - 4 * 64 - 8), F32)], axis=1)
    return jnp.concatenate([vals["ffn1_norm"], vals["mix_norm"], vals["xa_norm"], vals["xa_mem_norm"],
                            vals["ffn2_norm"], row10, row11, jnp.zeros((SMALL_ROWS - 12, 1024), F32)], axis=0)


def unpack_small(arr):
    return {"ffn1_norm": arr[0:2], "mix_norm": arr[2:4], "xa_norm": arr[4:6], "xa_mem_norm": arr[6:8],
            "ffn2_norm": arr[8:10],
            "xa_q_gain": arr[10:11, 0:512].reshape(2, 256), "xa_k_gain": arr[10:11, 512:1024].reshape(2, 256),
            "ev_q_gain": arr[11:12, 0:64], "ev_k_gain": arr[11:12, 64:128], "od_q_gain": arr[11:12, 128:192],
            "od_k_gain": arr[11:12, 192:256], "ev_sinks": arr[11:12, 256:264]}


def local_step(x, mem, target, W, small, prereduce, later):
    depth = small["ffn1_norm"].shape[0]

    def row(name, l):
        return small[name][l:l + 1]

    saved = []
    for l in range(depth):
        j = l // 2
        def riding_gather(host):
            if l == 0 and host in later:
                return GatherBlocks(later[host][0], later[host][1]), later[host][0]
            return None, ()

        riders, rider_args = riding_gather("ffn1")
        x, s1, rode = ffn_fwd(x, row("ffn1_norm", l), W[("ffn1_w_gu", l)], W[("ffn1_w_down", l)], f"l{l}_f1",
                              riders=riders, rider_args=rider_args)
        if riders is not None:
            W = {**W, **later["ffn1"][2](rode)}
        if l % 2 == 0:
            riders, rider_args = riding_gather("stick")
            x, s2, rode = even_fwd(x, row("mix_norm", l), W[("ev_w_in", j)], row("ev_q_gain", j),
                                   row("ev_k_gain", j), small["ev_sinks"][j], W[("ev_w_out", j)], f"l{l}_ev",
                                   riders=riders, rider_args=rider_args)
            if riders is not None:
                W = {**W, **later["stick"][2](rode)}
        else:
            x, s2 = odd_fwd(x, row("mix_norm", l), W[("od_w_in", j)], row("od_q_gain", j), row("od_k_gain", j),
                            W[("od_w_out", j)], f"l{l}_od")
        x, s3 = xa_fwd(x, mem, row("xa_norm", l), row("xa_mem_norm", l), W[("xa_w_q", l)], W[("xa_w_kv", l)],
                       row("xa_q_gain", l), row("xa_k_gain", l), W[("xa_w_o", l)], f"l{l}_xa")
        riders, rider_args = riding_gather("ffn2")
        x, s4, rode = ffn_fwd(x, row("ffn2_norm", l), W[("ffn2_w_gu", l)], W[("ffn2_w_down", l)], f"l{l}_f2",
                              riders=riders, rider_args=rider_args)
        if riders is not None:
            W = {**W, **later["ffn2"][2](rode)}
        saved.append((s1, s2, s3, s4))
    loss, d = loss_kernel(x, target, tm=TM, name="loss")

    gw = {}
    gs = {name: [None] * small[name].shape[0] for name in SMALLS}
    pending, landed = None, {}
    for l in reversed(range(depth)):
        j = l // 2
        s1, s2, s3, s4 = saved[l]
        d, dg, dwgu, dwd, _ = ffn_bwd(d, s4, row("ffn2_norm", l), W[("ffn2_w_gu", l)], W[("ffn2_w_down", l)],
                                      f"l{l}_f2")
        gs["ffn2_norm"][l] = dg
        gw[("ffn2_w_gu", l)], gw[("ffn2_w_down", l)] = dwgu, dwd
        d, dg, dgm, dwq, dwkv, dqg, dkg, dwo = xa_bwd(
            d, s3, mem, row("xa_norm", l), row("xa_mem_norm", l), W[("xa_w_q", l)], W[("xa_w_kv", l)],
            row("xa_q_gain", l), row("xa_k_gain", l), W[("xa_w_o", l)], f"l{l}_xa")
        gs["xa_norm"][l], gs["xa_mem_norm"][l], gs["xa_q_gain"][l], gs["xa_k_gain"][l] = dg, dgm, dqg, dkg
        gw[("xa_w_q", l)], gw[("xa_w_kv", l)], gw[("xa_w_o", l)] = dwq, dwkv, dwo
        split = l == 0 and l % 2 == 0 and ("0a" in ROUNDS)
        early = []
        pre_early = None
        if l % 2 == 0:
            riders, rider_args = None, ()
            if pending is not None:
                riders, rider_args = ChipScatter(pending[1], _chunks_of(ROUNDS[pending[0]])), pending[1]

            def before_mixer_dx(dwin, dwout, j=j, early=early):
                gw[("ev_w_in", j)], gw[("ev_w_out", j)] = dwin, dwout
                early += prereduce.pack(gw, "0a")
                return Riding([(PairExchange(early, _chunks_of(ROUNDS["0a"])), early)])

            d, dg, dwin, dqg, dkg, dsk, dwout, (rode, rode_x) = even_bwd(
                d, s2, row("mix_norm", l), W[("ev_w_in", j)], row("ev_q_gain", j), row("ev_k_gain", j),
                small["ev_sinks"][j], W[("ev_w_out", j)], f"l{l}_ev", riders=riders, rider_args=rider_args,
                before_dx=before_mixer_dx if split else None)
            if pending is not None:
                landed[pending[0]], pending = rode, None
            gs["ev_q_gain"][j], gs["ev_k_gain"][j], gs["ev_sinks"][j] = dqg, dkg, dsk
            gw[("ev_w_in", j)], gw[("ev_w_out", j)] = dwin, dwout
            if split:
                pre_early = prereduce.sums(early, rode_x[0], "0a")
        else:
            d, dg, dwin, dqg, dkg, dwout = odd_bwd(
                d, s2, row("mix_norm", l), W[("od_w_in", j)], row("od_q_gain", j), row("od_k_gain", j),
                W[("od_w_out", j)], f"l{l}_od")
            gs["od_q_gain"][j], gs["od_k_gain"][j] = dqg, dkg
            gw[("od_w_in", j)], gw[("od_w_out", j)] = dwin, dwout
        gs["mix_norm"][l] = dg
        packed = []

        rnd = "0b" if pre_early is not None else str(l)
        final = l == 0

        def before_dx(dwgu, dwd, l=l, packed=packed, rnd=rnd, final=final):
            gw[("ffn1_w_gu", l)], gw[("ffn1_w_down", l)] = dwgu, dwd
            packed += prereduce.pack(gw, rnd)
            chunks = _chunks_of(ROUNDS[rnd])
            if not final:
                return Riding([(PairExchange(packed, chunks), packed)])
            sib = _standalone(PairExchange(packed, chunks), packed, f"pair_grads{rnd}")
            pre = prereduce.sums(packed, sib, rnd)
            return Riding([(ChipScatter(pre, chunks), pre)])

        ride_bact = ride_dwgu = None
        if pre_early is not None:
            chunks = _chunks_of(ROUNDS["0a"])
            ride_bact = Riding([(ChipScatter(pre_early[:2], chunks[:2]), pre_early[:2])])
            ride_dwgu = Riding([(ChipScatter(pre_early[2:], chunks[2:]), pre_early[2:])])
        d, dg, dwgu, dwd, (rode_a, rode_g, rode_x) = ffn_bwd(
            d, s1, row("ffn1_norm", l), W[("ffn1_w_gu", l)], W[("ffn1_w_down", l)], f"l{l}_f1",
            ride_bact=ride_bact, ride_dwgu=ride_dwgu, before_dx=before_dx)
        gs["ffn1_norm"][l] = dg
        if pre_early is not None:
            landed["0a"] = list(rode_a[0]) + list(rode_g[0])
        if pending is not None:
            landed[pending[0]] = chip_scatter(pending[1], _chunks_of(ROUNDS[pending[0]]),
                                              name=f"scatter_grads{pending[0]}")
        if final:
            landed[rnd], pending = rode_x[0], None
        else:
            pending = (rnd, prereduce.sums(packed, rode_x[0], rnd))
    if pending is not None:
        landed[pending[0]] = chip_scatter(pending[1], _chunks_of(ROUNDS[pending[0]]),
                                          name=f"scatter_grads{pending[0]}")
    gsmall = {name: jnp.concatenate(v, axis=0) for name, v in gs.items()}
    return loss, d, landed, gsmall


def kernel(x, mem, ffn1_norm, ffn1_w_gu, ffn1_w_down, mix_norm, ev_w_in, ev_q_gain, ev_k_gain, ev_sinks, ev_w_out, od_w_in, od_q_gain, od_k_gain, od_w_out, xa_norm, xa_mem_norm, xa_w_q, xa_w_kv, xa_q_gain, xa_k_gain, xa_w_o, ffn2_norm, ffn2_w_gu, ffn2_w_down, loss_target, m_ffn1_norm, m_ffn1_w_gu, m_ffn1_w_down, m_mix_norm, m_ev_w_in, m_ev_q_gain, m_ev_k_gain, m_ev_sinks, m_ev_w_out, m_od_w_in, m_od_q_gain, m_od_k_gain, m_od_w_out, m_xa_norm, m_xa_mem_norm, m_xa_w_q, m_xa_w_kv, m_xa_q_gain, m_xa_k_gain, m_xa_w_o, m_ffn2_norm, m_ffn2_w_gu, m_ffn2_w_down, v_ffn1_norm, v_ffn1_w_gu, v_ffn1_w_down, v_mix_norm, v_ev_w_in, v_ev_q_gain, v_ev_k_gain, v_ev_sinks, v_ev_w_out, v_od_w_in, v_od_q_gain, v_od_k_gain, v_od_w_out, v_xa_norm, v_xa_mem_norm, v_xa_w_q, v_xa_w_kv, v_xa_q_gain, v_xa_k_gain, v_xa_w_o, v_ffn2_norm, v_ffn2_w_gu, v_ffn2_w_down):
    given = dict(locals())
    w = {n: given[n] for n in WEIGHTS}
    m = {n: given["m_" + n] for n in WEIGHTS}
    v = {n: given["v_" + n] for n in WEIGHTS}
    c = lax.axis_index("c")
    shards = {name: w[name] for name, _ in MATS}
    small = {n: w[n] for n in SMALLS}

    def gathering(groups):
        return group_halves(shards, c, groups), _chunks_of(groups), lambda got: full_weights(got, shards, groups)

    halves, chunks, unpack = gathering(GATHER_FIRST)
    full = unpack(gather_blocks(halves, chunks, name="gather_weights0"))
    later = {"ffn1": gathering(GATHER_IN_FFN1), "stick": gathering(GATHER_IN_STICK), "ffn2": gathering(GATHER_IN_FFN2)}

    class prereduce:
        @staticmethod
        def pack(gw, rnd):
            return group_grads(gw, shards, ROUNDS[rnd])

        @staticmethod
        def sums(packed, sib, rnd):
            return [pair_sum(p, s, c, tr=g[2], name=f"pair_sum{rnd}_{i}")
                    for i, (g, p, s) in enumerate(zip(ROUNDS[rnd], packed, sib))]

    loss_b, grad_x, landed, gsmall = local_step(x[0], mem[0], loss_target[0], full, small, prereduce, later)

    per = {}
    for rnd, land in sorted(landed.items()):
        groups = ROUNDS[rnd]
        mine = [reduce_slots(a, tr=g[2], name=f"sum_grads{rnd}_{i}") for i, (g, a) in enumerate(zip(groups, land))]
        theirs = sibling_send(mine, _chunks_of(groups), name=f"swap_grads{rnd}")
        per.update(shard_grads(mine, theirs, c, shards, groups))
    g = {name: jnp.stack([per[(name, layer)] for layer in range(w[name].shape[0])], axis=0) for name, _ in MATS}
    land_small = gather_small(pack_small(gsmall), name="gather_small")
    g_small = unpack_small(reduce_slots(land_small, tr=SMALL_ROWS, name="sum_small"))
    g.update(g_small)

    delta, new_m, new_v = {}, {}, {}
    for name, _ in MATS:
        shp = w[name].shape
        flat = [a.reshape(-1, shp[-1]) for a in (w[name], g[name], m[name], v[name])]
        dl, nm, nv = adamw(*flat, br=BLK, name=f"adamw_{name}")
        delta[name], new_m[name], new_v[name] = dl.reshape(shp), nm.reshape(shp), nv.reshape(shp)
    dl, nm, nv = adamw(pack_small(small), pack_small(g_small), pack_small({n: m[n] for n in SMALLS}),
                       pack_small({n: v[n] for n in SMALLS}), br=SMALL_ROWS, name="adamw_small")
    for dst, arr in ((delta, dl), (new_m, nm), (new_v, nv)):
        dst.update(unpack_small(arr))

    loss = lax.psum(loss_b[0, 0], ("x", "y", "c"))
    return (loss, grad_x[None], *[g[n] for n in WEIGHTS], *[delta[n] for n in WEIGHTS],
            *[new_m[n] for n in WEIGHTS], *[new_v[n] for n in WEIGHTS])
```

```python
import jax
import jax.numpy as jnp
from jax import lax
from jax.experimental import pallas as pl
from jax.experimental.pallas import tpu as pltpu

F32 = jnp.float32
BF16 = jnp.bfloat16

D_MODEL = 1024
HEAD_DIM = 64
LANES = 128
BLK = 128
RMS_EPS = 1e-6
MEM_LEN = 256
X_HEADS = 4
X_HEAD_DIM = 256
A_Q_HEADS = 8
A_GROUP = 4
A_WINDOW = 128
C_HEADS = 16
C_PATTERNS = ((128, 1), (512, 4), (2048, 16))
NEG = -1e30
VMEM_LIMIT = 56 * 2 ** 20

ADAM_LR = 0.001
ADAM_B1 = 0.9
ADAM_B2 = 0.999
ADAM_EPS = 1e-08
ADAM_WD = 0.01
ADAM_STEP = 10

N_DEV = 8
MESH = pl.DeviceIdType.MESH


def _cparams(n):
    return pltpu.CompilerParams(dimension_semantics=("arbitrary",) * n, vmem_limit_bytes=VMEM_LIMIT)


def _dot(a, b):
    return jnp.dot(a, b, preferred_element_type=F32)


def _dot_nt(a, b):
    return lax.dot_general(a, b, (((1,), (1,)), ((), ())), preferred_element_type=F32)


def _dot_tn(a, b):
    return lax.dot_general(a, b, (((0,), (0,)), ((), ())), preferred_element_type=F32)


def _sigmoid(z):
    return 1.0 / (1.0 + jnp.exp(-z))


def norm_matmul(x, g, w, *, tm, tn, split, name):
    T, K = x.shape
    blocked = w.ndim == 3
    assert not blocked or w.shape[2] == tn
    N = w.shape[0] * w.shape[2] if blocked else w.shape[1]
    nj = N // tn

    def body(x_ref, g_ref, w_ref, o_ref, h_ref):
        @pl.when(pl.program_id(1) == 0)
        def _():
            xv = x_ref[...]
            r = lax.rsqrt(jnp.mean(xv * xv, axis=-1, keepdims=True) + RMS_EPS)
            h_ref[...] = (xv * r * g_ref[...]).astype(BF16)

        o_ref[...] = _dot(h_ref[...], w_ref[...]).astype(o_ref.dtype)

    if split:
        njh = nj // 2
        o_shape = jax.ShapeDtypeStruct((2, T, N // 2), BF16)
        o_spec = pl.BlockSpec((None, tm, tn), lambda i, j: (j // njh, i, j % njh))
    else:
        o_shape = jax.ShapeDtypeStruct((T, N), F32)
        o_spec = pl.BlockSpec((tm, tn), lambda i, j: (i, j))
    return pl.pallas_call(
        body, grid=(T // tm, nj),
        in_specs=[pl.BlockSpec((tm, K), lambda i, j: (i, 0)),
                  pl.BlockSpec((1, K), lambda i, j: (0, 0)),
                  (pl.BlockSpec((None, K, tn), lambda i, j: (j, 0, 0)) if blocked
                   else pl.BlockSpec((K, tn), lambda i, j: (0, j)))],
        out_specs=[o_spec, pl.BlockSpec((tm, K), lambda i, j: (i, 0))],
        out_shape=[o_shape, jax.ShapeDtypeStruct((T, K), BF16)],
        compiler_params=_cparams(2), name=name)(x, g, w)


def mm_nn(a, b, *, res, tm, tn, tk, name):
    T = a.shape[0]
    K, N = b.shape
    nk = K // tk

    def body(a_ref, b_ref, r_ref, o_ref, acc):
        k = pl.program_id(2)

        @pl.when(k == 0)
        def _():
            acc[...] = jnp.zeros_like(acc)

        acc[...] += _dot(a_ref[...].astype(BF16), b_ref[...])

        @pl.when(k == nk - 1)
        def _():
            o_ref[...] = r_ref[...] + acc[...]

    return pl.pallas_call(
        body, grid=(T // tm, N // tn, nk),
        in_specs=[pl.BlockSpec((tm, tk), lambda i, j, k: (i, k)), pl.BlockSpec((tk, tn), lambda i, j, k: (k, j)),
                  pl.BlockSpec((tm, tn), lambda i, j, k: (i, j))],
        out_specs=pl.BlockSpec((tm, tn), lambda i, j, k: (i, j)),
        out_shape=jax.ShapeDtypeStruct((T, N), F32),
        scratch_shapes=[pltpu.VMEM((tm, tn), F32)],
        compiler_params=_cparams(3), name=name)(a, b, res)


def mm_nt(a, b, *, tm, tn, tk, name):
    T, K = a.shape
    N = b.shape[0]
    nk = K // tk

    def body(a_ref, b_ref, o_ref, acc):
        k = pl.program_id(2)

        @pl.when(k == 0)
        def _():
            acc[...] = jnp.zeros_like(acc)

        acc[...] += _dot_nt(a_ref[...].astype(BF16), b_ref[...])

        @pl.when(k == nk - 1)
        def _():
            o_ref[...] = acc[...]

    return pl.pallas_call(
        body, grid=(T // tm, N // tn, nk),
        in_specs=[pl.BlockSpec((tm, tk), lambda i, j, k: (i, k)),
                  pl.BlockSpec((tn, tk), lambda i, j, k: (j, k))],
        out_specs=pl.BlockSpec((tm, tn), lambda i, j, k: (i, j)),
        out_shape=jax.ShapeDtypeStruct((T, N), F32),
        scratch_shapes=[pltpu.VMEM((tm, tn), F32)],
        compiler_params=_cparams(3), name=name)(a, b)


def ffn_bwd_act(d, wd, gu, *, tm, tn, name, riding=None):
    T, K = d.shape
    Fd = wd.shape[0]
    ni = T // tm

    def body(d_ref, w_ref, g_ref, u_ref, dgu_ref, dwd_ref, acc):
        i = pl.program_id(1)

        @pl.when(i == 0)
        def _():
            acc[...] = jnp.zeros_like(acc)

        db = d_ref[...].astype(BF16)
        da = 0.5 * _dot_nt(db, w_ref[...])
        gv = g_ref[...].astype(F32)
        uv = u_ref[...].astype(F32)
        s = _sigmoid(gv)
        silu = gv * s
        acc[...] += _dot_tn((silu * uv).astype(BF16), db)
        dgu_ref[0] = (da * uv * (s * (1.0 + gv * (1.0 - s)))).astype(BF16)
        dgu_ref[1] = (da * silu).astype(BF16)

        @pl.when(i == ni - 1)
        def _():
            dwd_ref[...] = (0.5 * acc[...]).astype(BF16)

    in_specs = [pl.BlockSpec((tm, K), lambda j, i: (i, 0)),
                pl.BlockSpec((tn, K), lambda j, i: (j, 0)),
                pl.BlockSpec((None, tm, tn), lambda j, i: (0, i, j)),
                pl.BlockSpec((None, tm, tn), lambda j, i: (1, i, j))]
    out_specs = [pl.BlockSpec((2, tm, tn), lambda j, i: (0, i, j)), pl.BlockSpec((tn, K), lambda j, i: (j, 0))]
    out_shape = [jax.ShapeDtypeStruct((2, T, Fd), BF16), jax.ShapeDtypeStruct((Fd, K), BF16)]
    return _call_with_riders(body, riding, (Fd // tn, ni), in_specs, out_specs, out_shape,
                             [pltpu.VMEM((tn, K), F32)], [d, wd, gu, gu], name)


def _call_with_riders(body, riding, grid, in_specs, out_specs, out_shape, scratch, args, name):
    n_out = len(out_shape)
    if riding is None:
        return pl.pallas_call(body, grid=grid, in_specs=in_specs, out_specs=out_specs, out_shape=out_shape,
                              scratch_shapes=scratch, compiler_params=_cparams(len(grid)), name=name)(*args)

    def is_first():
        ok = pl.program_id(0) == 0
        for ax in range(1, len(grid)):
            ok = ok & (pl.program_id(ax) == 0)
        return ok

    def is_last():
        ok = pl.program_id(0) == grid[0] - 1
        for ax in range(1, len(grid)):
            ok = ok & (pl.program_id(ax) == grid[ax] - 1)
        return ok

    outs = pl.pallas_call(
        riding.wrap(body, len(in_specs), n_out, len(scratch), is_first, is_last), grid=grid,
        in_specs=list(in_specs) + riding.in_specs, out_specs=list(out_specs) + riding.out_specs,
        out_shape=list(out_shape) + riding.out_shapes, scratch_shapes=list(scratch) + riding.scratch,
        compiler_params=_cparams(len(grid)), name=name)(*args, *riding.args)
    core, per = riding.split(outs, n_out)
    return (*core, *per)


def mm_nt_normbwd(a, b, x, g, res, *, a_split, tm, tk, name, riding=None):
    T, Dm = x.shape
    blocked = b.ndim == 3
    assert not blocked or b.shape[2] == tk
    K = b.shape[0] * b.shape[2] if blocked else b.shape[1]
    nk = K // tk
    nkh = nk // 2
    has_res = res is not None

    def body(*refs):
        if has_res:
            a_ref, b_ref, x_ref, g_ref, r_ref, dx_ref, dg_ref, acc = refs
        else:
            a_ref, b_ref, x_ref, g_ref, dx_ref, dg_ref, acc = refs
        i = pl.program_id(0)
        k = pl.program_id(1)

        @pl.when(k == 0)
        def _():
            acc[...] = jnp.zeros_like(acc)

        acc[...] += _dot_nt(a_ref[...].astype(BF16), b_ref[...])

        @pl.when(k == nk - 1)
        def _():
            xv = x_ref[...]
            r = lax.rsqrt(jnp.mean(xv * xv, axis=-1, keepdims=True) + RMS_EPS)
            xh = xv * r
            dh = acc[...]
            dxh = dh * g_ref[...]
            dx = r * (dxh - xh * jnp.mean(dxh * xh, axis=-1, keepdims=True))
            if has_res:
                dx = dx + r_ref[...]
            dx_ref[...] = dx
            part = jnp.sum(dh * xh, axis=0, keepdims=True)

            @pl.when(i == 0)
            def _():
                dg_ref[...] = part

            @pl.when(i > 0)
            def _():
                dg_ref[...] += part

    if a_split:
        a_spec = pl.BlockSpec((None, tm, tk), lambda i, k: (k // nkh, i, k % nkh))
    else:
        a_spec = pl.BlockSpec((tm, tk), lambda i, k: (i, k))
    in_specs = [a_spec,
                (pl.BlockSpec((None, Dm, tk), lambda i, k: (k, 0, 0)) if blocked
                 else pl.BlockSpec((Dm, tk), lambda i, k: (0, k))),
                pl.BlockSpec((tm, Dm), lambda i, k: (i, 0)),
                pl.BlockSpec((1, Dm), lambda i, k: (0, 0))]
    args = [a, b, x, g]
    if has_res:
        in_specs.append(pl.BlockSpec((tm, Dm), lambda i, k: (i, 0)))
        args.append(res)
    out_specs = [pl.BlockSpec((tm, Dm), lambda i, k: (i, 0)), pl.BlockSpec((1, Dm), lambda i, k: (0, 0))]
    out_shape = [jax.ShapeDtypeStruct((T, Dm), F32), jax.ShapeDtypeStruct((1, Dm), F32)]
    scratch = [pltpu.VMEM((tm, Dm), F32)]
    return _call_with_riders(body, riding, (T // tm, nk), in_specs, out_specs, out_shape, scratch, args, name)


def mm_tn(a, b, *, scale, a_split, b_split, tm, tn, tk, name, out_blocked=False, riding=None):
    T = a.shape[-2]
    M = a.shape[-1] * (2 if a_split else 1)
    N = b.shape[-1] * (2 if b_split else 1)
    ni, nj, nk = M // tm, N // tn, T // tk
    nih, njh = ni // 2, nj // 2

    def body(a_ref, b_ref, o_ref, acc):
        k = pl.program_id(2)

        @pl.when(k == 0)
        def _():
            acc[...] = jnp.zeros_like(acc)

        acc[...] += _dot_tn(a_ref[...].astype(BF16), b_ref[...].astype(BF16))

        @pl.when(k == nk - 1)
        def _():
            o_ref[...] = (acc[...] * scale).astype(o_ref.dtype)

    if a_split:
        a_spec = pl.BlockSpec((None, tk, tm), lambda i, j, k: (i // nih, k, i % nih))
    else:
        a_spec = pl.BlockSpec((tk, tm), lambda i, j, k: (k, i))
    if b_split:
        b_spec = pl.BlockSpec((None, tk, tn), lambda i, j, k: (j // njh, k, j % njh))
    else:
        b_spec = pl.BlockSpec((tk, tn), lambda i, j, k: (k, j))
    if out_blocked:
        o_spec = pl.BlockSpec((None, None, tm, tn), lambda i, j, k: (j, i, 0, 0))
        o_shape = jax.ShapeDtypeStruct((nj, ni, tm, tn), BF16)
    else:
        o_spec = pl.BlockSpec((tm, tn), lambda i, j, k: (i, j))
        o_shape = jax.ShapeDtypeStruct((M, N), BF16)
    outs = _call_with_riders(body, riding, (ni, nj, nk), [a_spec, b_spec], [o_spec], [o_shape],
                             [pltpu.VMEM((tm, tn), F32)], [a, b], name)
    return outs[0] if riding is None else tuple(outs)


def loss_kernel(y, target, *, tm, name):
    T, Dm = y.shape

    def body(y_ref, t_ref, l_ref, dy_ref):
        e = y_ref[...] - t_ref[...]
        dy_ref[...] = e * (1.0 / Dm)
        part = (0.5 / Dm) * jnp.sum(jnp.sum(e * e, axis=-1, keepdims=True), axis=0, keepdims=True)
        part = jnp.broadcast_to(part, (8, LANES))

        @pl.when(pl.program_id(0) == 0)
        def _():
            l_ref[...] = part

        @pl.when(pl.program_id(0) > 0)
        def _():
            l_ref[...] += part

    return pl.pallas_call(
        body, grid=(T // tm,),
        in_specs=[pl.BlockSpec((tm, Dm), lambda i: (i, 0)), pl.BlockSpec((tm, Dm), lambda i: (i, 0))],
        out_specs=[pl.BlockSpec((8, LANES), lambda i: (0, 0)), pl.BlockSpec((tm, Dm), lambda i: (i, 0))],
        out_shape=[jax.ShapeDtypeStruct((8, LANES), F32), jax.ShapeDtypeStruct((T, Dm), F32)],
        compiler_params=_cparams(1), name=name)(y, target)


def adamw(w, g, m, v, *, br, name):
    R, C = w.shape

    def body(w_ref, g_ref, m_ref, v_ref, d_ref, nm_ref, nv_ref):
        gv = g_ref[...]
        nm = ADAM_B1 * m_ref[...] + (1.0 - ADAM_B1) * gv
        nv = ADAM_B2 * v_ref[...] + (1.0 - ADAM_B2) * (gv * gv)
        m_hat = nm / (1.0 - ADAM_B1 ** ADAM_STEP)
        v_hat = nv / (1.0 - ADAM_B2 ** ADAM_STEP)
        d_ref[...] = -ADAM_LR * (m_hat / (jnp.sqrt(v_hat) + ADAM_EPS) + ADAM_WD * w_ref[...])
        nm_ref[...] = nm
        nv_ref[...] = nv

    spec = pl.BlockSpec((br, C), lambda i: (i, 0))
    shp = jax.ShapeDtypeStruct((R, C), F32)
    return pl.pallas_call(
        body, grid=(R // br,), in_specs=[spec] * 4, out_specs=[spec] * 3, out_shape=[shp] * 3,
        compiler_params=_cparams(1), name=name)(w, g, m, v)


def _lane0():
    return lax.broadcasted_iota(jnp.int32, (1, LANES), 1) < HEAD_DIM


def _half_sum(x, m0):
    s0 = jnp.sum(jnp.where(m0, x, 0.0), axis=-1, keepdims=True)
    s1 = jnp.sum(jnp.where(m0, 0.0, x), axis=-1, keepdims=True)
    return jnp.where(m0, s0, s1)


def _head_rms(x, m0):
    return lax.rsqrt(_half_sum(x * x, m0) * (1.0 / HEAD_DIM) + RMS_EPS)


def _alibi(n):
    return [float(2.0 ** (-8.0 * (h + 1) / n)) for h in range(n)]


def _mask_half(x, m0, e):
    return jnp.where(m0, x, 0.0) if e == 0 else jnp.where(m0, 0.0, x)


def _band_masks2(max_dist, has_prev, live):
    row = lax.broadcasted_iota(jnp.int32, (2 * BLK, 2 * BLK), 0)
    col = lax.broadcasted_iota(jnp.int32, (2 * BLK, 2 * BLK), 1)
    dist = (row & (BLK - 1)) - col + BLK
    lim = jnp.where(live, max_dist, -1)
    first = jnp.where(has_prev, 0, BLK)
    valid = (dist >= 0) & (dist <= lim) & (col >= first)
    top = lax.broadcasted_iota(jnp.int32, (2 * BLK, 1), 0) < BLK
    return dist.astype(F32), valid, top


def _stack_heads(x, m0, kes):
    parts = []
    for e in range(2):
        h = _mask_half(x, m0, e)
        parts.append(pltpu.roll(h, HEAD_DIM, 1) if kes[e] != e else h)
    return jnp.concatenate(parts, axis=0)


def _unstack_heads(y, m0, kes):
    parts = []
    for e in range(2):
        h = y[e * BLK:(e + 1) * BLK]
        parts.append(pltpu.roll(h, HEAD_DIM, 1) if kes[e] != e else h)
    return jnp.where(m0, parts[0], parts[1])


def _rows(r, dil):
    return pl.ds(r, BLK, stride=dil) if dil > 1 else pl.ds(0, BLK)


def _band_units(dil, nsub):
    assert dil == 1 or nsub == 1
    if nsub == 1:
        return [(_rows(r, dil), ("prev", _rows(r, dil)), 0) for r in range(dil)]
    units = [(pl.ds(0, BLK), ("prev", pl.ds(0, BLK)), 0)]
    units += [(pl.ds(BLK * s, BLK), ("cur", pl.ds(BLK * (s - 1), BLK)), s) for s in range(1, nsub)]
    return units


def _head_col_spec(ppk, RB, row_block):
    if ppk == 1:
        return pl.BlockSpec((None, RB, 1), lambda p, i: (p, row_block(i), 0))
    return pl.BlockSpec((ppk, RB, 1), lambda p, i: (p, row_block(i), 0))


def _band_specs(dil, nsub, ppk, q_blk, k_blk, v_blk, kv_shared, nb):
    RB = BLK * dil * nsub
    PB = BLK if nsub > 1 else RB
    qw = LANES * ppk
    kw = LANES if kv_shared else qw

    def cur(i):
        return jnp.minimum(i, nb - 1)

    def prev(i):
        return jnp.maximum(i * nsub - 1, 0) if nsub > 1 else jnp.maximum(i - 1, 0)

    def kidx(base):
        return (lambda p, i: (cur(i), base)) if kv_shared else (lambda p, i: (cur(i), base + p))

    def pidx(base):
        return (lambda p, i: (prev(i), base)) if kv_shared else (lambda p, i: (prev(i), base + p))

    specs = [pl.BlockSpec((RB, qw), lambda p, i: (cur(i), q_blk + p)),
             pl.BlockSpec((RB, kw), kidx(k_blk)), pl.BlockSpec((PB, kw), pidx(k_blk)),
             pl.BlockSpec((RB, kw), kidx(v_blk)), pl.BlockSpec((PB, kw), pidx(v_blk))]
    return specs if dil == 1 else [specs[0], specs[1], specs[3]]


def qk_norm(qkv, q_gain2, k_gain2, *, width, steps, n_q, tm, name):
    T = qkv.shape[0]
    nsb = width // LANES

    def body(x_ref, qg_ref, kg_ref, o_ref):
        m0 = _lane0()
        for b in range(nsb):
            is_q = ((pl.program_id(1) * nsb + b) < n_q).astype(F32)
            gain = qg_ref[...] * is_q + kg_ref[...] * (1.0 - is_q)
            cols = pl.ds(LANES * b, LANES)
            xv = x_ref[:, cols]
            o_ref[:, cols] = xv * _head_rms(xv, m0) * gain

    gspec = pl.BlockSpec((1, LANES), lambda i, j: (0, 0))
    return pl.pallas_call(
        body, grid=(T // tm, steps),
        in_specs=[pl.BlockSpec((tm, width), lambda i, j: (i, j)), gspec, gspec],
        out_specs=pl.BlockSpec((tm, width), lambda i, j: (i, j)),
        out_shape=jax.ShapeDtypeStruct((T, width * steps), F32),
        compiler_params=_cparams(2), name=name)(qkv, q_gain2, k_gain2)


def banded_fwd(qkn, qkv, slopes, sinks, *, dil, nsub, ppk, q_blk, k_blk, v_blk, n_heads, group,
               max_dist, name):
    T = qkv.shape[0]
    RB = BLK * dil * nsub
    nb = T // RB
    npair = n_heads // 2
    kv_shared = group > 1
    scale = HEAD_DIM ** -0.5
    has_sink = sinks is not None

    def body(*refs):
        slope_ref = refs[0]
        if has_sink:
            sink_ref, refs = refs[1], refs[2:]
        else:
            refs = refs[1:]
        if dil == 1:
            q_ref, kc_ref, kp_ref, vc_ref, vp_ref, o_ref, l_ref, lc0_ref, lc1_ref = refs
        else:
            q_ref, kc_ref, vc_ref, o_ref, l_ref, lc0_ref, lc1_ref, kp_ref, vp_ref = refs
        pb = pl.program_id(0)
        i = pl.program_id(1)
        if dil > 1:
            @pl.when(i == 0)
            def _():
                kp_ref[...] = jnp.zeros_like(kp_ref)
                vp_ref[...] = jnp.zeros_like(vp_ref)
        m0 = _lane0()
        distf, valid_first, top = _band_masks2(max_dist, i > 0, i >= 0)
        valid_inner = _band_masks2(max_dist, i >= 0, i >= 0)[1] if nsub > 1 else None
        for u, (rows, (src, prows), sub) in enumerate(_band_units(dil, nsub)):
            valid = valid_first if sub == 0 else valid_inner
            kpr, vpr = (kp_ref, vp_ref) if src == "prev" else (kc_ref, vc_ref)
            kcache = {}
            for jp in range(ppk):
                cs = pl.ds(LANES * jp, LANES)
                jk = 0 if kv_shared else jp
                if jk not in kcache:
                    ks = pl.ds(LANES * jk, LANES)
                    kcur, vcur = kc_ref[rows, ks], vc_ref[rows, ks]
                    if dil == 1:
                        kprev, vprev = kpr[prows, ks], vpr[prows, ks]
                    else:
                        kprev, vprev = kp_ref[u, :, ks], vp_ref[u, :, ks]
                        kp_ref[u, :, ks] = kcur
                        vp_ref[u, :, ks] = vcur
                    kcat = jnp.concatenate([kprev, kcur], axis=0)
                    vcat = jnp.concatenate([vprev, vcur], axis=0)
                    kcache[jk] = (kcat.astype(BF16), vcat.astype(BF16))
                kn, vcat = kcache[jk]
                qn = q_ref[rows, cs]
                kes = [((2 * jp + e) // group) % 2 if kv_shared else e for e in range(2)]
                hidx = 2 * (pb * ppk + jp)
                qs = _stack_heads(qn, m0, kes).astype(BF16)
                slope = jnp.where(top, slope_ref[hidx], slope_ref[hidx + 1])
                s = jnp.where(valid, _dot_nt(qs, kn) * scale - slope * distf, NEG)
                m = jnp.max(s, axis=-1, keepdims=True)
                if has_sink:
                    sk = jnp.where(top, sink_ref[hidx], sink_ref[hidx + 1])
                    m = jnp.maximum(m, sk)
                p = jnp.exp(s - m)
                den = jnp.sum(p, axis=-1, keepdims=True)
                if has_sink:
                    den = den + jnp.exp(sk - m)
                o_full = _dot((p * (1.0 / den)).astype(BF16), vcat)
                o_ref[rows, cs] = _unstack_heads(o_full, m0, kes)
                lse = m + jnp.log(den)
                l_ref[rows, cs] = _unstack_heads(jnp.broadcast_to(lse, (2 * BLK, LANES)), m0, [0, 1])
                for e, lc_ref in enumerate((lc0_ref, lc1_ref)):
                    if ppk == 1:
                        lc_ref[rows, :] = lse[e * BLK:(e + 1) * BLK]
                    else:
                        lc_ref[jp, rows, :] = lse[e * BLK:(e + 1) * BLK]

    smem = pl.BlockSpec(memory_space=pltpu.SMEM)
    qw = LANES * ppk
    ospec = pl.BlockSpec((RB, qw), lambda p, i: (i, p))
    oshape = jax.ShapeDtypeStruct((T, n_heads * HEAD_DIM), F32)
    args = [slopes] + ([sinks] if has_sink else []) + ([qkn] * 3 + [qkv] * 2 if dil == 1 else [qkn, qkn, qkv])
    kw = LANES if kv_shared else qw
    prev_scratch = [] if dil == 1 else [pltpu.VMEM((dil, BLK, kw), F32)] * 2
    return pl.pallas_call(
        body, grid=(npair // ppk, nb),
        in_specs=[smem] * (2 if has_sink else 1) + _band_specs(dil, nsub, ppk, q_blk, k_blk, v_blk, kv_shared, nb),
        out_specs=[ospec, ospec] + [_head_col_spec(ppk, RB, lambda i: i)] * 2,
        out_shape=[oshape, oshape] + [jax.ShapeDtypeStruct((npair, T, 1), F32)] * 2,
        scratch_shapes=prev_scratch, compiler_params=_cparams(2), name=name)(*args)


def banded_bwd(qkn, qkv, slopes, sinks, do, o, lsec, w, omix, *, dil, nsub, ppk, q_blk, k_blk, v_blk,
               n_heads, group, max_dist, do_blk, name):
    T = qkv.shape[0]
    RB = BLK * dil * nsub
    nb = T // RB
    npair = n_heads // 2
    kv_shared = group > 1
    scale = HEAD_DIM ** -0.5
    has_sink = sinks is not None
    mixed = w is not None
    qw = LANES * ppk

    def body(*refs):
        slope_ref = refs[0]
        if has_sink:
            sink_ref, refs = refs[1], refs[2:]
        else:
            refs = refs[1:]
        if dil == 1:
            q_ref, kc_ref, kp_ref, vc_ref, vp_ref, do_ref, lc0_ref, lc1_ref = refs[:8]
            refs = refs[8:]
        else:
            q_ref, kc_ref, vc_ref, do_ref, lc0_ref, lc1_ref = refs[:6]
            refs, kp_ref, vp_ref = refs[6:-2], refs[-2], refs[-1]
        if mixed:
            w_ref, om_ref, refs = refs[0], refs[1], refs[2:]
        else:
            o_ref, refs = refs[0], refs[1:]
        dq_ref, dk_ref, dv_ref, dsk_ref, ck_ref, cv_ref = refs
        pb = pl.program_id(0)
        i = pl.program_id(1)
        live = i < nb
        m0 = _lane0()
        lane = lax.broadcasted_iota(jnp.int32, (1, LANES), 1)
        distf, valid_first, top = _band_masks2(max_dist, i > 0, live)
        valid_inner = _band_masks2(max_dist, i >= 0, live)[1] if nsub > 1 else None
        livef = live.astype(F32)

        def half_rows(x):
            s0 = jnp.sum(jnp.where(m0, x, 0.0), axis=-1, keepdims=True)
            s1 = jnp.sum(jnp.where(m0, 0.0, x), axis=-1, keepdims=True)
            return jnp.concatenate([s0, s1], axis=0)

        @pl.when((pb == 0) & (i == 0))
        def _():
            dsk_ref[...] = jnp.zeros_like(dsk_ref)

        @pl.when(i == 0)
        def _():
            ck_ref[...] = jnp.zeros_like(ck_ref)
            cv_ref[...] = jnp.zeros_like(cv_ref)

        dsk_acc = jnp.zeros((1, LANES), F32)
        if nsub > 1:
            dk_ref[...] = ck_ref[...]
            dv_ref[...] = cv_ref[...]
        if dil > 1:
            @pl.when(i == 0)
            def _():
                kp_ref[...] = jnp.zeros_like(kp_ref)
                vp_ref[...] = jnp.zeros_like(vp_ref)

        for u, (rows, (src, prows), sub) in enumerate(_band_units(dil, nsub)):
            valid = valid_first if sub == 0 else valid_inner
            kpr, vpr = (kp_ref, vp_ref) if src == "prev" else (kc_ref, vc_ref)
            ck_u, cv_u = (ck_ref.at[u], cv_ref.at[u]) if dil > 1 else (None, None)
            for jp in range(ppk):
                cs = pl.ds(LANES * jp, LANES)
                ks = pl.ds(0, LANES) if kv_shared else cs
                kcur, vcur = kc_ref[rows, ks], vc_ref[rows, ks]
                if dil == 1:
                    kprev, vprev = kpr[prows, ks], vpr[prows, ks]
                else:
                    kprev, vprev = kp_ref[u, :, ks], vp_ref[u, :, ks]
                    kp_ref[u, :, ks] = kcur
                    vp_ref[u, :, ks] = vcur
                kn = jnp.concatenate([kprev, kcur], axis=0).astype(BF16)
                vcat = jnp.concatenate([vprev, vcur], axis=0).astype(BF16)
                dov = do_ref[rows, cs]
                if mixed:
                    dov = dov * w_ref[rows, cs]
                    shift = half_rows(dov * om_ref[rows, cs])
                else:
                    shift = half_rows(dov * o_ref[rows, cs])
                kes = [((2 * jp + e) // group) % 2 if kv_shared else e for e in range(2)]
                hidx = 2 * (pb * ppk + jp)
                qs = _stack_heads(q_ref[rows, cs], m0, kes).astype(BF16)
                dos = _stack_heads(dov, m0, kes).astype(BF16)
                lse = jnp.concatenate([ref[rows, :] if ppk == 1 else ref[jp, rows, :]
                                       for ref in (lc0_ref, lc1_ref)], axis=0)
                slope = jnp.where(top, slope_ref[hidx], slope_ref[hidx + 1])
                p = jnp.where(valid, jnp.exp(_dot_nt(qs, kn) * scale - slope * distf - lse), 0.0)
                ds = (p * (_dot_nt(dos, vcat) - shift)).astype(BF16)
                dqn = _unstack_heads(_dot(ds, kn), m0, kes) * scale
                dkn = _dot_tn(ds, qs) * scale
                dvv = _dot_tn(p.astype(BF16), dos)
                if has_sink:
                    sk = jnp.where(top, sink_ref[hidx], sink_ref[hidx + 1])
                    contrib = -jnp.exp(sk - lse) * shift * livef
                    for e in range(2):
                        tot = jnp.sum(contrib[e * BLK:(e + 1) * BLK], axis=0, keepdims=True)
                        dsk_acc = dsk_acc + jnp.where(lane == (2 * jp + e), tot, 0.0)
                dk_raw = dkn

                @pl.when(live)
                def _():
                    dq_ref[rows, cs] = dqn

                if dil > 1:
                    dk_ref[rows, cs] = ck_u[:, cs] + dk_raw[:BLK]
                    dv_ref[rows, cs] = cv_u[:, cs] + dvv[:BLK]
                    ck_u[:, cs] = dk_raw[BLK:]
                    cv_u[:, cs] = dvv[BLK:]
                    continue
                if nsub == 1:
                    dk_ref[rows, cs] = ck_ref[rows, cs] + dk_raw[:BLK]
                    dv_ref[rows, cs] = cv_ref[rows, cs] + dvv[:BLK]
                elif sub == 0:
                    last = pl.ds(RB - BLK, BLK)
                    dk_ref[last, cs] += dk_raw[:BLK]
                    dv_ref[last, cs] += dvv[:BLK]
                else:
                    ck_ref[prows, cs] += dk_raw[:BLK]
                    cv_ref[prows, cs] += dvv[:BLK]
                ck_ref[rows, cs] = dk_raw[BLK:]
                cv_ref[rows, cs] = dvv[BLK:]
        dsk_ref[...] += dsk_acc

    smem = pl.BlockSpec(memory_space=pltpu.SMEM)
    gspec = pl.BlockSpec((1, LANES), lambda p, i: (0, 0))

    def cur(i):
        return jnp.minimum(i, nb - 1)

    qspec = pl.BlockSpec((RB, qw), lambda p, i: (cur(i), p))
    dospec = pl.BlockSpec((RB, qw), lambda p, i: (cur(i), do_blk + p))
    kvout = pl.BlockSpec((RB, qw), lambda p, i: (jnp.maximum(i - 1, 0), p))
    in_specs = ([smem] * (2 if has_sink else 1) + _band_specs(dil, nsub, ppk, q_blk, k_blk, v_blk, kv_shared, nb)
                + [dospec] + [_head_col_spec(ppk, RB, cur)] * 2
                + ([qspec, qspec] if mixed else [qspec]))
    args = ([slopes] + ([sinks] if has_sink else []) + ([qkn] * 3 + [qkv] * 2 if dil == 1 else [qkn, qkn, qkv])
            + [do, lsec[0], lsec[1]]
            + ([w, omix] if mixed else [o]))
    full = jax.ShapeDtypeStruct((T, n_heads * HEAD_DIM), F32)
    row = jax.ShapeDtypeStruct((1, LANES), F32)
    return pl.pallas_call(
        body, grid=(npair // ppk, nb + 1), in_specs=in_specs,
        out_specs=[qspec, kvout, kvout, gspec],
        out_shape=[full, full, full, row],
        scratch_shapes=([pltpu.VMEM((RB, qw), F32)] * 2 if dil == 1
                        else [pltpu.VMEM((dil, BLK, qw), F32)] * 2 + [pltpu.VMEM((dil, BLK, qw), F32)] * 2),
        compiler_params=_cparams(2), name=name)(*args)


def mix_fwd(o1, o2, o3, l1, l2, l3, *, tm, name):
    T, C = o1.shape

    def body(o1r, o2r, o3r, l1r, l2r, l3r, o_ref, w1r, w2r, w3r):
        a, b, c = l1r[...], l2r[...], l3r[...]
        m = jnp.maximum(jnp.maximum(a, b), c)
        ea, eb, ec = jnp.exp(a - m), jnp.exp(b - m), jnp.exp(c - m)
        inv = 1.0 / (ea + eb + ec)
        wa, wb, wc = ea * inv, eb * inv, ec * inv
        o_ref[...] = wa * o1r[...] + wb * o2r[...] + wc * o3r[...]
        w1r[...] = wa
        w2r[...] = wb
        w3r[...] = wc

    spec = pl.BlockSpec((tm, C), lambda i: (i, 0))
    shp = jax.ShapeDtypeStruct((T, C), F32)
    return pl.pallas_call(body, grid=(T // tm,), in_specs=[spec] * 6, out_specs=[spec] * 4, out_shape=[shp] * 4,
                          compiler_params=_cparams(1), name=name)(o1, o2, o3, l1, l2, l3)


def _qk_norm_bwd(raw, dn, gain, m0):
    r = _head_rms(raw, m0)
    h = raw * r
    dh = dn * gain
    d_raw = r * (dh - h * (_half_sum(dh * h, m0) * (1.0 / HEAD_DIM)))
    return d_raw, jnp.sum(dn * h, axis=0, keepdims=True)


def _acc_rows(ref, val):
    @pl.when(pl.program_id(0) == 0)
    def _():
        ref[...] = val

    @pl.when(pl.program_id(0) > 0)
    def _():
        ref[...] += val


def assemble_odd(parts, qkv, q_gain2, k_gain2, *, tm, name):
    T, C = parts[0][0].shape
    nbk = C // LANES

    def body(*refs):
        qkv_ref, qg_ref, kg_ref, o_ref, dqg_ref, dkg_ref = refs[9:]
        m0 = _lane0()
        sums = [refs[j][...] + refs[3 + j][...] + refs[6 + j][...] for j in range(3)]
        o_ref[:, pl.ds(2 * C, C)] = sums[2].astype(o_ref.dtype)
        for j, (g_ref, acc_ref) in enumerate(((qg_ref, dqg_ref), (kg_ref, dkg_ref))):
            dgain = jnp.zeros((1, LANES), F32)
            for b in range(nbk):
                cols = pl.ds(C * j + LANES * b, LANES)
                d_raw, part = _qk_norm_bwd(qkv_ref[:, cols], sums[j][:, LANES * b:LANES * (b + 1)], g_ref[...], m0)
                o_ref[:, cols] = d_raw.astype(o_ref.dtype)
                dgain = dgain + part
            _acc_rows(acc_ref, dgain)

    spec = pl.BlockSpec((tm, C), lambda i: (i, 0))
    gspec = pl.BlockSpec((1, LANES), lambda i: (0, 0))
    flat = [parts[p][j] for p in range(3) for j in range(3)]
    row = jax.ShapeDtypeStruct((1, LANES), F32)
    return pl.pallas_call(body, grid=(T // tm,),
                          in_specs=[spec] * 9 + [pl.BlockSpec((tm, 2 * C), lambda i: (i, 0)), gspec, gspec],
                          out_specs=[pl.BlockSpec((tm, 3 * C), lambda i: (i, 0)), gspec, gspec],
                          out_shape=[jax.ShapeDtypeStruct((T, 3 * C), BF16), row, row],
                          compiler_params=_cparams(1), name=name)(*flat, qkv, q_gain2, k_gain2)


def assemble_even(dqa, dka4, dva4, dqb, dkb, dvb, qkv, q_gain2, k_gain2, *, tm, name):
    T = dqa.shape[0]
    W = 512
    QK = 768

    def body(dqa_r, dka_r, dva_r, dqb_r, dkb_r, dvb_r, qkv_ref, qg_ref, kg_ref, o_ref, dqg_ref, dkg_ref):
        m0 = _lane0()
        ka = dka_r[...]
        va = dva_r[...]
        dqn = dqa_r[...]
        dgain = jnp.zeros((1, LANES), F32)
        for b in range(W // LANES):
            cols = pl.ds(LANES * b, LANES)
            d_raw, part = _qk_norm_bwd(qkv_ref[:, cols], dqn[:, LANES * b:LANES * (b + 1)], qg_ref[...], m0)
            o_ref[:, cols] = d_raw.astype(o_ref.dtype)
            dgain = dgain + part
        _acc_rows(dqg_ref, dgain)
        dkn = ka[:, 0:128] + ka[:, 128:256] + ka[:, 256:384] + ka[:, 384:512]
        d_raw, part = _qk_norm_bwd(qkv_ref[:, pl.ds(W, LANES)], dkn, kg_ref[...], m0)
        dt = o_ref.dtype
        o_ref[:, pl.ds(W, LANES)] = d_raw.astype(dt)
        _acc_rows(dkg_ref, part)
        o_ref[:, pl.ds(640, LANES)] = (va[:, 0:128] + va[:, 128:256] + va[:, 256:384] + va[:, 384:512]).astype(dt)
        o_ref[:, pl.ds(768, W)] = dqb_r[...].astype(dt)
        o_ref[:, pl.ds(1280, W)] = dkb_r[...].astype(dt)
        o_ref[:, pl.ds(1792, W)] = dvb_r[...].astype(dt)

    spec = pl.BlockSpec((tm, W), lambda i: (i, 0))
    gspec = pl.BlockSpec((1, LANES), lambda i: (0, 0))
    row = jax.ShapeDtypeStruct((1, LANES), F32)
    return pl.pallas_call(body, grid=(T // tm,),
                          in_specs=[spec] * 6 + [pl.BlockSpec((tm, QK), lambda i: (i, 0)), gspec, gspec],
                          out_specs=[pl.BlockSpec((tm, 2304), lambda i: (i, 0)), gspec, gspec],
                          out_shape=[jax.ShapeDtypeStruct((T, 2304), BF16), row, row],
                          compiler_params=_cparams(1), name=name)(dqa, dka4, dva4, dqb, dkb, dvb, qkv, q_gain2, k_gain2)


STICK_T = 256
STICK_DEAD = -110.0


def _split_bf16(x):
    hi = x.astype(BF16)
    lo = (x - hi.astype(F32)).astype(BF16)
    return hi, lo


def _stick_logits(qm, kt, scale, diag):
    n = STICK_T
    row = lax.broadcasted_iota(jnp.int32, (n, n), 0)
    col = lax.broadcasted_iota(jnp.int32, (n, n), 1)
    mask = col < row + jnp.where(diag, 0, n)
    z = _dot_nt(qm, kt) * scale
    lneg = -(jnp.maximum(z, 0.0) + jnp.log(1.0 + jnp.exp(-jnp.abs(z))))
    lpos = z + lneg
    lk = jnp.where(mask, lneg, 0.0)
    return mask, lpos, lneg, lk


def _cumsum_mm(x, tri):
    hi, lo = _split_bf16(x)
    return _dot(hi, tri) + _dot(lo, tri)


def stick_fwd(qkv, *, q_blk, k_blk, v_blk, n_pairs, name, riders=None, rider_args=()):
    T = qkv.shape[0]
    n = STICK_T
    nq = T // n
    scale = HEAD_DIM ** -0.5
    nc = riders.n if riders is not None else 0
    n_steps = n_pairs * nq
    stage_at = (0, (5 * n_steps) // 8, (15 * n_steps) // 16, n_steps - 1)

    def body(*refs):
        q_ref, k_ref, v_ref = refs[:3]
        x_refs, o_ref = refs[3:3 + nc], refs[3 + nc]
        out_refs, sems = refs[4 + nc:4 + 2 * nc], refs[4 + 2 * nc:]
        i = pl.program_id(1)
        step_id = pl.program_id(0) * nq + i

        def ride(which):
            if riders is not None:
                @pl.when(step_id == stage_at[which])
                def _():
                    riders.stage(which, x_refs, out_refs, sems)

        ride(0)
        ride(1)
        m0 = _lane0()
        r2 = lax.broadcasted_iota(jnp.int32, (n, n), 0)
        c2 = lax.broadcasted_iota(jnp.int32, (n, n), 1)
        tri_after = (r2 > c2).astype(BF16)
        qv = q_ref[...]
        out = jnp.zeros((n, LANES), F32)
        for e in range(2):
            qm = _mask_half(qv, m0, e).astype(BF16)

            def alive(st):
                t, _, carry = st
                return (t <= i) & (jnp.max(carry) > STICK_DEAD)

            def step(st, e=e, qm=qm):
                t, acc, carry = st
                start = pl.multiple_of((i - t) * n, n)
                kt = k_ref[pl.ds(start, n), :].astype(BF16)
                vt = _mask_half(v_ref[pl.ds(start, n), :], m0, e).astype(BF16)
                mask, lpos, _, lk = _stick_logits(qm, kt, scale, t == 0)
                after = _cumsum_mm(lk, tri_after) + carry
                a = jnp.where(mask, jnp.exp(lpos + after), 0.0)
                acc = acc + _dot(a.astype(BF16), vt)
                carry = carry + jnp.sum(lk, axis=-1, keepdims=True)
                return t + 1, acc, carry

            _, acc, _ = lax.while_loop(alive, step, (jnp.int32(0), jnp.zeros((n, LANES), F32),
                                                     jnp.zeros((n, 1), F32)))
            out = out + acc
        o_ref[...] = out
        ride(2)
        ride(3)

    outs = pl.pallas_call(
        body, grid=(n_pairs, nq),
        in_specs=[pl.BlockSpec((n, LANES), lambda p, i: (i, q_blk + p)),
                  pl.BlockSpec((T, LANES), lambda p, i: (0, k_blk + p)),
                  pl.BlockSpec((T, LANES), lambda p, i: (0, v_blk + p))] + [_ANY] * nc,
        out_specs=[pl.BlockSpec((n, LANES), lambda p, i: (i, p))] + [_ANY] * nc,
        out_shape=[jax.ShapeDtypeStruct((T, n_pairs * LANES), F32)] + (riders.shapes if nc else []),
        scratch_shapes=riders.sems if nc else [],
        compiler_params=_cparams(2), name=name)(qkv, qkv, qkv, *rider_args)
    return outs[0], list(outs[1:])


def stick_bwd(qkv, do, *, q_blk, k_blk, v_blk, do_blk, n_pairs, name, riders=None, rider_args=()):
    T = qkv.shape[0]
    n = STICK_T
    nq = T // n
    scale = HEAD_DIM ** -0.5
    nc = riders.n if riders is not None else 0

    def body(*refs):
        q_ref, k_ref, v_ref, do_ref = refs[:4]
        pre_refs = refs[4:4 + nc]
        dq_ref, dk_ref, dv_ref = refs[4 + nc:7 + nc]
        land_refs = refs[7 + nc:7 + 2 * nc]
        a_keep, g_keep, s_keep = refs[7 + 2 * nc:10 + 2 * nc]
        sems = refs[10 + 2 * nc:]
        i = pl.program_id(1)
        first_step = (pl.program_id(0) == 0) & (i == 0)
        last_step = (pl.program_id(0) == n_pairs - 1) & (i == nq - 1)
        m0 = _lane0()
        r2 = lax.broadcasted_iota(jnp.int32, (n, n), 0)
        c2 = lax.broadcasted_iota(jnp.int32, (n, n), 1)
        tri_after = (r2 > c2).astype(BF16)
        tri_from = (r2 >= c2).astype(BF16)

        if riders is not None:
            @pl.when(first_step)
            def _():
                riders.start(pre_refs, land_refs, sems)

        @pl.when(i == 0)
        def _():
            dk_ref[...] = jnp.zeros_like(dk_ref)
            dv_ref[...] = jnp.zeros_like(dv_ref)

        qv = q_ref[...]
        dov = do_ref[...]
        dq_out = jnp.zeros((n, LANES), F32)
        for e in range(2):
            qm = _mask_half(qv, m0, e).astype(BF16)
            dom = _mask_half(dov, m0, e).astype(BF16)

            def alive(st):
                t, carry, _ = st
                return (t <= i) & (jnp.max(carry) > STICK_DEAD)

            def scan(st, qm=qm, dom=dom):
                t, carry, gtot = st
                start = pl.multiple_of((i - t) * n, n)
                kt = k_ref[pl.ds(start, n), :].astype(BF16)
                vt = v_ref[pl.ds(start, n), :].astype(BF16)
                mask, lpos, lneg, lk = _stick_logits(qm, kt, scale, t == 0)
                a = jnp.where(mask, jnp.exp(lpos + _cumsum_mm(lk, tri_after) + carry), 0.0)
                g = _dot_nt(dom, vt) * a
                a_keep[t] = a.astype(BF16)
                g_keep[t] = g
                s_keep[t] = jnp.exp(lneg).astype(BF16)
                return (t + 1, carry + jnp.sum(lk, axis=-1, keepdims=True),
                        gtot + jnp.sum(g, axis=-1, keepdims=True))

            z1 = jnp.zeros((n, 1), F32)
            n_live, _, gtot = lax.while_loop(alive, scan, (jnp.int32(0), z1, z1))

            def step(t, st, e=e, qm=qm, dom=dom, gtot=gtot):
                dq_acc, gright = st
                start = pl.multiple_of((i - t) * n, n)
                g = g_keep[t]
                sneg = s_keep[t].astype(F32)
                before = gtot - (_cumsum_mm(g, tri_from) + gright)
                mask = c2 < r2 + jnp.where(t == 0, 0, n)
                dz = jnp.where(mask, g * sneg - before * (1.0 - sneg), 0.0) * scale
                dzb = dz.astype(BF16)
                dq_acc = dq_acc + _dot(dzb, _mask_half(k_ref[pl.ds(start, n), :], m0, e).astype(BF16))
                dk_ref[pl.ds(start, n), :] += _dot_tn(dzb, qm)
                dv_ref[pl.ds(start, n), :] += _dot_tn(a_keep[t], dom)
                return dq_acc, gright + jnp.sum(g, axis=-1, keepdims=True)

            dq_acc, _ = lax.fori_loop(0, n_live, step, (jnp.zeros((n, LANES), F32), z1))
            dq_out = dq_out + dq_acc
        dq_ref[...] = dq_out

        if riders is not None:
            @pl.when(last_step)
            def _():
                riders.finish(pre_refs, land_refs, sems)

    tile = pl.BlockSpec((n, LANES), lambda p, i: (i, p))
    whole = pl.BlockSpec((T, LANES), lambda p, i: (0, p))
    shp = jax.ShapeDtypeStruct((T, n_pairs * LANES), F32)
    outs = pl.pallas_call(
        body, grid=(n_pairs, nq),
        in_specs=[pl.BlockSpec((n, LANES), lambda p, i: (i, q_blk + p)),
                  pl.BlockSpec((T, LANES), lambda p, i: (0, k_blk + p)),
                  pl.BlockSpec((T, LANES), lambda p, i: (0, v_blk + p)),
                  pl.BlockSpec((n, LANES), lambda p, i: (i, do_blk + p))] + [_ANY] * nc,
        out_specs=[tile, whole, whole] + [_ANY] * nc,
        out_shape=[shp, shp, shp] + (riders.shapes if nc else []),
        scratch_shapes=[pltpu.VMEM((nq, n, n), BF16), pltpu.VMEM((nq, n, n), F32), pltpu.VMEM((nq, n, n), BF16)]
        + (riders.sems if nc else []),
        compiler_params=_cparams(2), name=name)(qkv, qkv, qkv, do, *rider_args)
    return outs[0], outs[1], outs[2], list(outs[3:])


def _xnorm(x):
    r = lax.rsqrt(jnp.mean(x * x, axis=-1, keepdims=True) + RMS_EPS)
    return r, x * r


def xattn_fwd(qraw, kvraw, q_gain, k_gain, *, tm, name):
    T = qraw.shape[0]
    scale = X_HEAD_DIM ** -0.5
    W = X_HEADS * X_HEAD_DIM

    def body(q_ref, kv_ref, qg_ref, kg_ref, o_ref):
        for h in range(X_HEADS):
            cs = pl.ds(X_HEAD_DIM * h, X_HEAD_DIM)
            _, qh = _xnorm(q_ref[:, cs])
            _, kh = _xnorm(kv_ref[:, cs])
            qn = (qh * qg_ref[...]).astype(BF16)
            kn = (kh * kg_ref[...]).astype(BF16)
            v = kv_ref[:, pl.ds(W + X_HEAD_DIM * h, X_HEAD_DIM)].astype(BF16)
            s = _dot_nt(qn, kn) * scale
            m = jnp.max(s, axis=-1, keepdims=True)
            p = jnp.exp(s - m)
            p = p / jnp.sum(p, axis=-1, keepdims=True)
            o_ref[:, cs] = _dot(p.astype(BF16), v)

    gspec = pl.BlockSpec((1, X_HEAD_DIM), lambda i: (0, 0))
    return pl.pallas_call(
        body, grid=(T // tm,),
        in_specs=[pl.BlockSpec((tm, W), lambda i: (i, 0)), pl.BlockSpec((MEM_LEN, 2 * W), lambda i: (0, 0)),
                  gspec, gspec],
        out_specs=pl.BlockSpec((tm, W), lambda i: (i, 0)),
        out_shape=jax.ShapeDtypeStruct((T, W), F32),
        compiler_params=_cparams(1), name=name)(qraw, kvraw, q_gain, k_gain)


def xattn_bwd(qraw, kvraw, q_gain, k_gain, do, o, *, tm, name):
    T = qraw.shape[0]
    nt = T // tm
    scale = X_HEAD_DIM ** -0.5
    W = X_HEADS * X_HEAD_DIM

    def body(q_ref, kv_ref, qg_ref, kg_ref, do_ref, o_ref, dq_ref, dkv_ref, dqg_ref, dkg_ref, dkn_ref):
        i = pl.program_id(0)

        @pl.when(i == 0)
        def _():
            dkv_ref[...] = jnp.zeros_like(dkv_ref)
            dkn_ref[...] = jnp.zeros_like(dkn_ref)
            dqg_ref[...] = jnp.zeros_like(dqg_ref)
            dkg_ref[...] = jnp.zeros_like(dkg_ref)

        qg = qg_ref[...]
        kg = kg_ref[...]
        dqg_acc = jnp.zeros((1, X_HEAD_DIM), F32)
        for h in range(X_HEADS):
            cs = pl.ds(X_HEAD_DIM * h, X_HEAD_DIM)
            vs = pl.ds(W + X_HEAD_DIM * h, X_HEAD_DIM)
            rq, qh = _xnorm(q_ref[:, cs])
            _, kh = _xnorm(kv_ref[:, cs])
            qn = (qh * qg).astype(BF16)
            kn = (kh * kg).astype(BF16)
            v = kv_ref[:, vs].astype(BF16)
            s = _dot_nt(qn, kn) * scale
            m = jnp.max(s, axis=-1, keepdims=True)
            p = jnp.exp(s - m)
            p = p / jnp.sum(p, axis=-1, keepdims=True)
            dov = do_ref[:, cs]
            delta = jnp.sum(dov * o_ref[:, cs], axis=-1, keepdims=True)
            dob = dov.astype(BF16)
            ds = (p * (_dot_nt(dob, v) - delta)).astype(BF16)
            dqn = _dot(ds, kn) * scale
            dkn_ref[:, cs] += _dot_tn(ds, qn) * scale
            dkv_ref[:, vs] += _dot_tn(p.astype(BF16), dob)
            dqg_acc = dqg_acc + jnp.sum(dqn * qh, axis=0, keepdims=True)
            dqh = dqn * qg
            dq_ref[:, cs] = (rq * (dqh - qh * jnp.mean(dqh * qh, axis=-1, keepdims=True))).astype(dq_ref.dtype)
        dqg_ref[...] += dqg_acc

        @pl.when(i == nt - 1)
        def _():
            dkg_acc = jnp.zeros((1, X_HEAD_DIM), F32)
            for h in range(X_HEADS):
                cs = pl.ds(X_HEAD_DIM * h, X_HEAD_DIM)
                rk, kh = _xnorm(kv_ref[:, cs])
                dkn = dkn_ref[:, cs]
                dkg_acc = dkg_acc + jnp.sum(dkn * kh, axis=0, keepdims=True)
                dkh = dkn * kg
                dkv_ref[:, cs] = rk * (dkh - kh * jnp.mean(dkh * kh, axis=-1, keepdims=True))
            dkg_ref[...] = dkg_acc

    gspec = pl.BlockSpec((1, X_HEAD_DIM), lambda i: (0, 0))
    tile = pl.BlockSpec((tm, W), lambda i: (i, 0))
    kvspec = pl.BlockSpec((MEM_LEN, 2 * W), lambda i: (0, 0))
    grow = jax.ShapeDtypeStruct((1, X_HEAD_DIM), F32)
    return pl.pallas_call(
        body, grid=(nt,), in_specs=[tile, kvspec, gspec, gspec, tile, tile],
        out_specs=[tile, kvspec, gspec, gspec],
        out_shape=[jax.ShapeDtypeStruct((T, W), BF16), jax.ShapeDtypeStruct((MEM_LEN, 2 * W), F32), grow, grow],
        scratch_shapes=[pltpu.VMEM((MEM_LEN, W), F32)],
        compiler_params=_cparams(1), name=name)(qraw, kvraw, q_gain, k_gain, do, o)


_ANY = pl.BlockSpec(memory_space=pl.ANY)


def _my_pos():
    return lax.axis_index("x"), lax.axis_index("y"), lax.axis_index("c")


def _pieces(arrays, chunks):
    out = []
    for a, (arr, n) in enumerate(zip(arrays, chunks)):
        rc = arr.shape[-2] // n
        out += [(a, pl.ds(ch * rc, rc)) for ch in range(n)]
    return out


class GatherBlocks:
    N_STAGES = 4

    def __init__(self, blks, chunks):
        self.shapes = [jax.ShapeDtypeStruct((N_DEV,) + b.shape, b.dtype) for b in blks]
        self.n = len(blks)
        self.pieces = _pieces(blks, chunks)
        n_p = len(self.pieces)
        self.sems = [pltpu.SemaphoreType.DMA((7 * n_p,)), pltpu.SemaphoreType.DMA((7 * n_p,)),
                     pltpu.SemaphoreType.DMA((n_p,))]

    def stage(self, which, x_refs, out_refs, sems):
        send_sems, recv_sems, local_sems = sems
        pieces, n_p = self.pieces, len(self.pieces)
        x, y, c = _my_pos()
        me, sibling = (x, y, c), (x, y, 1 - c)
        chips = [(1 - x, y), (x, 1 - y), (1 - x, 1 - y)]
        xn, yn, dg = [(*chip, c) for chip in chips]
        ps = range(n_p)

        def slot(block, p):
            px, py, pc = block
            a, rows = pieces[p]
            return out_refs[a].at[4 * px + 2 * py + pc, rows]

        def own(p):
            a, rows = pieces[p]
            return x_refs[a].at[rows]

        def copy(k, p, block, to, from_input=False):
            return pltpu.make_async_remote_copy(
                src_ref=own(p) if from_input else slot(block, p), dst_ref=slot(block, p),
                send_sem=send_sems.at[k * n_p + p], recv_sem=recv_sems.at[k * n_p + p],
                device_id=to, device_id_type=MESH)

        mine = [pltpu.make_async_copy(own(p), slot(me, p), local_sems.at[p]) for p in ps]
        first = [copy(k, p, me, to, from_input=True) for p in ps for k, to in ((1, xn), (2, yn), (0, sibling))]
        on_x = [copy(3, p, xn, yn) for p in ps if p % 2 == 0] + [copy(4, p, xn, sibling) for p in ps]
        on_y = [copy(3, p, yn, xn) for p in ps if p % 2 == 1] + [copy(5, p, yn, sibling) for p in ps]
        on_d = [copy(6, p, dg, sibling) for p in ps]
        if which == 0:
            for cp in first + mine:
                cp.start()
        elif which == 1:
            for p in ps:
                copy(1, p, xn, me).wait_recv()
                if p % 2 == 0:
                    copy(3, p, xn, yn).start()
                copy(4, p, xn, sibling).start()
                copy(2, p, yn, me).wait_recv()
                if p % 2 == 1:
                    copy(3, p, yn, xn).start()
                copy(5, p, yn, sibling).start()
        elif which == 2:
            for p in ps:
                copy(3, p, dg, me).wait_recv()
                copy(6, p, dg, sibling).start()
        else:
            for p in ps:
                copy(0, p, sibling, me).wait_recv()
            for k, chip in zip((4, 5, 6), chips):
                for p in ps:
                    copy(k, p, (*chip, 1 - c), me).wait_recv()
            for cp in first + on_x + on_y + on_d:
                cp.wait_send()
            for cp in mine:
                cp.wait()


def gather_blocks(blks, chunks, *, name):
    gb = GatherBlocks(blks, chunks)
    n = gb.n

    def body(*refs):
        x_refs, out_refs, sems = refs[:n], refs[n:2 * n], refs[2 * n:]
        for which in range(gb.N_STAGES):
            gb.stage(which, x_refs, out_refs, sems)

    return pl.pallas_call(body, out_shape=gb.shapes, in_specs=[_ANY] * n, out_specs=[_ANY] * n,
                          scratch_shapes=gb.sems, name=name)(*blks)


def gather_small(small, *, name):
    S, C = small.shape

    def body(s_ref, out_ref, send_sems, recv_sems, local_sem):
        x, y, c = _my_pos()
        my_id = 4 * x + 2 * y + c

        def copy(k, slot):
            px, py, pc = x ^ ((k >> 2) & 1), y ^ ((k >> 1) & 1), c ^ (k & 1)
            dst = my_id if slot == "mine" else 4 * px + 2 * py + pc
            return pltpu.make_async_remote_copy(
                src_ref=s_ref, dst_ref=out_ref.at[dst], send_sem=send_sems.at[k - 1], recv_sem=recv_sems.at[k - 1],
                device_id=(px, py, pc), device_id_type=MESH)

        own = pltpu.make_async_copy(s_ref, out_ref.at[my_id], local_sem)
        own.start()
        sends = [copy(k, "mine") for k in range(1, N_DEV)]
        for cp in sends:
            cp.start()
        for k in range(1, N_DEV):
            copy(k, "theirs").wait_recv()
        for cp in sends:
            cp.wait_send()
        own.wait()

    dma7 = pltpu.SemaphoreType.DMA((7,))
    return pl.pallas_call(
        body, out_shape=jax.ShapeDtypeStruct((N_DEV, S, C), small.dtype), in_specs=[_ANY], out_specs=_ANY,
        scratch_shapes=[dma7, dma7, pltpu.SemaphoreType.DMA], name=name)(small)


class PairExchange:
    def __init__(self, bigs, chunks):
        self.shapes = [jax.ShapeDtypeStruct((4,) + b.shape[1:], b.dtype) for b in bigs]
        self.n = len(bigs)
        self.pieces = _pieces(bigs, chunks)
        n_p = len(self.pieces)
        self.sems = [pltpu.SemaphoreType.DMA((4 * n_p,)), pltpu.SemaphoreType.DMA((4 * n_p,))]

    def _copies(self, big_refs, out_refs, sems):
        send_sems, recv_sems = sems
        n_p = len(self.pieces)
        x, y, c = _my_pos()

        def copy(b, p):
            a, rows = self.pieces[p]
            return pltpu.make_async_remote_copy(
                src_ref=big_refs[a].at[2 * b + (1 - c), rows], dst_ref=out_refs[a].at[b, rows],
                send_sem=send_sems.at[b * n_p + p], recv_sem=recv_sems.at[b * n_p + p],
                device_id=(x, y, 1 - c), device_id_type=MESH)

        return [copy(b, p) for b in range(4) for p in range(n_p)]

    def start(self, big_refs, out_refs, sems):
        for cp in self._copies(big_refs, out_refs, sems):
            cp.start()

    def finish(self, big_refs, out_refs, sems):
        cps = self._copies(big_refs, out_refs, sems)
        for cp in cps:
            cp.wait_recv()
        for cp in cps:
            cp.wait_send()


def _standalone(exchange, args, name):
    n = exchange.n

    def body(*refs):
        exchange.start(refs[:n], refs[n:2 * n], refs[2 * n:])
        exchange.finish(refs[:n], refs[n:2 * n], refs[2 * n:])

    return pl.pallas_call(body, out_shape=exchange.shapes, in_specs=[_ANY] * n, out_specs=[_ANY] * n,
                          scratch_shapes=exchange.sems, name=name)(*args)


class Riding:
    def __init__(self, riders):
        self.riders = [(ex, list(args)) for ex, args in riders]
        self.args = [a for _, args in self.riders for a in args]
        self.in_specs = [_ANY] * len(self.args)
        self.out_shapes = [s for ex, _ in self.riders for s in ex.shapes]
        self.out_specs = [_ANY] * len(self.out_shapes)
        self.scratch = [s for ex, _ in self.riders for s in ex.sems]

    def wrap(self, body, n_in, n_out, n_scratch, is_first, is_last):
        def wrapped(*refs):
            k = 0
            core = list(refs[:n_in])
            k = n_in
            r_in = []
            for ex, _ in self.riders:
                r_in.append(refs[k:k + ex.n])
                k += ex.n
            core += refs[k:k + n_out]
            k += n_out
            r_out = []
            for ex, _ in self.riders:
                r_out.append(refs[k:k + ex.n])
                k += ex.n
            core += refs[k:k + n_scratch]
            k += n_scratch
            r_sem = []
            for ex, _ in self.riders:
                r_sem.append(refs[k:k + len(ex.sems)])
                k += len(ex.sems)

            @pl.when(is_first())
            def _():
                for (ex, _), a, b, s in zip(self.riders, r_in, r_out, r_sem):
                    ex.start(a, b, s)

            body(*core)

            @pl.when(is_last())
            def _():
                for (ex, _), a, b, s in zip(self.riders, r_in, r_out, r_sem):
                    ex.finish(a, b, s)

        return wrapped

    def split(self, outs, n_out):
        core, rest, per = list(outs[:n_out]), list(outs[n_out:]), []
        for ex, _ in self.riders:
            per.append(rest[:ex.n])
            rest = rest[ex.n:]
        return core, per


def pair_sum(big, sib, c, *, tr, name):
    _, R, C = big.shape

    def body(c_ref, a_ref, s_ref, o_ref):
        o_ref[...] = (a_ref[...].astype(F32) + s_ref[...].astype(F32)).astype(o_ref.dtype)

    grid_spec = pltpu.PrefetchScalarGridSpec(
        num_scalar_prefetch=1, grid=(4, R // tr),
        in_specs=[pl.BlockSpec((None, tr, C), lambda b, i, c_ref: (2 * b + c_ref[0], i, 0)),
                  pl.BlockSpec((None, tr, C), lambda b, i, c_ref: (b, i, 0))],
        out_specs=pl.BlockSpec((None, tr, C), lambda b, i, c_ref: (b, i, 0)))
    return pl.pallas_call(body, grid_spec=grid_spec, out_shape=jax.ShapeDtypeStruct((4, R, C), big.dtype),
                          compiler_params=_cparams(2), name=name)(c.reshape(1).astype(jnp.int32), big, sib)


class ChipScatter:
    def __init__(self, pres, chunks):
        self.shapes = [jax.ShapeDtypeStruct(p.shape, p.dtype) for p in pres]
        self.n = len(pres)
        self.pieces = _pieces(pres, chunks)
        n_p = len(self.pieces)
        self.sems = [pltpu.SemaphoreType.DMA((3 * n_p,)), pltpu.SemaphoreType.DMA((3 * n_p,)),
                     pltpu.SemaphoreType.DMA((n_p,))]

    def _copies(self, pre_refs, out_refs, sems):
        send_sems, recv_sems, local_sems = sems
        n_p = len(self.pieces)
        x, y, c = _my_pos()
        my_chip = 2 * x + y
        chips = [(1 - x, y), (x, 1 - y), (1 - x, 1 - y)]

        def copy(j, p, slot):
            px, py = chips[j]
            a, rows = self.pieces[p]
            src_slot, dst_slot = (2 * px + py, my_chip) if slot == "mine" else (my_chip, 2 * px + py)
            return pltpu.make_async_remote_copy(
                src_ref=pre_refs[a].at[src_slot, rows], dst_ref=out_refs[a].at[dst_slot, rows],
                send_sem=send_sems.at[j * n_p + p], recv_sem=recv_sems.at[j * n_p + p],
                device_id=(px, py, c), device_id_type=MESH)

        own = [pltpu.make_async_copy(pre_refs[a].at[my_chip, rows], out_refs[a].at[my_chip, rows], local_sems.at[p])
               for p, (a, rows) in enumerate(self.pieces)]
        sends = [copy(j, p, "mine") for j in range(3) for p in range(n_p)]
        recvs = [copy(j, p, "theirs") for j in range(3) for p in range(n_p)]
        return own, sends, recvs

    def start(self, pre_refs, out_refs, sems):
        own, sends, _ = self._copies(pre_refs, out_refs, sems)
        for cp in sends + own:
            cp.start()

    def finish(self, pre_refs, out_refs, sems):
        own, sends, recvs = self._copies(pre_refs, out_refs, sems)
        for cp in recvs:
            cp.wait_recv()
        for cp in sends:
            cp.wait_send()
        for cp in own:
            cp.wait()


def chip_scatter(pres, chunks, *, name):
    cs = ChipScatter(pres, chunks)
    n = cs.n

    def body(*refs):
        pre_refs, out_refs, sems = refs[:n], refs[n:2 * n], refs[2 * n:]
        cs.start(pre_refs, out_refs, sems)
        cs.finish(pre_refs, out_refs, sems)

    return pl.pallas_call(body, out_shape=cs.shapes, in_specs=[_ANY] * n, out_specs=[_ANY] * n,
                          scratch_shapes=cs.sems, name=name)(*pres)


def sibling_send(blks, chunks, *, name):
    n = len(blks)
    pieces = _pieces(blks, chunks)
    n_p = len(pieces)

    def body(*refs):
        x_refs, out_refs = refs[:n], refs[n:2 * n]
        send_sems, recv_sems = refs[2 * n:]
        x, y, c = _my_pos()
        cps = [pltpu.make_async_remote_copy(
            src_ref=x_refs[a].at[rows], dst_ref=out_refs[a].at[rows], send_sem=send_sems.at[p],
            recv_sem=recv_sems.at[p], device_id=(x, y, 1 - c), device_id_type=MESH)
            for p, (a, rows) in enumerate(pieces)]
        for cp in cps:
            cp.start()
        for cp in cps:
            cp.wait_recv()
        for cp in cps:
            cp.wait_send()

    return pl.pallas_call(
        body, out_shape=[jax.ShapeDtypeStruct(b.shape, b.dtype) for b in blks],
        in_specs=[_ANY] * n, out_specs=[_ANY] * n,
        scratch_shapes=[pltpu.SemaphoreType.DMA((n_p,)), pltpu.SemaphoreType.DMA((n_p,))],
        name=name)(*blks)


def reduce_slots(land, *, tr, name):
    n, R, C = land.shape

    def body(l_ref, o_ref):
        acc = l_ref[0].astype(F32)
        for s in range(1, n):
            acc = acc + l_ref[s].astype(F32)
        o_ref[...] = acc

    return pl.pallas_call(
        body, grid=(R // tr,), in_specs=[pl.BlockSpec((n, tr, C), lambda i: (0, i, 0))],
        out_specs=pl.BlockSpec((tr, C), lambda i: (i, 0)), out_shape=jax.ShapeDtypeStruct((R, C), F32),
        compiler_params=_cparams(1), name=name)(land)


TM = 512


def _tk(d):
    return min(d.shape[0], 1024)


def ffn_fwd_fused(x, g, wgu, wd, *, tm, name, riders=None, rider_args=()):
    T, Dm = x.shape
    nb, _, cb = wgu.shape
    nh = nb // 2
    Fd = nh * cb
    nc = riders.n if riders is not None else 0
    n_steps = T // tm
    stage_at = (0, n_steps // 2, (13 * n_steps) // 16, n_steps - 1)

    def body(*refs):
        x_ref, g_ref, wgu_ref, wd_ref = refs[:4]
        r_in = refs[4:4 + nc]
        o_ref, gu_ref, h_ref = refs[4 + nc:7 + nc]
        r_out, sems = refs[7 + nc:7 + 2 * nc], refs[7 + 2 * nc:]

        def ride(which):
            if riders is not None:
                @pl.when(pl.program_id(0) == stage_at[which])
                def _():
                    riders.stage(which, r_in, r_out, sems)

        ride(0)
        ride(1)
        xv = x_ref[...]
        r = lax.rsqrt(jnp.mean(xv * xv, axis=-1, keepdims=True) + RMS_EPS)
        hb = (xv * r * g_ref[...]).astype(BF16)
        h_ref[...] = hb
        acc = jnp.zeros((tm, Dm), F32)
        for jj in range(nh):
            cols = pl.ds(cb * jj, cb)
            gate = _dot(hb, wgu_ref[jj]).astype(BF16)
            up = _dot(hb, wgu_ref[nh + jj]).astype(BF16)
            gu_ref[0, :, cols] = gate
            gu_ref[1, :, cols] = up
            gv = gate.astype(F32)
            act = (gv * _sigmoid(gv) * up.astype(F32)).astype(BF16)
            acc = acc + _dot(act, wd_ref[cols, :])
        o_ref[...] = xv + 0.5 * acc
        ride(2)
        ride(3)

    outs = pl.pallas_call(
        body, grid=(n_steps,),
        in_specs=[pl.BlockSpec((tm, Dm), lambda i: (i, 0)), pl.BlockSpec((1, Dm), lambda i: (0, 0)),
                  pl.BlockSpec((nb, Dm, cb), lambda i: (0, 0, 0)), pl.BlockSpec((Fd, Dm), lambda i: (0, 0))]
        + [_ANY] * nc,
        out_specs=[pl.BlockSpec((tm, Dm), lambda i: (i, 0)), pl.BlockSpec((2, tm, Fd), lambda i: (0, i, 0)),
                   pl.BlockSpec((tm, Dm), lambda i: (i, 0))] + [_ANY] * nc,
        out_shape=[jax.ShapeDtypeStruct((T, Dm), F32), jax.ShapeDtypeStruct((2, T, Fd), BF16),
                   jax.ShapeDtypeStruct((T, Dm), BF16)] + (riders.shapes if nc else []),
        scratch_shapes=riders.sems if nc else [],
        compiler_params=_cparams(1), name=name)(x, g, wgu, wd, *rider_args)
    return outs[0], outs[1], outs[2], list(outs[3:])


def ffn_fwd(x, g, wgu, wd, tag, riders=None, rider_args=()):
    xo, gu, h, rode = ffn_fwd_fused(x, g, wgu, wd, tm=256, name=f"{tag}_fwd", riders=riders, rider_args=rider_args)
    return xo, (x, gu, h), rode


def ffn_bwd(d, saved, g, wgu, wd, tag, ride_bact=None, ride_dwgu=None, before_dx=None):
    x, gu, h = saved
    dgu, dwd, *rode_a = ffn_bwd_act(d, wd, gu, tm=TM, tn=1408, name=f"{tag}_bact", riding=ride_bact)
    dwgu = mm_tn(h, dgu, scale=1.0, a_split=False, b_split=True, tm=TM, tn=1408, tk=_tk(d), out_blocked=True,
                 name=f"{tag}_dwgu", riding=ride_dwgu)
    rode_g = []
    if ride_dwgu is not None:
        dwgu, *rode_g = dwgu
    riding = before_dx(dwgu, dwd) if before_dx is not None else None
    dx, dg, *rode_x = mm_nt_normbwd(dgu, wgu, x, g, d, a_split=True, tm=_tk(d), tk=1408, name=f"{tag}_dx",
                                    riding=riding)
    return dx, dg, dwgu, dwd, (rode_a, rode_g, rode_x)


def _tile2(v):
    return jnp.concatenate([v, v], axis=-1).reshape(1, LANES)


def _fold2(v):
    return v[:, :HEAD_DIM] + v[:, HEAD_DIM:]


EVEN = dict(dil=1, nsub=2, ppk=4, q_blk=0, k_blk=4, v_blk=5, n_heads=A_Q_HEADS, group=A_GROUP, max_dist=A_WINDOW - 1)
STICK = dict(q_blk=6, k_blk=10, v_blk=14, n_pairs=4)


def _odd_cfg(dil, backward=False):
    nsub = (4 if backward else 16) if dil == 1 else 1
    return dict(dil=dil, nsub=nsub, ppk=1, q_blk=0, k_blk=8, v_blk=16, n_heads=C_HEADS, group=1, max_dist=BLK)


def even_fwd(x, g, win, qg, kg, sinks, wout, tag, riders=None, rider_args=()):
    qkv, h = norm_matmul(x, g, win, tm=_tk(x), tn=1152, split=False, name=f"{tag}_in")
    qg2, kg2 = _tile2(qg), _tile2(kg)
    slopes = jnp.asarray(_alibi(A_Q_HEADS), F32)
    qkn = qk_norm(qkv, qg2, kg2, width=768, steps=1, n_q=4, tm=TM, name=f"{tag}_qkn")
    oa, _, *lse = banded_fwd(qkn, qkv, slopes, sinks, name=f"{tag}_swa", **EVEN)
    ob, rode = stick_fwd(qkv, name=f"{tag}_stick", riders=riders, rider_args=rider_args, **STICK)
    o = jnp.concatenate([oa, ob], axis=1)
    xo = mm_nn(o, wout, res=x, tm=TM, tn=D_MODEL, tk=D_MODEL, name=f"{tag}_out")
    return xo, (x, qkv, qkn, h, oa, lse, o), rode


def even_bwd(d, saved, g, win, qg, kg, sinks, wout, tag, riders=None, rider_args=(), before_dx=None):
    x, qkv, qkn, h, oa, lse, o = saved
    qg2, kg2 = _tile2(qg), _tile2(kg)
    slopes = jnp.asarray(_alibi(A_Q_HEADS), F32)
    dwout = mm_tn(o, d, scale=1.0, a_split=False, b_split=False, tm=D_MODEL, tn=D_MODEL, tk=_tk(d), name=f"{tag}_dwout")
    do = mm_nt(d, wout, tm=TM, tn=D_MODEL, tk=D_MODEL, name=f"{tag}_do")
    dqa, dka4, dva4, dsk = banded_bwd(qkn, qkv, slopes, sinks, do, oa, lse, None, None,
                                      do_blk=0, name=f"{tag}_swa_b", **EVEN)
    dqb, dkb, dvb, rode = stick_bwd(qkv, do, do_blk=4, name=f"{tag}_stick_b", riders=riders, rider_args=rider_args,
                                    **STICK)
    dqkv, dqg, dkg = assemble_even(dqa, dka4, dva4, dqb, dkb, dvb, qkv, qg2, kg2, tm=TM, name=f"{tag}_asm")
    dwin = mm_tn(h, dqkv, scale=1.0, a_split=False, b_split=False, tm=D_MODEL, tn=1152, tk=_tk(d), name=f"{tag}_dwin")
    riding = before_dx(dwin, dwout) if before_dx is not None else None
    dx, dg, *rode_x = mm_nt_normbwd(dqkv, win, x, g, d, a_split=False, tm=TM, tk=1152, name=f"{tag}_dx", riding=riding)
    return dx, dg, dwin, _fold2(dqg), _fold2(dkg), dsk[:, :A_Q_HEADS], dwout, (rode, rode_x)


def odd_fwd(x, g, win, qg, kg, wout, tag):
    qkv, h = norm_matmul(x, g, win, tm=_tk(x), tn=768, split=False, name=f"{tag}_in")
    qg2, kg2 = _tile2(qg), _tile2(kg)
    qkn = qk_norm(qkv, qg2, kg2, width=D_MODEL, steps=2, n_q=8, tm=TM, name=f"{tag}_qkn")
    outs = []
    for p, (window, dil) in enumerate(C_PATTERNS):
        slopes = jnp.asarray(_alibi(C_HEADS), F32) * float(dil)
        outs.append(banded_fwd(qkn, qkv, slopes, None, name=f"{tag}_dil{p}", **_odd_cfg(dil)))
    o, w1, w2, w3 = mix_fwd(outs[0][0], outs[1][0], outs[2][0], outs[0][1], outs[1][1], outs[2][1],
                            tm=TM, name=f"{tag}_mix")
    xo = mm_nn(o, wout, res=x, tm=TM, tn=D_MODEL, tk=D_MODEL, name=f"{tag}_out")
    return xo, (x, qkv, qkn, h, outs, (w1, w2, w3), o)


def odd_bwd(d, saved, g, win, qg, kg, wout, tag):
    x, qkv, qkn, h, outs, ws, o = saved
    qg2, kg2 = _tile2(qg), _tile2(kg)
    dwout = mm_tn(o, d, scale=1.0, a_split=False, b_split=False, tm=D_MODEL, tn=D_MODEL, tk=_tk(d), name=f"{tag}_dwout")
    do = mm_nt(d, wout, tm=TM, tn=D_MODEL, tk=D_MODEL, name=f"{tag}_do")
    parts = []
    for p, (window, dil) in enumerate(C_PATTERNS):
        slopes = jnp.asarray(_alibi(C_HEADS), F32) * float(dil)
        dq, dk, dv, _ = banded_bwd(qkn, qkv, slopes, None, do, None, outs[p][2:], ws[p], o,
                                   do_blk=0, name=f"{tag}_dil{p}_b", **_odd_cfg(dil, backward=True))
        parts.append((dq, dk, dv))
    dqkv, dqg, dkg = assemble_odd(parts, qkv, qg2, kg2, tm=256, name=f"{tag}_asm")
    dwin = mm_tn(h, dqkv, scale=1.0, a_split=False, b_split=False, tm=TM, tn=768, tk=_tk(d), out_blocked=True,
                 name=f"{tag}_dwin")
    dx, dg = mm_nt_normbwd(dqkv, win, x, g, d, a_split=False, tm=TM, tk=768, name=f"{tag}_dx")
    return dx, dg, dwin, _fold2(dqg), _fold2(dkg), dwout


def xa_fwd(x, mem, g, gm, wq, wkv, qg, kg, wo, tag):
    qraw, h = norm_matmul(x, g, wq, tm=TM, tn=D_MODEL, split=False, name=f"{tag}_q")
    kvraw, hm = norm_matmul(mem, gm, wkv, tm=MEM_LEN, tn=512, split=False, name=f"{tag}_kv")
    o = xattn_fwd(qraw, kvraw, qg, kg, tm=TM, name=f"{tag}_att")
    xo = mm_nn(o, wo, res=x, tm=TM, tn=D_MODEL, tk=D_MODEL, name=f"{tag}_o")
    return xo, (x, qraw, h, kvraw, hm, o)


def xa_bwd(d, saved, mem, g, gm, wq, wkv, qg, kg, wo, tag):
    x, qraw, h, kvraw, hm, o = saved
    dwo = mm_tn(o, d, scale=1.0, a_split=False, b_split=False, tm=D_MODEL, tn=D_MODEL, tk=_tk(d), name=f"{tag}_dwo")
    do = mm_nt(d, wo, tm=TM, tn=D_MODEL, tk=D_MODEL, name=f"{tag}_do")
    dq, dkv, dqg, dkg = xattn_bwd(qraw, kvraw, qg, kg, do, o, tm=TM, name=f"{tag}_att_b")
    dwq = mm_tn(h, dq, scale=1.0, a_split=False, b_split=False, tm=D_MODEL, tn=D_MODEL, tk=_tk(d), name=f"{tag}_dwq")
    dx, dg = mm_nt_normbwd(dq, wq, x, g, d, a_split=False, tm=TM, tk=D_MODEL, name=f"{tag}_dx")
    dwkv = mm_tn(hm, dkv, scale=1.0, a_split=False, b_split=False, tm=TM, tn=512, tk=MEM_LEN, out_blocked=True,
                 name=f"{tag}_dwkv")
    _, dgm = mm_nt_normbwd(dkv, wkv, mem, gm, None, a_split=False, tm=MEM_LEN, tk=512, name=f"{tag}_dmem")
    return dx, dg, dgm, dwq, dwkv, dqg, dkg, dwo


MATS = (("ffn1_w_gu", 1), ("ffn1_w_down", 0), ("ev_w_in", 1), ("ev_w_out", 0), ("od_w_in", 1), ("od_w_out", 0),
        ("xa_w_q", 0), ("xa_w_kv", 1), ("xa_w_o", 0), ("ffn2_w_gu", 1), ("ffn2_w_down", 0))
SMALLS = ("ffn1_norm", "mix_norm", "ev_q_gain", "ev_k_gain", "ev_sinks", "od_q_gain", "od_k_gain", "xa_norm",
          "xa_mem_norm", "xa_q_gain", "xa_k_gain", "ffn2_norm")
WEIGHTS = ("ffn1_norm", "ffn1_w_gu", "ffn1_w_down", "mix_norm", "ev_w_in", "ev_q_gain", "ev_k_gain", "ev_sinks",
           "ev_w_out", "od_w_in", "od_q_gain", "od_k_gain", "od_w_out", "xa_norm", "xa_mem_norm", "xa_w_q",
           "xa_w_kv", "xa_q_gain", "xa_k_gain", "xa_w_o", "ffn2_norm", "ffn2_w_gu", "ffn2_w_down")
SMALL_ROWS = 16
LAYER_GROUPS = (
    (((("ffn1_w_gu", 0), ("ffn2_w_gu", 0)), 4, 512),
     ((("ffn1_w_down", 0), ("ffn2_w_down", 0)), 2, 352),
     ((("ev_w_out", 0), ("xa_w_q", 0), ("xa_w_o", 0)), 1, 384),
     ((("xa_w_kv", 0),), 1, 512),
     ((("ev_w_in", 0),), 1, 512)),
    (((("ffn1_w_gu", 1), ("ffn2_w_gu", 1)), 4, 512),
     ((("ffn1_w_down", 1), ("ffn2_w_down", 1)), 2, 352),
     ((("od_w_out", 0), ("xa_w_q", 1), ("xa_w_o", 1)), 1, 384),
     ((("xa_w_kv", 1),), 1, 512),
     ((("od_w_in", 0),), 1, 512)),
)
GATHER_FIRST = (((("ffn1_w_gu", 0),), 2, 512), ((("ffn1_w_down", 0),), 1, 352))
GATHER_IN_FFN1 = (((("ffn2_w_gu", 0),), 2, 512), ((("ffn2_w_down", 0),), 1, 352), ((("ev_w_out", 0),), 1, 128),
                  ((("ev_w_in", 0),), 1, 512))
GATHER_IN_STICK = (((("ffn1_w_gu", 1),), 2, 512), ((("ffn1_w_down", 1),), 1, 352), ((("od_w_out", 0),), 1, 128),
                   ((("od_w_in", 0),), 1, 512),
                   ((("xa_w_q", 0), ("xa_w_o", 0), ("xa_w_q", 1), ("xa_w_o", 1)), 1, 512),
                   ((("xa_w_kv", 0), ("xa_w_kv", 1)), 1, 512))
GATHER_IN_FFN2 = (((("ffn2_w_gu", 1),), 2, 512), ((("ffn2_w_down", 1),), 1, 352))
ROUNDS = {
    "1": LAYER_GROUPS[1],
    "0a": (((("ffn2_w_gu", 0),), 2, 512), ((("ffn2_w_down", 0),), 1, 352)) + LAYER_GROUPS[0][2:],
    "0b": (((("ffn1_w_gu", 0),), 2, 512), ((("ffn1_w_down", 0),), 1, 352)),
}


def _chunks_of(groups):
    return tuple(g[1] for g in groups)
COL_SHARDED = {name for name, axis in MATS if axis == 1}
BLOCKED = {"ffn1_w_gu", "ffn2_w_gu", "xa_w_kv", "od_w_in"}


def group_halves(shards, c, groups):
    out = []
    for members, _, _ in groups:
        halves = []
        for name, layer in members:
            _, r, cc = shards[name].shape
            half = lax.dynamic_index_in_dim(shards[name][layer].reshape(2, r // 2, cc), c, 0, keepdims=False)
            halves.append(half.astype(BF16))
        out.append(jnp.concatenate(halves, axis=0))
    return out


def full_weights(gathered, shards, groups):
    full = {}
    for (members, _, _), arr in zip(groups, gathered):
        for w, (name, layer) in enumerate(members):
            _, r, cc = shards[name].shape
            piece = arr[:, w * (r // 2):(w + 1) * (r // 2)].reshape(4, r, cc)
            if name not in COL_SHARDED:
                piece = piece.reshape(4 * r, cc)
            elif name not in BLOCKED:
                piece = piece.transpose(1, 0, 2).reshape(r, 4 * cc)
            full[(name, layer)] = piece
    return full


def group_grads(grads, shards, groups):
    out = []
    for members, _, _ in groups:
        parts = []
        for name, layer in members:
            _, r, cc = shards[name].shape
            gfull = grads[(name, layer)]
            if name in COL_SHARDED and name not in BLOCKED:
                gfull = gfull.reshape(2, r // 2, 4, cc).transpose(2, 0, 1, 3)
            parts.append(gfull.reshape(N_DEV, r // 2, cc))
        out.append(jnp.concatenate(parts, axis=1))
    return out


def shard_grads(mine, theirs, c, shards, groups):
    per = {}
    for (members, _, _), a, b in zip(groups, mine, theirs):
        for w, (name, layer) in enumerate(members):
            _, r, cc = shards[name].shape
            rows = slice(w * (r // 2), (w + 1) * (r // 2))
            lo = jnp.where(c == 0, a[rows], b[rows])
            hi = jnp.where(c == 0, b[rows], a[rows])
            per[(name, layer)] = jnp.concatenate([lo, hi], axis=0)
    return per


def pack_small(vals):
    row10 = jnp.concatenate([vals["xa_q_gain"].reshape(1, 512), vals["xa_k_gain"].reshape(1, 512)], axis=1)
    row11 = jnp.concatenate([vals["ev_q_gain"], vals["ev_k_gain"], vals["od_q_gain"], vals["od_k_gain"],
                             vals["ev_sinks"], jnp.zeros((1, 1024 - 4 * 64 - 8), F32)], axis=1)
    return jnp.concatenate([vals["ffn1_norm"], vals["mix_norm"], vals["xa_norm"], vals["xa_mem_norm"],
                            vals["ffn2_norm"], row10, row11, jnp.zeros((SMALL_ROWS - 12, 1024), F32)], axis=0)


def unpack_small(arr):
    return {"ffn1_norm": arr[0:2], "mix_norm": arr[2:4], "xa_norm": arr[4:6], "xa_mem_norm": arr[6:8],
            "ffn2_norm": arr[8:10],
            "xa_q_gain": arr[10:11, 0:512].reshape(2, 256), "xa_k_gain": arr[10:11, 512:1024].reshape(2, 256),
            "ev_q_gain": arr[11:12, 0:64], "ev_k_gain": arr[11:12, 64:128], "od_q_gain": arr[11:12, 128:192],
            "od_k_gain": arr[11:12, 192:256], "ev_sinks": arr[11:12, 256:264]}


def local_step(x, mem, target, W, small, prereduce, later):
    depth = small["ffn1_norm"].shape[0]

    def row(name, l):
        return small[name][l:l + 1]

    saved = []
    for l in range(depth):
        j = l // 2
        def riding_gather(host):
            if l == 0 and host in later:
                return GatherBlocks(later[host][0], later[host][1]), later[host][0]
            return None, ()

        riders, rider_args = riding_gather("ffn1")
        x, s1, rode = ffn_fwd(x, row("ffn1_norm", l), W[("ffn1_w_gu", l)], W[("ffn1_w_down", l)], f"l{l}_f1",
                              riders=riders, rider_args=rider_args)
        if riders is not None:
            W = {**W, **later["ffn1"][2](rode)}
        if l % 2 == 0:
            riders, rider_args = riding_gather("stick")
            x, s2, rode = even_fwd(x, row("mix_norm", l), W[("ev_w_in", j)], row("ev_q_gain", j),
                                   row("ev_k_gain", j), small["ev_sinks"][j], W[("ev_w_out", j)], f"l{l}_ev",
                                   riders=riders, rider_args=rider_args)
            if riders is not None:
                W = {**W, **later["stick"][2](rode)}
        else:
            x, s2 = odd_fwd(x, row("mix_norm", l), W[("od_w_in", j)], row("od_q_gain", j), row("od_k_gain", j),
                            W[("od_w_out", j)], f"l{l}_od")
        x, s3 = xa_fwd(x, mem, row("xa_norm", l), row("xa_mem_norm", l), W[("xa_w_q", l)], W[("xa_w_kv", l)],
                       row("xa_q_gain", l), row("xa_k_gain", l), W[("xa_w_o", l)], f"l{l}_xa")
        riders, rider_args = riding_gather("ffn2")
        x, s4, rode = ffn_fwd(x, row("ffn2_norm", l), W[("ffn2_w_gu", l)], W[("ffn2_w_down", l)], f"l{l}_f2",
                              riders=riders, rider_args=rider_args)
        if riders is not None:
            W = {**W, **later["ffn2"][2](rode)}
        saved.append((s1, s2, s3, s4))
    loss, d = loss_kernel(x, target, tm=TM, name="loss")

    gw = {}
    gs = {name: [None] * small[name].shape[0] for name in SMALLS}
    pending, landed = None, {}
    for l in reversed(range(depth)):
        j = l // 2
        s1, s2, s3, s4 = saved[l]
        d, dg, dwgu, dwd, _ = ffn_bwd(d, s4, row("ffn2_norm", l), W[("ffn2_w_gu", l)], W[("ffn2_w_down", l)],
                                      f"l{l}_f2")
        gs["ffn2_norm"][l] = dg
        gw[("ffn2_w_gu", l)], gw[("ffn2_w_down", l)] = dwgu, dwd
        d, dg, dgm, dwq, dwkv, dqg, dkg, dwo = xa_bwd(
            d, s3, mem, row("xa_norm", l), row("xa_mem_norm", l), W[("xa_w_q", l)], W[("xa_w_kv", l)],
            row("xa_q_gain", l), row("xa_k_gain", l), W[("xa_w_o", l)], f"l{l}_xa")
        gs["xa_norm"][l], gs["xa_mem_norm"][l], gs["xa_q_gain"][l], gs["xa_k_gain"][l] = dg, dgm, dqg, dkg
        gw[("xa_w_q", l)], gw[("xa_w_kv", l)], gw[("xa_w_o", l)] = dwq, dwkv, dwo
        split = l == 0 and l % 2 == 0 and ("0a" in ROUNDS)
        early = []
        pre_early = None
        if l % 2 == 0:
            riders, rider_args = None, ()
            if pending is not None:
                riders, rider_args = ChipScatter(pending[1], _chunks_of(ROUNDS[pending[0]])), pending[1]

            def before_mixer_dx(dwin, dwout, j=j, early=early):
                gw[("ev_w_in", j)], gw[("ev_w_out", j)] = dwin, dwout
                early += prereduce.pack(gw, "0a")
                return Riding([(PairExchange(early, _chunks_of(ROUNDS["0a"])), early)])

            d, dg, dwin, dqg, dkg, dsk, dwout, (rode, rode_x) = even_bwd(
                d, s2, row("mix_norm", l), W[("ev_w_in", j)], row("ev_q_gain", j), row("ev_k_gain", j),
                small["ev_sinks"][j], W[("ev_w_out", j)], f"l{l}_ev", riders=riders, rider_args=rider_args,
                before_dx=before_mixer_dx if split else None)
            if pending is not None:
                landed[pending[0]], pending = rode, None
            gs["ev_q_gain"][j], gs["ev_k_gain"][j], gs["ev_sinks"][j] = dqg, dkg, dsk
            gw[("ev_w_in", j)], gw[("ev_w_out", j)] = dwin, dwout
            if split:
                pre_early = prereduce.sums(early, rode_x[0], "0a")
        else:
            d, dg, dwin, dqg, dkg, dwout = odd_bwd(
                d, s2, row("mix_norm", l), W[("od_w_in", j)], row("od_q_gain", j), row("od_k_gain", j),
                W[("od_w_out", j)], f"l{l}_od")
            gs["od_q_gain"][j], gs["od_k_gain"][j] = dqg, dkg
            gw[("od_w_in", j)], gw[("od_w_out", j)] = dwin, dwout
        gs["mix_norm"][l] = dg
        packed = []

        rnd = "0b" if pre_early is not None else str(l)
        final = l == 0

        def before_dx(dwgu, dwd, l=l, packed=packed, rnd=rnd, final=final):
            gw[("ffn1_w_gu", l)], gw[("ffn1_w_down", l)] = dwgu, dwd
            packed += prereduce.pack(gw, rnd)
            chunks = _chunks_of(ROUNDS[rnd])
            if not final:
                return Riding([(PairExchange(packed, chunks), packed)])
            sib = _standalone(PairExchange(packed, chunks), packed, f"pair_grads{rnd}")
            pre = prereduce.sums(packed, sib, rnd)
            return Riding([(ChipScatter(pre, chunks), pre)])

        ride_bact = ride_dwgu = None
        if pre_early is not None:
            chunks = _chunks_of(ROUNDS["0a"])
            ride_bact = Riding([(ChipScatter(pre_early[:2], chunks[:2]), pre_early[:2])])
            ride_dwgu = Riding([(ChipScatter(pre_early[2:], chunks[2:]), pre_early[2:])])
        d, dg, dwgu, dwd, (rode_a, rode_g, rode_x) = ffn_bwd(
            d, s1, row("ffn1_norm", l), W[("ffn1_w_gu", l)], W[("ffn1_w_down", l)], f"l{l}_f1",
            ride_bact=ride_bact, ride_dwgu=ride_dwgu, before_dx=before_dx)
        gs["ffn1_norm"][l] = dg
        if pre_early is not None:
            landed["0a"] = list(rode_a[0]) + list(rode_g[0])
        if pending is not None:
            landed[pending[0]] = chip_scatter(pending[1], _chunks_of(ROUNDS[pending[0]]),
                                              name=f"scatter_grads{pending[0]}")
        if final:
            landed[rnd], pending = rode_x[0], None
        else:
            pending = (rnd, prereduce.sums(packed, rode_x[0], rnd))
    if pending is not None:
        landed[pending[0]] = chip_scatter(pending[1], _chunks_of(ROUNDS[pending[0]]),
                                          name=f"scatter_grads{pending[0]}")
    gsmall = {name: jnp.concatenate(v, axis=0) for name, v in gs.items()}
    return loss, d, landed, gsmall


def kernel(x, mem, ffn1_norm, ffn1_w_gu, ffn1_w_down, mix_norm, ev_w_in, ev_q_gain, ev_k_gain, ev_sinks, ev_w_out, od_w_in, od_q_gain, od_k_gain, od_w_out, xa_norm, xa_mem_norm, xa_w_q, xa_w_kv, xa_q_gain, xa_k_gain, xa_w_o, ffn2_norm, ffn2_w_gu, ffn2_w_down, loss_target, m_ffn1_norm, m_ffn1_w_gu, m_ffn1_w_down, m_mix_norm, m_ev_w_in, m_ev_q_gain, m_ev_k_gain, m_ev_sinks, m_ev_w_out, m_od_w_in, m_od_q_gain, m_od_k_gain, m_od_w_out, m_xa_norm, m_xa_mem_norm, m_xa_w_q, m_xa_w_kv, m_xa_q_gain, m_xa_k_gain, m_xa_w_o, m_ffn2_norm, m_ffn2_w_gu, m_ffn2_w_down, v_ffn1_norm, v_ffn1_w_gu, v_ffn1_w_down, v_mix_norm, v_ev_w_in, v_ev_q_gain, v_ev_k_gain, v_ev_sinks, v_ev_w_out, v_od_w_in, v_od_q_gain, v_od_k_gain, v_od_w_out, v_xa_norm, v_xa_mem_norm, v_xa_w_q, v_xa_w_kv, v_xa_q_gain, v_xa_k_gain, v_xa_w_o, v_ffn2_norm, v_ffn2_w_gu, v_ffn2_w_down):
    given = dict(locals())
    w = {n: given[n] for n in WEIGHTS}
    m = {n: given["m_" + n] for n in WEIGHTS}
    v = {n: given["v_" + n] for n in WEIGHTS}
    c = lax.axis_index("c")
    shards = {name: w[name] for name, _ in MATS}
    small = {n: w[n] for n in SMALLS}

    def gathering(groups):
        return group_halves(shards, c, groups), _chunks_of(groups), lambda got: full_weights(got, shards, groups)

    halves, chunks, unpack = gathering(GATHER_FIRST)
    full = unpack(gather_blocks(halves, chunks, name="gather_weights0"))
    later = {"ffn1": gathering(GATHER_IN_FFN1), "stick": gathering(GATHER_IN_STICK), "ffn2": gathering(GATHER_IN_FFN2)}

    class prereduce:
        @staticmethod
        def pack(gw, rnd):
            return group_grads(gw, shards, ROUNDS[rnd])

        @staticmethod
        def sums(packed, sib, rnd):
            return [pair_sum(p, s, c, tr=g[2], name=f"pair_sum{rnd}_{i}")
                    for i, (g, p, s) in enumerate(zip(ROUNDS[rnd], packed, sib))]

    loss_b, grad_x, landed, gsmall = local_step(x[0], mem[0], loss_target[0], full, small, prereduce, later)

    per = {}
    for rnd, land in sorted(landed.items()):
        groups = ROUNDS[rnd]
        mine = [reduce_slots(a, tr=g[2], name=f"sum_grads{rnd}_{i}") for i, (g, a) in enumerate(zip(groups, land))]
        theirs = sibling_send(mine, _chunks_of(groups), name=f"swap_grads{rnd}")
        per.update(shard_grads(mine, theirs, c, shards, groups))
    g = {name: jnp.stack([per[(name, layer)] for layer in range(w[name].shape[0])], axis=0) for name, _ in MATS}
    land_small = gather_small(pack_small(gsmall), name="gather_small")
    g_small = unpack_small(reduce_slots(land_small, tr=SMALL_ROWS, name="sum_small"))
    g.update(g_small)

    delta, new_m, new_v = {}, {}, {}
    for name, _ in MATS:
        shp = w[name].shape
        flat = [a.reshape(-1, shp[-1]) for a in (w[name], g[name], m[name], v[name])]
        dl, nm, nv = adamw(*flat, br=BLK, name=f"adamw_{name}")
        delta[name], new_m[name], new_v[name] = dl.reshape(shp), nm.reshape(shp), nv.reshape(shp)
    dl, nm, nv = adamw(pack_small(small), pack_small(g_small), pack_small({n: m[n] for n in SMALLS}),
                       pack_small({n: v[n] for n in SMALLS}), br=SMALL_ROWS, name="adamw_small")
    for dst, arr in ((delta, dl), (new_m, nm), (new_v, nv)):
        dst.update(unpack_small(arr))

    loss = lax.psum(loss_b[0, 0], ("x", "y", "c"))
    return (loss, grad_x[None], *[g[n] for n in WEIGHTS], *[delta[n] for n in WEIGHTS],
            *[new_m[n] for n in WEIGHTS], *[new_v[n] for n in WEIGHTS])
```
